```python
import math
import jax, jax.numpy as jnp
from jax import lax
import numpy as np

D_MODEL = 2048
BATCH = 8
SEQ = 2048
DEPTH = 2

GATE_WIDTH = D_MODEL
SB_WIDTH = D_MODEL // 2
SB_HEAD_DIM = 128
SB_HEADS = SB_WIDTH // SB_HEAD_DIM
SB_BLOCK = 128
POOL_WIDTH = D_MODEL - SB_WIDTH
POOL_WINDOWS = (2, 4, 8, 16)
POOL_GROUPS = len(POOL_WINDOWS)
POOL_GROUP_DIM = POOL_WIDTH // POOL_GROUPS
EVEN_IN = 3 * SB_WIDTH + POOL_WIDTH + GATE_WIDTH
SCONV_WIDTH = D_MODEL // 2
SCONV_K = 3
CONF_WIDTH = D_MODEL - SCONV_WIDTH
CONF_K = 31
ODD_IN = 3 * SCONV_WIDTH + 2 * CONF_WIDTH + GATE_WIDTH
N_EVEN = (DEPTH + 1) // 2
N_ODD = DEPTH // 2
EPS = 1e-6

kernel_name = "hybrid_stickbreak_pool_shortconv_conformer"


def rms_norm(x, g):
    xf = x.astype(jnp.float32)
    y = xf * lax.rsqrt(jnp.mean(xf * xf, axis=-1, keepdims=True) + EPS)
    return (y * g.astype(jnp.float32)).astype(x.dtype)


def layer_norm(x, g, b):
    xf = x.astype(jnp.float32)
    mu = jnp.mean(xf, axis=-1, keepdims=True)
    var = jnp.mean(jnp.square(xf - mu), axis=-1, keepdims=True)
    y = (xf - mu) * lax.rsqrt(var + EPS)
    return (y * g.astype(jnp.float32) + b.astype(jnp.float32)).astype(x.dtype)


def causal_depthwise_conv(x, w):
    k, c = w.shape
    return lax.conv_general_dilated(
        x, w.astype(x.dtype)[:, None, :], window_strides=(1,), padding=[(k - 1, 0)],
        dimension_numbers=("NWC", "WIO", "NWC"), feature_group_count=c)


def stick_breaking_attention(q, k, v):
    b, s_len, h, dh = q.shape
    scale = 1.0 / math.sqrt(dh)
    outs = []
    for qb in range(s_len // SB_BLOCK):
        q0 = qb * SB_BLOCK
        kend = q0 + SB_BLOCK
        z = jnp.einsum("bqhd,bkhd->bhqk", q[:, q0:kend], k[:, :kend]).astype(jnp.float32) * scale
        t_idx = q0 + jnp.arange(SB_BLOCK)[:, None]
        s_idx = jnp.arange(kend)[None, :]
        mask = s_idx < t_idx
        log_beta = jax.nn.log_sigmoid(z)
        log_1m = jnp.where(mask, jax.nn.log_sigmoid(-z), 0.0)
        log_stay = lax.cumsum(log_1m, axis=3, reverse=True) - log_1m
        wts = jnp.where(mask, jnp.exp(log_beta + log_stay), 0.0)
        outs.append(jnp.einsum("bhqk,bkhd->bqhd", wts.astype(v.dtype), v[:, :kend]))
    return jnp.concatenate(outs, axis=1)


def multiscale_pool(u, pool_w, pool_scale):
    b, s_len, _ = u.shape
    ug = u.reshape(b, s_len, POOL_GROUPS, POOL_GROUP_DIM)
    cs = jnp.cumsum(ug.astype(jnp.float32), axis=1)
    pos1 = jnp.arange(1, s_len + 1)
    pooled = []
    for gi, win in enumerate(POOL_WINDOWS):
        c = cs[:, :, gi]
        prev = jnp.pad(c, ((0, 0), (win, 0), (0, 0)))[:, :s_len]
        count = jnp.minimum(win, pos1).astype(jnp.float32)[None, :, None]
        pooled.append(((c - prev) / count).astype(u.dtype) - ug[:, :, gi])
    pooled = jnp.stack(pooled, axis=2)
    y = jnp.einsum("bsgc,gcd->bsgd", pooled, pool_w).reshape(b, s_len, POOL_WIDTH)
    return y * pool_scale


def even_mixer(h, w_in, pool_w, pool_scale, w_out):
    b, s_len, _ = h.shape
    p = h @ w_in
    q, k, v, u, g = jnp.split(p, np.cumsum([SB_WIDTH, SB_WIDTH, SB_WIDTH, POOL_WIDTH]).tolist(), axis=-1)
    hs = (b, s_len, SB_HEADS, SB_HEAD_DIM)
    a = stick_breaking_attention(q.reshape(hs), k.reshape(hs), v.reshape(hs)).reshape(b, s_len, SB_WIDTH)
    po = multiscale_pool(u, pool_w, pool_scale)
    y = jnp.concatenate([a, po], axis=-1) * jax.nn.silu(g)
    return y @ w_out


def odd_mixer(h, w_in, sconv_w, dconv_w, dconv_b, cnorm_g, cnorm_b, w_out):
    p = h @ w_in
    hc, bc, cc, ga, gb, g = jnp.split(
        p, np.cumsum([SCONV_WIDTH, SCONV_WIDTH, SCONV_WIDTH, CONF_WIDTH, CONF_WIDTH]).tolist(), axis=-1)
    c_out = bc * causal_depthwise_conv(cc * hc, sconv_w)
    d = ga * jax.nn.sigmoid(gb)
    d = causal_depthwise_conv(d, dconv_w) + dconv_b
    d = jax.nn.silu(layer_norm(d, cnorm_g, cnorm_b))
    y = jnp.concatenate([c_out, d], axis=-1) * jax.nn.silu(g)
    return y @ w_out


def _fwd_setup_inputs(seed: int = 0) -> dict:
    key = jax.random.key(seed)
    ks = jax.random.split(key, 20)
    f32 = jnp.float32
    nrm = lambda k, shape, s: jax.random.normal(k, shape, f32) * s
    return {
        "x": jax.random.normal(ks[0], (BATCH, SEQ, D_MODEL), f32),
        "ln_pre_even": 1.0 + nrm(ks[1], (N_EVEN, D_MODEL), 0.05),
        "w_in_even": nrm(ks[2], (N_EVEN, D_MODEL, EVEN_IN), D_MODEL ** -0.5),
        "pool_w": nrm(ks[3], (N_EVEN, POOL_GROUPS, POOL_GROUP_DIM, POOL_GROUP_DIM), POOL_GROUP_DIM ** -0.5),
        "pool_scale": 1.0 + nrm(ks[4], (N_EVEN, POOL_WIDTH), 0.1),
        "w_out_even": nrm(ks[5], (N_EVEN, D_MODEL, D_MODEL), D_MODEL ** -0.5),
        "ln_post_even": 1.0 + nrm(ks[6], (N_EVEN, D_MODEL), 0.05),
        "ln_pre_odd": 1.0 + nrm(ks[7], (N_ODD, D_MODEL), 0.05),
        "w_in_odd": nrm(ks[8], (N_ODD, D_MODEL, ODD_IN), D_MODEL ** -0.5),
        "sconv_w": nrm(ks[9], (N_ODD, SCONV_K, SCONV_WIDTH), SCONV_K ** -0.5),
        "dconv_w": nrm(ks[10], (N_ODD, CONF_K, CONF_WIDTH), CONF_K ** -0.5),
        "dconv_b": nrm(ks[11], (N_ODD, CONF_WIDTH), 0.02),
        "cnorm_g": 1.0 + nrm(ks[12], (N_ODD, CONF_WIDTH), 0.05),
        "cnorm_b": nrm(ks[13], (N_ODD, CONF_WIDTH), 0.02),
        "w_out_odd": nrm(ks[14], (N_ODD, D_MODEL, D_MODEL), D_MODEL ** -0.5),
        "ln_post_odd": 1.0 + nrm(ks[15], (N_ODD, D_MODEL), 0.05),
    }


def _fwd_reference(x, ln_pre_even, w_in_even, pool_w, pool_scale, w_out_even, ln_post_even,
              ln_pre_odd, w_in_odd, sconv_w, dconv_w, dconv_b, cnorm_g, cnorm_b, w_out_odd, ln_post_odd):
    for layer in range(DEPTH):
        i = layer // 2
        if layer % 2 == 0:
            h = rms_norm(x, ln_pre_even[i])
            o = even_mixer(h, w_in_even[i], pool_w[i], pool_scale[i], w_out_even[i])
            x = x + rms_norm(o, ln_post_even[i])
        else:
            h = rms_norm(x, ln_pre_odd[i])
            o = odd_mixer(h, w_in_odd[i], sconv_w[i], dconv_w[i], dconv_b[i], cnorm_g[i], cnorm_b[i], w_out_odd[i])
            x = x + rms_norm(o, ln_post_odd[i])
    return x


import jax as _jax
import jax.numpy as _jnp

TWIN_FORMAT = 'train_step'
FWD_PARAMS = ['x', 'ln_pre_even', 'w_in_even', 'pool_w', 'pool_scale', 'w_out_even', 'ln_post_even', 'ln_pre_odd', 'w_in_odd', 'sconv_w', 'dconv_w', 'dconv_b', 'cnorm_g', 'cnorm_b', 'w_out_odd', 'ln_post_odd']
TWIN_WEIGHTS = ['ln_pre_even', 'w_in_even', 'pool_w', 'pool_scale', 'w_out_even', 'ln_post_even', 'ln_pre_odd', 'w_in_odd', 'sconv_w', 'dconv_w', 'dconv_b', 'cnorm_g', 'cnorm_b', 'w_out_odd', 'ln_post_odd']
TWIN_DIFF_INPUT = 'x'
TWIN_INPUTS = ['x', 'ln_pre_even', 'w_in_even', 'pool_w', 'pool_scale', 'w_out_even', 'ln_post_even', 'ln_pre_odd', 'w_in_odd', 'sconv_w', 'dconv_w', 'dconv_b', 'cnorm_g', 'cnorm_b', 'w_out_odd', 'ln_post_odd', 'loss_target', 'm_ln_pre_even', 'm_w_in_even', 'm_pool_w', 'm_pool_scale', 'm_w_out_even', 'm_ln_post_even', 'm_ln_pre_odd', 'm_w_in_odd', 'm_sconv_w', 'm_dconv_w', 'm_dconv_b', 'm_cnorm_g', 'm_cnorm_b', 'm_w_out_odd', 'm_ln_post_odd', 'v_ln_pre_even', 'v_w_in_even', 'v_pool_w', 'v_pool_scale', 'v_w_out_even', 'v_ln_post_even', 'v_ln_pre_odd', 'v_w_in_odd', 'v_sconv_w', 'v_dconv_w', 'v_dconv_b', 'v_cnorm_g', 'v_cnorm_b', 'v_w_out_odd', 'v_ln_post_odd']
TWIN_OUTPUTS = ['loss', 'grad_x', 'grad_ln_pre_even', 'grad_w_in_even', 'grad_pool_w', 'grad_pool_scale', 'grad_w_out_even', 'grad_ln_post_even', 'grad_ln_pre_odd', 'grad_w_in_odd', 'grad_sconv_w', 'grad_dconv_w', 'grad_dconv_b', 'grad_cnorm_g', 'grad_cnorm_b', 'grad_w_out_odd', 'grad_ln_post_odd', 'delta_ln_pre_even', 'delta_w_in_even', 'delta_pool_w', 'delta_pool_scale', 'delta_w_out_even', 'delta_ln_post_even', 'delta_ln_pre_odd', 'delta_w_in_odd', 'delta_sconv_w', 'delta_dconv_w', 'delta_dconv_b', 'delta_cnorm_g', 'delta_cnorm_b', 'delta_w_out_odd', 'delta_ln_post_odd', 'new_m_ln_pre_even', 'new_m_w_in_even', 'new_m_pool_w', 'new_m_pool_scale', 'new_m_w_out_even', 'new_m_ln_post_even', 'new_m_ln_pre_odd', 'new_m_w_in_odd', 'new_m_sconv_w', 'new_m_dconv_w', 'new_m_dconv_b', 'new_m_cnorm_g', 'new_m_cnorm_b', 'new_m_w_out_odd', 'new_m_ln_post_odd', 'new_v_ln_pre_even', 'new_v_w_in_even', 'new_v_pool_w', 'new_v_pool_scale', 'new_v_w_out_even', 'new_v_ln_post_even', 'new_v_ln_pre_odd', 'new_v_w_in_odd', 'new_v_sconv_w', 'new_v_dconv_w', 'new_v_dconv_b', 'new_v_cnorm_g', 'new_v_cnorm_b', 'new_v_w_out_odd', 'new_v_ln_post_odd']
TWIN_LEAF_KINDS = {'loss': 'loss', 'grad_x': 'grad_x', 'grad_ln_pre_even': 'grad_w', 'grad_w_in_even': 'grad_w', 'grad_pool_w': 'grad_w', 'grad_pool_scale': 'grad_w', 'grad_w_out_even': 'grad_w', 'grad_ln_post_even': 'grad_w', 'grad_ln_pre_odd': 'grad_w', 'grad_w_in_odd': 'grad_w', 'grad_sconv_w': 'grad_w', 'grad_dconv_w': 'grad_w', 'grad_dconv_b': 'grad_w', 'grad_cnorm_g': 'grad_w', 'grad_cnorm_b': 'grad_w', 'grad_w_out_odd': 'grad_w', 'grad_ln_post_odd': 'grad_w', 'delta_ln_pre_even': 'delta_w', 'delta_w_in_even': 'delta_w', 'delta_pool_w': 'delta_w', 'delta_pool_scale': 'delta_w', 'delta_w_out_even': 'delta_w', 'delta_ln_post_even': 'delta_w', 'delta_ln_pre_odd': 'delta_w', 'delta_w_in_odd': 'delta_w', 'delta_sconv_w': 'delta_w', 'delta_dconv_w': 'delta_w', 'delta_dconv_b': 'delta_w', 'delta_cnorm_g': 'delta_w', 'delta_cnorm_b': 'delta_w', 'delta_w_out_odd': 'delta_w', 'delta_ln_post_odd': 'delta_w', 'new_m_ln_pre_even': 'new_m', 'new_m_w_in_even': 'new_m', 'new_m_pool_w': 'new_m', 'new_m_pool_scale': 'new_m', 'new_m_w_out_even': 'new_m', 'new_m_ln_post_even': 'new_m', 'new_m_ln_pre_odd': 'new_m', 'new_m_w_in_odd': 'new_m', 'new_m_sconv_w': 'new_m', 'new_m_dconv_w': 'new_m', 'new_m_dconv_b': 'new_m', 'new_m_cnorm_g': 'new_m', 'new_m_cnorm_b': 'new_m', 'new_m_w_out_odd': 'new_m', 'new_m_ln_post_odd': 'new_m', 'new_v_ln_pre_even': 'new_v', 'new_v_w_in_even': 'new_v', 'new_v_pool_w': 'new_v', 'new_v_pool_scale': 'new_v', 'new_v_w_out_even': 'new_v', 'new_v_ln_post_even': 'new_v', 'new_v_ln_pre_odd': 'new_v', 'new_v_w_in_odd': 'new_v', 'new_v_sconv_w': 'new_v', 'new_v_dconv_w': 'new_v', 'new_v_dconv_b': 'new_v', 'new_v_cnorm_g': 'new_v', 'new_v_cnorm_b': 'new_v', 'new_v_w_out_odd': 'new_v', 'new_v_ln_post_odd': 'new_v'}


def _forward(args):
    return _fwd_reference(*[args[k] for k in FWD_PARAMS])


def _output_shape():
    out = _jax.eval_shape(lambda: _forward(_fwd_setup_inputs(0)))
    return out.shape, out.dtype

N_MICROBATCH = 1
ADAM_LR = 0.001
ADAM_B1 = 0.9
ADAM_B2 = 0.999
ADAM_EPS = 1e-08
ADAM_WD = 0.01
ADAM_STEP = 10
PER_EXAMPLE_BATCH_AXIS = {'x': 0, 'loss_target': 0}
SHARED_INPUTS = []
_WEIGHT_DTYPES = {'ln_pre_even': _jnp.float32, 'w_in_even': _jnp.float32, 'pool_w': _jnp.float32, 'pool_scale': _jnp.float32, 'w_out_even': _jnp.float32, 'ln_post_even': _jnp.float32, 'ln_pre_odd': _jnp.float32, 'w_in_odd': _jnp.float32, 'sconv_w': _jnp.float32, 'dconv_w': _jnp.float32, 'dconv_b': _jnp.float32, 'cnorm_g': _jnp.float32, 'cnorm_b': _jnp.float32, 'w_out_odd': _jnp.float32, 'ln_post_odd': _jnp.float32}
MOMENT_SCALE = {'ln_pre_even': 2.722233e-01, 'w_in_even': 1.582575e-01, 'pool_w': 2.134645e-01, 'pool_scale': 2.084527e-01, 'w_out_even': 1.866823e-01, 'ln_post_even': 8.025156e+00, 'ln_pre_odd': 2.098717e-01, 'w_in_odd': 1.120392e-01, 'sconv_w': 1.327373e-01, 'dconv_w': 8.399220e-02, 'dconv_b': 2.189121e-01, 'cnorm_g': 1.167427e-01, 'cnorm_b': 1.301269e-01, 'w_out_odd': 1.138713e-01, 'ln_post_odd': 8.015676e+00}


def _to_microbatches(a, axis):
    t = _jnp.moveaxis(a, axis, 0)
    t = t.reshape((N_MICROBATCH, t.shape[0] // N_MICROBATCH) + t.shape[1:])
    return _jnp.moveaxis(t, 1, axis + 1)


def setup_inputs(seed: int = 0) -> dict:
    inp = _fwd_setup_inputs(seed)
    key = _jax.random.fold_in(_jax.random.key(seed), 7919)
    shape, _ = _output_shape()
    out = dict(inp)
    out["loss_target"] = _jax.random.normal(_jax.random.fold_in(key, 0), shape, _jnp.float32)
    for i, name in enumerate(TWIN_WEIGHTS):
        w = inp[name].astype(_jnp.float32)
        if MOMENT_SCALE is None:
            s = _jnp.sqrt(_jnp.mean(_jnp.square(w)) + 1e-30)
        else:
            s = MOMENT_SCALE[name]
        km, kv = _jax.random.split(_jax.random.fold_in(key, i + 1))
        out[name] = w
        out["m_" + name] = s * _jax.random.normal(km, w.shape, _jnp.float32)
        out["v_" + name] = (s * s) * _jax.random.uniform(kv, w.shape, _jnp.float32, 0.5, 1.5)
    if N_MICROBATCH > 1:
        for name, axis in PER_EXAMPLE_BATCH_AXIS.items():
            out[name] = _to_microbatches(out[name], axis)
    return {'x': out['x'], 'ln_pre_even': out['ln_pre_even'], 'w_in_even': out['w_in_even'], 'pool_w': out['pool_w'], 'pool_scale': out['pool_scale'], 'w_out_even': out['w_out_even'], 'ln_post_even': out['ln_post_even'], 'ln_pre_odd': out['ln_pre_odd'], 'w_in_odd': out['w_in_odd'], 'sconv_w': out['sconv_w'], 'dconv_w': out['dconv_w'], 'dconv_b': out['dconv_b'], 'cnorm_g': out['cnorm_g'], 'cnorm_b': out['cnorm_b'], 'w_out_odd': out['w_out_odd'], 'ln_post_odd': out['ln_post_odd'], 'loss_target': out['loss_target'], 'm_ln_pre_even': out['m_ln_pre_even'], 'm_w_in_even': out['m_w_in_even'], 'm_pool_w': out['m_pool_w'], 'm_pool_scale': out['m_pool_scale'], 'm_w_out_even': out['m_w_out_even'], 'm_ln_post_even': out['m_ln_post_even'], 'm_ln_pre_odd': out['m_ln_pre_odd'], 'm_w_in_odd': out['m_w_in_odd'], 'm_sconv_w': out['m_sconv_w'], 'm_dconv_w': out['m_dconv_w'], 'm_dconv_b': out['m_dconv_b'], 'm_cnorm_g': out['m_cnorm_g'], 'm_cnorm_b': out['m_cnorm_b'], 'm_w_out_odd': out['m_w_out_odd'], 'm_ln_post_odd': out['m_ln_post_odd'], 'v_ln_pre_even': out['v_ln_pre_even'], 'v_w_in_even': out['v_w_in_even'], 'v_pool_w': out['v_pool_w'], 'v_pool_scale': out['v_pool_scale'], 'v_w_out_even': out['v_w_out_even'], 'v_ln_post_even': out['v_ln_post_even'], 'v_ln_pre_odd': out['v_ln_pre_odd'], 'v_w_in_odd': out['v_w_in_odd'], 'v_sconv_w': out['v_sconv_w'], 'v_dconv_w': out['v_dconv_w'], 'v_dconv_b': out['v_dconv_b'], 'v_cnorm_g': out['v_cnorm_g'], 'v_cnorm_b': out['v_cnorm_b'], 'v_w_out_odd': out['v_w_out_odd'], 'v_ln_post_odd': out['v_ln_post_odd']}


def _loss(weights, diff, rest, loss_target):
    with _jax.named_scope("forward"):
        args = {**rest, TWIN_DIFF_INPUT: diff, **{k: w.astype(_WEIGHT_DTYPES[k]) for k, w in weights.items()}}
        y = _forward(args)
    with _jax.named_scope("loss_head"):
        err = _jnp.square(y.astype(_jnp.float32) - loss_target)
        return 0.5 * _jnp.sum(_jnp.mean(err, axis=-1)) if err.ndim else 0.5 * err


def _adamw(w, g, m, v):
    m = ADAM_B1 * m + (1.0 - ADAM_B1) * g
    v = ADAM_B2 * v + (1.0 - ADAM_B2) * _jnp.square(g)
    m_hat = m / (1.0 - ADAM_B1 ** ADAM_STEP)
    v_hat = v / (1.0 - ADAM_B2 ** ADAM_STEP)
    delta = -ADAM_LR * (m_hat / (_jnp.sqrt(v_hat) + ADAM_EPS) + ADAM_WD * w)
    return delta, m, v


def reference(x, ln_pre_even, w_in_even, pool_w, pool_scale, w_out_even, ln_post_even, ln_pre_odd, w_in_odd, sconv_w, dconv_w, dconv_b, cnorm_g, cnorm_b, w_out_odd, ln_post_odd, loss_target, m_ln_pre_even, m_w_in_even, m_pool_w, m_pool_scale, m_w_out_even, m_ln_post_even, m_ln_pre_odd, m_w_in_odd, m_sconv_w, m_dconv_w, m_dconv_b, m_cnorm_g, m_cnorm_b, m_w_out_odd, m_ln_post_odd, v_ln_pre_even, v_w_in_even, v_pool_w, v_pool_scale, v_w_out_even, v_ln_post_even, v_ln_pre_odd, v_w_in_odd, v_sconv_w, v_dconv_w, v_dconv_b, v_cnorm_g, v_cnorm_b, v_w_out_odd, v_ln_post_odd):
    given = dict(x=x, ln_pre_even=ln_pre_even, w_in_even=w_in_even, pool_w=pool_w, pool_scale=pool_scale, w_out_even=w_out_even, ln_post_even=ln_post_even, ln_pre_odd=ln_pre_odd, w_in_odd=w_in_odd, sconv_w=sconv_w, dconv_w=dconv_w, dconv_b=dconv_b, cnorm_g=cnorm_g, cnorm_b=cnorm_b, w_out_odd=w_out_odd, ln_post_odd=ln_post_odd, loss_target=loss_target, m_ln_pre_even=m_ln_pre_even, m_w_in_even=m_w_in_even, m_pool_w=m_pool_w, m_pool_scale=m_pool_scale, m_w_out_even=m_w_out_even, m_ln_post_even=m_ln_post_even, m_ln_pre_odd=m_ln_pre_odd, m_w_in_odd=m_w_in_odd, m_sconv_w=m_sconv_w, m_dconv_w=m_dconv_w, m_dconv_b=m_dconv_b, m_cnorm_g=m_cnorm_g, m_cnorm_b=m_cnorm_b, m_w_out_odd=m_w_out_odd, m_ln_post_odd=m_ln_post_odd, v_ln_pre_even=v_ln_pre_even, v_w_in_even=v_w_in_even, v_pool_w=v_pool_w, v_pool_scale=v_pool_scale, v_w_out_even=v_w_out_even, v_ln_post_even=v_ln_post_even, v_ln_pre_odd=v_ln_pre_odd, v_w_in_odd=v_w_in_odd, v_sconv_w=v_sconv_w, v_dconv_w=v_dconv_w, v_dconv_b=v_dconv_b, v_cnorm_g=v_cnorm_g, v_cnorm_b=v_cnorm_b, v_w_out_odd=v_w_out_odd, v_ln_post_odd=v_ln_post_odd)
    weights = {n: given[n] for n in TWIN_WEIGHTS}
    shared = {n: given[n] for n in SHARED_INPUTS}
    per_example = {n: given[n] for n in ['x']}
    grad_fn = _jax.value_and_grad(_loss, argnums=(0, 1))

    def one_microbatch(ex, loss_target):
        ex = dict(ex)
        diff = ex.pop(TWIN_DIFF_INPUT)
        return grad_fn(weights, diff, {**shared, **ex}, loss_target)

    if N_MICROBATCH == 1:
        loss, (grad_w, grad_x) = one_microbatch(per_example, given["loss_target"])
    else:
        def body(carry, xs):
            loss_sum, grad_sum = carry
            l_k, (gw_k, gx_k) = one_microbatch(xs[0], xs[1])
            with _jax.named_scope("update"):
                return (loss_sum + l_k, _jax.tree.map(_jnp.add, grad_sum, gw_k)), gx_k

        init = (_jnp.zeros((), _jnp.float32), _jax.tree.map(_jnp.zeros_like, weights))
        (loss, grad_w), grad_x = _jax.lax.scan(body, init, (per_example, given["loss_target"]))
    with _jax.named_scope("update"):
        delta_w, new_m, new_v = {}, {}, {}
        for n in TWIN_WEIGHTS:
            delta_w[n], new_m[n], new_v[n] = _adamw(weights[n], grad_w[n], given["m_" + n], given["v_" + n])
    return (loss, grad_x, *[grad_w[n] for n in TWIN_WEIGHTS], *[delta_w[n] for n in TWIN_WEIGHTS],
            *[new_m[n] for n in TWIN_WEIGHTS], *[new_v[n] for n in TWIN_WEIGHTS])
```

```python
import functools
import math

import jax
import jax.numpy as jnp
from jax import lax
from jax.experimental import pallas as pl
from jax.experimental.pallas import tpu as pltpu

F32 = jnp.float32
BF16 = jnp.bfloat16
EPS = 1e-6
HEAD_DIM = 128
POOL_WINDOWS = (2, 4, 8, 16)
SCONV_K = 3
CONF_K = 31
HALO = 32
N_DEV = 8
VMEM_LIMIT = 56 * 1024 * 1024
MESH = pl.DeviceIdType.MESH

ADAM_LR = 0.001
ADAM_B1 = 0.9
ADAM_B2 = 0.999
ADAM_EPS = 1e-08
ADAM_WD = 0.01
ADAM_STEP = 10


def _params(*sem):
    return pltpu.CompilerParams(dimension_semantics=sem, vmem_limit_bytes=VMEM_LIMIT)


def _sigmoid(v):
    return 1.0 / (1.0 + jnp.exp(-v))


def _silu(v):
    return v * _sigmoid(v)


def _silu_and_grad(v):
    s = _sigmoid(v)
    return v * s, s * (1.0 + v * (1.0 - s))


def _rowsum8(v):
    r, c = v.shape
    return jnp.sum(v.reshape(r // 8, 8, c), axis=0)


def _tap_before(xx, i, rows):
    if i == 0:
        return xx[HALO:HALO + rows]
    return pltpu.roll(xx, i, 0)[HALO:HALO + rows]


def _tap_after(xx, i, rows):
    if i == 0:
        return xx[0:rows]
    return pltpu.roll(xx, xx.shape[0] - i, 0)[0:rows]


def rms_fwd(x, g, name, tm=256):
    s, d = x.shape

    def body(x_ref, g_ref, h_ref):
        xv = x_ref[...]
        r = lax.rsqrt(jnp.mean(xv * xv, axis=-1, keepdims=True) + EPS)
        h_ref[...] = (xv * r * g_ref[...]).astype(BF16)

    return pl.pallas_call(
        body, name=name, grid=(s // tm,),
        in_specs=[pl.BlockSpec((tm, d), lambda i: (i, 0)), pl.BlockSpec((1, d), lambda i: (0, 0))],
        out_specs=pl.BlockSpec((tm, d), lambda i: (i, 0)),
        out_shape=jax.ShapeDtypeStruct((s, d), BF16),
        compiler_params=_params("parallel"),
    )(x, g)


def postnorm_fwd(x, o, g, name, tm=256):
    s, d = x.shape

    def body(x_ref, o_ref, g_ref, y_ref):
        ov = o_ref[...]
        r = lax.rsqrt(jnp.mean(ov * ov, axis=-1, keepdims=True) + EPS)
        y_ref[...] = x_ref[...] + ov * r * g_ref[...]

    return pl.pallas_call(
        body, name=name, grid=(s // tm,),
        in_specs=[pl.BlockSpec((tm, d), lambda i: (i, 0)), pl.BlockSpec((tm, d), lambda i: (i, 0)),
                  pl.BlockSpec((1, d), lambda i: (0, 0))],
        out_specs=pl.BlockSpec((tm, d), lambda i: (i, 0)),
        out_shape=jax.ShapeDtypeStruct((s, d), F32),
        compiler_params=_params("parallel"),
    )(x, o, g)


def final_fwd_bwd(x1, o, g, target, name, tm=256):
    s, d = x1.shape
    n = s // tm

    def body(x_ref, o_ref, g_ref, t_ref, loss_ref, gx_ref, do_ref, dg_ref, lacc, gacc):
        i = pl.program_id(0)

        @pl.when(i == 0)
        def _():
            lacc[...] = jnp.zeros_like(lacc)
            gacc[...] = jnp.zeros_like(gacc)

        ov = o_ref[...]
        gv = g_ref[...]
        r = lax.rsqrt(jnp.mean(ov * ov, axis=-1, keepdims=True) + EPS)
        oh = ov * r
        diff = x_ref[...] + oh * gv - t_ref[...]
        lacc[...] += _rowsum8(diff * diff)
        gx = diff * (1.0 / d)
        gx_ref[...] = gx
        gacc[...] += _rowsum8(gx * oh)
        dn = gx * gv
        do_ref[...] = (r * (dn - oh * jnp.mean(dn * oh, axis=-1, keepdims=True))).astype(BF16)

        @pl.when(i == n - 1)
        def _():
            tot = jnp.sum(jnp.sum(lacc[...], axis=0, keepdims=True), axis=1, keepdims=True)
            loss_ref[...] = jnp.broadcast_to(tot * (0.5 / d), loss_ref.shape)
            dg_ref[...] = jnp.sum(gacc[...], axis=0, keepdims=True)

    row = pl.BlockSpec((tm, d), lambda i: (i, 0))
    vec = pl.BlockSpec((1, d), lambda i: (0, 0))
    return pl.pallas_call(
        body, name=name, grid=(n,),
        in_specs=[row, row, vec, row],
        out_specs=[pl.BlockSpec((8, 128), lambda i: (0, 0)), row, row, vec],
        out_shape=[jax.ShapeDtypeStruct((8, 128), F32), jax.ShapeDtypeStruct((s, d), F32),
                   jax.ShapeDtypeStruct((s, d), BF16), jax.ShapeDtypeStruct((1, d), F32)],
        scratch_shapes=[pltpu.VMEM((8, d), F32), pltpu.VMEM((8, d), F32)],
        compiler_params=_params("arbitrary"),
    )(x1, o, g, target)


def norm_bwd(dy, inp, g, resid, out_dtype, name, tm=256):
    s, d = inp.shape
    n = s // tm
    has_resid = resid is not None

    def body(*refs):
        if has_resid:
            dy_ref, x_ref, g_ref, r_ref, dx_ref, dg_ref, gacc = refs
        else:
            dy_ref, x_ref, g_ref, dx_ref, dg_ref, gacc = refs
        i = pl.program_id(0)

        @pl.when(i == 0)
        def _():
            gacc[...] = jnp.zeros_like(gacc)

        xv = x_ref[...]
        dyv = dy_ref[...].astype(F32)
        r = lax.rsqrt(jnp.mean(xv * xv, axis=-1, keepdims=True) + EPS)
        xh = xv * r
        gacc[...] += _rowsum8(dyv * xh)
        dn = dyv * g_ref[...]
        dx = r * (dn - xh * jnp.mean(dn * xh, axis=-1, keepdims=True))
        if has_resid:
            dx = dx + r_ref[...]
        dx_ref[...] = dx.astype(out_dtype)

        @pl.when(i == n - 1)
        def _():
            dg_ref[...] = jnp.sum(gacc[...], axis=0, keepdims=True)

    row = pl.BlockSpec((tm, d), lambda i: (i, 0))
    vec = pl.BlockSpec((1, d), lambda i: (0, 0))
    args = [dy, inp, g] + ([resid] if has_resid else [])
    return pl.pallas_call(
        body, name=name, grid=(n,),
        in_specs=[row, row, vec] + ([row] if has_resid else []),
        out_specs=[row, vec],
        out_shape=[jax.ShapeDtypeStruct((s, d), out_dtype), jax.ShapeDtypeStruct((1, d), F32)],
        scratch_shapes=[pltpu.VMEM((8, d), F32)],
        compiler_params=_params("arbitrary"),
    )(*args)


def mm_nn(a, w, out_dtype, name, tm=512, tn=None):
    m, k = a.shape
    ns, _, n = w.shape
    tn = n if tn is None else tn
    nj = n // tn

    def body(a_ref, w_ref, o_ref):
        o_ref[...] = jnp.dot(a_ref[...], w_ref[0], preferred_element_type=F32).astype(out_dtype)

    return pl.pallas_call(
        body, name=name, grid=(ns, nj, m // tm),
        in_specs=[pl.BlockSpec((tm, k), lambda s, j, i: (i, 0)),
                  pl.BlockSpec((1, k, tn), lambda s, j, i: (s, 0, j))],
        out_specs=pl.BlockSpec((tm, tn), lambda s, j, i: (i, s * nj + j)),
        out_shape=jax.ShapeDtypeStruct((m, ns * n), out_dtype),
        compiler_params=_params("parallel", "parallel", "parallel"),
    )(a, w)


def mm_nt(a, w, out_dtype, name, tm=512, tn=None):
    m = a.shape[0]
    ns, k, n = w.shape
    tn = n if tn is None else tn
    nj = n // tn
    steps = ns * nj

    def body(a_ref, w_ref, o_ref, acc):
        r = pl.program_id(1)

        @pl.when(r == 0)
        def _():
            acc[...] = jnp.zeros_like(acc)

        acc[...] += lax.dot_general(a_ref[...], w_ref[0], (((1,), (1,)), ((), ())),
                                    preferred_element_type=F32)

        @pl.when(r == steps - 1)
        def _():
            o_ref[...] = acc[...].astype(out_dtype)

    return pl.pallas_call(
        body, name=name, grid=(m // tm, steps),
        in_specs=[pl.BlockSpec((tm, tn), lambda i, r: (i, r)),
                  pl.BlockSpec((1, k, tn), lambda i, r: (r // nj, 0, r % nj))],
        out_specs=pl.BlockSpec((tm, k), lambda i, r: (i, 0)),
        out_shape=jax.ShapeDtypeStruct((m, k), out_dtype),
        scratch_shapes=[pltpu.VMEM((tm, k), F32)],
        compiler_params=_params("parallel", "arbitrary"),
    )(a, w)


def mm_tn(a, b, ns, out_dtype, name, tk=1024, tm=512):
    m, k = a.shape
    n = b.shape[1] // ns
    steps = m // tm

    def body(a_ref, b_ref, o_ref, acc):
        r = pl.program_id(2)

        @pl.when(r == 0)
        def _():
            acc[...] = jnp.zeros_like(acc)

        acc[...] += lax.dot_general(a_ref[...], b_ref[...], (((0,), (0,)), ((), ())),
                                    preferred_element_type=F32)

        @pl.when(r == steps - 1)
        def _():
            o_ref[0] = acc[...].astype(out_dtype)

    return pl.pallas_call(
        body, name=name, grid=(ns, k // tk, steps),
        in_specs=[pl.BlockSpec((tm, tk), lambda s, j, r: (r, j)),
                  pl.BlockSpec((tm, n), lambda s, j, r: (r, s))],
        out_specs=pl.BlockSpec((1, tk, n), lambda s, j, r: (s, j, 0)),
        out_shape=jax.ShapeDtypeStruct((ns, k, n), out_dtype),
        scratch_shapes=[pltpu.VMEM((tk, n), F32)],
        compiler_params=_params("parallel", "parallel", "arbitrary"),
    )(a, b)


def _sb_block(q, kblk, it, scale, tq, tri_ex, c1):
    row = lax.broadcasted_iota(jnp.int32, (tq, tq), 0)
    col = lax.broadcasted_iota(jnp.int32, (tq, tq), 1)
    z = lax.dot_general(q, kblk, (((1,), (1,)), ((), ())), preferred_element_type=F32) * scale
    mask = (col - row) < jnp.where(it > 0, tq, 0)
    sp = jnp.log(1.0 + jnp.exp(-jnp.abs(z)))
    lb = jnp.minimum(z, 0.0) - sp
    l1m = jnp.where(mask, lb - z, 0.0)
    hi = l1m.astype(BF16)
    lo = (l1m - hi.astype(F32)).astype(BF16)
    ls = (jnp.dot(hi, tri_ex, preferred_element_type=F32)
          + jnp.dot(lo, tri_ex, preferred_element_type=F32) + c1)
    return z, mask, lb, l1m, ls


def sb_fwd(p, n_heads, name, tq=128):
    s = p.shape[0]
    h_n = n_heads
    scale = 1.0 / math.sqrt(HEAD_DIM)

    def body(q_ref, k_ref, v_ref, o_ref):
        qi = pl.program_id(1)
        q = q_ref[...]
        row = lax.broadcasted_iota(jnp.int32, (tq, tq), 0)
        col = lax.broadcasted_iota(jnp.int32, (tq, tq), 1)
        tri_ex = jnp.where(row > col, 1.0, 0.0).astype(BF16)

        def step(it, carry):
            c1, acc = carry
            off = pl.multiple_of((qi - it) * tq, tq)
            kblk = k_ref[pl.ds(off, tq), :]
            vblk = v_ref[pl.ds(off, tq), :]
            _, mask, lb, l1m, ls = _sb_block(q, kblk, it, scale, tq, tri_ex, c1)
            w = jnp.where(mask, jnp.exp(lb + ls), 0.0)
            acc = acc + jnp.dot(w.astype(BF16), vblk, preferred_element_type=F32)
            c1 = c1 + jnp.sum(l1m, axis=1, keepdims=True)
            return c1, acc

        _, acc = lax.fori_loop(0, qi + 1, step,
                               (jnp.zeros((tq, 1), F32), jnp.zeros((tq, HEAD_DIM), F32)))
        o_ref[...] = acc

    return pl.pallas_call(
        body, name=name, grid=(h_n, s // tq),
        in_specs=[pl.BlockSpec((tq, HEAD_DIM), lambda h, i: (i, h)),
                  pl.BlockSpec((s, HEAD_DIM), lambda h, i: (0, h_n + h)),
                  pl.BlockSpec((s, HEAD_DIM), lambda h, i: (0, 2 * h_n + h))],
        out_specs=pl.BlockSpec((tq, HEAD_DIM), lambda h, i: (i, h)),
        out_shape=jax.ShapeDtypeStruct((s, h_n * HEAD_DIM), F32),
        compiler_params=_params("parallel", "arbitrary"),
    )(p, p, p)


def sb_bwd(p, a, da, n_heads, name, tq=128):
    s = p.shape[0]
    h_n = n_heads
    nq = s // tq
    scale = 1.0 / math.sqrt(HEAD_DIM)

    def body(q_ref, k_ref, v_ref, a_ref, da_ref, dq_ref, dk_ref, dv_ref, dk_acc, dv_acc):
        qi = pl.program_id(1)

        @pl.when(qi == 0)
        def _():
            dk_acc[...] = jnp.zeros_like(dk_acc)
            dv_acc[...] = jnp.zeros_like(dv_acc)

        q = q_ref[...]
        d_o = da_ref[...]
        tot = jnp.sum(d_o.astype(F32) * a_ref[...], axis=1, keepdims=True)
        row = lax.broadcasted_iota(jnp.int32, (tq, tq), 0)
        col = lax.broadcasted_iota(jnp.int32, (tq, tq), 1)
        tri_ex = jnp.where(row > col, 1.0, 0.0).astype(BF16)
        tri_in = jnp.where(row >= col, 1.0, 0.0).astype(BF16)

        def step(it, carry):
            c1, c2, dq = carry
            off = pl.multiple_of((qi - it) * tq, tq)
            kblk = k_ref[pl.ds(off, tq), :]
            vblk = v_ref[pl.ds(off, tq), :]
            _, mask, lb, l1m, ls = _sb_block(q, kblk, it, scale, tq, tri_ex, c1)
            wr = jnp.where(mask, jnp.exp(lb + ls), 0.0).astype(BF16)
            dw = lax.dot_general(d_o, vblk, (((1,), (1,)), ((), ())), preferred_element_type=F32)
            e = dw * wr.astype(F32)
            e_hi = e.astype(BF16)
            e_lo = (e - e_hi.astype(F32)).astype(BF16)
            later = (jnp.dot(e_hi, tri_in, preferred_element_type=F32)
                     + jnp.dot(e_lo, tri_in, preferred_element_type=F32) + c2)
            beta = jnp.exp(lb)
            dz = jnp.where(mask, e * (1.0 - beta) - beta * (tot - later), 0.0) * scale
            dzb = dz.astype(BF16)
            dq = dq + jnp.dot(dzb, kblk, preferred_element_type=F32)
            dk_acc[pl.ds(off, tq), :] += lax.dot_general(dzb, q, (((0,), (0,)), ((), ())),
                                                         preferred_element_type=F32)
            dv_acc[pl.ds(off, tq), :] += lax.dot_general(wr, d_o, (((0,), (0,)), ((), ())),
                                                         preferred_element_type=F32)
            c1 = c1 + jnp.sum(l1m, axis=1, keepdims=True)
            c2 = c2 + jnp.sum(e, axis=1, keepdims=True)
            return c1, c2, dq

        zero = jnp.zeros((tq, 1), F32)
        _, _, dq = lax.fori_loop(0, qi + 1, step, (zero, zero, jnp.zeros((tq, HEAD_DIM), F32)))
        dq_ref[...] = dq.astype(BF16)

        @pl.when(qi == nq - 1)
        def _():
            dk_ref[...] = dk_acc[...].astype(BF16)
            dv_ref[...] = dv_acc[...].astype(BF16)

    blk = pl.BlockSpec((tq, HEAD_DIM), lambda h, i: (i, h))
    full = pl.BlockSpec((s, HEAD_DIM), lambda h, i: (0, h))
    return pl.pallas_call(
        body, name=name, grid=(h_n, nq),
        in_specs=[blk, pl.BlockSpec((s, HEAD_DIM), lambda h, i: (0, h_n + h)),
                  pl.BlockSpec((s, HEAD_DIM), lambda h, i: (0, 2 * h_n + h)), blk, blk],
        out_specs=[blk, full, full],
        out_shape=[jax.ShapeDtypeStruct((s, h_n * HEAD_DIM), BF16)] * 3,
        scratch_shapes=[pltpu.VMEM((s, HEAD_DIM), F32), pltpu.VMEM((s, HEAD_DIM), F32)],
        compiler_params=_params("parallel", "arbitrary"),
    )(p, p, p, a, da)


def _pool_window(xx, win, r0, rc):
    cur = xx[HALO:HALO + rc]
    ws = cur
    for i in range(1, win):
        ws = ws + _tap_before(xx, i, rc)
    t_idx = r0 + lax.broadcasted_iota(jnp.int32, (rc, 1), 0)
    inv = 1.0 / jnp.minimum(win, t_idx + 1).astype(F32)
    return ws * inv - cur, inv


def even_mix_fwd(a, p, pool_w, pool_scale, name, rc=64):
    s = p.shape[0]
    ng = len(POOL_WINDOWS)
    cw = pool_w.shape[1]
    n_chunks = s // rc

    def body(a_ref, u_ref, g_ref, w_ref, sc_ref, y_ref, upad):
        j = pl.program_id(0)

        @pl.when(j < ng)
        def _():
            def chunk(ci, carry):
                rows = pl.ds(pl.multiple_of(ci * rc, rc), rc)
                y_ref[rows, :] = (a_ref[rows, :] * _silu(g_ref[rows, :].astype(F32))).astype(BF16)
                return carry

            lax.fori_loop(0, n_chunks, chunk, 0)

        for gi, win in enumerate(POOL_WINDOWS):
            @pl.when(j == ng + gi)
            def _(win=win):
                upad[0:HALO, :] = jnp.zeros((HALO, cw), F32)

                def fill(ci, carry):
                    r0 = pl.multiple_of(ci * rc, rc)
                    upad[pl.ds(pl.multiple_of(r0 + HALO, HALO), rc), :] = u_ref[pl.ds(r0, rc), :].astype(F32)
                    return carry

                lax.fori_loop(0, n_chunks, fill, 0)

                def chunk(ci, carry):
                    r0 = pl.multiple_of(ci * rc, rc)
                    rows = pl.ds(r0, rc)
                    pooled, _ = _pool_window(upad[pl.ds(r0, HALO + rc), :], win, r0, rc)
                    t = jnp.dot(pooled.astype(BF16), w_ref[0], preferred_element_type=F32)
                    y_ref[rows, :] = (t * sc_ref[...] * _silu(g_ref[rows, :].astype(F32))).astype(BF16)
                    return carry

                lax.fori_loop(0, n_chunks, chunk, 0)

    grp = lambda j: jnp.maximum(j - ng, 0)
    return pl.pallas_call(
        body, name=name, grid=(2 * ng,),
        in_specs=[pl.BlockSpec((s, cw), lambda j: (0, jnp.minimum(j, ng - 1))),
                  pl.BlockSpec((s, cw), lambda j: (0, 3 * ng + grp(j))),
                  pl.BlockSpec((s, cw), lambda j: (0, 4 * ng + j)),
                  pl.BlockSpec((1, cw, cw), lambda j: (grp(j), 0, 0)),
                  pl.BlockSpec((1, cw), lambda j: (0, grp(j)))],
        out_specs=pl.BlockSpec((s, cw), lambda j: (0, j)),
        out_shape=jax.ShapeDtypeStruct((s, 2 * ng * cw), BF16),
        scratch_shapes=[pltpu.VMEM((HALO + s, cw), F32)],
        compiler_params=_params("arbitrary"),
    )(a, p, p, pool_w, pool_scale)


def even_mix_bwd(dy, a, p, pool_w, pool_scale, name, rc=64):
    s = p.shape[0]
    ng = len(POOL_WINDOWS)
    cw = pool_w.shape[1]
    n_chunks = s // rc

    def body(dy_ref, a_ref, u_ref, g_ref, w_ref, sc_ref, da_ref, du_ref, dg_ref, dw_ref, dsc_ref,
             upad, rpad, dpl, dw_acc, dsc_acc):
        j = pl.program_id(0)

        @pl.when(j < ng)
        def _():
            def chunk(ci, carry):
                rows = pl.ds(pl.multiple_of(ci * rc, rc), rc)
                dyv = dy_ref[rows, :].astype(F32)
                sg, dsg = _silu_and_grad(g_ref[rows, :].astype(F32))
                da_ref[rows, :] = (dyv * sg).astype(BF16)
                dg_ref[rows, :] = (dyv * a_ref[rows, :] * dsg).astype(BF16)
                return carry

            lax.fori_loop(0, n_chunks, chunk, 0)

        for gi, win in enumerate(POOL_WINDOWS):
            @pl.when(j == ng + gi)
            def _(win=win):
                upad[0:HALO, :] = jnp.zeros((HALO, cw), F32)
                rpad[s:s + HALO, :] = jnp.zeros((HALO, cw), F32)
                dw_acc[...] = jnp.zeros_like(dw_acc)
                dsc_acc[...] = jnp.zeros_like(dsc_acc)

                def fill(ci, carry):
                    r0 = pl.multiple_of(ci * rc, rc)
                    upad[pl.ds(pl.multiple_of(r0 + HALO, HALO), rc), :] = u_ref[pl.ds(r0, rc), :].astype(F32)
                    return carry

                lax.fori_loop(0, n_chunks, fill, 0)

                def chunk(ci, carry):
                    r0 = pl.multiple_of(ci * rc, rc)
                    rows = pl.ds(r0, rc)
                    pooled, inv = _pool_window(upad[pl.ds(r0, HALO + rc), :], win, r0, rc)
                    pb = pooled.astype(BF16)
                    wv = w_ref[0]
                    t = jnp.dot(pb, wv, preferred_element_type=F32)
                    scv = sc_ref[...]
                    dyv = dy_ref[rows, :].astype(F32)
                    sg, dsg = _silu_and_grad(g_ref[rows, :].astype(F32))
                    dpo = dyv * sg
                    dg_ref[rows, :] = (dyv * t * scv * dsg).astype(BF16)
                    dsc_acc[...] += _rowsum8(dpo * t)
                    dtb = (dpo * scv).astype(BF16)
                    dw_acc[...] += lax.dot_general(pb, dtb, (((0,), (0,)), ((), ())),
                                                   preferred_element_type=F32)
                    dpooled = lax.dot_general(dtb, wv, (((1,), (1,)), ((), ())),
                                              preferred_element_type=F32)
                    dpl[rows, :] = dpooled
                    rpad[rows, :] = dpooled * inv
                    return carry

                lax.fori_loop(0, n_chunks, chunk, 0)

                def chunk2(ci, carry):
                    r0 = pl.multiple_of(ci * rc, rc)
                    rows = pl.ds(r0, rc)
                    xx = rpad[pl.ds(r0, rc + HALO), :]
                    fs = xx[0:rc]
                    for i in range(1, win):
                        fs = fs + _tap_after(xx, i, rc)
                    du_ref[rows, :] = (fs - dpl[rows, :]).astype(BF16)
                    return carry

                lax.fori_loop(0, n_chunks, chunk2, 0)
                dw_ref[0] = dw_acc[...]
                dsc_ref[...] = jnp.sum(dsc_acc[...], axis=0, keepdims=True)

    grp = lambda j: jnp.maximum(j - ng, 0)
    att = lambda j: jnp.minimum(j, ng - 1)
    return pl.pallas_call(
        body, name=name, grid=(2 * ng,),
        in_specs=[pl.BlockSpec((s, cw), lambda j: (0, j)),
                  pl.BlockSpec((s, cw), lambda j: (0, att(j))),
                  pl.BlockSpec((s, cw), lambda j: (0, 3 * ng + grp(j))),
                  pl.BlockSpec((s, cw), lambda j: (0, 4 * ng + j)),
                  pl.BlockSpec((1, cw, cw), lambda j: (grp(j), 0, 0)),
                  pl.BlockSpec((1, cw), lambda j: (0, grp(j)))],
        out_specs=[pl.BlockSpec((s, cw), lambda j: (0, att(j))),
                   pl.BlockSpec((s, cw), lambda j: (0, grp(j))),
                   pl.BlockSpec((s, cw), lambda j: (0, j)),
                   pl.BlockSpec((1, cw, cw), lambda j: (grp(j), 0, 0)),
                   pl.BlockSpec((1, cw), lambda j: (0, grp(j)))],
        out_shape=[jax.ShapeDtypeStruct((s, ng * cw), BF16), jax.ShapeDtypeStruct((s, ng * cw), BF16),
                   jax.ShapeDtypeStruct((s, 2 * ng * cw), BF16),
                   jax.ShapeDtypeStruct((ng, cw, cw), F32), jax.ShapeDtypeStruct((1, ng * cw), F32)],
        scratch_shapes=[pltpu.VMEM((HALO + s, cw), F32), pltpu.VMEM((s + HALO, cw), F32),
                        pltpu.VMEM((s, cw), F32), pltpu.VMEM((cw, cw), F32), pltpu.VMEM((8, cw), F32)],
        compiler_params=_params("arbitrary"),
    )(dy, a, p, p, pool_w, pool_scale)


def _halo_before(tm):
    return lambda i: jnp.maximum(i * (tm // HALO) - 1, 0)


def _halo_after(tm, s):
    return lambda i: jnp.minimum((i + 1) * (tm // HALO), s // HALO - 1)


def odd_mix_fwd(p, sconv_w, dconv_w, dconv_b, cnorm_g, cnorm_b, name, tm=128):
    s = p.shape[0]
    cw = sconv_w.shape[1]
    n = s // tm
    lanes = 128
    hb = _halo_before(tm)

    def body(hc_ref, hch_ref, bc_ref, cc_ref, cch_ref, ga_ref, gah_ref, gb_ref, gbh_ref, g1_ref, g2_ref,
             sw_ref, dw_ref, db_ref, gam_ref, bet_ref, y_ref, dc_ref):
        first = pl.program_id(0) == 0
        for l in range(cw // lanes):
            cols = slice(l * lanes, (l + 1) * lanes)
            mh = jnp.where(first, 0.0, cch_ref[:, cols].astype(F32) * hch_ref[:, cols].astype(F32))
            mm = cc_ref[:, cols].astype(F32) * hc_ref[:, cols].astype(F32)
            xx = jnp.concatenate([mh, mm], axis=0)
            cv = jnp.zeros((tm, lanes), F32)
            for k in range(SCONV_K):
                cv = cv + sw_ref[k:k + 1, cols] * _tap_before(xx, SCONV_K - 1 - k, tm)
            c_out = bc_ref[:, cols].astype(F32) * cv
            y_ref[:, cols] = (c_out * _silu(g1_ref[:, cols].astype(F32))).astype(BF16)
            dh = jnp.where(first, 0.0, gah_ref[:, cols].astype(F32) * _sigmoid(gbh_ref[:, cols].astype(F32)))
            dm = ga_ref[:, cols].astype(F32) * _sigmoid(gb_ref[:, cols].astype(F32))
            xx = jnp.concatenate([dh, dm], axis=0)
            acc = jnp.zeros((tm, lanes), F32) + db_ref[:, cols]
            for k in range(CONF_K):
                acc = acc + dw_ref[k:k + 1, cols] * _tap_before(xx, CONF_K - 1 - k, tm)
            dc_ref[:, cols] = acc
        rs = 32
        for r in range(tm // rs):
            rows = slice(r * rs, (r + 1) * rs)
            xv = dc_ref[rows, :]
            mu = jnp.mean(xv, axis=-1, keepdims=True)
            xc = xv - mu
            rstd = lax.rsqrt(jnp.mean(xc * xc, axis=-1, keepdims=True) + EPS)
            ln = xc * rstd * gam_ref[...] + bet_ref[...]
            y_ref[rows, cw:2 * cw] = (_silu(ln) * _silu(g2_ref[rows, :].astype(F32))).astype(BF16)

    main = lambda c: pl.BlockSpec((tm, cw), lambda i: (i, c))
    halo = lambda c: pl.BlockSpec((HALO, cw), lambda i: (hb(i), c))
    vec = lambda r: pl.BlockSpec((r, cw), lambda i: (0, 0))
    return pl.pallas_call(
        body, name=name, grid=(n,),
        in_specs=[main(0), halo(0), main(1), main(2), halo(2), main(3), halo(3), main(4), halo(4),
                  main(5), main(6), vec(SCONV_K), vec(CONF_K), vec(1), vec(1), vec(1)],
        out_specs=[pl.BlockSpec((tm, 2 * cw), lambda i: (i, 0)), pl.BlockSpec((tm, cw), lambda i: (i, 0))],
        out_shape=[jax.ShapeDtypeStruct((s, 2 * cw), BF16), jax.ShapeDtypeStruct((s, cw), F32)],
        compiler_params=_params("parallel"),
    )(p, p, p, p, p, p, p, p, p, p, p, sconv_w, dconv_w, dconv_b, cnorm_g, cnorm_b)


def odd_bwd_ln(dy, p, dc, cnorm_g, cnorm_b, name, tm=256):
    s = p.shape[0]
    cw = dc.shape[1]
    n = s // tm
    rs = 32

    def body(dy_ref, g2_ref, dc_ref, gam_ref, bet_ref, ddc_ref, dg_ref, dgam_ref, dbet_ref, gacc, bacc):
        i = pl.program_id(0)

        @pl.when(i == 0)
        def _():
            gacc[...] = jnp.zeros_like(gacc)
            bacc[...] = jnp.zeros_like(bacc)

        def chunk(ci, carry):
            rows = pl.ds(pl.multiple_of(ci * rs, rs), rs)
            xv = dc_ref[rows, :]
            mu = jnp.mean(xv, axis=-1, keepdims=True)
            xc = xv - mu
            rstd = lax.rsqrt(jnp.mean(xc * xc, axis=-1, keepdims=True) + EPS)
            xh = xc * rstd
            gam = gam_ref[...]
            sl, dsl = _silu_and_grad(xh * gam + bet_ref[...])
            sg, dsg = _silu_and_grad(g2_ref[rows, :].astype(F32))
            dyv = dy_ref[rows, :].astype(F32)
            dg_ref[rows, :] = (dyv * sl * dsg).astype(BF16)
            dln = dyv * sg * dsl
            gacc[...] += _rowsum8(dln * xh)
            bacc[...] += _rowsum8(dln)
            dxh = dln * gam
            ddc_ref[rows, :] = rstd * (dxh - jnp.mean(dxh, axis=-1, keepdims=True)
                                       - xh * jnp.mean(dxh * xh, axis=-1, keepdims=True))
            return carry

        lax.fori_loop(0, tm // rs, chunk, 0)

        @pl.when(i == n - 1)
        def _():
            dgam_ref[...] = jnp.sum(gacc[...], axis=0, keepdims=True)
            dbet_ref[...] = jnp.sum(bacc[...], axis=0, keepdims=True)

    vec = pl.BlockSpec((1, cw), lambda i: (0, 0))
    return pl.pallas_call(
        body, name=name, grid=(n,),
        in_specs=[pl.BlockSpec((tm, cw), lambda i: (i, 1)), pl.BlockSpec((tm, cw), lambda i: (i, 6)),
                  pl.BlockSpec((tm, cw), lambda i: (i, 0)), vec, vec],
        out_specs=[pl.BlockSpec((tm, cw), lambda i: (i, 0)), pl.BlockSpec((tm, cw), lambda i: (i, 0)), vec, vec],
        out_shape=[jax.ShapeDtypeStruct((s, cw), F32), jax.ShapeDtypeStruct((s, cw), BF16),
                   jax.ShapeDtypeStruct((1, cw), F32), jax.ShapeDtypeStruct((1, cw), F32)],
        scratch_shapes=[pltpu.VMEM((8, cw), F32), pltpu.VMEM((8, cw), F32)],
        compiler_params=_params("arbitrary"),
    )(dy, p, dc, cnorm_g, cnorm_b)


def odd_bwd_conv(dy, p, ddc, dg2, sconv_w, dconv_w, name, tm=128):
    s = p.shape[0]
    cw = ddc.shape[1]
    n = s // tm
    lanes = 128
    hb = _halo_before(tm)
    ha = _halo_after(tm, s)

    def body(dy_ref, dya_ref, g1_ref, g1a_ref, bc_ref, bca_ref, hc_ref, hch_ref, cc_ref, cch_ref,
             ddc_ref, ddca_ref, ga_ref, gah_ref, gb_ref, gbh_ref, dg2_ref, sw_ref, dw_ref,
             dp_ref, dsw_ref, ddw_ref, ddb_ref, sw_acc, dw_acc, db_acc):
        i = pl.program_id(0)
        first = i == 0
        last = i == n - 1

        @pl.when(first)
        def _():
            sw_acc[...] = jnp.zeros_like(sw_acc)
            dw_acc[...] = jnp.zeros_like(dw_acc)
            db_acc[...] = jnp.zeros_like(db_acc)

        for l in range(cw // lanes):
            cols = slice(l * lanes, (l + 1) * lanes)
            mh = jnp.where(first, 0.0, cch_ref[:, cols].astype(F32) * hch_ref[:, cols].astype(F32))
            hcv = hc_ref[:, cols].astype(F32)
            ccv = cc_ref[:, cols].astype(F32)
            xx = jnp.concatenate([mh, ccv * hcv], axis=0)
            taps = [_tap_before(xx, SCONV_K - 1 - k, tm) for k in range(SCONV_K)]
            cv = jnp.zeros((tm, lanes), F32)
            for k in range(SCONV_K):
                cv = cv + sw_ref[k:k + 1, cols] * taps[k]
            bcv = bc_ref[:, cols].astype(F32)
            dyv = dy_ref[:, cols].astype(F32)
            sg, dsg = _silu_and_grad(g1_ref[:, cols].astype(F32))
            dco = dyv * sg
            dp_ref[:, 5 * cw + l * lanes:5 * cw + (l + 1) * lanes] = (dyv * bcv * cv * dsg).astype(BF16)
            dp_ref[:, cw + l * lanes:cw + (l + 1) * lanes] = (dco * cv).astype(BF16)
            dcv = dco * bcv
            for k in range(SCONV_K):
                sw_acc[k * 8:(k + 1) * 8, cols] += _rowsum8(dcv * taps[k])
            dcv_a = jnp.where(last, 0.0, dya_ref[:, cols].astype(F32) * _silu(g1a_ref[:, cols].astype(F32))
                              * bca_ref[:, cols].astype(F32))
            xx = jnp.concatenate([dcv, dcv_a], axis=0)
            dm = jnp.zeros((tm, lanes), F32)
            for k in range(SCONV_K):
                dm = dm + sw_ref[k:k + 1, cols] * _tap_after(xx, SCONV_K - 1 - k, tm)
            dp_ref[:, l * lanes:(l + 1) * lanes] = (dm * ccv).astype(BF16)
            dp_ref[:, 2 * cw + l * lanes:2 * cw + (l + 1) * lanes] = (dm * hcv).astype(BF16)
            gav = ga_ref[:, cols].astype(F32)
            sb = _sigmoid(gb_ref[:, cols].astype(F32))
            dh = jnp.where(first, 0.0, gah_ref[:, cols].astype(F32) * _sigmoid(gbh_ref[:, cols].astype(F32)))
            xx = jnp.concatenate([dh, gav * sb], axis=0)
            ddcv = ddc_ref[:, cols]
            db_acc[:, cols] += _rowsum8(ddcv)
            for k in range(CONF_K):
                dw_acc[k * 8:(k + 1) * 8, cols] += _rowsum8(ddcv * _tap_before(xx, CONF_K - 1 - k, tm))
            ddc_a = jnp.where(last, 0.0, ddca_ref[:, cols])
            xx = jnp.concatenate([ddcv, ddc_a], axis=0)
            dgl = jnp.zeros((tm, lanes), F32)
            for k in range(CONF_K):
                dgl = dgl + dw_ref[k:k + 1, cols] * _tap_after(xx, CONF_K - 1 - k, tm)
            dp_ref[:, 3 * cw + l * lanes:3 * cw + (l + 1) * lanes] = (dgl * sb).astype(BF16)
            dp_ref[:, 4 * cw + l * lanes:4 * cw + (l + 1) * lanes] = (dgl * gav * sb * (1.0 - sb)).astype(BF16)
        dp_ref[:, 6 * cw:7 * cw] = dg2_ref[...]

        @pl.when(last)
        def _():
            for k in range(SCONV_K):
                dsw_ref[k:k + 1, :] = jnp.sum(sw_acc[k * 8:(k + 1) * 8, :], axis=0, keepdims=True)
            for k in range(CONF_K):
                ddw_ref[k:k + 1, :] = jnp.sum(dw_acc[k * 8:(k + 1) * 8, :], axis=0, keepdims=True)
            ddb_ref[...] = jnp.sum(db_acc[...], axis=0, keepdims=True)

    def main(c):
        return pl.BlockSpec((tm, cw), lambda i: (i, c))

    def before(c):
        return pl.BlockSpec((HALO, cw), lambda i: (hb(i), c))

    def after(c):
        return pl.BlockSpec((HALO, cw), lambda i: (ha(i), c))

    def vec(r):
        return pl.BlockSpec((r, cw), lambda i: (0, 0))

    return pl.pallas_call(
        body, name=name, grid=(n,),
        in_specs=[main(0), after(0), main(5), after(5), main(1), after(1), main(0), before(0), main(2), before(2),
                  main(0), after(0), main(3), before(3), main(4), before(4), main(0), vec(SCONV_K), vec(CONF_K)],
        out_specs=[pl.BlockSpec((tm, 7 * cw), lambda i: (i, 0)), vec(SCONV_K), vec(CONF_K), vec(1)],
        out_shape=[jax.ShapeDtypeStruct((s, 7 * cw), BF16), jax.ShapeDtypeStruct((SCONV_K, cw), F32),
                   jax.ShapeDtypeStruct((CONF_K, cw), F32), jax.ShapeDtypeStruct((1, cw), F32)],
        scratch_shapes=[pltpu.VMEM((8 * SCONV_K, cw), F32), pltpu.VMEM((8 * CONF_K, cw), F32),
                        pltpu.VMEM((8, cw), F32)],
        compiler_params=_params("arbitrary"),
    )(dy, dy, p, p, p, p, p, p, p, p, ddc, ddc, p, p, p, p, dg2, sconv_w, dconv_w)


_ANY = pl.BlockSpec(memory_space=pl.ANY)


def _place():
    return lax.axis_index("x"), lax.axis_index("y"), lax.axis_index("c")


def all_gather(arrs, name):
    n = len(arrs)

    def body(*refs):
        ins, outs = refs[:n], refs[n:2 * n]
        send_sems, recv_sems, local_sems = refs[2 * n:]
        x, y, c = _place()
        me, sibling = (x, y, c), (x, y, 1 - c)
        chips = [(1 - x, y), (x, 1 - y), (1 - x, 1 - y)]

        def copy(a, k, block, to, src=None):
            px, py, pc = block
            dst = outs[a].at[4 * px + 2 * py + pc]
            return pltpu.make_async_remote_copy(
                src_ref=dst if src is None else src, dst_ref=dst,
                send_sem=send_sems.at[7 * a + k], recv_sem=recv_sems.at[7 * a + k],
                device_id=to, device_id_type=MESH)

        mine = [pltpu.make_async_copy(ins[a], outs[a].at[4 * x + 2 * y + c], local_sems.at[a]) for a in range(n)]
        for cp in mine:
            cp.start()
        first = []
        for a in range(n):
            first.append(copy(a, 0, me, sibling, src=ins[a]))
            first += [copy(a, 1 + j, me, (*chip, c), src=ins[a]) for j, chip in enumerate(chips)]
        for cp in first:
            cp.start()
        passed = []
        for a in range(n):
            for j, chip in enumerate(chips):
                copy(a, 1 + j, (*chip, c), me).wait_recv()
                cp = copy(a, 4 + j, (*chip, c), sibling)
                cp.start()
                passed.append(cp)
        for a in range(n):
            copy(a, 0, sibling, me).wait_recv()
            for j, chip in enumerate(chips):
                copy(a, 4 + j, (*chip, 1 - c), me).wait_recv()
        for cp in first + passed:
            cp.wait_send()
        for cp in mine:
            cp.wait()

    return pl.pallas_call(
        body, name=name,
        out_shape=[jax.ShapeDtypeStruct((N_DEV,) + a.shape, a.dtype) for a in arrs],
        in_specs=[_ANY] * n, out_specs=[_ANY] * n,
        scratch_shapes=[pltpu.SemaphoreType.DMA((7 * n,)), pltpu.SemaphoreType.DMA((7 * n,)),
                        pltpu.SemaphoreType.DMA((n,))],
    )(*arrs)


def sibling_exchange(arrs, name):
    n = len(arrs)

    def body(*refs):
        ins, outs = refs[:n], refs[n:2 * n]
        send_sems, recv_sems = refs[2 * n:]
        x, y, c = _place()
        cps = [pltpu.make_async_remote_copy(
            src_ref=ins[a].at[:, pl.ds(1 - c, 1)], dst_ref=outs[a],
            send_sem=send_sems.at[a], recv_sem=recv_sems.at[a],
            device_id=(x, y, 1 - c), device_id_type=MESH) for a in range(n)]
        for cp in cps:
            cp.start()
        for cp in cps:
            cp.wait()

    return pl.pallas_call(
        body, name=name,
        out_shape=[jax.ShapeDtypeStruct((4, 1) + a.shape[2:], a.dtype) for a in arrs],
        in_specs=[_ANY] * n, out_specs=[_ANY] * n,
        scratch_shapes=[pltpu.SemaphoreType.DMA((n,)), pltpu.SemaphoreType.DMA((n,))],
    )(*arrs)


def chip_exchange(arrs, name):
    n = len(arrs)

    def body(*refs):
        ins, outs = refs[:n], refs[n:2 * n]
        send_sems, recv_sems, local_sems = refs[2 * n:]
        x, y, c = _place()
        kc = 2 * x + y
        chips = [(1 - x, y), (x, 1 - y), (1 - x, 1 - y)]

        def copy(a, j, chip):
            px, py = chip
            return pltpu.make_async_remote_copy(
                src_ref=ins[a].at[2 * px + py], dst_ref=outs[a].at[kc],
                send_sem=send_sems.at[3 * a + j], recv_sem=recv_sems.at[3 * a + j],
                device_id=(px, py, c), device_id_type=MESH)

        def arrival(a, j, chip):
            px, py = chip
            return pltpu.make_async_remote_copy(
                src_ref=ins[a].at[2 * px + py], dst_ref=outs[a].at[2 * px + py],
                send_sem=send_sems.at[3 * a + j], recv_sem=recv_sems.at[3 * a + j],
                device_id=(px, py, c), device_id_type=MESH)

        mine = [pltpu.make_async_copy(ins[a].at[kc], outs[a].at[kc], local_sems.at[a]) for a in range(n)]
        sends = [copy(a, j, chip) for a in range(n) for j, chip in enumerate(chips)]
        for cp in mine + sends:
            cp.start()
        for a in range(n):
            for j, chip in enumerate(chips):
                arrival(a, j, chip).wait_recv()
        for cp in sends:
            cp.wait_send()
        for cp in mine:
            cp.wait()

    return pl.pallas_call(
        body, name=name,
        out_shape=[jax.ShapeDtypeStruct(a.shape, a.dtype) for a in arrs],
        in_specs=[_ANY] * n, out_specs=[_ANY] * n,
        scratch_shapes=[pltpu.SemaphoreType.DMA((3 * n,)), pltpu.SemaphoreType.DMA((3 * n,)),
                        pltpu.SemaphoreType.DMA((n,))],
    )(*arrs)


def pair_add(own, recv, core, name):
    _, _, r, c = own.shape
    tr = min(r, 512)

    def body(core_ref, own_ref, recv_ref, o_ref):
        del core_ref
        o_ref[...] = (own_ref[...].astype(F32) + recv_ref[...].astype(F32)).astype(BF16)

    return pl.pallas_call(
        body, name=name,
        grid_spec=pltpu.PrefetchScalarGridSpec(
            num_scalar_prefetch=1, grid=(4, r // tr),
            in_specs=[pl.BlockSpec((None, None, tr, c), lambda k, i, core_ref: (k, core_ref[0], i, 0)),
                      pl.BlockSpec((None, None, tr, c), lambda k, i, core_ref: (k, 0, i, 0))],
            out_specs=pl.BlockSpec((None, tr, c), lambda k, i, core_ref: (k, i, 0))),
        out_shape=jax.ShapeDtypeStruct((4, r, c), BF16),
        compiler_params=_params("parallel", "parallel"),
    )(core, own, recv)


def _adamw_math(w, g, m, v):
    m2 = ADAM_B1 * m + (1.0 - ADAM_B1) * g
    v2 = ADAM_B2 * v + (1.0 - ADAM_B2) * (g * g)
    m_hat = m2 / (1.0 - ADAM_B1 ** ADAM_STEP)
    v_hat = v2 / (1.0 - ADAM_B2 ** ADAM_STEP)
    delta = -ADAM_LR * (m_hat / (jnp.sqrt(v_hat) + ADAM_EPS) + ADAM_WD * w)
    return delta, m2, v2


def adamw_big(w, m, v, parts, name):
    r, c = w.shape
    tr = min(r, 256)

    def body(w_ref, m_ref, v_ref, p_ref, g_ref, d_ref, m2_ref, v2_ref):
        g = ((p_ref[0].astype(F32) + p_ref[1].astype(F32)) + p_ref[2].astype(F32)) + p_ref[3].astype(F32)
        delta, m2, v2 = _adamw_math(w_ref[...], g, m_ref[...], v_ref[...])
        g_ref[...] = g
        d_ref[...] = delta
        m2_ref[...] = m2
        v2_ref[...] = v2

    row = pl.BlockSpec((tr, c), lambda i: (i, 0))
    return pl.pallas_call(
        body, name=name, grid=(r // tr,),
        in_specs=[row, row, row, pl.BlockSpec((4, tr, c), lambda i: (0, i, 0))],
        out_specs=[row] * 4,
        out_shape=[jax.ShapeDtypeStruct((r, c), F32)] * 4,
        compiler_params=_params("parallel"),
    )(w, m, v, parts)


def sum_devices(g8, name):
    def body(g_ref, o_ref):
        tot = g_ref[0]
        for k in range(1, N_DEV):
            tot = tot + g_ref[k]
        o_ref[...] = tot

    return pl.pallas_call(body, name=name, out_shape=jax.ShapeDtypeStruct(g8.shape[1:], F32))(g8)


def adamw_small(ws, gs, ms, vs, name):
    n = len(ws)

    def body(*refs):
        w_r, g_r, m_r, v_r = refs[:n], refs[n:2 * n], refs[2 * n:3 * n], refs[3 * n:4 * n]
        d_o, m_o, v_o = refs[4 * n:5 * n], refs[5 * n:6 * n], refs[6 * n:7 * n]
        for k in range(n):
            delta, m2, v2 = _adamw_math(w_r[k][...], g_r[k][...], m_r[k][...], v_r[k][...])
            d_o[k][...] = delta
            m_o[k][...] = m2
            v_o[k][...] = v2

    shapes = [jax.ShapeDtypeStruct(w.shape, F32) for w in ws]
    outs = pl.pallas_call(body, name=name, out_shape=shapes * 3)(*ws, *gs, *ms, *vs)
    return outs[:n], outs[n:2 * n], outs[2 * n:]


def _rows128(a):
    return a.reshape(-1, 128)


def _pad_rows(a, rows):
    return jnp.pad(a, ((0, rows - a.shape[0]), (0, 0)))


def kernel(x, ln_pre_even, w_in_even, pool_w, pool_scale, w_out_even, ln_post_even, ln_pre_odd, w_in_odd, sconv_w, dconv_w, dconv_b, cnorm_g, cnorm_b, w_out_odd, ln_post_odd, loss_target, m_ln_pre_even, m_w_in_even, m_pool_w, m_pool_scale, m_w_out_even, m_ln_post_even, m_ln_pre_odd, m_w_in_odd, m_sconv_w, m_dconv_w, m_dconv_b, m_cnorm_g, m_cnorm_b, m_w_out_odd, m_ln_post_odd, v_ln_pre_even, v_w_in_even, v_pool_w, v_pool_scale, v_w_out_even, v_ln_post_even, v_ln_pre_odd, v_w_in_odd, v_sconv_w, v_dconv_w, v_dconv_b, v_cnorm_g, v_cnorm_b, v_w_out_odd, v_ln_post_odd):
    xs = x[0]
    tgt = loss_target[0]
    s, d = xs.shape
    half = d // 2
    n_heads = half // HEAD_DIM
    ng = len(POOL_WINDOWS)
    cwp = half // ng
    dev = 4 * lax.axis_index("x") + 2 * lax.axis_index("y") + lax.axis_index("c")
    core = lax.axis_index("c").astype(jnp.int32).reshape(1)

    n_in_e = w_in_even.shape[2]
    n_in_o = w_in_odd.shape[2]
    r_out = w_out_even.shape[1]
    pr = pool_w.shape[2]
    cl = sconv_w.shape[2]
    small_parts = [(_rows128(ln_pre_odd), 8), (sconv_w[0], 8), (dconv_w[0], 32), (dconv_b, 8),
                   (cnorm_g, 8), (cnorm_b, 8), (_rows128(ln_post_odd), 8)]
    small_local = jnp.concatenate([_pad_rows(a, r) for a, r in small_parts], axis=0)
    g_wie, g_pw, g_woe, g_wio, g_woo, g_small = all_gather(
        [w_in_even[0].astype(BF16), pool_w[0].reshape(ng * pr, cwp).astype(BF16), w_out_even[0].astype(BF16),
         w_in_odd[0].astype(BF16), w_out_odd[0].astype(BF16), small_local], "ag_weights")
    pool_full = g_pw.reshape(N_DEV, ng, pr, cwp).transpose(1, 0, 2, 3).reshape(ng, cwp, cwp)
    w_out_e = g_woe.reshape(1, d, d)
    w_out_o = g_woo.reshape(1, d, d)
    nl = ln_pre_odd.shape[1] // 128

    def chan(lo, rows):
        return g_small[:, lo:lo + rows].transpose(1, 0, 2).reshape(rows, N_DEV * cl)

    ln_pre_odd_f = g_small[:, 0:nl].reshape(1, d)
    sconv_f = chan(8, SCONV_K)
    dconv_f = chan(16, CONF_K)
    dconv_b_f = chan(48, 1)
    cnorm_g_f = chan(56, 1)
    cnorm_b_f = chan(64, 1)
    ln_post_odd_f = g_small[:, 72:72 + nl].reshape(1, d)

    (loss_blk, grad_x, dw_in_e, dpool, dw_out_e, dw_in_o, dw_out_o, small_g) = _fwd_bwd(
        xs, tgt, ln_pre_even, g_wie, pool_full, pool_scale, w_out_e, ln_post_even, ln_pre_odd_f, g_wio,
        sconv_f, dconv_f, dconv_b_f, cnorm_g_f, cnorm_b_f, w_out_o, ln_post_odd_f)
    loss = lax.psum(loss_blk[0, 0], ("x", "y", "c"))
    big = [(w_in_even, m_w_in_even, v_w_in_even, (d, w_in_even.shape[2])),
           (pool_w, m_pool_w, v_pool_w, (ng * pr, cwp)),
           (w_out_even, m_w_out_even, v_w_out_even, (r_out, d)),
           (w_in_odd, m_w_in_odd, v_w_in_odd, (d, w_in_odd.shape[2])),
           (w_out_odd, m_w_out_odd, v_w_out_odd, (r_out, d))]
    small_w = [ln_pre_even, pool_scale, ln_post_even, ln_pre_odd, sconv_w[0], dconv_w[0], dconv_b, cnorm_g, cnorm_b, ln_post_odd]
    small_m = [m_ln_pre_even, m_pool_scale, m_ln_post_even, m_ln_pre_odd, m_sconv_w[0], m_dconv_w[0], m_dconv_b, m_cnorm_g, m_cnorm_b, m_ln_post_odd]
    small_v = [v_ln_pre_even, v_pool_scale, v_ln_post_even, v_ln_pre_odd, v_sconv_w[0], v_dconv_w[0], v_dconv_b, v_cnorm_g, v_cnorm_b, v_ln_post_odd]
    dpool_slabs = dpool.astype(BF16).reshape(ng, N_DEV, pr, cwp).transpose(1, 0, 2, 3).reshape(N_DEV, ng * pr, cwp)
    partial_w = [dw_in_e, dpool_slabs, dw_out_e.reshape(N_DEV, r_out, d), dw_in_o, dw_out_o.reshape(N_DEV, r_out, d)]
    big_out, small_out = _reduce_and_update(partial_w, big, small_g, small_w, small_m, small_v, dev, core, d, cl)
    (g_wie_o, d_wie, m_wie, v_wie), (g_pw_o, d_pw, m_pw, v_pw), (g_woe_o, d_woe, m_woe, v_woe), \
        (g_wio_o, d_wio, m_wio, v_wio), (g_woo_o, d_woo, m_woo, v_woo) = big_out
    sg, sd, sm, sv = small_out

    def order(small, wie, pw, woe, wio, woo):
        return [small[0], wie, pw, small[1], woe, small[2], small[3], wio, small[4], small[5], small[6],
                small[7], small[8], woo, small[9]]

    grads = order(sg, g_wie_o, g_pw_o, g_woe_o, g_wio_o, g_woo_o)
    deltas = order(sd, d_wie, d_pw, d_woe, d_wio, d_woo)
    new_m = order(sm, m_wie, m_pw, m_woe, m_wio, m_woo)
    new_v = order(sv, v_wie, v_pw, v_woe, v_wio, v_woo)
    return (loss, grad_x[None], *grads, *deltas, *new_m, *new_v)


def _fwd_bwd(xs, tgt, ln_pre_even, g_wie, pool_full, pool_scale, w_out_e, ln_post_even, ln_pre_odd_f, g_wio,
             sconv_f, dconv_f, dconv_b_f, cnorm_g_f, cnorm_b_f, w_out_o, ln_post_odd_f):
    n_heads = xs.shape[1] // 2 // HEAD_DIM
    h0 = rms_fwd(xs, ln_pre_even, "rms_pre_even")
    p0 = mm_nn(h0, g_wie, BF16, "in_proj_even")
    a0 = sb_fwd(p0, n_heads, "sb_fwd")
    y0 = even_mix_fwd(a0, p0, pool_full, pool_scale, "even_mix_fwd")
    o0 = mm_nn(y0, w_out_e, F32, "out_proj_even", tn=512)
    x1 = postnorm_fwd(xs, o0, ln_post_even, "post_even")
    h1 = rms_fwd(x1, ln_pre_odd_f, "rms_pre_odd")
    p1 = mm_nn(h1, g_wio, BF16, "in_proj_odd")
    y1, dc = odd_mix_fwd(p1, sconv_f, dconv_f, dconv_b_f, cnorm_g_f, cnorm_b_f, "odd_mix_fwd")
    o1 = mm_nn(y1, w_out_o, F32, "out_proj_odd", tn=512)
    loss_blk, gx2, do1, dg_post_odd = final_fwd_bwd(x1, o1, ln_post_odd_f, tgt, "post_odd_loss")

    dw_out_o = mm_tn(y1, do1, 1, BF16, "dw_out_odd", tk=512)
    dy1 = mm_nt(do1, w_out_o, BF16, "dy_odd", tn=512)
    ddc, dg2, dgam, dbet = odd_bwd_ln(dy1, p1, dc, cnorm_g_f, cnorm_b_f, "odd_bwd_ln")
    dp1, dsconv, ddconv, ddconv_b = odd_bwd_conv(dy1, p1, ddc, dg2, sconv_f, dconv_f, "odd_bwd_conv")
    dw_in_o = mm_tn(h1, dp1, N_DEV, BF16, "dw_in_odd")
    dh1 = mm_nt(dp1, g_wio, F32, "dh_odd")
    gx1, dg_pre_odd = norm_bwd(dh1, x1, ln_pre_odd_f, gx2, F32, "pre_odd_bwd")

    do0, dg_post_even = norm_bwd(gx1, o0, ln_post_even, None, BF16, "post_even_bwd")
    dw_out_e = mm_tn(y0, do0, 1, BF16, "dw_out_even", tk=512)
    dy0 = mm_nt(do0, w_out_e, BF16, "dy_even", tn=512)
    da0, du0, dg0, dpool, dpool_scale = even_mix_bwd(dy0, a0, p0, pool_full, pool_scale, "even_mix_bwd")
    dq0, dk0, dv0 = sb_bwd(p0, a0, da0, n_heads, "sb_bwd")
    dp0 = jnp.concatenate([dq0, dk0, dv0, du0, dg0], axis=1)
    dw_in_e = mm_tn(h0, dp0, N_DEV, BF16, "dw_in_even")
    dh0 = mm_nt(dp0, g_wie, F32, "dh_even")
    grad_x, dg_pre_even = norm_bwd(dh0, xs, ln_pre_even, gx1, F32, "pre_even_bwd")
    small_g = [dg_pre_even, dpool_scale, dg_post_even, dg_pre_odd, dsconv, ddconv, ddconv_b, dgam, dbet, dg_post_odd]
    return loss_blk, grad_x, dw_in_e, dpool, dw_out_e, dw_in_o, dw_out_o, small_g


def _reduce_and_update(partial_w, big, small_g, small_w, small_m, small_v, dev, core, d, cl):
    partial_w = [a.reshape((4, 2) + a.shape[1:]) for a in partial_w]
    from_sibling = sibling_exchange(partial_w, "rs_sibling")
    chip_sums = [pair_add(o, r, core, "rs_pair_add_%d" % k) for k, (o, r) in enumerate(zip(partial_w, from_sibling))]
    from_chips = chip_exchange(chip_sums, "rs_chips")
    big_out = []
    for k, ((w, m, v, shp), parts) in enumerate(zip(big, from_chips)):
        outs = adamw_big(w.reshape(shp), m.reshape(shp), v.reshape(shp), parts, "adamw_big_%d" % k)
        big_out.append([o.reshape(w.shape) for o in outs])

    packed = jnp.concatenate([_rows128(g) for g in small_g], axis=0)
    (g8,) = all_gather([packed], "ag_small_grads")
    tot = sum_devices(g8, "sum_small_grads")
    full_g = []
    lo = 0
    for g in small_g:
        rows = g.size // 128
        full_g.append(tot[lo:lo + rows].reshape(g.shape))
        lo += rows

    def mine(g, width):
        return lax.dynamic_slice_in_dim(g, dev * width, width, axis=g.ndim - 1)

    fg = full_g
    small_gl = [fg[0], fg[1], fg[2], mine(fg[3], d // N_DEV), mine(fg[4], cl), mine(fg[5], cl), mine(fg[6], cl),
                mine(fg[7], cl), mine(fg[8], cl), mine(fg[9], d // N_DEV)]
    sd, sm, sv = adamw_small(small_w, small_gl, small_m, small_v, "adamw_small")

    def like(k, a):
        return a[None] if k in (4, 5) else a

    sg = [like(k, a) for k, a in enumerate(small_gl)]
    sd = [like(k, a) for k, a in enumerate(sd)]
    sm = [like(k, a) for k, a in enumerate(sm)]
    sv = [like(k, a) for k, a in enumerate(sv)]
    return big_out, (sg, sd, sm, sv)
```

```python
import functools
import math

import jax
import jax.numpy as jnp
from jax import lax
from jax.experimental import pallas as pl
from jax.experimental.pallas import tpu as pltpu

F32 = jnp.float32
BF16 = jnp.bfloat16
EPS = 1e-6
HEAD_DIM = 128
POOL_WINDOWS = (2, 4, 8, 16)
SCONV_K = 3
CONF_K = 31
HALO = 32
N_DEV = 8
VMEM_LIMIT = 56 * 1024 * 1024
MESH = pl.DeviceIdType.MESH

ADAM_LR = 0.001
ADAM_B1 = 0.9
ADAM_B2 = 0.999
ADAM_EPS = 1e-08
ADAM_WD = 0.01
ADAM_STEP = 10


def _params(*sem):
    return pltpu.CompilerParams(dimension_semantics=sem, vmem_limit_bytes=VMEM_LIMIT)


def _sigmoid(v):
    return 1.0 / (1.0 + jnp.exp(-v))


def _silu(v):
    return v * _sigmoid(v)


def _silu_and_grad(v):
    s = _sigmoid(v)
    return v * s, s * (1.0 + v * (1.0 - s))


def _rowsum8(v):
    r, c = v.shape
    return jnp.sum(v.reshape(r // 8, 8, c), axis=0)


def _tap_before(xx, i, rows):
    if i == 0:
        return xx[HALO:HALO + rows]
    return pltpu.roll(xx, i, 0)[HALO:HALO + rows]


def _tap_after(xx, i, rows):
    if i == 0:
        return xx[0:rows]
    return pltpu.roll(xx, xx.shape[0] - i, 0)[0:rows]


def rms_fwd(x, g, name, tm=256):
    s, d = x.shape

    def body(x_ref, g_ref, h_ref):
        xv = x_ref[...]
        r = lax.rsqrt(jnp.mean(xv * xv, axis=-1, keepdims=True) + EPS)
        h_ref[...] = (xv * r * g_ref[...]).astype(BF16)

    return pl.pallas_call(
        body, name=name, grid=(s // tm,),
        in_specs=[pl.BlockSpec((tm, d), lambda i: (i, 0)), pl.BlockSpec((1, d), lambda i: (0, 0))],
        out_specs=pl.BlockSpec((tm, d), lambda i: (i, 0)),
        out_shape=jax.ShapeDtypeStruct((s, d), BF16),
        compiler_params=_params("parallel"),
    )(x, g)


def postnorm_fwd(x, o, g, name, tm=256):
    s, d = x.shape

    def body(x_ref, o_ref, g_ref, y_ref):
        ov = o_ref[...]
        r = lax.rsqrt(jnp.mean(ov * ov, axis=-1, keepdims=True) + EPS)
        y_ref[...] = x_ref[...] + ov * r * g_ref[...]

    return pl.pallas_call(
        body, name=name, grid=(s // tm,),
        in_specs=[pl.BlockSpec((tm, d), lambda i: (i, 0)), pl.BlockSpec((tm, d), lambda i: (i, 0)),
                  pl.BlockSpec((1, d), lambda i: (0, 0))],
        out_specs=pl.BlockSpec((tm, d), lambda i: (i, 0)),
        out_shape=jax.ShapeDtypeStruct((s, d), F32),
        compiler_params=_params("parallel"),
    )(x, o, g)


def final_fwd_bwd(x1, o, g, target, name, tm=256):
    s, d = x1.shape
    n = s // tm

    def body(x_ref, o_ref, g_ref, t_ref, loss_ref, gx_ref, do_ref, dg_ref, lacc, gacc):
        i = pl.program_id(0)

        @pl.when(i == 0)
        def _():
            lacc[...] = jnp.zeros_like(lacc)
            gacc[...] = jnp.zeros_like(gacc)

        ov = o_ref[...]
        gv = g_ref[...]
        r = lax.rsqrt(jnp.mean(ov * ov, axis=-1, keepdims=True) + EPS)
        oh = ov * r
        diff = x_ref[...] + oh * gv - t_ref[...]
        lacc[...] += _rowsum8(diff * diff)
        gx = diff * (1.0 / d)
        gx_ref[...] = gx
        gacc[...] += _rowsum8(gx * oh)
        dn = gx * gv
        do_ref[...] = (r * (dn - oh * jnp.mean(dn * oh, axis=-1, keepdims=True))).astype(BF16)

        @pl.when(i == n - 1)
        def _():
            tot = jnp.sum(jnp.sum(lacc[...], axis=0, keepdims=True), axis=1, keepdims=True)
            loss_ref[...] = jnp.broadcast_to(tot * (0.5 / d), loss_ref.shape)
            dg_ref[...] = jnp.sum(gacc[...], axis=0, keepdims=True)

    row = pl.BlockSpec((tm, d), lambda i: (i, 0))
    vec = pl.BlockSpec((1, d), lambda i: (0, 0))
    return pl.pallas_call(
        body, name=name, grid=(n,),
        in_specs=[row, row, vec, row],
        out_specs=[pl.BlockSpec((8, 128), lambda i: (0, 0)), row, row, vec],
        out_shape=[jax.ShapeDtypeStruct((8, 128), F32), jax.ShapeDtypeStruct((s, d), F32),
                   jax.ShapeDtypeStruct((s, d), BF16), jax.ShapeDtypeStruct((1, d), F32)],
        scratch_shapes=[pltpu.VMEM((8, d), F32), pltpu.VMEM((8, d), F32)],
        compiler_params=_params("arbitrary"),
    )(x1, o, g, target)


def norm_bwd(dy, inp, g, resid, out_dtype, name, tm=256):
    s, d = inp.shape
    n = s // tm
    has_resid = resid is not None

    def body(*refs):
        if has_resid:
            dy_ref, x_ref, g_ref, r_ref, dx_ref, dg_ref, gacc = refs
        else:
            dy_ref, x_ref, g_ref, dx_ref, dg_ref, gacc = refs
        i = pl.program_id(0)

        @pl.when(i == 0)
        def _():
            gacc[...] = jnp.zeros_like(gacc)

        xv = x_ref[...]
        dyv = dy_ref[...].astype(F32)
        r = lax.rsqrt(jnp.mean(xv * xv, axis=-1, keepdims=True) + EPS)
        xh = xv * r
        gacc[...] += _rowsum8(dyv * xh)
        dn = dyv * g_ref[...]
        dx = r * (dn - xh * jnp.mean(dn * xh, axis=-1, keepdims=True))
        if has_resid:
            dx = dx + r_ref[...]
        dx_ref[...] = dx.astype(out_dtype)

        @pl.when(i == n - 1)
        def _():
            dg_ref[...] = jnp.sum(gacc[...], axis=0, keepdims=True)

    row = pl.BlockSpec((tm, d), lambda i: (i, 0))
    vec = pl.BlockSpec((1, d), lambda i: (0, 0))
    args = [dy, inp, g] + ([resid] if has_resid else [])
    return pl.pallas_call(
        body, name=name, grid=(n,),
        in_specs=[row, row, vec] + ([row] if has_resid else []),
        out_specs=[row, vec],
        out_shape=[jax.ShapeDtypeStruct((s, d), out_dtype), jax.ShapeDtypeStruct((1, d), F32)],
        scratch_shapes=[pltpu.VMEM((8, d), F32)],
        compiler_params=_params("arbitrary"),
    )(*args)


def mm_nn(a, w, out_dtype, name, tm=512, tn=None):
    m, k = a.shape
    ns, _, n = w.shape
    tn = n if tn is None else tn
    nj = n // tn

    def body(a_ref, w_ref, o_ref):
        o_ref[...] = jnp.dot(a_ref[...], w_ref[0], preferred_element_type=F32).astype(out_dtype)

    return pl.pallas_call(
        body, name=name, grid=(ns, nj, m // tm),
        in_specs=[pl.BlockSpec((tm, k), lambda s, j, i: (i, 0)),
                  pl.BlockSpec((1, k, tn), lambda s, j, i: (s, 0, j))],
        out_specs=pl.BlockSpec((tm, tn), lambda s, j, i: (i, s * nj + j)),
        out_shape=jax.ShapeDtypeStruct((m, ns * n), out_dtype),
        compiler_params=_params("parallel", "parallel", "parallel"),
    )(a, w)


def mm_nt(a, w, out_dtype, name, tm=512, tn=None):
    m = a.shape[0]
    ns, k, n = w.shape
    tn = n if tn is None else tn
    nj = n // tn
    steps = ns * nj

    def body(a_ref, w_ref, o_ref, acc):
        r = pl.program_id(1)

        @pl.when(r == 0)
        def _():
            acc[...] = jnp.zeros_like(acc)

        acc[...] += lax.dot_general(a_ref[...], w_ref[0], (((1,), (1,)), ((), ())),
                                    preferred_element_type=F32)

        @pl.when(r == steps - 1)
        def _():
            o_ref[...] = acc[...].astype(out_dtype)

    return pl.pallas_call(
        body, name=name, grid=(m // tm, steps),
        in_specs=[pl.BlockSpec((tm, tn), lambda i, r: (i, r)),
                  pl.BlockSpec((1, k, tn), lambda i, r: (r // nj, 0, r % nj))],
        out_specs=pl.BlockSpec((tm, k), lambda i, r: (i, 0)),
        out_shape=jax.ShapeDtypeStruct((m, k), out_dtype),
        scratch_shapes=[pltpu.VMEM((tm, k), F32)],
        compiler_params=_params("parallel", "arbitrary"),
    )(a, w)


def mm_tn(a, b, ns, out_dtype, name, tk=1024, tm=512):
    m, k = a.shape
    n = b.shape[1] // ns
    steps = m // tm

    def body(a_ref, b_ref, o_ref, acc):
        r = pl.program_id(2)

        @pl.when(r == 0)
        def _():
            acc[...] = jnp.zeros_like(acc)

        acc[...] += lax.dot_general(a_ref[...], b_ref[...], (((0,), (0,)), ((), ())),
                                    preferred_element_type=F32)

        @pl.when(r == steps - 1)
        def _():
            o_ref[0] = acc[...].astype(out_dtype)

    return pl.pallas_call(
        body, name=name, grid=(ns, k // tk, steps),
        in_specs=[pl.BlockSpec((tm, tk), lambda s, j, r: (r, j)),
                  pl.BlockSpec((tm, n), lambda s, j, r: (r, s))],
        out_specs=pl.BlockSpec((1, tk, n), lambda s, j, r: (s, j, 0)),
        out_shape=jax.ShapeDtypeStruct((ns, k, n), out_dtype),
        scratch_shapes=[pltpu.VMEM((tk, n), F32)],
        compiler_params=_params("parallel", "parallel", "arbitrary"),
    )(a, b)


SB_BLK = 128


def _split_dot(v, tri):
    hi = v.astype(BF16)
    lo = (v - hi.astype(F32)).astype(BF16)
    return jnp.dot(hi, tri, preferred_element_type=F32) + jnp.dot(lo, tri, preferred_element_type=F32)


def _sb_scores(z, lim, dcol, tri_ex):
    mask = dcol < lim
    sp = jnp.log(1.0 + jnp.exp(-jnp.abs(z)))
    lb = jnp.minimum(z, 0.0) - sp
    l1m = jnp.where(mask, lb - z, 0.0)
    return mask, lb, l1m, _split_dot(l1m, tri_ex)


def _sb_consts():
    row = lax.broadcasted_iota(jnp.int32, (SB_BLK, SB_BLK), 0)
    col = lax.broadcasted_iota(jnp.int32, (SB_BLK, SB_BLK), 1)
    tri_ex = jnp.where(row > col, 1.0, 0.0).astype(BF16)
    tri_in = jnp.where(row >= col, 1.0, 0.0).astype(BF16)
    return col - row, tri_ex, tri_in


def sb_fwd(p, n_heads, name, tq=256, nsub=4):
    s = p.shape[0]
    h_n = n_heads
    b = SB_BLK
    nqs = tq // b
    tk = nsub * b
    scale = 1.0 / math.sqrt(HEAD_DIM)

    def body(q_ref, k_ref, v_ref, o_ref):
        qi = pl.program_id(1)
        dcol, tri_ex, _ = _sb_consts()
        qv = [q_ref[qs * b:(qs + 1) * b, :] for qs in range(nqs)]
        n_groups = ((qi + 1) * nqs - 1) // nsub + 1

        def step(it, carry):
            c1s, accs = carry
            g = n_groups - 1 - it
            off = pl.multiple_of(g * tk, tk)
            kg = k_ref[pl.ds(off, tk), :]
            vg = v_ref[pl.ds(off, tk), :]
            new_c1, new_acc = [], []
            for qs in range(nqs):
                qb = qi * nqs + qs
                z = lax.dot_general(qv[qs], kg, (((1,), (1,)), ((), ())), preferred_element_type=F32) * scale
                blocks = [_sb_scores(z[:, j * b:(j + 1) * b], (qb - (g * nsub + j)) * b, dcol, tri_ex)
                          for j in range(nsub)]
                run = c1s[qs]
                ws = [None] * nsub
                for j in reversed(range(nsub)):
                    mask, lb, l1m, ls_loc = blocks[j]
                    ws[j] = jnp.where(mask, jnp.exp(lb + ls_loc + run), 0.0).astype(BF16)
                    run = run + jnp.sum(l1m, axis=1, keepdims=True)
                w = jnp.concatenate(ws, axis=1)
                new_acc.append(accs[qs] + jnp.dot(w, vg, preferred_element_type=F32))
                new_c1.append(run)
            return tuple(new_c1), tuple(new_acc)

        init = (tuple(jnp.zeros((b, 1), F32) for _ in range(nqs)),
                tuple(jnp.zeros((b, HEAD_DIM), F32) for _ in range(nqs)))
        _, accs = lax.fori_loop(0, n_groups, step, init)
        for qs in range(nqs):
            o_ref[qs * b:(qs + 1) * b, :] = accs[qs]

    return pl.pallas_call(
        body, name=name, grid=(h_n, s // tq),
        in_specs=[pl.BlockSpec((tq, HEAD_DIM), lambda h, i: (i, h)),
                  pl.BlockSpec((s, HEAD_DIM), lambda h, i: (0, h_n + h)),
                  pl.BlockSpec((s, HEAD_DIM), lambda h, i: (0, 2 * h_n + h))],
        out_specs=pl.BlockSpec((tq, HEAD_DIM), lambda h, i: (i, h)),
        out_shape=jax.ShapeDtypeStruct((s, h_n * HEAD_DIM), F32),
        compiler_params=_params("parallel", "arbitrary"),
    )(p, p, p)


def sb_bwd(p, a, da, n_heads, name, tq=256, nsub=4):
    s = p.shape[0]
    h_n = n_heads
    nq = s // tq
    b = SB_BLK
    nqs = tq // b
    tk = nsub * b
    scale = 1.0 / math.sqrt(HEAD_DIM)

    def body(q_ref, k_ref, v_ref, a_ref, da_ref, dq_ref, dk_ref, dv_ref, dk_acc, dv_acc):
        qi = pl.program_id(1)

        @pl.when(qi == 0)
        def _():
            dk_acc[...] = jnp.zeros_like(dk_acc)
            dv_acc[...] = jnp.zeros_like(dv_acc)

        dcol, tri_ex, tri_in = _sb_consts()
        q_all = q_ref[...]
        do_all = da_ref[...]
        qv = [q_ref[qs * b:(qs + 1) * b, :] for qs in range(nqs)]
        dov = [da_ref[qs * b:(qs + 1) * b, :] for qs in range(nqs)]
        tots = [jnp.sum(dov[qs].astype(F32) * a_ref[qs * b:(qs + 1) * b, :], axis=1, keepdims=True)
                for qs in range(nqs)]
        n_groups = ((qi + 1) * nqs - 1) // nsub + 1

        def step(it, carry):
            c1s, c2s, dqs = carry
            g = n_groups - 1 - it
            off = pl.multiple_of(g * tk, tk)
            kg = k_ref[pl.ds(off, tk), :]
            vg = v_ref[pl.ds(off, tk), :]
            new_c1, new_c2, new_dq, dz_rows, wr_rows = [], [], [], [], []
            for qs in range(nqs):
                qb = qi * nqs + qs
                z = lax.dot_general(qv[qs], kg, (((1,), (1,)), ((), ())), preferred_element_type=F32) * scale
                dw = lax.dot_general(dov[qs], vg, (((1,), (1,)), ((), ())), preferred_element_type=F32)
                blocks = [_sb_scores(z[:, j * b:(j + 1) * b], (qb - (g * nsub + j)) * b, dcol, tri_ex)
                          for j in range(nsub)]
                run1, run2 = c1s[qs], c2s[qs]
                dzs, wrs = [None] * nsub, [None] * nsub
                for j in reversed(range(nsub)):
                    mask, lb, l1m, ls_loc = blocks[j]
                    wr = jnp.where(mask, jnp.exp(lb + ls_loc + run1), 0.0).astype(BF16)
                    e = dw[:, j * b:(j + 1) * b] * wr.astype(F32)
                    later = _split_dot(e, tri_in) + run2
                    beta = jnp.exp(lb)
                    dz = jnp.where(mask, e * (1.0 - beta) - beta * (tots[qs] - later), 0.0) * scale
                    dzs[j] = dz.astype(BF16)
                    wrs[j] = wr
                    run1 = run1 + jnp.sum(l1m, axis=1, keepdims=True)
                    run2 = run2 + jnp.sum(e, axis=1, keepdims=True)
                dzq = jnp.concatenate(dzs, axis=1)
                new_dq.append(dqs[qs] + jnp.dot(dzq, kg, preferred_element_type=F32))
                new_c1.append(run1)
                new_c2.append(run2)
                dz_rows.append(dzq)
                wr_rows.append(jnp.concatenate(wrs, axis=1))
            dz_all = jnp.concatenate(dz_rows, axis=0)
            wr_all = jnp.concatenate(wr_rows, axis=0)
            dk_acc[pl.ds(off, tk), :] += lax.dot_general(dz_all, q_all, (((0,), (0,)), ((), ())),
                                                         preferred_element_type=F32)
            dv_acc[pl.ds(off, tk), :] += lax.dot_general(wr_all, do_all, (((0,), (0,)), ((), ())),
                                                         preferred_element_type=F32)
            return tuple(new_c1), tuple(new_c2), tuple(new_dq)

        zeros = tuple(jnp.zeros((b, 1), F32) for _ in range(nqs))
        _, _, dqs = lax.fori_loop(0, n_groups, step,
                                  (zeros, zeros, tuple(jnp.zeros((b, HEAD_DIM), F32) for _ in range(nqs))))
        for qs in range(nqs):
            dq_ref[qs * b:(qs + 1) * b, :] = dqs[qs].astype(BF16)

        @pl.when(qi == nq - 1)
        def _():
            dk_ref[...] = dk_acc[...].astype(BF16)
            dv_ref[...] = dv_acc[...].astype(BF16)

    blk = pl.BlockSpec((tq, HEAD_DIM), lambda h, i: (i, h))
    full = pl.BlockSpec((s, HEAD_DIM), lambda h, i: (0, h))
    return pl.pallas_call(
        body, name=name, grid=(h_n, nq),
        in_specs=[blk, pl.BlockSpec((s, HEAD_DIM), lambda h, i: (0, h_n + h)),
                  pl.BlockSpec((s, HEAD_DIM), lambda h, i: (0, 2 * h_n + h)), blk, blk],
        out_specs=[blk, full, full],
        out_shape=[jax.ShapeDtypeStruct((s, h_n * HEAD_DIM), BF16)] * 3,
        scratch_shapes=[pltpu.VMEM((s, HEAD_DIM), F32), pltpu.VMEM((s, HEAD_DIM), F32)],
        compiler_params=_params("parallel", "arbitrary"),
    )(p, p, p, a, da)


def _pool_window(xx, win, r0, rc):
    cur = xx[HALO:HALO + rc]
    ws = cur
    for i in range(1, win):
        ws = ws + _tap_before(xx, i, rc)
    t_idx = r0 + lax.broadcasted_iota(jnp.int32, (rc, 1), 0)
    inv = 1.0 / jnp.minimum(win, t_idx + 1).astype(F32)
    return ws * inv - cur, inv


def even_mix_fwd(a, p, pool_w, pool_scale, name, rc=64):
    s = p.shape[0]
    ng = len(POOL_WINDOWS)
    cw = pool_w.shape[1]
    n_chunks = s // rc

    def body(a_ref, u_ref, g_ref, w_ref, sc_ref, y_ref, upad):
        j = pl.program_id(0)

        @pl.when(j < ng)
        def _():
            def chunk(ci, carry):
                rows = pl.ds(pl.multiple_of(ci * rc, rc), rc)
                y_ref[rows, :] = (a_ref[rows, :] * _silu(g_ref[rows, :].astype(F32))).astype(BF16)
                return carry

            lax.fori_loop(0, n_chunks, chunk, 0)

        for gi, win in enumerate(POOL_WINDOWS):
            @pl.when(j == ng + gi)
            def _(win=win):
                upad[0:HALO, :] = jnp.zeros((HALO, cw), F32)

                def fill(ci, carry):
                    r0 = pl.multiple_of(ci * rc, rc)
                    upad[pl.ds(pl.multiple_of(r0 + HALO, HALO), rc), :] = u_ref[pl.ds(r0, rc), :].astype(F32)
                    return carry

                lax.fori_loop(0, n_chunks, fill, 0)

                def chunk(ci, carry):
                    r0 = pl.multiple_of(ci * rc, rc)
                    rows = pl.ds(r0, rc)
                    pooled, _ = _pool_window(upad[pl.ds(r0, HALO + rc), :], win, r0, rc)
                    t = jnp.dot(pooled.astype(BF16), w_ref[0], preferred_element_type=F32)
                    y_ref[rows, :] = (t * sc_ref[...] * _silu(g_ref[rows, :].astype(F32))).astype(BF16)
                    return carry

                lax.fori_loop(0, n_chunks, chunk, 0)

    grp = lambda j: jnp.maximum(j - ng, 0)
    return pl.pallas_call(
        body, name=name, grid=(2 * ng,),
        in_specs=[pl.BlockSpec((s, cw), lambda j: (0, jnp.minimum(j, ng - 1))),
                  pl.BlockSpec((s, cw), lambda j: (0, 3 * ng + grp(j))),
                  pl.BlockSpec((s, cw), lambda j: (0, 4 * ng + j)),
                  pl.BlockSpec((1, cw, cw), lambda j: (grp(j), 0, 0)),
                  pl.BlockSpec((1, cw), lambda j: (0, grp(j)))],
        out_specs=pl.BlockSpec((s, cw), lambda j: (0, j)),
        out_shape=jax.ShapeDtypeStruct((s, 2 * ng * cw), BF16),
        scratch_shapes=[pltpu.VMEM((HALO + s, cw), F32)],
        compiler_params=_params("arbitrary"),
    )(a, p, p, pool_w, pool_scale)


def even_mix_bwd(dy, a, p, pool_w, pool_scale, name, rc=64):
    s = p.shape[0]
    ng = len(POOL_WINDOWS)
    cw = pool_w.shape[1]
    n_chunks = s // rc

    def body(dy_ref, a_ref, u_ref, g_ref, w_ref, sc_ref, da_ref, du_ref, dg_ref, dw_ref, dsc_ref,
             upad, rpad, dpl, dw_acc, dsc_acc):
        j = pl.program_id(0)

        @pl.when(j < ng)
        def _():
            def chunk(ci, carry):
                rows = pl.ds(pl.multiple_of(ci * rc, rc), rc)
                dyv = dy_ref[rows, :].astype(F32)
                sg, dsg = _silu_and_grad(g_ref[rows, :].astype(F32))
                da_ref[rows, :] = (dyv * sg).astype(BF16)
                dg_ref[rows, :] = (dyv * a_ref[rows, :] * dsg).astype(BF16)
                return carry

            lax.fori_loop(0, n_chunks, chunk, 0)

        for gi, win in enumerate(POOL_WINDOWS):
            @pl.when(j == ng + gi)
            def _(win=win):
                upad[0:HALO, :] = jnp.zeros((HALO, cw), F32)
                rpad[s:s + HALO, :] = jnp.zeros((HALO, cw), F32)
                dw_acc[...] = jnp.zeros_like(dw_acc)
                dsc_acc[...] = jnp.zeros_like(dsc_acc)

                def fill(ci, carry):
                    r0 = pl.multiple_of(ci * rc, rc)
                    upad[pl.ds(pl.multiple_of(r0 + HALO, HALO), rc), :] = u_ref[pl.ds(r0, rc), :].astype(F32)
                    return carry

                lax.fori_loop(0, n_chunks, fill, 0)

                def chunk(ci, carry):
                    r0 = pl.multiple_of(ci * rc, rc)
                    rows = pl.ds(r0, rc)
                    pooled, inv = _pool_window(upad[pl.ds(r0, HALO + rc), :], win, r0, rc)
                    pb = pooled.astype(BF16)
                    wv = w_ref[0]
                    t = jnp.dot(pb, wv, preferred_element_type=F32)
                    scv = sc_ref[...]
                    dyv = dy_ref[rows, :].astype(F32)
                    sg, dsg = _silu_and_grad(g_ref[rows, :].astype(F32))
                    dpo = dyv * sg
                    dg_ref[rows, :] = (dyv * t * scv * dsg).astype(BF16)
                    dsc_acc[...] += _rowsum8(dpo * t)
                    dtb = (dpo * scv).astype(BF16)
                    dw_acc[...] += lax.dot_general(pb, dtb, (((0,), (0,)), ((), ())),
                                                   preferred_element_type=F32)
                    dpooled = lax.dot_general(dtb, wv, (((1,), (1,)), ((), ())),
                                              preferred_element_type=F32)
                    dpl[rows, :] = dpooled
                    rpad[rows, :] = dpooled * inv
                    return carry

                lax.fori_loop(0, n_chunks, chunk, 0)

                def chunk2(ci, carry):
                    r0 = pl.multiple_of(ci * rc, rc)
                    rows = pl.ds(r0, rc)
                    xx = rpad[pl.ds(r0, rc + HALO), :]
                    fs = xx[0:rc]
                    for i in range(1, win):
                        fs = fs + _tap_after(xx, i, rc)
                    du_ref[rows, :] = (fs - dpl[rows, :]).astype(BF16)
                    return carry

                lax.fori_loop(0, n_chunks, chunk2, 0)
                dw_ref[0] = dw_acc[...]
                dsc_ref[...] = jnp.sum(dsc_acc[...], axis=0, keepdims=True)

    grp = lambda j: jnp.maximum(j - ng, 0)
    att = lambda j: jnp.minimum(j, ng - 1)
    return pl.pallas_call(
        body, name=name, grid=(2 * ng,),
        in_specs=[pl.BlockSpec((s, cw), lambda j: (0, j)),
                  pl.BlockSpec((s, cw), lambda j: (0, att(j))),
                  pl.BlockSpec((s, cw), lambda j: (0, 3 * ng + grp(j))),
                  pl.BlockSpec((s, cw), lambda j: (0, 4 * ng + j)),
                  pl.BlockSpec((1, cw, cw), lambda j: (grp(j), 0, 0)),
                  pl.BlockSpec((1, cw), lambda j: (0, grp(j)))],
        out_specs=[pl.BlockSpec((s, cw), lambda j: (0, att(j))),
                   pl.BlockSpec((s, cw), lambda j: (0, grp(j))),
                   pl.BlockSpec((s, cw), lambda j: (0, j)),
                   pl.BlockSpec((1, cw, cw), lambda j: (grp(j), 0, 0)),
                   pl.BlockSpec((1, cw), lambda j: (0, grp(j)))],
        out_shape=[jax.ShapeDtypeStruct((s, ng * cw), BF16), jax.ShapeDtypeStruct((s, ng * cw), BF16),
                   jax.ShapeDtypeStruct((s, 2 * ng * cw), BF16),
                   jax.ShapeDtypeStruct((ng, cw, cw), F32), jax.ShapeDtypeStruct((1, ng * cw), F32)],
        scratch_shapes=[pltpu.VMEM((HALO + s, cw), F32), pltpu.VMEM((s + HALO, cw), F32),
                        pltpu.VMEM((s, cw), F32), pltpu.VMEM((cw, cw), F32), pltpu.VMEM((8, cw), F32)],
        compiler_params=_params("arbitrary"),
    )(dy, a, p, p, pool_w, pool_scale)


def _halo_before(tm):
    return lambda i: jnp.maximum(i * (tm // HALO) - 1, 0)


def _halo_after(tm, s):
    return lambda i: jnp.minimum((i + 1) * (tm // HALO), s // HALO - 1)


def odd_mix_fwd(p, sconv_w, dconv_w, dconv_b, cnorm_g, cnorm_b, name, tm=128):
    s = p.shape[0]
    cw = sconv_w.shape[1]
    n = s // tm
    lanes = 128
    hb = _halo_before(tm)

    def body(hc_ref, hch_ref, bc_ref, cc_ref, cch_ref, ga_ref, gah_ref, gb_ref, gbh_ref, g1_ref, g2_ref,
             sw_ref, dw_ref, db_ref, gam_ref, bet_ref, y_ref, dc_ref):
        first = pl.program_id(0) == 0
        for l in range(cw // lanes):
            cols = slice(l * lanes, (l + 1) * lanes)
            mh = jnp.where(first, 0.0, cch_ref[:, cols].astype(F32) * hch_ref[:, cols].astype(F32))
            mm = cc_ref[:, cols].astype(F32) * hc_ref[:, cols].astype(F32)
            xx = jnp.concatenate([mh, mm], axis=0)
            cv = jnp.zeros((tm, lanes), F32)
            for k in range(SCONV_K):
                cv = cv + sw_ref[k:k + 1, cols] * _tap_before(xx, SCONV_K - 1 - k, tm)
            c_out = bc_ref[:, cols].astype(F32) * cv
            y_ref[:, cols] = (c_out * _silu(g1_ref[:, cols].astype(F32))).astype(BF16)
            dh = jnp.where(first, 0.0, gah_ref[:, cols].astype(F32) * _sigmoid(gbh_ref[:, cols].astype(F32)))
            dm = ga_ref[:, cols].astype(F32) * _sigmoid(gb_ref[:, cols].astype(F32))
            xx = jnp.concatenate([dh, dm], axis=0)
            acc = jnp.zeros((tm, lanes), F32) + db_ref[:, cols]
            for k in range(CONF_K):
                acc = acc + dw_ref[k:k + 1, cols] * _tap_before(xx, CONF_K - 1 - k, tm)
            dc_ref[:, cols] = acc
        rs = 32
        for r in range(tm // rs):
            rows = slice(r * rs, (r + 1) * rs)
            xv = dc_ref[rows, :]
            mu = jnp.mean(xv, axis=-1, keepdims=True)
            xc = xv - mu
            rstd = lax.rsqrt(jnp.mean(xc * xc, axis=-1, keepdims=True) + EPS)
            ln = xc * rstd * gam_ref[...] + bet_ref[...]
            y_ref[rows, cw:2 * cw] = (_silu(ln) * _silu(g2_ref[rows, :].astype(F32))).astype(BF16)

    main = lambda c: pl.BlockSpec((tm, cw), lambda i: (i, c))
    halo = lambda c: pl.BlockSpec((HALO, cw), lambda i: (hb(i), c))
    vec = lambda r: pl.BlockSpec((r, cw), lambda i: (0, 0))
    return pl.pallas_call(
        body, name=name, grid=(n,),
        in_specs=[main(0), halo(0), main(1), main(2), halo(2), main(3), halo(3), main(4), halo(4),
                  main(5), main(6), vec(SCONV_K), vec(CONF_K), vec(1), vec(1), vec(1)],
        out_specs=[pl.BlockSpec((tm, 2 * cw), lambda i: (i, 0)), pl.BlockSpec((tm, cw), lambda i: (i, 0))],
        out_shape=[jax.ShapeDtypeStruct((s, 2 * cw), BF16), jax.ShapeDtypeStruct((s, cw), F32)],
        compiler_params=_params("parallel"),
    )(p, p, p, p, p, p, p, p, p, p, p, sconv_w, dconv_w, dconv_b, cnorm_g, cnorm_b)


def odd_bwd_ln(dy, p, dc, cnorm_g, cnorm_b, name, tm=256):
    s = p.shape[0]
    cw = dc.shape[1]
    n = s // tm
    rs = 32

    def body(dy_ref, g2_ref, dc_ref, gam_ref, bet_ref, ddc_ref, dg_ref, dgam_ref, dbet_ref, gacc, bacc):
        i = pl.program_id(0)

        @pl.when(i == 0)
        def _():
            gacc[...] = jnp.zeros_like(gacc)
            bacc[...] = jnp.zeros_like(bacc)

        def chunk(ci, carry):
            rows = pl.ds(pl.multiple_of(ci * rs, rs), rs)
            xv = dc_ref[rows, :]
            mu = jnp.mean(xv, axis=-1, keepdims=True)
            xc = xv - mu
            rstd = lax.rsqrt(jnp.mean(xc * xc, axis=-1, keepdims=True) + EPS)
            xh = xc * rstd
            gam = gam_ref[...]
            sl, dsl = _silu_and_grad(xh * gam + bet_ref[...])
            sg, dsg = _silu_and_grad(g2_ref[rows, :].astype(F32))
            dyv = dy_ref[rows, :].astype(F32)
            dg_ref[rows, :] = (dyv * sl * dsg).astype(BF16)
            dln = dyv * sg * dsl
            gacc[...] += _rowsum8(dln * xh)
            bacc[...] += _rowsum8(dln)
            dxh = dln * gam
            ddc_ref[rows, :] = rstd * (dxh - jnp.mean(dxh, axis=-1, keepdims=True)
                                       - xh * jnp.mean(dxh * xh, axis=-1, keepdims=True))
            return carry

        lax.fori_loop(0, tm // rs, chunk, 0)

        @pl.when(i == n - 1)
        def _():
            dgam_ref[...] = jnp.sum(gacc[...], axis=0, keepdims=True)
            dbet_ref[...] = jnp.sum(bacc[...], axis=0, keepdims=True)

    vec = pl.BlockSpec((1, cw), lambda i: (0, 0))
    return pl.pallas_call(
        body, name=name, grid=(n,),
        in_specs=[pl.BlockSpec((tm, cw), lambda i: (i, 1)), pl.BlockSpec((tm, cw), lambda i: (i, 6)),
                  pl.BlockSpec((tm, cw), lambda i: (i, 0)), vec, vec],
        out_specs=[pl.BlockSpec((tm, cw), lambda i: (i, 0)), pl.BlockSpec((tm, cw), lambda i: (i, 0)), vec, vec],
        out_shape=[jax.ShapeDtypeStruct((s, cw), F32), jax.ShapeDtypeStruct((s, cw), BF16),
                   jax.ShapeDtypeStruct((1, cw), F32), jax.ShapeDtypeStruct((1, cw), F32)],
        scratch_shapes=[pltpu.VMEM((8, cw), F32), pltpu.VMEM((8, cw), F32)],
        compiler_params=_params("arbitrary"),
    )(dy, p, dc, cnorm_g, cnorm_b)


def odd_bwd_conv(dy, p, ddc, dg2, sconv_w, dconv_w, name, tm=128):
    s = p.shape[0]
    cw = ddc.shape[1]
    n = s // tm
    lanes = 128
    hb = _halo_before(tm)
    ha = _halo_after(tm, s)

    def body(dy_ref, dya_ref, g1_ref, g1a_ref, bc_ref, bca_ref, hc_ref, hch_ref, cc_ref, cch_ref,
             ddc_ref, ddca_ref, ga_ref, gah_ref, gb_ref, gbh_ref, dg2_ref, sw_ref, dw_ref,
             dp_ref, dsw_ref, ddw_ref, ddb_ref, sw_acc, dw_acc, db_acc):
        i = pl.program_id(0)
        first = i == 0
        last = i == n - 1

        @pl.when(first)
        def _():
            sw_acc[...] = jnp.zeros_like(sw_acc)
            dw_acc[...] = jnp.zeros_like(dw_acc)
            db_acc[...] = jnp.zeros_like(db_acc)

        for l in range(cw // lanes):
            cols = slice(l * lanes, (l + 1) * lanes)
            mh = jnp.where(first, 0.0, cch_ref[:, cols].astype(F32) * hch_ref[:, cols].astype(F32))
            hcv = hc_ref[:, cols].astype(F32)
            ccv = cc_ref[:, cols].astype(F32)
            xx = jnp.concatenate([mh, ccv * hcv], axis=0)
            taps = [_tap_before(xx, SCONV_K - 1 - k, tm) for k in range(SCONV_K)]
            cv = jnp.zeros((tm, lanes), F32)
            for k in range(SCONV_K):
                cv = cv + sw_ref[k:k + 1, cols] * taps[k]
            bcv = bc_ref[:, cols].astype(F32)
            dyv = dy_ref[:, cols].astype(F32)
            sg, dsg = _silu_and_grad(g1_ref[:, cols].astype(F32))
            dco = dyv * sg
            dp_ref[:, 5 * cw + l * lanes:5 * cw + (l + 1) * lanes] = (dyv * bcv * cv * dsg).astype(BF16)
            dp_ref[:, cw + l * lanes:cw + (l + 1) * lanes] = (dco * cv).astype(BF16)
            dcv = dco * bcv
            for k in range(SCONV_K):
                sw_acc[k * 8:(k + 1) * 8, cols] += _rowsum8(dcv * taps[k])
            dcv_a = jnp.where(last, 0.0, dya_ref[:, cols].astype(F32) * _silu(g1a_ref[:, cols].astype(F32))
                              * bca_ref[:, cols].astype(F32))
            xx = jnp.concatenate([dcv, dcv_a], axis=0)
            dm = jnp.zeros((tm, lanes), F32)
            for k in range(SCONV_K):
                dm = dm + sw_ref[k:k + 1, cols] * _tap_after(xx, SCONV_K - 1 - k, tm)
            dp_ref[:, l * lanes:(l + 1) * lanes] = (dm * ccv).astype(BF16)
            dp_ref[:, 2 * cw + l * lanes:2 * cw + (l + 1) * lanes] = (dm * hcv).astype(BF16)
            gav = ga_ref[:, cols].astype(F32)
            sb = _sigmoid(gb_ref[:, cols].astype(F32))
            dh = jnp.where(first, 0.0, gah_ref[:, cols].astype(F32) * _sigmoid(gbh_ref[:, cols].astype(F32)))
            xx = jnp.concatenate([dh, gav * sb], axis=0)
            ddcv = ddc_ref[:, cols]
            db_acc[:, cols] += _rowsum8(ddcv)
            for k in range(CONF_K):
                dw_acc[k * 8:(k + 1) * 8, cols] += _rowsum8(ddcv * _tap_before(xx, CONF_K - 1 - k, tm))
            ddc_a = jnp.where(last, 0.0, ddca_ref[:, cols])
            xx = jnp.concatenate([ddcv, ddc_a], axis=0)
            dgl = jnp.zeros((tm, lanes), F32)
            for k in range(CONF_K):
                dgl = dgl + dw_ref[k:k + 1, cols] * _tap_after(xx, CONF_K - 1 - k, tm)
            dp_ref[:, 3 * cw + l * lanes:3 * cw + (l + 1) * lanes] = (dgl * sb).astype(BF16)
            dp_ref[:, 4 * cw + l * lanes:4 * cw + (l + 1) * lanes] = (dgl * gav * sb * (1.0 - sb)).astype(BF16)
        dp_ref[:, 6 * cw:7 * cw] = dg2_ref[...]

        @pl.when(last)
        def _():
            for k in range(SCONV_K):
                dsw_ref[k:k + 1, :] = jnp.sum(sw_acc[k * 8:(k + 1) * 8, :], axis=0, keepdims=True)
            for k in range(CONF_K):
                ddw_ref[k:k + 1, :] = jnp.sum(dw_acc[k * 8:(k + 1) * 8, :], axis=0, keepdims=True)
            ddb_ref[...] = jnp.sum(db_acc[...], axis=0, keepdims=True)

    def main(c):
        return pl.BlockSpec((tm, cw), lambda i: (i, c))

    def before(c):
        return pl.BlockSpec((HALO, cw), lambda i: (hb(i), c))

    def after(c):
        return pl.BlockSpec((HALO, cw), lambda i: (ha(i), c))

    def vec(r):
        return pl.BlockSpec((r, cw), lambda i: (0, 0))

    return pl.pallas_call(
        body, name=name, grid=(n,),
        in_specs=[main(0), after(0), main(5), after(5), main(1), after(1), main(0), before(0), main(2), before(2),
                  main(0), after(0), main(3), before(3), main(4), before(4), main(0), vec(SCONV_K), vec(CONF_K)],
        out_specs=[pl.BlockSpec((tm, 7 * cw), lambda i: (i, 0)), vec(SCONV_K), vec(CONF_K), vec(1)],
        out_shape=[jax.ShapeDtypeStruct((s, 7 * cw), BF16), jax.ShapeDtypeStruct((SCONV_K, cw), F32),
                   jax.ShapeDtypeStruct((CONF_K, cw), F32), jax.ShapeDtypeStruct((1, cw), F32)],
        scratch_shapes=[pltpu.VMEM((8 * SCONV_K, cw), F32), pltpu.VMEM((8 * CONF_K, cw), F32),
                        pltpu.VMEM((8, cw), F32)],
        compiler_params=_params("arbitrary"),
    )(dy, dy, p, p, p, p, p, p, p, p, ddc, ddc, p, p, p, p, dg2, sconv_w, dconv_w)


_ANY = pl.BlockSpec(memory_space=pl.ANY)


def _place():
    return lax.axis_index("x"), lax.axis_index("y"), lax.axis_index("c")


def all_gather(arrs, name):
    n = len(arrs)

    def body(*refs):
        ins, outs = refs[:n], refs[n:2 * n]
        send_sems, recv_sems, local_sems = refs[2 * n:]
        x, y, c = _place()
        me, sibling = (x, y, c), (x, y, 1 - c)
        chips = [(1 - x, y), (x, 1 - y), (1 - x, 1 - y)]

        def copy(a, k, block, to, src=None):
            px, py, pc = block
            dst = outs[a].at[4 * px + 2 * py + pc]
            return pltpu.make_async_remote_copy(
                src_ref=dst if src is None else src, dst_ref=dst,
                send_sem=send_sems.at[7 * a + k], recv_sem=recv_sems.at[7 * a + k],
                device_id=to, device_id_type=MESH)

        mine = [pltpu.make_async_copy(ins[a], outs[a].at[4 * x + 2 * y + c], local_sems.at[a]) for a in range(n)]
        for cp in mine:
            cp.start()
        first = []
        for a in range(n):
            first.append(copy(a, 0, me, sibling, src=ins[a]))
            first += [copy(a, 1 + j, me, (*chip, c), src=ins[a]) for j, chip in enumerate(chips)]
        for cp in first:
            cp.start()
        passed = []
        for a in range(n):
            for j, chip in enumerate(chips):
                copy(a, 1 + j, (*chip, c), me).wait_recv()
                cp = copy(a, 4 + j, (*chip, c), sibling)
                cp.start()
                passed.append(cp)
        for a in range(n):
            copy(a, 0, sibling, me).wait_recv()
            for j, chip in enumerate(chips):
                copy(a, 4 + j, (*chip, 1 - c), me).wait_recv()
        for cp in first + passed:
            cp.wait_send()
        for cp in mine:
            cp.wait()

    return pl.pallas_call(
        body, name=name,
        out_shape=[jax.ShapeDtypeStruct((N_DEV,) + a.shape, a.dtype) for a in arrs],
        in_specs=[_ANY] * n, out_specs=[_ANY] * n,
        scratch_shapes=[pltpu.SemaphoreType.DMA((7 * n,)), pltpu.SemaphoreType.DMA((7 * n,)),
                        pltpu.SemaphoreType.DMA((n,))],
    )(*arrs)


def sibling_exchange(arrs, name):
    n = len(arrs)

    def body(*refs):
        ins, outs = refs[:n], refs[n:2 * n]
        send_sems, recv_sems = refs[2 * n:]
        x, y, c = _place()
        cps = [pltpu.make_async_remote_copy(
            src_ref=ins[a].at[:, pl.ds(1 - c, 1)], dst_ref=outs[a],
            send_sem=send_sems.at[a], recv_sem=recv_sems.at[a],
            device_id=(x, y, 1 - c), device_id_type=MESH) for a in range(n)]
        for cp in cps:
            cp.start()
        for cp in cps:
            cp.wait()

    return pl.pallas_call(
        body, name=name,
        out_shape=[jax.ShapeDtypeStruct((4, 1) + a.shape[2:], a.dtype) for a in arrs],
        in_specs=[_ANY] * n, out_specs=[_ANY] * n,
        scratch_shapes=[pltpu.SemaphoreType.DMA((n,)), pltpu.SemaphoreType.DMA((n,))],
    )(*arrs)


def chip_exchange(arrs, name):
    n = len(arrs)

    def body(*refs):
        ins, outs = refs[:n], refs[n:2 * n]
        send_sems, recv_sems, local_sems = refs[2 * n:]
        x, y, c = _place()
        kc = 2 * x + y
        chips = [(1 - x, y), (x, 1 - y), (1 - x, 1 - y)]

        def copy(a, j, chip):
            px, py = chip
            return pltpu.make_async_remote_copy(
                src_ref=ins[a].at[2 * px + py], dst_ref=outs[a].at[kc],
                send_sem=send_sems.at[3 * a + j], recv_sem=recv_sems.at[3 * a + j],
                device_id=(px, py, c), device_id_type=MESH)

        def arrival(a, j, chip):
            px, py = chip
            return pltpu.make_async_remote_copy(
                src_ref=ins[a].at[2 * px + py], dst_ref=outs[a].at[2 * px + py],
                send_sem=send_sems.at[3 * a + j], recv_sem=recv_sems.at[3 * a + j],
                device_id=(px, py, c), device_id_type=MESH)

        mine = [pltpu.make_async_copy(ins[a].at[kc], outs[a].at[kc], local_sems.at[a]) for a in range(n)]
        sends = [copy(a, j, chip) for a in range(n) for j, chip in enumerate(chips)]
        for cp in mine + sends:
            cp.start()
        for a in range(n):
            for j, chip in enumerate(chips):
                arrival(a, j, chip).wait_recv()
        for cp in sends:
            cp.wait_send()
        for cp in mine:
            cp.wait()

    return pl.pallas_call(
        body, name=name,
        out_shape=[jax.ShapeDtypeStruct(a.shape, a.dtype) for a in arrs],
        in_specs=[_ANY] * n, out_specs=[_ANY] * n,
        scratch_shapes=[pltpu.SemaphoreType.DMA((3 * n,)), pltpu.SemaphoreType.DMA((3 * n,)),
                        pltpu.SemaphoreType.DMA((n,))],
    )(*arrs)


def pair_add(own, recv, core, name):
    _, _, r, c = own.shape
    tr = min(r, 512)

    def body(core_ref, own_ref, recv_ref, o_ref):
        del core_ref
        o_ref[...] = (own_ref[...].astype(F32) + recv_ref[...].astype(F32)).astype(BF16)

    return pl.pallas_call(
        body, name=name,
        grid_spec=pltpu.PrefetchScalarGridSpec(
            num_scalar_prefetch=1, grid=(4, r // tr),
            in_specs=[pl.BlockSpec((None, None, tr, c), lambda k, i, core_ref: (k, core_ref[0], i, 0)),
                      pl.BlockSpec((None, None, tr, c), lambda k, i, core_ref: (k, 0, i, 0))],
            out_specs=pl.BlockSpec((None, tr, c), lambda k, i, core_ref: (k, i, 0))),
        out_shape=jax.ShapeDtypeStruct((4, r, c), BF16),
        compiler_params=_params("parallel", "parallel"),
    )(core, own, recv)


def _adamw_math(w, g, m, v):
    m2 = ADAM_B1 * m + (1.0 - ADAM_B1) * g
    v2 = ADAM_B2 * v + (1.0 - ADAM_B2) * (g * g)
    m_hat = m2 / (1.0 - ADAM_B1 ** ADAM_STEP)
    v_hat = v2 / (1.0 - ADAM_B2 ** ADAM_STEP)
    delta = -ADAM_LR * (m_hat / (jnp.sqrt(v_hat) + ADAM_EPS) + ADAM_WD * w)
    return delta, m2, v2


def adamw_big(w, m, v, parts, name):
    r, c = w.shape
    tr = min(r, 256)

    def body(w_ref, m_ref, v_ref, p_ref, g_ref, d_ref, m2_ref, v2_ref):
        g = ((p_ref[0].astype(F32) + p_ref[1].astype(F32)) + p_ref[2].astype(F32)) + p_ref[3].astype(F32)
        delta, m2, v2 = _adamw_math(w_ref[...], g, m_ref[...], v_ref[...])
        g_ref[...] = g
        d_ref[...] = delta
        m2_ref[...] = m2
        v2_ref[...] = v2

    row = pl.BlockSpec((tr, c), lambda i: (i, 0))
    return pl.pallas_call(
        body, name=name, grid=(r // tr,),
        in_specs=[row, row, row, pl.BlockSpec((4, tr, c), lambda i: (0, i, 0))],
        out_specs=[row] * 4,
        out_shape=[jax.ShapeDtypeStruct((r, c), F32)] * 4,
        compiler_params=_params("parallel"),
    )(w, m, v, parts)


def sum_devices(g8, name):
    def body(g_ref, o_ref):
        tot = g_ref[0]
        for k in range(1, N_DEV):
            tot = tot + g_ref[k]
        o_ref[...] = tot

    return pl.pallas_call(body, name=name, out_shape=jax.ShapeDtypeStruct(g8.shape[1:], F32))(g8)


def adamw_small(ws, gs, ms, vs, name):
    n = len(ws)

    def body(*refs):
        w_r, g_r, m_r, v_r = refs[:n], refs[n:2 * n], refs[2 * n:3 * n], refs[3 * n:4 * n]
        d_o, m_o, v_o = refs[4 * n:5 * n], refs[5 * n:6 * n], refs[6 * n:7 * n]
        for k in range(n):
            delta, m2, v2 = _adamw_math(w_r[k][...], g_r[k][...], m_r[k][...], v_r[k][...])
            d_o[k][...] = delta
            m_o[k][...] = m2
            v_o[k][...] = v2

    shapes = [jax.ShapeDtypeStruct(w.shape, F32) for w in ws]
    outs = pl.pallas_call(body, name=name, out_shape=shapes * 3)(*ws, *gs, *ms, *vs)
    return outs[:n], outs[n:2 * n], outs[2 * n:]


def _rows128(a):
    return a.reshape(-1, 128)


def _pad_rows(a, rows):
    return jnp.pad(a, ((0, rows - a.shape[0]), (0, 0)))


def kernel(x, ln_pre_even, w_in_even, pool_w, pool_scale, w_out_even, ln_post_even, ln_pre_odd, w_in_odd, sconv_w, dconv_w, dconv_b, cnorm_g, cnorm_b, w_out_odd, ln_post_odd, loss_target, m_ln_pre_even, m_w_in_even, m_pool_w, m_pool_scale, m_w_out_even, m_ln_post_even, m_ln_pre_odd, m_w_in_odd, m_sconv_w, m_dconv_w, m_dconv_b, m_cnorm_g, m_cnorm_b, m_w_out_odd, m_ln_post_odd, v_ln_pre_even, v_w_in_even, v_pool_w, v_pool_scale, v_w_out_even, v_ln_post_even, v_ln_pre_odd, v_w_in_odd, v_sconv_w, v_dconv_w, v_dconv_b, v_cnorm_g, v_cnorm_b, v_w_out_odd, v_ln_post_odd):
    xs = x[0]
    tgt = loss_target[0]
    s, d = xs.shape
    half = d // 2
    n_heads = half // HEAD_DIM
    ng = len(POOL_WINDOWS)
    cwp = half // ng
    dev = 4 * lax.axis_index("x") + 2 * lax.axis_index("y") + lax.axis_index("c")
    core = lax.axis_index("c").astype(jnp.int32).reshape(1)

    n_in_e = w_in_even.shape[2]
    n_in_o = w_in_odd.shape[2]
    r_out = w_out_even.shape[1]
    pr = pool_w.shape[2]
    cl = sconv_w.shape[2]
    small_parts = [(_rows128(ln_pre_odd), 8), (sconv_w[0], 8), (dconv_w[0], 32), (dconv_b, 8),
                   (cnorm_g, 8), (cnorm_b, 8), (_rows128(ln_post_odd), 8)]
    small_local = jnp.concatenate([_pad_rows(a, r) for a, r in small_parts], axis=0)
    g_wie, g_pw, g_woe, g_wio, g_woo, g_small = all_gather(
        [w_in_even[0].astype(BF16), pool_w[0].reshape(ng * pr, cwp).astype(BF16), w_out_even[0].astype(BF16),
         w_in_odd[0].astype(BF16), w_out_odd[0].astype(BF16), small_local], "ag_weights")
    pool_full = g_pw.reshape(N_DEV, ng, pr, cwp).transpose(1, 0, 2, 3).reshape(ng, cwp, cwp)
    w_out_e = g_woe.reshape(1, d, d)
    w_out_o = g_woo.reshape(1, d, d)
    nl = ln_pre_odd.shape[1] // 128

    def chan(lo, rows):
        return g_small[:, lo:lo + rows].transpose(1, 0, 2).reshape(rows, N_DEV * cl)

    ln_pre_odd_f = g_small[:, 0:nl].reshape(1, d)
    sconv_f = chan(8, SCONV_K)
    dconv_f = chan(16, CONF_K)
    dconv_b_f = chan(48, 1)
    cnorm_g_f = chan(56, 1)
    cnorm_b_f = chan(64, 1)
    ln_post_odd_f = g_small[:, 72:72 + nl].reshape(1, d)

    (loss_blk, grad_x, dw_in_e, dpool, dw_out_e, dw_in_o, dw_out_o, small_g) = _fwd_bwd(
        xs, tgt, ln_pre_even, g_wie, pool_full, pool_scale, w_out_e, ln_post_even, ln_pre_odd_f, g_wio,
        sconv_f, dconv_f, dconv_b_f, cnorm_g_f, cnorm_b_f, w_out_o, ln_post_odd_f)
    loss = lax.psum(loss_blk[0, 0], ("x", "y", "c"))
    big = [(w_in_even, m_w_in_even, v_w_in_even, (d, w_in_even.shape[2])),
           (pool_w, m_pool_w, v_pool_w, (ng * pr, cwp)),
           (w_out_even, m_w_out_even, v_w_out_even, (r_out, d)),
           (w_in_odd, m_w_in_odd, v_w_in_odd, (d, w_in_odd.shape[2])),
           (w_out_odd, m_w_out_odd, v_w_out_odd, (r_out, d))]
    small_w = [ln_pre_even, pool_scale, ln_post_even, ln_pre_odd, sconv_w[0], dconv_w[0], dconv_b, cnorm_g, cnorm_b, ln_post_odd]
    small_m = [m_ln_pre_even, m_pool_scale, m_ln_post_even, m_ln_pre_odd, m_sconv_w[0], m_dconv_w[0], m_dconv_b, m_cnorm_g, m_cnorm_b, m_ln_post_odd]
    small_v = [v_ln_pre_even, v_pool_scale, v_ln_post_even, v_ln_pre_odd, v_sconv_w[0], v_dconv_w[0], v_dconv_b, v_cnorm_g, v_cnorm_b, v_ln_post_odd]
    dpool_slabs = dpool.astype(BF16).reshape(ng, N_DEV, pr, cwp).transpose(1, 0, 2, 3).reshape(N_DEV, ng * pr, cwp)
    partial_w = [dw_in_e, dpool_slabs, dw_out_e.reshape(N_DEV, r_out, d), dw_in_o, dw_out_o.reshape(N_DEV, r_out, d)]
    big_out, small_out = _reduce_and_update(partial_w, big, small_g, small_w, small_m, small_v, dev, core, d, cl)
    (g_wie_o, d_wie, m_wie, v_wie), (g_pw_o, d_pw, m_pw, v_pw), (g_woe_o, d_woe, m_woe, v_woe), \
        (g_wio_o, d_wio, m_wio, v_wio), (g_woo_o, d_woo, m_woo, v_woo) = big_out
    sg, sd, sm, sv = small_out

    def order(small, wie, pw, woe, wio, woo):
        return [small[0], wie, pw, small[1], woe, small[2], small[3], wio, small[4], small[5], small[6],
                small[7], small[8], woo, small[9]]

    grads = order(sg, g_wie_o, g_pw_o, g_woe_o, g_wio_o, g_woo_o)
    deltas = order(sd, d_wie, d_pw, d_woe, d_wio, d_woo)
    new_m = order(sm, m_wie, m_pw, m_woe, m_wio, m_woo)
    new_v = order(sv, v_wie, v_pw, v_woe, v_wio, v_woo)
    return (loss, grad_x[None], *grads, *deltas, *new_m, *new_v)


def _fwd_bwd(xs, tgt, ln_pre_even, g_wie, pool_full, pool_scale, w_out_e, ln_post_even, ln_pre_odd_f, g_wio,
             sconv_f, dconv_f, dconv_b_f, cnorm_g_f, cnorm_b_f, w_out_o, ln_post_odd_f):
    n_heads = xs.shape[1] // 2 // HEAD_DIM
    h0 = rms_fwd(xs, ln_pre_even, "rms_pre_even")
    p0 = mm_nn(h0, g_wie, BF16, "in_proj_even")
    a0 = sb_fwd(p0, n_heads, "sb_fwd")
    y0 = even_mix_fwd(a0, p0, pool_full, pool_scale, "even_mix_fwd")
    o0 = mm_nn(y0, w_out_e, F32, "out_proj_even", tn=512)
    x1 = postnorm_fwd(xs, o0, ln_post_even, "post_even")
    h1 = rms_fwd(x1, ln_pre_odd_f, "rms_pre_odd")
    p1 = mm_nn(h1, g_wio, BF16, "in_proj_odd")
    y1, dc = odd_mix_fwd(p1, sconv_f, dconv_f, dconv_b_f, cnorm_g_f, cnorm_b_f, "odd_mix_fwd")
    o1 = mm_nn(y1, w_out_o, F32, "out_proj_odd", tn=512)
    loss_blk, gx2, do1, dg_post_odd = final_fwd_bwd(x1, o1, ln_post_odd_f, tgt, "post_odd_loss")

    dw_out_o = mm_tn(y1, do1, 1, BF16, "dw_out_odd", tk=512)
    dy1 = mm_nt(do1, w_out_o, BF16, "dy_odd", tn=512)
    ddc, dg2, dgam, dbet = odd_bwd_ln(dy1, p1, dc, cnorm_g_f, cnorm_b_f, "odd_bwd_ln")
    dp1, dsconv, ddconv, ddconv_b = odd_bwd_conv(dy1, p1, ddc, dg2, sconv_f, dconv_f, "odd_bwd_conv")
    dw_in_o = mm_tn(h1, dp1, N_DEV, BF16, "dw_in_odd")
    dh1 = mm_nt(dp1, g_wio, F32, "dh_odd")
    gx1, dg_pre_odd = norm_bwd(dh1, x1, ln_pre_odd_f, gx2, F32, "pre_odd_bwd")

    do0, dg_post_even = norm_bwd(gx1, o0, ln_post_even, None, BF16, "post_even_bwd")
    dw_out_e = mm_tn(y0, do0, 1, BF16, "dw_out_even", tk=512)
    dy0 = mm_nt(do0, w_out_e, BF16, "dy_even", tn=512)
    da0, du0, dg0, dpool, dpool_scale = even_mix_bwd(dy0, a0, p0, pool_full, pool_scale, "even_mix_bwd")
    dq0, dk0, dv0 = sb_bwd(p0, a0, da0, n_heads, "sb_bwd")
    dp0 = jnp.concatenate([dq0, dk0, dv0, du0, dg0], axis=1)
    dw_in_e = mm_tn(h0, dp0, N_DEV, BF16, "dw_in_even")
    dh0 = mm_nt(dp0, g_wie, F32, "dh_even")
    grad_x, dg_pre_even = norm_bwd(dh0, xs, ln_pre_even, gx1, F32, "pre_even_bwd")
    small_g = [dg_pre_even, dpool_scale, dg_post_even, dg_pre_odd, dsconv, ddconv, ddconv_b, dgam, dbet, dg_post_odd]
    return loss_blk, grad_x, dw_in_e, dpool, dw_out_e, dw_in_o, dw_out_o, small_g


def _reduce_and_update(partial_w, big, small_g, small_w, small_m, small_v, dev, core, d, cl):
    partial_w = [a.reshape((4, 2) + a.shape[1:]) for a in partial_w]
    from_sibling = sibling_exchange(partial_w, "rs_sibling")
    chip_sums = [pair_add(o, r, core, "rs_pair_add_%d" % k) for k, (o, r) in enumerate(zip(partial_w, from_sibling))]
    from_chips = chip_exchange(chip_sums, "rs_chips")
    big_out = []
    for k, ((w, m, v, shp), parts) in enumerate(zip(big, from_chips)):
        outs = adamw_big(w.reshape(shp), m.reshape(shp), v.reshape(shp), parts, "adamw_big_%d" % k)
        big_out.append([o.reshape(w.shape) for o in outs])

    packed = jnp.concatenate([_rows128(g) for g in small_g], axis=0)
    (g8,) = all_gather([packed], "ag_small_grads")
    tot = sum_devices(g8, "sum_small_grads")
    full_g = []
    lo = 0
    for g in small_g:
        rows = g.size // 128
        full_g.append(tot[lo:lo + rows].reshape(g.shape))
        lo += rows

    def mine(g, width):
        return lax.dynamic_slice_in_dim(g, dev * width, width, axis=g.ndim - 1)

    fg = full_g
    small_gl = [fg[0], fg[1], fg[2], mine(fg[3], d // N_DEV), mine(fg[4], cl), mine(fg[5], cl), mine(fg[6], cl),
                mine(fg[7], cl), mine(fg[8], cl), mine(fg[9], d // N_DEV)]
    sd, sm, sv = adamw_small(small_w, small_gl, small_m, small_v, "adamw_small")

    def like(k, a):
        return a[None] if k in (4, 5) else a

    sg = [like(k, a) for k, a in enumerate(small_gl)]
    sd = [like(k, a) for k, a in enumerate(sd)]
    sm = [like(k, a) for k, a in enumerate(sm)]
    sv = [like(k, a) for k, a in enumerate(sv)]
    return big_out, (sg, sd, sm, sv)
```

```python
import functools
import math

import jax
import jax.numpy as jnp
from jax import lax
from jax.experimental import pallas as pl
from jax.experimental.pallas import tpu as pltpu

F32 = jnp.float32
BF16 = jnp.bfloat16
EPS = 1e-6
HEAD_DIM = 128
POOL_WINDOWS = (2, 4, 8, 16)
SCONV_K = 3
CONF_K = 31
HALO = 32
N_DEV = 8
VMEM_LIMIT = 56 * 1024 * 1024
MESH = pl.DeviceIdType.MESH

ADAM_LR = 0.001
ADAM_B1 = 0.9
ADAM_B2 = 0.999
ADAM_EPS = 1e-08
ADAM_WD = 0.01
ADAM_STEP = 10


def _params(*sem):
    return pltpu.CompilerParams(dimension_semantics=sem, vmem_limit_bytes=VMEM_LIMIT)


def _sigmoid(v):
    return 1.0 / (1.0 + jnp.exp(-v))


def _silu(v):
    return v * _sigmoid(v)


def _silu_and_grad(v):
    s = _sigmoid(v)
    return v * s, s * (1.0 + v * (1.0 - s))


def _rowsum8(v):
    r, c = v.shape
    return jnp.sum(v.reshape(r // 8, 8, c), axis=0)


def _tap_before(xx, i, rows):
    if i == 0:
        return xx[HALO:HALO + rows]
    return pltpu.roll(xx, i, 0)[HALO:HALO + rows]


def _tap_after(xx, i, rows):
    if i == 0:
        return xx[0:rows]
    return pltpu.roll(xx, xx.shape[0] - i, 0)[0:rows]


def rms_fwd(x, g, name, tm=256):
    s, d = x.shape

    def body(x_ref, g_ref, h_ref):
        xv = x_ref[...]
        r = lax.rsqrt(jnp.mean(xv * xv, axis=-1, keepdims=True) + EPS)
        h_ref[...] = (xv * r * g_ref[...]).astype(BF16)

    return pl.pallas_call(
        body, name=name, grid=(s // tm,),
        in_specs=[pl.BlockSpec((tm, d), lambda i: (i, 0)), pl.BlockSpec((1, d), lambda i: (0, 0))],
        out_specs=pl.BlockSpec((tm, d), lambda i: (i, 0)),
        out_shape=jax.ShapeDtypeStruct((s, d), BF16),
        compiler_params=_params("parallel"),
    )(x, g)


def postnorm_fwd(x, o, g, name, tm=256):
    s, d = x.shape

    def body(x_ref, o_ref, g_ref, y_ref):
        ov = o_ref[...]
        r = lax.rsqrt(jnp.mean(ov * ov, axis=-1, keepdims=True) + EPS)
        y_ref[...] = x_ref[...] + ov * r * g_ref[...]

    return pl.pallas_call(
        body, name=name, grid=(s // tm,),
        in_specs=[pl.BlockSpec((tm, d), lambda i: (i, 0)), pl.BlockSpec((tm, d), lambda i: (i, 0)),
                  pl.BlockSpec((1, d), lambda i: (0, 0))],
        out_specs=pl.BlockSpec((tm, d), lambda i: (i, 0)),
        out_shape=jax.ShapeDtypeStruct((s, d), F32),
        compiler_params=_params("parallel"),
    )(x, o, g)


def final_fwd_bwd(x1, o, g, target, name, tm=256):
    s, d = x1.shape
    n = s // tm

    def body(x_ref, o_ref, g_ref, t_ref, loss_ref, gx_ref, do_ref, dg_ref, lacc, gacc):
        i = pl.program_id(0)

        @pl.when(i == 0)
        def _():
            lacc[...] = jnp.zeros_like(lacc)
            gacc[...] = jnp.zeros_like(gacc)

        ov = o_ref[...]
        gv = g_ref[...]
        r = lax.rsqrt(jnp.mean(ov * ov, axis=-1, keepdims=True) + EPS)
        oh = ov * r
        diff = x_ref[...] + oh * gv - t_ref[...]
        lacc[...] += _rowsum8(diff * diff)
        gx = diff * (1.0 / d)
        gx_ref[...] = gx
        gacc[...] += _rowsum8(gx * oh)
        dn = gx * gv
        do_ref[...] = (r * (dn - oh * jnp.mean(dn * oh, axis=-1, keepdims=True))).astype(BF16)

        @pl.when(i == n - 1)
        def _():
            tot = jnp.sum(jnp.sum(lacc[...], axis=0, keepdims=True), axis=1, keepdims=True)
            loss_ref[...] = jnp.broadcast_to(tot * (0.5 / d), loss_ref.shape)
            dg_ref[...] = jnp.sum(gacc[...], axis=0, keepdims=True)

    row = pl.BlockSpec((tm, d), lambda i: (i, 0))
    vec = pl.BlockSpec((1, d), lambda i: (0, 0))
    return pl.pallas_call(
        body, name=name, grid=(n,),
        in_specs=[row, row, vec, row],
        out_specs=[pl.BlockSpec((8, 128), lambda i: (0, 0)), row, row, vec],
        out_shape=[jax.ShapeDtypeStruct((8, 128), F32), jax.ShapeDtypeStruct((s, d), F32),
                   jax.ShapeDtypeStruct((s, d), BF16), jax.ShapeDtypeStruct((1, d), F32)],
        scratch_shapes=[pltpu.VMEM((8, d), F32), pltpu.VMEM((8, d), F32)],
        compiler_params=_params("arbitrary"),
    )(x1, o, g, target)


def norm_bwd(dy, inp, g, resid, out_dtype, name, tm=256):
    s, d = inp.shape
    n = s // tm
    has_resid = resid is not None

    def body(*refs):
        if has_resid:
            dy_ref, x_ref, g_ref, r_ref, dx_ref, dg_ref, gacc = refs
        else:
            dy_ref, x_ref, g_ref, dx_ref, dg_ref, gacc = refs
        i = pl.program_id(0)

        @pl.when(i == 0)
        def _():
            gacc[...] = jnp.zeros_like(gacc)

        xv = x_ref[...]
        dyv = dy_ref[...].astype(F32)
        r = lax.rsqrt(jnp.mean(xv * xv, axis=-1, keepdims=True) + EPS)
        xh = xv * r
        gacc[...] += _rowsum8(dyv * xh)
        dn = dyv * g_ref[...]
        dx = r * (dn - xh * jnp.mean(dn * xh, axis=-1, keepdims=True))
        if has_resid:
            dx = dx + r_ref[...]
        dx_ref[...] = dx.astype(out_dtype)

        @pl.when(i == n - 1)
        def _():
            dg_ref[...] = jnp.sum(gacc[...], axis=0, keepdims=True)

    row = pl.BlockSpec((tm, d), lambda i: (i, 0))
    vec = pl.BlockSpec((1, d), lambda i: (0, 0))
    args = [dy, inp, g] + ([resid] if has_resid else [])
    return pl.pallas_call(
        body, name=name, grid=(n,),
        in_specs=[row, row, vec] + ([row] if has_resid else []),
        out_specs=[row, vec],
        out_shape=[jax.ShapeDtypeStruct((s, d), out_dtype), jax.ShapeDtypeStruct((1, d), F32)],
        scratch_shapes=[pltpu.VMEM((8, d), F32)],
        compiler_params=_params("arbitrary"),
    )(*args)


def _after(dep):
    if dep is None:
        return [], []
    return [dep], [pl.BlockSpec((8, 128), lambda *_: (0, 0))]


def mm_nn(a, w, out_dtype, name, tm=512, tn=None, dep=None):
    m, k = a.shape
    ns, _, n = w.shape
    tn = n if tn is None else tn
    nj = n // tn
    dep_args, dep_specs = _after(dep)

    def body(a_ref, w_ref, *rest):
        o_ref = rest[-1]
        o_ref[...] = jnp.dot(a_ref[...], w_ref[0], preferred_element_type=F32).astype(out_dtype)

    return pl.pallas_call(
        body, name=name, grid=(ns, nj, m // tm),
        in_specs=[pl.BlockSpec((tm, k), lambda s, j, i: (i, 0)),
                  pl.BlockSpec((1, k, tn), lambda s, j, i: (s, 0, j))] + dep_specs,
        out_specs=pl.BlockSpec((tm, tn), lambda s, j, i: (i, s * nj + j)),
        out_shape=jax.ShapeDtypeStruct((m, ns * n), out_dtype),
        compiler_params=_params("parallel", "parallel", "parallel"),
    )(a, w, *dep_args)


def mm_nt(a, w, out_dtype, name, tm=512, tn=None, dep=None):
    m = a.shape[0]
    ns, k, n = w.shape
    tn = n if tn is None else tn
    nj = n // tn
    steps = ns * nj
    dep_args, dep_specs = _after(dep)

    def body(a_ref, w_ref, *rest):
        o_ref, acc = rest[-2:]
        r = pl.program_id(1)

        @pl.when(r == 0)
        def _():
            acc[...] = jnp.zeros_like(acc)

        acc[...] += lax.dot_general(a_ref[...], w_ref[0], (((1,), (1,)), ((), ())),
                                    preferred_element_type=F32)

        @pl.when(r == steps - 1)
        def _():
            o_ref[...] = acc[...].astype(out_dtype)

    return pl.pallas_call(
        body, name=name, grid=(m // tm, steps),
        in_specs=[pl.BlockSpec((tm, tn), lambda i, r: (i, r)),
                  pl.BlockSpec((1, k, tn), lambda i, r: (r // nj, 0, r % nj))] + dep_specs,
        out_specs=pl.BlockSpec((tm, k), lambda i, r: (i, 0)),
        out_shape=jax.ShapeDtypeStruct((m, k), out_dtype),
        scratch_shapes=[pltpu.VMEM((tm, k), F32)],
        compiler_params=_params("parallel", "arbitrary"),
    )(a, w, *dep_args)


def mm_tn(a, b, ns, out_dtype, name, tk=1024, tm=512):
    m, k = a.shape
    n = b.shape[1] // ns
    steps = m // tm

    def body(a_ref, b_ref, o_ref, acc):
        r = pl.program_id(2)

        @pl.when(r == 0)
        def _():
            acc[...] = jnp.zeros_like(acc)

        acc[...] += lax.dot_general(a_ref[...], b_ref[...], (((0,), (0,)), ((), ())),
                                    preferred_element_type=F32)

        @pl.when(r == steps - 1)
        def _():
            o_ref[0] = acc[...].astype(out_dtype)

    return pl.pallas_call(
        body, name=name, grid=(ns, k // tk, steps),
        in_specs=[pl.BlockSpec((tm, tk), lambda s, j, r: (r, j)),
                  pl.BlockSpec((tm, n), lambda s, j, r: (r, s))],
        out_specs=pl.BlockSpec((1, tk, n), lambda s, j, r: (s, j, 0)),
        out_shape=jax.ShapeDtypeStruct((ns, k, n), out_dtype),
        scratch_shapes=[pltpu.VMEM((tk, n), F32)],
        compiler_params=_params("parallel", "parallel", "arbitrary"),
    )(a, b)


SB_BLK = 128


def _split_dot(v, tri):
    hi = v.astype(BF16)
    lo = (v - hi.astype(F32)).astype(BF16)
    return jnp.dot(hi, tri, preferred_element_type=F32) + jnp.dot(lo, tri, preferred_element_type=F32)


def _sb_scores(z, lim, dcol, tri_ex):
    mask = dcol < lim
    sp = jnp.log(1.0 + jnp.exp(-jnp.abs(z)))
    lb = jnp.minimum(z, 0.0) - sp
    l1m = jnp.where(mask, lb - z, 0.0)
    return mask, lb, l1m, _split_dot(l1m, tri_ex)


def _sb_consts():
    row = lax.broadcasted_iota(jnp.int32, (SB_BLK, SB_BLK), 0)
    col = lax.broadcasted_iota(jnp.int32, (SB_BLK, SB_BLK), 1)
    tri_ex = jnp.where(row > col, 1.0, 0.0).astype(BF16)
    tri_in = jnp.where(row >= col, 1.0, 0.0).astype(BF16)
    return col - row, tri_ex, tri_in


def sb_fwd(p, n_heads, name, tq=256, nsub=4):
    s = p.shape[0]
    h_n = n_heads
    b = SB_BLK
    nqs = tq // b
    tk = nsub * b
    scale = 1.0 / math.sqrt(HEAD_DIM)

    def body(q_ref, k_ref, v_ref, o_ref):
        qi = pl.program_id(1)
        dcol, tri_ex, _ = _sb_consts()
        qv = [q_ref[qs * b:(qs + 1) * b, :] for qs in range(nqs)]
        n_groups = ((qi + 1) * nqs - 1) // nsub + 1

        def step(it, carry):
            c1s, accs = carry
            g = n_groups - 1 - it
            off = pl.multiple_of(g * tk, tk)
            kg = k_ref[pl.ds(off, tk), :]
            vg = v_ref[pl.ds(off, tk), :]
            new_c1, new_acc = [], []
            for qs in range(nqs):
                qb = qi * nqs + qs
                z = lax.dot_general(qv[qs], kg, (((1,), (1,)), ((), ())), preferred_element_type=F32) * scale
                blocks = [_sb_scores(z[:, j * b:(j + 1) * b], (qb - (g * nsub + j)) * b, dcol, tri_ex)
                          for j in range(nsub)]
                run = c1s[qs]
                ws = [None] * nsub
                for j in reversed(range(nsub)):
                    mask, lb, l1m, ls_loc = blocks[j]
                    ws[j] = jnp.where(mask, jnp.exp(lb + ls_loc + run), 0.0).astype(BF16)
                    run = run + jnp.sum(l1m, axis=1, keepdims=True)
                w = jnp.concatenate(ws, axis=1)
                new_acc.append(accs[qs] + jnp.dot(w, vg, preferred_element_type=F32))
                new_c1.append(run)
            return tuple(new_c1), tuple(new_acc)

        init = (tuple(jnp.zeros((b, 1), F32) for _ in range(nqs)),
                tuple(jnp.zeros((b, HEAD_DIM), F32) for _ in range(nqs)))
        _, accs = lax.fori_loop(0, n_groups, step, init)
        for qs in range(nqs):
            o_ref[qs * b:(qs + 1) * b, :] = accs[qs]

    return pl.pallas_call(
        body, name=name, grid=(h_n, s // tq),
        in_specs=[pl.BlockSpec((tq, HEAD_DIM), lambda h, i: (i, h)),
                  pl.BlockSpec((s, HEAD_DIM), lambda h, i: (0, h_n + h)),
                  pl.BlockSpec((s, HEAD_DIM), lambda h, i: (0, 2 * h_n + h))],
        out_specs=pl.BlockSpec((tq, HEAD_DIM), lambda h, i: (i, h)),
        out_shape=jax.ShapeDtypeStruct((s, h_n * HEAD_DIM), F32),
        compiler_params=_params("parallel", "arbitrary"),
    )(p, p, p)


def sb_bwd(p, a, da, n_heads, name, tq=256, nsub=4, dep=None):
    s = p.shape[0]
    h_n = n_heads
    nq = s // tq
    b = SB_BLK
    nqs = tq // b
    tk = nsub * b
    scale = 1.0 / math.sqrt(HEAD_DIM)
    dep_args, dep_specs = _after(dep)

    def body(q_ref, k_ref, v_ref, a_ref, da_ref, *rest):
        dq_ref, dk_ref, dv_ref, dk_acc, dv_acc = rest[-5:]
        qi = pl.program_id(1)

        @pl.when(qi == 0)
        def _():
            dk_acc[...] = jnp.zeros_like(dk_acc)
            dv_acc[...] = jnp.zeros_like(dv_acc)

        dcol, tri_ex, tri_in = _sb_consts()
        q_all = q_ref[...]
        do_all = da_ref[...]
        qv = [q_ref[qs * b:(qs + 1) * b, :] for qs in range(nqs)]
        dov = [da_ref[qs * b:(qs + 1) * b, :] for qs in range(nqs)]
        tots = [jnp.sum(dov[qs].astype(F32) * a_ref[qs * b:(qs + 1) * b, :], axis=1, keepdims=True)
                for qs in range(nqs)]
        n_groups = ((qi + 1) * nqs - 1) // nsub + 1

        def step(it, carry):
            c1s, c2s, dqs = carry
            g = n_groups - 1 - it
            off = pl.multiple_of(g * tk, tk)
            kg = k_ref[pl.ds(off, tk), :]
            vg = v_ref[pl.ds(off, tk), :]
            new_c1, new_c2, new_dq, dz_rows, wr_rows = [], [], [], [], []
            for qs in range(nqs):
                qb = qi * nqs + qs
                z = lax.dot_general(qv[qs], kg, (((1,), (1,)), ((), ())), preferred_element_type=F32) * scale
                dw = lax.dot_general(dov[qs], vg, (((1,), (1,)), ((), ())), preferred_element_type=F32)
                blocks = [_sb_scores(z[:, j * b:(j + 1) * b], (qb - (g * nsub + j)) * b, dcol, tri_ex)
                          for j in range(nsub)]
                run1, run2 = c1s[qs], c2s[qs]
                dzs, wrs = [None] * nsub, [None] * nsub
                for j in reversed(range(nsub)):
                    mask, lb, l1m, ls_loc = blocks[j]
                    wr = jnp.where(mask, jnp.exp(lb + ls_loc + run1), 0.0).astype(BF16)
                    e = dw[:, j * b:(j + 1) * b] * wr.astype(F32)
                    later = _split_dot(e, tri_in) + run2
                    beta = jnp.exp(lb)
                    dz = jnp.where(mask, e * (1.0 - beta) - beta * (tots[qs] - later), 0.0) * scale
                    dzs[j] = dz.astype(BF16)
                    wrs[j] = wr
                    run1 = run1 + jnp.sum(l1m, axis=1, keepdims=True)
                    run2 = run2 + jnp.sum(e, axis=1, keepdims=True)
                dzq = jnp.concatenate(dzs, axis=1)
                new_dq.append(dqs[qs] + jnp.dot(dzq, kg, preferred_element_type=F32))
                new_c1.append(run1)
                new_c2.append(run2)
                dz_rows.append(dzq)
                wr_rows.append(jnp.concatenate(wrs, axis=1))
            dz_all = jnp.concatenate(dz_rows, axis=0)
            wr_all = jnp.concatenate(wr_rows, axis=0)
            dk_acc[pl.ds(off, tk), :] += lax.dot_general(dz_all, q_all, (((0,), (0,)), ((), ())),
                                                         preferred_element_type=F32)
            dv_acc[pl.ds(off, tk), :] += lax.dot_general(wr_all, do_all, (((0,), (0,)), ((), ())),
                                                         preferred_element_type=F32)
            return tuple(new_c1), tuple(new_c2), tuple(new_dq)

        zeros = tuple(jnp.zeros((b, 1), F32) for _ in range(nqs))
        _, _, dqs = lax.fori_loop(0, n_groups, step,
                                  (zeros, zeros, tuple(jnp.zeros((b, HEAD_DIM), F32) for _ in range(nqs))))
        for qs in range(nqs):
            dq_ref[qs * b:(qs + 1) * b, :] = dqs[qs].astype(BF16)

        @pl.when(qi == nq - 1)
        def _():
            dk_ref[...] = dk_acc[...].astype(BF16)
            dv_ref[...] = dv_acc[...].astype(BF16)

    blk = pl.BlockSpec((tq, HEAD_DIM), lambda h, i: (i, h))
    full = pl.BlockSpec((s, HEAD_DIM), lambda h, i: (0, h))
    return pl.pallas_call(
        body, name=name, grid=(h_n, nq),
        in_specs=[blk, pl.BlockSpec((s, HEAD_DIM), lambda h, i: (0, h_n + h)),
                  pl.BlockSpec((s, HEAD_DIM), lambda h, i: (0, 2 * h_n + h)), blk, blk] + dep_specs,
        out_specs=[blk, full, full],
        out_shape=[jax.ShapeDtypeStruct((s, h_n * HEAD_DIM), BF16)] * 3,
        scratch_shapes=[pltpu.VMEM((s, HEAD_DIM), F32), pltpu.VMEM((s, HEAD_DIM), F32)],
        compiler_params=_params("parallel", "arbitrary"),
    )(p, p, p, a, da, *dep_args)


def _pool_window(xx, win, r0, rc):
    cur = xx[HALO:HALO + rc]
    ws = cur
    for i in range(1, win):
        ws = ws + _tap_before(xx, i, rc)
    t_idx = r0 + lax.broadcasted_iota(jnp.int32, (rc, 1), 0)
    inv = 1.0 / jnp.minimum(win, t_idx + 1).astype(F32)
    return ws * inv - cur, inv


def even_mix_fwd(a, p, pool_w, pool_scale, name, rc=64):
    s = p.shape[0]
    ng = len(POOL_WINDOWS)
    cw = pool_w.shape[1]
    n_chunks = s // rc

    def body(a_ref, u_ref, g_ref, w_ref, sc_ref, y_ref, upad):
        j = pl.program_id(0)

        @pl.when(j < ng)
        def _():
            def chunk(ci, carry):
                rows = pl.ds(pl.multiple_of(ci * rc, rc), rc)
                y_ref[rows, :] = (a_ref[rows, :] * _silu(g_ref[rows, :].astype(F32))).astype(BF16)
                return carry

            lax.fori_loop(0, n_chunks, chunk, 0)

        for gi, win in enumerate(POOL_WINDOWS):
            @pl.when(j == ng + gi)
            def _(win=win):
                upad[0:HALO, :] = jnp.zeros((HALO, cw), F32)

                def fill(ci, carry):
                    r0 = pl.multiple_of(ci * rc, rc)
                    upad[pl.ds(pl.multiple_of(r0 + HALO, HALO), rc), :] = u_ref[pl.ds(r0, rc), :].astype(F32)
                    return carry

                lax.fori_loop(0, n_chunks, fill, 0)

                def chunk(ci, carry):
                    r0 = pl.multiple_of(ci * rc, rc)
                    rows = pl.ds(r0, rc)
                    pooled, _ = _pool_window(upad[pl.ds(r0, HALO + rc), :], win, r0, rc)
                    t = jnp.dot(pooled.astype(BF16), w_ref[0], preferred_element_type=F32)
                    y_ref[rows, :] = (t * sc_ref[...] * _silu(g_ref[rows, :].astype(F32))).astype(BF16)
                    return carry

                lax.fori_loop(0, n_chunks, chunk, 0)

    grp = lambda j: jnp.maximum(j - ng, 0)
    return pl.pallas_call(
        body, name=name, grid=(2 * ng,),
        in_specs=[pl.BlockSpec((s, cw), lambda j: (0, jnp.minimum(j, ng - 1))),
                  pl.BlockSpec((s, cw), lambda j: (0, 3 * ng + grp(j))),
                  pl.BlockSpec((s, cw), lambda j: (0, 4 * ng + j)),
                  pl.BlockSpec((1, cw, cw), lambda j: (grp(j), 0, 0)),
                  pl.BlockSpec((1, cw), lambda j: (0, grp(j)))],
        out_specs=pl.BlockSpec((s, cw), lambda j: (0, j)),
        out_shape=jax.ShapeDtypeStruct((s, 2 * ng * cw), BF16),
        scratch_shapes=[pltpu.VMEM((HALO + s, cw), F32)],
        compiler_params=_params("arbitrary"),
    )(a, p, p, pool_w, pool_scale)


def even_mix_bwd(dy, a, p, pool_w, pool_scale, name, rc=64):
    s = p.shape[0]
    ng = len(POOL_WINDOWS)
    cw = pool_w.shape[1]
    n_chunks = s // rc

    def body(dy_ref, a_ref, u_ref, g_ref, w_ref, sc_ref, da_ref, du_ref, dg_ref, dw_ref, dsc_ref,
             upad, rpad, dpl, dw_acc, dsc_acc):
        j = pl.program_id(0)

        @pl.when(j < ng)
        def _():
            def chunk(ci, carry):
                rows = pl.ds(pl.multiple_of(ci * rc, rc), rc)
                dyv = dy_ref[rows, :].astype(F32)
                sg, dsg = _silu_and_grad(g_ref[rows, :].astype(F32))
                da_ref[rows, :] = (dyv * sg).astype(BF16)
                dg_ref[rows, :] = (dyv * a_ref[rows, :] * dsg).astype(BF16)
                return carry

            lax.fori_loop(0, n_chunks, chunk, 0)

        for gi, win in enumerate(POOL_WINDOWS):
            @pl.when(j == ng + gi)
            def _(win=win):
                upad[0:HALO, :] = jnp.zeros((HALO, cw), F32)
                rpad[s:s + HALO, :] = jnp.zeros((HALO, cw), F32)
                dw_acc[...] = jnp.zeros_like(dw_acc)
                dsc_acc[...] = jnp.zeros_like(dsc_acc)

                def fill(ci, carry):
                    r0 = pl.multiple_of(ci * rc, rc)
                    upad[pl.ds(pl.multiple_of(r0 + HALO, HALO), rc), :] = u_ref[pl.ds(r0, rc), :].astype(F32)
                    return carry

                lax.fori_loop(0, n_chunks, fill, 0)

                def chunk(ci, carry):
                    r0 = pl.multiple_of(ci * rc, rc)
                    rows = pl.ds(r0, rc)
                    pooled, inv = _pool_window(upad[pl.ds(r0, HALO + rc), :], win, r0, rc)
                    pb = pooled.astype(BF16)
                    wv = w_ref[0]
                    t = jnp.dot(pb, wv, preferred_element_type=F32)
                    scv = sc_ref[...]
                    dyv = dy_ref[rows, :].astype(F32)
                    sg, dsg = _silu_and_grad(g_ref[rows, :].astype(F32))
                    dpo = dyv * sg
                    dg_ref[rows, :] = (dyv * t * scv * dsg).astype(BF16)
                    dsc_acc[...] += _rowsum8(dpo * t)
                    dtb = (dpo * scv).astype(BF16)
                    dw_acc[...] += lax.dot_general(pb, dtb, (((0,), (0,)), ((), ())),
                                                   preferred_element_type=F32)
                    dpooled = lax.dot_general(dtb, wv, (((1,), (1,)), ((), ())),
                                              preferred_element_type=F32)
                    dpl[rows, :] = dpooled
                    rpad[rows, :] = dpooled * inv
                    return carry

                lax.fori_loop(0, n_chunks, chunk, 0)

                def chunk2(ci, carry):
                    r0 = pl.multiple_of(ci * rc, rc)
                    rows = pl.ds(r0, rc)
                    xx = rpad[pl.ds(r0, rc + HALO), :]
                    fs = xx[0:rc]
                    for i in range(1, win):
                        fs = fs + _tap_after(xx, i, rc)
                    du_ref[rows, :] = (fs - dpl[rows, :]).astype(BF16)
                    return carry

                lax.fori_loop(0, n_chunks, chunk2, 0)
                dw_ref[0] = dw_acc[...]
                dsc_ref[...] = jnp.sum(dsc_acc[...], axis=0, keepdims=True)

    grp = lambda j: jnp.maximum(j - ng, 0)
    att = lambda j: jnp.minimum(j, ng - 1)
    return pl.pallas_call(
        body, name=name, grid=(2 * ng,),
        in_specs=[pl.BlockSpec((s, cw), lambda j: (0, j)),
                  pl.BlockSpec((s, cw), lambda j: (0, att(j))),
                  pl.BlockSpec((s, cw), lambda j: (0, 3 * ng + grp(j))),
                  pl.BlockSpec((s, cw), lambda j: (0, 4 * ng + j)),
                  pl.BlockSpec((1, cw, cw), lambda j: (grp(j), 0, 0)),
                  pl.BlockSpec((1, cw), lambda j: (0, grp(j)))],
        out_specs=[pl.BlockSpec((s, cw), lambda j: (0, att(j))),
                   pl.BlockSpec((s, cw), lambda j: (0, grp(j))),
                   pl.BlockSpec((s, cw), lambda j: (0, j)),
                   pl.BlockSpec((1, cw, cw), lambda j: (grp(j), 0, 0)),
                   pl.BlockSpec((1, cw), lambda j: (0, grp(j)))],
        out_shape=[jax.ShapeDtypeStruct((s, ng * cw), BF16), jax.ShapeDtypeStruct((s, ng * cw), BF16),
                   jax.ShapeDtypeStruct((s, 2 * ng * cw), BF16),
                   jax.ShapeDtypeStruct((ng, cw, cw), F32), jax.ShapeDtypeStruct((1, ng * cw), F32)],
        scratch_shapes=[pltpu.VMEM((HALO + s, cw), F32), pltpu.VMEM((s + HALO, cw), F32),
                        pltpu.VMEM((s, cw), F32), pltpu.VMEM((cw, cw), F32), pltpu.VMEM((8, cw), F32)],
        compiler_params=_params("arbitrary"),
    )(dy, a, p, p, pool_w, pool_scale)


def _halo_before(tm):
    return lambda i: jnp.maximum(i * (tm // HALO) - 1, 0)


def _halo_after(tm, s):
    return lambda i: jnp.minimum((i + 1) * (tm // HALO), s // HALO - 1)


def odd_mix_fwd(p, sconv_w, dconv_w, dconv_b, cnorm_g, cnorm_b, name, tm=128):
    s = p.shape[0]
    cw = sconv_w.shape[1]
    n = s // tm
    lanes = 128
    hb = _halo_before(tm)

    def body(hc_ref, hch_ref, bc_ref, cc_ref, cch_ref, ga_ref, gah_ref, gb_ref, gbh_ref, g1_ref, g2_ref,
             sw_ref, dw_ref, db_ref, gam_ref, bet_ref, y_ref, dc_ref):
        first = pl.program_id(0) == 0
        for l in range(cw // lanes):
            cols = slice(l * lanes, (l + 1) * lanes)
            mh = jnp.where(first, 0.0, cch_ref[:, cols].astype(F32) * hch_ref[:, cols].astype(F32))
            mm = cc_ref[:, cols].astype(F32) * hc_ref[:, cols].astype(F32)
            xx = jnp.concatenate([mh, mm], axis=0)
            cv = jnp.zeros((tm, lanes), F32)
            for k in range(SCONV_K):
                cv = cv + sw_ref[k:k + 1, cols] * _tap_before(xx, SCONV_K - 1 - k, tm)
            c_out = bc_ref[:, cols].astype(F32) * cv
            y_ref[:, cols] = (c_out * _silu(g1_ref[:, cols].astype(F32))).astype(BF16)
            dh = jnp.where(first, 0.0, gah_ref[:, cols].astype(F32) * _sigmoid(gbh_ref[:, cols].astype(F32)))
            dm = ga_ref[:, cols].astype(F32) * _sigmoid(gb_ref[:, cols].astype(F32))
            xx = jnp.concatenate([dh, dm], axis=0)
            acc = jnp.zeros((tm, lanes), F32) + db_ref[:, cols]
            for k in range(CONF_K):
                acc = acc + dw_ref[k:k + 1, cols] * _tap_before(xx, CONF_K - 1 - k, tm)
            dc_ref[:, cols] = acc
        rs = 32
        for r in range(tm // rs):
            rows = slice(r * rs, (r + 1) * rs)
            xv = dc_ref[rows, :]
            mu = jnp.mean(xv, axis=-1, keepdims=True)
            xc = xv - mu
            rstd = lax.rsqrt(jnp.mean(xc * xc, axis=-1, keepdims=True) + EPS)
            ln = xc * rstd * gam_ref[...] + bet_ref[...]
            y_ref[rows, cw:2 * cw] = (_silu(ln) * _silu(g2_ref[rows, :].astype(F32))).astype(BF16)

    main = lambda c: pl.BlockSpec((tm, cw), lambda i: (i, c))
    halo = lambda c: pl.BlockSpec((HALO, cw), lambda i: (hb(i), c))
    vec = lambda r: pl.BlockSpec((r, cw), lambda i: (0, 0))
    return pl.pallas_call(
        body, name=name, grid=(n,),
        in_specs=[main(0), halo(0), main(1), main(2), halo(2), main(3), halo(3), main(4), halo(4),
                  main(5), main(6), vec(SCONV_K), vec(CONF_K), vec(1), vec(1), vec(1)],
        out_specs=[pl.BlockSpec((tm, 2 * cw), lambda i: (i, 0)), pl.BlockSpec((tm, cw), lambda i: (i, 0))],
        out_shape=[jax.ShapeDtypeStruct((s, 2 * cw), BF16), jax.ShapeDtypeStruct((s, cw), F32)],
        compiler_params=_params("parallel"),
    )(p, p, p, p, p, p, p, p, p, p, p, sconv_w, dconv_w, dconv_b, cnorm_g, cnorm_b)


def odd_bwd_ln(dy, p, dc, cnorm_g, cnorm_b, name, tm=256):
    s = p.shape[0]
    cw = dc.shape[1]
    n = s // tm
    rs = 32

    def body(dy_ref, g2_ref, dc_ref, gam_ref, bet_ref, ddc_ref, dg_ref, dgam_ref, dbet_ref, gacc, bacc):
        i = pl.program_id(0)

        @pl.when(i == 0)
        def _():
            gacc[...] = jnp.zeros_like(gacc)
            bacc[...] = jnp.zeros_like(bacc)

        def chunk(ci, carry):
            rows = pl.ds(pl.multiple_of(ci * rs, rs), rs)
            xv = dc_ref[rows, :]
            mu = jnp.mean(xv, axis=-1, keepdims=True)
            xc = xv - mu
            rstd = lax.rsqrt(jnp.mean(xc * xc, axis=-1, keepdims=True) + EPS)
            xh = xc * rstd
            gam = gam_ref[...]
            sl, dsl = _silu_and_grad(xh * gam + bet_ref[...])
            sg, dsg = _silu_and_grad(g2_ref[rows, :].astype(F32))
            dyv = dy_ref[rows, :].astype(F32)
            dg_ref[rows, :] = (dyv * sl * dsg).astype(BF16)
            dln = dyv * sg * dsl
            gacc[...] += _rowsum8(dln * xh)
            bacc[...] += _rowsum8(dln)
            dxh = dln * gam
            ddc_ref[rows, :] = rstd * (dxh - jnp.mean(dxh, axis=-1, keepdims=True)
                                       - xh * jnp.mean(dxh * xh, axis=-1, keepdims=True))
            return carry

        lax.fori_loop(0, tm // rs, chunk, 0)

        @pl.when(i == n - 1)
        def _():
            dgam_ref[...] = jnp.sum(gacc[...], axis=0, keepdims=True)
            dbet_ref[...] = jnp.sum(bacc[...], axis=0, keepdims=True)

    vec = pl.BlockSpec((1, cw), lambda i: (0, 0))
    return pl.pallas_call(
        body, name=name, grid=(n,),
        in_specs=[pl.BlockSpec((tm, cw), lambda i: (i, 1)), pl.BlockSpec((tm, cw), lambda i: (i, 6)),
                  pl.BlockSpec((tm, cw), lambda i: (i, 0)), vec, vec],
        out_specs=[pl.BlockSpec((tm, cw), lambda i: (i, 0)), pl.BlockSpec((tm, cw), lambda i: (i, 0)), vec, vec],
        out_shape=[jax.ShapeDtypeStruct((s, cw), F32), jax.ShapeDtypeStruct((s, cw), BF16),
                   jax.ShapeDtypeStruct((1, cw), F32), jax.ShapeDtypeStruct((1, cw), F32)],
        scratch_shapes=[pltpu.VMEM((8, cw), F32), pltpu.VMEM((8, cw), F32)],
        compiler_params=_params("arbitrary"),
    )(dy, p, dc, cnorm_g, cnorm_b)


def odd_bwd_conv(dy, p, ddc, dg2, sconv_w, dconv_w, name, tm=128):
    s = p.shape[0]
    cw = ddc.shape[1]
    n = s // tm
    lanes = 128
    hb = _halo_before(tm)
    ha = _halo_after(tm, s)

    def body(dy_ref, dya_ref, g1_ref, g1a_ref, bc_ref, bca_ref, hc_ref, hch_ref, cc_ref, cch_ref,
             ddc_ref, ddca_ref, ga_ref, gah_ref, gb_ref, gbh_ref, dg2_ref, sw_ref, dw_ref,
             dp_ref, dsw_ref, ddw_ref, ddb_ref, sw_acc, dw_acc, db_acc):
        i = pl.program_id(0)
        first = i == 0
        last = i == n - 1

        @pl.when(first)
        def _():
            sw_acc[...] = jnp.zeros_like(sw_acc)
            dw_acc[...] = jnp.zeros_like(dw_acc)
            db_acc[...] = jnp.zeros_like(db_acc)

        for l in range(cw // lanes):
            cols = slice(l * lanes, (l + 1) * lanes)
            mh = jnp.where(first, 0.0, cch_ref[:, cols].astype(F32) * hch_ref[:, cols].astype(F32))
            hcv = hc_ref[:, cols].astype(F32)
            ccv = cc_ref[:, cols].astype(F32)
            xx = jnp.concatenate([mh, ccv * hcv], axis=0)
            taps = [_tap_before(xx, SCONV_K - 1 - k, tm) for k in range(SCONV_K)]
            cv = jnp.zeros((tm, lanes), F32)
            for k in range(SCONV_K):
                cv = cv + sw_ref[k:k + 1, cols] * taps[k]
            bcv = bc_ref[:, cols].astype(F32)
            dyv = dy_ref[:, cols].astype(F32)
            sg, dsg = _silu_and_grad(g1_ref[:, cols].astype(F32))
            dco = dyv * sg
            dp_ref[:, 5 * cw + l * lanes:5 * cw + (l + 1) * lanes] = (dyv * bcv * cv * dsg).astype(BF16)
            dp_ref[:, cw + l * lanes:cw + (l + 1) * lanes] = (dco * cv).astype(BF16)
            dcv = dco * bcv
            for k in range(SCONV_K):
                sw_acc[k * 8:(k + 1) * 8, cols] += _rowsum8(dcv * taps[k])
            dcv_a = jnp.where(last, 0.0, dya_ref[:, cols].astype(F32) * _silu(g1a_ref[:, cols].astype(F32))
                              * bca_ref[:, cols].astype(F32))
            xx = jnp.concatenate([dcv, dcv_a], axis=0)
            dm = jnp.zeros((tm, lanes), F32)
            for k in range(SCONV_K):
                dm = dm + sw_ref[k:k + 1, cols] * _tap_after(xx, SCONV_K - 1 - k, tm)
            dp_ref[:, l * lanes:(l + 1) * lanes] = (dm * ccv).astype(BF16)
            dp_ref[:, 2 * cw + l * lanes:2 * cw + (l + 1) * lanes] = (dm * hcv).astype(BF16)
            gav = ga_ref[:, cols].astype(F32)
            sb = _sigmoid(gb_ref[:, cols].astype(F32))
            dh = jnp.where(first, 0.0, gah_ref[:, cols].astype(F32) * _sigmoid(gbh_ref[:, cols].astype(F32)))
            xx = jnp.concatenate([dh, gav * sb], axis=0)
            ddcv = ddc_ref[:, cols]
            db_acc[:, cols] += _rowsum8(ddcv)
            for k in range(CONF_K):
                dw_acc[k * 8:(k + 1) * 8, cols] += _rowsum8(ddcv * _tap_before(xx, CONF_K - 1 - k, tm))
            ddc_a = jnp.where(last, 0.0, ddca_ref[:, cols])
            xx = jnp.concatenate([ddcv, ddc_a], axis=0)
            dgl = jnp.zeros((tm, lanes), F32)
            for k in range(CONF_K):
                dgl = dgl + dw_ref[k:k + 1, cols] * _tap_after(xx, CONF_K - 1 - k, tm)
            dp_ref[:, 3 * cw + l * lanes:3 * cw + (l + 1) * lanes] = (dgl * sb).astype(BF16)
            dp_ref[:, 4 * cw + l * lanes:4 * cw + (l + 1) * lanes] = (dgl * gav * sb * (1.0 - sb)).astype(BF16)
        dp_ref[:, 6 * cw:7 * cw] = dg2_ref[...]

        @pl.when(last)
        def _():
            for k in range(SCONV_K):
                dsw_ref[k:k + 1, :] = jnp.sum(sw_acc[k * 8:(k + 1) * 8, :], axis=0, keepdims=True)
            for k in range(CONF_K):
                ddw_ref[k:k + 1, :] = jnp.sum(dw_acc[k * 8:(k + 1) * 8, :], axis=0, keepdims=True)
            ddb_ref[...] = jnp.sum(db_acc[...], axis=0, keepdims=True)

    def main(c):
        return pl.BlockSpec((tm, cw), lambda i: (i, c))

    def before(c):
        return pl.BlockSpec((HALO, cw), lambda i: (hb(i), c))

    def after(c):
        return pl.BlockSpec((HALO, cw), lambda i: (ha(i), c))

    def vec(r):
        return pl.BlockSpec((r, cw), lambda i: (0, 0))

    return pl.pallas_call(
        body, name=name, grid=(n,),
        in_specs=[main(0), after(0), main(5), after(5), main(1), after(1), main(0), before(0), main(2), before(2),
                  main(0), after(0), main(3), before(3), main(4), before(4), main(0), vec(SCONV_K), vec(CONF_K)],
        out_specs=[pl.BlockSpec((tm, 7 * cw), lambda i: (i, 0)), vec(SCONV_K), vec(CONF_K), vec(1)],
        out_shape=[jax.ShapeDtypeStruct((s, 7 * cw), BF16), jax.ShapeDtypeStruct((SCONV_K, cw), F32),
                   jax.ShapeDtypeStruct((CONF_K, cw), F32), jax.ShapeDtypeStruct((1, cw), F32)],
        scratch_shapes=[pltpu.VMEM((8 * SCONV_K, cw), F32), pltpu.VMEM((8 * CONF_K, cw), F32),
                        pltpu.VMEM((8, cw), F32)],
        compiler_params=_params("arbitrary"),
    )(dy, dy, p, p, p, p, p, p, p, p, ddc, ddc, p, p, p, p, dg2, sconv_w, dconv_w)


_ANY = pl.BlockSpec(memory_space=pl.ANY)


def _place():
    return lax.axis_index("x"), lax.axis_index("y"), lax.axis_index("c")


def all_gather(arrs, name):
    n = len(arrs)

    def body(*refs):
        ins, outs = refs[:n], refs[n:2 * n]
        send_sems, recv_sems, local_sems = refs[2 * n:]
        x, y, c = _place()
        me, sibling = (x, y, c), (x, y, 1 - c)
        chips = [(1 - x, y), (x, 1 - y), (1 - x, 1 - y)]

        def copy(a, k, block, to, src=None):
            px, py, pc = block
            dst = outs[a].at[4 * px + 2 * py + pc]
            return pltpu.make_async_remote_copy(
                src_ref=dst if src is None else src, dst_ref=dst,
                send_sem=send_sems.at[7 * a + k], recv_sem=recv_sems.at[7 * a + k],
                device_id=to, device_id_type=MESH)

        mine = [pltpu.make_async_copy(ins[a], outs[a].at[4 * x + 2 * y + c], local_sems.at[a]) for a in range(n)]
        for cp in mine:
            cp.start()
        first = []
        for a in range(n):
            first.append(copy(a, 0, me, sibling, src=ins[a]))
            first += [copy(a, 1 + j, me, (*chip, c), src=ins[a]) for j, chip in enumerate(chips)]
        for cp in first:
            cp.start()
        passed = []
        for a in range(n):
            for j, chip in enumerate(chips):
                copy(a, 1 + j, (*chip, c), me).wait_recv()
                cp = copy(a, 4 + j, (*chip, c), sibling)
                cp.start()
                passed.append(cp)
        for a in range(n):
            copy(a, 0, sibling, me).wait_recv()
            for j, chip in enumerate(chips):
                copy(a, 4 + j, (*chip, 1 - c), me).wait_recv()
        for cp in first + passed:
            cp.wait_send()
        for cp in mine:
            cp.wait()

    return pl.pallas_call(
        body, name=name,
        out_shape=[jax.ShapeDtypeStruct((N_DEV,) + a.shape, a.dtype) for a in arrs],
        in_specs=[_ANY] * n, out_specs=[_ANY] * n,
        scratch_shapes=[pltpu.SemaphoreType.DMA((7 * n,)), pltpu.SemaphoreType.DMA((7 * n,)),
                        pltpu.SemaphoreType.DMA((n,))],
    )(*arrs)


def sibling_exchange(arrs, name):
    n = len(arrs)

    def body(*refs):
        ins, outs = refs[:n], refs[n:2 * n]
        send_sems, recv_sems = refs[2 * n:]
        x, y, c = _place()
        cps = [pltpu.make_async_remote_copy(
            src_ref=ins[a].at[:, pl.ds(1 - c, 1)], dst_ref=outs[a],
            send_sem=send_sems.at[a], recv_sem=recv_sems.at[a],
            device_id=(x, y, 1 - c), device_id_type=MESH) for a in range(n)]
        for cp in cps:
            cp.start()
        for cp in cps:
            cp.wait()

    return pl.pallas_call(
        body, name=name,
        out_shape=[jax.ShapeDtypeStruct((4, 1) + a.shape[2:], a.dtype) for a in arrs],
        in_specs=[_ANY] * n, out_specs=[_ANY] * n,
        scratch_shapes=[pltpu.SemaphoreType.DMA((n,)), pltpu.SemaphoreType.DMA((n,))],
    )(*arrs)


_HBM = pl.BlockSpec(memory_space=pltpu.HBM)
_SEM = pl.BlockSpec(memory_space=pltpu.SEMAPHORE)
_DATAFLOW = pltpu.SideEffectType.DATAFLOW_SIDE_EFFECTING


def _chip_copies(kind, srcs, lands, send_sems, recv_sems):
    x, y, c = _place()
    out = []
    for a in range(len(srcs)):
        for j, (px, py) in enumerate([(1 - x, y), (x, 1 - y), (1 - x, 1 - y)]):
            if kind == "gather":
                src, dst, arrives = srcs[a], lands[a].at[4 * x + 2 * y + c], lands[a].at[4 * px + 2 * py + c]
            else:
                src, dst, arrives = srcs[a].at[2 * px + py], lands[a].at[2 * x + y], lands[a].at[2 * px + py]
            sems = dict(send_sem=send_sems.at[3 * a + j], recv_sem=recv_sems.at[3 * a + j],
                        device_id=(px, py, c), device_id_type=MESH)
            out.append((pltpu.make_async_remote_copy(src_ref=src, dst_ref=dst, **sems),
                        pltpu.make_async_remote_copy(src_ref=src, dst_ref=arrives, **sems)))
    return out


def chips_start(kind, srcs, lands, deps, name):
    n = len(srcs)

    def body(*refs):
        send_sems, recv_sems = refs[2 * n + len(deps)], refs[2 * n + len(deps) + 1]
        for copy, _ in _chip_copies(kind, refs[:n], refs[n:2 * n], send_sems, recv_sems):
            copy.start()
        token = refs[-1]
        token[...] = jnp.zeros_like(token)

    held = [pltpu.HBM(a.shape, a.dtype) for a in list(srcs) + list(lands)]
    outs = pl.pallas_call(
        body, name=name,
        out_shape=(pltpu.SemaphoreType.DMA((3 * n,)), pltpu.SemaphoreType.DMA((3 * n,)), *held,
                   jax.ShapeDtypeStruct((8, 128), F32)),
        in_specs=[_HBM] * (2 * n) + [_ANY] * len(deps),
        out_specs=(_SEM, _SEM, *([_HBM] * (2 * n)), pl.BlockSpec(memory_space=pltpu.VMEM)),
        input_output_aliases={i: 2 + i for i in range(2 * n)},
        compiler_params=pltpu.CompilerParams(has_side_effects=_DATAFLOW),
    )(*[pltpu.with_memory_space_constraint(a, pltpu.HBM) for a in list(srcs) + list(lands)], *deps)
    return outs[0], outs[1], list(outs[2:2 + n]), list(outs[2 + n:2 + 2 * n]), outs[-1]


def chips_wait(kind, send_sems, recv_sems, srcs, lands, afters, name):
    n = len(srcs)

    def body(*refs):
        for _, arrival in _chip_copies(kind, refs[:n], refs[n:2 * n], refs[2 * n], refs[2 * n + 1]):
            arrival.wait_send()
            arrival.wait_recv()

    outs = pl.pallas_call(
        body, name=name,
        out_shape=[pltpu.HBM(a.shape, a.dtype) for a in list(srcs) + list(lands)],
        in_specs=[_HBM] * (2 * n) + [_SEM, _SEM] + [_ANY] * len(afters),
        out_specs=[_HBM] * (2 * n),
        input_output_aliases={i: i for i in range(2 * n)},
        compiler_params=pltpu.CompilerParams(has_side_effects=_DATAFLOW),
    )(*srcs, *lands, send_sems, recv_sems, *afters)
    return list(outs[:n]), list(outs[n:])


def gather_finish(srcs, lands, name):
    n = len(srcs)

    def body(*refs):
        srcs_r, outs = refs[:n], refs[2 * n:3 * n]
        send_sems, recv_sems, local_sems = refs[3 * n:]
        x, y, c = _place()
        blocks = [(x, y), (1 - x, y), (x, 1 - y), (1 - x, 1 - y)]

        def copy(a, j, core, src=None):
            px, py = blocks[j]
            dst = outs[a].at[4 * px + 2 * py + core]
            return pltpu.make_async_remote_copy(
                src_ref=dst if src is None else src, dst_ref=dst,
                send_sem=send_sems.at[4 * a + j], recv_sem=recv_sems.at[4 * a + j],
                device_id=(x, y, 1 - c), device_id_type=MESH)

        mine = [pltpu.make_async_copy(srcs_r[a], outs[a].at[4 * x + 2 * y + c], local_sems.at[a]) for a in range(n)]
        sends = [copy(a, j, c, src=srcs_r[a] if j == 0 else None) for a in range(n) for j in range(4)]
        for cp in mine + sends:
            cp.start()
        for a in range(n):
            for j in range(4):
                copy(a, j, 1 - c).wait_recv()
        for cp in sends:
            cp.wait_send()
        for cp in mine:
            cp.wait()

    return pl.pallas_call(
        body, name=name,
        out_shape=[jax.ShapeDtypeStruct(a.shape, a.dtype) for a in lands],
        in_specs=[_ANY] * (2 * n), out_specs=[_ANY] * n,
        input_output_aliases={n + i: i for i in range(n)},
        scratch_shapes=[pltpu.SemaphoreType.DMA((4 * n,)), pltpu.SemaphoreType.DMA((4 * n,)),
                        pltpu.SemaphoreType.DMA((n,))],
    )(*srcs, *lands)


def pair_add(own, recv, core, name):
    _, _, r, c = own.shape
    tr = min(r, 512)

    def body(core_ref, own_ref, recv_ref, o_ref):
        del core_ref
        o_ref[...] = (own_ref[...].astype(F32) + recv_ref[...].astype(F32)).astype(BF16)

    return pl.pallas_call(
        body, name=name,
        grid_spec=pltpu.PrefetchScalarGridSpec(
            num_scalar_prefetch=1, grid=(4, r // tr),
            in_specs=[pl.BlockSpec((None, None, tr, c), lambda k, i, core_ref: (k, core_ref[0], i, 0)),
                      pl.BlockSpec((None, None, tr, c), lambda k, i, core_ref: (k, 0, i, 0))],
            out_specs=pl.BlockSpec((None, tr, c), lambda k, i, core_ref: (k, i, 0))),
        out_shape=jax.ShapeDtypeStruct((4, r, c), BF16),
        compiler_params=_params("parallel", "parallel"),
    )(core, own, recv)


def _adamw_math(w, g, m, v):
    m2 = ADAM_B1 * m + (1.0 - ADAM_B1) * g
    v2 = ADAM_B2 * v + (1.0 - ADAM_B2) * (g * g)
    m_hat = m2 / (1.0 - ADAM_B1 ** ADAM_STEP)
    v_hat = v2 / (1.0 - ADAM_B2 ** ADAM_STEP)
    delta = -ADAM_LR * (m_hat / (jnp.sqrt(v_hat) + ADAM_EPS) + ADAM_WD * w)
    return delta, m2, v2


def adamw_big(w, m, v, own, got, chip, name):
    r, c = w.shape
    tr = min(r, 256)

    def body(chip_ref, w_ref, m_ref, v_ref, p0, p1, p2, p3, g_ref, d_ref, m2_ref, v2_ref):
        del chip_ref
        g = ((p0[...].astype(F32) + p1[...].astype(F32)) + p2[...].astype(F32)) + p3[...].astype(F32)
        delta, m2, v2 = _adamw_math(w_ref[...], g, m_ref[...], v_ref[...])
        g_ref[...] = g
        d_ref[...] = delta
        m2_ref[...] = m2
        v2_ref[...] = v2

    row = pl.BlockSpec((tr, c), lambda i, chip_ref: (i, 0))

    def slab(flip):
        return pl.BlockSpec((None, tr, c), lambda i, chip_ref: (chip_ref[0] ^ flip, i, 0))

    return pl.pallas_call(
        body, name=name,
        grid_spec=pltpu.PrefetchScalarGridSpec(
            num_scalar_prefetch=1, grid=(r // tr,),
            in_specs=[row, row, row, slab(0), slab(1), slab(2), slab(3)],
            out_specs=[row] * 4),
        out_shape=[jax.ShapeDtypeStruct((r, c), F32)] * 4,
        compiler_params=_params("parallel"),
    )(chip, w, m, v, own, got, got, got)


def sum_devices(g8, name):
    def body(g_ref, o_ref):
        tot = g_ref[0]
        for k in range(1, N_DEV):
            tot = tot + g_ref[k]
        o_ref[...] = tot

    return pl.pallas_call(body, name=name, out_shape=jax.ShapeDtypeStruct(g8.shape[1:], F32))(g8)


def adamw_small(ws, gs, ms, vs, name):
    n = len(ws)

    def body(*refs):
        w_r, g_r, m_r, v_r = refs[:n], refs[n:2 * n], refs[2 * n:3 * n], refs[3 * n:4 * n]
        d_o, m_o, v_o = refs[4 * n:5 * n], refs[5 * n:6 * n], refs[6 * n:7 * n]
        for k in range(n):
            delta, m2, v2 = _adamw_math(w_r[k][...], g_r[k][...], m_r[k][...], v_r[k][...])
            d_o[k][...] = delta
            m_o[k][...] = m2
            v_o[k][...] = v2

    shapes = [jax.ShapeDtypeStruct(w.shape, F32) for w in ws]
    outs = pl.pallas_call(body, name=name, out_shape=shapes * 3)(*ws, *gs, *ms, *vs)
    return outs[:n], outs[n:2 * n], outs[2 * n:]


def _rows128(a):
    return a.reshape(-1, 128)


def _pad_rows(a, rows):
    return jnp.pad(a, ((0, rows - a.shape[0]), (0, 0)))


def kernel(x, ln_pre_even, w_in_even, pool_w, pool_scale, w_out_even, ln_post_even, ln_pre_odd, w_in_odd, sconv_w, dconv_w, dconv_b, cnorm_g, cnorm_b, w_out_odd, ln_post_odd, loss_target, m_ln_pre_even, m_w_in_even, m_pool_w, m_pool_scale, m_w_out_even, m_ln_post_even, m_ln_pre_odd, m_w_in_odd, m_sconv_w, m_dconv_w, m_dconv_b, m_cnorm_g, m_cnorm_b, m_w_out_odd, m_ln_post_odd, v_ln_pre_even, v_w_in_even, v_pool_w, v_pool_scale, v_w_out_even, v_ln_post_even, v_ln_pre_odd, v_w_in_odd, v_sconv_w, v_dconv_w, v_dconv_b, v_cnorm_g, v_cnorm_b, v_w_out_odd, v_ln_post_odd):
    xs = x[0]
    tgt = loss_target[0]
    s, d = xs.shape
    half = d // 2
    n_heads = half // HEAD_DIM
    ng = len(POOL_WINDOWS)
    cwp = half // ng
    dev = 4 * lax.axis_index("x") + 2 * lax.axis_index("y") + lax.axis_index("c")
    core = lax.axis_index("c").astype(jnp.int32).reshape(1)

    pr = pool_w.shape[2]
    cl = sconv_w.shape[2]
    small_parts = [(_rows128(ln_pre_odd), 8), (sconv_w[0], 8), (dconv_w[0], 32), (dconv_b, 8),
                   (cnorm_g, 8), (cnorm_b, 8), (_rows128(ln_post_odd), 8)]
    small_local = jnp.concatenate([_pad_rows(a, r) for a, r in small_parts], axis=0)
    g_wie, g_pw, g_small = all_gather(
        [w_in_even[0].astype(BF16), pool_w[0].reshape(ng * pr, cwp).astype(BF16), small_local], "ag_first")
    comm = _Exchanges(dev, core, d)
    in_proj_dep = comm.start_weights([w_out_even[0].astype(BF16), w_in_odd[0].astype(BF16),
                                      w_out_odd[0].astype(BF16)], after=g_wie)
    pool_full = g_pw.reshape(N_DEV, ng, pr, cwp).transpose(1, 0, 2, 3).reshape(ng, cwp, cwp)
    nl = ln_pre_odd.shape[1] // 128

    def chan(lo, rows):
        return g_small[:, lo:lo + rows].transpose(1, 0, 2).reshape(rows, N_DEV * cl)

    ln_pre_odd_f = g_small[:, 0:nl].reshape(1, d)
    sconv_f = chan(8, SCONV_K)
    dconv_f = chan(16, CONF_K)
    dconv_b_f = chan(48, 1)
    cnorm_g_f = chan(56, 1)
    cnorm_b_f = chan(64, 1)
    ln_post_odd_f = g_small[:, 72:72 + nl].reshape(1, d)

    loss_blk, grad_x, small_g = _fwd_bwd(
        xs, tgt, ln_pre_even, g_wie, pool_full, pool_scale, ln_post_even, ln_pre_odd_f,
        sconv_f, dconv_f, dconv_b_f, cnorm_g_f, cnorm_b_f, ln_post_odd_f, comm, in_proj_dep)
    loss = lax.psum(loss_blk[0, 0], ("x", "y", "c"))
    small_w = [ln_pre_even, pool_scale, ln_post_even, ln_pre_odd, sconv_w[0], dconv_w[0], dconv_b, cnorm_g, cnorm_b, ln_post_odd]
    small_m = [m_ln_pre_even, m_pool_scale, m_ln_post_even, m_ln_pre_odd, m_sconv_w[0], m_dconv_w[0], m_dconv_b, m_cnorm_g, m_cnorm_b, m_ln_post_odd]
    small_v = [v_ln_pre_even, v_pool_scale, v_ln_post_even, v_ln_pre_odd, v_sconv_w[0], v_dconv_w[0], v_dconv_b, v_cnorm_g, v_cnorm_b, v_ln_post_odd]
    sg, sd, sm, sv = _update_small(small_g, small_w, small_m, small_v, dev, d, cl)
    big = {"w_in_even": (w_in_even, m_w_in_even, v_w_in_even), "pool_w": (pool_w, m_pool_w, v_pool_w),
           "w_out_even": (w_out_even, m_w_out_even, v_w_out_even), "w_in_odd": (w_in_odd, m_w_in_odd, v_w_in_odd),
           "w_out_odd": (w_out_odd, m_w_out_odd, v_w_out_odd)}
    upd = comm.finish_updates(big, last_after=[grad_x] + sd)
    (g_wie_o, d_wie, m_wie, v_wie), (g_pw_o, d_pw, m_pw, v_pw) = upd["w_in_even"], upd["pool_w"]
    (g_woe_o, d_woe, m_woe, v_woe), (g_wio_o, d_wio, m_wio, v_wio) = upd["w_out_even"], upd["w_in_odd"]
    g_woo_o, d_woo, m_woo, v_woo = upd["w_out_odd"]

    def order(small, wie, pw, woe, wio, woo):
        return [small[0], wie, pw, small[1], woe, small[2], small[3], wio, small[4], small[5], small[6],
                small[7], small[8], woo, small[9]]

    grads = order(sg, g_wie_o, g_pw_o, g_woe_o, g_wio_o, g_woo_o)
    deltas = order(sd, d_wie, d_pw, d_woe, d_wio, d_woo)
    new_m = order(sm, m_wie, m_pw, m_woe, m_wio, m_woo)
    new_v = order(sv, v_wie, v_pw, v_woe, v_wio, v_woo)
    return (loss, grad_x[None], *grads, *deltas, *new_m, *new_v)


def _fwd_bwd(xs, tgt, ln_pre_even, g_wie, pool_full, pool_scale, ln_post_even, ln_pre_odd_f,
             sconv_f, dconv_f, dconv_b_f, cnorm_g_f, cnorm_b_f, ln_post_odd_f, comm, in_proj_dep):
    d = xs.shape[1]
    n_heads = d // 2 // HEAD_DIM
    ng, cwp = pool_full.shape[0], pool_full.shape[1]
    h0 = rms_fwd(xs, ln_pre_even, "rms_pre_even")
    p0 = mm_nn(h0, g_wie, BF16, "in_proj_even", dep=in_proj_dep)
    a0 = sb_fwd(p0, n_heads, "sb_fwd")
    y0 = even_mix_fwd(a0, p0, pool_full, pool_scale, "even_mix_fwd")
    w_out_e, g_wio, w_out_o = comm.rest_of_weights(after=y0)
    o0 = mm_nn(y0, w_out_e, F32, "out_proj_even", tn=512)
    x1 = postnorm_fwd(xs, o0, ln_post_even, "post_even")
    h1 = rms_fwd(x1, ln_pre_odd_f, "rms_pre_odd")
    p1 = mm_nn(h1, g_wio, BF16, "in_proj_odd")
    y1, dc = odd_mix_fwd(p1, sconv_f, dconv_f, dconv_b_f, cnorm_g_f, cnorm_b_f, "odd_mix_fwd")
    o1 = mm_nn(y1, w_out_o, F32, "out_proj_odd", tn=512)
    loss_blk, gx2, do1, dg_post_odd = final_fwd_bwd(x1, o1, ln_post_odd_f, tgt, "post_odd_loss")

    dw_out_o = mm_tn(y1, do1, 1, BF16, "dw_out_odd", tk=512)
    dy1 = mm_nt(do1, w_out_o, BF16, "dy_odd", tn=512)
    ddc, dg2, dgam, dbet = odd_bwd_ln(dy1, p1, dc, cnorm_g_f, cnorm_b_f, "odd_bwd_ln")
    dp1, dsconv, ddconv, ddconv_b = odd_bwd_conv(dy1, p1, ddc, dg2, sconv_f, dconv_f, "odd_bwd_conv")
    dw_in_o = mm_tn(h1, dp1, N_DEV, BF16, "dw_in_odd")
    dep = comm.reduce({"w_out_odd": dw_out_o.reshape(N_DEV, d // N_DEV, d), "w_in_odd": dw_in_o}, "odd")
    dh1 = mm_nt(dp1, g_wio, F32, "dh_odd", dep=dep)
    gx1, dg_pre_odd = norm_bwd(dh1, x1, ln_pre_odd_f, gx2, F32, "pre_odd_bwd")

    do0, dg_post_even = norm_bwd(gx1, o0, ln_post_even, None, BF16, "post_even_bwd")
    dw_out_e = mm_tn(y0, do0, 1, BF16, "dw_out_even", tk=512)
    dy0 = mm_nt(do0, w_out_e, BF16, "dy_even", tn=512)
    da0, du0, dg0, dpool, dpool_scale = even_mix_bwd(dy0, a0, p0, pool_full, pool_scale, "even_mix_bwd")
    pr = cwp // N_DEV
    dpool_slabs = dpool.astype(BF16).reshape(ng, N_DEV, pr, cwp).transpose(1, 0, 2, 3).reshape(N_DEV, ng * pr, cwp)
    dep = comm.reduce({"w_out_even": dw_out_e.reshape(N_DEV, d // N_DEV, d), "pool_w": dpool_slabs}, "even_out")
    dq0, dk0, dv0 = sb_bwd(p0, a0, da0, n_heads, "sb_bwd", dep=dep)
    dp0 = jnp.concatenate([dq0, dk0, dv0, du0, dg0], axis=1)
    dw_in_e = mm_tn(h0, dp0, N_DEV, BF16, "dw_in_even")
    dep = comm.reduce({"w_in_even": dw_in_e}, "even_in")
    dh0 = mm_nt(dp0, g_wie, F32, "dh_even", dep=dep)
    grad_x, dg_pre_even = norm_bwd(dh0, xs, ln_pre_even, gx1, F32, "pre_even_bwd")
    small_g = [dg_pre_even, dpool_scale, dg_post_even, dg_pre_odd, dsconv, ddconv, ddconv_b, dgam, dbet, dg_post_odd]
    return loss_blk, grad_x, small_g


class _Exchanges:
    def __init__(self, dev, core, d):
        self.core = core
        self.chip = (dev // 2).astype(jnp.int32).reshape(1)
        self.d = d
        self.weights = None
        self.pending = []

    def start_weights(self, blocks, after):
        lands = [lax.empty((N_DEV,) + b.shape, b.dtype) for b in blocks]
        send, recv, srcs, lands, token = chips_start("gather", blocks, lands, [after], "ag_rest_start")
        self.weights = (send, recv, srcs, lands)
        return token

    def rest_of_weights(self, after):
        send, recv, srcs, lands = self.weights
        srcs, lands = chips_wait("gather", send, recv, srcs, lands, [after], "ag_rest_wait")
        w_out_e, g_wio, w_out_o = gather_finish(srcs, lands, "ag_rest_finish")
        return w_out_e.reshape(1, self.d, self.d), g_wio, w_out_o.reshape(1, self.d, self.d)

    def reduce(self, partials, tag):
        names = list(partials)
        arrs = [partials[k].reshape((4, 2) + partials[k].shape[1:]) for k in names]
        from_sibling = sibling_exchange(arrs, "rs_sibling_" + tag)
        sums = [pair_add(o, r, self.core, "rs_pair_add_" + k) for k, o, r in zip(names, arrs, from_sibling)]
        lands = [lax.empty(a.shape, a.dtype) for a in sums]
        send, recv, srcs, lands, token = chips_start("scatter", sums, lands, [], "rs_start_" + tag)
        self.pending.append((tag, names, send, recv, srcs, lands))
        return token

    def finish_updates(self, big, last_after):
        out = {}
        done = []
        for i, (tag, names, send, recv, srcs, lands) in enumerate(self.pending):
            afters = list(last_after) + done if i == len(self.pending) - 1 else []
            srcs, lands = chips_wait("scatter", send, recv, srcs, lands, afters, "rs_wait_" + tag)
            for name, own, got in zip(names, srcs, lands):
                w, m, v = big[name]
                shp = own.shape[1:]
                outs = adamw_big(w.reshape(shp), m.reshape(shp), v.reshape(shp), own, got, self.chip, "adamw_" + name)
                out[name] = [o.reshape(w.shape) for o in outs]
                done.append(outs[1])
        return out


def _update_small(small_g, small_w, small_m, small_v, dev, d, cl):
    packed = jnp.concatenate([_rows128(g) for g in small_g], axis=0)
    (g8,) = all_gather([packed], "ag_small_grads")
    tot = sum_devices(g8, "sum_small_grads")
    full_g = []
    lo = 0
    for g in small_g:
        rows = g.size // 128
        full_g.append(tot[lo:lo + rows].reshape(g.shape))
        lo += rows

    def mine(g, width):
        return lax.dynamic_slice_in_dim(g, dev * width, width, axis=g.ndim - 1)

    fg = full_g
    small_gl = [fg[0], fg[1], fg[2], mine(fg[3], d // N_DEV), mine(fg[4], cl), mine(fg[5], cl), mine(fg[6], cl),
                mine(fg[7], cl), mine(fg[8], cl), mine(fg[9], d // N_DEV)]
    sd, sm, sv = adamw_small(small_w, small_gl, small_m, small_v, "adamw_small")

    def like(k, a):
        return a[None] if k in (4, 5) else a

    sg = [like(k, a) for k, a in enumerate(small_gl)]
    sd = [like(k, a) for k, a in enumerate(sd)]
    sm = [like(k, a) for k, a in enumerate(sm)]
    sv = [like(k, a) for k, a in enumerate(sv)]
    return sg, sd, sm, sv
```

```python
import functools
import math

import jax
import jax.numpy as jnp
from jax import lax
from jax.experimental import pallas as pl
from jax.experimental.pallas import tpu as pltpu

F32 = jnp.float32
BF16 = jnp.bfloat16
EPS = 1e-6
HEAD_DIM = 128
POOL_WINDOWS = (2, 4, 8, 16)
SCONV_K = 3
CONF_K = 31
HALO = 32
N_DEV = 8
VMEM_LIMIT = 56 * 1024 * 1024
MESH = pl.DeviceIdType.MESH

ADAM_LR = 0.001
ADAM_B1 = 0.9
ADAM_B2 = 0.999
ADAM_EPS = 1e-08
ADAM_WD = 0.01
ADAM_STEP = 10


def _params(*sem):
    return pltpu.CompilerParams(dimension_semantics=sem, vmem_limit_bytes=VMEM_LIMIT)


def _sigmoid(v):
    return 1.0 / (1.0 + jnp.exp(-v))


def _silu(v):
    return v * _sigmoid(v)


def _silu_and_grad(v):
    s = _sigmoid(v)
    return v * s, s * (1.0 + v * (1.0 - s))


def _rowsum8(v):
    r, c = v.shape
    return jnp.sum(v.reshape(r // 8, 8, c), axis=0)


def _tap_before(xx, i, rows):
    if i == 0:
        return xx[HALO:HALO + rows]
    return pltpu.roll(xx, i, 0)[HALO:HALO + rows]


def _tap_after(xx, i, rows):
    if i == 0:
        return xx[0:rows]
    return pltpu.roll(xx, xx.shape[0] - i, 0)[0:rows]


def rms_fwd(x, g, name, tm=256):
    s, d = x.shape

    def body(x_ref, g_ref, h_ref):
        xv = x_ref[...]
        r = lax.rsqrt(jnp.mean(xv * xv, axis=-1, keepdims=True) + EPS)
        h_ref[...] = (xv * r * g_ref[...]).astype(BF16)

    return pl.pallas_call(
        body, name=name, grid=(s // tm,),
        in_specs=[pl.BlockSpec((tm, d), lambda i: (i, 0)), pl.BlockSpec((1, d), lambda i: (0, 0))],
        out_specs=pl.BlockSpec((tm, d), lambda i: (i, 0)),
        out_shape=jax.ShapeDtypeStruct((s, d), BF16),
        compiler_params=_params("parallel"),
    )(x, g)


def postnorm_fwd(x, o, g, name, tm=256):
    s, d = x.shape

    def body(x_ref, o_ref, g_ref, y_ref):
        ov = o_ref[...]
        r = lax.rsqrt(jnp.mean(ov * ov, axis=-1, keepdims=True) + EPS)
        y_ref[...] = x_ref[...] + ov * r * g_ref[...]

    return pl.pallas_call(
        body, name=name, grid=(s // tm,),
        in_specs=[pl.BlockSpec((tm, d), lambda i: (i, 0)), pl.BlockSpec((tm, d), lambda i: (i, 0)),
                  pl.BlockSpec((1, d), lambda i: (0, 0))],
        out_specs=pl.BlockSpec((tm, d), lambda i: (i, 0)),
        out_shape=jax.ShapeDtypeStruct((s, d), F32),
        compiler_params=_params("parallel"),
    )(x, o, g)


def final_fwd_bwd(x1, o, g, target, name, tm=256):
    s, d = x1.shape
    n = s // tm

    def body(x_ref, o_ref, g_ref, t_ref, loss_ref, gx_ref, do_ref, dg_ref, lacc, gacc):
        i = pl.program_id(0)

        @pl.when(i == 0)
        def _():
            lacc[...] = jnp.zeros_like(lacc)
            gacc[...] = jnp.zeros_like(gacc)

        ov = o_ref[...]
        gv = g_ref[...]
        r = lax.rsqrt(jnp.mean(ov * ov, axis=-1, keepdims=True) + EPS)
        oh = ov * r
        diff = x_ref[...] + oh * gv - t_ref[...]
        lacc[...] += _rowsum8(diff * diff)
        gx = diff * (1.0 / d)
        gx_ref[...] = gx
        gacc[...] += _rowsum8(gx * oh)
        dn = gx * gv
        do_ref[...] = (r * (dn - oh * jnp.mean(dn * oh, axis=-1, keepdims=True))).astype(BF16)

        @pl.when(i == n - 1)
        def _():
            tot = jnp.sum(jnp.sum(lacc[...], axis=0, keepdims=True), axis=1, keepdims=True)
            loss_ref[...] = jnp.broadcast_to(tot * (0.5 / d), loss_ref.shape)
            dg_ref[...] = jnp.sum(gacc[...], axis=0, keepdims=True)

    row = pl.BlockSpec((tm, d), lambda i: (i, 0))
    vec = pl.BlockSpec((1, d), lambda i: (0, 0))
    return pl.pallas_call(
        body, name=name, grid=(n,),
        in_specs=[row, row, vec, row],
        out_specs=[pl.BlockSpec((8, 128), lambda i: (0, 0)), row, row, vec],
        out_shape=[jax.ShapeDtypeStruct((8, 128), F32), jax.ShapeDtypeStruct((s, d), F32),
                   jax.ShapeDtypeStruct((s, d), BF16), jax.ShapeDtypeStruct((1, d), F32)],
        scratch_shapes=[pltpu.VMEM((8, d), F32), pltpu.VMEM((8, d), F32)],
        compiler_params=_params("arbitrary"),
    )(x1, o, g, target)


def norm_bwd(dy, inp, g, resid, out_dtype, name, tm=256, dep=None):
    s, d = inp.shape
    n = s // tm
    has_resid = resid is not None

    def body(*refs):
        dy_ref, x_ref, g_ref = refs[:3]
        r_ref = refs[3] if has_resid else None
        dx_ref, dg_ref, gacc = refs[-3:]
        i = pl.program_id(0)

        @pl.when(i == 0)
        def _():
            gacc[...] = jnp.zeros_like(gacc)

        xv = x_ref[...]
        dyv = dy_ref[...].astype(F32)
        r = lax.rsqrt(jnp.mean(xv * xv, axis=-1, keepdims=True) + EPS)
        xh = xv * r
        gacc[...] += _rowsum8(dyv * xh)
        dn = dyv * g_ref[...]
        dx = r * (dn - xh * jnp.mean(dn * xh, axis=-1, keepdims=True))
        if has_resid:
            dx = dx + r_ref[...]
        dx_ref[...] = dx.astype(out_dtype)

        @pl.when(i == n - 1)
        def _():
            dg_ref[...] = jnp.sum(gacc[...], axis=0, keepdims=True)

    row = pl.BlockSpec((tm, d), lambda i: (i, 0))
    vec = pl.BlockSpec((1, d), lambda i: (0, 0))
    dep_args, dep_specs = _after(dep)
    args = [dy, inp, g] + ([resid] if has_resid else []) + dep_args
    return pl.pallas_call(
        body, name=name, grid=(n,),
        in_specs=[row, row, vec] + ([row] if has_resid else []) + dep_specs,
        out_specs=[row, vec],
        out_shape=[jax.ShapeDtypeStruct((s, d), out_dtype), jax.ShapeDtypeStruct((1, d), F32)],
        scratch_shapes=[pltpu.VMEM((8, d), F32)],
        compiler_params=_params("arbitrary"),
    )(*args)


def _after(dep):
    if dep is None:
        return [], []
    return [dep], [pl.BlockSpec((8, 128), lambda *_: (0, 0))]


def mm_nn(a, w, out_dtype, name, tm=512, tn=None, dep=None):
    m, k = a.shape
    ns, _, n = w.shape
    tn = n if tn is None else tn
    nj = n // tn
    dep_args, dep_specs = _after(dep)

    def body(a_ref, w_ref, *rest):
        o_ref = rest[-1]
        o_ref[...] = jnp.dot(a_ref[...], w_ref[0], preferred_element_type=F32).astype(out_dtype)

    return pl.pallas_call(
        body, name=name, grid=(ns, nj, m // tm),
        in_specs=[pl.BlockSpec((tm, k), lambda s, j, i: (i, 0)),
                  pl.BlockSpec((1, k, tn), lambda s, j, i: (s, 0, j))] + dep_specs,
        out_specs=pl.BlockSpec((tm, tn), lambda s, j, i: (i, s * nj + j)),
        out_shape=jax.ShapeDtypeStruct((m, ns * n), out_dtype),
        compiler_params=_params("parallel", "parallel", "parallel"),
    )(a, w, *dep_args)


def mm_nt(a, w, out_dtype, name, tm=512, tn=None, dep=None):
    m = a.shape[0]
    ns, k, n = w.shape
    tn = n if tn is None else tn
    nj = n // tn
    steps = ns * nj
    dep_args, dep_specs = _after(dep)

    def body(a_ref, w_ref, *rest):
        o_ref, acc = rest[-2:]
        r = pl.program_id(1)

        @pl.when(r == 0)
        def _():
            acc[...] = jnp.zeros_like(acc)

        acc[...] += lax.dot_general(a_ref[...], w_ref[0], (((1,), (1,)), ((), ())),
                                    preferred_element_type=F32)

        @pl.when(r == steps - 1)
        def _():
            o_ref[...] = acc[...].astype(out_dtype)

    return pl.pallas_call(
        body, name=name, grid=(m // tm, steps),
        in_specs=[pl.BlockSpec((tm, tn), lambda i, r: (i, r)),
                  pl.BlockSpec((1, k, tn), lambda i, r: (r // nj, 0, r % nj))] + dep_specs,
        out_specs=pl.BlockSpec((tm, k), lambda i, r: (i, 0)),
        out_shape=jax.ShapeDtypeStruct((m, k), out_dtype),
        scratch_shapes=[pltpu.VMEM((tm, k), F32)],
        compiler_params=_params("parallel", "arbitrary"),
    )(a, w, *dep_args)


def mm_tn(a, b, ns, out_dtype, name, tk=1024, tm=512, dep=None):
    m, k = a.shape
    n = b.shape[1] // ns
    steps = m // tm
    dep_args, dep_specs = _after(dep)

    def body(a_ref, b_ref, *rest):
        o_ref, acc = rest[-2:]
        r = pl.program_id(2)

        @pl.when(r == 0)
        def _():
            acc[...] = jnp.zeros_like(acc)

        acc[...] += lax.dot_general(a_ref[...], b_ref[...], (((0,), (0,)), ((), ())),
                                    preferred_element_type=F32)

        @pl.when(r == steps - 1)
        def _():
            o_ref[0] = acc[...].astype(out_dtype)

    return pl.pallas_call(
        body, name=name, grid=(ns, k // tk, steps),
        in_specs=[pl.BlockSpec((tm, tk), lambda s, j, r: (r, j)),
                  pl.BlockSpec((tm, n), lambda s, j, r: (r, s))] + dep_specs,
        out_specs=pl.BlockSpec((1, tk, n), lambda s, j, r: (s, j, 0)),
        out_shape=jax.ShapeDtypeStruct((ns, k, n), out_dtype),
        scratch_shapes=[pltpu.VMEM((tk, n), F32)],
        compiler_params=_params("parallel", "parallel", "arbitrary"),
    )(a, b, *dep_args)


SB_BLK = 128


def _split_dot(v, tri):
    hi = v.astype(BF16)
    lo = (v - hi.astype(F32)).astype(BF16)
    return jnp.dot(hi, tri, preferred_element_type=F32) + jnp.dot(lo, tri, preferred_element_type=F32)


def _sb_scores(z, lim, dcol, tri_ex):
    mask = dcol < lim
    sp = jnp.log(1.0 + jnp.exp(-jnp.abs(z)))
    lb = jnp.minimum(z, 0.0) - sp
    l1m = jnp.where(mask, lb - z, 0.0)
    return mask, lb, l1m, _split_dot(l1m, tri_ex)


def _sb_consts():
    row = lax.broadcasted_iota(jnp.int32, (SB_BLK, SB_BLK), 0)
    col = lax.broadcasted_iota(jnp.int32, (SB_BLK, SB_BLK), 1)
    tri_ex = jnp.where(row > col, 1.0, 0.0).astype(BF16)
    tri_in = jnp.where(row >= col, 1.0, 0.0).astype(BF16)
    return col - row, tri_ex, tri_in


def sb_fwd(p, n_heads, name, tq=256, nsub=4):
    s = p.shape[0]
    h_n = n_heads
    b = SB_BLK
    nqs = tq // b
    tk = nsub * b
    scale = 1.0 / math.sqrt(HEAD_DIM)

    def body(q_ref, k_ref, v_ref, o_ref):
        qi = pl.program_id(1)
        dcol, tri_ex, _ = _sb_consts()
        qv = [q_ref[qs * b:(qs + 1) * b, :] for qs in range(nqs)]
        n_groups = ((qi + 1) * nqs - 1) // nsub + 1

        def step(it, carry):
            c1s, accs = carry
            g = n_groups - 1 - it
            off = pl.multiple_of(g * tk, tk)
            kg = k_ref[pl.ds(off, tk), :]
            vg = v_ref[pl.ds(off, tk), :]
            new_c1, new_acc = [], []
            for qs in range(nqs):
                qb = qi * nqs + qs
                z = lax.dot_general(qv[qs], kg, (((1,), (1,)), ((), ())), preferred_element_type=F32) * scale
                blocks = [_sb_scores(z[:, j * b:(j + 1) * b], (qb - (g * nsub + j)) * b, dcol, tri_ex)
                          for j in range(nsub)]
                run = c1s[qs]
                ws = [None] * nsub
                for j in reversed(range(nsub)):
                    mask, lb, l1m, ls_loc = blocks[j]
                    ws[j] = jnp.where(mask, jnp.exp(lb + ls_loc + run), 0.0).astype(BF16)
                    run = run + jnp.sum(l1m, axis=1, keepdims=True)
                w = jnp.concatenate(ws, axis=1)
                new_acc.append(accs[qs] + jnp.dot(w, vg, preferred_element_type=F32))
                new_c1.append(run)
            return tuple(new_c1), tuple(new_acc)

        init = (tuple(jnp.zeros((b, 1), F32) for _ in range(nqs)),
                tuple(jnp.zeros((b, HEAD_DIM), F32) for _ in range(nqs)))
        _, accs = lax.fori_loop(0, n_groups, step, init)
        for qs in range(nqs):
            o_ref[qs * b:(qs + 1) * b, :] = accs[qs]

    return pl.pallas_call(
        body, name=name, grid=(h_n, s // tq),
        in_specs=[pl.BlockSpec((tq, HEAD_DIM), lambda h, i: (i, h)),
                  pl.BlockSpec((s, HEAD_DIM), lambda h, i: (0, h_n + h)),
                  pl.BlockSpec((s, HEAD_DIM), lambda h, i: (0, 2 * h_n + h))],
        out_specs=pl.BlockSpec((tq, HEAD_DIM), lambda h, i: (i, h)),
        out_shape=jax.ShapeDtypeStruct((s, h_n * HEAD_DIM), F32),
        compiler_params=_params("parallel", "arbitrary"),
    )(p, p, p)


def sb_bwd(p, a, da, n_heads, name, tq=256, nsub=4, dep=None):
    s = p.shape[0]
    h_n = n_heads
    nq = s // tq
    b = SB_BLK
    nqs = tq // b
    tk = nsub * b
    scale = 1.0 / math.sqrt(HEAD_DIM)
    dep_args, dep_specs = _after(dep)

    def body(q_ref, k_ref, v_ref, a_ref, da_ref, *rest):
        dq_ref, dk_ref, dv_ref, dk_acc, dv_acc = rest[-5:]
        qi = pl.program_id(1)

        @pl.when(qi == 0)
        def _():
            dk_acc[...] = jnp.zeros_like(dk_acc)
            dv_acc[...] = jnp.zeros_like(dv_acc)

        dcol, tri_ex, tri_in = _sb_consts()
        q_all = q_ref[...]
        do_all = da_ref[...]
        qv = [q_ref[qs * b:(qs + 1) * b, :] for qs in range(nqs)]
        dov = [da_ref[qs * b:(qs + 1) * b, :] for qs in range(nqs)]
        tots = [jnp.sum(dov[qs].astype(F32) * a_ref[qs * b:(qs + 1) * b, :], axis=1, keepdims=True)
                for qs in range(nqs)]
        n_groups = ((qi + 1) * nqs - 1) // nsub + 1

        def step(it, carry):
            c1s, c2s, dqs = carry
            g = n_groups - 1 - it
            off = pl.multiple_of(g * tk, tk)
            kg = k_ref[pl.ds(off, tk), :]
            vg = v_ref[pl.ds(off, tk), :]
            new_c1, new_c2, new_dq, dz_rows, wr_rows = [], [], [], [], []
            for qs in range(nqs):
                qb = qi * nqs + qs
                z = lax.dot_general(qv[qs], kg, (((1,), (1,)), ((), ())), preferred_element_type=F32) * scale
                dw = lax.dot_general(dov[qs], vg, (((1,), (1,)), ((), ())), preferred_element_type=F32)
                blocks = [_sb_scores(z[:, j * b:(j + 1) * b], (qb - (g * nsub + j)) * b, dcol, tri_ex)
                          for j in range(nsub)]
                run1, run2 = c1s[qs], c2s[qs]
                dzs, wrs = [None] * nsub, [None] * nsub
                for j in reversed(range(nsub)):
                    mask, lb, l1m, ls_loc = blocks[j]
                    wr = jnp.where(mask, jnp.exp(lb + ls_loc + run1), 0.0).astype(BF16)
                    e = dw[:, j * b:(j + 1) * b] * wr.astype(F32)
                    later = _split_dot(e, tri_in) + run2
                    beta = jnp.exp(lb)
                    dz = jnp.where(mask, e * (1.0 - beta) - beta * (tots[qs] - later), 0.0) * scale
                    dzs[j] = dz.astype(BF16)
                    wrs[j] = wr
                    run1 = run1 + jnp.sum(l1m, axis=1, keepdims=True)
                    run2 = run2 + jnp.sum(e, axis=1, keepdims=True)
                dzq = jnp.concatenate(dzs, axis=1)
                new_dq.append(dqs[qs] + jnp.dot(dzq, kg, preferred_element_type=F32))
                new_c1.append(run1)
                new_c2.append(run2)
                dz_rows.append(dzq)
                wr_rows.append(jnp.concatenate(wrs, axis=1))
            dz_all = jnp.concatenate(dz_rows, axis=0)
            wr_all = jnp.concatenate(wr_rows, axis=0)
            dk_acc[pl.ds(off, tk), :] += lax.dot_general(dz_all, q_all, (((0,), (0,)), ((), ())),
                                                         preferred_element_type=F32)
            dv_acc[pl.ds(off, tk), :] += lax.dot_general(wr_all, do_all, (((0,), (0,)), ((), ())),
                                                         preferred_element_type=F32)
            return tuple(new_c1), tuple(new_c2), tuple(new_dq)

        zeros = tuple(jnp.zeros((b, 1), F32) for _ in range(nqs))
        _, _, dqs = lax.fori_loop(0, n_groups, step,
                                  (zeros, zeros, tuple(jnp.zeros((b, HEAD_DIM), F32) for _ in range(nqs))))
        for qs in range(nqs):
            dq_ref[qs * b:(qs + 1) * b, :] = dqs[qs].astype(BF16)

        @pl.when(qi == nq - 1)
        def _():
            dk_ref[...] = dk_acc[...].astype(BF16)
            dv_ref[...] = dv_acc[...].astype(BF16)

    blk = pl.BlockSpec((tq, HEAD_DIM), lambda h, i: (i, h))
    full = pl.BlockSpec((s, HEAD_DIM), lambda h, i: (0, h))
    return pl.pallas_call(
        body, name=name, grid=(h_n, nq),
        in_specs=[blk, pl.BlockSpec((s, HEAD_DIM), lambda h, i: (0, h_n + h)),
                  pl.BlockSpec((s, HEAD_DIM), lambda h, i: (0, 2 * h_n + h)), blk, blk] + dep_specs,
        out_specs=[blk, full, full],
        out_shape=[jax.ShapeDtypeStruct((s, h_n * HEAD_DIM), BF16)] * 3,
        scratch_shapes=[pltpu.VMEM((s, HEAD_DIM), F32), pltpu.VMEM((s, HEAD_DIM), F32)],
        compiler_params=_params("parallel", "arbitrary"),
    )(p, p, p, a, da, *dep_args)


def _pool_window(xx, win, r0, rc):
    cur = xx[HALO:HALO + rc]
    ws = cur
    for i in range(1, win):
        ws = ws + _tap_before(xx, i, rc)
    t_idx = r0 + lax.broadcasted_iota(jnp.int32, (rc, 1), 0)
    inv = 1.0 / jnp.minimum(win, t_idx + 1).astype(F32)
    return ws * inv - cur, inv


def even_mix_fwd(a, p, pool_w, pool_scale, name, rc=64):
    s = p.shape[0]
    ng = len(POOL_WINDOWS)
    cw = pool_w.shape[1]
    n_chunks = s // rc

    def body(a_ref, u_ref, g_ref, w_ref, sc_ref, y_ref, upad):
        j = pl.program_id(0)

        @pl.when(j < ng)
        def _():
            def chunk(ci, carry):
                rows = pl.ds(pl.multiple_of(ci * rc, rc), rc)
                y_ref[rows, :] = (a_ref[rows, :] * _silu(g_ref[rows, :].astype(F32))).astype(BF16)
                return carry

            lax.fori_loop(0, n_chunks, chunk, 0)

        for gi, win in enumerate(POOL_WINDOWS):
            @pl.when(j == ng + gi)
            def _(win=win):
                upad[0:HALO, :] = jnp.zeros((HALO, cw), F32)

                def fill(ci, carry):
                    r0 = pl.multiple_of(ci * rc, rc)
                    upad[pl.ds(pl.multiple_of(r0 + HALO, HALO), rc), :] = u_ref[pl.ds(r0, rc), :].astype(F32)
                    return carry

                lax.fori_loop(0, n_chunks, fill, 0)

                def chunk(ci, carry):
                    r0 = pl.multiple_of(ci * rc, rc)
                    rows = pl.ds(r0, rc)
                    pooled, _ = _pool_window(upad[pl.ds(r0, HALO + rc), :], win, r0, rc)
                    t = jnp.dot(pooled.astype(BF16), w_ref[0], preferred_element_type=F32)
                    y_ref[rows, :] = (t * sc_ref[...] * _silu(g_ref[rows, :].astype(F32))).astype(BF16)
                    return carry

                lax.fori_loop(0, n_chunks, chunk, 0)

    grp = lambda j: jnp.maximum(j - ng, 0)
    return pl.pallas_call(
        body, name=name, grid=(2 * ng,),
        in_specs=[pl.BlockSpec((s, cw), lambda j: (0, jnp.minimum(j, ng - 1))),
                  pl.BlockSpec((s, cw), lambda j: (0, 3 * ng + grp(j))),
                  pl.BlockSpec((s, cw), lambda j: (0, 4 * ng + j)),
                  pl.BlockSpec((1, cw, cw), lambda j: (grp(j), 0, 0)),
                  pl.BlockSpec((1, cw), lambda j: (0, grp(j)))],
        out_specs=pl.BlockSpec((s, cw), lambda j: (0, j)),
        out_shape=jax.ShapeDtypeStruct((s, 2 * ng * cw), BF16),
        scratch_shapes=[pltpu.VMEM((HALO + s, cw), F32)],
        compiler_params=_params("arbitrary"),
    )(a, p, p, pool_w, pool_scale)


def even_mix_bwd(dy, a, p, pool_w, pool_scale, name, rc=64):
    s = p.shape[0]
    ng = len(POOL_WINDOWS)
    cw = pool_w.shape[1]
    n_chunks = s // rc

    def body(dy_ref, a_ref, u_ref, g_ref, w_ref, sc_ref, da_ref, du_ref, dg_ref, dw_ref, dsc_ref,
             upad, rpad, dpl, dw_acc, dsc_acc):
        j = pl.program_id(0)

        @pl.when(j < ng)
        def _():
            def chunk(ci, carry):
                rows = pl.ds(pl.multiple_of(ci * rc, rc), rc)
                dyv = dy_ref[rows, :].astype(F32)
                sg, dsg = _silu_and_grad(g_ref[rows, :].astype(F32))
                da_ref[rows, :] = (dyv * sg).astype(BF16)
                dg_ref[rows, :] = (dyv * a_ref[rows, :] * dsg).astype(BF16)
                return carry

            lax.fori_loop(0, n_chunks, chunk, 0)

        for gi, win in enumerate(POOL_WINDOWS):
            @pl.when(j == ng + gi)
            def _(win=win):
                upad[0:HALO, :] = jnp.zeros((HALO, cw), F32)
                rpad[s:s + HALO, :] = jnp.zeros((HALO, cw), F32)
                dw_acc[...] = jnp.zeros_like(dw_acc)
                dsc_acc[...] = jnp.zeros_like(dsc_acc)

                def fill(ci, carry):
                    r0 = pl.multiple_of(ci * rc, rc)
                    upad[pl.ds(pl.multiple_of(r0 + HALO, HALO), rc), :] = u_ref[pl.ds(r0, rc), :].astype(F32)
                    return carry

                lax.fori_loop(0, n_chunks, fill, 0)

                def chunk(ci, carry):
                    r0 = pl.multiple_of(ci * rc, rc)
                    rows = pl.ds(r0, rc)
                    pooled, inv = _pool_window(upad[pl.ds(r0, HALO + rc), :], win, r0, rc)
                    pb = pooled.astype(BF16)
                    wv = w_ref[0]
                    t = jnp.dot(pb, wv, preferred_element_type=F32)
                    scv = sc_ref[...]
                    dyv = dy_ref[rows, :].astype(F32)
                    sg, dsg = _silu_and_grad(g_ref[rows, :].astype(F32))
                    dpo = dyv * sg
                    dg_ref[rows, :] = (dyv * t * scv * dsg).astype(BF16)
                    dsc_acc[...] += _rowsum8(dpo * t)
                    dtb = (dpo * scv).astype(BF16)
                    dw_acc[...] += lax.dot_general(pb, dtb, (((0,), (0,)), ((), ())),
                                                   preferred_element_type=F32)
                    dpooled = lax.dot_general(dtb, wv, (((1,), (1,)), ((), ())),
                                              preferred_element_type=F32)
                    dpl[rows, :] = dpooled
                    rpad[rows, :] = dpooled * inv
                    return carry

                lax.fori_loop(0, n_chunks, chunk, 0)

                def chunk2(ci, carry):
                    r0 = pl.multiple_of(ci * rc, rc)
                    rows = pl.ds(r0, rc)
                    xx = rpad[pl.ds(r0, rc + HALO), :]
                    fs = xx[0:rc]
                    for i in range(1, win):
                        fs = fs + _tap_after(xx, i, rc)
                    du_ref[rows, :] = (fs - dpl[rows, :]).astype(BF16)
                    return carry

                lax.fori_loop(0, n_chunks, chunk2, 0)
                dw_ref[0] = dw_acc[...]
                dsc_ref[...] = jnp.sum(dsc_acc[...], axis=0, keepdims=True)

    grp = lambda j: jnp.maximum(j - ng, 0)
    att = lambda j: jnp.minimum(j, ng - 1)
    return pl.pallas_call(
        body, name=name, grid=(2 * ng,),
        in_specs=[pl.BlockSpec((s, cw), lambda j: (0, j)),
                  pl.BlockSpec((s, cw), lambda j: (0, att(j))),
                  pl.BlockSpec((s, cw), lambda j: (0, 3 * ng + grp(j))),
                  pl.BlockSpec((s, cw), lambda j: (0, 4 * ng + j)),
                  pl.BlockSpec((1, cw, cw), lambda j: (grp(j), 0, 0)),
                  pl.BlockSpec((1, cw), lambda j: (0, grp(j)))],
        out_specs=[pl.BlockSpec((s, cw), lambda j: (0, att(j))),
                   pl.BlockSpec((s, cw), lambda j: (0, grp(j))),
                   pl.BlockSpec((s, cw), lambda j: (0, j)),
                   pl.BlockSpec((1, cw, cw), lambda j: (grp(j), 0, 0)),
                   pl.BlockSpec((1, cw), lambda j: (0, grp(j)))],
        out_shape=[jax.ShapeDtypeStruct((s, ng * cw), BF16), jax.ShapeDtypeStruct((s, ng * cw), BF16),
                   jax.ShapeDtypeStruct((s, 2 * ng * cw), BF16),
                   jax.ShapeDtypeStruct((ng, cw, cw), F32), jax.ShapeDtypeStruct((1, ng * cw), F32)],
        scratch_shapes=[pltpu.VMEM((HALO + s, cw), F32), pltpu.VMEM((s + HALO, cw), F32),
                        pltpu.VMEM((s, cw), F32), pltpu.VMEM((cw, cw), F32), pltpu.VMEM((8, cw), F32)],
        compiler_params=_params("arbitrary"),
    )(dy, a, p, p, pool_w, pool_scale)


def _halo_before(tm):
    return lambda i: jnp.maximum(i * (tm // HALO) - 1, 0)


def _halo_after(tm, s):
    return lambda i: jnp.minimum((i + 1) * (tm // HALO), s // HALO - 1)


def odd_mix_fwd(p, sconv_w, dconv_w, dconv_b, cnorm_g, cnorm_b, name, tm=128):
    s = p.shape[0]
    cw = sconv_w.shape[1]
    n = s // tm
    lanes = 128
    hb = _halo_before(tm)

    def body(hc_ref, hch_ref, bc_ref, cc_ref, cch_ref, ga_ref, gah_ref, gb_ref, gbh_ref, g1_ref, g2_ref,
             sw_ref, dw_ref, db_ref, gam_ref, bet_ref, y_ref, dc_ref):
        first = pl.program_id(0) == 0
        for l in range(cw // lanes):
            cols = slice(l * lanes, (l + 1) * lanes)
            mh = jnp.where(first, 0.0, cch_ref[:, cols].astype(F32) * hch_ref[:, cols].astype(F32))
            mm = cc_ref[:, cols].astype(F32) * hc_ref[:, cols].astype(F32)
            xx = jnp.concatenate([mh, mm], axis=0)
            cv = jnp.zeros((tm, lanes), F32)
            for k in range(SCONV_K):
                cv = cv + sw_ref[k:k + 1, cols] * _tap_before(xx, SCONV_K - 1 - k, tm)
            c_out = bc_ref[:, cols].astype(F32) * cv
            y_ref[:, cols] = (c_out * _silu(g1_ref[:, cols].astype(F32))).astype(BF16)
            dh = jnp.where(first, 0.0, gah_ref[:, cols].astype(F32) * _sigmoid(gbh_ref[:, cols].astype(F32)))
            dm = ga_ref[:, cols].astype(F32) * _sigmoid(gb_ref[:, cols].astype(F32))
            xx = jnp.concatenate([dh, dm], axis=0)
            acc = jnp.zeros((tm, lanes), F32) + db_ref[:, cols]
            for k in range(CONF_K):
                acc = acc + dw_ref[k:k + 1, cols] * _tap_before(xx, CONF_K - 1 - k, tm)
            dc_ref[:, cols] = acc
        rs = 32
        for r in range(tm // rs):
            rows = slice(r * rs, (r + 1) * rs)
            xv = dc_ref[rows, :]
            mu = jnp.mean(xv, axis=-1, keepdims=True)
            xc = xv - mu
            rstd = lax.rsqrt(jnp.mean(xc * xc, axis=-1, keepdims=True) + EPS)
            ln = xc * rstd * gam_ref[...] + bet_ref[...]
            y_ref[rows, cw:2 * cw] = (_silu(ln) * _silu(g2_ref[rows, :].astype(F32))).astype(BF16)

    main = lambda c: pl.BlockSpec((tm, cw), lambda i: (i, c))
    halo = lambda c: pl.BlockSpec((HALO, cw), lambda i: (hb(i), c))
    vec = lambda r: pl.BlockSpec((r, cw), lambda i: (0, 0))
    return pl.pallas_call(
        body, name=name, grid=(n,),
        in_specs=[main(0), halo(0), main(1), main(2), halo(2), main(3), halo(3), main(4), halo(4),
                  main(5), main(6), vec(SCONV_K), vec(CONF_K), vec(1), vec(1), vec(1)],
        out_specs=[pl.BlockSpec((tm, 2 * cw), lambda i: (i, 0)), pl.BlockSpec((tm, cw), lambda i: (i, 0))],
        out_shape=[jax.ShapeDtypeStruct((s, 2 * cw), BF16), jax.ShapeDtypeStruct((s, cw), F32)],
        compiler_params=_params("parallel"),
    )(p, p, p, p, p, p, p, p, p, p, p, sconv_w, dconv_w, dconv_b, cnorm_g, cnorm_b)


def odd_bwd_ln(dy, p, dc, cnorm_g, cnorm_b, name, tm=256):
    s = p.shape[0]
    cw = dc.shape[1]
    n = s // tm
    rs = 32

    def body(dy_ref, g2_ref, dc_ref, gam_ref, bet_ref, ddc_ref, dg_ref, dgam_ref, dbet_ref, gacc, bacc):
        i = pl.program_id(0)

        @pl.when(i == 0)
        def _():
            gacc[...] = jnp.zeros_like(gacc)
            bacc[...] = jnp.zeros_like(bacc)

        def chunk(ci, carry):
            rows = pl.ds(pl.multiple_of(ci * rs, rs), rs)
            xv = dc_ref[rows, :]
            mu = jnp.mean(xv, axis=-1, keepdims=True)
            xc = xv - mu
            rstd = lax.rsqrt(jnp.mean(xc * xc, axis=-1, keepdims=True) + EPS)
            xh = xc * rstd
            gam = gam_ref[...]
            sl, dsl = _silu_and_grad(xh * gam + bet_ref[...])
            sg, dsg = _silu_and_grad(g2_ref[rows, :].astype(F32))
            dyv = dy_ref[rows, :].astype(F32)
            dg_ref[rows, :] = (dyv * sl * dsg).astype(BF16)
            dln = dyv * sg * dsl
            gacc[...] += _rowsum8(dln * xh)
            bacc[...] += _rowsum8(dln)
            dxh = dln * gam
            ddc_ref[rows, :] = rstd * (dxh - jnp.mean(dxh, axis=-1, keepdims=True)
                                       - xh * jnp.mean(dxh * xh, axis=-1, keepdims=True))
            return carry

        lax.fori_loop(0, tm // rs, chunk, 0)

        @pl.when(i == n - 1)
        def _():
            dgam_ref[...] = jnp.sum(gacc[...], axis=0, keepdims=True)
            dbet_ref[...] = jnp.sum(bacc[...], axis=0, keepdims=True)

    vec = pl.BlockSpec((1, cw), lambda i: (0, 0))
    return pl.pallas_call(
        body, name=name, grid=(n,),
        in_specs=[pl.BlockSpec((tm, cw), lambda i: (i, 1)), pl.BlockSpec((tm, cw), lambda i: (i, 6)),
                  pl.BlockSpec((tm, cw), lambda i: (i, 0)), vec, vec],
        out_specs=[pl.BlockSpec((tm, cw), lambda i: (i, 0)), pl.BlockSpec((tm, cw), lambda i: (i, 0)), vec, vec],
        out_shape=[jax.ShapeDtypeStruct((s, cw), F32), jax.ShapeDtypeStruct((s, cw), BF16),
                   jax.ShapeDtypeStruct((1, cw), F32), jax.ShapeDtypeStruct((1, cw), F32)],
        scratch_shapes=[pltpu.VMEM((8, cw), F32), pltpu.VMEM((8, cw), F32)],
        compiler_params=_params("arbitrary"),
    )(dy, p, dc, cnorm_g, cnorm_b)


def odd_bwd_conv(dy, p, ddc, dg2, sconv_w, dconv_w, name, tm=128):
    s = p.shape[0]
    cw = ddc.shape[1]
    n = s // tm
    lanes = 128
    hb = _halo_before(tm)
    ha = _halo_after(tm, s)

    def body(dy_ref, dya_ref, g1_ref, g1a_ref, bc_ref, bca_ref, hc_ref, hch_ref, cc_ref, cch_ref,
             ddc_ref, ddca_ref, ga_ref, gah_ref, gb_ref, gbh_ref, dg2_ref, sw_ref, dw_ref,
             dp_ref, dsw_ref, ddw_ref, ddb_ref, sw_acc, dw_acc, db_acc):
        i = pl.program_id(0)
        first = i == 0
        last = i == n - 1

        @pl.when(first)
        def _():
            sw_acc[...] = jnp.zeros_like(sw_acc)
            dw_acc[...] = jnp.zeros_like(dw_acc)
            db_acc[...] = jnp.zeros_like(db_acc)

        for l in range(cw // lanes):
            cols = slice(l * lanes, (l + 1) * lanes)
            mh = jnp.where(first, 0.0, cch_ref[:, cols].astype(F32) * hch_ref[:, cols].astype(F32))
            hcv = hc_ref[:, cols].astype(F32)
            ccv = cc_ref[:, cols].astype(F32)
            xx = jnp.concatenate([mh, ccv * hcv], axis=0)
            taps = [_tap_before(xx, SCONV_K - 1 - k, tm) for k in range(SCONV_K)]
            cv = jnp.zeros((tm, lanes), F32)
            for k in range(SCONV_K):
                cv = cv + sw_ref[k:k + 1, cols] * taps[k]
            bcv = bc_ref[:, cols].astype(F32)
            dyv = dy_ref[:, cols].astype(F32)
            sg, dsg = _silu_and_grad(g1_ref[:, cols].astype(F32))
            dco = dyv * sg
            dp_ref[:, 5 * cw + l * lanes:5 * cw + (l + 1) * lanes] = (dyv * bcv * cv * dsg).astype(BF16)
            dp_ref[:, cw + l * lanes:cw + (l + 1) * lanes] = (dco * cv).astype(BF16)
            dcv = dco * bcv
            for k in range(SCONV_K):
                sw_acc[k * 8:(k + 1) * 8, cols] += _rowsum8(dcv * taps[k])
            dcv_a = jnp.where(last, 0.0, dya_ref[:, cols].astype(F32) * _silu(g1a_ref[:, cols].astype(F32))
                              * bca_ref[:, cols].astype(F32))
            xx = jnp.concatenate([dcv, dcv_a], axis=0)
            dm = jnp.zeros((tm, lanes), F32)
            for k in range(SCONV_K):
                dm = dm + sw_ref[k:k + 1, cols] * _tap_after(xx, SCONV_K - 1 - k, tm)
            dp_ref[:, l * lanes:(l + 1) * lanes] = (dm * ccv).astype(BF16)
            dp_ref[:, 2 * cw + l * lanes:2 * cw + (l + 1) * lanes] = (dm * hcv).astype(BF16)
            gav = ga_ref[:, cols].astype(F32)
            sb = _sigmoid(gb_ref[:, cols].astype(F32))
            dh = jnp.where(first, 0.0, gah_ref[:, cols].astype(F32) * _sigmoid(gbh_ref[:, cols].astype(F32)))
            xx = jnp.concatenate([dh, gav * sb], axis=0)
            ddcv = ddc_ref[:, cols]
            db_acc[:, cols] += _rowsum8(ddcv)
            for k in range(CONF_K):
                dw_acc[k * 8:(k + 1) * 8, cols] += _rowsum8(ddcv * _tap_before(xx, CONF_K - 1 - k, tm))
            ddc_a = jnp.where(last, 0.0, ddca_ref[:, cols])
            xx = jnp.concatenate([ddcv, ddc_a], axis=0)
            dgl = jnp.zeros((tm, lanes), F32)
            for k in range(CONF_K):
                dgl = dgl + dw_ref[k:k + 1, cols] * _tap_after(xx, CONF_K - 1 - k, tm)
            dp_ref[:, 3 * cw + l * lanes:3 * cw + (l + 1) * lanes] = (dgl * sb).astype(BF16)
            dp_ref[:, 4 * cw + l * lanes:4 * cw + (l + 1) * lanes] = (dgl * gav * sb * (1.0 - sb)).astype(BF16)
        dp_ref[:, 6 * cw:7 * cw] = dg2_ref[...]

        @pl.when(last)
        def _():
            for k in range(SCONV_K):
                dsw_ref[k:k + 1, :] = jnp.sum(sw_acc[k * 8:(k + 1) * 8, :], axis=0, keepdims=True)
            for k in range(CONF_K):
                ddw_ref[k:k + 1, :] = jnp.sum(dw_acc[k * 8:(k + 1) * 8, :], axis=0, keepdims=True)
            ddb_ref[...] = jnp.sum(db_acc[...], axis=0, keepdims=True)

    def main(c):
        return pl.BlockSpec((tm, cw), lambda i: (i, c))

    def before(c):
        return pl.BlockSpec((HALO, cw), lambda i: (hb(i), c))

    def after(c):
        return pl.BlockSpec((HALO, cw), lambda i: (ha(i), c))

    def vec(r):
        return pl.BlockSpec((r, cw), lambda i: (0, 0))

    return pl.pallas_call(
        body, name=name, grid=(n,),
        in_specs=[main(0), after(0), main(5), after(5), main(1), after(1), main(0), before(0), main(2), before(2),
                  main(0), after(0), main(3), before(3), main(4), before(4), main(0), vec(SCONV_K), vec(CONF_K)],
        out_specs=[pl.BlockSpec((tm, 7 * cw), lambda i: (i, 0)), vec(SCONV_K), vec(CONF_K), vec(1)],
        out_shape=[jax.ShapeDtypeStruct((s, 7 * cw), BF16), jax.ShapeDtypeStruct((SCONV_K, cw), F32),
                   jax.ShapeDtypeStruct((CONF_K, cw), F32), jax.ShapeDtypeStruct((1, cw), F32)],
        scratch_shapes=[pltpu.VMEM((8 * SCONV_K, cw), F32), pltpu.VMEM((8 * CONF_K, cw), F32),
                        pltpu.VMEM((8, cw), F32)],
        compiler_params=_params("arbitrary"),
    )(dy, dy, p, p, p, p, p, p, p, p, ddc, ddc, p, p, p, p, dg2, sconv_w, dconv_w)


_ANY = pl.BlockSpec(memory_space=pl.ANY)


def _place():
    return lax.axis_index("x"), lax.axis_index("y"), lax.axis_index("c")


def all_gather(arrs, name):
    n = len(arrs)

    def body(*refs):
        ins, outs = refs[:n], refs[n:2 * n]
        send_sems, recv_sems, local_sems = refs[2 * n:]
        x, y, c = _place()
        me, sibling = (x, y, c), (x, y, 1 - c)
        chips = [(1 - x, y), (x, 1 - y), (1 - x, 1 - y)]

        def copy(a, k, block, to, src=None):
            px, py, pc = block
            dst = outs[a].at[4 * px + 2 * py + pc]
            return pltpu.make_async_remote_copy(
                src_ref=dst if src is None else src, dst_ref=dst,
                send_sem=send_sems.at[7 * a + k], recv_sem=recv_sems.at[7 * a + k],
                device_id=to, device_id_type=MESH)

        mine = [pltpu.make_async_copy(ins[a], outs[a].at[4 * x + 2 * y + c], local_sems.at[a]) for a in range(n)]
        for cp in mine:
            cp.start()
        first = []
        for a in range(n):
            first.append(copy(a, 0, me, sibling, src=ins[a]))
            first += [copy(a, 1 + j, me, (*chip, c), src=ins[a]) for j, chip in enumerate(chips)]
        for cp in first:
            cp.start()
        passed = []
        for a in range(n):
            for j, chip in enumerate(chips):
                copy(a, 1 + j, (*chip, c), me).wait_recv()
                cp = copy(a, 4 + j, (*chip, c), sibling)
                cp.start()
                passed.append(cp)
        for a in range(n):
            copy(a, 0, sibling, me).wait_recv()
            for j, chip in enumerate(chips):
                copy(a, 4 + j, (*chip, 1 - c), me).wait_recv()
        for cp in first + passed:
            cp.wait_send()
        for cp in mine:
            cp.wait()

    return pl.pallas_call(
        body, name=name,
        out_shape=[jax.ShapeDtypeStruct((N_DEV,) + a.shape, a.dtype) for a in arrs],
        in_specs=[_ANY] * n, out_specs=[_ANY] * n,
        scratch_shapes=[pltpu.SemaphoreType.DMA((7 * n,)), pltpu.SemaphoreType.DMA((7 * n,)),
                        pltpu.SemaphoreType.DMA((n,))],
    )(*arrs)


_HBM = pl.BlockSpec(memory_space=pltpu.HBM)
_SEM = pl.BlockSpec(memory_space=pltpu.SEMAPHORE)
_DATAFLOW = pltpu.SideEffectType.DATAFLOW_SIDE_EFFECTING


def _peers_per_array(kind):
    return 1 if kind == "sibling" else 3


def _split_copies(kind, srcs, lands, send_sems, recv_sems):
    x, y, c = _place()
    per = _peers_per_array(kind)
    out = []
    for a in range(len(srcs)):
        if kind == "sibling":
            peers = [((x, y, 1 - c), srcs[a].at[:, pl.ds(1 - c, 1)], lands[a], lands[a])]
        else:
            peers = []
            for px, py in [(1 - x, y), (x, 1 - y), (1 - x, 1 - y)]:
                if kind == "gather":
                    views = (srcs[a], lands[a].at[4 * x + 2 * y + c], lands[a].at[4 * px + 2 * py + c])
                else:
                    views = (srcs[a].at[2 * px + py], lands[a].at[2 * x + y], lands[a].at[2 * px + py])
                peers.append(((px, py, c),) + views)
        for j, (peer, src, dst, arrives) in enumerate(peers):
            sems = dict(send_sem=send_sems.at[per * a + j], recv_sem=recv_sems.at[per * a + j],
                        device_id=peer, device_id_type=MESH)
            out.append((pltpu.make_async_remote_copy(src_ref=src, dst_ref=dst, **sems),
                        pltpu.make_async_remote_copy(src_ref=src, dst_ref=arrives, **sems)))
    return out


def split_start(kind, srcs, lands, deps, name):
    n = len(srcs)
    n_sems = _peers_per_array(kind) * n

    def body(*refs):
        send_sems, recv_sems = refs[2 * n + len(deps)], refs[2 * n + len(deps) + 1]
        for copy, _ in _split_copies(kind, refs[:n], refs[n:2 * n], send_sems, recv_sems):
            copy.start()
        token = refs[-1]
        token[...] = jnp.zeros_like(token)

    held = [pltpu.HBM(a.shape, a.dtype) for a in list(srcs) + list(lands)]
    outs = pl.pallas_call(
        body, name=name,
        out_shape=(pltpu.SemaphoreType.DMA((n_sems,)), pltpu.SemaphoreType.DMA((n_sems,)), *held,
                   jax.ShapeDtypeStruct((8, 128), F32)),
        in_specs=[_HBM] * (2 * n) + [_ANY] * len(deps),
        out_specs=(_SEM, _SEM, *([_HBM] * (2 * n)), pl.BlockSpec(memory_space=pltpu.VMEM)),
        input_output_aliases={i: 2 + i for i in range(2 * n)},
        compiler_params=pltpu.CompilerParams(has_side_effects=_DATAFLOW),
    )(*[pltpu.with_memory_space_constraint(a, pltpu.HBM) for a in list(srcs) + list(lands)], *deps)
    return outs[0], outs[1], list(outs[2:2 + n]), list(outs[2 + n:2 + 2 * n]), outs[-1]


def split_wait(kind, send_sems, recv_sems, srcs, lands, afters, name):
    n = len(srcs)

    def body(*refs):
        for _, arrival in _split_copies(kind, refs[:n], refs[n:2 * n], refs[2 * n], refs[2 * n + 1]):
            arrival.wait_send()
            arrival.wait_recv()

    outs = pl.pallas_call(
        body, name=name,
        out_shape=[pltpu.HBM(a.shape, a.dtype) for a in list(srcs) + list(lands)],
        in_specs=[_HBM] * (2 * n) + [_SEM, _SEM] + [_ANY] * len(afters),
        out_specs=[_HBM] * (2 * n),
        input_output_aliases={i: i for i in range(2 * n)},
        compiler_params=pltpu.CompilerParams(has_side_effects=_DATAFLOW),
    )(*srcs, *lands, send_sems, recv_sems, *afters)
    return list(outs[:n]), list(outs[n:])


def place_block(land, block, dev, name):
    r, c = block.shape
    tr = min(r, 512)

    def body(dev_ref, land_ref, b_ref, o_ref):
        del dev_ref, land_ref
        o_ref[...] = b_ref[...]

    return pl.pallas_call(
        body, name=name,
        grid_spec=pltpu.PrefetchScalarGridSpec(
            num_scalar_prefetch=1, grid=(r // tr,),
            in_specs=[_ANY, pl.BlockSpec((tr, c), lambda i, dev_ref: (i, 0))],
            out_specs=pl.BlockSpec((None, tr, c), lambda i, dev_ref: (dev_ref[0], i, 0))),
        out_shape=jax.ShapeDtypeStruct(land.shape, land.dtype),
        input_output_aliases={1: 0},
        compiler_params=_params("parallel"),
    )(dev, land, block)


def gather_finish(lands, name):
    n = len(lands)

    def body(*refs):
        outs = refs[n:2 * n]
        send_sems, recv_sems = refs[2 * n:]
        x, y, c = _place()
        cps = [pltpu.make_async_remote_copy(
            src_ref=outs[a].at[:, pl.ds(c, 1)], dst_ref=outs[a].at[:, pl.ds(c, 1)],
            send_sem=send_sems.at[a], recv_sem=recv_sems.at[a],
            device_id=(x, y, 1 - c), device_id_type=MESH) for a in range(n)]
        for cp in cps:
            cp.start()
        for cp in cps:
            cp.wait()

    return pl.pallas_call(
        body, name=name,
        out_shape=[jax.ShapeDtypeStruct(a.shape, a.dtype) for a in lands],
        in_specs=[_ANY] * n, out_specs=[_ANY] * n,
        input_output_aliases={i: i for i in range(n)},
        scratch_shapes=[pltpu.SemaphoreType.DMA((n,)), pltpu.SemaphoreType.DMA((n,))],
    )(*lands)


def pair_add(own, recv, core, name):
    _, _, r, c = own.shape
    tr = min(r, 512)

    def body(core_ref, own_ref, recv_ref, o_ref):
        del core_ref
        o_ref[...] = (own_ref[...].astype(F32) + recv_ref[...].astype(F32)).astype(BF16)

    return pl.pallas_call(
        body, name=name,
        grid_spec=pltpu.PrefetchScalarGridSpec(
            num_scalar_prefetch=1, grid=(4, r // tr),
            in_specs=[pl.BlockSpec((None, None, tr, c), lambda k, i, core_ref: (k, core_ref[0], i, 0)),
                      pl.BlockSpec((None, None, tr, c), lambda k, i, core_ref: (k, 0, i, 0))],
            out_specs=pl.BlockSpec((None, tr, c), lambda k, i, core_ref: (k, i, 0))),
        out_shape=jax.ShapeDtypeStruct((4, r, c), BF16),
        compiler_params=_params("parallel", "parallel"),
    )(core, own, recv)


def _adamw_math(w, g, m, v):
    m2 = ADAM_B1 * m + (1.0 - ADAM_B1) * g
    v2 = ADAM_B2 * v + (1.0 - ADAM_B2) * (g * g)
    m_hat = m2 / (1.0 - ADAM_B1 ** ADAM_STEP)
    v_hat = v2 / (1.0 - ADAM_B2 ** ADAM_STEP)
    delta = -ADAM_LR * (m_hat / (jnp.sqrt(v_hat) + ADAM_EPS) + ADAM_WD * w)
    return delta, m2, v2


def adamw_big(w, m, v, own, got, chip, name):
    r, c = w.shape
    tr = min(r, 256)

    def body(chip_ref, w_ref, m_ref, v_ref, p0, p1, p2, p3, g_ref, d_ref, m2_ref, v2_ref):
        del chip_ref
        g = ((p0[...].astype(F32) + p1[...].astype(F32)) + p2[...].astype(F32)) + p3[...].astype(F32)
        delta, m2, v2 = _adamw_math(w_ref[...], g, m_ref[...], v_ref[...])
        g_ref[...] = g
        d_ref[...] = delta
        m2_ref[...] = m2
        v2_ref[...] = v2

    row = pl.BlockSpec((tr, c), lambda i, chip_ref: (i, 0))

    def slab(flip):
        return pl.BlockSpec((None, tr, c), lambda i, chip_ref: (chip_ref[0] ^ flip, i, 0))

    return pl.pallas_call(
        body, name=name,
        grid_spec=pltpu.PrefetchScalarGridSpec(
            num_scalar_prefetch=1, grid=(r // tr,),
            in_specs=[row, row, row, slab(0), slab(1), slab(2), slab(3)],
            out_specs=[row] * 4),
        out_shape=[jax.ShapeDtypeStruct((r, c), F32)] * 4,
        compiler_params=_params("parallel"),
    )(chip, w, m, v, own, got, got, got)


def sum_devices(g8, name):
    def body(g_ref, o_ref):
        tot = g_ref[0]
        for k in range(1, N_DEV):
            tot = tot + g_ref[k]
        o_ref[...] = tot

    return pl.pallas_call(body, name=name, out_shape=jax.ShapeDtypeStruct(g8.shape[1:], F32))(g8)


def adamw_small(ws, gs, ms, vs, name):
    n = len(ws)

    def body(*refs):
        w_r, g_r, m_r, v_r = refs[:n], refs[n:2 * n], refs[2 * n:3 * n], refs[3 * n:4 * n]
        d_o, m_o, v_o = refs[4 * n:5 * n], refs[5 * n:6 * n], refs[6 * n:7 * n]
        for k in range(n):
            delta, m2, v2 = _adamw_math(w_r[k][...], g_r[k][...], m_r[k][...], v_r[k][...])
            d_o[k][...] = delta
            m_o[k][...] = m2
            v_o[k][...] = v2

    shapes = [jax.ShapeDtypeStruct(w.shape, F32) for w in ws]
    outs = pl.pallas_call(body, name=name, out_shape=shapes * 3)(*ws, *gs, *ms, *vs)
    return outs[:n], outs[n:2 * n], outs[2 * n:]


def _rows128(a):
    return a.reshape(-1, 128)


def _pad_rows(a, rows):
    return jnp.pad(a, ((0, rows - a.shape[0]), (0, 0)))


def kernel(x, ln_pre_even, w_in_even, pool_w, pool_scale, w_out_even, ln_post_even, ln_pre_odd, w_in_odd, sconv_w, dconv_w, dconv_b, cnorm_g, cnorm_b, w_out_odd, ln_post_odd, loss_target, m_ln_pre_even, m_w_in_even, m_pool_w, m_pool_scale, m_w_out_even, m_ln_post_even, m_ln_pre_odd, m_w_in_odd, m_sconv_w, m_dconv_w, m_dconv_b, m_cnorm_g, m_cnorm_b, m_w_out_odd, m_ln_post_odd, v_ln_pre_even, v_w_in_even, v_pool_w, v_pool_scale, v_w_out_even, v_ln_post_even, v_ln_pre_odd, v_w_in_odd, v_sconv_w, v_dconv_w, v_dconv_b, v_cnorm_g, v_cnorm_b, v_w_out_odd, v_ln_post_odd):
    xs = x[0]
    tgt = loss_target[0]
    s, d = xs.shape
    half = d // 2
    n_heads = half // HEAD_DIM
    ng = len(POOL_WINDOWS)
    cwp = half // ng
    dev = 4 * lax.axis_index("x") + 2 * lax.axis_index("y") + lax.axis_index("c")
    core = lax.axis_index("c").astype(jnp.int32).reshape(1)

    pr = pool_w.shape[2]
    cl = sconv_w.shape[2]
    small_parts = [(_rows128(ln_pre_odd), 8), (sconv_w[0], 8), (dconv_w[0], 32), (dconv_b, 8),
                   (cnorm_g, 8), (cnorm_b, 8), (_rows128(ln_post_odd), 8)]
    small_local = jnp.concatenate([_pad_rows(a, r) for a, r in small_parts], axis=0)
    g_wie, g_pw, g_small = all_gather(
        [w_in_even[0].astype(BF16), pool_w[0].reshape(ng * pr, cwp).astype(BF16), small_local], "ag_first")
    comm = _Exchanges(dev, core, d)
    in_proj_dep = comm.start_weights([w_out_even[0].astype(BF16), w_in_odd[0].astype(BF16),
                                      w_out_odd[0].astype(BF16)], after=g_wie)
    pool_full = g_pw.reshape(N_DEV, ng, pr, cwp).transpose(1, 0, 2, 3).reshape(ng, cwp, cwp)
    nl = ln_pre_odd.shape[1] // 128

    def chan(lo, rows):
        return g_small[:, lo:lo + rows].transpose(1, 0, 2).reshape(rows, N_DEV * cl)

    ln_pre_odd_f = g_small[:, 0:nl].reshape(1, d)
    sconv_f = chan(8, SCONV_K)
    dconv_f = chan(16, CONF_K)
    dconv_b_f = chan(48, 1)
    cnorm_g_f = chan(56, 1)
    cnorm_b_f = chan(64, 1)
    ln_post_odd_f = g_small[:, 72:72 + nl].reshape(1, d)

    loss_blk, grad_x, small_g = _fwd_bwd(
        xs, tgt, ln_pre_even, g_wie, pool_full, pool_scale, ln_post_even, ln_pre_odd_f,
        sconv_f, dconv_f, dconv_b_f, cnorm_g_f, cnorm_b_f, ln_post_odd_f, comm, in_proj_dep)
    loss = lax.psum(loss_blk[0, 0], ("x", "y", "c"))
    small_w = [ln_pre_even, pool_scale, ln_post_even, ln_pre_odd, sconv_w[0], dconv_w[0], dconv_b, cnorm_g, cnorm_b, ln_post_odd]
    small_m = [m_ln_pre_even, m_pool_scale, m_ln_post_even, m_ln_pre_odd, m_sconv_w[0], m_dconv_w[0], m_dconv_b, m_cnorm_g, m_cnorm_b, m_ln_post_odd]
    small_v = [v_ln_pre_even, v_pool_scale, v_ln_post_even, v_ln_pre_odd, v_sconv_w[0], v_dconv_w[0], v_dconv_b, v_cnorm_g, v_cnorm_b, v_ln_post_odd]
    sg, sd, sm, sv = _update_small(small_g, small_w, small_m, small_v, dev, d, cl)
    big = {"w_in_even": (w_in_even, m_w_in_even, v_w_in_even), "pool_w": (pool_w, m_pool_w, v_pool_w),
           "w_out_even": (w_out_even, m_w_out_even, v_w_out_even), "w_in_odd": (w_in_odd, m_w_in_odd, v_w_in_odd),
           "w_out_odd": (w_out_odd, m_w_out_odd, v_w_out_odd)}
    upd = comm.finish_updates(big, last_after=[grad_x] + sd)
    (g_wie_o, d_wie, m_wie, v_wie), (g_pw_o, d_pw, m_pw, v_pw) = upd["w_in_even"], upd["pool_w"]
    (g_woe_o, d_woe, m_woe, v_woe), (g_wio_o, d_wio, m_wio, v_wio) = upd["w_out_even"], upd["w_in_odd"]
    g_woo_o, d_woo, m_woo, v_woo = upd["w_out_odd"]

    def order(small, wie, pw, woe, wio, woo):
        return [small[0], wie, pw, small[1], woe, small[2], small[3], wio, small[4], small[5], small[6],
                small[7], small[8], woo, small[9]]

    grads = order(sg, g_wie_o, g_pw_o, g_woe_o, g_wio_o, g_woo_o)
    deltas = order(sd, d_wie, d_pw, d_woe, d_wio, d_woo)
    new_m = order(sm, m_wie, m_pw, m_woe, m_wio, m_woo)
    new_v = order(sv, v_wie, v_pw, v_woe, v_wio, v_woo)
    return (loss, grad_x[None], *grads, *deltas, *new_m, *new_v)


def _fwd_bwd(xs, tgt, ln_pre_even, g_wie, pool_full, pool_scale, ln_post_even, ln_pre_odd_f,
             sconv_f, dconv_f, dconv_b_f, cnorm_g_f, cnorm_b_f, ln_post_odd_f, comm, in_proj_dep):
    d = xs.shape[1]
    n_heads = d // 2 // HEAD_DIM
    ng, cwp = pool_full.shape[0], pool_full.shape[1]
    h0 = rms_fwd(xs, ln_pre_even, "rms_pre_even")
    p0 = mm_nn(h0, g_wie, BF16, "in_proj_even", dep=in_proj_dep)
    a0 = sb_fwd(p0, n_heads, "sb_fwd")
    y0 = even_mix_fwd(a0, p0, pool_full, pool_scale, "even_mix_fwd")
    w_out_e, g_wio, w_out_o = comm.rest_of_weights(after=y0)
    o0 = mm_nn(y0, w_out_e, F32, "out_proj_even", tn=512)
    x1 = postnorm_fwd(xs, o0, ln_post_even, "post_even")
    h1 = rms_fwd(x1, ln_pre_odd_f, "rms_pre_odd")
    p1 = mm_nn(h1, g_wio, BF16, "in_proj_odd")
    y1, dc = odd_mix_fwd(p1, sconv_f, dconv_f, dconv_b_f, cnorm_g_f, cnorm_b_f, "odd_mix_fwd")
    o1 = mm_nn(y1, w_out_o, F32, "out_proj_odd", tn=512)
    loss_blk, gx2, do1, dg_post_odd = final_fwd_bwd(x1, o1, ln_post_odd_f, tgt, "post_odd_loss")

    dw_out_o = mm_tn(y1, do1, 1, BF16, "dw_out_odd", tk=512)
    dy1 = mm_nt(do1, w_out_o, BF16, "dy_odd", tn=512)
    ddc, dg2, dgam, dbet = odd_bwd_ln(dy1, p1, dc, cnorm_g_f, cnorm_b_f, "odd_bwd_ln")
    dp1, dsconv, ddconv, ddconv_b = odd_bwd_conv(dy1, p1, ddc, dg2, sconv_f, dconv_f, "odd_bwd_conv")
    dw_in_o = mm_tn(h1, dp1, N_DEV, BF16, "dw_in_odd")
    dep = comm.reduce_begin({"w_out_odd": dw_out_o.reshape(N_DEV, d // N_DEV, d), "w_in_odd": dw_in_o}, "odd")
    dh1 = mm_nt(dp1, g_wio, F32, "dh_odd", dep=dep)
    dep = comm.reduce_send(after=dh1)
    gx1, dg_pre_odd = norm_bwd(dh1, x1, ln_pre_odd_f, gx2, F32, "pre_odd_bwd", dep=dep)

    do0, dg_post_even = norm_bwd(gx1, o0, ln_post_even, None, BF16, "post_even_bwd")
    dw_out_e = mm_tn(y0, do0, 1, BF16, "dw_out_even", tk=512)
    dy0 = mm_nt(do0, w_out_e, BF16, "dy_even", tn=512)
    da0, du0, dg0, dpool, dpool_scale = even_mix_bwd(dy0, a0, p0, pool_full, pool_scale, "even_mix_bwd")
    pr = cwp // N_DEV
    dpool_slabs = dpool.astype(BF16).reshape(ng, N_DEV, pr, cwp).transpose(1, 0, 2, 3).reshape(N_DEV, ng * pr, cwp)
    dep = comm.reduce_begin({"w_out_even": dw_out_e.reshape(N_DEV, d // N_DEV, d), "pool_w": dpool_slabs}, "even_out")
    dq0, dk0, dv0 = sb_bwd(p0, a0, da0, n_heads, "sb_bwd", dep=dep)
    dep = comm.reduce_send(after=dq0)
    dp0 = jnp.concatenate([dq0, dk0, dv0, du0, dg0], axis=1)
    dw_in_e = mm_tn(h0, dp0, N_DEV, BF16, "dw_in_even", dep=dep)
    dep = comm.reduce_begin({"w_in_even": dw_in_e}, "even_in")
    dh0 = mm_nt(dp0, g_wie, F32, "dh_even", dep=dep)
    dep = comm.reduce_send(after=dh0)
    grad_x, dg_pre_even = norm_bwd(dh0, xs, ln_pre_even, gx1, F32, "pre_even_bwd", dep=dep)
    small_g = [dg_pre_even, dpool_scale, dg_post_even, dg_pre_odd, dsconv, ddconv, ddconv_b, dgam, dbet, dg_post_odd]
    return loss_blk, grad_x, small_g


class _Exchanges:
    def __init__(self, dev, core, d):
        self.dev = dev.astype(jnp.int32).reshape(1)
        self.core = core
        self.chip = (dev // 2).astype(jnp.int32).reshape(1)
        self.d = d
        self.weights = None
        self.to_sibling = None
        self.pending = []

    def start_weights(self, blocks, after):
        lands = [lax.empty((N_DEV,) + b.shape, b.dtype) for b in blocks]
        send, recv, srcs, lands, token = split_start("gather", blocks, lands, [after], "ag_rest_start")
        self.weights = (send, recv, srcs, lands)
        return token

    def rest_of_weights(self, after):
        send, recv, srcs, lands = self.weights
        srcs, lands = split_wait("gather", send, recv, srcs, lands, [after], "ag_rest_wait")
        lands = [place_block(l, b, self.dev, "ag_rest_own_%d" % k) for k, (l, b) in enumerate(zip(lands, srcs))]
        full = gather_finish([l.reshape((4, 2) + l.shape[1:]) for l in lands], "ag_rest_finish")
        w_out_e, g_wio, w_out_o = [f.reshape((N_DEV,) + f.shape[2:]) for f in full]
        return w_out_e.reshape(1, self.d, self.d), g_wio, w_out_o.reshape(1, self.d, self.d)

    def reduce_begin(self, partials, tag):
        names = list(partials)
        arrs = [partials[k].reshape((4, 2) + partials[k].shape[1:]) for k in names]
        lands = [lax.empty((4, 1) + a.shape[2:], a.dtype) for a in arrs]
        send, recv, srcs, lands, token = split_start("sibling", arrs, lands, [], "rs_sibling_start_" + tag)
        self.to_sibling = (tag, names, send, recv, srcs, lands)
        return token

    def reduce_send(self, after):
        tag, names, send, recv, srcs, lands = self.to_sibling
        srcs, lands = split_wait("sibling", send, recv, srcs, lands, [after], "rs_sibling_wait_" + tag)
        sums = [pair_add(o, r, self.core, "rs_pair_add_" + k) for k, o, r in zip(names, srcs, lands)]
        zones = [lax.empty(a.shape, a.dtype) for a in sums]
        send, recv, srcs, zones, token = split_start("scatter", sums, zones, [], "rs_start_" + tag)
        self.pending.append((tag, names, send, recv, srcs, zones))
        return token

    def finish_updates(self, big, last_after):
        out = {}
        done = []
        for i, (tag, names, send, recv, srcs, lands) in enumerate(self.pending):
            afters = list(last_after) + done if i == len(self.pending) - 1 else []
            srcs, lands = split_wait("scatter", send, recv, srcs, lands, afters, "rs_wait_" + tag)
            for name, own, got in zip(names, srcs, lands):
                w, m, v = big[name]
                shp = own.shape[1:]
                outs = adamw_big(w.reshape(shp), m.reshape(shp), v.reshape(shp), own, got, self.chip, "adamw_" + name)
                out[name] = [o.reshape(w.shape) for o in outs]
                done.append(outs[1])
        return out


def _update_small(small_g, small_w, small_m, small_v, dev, d, cl):
    packed = jnp.concatenate([_rows128(g) for g in small_g], axis=0)
    (g8,) = all_gather([packed], "ag_small_grads")
    tot = sum_devices(g8, "sum_small_grads")
    full_g = []
    lo = 0
    for g in small_g:
        rows = g.size // 128
        full_g.append(tot[lo:lo + rows].reshape(g.shape))
        lo += rows

    def mine(g, width):
        return lax.dynamic_slice_in_dim(g, dev * width, width, axis=g.ndim - 1)

    fg = full_g
    small_gl = [fg[0], fg[1], fg[2], mine(fg[3], d // N_DEV), mine(fg[4], cl), mine(fg[5], cl), mine(fg[6], cl),
                mine(fg[7], cl), mine(fg[8], cl), mine(fg[9], d // N_DEV)]
    sd, sm, sv = adamw_small(small_w, small_gl, small_m, small_v, "adamw_small")

    def like(k, a):
        return a[None] if k in (4, 5) else a

    sg = [like(k, a) for k, a in enumerate(small_gl)]
    sd = [like(k, a) for k, a in enumerate(sd)]
    sm = [like(k, a) for k, a in enumerate(sm)]
    sv = [like(k, a) for k, a in enumerate(sv)]
    return sg, sd, sm, sv
```

```python
import functools
import math

import jax
import jax.numpy as jnp
from jax import lax
from jax.experimental import pallas as pl
from jax.experimental.pallas import tpu as pltpu

F32 = jnp.float32
BF16 = jnp.bfloat16
EPS = 1e-6
HEAD_DIM = 128
POOL_WINDOWS = (2, 4, 8, 16)
SCONV_K = 3
CONF_K = 31
HALO = 32
N_DEV = 8
VMEM_LIMIT = 56 * 1024 * 1024
MESH = pl.DeviceIdType.MESH

ADAM_LR = 0.001
ADAM_B1 = 0.9
ADAM_B2 = 0.999
ADAM_EPS = 1e-08
ADAM_WD = 0.01
ADAM_STEP = 10


def _params(*sem):
    return pltpu.CompilerParams(dimension_semantics=sem, vmem_limit_bytes=VMEM_LIMIT)


def _sigmoid(v):
    return 1.0 / (1.0 + jnp.exp(-v))


def _silu(v):
    return v * _sigmoid(v)


def _silu_and_grad(v):
    s = _sigmoid(v)
    return v * s, s * (1.0 + v * (1.0 - s))


def _rowsum8(v):
    r, c = v.shape
    return jnp.sum(v.reshape(r // 8, 8, c), axis=0)


def _tap_before(xx, i, rows):
    if i == 0:
        return xx[HALO:HALO + rows]
    return pltpu.roll(xx, i, 0)[HALO:HALO + rows]


def _tap_after(xx, i, rows):
    if i == 0:
        return xx[0:rows]
    return pltpu.roll(xx, xx.shape[0] - i, 0)[0:rows]


def rms_fwd(x, g, name, tm=256):
    s, d = x.shape

    def body(x_ref, g_ref, h_ref):
        xv = x_ref[...]
        r = lax.rsqrt(jnp.mean(xv * xv, axis=-1, keepdims=True) + EPS)
        h_ref[...] = (xv * r * g_ref[...]).astype(BF16)

    return pl.pallas_call(
        body, name=name, grid=(s // tm,),
        in_specs=[pl.BlockSpec((tm, d), lambda i: (i, 0)), pl.BlockSpec((1, d), lambda i: (0, 0))],
        out_specs=pl.BlockSpec((tm, d), lambda i: (i, 0)),
        out_shape=jax.ShapeDtypeStruct((s, d), BF16),
        compiler_params=_params("parallel"),
    )(x, g)


def postnorm_fwd(x, o, g, name, tm=256):
    s, d = x.shape

    def body(x_ref, o_ref, g_ref, y_ref):
        ov = o_ref[...]
        r = lax.rsqrt(jnp.mean(ov * ov, axis=-1, keepdims=True) + EPS)
        y_ref[...] = x_ref[...] + ov * r * g_ref[...]

    return pl.pallas_call(
        body, name=name, grid=(s // tm,),
        in_specs=[pl.BlockSpec((tm, d), lambda i: (i, 0)), pl.BlockSpec((tm, d), lambda i: (i, 0)),
                  pl.BlockSpec((1, d), lambda i: (0, 0))],
        out_specs=pl.BlockSpec((tm, d), lambda i: (i, 0)),
        out_shape=jax.ShapeDtypeStruct((s, d), F32),
        compiler_params=_params("parallel"),
    )(x, o, g)


def final_fwd_bwd(x1, o, g, target, name, tm=256):
    s, d = x1.shape
    n = s // tm

    def body(x_ref, o_ref, g_ref, t_ref, loss_ref, gx_ref, do_ref, dg_ref, lacc, gacc):
        i = pl.program_id(0)

        @pl.when(i == 0)
        def _():
            lacc[...] = jnp.zeros_like(lacc)
            gacc[...] = jnp.zeros_like(gacc)

        ov = o_ref[...]
        gv = g_ref[...]
        r = lax.rsqrt(jnp.mean(ov * ov, axis=-1, keepdims=True) + EPS)
        oh = ov * r
        diff = x_ref[...] + oh * gv - t_ref[...]
        lacc[...] += _rowsum8(diff * diff)
        gx = diff * (1.0 / d)
        gx_ref[...] = gx
        gacc[...] += _rowsum8(gx * oh)
        dn = gx * gv
        do_ref[...] = (r * (dn - oh * jnp.mean(dn * oh, axis=-1, keepdims=True))).astype(BF16)

        @pl.when(i == n - 1)
        def _():
            tot = jnp.sum(jnp.sum(lacc[...], axis=0, keepdims=True), axis=1, keepdims=True)
            loss_ref[...] = jnp.broadcast_to(tot * (0.5 / d), loss_ref.shape)
            dg_ref[...] = jnp.sum(gacc[...], axis=0, keepdims=True)

    row = pl.BlockSpec((tm, d), lambda i: (i, 0))
    vec = pl.BlockSpec((1, d), lambda i: (0, 0))
    return pl.pallas_call(
        body, name=name, grid=(n,),
        in_specs=[row, row, vec, row],
        out_specs=[pl.BlockSpec((8, 128), lambda i: (0, 0)), row, row, vec],
        out_shape=[jax.ShapeDtypeStruct((8, 128), F32), jax.ShapeDtypeStruct((s, d), F32),
                   jax.ShapeDtypeStruct((s, d), BF16), jax.ShapeDtypeStruct((1, d), F32)],
        scratch_shapes=[pltpu.VMEM((8, d), F32), pltpu.VMEM((8, d), F32)],
        compiler_params=_params("arbitrary"),
    )(x1, o, g, target)


def norm_bwd(dy, inp, g, resid, out_dtype, name, tm=256, dep=None):
    s, d = inp.shape
    n = s // tm
    has_resid = resid is not None

    def body(*refs):
        dy_ref, x_ref, g_ref = refs[:3]
        r_ref = refs[3] if has_resid else None
        dx_ref, dg_ref, gacc = refs[-3:]
        i = pl.program_id(0)

        @pl.when(i == 0)
        def _():
            gacc[...] = jnp.zeros_like(gacc)

        xv = x_ref[...]
        dyv = dy_ref[...].astype(F32)
        r = lax.rsqrt(jnp.mean(xv * xv, axis=-1, keepdims=True) + EPS)
        xh = xv * r
        gacc[...] += _rowsum8(dyv * xh)
        dn = dyv * g_ref[...]
        dx = r * (dn - xh * jnp.mean(dn * xh, axis=-1, keepdims=True))
        if has_resid:
            dx = dx + r_ref[...]
        dx_ref[...] = dx.astype(out_dtype)

        @pl.when(i == n - 1)
        def _():
            dg_ref[...] = jnp.sum(gacc[...], axis=0, keepdims=True)

    row = pl.BlockSpec((tm, d), lambda i: (i, 0))
    vec = pl.BlockSpec((1, d), lambda i: (0, 0))
    dep_args, dep_specs = _after(dep)
    args = [dy, inp, g] + ([resid] if has_resid else []) + dep_args
    return pl.pallas_call(
        body, name=name, grid=(n,),
        in_specs=[row, row, vec] + ([row] if has_resid else []) + dep_specs,
        out_specs=[row, vec],
        out_shape=[jax.ShapeDtypeStruct((s, d), out_dtype), jax.ShapeDtypeStruct((1, d), F32)],
        scratch_shapes=[pltpu.VMEM((8, d), F32)],
        compiler_params=_params("arbitrary"),
    )(*args)


def _after(dep):
    if dep is None:
        return [], []
    return [dep], [pl.BlockSpec((8, 128), lambda *_: (0, 0))]


def mm_nn(a, w, out_dtype, name, tm=512, tn=None, dep=None):
    m, k = a.shape
    ns, _, n = w.shape
    tn = n if tn is None else tn
    nj = n // tn
    dep_args, dep_specs = _after(dep)

    def body(a_ref, w_ref, *rest):
        o_ref = rest[-1]
        o_ref[...] = jnp.dot(a_ref[...], w_ref[0], preferred_element_type=F32).astype(out_dtype)

    return pl.pallas_call(
        body, name=name, grid=(ns, nj, m // tm),
        in_specs=[pl.BlockSpec((tm, k), lambda s, j, i: (i, 0)),
                  pl.BlockSpec((1, k, tn), lambda s, j, i: (s, 0, j))] + dep_specs,
        out_specs=pl.BlockSpec((tm, tn), lambda s, j, i: (i, s * nj + j)),
        out_shape=jax.ShapeDtypeStruct((m, ns * n), out_dtype),
        compiler_params=_params("parallel", "parallel", "parallel"),
    )(a, w, *dep_args)


def mm_nt(a, w, out_dtype, name, tm=512, tn=None, dep=None):
    m = a.shape[0]
    ns, k, n = w.shape
    tn = n if tn is None else tn
    nj = n // tn
    steps = ns * nj
    dep_args, dep_specs = _after(dep)

    def body(a_ref, w_ref, *rest):
        o_ref, acc = rest[-2:]
        r = pl.program_id(1)

        @pl.when(r == 0)
        def _():
            acc[...] = jnp.zeros_like(acc)

        acc[...] += lax.dot_general(a_ref[...], w_ref[0], (((1,), (1,)), ((), ())),
                                    preferred_element_type=F32)

        @pl.when(r == steps - 1)
        def _():
            o_ref[...] = acc[...].astype(out_dtype)

    return pl.pallas_call(
        body, name=name, grid=(m // tm, steps),
        in_specs=[pl.BlockSpec((tm, tn), lambda i, r: (i, r)),
                  pl.BlockSpec((1, k, tn), lambda i, r: (r // nj, 0, r % nj))] + dep_specs,
        out_specs=pl.BlockSpec((tm, k), lambda i, r: (i, 0)),
        out_shape=jax.ShapeDtypeStruct((m, k), out_dtype),
        scratch_shapes=[pltpu.VMEM((tm, k), F32)],
        compiler_params=_params("parallel", "arbitrary"),
    )(a, w, *dep_args)


def mm_tn(a, b, ns, out_dtype, name, tk=1024, tm=512, dep=None):
    m, k = a.shape
    n = b.shape[1] // ns
    steps = m // tm
    dep_args, dep_specs = _after(dep)

    def body(a_ref, b_ref, *rest):
        o_ref, acc = rest[-2:]
        r = pl.program_id(2)

        @pl.when(r == 0)
        def _():
            acc[...] = jnp.zeros_like(acc)

        acc[...] += lax.dot_general(a_ref[...], b_ref[...], (((0,), (0,)), ((), ())),
                                    preferred_element_type=F32)

        @pl.when(r == steps - 1)
        def _():
            o_ref[0] = acc[...].astype(out_dtype)

    return pl.pallas_call(
        body, name=name, grid=(ns, k // tk, steps),
        in_specs=[pl.BlockSpec((tm, tk), lambda s, j, r: (r, j)),
                  pl.BlockSpec((tm, n), lambda s, j, r: (r, s))] + dep_specs,
        out_specs=pl.BlockSpec((1, tk, n), lambda s, j, r: (s, j, 0)),
        out_shape=jax.ShapeDtypeStruct((ns, k, n), out_dtype),
        scratch_shapes=[pltpu.VMEM((tk, n), F32)],
        compiler_params=_params("parallel", "parallel", "arbitrary"),
    )(a, b, *dep_args)


SB_BLK = 128


def _split_dot(v, tri):
    hi = v.astype(BF16)
    lo = (v - hi.astype(F32)).astype(BF16)
    return jnp.dot(hi, tri, preferred_element_type=F32) + jnp.dot(lo, tri, preferred_element_type=F32)


def _sb_scores(z, lim, dcol, tri_ex):
    mask = dcol < lim
    sp = jnp.log(1.0 + jnp.exp(-jnp.abs(z)))
    lb = jnp.minimum(z, 0.0) - sp
    l1m = jnp.where(mask, lb - z, 0.0)
    return mask, lb, l1m, _split_dot(l1m, tri_ex)


def _sb_consts():
    row = lax.broadcasted_iota(jnp.int32, (SB_BLK, SB_BLK), 0)
    col = lax.broadcasted_iota(jnp.int32, (SB_BLK, SB_BLK), 1)
    tri_ex = jnp.where(row > col, 1.0, 0.0).astype(BF16)
    tri_in = jnp.where(row >= col, 1.0, 0.0).astype(BF16)
    return col - row, tri_ex, tri_in


def sb_fwd(p, n_heads, name, tq=256, nsub=4):
    s = p.shape[0]
    h_n = n_heads
    b = SB_BLK
    nqs = tq // b
    tk = nsub * b
    scale = 1.0 / math.sqrt(HEAD_DIM)

    def body(q_ref, k_ref, v_ref, o_ref, w_ref):
        qi = pl.program_id(1)
        dcol, tri_ex, _ = _sb_consts()
        qv = [q_ref[qs * b:(qs + 1) * b, :] for qs in range(nqs)]
        n_groups = ((qi + 1) * nqs - 1) // nsub + 1

        def step(it, carry):
            c1s, accs = carry
            g = n_groups - 1 - it
            off = pl.multiple_of(g * tk, tk)
            kg = k_ref[pl.ds(off, tk), :]
            vg = v_ref[pl.ds(off, tk), :]
            new_c1, new_acc = [], []
            for qs in range(nqs):
                qb = qi * nqs + qs
                z = lax.dot_general(qv[qs], kg, (((1,), (1,)), ((), ())), preferred_element_type=F32) * scale
                blocks = [_sb_scores(z[:, j * b:(j + 1) * b], (qb - (g * nsub + j)) * b, dcol, tri_ex)
                          for j in range(nsub)]
                run = c1s[qs]
                ws = [None] * nsub
                for j in reversed(range(nsub)):
                    mask, lb, l1m, ls_loc = blocks[j]
                    ws[j] = jnp.where(mask, jnp.exp(lb + ls_loc + run), 0.0).astype(BF16)
                    run = run + jnp.sum(l1m, axis=1, keepdims=True)
                w = jnp.concatenate(ws, axis=1)
                w_ref[0, g, qs * b:(qs + 1) * b, :] = w
                new_acc.append(accs[qs] + jnp.dot(w, vg, preferred_element_type=F32))
                new_c1.append(run)
            return tuple(new_c1), tuple(new_acc)

        init = (tuple(jnp.zeros((b, 1), F32) for _ in range(nqs)),
                tuple(jnp.zeros((b, HEAD_DIM), F32) for _ in range(nqs)))
        _, accs = lax.fori_loop(0, n_groups, step, init)
        for qs in range(nqs):
            o_ref[qs * b:(qs + 1) * b, :] = accs[qs]

    return pl.pallas_call(
        body, name=name, grid=(h_n, s // tq),
        in_specs=[pl.BlockSpec((tq, HEAD_DIM), lambda h, i: (i, h)),
                  pl.BlockSpec((s, HEAD_DIM), lambda h, i: (0, h_n + h)),
                  pl.BlockSpec((s, HEAD_DIM), lambda h, i: (0, 2 * h_n + h))],
        out_specs=[pl.BlockSpec((tq, HEAD_DIM), lambda h, i: (i, h)),
                   pl.BlockSpec((1, s // tk, tq, tk), lambda h, i: (h, 0, i, 0))],
        out_shape=[jax.ShapeDtypeStruct((s, h_n * HEAD_DIM), F32),
                   jax.ShapeDtypeStruct((h_n, s // tk, s, tk), BF16)],
        compiler_params=_params("parallel", "arbitrary"),
    )(p, p, p)


def sb_bwd(p, a, wts, da, n_heads, name, tq=256, dep=None):
    s = p.shape[0]
    h_n = n_heads
    nq = s // tq
    b = SB_BLK
    nqs = tq // b
    tk = wts.shape[3]
    nsub = tk // b
    scale = 1.0 / math.sqrt(HEAD_DIM)
    dep_args, dep_specs = _after(dep)

    def body(q_ref, k_ref, v_ref, a_ref, da_ref, w_ref, *rest):
        dq_ref, dk_ref, dv_ref, dk_acc, dv_acc = rest[-5:]
        qi = pl.program_id(1)

        @pl.when(qi == 0)
        def _():
            dk_acc[...] = jnp.zeros_like(dk_acc)
            dv_acc[...] = jnp.zeros_like(dv_acc)

        dcol, _, tri_in = _sb_consts()
        q_all = q_ref[...]
        do_all = da_ref[...]
        qv = [q_ref[qs * b:(qs + 1) * b, :] for qs in range(nqs)]
        dov = [da_ref[qs * b:(qs + 1) * b, :] for qs in range(nqs)]
        tots = [jnp.sum(dov[qs].astype(F32) * a_ref[qs * b:(qs + 1) * b, :], axis=1, keepdims=True)
                for qs in range(nqs)]
        n_groups = ((qi + 1) * nqs - 1) // nsub + 1

        def step(it, carry):
            c2s, dqs = carry
            g = n_groups - 1 - it
            off = pl.multiple_of(g * tk, tk)
            kg = k_ref[pl.ds(off, tk), :]
            vg = v_ref[pl.ds(off, tk), :]
            w_all = w_ref[0, g]
            new_c2, new_dq, dz_rows = [], [], []
            for qs in range(nqs):
                qb = qi * nqs + qs
                z = lax.dot_general(qv[qs], kg, (((1,), (1,)), ((), ())), preferred_element_type=F32) * scale
                dw = lax.dot_general(dov[qs], vg, (((1,), (1,)), ((), ())), preferred_element_type=F32)
                beta = 1.0 / (1.0 + jnp.exp(-z))
                e = dw * w_all[qs * b:(qs + 1) * b, :].astype(F32)
                run2 = c2s[qs]
                dzs = [None] * nsub
                for j in reversed(range(nsub)):
                    cols = slice(j * b, (j + 1) * b)
                    mask = dcol < (qb - (g * nsub + j)) * b
                    later = _split_dot(e[:, cols], tri_in) + run2
                    bj = beta[:, cols]
                    dz = jnp.where(mask, e[:, cols] * (1.0 - bj) - bj * (tots[qs] - later), 0.0) * scale
                    dzs[j] = dz.astype(BF16)
                    run2 = run2 + jnp.sum(e[:, cols], axis=1, keepdims=True)
                dzq = jnp.concatenate(dzs, axis=1)
                new_dq.append(dqs[qs] + jnp.dot(dzq, kg, preferred_element_type=F32))
                new_c2.append(run2)
                dz_rows.append(dzq)
            dz_all = jnp.concatenate(dz_rows, axis=0)
            dk_acc[pl.ds(off, tk), :] += lax.dot_general(dz_all, q_all, (((0,), (0,)), ((), ())),
                                                         preferred_element_type=F32)
            dv_acc[pl.ds(off, tk), :] += lax.dot_general(w_all, do_all, (((0,), (0,)), ((), ())),
                                                         preferred_element_type=F32)
            return tuple(new_c2), tuple(new_dq)

        zeros = tuple(jnp.zeros((b, 1), F32) for _ in range(nqs))
        _, dqs = lax.fori_loop(0, n_groups, step,
                               (zeros, tuple(jnp.zeros((b, HEAD_DIM), F32) for _ in range(nqs))))
        for qs in range(nqs):
            dq_ref[qs * b:(qs + 1) * b, :] = dqs[qs].astype(BF16)

        @pl.when(qi == nq - 1)
        def _():
            dk_ref[...] = dk_acc[...].astype(BF16)
            dv_ref[...] = dv_acc[...].astype(BF16)

    blk = pl.BlockSpec((tq, HEAD_DIM), lambda h, i: (i, h))
    full = pl.BlockSpec((s, HEAD_DIM), lambda h, i: (0, h))
    return pl.pallas_call(
        body, name=name, grid=(h_n, nq),
        in_specs=[blk, pl.BlockSpec((s, HEAD_DIM), lambda h, i: (0, h_n + h)),
                  pl.BlockSpec((s, HEAD_DIM), lambda h, i: (0, 2 * h_n + h)), blk, blk,
                  pl.BlockSpec((1, s // tk, tq, tk), lambda h, i: (h, 0, i, 0))] + dep_specs,
        out_specs=[blk, full, full],
        out_shape=[jax.ShapeDtypeStruct((s, h_n * HEAD_DIM), BF16)] * 3,
        scratch_shapes=[pltpu.VMEM((s, HEAD_DIM), F32), pltpu.VMEM((s, HEAD_DIM), F32)],
        compiler_params=_params("parallel", "arbitrary"),
    )(p, p, p, a, da, wts, *dep_args)


def _pool_window(xx, win, r0, rc):
    cur = xx[HALO:HALO + rc]
    ws = cur
    for i in range(1, win):
        ws = ws + _tap_before(xx, i, rc)
    t_idx = r0 + lax.broadcasted_iota(jnp.int32, (rc, 1), 0)
    inv = 1.0 / jnp.minimum(win, t_idx + 1).astype(F32)
    return ws * inv - cur, inv


def even_mix_fwd(a, p, pool_w, pool_scale, name, rc=64):
    s = p.shape[0]
    ng = len(POOL_WINDOWS)
    cw = pool_w.shape[1]
    n_chunks = s // rc

    def body(a_ref, u_ref, g_ref, w_ref, sc_ref, y_ref, upad):
        j = pl.program_id(0)

        @pl.when(j < ng)
        def _():
            def chunk(ci, carry):
                rows = pl.ds(pl.multiple_of(ci * rc, rc), rc)
                y_ref[rows, :] = (a_ref[rows, :] * _silu(g_ref[rows, :].astype(F32))).astype(BF16)
                return carry

            lax.fori_loop(0, n_chunks, chunk, 0)

        for gi, win in enumerate(POOL_WINDOWS):
            @pl.when(j == ng + gi)
            def _(win=win):
                upad[0:HALO, :] = jnp.zeros((HALO, cw), F32)

                def fill(ci, carry):
                    r0 = pl.multiple_of(ci * rc, rc)
                    upad[pl.ds(pl.multiple_of(r0 + HALO, HALO), rc), :] = u_ref[pl.ds(r0, rc), :].astype(F32)
                    return carry

                lax.fori_loop(0, n_chunks, fill, 0)

                def chunk(ci, carry):
                    r0 = pl.multiple_of(ci * rc, rc)
                    rows = pl.ds(r0, rc)
                    pooled, _ = _pool_window(upad[pl.ds(r0, HALO + rc), :], win, r0, rc)
                    t = jnp.dot(pooled.astype(BF16), w_ref[0], preferred_element_type=F32)
                    y_ref[rows, :] = (t * sc_ref[...] * _silu(g_ref[rows, :].astype(F32))).astype(BF16)
                    return carry

                lax.fori_loop(0, n_chunks, chunk, 0)

    grp = lambda j: jnp.maximum(j - ng, 0)
    return pl.pallas_call(
        body, name=name, grid=(2 * ng,),
        in_specs=[pl.BlockSpec((s, cw), lambda j: (0, jnp.minimum(j, ng - 1))),
                  pl.BlockSpec((s, cw), lambda j: (0, 3 * ng + grp(j))),
                  pl.BlockSpec((s, cw), lambda j: (0, 4 * ng + j)),
                  pl.BlockSpec((1, cw, cw), lambda j: (grp(j), 0, 0)),
                  pl.BlockSpec((1, cw), lambda j: (0, grp(j)))],
        out_specs=pl.BlockSpec((s, cw), lambda j: (0, j)),
        out_shape=jax.ShapeDtypeStruct((s, 2 * ng * cw), BF16),
        scratch_shapes=[pltpu.VMEM((HALO + s, cw), F32)],
        compiler_params=_params("arbitrary"),
    )(a, p, p, pool_w, pool_scale)


def even_mix_bwd(dy, a, p, pool_w, pool_scale, name, rc=64):
    s = p.shape[0]
    ng = len(POOL_WINDOWS)
    cw = pool_w.shape[1]
    n_chunks = s // rc

    def body(dy_ref, a_ref, u_ref, g_ref, w_ref, sc_ref, da_ref, du_ref, dg_ref, dw_ref, dsc_ref,
             upad, rpad, dpl, dw_acc, dsc_acc):
        j = pl.program_id(0)

        @pl.when(j < ng)
        def _():
            def chunk(ci, carry):
                rows = pl.ds(pl.multiple_of(ci * rc, rc), rc)
                dyv = dy_ref[rows, :].astype(F32)
                sg, dsg = _silu_and_grad(g_ref[rows, :].astype(F32))
                da_ref[rows, :] = (dyv * sg).astype(BF16)
                dg_ref[rows, :] = (dyv * a_ref[rows, :] * dsg).astype(BF16)
                return carry

            lax.fori_loop(0, n_chunks, chunk, 0)

        for gi, win in enumerate(POOL_WINDOWS):
            @pl.when(j == ng + gi)
            def _(win=win):
                upad[0:HALO, :] = jnp.zeros((HALO, cw), F32)
                rpad[s:s + HALO, :] = jnp.zeros((HALO, cw), F32)
                dw_acc[...] = jnp.zeros_like(dw_acc)
                dsc_acc[...] = jnp.zeros_like(dsc_acc)

                def fill(ci, carry):
                    r0 = pl.multiple_of(ci * rc, rc)
                    upad[pl.ds(pl.multiple_of(r0 + HALO, HALO), rc), :] = u_ref[pl.ds(r0, rc), :].astype(F32)
                    return carry

                lax.fori_loop(0, n_chunks, fill, 0)

                def chunk(ci, carry):
                    r0 = pl.multiple_of(ci * rc, rc)
                    rows = pl.ds(r0, rc)
                    pooled, inv = _pool_window(upad[pl.ds(r0, HALO + rc), :], win, r0, rc)
                    pb = pooled.astype(BF16)
                    wv = w_ref[0]
                    t = jnp.dot(pb, wv, preferred_element_type=F32)
                    scv = sc_ref[...]
                    dyv = dy_ref[rows, :].astype(F32)
                    sg, dsg = _silu_and_grad(g_ref[rows, :].astype(F32))
                    dpo = dyv * sg
                    dg_ref[rows, :] = (dyv * t * scv * dsg).astype(BF16)
                    dsc_acc[...] += _rowsum8(dpo * t)
                    dtb = (dpo * scv).astype(BF16)
                    dw_acc[...] += lax.dot_general(pb, dtb, (((0,), (0,)), ((), ())),
                                                   preferred_element_type=F32)
                    dpooled = lax.dot_general(dtb, wv, (((1,), (1,)), ((), ())),
                                              preferred_element_type=F32)
                    dpl[rows, :] = dpooled
                    rpad[rows, :] = dpooled * inv
                    return carry

                lax.fori_loop(0, n_chunks, chunk, 0)

                def chunk2(ci, carry):
                    r0 = pl.multiple_of(ci * rc, rc)
                    rows = pl.ds(r0, rc)
                    xx = rpad[pl.ds(r0, rc + HALO), :]
                    fs = xx[0:rc]
                    for i in range(1, win):
                        fs = fs + _tap_after(xx, i, rc)
                    du_ref[rows, :] = (fs - dpl[rows, :]).astype(BF16)
                    return carry

                lax.fori_loop(0, n_chunks, chunk2, 0)
                dw_ref[0] = dw_acc[...]
                dsc_ref[...] = jnp.sum(dsc_acc[...], axis=0, keepdims=True)

    grp = lambda j: jnp.maximum(j - ng, 0)
    att = lambda j: jnp.minimum(j, ng - 1)
    return pl.pallas_call(
        body, name=name, grid=(2 * ng,),
        in_specs=[pl.BlockSpec((s, cw), lambda j: (0, j)),
                  pl.BlockSpec((s, cw), lambda j: (0, att(j))),
                  pl.BlockSpec((s, cw), lambda j: (0, 3 * ng + grp(j))),
                  pl.BlockSpec((s, cw), lambda j: (0, 4 * ng + j)),
                  pl.BlockSpec((1, cw, cw), lambda j: (grp(j), 0, 0)),
                  pl.BlockSpec((1, cw), lambda j: (0, grp(j)))],
        out_specs=[pl.BlockSpec((s, cw), lambda j: (0, att(j))),
                   pl.BlockSpec((s, cw), lambda j: (0, grp(j))),
                   pl.BlockSpec((s, cw), lambda j: (0, j)),
                   pl.BlockSpec((1, cw, cw), lambda j: (grp(j), 0, 0)),
                   pl.BlockSpec((1, cw), lambda j: (0, grp(j)))],
        out_shape=[jax.ShapeDtypeStruct((s, ng * cw), BF16), jax.ShapeDtypeStruct((s, ng * cw), BF16),
                   jax.ShapeDtypeStruct((s, 2 * ng * cw), BF16),
                   jax.ShapeDtypeStruct((ng, cw, cw), F32), jax.ShapeDtypeStruct((1, ng * cw), F32)],
        scratch_shapes=[pltpu.VMEM((HALO + s, cw), F32), pltpu.VMEM((s + HALO, cw), F32),
                        pltpu.VMEM((s, cw), F32), pltpu.VMEM((cw, cw), F32), pltpu.VMEM((8, cw), F32)],
        compiler_params=_params("arbitrary"),
    )(dy, a, p, p, pool_w, pool_scale)


def _halo_before(tm):
    return lambda i: jnp.maximum(i * (tm // HALO) - 1, 0)


def _halo_after(tm, s):
    return lambda i: jnp.minimum((i + 1) * (tm // HALO), s // HALO - 1)


def odd_mix_fwd(p, sconv_w, dconv_w, dconv_b, cnorm_g, cnorm_b, name, tm=128):
    s = p.shape[0]
    cw = sconv_w.shape[1]
    n = s // tm
    lanes = 128
    hb = _halo_before(tm)

    def body(hc_ref, hch_ref, bc_ref, cc_ref, cch_ref, ga_ref, gah_ref, gb_ref, gbh_ref, g1_ref, g2_ref,
             sw_ref, dw_ref, db_ref, gam_ref, bet_ref, y_ref, dc_ref):
        first = pl.program_id(0) == 0
        for l in range(cw // lanes):
            cols = slice(l * lanes, (l + 1) * lanes)
            mh = jnp.where(first, 0.0, cch_ref[:, cols].astype(F32) * hch_ref[:, cols].astype(F32))
            mm = cc_ref[:, cols].astype(F32) * hc_ref[:, cols].astype(F32)
            xx = jnp.concatenate([mh, mm], axis=0)
            cv = jnp.zeros((tm, lanes), F32)
            for k in range(SCONV_K):
                cv = cv + sw_ref[k:k + 1, cols] * _tap_before(xx, SCONV_K - 1 - k, tm)
            c_out = bc_ref[:, cols].astype(F32) * cv
            y_ref[:, cols] = (c_out * _silu(g1_ref[:, cols].astype(F32))).astype(BF16)
            dh = jnp.where(first, 0.0, gah_ref[:, cols].astype(F32) * _sigmoid(gbh_ref[:, cols].astype(F32)))
            dm = ga_ref[:, cols].astype(F32) * _sigmoid(gb_ref[:, cols].astype(F32))
            xx = jnp.concatenate([dh, dm], axis=0)
            acc = jnp.zeros((tm, lanes), F32) + db_ref[:, cols]
            for k in range(CONF_K):
                acc = acc + dw_ref[k:k + 1, cols] * _tap_before(xx, CONF_K - 1 - k, tm)
            dc_ref[:, cols] = acc
        rs = 32
        for r in range(tm // rs):
            rows = slice(r * rs, (r + 1) * rs)
            xv = dc_ref[rows, :]
            mu = jnp.mean(xv, axis=-1, keepdims=True)
            xc = xv - mu
            rstd = lax.rsqrt(jnp.mean(xc * xc, axis=-1, keepdims=True) + EPS)
            ln = xc * rstd * gam_ref[...] + bet_ref[...]
            y_ref[rows, cw:2 * cw] = (_silu(ln) * _silu(g2_ref[rows, :].astype(F32))).astype(BF16)

    main = lambda c: pl.BlockSpec((tm, cw), lambda i: (i, c))
    halo = lambda c: pl.BlockSpec((HALO, cw), lambda i: (hb(i), c))
    vec = lambda r: pl.BlockSpec((r, cw), lambda i: (0, 0))
    return pl.pallas_call(
        body, name=name, grid=(n,),
        in_specs=[main(0), halo(0), main(1), main(2), halo(2), main(3), halo(3), main(4), halo(4),
                  main(5), main(6), vec(SCONV_K), vec(CONF_K), vec(1), vec(1), vec(1)],
        out_specs=[pl.BlockSpec((tm, 2 * cw), lambda i: (i, 0)), pl.BlockSpec((tm, cw), lambda i: (i, 0))],
        out_shape=[jax.ShapeDtypeStruct((s, 2 * cw), BF16), jax.ShapeDtypeStruct((s, cw), F32)],
        compiler_params=_params("parallel"),
    )(p, p, p, p, p, p, p, p, p, p, p, sconv_w, dconv_w, dconv_b, cnorm_g, cnorm_b)


def odd_bwd_ln(dy, p, dc, cnorm_g, cnorm_b, name, tm=256):
    s = p.shape[0]
    cw = dc.shape[1]
    n = s // tm
    rs = 32

    def body(dy_ref, g2_ref, dc_ref, gam_ref, bet_ref, ddc_ref, dg_ref, dgam_ref, dbet_ref, gacc, bacc):
        i = pl.program_id(0)

        @pl.when(i == 0)
        def _():
            gacc[...] = jnp.zeros_like(gacc)
            bacc[...] = jnp.zeros_like(bacc)

        def chunk(ci, carry):
            rows = pl.ds(pl.multiple_of(ci * rs, rs), rs)
            xv = dc_ref[rows, :]
            mu = jnp.mean(xv, axis=-1, keepdims=True)
            xc = xv - mu
            rstd = lax.rsqrt(jnp.mean(xc * xc, axis=-1, keepdims=True) + EPS)
            xh = xc * rstd
            gam = gam_ref[...]
            sl, dsl = _silu_and_grad(xh * gam + bet_ref[...])
            sg, dsg = _silu_and_grad(g2_ref[rows, :].astype(F32))
            dyv = dy_ref[rows, :].astype(F32)
            dg_ref[rows, :] = (dyv * sl * dsg).astype(BF16)
            dln = dyv * sg * dsl
            gacc[...] += _rowsum8(dln * xh)
            bacc[...] += _rowsum8(dln)
            dxh = dln * gam
            ddc_ref[rows, :] = rstd * (dxh - jnp.mean(dxh, axis=-1, keepdims=True)
                                       - xh * jnp.mean(dxh * xh, axis=-1, keepdims=True))
            return carry

        lax.fori_loop(0, tm // rs, chunk, 0)

        @pl.when(i == n - 1)
        def _():
            dgam_ref[...] = jnp.sum(gacc[...], axis=0, keepdims=True)
            dbet_ref[...] = jnp.sum(bacc[...], axis=0, keepdims=True)

    vec = pl.BlockSpec((1, cw), lambda i: (0, 0))
    return pl.pallas_call(
        body, name=name, grid=(n,),
        in_specs=[pl.BlockSpec((tm, cw), lambda i: (i, 1)), pl.BlockSpec((tm, cw), lambda i: (i, 6)),
                  pl.BlockSpec((tm, cw), lambda i: (i, 0)), vec, vec],
        out_specs=[pl.BlockSpec((tm, cw), lambda i: (i, 0)), pl.BlockSpec((tm, cw), lambda i: (i, 0)), vec, vec],
        out_shape=[jax.ShapeDtypeStruct((s, cw), F32), jax.ShapeDtypeStruct((s, cw), BF16),
                   jax.ShapeDtypeStruct((1, cw), F32), jax.ShapeDtypeStruct((1, cw), F32)],
        scratch_shapes=[pltpu.VMEM((8, cw), F32), pltpu.VMEM((8, cw), F32)],
        compiler_params=_params("arbitrary"),
    )(dy, p, dc, cnorm_g, cnorm_b)


def odd_bwd_conv(dy, p, ddc, dg2, sconv_w, dconv_w, name, tm=128):
    s = p.shape[0]
    cw = ddc.shape[1]
    n = s // tm
    lanes = 128
    hb = _halo_before(tm)
    ha = _halo_after(tm, s)

    def body(dy_ref, dya_ref, g1_ref, g1a_ref, bc_ref, bca_ref, hc_ref, hch_ref, cc_ref, cch_ref,
             ddc_ref, ddca_ref, ga_ref, gah_ref, gb_ref, gbh_ref, dg2_ref, sw_ref, dw_ref,
             dp_ref, dsw_ref, ddw_ref, ddb_ref, sw_acc, dw_acc, db_acc):
        i = pl.program_id(0)
        first = i == 0
        last = i == n - 1

        @pl.when(first)
        def _():
            sw_acc[...] = jnp.zeros_like(sw_acc)
            dw_acc[...] = jnp.zeros_like(dw_acc)
            db_acc[...] = jnp.zeros_like(db_acc)

        for l in range(cw // lanes):
            cols = slice(l * lanes, (l + 1) * lanes)
            mh = jnp.where(first, 0.0, cch_ref[:, cols].astype(F32) * hch_ref[:, cols].astype(F32))
            hcv = hc_ref[:, cols].astype(F32)
            ccv = cc_ref[:, cols].astype(F32)
            xx = jnp.concatenate([mh, ccv * hcv], axis=0)
            taps = [_tap_before(xx, SCONV_K - 1 - k, tm) for k in range(SCONV_K)]
            cv = jnp.zeros((tm, lanes), F32)
            for k in range(SCONV_K):
                cv = cv + sw_ref[k:k + 1, cols] * taps[k]
            bcv = bc_ref[:, cols].astype(F32)
            dyv = dy_ref[:, cols].astype(F32)
            sg, dsg = _silu_and_grad(g1_ref[:, cols].astype(F32))
            dco = dyv * sg
            dp_ref[:, 5 * cw + l * lanes:5 * cw + (l + 1) * lanes] = (dyv * bcv * cv * dsg).astype(BF16)
            dp_ref[:, cw + l * lanes:cw + (l + 1) * lanes] = (dco * cv).astype(BF16)
            dcv = dco * bcv
            for k in range(SCONV_K):
                sw_acc[k * 8:(k + 1) * 8, cols] += _rowsum8(dcv * taps[k])
            dcv_a = jnp.where(last, 0.0, dya_ref[:, cols].astype(F32) * _silu(g1a_ref[:, cols].astype(F32))
                              * bca_ref[:, cols].astype(F32))
            xx = jnp.concatenate([dcv, dcv_a], axis=0)
            dm = jnp.zeros((tm, lanes), F32)
            for k in range(SCONV_K):
                dm = dm + sw_ref[k:k + 1, cols] * _tap_after(xx, SCONV_K - 1 - k, tm)
            dp_ref[:, l * lanes:(l + 1) * lanes] = (dm * ccv).astype(BF16)
            dp_ref[:, 2 * cw + l * lanes:2 * cw + (l + 1) * lanes] = (dm * hcv).astype(BF16)
            gav = ga_ref[:, cols].astype(F32)
            sb = _sigmoid(gb_ref[:, cols].astype(F32))
            dh = jnp.where(first, 0.0, gah_ref[:, cols].astype(F32) * _sigmoid(gbh_ref[:, cols].astype(F32)))
            xx = jnp.concatenate([dh, gav * sb], axis=0)
            ddcv = ddc_ref[:, cols]
            db_acc[:, cols] += _rowsum8(ddcv)
            for k in range(CONF_K):
                dw_acc[k * 8:(k + 1) * 8, cols] += _rowsum8(ddcv * _tap_before(xx, CONF_K - 1 - k, tm))
            ddc_a = jnp.where(last, 0.0, ddca_ref[:, cols])
            xx = jnp.concatenate([ddcv, ddc_a], axis=0)
            dgl = jnp.zeros((tm, lanes), F32)
            for k in range(CONF_K):
                dgl = dgl + dw_ref[k:k + 1, cols] * _tap_after(xx, CONF_K - 1 - k, tm)
            dp_ref[:, 3 * cw + l * lanes:3 * cw + (l + 1) * lanes] = (dgl * sb).astype(BF16)
            dp_ref[:, 4 * cw + l * lanes:4 * cw + (l + 1) * lanes] = (dgl * gav * sb * (1.0 - sb)).astype(BF16)
        dp_ref[:, 6 * cw:7 * cw] = dg2_ref[...]

        @pl.when(last)
        def _():
            for k in range(SCONV_K):
                dsw_ref[k:k + 1, :] = jnp.sum(sw_acc[k * 8:(k + 1) * 8, :], axis=0, keepdims=True)
            for k in range(CONF_K):
                ddw_ref[k:k + 1, :] = jnp.sum(dw_acc[k * 8:(k + 1) * 8, :], axis=0, keepdims=True)
            ddb_ref[...] = jnp.sum(db_acc[...], axis=0, keepdims=True)

    def main(c):
        return pl.BlockSpec((tm, cw), lambda i: (i, c))

    def before(c):
        return pl.BlockSpec((HALO, cw), lambda i: (hb(i), c))

    def after(c):
        return pl.BlockSpec((HALO, cw), lambda i: (ha(i), c))

    def vec(r):
        return pl.BlockSpec((r, cw), lambda i: (0, 0))

    return pl.pallas_call(
        body, name=name, grid=(n,),
        in_specs=[main(0), after(0), main(5), after(5), main(1), after(1), main(0), before(0), main(2), before(2),
                  main(0), after(0), main(3), before(3), main(4), before(4), main(0), vec(SCONV_K), vec(CONF_K)],
        out_specs=[pl.BlockSpec((tm, 7 * cw), lambda i: (i, 0)), vec(SCONV_K), vec(CONF_K), vec(1)],
        out_shape=[jax.ShapeDtypeStruct((s, 7 * cw), BF16), jax.ShapeDtypeStruct((SCONV_K, cw), F32),
                   jax.ShapeDtypeStruct((CONF_K, cw), F32), jax.ShapeDtypeStruct((1, cw), F32)],
        scratch_shapes=[pltpu.VMEM((8 * SCONV_K, cw), F32), pltpu.VMEM((8 * CONF_K, cw), F32),
                        pltpu.VMEM((8, cw), F32)],
        compiler_params=_params("arbitrary"),
    )(dy, dy, p, p, p, p, p, p, p, p, ddc, ddc, p, p, p, p, dg2, sconv_w, dconv_w)


_ANY = pl.BlockSpec(memory_space=pl.ANY)


def _place():
    return lax.axis_index("x"), lax.axis_index("y"), lax.axis_index("c")


def all_gather(arrs, name, deps=()):
    n = len(arrs)

    def body(*refs):
        ins, outs = refs[:n], refs[n + len(deps):2 * n + len(deps)]
        send_sems, recv_sems, local_sems = refs[-3:]
        x, y, c = _place()
        me, sibling = (x, y, c), (x, y, 1 - c)
        chips = [(1 - x, y), (x, 1 - y), (1 - x, 1 - y)]

        def copy(a, k, block, to, src=None):
            px, py, pc = block
            dst = outs[a].at[4 * px + 2 * py + pc]
            return pltpu.make_async_remote_copy(
                src_ref=dst if src is None else src, dst_ref=dst,
                send_sem=send_sems.at[7 * a + k], recv_sem=recv_sems.at[7 * a + k],
                device_id=to, device_id_type=MESH)

        mine = [pltpu.make_async_copy(ins[a], outs[a].at[4 * x + 2 * y + c], local_sems.at[a]) for a in range(n)]
        for cp in mine:
            cp.start()
        first = []
        for a in range(n):
            first.append(copy(a, 0, me, sibling, src=ins[a]))
            first += [copy(a, 1 + j, me, (*chip, c), src=ins[a]) for j, chip in enumerate(chips)]
        for cp in first:
            cp.start()
        passed = []
        for a in range(n):
            for j, chip in enumerate(chips):
                copy(a, 1 + j, (*chip, c), me).wait_recv()
                cp = copy(a, 4 + j, (*chip, c), sibling)
                cp.start()
                passed.append(cp)
        for a in range(n):
            copy(a, 0, sibling, me).wait_recv()
            for j, chip in enumerate(chips):
                copy(a, 4 + j, (*chip, 1 - c), me).wait_recv()
        for cp in first + passed:
            cp.wait_send()
        for cp in mine:
            cp.wait()

    return pl.pallas_call(
        body, name=name,
        out_shape=[jax.ShapeDtypeStruct((N_DEV,) + a.shape, a.dtype) for a in arrs],
        in_specs=[_ANY] * (n + len(deps)), out_specs=[_ANY] * n,
        scratch_shapes=[pltpu.SemaphoreType.DMA((7 * n,)), pltpu.SemaphoreType.DMA((7 * n,)),
                        pltpu.SemaphoreType.DMA((n,))],
    )(*arrs, *deps)


_HBM = pl.BlockSpec(memory_space=pltpu.HBM)
_SEM = pl.BlockSpec(memory_space=pltpu.SEMAPHORE)
_DATAFLOW = pltpu.SideEffectType.DATAFLOW_SIDE_EFFECTING


def _peers_per_array(kind):
    return 1 if kind == "sibling" else 3


def _split_copies(kind, srcs, lands, send_sems, recv_sems):
    x, y, c = _place()
    per = _peers_per_array(kind)
    out = []
    for a in range(len(srcs)):
        if kind == "sibling":
            peers = [((x, y, 1 - c), srcs[a].at[:, pl.ds(1 - c, 1)], lands[a], lands[a])]
        else:
            peers = []
            for px, py in [(1 - x, y), (x, 1 - y), (1 - x, 1 - y)]:
                if kind == "gather":
                    views = (srcs[a], lands[a].at[4 * x + 2 * y + c], lands[a].at[4 * px + 2 * py + c])
                else:
                    views = (srcs[a].at[2 * px + py], lands[a].at[2 * x + y], lands[a].at[2 * px + py])
                peers.append(((px, py, c),) + views)
        for j, (peer, src, dst, arrives) in enumerate(peers):
            sems = dict(send_sem=send_sems.at[per * a + j], recv_sem=recv_sems.at[per * a + j],
                        device_id=peer, device_id_type=MESH)
            out.append((pltpu.make_async_remote_copy(src_ref=src, dst_ref=dst, **sems),
                        pltpu.make_async_remote_copy(src_ref=src, dst_ref=arrives, **sems)))
    return out


def split_start(kind, srcs, lands, deps, name):
    n = len(srcs)
    n_sems = _peers_per_array(kind) * n

    def body(*refs):
        send_sems, recv_sems = refs[2 * n + len(deps)], refs[2 * n + len(deps) + 1]
        for copy, _ in _split_copies(kind, refs[:n], refs[n:2 * n], send_sems, recv_sems):
            copy.start()
        token = refs[-1]
        token[...] = jnp.zeros_like(token)

    held = [pltpu.HBM(a.shape, a.dtype) for a in list(srcs) + list(lands)]
    outs = pl.pallas_call(
        body, name=name,
        out_shape=(pltpu.SemaphoreType.DMA((n_sems,)), pltpu.SemaphoreType.DMA((n_sems,)), *held,
                   jax.ShapeDtypeStruct((8, 128), F32)),
        in_specs=[_HBM] * (2 * n) + [_ANY] * len(deps),
        out_specs=(_SEM, _SEM, *([_HBM] * (2 * n)), pl.BlockSpec(memory_space=pltpu.VMEM)),
        input_output_aliases={i: 2 + i for i in range(2 * n)},
        compiler_params=pltpu.CompilerParams(has_side_effects=_DATAFLOW),
    )(*[pltpu.with_memory_space_constraint(a, pltpu.HBM) for a in list(srcs) + list(lands)], *deps)
    return outs[0], outs[1], list(outs[2:2 + n]), list(outs[2 + n:2 + 2 * n]), outs[-1]


def split_wait(kind, send_sems, recv_sems, srcs, lands, afters, name):
    n = len(srcs)

    def body(*refs):
        for _, arrival in _split_copies(kind, refs[:n], refs[n:2 * n], refs[2 * n], refs[2 * n + 1]):
            arrival.wait_send()
            arrival.wait_recv()

    outs = pl.pallas_call(
        body, name=name,
        out_shape=[pltpu.HBM(a.shape, a.dtype) for a in list(srcs) + list(lands)],
        in_specs=[_HBM] * (2 * n) + [_SEM, _SEM] + [_ANY] * len(afters),
        out_specs=[_HBM] * (2 * n),
        input_output_aliases={i: i for i in range(2 * n)},
        compiler_params=pltpu.CompilerParams(has_side_effects=_DATAFLOW),
    )(*srcs, *lands, send_sems, recv_sems, *afters)
    return list(outs[:n]), list(outs[n:])


def place_block(land, block, dev, name):
    r, c = block.shape
    tr = min(r, 512)

    def body(dev_ref, land_ref, b_ref, o_ref):
        del dev_ref, land_ref
        o_ref[...] = b_ref[...]

    return pl.pallas_call(
        body, name=name,
        grid_spec=pltpu.PrefetchScalarGridSpec(
            num_scalar_prefetch=1, grid=(r // tr,),
            in_specs=[_ANY, pl.BlockSpec((tr, c), lambda i, dev_ref: (i, 0))],
            out_specs=pl.BlockSpec((None, tr, c), lambda i, dev_ref: (dev_ref[0], i, 0))),
        out_shape=jax.ShapeDtypeStruct(land.shape, land.dtype),
        input_output_aliases={1: 0},
        compiler_params=_params("parallel"),
    )(dev, land, block)


def gather_finish(lands, name):
    n = len(lands)

    def body(*refs):
        outs = refs[n:2 * n]
        send_sems, recv_sems = refs[2 * n:]
        x, y, c = _place()
        cps = [pltpu.make_async_remote_copy(
            src_ref=outs[a].at[:, pl.ds(c, 1)], dst_ref=outs[a].at[:, pl.ds(c, 1)],
            send_sem=send_sems.at[a], recv_sem=recv_sems.at[a],
            device_id=(x, y, 1 - c), device_id_type=MESH) for a in range(n)]
        for cp in cps:
            cp.start()
        for cp in cps:
            cp.wait()

    return pl.pallas_call(
        body, name=name,
        out_shape=[jax.ShapeDtypeStruct(a.shape, a.dtype) for a in lands],
        in_specs=[_ANY] * n, out_specs=[_ANY] * n,
        input_output_aliases={i: i for i in range(n)},
        scratch_shapes=[pltpu.SemaphoreType.DMA((n,)), pltpu.SemaphoreType.DMA((n,))],
    )(*lands)


def pair_add(own, recv, core, name):
    _, _, r, c = own.shape
    tr = min(r, 512)

    def body(core_ref, own_ref, recv_ref, o_ref):
        del core_ref
        o_ref[...] = (own_ref[...].astype(F32) + recv_ref[...].astype(F32)).astype(BF16)

    return pl.pallas_call(
        body, name=name,
        grid_spec=pltpu.PrefetchScalarGridSpec(
            num_scalar_prefetch=1, grid=(4, r // tr),
            in_specs=[pl.BlockSpec((None, None, tr, c), lambda k, i, core_ref: (k, core_ref[0], i, 0)),
                      pl.BlockSpec((None, None, tr, c), lambda k, i, core_ref: (k, 0, i, 0))],
            out_specs=pl.BlockSpec((None, tr, c), lambda k, i, core_ref: (k, i, 0))),
        out_shape=jax.ShapeDtypeStruct((4, r, c), BF16),
        compiler_params=_params("parallel", "parallel"),
    )(core, own, recv)


def _adamw_math(w, g, m, v):
    m2 = ADAM_B1 * m + (1.0 - ADAM_B1) * g
    v2 = ADAM_B2 * v + (1.0 - ADAM_B2) * (g * g)
    m_hat = m2 / (1.0 - ADAM_B1 ** ADAM_STEP)
    v_hat = v2 / (1.0 - ADAM_B2 ** ADAM_STEP)
    delta = -ADAM_LR * (m_hat / (jnp.sqrt(v_hat) + ADAM_EPS) + ADAM_WD * w)
    return delta, m2, v2


def adamw_big(w, m, v, own, got, chip, name):
    r, c = w.shape
    tr = min(r, 256)

    def body(chip_ref, w_ref, m_ref, v_ref, p0, p1, p2, p3, g_ref, d_ref, m2_ref, v2_ref):
        del chip_ref
        g = ((p0[...].astype(F32) + p1[...].astype(F32)) + p2[...].astype(F32)) + p3[...].astype(F32)
        delta, m2, v2 = _adamw_math(w_ref[...], g, m_ref[...], v_ref[...])
        g_ref[...] = g
        d_ref[...] = delta
        m2_ref[...] = m2
        v2_ref[...] = v2

    row = pl.BlockSpec((tr, c), lambda i, chip_ref: (i, 0))

    def slab(flip):
        return pl.BlockSpec((None, tr, c), lambda i, chip_ref: (chip_ref[0] ^ flip, i, 0))

    return pl.pallas_call(
        body, name=name,
        grid_spec=pltpu.PrefetchScalarGridSpec(
            num_scalar_prefetch=1, grid=(r // tr,),
            in_specs=[row, row, row, slab(0), slab(1), slab(2), slab(3)],
            out_specs=[row] * 4),
        out_shape=[jax.ShapeDtypeStruct((r, c), F32)] * 4,
        compiler_params=_params("parallel"),
    )(chip, w, m, v, own, got, got, got)


def sum_devices(g8, name):
    def body(g_ref, o_ref):
        tot = g_ref[0]
        for k in range(1, N_DEV):
            tot = tot + g_ref[k]
        o_ref[...] = tot

    return pl.pallas_call(body, name=name, out_shape=jax.ShapeDtypeStruct(g8.shape[1:], F32))(g8)


def adamw_small(ws, gs, ms, vs, name):
    n = len(ws)

    def body(*refs):
        w_r, g_r, m_r, v_r = refs[:n], refs[n:2 * n], refs[2 * n:3 * n], refs[3 * n:4 * n]
        d_o, m_o, v_o = refs[4 * n:5 * n], refs[5 * n:6 * n], refs[6 * n:7 * n]
        for k in range(n):
            delta, m2, v2 = _adamw_math(w_r[k][...], g_r[k][...], m_r[k][...], v_r[k][...])
            d_o[k][...] = delta
            m_o[k][...] = m2
            v_o[k][...] = v2

    shapes = [jax.ShapeDtypeStruct(w.shape, F32) for w in ws]
    outs = pl.pallas_call(body, name=name, out_shape=shapes * 3)(*ws, *gs, *ms, *vs)
    return outs[:n], outs[n:2 * n], outs[2 * n:]


def _rows128(a):
    return a.reshape(-1, 128)


def _pad_rows(a, rows):
    return jnp.pad(a, ((0, rows - a.shape[0]), (0, 0)))


def kernel(x, ln_pre_even, w_in_even, pool_w, pool_scale, w_out_even, ln_post_even, ln_pre_odd, w_in_odd, sconv_w, dconv_w, dconv_b, cnorm_g, cnorm_b, w_out_odd, ln_post_odd, loss_target, m_ln_pre_even, m_w_in_even, m_pool_w, m_pool_scale, m_w_out_even, m_ln_post_even, m_ln_pre_odd, m_w_in_odd, m_sconv_w, m_dconv_w, m_dconv_b, m_cnorm_g, m_cnorm_b, m_w_out_odd, m_ln_post_odd, v_ln_pre_even, v_w_in_even, v_pool_w, v_pool_scale, v_w_out_even, v_ln_post_even, v_ln_pre_odd, v_w_in_odd, v_sconv_w, v_dconv_w, v_dconv_b, v_cnorm_g, v_cnorm_b, v_w_out_odd, v_ln_post_odd):
    xs = x[0]
    tgt = loss_target[0]
    s, d = xs.shape
    half = d // 2
    n_heads = half // HEAD_DIM
    ng = len(POOL_WINDOWS)
    cwp = half // ng
    dev = 4 * lax.axis_index("x") + 2 * lax.axis_index("y") + lax.axis_index("c")
    core = lax.axis_index("c").astype(jnp.int32).reshape(1)

    pr = pool_w.shape[2]
    cl = sconv_w.shape[2]
    small_parts = [(_rows128(ln_pre_odd), 8), (sconv_w[0], 8), (dconv_w[0], 32), (dconv_b, 8),
                   (cnorm_g, 8), (cnorm_b, 8), (_rows128(ln_post_odd), 8)]
    small_local = jnp.concatenate([_pad_rows(a, r) for a, r in small_parts], axis=0)
    g_wie, g_pw, g_small = all_gather(
        [w_in_even[0].astype(BF16), pool_w[0].reshape(ng * pr, cwp).astype(BF16), small_local], "ag_first")
    comm = _Exchanges(dev, core, d)
    in_proj_dep = comm.start_weights([w_out_even[0].astype(BF16), w_in_odd[0].astype(BF16),
                                      w_out_odd[0].astype(BF16)], after=g_wie)
    pool_full = g_pw.reshape(N_DEV, ng, pr, cwp).transpose(1, 0, 2, 3).reshape(ng, cwp, cwp)
    nl = ln_pre_odd.shape[1] // 128

    def chan(lo, rows):
        return g_small[:, lo:lo + rows].transpose(1, 0, 2).reshape(rows, N_DEV * cl)

    ln_pre_odd_f = g_small[:, 0:nl].reshape(1, d)
    sconv_f = chan(8, SCONV_K)
    dconv_f = chan(16, CONF_K)
    dconv_b_f = chan(48, 1)
    cnorm_g_f = chan(56, 1)
    cnorm_b_f = chan(64, 1)
    ln_post_odd_f = g_small[:, 72:72 + nl].reshape(1, d)

    loss_blk, grad_x, small_g = _fwd_bwd(
        xs, tgt, ln_pre_even, g_wie, pool_full, pool_scale, ln_post_even, ln_pre_odd_f,
        sconv_f, dconv_f, dconv_b_f, cnorm_g_f, cnorm_b_f, ln_post_odd_f, comm, in_proj_dep)
    small_w = [ln_pre_even, pool_scale, ln_post_even, ln_pre_odd, sconv_w[0], dconv_w[0], dconv_b, cnorm_g, cnorm_b, ln_post_odd]
    small_m = [m_ln_pre_even, m_pool_scale, m_ln_post_even, m_ln_pre_odd, m_sconv_w[0], m_dconv_w[0], m_dconv_b, m_cnorm_g, m_cnorm_b, m_ln_post_odd]
    small_v = [v_ln_pre_even, v_pool_scale, v_ln_post_even, v_ln_pre_odd, v_sconv_w[0], v_dconv_w[0], v_dconv_b, v_cnorm_g, v_cnorm_b, v_ln_post_odd]
    big = {"w_in_even": (w_in_even, m_w_in_even, v_w_in_even), "pool_w": (pool_w, m_pool_w, v_pool_w),
           "w_out_even": (w_out_even, m_w_out_even, v_w_out_even), "w_in_odd": (w_in_odd, m_w_in_odd, v_w_in_odd),
           "w_out_odd": (w_out_odd, m_w_out_odd, v_w_out_odd)}
    upd = comm.finish_updates(big, [grad_x])
    upd.update(comm.finish_updates(big, [grad_x]))
    sg, sd, sm, sv, loss = _update_small(small_g, loss_blk, small_w, small_m, small_v, dev, d, cl,
                                         deps=[upd["w_in_odd"][1], upd["w_out_even"][1]])
    upd.update(comm.finish_updates(big, sd))
    (g_wie_o, d_wie, m_wie, v_wie), (g_pw_o, d_pw, m_pw, v_pw) = upd["w_in_even"], upd["pool_w"]
    (g_woe_o, d_woe, m_woe, v_woe), (g_wio_o, d_wio, m_wio, v_wio) = upd["w_out_even"], upd["w_in_odd"]
    g_woo_o, d_woo, m_woo, v_woo = upd["w_out_odd"]

    def order(small, wie, pw, woe, wio, woo):
        return [small[0], wie, pw, small[1], woe, small[2], small[3], wio, small[4], small[5], small[6],
                small[7], small[8], woo, small[9]]

    grads = order(sg, g_wie_o, g_pw_o, g_woe_o, g_wio_o, g_woo_o)
    deltas = order(sd, d_wie, d_pw, d_woe, d_wio, d_woo)
    new_m = order(sm, m_wie, m_pw, m_woe, m_wio, m_woo)
    new_v = order(sv, v_wie, v_pw, v_woe, v_wio, v_woo)
    return (loss, grad_x[None], *grads, *deltas, *new_m, *new_v)


def _fwd_bwd(xs, tgt, ln_pre_even, g_wie, pool_full, pool_scale, ln_post_even, ln_pre_odd_f,
             sconv_f, dconv_f, dconv_b_f, cnorm_g_f, cnorm_b_f, ln_post_odd_f, comm, in_proj_dep):
    d = xs.shape[1]
    n_heads = d // 2 // HEAD_DIM
    ng, cwp = pool_full.shape[0], pool_full.shape[1]
    h0 = rms_fwd(xs, ln_pre_even, "rms_pre_even")
    p0 = mm_nn(h0, g_wie, BF16, "in_proj_even", dep=in_proj_dep)
    a0, sb_wts = sb_fwd(p0, n_heads, "sb_fwd")
    y0 = even_mix_fwd(a0, p0, pool_full, pool_scale, "even_mix_fwd")
    w_out_e, g_wio, w_out_o = comm.rest_of_weights(after=y0)
    o0 = mm_nn(y0, w_out_e, F32, "out_proj_even", tn=512)
    x1 = postnorm_fwd(xs, o0, ln_post_even, "post_even")
    h1 = rms_fwd(x1, ln_pre_odd_f, "rms_pre_odd")
    p1 = mm_nn(h1, g_wio, BF16, "in_proj_odd")
    y1, dc = odd_mix_fwd(p1, sconv_f, dconv_f, dconv_b_f, cnorm_g_f, cnorm_b_f, "odd_mix_fwd")
    o1 = mm_nn(y1, w_out_o, F32, "out_proj_odd", tn=512)
    loss_blk, gx2, do1, dg_post_odd = final_fwd_bwd(x1, o1, ln_post_odd_f, tgt, "post_odd_loss")

    dw_out_o = mm_tn(y1, do1, 1, BF16, "dw_out_odd", tk=512)
    dy1 = mm_nt(do1, w_out_o, BF16, "dy_odd", tn=512)
    ddc, dg2, dgam, dbet = odd_bwd_ln(dy1, p1, dc, cnorm_g_f, cnorm_b_f, "odd_bwd_ln")
    dp1, dsconv, ddconv, ddconv_b = odd_bwd_conv(dy1, p1, ddc, dg2, sconv_f, dconv_f, "odd_bwd_conv")
    dw_in_o = mm_tn(h1, dp1, N_DEV, BF16, "dw_in_odd")
    dep = comm.reduce_begin({"w_out_odd": dw_out_o.reshape(N_DEV, d // N_DEV, d), "w_in_odd": dw_in_o}, "odd")
    dh1 = mm_nt(dp1, g_wio, F32, "dh_odd", dep=dep)
    dep = comm.reduce_send(after=dh1)
    gx1, dg_pre_odd = norm_bwd(dh1, x1, ln_pre_odd_f, gx2, F32, "pre_odd_bwd", dep=dep)

    do0, dg_post_even = norm_bwd(gx1, o0, ln_post_even, None, BF16, "post_even_bwd")
    dw_out_e = mm_tn(y0, do0, 1, BF16, "dw_out_even", tk=512)
    dy0 = mm_nt(do0, w_out_e, BF16, "dy_even", tn=512)
    da0, du0, dg0, dpool, dpool_scale = even_mix_bwd(dy0, a0, p0, pool_full, pool_scale, "even_mix_bwd")
    pr = cwp // N_DEV
    dpool_slabs = dpool.astype(BF16).reshape(ng, N_DEV, pr, cwp).transpose(1, 0, 2, 3).reshape(N_DEV, ng * pr, cwp)
    dep = comm.reduce_begin({"w_out_even": dw_out_e.reshape(N_DEV, d // N_DEV, d), "pool_w": dpool_slabs}, "even_out")
    dq0, dk0, dv0 = sb_bwd(p0, a0, sb_wts, da0, n_heads, "sb_bwd", dep=dep)
    dep = comm.reduce_send(after=dq0)
    dp0 = jnp.concatenate([dq0, dk0, dv0, du0, dg0], axis=1)
    dw_in_e = mm_tn(h0, dp0, N_DEV, BF16, "dw_in_even", dep=dep)
    dep = comm.reduce_begin({"w_in_even": dw_in_e}, "even_in")
    dh0 = mm_nt(dp0, g_wie, F32, "dh_even", dep=dep)
    dep = comm.reduce_send(after=dh0)
    grad_x, dg_pre_even = norm_bwd(dh0, xs, ln_pre_even, gx1, F32, "pre_even_bwd", dep=dep)
    small_g = [dg_pre_even, dpool_scale, dg_post_even, dg_pre_odd, dsconv, ddconv, ddconv_b, dgam, dbet, dg_post_odd]
    return loss_blk, grad_x, small_g


class _Exchanges:
    def __init__(self, dev, core, d):
        self.dev = dev.astype(jnp.int32).reshape(1)
        self.core = core
        self.chip = (dev // 2).astype(jnp.int32).reshape(1)
        self.d = d
        self.weights = None
        self.to_sibling = None
        self.pending = []

    def start_weights(self, blocks, after):
        lands = [lax.empty((N_DEV,) + b.shape, b.dtype) for b in blocks]
        send, recv, srcs, lands, token = split_start("gather", blocks, lands, [after], "ag_rest_start")
        self.weights = (send, recv, srcs, lands)
        return token

    def rest_of_weights(self, after):
        send, recv, srcs, lands = self.weights
        srcs, lands = split_wait("gather", send, recv, srcs, lands, [after], "ag_rest_wait")
        lands = [place_block(l, b, self.dev, "ag_rest_own_%d" % k) for k, (l, b) in enumerate(zip(lands, srcs))]
        full = gather_finish([l.reshape((4, 2) + l.shape[1:]) for l in lands], "ag_rest_finish")
        w_out_e, g_wio, w_out_o = [f.reshape((N_DEV,) + f.shape[2:]) for f in full]
        return w_out_e.reshape(1, self.d, self.d), g_wio, w_out_o.reshape(1, self.d, self.d)

    def reduce_begin(self, partials, tag):
        names = list(partials)
        arrs = [partials[k].reshape((4, 2) + partials[k].shape[1:]) for k in names]
        lands = [lax.empty((4, 1) + a.shape[2:], a.dtype) for a in arrs]
        send, recv, srcs, lands, token = split_start("sibling", arrs, lands, [], "rs_sibling_start_" + tag)
        self.to_sibling = (tag, names, send, recv, srcs, lands)
        return token

    def reduce_send(self, after):
        tag, names, send, recv, srcs, lands = self.to_sibling
        srcs, lands = split_wait("sibling", send, recv, srcs, lands, [after], "rs_sibling_wait_" + tag)
        sums = [pair_add(o, r, self.core, "rs_pair_add_" + k) for k, o, r in zip(names, srcs, lands)]
        zones = [lax.empty(a.shape, a.dtype) for a in sums]
        send, recv, srcs, zones, token = split_start("scatter", sums, zones, [], "rs_start_" + tag)
        self.pending.append((tag, names, send, recv, srcs, zones))
        return token

    def finish_updates(self, big, afters):
        tag, names, send, recv, srcs, lands = self.pending.pop(0)
        srcs, lands = split_wait("scatter", send, recv, srcs, lands, afters, "rs_wait_" + tag)
        out = {}
        for name, own, got in zip(names, srcs, lands):
            w, m, v = big[name]
            shp = own.shape[1:]
            outs = adamw_big(w.reshape(shp), m.reshape(shp), v.reshape(shp), own, got, self.chip, "adamw_" + name)
            out[name] = [o.reshape(w.shape) for o in outs]
        return out


def _update_small(small_g, loss_blk, small_w, small_m, small_v, dev, d, cl, deps):
    packed = jnp.concatenate([_rows128(g) for g in small_g] + [loss_blk], axis=0)
    (g8,) = all_gather([packed], "ag_small_grads", deps)
    tot = sum_devices(g8, "sum_small_grads")
    loss = tot[packed.shape[0] - 8, 0]
    full_g = []
    lo = 0
    for g in small_g:
        rows = g.size // 128
        full_g.append(tot[lo:lo + rows].reshape(g.shape))
        lo += rows

    def mine(g, width):
        return lax.dynamic_slice_in_dim(g, dev * width, width, axis=g.ndim - 1)

    fg = full_g
    small_gl = [fg[0], fg[1], fg[2], mine(fg[3], d // N_DEV), mine(fg[4], cl), mine(fg[5], cl), mine(fg[6], cl),
                mine(fg[7], cl), mine(fg[8], cl), mine(fg[9], d // N_DEV)]
    sd, sm, sv = adamw_small(small_w, small_gl, small_m, small_v, "adamw_small")

    def like(k, a):
        return a[None] if k in (4, 5) else a

    sg = [like(k, a) for k, a in enumerate(small_gl)]
    sd = [like(k, a) for k, a in enumerate(sd)]
    sm = [like(k, a) for k, a in enumerate(sm)]
    sv = [like(k, a) for k, a in enumerate(sv)]
    return sg, sd, sm, sv, loss
```

```python
import functools
import math

import jax
import jax.numpy as jnp
from jax import lax
from jax.experimental import pallas as pl
from jax.experimental.pallas import tpu as pltpu

F32 = jnp.float32
BF16 = jnp.bfloat16
EPS = 1e-6
HEAD_DIM = 128
POOL_WINDOWS = (2, 4, 8, 16)
SCONV_K = 3
CONF_K = 31
HALO = 32
N_DEV = 8
VMEM_LIMIT = 56 * 1024 * 1024
MESH = pl.DeviceIdType.MESH

ADAM_LR = 0.001
ADAM_B1 = 0.9
ADAM_B2 = 0.999
ADAM_EPS = 1e-08
ADAM_WD = 0.01
ADAM_STEP = 10


def _params(*sem):
    return pltpu.CompilerParams(dimension_semantics=sem, vmem_limit_bytes=VMEM_LIMIT)


def _sigmoid(v):
    return 1.0 / (1.0 + jnp.exp(-v))


def _silu(v):
    return v * _sigmoid(v)


def _silu_and_grad(v):
    s = _sigmoid(v)
    return v * s, s * (1.0 + v * (1.0 - s))


def _rowsum8(v):
    r, c = v.shape
    return jnp.sum(v.reshape(r // 8, 8, c), axis=0)


def _tap_before(xx, i, rows):
    if i == 0:
        return xx[HALO:HALO + rows]
    return pltpu.roll(xx, i, 0)[HALO:HALO + rows]


def _tap_after(xx, i, rows):
    if i == 0:
        return xx[0:rows]
    return pltpu.roll(xx, xx.shape[0] - i, 0)[0:rows]


def rms_fwd(x, g, name, tm=256):
    s, d = x.shape

    def body(x_ref, g_ref, h_ref):
        xv = x_ref[...]
        r = lax.rsqrt(jnp.mean(xv * xv, axis=-1, keepdims=True) + EPS)
        h_ref[...] = (xv * r * g_ref[...]).astype(BF16)

    return pl.pallas_call(
        body, name=name, grid=(s // tm,),
        in_specs=[pl.BlockSpec((tm, d), lambda i: (i, 0)), pl.BlockSpec((1, d), lambda i: (0, 0))],
        out_specs=pl.BlockSpec((tm, d), lambda i: (i, 0)),
        out_shape=jax.ShapeDtypeStruct((s, d), BF16),
        compiler_params=_params("parallel"),
    )(x, g)


def postnorm_fwd(x, o, g, name, tm=256):
    s, d = x.shape

    def body(x_ref, o_ref, g_ref, y_ref):
        ov = o_ref[...]
        r = lax.rsqrt(jnp.mean(ov * ov, axis=-1, keepdims=True) + EPS)
        y_ref[...] = x_ref[...] + ov * r * g_ref[...]

    return pl.pallas_call(
        body, name=name, grid=(s // tm,),
        in_specs=[pl.BlockSpec((tm, d), lambda i: (i, 0)), pl.BlockSpec((tm, d), lambda i: (i, 0)),
                  pl.BlockSpec((1, d), lambda i: (0, 0))],
        out_specs=pl.BlockSpec((tm, d), lambda i: (i, 0)),
        out_shape=jax.ShapeDtypeStruct((s, d), F32),
        compiler_params=_params("parallel"),
    )(x, o, g)


def final_fwd_bwd(x1, o, g, target, name, tm=256):
    s, d = x1.shape
    n = s // tm

    def body(x_ref, o_ref, g_ref, t_ref, loss_ref, gx_ref, do_ref, dg_ref, lacc, gacc):
        i = pl.program_id(0)

        @pl.when(i == 0)
        def _():
            lacc[...] = jnp.zeros_like(lacc)
            gacc[...] = jnp.zeros_like(gacc)

        ov = o_ref[...]
        gv = g_ref[...]
        r = lax.rsqrt(jnp.mean(ov * ov, axis=-1, keepdims=True) + EPS)
        oh = ov * r
        diff = x_ref[...] + oh * gv - t_ref[...]
        lacc[...] += _rowsum8(diff * diff)
        gx = diff * (1.0 / d)
        gx_ref[...] = gx
        gacc[...] += _rowsum8(gx * oh)
        dn = gx * gv
        do_ref[...] = (r * (dn - oh * jnp.mean(dn * oh, axis=-1, keepdims=True))).astype(BF16)

        @pl.when(i == n - 1)
        def _():
            tot = jnp.sum(jnp.sum(lacc[...], axis=0, keepdims=True), axis=1, keepdims=True)
            loss_ref[...] = jnp.broadcast_to(tot * (0.5 / d), loss_ref.shape)
            dg_ref[...] = jnp.sum(gacc[...], axis=0, keepdims=True)

    row = pl.BlockSpec((tm, d), lambda i: (i, 0))
    vec = pl.BlockSpec((1, d), lambda i: (0, 0))
    return pl.pallas_call(
        body, name=name, grid=(n,),
        in_specs=[row, row, vec, row],
        out_specs=[pl.BlockSpec((8, 128), lambda i: (0, 0)), row, row, vec],
        out_shape=[jax.ShapeDtypeStruct((8, 128), F32), jax.ShapeDtypeStruct((s, d), F32),
                   jax.ShapeDtypeStruct((s, d), BF16), jax.ShapeDtypeStruct((1, d), F32)],
        scratch_shapes=[pltpu.VMEM((8, d), F32), pltpu.VMEM((8, d), F32)],
        compiler_params=_params("arbitrary"),
    )(x1, o, g, target)


def norm_bwd(dy, inp, g, resid, out_dtype, name, tm=256, dep=None):
    s, d = inp.shape
    n = s // tm
    has_resid = resid is not None

    def body(*refs):
        dy_ref, x_ref, g_ref = refs[:3]
        r_ref = refs[3] if has_resid else None
        dx_ref, dg_ref, gacc = refs[-3:]
        i = pl.program_id(0)

        @pl.when(i == 0)
        def _():
            gacc[...] = jnp.zeros_like(gacc)

        xv = x_ref[...]
        dyv = dy_ref[...].astype(F32)
        r = lax.rsqrt(jnp.mean(xv * xv, axis=-1, keepdims=True) + EPS)
        xh = xv * r
        gacc[...] += _rowsum8(dyv * xh)
        dn = dyv * g_ref[...]
        dx = r * (dn - xh * jnp.mean(dn * xh, axis=-1, keepdims=True))
        if has_resid:
            dx = dx + r_ref[...]
        dx_ref[...] = dx.astype(out_dtype)

        @pl.when(i == n - 1)
        def _():
            dg_ref[...] = jnp.sum(gacc[...], axis=0, keepdims=True)

    row = pl.BlockSpec((tm, d), lambda i: (i, 0))
    vec = pl.BlockSpec((1, d), lambda i: (0, 0))
    dep_args, dep_specs = _after(dep)
    args = [dy, inp, g] + ([resid] if has_resid else []) + dep_args
    return pl.pallas_call(
        body, name=name, grid=(n,),
        in_specs=[row, row, vec] + ([row] if has_resid else []) + dep_specs,
        out_specs=[row, vec],
        out_shape=[jax.ShapeDtypeStruct((s, d), out_dtype), jax.ShapeDtypeStruct((1, d), F32)],
        scratch_shapes=[pltpu.VMEM((8, d), F32)],
        compiler_params=_params("arbitrary"),
    )(*args)


def _after(dep):
    if dep is None:
        return [], []
    return [dep], [pl.BlockSpec((8, 128), lambda *_: (0, 0))]


def mm_nn(a, w, out_dtype, name, tm=1024, tn=None, dep=None):
    m, k = a.shape
    tm = min(tm, m)
    ns, _, n = w.shape
    tn = n if tn is None else tn
    nj = n // tn
    dep_args, dep_specs = _after(dep)

    def body(a_ref, w_ref, *rest):
        o_ref = rest[-1]
        o_ref[...] = jnp.dot(a_ref[...], w_ref[0], preferred_element_type=F32).astype(out_dtype)

    return pl.pallas_call(
        body, name=name, grid=(ns, nj, m // tm),
        in_specs=[pl.BlockSpec((tm, k), lambda s, j, i: (i, 0)),
                  pl.BlockSpec((1, k, tn), lambda s, j, i: (s, 0, j))] + dep_specs,
        out_specs=pl.BlockSpec((tm, tn), lambda s, j, i: (i, s * nj + j)),
        out_shape=jax.ShapeDtypeStruct((m, ns * n), out_dtype),
        compiler_params=_params("parallel", "parallel", "parallel"),
    )(a, w, *dep_args)


def mm_nt(a, w, out_dtype, name, tm=1024, tn=None, dep=None):
    m = a.shape[0]
    tm = min(tm, m)
    ns, k, n = w.shape
    tn = n if tn is None else tn
    nj = n // tn
    steps = ns * nj
    dep_args, dep_specs = _after(dep)

    def body(a_ref, w_ref, *rest):
        o_ref, acc = rest[-2:]
        r = pl.program_id(1)

        @pl.when(r == 0)
        def _():
            acc[...] = jnp.zeros_like(acc)

        acc[...] += lax.dot_general(a_ref[...], w_ref[0], (((1,), (1,)), ((), ())),
                                    preferred_element_type=F32)

        @pl.when(r == steps - 1)
        def _():
            o_ref[...] = acc[...].astype(out_dtype)

    return pl.pallas_call(
        body, name=name, grid=(m // tm, steps),
        in_specs=[pl.BlockSpec((tm, tn), lambda i, r: (i, r)),
                  pl.BlockSpec((1, k, tn), lambda i, r: (r // nj, 0, r % nj))] + dep_specs,
        out_specs=pl.BlockSpec((tm, k), lambda i, r: (i, 0)),
        out_shape=jax.ShapeDtypeStruct((m, k), out_dtype),
        scratch_shapes=[pltpu.VMEM((tm, k), F32)],
        compiler_params=_params("parallel", "arbitrary"),
    )(a, w, *dep_args)


def mm_tn(a, b, ns, out_dtype, name, tk=1024, tm=1024, dep=None):
    m, k = a.shape
    tm = min(tm, m)
    n = b.shape[1] // ns
    steps = m // tm
    dep_args, dep_specs = _after(dep)

    def body(a_ref, b_ref, *rest):
        o_ref, acc = rest[-2:]
        r = pl.program_id(2)

        @pl.when(r == 0)
        def _():
            acc[...] = jnp.zeros_like(acc)

        acc[...] += lax.dot_general(a_ref[...], b_ref[...], (((0,), (0,)), ((), ())),
                                    preferred_element_type=F32)

        @pl.when(r == steps - 1)
        def _():
            o_ref[0] = acc[...].astype(out_dtype)

    return pl.pallas_call(
        body, name=name, grid=(ns, k // tk, steps),
        in_specs=[pl.BlockSpec((tm, tk), lambda s, j, r: (r, j)),
                  pl.BlockSpec((tm, n), lambda s, j, r: (r, s))] + dep_specs,
        out_specs=pl.BlockSpec((1, tk, n), lambda s, j, r: (s, j, 0)),
        out_shape=jax.ShapeDtypeStruct((ns, k, n), out_dtype),
        scratch_shapes=[pltpu.VMEM((tk, n), F32)],
        compiler_params=_params("parallel", "parallel", "arbitrary"),
    )(a, b, *dep_args)


SB_BLK = 128


def _split_dot(v, tri):
    hi = v.astype(BF16)
    lo = (v - hi.astype(F32)).astype(BF16)
    return jnp.dot(hi, tri, preferred_element_type=F32) + jnp.dot(lo, tri, preferred_element_type=F32)


def _sb_scores(z, lim, dcol, tri_ex):
    mask = dcol < lim
    sp = jnp.log(1.0 + jnp.exp(-jnp.abs(z)))
    lb = jnp.minimum(z, 0.0) - sp
    l1m = jnp.where(mask, lb - z, 0.0)
    return mask, lb, l1m, _split_dot(l1m, tri_ex)


def _sb_consts():
    row = lax.broadcasted_iota(jnp.int32, (SB_BLK, SB_BLK), 0)
    col = lax.broadcasted_iota(jnp.int32, (SB_BLK, SB_BLK), 1)
    tri_ex = jnp.where(row > col, 1.0, 0.0).astype(BF16)
    tri_in = jnp.where(row >= col, 1.0, 0.0).astype(BF16)
    return col - row, tri_ex, tri_in


def sb_fwd(p, n_heads, name, tq=256, nsub=4):
    s = p.shape[0]
    h_n = n_heads
    b = SB_BLK
    nqs = tq // b
    tk = nsub * b
    scale = 1.0 / math.sqrt(HEAD_DIM)

    def body(q_ref, k_ref, v_ref, o_ref, w_ref):
        qi = pl.program_id(1)
        dcol, tri_ex, _ = _sb_consts()
        qv = [q_ref[qs * b:(qs + 1) * b, :] for qs in range(nqs)]
        n_groups = ((qi + 1) * nqs - 1) // nsub + 1

        def step(it, carry):
            c1s, accs = carry
            g = n_groups - 1 - it
            off = pl.multiple_of(g * tk, tk)
            kg = k_ref[pl.ds(off, tk), :]
            vg = v_ref[pl.ds(off, tk), :]
            new_c1, new_acc = [], []
            for qs in range(nqs):
                qb = qi * nqs + qs
                z = lax.dot_general(qv[qs], kg, (((1,), (1,)), ((), ())), preferred_element_type=F32) * scale
                blocks = [_sb_scores(z[:, j * b:(j + 1) * b], (qb - (g * nsub + j)) * b, dcol, tri_ex)
                          for j in range(nsub)]
                run = c1s[qs]
                ws = [None] * nsub
                for j in reversed(range(nsub)):
                    mask, lb, l1m, ls_loc = blocks[j]
                    ws[j] = jnp.where(mask, jnp.exp(lb + ls_loc + run), 0.0).astype(BF16)
                    run = run + jnp.sum(l1m, axis=1, keepdims=True)
                w = jnp.concatenate(ws, axis=1)
                w_ref[0, g, qs * b:(qs + 1) * b, :] = w
                new_acc.append(accs[qs] + jnp.dot(w, vg, preferred_element_type=F32))
                new_c1.append(run)
            return tuple(new_c1), tuple(new_acc)

        init = (tuple(jnp.zeros((b, 1), F32) for _ in range(nqs)),
                tuple(jnp.zeros((b, HEAD_DIM), F32) for _ in range(nqs)))
        _, accs = lax.fori_loop(0, n_groups, step, init)
        for qs in range(nqs):
            o_ref[qs * b:(qs + 1) * b, :] = accs[qs]

    return pl.pallas_call(
        body, name=name, grid=(h_n, s // tq),
        in_specs=[pl.BlockSpec((tq, HEAD_DIM), lambda h, i: (i, h)),
                  pl.BlockSpec((s, HEAD_DIM), lambda h, i: (0, h_n + h)),
                  pl.BlockSpec((s, HEAD_DIM), lambda h, i: (0, 2 * h_n + h))],
        out_specs=[pl.BlockSpec((tq, HEAD_DIM), lambda h, i: (i, h)),
                   pl.BlockSpec((1, s // tk, tq, tk), lambda h, i: (h, 0, i, 0))],
        out_shape=[jax.ShapeDtypeStruct((s, h_n * HEAD_DIM), F32),
                   jax.ShapeDtypeStruct((h_n, s // tk, s, tk), BF16)],
        compiler_params=_params("parallel", "arbitrary"),
    )(p, p, p)


def sb_bwd(p, a, wts, da, n_heads, name, tq=256, dep=None):
    s = p.shape[0]
    h_n = n_heads
    nq = s // tq
    b = SB_BLK
    nqs = tq // b
    tk = wts.shape[3]
    nsub = tk // b
    scale = 1.0 / math.sqrt(HEAD_DIM)
    dep_args, dep_specs = _after(dep)

    def body(q_ref, k_ref, v_ref, a_ref, da_ref, w_ref, *rest):
        dq_ref, dk_ref, dv_ref, dk_acc, dv_acc = rest[-5:]
        qi = pl.program_id(1)

        @pl.when(qi == 0)
        def _():
            dk_acc[...] = jnp.zeros_like(dk_acc)
            dv_acc[...] = jnp.zeros_like(dv_acc)

        dcol, _, tri_in = _sb_consts()
        q_all = q_ref[...]
        do_all = da_ref[...]
        qv = [q_ref[qs * b:(qs + 1) * b, :] for qs in range(nqs)]
        dov = [da_ref[qs * b:(qs + 1) * b, :] for qs in range(nqs)]
        tots = [jnp.sum(dov[qs].astype(F32) * a_ref[qs * b:(qs + 1) * b, :], axis=1, keepdims=True)
                for qs in range(nqs)]
        n_groups = ((qi + 1) * nqs - 1) // nsub + 1

        def step(it, carry):
            c2s, dqs = carry
            g = n_groups - 1 - it
            off = pl.multiple_of(g * tk, tk)
            kg = k_ref[pl.ds(off, tk), :]
            vg = v_ref[pl.ds(off, tk), :]
            w_all = w_ref[0, g]
            new_c2, new_dq, dz_rows = [], [], []
            for qs in range(nqs):
                qb = qi * nqs + qs
                z = lax.dot_general(qv[qs], kg, (((1,), (1,)), ((), ())), preferred_element_type=F32) * scale
                dw = lax.dot_general(dov[qs], vg, (((1,), (1,)), ((), ())), preferred_element_type=F32)
                beta = 1.0 / (1.0 + jnp.exp(-z))
                e = dw * w_all[qs * b:(qs + 1) * b, :].astype(F32)
                run2 = c2s[qs]
                dzs = [None] * nsub
                for j in reversed(range(nsub)):
                    cols = slice(j * b, (j + 1) * b)
                    mask = dcol < (qb - (g * nsub + j)) * b
                    later = _split_dot(e[:, cols], tri_in) + run2
                    bj = beta[:, cols]
                    dz = jnp.where(mask, e[:, cols] * (1.0 - bj) - bj * (tots[qs] - later), 0.0) * scale
                    dzs[j] = dz.astype(BF16)
                    run2 = run2 + jnp.sum(e[:, cols], axis=1, keepdims=True)
                dzq = jnp.concatenate(dzs, axis=1)
                new_dq.append(dqs[qs] + jnp.dot(dzq, kg, preferred_element_type=F32))
                new_c2.append(run2)
                dz_rows.append(dzq)
            dz_all = jnp.concatenate(dz_rows, axis=0)
            dk_acc[pl.ds(off, tk), :] += lax.dot_general(dz_all, q_all, (((0,), (0,)), ((), ())),
                                                         preferred_element_type=F32)
            dv_acc[pl.ds(off, tk), :] += lax.dot_general(w_all, do_all, (((0,), (0,)), ((), ())),
                                                         preferred_element_type=F32)
            return tuple(new_c2), tuple(new_dq)

        zeros = tuple(jnp.zeros((b, 1), F32) for _ in range(nqs))
        _, dqs = lax.fori_loop(0, n_groups, step,
                               (zeros, tuple(jnp.zeros((b, HEAD_DIM), F32) for _ in range(nqs))))
        for qs in range(nqs):
            dq_ref[qs * b:(qs + 1) * b, :] = dqs[qs].astype(BF16)

        @pl.when(qi == nq - 1)
        def _():
            dk_ref[...] = dk_acc[...].astype(BF16)
            dv_ref[...] = dv_acc[...].astype(BF16)

    blk = pl.BlockSpec((tq, HEAD_DIM), lambda h, i: (i, h))
    full = pl.BlockSpec((s, HEAD_DIM), lambda h, i: (0, h))
    return pl.pallas_call(
        body, name=name, grid=(h_n, nq),
        in_specs=[blk, pl.BlockSpec((s, HEAD_DIM), lambda h, i: (0, h_n + h)),
                  pl.BlockSpec((s, HEAD_DIM), lambda h, i: (0, 2 * h_n + h)), blk, blk,
                  pl.BlockSpec((1, s // tk, tq, tk), lambda h, i: (h, 0, i, 0))] + dep_specs,
        out_specs=[blk, full, full],
        out_shape=[jax.ShapeDtypeStruct((s, h_n * HEAD_DIM), BF16)] * 3,
        scratch_shapes=[pltpu.VMEM((s, HEAD_DIM), F32), pltpu.VMEM((s, HEAD_DIM), F32)],
        compiler_params=_params("parallel", "arbitrary"),
    )(p, p, p, a, da, wts, *dep_args)


def _pool_window(xx, win, r0, rc):
    cur = xx[HALO:HALO + rc]
    ws = cur
    for i in range(1, win):
        ws = ws + _tap_before(xx, i, rc)
    t_idx = r0 + lax.broadcasted_iota(jnp.int32, (rc, 1), 0)
    inv = 1.0 / jnp.minimum(win, t_idx + 1).astype(F32)
    return ws * inv - cur, inv


def even_mix_fwd(a, p, pool_w, pool_scale, name, rc=64):
    s = p.shape[0]
    ng = len(POOL_WINDOWS)
    cw = pool_w.shape[1]
    n_chunks = s // rc

    def body(a_ref, u_ref, g_ref, w_ref, sc_ref, y_ref, upad):
        j = pl.program_id(0)

        @pl.when(j < ng)
        def _():
            def chunk(ci, carry):
                rows = pl.ds(pl.multiple_of(ci * rc, rc), rc)
                y_ref[rows, :] = (a_ref[rows, :] * _silu(g_ref[rows, :].astype(F32))).astype(BF16)
                return carry

            lax.fori_loop(0, n_chunks, chunk, 0)

        for gi, win in enumerate(POOL_WINDOWS):
            @pl.when(j == ng + gi)
            def _(win=win):
                upad[0:HALO, :] = jnp.zeros((HALO, cw), F32)

                def fill(ci, carry):
                    r0 = pl.multiple_of(ci * rc, rc)
                    upad[pl.ds(pl.multiple_of(r0 + HALO, HALO), rc), :] = u_ref[pl.ds(r0, rc), :].astype(F32)
                    return carry

                lax.fori_loop(0, n_chunks, fill, 0)

                def chunk(ci, carry):
                    r0 = pl.multiple_of(ci * rc, rc)
                    rows = pl.ds(r0, rc)
                    pooled, _ = _pool_window(upad[pl.ds(r0, HALO + rc), :], win, r0, rc)
                    t = jnp.dot(pooled.astype(BF16), w_ref[0], preferred_element_type=F32)
                    y_ref[rows, :] = (t * sc_ref[...] * _silu(g_ref[rows, :].astype(F32))).astype(BF16)
                    return carry

                lax.fori_loop(0, n_chunks, chunk, 0)

    grp = lambda j: jnp.maximum(j - ng, 0)
    return pl.pallas_call(
        body, name=name, grid=(2 * ng,),
        in_specs=[pl.BlockSpec((s, cw), lambda j: (0, jnp.minimum(j, ng - 1))),
                  pl.BlockSpec((s, cw), lambda j: (0, 3 * ng + grp(j))),
                  pl.BlockSpec((s, cw), lambda j: (0, 4 * ng + j)),
                  pl.BlockSpec((1, cw, cw), lambda j: (grp(j), 0, 0)),
                  pl.BlockSpec((1, cw), lambda j: (0, grp(j)))],
        out_specs=pl.BlockSpec((s, cw), lambda j: (0, j)),
        out_shape=jax.ShapeDtypeStruct((s, 2 * ng * cw), BF16),
        scratch_shapes=[pltpu.VMEM((HALO + s, cw), F32)],
        compiler_params=_params("arbitrary"),
    )(a, p, p, pool_w, pool_scale)


def even_mix_bwd(dy, a, p, pool_w, pool_scale, name, rc=64):
    s = p.shape[0]
    ng = len(POOL_WINDOWS)
    cw = pool_w.shape[1]
    n_chunks = s // rc

    def body(dy_ref, a_ref, u_ref, g_ref, w_ref, sc_ref, da_ref, du_ref, dg_ref, dw_ref, dsc_ref,
             upad, rpad, dpl, dw_acc, dsc_acc):
        j = pl.program_id(0)

        @pl.when(j < ng)
        def _():
            def chunk(ci, carry):
                rows = pl.ds(pl.multiple_of(ci * rc, rc), rc)
                dyv = dy_ref[rows, :].astype(F32)
                sg, dsg = _silu_and_grad(g_ref[rows, :].astype(F32))
                da_ref[rows, :] = (dyv * sg).astype(BF16)
                dg_ref[rows, :] = (dyv * a_ref[rows, :] * dsg).astype(BF16)
                return carry

            lax.fori_loop(0, n_chunks, chunk, 0)

        for gi, win in enumerate(POOL_WINDOWS):
            @pl.when(j == ng + gi)
            def _(win=win):
                upad[0:HALO, :] = jnp.zeros((HALO, cw), F32)
                rpad[s:s + HALO, :] = jnp.zeros((HALO, cw), F32)
                dw_acc[...] = jnp.zeros_like(dw_acc)
                dsc_acc[...] = jnp.zeros_like(dsc_acc)

                def fill(ci, carry):
                    r0 = pl.multiple_of(ci * rc, rc)
                    upad[pl.ds(pl.multiple_of(r0 + HALO, HALO), rc), :] = u_ref[pl.ds(r0, rc), :].astype(F32)
                    return carry

                lax.fori_loop(0, n_chunks, fill, 0)

                def chunk(ci, carry):
                    r0 = pl.multiple_of(ci * rc, rc)
                    rows = pl.ds(r0, rc)
                    pooled, inv = _pool_window(upad[pl.ds(r0, HALO + rc), :], win, r0, rc)
                    pb = pooled.astype(BF16)
                    wv = w_ref[0]
                    t = jnp.dot(pb, wv, preferred_element_type=F32)
                    scv = sc_ref[...]
                    dyv = dy_ref[rows, :].astype(F32)
                    sg, dsg = _silu_and_grad(g_ref[rows, :].astype(F32))
                    dpo = dyv * sg
                    dg_ref[rows, :] = (dyv * t * scv * dsg).astype(BF16)
                    dsc_acc[...] += _rowsum8(dpo * t)
                    dtb = (dpo * scv).astype(BF16)
                    dw_acc[...] += lax.dot_general(pb, dtb, (((0,), (0,)), ((), ())),
                                                   preferred_element_type=F32)
                    dpooled = lax.dot_general(dtb, wv, (((1,), (1,)), ((), ())),
                                              preferred_element_type=F32)
                    dpl[rows, :] = dpooled
                    rpad[rows, :] = dpooled * inv
                    return carry

                lax.fori_loop(0, n_chunks, chunk, 0)

                def chunk2(ci, carry):
                    r0 = pl.multiple_of(ci * rc, rc)
                    rows = pl.ds(r0, rc)
                    xx = rpad[pl.ds(r0, rc + HALO), :]
                    fs = xx[0:rc]
                    for i in range(1, win):
                        fs = fs + _tap_after(xx, i, rc)
                    du_ref[rows, :] = (fs - dpl[rows, :]).astype(BF16)
                    return carry

                lax.fori_loop(0, n_chunks, chunk2, 0)
                dw_ref[0] = dw_acc[...]
                dsc_ref[...] = jnp.sum(dsc_acc[...], axis=0, keepdims=True)

    grp = lambda j: jnp.maximum(j - ng, 0)
    att = lambda j: jnp.minimum(j, ng - 1)
    return pl.pallas_call(
        body, name=name, grid=(2 * ng,),
        in_specs=[pl.BlockSpec((s, cw), lambda j: (0, j)),
                  pl.BlockSpec((s, cw), lambda j: (0, att(j))),
                  pl.BlockSpec((s, cw), lambda j: (0, 3 * ng + grp(j))),
                  pl.BlockSpec((s, cw), lambda j: (0, 4 * ng + j)),
                  pl.BlockSpec((1, cw, cw), lambda j: (grp(j), 0, 0)),
                  pl.BlockSpec((1, cw), lambda j: (0, grp(j)))],
        out_specs=[pl.BlockSpec((s, cw), lambda j: (0, att(j))),
                   pl.BlockSpec((s, cw), lambda j: (0, grp(j))),
                   pl.BlockSpec((s, cw), lambda j: (0, j)),
                   pl.BlockSpec((1, cw, cw), lambda j: (grp(j), 0, 0)),
                   pl.BlockSpec((1, cw), lambda j: (0, grp(j)))],
        out_shape=[jax.ShapeDtypeStruct((s, ng * cw), BF16), jax.ShapeDtypeStruct((s, ng * cw), BF16),
                   jax.ShapeDtypeStruct((s, 2 * ng * cw), BF16),
                   jax.ShapeDtypeStruct((ng, cw, cw), F32), jax.ShapeDtypeStruct((1, ng * cw), F32)],
        scratch_shapes=[pltpu.VMEM((HALO + s, cw), F32), pltpu.VMEM((s + HALO, cw), F32),
                        pltpu.VMEM((s, cw), F32), pltpu.VMEM((cw, cw), F32), pltpu.VMEM((8, cw), F32)],
        compiler_params=_params("arbitrary"),
    )(dy, a, p, p, pool_w, pool_scale)


def _halo_before(tm):
    return lambda i: jnp.maximum(i * (tm // HALO) - 1, 0)


def _halo_after(tm, s):
    return lambda i: jnp.minimum((i + 1) * (tm // HALO), s // HALO - 1)


def odd_mix_fwd(p, sconv_w, dconv_w, dconv_b, cnorm_g, cnorm_b, name, tm=128):
    s = p.shape[0]
    cw = sconv_w.shape[1]
    n = s // tm
    lanes = 128
    hb = _halo_before(tm)

    def body(hc_ref, hch_ref, bc_ref, cc_ref, cch_ref, ga_ref, gah_ref, gb_ref, gbh_ref, g1_ref, g2_ref,
             sw_ref, dw_ref, db_ref, gam_ref, bet_ref, y_ref, dc_ref):
        first = pl.program_id(0) == 0
        for l in range(cw // lanes):
            cols = slice(l * lanes, (l + 1) * lanes)
            mh = jnp.where(first, 0.0, cch_ref[:, cols].astype(F32) * hch_ref[:, cols].astype(F32))
            mm = cc_ref[:, cols].astype(F32) * hc_ref[:, cols].astype(F32)
            xx = jnp.concatenate([mh, mm], axis=0)
            cv = jnp.zeros((tm, lanes), F32)
            for k in range(SCONV_K):
                cv = cv + sw_ref[k:k + 1, cols] * _tap_before(xx, SCONV_K - 1 - k, tm)
            c_out = bc_ref[:, cols].astype(F32) * cv
            y_ref[:, cols] = (c_out * _silu(g1_ref[:, cols].astype(F32))).astype(BF16)
            dh = jnp.where(first, 0.0, gah_ref[:, cols].astype(F32) * _sigmoid(gbh_ref[:, cols].astype(F32)))
            dm = ga_ref[:, cols].astype(F32) * _sigmoid(gb_ref[:, cols].astype(F32))
            xx = jnp.concatenate([dh, dm], axis=0)
            acc = jnp.zeros((tm, lanes), F32) + db_ref[:, cols]
            for k in range(CONF_K):
                acc = acc + dw_ref[k:k + 1, cols] * _tap_before(xx, CONF_K - 1 - k, tm)
            dc_ref[:, cols] = acc
        rs = 32
        for r in range(tm // rs):
            rows = slice(r * rs, (r + 1) * rs)
            xv = dc_ref[rows, :]
            mu = jnp.mean(xv, axis=-1, keepdims=True)
            xc = xv - mu
            rstd = lax.rsqrt(jnp.mean(xc * xc, axis=-1, keepdims=True) + EPS)
            ln = xc * rstd * gam_ref[...] + bet_ref[...]
            y_ref[rows, cw:2 * cw] = (_silu(ln) * _silu(g2_ref[rows, :].astype(F32))).astype(BF16)

    main = lambda c: pl.BlockSpec((tm, cw), lambda i: (i, c))
    halo = lambda c: pl.BlockSpec((HALO, cw), lambda i: (hb(i), c))
    vec = lambda r: pl.BlockSpec((r, cw), lambda i: (0, 0))
    return pl.pallas_call(
        body, name=name, grid=(n,),
        in_specs=[main(0), halo(0), main(1), main(2), halo(2), main(3), halo(3), main(4), halo(4),
                  main(5), main(6), vec(SCONV_K), vec(CONF_K), vec(1), vec(1), vec(1)],
        out_specs=[pl.BlockSpec((tm, 2 * cw), lambda i: (i, 0)), pl.BlockSpec((tm, cw), lambda i: (i, 0))],
        out_shape=[jax.ShapeDtypeStruct((s, 2 * cw), BF16), jax.ShapeDtypeStruct((s, cw), F32)],
        compiler_params=_params("parallel"),
    )(p, p, p, p, p, p, p, p, p, p, p, sconv_w, dconv_w, dconv_b, cnorm_g, cnorm_b)


def odd_bwd_ln(dy, p, dc, cnorm_g, cnorm_b, name, tm=256):
    s = p.shape[0]
    cw = dc.shape[1]
    n = s // tm
    rs = 32

    def body(dy_ref, g2_ref, dc_ref, gam_ref, bet_ref, ddc_ref, dg_ref, dgam_ref, dbet_ref, gacc, bacc):
        i = pl.program_id(0)

        @pl.when(i == 0)
        def _():
            gacc[...] = jnp.zeros_like(gacc)
            bacc[...] = jnp.zeros_like(bacc)

        def chunk(ci, carry):
            rows = pl.ds(pl.multiple_of(ci * rs, rs), rs)
            xv = dc_ref[rows, :]
            mu = jnp.mean(xv, axis=-1, keepdims=True)
            xc = xv - mu
            rstd = lax.rsqrt(jnp.mean(xc * xc, axis=-1, keepdims=True) + EPS)
            xh = xc * rstd
            gam = gam_ref[...]
            sl, dsl = _silu_and_grad(xh * gam + bet_ref[...])
            sg, dsg = _silu_and_grad(g2_ref[rows, :].astype(F32))
            dyv = dy_ref[rows, :].astype(F32)
            dg_ref[rows, :] = (dyv * sl * dsg).astype(BF16)
            dln = dyv * sg * dsl
            gacc[...] += _rowsum8(dln * xh)
            bacc[...] += _rowsum8(dln)
            dxh = dln * gam
            ddc_ref[rows, :] = rstd * (dxh - jnp.mean(dxh, axis=-1, keepdims=True)
                                       - xh * jnp.mean(dxh * xh, axis=-1, keepdims=True))
            return carry

        lax.fori_loop(0, tm // rs, chunk, 0)

        @pl.when(i == n - 1)
        def _():
            dgam_ref[...] = jnp.sum(gacc[...], axis=0, keepdims=True)
            dbet_ref[...] = jnp.sum(bacc[...], axis=0, keepdims=True)

    vec = pl.BlockSpec((1, cw), lambda i: (0, 0))
    return pl.pallas_call(
        body, name=name, grid=(n,),
        in_specs=[pl.BlockSpec((tm, cw), lambda i: (i, 1)), pl.BlockSpec((tm, cw), lambda i: (i, 6)),
                  pl.BlockSpec((tm, cw), lambda i: (i, 0)), vec, vec],
        out_specs=[pl.BlockSpec((tm, cw), lambda i: (i, 0)), pl.BlockSpec((tm, cw), lambda i: (i, 0)), vec, vec],
        out_shape=[jax.ShapeDtypeStruct((s, cw), F32), jax.ShapeDtypeStruct((s, cw), BF16),
                   jax.ShapeDtypeStruct((1, cw), F32), jax.ShapeDtypeStruct((1, cw), F32)],
        scratch_shapes=[pltpu.VMEM((8, cw), F32), pltpu.VMEM((8, cw), F32)],
        compiler_params=_params("arbitrary"),
    )(dy, p, dc, cnorm_g, cnorm_b)


def odd_bwd_conv(dy, p, ddc, dg2, sconv_w, dconv_w, name, tm=128):
    s = p.shape[0]
    cw = ddc.shape[1]
    n = s // tm
    lanes = 128
    hb = _halo_before(tm)
    ha = _halo_after(tm, s)

    def body(dy_ref, dya_ref, g1_ref, g1a_ref, bc_ref, bca_ref, hc_ref, hch_ref, cc_ref, cch_ref,
             ddc_ref, ddca_ref, ga_ref, gah_ref, gb_ref, gbh_ref, dg2_ref, sw_ref, dw_ref,
             dp_ref, dsw_ref, ddw_ref, ddb_ref, sw_acc, dw_acc, db_acc):
        i = pl.program_id(0)
        first = i == 0
        last = i == n - 1

        @pl.when(first)
        def _():
            sw_acc[...] = jnp.zeros_like(sw_acc)
            dw_acc[...] = jnp.zeros_like(dw_acc)
            db_acc[...] = jnp.zeros_like(db_acc)

        for l in range(cw // lanes):
            cols = slice(l * lanes, (l + 1) * lanes)
            mh = jnp.where(first, 0.0, cch_ref[:, cols].astype(F32) * hch_ref[:, cols].astype(F32))
            hcv = hc_ref[:, cols].astype(F32)
            ccv = cc_ref[:, cols].astype(F32)
            xx = jnp.concatenate([mh, ccv * hcv], axis=0)
            taps = [_tap_before(xx, SCONV_K - 1 - k, tm) for k in range(SCONV_K)]
            cv = jnp.zeros((tm, lanes), F32)
            for k in range(SCONV_K):
                cv = cv + sw_ref[k:k + 1, cols] * taps[k]
            bcv = bc_ref[:, cols].astype(F32)
            dyv = dy_ref[:, cols].astype(F32)
            sg, dsg = _silu_and_grad(g1_ref[:, cols].astype(F32))
            dco = dyv * sg
            dp_ref[:, 5 * cw + l * lanes:5 * cw + (l + 1) * lanes] = (dyv * bcv * cv * dsg).astype(BF16)
            dp_ref[:, cw + l * lanes:cw + (l + 1) * lanes] = (dco * cv).astype(BF16)
            dcv = dco * bcv
            for k in range(SCONV_K):
                sw_acc[k * 8:(k + 1) * 8, cols] += _rowsum8(dcv * taps[k])
            dcv_a = jnp.where(last, 0.0, dya_ref[:, cols].astype(F32) * _silu(g1a_ref[:, cols].astype(F32))
                              * bca_ref[:, cols].astype(F32))
            xx = jnp.concatenate([dcv, dcv_a], axis=0)
            dm = jnp.zeros((tm, lanes), F32)
            for k in range(SCONV_K):
                dm = dm + sw_ref[k:k + 1, cols] * _tap_after(xx, SCONV_K - 1 - k, tm)
            dp_ref[:, l * lanes:(l + 1) * lanes] = (dm * ccv).astype(BF16)
            dp_ref[:, 2 * cw + l * lanes:2 * cw + (l + 1) * lanes] = (dm * hcv).astype(BF16)
            gav = ga_ref[:, cols].astype(F32)
            sb = _sigmoid(gb_ref[:, cols].astype(F32))
            dh = jnp.where(first, 0.0, gah_ref[:, cols].astype(F32) * _sigmoid(gbh_ref[:, cols].astype(F32)))
            xx = jnp.concatenate([dh, gav * sb], axis=0)
            ddcv = ddc_ref[:, cols]
            db_acc[:, cols] += _rowsum8(ddcv)
            for k in range(CONF_K):
                dw_acc[k * 8:(k + 1) * 8, cols] += _rowsum8(ddcv * _tap_before(xx, CONF_K - 1 - k, tm))
            ddc_a = jnp.where(last, 0.0, ddca_ref[:, cols])
            xx = jnp.concatenate([ddcv, ddc_a], axis=0)
            dgl = jnp.zeros((tm, lanes), F32)
            for k in range(CONF_K):
                dgl = dgl + dw_ref[k:k + 1, cols] * _tap_after(xx, CONF_K - 1 - k, tm)
            dp_ref[:, 3 * cw + l * lanes:3 * cw + (l + 1) * lanes] = (dgl * sb).astype(BF16)
            dp_ref[:, 4 * cw + l * lanes:4 * cw + (l + 1) * lanes] = (dgl * gav * sb * (1.0 - sb)).astype(BF16)
        dp_ref[:, 6 * cw:7 * cw] = dg2_ref[...]

        @pl.when(last)
        def _():
            for k in range(SCONV_K):
                dsw_ref[k:k + 1, :] = jnp.sum(sw_acc[k * 8:(k + 1) * 8, :], axis=0, keepdims=True)
            for k in range(CONF_K):
                ddw_ref[k:k + 1, :] = jnp.sum(dw_acc[k * 8:(k + 1) * 8, :], axis=0, keepdims=True)
            ddb_ref[...] = jnp.sum(db_acc[...], axis=0, keepdims=True)

    def main(c):
        return pl.BlockSpec((tm, cw), lambda i: (i, c))

    def before(c):
        return pl.BlockSpec((HALO, cw), lambda i: (hb(i), c))

    def after(c):
        return pl.BlockSpec((HALO, cw), lambda i: (ha(i), c))

    def vec(r):
        return pl.BlockSpec((r, cw), lambda i: (0, 0))

    return pl.pallas_call(
        body, name=name, grid=(n,),
        in_specs=[main(0), after(0), main(5), after(5), main(1), after(1), main(0), before(0), main(2), before(2),
                  main(0), after(0), main(3), before(3), main(4), before(4), main(0), vec(SCONV_K), vec(CONF_K)],
        out_specs=[pl.BlockSpec((tm, 7 * cw), lambda i: (i, 0)), vec(SCONV_K), vec(CONF_K), vec(1)],
        out_shape=[jax.ShapeDtypeStruct((s, 7 * cw), BF16), jax.ShapeDtypeStruct((SCONV_K, cw), F32),
                   jax.ShapeDtypeStruct((CONF_K, cw), F32), jax.ShapeDtypeStruct((1, cw), F32)],
        scratch_shapes=[pltpu.VMEM((8 * SCONV_K, cw), F32), pltpu.VMEM((8 * CONF_K, cw), F32),
                        pltpu.VMEM((8, cw), F32)],
        compiler_params=_params("arbitrary"),
    )(dy, dy, p, p, p, p, p, p, p, p, ddc, ddc, p, p, p, p, dg2, sconv_w, dconv_w)


_ANY = pl.BlockSpec(memory_space=pl.ANY)


def _place():
    return lax.axis_index("x"), lax.axis_index("y"), lax.axis_index("c")


def all_gather(arrs, name, deps=()):
    n = len(arrs)

    def body(*refs):
        ins, outs = refs[:n], refs[n + len(deps):2 * n + len(deps)]
        send_sems, recv_sems, local_sems = refs[-3:]
        x, y, c = _place()
        me, sibling = (x, y, c), (x, y, 1 - c)
        chips = [(1 - x, y), (x, 1 - y), (1 - x, 1 - y)]

        def copy(a, k, block, to, src=None):
            px, py, pc = block
            dst = outs[a].at[4 * px + 2 * py + pc]
            return pltpu.make_async_remote_copy(
                src_ref=dst if src is None else src, dst_ref=dst,
                send_sem=send_sems.at[7 * a + k], recv_sem=recv_sems.at[7 * a + k],
                device_id=to, device_id_type=MESH)

        mine = [pltpu.make_async_copy(ins[a], outs[a].at[4 * x + 2 * y + c], local_sems.at[a]) for a in range(n)]
        for cp in mine:
            cp.start()
        first = []
        for a in range(n):
            first.append(copy(a, 0, me, sibling, src=ins[a]))
            first += [copy(a, 1 + j, me, (*chip, c), src=ins[a]) for j, chip in enumerate(chips)]
        for cp in first:
            cp.start()
        passed = []
        for a in range(n):
            for j, chip in enumerate(chips):
                copy(a, 1 + j, (*chip, c), me).wait_recv()
                cp = copy(a, 4 + j, (*chip, c), sibling)
                cp.start()
                passed.append(cp)
        for a in range(n):
            copy(a, 0, sibling, me).wait_recv()
            for j, chip in enumerate(chips):
                copy(a, 4 + j, (*chip, 1 - c), me).wait_recv()
        for cp in first + passed:
            cp.wait_send()
        for cp in mine:
            cp.wait()

    return pl.pallas_call(
        body, name=name,
        out_shape=[jax.ShapeDtypeStruct((N_DEV,) + a.shape, a.dtype) for a in arrs],
        in_specs=[_ANY] * (n + len(deps)), out_specs=[_ANY] * n,
        scratch_shapes=[pltpu.SemaphoreType.DMA((7 * n,)), pltpu.SemaphoreType.DMA((7 * n,)),
                        pltpu.SemaphoreType.DMA((n,))],
    )(*arrs, *deps)


_HBM = pl.BlockSpec(memory_space=pltpu.HBM)
_SEM = pl.BlockSpec(memory_space=pltpu.SEMAPHORE)
_DATAFLOW = pltpu.SideEffectType.DATAFLOW_SIDE_EFFECTING


def _peers_per_array(kind):
    return 1 if kind == "sibling" else 3


def _split_copies(kind, srcs, lands, send_sems, recv_sems):
    x, y, c = _place()
    per = _peers_per_array(kind)
    out = []
    for a in range(len(srcs)):
        if kind == "sibling":
            peers = [((x, y, 1 - c), srcs[a].at[:, pl.ds(1 - c, 1)], lands[a], lands[a])]
        else:
            peers = []
            for px, py in [(1 - x, y), (x, 1 - y), (1 - x, 1 - y)]:
                if kind == "gather":
                    views = (srcs[a], lands[a].at[4 * x + 2 * y + c], lands[a].at[4 * px + 2 * py + c])
                else:
                    views = (srcs[a].at[2 * px + py], lands[a].at[2 * x + y], lands[a].at[2 * px + py])
                peers.append(((px, py, c),) + views)
        for j, (peer, src, dst, arrives) in enumerate(peers):
            sems = dict(send_sem=send_sems.at[per * a + j], recv_sem=recv_sems.at[per * a + j],
                        device_id=peer, device_id_type=MESH)
            out.append((pltpu.make_async_remote_copy(src_ref=src, dst_ref=dst, **sems),
                        pltpu.make_async_remote_copy(src_ref=src, dst_ref=arrives, **sems)))
    return out


def split_start(kind, srcs, lands, deps, name):
    n = len(srcs)
    n_sems = _peers_per_array(kind) * n

    def body(*refs):
        send_sems, recv_sems = refs[2 * n + len(deps)], refs[2 * n + len(deps) + 1]
        for copy, _ in _split_copies(kind, refs[:n], refs[n:2 * n], send_sems, recv_sems):
            copy.start()
        token = refs[-1]
        token[...] = jnp.zeros_like(token)

    held = [pltpu.HBM(a.shape, a.dtype) for a in list(srcs) + list(lands)]
    outs = pl.pallas_call(
        body, name=name,
        out_shape=(pltpu.SemaphoreType.DMA((n_sems,)), pltpu.SemaphoreType.DMA((n_sems,)), *held,
                   jax.ShapeDtypeStruct((8, 128), F32)),
        in_specs=[_HBM] * (2 * n) + [_ANY] * len(deps),
        out_specs=(_SEM, _SEM, *([_HBM] * (2 * n)), pl.BlockSpec(memory_space=pltpu.VMEM)),
        input_output_aliases={i: 2 + i for i in range(2 * n)},
        compiler_params=pltpu.CompilerParams(has_side_effects=_DATAFLOW),
    )(*[pltpu.with_memory_space_constraint(a, pltpu.HBM) for a in list(srcs) + list(lands)], *deps)
    return outs[0], outs[1], list(outs[2:2 + n]), list(outs[2 + n:2 + 2 * n]), outs[-1]


def split_wait(kind, send_sems, recv_sems, srcs, lands, afters, name):
    n = len(srcs)

    def body(*refs):
        for _, arrival in _split_copies(kind, refs[:n], refs[n:2 * n], refs[2 * n], refs[2 * n + 1]):
            arrival.wait_send()
            arrival.wait_recv()

    outs = pl.pallas_call(
        body, name=name,
        out_shape=[pltpu.HBM(a.shape, a.dtype) for a in list(srcs) + list(lands)],
        in_specs=[_HBM] * (2 * n) + [_SEM, _SEM] + [_ANY] * len(afters),
        out_specs=[_HBM] * (2 * n),
        input_output_aliases={i: i for i in range(2 * n)},
        compiler_params=pltpu.CompilerParams(has_side_effects=_DATAFLOW),
    )(*srcs, *lands, send_sems, recv_sems, *afters)
    return list(outs[:n]), list(outs[n:])


def place_block(land, block, dev, name):
    r, c = block.shape
    tr = min(r, 512)

    def body(dev_ref, land_ref, b_ref, o_ref):
        del dev_ref, land_ref
        o_ref[...] = b_ref[...]

    return pl.pallas_call(
        body, name=name,
        grid_spec=pltpu.PrefetchScalarGridSpec(
            num_scalar_prefetch=1, grid=(r // tr,),
            in_specs=[_ANY, pl.BlockSpec((tr, c), lambda i, dev_ref: (i, 0))],
            out_specs=pl.BlockSpec((None, tr, c), lambda i, dev_ref: (dev_ref[0], i, 0))),
        out_shape=jax.ShapeDtypeStruct(land.shape, land.dtype),
        input_output_aliases={1: 0},
        compiler_params=_params("parallel"),
    )(dev, land, block)


def gather_finish(lands, name):
    n = len(lands)

    def body(*refs):
        outs = refs[n:2 * n]
        send_sems, recv_sems = refs[2 * n:]
        x, y, c = _place()
        cps = [pltpu.make_async_remote_copy(
            src_ref=outs[a].at[:, pl.ds(c, 1)], dst_ref=outs[a].at[:, pl.ds(c, 1)],
            send_sem=send_sems.at[a], recv_sem=recv_sems.at[a],
            device_id=(x, y, 1 - c), device_id_type=MESH) for a in range(n)]
        for cp in cps:
            cp.start()
        for cp in cps:
            cp.wait()

    return pl.pallas_call(
        body, name=name,
        out_shape=[jax.ShapeDtypeStruct(a.shape, a.dtype) for a in lands],
        in_specs=[_ANY] * n, out_specs=[_ANY] * n,
        input_output_aliases={i: i for i in range(n)},
        scratch_shapes=[pltpu.SemaphoreType.DMA((n,)), pltpu.SemaphoreType.DMA((n,))],
    )(*lands)


def pair_add(own, recv, core, name):
    _, _, r, c = own.shape
    tr = min(r, 512)

    def body(core_ref, own_ref, recv_ref, o_ref):
        del core_ref
        o_ref[...] = (own_ref[...].astype(F32) + recv_ref[...].astype(F32)).astype(BF16)

    return pl.pallas_call(
        body, name=name,
        grid_spec=pltpu.PrefetchScalarGridSpec(
            num_scalar_prefetch=1, grid=(4, r // tr),
            in_specs=[pl.BlockSpec((None, None, tr, c), lambda k, i, core_ref: (k, core_ref[0], i, 0)),
                      pl.BlockSpec((None, None, tr, c), lambda k, i, core_ref: (k, 0, i, 0))],
            out_specs=pl.BlockSpec((None, tr, c), lambda k, i, core_ref: (k, i, 0))),
        out_shape=jax.ShapeDtypeStruct((4, r, c), BF16),
        compiler_params=_params("parallel", "parallel"),
    )(core, own, recv)


def _adamw_math(w, g, m, v):
    m2 = ADAM_B1 * m + (1.0 - ADAM_B1) * g
    v2 = ADAM_B2 * v + (1.0 - ADAM_B2) * (g * g)
    m_hat = m2 / (1.0 - ADAM_B1 ** ADAM_STEP)
    v_hat = v2 / (1.0 - ADAM_B2 ** ADAM_STEP)
    delta = -ADAM_LR * (m_hat / (jnp.sqrt(v_hat) + ADAM_EPS) + ADAM_WD * w)
    return delta, m2, v2


def adamw_big(w, m, v, own, got, chip, name):
    r, c = w.shape
    tr = min(r, 256)

    def body(chip_ref, w_ref, m_ref, v_ref, p0, p1, p2, p3, g_ref, d_ref, m2_ref, v2_ref):
        del chip_ref
        g = ((p0[...].astype(F32) + p1[...].astype(F32)) + p2[...].astype(F32)) + p3[...].astype(F32)
        delta, m2, v2 = _adamw_math(w_ref[...], g, m_ref[...], v_ref[...])
        g_ref[...] = g
        d_ref[...] = delta
        m2_ref[...] = m2
        v2_ref[...] = v2

    row = pl.BlockSpec((tr, c), lambda i, chip_ref: (i, 0))

    def slab(flip):
        return pl.BlockSpec((None, tr, c), lambda i, chip_ref: (chip_ref[0] ^ flip, i, 0))

    return pl.pallas_call(
        body, name=name,
        grid_spec=pltpu.PrefetchScalarGridSpec(
            num_scalar_prefetch=1, grid=(r // tr,),
            in_specs=[row, row, row, slab(0), slab(1), slab(2), slab(3)],
            out_specs=[row] * 4),
        out_shape=[jax.ShapeDtypeStruct((r, c), F32)] * 4,
        compiler_params=_params("parallel"),
    )(chip, w, m, v, own, got, got, got)


def sum_devices(g8, name):
    def body(g_ref, o_ref):
        tot = g_ref[0]
        for k in range(1, N_DEV):
            tot = tot + g_ref[k]
        o_ref[...] = tot

    return pl.pallas_call(body, name=name, out_shape=jax.ShapeDtypeStruct(g8.shape[1:], F32))(g8)


def adamw_small(ws, gs, ms, vs, name):
    n = len(ws)

    def body(*refs):
        w_r, g_r, m_r, v_r = refs[:n], refs[n:2 * n], refs[2 * n:3 * n], refs[3 * n:4 * n]
        d_o, m_o, v_o = refs[4 * n:5 * n], refs[5 * n:6 * n], refs[6 * n:7 * n]
        for k in range(n):
            delta, m2, v2 = _adamw_math(w_r[k][...], g_r[k][...], m_r[k][...], v_r[k][...])
            d_o[k][...] = delta
            m_o[k][...] = m2
            v_o[k][...] = v2

    shapes = [jax.ShapeDtypeStruct(w.shape, F32) for w in ws]
    outs = pl.pallas_call(body, name=name, out_shape=shapes * 3)(*ws, *gs, *ms, *vs)
    return outs[:n], outs[n:2 * n], outs[2 * n:]


def _rows128(a):
    return a.reshape(-1, 128)


def _pad_rows(a, rows):
    return jnp.pad(a, ((0, rows - a.shape[0]), (0, 0)))


def kernel(x, ln_pre_even, w_in_even, pool_w, pool_scale, w_out_even, ln_post_even, ln_pre_odd, w_in_odd, sconv_w, dconv_w, dconv_b, cnorm_g, cnorm_b, w_out_odd, ln_post_odd, loss_target, m_ln_pre_even, m_w_in_even, m_pool_w, m_pool_scale, m_w_out_even, m_ln_post_even, m_ln_pre_odd, m_w_in_odd, m_sconv_w, m_dconv_w, m_dconv_b, m_cnorm_g, m_cnorm_b, m_w_out_odd, m_ln_post_odd, v_ln_pre_even, v_w_in_even, v_pool_w, v_pool_scale, v_w_out_even, v_ln_post_even, v_ln_pre_odd, v_w_in_odd, v_sconv_w, v_dconv_w, v_dconv_b, v_cnorm_g, v_cnorm_b, v_w_out_odd, v_ln_post_odd):
    xs = x[0]
    tgt = loss_target[0]
    s, d = xs.shape
    half = d // 2
    n_heads = half // HEAD_DIM
    ng = len(POOL_WINDOWS)
    cwp = half // ng
    dev = 4 * lax.axis_index("x") + 2 * lax.axis_index("y") + lax.axis_index("c")
    core = lax.axis_index("c").astype(jnp.int32).reshape(1)

    pr = pool_w.shape[2]
    cl = sconv_w.shape[2]
    small_parts = [(_rows128(ln_pre_odd), 8), (sconv_w[0], 8), (dconv_w[0], 32), (dconv_b, 8),
                   (cnorm_g, 8), (cnorm_b, 8), (_rows128(ln_post_odd), 8)]
    small_local = jnp.concatenate([_pad_rows(a, r) for a, r in small_parts], axis=0)
    g_wie, g_pw, g_small = all_gather(
        [w_in_even[0].astype(BF16), pool_w[0].reshape(ng * pr, cwp).astype(BF16), small_local], "ag_first")
    comm = _Exchanges(dev, core, d)
    in_proj_dep = comm.start_weights([w_out_even[0].astype(BF16), w_in_odd[0].astype(BF16),
                                      w_out_odd[0].astype(BF16)], after=g_wie)
    pool_full = g_pw.reshape(N_DEV, ng, pr, cwp).transpose(1, 0, 2, 3).reshape(ng, cwp, cwp)
    nl = ln_pre_odd.shape[1] // 128

    def chan(lo, rows):
        return g_small[:, lo:lo + rows].transpose(1, 0, 2).reshape(rows, N_DEV * cl)

    ln_pre_odd_f = g_small[:, 0:nl].reshape(1, d)
    sconv_f = chan(8, SCONV_K)
    dconv_f = chan(16, CONF_K)
    dconv_b_f = chan(48, 1)
    cnorm_g_f = chan(56, 1)
    cnorm_b_f = chan(64, 1)
    ln_post_odd_f = g_small[:, 72:72 + nl].reshape(1, d)

    loss_blk, grad_x, small_g = _fwd_bwd(
        xs, tgt, ln_pre_even, g_wie, pool_full, pool_scale, ln_post_even, ln_pre_odd_f,
        sconv_f, dconv_f, dconv_b_f, cnorm_g_f, cnorm_b_f, ln_post_odd_f, comm, in_proj_dep)
    small_w = [ln_pre_even, pool_scale, ln_post_even, ln_pre_odd, sconv_w[0], dconv_w[0], dconv_b, cnorm_g, cnorm_b, ln_post_odd]
    small_m = [m_ln_pre_even, m_pool_scale, m_ln_post_even, m_ln_pre_odd, m_sconv_w[0], m_dconv_w[0], m_dconv_b, m_cnorm_g, m_cnorm_b, m_ln_post_odd]
    small_v = [v_ln_pre_even, v_pool_scale, v_ln_post_even, v_ln_pre_odd, v_sconv_w[0], v_dconv_w[0], v_dconv_b, v_cnorm_g, v_cnorm_b, v_ln_post_odd]
    big = {"w_in_even": (w_in_even, m_w_in_even, v_w_in_even), "pool_w": (pool_w, m_pool_w, v_pool_w),
           "w_out_even": (w_out_even, m_w_out_even, v_w_out_even), "w_in_odd": (w_in_odd, m_w_in_odd, v_w_in_odd),
           "w_out_odd": (w_out_odd, m_w_out_odd, v_w_out_odd)}
    upd = comm.finish_updates(big, [grad_x])
    upd.update(comm.finish_updates(big, [grad_x]))
    sg, sd, sm, sv, loss = _update_small(small_g, loss_blk, small_w, small_m, small_v, dev, d, cl,
                                         deps=[upd["w_in_odd"][1], upd["w_out_even"][1]])
    upd.update(comm.finish_updates(big, sd))
    (g_wie_o, d_wie, m_wie, v_wie), (g_pw_o, d_pw, m_pw, v_pw) = upd["w_in_even"], upd["pool_w"]
    (g_woe_o, d_woe, m_woe, v_woe), (g_wio_o, d_wio, m_wio, v_wio) = upd["w_out_even"], upd["w_in_odd"]
    g_woo_o, d_woo, m_woo, v_woo = upd["w_out_odd"]

    def order(small, wie, pw, woe, wio, woo):
        return [small[0], wie, pw, small[1], woe, small[2], small[3], wio, small[4], small[5], small[6],
                small[7], small[8], woo, small[9]]

    grads = order(sg, g_wie_o, g_pw_o, g_woe_o, g_wio_o, g_woo_o)
    deltas = order(sd, d_wie, d_pw, d_woe, d_wio, d_woo)
    new_m = order(sm, m_wie, m_pw, m_woe, m_wio, m_woo)
    new_v = order(sv, v_wie, v_pw, v_woe, v_wio, v_woo)
    return (loss, grad_x[None], *grads, *deltas, *new_m, *new_v)


def _fwd_bwd(xs, tgt, ln_pre_even, g_wie, pool_full, pool_scale, ln_post_even, ln_pre_odd_f,
             sconv_f, dconv_f, dconv_b_f, cnorm_g_f, cnorm_b_f, ln_post_odd_f, comm, in_proj_dep):
    d = xs.shape[1]
    n_heads = d // 2 // HEAD_DIM
    ng, cwp = pool_full.shape[0], pool_full.shape[1]
    h0 = rms_fwd(xs, ln_pre_even, "rms_pre_even")
    p0 = mm_nn(h0, g_wie, BF16, "in_proj_even", dep=in_proj_dep)
    a0, sb_wts = sb_fwd(p0, n_heads, "sb_fwd")
    y0 = even_mix_fwd(a0, p0, pool_full, pool_scale, "even_mix_fwd")
    w_out_e, g_wio, w_out_o = comm.rest_of_weights(after=y0)
    o0 = mm_nn(y0, w_out_e, F32, "out_proj_even", tn=1024)
    x1 = postnorm_fwd(xs, o0, ln_post_even, "post_even")
    h1 = rms_fwd(x1, ln_pre_odd_f, "rms_pre_odd")
    p1 = mm_nn(h1, g_wio, BF16, "in_proj_odd")
    y1, dc = odd_mix_fwd(p1, sconv_f, dconv_f, dconv_b_f, cnorm_g_f, cnorm_b_f, "odd_mix_fwd")
    o1 = mm_nn(y1, w_out_o, F32, "out_proj_odd", tn=1024)
    loss_blk, gx2, do1, dg_post_odd = final_fwd_bwd(x1, o1, ln_post_odd_f, tgt, "post_odd_loss")

    dw_out_o = mm_tn(y1, do1, 1, BF16, "dw_out_odd")
    dy1 = mm_nt(do1, w_out_o, BF16, "dy_odd", tn=1024)
    ddc, dg2, dgam, dbet = odd_bwd_ln(dy1, p1, dc, cnorm_g_f, cnorm_b_f, "odd_bwd_ln")
    dp1, dsconv, ddconv, ddconv_b = odd_bwd_conv(dy1, p1, ddc, dg2, sconv_f, dconv_f, "odd_bwd_conv")
    dw_in_o = mm_tn(h1, dp1, N_DEV, BF16, "dw_in_odd")
    dep = comm.reduce_begin({"w_out_odd": dw_out_o.reshape(N_DEV, d // N_DEV, d), "w_in_odd": dw_in_o}, "odd")
    dh1 = mm_nt(dp1, g_wio, F32, "dh_odd", dep=dep)
    dep = comm.reduce_send(after=dh1)
    gx1, dg_pre_odd = norm_bwd(dh1, x1, ln_pre_odd_f, gx2, F32, "pre_odd_bwd", dep=dep)

    do0, dg_post_even = norm_bwd(gx1, o0, ln_post_even, None, BF16, "post_even_bwd")
    dw_out_e = mm_tn(y0, do0, 1, BF16, "dw_out_even")
    dy0 = mm_nt(do0, w_out_e, BF16, "dy_even", tn=1024)
    da0, du0, dg0, dpool, dpool_scale = even_mix_bwd(dy0, a0, p0, pool_full, pool_scale, "even_mix_bwd")
    pr = cwp // N_DEV
    dpool_slabs = dpool.astype(BF16).reshape(ng, N_DEV, pr, cwp).transpose(1, 0, 2, 3).reshape(N_DEV, ng * pr, cwp)
    dep = comm.reduce_begin({"w_out_even": dw_out_e.reshape(N_DEV, d // N_DEV, d), "pool_w": dpool_slabs}, "even_out")
    dq0, dk0, dv0 = sb_bwd(p0, a0, sb_wts, da0, n_heads, "sb_bwd", dep=dep)
    dep = comm.reduce_send(after=dq0)
    dp0 = jnp.concatenate([dq0, dk0, dv0, du0, dg0], axis=1)
    dw_in_e = mm_tn(h0, dp0, N_DEV, BF16, "dw_in_even", dep=dep)
    dep = comm.reduce_begin({"w_in_even": dw_in_e}, "even_in")
    dh0 = mm_nt(dp0, g_wie, F32, "dh_even", dep=dep)
    dep = comm.reduce_send(after=dh0)
    grad_x, dg_pre_even = norm_bwd(dh0, xs, ln_pre_even, gx1, F32, "pre_even_bwd", dep=dep)
    small_g = [dg_pre_even, dpool_scale, dg_post_even, dg_pre_odd, dsconv, ddconv, ddconv_b, dgam, dbet, dg_post_odd]
    return loss_blk, grad_x, small_g


class _Exchanges:
    def __init__(self, dev, core, d):
        self.dev = dev.astype(jnp.int32).reshape(1)
        self.core = core
        self.chip = (dev // 2).astype(jnp.int32).reshape(1)
        self.d = d
        self.weights = None
        self.to_sibling = None
        self.pending = []

    def start_weights(self, blocks, after):
        lands = [lax.empty((N_DEV,) + b.shape, b.dtype) for b in blocks]
        send, recv, srcs, lands, token = split_start("gather", blocks, lands, [after], "ag_rest_start")
        self.weights = (send, recv, srcs, lands)
        return token

    def rest_of_weights(self, after):
        send, recv, srcs, lands = self.weights
        srcs, lands = split_wait("gather", send, recv, srcs, lands, [after], "ag_rest_wait")
        lands = [place_block(l, b, self.dev, "ag_rest_own_%d" % k) for k, (l, b) in enumerate(zip(lands, srcs))]
        full = gather_finish([l.reshape((4, 2) + l.shape[1:]) for l in lands], "ag_rest_finish")
        w_out_e, g_wio, w_out_o = [f.reshape((N_DEV,) + f.shape[2:]) for f in full]
        return w_out_e.reshape(1, self.d, self.d), g_wio, w_out_o.reshape(1, self.d, self.d)

    def reduce_begin(self, partials, tag):
        names = list(partials)
        arrs = [partials[k].reshape((4, 2) + partials[k].shape[1:]) for k in names]
        lands = [lax.empty((4, 1) + a.shape[2:], a.dtype) for a in arrs]
        send, recv, srcs, lands, token = split_start("sibling", arrs, lands, [], "rs_sibling_start_" + tag)
        self.to_sibling = (tag, names, send, recv, srcs, lands)
        return token

    def reduce_send(self, after):
        tag, names, send, recv, srcs, lands = self.to_sibling
        srcs, lands = split_wait("sibling", send, recv, srcs, lands, [after], "rs_sibling_wait_" + tag)
        sums = [pair_add(o, r, self.core, "rs_pair_add_" + k) for k, o, r in zip(names, srcs, lands)]
        zones = [lax.empty(a.shape, a.dtype) for a in sums]
        send, recv, srcs, zones, token = split_start("scatter", sums, zones, [], "rs_start_" + tag)
        self.pending.append((tag, names, send, recv, srcs, zones))
        return token

    def finish_updates(self, big, afters):
        tag, names, send, recv, srcs, lands = self.pending.pop(0)
        srcs, lands = split_wait("scatter", send, recv, srcs, lands, afters, "rs_wait_" + tag)
        out = {}
        for name, own, got in zip(names, srcs, lands):
            w, m, v = big[name]
            shp = own.shape[1:]
            outs = adamw_big(w.reshape(shp), m.reshape(shp), v.reshape(shp), own, got, self.chip, "adamw_" + name)
            out[name] = [o.reshape(w.shape) for o in outs]
        return out


def _update_small(small_g, loss_blk, small_w, small_m, small_v, dev, d, cl, deps):
    packed = jnp.concatenate([_rows128(g) for g in small_g] + [loss_blk], axis=0)
    (g8,) = all_gather([packed], "ag_small_grads", deps)
    tot = sum_devices(g8, "sum_small_grads")
    loss = tot[packed.shape[0] - 8, 0]
    full_g = []
    lo = 0
    for g in small_g:
        rows = g.size // 128
        full_g.append(tot[lo:lo + rows].reshape(g.shape))
        lo += rows

    def mine(g, width):
        return lax.dynamic_slice_in_dim(g, dev * width, width, axis=g.ndim - 1)

    fg = full_g
    small_gl = [fg[0], fg[1], fg[2], mine(fg[3], d // N_DEV), mine(fg[4], cl), mine(fg[5], cl), mine(fg[6], cl),
                mine(fg[7], cl), mine(fg[8], cl), mine(fg[9], d // N_DEV)]
    sd, sm, sv = adamw_small(small_w, small_gl, small_m, small_v, "adamw_small")

    def like(k, a):
        return a[None] if k in (4, 5) else a

    sg = [like(k, a) for k, a in enumerate(small_gl)]
    sd = [like(k, a) for k, a in enumerate(sd)]
    sm = [like(k, a) for k, a in enumerate(sm)]
    sv = [like(k, a) for k, a in enumerate(sv)]
    return sg, sd, sm, sv, loss
```

```python
import functools
import math

import jax
import jax.numpy as jnp
from jax import lax
from jax.experimental import pallas as pl
from jax.experimental.pallas import tpu as pltpu

F32 = jnp.float32
BF16 = jnp.bfloat16
EPS = 1e-6
HEAD_DIM = 128
POOL_WINDOWS = (2, 4, 8, 16)
SCONV_K = 3
CONF_K = 31
HALO = 32
N_DEV = 8
VMEM_LIMIT = 56 * 1024 * 1024
MESH = pl.DeviceIdType.MESH

ADAM_LR = 0.001
ADAM_B1 = 0.9
ADAM_B2 = 0.999
ADAM_EPS = 1e-08
ADAM_WD = 0.01
ADAM_STEP = 10


def _params(*sem):
    return pltpu.CompilerParams(dimension_semantics=sem, vmem_limit_bytes=VMEM_LIMIT)


def _sigmoid(v):
    return 1.0 / (1.0 + jnp.exp(-v))


def _silu(v):
    return v * _sigmoid(v)


def _silu_and_grad(v):
    s = _sigmoid(v)
    return v * s, s * (1.0 + v * (1.0 - s))


def _rowsum8(v):
    r, c = v.shape
    return jnp.sum(v.reshape(r // 8, 8, c), axis=0)


def _tap_before(xx, i, rows):
    if i == 0:
        return xx[HALO:HALO + rows]
    return pltpu.roll(xx, i, 0)[HALO:HALO + rows]


def _tap_after(xx, i, rows):
    if i == 0:
        return xx[0:rows]
    return pltpu.roll(xx, xx.shape[0] - i, 0)[0:rows]


def rms_fwd(x, g, name, tm=256):
    s, d = x.shape

    def body(x_ref, g_ref, h_ref):
        xv = x_ref[...]
        r = lax.rsqrt(jnp.mean(xv * xv, axis=-1, keepdims=True) + EPS)
        h_ref[...] = (xv * r * g_ref[...]).astype(BF16)

    return pl.pallas_call(
        body, name=name, grid=(s // tm,),
        in_specs=[pl.BlockSpec((tm, d), lambda i: (i, 0)), pl.BlockSpec((1, d), lambda i: (0, 0))],
        out_specs=pl.BlockSpec((tm, d), lambda i: (i, 0)),
        out_shape=jax.ShapeDtypeStruct((s, d), BF16),
        compiler_params=_params("parallel"),
    )(x, g)


def postnorm_fwd(x, o, g, name, tm=256):
    s, d = x.shape

    def body(x_ref, o_ref, g_ref, y_ref):
        ov = o_ref[...]
        r = lax.rsqrt(jnp.mean(ov * ov, axis=-1, keepdims=True) + EPS)
        y_ref[...] = x_ref[...] + ov * r * g_ref[...]

    return pl.pallas_call(
        body, name=name, grid=(s // tm,),
        in_specs=[pl.BlockSpec((tm, d), lambda i: (i, 0)), pl.BlockSpec((tm, d), lambda i: (i, 0)),
                  pl.BlockSpec((1, d), lambda i: (0, 0))],
        out_specs=pl.BlockSpec((tm, d), lambda i: (i, 0)),
        out_shape=jax.ShapeDtypeStruct((s, d), F32),
        compiler_params=_params("parallel"),
    )(x, o, g)


def final_fwd_bwd(x1, o, g, target, name, tm=256):
    s, d = x1.shape
    n = s // tm

    def body(x_ref, o_ref, g_ref, t_ref, loss_ref, gx_ref, do_ref, dg_ref, lacc, gacc):
        i = pl.program_id(0)

        @pl.when(i == 0)
        def _():
            lacc[...] = jnp.zeros_like(lacc)
            gacc[...] = jnp.zeros_like(gacc)

        ov = o_ref[...]
        gv = g_ref[...]
        r = lax.rsqrt(jnp.mean(ov * ov, axis=-1, keepdims=True) + EPS)
        oh = ov * r
        diff = x_ref[...] + oh * gv - t_ref[...]
        lacc[...] += _rowsum8(diff * diff)
        gx = diff * (1.0 / d)
        gx_ref[...] = gx
        gacc[...] += _rowsum8(gx * oh)
        dn = gx * gv
        do_ref[...] = (r * (dn - oh * jnp.mean(dn * oh, axis=-1, keepdims=True))).astype(BF16)

        @pl.when(i == n - 1)
        def _():
            tot = jnp.sum(jnp.sum(lacc[...], axis=0, keepdims=True), axis=1, keepdims=True)
            loss_ref[...] = jnp.broadcast_to(tot * (0.5 / d), loss_ref.shape)
            dg_ref[...] = jnp.sum(gacc[...], axis=0, keepdims=True)

    row = pl.BlockSpec((tm, d), lambda i: (i, 0))
    vec = pl.BlockSpec((1, d), lambda i: (0, 0))
    return pl.pallas_call(
        body, name=name, grid=(n,),
        in_specs=[row, row, vec, row],
        out_specs=[pl.BlockSpec((8, 128), lambda i: (0, 0)), row, row, vec],
        out_shape=[jax.ShapeDtypeStruct((8, 128), F32), jax.ShapeDtypeStruct((s, d), F32),
                   jax.ShapeDtypeStruct((s, d), BF16), jax.ShapeDtypeStruct((1, d), F32)],
        scratch_shapes=[pltpu.VMEM((8, d), F32), pltpu.VMEM((8, d), F32)],
        compiler_params=_params("arbitrary"),
    )(x1, o, g, target)


def norm_bwd(dy, inp, g, resid, out_dtype, name, tm=256, dep=None):
    s, d = inp.shape
    n = s // tm
    has_resid = resid is not None

    def body(*refs):
        dy_ref, x_ref, g_ref = refs[:3]
        r_ref = refs[3] if has_resid else None
        dx_ref, dg_ref, gacc = refs[-3:]
        i = pl.program_id(0)

        @pl.when(i == 0)
        def _():
            gacc[...] = jnp.zeros_like(gacc)

        xv = x_ref[...]
        dyv = dy_ref[...].astype(F32)
        r = lax.rsqrt(jnp.mean(xv * xv, axis=-1, keepdims=True) + EPS)
        xh = xv * r
        gacc[...] += _rowsum8(dyv * xh)
        dn = dyv * g_ref[...]
        dx = r * (dn - xh * jnp.mean(dn * xh, axis=-1, keepdims=True))
        if has_resid:
            dx = dx + r_ref[...]
        dx_ref[...] = dx.astype(out_dtype)

        @pl.when(i == n - 1)
        def _():
            dg_ref[...] = jnp.sum(gacc[...], axis=0, keepdims=True)

    row = pl.BlockSpec((tm, d), lambda i: (i, 0))
    vec = pl.BlockSpec((1, d), lambda i: (0, 0))
    dep_args, dep_specs = _after(dep)
    args = [dy, inp, g] + ([resid] if has_resid else []) + dep_args
    return pl.pallas_call(
        body, name=name, grid=(n,),
        in_specs=[row, row, vec] + ([row] if has_resid else []) + dep_specs,
        out_specs=[row, vec],
        out_shape=[jax.ShapeDtypeStruct((s, d), out_dtype), jax.ShapeDtypeStruct((1, d), F32)],
        scratch_shapes=[pltpu.VMEM((8, d), F32)],
        compiler_params=_params("arbitrary"),
    )(*args)


def _after(dep):
    if dep is None:
        return [], []
    return [dep], [pl.BlockSpec((8, 128), lambda *_: (0, 0))]


def mm_nn(a, w, out_dtype, name, tm=2048, tn=None, dep=None):
    m, k = a.shape
    tm = min(tm, m)
    ns, _, n = w.shape
    tn = n if tn is None else tn
    nj = n // tn
    dep_args, dep_specs = _after(dep)

    def body(a_ref, w_ref, *rest):
        o_ref = rest[-1]
        o_ref[...] = jnp.dot(a_ref[...], w_ref[0], preferred_element_type=F32).astype(out_dtype)

    return pl.pallas_call(
        body, name=name, grid=(ns, nj, m // tm),
        in_specs=[pl.BlockSpec((tm, k), lambda s, j, i: (i, 0)),
                  pl.BlockSpec((1, k, tn), lambda s, j, i: (s, 0, j))] + dep_specs,
        out_specs=pl.BlockSpec((tm, tn), lambda s, j, i: (i, s * nj + j)),
        out_shape=jax.ShapeDtypeStruct((m, ns * n), out_dtype),
        compiler_params=_params("parallel", "parallel", "parallel"),
    )(a, w, *dep_args)


def mm_nt(a, w, out_dtype, name, tm=1024, tn=None, dep=None):
    m = a.shape[0]
    tm = min(tm, m)
    ns, k, n = w.shape
    tn = n if tn is None else tn
    nj = n // tn
    steps = ns * nj
    dep_args, dep_specs = _after(dep)

    def body(a_ref, w_ref, *rest):
        o_ref, acc = rest[-2:]
        r = pl.program_id(1)

        @pl.when(r == 0)
        def _():
            acc[...] = jnp.zeros_like(acc)

        acc[...] += lax.dot_general(a_ref[...], w_ref[0], (((1,), (1,)), ((), ())),
                                    preferred_element_type=F32)

        @pl.when(r == steps - 1)
        def _():
            o_ref[...] = acc[...].astype(out_dtype)

    return pl.pallas_call(
        body, name=name, grid=(m // tm, steps),
        in_specs=[pl.BlockSpec((tm, tn), lambda i, r: (i, r)),
                  pl.BlockSpec((1, k, tn), lambda i, r: (r // nj, 0, r % nj))] + dep_specs,
        out_specs=pl.BlockSpec((tm, k), lambda i, r: (i, 0)),
        out_shape=jax.ShapeDtypeStruct((m, k), out_dtype),
        scratch_shapes=[pltpu.VMEM((tm, k), F32)],
        compiler_params=_params("parallel", "arbitrary"),
    )(a, w, *dep_args)


def mm_tn(a, b, ns, out_dtype, name, tk=1024, tm=2048, dep=None):
    m, k = a.shape
    tm = min(tm, m)
    n = b.shape[1] // ns
    steps = m // tm
    dep_args, dep_specs = _after(dep)

    def body(a_ref, b_ref, *rest):
        o_ref, acc = rest[-2:]
        r = pl.program_id(2)

        @pl.when(r == 0)
        def _():
            acc[...] = jnp.zeros_like(acc)

        acc[...] += lax.dot_general(a_ref[...], b_ref[...], (((0,), (0,)), ((), ())),
                                    preferred_element_type=F32)

        @pl.when(r == steps - 1)
        def _():
            o_ref[0] = acc[...].astype(out_dtype)

    return pl.pallas_call(
        body, name=name, grid=(ns, k // tk, steps),
        in_specs=[pl.BlockSpec((tm, tk), lambda s, j, r: (r, j)),
                  pl.BlockSpec((tm, n), lambda s, j, r: (r, s))] + dep_specs,
        out_specs=pl.BlockSpec((1, tk, n), lambda s, j, r: (s, j, 0)),
        out_shape=jax.ShapeDtypeStruct((ns, k, n), out_dtype),
        scratch_shapes=[pltpu.VMEM((tk, n), F32)],
        compiler_params=_params("parallel", "parallel", "arbitrary"),
    )(a, b, *dep_args)


SB_BLK = 128


LOG2E = 1.0 / math.log(2.0)


def _split_dot(v, tri):
    hi = pltpu.bitcast(pltpu.bitcast(v, jnp.uint32) & jnp.uint32(0xFFFF0000), F32)
    lo = (v - hi).astype(BF16)
    return (jnp.dot(hi.astype(BF16), tri, preferred_element_type=F32)
            + jnp.dot(lo, tri, preferred_element_type=F32))


def _sb_scores(z2, lim, dcol, tri_ex, masked):
    sp = jnp.log2(1.0 + jnp.exp2(-jnp.abs(z2)))
    lb = jnp.minimum(z2, 0.0) - sp
    l1m = lb - z2
    mask = None
    if masked:
        mask = dcol < lim
        l1m = jnp.where(mask, l1m, 0.0)
    return mask, lb, l1m, _split_dot(l1m, tri_ex)


def _sb_consts():
    row = lax.broadcasted_iota(jnp.int32, (SB_BLK, SB_BLK), 0)
    col = lax.broadcasted_iota(jnp.int32, (SB_BLK, SB_BLK), 1)
    tri_ex = jnp.where(row > col, 1.0, 0.0).astype(BF16)
    tri_in = jnp.where(row >= col, 1.0, 0.0).astype(BF16)
    return col - row, tri_ex, tri_in


def sb_fwd(p, n_heads, name, tq=256, nsub=4):
    s = p.shape[0]
    h_n = n_heads
    b = SB_BLK
    nqs = tq // b
    tk = nsub * b
    scale = 1.0 / math.sqrt(HEAD_DIM)

    def body(q_ref, k_ref, v_ref, o_ref, w_ref):
        qi = pl.program_id(1)
        dcol, tri_ex, _ = _sb_consts()
        qv = [q_ref[qs * b:(qs + 1) * b, :] for qs in range(nqs)]
        n_groups = ((qi + 1) * nqs - 1) // nsub + 1

        def step(it, carry, masked):
            c1s, accs = carry
            g = n_groups - 1 - it
            off = pl.multiple_of(g * tk, tk)
            kg = k_ref[pl.ds(off, tk), :]
            vg = v_ref[pl.ds(off, tk), :]
            new_c1, new_acc = [], []
            for qs in range(nqs):
                qb = qi * nqs + qs
                z2 = lax.dot_general(qv[qs], kg, (((1,), (1,)), ((), ())),
                                     preferred_element_type=F32) * (scale * LOG2E)
                blocks = [_sb_scores(z2[:, j * b:(j + 1) * b], (qb - (g * nsub + j)) * b, dcol, tri_ex, masked)
                          for j in range(nsub)]
                run = c1s[qs]
                ws = [None] * nsub
                for j in reversed(range(nsub)):
                    mask, lb, l1m, ls_loc = blocks[j]
                    wj = jnp.exp2(lb + ls_loc + run)
                    ws[j] = (jnp.where(mask, wj, 0.0) if masked else wj).astype(BF16)
                    run = run + jnp.sum(l1m, axis=1, keepdims=True)
                w = jnp.concatenate(ws, axis=1)
                w_ref[0, g, qs * b:(qs + 1) * b, :] = w
                new_acc.append(accs[qs] + jnp.dot(w, vg, preferred_element_type=F32))
                new_c1.append(run)
            return tuple(new_c1), tuple(new_acc)

        init = (tuple(jnp.zeros((b, 1), F32) for _ in range(nqs)),
                tuple(jnp.zeros((b, HEAD_DIM), F32) for _ in range(nqs)))
        assert nqs == 2 and nsub % 2 == 0
        first = step(0, init, True)
        _, accs = lax.fori_loop(1, n_groups, functools.partial(step, masked=False), first)
        for qs in range(nqs):
            o_ref[qs * b:(qs + 1) * b, :] = accs[qs]

    return pl.pallas_call(
        body, name=name, grid=(h_n, s // tq),
        in_specs=[pl.BlockSpec((tq, HEAD_DIM), lambda h, i: (i, h)),
                  pl.BlockSpec((s, HEAD_DIM), lambda h, i: (0, h_n + h)),
                  pl.BlockSpec((s, HEAD_DIM), lambda h, i: (0, 2 * h_n + h))],
        out_specs=[pl.BlockSpec((tq, HEAD_DIM), lambda h, i: (i, h)),
                   pl.BlockSpec((1, s // tk, tq, tk), lambda h, i: (h, 0, i, 0))],
        out_shape=[jax.ShapeDtypeStruct((s, h_n * HEAD_DIM), F32),
                   jax.ShapeDtypeStruct((h_n, s // tk, s, tk), BF16)],
        compiler_params=_params("parallel", "arbitrary"),
    )(p, p, p)


def sb_bwd(p, a, wts, da, n_heads, name, tq=256, dep=None):
    s = p.shape[0]
    h_n = n_heads
    nq = s // tq
    b = SB_BLK
    nqs = tq // b
    tk = wts.shape[3]
    nsub = tk // b
    scale = 1.0 / math.sqrt(HEAD_DIM)
    dep_args, dep_specs = _after(dep)

    def body(q_ref, k_ref, v_ref, a_ref, da_ref, w_ref, *rest):
        dq_ref, dk_ref, dv_ref, dk_acc, dv_acc = rest[-5:]
        qi = pl.program_id(1)

        @pl.when(qi == 0)
        def _():
            dk_acc[...] = jnp.zeros_like(dk_acc)
            dv_acc[...] = jnp.zeros_like(dv_acc)

        dcol, _, tri_in = _sb_consts()
        q_all = q_ref[...]
        do_all = da_ref[...]
        qv = [q_ref[qs * b:(qs + 1) * b, :] for qs in range(nqs)]
        dov = [da_ref[qs * b:(qs + 1) * b, :] for qs in range(nqs)]
        tots = [jnp.sum(dov[qs].astype(F32) * a_ref[qs * b:(qs + 1) * b, :], axis=1, keepdims=True)
                for qs in range(nqs)]
        n_groups = ((qi + 1) * nqs - 1) // nsub + 1

        def step(it, carry, masked):
            c2s, dqs = carry
            g = n_groups - 1 - it
            off = pl.multiple_of(g * tk, tk)
            kg = k_ref[pl.ds(off, tk), :]
            vg = v_ref[pl.ds(off, tk), :]
            w_all = w_ref[0, g]
            new_c2, new_dq, dz_rows = [], [], []
            for qs in range(nqs):
                qb = qi * nqs + qs
                z2 = lax.dot_general(qv[qs], kg, (((1,), (1,)), ((), ())),
                                     preferred_element_type=F32) * (-scale * LOG2E)
                dw = lax.dot_general(dov[qs], vg, (((1,), (1,)), ((), ())), preferred_element_type=F32)
                beta = 1.0 / (1.0 + jnp.exp2(z2))
                e = dw * w_all[qs * b:(qs + 1) * b, :].astype(F32)
                run2 = c2s[qs]
                dzs = [None] * nsub
                for j in reversed(range(nsub)):
                    cols = slice(j * b, (j + 1) * b)
                    later = _split_dot(e[:, cols], tri_in) + run2
                    bj = beta[:, cols]
                    dz = (e[:, cols] * (1.0 - bj) - bj * (tots[qs] - later)) * scale
                    if masked:
                        dz = jnp.where(dcol < (qb - (g * nsub + j)) * b, dz, 0.0)
                    dzs[j] = dz.astype(BF16)
                    run2 = run2 + jnp.sum(e[:, cols], axis=1, keepdims=True)
                dzq = jnp.concatenate(dzs, axis=1)
                new_dq.append(dqs[qs] + jnp.dot(dzq, kg, preferred_element_type=F32))
                new_c2.append(run2)
                dz_rows.append(dzq)
            dz_all = jnp.concatenate(dz_rows, axis=0)
            dk_acc[pl.ds(off, tk), :] += lax.dot_general(dz_all, q_all, (((0,), (0,)), ((), ())),
                                                         preferred_element_type=F32)
            dv_acc[pl.ds(off, tk), :] += lax.dot_general(w_all, do_all, (((0,), (0,)), ((), ())),
                                                         preferred_element_type=F32)
            return tuple(new_c2), tuple(new_dq)

        zeros = tuple(jnp.zeros((b, 1), F32) for _ in range(nqs))
        assert nqs == 2 and nsub % 2 == 0
        first = step(0, (zeros, tuple(jnp.zeros((b, HEAD_DIM), F32) for _ in range(nqs))), True)
        _, dqs = lax.fori_loop(1, n_groups, functools.partial(step, masked=False), first)
        for qs in range(nqs):
            dq_ref[qs * b:(qs + 1) * b, :] = dqs[qs].astype(BF16)

        @pl.when(qi == nq - 1)
        def _():
            dk_ref[...] = dk_acc[...].astype(BF16)
            dv_ref[...] = dv_acc[...].astype(BF16)

    blk = pl.BlockSpec((tq, HEAD_DIM), lambda h, i: (i, h))
    full = pl.BlockSpec((s, HEAD_DIM), lambda h, i: (0, h))
    return pl.pallas_call(
        body, name=name, grid=(h_n, nq),
        in_specs=[blk, pl.BlockSpec((s, HEAD_DIM), lambda h, i: (0, h_n + h)),
                  pl.BlockSpec((s, HEAD_DIM), lambda h, i: (0, 2 * h_n + h)), blk, blk,
                  pl.BlockSpec((1, s // tk, tq, tk), lambda h, i: (h, 0, i, 0))] + dep_specs,
        out_specs=[blk, full, full],
        out_shape=[jax.ShapeDtypeStruct((s, h_n * HEAD_DIM), BF16)] * 3,
        scratch_shapes=[pltpu.VMEM((s, HEAD_DIM), F32), pltpu.VMEM((s, HEAD_DIM), F32)],
        compiler_params=_params("parallel", "arbitrary"),
    )(p, p, p, a, da, wts, *dep_args)


def _pool_window(xx, win, r0, rc):
    cur = xx[HALO:HALO + rc]
    ws = cur
    for i in range(1, win):
        ws = ws + _tap_before(xx, i, rc)
    t_idx = r0 + lax.broadcasted_iota(jnp.int32, (rc, 1), 0)
    inv = 1.0 / jnp.minimum(win, t_idx + 1).astype(F32)
    return ws * inv - cur, inv


def even_mix_fwd(a, p, pool_w, pool_scale, name, rc=64):
    s = p.shape[0]
    ng = len(POOL_WINDOWS)
    cw = pool_w.shape[1]
    n_chunks = s // rc

    def body(a_ref, u_ref, g_ref, w_ref, sc_ref, y_ref, upad):
        j = pl.program_id(0)

        @pl.when(j < ng)
        def _():
            def chunk(ci, carry):
                rows = pl.ds(pl.multiple_of(ci * rc, rc), rc)
                y_ref[rows, :] = (a_ref[rows, :] * _silu(g_ref[rows, :].astype(F32))).astype(BF16)
                return carry

            lax.fori_loop(0, n_chunks, chunk, 0)

        for gi, win in enumerate(POOL_WINDOWS):
            @pl.when(j == ng + gi)
            def _(win=win):
                upad[0:HALO, :] = jnp.zeros((HALO, cw), F32)

                def fill(ci, carry):
                    r0 = pl.multiple_of(ci * rc, rc)
                    upad[pl.ds(pl.multiple_of(r0 + HALO, HALO), rc), :] = u_ref[pl.ds(r0, rc), :].astype(F32)
                    return carry

                lax.fori_loop(0, n_chunks, fill, 0)

                def chunk(ci, carry):
                    r0 = pl.multiple_of(ci * rc, rc)
                    rows = pl.ds(r0, rc)
                    pooled, _ = _pool_window(upad[pl.ds(r0, HALO + rc), :], win, r0, rc)
                    t = jnp.dot(pooled.astype(BF16), w_ref[0], preferred_element_type=F32)
                    y_ref[rows, :] = (t * sc_ref[...] * _silu(g_ref[rows, :].astype(F32))).astype(BF16)
                    return carry

                lax.fori_loop(0, n_chunks, chunk, 0)

    grp = lambda j: jnp.maximum(j - ng, 0)
    return pl.pallas_call(
        body, name=name, grid=(2 * ng,),
        in_specs=[pl.BlockSpec((s, cw), lambda j: (0, jnp.minimum(j, ng - 1))),
                  pl.BlockSpec((s, cw), lambda j: (0, 3 * ng + grp(j))),
                  pl.BlockSpec((s, cw), lambda j: (0, 4 * ng + j)),
                  pl.BlockSpec((1, cw, cw), lambda j: (grp(j), 0, 0)),
                  pl.BlockSpec((1, cw), lambda j: (0, grp(j)))],
        out_specs=pl.BlockSpec((s, cw), lambda j: (0, j)),
        out_shape=jax.ShapeDtypeStruct((s, 2 * ng * cw), BF16),
        scratch_shapes=[pltpu.VMEM((HALO + s, cw), F32)],
        compiler_params=_params("arbitrary"),
    )(a, p, p, pool_w, pool_scale)


def even_mix_bwd(dy, a, p, pool_w, pool_scale, name, rc=64):
    s = p.shape[0]
    ng = len(POOL_WINDOWS)
    cw = pool_w.shape[1]
    n_chunks = s // rc

    def body(dy_ref, a_ref, u_ref, g_ref, w_ref, sc_ref, da_ref, du_ref, dg_ref, dw_ref, dsc_ref,
             upad, rpad, dpl, dw_acc, dsc_acc):
        j = pl.program_id(0)

        @pl.when(j < ng)
        def _():
            def chunk(ci, carry):
                rows = pl.ds(pl.multiple_of(ci * rc, rc), rc)
                dyv = dy_ref[rows, :].astype(F32)
                sg, dsg = _silu_and_grad(g_ref[rows, :].astype(F32))
                da_ref[rows, :] = (dyv * sg).astype(BF16)
                dg_ref[rows, :] = (dyv * a_ref[rows, :] * dsg).astype(BF16)
                return carry

            lax.fori_loop(0, n_chunks, chunk, 0)

        for gi, win in enumerate(POOL_WINDOWS):
            @pl.when(j == ng + gi)
            def _(win=win):
                upad[0:HALO, :] = jnp.zeros((HALO, cw), F32)
                rpad[s:s + HALO, :] = jnp.zeros((HALO, cw), F32)
                dw_acc[...] = jnp.zeros_like(dw_acc)
                dsc_acc[...] = jnp.zeros_like(dsc_acc)

                def fill(ci, carry):
                    r0 = pl.multiple_of(ci * rc, rc)
                    upad[pl.ds(pl.multiple_of(r0 + HALO, HALO), rc), :] = u_ref[pl.ds(r0, rc), :].astype(F32)
                    return carry

                lax.fori_loop(0, n_chunks, fill, 0)

                def chunk(ci, carry):
                    r0 = pl.multiple_of(ci * rc, rc)
                    rows = pl.ds(r0, rc)
                    pooled, inv = _pool_window(upad[pl.ds(r0, HALO + rc), :], win, r0, rc)
                    pb = pooled.astype(BF16)
                    wv = w_ref[0]
                    t = jnp.dot(pb, wv, preferred_element_type=F32)
                    scv = sc_ref[...]
                    dyv = dy_ref[rows, :].astype(F32)
                    sg, dsg = _silu_and_grad(g_ref[rows, :].astype(F32))
                    dpo = dyv * sg
                    dg_ref[rows, :] = (dyv * t * scv * dsg).astype(BF16)
                    dsc_acc[...] += _rowsum8(dpo * t)
                    dtb = (dpo * scv).astype(BF16)
                    dw_acc[...] += lax.dot_general(pb, dtb, (((0,), (0,)), ((), ())),
                                                   preferred_element_type=F32)
                    dpooled = lax.dot_general(dtb, wv, (((1,), (1,)), ((), ())),
                                              preferred_element_type=F32)
                    dpl[rows, :] = dpooled
                    rpad[rows, :] = dpooled * inv
                    return carry

                lax.fori_loop(0, n_chunks, chunk, 0)

                def chunk2(ci, carry):
                    r0 = pl.multiple_of(ci * rc, rc)
                    rows = pl.ds(r0, rc)
                    xx = rpad[pl.ds(r0, rc + HALO), :]
                    fs = xx[0:rc]
                    for i in range(1, win):
                        fs = fs + _tap_after(xx, i, rc)
                    du_ref[rows, :] = (fs - dpl[rows, :]).astype(BF16)
                    return carry

                lax.fori_loop(0, n_chunks, chunk2, 0)
                dw_ref[0] = dw_acc[...]
                dsc_ref[...] = jnp.sum(dsc_acc[...], axis=0, keepdims=True)

    grp = lambda j: jnp.maximum(j - ng, 0)
    att = lambda j: jnp.minimum(j, ng - 1)
    return pl.pallas_call(
        body, name=name, grid=(2 * ng,),
        in_specs=[pl.BlockSpec((s, cw), lambda j: (0, j)),
                  pl.BlockSpec((s, cw), lambda j: (0, att(j))),
                  pl.BlockSpec((s, cw), lambda j: (0, 3 * ng + grp(j))),
                  pl.BlockSpec((s, cw), lambda j: (0, 4 * ng + j)),
                  pl.BlockSpec((1, cw, cw), lambda j: (grp(j), 0, 0)),
                  pl.BlockSpec((1, cw), lambda j: (0, grp(j)))],
        out_specs=[pl.BlockSpec((s, cw), lambda j: (0, att(j))),
                   pl.BlockSpec((s, cw), lambda j: (0, grp(j))),
                   pl.BlockSpec((s, cw), lambda j: (0, j)),
                   pl.BlockSpec((1, cw, cw), lambda j: (grp(j), 0, 0)),
                   pl.BlockSpec((1, cw), lambda j: (0, grp(j)))],
        out_shape=[jax.ShapeDtypeStruct((s, ng * cw), BF16), jax.ShapeDtypeStruct((s, ng * cw), BF16),
                   jax.ShapeDtypeStruct((s, 2 * ng * cw), BF16),
                   jax.ShapeDtypeStruct((ng, cw, cw), F32), jax.ShapeDtypeStruct((1, ng * cw), F32)],
        scratch_shapes=[pltpu.VMEM((HALO + s, cw), F32), pltpu.VMEM((s + HALO, cw), F32),
                        pltpu.VMEM((s, cw), F32), pltpu.VMEM((cw, cw), F32), pltpu.VMEM((8, cw), F32)],
        compiler_params=_params("arbitrary"),
    )(dy, a, p, p, pool_w, pool_scale)


def _halo_before(tm):
    return lambda i: jnp.maximum(i * (tm // HALO) - 1, 0)


def _halo_after(tm, s):
    return lambda i: jnp.minimum((i + 1) * (tm // HALO), s // HALO - 1)


def odd_mix_fwd(p, sconv_w, dconv_w, dconv_b, cnorm_g, cnorm_b, name, tm=128):
    s = p.shape[0]
    cw = sconv_w.shape[1]
    n = s // tm
    lanes = 128
    hb = _halo_before(tm)

    def body(hc_ref, hch_ref, bc_ref, cc_ref, cch_ref, ga_ref, gah_ref, gb_ref, gbh_ref, g1_ref, g2_ref,
             sw_ref, dw_ref, db_ref, gam_ref, bet_ref, y_ref, dc_ref):
        first = pl.program_id(0) == 0
        for l in range(cw // lanes):
            cols = slice(l * lanes, (l + 1) * lanes)
            mh = jnp.where(first, 0.0, cch_ref[:, cols].astype(F32) * hch_ref[:, cols].astype(F32))
            mm = cc_ref[:, cols].astype(F32) * hc_ref[:, cols].astype(F32)
            xx = jnp.concatenate([mh, mm], axis=0)
            cv = jnp.zeros((tm, lanes), F32)
            for k in range(SCONV_K):
                cv = cv + sw_ref[k:k + 1, cols] * _tap_before(xx, SCONV_K - 1 - k, tm)
            c_out = bc_ref[:, cols].astype(F32) * cv
            y_ref[:, cols] = (c_out * _silu(g1_ref[:, cols].astype(F32))).astype(BF16)
            dh = jnp.where(first, 0.0, gah_ref[:, cols].astype(F32) * _sigmoid(gbh_ref[:, cols].astype(F32)))
            dm = ga_ref[:, cols].astype(F32) * _sigmoid(gb_ref[:, cols].astype(F32))
            xx = jnp.concatenate([dh, dm], axis=0)
            acc = jnp.zeros((tm, lanes), F32) + db_ref[:, cols]
            for k in range(CONF_K):
                acc = acc + dw_ref[k:k + 1, cols] * _tap_before(xx, CONF_K - 1 - k, tm)
            dc_ref[:, cols] = acc
        rs = 32
        for r in range(tm // rs):
            rows = slice(r * rs, (r + 1) * rs)
            xv = dc_ref[rows, :]
            mu = jnp.mean(xv, axis=-1, keepdims=True)
            xc = xv - mu
            rstd = lax.rsqrt(jnp.mean(xc * xc, axis=-1, keepdims=True) + EPS)
            ln = xc * rstd * gam_ref[...] + bet_ref[...]
            y_ref[rows, cw:2 * cw] = (_silu(ln) * _silu(g2_ref[rows, :].astype(F32))).astype(BF16)

    main = lambda c: pl.BlockSpec((tm, cw), lambda i: (i, c))
    halo = lambda c: pl.BlockSpec((HALO, cw), lambda i: (hb(i), c))
    vec = lambda r: pl.BlockSpec((r, cw), lambda i: (0, 0))
    return pl.pallas_call(
        body, name=name, grid=(n,),
        in_specs=[main(0), halo(0), main(1), main(2), halo(2), main(3), halo(3), main(4), halo(4),
                  main(5), main(6), vec(SCONV_K), vec(CONF_K), vec(1), vec(1), vec(1)],
        out_specs=[pl.BlockSpec((tm, 2 * cw), lambda i: (i, 0)), pl.BlockSpec((tm, cw), lambda i: (i, 0))],
        out_shape=[jax.ShapeDtypeStruct((s, 2 * cw), BF16), jax.ShapeDtypeStruct((s, cw), F32)],
        compiler_params=_params("parallel"),
    )(p, p, p, p, p, p, p, p, p, p, p, sconv_w, dconv_w, dconv_b, cnorm_g, cnorm_b)


def odd_bwd_ln(dy, p, dc, cnorm_g, cnorm_b, name, tm=256):
    s = p.shape[0]
    cw = dc.shape[1]
    n = s // tm
    rs = 32

    def body(dy_ref, g2_ref, dc_ref, gam_ref, bet_ref, ddc_ref, dg_ref, dgam_ref, dbet_ref, gacc, bacc):
        i = pl.program_id(0)

        @pl.when(i == 0)
        def _():
            gacc[...] = jnp.zeros_like(gacc)
            bacc[...] = jnp.zeros_like(bacc)

        def chunk(ci, carry):
            rows = pl.ds(pl.multiple_of(ci * rs, rs), rs)
            xv = dc_ref[rows, :]
            mu = jnp.mean(xv, axis=-1, keepdims=True)
            xc = xv - mu
            rstd = lax.rsqrt(jnp.mean(xc * xc, axis=-1, keepdims=True) + EPS)
            xh = xc * rstd
            gam = gam_ref[...]
            sl, dsl = _silu_and_grad(xh * gam + bet_ref[...])
            sg, dsg = _silu_and_grad(g2_ref[rows, :].astype(F32))
            dyv = dy_ref[rows, :].astype(F32)
            dg_ref[rows, :] = (dyv * sl * dsg).astype(BF16)
            dln = dyv * sg * dsl
            gacc[...] += _rowsum8(dln * xh)
            bacc[...] += _rowsum8(dln)
            dxh = dln * gam
            ddc_ref[rows, :] = rstd * (dxh - jnp.mean(dxh, axis=-1, keepdims=True)
                                       - xh * jnp.mean(dxh * xh, axis=-1, keepdims=True))
            return carry

        lax.fori_loop(0, tm // rs, chunk, 0)

        @pl.when(i == n - 1)
        def _():
            dgam_ref[...] = jnp.sum(gacc[...], axis=0, keepdims=True)
            dbet_ref[...] = jnp.sum(bacc[...], axis=0, keepdims=True)

    vec = pl.BlockSpec((1, cw), lambda i: (0, 0))
    return pl.pallas_call(
        body, name=name, grid=(n,),
        in_specs=[pl.BlockSpec((tm, cw), lambda i: (i, 1)), pl.BlockSpec((tm, cw), lambda i: (i, 6)),
                  pl.BlockSpec((tm, cw), lambda i: (i, 0)), vec, vec],
        out_specs=[pl.BlockSpec((tm, cw), lambda i: (i, 0)), pl.BlockSpec((tm, cw), lambda i: (i, 0)), vec, vec],
        out_shape=[jax.ShapeDtypeStruct((s, cw), F32), jax.ShapeDtypeStruct((s, cw), BF16),
                   jax.ShapeDtypeStruct((1, cw), F32), jax.ShapeDtypeStruct((1, cw), F32)],
        scratch_shapes=[pltpu.VMEM((8, cw), F32), pltpu.VMEM((8, cw), F32)],
        compiler_params=_params("arbitrary"),
    )(dy, p, dc, cnorm_g, cnorm_b)


def odd_bwd_conv(dy, p, ddc, dg2, sconv_w, dconv_w, name, tm=128):
    s = p.shape[0]
    cw = ddc.shape[1]
    n = s // tm
    lanes = 128
    hb = _halo_before(tm)
    ha = _halo_after(tm, s)

    def body(dy_ref, dya_ref, g1_ref, g1a_ref, bc_ref, bca_ref, hc_ref, hch_ref, cc_ref, cch_ref,
             ddc_ref, ddca_ref, ga_ref, gah_ref, gb_ref, gbh_ref, dg2_ref, sw_ref, dw_ref,
             dp_ref, dsw_ref, ddw_ref, ddb_ref, sw_acc, dw_acc, db_acc):
        i = pl.program_id(0)
        first = i == 0
        last = i == n - 1

        @pl.when(first)
        def _():
            sw_acc[...] = jnp.zeros_like(sw_acc)
            dw_acc[...] = jnp.zeros_like(dw_acc)
            db_acc[...] = jnp.zeros_like(db_acc)

        for l in range(cw // lanes):
            cols = slice(l * lanes, (l + 1) * lanes)
            mh = jnp.where(first, 0.0, cch_ref[:, cols].astype(F32) * hch_ref[:, cols].astype(F32))
            hcv = hc_ref[:, cols].astype(F32)
            ccv = cc_ref[:, cols].astype(F32)
            xx = jnp.concatenate([mh, ccv * hcv], axis=0)
            taps = [_tap_before(xx, SCONV_K - 1 - k, tm) for k in range(SCONV_K)]
            cv = jnp.zeros((tm, lanes), F32)
            for k in range(SCONV_K):
                cv = cv + sw_ref[k:k + 1, cols] * taps[k]
            bcv = bc_ref[:, cols].astype(F32)
            dyv = dy_ref[:, cols].astype(F32)
            sg, dsg = _silu_and_grad(g1_ref[:, cols].astype(F32))
            dco = dyv * sg
            dp_ref[:, 5 * cw + l * lanes:5 * cw + (l + 1) * lanes] = (dyv * bcv * cv * dsg).astype(BF16)
            dp_ref[:, cw + l * lanes:cw + (l + 1) * lanes] = (dco * cv).astype(BF16)
            dcv = dco * bcv
            for k in range(SCONV_K):
                sw_acc[k * 8:(k + 1) * 8, cols] += _rowsum8(dcv * taps[k])
            dcv_a = jnp.where(last, 0.0, dya_ref[:, cols].astype(F32) * _silu(g1a_ref[:, cols].astype(F32))
                              * bca_ref[:, cols].astype(F32))
            xx = jnp.concatenate([dcv, dcv_a], axis=0)
            dm = jnp.zeros((tm, lanes), F32)
            for k in range(SCONV_K):
                dm = dm + sw_ref[k:k + 1, cols] * _tap_after(xx, SCONV_K - 1 - k, tm)
            dp_ref[:, l * lanes:(l + 1) * lanes] = (dm * ccv).astype(BF16)
            dp_ref[:, 2 * cw + l * lanes:2 * cw + (l + 1) * lanes] = (dm * hcv).astype(BF16)
            gav = ga_ref[:, cols].astype(F32)
            sb = _sigmoid(gb_ref[:, cols].astype(F32))
            dh = jnp.where(first, 0.0, gah_ref[:, cols].astype(F32) * _sigmoid(gbh_ref[:, cols].astype(F32)))
            xx = jnp.concatenate([dh, gav * sb], axis=0)
            ddcv = ddc_ref[:, cols]
            db_acc[:, cols] += _rowsum8(ddcv)
            for k in range(CONF_K):
                dw_acc[k * 8:(k + 1) * 8, cols] += _rowsum8(ddcv * _tap_before(xx, CONF_K - 1 - k, tm))
            ddc_a = jnp.where(last, 0.0, ddca_ref[:, cols])
            xx = jnp.concatenate([ddcv, ddc_a], axis=0)
            dgl = jnp.zeros((tm, lanes), F32)
            for k in range(CONF_K):
                dgl = dgl + dw_ref[k:k + 1, cols] * _tap_after(xx, CONF_K - 1 - k, tm)
            dp_ref[:, 3 * cw + l * lanes:3 * cw + (l + 1) * lanes] = (dgl * sb).astype(BF16)
            dp_ref[:, 4 * cw + l * lanes:4 * cw + (l + 1) * lanes] = (dgl * gav * sb * (1.0 - sb)).astype(BF16)
        dp_ref[:, 6 * cw:7 * cw] = dg2_ref[...]

        @pl.when(last)
        def _():
            for k in range(SCONV_K):
                dsw_ref[k:k + 1, :] = jnp.sum(sw_acc[k * 8:(k + 1) * 8, :], axis=0, keepdims=True)
            for k in range(CONF_K):
                ddw_ref[k:k + 1, :] = jnp.sum(dw_acc[k * 8:(k + 1) * 8, :], axis=0, keepdims=True)
            ddb_ref[...] = jnp.sum(db_acc[...], axis=0, keepdims=True)

    def main(c):
        return pl.BlockSpec((tm, cw), lambda i: (i, c))

    def before(c):
        return pl.BlockSpec((HALO, cw), lambda i: (hb(i), c))

    def after(c):
        return pl.BlockSpec((HALO, cw), lambda i: (ha(i), c))

    def vec(r):
        return pl.BlockSpec((r, cw), lambda i: (0, 0))

    return pl.pallas_call(
        body, name=name, grid=(n,),
        in_specs=[main(0), after(0), main(5), after(5), main(1), after(1), main(0), before(0), main(2), before(2),
                  main(0), after(0), main(3), before(3), main(4), before(4), main(0), vec(SCONV_K), vec(CONF_K)],
        out_specs=[pl.BlockSpec((tm, 7 * cw), lambda i: (i, 0)), vec(SCONV_K), vec(CONF_K), vec(1)],
        out_shape=[jax.ShapeDtypeStruct((s, 7 * cw), BF16), jax.ShapeDtypeStruct((SCONV_K, cw), F32),
                   jax.ShapeDtypeStruct((CONF_K, cw), F32), jax.ShapeDtypeStruct((1, cw), F32)],
        scratch_shapes=[pltpu.VMEM((8 * SCONV_K, cw), F32), pltpu.VMEM((8 * CONF_K, cw), F32),
                        pltpu.VMEM((8, cw), F32)],
        compiler_params=_params("arbitrary"),
    )(dy, dy, p, p, p, p, p, p, p, p, ddc, ddc, p, p, p, p, dg2, sconv_w, dconv_w)


_ANY = pl.BlockSpec(memory_space=pl.ANY)


def _place():
    return lax.axis_index("x"), lax.axis_index("y"), lax.axis_index("c")


def all_gather(arrs, name, deps=()):
    n = len(arrs)

    def body(*refs):
        ins, outs = refs[:n], refs[n + len(deps):2 * n + len(deps)]
        send_sems, recv_sems, local_sems = refs[-3:]
        x, y, c = _place()
        me, sibling = (x, y, c), (x, y, 1 - c)
        chips = [(1 - x, y), (x, 1 - y), (1 - x, 1 - y)]

        def copy(a, k, block, to, src=None):
            px, py, pc = block
            dst = outs[a].at[4 * px + 2 * py + pc]
            return pltpu.make_async_remote_copy(
                src_ref=dst if src is None else src, dst_ref=dst,
                send_sem=send_sems.at[7 * a + k], recv_sem=recv_sems.at[7 * a + k],
                device_id=to, device_id_type=MESH)

        mine = [pltpu.make_async_copy(ins[a], outs[a].at[4 * x + 2 * y + c], local_sems.at[a]) for a in range(n)]
        for cp in mine:
            cp.start()
        first = []
        for a in range(n):
            first.append(copy(a, 0, me, sibling, src=ins[a]))
            first += [copy(a, 1 + j, me, (*chip, c), src=ins[a]) for j, chip in enumerate(chips)]
        for cp in first:
            cp.start()
        passed = []
        for a in range(n):
            for j, chip in enumerate(chips):
                copy(a, 1 + j, (*chip, c), me).wait_recv()
                cp = copy(a, 4 + j, (*chip, c), sibling)
                cp.start()
                passed.append(cp)
        for a in range(n):
            copy(a, 0, sibling, me).wait_recv()
            for j, chip in enumerate(chips):
                copy(a, 4 + j, (*chip, 1 - c), me).wait_recv()
        for cp in first + passed:
            cp.wait_send()
        for cp in mine:
            cp.wait()

    return pl.pallas_call(
        body, name=name,
        out_shape=[jax.ShapeDtypeStruct((N_DEV,) + a.shape, a.dtype) for a in arrs],
        in_specs=[_ANY] * (n + len(deps)), out_specs=[_ANY] * n,
        scratch_shapes=[pltpu.SemaphoreType.DMA((7 * n,)), pltpu.SemaphoreType.DMA((7 * n,)),
                        pltpu.SemaphoreType.DMA((n,))],
    )(*arrs, *deps)


_HBM = pl.BlockSpec(memory_space=pltpu.HBM)
_SEM = pl.BlockSpec(memory_space=pltpu.SEMAPHORE)
_DATAFLOW = pltpu.SideEffectType.DATAFLOW_SIDE_EFFECTING


def _peers_per_array(kind):
    return 1 if kind == "sibling" else 3


def _split_copies(kind, srcs, lands, send_sems, recv_sems):
    x, y, c = _place()
    per = _peers_per_array(kind)
    out = []
    for a in range(len(srcs)):
        if kind == "sibling":
            peers = [((x, y, 1 - c), srcs[a].at[:, pl.ds(1 - c, 1)], lands[a], lands[a])]
        else:
            peers = []
            for px, py in [(1 - x, y), (x, 1 - y), (1 - x, 1 - y)]:
                if kind == "gather":
                    views = (srcs[a], lands[a].at[4 * x + 2 * y + c], lands[a].at[4 * px + 2 * py + c])
                else:
                    views = (srcs[a].at[2 * px + py], lands[a].at[2 * x + y], lands[a].at[2 * px + py])
                peers.append(((px, py, c),) + views)
        for j, (peer, src, dst, arrives) in enumerate(peers):
            sems = dict(send_sem=send_sems.at[per * a + j], recv_sem=recv_sems.at[per * a + j],
                        device_id=peer, device_id_type=MESH)
            out.append((pltpu.make_async_remote_copy(src_ref=src, dst_ref=dst, **sems),
                        pltpu.make_async_remote_copy(src_ref=src, dst_ref=arrives, **sems)))
    return out


def split_start(kind, srcs, lands, deps, name):
    n = len(srcs)
    n_sems = _peers_per_array(kind) * n

    def body(*refs):
        send_sems, recv_sems = refs[2 * n + len(deps)], refs[2 * n + len(deps) + 1]
        for copy, _ in _split_copies(kind, refs[:n], refs[n:2 * n], send_sems, recv_sems):
            copy.start()
        token = refs[-1]
        token[...] = jnp.zeros_like(token)

    held = [pltpu.HBM(a.shape, a.dtype) for a in list(srcs) + list(lands)]
    outs = pl.pallas_call(
        body, name=name,
        out_shape=(pltpu.SemaphoreType.DMA((n_sems,)), pltpu.SemaphoreType.DMA((n_sems,)), *held,
                   jax.ShapeDtypeStruct((8, 128), F32)),
        in_specs=[_HBM] * (2 * n) + [_ANY] * len(deps),
        out_specs=(_SEM, _SEM, *([_HBM] * (2 * n)), pl.BlockSpec(memory_space=pltpu.VMEM)),
        input_output_aliases={i: 2 + i for i in range(2 * n)},
        compiler_params=pltpu.CompilerParams(has_side_effects=_DATAFLOW),
    )(*[pltpu.with_memory_space_constraint(a, pltpu.HBM) for a in list(srcs) + list(lands)], *deps)
    return outs[0], outs[1], list(outs[2:2 + n]), list(outs[2 + n:2 + 2 * n]), outs[-1]


def split_wait(kind, send_sems, recv_sems, srcs, lands, afters, name):
    n = len(srcs)

    def body(*refs):
        for _, arrival in _split_copies(kind, refs[:n], refs[n:2 * n], refs[2 * n], refs[2 * n + 1]):
            arrival.wait_send()
            arrival.wait_recv()

    outs = pl.pallas_call(
        body, name=name,
        out_shape=[pltpu.HBM(a.shape, a.dtype) for a in list(srcs) + list(lands)],
        in_specs=[_HBM] * (2 * n) + [_SEM, _SEM] + [_ANY] * len(afters),
        out_specs=[_HBM] * (2 * n),
        input_output_aliases={i: i for i in range(2 * n)},
        compiler_params=pltpu.CompilerParams(has_side_effects=_DATAFLOW),
    )(*srcs, *lands, send_sems, recv_sems, *afters)
    return list(outs[:n]), list(outs[n:])


def place_block(land, block, dev, name):
    r, c = block.shape
    tr = min(r, 512)

    def body(dev_ref, land_ref, b_ref, o_ref):
        del dev_ref, land_ref
        o_ref[...] = b_ref[...]

    return pl.pallas_call(
        body, name=name,
        grid_spec=pltpu.PrefetchScalarGridSpec(
            num_scalar_prefetch=1, grid=(r // tr,),
            in_specs=[_ANY, pl.BlockSpec((tr, c), lambda i, dev_ref: (i, 0))],
            out_specs=pl.BlockSpec((None, tr, c), lambda i, dev_ref: (dev_ref[0], i, 0))),
        out_shape=jax.ShapeDtypeStruct(land.shape, land.dtype),
        input_output_aliases={1: 0},
        compiler_params=_params("parallel"),
    )(dev, land, block)


def gather_finish(lands, name):
    n = len(lands)

    def body(*refs):
        outs = refs[n:2 * n]
        send_sems, recv_sems = refs[2 * n:]
        x, y, c = _place()
        cps = [pltpu.make_async_remote_copy(
            src_ref=outs[a].at[:, pl.ds(c, 1)], dst_ref=outs[a].at[:, pl.ds(c, 1)],
            send_sem=send_sems.at[a], recv_sem=recv_sems.at[a],
            device_id=(x, y, 1 - c), device_id_type=MESH) for a in range(n)]
        for cp in cps:
            cp.start()
        for cp in cps:
            cp.wait()

    return pl.pallas_call(
        body, name=name,
        out_shape=[jax.ShapeDtypeStruct(a.shape, a.dtype) for a in lands],
        in_specs=[_ANY] * n, out_specs=[_ANY] * n,
        input_output_aliases={i: i for i in range(n)},
        scratch_shapes=[pltpu.SemaphoreType.DMA((n,)), pltpu.SemaphoreType.DMA((n,))],
    )(*lands)


def pair_add(own, recv, core, name):
    _, _, r, c = own.shape
    tr = min(r, 512)

    def body(core_ref, own_ref, recv_ref, o_ref):
        del core_ref
        o_ref[...] = (own_ref[...].astype(F32) + recv_ref[...].astype(F32)).astype(BF16)

    return pl.pallas_call(
        body, name=name,
        grid_spec=pltpu.PrefetchScalarGridSpec(
            num_scalar_prefetch=1, grid=(4, r // tr),
            in_specs=[pl.BlockSpec((None, None, tr, c), lambda k, i, core_ref: (k, core_ref[0], i, 0)),
                      pl.BlockSpec((None, None, tr, c), lambda k, i, core_ref: (k, 0, i, 0))],
            out_specs=pl.BlockSpec((None, tr, c), lambda k, i, core_ref: (k, i, 0))),
        out_shape=jax.ShapeDtypeStruct((4, r, c), BF16),
        compiler_params=_params("parallel", "parallel"),
    )(core, own, recv)


def _adamw_math(w, g, m, v):
    m2 = ADAM_B1 * m + (1.0 - ADAM_B1) * g
    v2 = ADAM_B2 * v + (1.0 - ADAM_B2) * (g * g)
    m_hat = m2 / (1.0 - ADAM_B1 ** ADAM_STEP)
    v_hat = v2 / (1.0 - ADAM_B2 ** ADAM_STEP)
    delta = -ADAM_LR * (m_hat / (jnp.sqrt(v_hat) + ADAM_EPS) + ADAM_WD * w)
    return delta, m2, v2


def adamw_big(w, m, v, own, got, chip, name):
    r, c = w.shape
    tr = min(r, 256)

    def body(chip_ref, w_ref, m_ref, v_ref, p0, p1, p2, p3, g_ref, d_ref, m2_ref, v2_ref):
        del chip_ref
        g = ((p0[...].astype(F32) + p1[...].astype(F32)) + p2[...].astype(F32)) + p3[...].astype(F32)
        delta, m2, v2 = _adamw_math(w_ref[...], g, m_ref[...], v_ref[...])
        g_ref[...] = g
        d_ref[...] = delta
        m2_ref[...] = m2
        v2_ref[...] = v2

    row = pl.BlockSpec((tr, c), lambda i, chip_ref: (i, 0))

    def slab(flip):
        return pl.BlockSpec((None, tr, c), lambda i, chip_ref: (chip_ref[0] ^ flip, i, 0))

    return pl.pallas_call(
        body, name=name,
        grid_spec=pltpu.PrefetchScalarGridSpec(
            num_scalar_prefetch=1, grid=(r // tr,),
            in_specs=[row, row, row, slab(0), slab(1), slab(2), slab(3)],
            out_specs=[row] * 4),
        out_shape=[jax.ShapeDtypeStruct((r, c), F32)] * 4,
        compiler_params=_params("parallel"),
    )(chip, w, m, v, own, got, got, got)


def sum_devices(g8, name):
    def body(g_ref, o_ref):
        tot = g_ref[0]
        for k in range(1, N_DEV):
            tot = tot + g_ref[k]
        o_ref[...] = tot

    return pl.pallas_call(body, name=name, out_shape=jax.ShapeDtypeStruct(g8.shape[1:], F32))(g8)


def adamw_small(ws, gs, ms, vs, name):
    n = len(ws)

    def body(*refs):
        w_r, g_r, m_r, v_r = refs[:n], refs[n:2 * n], refs[2 * n:3 * n], refs[3 * n:4 * n]
        d_o, m_o, v_o = refs[4 * n:5 * n], refs[5 * n:6 * n], refs[6 * n:7 * n]
        for k in range(n):
            delta, m2, v2 = _adamw_math(w_r[k][...], g_r[k][...], m_r[k][...], v_r[k][...])
            d_o[k][...] = delta
            m_o[k][...] = m2
            v_o[k][...] = v2

    shapes = [jax.ShapeDtypeStruct(w.shape, F32) for w in ws]
    outs = pl.pallas_call(body, name=name, out_shape=shapes * 3)(*ws, *gs, *ms, *vs)
    return outs[:n], outs[n:2 * n], outs[2 * n:]


def _rows128(a):
    return a.reshape(-1, 128)


def _pad_rows(a, rows):
    return jnp.pad(a, ((0, rows - a.shape[0]), (0, 0)))


def kernel(x, ln_pre_even, w_in_even, pool_w, pool_scale, w_out_even, ln_post_even, ln_pre_odd, w_in_odd, sconv_w, dconv_w, dconv_b, cnorm_g, cnorm_b, w_out_odd, ln_post_odd, loss_target, m_ln_pre_even, m_w_in_even, m_pool_w, m_pool_scale, m_w_out_even, m_ln_post_even, m_ln_pre_odd, m_w_in_odd, m_sconv_w, m_dconv_w, m_dconv_b, m_cnorm_g, m_cnorm_b, m_w_out_odd, m_ln_post_odd, v_ln_pre_even, v_w_in_even, v_pool_w, v_pool_scale, v_w_out_even, v_ln_post_even, v_ln_pre_odd, v_w_in_odd, v_sconv_w, v_dconv_w, v_dconv_b, v_cnorm_g, v_cnorm_b, v_w_out_odd, v_ln_post_odd):
    xs = x[0]
    tgt = loss_target[0]
    s, d = xs.shape
    half = d // 2
    n_heads = half // HEAD_DIM
    ng = len(POOL_WINDOWS)
    cwp = half // ng
    dev = 4 * lax.axis_index("x") + 2 * lax.axis_index("y") + lax.axis_index("c")
    core = lax.axis_index("c").astype(jnp.int32).reshape(1)

    pr = pool_w.shape[2]
    cl = sconv_w.shape[2]
    small_parts = [(_rows128(ln_pre_odd), 8), (sconv_w[0], 8), (dconv_w[0], 32), (dconv_b, 8),
                   (cnorm_g, 8), (cnorm_b, 8), (_rows128(ln_post_odd), 8)]
    small_local = jnp.concatenate([_pad_rows(a, r) for a, r in small_parts], axis=0)
    g_wie, g_pw, g_small = all_gather(
        [w_in_even[0].astype(BF16), pool_w[0].reshape(ng * pr, cwp).astype(BF16), small_local], "ag_first")
    comm = _Exchanges(dev, core, d)
    in_proj_dep = comm.start_weights([w_out_even[0].astype(BF16), w_in_odd[0].astype(BF16),
                                      w_out_odd[0].astype(BF16)], after=g_wie)
    pool_full = g_pw.reshape(N_DEV, ng, pr, cwp).transpose(1, 0, 2, 3).reshape(ng, cwp, cwp)
    nl = ln_pre_odd.shape[1] // 128

    def chan(lo, rows):
        return g_small[:, lo:lo + rows].transpose(1, 0, 2).reshape(rows, N_DEV * cl)

    ln_pre_odd_f = g_small[:, 0:nl].reshape(1, d)
    sconv_f = chan(8, SCONV_K)
    dconv_f = chan(16, CONF_K)
    dconv_b_f = chan(48, 1)
    cnorm_g_f = chan(56, 1)
    cnorm_b_f = chan(64, 1)
    ln_post_odd_f = g_small[:, 72:72 + nl].reshape(1, d)

    loss_blk, grad_x, small_g = _fwd_bwd(
        xs, tgt, ln_pre_even, g_wie, pool_full, pool_scale, ln_post_even, ln_pre_odd_f,
        sconv_f, dconv_f, dconv_b_f, cnorm_g_f, cnorm_b_f, ln_post_odd_f, comm, in_proj_dep)
    small_w = [ln_pre_even, pool_scale, ln_post_even, ln_pre_odd, sconv_w[0], dconv_w[0], dconv_b, cnorm_g, cnorm_b, ln_post_odd]
    small_m = [m_ln_pre_even, m_pool_scale, m_ln_post_even, m_ln_pre_odd, m_sconv_w[0], m_dconv_w[0], m_dconv_b, m_cnorm_g, m_cnorm_b, m_ln_post_odd]
    small_v = [v_ln_pre_even, v_pool_scale, v_ln_post_even, v_ln_pre_odd, v_sconv_w[0], v_dconv_w[0], v_dconv_b, v_cnorm_g, v_cnorm_b, v_ln_post_odd]
    big = {"w_in_even": (w_in_even, m_w_in_even, v_w_in_even), "pool_w": (pool_w, m_pool_w, v_pool_w),
           "w_out_even": (w_out_even, m_w_out_even, v_w_out_even), "w_in_odd": (w_in_odd, m_w_in_odd, v_w_in_odd),
           "w_out_odd": (w_out_odd, m_w_out_odd, v_w_out_odd)}
    upd = comm.finish_updates(big, [grad_x])
    upd.update(comm.finish_updates(big, [grad_x]))
    sg, sd, sm, sv, loss = _update_small(small_g, loss_blk, small_w, small_m, small_v, dev, d, cl,
                                         deps=[upd["w_in_odd"][1], upd["w_out_even"][1]])
    upd.update(comm.finish_updates(big, sd))
    (g_wie_o, d_wie, m_wie, v_wie), (g_pw_o, d_pw, m_pw, v_pw) = upd["w_in_even"], upd["pool_w"]
    (g_woe_o, d_woe, m_woe, v_woe), (g_wio_o, d_wio, m_wio, v_wio) = upd["w_out_even"], upd["w_in_odd"]
    g_woo_o, d_woo, m_woo, v_woo = upd["w_out_odd"]

    def order(small, wie, pw, woe, wio, woo):
        return [small[0], wie, pw, small[1], woe, small[2], small[3], wio, small[4], small[5], small[6],
                small[7], small[8], woo, small[9]]

    grads = order(sg, g_wie_o, g_pw_o, g_woe_o, g_wio_o, g_woo_o)
    deltas = order(sd, d_wie, d_pw, d_woe, d_wio, d_woo)
    new_m = order(sm, m_wie, m_pw, m_woe, m_wio, m_woo)
    new_v = order(sv, v_wie, v_pw, v_woe, v_wio, v_woo)
    return (loss, grad_x[None], *grads, *deltas, *new_m, *new_v)


def _fwd_bwd(xs, tgt, ln_pre_even, g_wie, pool_full, pool_scale, ln_post_even, ln_pre_odd_f,
             sconv_f, dconv_f, dconv_b_f, cnorm_g_f, cnorm_b_f, ln_post_odd_f, comm, in_proj_dep):
    d = xs.shape[1]
    n_heads = d // 2 // HEAD_DIM
    ng, cwp = pool_full.shape[0], pool_full.shape[1]
    h0 = rms_fwd(xs, ln_pre_even, "rms_pre_even")
    p0 = mm_nn(h0, g_wie, BF16, "in_proj_even", dep=in_proj_dep)
    a0, sb_wts = sb_fwd(p0, n_heads, "sb_fwd")
    y0 = even_mix_fwd(a0, p0, pool_full, pool_scale, "even_mix_fwd")
    w_out_e, g_wio, w_out_o = comm.rest_of_weights(after=y0)
    o0 = mm_nn(y0, w_out_e, F32, "out_proj_even", tm=1024, tn=1024)
    x1 = postnorm_fwd(xs, o0, ln_post_even, "post_even")
    h1 = rms_fwd(x1, ln_pre_odd_f, "rms_pre_odd")
    p1 = mm_nn(h1, g_wio, BF16, "in_proj_odd")
    y1, dc = odd_mix_fwd(p1, sconv_f, dconv_f, dconv_b_f, cnorm_g_f, cnorm_b_f, "odd_mix_fwd")
    o1 = mm_nn(y1, w_out_o, F32, "out_proj_odd", tm=1024, tn=1024)
    loss_blk, gx2, do1, dg_post_odd = final_fwd_bwd(x1, o1, ln_post_odd_f, tgt, "post_odd_loss")

    dw_out_o = mm_tn(y1, do1, 1, BF16, "dw_out_odd", tm=1024)
    dy1 = mm_nt(do1, w_out_o, BF16, "dy_odd", tn=1024)
    ddc, dg2, dgam, dbet = odd_bwd_ln(dy1, p1, dc, cnorm_g_f, cnorm_b_f, "odd_bwd_ln")
    dp1, dsconv, ddconv, ddconv_b = odd_bwd_conv(dy1, p1, ddc, dg2, sconv_f, dconv_f, "odd_bwd_conv")
    dw_in_o = mm_tn(h1, dp1, N_DEV, BF16, "dw_in_odd")
    dep = comm.reduce_begin({"w_out_odd": dw_out_o.reshape(N_DEV, d // N_DEV, d), "w_in_odd": dw_in_o}, "odd")
    dh1 = mm_nt(dp1, g_wio, F32, "dh_odd", dep=dep)
    dep = comm.reduce_send(after=dh1)
    gx1, dg_pre_odd = norm_bwd(dh1, x1, ln_pre_odd_f, gx2, F32, "pre_odd_bwd", dep=dep)

    do0, dg_post_even = norm_bwd(gx1, o0, ln_post_even, None, BF16, "post_even_bwd")
    dw_out_e = mm_tn(y0, do0, 1, BF16, "dw_out_even", tm=1024)
    dy0 = mm_nt(do0, w_out_e, BF16, "dy_even", tn=1024)
    da0, du0, dg0, dpool, dpool_scale = even_mix_bwd(dy0, a0, p0, pool_full, pool_scale, "even_mix_bwd")
    pr = cwp // N_DEV
    dpool_slabs = dpool.astype(BF16).reshape(ng, N_DEV, pr, cwp).transpose(1, 0, 2, 3).reshape(N_DEV, ng * pr, cwp)
    dep = comm.reduce_begin({"w_out_even": dw_out_e.reshape(N_DEV, d // N_DEV, d), "pool_w": dpool_slabs}, "even_out")
    dq0, dk0, dv0 = sb_bwd(p0, a0, sb_wts, da0, n_heads, "sb_bwd", dep=dep)
    dep = comm.reduce_send(after=dq0)
    dp0 = jnp.concatenate([dq0, dk0, dv0, du0, dg0], axis=1)
    dw_in_e = mm_tn(h0, dp0, N_DEV, BF16, "dw_in_even", dep=dep)
    dep = comm.reduce_begin({"w_in_even": dw_in_e}, "even_in")
    dh0 = mm_nt(dp0, g_wie, F32, "dh_even", dep=dep)
    dep = comm.reduce_send(after=dh0)
    grad_x, dg_pre_even = norm_bwd(dh0, xs, ln_pre_even, gx1, F32, "pre_even_bwd", dep=dep)
    small_g = [dg_pre_even, dpool_scale, dg_post_even, dg_pre_odd, dsconv, ddconv, ddconv_b, dgam, dbet, dg_post_odd]
    return loss_blk, grad_x, small_g


class _Exchanges:
    def __init__(self, dev, core, d):
        self.dev = dev.astype(jnp.int32).reshape(1)
        self.core = core
        self.chip = (dev // 2).astype(jnp.int32).reshape(1)
        self.d = d
        self.weights = None
        self.to_sibling = None
        self.pending = []

    def start_weights(self, blocks, after):
        lands = [lax.empty((N_DEV,) + b.shape, b.dtype) for b in blocks]
        send, recv, srcs, lands, token = split_start("gather", blocks, lands, [after], "ag_rest_start")
        self.weights = (send, recv, srcs, lands)
        return token

    def rest_of_weights(self, after):
        send, recv, srcs, lands = self.weights
        srcs, lands = split_wait("gather", send, recv, srcs, lands, [after], "ag_rest_wait")
        lands = [place_block(l, b, self.dev, "ag_rest_own_%d" % k) for k, (l, b) in enumerate(zip(lands, srcs))]
        full = gather_finish([l.reshape((4, 2) + l.shape[1:]) for l in lands], "ag_rest_finish")
        w_out_e, g_wio, w_out_o = [f.reshape((N_DEV,) + f.shape[2:]) for f in full]
        return w_out_e.reshape(1, self.d, self.d), g_wio, w_out_o.reshape(1, self.d, self.d)

    def reduce_begin(self, partials, tag):
        names = list(partials)
        arrs = [partials[k].reshape((4, 2) + partials[k].shape[1:]) for k in names]
        lands = [lax.empty((4, 1) + a.shape[2:], a.dtype) for a in arrs]
        send, recv, srcs, lands, token = split_start("sibling", arrs, lands, [], "rs_sibling_start_" + tag)
        self.to_sibling = (tag, names, send, recv, srcs, lands)
        return token

    def reduce_send(self, after):
        tag, names, send, recv, srcs, lands = self.to_sibling
        srcs, lands = split_wait("sibling", send, recv, srcs, lands, [after], "rs_sibling_wait_" + tag)
        sums = [pair_add(o, r, self.core, "rs_pair_add_" + k) for k, o, r in zip(names, srcs, lands)]
        zones = [lax.empty(a.shape, a.dtype) for a in sums]
        send, recv, srcs, zones, token = split_start("scatter", sums, zones, [], "rs_start_" + tag)
        self.pending.append((tag, names, send, recv, srcs, zones))
        return token

    def finish_updates(self, big, afters):
        tag, names, send, recv, srcs, lands = self.pending.pop(0)
        srcs, lands = split_wait("scatter", send, recv, srcs, lands, afters, "rs_wait_" + tag)
        out = {}
        for name, own, got in zip(names, srcs, lands):
            w, m, v = big[name]
            shp = own.shape[1:]
            outs = adamw_big(w.reshape(shp), m.reshape(shp), v.reshape(shp), own, got, self.chip, "adamw_" + name)
            out[name] = [o.reshape(w.shape) for o in outs]
        return out


def _update_small(small_g, loss_blk, small_w, small_m, small_v, dev, d, cl, deps):
    packed = jnp.concatenate([_rows128(g) for g in small_g] + [loss_blk], axis=0)
    (g8,) = all_gather([packed], "ag_small_grads", deps)
    tot = sum_devices(g8, "sum_small_grads")
    loss = tot[packed.shape[0] - 8, 0]
    full_g = []
    lo = 0
    for g in small_g:
        rows = g.size // 128
        full_g.append(tot[lo:lo + rows].reshape(g.shape))
        lo += rows

    def mine(g, width):
        return lax.dynamic_slice_in_dim(g, dev * width, width, axis=g.ndim - 1)

    fg = full_g
    small_gl = [fg[0], fg[1], fg[2], mine(fg[3], d // N_DEV), mine(fg[4], cl), mine(fg[5], cl), mine(fg[6], cl),
                mine(fg[7], cl), mine(fg[8], cl), mine(fg[9], d // N_DEV)]
    sd, sm, sv = adamw_small(small_w, small_gl, small_m, small_v, "adamw_small")

    def like(k, a):
        return a[None] if k in (4, 5) else a

    sg = [like(k, a) for k, a in enumerate(small_gl)]
    sd = [like(k, a) for k, a in enumerate(sd)]
    sm = [like(k, a) for k, a in enumerate(sm)]
    sv = [like(k, a) for k, a in enumerate(sv)]
    return sg, sd, sm, sv, loss
```

```python
import functools
import math

import jax
import jax.numpy as jnp
from jax import lax
from jax.experimental import pallas as pl
from jax.experimental.pallas import tpu as pltpu

F32 = jnp.float32
BF16 = jnp.bfloat16
EPS = 1e-6
HEAD_DIM = 128
POOL_WINDOWS = (2, 4, 8, 16)
SCONV_K = 3
CONF_K = 31
HALO = 32
N_DEV = 8
VMEM_LIMIT = 56 * 1024 * 1024
MESH = pl.DeviceIdType.MESH

ADAM_LR = 0.001
ADAM_B1 = 0.9
ADAM_B2 = 0.999
ADAM_EPS = 1e-08
ADAM_WD = 0.01
ADAM_STEP = 10


def _params(*sem):
    return pltpu.CompilerParams(dimension_semantics=sem, vmem_limit_bytes=VMEM_LIMIT)


def _sigmoid(v):
    return 1.0 / (1.0 + jnp.exp(-v))


def _silu(v):
    return v * _sigmoid(v)


def _silu_and_grad(v):
    s = _sigmoid(v)
    return v * s, s * (1.0 + v * (1.0 - s))


def _rowsum8(v):
    r, c = v.shape
    return jnp.sum(v.reshape(r // 8, 8, c), axis=0)


def _tap_before(xx, i, rows):
    if i == 0:
        return xx[HALO:HALO + rows]
    return pltpu.roll(xx, i, 0)[HALO:HALO + rows]


def _tap_after(xx, i, rows):
    if i == 0:
        return xx[0:rows]
    return pltpu.roll(xx, xx.shape[0] - i, 0)[0:rows]


def rms_fwd(x, g, name, tm=256):
    s, d = x.shape

    def body(x_ref, g_ref, h_ref):
        xv = x_ref[...]
        r = lax.rsqrt(jnp.mean(xv * xv, axis=-1, keepdims=True) + EPS)
        h_ref[...] = (xv * r * g_ref[...]).astype(BF16)

    return pl.pallas_call(
        body, name=name, grid=(s // tm,),
        in_specs=[pl.BlockSpec((tm, d), lambda i: (i, 0)), pl.BlockSpec((1, d), lambda i: (0, 0))],
        out_specs=pl.BlockSpec((tm, d), lambda i: (i, 0)),
        out_shape=jax.ShapeDtypeStruct((s, d), BF16),
        compiler_params=_params("parallel"),
    )(x, g)


def postnorm_fwd(x, o, g, name, tm=256):
    s, d = x.shape

    def body(x_ref, o_ref, g_ref, y_ref):
        ov = o_ref[...]
        r = lax.rsqrt(jnp.mean(ov * ov, axis=-1, keepdims=True) + EPS)
        y_ref[...] = x_ref[...] + ov * r * g_ref[...]

    return pl.pallas_call(
        body, name=name, grid=(s // tm,),
        in_specs=[pl.BlockSpec((tm, d), lambda i: (i, 0)), pl.BlockSpec((tm, d), lambda i: (i, 0)),
                  pl.BlockSpec((1, d), lambda i: (0, 0))],
        out_specs=pl.BlockSpec((tm, d), lambda i: (i, 0)),
        out_shape=jax.ShapeDtypeStruct((s, d), F32),
        compiler_params=_params("parallel"),
    )(x, o, g)


def final_fwd_bwd(x1, o, g, target, name, tm=256):
    s, d = x1.shape
    n = s // tm

    def body(x_ref, o_ref, g_ref, t_ref, loss_ref, gx_ref, do_ref, dg_ref, lacc, gacc):
        i = pl.program_id(0)

        @pl.when(i == 0)
        def _():
            lacc[...] = jnp.zeros_like(lacc)
            gacc[...] = jnp.zeros_like(gacc)

        ov = o_ref[...]
        gv = g_ref[...]
        r = lax.rsqrt(jnp.mean(ov * ov, axis=-1, keepdims=True) + EPS)
        oh = ov * r
        diff = x_ref[...] + oh * gv - t_ref[...]
        lacc[...] += _rowsum8(diff * diff)
        gx = diff * (1.0 / d)
        gx_ref[...] = gx
        gacc[...] += _rowsum8(gx * oh)
        dn = gx * gv
        do_ref[...] = (r * (dn - oh * jnp.mean(dn * oh, axis=-1, keepdims=True))).astype(BF16)

        @pl.when(i == n - 1)
        def _():
            tot = jnp.sum(jnp.sum(lacc[...], axis=0, keepdims=True), axis=1, keepdims=True)
            loss_ref[...] = jnp.broadcast_to(tot * (0.5 / d), loss_ref.shape)
            dg_ref[...] = jnp.sum(gacc[...], axis=0, keepdims=True)

    row = pl.BlockSpec((tm, d), lambda i: (i, 0))
    vec = pl.BlockSpec((1, d), lambda i: (0, 0))
    return pl.pallas_call(
        body, name=name, grid=(n,),
        in_specs=[row, row, vec, row],
        out_specs=[pl.BlockSpec((8, 128), lambda i: (0, 0)), row, row, vec],
        out_shape=[jax.ShapeDtypeStruct((8, 128), F32), jax.ShapeDtypeStruct((s, d), F32),
                   jax.ShapeDtypeStruct((s, d), BF16), jax.ShapeDtypeStruct((1, d), F32)],
        scratch_shapes=[pltpu.VMEM((8, d), F32), pltpu.VMEM((8, d), F32)],
        compiler_params=_params("arbitrary"),
    )(x1, o, g, target)


def norm_bwd(dy, inp, g, resid, out_dtype, name, tm=256, dep=None):
    s, d = inp.shape
    n = s // tm
    has_resid = resid is not None

    def body(*refs):
        dy_ref, x_ref, g_ref = refs[:3]
        r_ref = refs[3] if has_resid else None
        dx_ref, dg_ref, gacc = refs[-3:]
        i = pl.program_id(0)

        @pl.when(i == 0)
        def _():
            gacc[...] = jnp.zeros_like(gacc)

        xv = x_ref[...]
        dyv = dy_ref[...].astype(F32)
        r = lax.rsqrt(jnp.mean(xv * xv, axis=-1, keepdims=True) + EPS)
        xh = xv * r
        gacc[...] += _rowsum8(dyv * xh)
        dn = dyv * g_ref[...]
        dx = r * (dn - xh * jnp.mean(dn * xh, axis=-1, keepdims=True))
        if has_resid:
            dx = dx + r_ref[...]
        dx_ref[...] = dx.astype(out_dtype)

        @pl.when(i == n - 1)
        def _():
            dg_ref[...] = jnp.sum(gacc[...], axis=0, keepdims=True)

    row = pl.BlockSpec((tm, d), lambda i: (i, 0))
    vec = pl.BlockSpec((1, d), lambda i: (0, 0))
    dep_args, dep_specs = _after(dep)
    args = [dy, inp, g] + ([resid] if has_resid else []) + dep_args
    return pl.pallas_call(
        body, name=name, grid=(n,),
        in_specs=[row, row, vec] + ([row] if has_resid else []) + dep_specs,
        out_specs=[row, vec],
        out_shape=[jax.ShapeDtypeStruct((s, d), out_dtype), jax.ShapeDtypeStruct((1, d), F32)],
        scratch_shapes=[pltpu.VMEM((8, d), F32)],
        compiler_params=_params("arbitrary"),
    )(*args)


def _after(dep):
    if dep is None:
        return [], []
    return [dep], [pl.BlockSpec((8, 128), lambda *_: (0, 0))]


def mm_nn(a, w, out_dtype, name, tm=2048, tn=None, dep=None):
    m, k = a.shape
    tm = min(tm, m)
    ns, _, n = w.shape
    tn = n if tn is None else tn
    nj = n // tn
    dep_args, dep_specs = _after(dep)

    def body(a_ref, w_ref, *rest):
        o_ref = rest[-1]
        o_ref[...] = jnp.dot(a_ref[...], w_ref[0], preferred_element_type=F32).astype(out_dtype)

    return pl.pallas_call(
        body, name=name, grid=(ns, nj, m // tm),
        in_specs=[pl.BlockSpec((tm, k), lambda s, j, i: (i, 0)),
                  pl.BlockSpec((1, k, tn), lambda s, j, i: (s, 0, j))] + dep_specs,
        out_specs=pl.BlockSpec((tm, tn), lambda s, j, i: (i, s * nj + j)),
        out_shape=jax.ShapeDtypeStruct((m, ns * n), out_dtype),
        compiler_params=_params("parallel", "parallel", "parallel"),
    )(a, w, *dep_args)


def mm_nt(a, w, out_dtype, name, tm=1024, tn=None, dep=None):
    m = a.shape[0]
    tm = min(tm, m)
    ns, k, n = w.shape
    tn = n if tn is None else tn
    nj = n // tn
    steps = ns * nj
    dep_args, dep_specs = _after(dep)

    def body(a_ref, w_ref, *rest):
        o_ref, acc = rest[-2:]
        r = pl.program_id(1)

        @pl.when(r == 0)
        def _():
            acc[...] = jnp.zeros_like(acc)

        acc[...] += lax.dot_general(a_ref[...], w_ref[0], (((1,), (1,)), ((), ())),
                                    preferred_element_type=F32)

        @pl.when(r == steps - 1)
        def _():
            o_ref[...] = acc[...].astype(out_dtype)

    return pl.pallas_call(
        body, name=name, grid=(m // tm, steps),
        in_specs=[pl.BlockSpec((tm, tn), lambda i, r: (i, r)),
                  pl.BlockSpec((1, k, tn), lambda i, r: (r // nj, 0, r % nj))] + dep_specs,
        out_specs=pl.BlockSpec((tm, k), lambda i, r: (i, 0)),
        out_shape=jax.ShapeDtypeStruct((m, k), out_dtype),
        scratch_shapes=[pltpu.VMEM((tm, k), F32)],
        compiler_params=_params("parallel", "arbitrary"),
    )(a, w, *dep_args)


def mm_tn(a, b, ns, out_dtype, name, tk=1024, tm=2048, dep=None):
    m, k = a.shape
    tm = min(tm, m)
    n = b.shape[1] // ns
    steps = m // tm
    dep_args, dep_specs = _after(dep)

    def body(a_ref, b_ref, *rest):
        o_ref, acc = rest[-2:]
        r = pl.program_id(2)

        @pl.when(r == 0)
        def _():
            acc[...] = jnp.zeros_like(acc)

        acc[...] += lax.dot_general(a_ref[...], b_ref[...], (((0,), (0,)), ((), ())),
                                    preferred_element_type=F32)

        @pl.when(r == steps - 1)
        def _():
            o_ref[0] = acc[...].astype(out_dtype)

    return pl.pallas_call(
        body, name=name, grid=(ns, k // tk, steps),
        in_specs=[pl.BlockSpec((tm, tk), lambda s, j, r: (r, j)),
                  pl.BlockSpec((tm, n), lambda s, j, r: (r, s))] + dep_specs,
        out_specs=pl.BlockSpec((1, tk, n), lambda s, j, r: (s, j, 0)),
        out_shape=jax.ShapeDtypeStruct((ns, k, n), out_dtype),
        scratch_shapes=[pltpu.VMEM((tk, n), F32)],
        compiler_params=_params("parallel", "parallel", "arbitrary"),
    )(a, b, *dep_args)


SB_BLK = 128


LOG2E = 1.0 / math.log(2.0)


def _split_dot(v, tri):
    hi = pltpu.bitcast(pltpu.bitcast(v, jnp.uint32) & jnp.uint32(0xFFFF0000), F32)
    lo = (v - hi).astype(BF16)
    return (jnp.dot(hi.astype(BF16), tri, preferred_element_type=F32)
            + jnp.dot(lo, tri, preferred_element_type=F32))


def _sb_scores(z2, lim, dcol, tri_ex, masked):
    sp = jnp.log2(1.0 + jnp.exp2(-jnp.abs(z2)))
    lb = jnp.minimum(z2, 0.0) - sp
    l1m = lb - z2
    mask = None
    if masked:
        mask = dcol < lim
        l1m = jnp.where(mask, l1m, 0.0)
    return mask, lb, l1m, _split_dot(l1m, tri_ex)


def _sb_consts():
    row = lax.broadcasted_iota(jnp.int32, (SB_BLK, SB_BLK), 0)
    col = lax.broadcasted_iota(jnp.int32, (SB_BLK, SB_BLK), 1)
    tri_ex = jnp.where(row > col, 1.0, 0.0).astype(BF16)
    tri_in = jnp.where(row >= col, 1.0, 0.0).astype(BF16)
    return col - row, tri_ex, tri_in


def sb_fwd(p, n_heads, name, tq=256, nsub=4, dep=None):
    s = p.shape[0]
    h_n = n_heads
    b = SB_BLK
    nqs = tq // b
    tk = nsub * b
    scale = 1.0 / math.sqrt(HEAD_DIM)

    dep_args, dep_specs = _after(dep)

    def body(q_ref, k_ref, v_ref, *rest):
        o_ref, w_ref = rest[-2:]
        qi = pl.program_id(1)
        dcol, tri_ex, _ = _sb_consts()
        qv = [q_ref[qs * b:(qs + 1) * b, :] for qs in range(nqs)]
        n_groups = ((qi + 1) * nqs - 1) // nsub + 1

        def step(it, carry, masked):
            c1s, accs = carry
            g = n_groups - 1 - it
            off = pl.multiple_of(g * tk, tk)
            kg = k_ref[pl.ds(off, tk), :]
            vg = v_ref[pl.ds(off, tk), :]
            new_c1, new_acc = [], []
            for qs in range(nqs):
                qb = qi * nqs + qs
                z2 = lax.dot_general(qv[qs], kg, (((1,), (1,)), ((), ())),
                                     preferred_element_type=F32) * (scale * LOG2E)
                blocks = [_sb_scores(z2[:, j * b:(j + 1) * b], (qb - (g * nsub + j)) * b, dcol, tri_ex, masked)
                          for j in range(nsub)]
                run = c1s[qs]
                ws = [None] * nsub
                for j in reversed(range(nsub)):
                    mask, lb, l1m, ls_loc = blocks[j]
                    wj = jnp.exp2(lb + ls_loc + run)
                    ws[j] = (jnp.where(mask, wj, 0.0) if masked else wj).astype(BF16)
                    run = run + jnp.sum(l1m, axis=1, keepdims=True)
                w = jnp.concatenate(ws, axis=1)
                w_ref[0, g, qs * b:(qs + 1) * b, :] = w
                new_acc.append(accs[qs] + jnp.dot(w, vg, preferred_element_type=F32))
                new_c1.append(run)
            return tuple(new_c1), tuple(new_acc)

        init = (tuple(jnp.zeros((b, 1), F32) for _ in range(nqs)),
                tuple(jnp.zeros((b, HEAD_DIM), F32) for _ in range(nqs)))
        assert nqs == 2 and nsub % 2 == 0
        first = step(0, init, True)
        _, accs = lax.fori_loop(1, n_groups, functools.partial(step, masked=False), first)
        for qs in range(nqs):
            o_ref[qs * b:(qs + 1) * b, :] = accs[qs]

    return pl.pallas_call(
        body, name=name, grid=(h_n, s // tq),
        in_specs=[pl.BlockSpec((tq, HEAD_DIM), lambda h, i: (i, h)),
                  pl.BlockSpec((s, HEAD_DIM), lambda h, i: (0, h_n + h)),
                  pl.BlockSpec((s, HEAD_DIM), lambda h, i: (0, 2 * h_n + h))] + dep_specs,
        out_specs=[pl.BlockSpec((tq, HEAD_DIM), lambda h, i: (i, h)),
                   pl.BlockSpec((1, s // tk, tq, tk), lambda h, i: (h, 0, i, 0))],
        out_shape=[jax.ShapeDtypeStruct((s, h_n * HEAD_DIM), F32),
                   jax.ShapeDtypeStruct((h_n, s // tk, s, tk), BF16)],
        compiler_params=_params("parallel", "arbitrary"),
    )(p, p, p, *dep_args)


def sb_bwd(p, a, wts, da, n_heads, name, tq=256, dep=None):
    s = p.shape[0]
    h_n = n_heads
    nq = s // tq
    b = SB_BLK
    nqs = tq // b
    tk = wts.shape[3]
    nsub = tk // b
    scale = 1.0 / math.sqrt(HEAD_DIM)
    dep_args, dep_specs = _after(dep)

    def body(q_ref, k_ref, v_ref, a_ref, da_ref, w_ref, *rest):
        dq_ref, dk_ref, dv_ref, dk_acc, dv_acc = rest[-5:]
        qi = pl.program_id(1)

        @pl.when(qi == 0)
        def _():
            dk_acc[...] = jnp.zeros_like(dk_acc)
            dv_acc[...] = jnp.zeros_like(dv_acc)

        dcol, _, tri_in = _sb_consts()
        q_all = q_ref[...]
        do_all = da_ref[...]
        qv = [q_ref[qs * b:(qs + 1) * b, :] for qs in range(nqs)]
        dov = [da_ref[qs * b:(qs + 1) * b, :] for qs in range(nqs)]
        tots = [jnp.sum(dov[qs].astype(F32) * a_ref[qs * b:(qs + 1) * b, :], axis=1, keepdims=True)
                for qs in range(nqs)]
        n_groups = ((qi + 1) * nqs - 1) // nsub + 1

        def step(it, carry, masked):
            c2s, dqs = carry
            g = n_groups - 1 - it
            off = pl.multiple_of(g * tk, tk)
            kg = k_ref[pl.ds(off, tk), :]
            vg = v_ref[pl.ds(off, tk), :]
            w_all = w_ref[0, g]
            new_c2, new_dq, dz_rows = [], [], []
            for qs in range(nqs):
                qb = qi * nqs + qs
                z2 = lax.dot_general(qv[qs], kg, (((1,), (1,)), ((), ())),
                                     preferred_element_type=F32) * (-scale * LOG2E)
                dw = lax.dot_general(dov[qs], vg, (((1,), (1,)), ((), ())), preferred_element_type=F32)
                beta = 1.0 / (1.0 + jnp.exp2(z2))
                e = dw * w_all[qs * b:(qs + 1) * b, :].astype(F32)
                run2 = c2s[qs]
                dzs = [None] * nsub
                for j in reversed(range(nsub)):
                    cols = slice(j * b, (j + 1) * b)
                    later = _split_dot(e[:, cols], tri_in) + run2
                    bj = beta[:, cols]
                    dz = (e[:, cols] * (1.0 - bj) - bj * (tots[qs] - later)) * scale
                    if masked:
                        dz = jnp.where(dcol < (qb - (g * nsub + j)) * b, dz, 0.0)
                    dzs[j] = dz.astype(BF16)
                    run2 = run2 + jnp.sum(e[:, cols], axis=1, keepdims=True)
                dzq = jnp.concatenate(dzs, axis=1)
                new_dq.append(dqs[qs] + jnp.dot(dzq, kg, preferred_element_type=F32))
                new_c2.append(run2)
                dz_rows.append(dzq)
            dz_all = jnp.concatenate(dz_rows, axis=0)
            dk_acc[pl.ds(off, tk), :] += lax.dot_general(dz_all, q_all, (((0,), (0,)), ((), ())),
                                                         preferred_element_type=F32)
            dv_acc[pl.ds(off, tk), :] += lax.dot_general(w_all, do_all, (((0,), (0,)), ((), ())),
                                                         preferred_element_type=F32)
            return tuple(new_c2), tuple(new_dq)

        zeros = tuple(jnp.zeros((b, 1), F32) for _ in range(nqs))
        assert nqs == 2 and nsub % 2 == 0
        first = step(0, (zeros, tuple(jnp.zeros((b, HEAD_DIM), F32) for _ in range(nqs))), True)
        _, dqs = lax.fori_loop(1, n_groups, functools.partial(step, masked=False), first)
        for qs in range(nqs):
            dq_ref[qs * b:(qs + 1) * b, :] = dqs[qs].astype(BF16)

        @pl.when(qi == nq - 1)
        def _():
            dk_ref[...] = dk_acc[...].astype(BF16)
            dv_ref[...] = dv_acc[...].astype(BF16)

    blk = pl.BlockSpec((tq, HEAD_DIM), lambda h, i: (i, h))
    full = pl.BlockSpec((s, HEAD_DIM), lambda h, i: (0, h))
    return pl.pallas_call(
        body, name=name, grid=(h_n, nq),
        in_specs=[blk, pl.BlockSpec((s, HEAD_DIM), lambda h, i: (0, h_n + h)),
                  pl.BlockSpec((s, HEAD_DIM), lambda h, i: (0, 2 * h_n + h)), blk, blk,
                  pl.BlockSpec((1, s // tk, tq, tk), lambda h, i: (h, 0, i, 0))] + dep_specs,
        out_specs=[blk, full, full],
        out_shape=[jax.ShapeDtypeStruct((s, h_n * HEAD_DIM), BF16)] * 3,
        scratch_shapes=[pltpu.VMEM((s, HEAD_DIM), F32), pltpu.VMEM((s, HEAD_DIM), F32)],
        compiler_params=_params("parallel", "arbitrary"),
    )(p, p, p, a, da, wts, *dep_args)


def _pool_window(xx, win, r0, rc):
    cur = xx[HALO:HALO + rc]
    ws = cur
    for i in range(1, win):
        ws = ws + _tap_before(xx, i, rc)
    t_idx = r0 + lax.broadcasted_iota(jnp.int32, (rc, 1), 0)
    inv = 1.0 / jnp.minimum(win, t_idx + 1).astype(F32)
    return ws * inv - cur, inv


def even_mix_fwd(a, p, pool_w, pool_scale, name, rc=64):
    s = p.shape[0]
    ng = len(POOL_WINDOWS)
    cw = pool_w.shape[1]
    n_chunks = s // rc

    def body(a_ref, u_ref, g_ref, w_ref, sc_ref, y_ref, upad):
        j = pl.program_id(0)

        @pl.when(j < ng)
        def _():
            def chunk(ci, carry):
                rows = pl.ds(pl.multiple_of(ci * rc, rc), rc)
                y_ref[rows, :] = (a_ref[rows, :] * _silu(g_ref[rows, :].astype(F32))).astype(BF16)
                return carry

            lax.fori_loop(0, n_chunks, chunk, 0)

        for gi, win in enumerate(POOL_WINDOWS):
            @pl.when(j == ng + gi)
            def _(win=win):
                upad[0:HALO, :] = jnp.zeros((HALO, cw), F32)

                def fill(ci, carry):
                    r0 = pl.multiple_of(ci * rc, rc)
                    upad[pl.ds(pl.multiple_of(r0 + HALO, HALO), rc), :] = u_ref[pl.ds(r0, rc), :].astype(F32)
                    return carry

                lax.fori_loop(0, n_chunks, fill, 0)

                def chunk(ci, carry):
                    r0 = pl.multiple_of(ci * rc, rc)
                    rows = pl.ds(r0, rc)
                    pooled, _ = _pool_window(upad[pl.ds(r0, HALO + rc), :], win, r0, rc)
                    t = jnp.dot(pooled.astype(BF16), w_ref[0], preferred_element_type=F32)
                    y_ref[rows, :] = (t * sc_ref[...] * _silu(g_ref[rows, :].astype(F32))).astype(BF16)
                    return carry

                lax.fori_loop(0, n_chunks, chunk, 0)

    grp = lambda j: jnp.maximum(j - ng, 0)
    return pl.pallas_call(
        body, name=name, grid=(2 * ng,),
        in_specs=[pl.BlockSpec((s, cw), lambda j: (0, jnp.minimum(j, ng - 1))),
                  pl.BlockSpec((s, cw), lambda j: (0, 3 * ng + grp(j))),
                  pl.BlockSpec((s, cw), lambda j: (0, 4 * ng + j)),
                  pl.BlockSpec((1, cw, cw), lambda j: (grp(j), 0, 0)),
                  pl.BlockSpec((1, cw), lambda j: (0, grp(j)))],
        out_specs=pl.BlockSpec((s, cw), lambda j: (0, j)),
        out_shape=jax.ShapeDtypeStruct((s, 2 * ng * cw), BF16),
        scratch_shapes=[pltpu.VMEM((HALO + s, cw), F32)],
        compiler_params=_params("arbitrary"),
    )(a, p, p, pool_w, pool_scale)


def even_mix_bwd(dy, a, p, pool_w, pool_scale, name, rc=64):
    s = p.shape[0]
    ng = len(POOL_WINDOWS)
    cw = pool_w.shape[1]
    n_chunks = s // rc

    def body(dy_ref, a_ref, u_ref, g_ref, w_ref, sc_ref, da_ref, du_ref, dg_ref, dw_ref, dsc_ref,
             upad, rpad, dpl, dw_acc, dsc_acc):
        j = pl.program_id(0)

        @pl.when(j < ng)
        def _():
            def chunk(ci, carry):
                rows = pl.ds(pl.multiple_of(ci * rc, rc), rc)
                dyv = dy_ref[rows, :].astype(F32)
                sg, dsg = _silu_and_grad(g_ref[rows, :].astype(F32))
                da_ref[rows, :] = (dyv * sg).astype(BF16)
                dg_ref[rows, :] = (dyv * a_ref[rows, :] * dsg).astype(BF16)
                return carry

            lax.fori_loop(0, n_chunks, chunk, 0)

        for gi, win in enumerate(POOL_WINDOWS):
            @pl.when(j == ng + gi)
            def _(win=win):
                upad[0:HALO, :] = jnp.zeros((HALO, cw), F32)
                rpad[s:s + HALO, :] = jnp.zeros((HALO, cw), F32)
                dw_acc[...] = jnp.zeros_like(dw_acc)
                dsc_acc[...] = jnp.zeros_like(dsc_acc)

                def fill(ci, carry):
                    r0 = pl.multiple_of(ci * rc, rc)
                    upad[pl.ds(pl.multiple_of(r0 + HALO, HALO), rc), :] = u_ref[pl.ds(r0, rc), :].astype(F32)
                    return carry

                lax.fori_loop(0, n_chunks, fill, 0)

                def chunk(ci, carry):
                    r0 = pl.multiple_of(ci * rc, rc)
                    rows = pl.ds(r0, rc)
                    pooled, inv = _pool_window(upad[pl.ds(r0, HALO + rc), :], win, r0, rc)
                    pb = pooled.astype(BF16)
                    wv = w_ref[0]
                    t = jnp.dot(pb, wv, preferred_element_type=F32)
                    scv = sc_ref[...]
                    dyv = dy_ref[rows, :].astype(F32)
                    sg, dsg = _silu_and_grad(g_ref[rows, :].astype(F32))
                    dpo = dyv * sg
                    dg_ref[rows, :] = (dyv * t * scv * dsg).astype(BF16)
                    dsc_acc[...] += _rowsum8(dpo * t)
                    dtb = (dpo * scv).astype(BF16)
                    dw_acc[...] += lax.dot_general(pb, dtb, (((0,), (0,)), ((), ())),
                                                   preferred_element_type=F32)
                    dpooled = lax.dot_general(dtb, wv, (((1,), (1,)), ((), ())),
                                              preferred_element_type=F32)
                    dpl[rows, :] = dpooled
                    rpad[rows, :] = dpooled * inv
                    return carry

                lax.fori_loop(0, n_chunks, chunk, 0)

                def chunk2(ci, carry):
                    r0 = pl.multiple_of(ci * rc, rc)
                    rows = pl.ds(r0, rc)
                    xx = rpad[pl.ds(r0, rc + HALO), :]
                    fs = xx[0:rc]
                    for i in range(1, win):
                        fs = fs + _tap_after(xx, i, rc)
                    du_ref[rows, :] = (fs - dpl[rows, :]).astype(BF16)
                    return carry

                lax.fori_loop(0, n_chunks, chunk2, 0)
                dw_ref[0] = dw_acc[...]
                dsc_ref[...] = jnp.sum(dsc_acc[...], axis=0, keepdims=True)

    grp = lambda j: jnp.maximum(j - ng, 0)
    att = lambda j: jnp.minimum(j, ng - 1)
    return pl.pallas_call(
        body, name=name, grid=(2 * ng,),
        in_specs=[pl.BlockSpec((s, cw), lambda j: (0, j)),
                  pl.BlockSpec((s, cw), lambda j: (0, att(j))),
                  pl.BlockSpec((s, cw), lambda j: (0, 3 * ng + grp(j))),
                  pl.BlockSpec((s, cw), lambda j: (0, 4 * ng + j)),
                  pl.BlockSpec((1, cw, cw), lambda j: (grp(j), 0, 0)),
                  pl.BlockSpec((1, cw), lambda j: (0, grp(j)))],
        out_specs=[pl.BlockSpec((s, cw), lambda j: (0, att(j))),
                   pl.BlockSpec((s, cw), lambda j: (0, grp(j))),
                   pl.BlockSpec((s, cw), lambda j: (0, j)),
                   pl.BlockSpec((1, cw, cw), lambda j: (grp(j), 0, 0)),
                   pl.BlockSpec((1, cw), lambda j: (0, grp(j)))],
        out_shape=[jax.ShapeDtypeStruct((s, ng * cw), BF16), jax.ShapeDtypeStruct((s, ng * cw), BF16),
                   jax.ShapeDtypeStruct((s, 2 * ng * cw), BF16),
                   jax.ShapeDtypeStruct((ng, cw, cw), F32), jax.ShapeDtypeStruct((1, ng * cw), F32)],
        scratch_shapes=[pltpu.VMEM((HALO + s, cw), F32), pltpu.VMEM((s + HALO, cw), F32),
                        pltpu.VMEM((s, cw), F32), pltpu.VMEM((cw, cw), F32), pltpu.VMEM((8, cw), F32)],
        compiler_params=_params("arbitrary"),
    )(dy, a, p, p, pool_w, pool_scale)


def _halo_before(tm):
    return lambda i: jnp.maximum(i * (tm // HALO) - 1, 0)


def _halo_after(tm, s):
    return lambda i: jnp.minimum((i + 1) * (tm // HALO), s // HALO - 1)


def odd_mix_fwd(p, sconv_w, dconv_w, dconv_b, cnorm_g, cnorm_b, name, tm=128):
    s = p.shape[0]
    cw = sconv_w.shape[1]
    n = s // tm
    lanes = 128
    hb = _halo_before(tm)

    def body(hc_ref, hch_ref, bc_ref, cc_ref, cch_ref, ga_ref, gah_ref, gb_ref, gbh_ref, g1_ref, g2_ref,
             sw_ref, dw_ref, db_ref, gam_ref, bet_ref, y_ref, dc_ref):
        first = pl.program_id(0) == 0
        for l in range(cw // lanes):
            cols = slice(l * lanes, (l + 1) * lanes)
            mh = jnp.where(first, 0.0, cch_ref[:, cols].astype(F32) * hch_ref[:, cols].astype(F32))
            mm = cc_ref[:, cols].astype(F32) * hc_ref[:, cols].astype(F32)
            xx = jnp.concatenate([mh, mm], axis=0)
            cv = jnp.zeros((tm, lanes), F32)
            for k in range(SCONV_K):
                cv = cv + sw_ref[k:k + 1, cols] * _tap_before(xx, SCONV_K - 1 - k, tm)
            c_out = bc_ref[:, cols].astype(F32) * cv
            y_ref[:, cols] = (c_out * _silu(g1_ref[:, cols].astype(F32))).astype(BF16)
            dh = jnp.where(first, 0.0, gah_ref[:, cols].astype(F32) * _sigmoid(gbh_ref[:, cols].astype(F32)))
            dm = ga_ref[:, cols].astype(F32) * _sigmoid(gb_ref[:, cols].astype(F32))
            xx = jnp.concatenate([dh, dm], axis=0)
            acc = jnp.zeros((tm, lanes), F32) + db_ref[:, cols]
            for k in range(CONF_K):
                acc = acc + dw_ref[k:k + 1, cols] * _tap_before(xx, CONF_K - 1 - k, tm)
            dc_ref[:, cols] = acc
        rs = 32
        for r in range(tm // rs):
            rows = slice(r * rs, (r + 1) * rs)
            xv = dc_ref[rows, :]
            mu = jnp.mean(xv, axis=-1, keepdims=True)
            xc = xv - mu
            rstd = lax.rsqrt(jnp.mean(xc * xc, axis=-1, keepdims=True) + EPS)
            ln = xc * rstd * gam_ref[...] + bet_ref[...]
            y_ref[rows, cw:2 * cw] = (_silu(ln) * _silu(g2_ref[rows, :].astype(F32))).astype(BF16)

    main = lambda c: pl.BlockSpec((tm, cw), lambda i: (i, c))
    halo = lambda c: pl.BlockSpec((HALO, cw), lambda i: (hb(i), c))
    vec = lambda r: pl.BlockSpec((r, cw), lambda i: (0, 0))
    return pl.pallas_call(
        body, name=name, grid=(n,),
        in_specs=[main(0), halo(0), main(1), main(2), halo(2), main(3), halo(3), main(4), halo(4),
                  main(5), main(6), vec(SCONV_K), vec(CONF_K), vec(1), vec(1), vec(1)],
        out_specs=[pl.BlockSpec((tm, 2 * cw), lambda i: (i, 0)), pl.BlockSpec((tm, cw), lambda i: (i, 0))],
        out_shape=[jax.ShapeDtypeStruct((s, 2 * cw), BF16), jax.ShapeDtypeStruct((s, cw), F32)],
        compiler_params=_params("parallel"),
    )(p, p, p, p, p, p, p, p, p, p, p, sconv_w, dconv_w, dconv_b, cnorm_g, cnorm_b)


def odd_bwd_ln(dy, p, dc, cnorm_g, cnorm_b, name, tm=256):
    s = p.shape[0]
    cw = dc.shape[1]
    n = s // tm
    rs = 32

    def body(dy_ref, g2_ref, dc_ref, gam_ref, bet_ref, ddc_ref, dg_ref, dgam_ref, dbet_ref, gacc, bacc):
        i = pl.program_id(0)

        @pl.when(i == 0)
        def _():
            gacc[...] = jnp.zeros_like(gacc)
            bacc[...] = jnp.zeros_like(bacc)

        def chunk(ci, carry):
            rows = pl.ds(pl.multiple_of(ci * rs, rs), rs)
            xv = dc_ref[rows, :]
            mu = jnp.mean(xv, axis=-1, keepdims=True)
            xc = xv - mu
            rstd = lax.rsqrt(jnp.mean(xc * xc, axis=-1, keepdims=True) + EPS)
            xh = xc * rstd
            gam = gam_ref[...]
            sl, dsl = _silu_and_grad(xh * gam + bet_ref[...])
            sg, dsg = _silu_and_grad(g2_ref[rows, :].astype(F32))
            dyv = dy_ref[rows, :].astype(F32)
            dg_ref[rows, :] = (dyv * sl * dsg).astype(BF16)
            dln = dyv * sg * dsl
            gacc[...] += _rowsum8(dln * xh)
            bacc[...] += _rowsum8(dln)
            dxh = dln * gam
            ddc_ref[rows, :] = rstd * (dxh - jnp.mean(dxh, axis=-1, keepdims=True)
                                       - xh * jnp.mean(dxh * xh, axis=-1, keepdims=True))
            return carry

        lax.fori_loop(0, tm // rs, chunk, 0)

        @pl.when(i == n - 1)
        def _():
            dgam_ref[...] = jnp.sum(gacc[...], axis=0, keepdims=True)
            dbet_ref[...] = jnp.sum(bacc[...], axis=0, keepdims=True)

    vec = pl.BlockSpec((1, cw), lambda i: (0, 0))
    return pl.pallas_call(
        body, name=name, grid=(n,),
        in_specs=[pl.BlockSpec((tm, cw), lambda i: (i, 1)), pl.BlockSpec((tm, cw), lambda i: (i, 6)),
                  pl.BlockSpec((tm, cw), lambda i: (i, 0)), vec, vec],
        out_specs=[pl.BlockSpec((tm, cw), lambda i: (i, 0)), pl.BlockSpec((tm, cw), lambda i: (i, 0)), vec, vec],
        out_shape=[jax.ShapeDtypeStruct((s, cw), F32), jax.ShapeDtypeStruct((s, cw), BF16),
                   jax.ShapeDtypeStruct((1, cw), F32), jax.ShapeDtypeStruct((1, cw), F32)],
        scratch_shapes=[pltpu.VMEM((8, cw), F32), pltpu.VMEM((8, cw), F32)],
        compiler_params=_params("arbitrary"),
    )(dy, p, dc, cnorm_g, cnorm_b)


def odd_bwd_conv(dy, p, ddc, dg2, sconv_w, dconv_w, name, tm=128):
    s = p.shape[0]
    cw = ddc.shape[1]
    n = s // tm
    lanes = 128
    hb = _halo_before(tm)
    ha = _halo_after(tm, s)

    def body(dy_ref, dya_ref, g1_ref, g1a_ref, bc_ref, bca_ref, hc_ref, hch_ref, cc_ref, cch_ref,
             ddc_ref, ddca_ref, ga_ref, gah_ref, gb_ref, gbh_ref, dg2_ref, sw_ref, dw_ref,
             dp_ref, dsw_ref, ddw_ref, ddb_ref, sw_acc, dw_acc, db_acc):
        i = pl.program_id(0)
        first = i == 0
        last = i == n - 1

        @pl.when(first)
        def _():
            sw_acc[...] = jnp.zeros_like(sw_acc)
            dw_acc[...] = jnp.zeros_like(dw_acc)
            db_acc[...] = jnp.zeros_like(db_acc)

        for l in range(cw // lanes):
            cols = slice(l * lanes, (l + 1) * lanes)
            mh = jnp.where(first, 0.0, cch_ref[:, cols].astype(F32) * hch_ref[:, cols].astype(F32))
            hcv = hc_ref[:, cols].astype(F32)
            ccv = cc_ref[:, cols].astype(F32)
            xx = jnp.concatenate([mh, ccv * hcv], axis=0)
            taps = [_tap_before(xx, SCONV_K - 1 - k, tm) for k in range(SCONV_K)]
            cv = jnp.zeros((tm, lanes), F32)
            for k in range(SCONV_K):
                cv = cv + sw_ref[k:k + 1, cols] * taps[k]
            bcv = bc_ref[:, cols].astype(F32)
            dyv = dy_ref[:, cols].astype(F32)
            sg, dsg = _silu_and_grad(g1_ref[:, cols].astype(F32))
            dco = dyv * sg
            dp_ref[:, 5 * cw + l * lanes:5 * cw + (l + 1) * lanes] = (dyv * bcv * cv * dsg).astype(BF16)
            dp_ref[:, cw + l * lanes:cw + (l + 1) * lanes] = (dco * cv).astype(BF16)
            dcv = dco * bcv
            for k in range(SCONV_K):
                sw_acc[k * 8:(k + 1) * 8, cols] += _rowsum8(dcv * taps[k])
            dcv_a = jnp.where(last, 0.0, dya_ref[:, cols].astype(F32) * _silu(g1a_ref[:, cols].astype(F32))
                              * bca_ref[:, cols].astype(F32))
            xx = jnp.concatenate([dcv, dcv_a], axis=0)
            dm = jnp.zeros((tm, lanes), F32)
            for k in range(SCONV_K):
                dm = dm + sw_ref[k:k + 1, cols] * _tap_after(xx, SCONV_K - 1 - k, tm)
            dp_ref[:, l * lanes:(l + 1) * lanes] = (dm * ccv).astype(BF16)
            dp_ref[:, 2 * cw + l * lanes:2 * cw + (l + 1) * lanes] = (dm * hcv).astype(BF16)
            gav = ga_ref[:, cols].astype(F32)
            sb = _sigmoid(gb_ref[:, cols].astype(F32))
            dh = jnp.where(first, 0.0, gah_ref[:, cols].astype(F32) * _sigmoid(gbh_ref[:, cols].astype(F32)))
            xx = jnp.concatenate([dh, gav * sb], axis=0)
            ddcv = ddc_ref[:, cols]
            db_acc[:, cols] += _rowsum8(ddcv)
            for k in range(CONF_K):
                dw_acc[k * 8:(k + 1) * 8, cols] += _rowsum8(ddcv * _tap_before(xx, CONF_K - 1 - k, tm))
            ddc_a = jnp.where(last, 0.0, ddca_ref[:, cols])
            xx = jnp.concatenate([ddcv, ddc_a], axis=0)
            dgl = jnp.zeros((tm, lanes), F32)
            for k in range(CONF_K):
                dgl = dgl + dw_ref[k:k + 1, cols] * _tap_after(xx, CONF_K - 1 - k, tm)
            dp_ref[:, 3 * cw + l * lanes:3 * cw + (l + 1) * lanes] = (dgl * sb).astype(BF16)
            dp_ref[:, 4 * cw + l * lanes:4 * cw + (l + 1) * lanes] = (dgl * gav * sb * (1.0 - sb)).astype(BF16)
        dp_ref[:, 6 * cw:7 * cw] = dg2_ref[...]

        @pl.when(last)
        def _():
            for k in range(SCONV_K):
                dsw_ref[k:k + 1, :] = jnp.sum(sw_acc[k * 8:(k + 1) * 8, :], axis=0, keepdims=True)
            for k in range(CONF_K):
                ddw_ref[k:k + 1, :] = jnp.sum(dw_acc[k * 8:(k + 1) * 8, :], axis=0, keepdims=True)
            ddb_ref[...] = jnp.sum(db_acc[...], axis=0, keepdims=True)

    def main(c):
        return pl.BlockSpec((tm, cw), lambda i: (i, c))

    def before(c):
        return pl.BlockSpec((HALO, cw), lambda i: (hb(i), c))

    def after(c):
        return pl.BlockSpec((HALO, cw), lambda i: (ha(i), c))

    def vec(r):
        return pl.BlockSpec((r, cw), lambda i: (0, 0))

    return pl.pallas_call(
        body, name=name, grid=(n,),
        in_specs=[main(0), after(0), main(5), after(5), main(1), after(1), main(0), before(0), main(2), before(2),
                  main(0), after(0), main(3), before(3), main(4), before(4), main(0), vec(SCONV_K), vec(CONF_K)],
        out_specs=[pl.BlockSpec((tm, 7 * cw), lambda i: (i, 0)), vec(SCONV_K), vec(CONF_K), vec(1)],
        out_shape=[jax.ShapeDtypeStruct((s, 7 * cw), BF16), jax.ShapeDtypeStruct((SCONV_K, cw), F32),
                   jax.ShapeDtypeStruct((CONF_K, cw), F32), jax.ShapeDtypeStruct((1, cw), F32)],
        scratch_shapes=[pltpu.VMEM((8 * SCONV_K, cw), F32), pltpu.VMEM((8 * CONF_K, cw), F32),
                        pltpu.VMEM((8, cw), F32)],
        compiler_params=_params("arbitrary"),
    )(dy, dy, p, p, p, p, p, p, p, p, ddc, ddc, p, p, p, p, dg2, sconv_w, dconv_w)


_ANY = pl.BlockSpec(memory_space=pl.ANY)


def _place():
    return lax.axis_index("x"), lax.axis_index("y"), lax.axis_index("c")


def all_gather(arrs, name, deps=()):
    n = len(arrs)

    def body(*refs):
        ins, outs = refs[:n], refs[n + len(deps):2 * n + len(deps)]
        send_sems, recv_sems, local_sems = refs[-3:]
        x, y, c = _place()
        me, sibling = (x, y, c), (x, y, 1 - c)
        chips = [(1 - x, y), (x, 1 - y), (1 - x, 1 - y)]

        def copy(a, k, block, to, src=None):
            px, py, pc = block
            dst = outs[a].at[4 * px + 2 * py + pc]
            return pltpu.make_async_remote_copy(
                src_ref=dst if src is None else src, dst_ref=dst,
                send_sem=send_sems.at[7 * a + k], recv_sem=recv_sems.at[7 * a + k],
                device_id=to, device_id_type=MESH)

        mine = [pltpu.make_async_copy(ins[a], outs[a].at[4 * x + 2 * y + c], local_sems.at[a]) for a in range(n)]
        for cp in mine:
            cp.start()
        first = []
        for a in range(n):
            first.append(copy(a, 0, me, sibling, src=ins[a]))
            first += [copy(a, 1 + j, me, (*chip, c), src=ins[a]) for j, chip in enumerate(chips)]
        for cp in first:
            cp.start()
        passed = []
        for a in range(n):
            for j, chip in enumerate(chips):
                copy(a, 1 + j, (*chip, c), me).wait_recv()
                cp = copy(a, 4 + j, (*chip, c), sibling)
                cp.start()
                passed.append(cp)
        for a in range(n):
            copy(a, 0, sibling, me).wait_recv()
            for j, chip in enumerate(chips):
                copy(a, 4 + j, (*chip, 1 - c), me).wait_recv()
        for cp in first + passed:
            cp.wait_send()
        for cp in mine:
            cp.wait()

    return pl.pallas_call(
        body, name=name,
        out_shape=[jax.ShapeDtypeStruct((N_DEV,) + a.shape, a.dtype) for a in arrs],
        in_specs=[_ANY] * (n + len(deps)), out_specs=[_ANY] * n,
        scratch_shapes=[pltpu.SemaphoreType.DMA((7 * n,)), pltpu.SemaphoreType.DMA((7 * n,)),
                        pltpu.SemaphoreType.DMA((n,))],
    )(*arrs, *deps)


def in_proj_gathered(h, w_own, extras, name, tm=512):
    s, d = h.shape
    n = w_own.shape[1]
    arrs = [w_own] + list(extras)
    na = len(arrs)

    def body(*refs):
        h_ref, ins = refs[0], refs[1:1 + na]
        p_ref, outs = refs[1 + na], refs[2 + na:2 + 2 * na]
        wbuf, obuf, send_sems, recv_sems, load_sem, store_sems, own_sems = refs[2 + 2 * na:]
        x, y, c = _place()
        me, sibling = (x, y, c), (x, y, 1 - c)
        chips = [(1 - x, y), (x, 1 - y), (1 - x, 1 - y)]

        def slot(block):
            return 4 * block[0] + 2 * block[1] + block[2]

        def copy(a, k, block, to, src=None):
            dst = outs[a].at[slot(block)]
            return pltpu.make_async_remote_copy(
                src_ref=dst if src is None else src, dst_ref=dst,
                send_sem=send_sems.at[7 * a + k], recv_sem=recv_sems.at[7 * a + k],
                device_id=to, device_id_type=MESH)

        first = []
        for a in range(na):
            first.append(copy(a, 0, me, sibling, src=ins[a]))
            first += [copy(a, 1 + j, me, (*chip, c), src=ins[a]) for j, chip in enumerate(chips)]
        mine = [pltpu.make_async_copy(ins[a], outs[a].at[slot(me)], own_sems.at[a]) for a in range(na)]
        for cp in first + mine:
            cp.start()

        stores = [None, None]

        def multiply(k, block, w_from):
            b = k % 2
            load = pltpu.make_async_copy(w_from, wbuf.at[b], load_sem)
            load.start()
            if stores[b] is not None:
                stores[b].wait()
            load.wait()

            def chunk(i, carry):
                rows = pl.ds(pl.multiple_of(i * tm, tm), tm)
                obuf[b, rows, :] = jnp.dot(h_ref[rows, :], wbuf[b], preferred_element_type=F32).astype(BF16)
                return carry

            lax.fori_loop(0, s // tm, chunk, 0)
            stores[b] = pltpu.make_async_copy(
                obuf.at[b], p_ref.at[:, pl.ds(pl.multiple_of(slot(block) * n, 128), n)], store_sems.at[b])
            stores[b].start()

        multiply(0, me, ins[0])
        for a in range(na):
            copy(a, 0, sibling, me).wait_recv()
        multiply(1, sibling, outs[0].at[slot(sibling)])
        passed = []
        for j, chip in enumerate(chips):
            for a in range(na):
                copy(a, 1 + j, (*chip, c), me).wait_recv()
                cp = copy(a, 4 + j, (*chip, c), sibling)
                cp.start()
                passed.append(cp)
            multiply(2 + j, (*chip, c), outs[0].at[slot((*chip, c))])
        for j, chip in enumerate(chips):
            for a in range(na):
                copy(a, 4 + j, (*chip, 1 - c), me).wait_recv()
            multiply(5 + j, (*chip, 1 - c), outs[0].at[slot((*chip, 1 - c))])
        for cp in first + passed:
            cp.wait_send()
        for cp in mine + stores:
            cp.wait()

    vmem = pl.BlockSpec(memory_space=pltpu.VMEM)
    outs = pl.pallas_call(
        body, name=name,
        out_shape=[jax.ShapeDtypeStruct((s, N_DEV * n), BF16)]
        + [jax.ShapeDtypeStruct((N_DEV,) + a.shape, a.dtype) for a in arrs],
        in_specs=[vmem] + [_ANY] * na, out_specs=[_ANY] * (1 + na),
        scratch_shapes=[pltpu.VMEM((2, d, n), BF16), pltpu.VMEM((2, s, n), BF16),
                        pltpu.SemaphoreType.DMA((7 * na,)), pltpu.SemaphoreType.DMA((7 * na,)),
                        pltpu.SemaphoreType.DMA, pltpu.SemaphoreType.DMA((2,)), pltpu.SemaphoreType.DMA((na,))],
        compiler_params=pltpu.CompilerParams(vmem_limit_bytes=VMEM_LIMIT),
    )(h, *arrs)
    return outs[0], outs[1], outs[2:]


_HBM = pl.BlockSpec(memory_space=pltpu.HBM)
_SEM = pl.BlockSpec(memory_space=pltpu.SEMAPHORE)
_DATAFLOW = pltpu.SideEffectType.DATAFLOW_SIDE_EFFECTING


def _peers_per_array(kind):
    return 1 if kind == "sibling" else 3


def _split_copies(kind, srcs, lands, send_sems, recv_sems):
    x, y, c = _place()
    per = _peers_per_array(kind)
    out = []
    for a in range(len(srcs)):
        if kind == "sibling":
            peers = [((x, y, 1 - c), srcs[a].at[:, pl.ds(1 - c, 1)], lands[a], lands[a])]
        else:
            peers = []
            for px, py in [(1 - x, y), (x, 1 - y), (1 - x, 1 - y)]:
                if kind == "gather":
                    views = (srcs[a], lands[a].at[4 * x + 2 * y + c], lands[a].at[4 * px + 2 * py + c])
                else:
                    views = (srcs[a].at[2 * px + py], lands[a].at[2 * x + y], lands[a].at[2 * px + py])
                peers.append(((px, py, c),) + views)
        for j, (peer, src, dst, arrives) in enumerate(peers):
            sems = dict(send_sem=send_sems.at[per * a + j], recv_sem=recv_sems.at[per * a + j],
                        device_id=peer, device_id_type=MESH)
            out.append((pltpu.make_async_remote_copy(src_ref=src, dst_ref=dst, **sems),
                        pltpu.make_async_remote_copy(src_ref=src, dst_ref=arrives, **sems)))
    return out


def split_start(kind, srcs, lands, deps, name):
    n = len(srcs)
    n_sems = _peers_per_array(kind) * n

    def body(*refs):
        send_sems, recv_sems = refs[2 * n + len(deps)], refs[2 * n + len(deps) + 1]
        for copy, _ in _split_copies(kind, refs[:n], refs[n:2 * n], send_sems, recv_sems):
            copy.start()
        token = refs[-1]
        token[...] = jnp.zeros_like(token)

    held = [pltpu.HBM(a.shape, a.dtype) for a in list(srcs) + list(lands)]
    outs = pl.pallas_call(
        body, name=name,
        out_shape=(pltpu.SemaphoreType.DMA((n_sems,)), pltpu.SemaphoreType.DMA((n_sems,)), *held,
                   jax.ShapeDtypeStruct((8, 128), F32)),
        in_specs=[_HBM] * (2 * n) + [_ANY] * len(deps),
        out_specs=(_SEM, _SEM, *([_HBM] * (2 * n)), pl.BlockSpec(memory_space=pltpu.VMEM)),
        input_output_aliases={i: 2 + i for i in range(2 * n)},
        compiler_params=pltpu.CompilerParams(has_side_effects=_DATAFLOW),
    )(*[pltpu.with_memory_space_constraint(a, pltpu.HBM) for a in list(srcs) + list(lands)], *deps)
    return outs[0], outs[1], list(outs[2:2 + n]), list(outs[2 + n:2 + 2 * n]), outs[-1]


def split_wait(kind, send_sems, recv_sems, srcs, lands, afters, name):
    n = len(srcs)

    def body(*refs):
        for _, arrival in _split_copies(kind, refs[:n], refs[n:2 * n], refs[2 * n], refs[2 * n + 1]):
            arrival.wait_send()
            arrival.wait_recv()

    outs = pl.pallas_call(
        body, name=name,
        out_shape=[pltpu.HBM(a.shape, a.dtype) for a in list(srcs) + list(lands)],
        in_specs=[_HBM] * (2 * n) + [_SEM, _SEM] + [_ANY] * len(afters),
        out_specs=[_HBM] * (2 * n),
        input_output_aliases={i: i for i in range(2 * n)},
        compiler_params=pltpu.CompilerParams(has_side_effects=_DATAFLOW),
    )(*srcs, *lands, send_sems, recv_sems, *afters)
    return list(outs[:n]), list(outs[n:])


def place_block(land, block, dev, name):
    r, c = block.shape
    tr = min(r, 512)

    def body(dev_ref, land_ref, b_ref, o_ref):
        del dev_ref, land_ref
        o_ref[...] = b_ref[...]

    return pl.pallas_call(
        body, name=name,
        grid_spec=pltpu.PrefetchScalarGridSpec(
            num_scalar_prefetch=1, grid=(r // tr,),
            in_specs=[_ANY, pl.BlockSpec((tr, c), lambda i, dev_ref: (i, 0))],
            out_specs=pl.BlockSpec((None, tr, c), lambda i, dev_ref: (dev_ref[0], i, 0))),
        out_shape=jax.ShapeDtypeStruct(land.shape, land.dtype),
        input_output_aliases={1: 0},
        compiler_params=_params("parallel"),
    )(dev, land, block)


def gather_finish(lands, name):
    n = len(lands)

    def body(*refs):
        outs = refs[n:2 * n]
        send_sems, recv_sems = refs[2 * n:]
        x, y, c = _place()
        cps = [pltpu.make_async_remote_copy(
            src_ref=outs[a].at[:, pl.ds(c, 1)], dst_ref=outs[a].at[:, pl.ds(c, 1)],
            send_sem=send_sems.at[a], recv_sem=recv_sems.at[a],
            device_id=(x, y, 1 - c), device_id_type=MESH) for a in range(n)]
        for cp in cps:
            cp.start()
        for cp in cps:
            cp.wait()

    return pl.pallas_call(
        body, name=name,
        out_shape=[jax.ShapeDtypeStruct(a.shape, a.dtype) for a in lands],
        in_specs=[_ANY] * n, out_specs=[_ANY] * n,
        input_output_aliases={i: i for i in range(n)},
        scratch_shapes=[pltpu.SemaphoreType.DMA((n,)), pltpu.SemaphoreType.DMA((n,))],
    )(*lands)


def pair_add(own, recv, core, name):
    _, _, r, c = own.shape
    tr = min(r, 512)

    def body(core_ref, own_ref, recv_ref, o_ref):
        del core_ref
        o_ref[...] = (own_ref[...].astype(F32) + recv_ref[...].astype(F32)).astype(BF16)

    return pl.pallas_call(
        body, name=name,
        grid_spec=pltpu.PrefetchScalarGridSpec(
            num_scalar_prefetch=1, grid=(4, r // tr),
            in_specs=[pl.BlockSpec((None, None, tr, c), lambda k, i, core_ref: (k, core_ref[0], i, 0)),
                      pl.BlockSpec((None, None, tr, c), lambda k, i, core_ref: (k, 0, i, 0))],
            out_specs=pl.BlockSpec((None, tr, c), lambda k, i, core_ref: (k, i, 0))),
        out_shape=jax.ShapeDtypeStruct((4, r, c), BF16),
        compiler_params=_params("parallel", "parallel"),
    )(core, own, recv)


def _adamw_math(w, g, m, v):
    m2 = ADAM_B1 * m + (1.0 - ADAM_B1) * g
    v2 = ADAM_B2 * v + (1.0 - ADAM_B2) * (g * g)
    m_hat = m2 / (1.0 - ADAM_B1 ** ADAM_STEP)
    v_hat = v2 / (1.0 - ADAM_B2 ** ADAM_STEP)
    delta = -ADAM_LR * (m_hat / (jnp.sqrt(v_hat) + ADAM_EPS) + ADAM_WD * w)
    return delta, m2, v2


def adamw_big(w, m, v, own, got, chip, name):
    r, c = w.shape
    tr = min(r, 256)

    def body(chip_ref, w_ref, m_ref, v_ref, p0, p1, p2, p3, g_ref, d_ref, m2_ref, v2_ref):
        del chip_ref
        g = ((p0[...].astype(F32) + p1[...].astype(F32)) + p2[...].astype(F32)) + p3[...].astype(F32)
        delta, m2, v2 = _adamw_math(w_ref[...], g, m_ref[...], v_ref[...])
        g_ref[...] = g
        d_ref[...] = delta
        m2_ref[...] = m2
        v2_ref[...] = v2

    row = pl.BlockSpec((tr, c), lambda i, chip_ref: (i, 0))

    def slab(flip):
        return pl.BlockSpec((None, tr, c), lambda i, chip_ref: (chip_ref[0] ^ flip, i, 0))

    return pl.pallas_call(
        body, name=name,
        grid_spec=pltpu.PrefetchScalarGridSpec(
            num_scalar_prefetch=1, grid=(r // tr,),
            in_specs=[row, row, row, slab(0), slab(1), slab(2), slab(3)],
            out_specs=[row] * 4),
        out_shape=[jax.ShapeDtypeStruct((r, c), F32)] * 4,
        compiler_params=_params("parallel"),
    )(chip, w, m, v, own, got, got, got)


def sum_devices(g8, name):
    def body(g_ref, o_ref):
        tot = g_ref[0]
        for k in range(1, N_DEV):
            tot = tot + g_ref[k]
        o_ref[...] = tot

    return pl.pallas_call(body, name=name, out_shape=jax.ShapeDtypeStruct(g8.shape[1:], F32))(g8)


def adamw_small(ws, gs, ms, vs, name):
    n = len(ws)

    def body(*refs):
        w_r, g_r, m_r, v_r = refs[:n], refs[n:2 * n], refs[2 * n:3 * n], refs[3 * n:4 * n]
        d_o, m_o, v_o = refs[4 * n:5 * n], refs[5 * n:6 * n], refs[6 * n:7 * n]
        for k in range(n):
            delta, m2, v2 = _adamw_math(w_r[k][...], g_r[k][...], m_r[k][...], v_r[k][...])
            d_o[k][...] = delta
            m_o[k][...] = m2
            v_o[k][...] = v2

    shapes = [jax.ShapeDtypeStruct(w.shape, F32) for w in ws]
    outs = pl.pallas_call(body, name=name, out_shape=shapes * 3)(*ws, *gs, *ms, *vs)
    return outs[:n], outs[n:2 * n], outs[2 * n:]


def _rows128(a):
    return a.reshape(-1, 128)


def _pad_rows(a, rows):
    return jnp.pad(a, ((0, rows - a.shape[0]), (0, 0)))


def kernel(x, ln_pre_even, w_in_even, pool_w, pool_scale, w_out_even, ln_post_even, ln_pre_odd, w_in_odd, sconv_w, dconv_w, dconv_b, cnorm_g, cnorm_b, w_out_odd, ln_post_odd, loss_target, m_ln_pre_even, m_w_in_even, m_pool_w, m_pool_scale, m_w_out_even, m_ln_post_even, m_ln_pre_odd, m_w_in_odd, m_sconv_w, m_dconv_w, m_dconv_b, m_cnorm_g, m_cnorm_b, m_w_out_odd, m_ln_post_odd, v_ln_pre_even, v_w_in_even, v_pool_w, v_pool_scale, v_w_out_even, v_ln_post_even, v_ln_pre_odd, v_w_in_odd, v_sconv_w, v_dconv_w, v_dconv_b, v_cnorm_g, v_cnorm_b, v_w_out_odd, v_ln_post_odd):
    xs = x[0]
    tgt = loss_target[0]
    s, d = xs.shape
    half = d // 2
    n_heads = half // HEAD_DIM
    ng = len(POOL_WINDOWS)
    cwp = half // ng
    dev = 4 * lax.axis_index("x") + 2 * lax.axis_index("y") + lax.axis_index("c")
    core = lax.axis_index("c").astype(jnp.int32).reshape(1)

    pr = pool_w.shape[2]
    cl = sconv_w.shape[2]
    small_parts = [(_rows128(ln_pre_odd), 8), (sconv_w[0], 8), (dconv_w[0], 32), (dconv_b, 8),
                   (cnorm_g, 8), (cnorm_b, 8), (_rows128(ln_post_odd), 8)]
    small_local = jnp.concatenate([_pad_rows(a, r) for a, r in small_parts], axis=0)
    h0 = rms_fwd(xs, ln_pre_even, "rms_pre_even")
    p0, g_wie, (g_pw, g_small) = in_proj_gathered(
        h0, w_in_even[0].astype(BF16), [pool_w[0].reshape(ng * pr, cwp).astype(BF16), small_local],
        "ag_in_proj_even")
    comm = _Exchanges(dev, core, d)
    token = comm.start_weights("out_even", [w_out_even[0].astype(BF16)], [p0])
    sb_dep = comm.start_weights("odd", [w_in_odd[0].astype(BF16), w_out_odd[0].astype(BF16)], [token])
    pool_full = g_pw.reshape(N_DEV, ng, pr, cwp).transpose(1, 0, 2, 3).reshape(ng, cwp, cwp)
    nl = ln_pre_odd.shape[1] // 128

    def chan(lo, rows):
        return g_small[:, lo:lo + rows].transpose(1, 0, 2).reshape(rows, N_DEV * cl)

    ln_pre_odd_f = g_small[:, 0:nl].reshape(1, d)
    sconv_f = chan(8, SCONV_K)
    dconv_f = chan(16, CONF_K)
    dconv_b_f = chan(48, 1)
    cnorm_g_f = chan(56, 1)
    cnorm_b_f = chan(64, 1)
    ln_post_odd_f = g_small[:, 72:72 + nl].reshape(1, d)

    loss_blk, grad_x, small_g = _fwd_bwd(
        xs, tgt, ln_pre_even, h0, p0, g_wie, pool_full, pool_scale, ln_post_even, ln_pre_odd_f,
        sconv_f, dconv_f, dconv_b_f, cnorm_g_f, cnorm_b_f, ln_post_odd_f, comm, sb_dep)
    small_w = [ln_pre_even, pool_scale, ln_post_even, ln_pre_odd, sconv_w[0], dconv_w[0], dconv_b, cnorm_g, cnorm_b, ln_post_odd]
    small_m = [m_ln_pre_even, m_pool_scale, m_ln_post_even, m_ln_pre_odd, m_sconv_w[0], m_dconv_w[0], m_dconv_b, m_cnorm_g, m_cnorm_b, m_ln_post_odd]
    small_v = [v_ln_pre_even, v_pool_scale, v_ln_post_even, v_ln_pre_odd, v_sconv_w[0], v_dconv_w[0], v_dconv_b, v_cnorm_g, v_cnorm_b, v_ln_post_odd]
    big = {"w_in_even": (w_in_even, m_w_in_even, v_w_in_even), "pool_w": (pool_w, m_pool_w, v_pool_w),
           "w_out_even": (w_out_even, m_w_out_even, v_w_out_even), "w_in_odd": (w_in_odd, m_w_in_odd, v_w_in_odd),
           "w_out_odd": (w_out_odd, m_w_out_odd, v_w_out_odd)}
    upd = comm.finish_updates(big, [grad_x])
    upd.update(comm.finish_updates(big, [grad_x]))
    sg, sd, sm, sv, loss = _update_small(small_g, loss_blk, small_w, small_m, small_v, dev, d, cl,
                                         deps=[upd["w_in_odd"][1], upd["w_out_even"][1]])
    upd.update(comm.finish_updates(big, sd))
    (g_wie_o, d_wie, m_wie, v_wie), (g_pw_o, d_pw, m_pw, v_pw) = upd["w_in_even"], upd["pool_w"]
    (g_woe_o, d_woe, m_woe, v_woe), (g_wio_o, d_wio, m_wio, v_wio) = upd["w_out_even"], upd["w_in_odd"]
    g_woo_o, d_woo, m_woo, v_woo = upd["w_out_odd"]

    def order(small, wie, pw, woe, wio, woo):
        return [small[0], wie, pw, small[1], woe, small[2], small[3], wio, small[4], small[5], small[6],
                small[7], small[8], woo, small[9]]

    grads = order(sg, g_wie_o, g_pw_o, g_woe_o, g_wio_o, g_woo_o)
    deltas = order(sd, d_wie, d_pw, d_woe, d_wio, d_woo)
    new_m = order(sm, m_wie, m_pw, m_woe, m_wio, m_woo)
    new_v = order(sv, v_wie, v_pw, v_woe, v_wio, v_woo)
    return (loss, grad_x[None], *grads, *deltas, *new_m, *new_v)


def _fwd_bwd(xs, tgt, ln_pre_even, h0, p0, g_wie, pool_full, pool_scale, ln_post_even, ln_pre_odd_f,
             sconv_f, dconv_f, dconv_b_f, cnorm_g_f, cnorm_b_f, ln_post_odd_f, comm, sb_dep):
    d = xs.shape[1]
    n_heads = d // 2 // HEAD_DIM
    ng, cwp = pool_full.shape[0], pool_full.shape[1]
    a0, sb_wts = sb_fwd(p0, n_heads, "sb_fwd", dep=sb_dep)
    y0 = even_mix_fwd(a0, p0, pool_full, pool_scale, "even_mix_fwd")
    (w_out_e,) = comm.weights("out_even", after=y0)
    w_out_e = w_out_e.reshape(1, d, d)
    o0 = mm_nn(y0, w_out_e, F32, "out_proj_even", tm=1024, tn=1024)
    x1 = postnorm_fwd(xs, o0, ln_post_even, "post_even")
    g_wio, w_out_o = comm.weights("odd", after=x1)
    w_out_o = w_out_o.reshape(1, d, d)
    h1 = rms_fwd(x1, ln_pre_odd_f, "rms_pre_odd")
    p1 = mm_nn(h1, g_wio, BF16, "in_proj_odd")
    y1, dc = odd_mix_fwd(p1, sconv_f, dconv_f, dconv_b_f, cnorm_g_f, cnorm_b_f, "odd_mix_fwd")
    o1 = mm_nn(y1, w_out_o, F32, "out_proj_odd", tm=1024, tn=1024)
    loss_blk, gx2, do1, dg_post_odd = final_fwd_bwd(x1, o1, ln_post_odd_f, tgt, "post_odd_loss")

    dw_out_o = mm_tn(y1, do1, 1, BF16, "dw_out_odd", tm=1024)
    dy1 = mm_nt(do1, w_out_o, BF16, "dy_odd", tn=1024)
    ddc, dg2, dgam, dbet = odd_bwd_ln(dy1, p1, dc, cnorm_g_f, cnorm_b_f, "odd_bwd_ln")
    dp1, dsconv, ddconv, ddconv_b = odd_bwd_conv(dy1, p1, ddc, dg2, sconv_f, dconv_f, "odd_bwd_conv")
    dw_in_o = mm_tn(h1, dp1, N_DEV, BF16, "dw_in_odd")
    dep = comm.reduce_begin({"w_out_odd": dw_out_o.reshape(N_DEV, d // N_DEV, d), "w_in_odd": dw_in_o}, "odd")
    dh1 = mm_nt(dp1, g_wio, F32, "dh_odd", dep=dep)
    dep = comm.reduce_send(after=dh1)
    gx1, dg_pre_odd = norm_bwd(dh1, x1, ln_pre_odd_f, gx2, F32, "pre_odd_bwd", dep=dep)

    do0, dg_post_even = norm_bwd(gx1, o0, ln_post_even, None, BF16, "post_even_bwd")
    dw_out_e = mm_tn(y0, do0, 1, BF16, "dw_out_even", tm=1024)
    dy0 = mm_nt(do0, w_out_e, BF16, "dy_even", tn=1024)
    da0, du0, dg0, dpool, dpool_scale = even_mix_bwd(dy0, a0, p0, pool_full, pool_scale, "even_mix_bwd")
    pr = cwp // N_DEV
    dpool_slabs = dpool.astype(BF16).reshape(ng, N_DEV, pr, cwp).transpose(1, 0, 2, 3).reshape(N_DEV, ng * pr, cwp)
    dep = comm.reduce_begin({"w_out_even": dw_out_e.reshape(N_DEV, d // N_DEV, d), "pool_w": dpool_slabs}, "even_out")
    dq0, dk0, dv0 = sb_bwd(p0, a0, sb_wts, da0, n_heads, "sb_bwd", dep=dep)
    dep = comm.reduce_send(after=dq0)
    dp0 = jnp.concatenate([dq0, dk0, dv0, du0, dg0], axis=1)
    dw_in_e = mm_tn(h0, dp0, N_DEV, BF16, "dw_in_even", dep=dep)
    dep = comm.reduce_begin({"w_in_even": dw_in_e}, "even_in")
    dh0 = mm_nt(dp0, g_wie, F32, "dh_even", dep=dep)
    dep = comm.reduce_send(after=dh0)
    grad_x, dg_pre_even = norm_bwd(dh0, xs, ln_pre_even, gx1, F32, "pre_even_bwd", dep=dep)
    small_g = [dg_pre_even, dpool_scale, dg_post_even, dg_pre_odd, dsconv, ddconv, ddconv_b, dgam, dbet, dg_post_odd]
    return loss_blk, grad_x, small_g


class _Exchanges:
    def __init__(self, dev, core, d):
        self.dev = dev.astype(jnp.int32).reshape(1)
        self.core = core
        self.chip = (dev // 2).astype(jnp.int32).reshape(1)
        self.d = d
        self.in_flight = {}
        self.to_sibling = None
        self.pending = []

    def start_weights(self, tag, blocks, afters):
        lands = [lax.empty((N_DEV,) + b.shape, b.dtype) for b in blocks]
        send, recv, srcs, lands, token = split_start("gather", blocks, lands, afters, "ag_start_" + tag)
        self.in_flight[tag] = (send, recv, srcs, lands)
        return token

    def weights(self, tag, after):
        send, recv, srcs, lands = self.in_flight.pop(tag)
        srcs, lands = split_wait("gather", send, recv, srcs, lands, [after], "ag_wait_" + tag)
        lands = [place_block(l, b, self.dev, "ag_own_%s_%d" % (tag, k)) for k, (l, b) in enumerate(zip(lands, srcs))]
        full = gather_finish([l.reshape((4, 2) + l.shape[1:]) for l in lands], "ag_finish_" + tag)
        return [f.reshape((N_DEV,) + f.shape[2:]) for f in full]

    def reduce_begin(self, partials, tag):
        names = list(partials)
        arrs = [partials[k].reshape((4, 2) + partials[k].shape[1:]) for k in names]
        lands = [lax.empty((4, 1) + a.shape[2:], a.dtype) for a in arrs]
        send, recv, srcs, lands, token = split_start("sibling", arrs, lands, [], "rs_sibling_start_" + tag)
        self.to_sibling = (tag, names, send, recv, srcs, lands)
        return token

    def reduce_send(self, after):
        tag, names, send, recv, srcs, lands = self.to_sibling
        srcs, lands = split_wait("sibling", send, recv, srcs, lands, [after], "rs_sibling_wait_" + tag)
        sums = [pair_add(o, r, self.core, "rs_pair_add_" + k) for k, o, r in zip(names, srcs, lands)]
        zones = [lax.empty(a.shape, a.dtype) for a in sums]
        send, recv, srcs, zones, token = split_start("scatter", sums, zones, [], "rs_start_" + tag)
        self.pending.append((tag, names, send, recv, srcs, zones))
        return token

    def finish_updates(self, big, afters):
        tag, names, send, recv, srcs, lands = self.pending.pop(0)
        srcs, lands = split_wait("scatter", send, recv, srcs, lands, afters, "rs_wait_" + tag)
        out = {}
        for name, own, got in zip(names, srcs, lands):
            w, m, v = big[name]
            shp = own.shape[1:]
            outs = adamw_big(w.reshape(shp), m.reshape(shp), v.reshape(shp), own, got, self.chip, "adamw_" + name)
            out[name] = [o.reshape(w.shape) for o in outs]
        return out


def _update_small(small_g, loss_blk, small_w, small_m, small_v, dev, d, cl, deps):
    packed = jnp.concatenate([_rows128(g) for g in small_g] + [loss_blk], axis=0)
    (g8,) = all_gather([packed], "ag_small_grads", deps)
    tot = sum_devices(g8, "sum_small_grads")
    loss = tot[packed.shape[0] - 8, 0]
    full_g = []
    lo = 0
    for g in small_g:
        rows = g.size // 128
        full_g.append(tot[lo:lo + rows].reshape(g.shape))
        lo += rows

    def mine(g, width):
        return lax.dynamic_slice_in_dim(g, dev * width, width, axis=g.ndim - 1)

    fg = full_g
    small_gl = [fg[0], fg[1], fg[2], mine(fg[3], d // N_DEV), mine(fg[4], cl), mine(fg[5], cl), mine(fg[6], cl),
                mine(fg[7], cl), mine(fg[8], cl), mine(fg[9], d // N_DEV)]
    sd, sm, sv = adamw_small(small_w, small_gl, small_m, small_v, "adamw_small")

    def like(k, a):
        return a[None] if k in (4, 5) else a

    sg = [like(k, a) for k, a in enumerate(small_gl)]
    sd = [like(k, a) for k, a in enumerate(sd)]
    sm = [like(k, a) for k, a in enumerate(sm)]
    sv = [like(k, a) for k, a in enumerate(sv)]
    return sg, sd, sm, sv, loss
```

```python
import functools
import math

import jax
import jax.numpy as jnp
from jax import lax
from jax.experimental import pallas as pl
from jax.experimental.pallas import tpu as pltpu

F32 = jnp.float32
BF16 = jnp.bfloat16
EPS = 1e-6
HEAD_DIM = 128
POOL_WINDOWS = (2, 4, 8, 16)
SCONV_K = 3
CONF_K = 31
HALO = 32
N_DEV = 8
VMEM_LIMIT = 56 * 1024 * 1024
MESH = pl.DeviceIdType.MESH

ADAM_LR = 0.001
ADAM_B1 = 0.9
ADAM_B2 = 0.999
ADAM_EPS = 1e-08
ADAM_WD = 0.01
ADAM_STEP = 10


def _params(*sem):
    return pltpu.CompilerParams(dimension_semantics=sem, vmem_limit_bytes=VMEM_LIMIT)


def _sigmoid(v):
    return 1.0 / (1.0 + jnp.exp(-v))


def _silu(v):
    return v * _sigmoid(v)


def _silu_and_grad(v):
    s = _sigmoid(v)
    return v * s, s * (1.0 + v * (1.0 - s))


def _rowsum8(v):
    r, c = v.shape
    return jnp.sum(v.reshape(r // 8, 8, c), axis=0)


def _tap_before(xx, i, rows):
    if i == 0:
        return xx[HALO:HALO + rows]
    return pltpu.roll(xx, i, 0)[HALO:HALO + rows]


def _tap_after(xx, i, rows):
    if i == 0:
        return xx[0:rows]
    return pltpu.roll(xx, xx.shape[0] - i, 0)[0:rows]


def rms_fwd(x, g, name, tm=256):
    s, d = x.shape

    def body(x_ref, g_ref, h_ref):
        xv = x_ref[...]
        r = lax.rsqrt(jnp.mean(xv * xv, axis=-1, keepdims=True) + EPS)
        h_ref[...] = (xv * r * g_ref[...]).astype(BF16)

    return pl.pallas_call(
        body, name=name, grid=(s // tm,),
        in_specs=[pl.BlockSpec((tm, d), lambda i: (i, 0)), pl.BlockSpec((1, d), lambda i: (0, 0))],
        out_specs=pl.BlockSpec((tm, d), lambda i: (i, 0)),
        out_shape=jax.ShapeDtypeStruct((s, d), BF16),
        compiler_params=_params("parallel"),
    )(x, g)


def postnorm_fwd(x, o, g, name, tm=256):
    s, d = x.shape

    def body(x_ref, o_ref, g_ref, y_ref):
        ov = o_ref[...]
        r = lax.rsqrt(jnp.mean(ov * ov, axis=-1, keepdims=True) + EPS)
        y_ref[...] = x_ref[...] + ov * r * g_ref[...]

    return pl.pallas_call(
        body, name=name, grid=(s // tm,),
        in_specs=[pl.BlockSpec((tm, d), lambda i: (i, 0)), pl.BlockSpec((tm, d), lambda i: (i, 0)),
                  pl.BlockSpec((1, d), lambda i: (0, 0))],
        out_specs=pl.BlockSpec((tm, d), lambda i: (i, 0)),
        out_shape=jax.ShapeDtypeStruct((s, d), F32),
        compiler_params=_params("parallel"),
    )(x, o, g)


def final_fwd_bwd(x1, o, g, target, name, tm=256):
    s, d = x1.shape
    n = s // tm

    def body(x_ref, o_ref, g_ref, t_ref, loss_ref, gx_ref, do_ref, dg_ref, lacc, gacc):
        i = pl.program_id(0)

        @pl.when(i == 0)
        def _():
            lacc[...] = jnp.zeros_like(lacc)
            gacc[...] = jnp.zeros_like(gacc)

        ov = o_ref[...]
        gv = g_ref[...]
        r = lax.rsqrt(jnp.mean(ov * ov, axis=-1, keepdims=True) + EPS)
        oh = ov * r
        diff = x_ref[...] + oh * gv - t_ref[...]
        lacc[...] += _rowsum8(diff * diff)
        gx = diff * (1.0 / d)
        gx_ref[...] = gx
        gacc[...] += _rowsum8(gx * oh)
        dn = gx * gv
        do_ref[...] = (r * (dn - oh * jnp.mean(dn * oh, axis=-1, keepdims=True))).astype(BF16)

        @pl.when(i == n - 1)
        def _():
            tot = jnp.sum(jnp.sum(lacc[...], axis=0, keepdims=True), axis=1, keepdims=True)
            loss_ref[...] = jnp.broadcast_to(tot * (0.5 / d), loss_ref.shape)
            dg_ref[...] = jnp.sum(gacc[...], axis=0, keepdims=True)

    row = pl.BlockSpec((tm, d), lambda i: (i, 0))
    vec = pl.BlockSpec((1, d), lambda i: (0, 0))
    return pl.pallas_call(
        body, name=name, grid=(n,),
        in_specs=[row, row, vec, row],
        out_specs=[pl.BlockSpec((8, 128), lambda i: (0, 0)), row, row, vec],
        out_shape=[jax.ShapeDtypeStruct((8, 128), F32), jax.ShapeDtypeStruct((s, d), F32),
                   jax.ShapeDtypeStruct((s, d), BF16), jax.ShapeDtypeStruct((1, d), F32)],
        scratch_shapes=[pltpu.VMEM((8, d), F32), pltpu.VMEM((8, d), F32)],
        compiler_params=_params("arbitrary"),
    )(x1, o, g, target)


def norm_bwd(dy, inp, g, resid, out_dtype, name, tm=256, dep=None):
    s, d = inp.shape
    n = s // tm
    has_resid = resid is not None

    def body(*refs):
        dy_ref, x_ref, g_ref = refs[:3]
        r_ref = refs[3] if has_resid else None
        dx_ref, dg_ref, gacc = refs[-3:]
        i = pl.program_id(0)

        @pl.when(i == 0)
        def _():
            gacc[...] = jnp.zeros_like(gacc)

        xv = x_ref[...]
        dyv = dy_ref[...].astype(F32)
        r = lax.rsqrt(jnp.mean(xv * xv, axis=-1, keepdims=True) + EPS)
        xh = xv * r
        gacc[...] += _rowsum8(dyv * xh)
        dn = dyv * g_ref[...]
        dx = r * (dn - xh * jnp.mean(dn * xh, axis=-1, keepdims=True))
        if has_resid:
            dx = dx + r_ref[...]
        dx_ref[...] = dx.astype(out_dtype)

        @pl.when(i == n - 1)
        def _():
            dg_ref[...] = jnp.sum(gacc[...], axis=0, keepdims=True)

    row = pl.BlockSpec((tm, d), lambda i: (i, 0))
    vec = pl.BlockSpec((1, d), lambda i: (0, 0))
    dep_args, dep_specs = _after(dep)
    args = [dy, inp, g] + ([resid] if has_resid else []) + dep_args
    return pl.pallas_call(
        body, name=name, grid=(n,),
        in_specs=[row, row, vec] + ([row] if has_resid else []) + dep_specs,
        out_specs=[row, vec],
        out_shape=[jax.ShapeDtypeStruct((s, d), out_dtype), jax.ShapeDtypeStruct((1, d), F32)],
        scratch_shapes=[pltpu.VMEM((8, d), F32)],
        compiler_params=_params("arbitrary"),
    )(*args)


def _after(dep):
    if dep is None:
        return [], []
    return [dep], [pl.BlockSpec((8, 128), lambda *_: (0, 0))]


def mm_nn(a, w, out_dtype, name, tm=2048, tn=None, dep=None):
    m, k = a.shape
    tm = min(tm, m)
    ns, _, n = w.shape
    tn = n if tn is None else tn
    nj = n // tn
    dep_args, dep_specs = _after(dep)

    def body(a_ref, w_ref, *rest):
        o_ref = rest[-1]
        o_ref[...] = jnp.dot(a_ref[...], w_ref[0], preferred_element_type=F32).astype(out_dtype)

    return pl.pallas_call(
        body, name=name, grid=(ns, nj, m // tm),
        in_specs=[pl.BlockSpec((tm, k), lambda s, j, i: (i, 0)),
                  pl.BlockSpec((1, k, tn), lambda s, j, i: (s, 0, j))] + dep_specs,
        out_specs=pl.BlockSpec((tm, tn), lambda s, j, i: (i, s * nj + j)),
        out_shape=jax.ShapeDtypeStruct((m, ns * n), out_dtype),
        compiler_params=_params("parallel", "parallel", "parallel"),
    )(a, w, *dep_args)


def mm_nt(a, w, out_dtype, name, tm=1024, tn=None, dep=None):
    m = a.shape[0]
    tm = min(tm, m)
    ns, k, n = w.shape
    tn = n if tn is None else tn
    nj = n // tn
    steps = ns * nj
    dep_args, dep_specs = _after(dep)

    def body(a_ref, w_ref, *rest):
        o_ref, acc = rest[-2:]
        r = pl.program_id(1)

        @pl.when(r == 0)
        def _():
            acc[...] = jnp.zeros_like(acc)

        acc[...] += lax.dot_general(a_ref[...], w_ref[0], (((1,), (1,)), ((), ())),
                                    preferred_element_type=F32)

        @pl.when(r == steps - 1)
        def _():
            o_ref[...] = acc[...].astype(out_dtype)

    return pl.pallas_call(
        body, name=name, grid=(m // tm, steps),
        in_specs=[pl.BlockSpec((tm, tn), lambda i, r: (i, r)),
                  pl.BlockSpec((1, k, tn), lambda i, r: (r // nj, 0, r % nj))] + dep_specs,
        out_specs=pl.BlockSpec((tm, k), lambda i, r: (i, 0)),
        out_shape=jax.ShapeDtypeStruct((m, k), out_dtype),
        scratch_shapes=[pltpu.VMEM((tm, k), F32)],
        compiler_params=_params("parallel", "arbitrary"),
    )(a, w, *dep_args)


def mm_tn(a, b, ns, out_dtype, name, tk=1024, tm=2048, dep=None):
    m, k = a.shape
    tm = min(tm, m)
    n = b.shape[1] // ns
    steps = m // tm
    dep_args, dep_specs = _after(dep)

    def body(a_ref, b_ref, *rest):
        o_ref, acc = rest[-2:]
        r = pl.program_id(2)

        @pl.when(r == 0)
        def _():
            acc[...] = jnp.zeros_like(acc)

        acc[...] += lax.dot_general(a_ref[...], b_ref[...], (((0,), (0,)), ((), ())),
                                    preferred_element_type=F32)

        @pl.when(r == steps - 1)
        def _():
            o_ref[0] = acc[...].astype(out_dtype)

    return pl.pallas_call(
        body, name=name, grid=(ns, k // tk, steps),
        in_specs=[pl.BlockSpec((tm, tk), lambda s, j, r: (r, j)),
                  pl.BlockSpec((tm, n), lambda s, j, r: (r, s))] + dep_specs,
        out_specs=pl.BlockSpec((1, tk, n), lambda s, j, r: (s, j, 0)),
        out_shape=jax.ShapeDtypeStruct((ns, k, n), out_dtype),
        scratch_shapes=[pltpu.VMEM((tk, n), F32)],
        compiler_params=_params("parallel", "parallel", "arbitrary"),
    )(a, b, *dep_args)


SB_BLK = 128


LOG2E = 1.0 / math.log(2.0)


def _split_dot(v, tri):
    hi = pltpu.bitcast(pltpu.bitcast(v, jnp.uint32) & jnp.uint32(0xFFFF0000), F32)
    lo = (v - hi).astype(BF16)
    return (jnp.dot(hi.astype(BF16), tri, preferred_element_type=F32)
            + jnp.dot(lo, tri, preferred_element_type=F32))


def _sb_scores(z2, lim, dcol, tri_ex, masked):
    sp = jnp.log2(1.0 + jnp.exp2(-jnp.abs(z2)))
    lb = jnp.minimum(z2, 0.0) - sp
    l1m = lb - z2
    mask = None
    if masked:
        mask = dcol < lim
        l1m = jnp.where(mask, l1m, 0.0)
    return mask, lb, l1m, _split_dot(l1m, tri_ex)


def _sb_consts():
    row = lax.broadcasted_iota(jnp.int32, (SB_BLK, SB_BLK), 0)
    col = lax.broadcasted_iota(jnp.int32, (SB_BLK, SB_BLK), 1)
    tri_ex = jnp.where(row > col, 1.0, 0.0).astype(BF16)
    tri_in = jnp.where(row >= col, 1.0, 0.0).astype(BF16)
    return col - row, tri_ex, tri_in


def sb_fwd(p, n_heads, name, tq=256, nsub=4, dep=None):
    s = p.shape[0]
    h_n = n_heads
    b = SB_BLK
    nqs = tq // b
    tk = nsub * b
    scale = 1.0 / math.sqrt(HEAD_DIM)

    dep_args, dep_specs = _after(dep)

    def body(q_ref, k_ref, v_ref, *rest):
        o_ref, w_ref = rest[-2:]
        qi = pl.program_id(1)
        dcol, tri_ex, _ = _sb_consts()
        qv = [q_ref[qs * b:(qs + 1) * b, :] for qs in range(nqs)]
        n_groups = ((qi + 1) * nqs - 1) // nsub + 1

        def step(it, carry, masked):
            c1s, accs = carry
            g = n_groups - 1 - it
            off = pl.multiple_of(g * tk, tk)
            kg = k_ref[pl.ds(off, tk), :]
            vg = v_ref[pl.ds(off, tk), :]
            new_c1, new_acc = [], []
            for qs in range(nqs):
                qb = qi * nqs + qs
                z2 = lax.dot_general(qv[qs], kg, (((1,), (1,)), ((), ())),
                                     preferred_element_type=F32) * (scale * LOG2E)
                blocks = [_sb_scores(z2[:, j * b:(j + 1) * b], (qb - (g * nsub + j)) * b, dcol, tri_ex, masked)
                          for j in range(nsub)]
                run = c1s[qs]
                ws = [None] * nsub
                for j in reversed(range(nsub)):
                    mask, lb, l1m, ls_loc = blocks[j]
                    wj = jnp.exp2(lb + ls_loc + run)
                    ws[j] = (jnp.where(mask, wj, 0.0) if masked else wj).astype(BF16)
                    run = run + jnp.sum(l1m, axis=1, keepdims=True)
                w = jnp.concatenate(ws, axis=1)
                w_ref[0, g, qs * b:(qs + 1) * b, :] = w
                new_acc.append(accs[qs] + jnp.dot(w, vg, preferred_element_type=F32))
                new_c1.append(run)
            return tuple(new_c1), tuple(new_acc)

        init = (tuple(jnp.zeros((b, 1), F32) for _ in range(nqs)),
                tuple(jnp.zeros((b, HEAD_DIM), F32) for _ in range(nqs)))
        assert nqs == 2 and nsub % 2 == 0
        first = step(0, init, True)
        _, accs = lax.fori_loop(1, n_groups, functools.partial(step, masked=False), first)
        for qs in range(nqs):
            o_ref[qs * b:(qs + 1) * b, :] = accs[qs]

    return pl.pallas_call(
        body, name=name, grid=(h_n, s // tq),
        in_specs=[pl.BlockSpec((tq, HEAD_DIM), lambda h, i: (i, h)),
                  pl.BlockSpec((s, HEAD_DIM), lambda h, i: (0, h_n + h)),
                  pl.BlockSpec((s, HEAD_DIM), lambda h, i: (0, 2 * h_n + h))] + dep_specs,
        out_specs=[pl.BlockSpec((tq, HEAD_DIM), lambda h, i: (i, h)),
                   pl.BlockSpec((1, s // tk, tq, tk), lambda h, i: (h, 0, i, 0))],
        out_shape=[jax.ShapeDtypeStruct((s, h_n * HEAD_DIM), F32),
                   jax.ShapeDtypeStruct((h_n, s // tk, s, tk), BF16)],
        compiler_params=_params("parallel", "arbitrary"),
    )(p, p, p, *dep_args)


def sb_bwd(p, a, wts, da, n_heads, name, tq=256, dep=None):
    s = p.shape[0]
    h_n = n_heads
    nq = s // tq
    b = SB_BLK
    nqs = tq // b
    tk = wts.shape[3]
    nsub = tk // b
    scale = 1.0 / math.sqrt(HEAD_DIM)
    dep_args, dep_specs = _after(dep)

    def body(q_ref, k_ref, v_ref, a_ref, da_ref, w_ref, *rest):
        dq_ref, dk_ref, dv_ref, dk_acc, dv_acc = rest[-5:]
        qi = pl.program_id(1)

        @pl.when(qi == 0)
        def _():
            dk_acc[...] = jnp.zeros_like(dk_acc)
            dv_acc[...] = jnp.zeros_like(dv_acc)

        dcol, _, tri_in = _sb_consts()
        q_all = q_ref[...]
        do_all = da_ref[...]
        qv = [q_ref[qs * b:(qs + 1) * b, :] for qs in range(nqs)]
        dov = [da_ref[qs * b:(qs + 1) * b, :] for qs in range(nqs)]
        tots = [jnp.sum(dov[qs].astype(F32) * a_ref[qs * b:(qs + 1) * b, :], axis=1, keepdims=True)
                for qs in range(nqs)]
        n_groups = ((qi + 1) * nqs - 1) // nsub + 1

        def step(it, carry, masked):
            c2s, dqs = carry
            g = n_groups - 1 - it
            off = pl.multiple_of(g * tk, tk)
            kg = k_ref[pl.ds(off, tk), :]
            vg = v_ref[pl.ds(off, tk), :]
            w_all = w_ref[0, g]
            new_c2, new_dq, dz_rows = [], [], []
            for qs in range(nqs):
                qb = qi * nqs + qs
                z2 = lax.dot_general(qv[qs], kg, (((1,), (1,)), ((), ())),
                                     preferred_element_type=F32) * (-scale * LOG2E)
                dw = lax.dot_general(dov[qs], vg, (((1,), (1,)), ((), ())), preferred_element_type=F32)
                beta = 1.0 / (1.0 + jnp.exp2(z2))
                e = dw * w_all[qs * b:(qs + 1) * b, :].astype(F32)
                run2 = c2s[qs]
                dzs = [None] * nsub
                for j in reversed(range(nsub)):
                    cols = slice(j * b, (j + 1) * b)
                    later = _split_dot(e[:, cols], tri_in) + run2
                    bj = beta[:, cols]
                    dz = (e[:, cols] * (1.0 - bj) - bj * (tots[qs] - later)) * scale
                    if masked:
                        dz = jnp.where(dcol < (qb - (g * nsub + j)) * b, dz, 0.0)
                    dzs[j] = dz.astype(BF16)
                    run2 = run2 + jnp.sum(e[:, cols], axis=1, keepdims=True)
                dzq = jnp.concatenate(dzs, axis=1)
                new_dq.append(dqs[qs] + jnp.dot(dzq, kg, preferred_element_type=F32))
                new_c2.append(run2)
                dz_rows.append(dzq)
            dz_all = jnp.concatenate(dz_rows, axis=0)
            dk_acc[pl.ds(off, tk), :] += lax.dot_general(dz_all, q_all, (((0,), (0,)), ((), ())),
                                                         preferred_element_type=F32)
            dv_acc[pl.ds(off, tk), :] += lax.dot_general(w_all, do_all, (((0,), (0,)), ((), ())),
                                                         preferred_element_type=F32)
            return tuple(new_c2), tuple(new_dq)

        zeros = tuple(jnp.zeros((b, 1), F32) for _ in range(nqs))
        assert nqs == 2 and nsub % 2 == 0
        first = step(0, (zeros, tuple(jnp.zeros((b, HEAD_DIM), F32) for _ in range(nqs))), True)
        _, dqs = lax.fori_loop(1, n_groups, functools.partial(step, masked=False), first)
        for qs in range(nqs):
            dq_ref[qs * b:(qs + 1) * b, :] = dqs[qs].astype(BF16)

        @pl.when(qi == nq - 1)
        def _():
            dk_ref[...] = dk_acc[...].astype(BF16)
            dv_ref[...] = dv_acc[...].astype(BF16)

    blk = pl.BlockSpec((tq, HEAD_DIM), lambda h, i: (i, h))
    full = pl.BlockSpec((s, HEAD_DIM), lambda h, i: (0, h))
    return pl.pallas_call(
        body, name=name, grid=(h_n, nq),
        in_specs=[blk, pl.BlockSpec((s, HEAD_DIM), lambda h, i: (0, h_n + h)),
                  pl.BlockSpec((s, HEAD_DIM), lambda h, i: (0, 2 * h_n + h)), blk, blk,
                  pl.BlockSpec((1, s // tk, tq, tk), lambda h, i: (h, 0, i, 0))] + dep_specs,
        out_specs=[blk, full, full],
        out_shape=[jax.ShapeDtypeStruct((s, h_n * HEAD_DIM), BF16)] * 3,
        scratch_shapes=[pltpu.VMEM((s, HEAD_DIM), F32), pltpu.VMEM((s, HEAD_DIM), F32)],
        compiler_params=_params("parallel", "arbitrary"),
    )(p, p, p, a, da, wts, *dep_args)


def _pool_window(xx, win, r0, rc):
    cur = xx[HALO:HALO + rc]
    ws = cur
    for i in range(1, win):
        ws = ws + _tap_before(xx, i, rc)
    t_idx = r0 + lax.broadcasted_iota(jnp.int32, (rc, 1), 0)
    inv = 1.0 / jnp.minimum(win, t_idx + 1).astype(F32)
    return ws * inv - cur, inv


def even_mix_fwd(a, p, pool_w, pool_scale, name, rc=64):
    s = p.shape[0]
    ng = len(POOL_WINDOWS)
    cw = pool_w.shape[1]
    n_chunks = s // rc

    def body(a_ref, u_ref, g_ref, w_ref, sc_ref, y_ref, upad):
        j = pl.program_id(0)

        @pl.when(j < ng)
        def _():
            def chunk(ci, carry):
                rows = pl.ds(pl.multiple_of(ci * rc, rc), rc)
                y_ref[rows, :] = (a_ref[rows, :] * _silu(g_ref[rows, :].astype(F32))).astype(BF16)
                return carry

            lax.fori_loop(0, n_chunks, chunk, 0)

        for gi, win in enumerate(POOL_WINDOWS):
            @pl.when(j == ng + gi)
            def _(win=win):
                upad[0:HALO, :] = jnp.zeros((HALO, cw), F32)

                def fill(ci, carry):
                    r0 = pl.multiple_of(ci * rc, rc)
                    upad[pl.ds(pl.multiple_of(r0 + HALO, HALO), rc), :] = u_ref[pl.ds(r0, rc), :].astype(F32)
                    return carry

                lax.fori_loop(0, n_chunks, fill, 0)

                def chunk(ci, carry):
                    r0 = pl.multiple_of(ci * rc, rc)
                    rows = pl.ds(r0, rc)
                    pooled, _ = _pool_window(upad[pl.ds(r0, HALO + rc), :], win, r0, rc)
                    t = jnp.dot(pooled.astype(BF16), w_ref[0], preferred_element_type=F32)
                    y_ref[rows, :] = (t * sc_ref[...] * _silu(g_ref[rows, :].astype(F32))).astype(BF16)
                    return carry

                lax.fori_loop(0, n_chunks, chunk, 0)

    grp = lambda j: jnp.maximum(j - ng, 0)
    return pl.pallas_call(
        body, name=name, grid=(2 * ng,),
        in_specs=[pl.BlockSpec((s, cw), lambda j: (0, jnp.minimum(j, ng - 1))),
                  pl.BlockSpec((s, cw), lambda j: (0, 3 * ng + grp(j))),
                  pl.BlockSpec((s, cw), lambda j: (0, 4 * ng + j)),
                  pl.BlockSpec((1, cw, cw), lambda j: (grp(j), 0, 0)),
                  pl.BlockSpec((1, cw), lambda j: (0, grp(j)))],
        out_specs=pl.BlockSpec((s, cw), lambda j: (0, j)),
        out_shape=jax.ShapeDtypeStruct((s, 2 * ng * cw), BF16),
        scratch_shapes=[pltpu.VMEM((HALO + s, cw), F32)],
        compiler_params=_params("arbitrary"),
    )(a, p, p, pool_w, pool_scale)


def even_mix_bwd(dy, a, p, pool_w, pool_scale, name, rc=64):
    s = p.shape[0]
    ng = len(POOL_WINDOWS)
    cw = pool_w.shape[1]
    n_chunks = s // rc

    def body(dy_ref, a_ref, u_ref, g_ref, w_ref, sc_ref, da_ref, du_ref, dg_ref, dw_ref, dsc_ref,
             upad, rpad, dpl, dw_acc, dsc_acc):
        j = pl.program_id(0)

        @pl.when(j < ng)
        def _():
            def chunk(ci, carry):
                rows = pl.ds(pl.multiple_of(ci * rc, rc), rc)
                dyv = dy_ref[rows, :].astype(F32)
                sg, dsg = _silu_and_grad(g_ref[rows, :].astype(F32))
                da_ref[rows, :] = (dyv * sg).astype(BF16)
                dg_ref[rows, :] = (dyv * a_ref[rows, :] * dsg).astype(BF16)
                return carry

            lax.fori_loop(0, n_chunks, chunk, 0)

        for gi, win in enumerate(POOL_WINDOWS):
            @pl.when(j == ng + gi)
            def _(win=win):
                upad[0:HALO, :] = jnp.zeros((HALO, cw), F32)
                rpad[s:s + HALO, :] = jnp.zeros((HALO, cw), F32)
                dw_acc[...] = jnp.zeros_like(dw_acc)
                dsc_acc[...] = jnp.zeros_like(dsc_acc)

                def fill(ci, carry):
                    r0 = pl.multiple_of(ci * rc, rc)
                    upad[pl.ds(pl.multiple_of(r0 + HALO, HALO), rc), :] = u_ref[pl.ds(r0, rc), :].astype(F32)
                    return carry

                lax.fori_loop(0, n_chunks, fill, 0)

                def chunk(ci, carry):
                    r0 = pl.multiple_of(ci * rc, rc)
                    rows = pl.ds(r0, rc)
                    pooled, inv = _pool_window(upad[pl.ds(r0, HALO + rc), :], win, r0, rc)
                    pb = pooled.astype(BF16)
                    wv = w_ref[0]
                    t = jnp.dot(pb, wv, preferred_element_type=F32)
                    scv = sc_ref[...]
                    dyv = dy_ref[rows, :].astype(F32)
                    sg, dsg = _silu_and_grad(g_ref[rows, :].astype(F32))
                    dpo = dyv * sg
                    dg_ref[rows, :] = (dyv * t * scv * dsg).astype(BF16)
                    dsc_acc[...] += _rowsum8(dpo * t)
                    dtb = (dpo * scv).astype(BF16)
                    dw_acc[...] += lax.dot_general(pb, dtb, (((0,), (0,)), ((), ())),
                                                   preferred_element_type=F32)
                    dpooled = lax.dot_general(dtb, wv, (((1,), (1,)), ((), ())),
                                              preferred_element_type=F32)
                    dpl[rows, :] = dpooled
                    rpad[rows, :] = dpooled * inv
                    return carry

                lax.fori_loop(0, n_chunks, chunk, 0)

                def chunk2(ci, carry):
                    r0 = pl.multiple_of(ci * rc, rc)
                    rows = pl.ds(r0, rc)
                    xx = rpad[pl.ds(r0, rc + HALO), :]
                    fs = xx[0:rc]
                    for i in range(1, win):
                        fs = fs + _tap_after(xx, i, rc)
                    du_ref[rows, :] = (fs - dpl[rows, :]).astype(BF16)
                    return carry

                lax.fori_loop(0, n_chunks, chunk2, 0)
                dw_ref[0] = dw_acc[...]
                dsc_ref[...] = jnp.sum(dsc_acc[...], axis=0, keepdims=True)

    grp = lambda j: jnp.maximum(j - ng, 0)
    att = lambda j: jnp.minimum(j, ng - 1)
    return pl.pallas_call(
        body, name=name, grid=(2 * ng,),
        in_specs=[pl.BlockSpec((s, cw), lambda j: (0, j)),
                  pl.BlockSpec((s, cw), lambda j: (0, att(j))),
                  pl.BlockSpec((s, cw), lambda j: (0, 3 * ng + grp(j))),
                  pl.BlockSpec((s, cw), lambda j: (0, 4 * ng + j)),
                  pl.BlockSpec((1, cw, cw), lambda j: (grp(j), 0, 0)),
                  pl.BlockSpec((1, cw), lambda j: (0, grp(j)))],
        out_specs=[pl.BlockSpec((s, cw), lambda j: (0, att(j))),
                   pl.BlockSpec((s, cw), lambda j: (0, grp(j))),
                   pl.BlockSpec((s, cw), lambda j: (0, j)),
                   pl.BlockSpec((1, cw, cw), lambda j: (grp(j), 0, 0)),
                   pl.BlockSpec((1, cw), lambda j: (0, grp(j)))],
        out_shape=[jax.ShapeDtypeStruct((s, ng * cw), BF16), jax.ShapeDtypeStruct((s, ng * cw), BF16),
                   jax.ShapeDtypeStruct((s, 2 * ng * cw), BF16),
                   jax.ShapeDtypeStruct((ng, cw, cw), F32), jax.ShapeDtypeStruct((1, ng * cw), F32)],
        scratch_shapes=[pltpu.VMEM((HALO + s, cw), F32), pltpu.VMEM((s + HALO, cw), F32),
                        pltpu.VMEM((s, cw), F32), pltpu.VMEM((cw, cw), F32), pltpu.VMEM((8, cw), F32)],
        compiler_params=_params("arbitrary"),
    )(dy, a, p, p, pool_w, pool_scale)


def _halo_before(tm):
    return lambda i: jnp.maximum(i * (tm // HALO) - 1, 0)


def _halo_after(tm, s):
    return lambda i: jnp.minimum((i + 1) * (tm // HALO), s // HALO - 1)


def odd_mix_fwd(p, sconv_w, dconv_w, dconv_b, cnorm_g, cnorm_b, name, tm=128):
    s = p.shape[0]
    cw = sconv_w.shape[1]
    n = s // tm
    lanes = 128
    hb = _halo_before(tm)

    def body(hc_ref, hch_ref, bc_ref, cc_ref, cch_ref, ga_ref, gah_ref, gb_ref, gbh_ref, g1_ref, g2_ref,
             sw_ref, dw_ref, db_ref, gam_ref, bet_ref, y_ref, dc_ref):
        first = pl.program_id(0) == 0
        for l in range(cw // lanes):
            cols = slice(l * lanes, (l + 1) * lanes)
            mh = jnp.where(first, 0.0, cch_ref[:, cols].astype(F32) * hch_ref[:, cols].astype(F32))
            mm = cc_ref[:, cols].astype(F32) * hc_ref[:, cols].astype(F32)
            xx = jnp.concatenate([mh, mm], axis=0)
            cv = jnp.zeros((tm, lanes), F32)
            for k in range(SCONV_K):
                cv = cv + sw_ref[k:k + 1, cols] * _tap_before(xx, SCONV_K - 1 - k, tm)
            c_out = bc_ref[:, cols].astype(F32) * cv
            y_ref[:, cols] = (c_out * _silu(g1_ref[:, cols].astype(F32))).astype(BF16)
            dh = jnp.where(first, 0.0, gah_ref[:, cols].astype(F32) * _sigmoid(gbh_ref[:, cols].astype(F32)))
            dm = ga_ref[:, cols].astype(F32) * _sigmoid(gb_ref[:, cols].astype(F32))
            xx = jnp.concatenate([dh, dm], axis=0)
            acc = jnp.zeros((tm, lanes), F32) + db_ref[:, cols]
            for k in range(CONF_K):
                acc = acc + dw_ref[k:k + 1, cols] * _tap_before(xx, CONF_K - 1 - k, tm)
            dc_ref[:, cols] = acc
        rs = 32
        for r in range(tm // rs):
            rows = slice(r * rs, (r + 1) * rs)
            xv = dc_ref[rows, :]
            mu = jnp.mean(xv, axis=-1, keepdims=True)
            xc = xv - mu
            rstd = lax.rsqrt(jnp.mean(xc * xc, axis=-1, keepdims=True) + EPS)
            ln = xc * rstd * gam_ref[...] + bet_ref[...]
            y_ref[rows, cw:2 * cw] = (_silu(ln) * _silu(g2_ref[rows, :].astype(F32))).astype(BF16)

    main = lambda c: pl.BlockSpec((tm, cw), lambda i: (i, c))
    halo = lambda c: pl.BlockSpec((HALO, cw), lambda i: (hb(i), c))
    vec = lambda r: pl.BlockSpec((r, cw), lambda i: (0, 0))
    return pl.pallas_call(
        body, name=name, grid=(n,),
        in_specs=[main(0), halo(0), main(1), main(2), halo(2), main(3), halo(3), main(4), halo(4),
                  main(5), main(6), vec(SCONV_K), vec(CONF_K), vec(1), vec(1), vec(1)],
        out_specs=[pl.BlockSpec((tm, 2 * cw), lambda i: (i, 0)), pl.BlockSpec((tm, cw), lambda i: (i, 0))],
        out_shape=[jax.ShapeDtypeStruct((s, 2 * cw), BF16), jax.ShapeDtypeStruct((s, cw), F32)],
        compiler_params=_params("parallel"),
    )(p, p, p, p, p, p, p, p, p, p, p, sconv_w, dconv_w, dconv_b, cnorm_g, cnorm_b)


def odd_bwd_ln(dy, p, dc, cnorm_g, cnorm_b, name, tm=256):
    s = p.shape[0]
    cw = dc.shape[1]
    n = s // tm
    rs = 32

    def body(dy_ref, g2_ref, dc_ref, gam_ref, bet_ref, ddc_ref, dg_ref, dgam_ref, dbet_ref, gacc, bacc):
        i = pl.program_id(0)

        @pl.when(i == 0)
        def _():
            gacc[...] = jnp.zeros_like(gacc)
            bacc[...] = jnp.zeros_like(bacc)

        def chunk(ci, carry):
            rows = pl.ds(pl.multiple_of(ci * rs, rs), rs)
            xv = dc_ref[rows, :]
            mu = jnp.mean(xv, axis=-1, keepdims=True)
            xc = xv - mu
            rstd = lax.rsqrt(jnp.mean(xc * xc, axis=-1, keepdims=True) + EPS)
            xh = xc * rstd
            gam = gam_ref[...]
            sl, dsl = _silu_and_grad(xh * gam + bet_ref[...])
            sg, dsg = _silu_and_grad(g2_ref[rows, :].astype(F32))
            dyv = dy_ref[rows, :].astype(F32)
            dg_ref[rows, :] = (dyv * sl * dsg).astype(BF16)
            dln = dyv * sg * dsl
            gacc[...] += _rowsum8(dln * xh)
            bacc[...] += _rowsum8(dln)
            dxh = dln * gam
            ddc_ref[rows, :] = rstd * (dxh - jnp.mean(dxh, axis=-1, keepdims=True)
                                       - xh * jnp.mean(dxh * xh, axis=-1, keepdims=True))
            return carry

        lax.fori_loop(0, tm // rs, chunk, 0)

        @pl.when(i == n - 1)
        def _():
            dgam_ref[...] = jnp.sum(gacc[...], axis=0, keepdims=True)
            dbet_ref[...] = jnp.sum(bacc[...], axis=0, keepdims=True)

    vec = pl.BlockSpec((1, cw), lambda i: (0, 0))
    return pl.pallas_call(
        body, name=name, grid=(n,),
        in_specs=[pl.BlockSpec((tm, cw), lambda i: (i, 1)), pl.BlockSpec((tm, cw), lambda i: (i, 6)),
                  pl.BlockSpec((tm, cw), lambda i: (i, 0)), vec, vec],
        out_specs=[pl.BlockSpec((tm, cw), lambda i: (i, 0)), pl.BlockSpec((tm, cw), lambda i: (i, 0)), vec, vec],
        out_shape=[jax.ShapeDtypeStruct((s, cw), F32), jax.ShapeDtypeStruct((s, cw), BF16),
                   jax.ShapeDtypeStruct((1, cw), F32), jax.ShapeDtypeStruct((1, cw), F32)],
        scratch_shapes=[pltpu.VMEM((8, cw), F32), pltpu.VMEM((8, cw), F32)],
        compiler_params=_params("arbitrary"),
    )(dy, p, dc, cnorm_g, cnorm_b)


def odd_bwd_conv(dy, p, ddc, dg2, sconv_w, dconv_w, name, tm=128):
    s = p.shape[0]
    cw = ddc.shape[1]
    n = s // tm
    lanes = 128
    hb = _halo_before(tm)
    ha = _halo_after(tm, s)

    def body(dy_ref, dya_ref, g1_ref, g1a_ref, bc_ref, bca_ref, hc_ref, hch_ref, cc_ref, cch_ref,
             ddc_ref, ddca_ref, ga_ref, gah_ref, gb_ref, gbh_ref, dg2_ref, sw_ref, dw_ref,
             dp_ref, dsw_ref, ddw_ref, ddb_ref, sw_acc, dw_acc, db_acc):
        i = pl.program_id(0)
        first = i == 0
        last = i == n - 1

        @pl.when(first)
        def _():
            sw_acc[...] = jnp.zeros_like(sw_acc)
            dw_acc[...] = jnp.zeros_like(dw_acc)
            db_acc[...] = jnp.zeros_like(db_acc)

        for l in range(cw // lanes):
            cols = slice(l * lanes, (l + 1) * lanes)
            mh = jnp.where(first, 0.0, cch_ref[:, cols].astype(F32) * hch_ref[:, cols].astype(F32))
            hcv = hc_ref[:, cols].astype(F32)
            ccv = cc_ref[:, cols].astype(F32)
            xx = jnp.concatenate([mh, ccv * hcv], axis=0)
            taps = [_tap_before(xx, SCONV_K - 1 - k, tm) for k in range(SCONV_K)]
            cv = jnp.zeros((tm, lanes), F32)
            for k in range(SCONV_K):
                cv = cv + sw_ref[k:k + 1, cols] * taps[k]
            bcv = bc_ref[:, cols].astype(F32)
            dyv = dy_ref[:, cols].astype(F32)
            sg, dsg = _silu_and_grad(g1_ref[:, cols].astype(F32))
            dco = dyv * sg
            dp_ref[:, 5 * cw + l * lanes:5 * cw + (l + 1) * lanes] = (dyv * bcv * cv * dsg).astype(BF16)
            dp_ref[:, cw + l * lanes:cw + (l + 1) * lanes] = (dco * cv).astype(BF16)
            dcv = dco * bcv
            for k in range(SCONV_K):
                sw_acc[k * 8:(k + 1) * 8, cols] += _rowsum8(dcv * taps[k])
            dcv_a = jnp.where(last, 0.0, dya_ref[:, cols].astype(F32) * _silu(g1a_ref[:, cols].astype(F32))
                              * bca_ref[:, cols].astype(F32))
            xx = jnp.concatenate([dcv, dcv_a], axis=0)
            dm = jnp.zeros((tm, lanes), F32)
            for k in range(SCONV_K):
                dm = dm + sw_ref[k:k + 1, cols] * _tap_after(xx, SCONV_K - 1 - k, tm)
            dp_ref[:, l * lanes:(l + 1) * lanes] = (dm * ccv).astype(BF16)
            dp_ref[:, 2 * cw + l * lanes:2 * cw + (l + 1) * lanes] = (dm * hcv).astype(BF16)
            gav = ga_ref[:, cols].astype(F32)
            sb = _sigmoid(gb_ref[:, cols].astype(F32))
            dh = jnp.where(first, 0.0, gah_ref[:, cols].astype(F32) * _sigmoid(gbh_ref[:, cols].astype(F32)))
            xx = jnp.concatenate([dh, gav * sb], axis=0)
            ddcv = ddc_ref[:, cols]
            db_acc[:, cols] += _rowsum8(ddcv)
            for k in range(CONF_K):
                dw_acc[k * 8:(k + 1) * 8, cols] += _rowsum8(ddcv * _tap_before(xx, CONF_K - 1 - k, tm))
            ddc_a = jnp.where(last, 0.0, ddca_ref[:, cols])
            xx = jnp.concatenate([ddcv, ddc_a], axis=0)
            dgl = jnp.zeros((tm, lanes), F32)
            for k in range(CONF_K):
                dgl = dgl + dw_ref[k:k + 1, cols] * _tap_after(xx, CONF_K - 1 - k, tm)
            dp_ref[:, 3 * cw + l * lanes:3 * cw + (l + 1) * lanes] = (dgl * sb).astype(BF16)
            dp_ref[:, 4 * cw + l * lanes:4 * cw + (l + 1) * lanes] = (dgl * gav * sb * (1.0 - sb)).astype(BF16)
        dp_ref[:, 6 * cw:7 * cw] = dg2_ref[...]

        @pl.when(last)
        def _():
            for k in range(SCONV_K):
                dsw_ref[k:k + 1, :] = jnp.sum(sw_acc[k * 8:(k + 1) * 8, :], axis=0, keepdims=True)
            for k in range(CONF_K):
                ddw_ref[k:k + 1, :] = jnp.sum(dw_acc[k * 8:(k + 1) * 8, :], axis=0, keepdims=True)
            ddb_ref[...] = jnp.sum(db_acc[...], axis=0, keepdims=True)

    def main(c):
        return pl.BlockSpec((tm, cw), lambda i: (i, c))

    def before(c):
        return pl.BlockSpec((HALO, cw), lambda i: (hb(i), c))

    def after(c):
        return pl.BlockSpec((HALO, cw), lambda i: (ha(i), c))

    def vec(r):
        return pl.BlockSpec((r, cw), lambda i: (0, 0))

    return pl.pallas_call(
        body, name=name, grid=(n,),
        in_specs=[main(0), after(0), main(5), after(5), main(1), after(1), main(0), before(0), main(2), before(2),
                  main(0), after(0), main(3), before(3), main(4), before(4), main(0), vec(SCONV_K), vec(CONF_K)],
        out_specs=[pl.BlockSpec((tm, 7 * cw), lambda i: (i, 0)), vec(SCONV_K), vec(CONF_K), vec(1)],
        out_shape=[jax.ShapeDtypeStruct((s, 7 * cw), BF16), jax.ShapeDtypeStruct((SCONV_K, cw), F32),
                   jax.ShapeDtypeStruct((CONF_K, cw), F32), jax.ShapeDtypeStruct((1, cw), F32)],
        scratch_shapes=[pltpu.VMEM((8 * SCONV_K, cw), F32), pltpu.VMEM((8 * CONF_K, cw), F32),
                        pltpu.VMEM((8, cw), F32)],
        compiler_params=_params("arbitrary"),
    )(dy, dy, p, p, p, p, p, p, p, p, ddc, ddc, p, p, p, p, dg2, sconv_w, dconv_w)


_ANY = pl.BlockSpec(memory_space=pl.ANY)


def _place():
    return lax.axis_index("x"), lax.axis_index("y"), lax.axis_index("c")


def all_gather(arrs, name, deps=()):
    n = len(arrs)

    def body(*refs):
        ins, outs = refs[:n], refs[n + len(deps):2 * n + len(deps)]
        send_sems, recv_sems, local_sems = refs[-3:]
        x, y, c = _place()
        me, sibling = (x, y, c), (x, y, 1 - c)
        chips = [(1 - x, y), (x, 1 - y), (1 - x, 1 - y)]

        def copy(a, k, block, to, src=None):
            px, py, pc = block
            dst = outs[a].at[4 * px + 2 * py + pc]
            return pltpu.make_async_remote_copy(
                src_ref=dst if src is None else src, dst_ref=dst,
                send_sem=send_sems.at[7 * a + k], recv_sem=recv_sems.at[7 * a + k],
                device_id=to, device_id_type=MESH)

        mine = [pltpu.make_async_copy(ins[a], outs[a].at[4 * x + 2 * y + c], local_sems.at[a]) for a in range(n)]
        for cp in mine:
            cp.start()
        first = []
        for a in range(n):
            first.append(copy(a, 0, me, sibling, src=ins[a]))
            first += [copy(a, 1 + j, me, (*chip, c), src=ins[a]) for j, chip in enumerate(chips)]
        for cp in first:
            cp.start()
        passed = []
        for a in range(n):
            for j, chip in enumerate(chips):
                copy(a, 1 + j, (*chip, c), me).wait_recv()
                cp = copy(a, 4 + j, (*chip, c), sibling)
                cp.start()
                passed.append(cp)
        for a in range(n):
            copy(a, 0, sibling, me).wait_recv()
            for j, chip in enumerate(chips):
                copy(a, 4 + j, (*chip, 1 - c), me).wait_recv()
        for cp in first + passed:
            cp.wait_send()
        for cp in mine:
            cp.wait()

    return pl.pallas_call(
        body, name=name,
        out_shape=[jax.ShapeDtypeStruct((N_DEV,) + a.shape, a.dtype) for a in arrs],
        in_specs=[_ANY] * (n + len(deps)), out_specs=[_ANY] * n,
        scratch_shapes=[pltpu.SemaphoreType.DMA((7 * n,)), pltpu.SemaphoreType.DMA((7 * n,)),
                        pltpu.SemaphoreType.DMA((n,))],
    )(*arrs, *deps)


def in_proj_gathered(h, w_own, extras, name, tm=512):
    s, d = h.shape
    n = w_own.shape[1]
    arrs = [w_own] + list(extras)
    na = len(arrs)

    def body(*refs):
        h_ref, ins = refs[0], refs[1:1 + na]
        p_ref, outs = refs[1 + na], refs[2 + na:2 + 2 * na]
        wbuf, obuf, send_sems, recv_sems, load_sem, store_sems, own_sems = refs[2 + 2 * na:]
        x, y, c = _place()
        me, sibling = (x, y, c), (x, y, 1 - c)
        chips = [(1 - x, y), (x, 1 - y), (1 - x, 1 - y)]

        def slot(block):
            return 4 * block[0] + 2 * block[1] + block[2]

        def copy(a, k, block, to, src=None):
            dst = outs[a].at[slot(block)]
            return pltpu.make_async_remote_copy(
                src_ref=dst if src is None else src, dst_ref=dst,
                send_sem=send_sems.at[7 * a + k], recv_sem=recv_sems.at[7 * a + k],
                device_id=to, device_id_type=MESH)

        first = []
        for a in range(na):
            first.append(copy(a, 0, me, sibling, src=ins[a]))
            first += [copy(a, 1 + j, me, (*chip, c), src=ins[a]) for j, chip in enumerate(chips)]
        mine = [pltpu.make_async_copy(ins[a], outs[a].at[slot(me)], own_sems.at[a]) for a in range(na)]
        for cp in first + mine:
            cp.start()

        stores = [None, None]

        def multiply(k, block, w_from):
            b = k % 2
            load = pltpu.make_async_copy(w_from, wbuf.at[b], load_sem)
            load.start()
            if stores[b] is not None:
                stores[b].wait()
            load.wait()

            def chunk(i, carry):
                rows = pl.ds(pl.multiple_of(i * tm, tm), tm)
                obuf[b, rows, :] = jnp.dot(h_ref[rows, :], wbuf[b], preferred_element_type=F32).astype(BF16)
                return carry

            lax.fori_loop(0, s // tm, chunk, 0)
            stores[b] = pltpu.make_async_copy(
                obuf.at[b], p_ref.at[:, pl.ds(pl.multiple_of(slot(block) * n, 128), n)], store_sems.at[b])
            stores[b].start()

        multiply(0, me, ins[0])
        for a in range(na):
            copy(a, 0, sibling, me).wait_recv()
        multiply(1, sibling, outs[0].at[slot(sibling)])
        passed = []
        for j, chip in enumerate(chips):
            for a in range(na):
                copy(a, 1 + j, (*chip, c), me).wait_recv()
                cp = copy(a, 4 + j, (*chip, c), sibling)
                cp.start()
                passed.append(cp)
            multiply(2 + 2 * j, (*chip, c), outs[0].at[slot((*chip, c))])
            for a in range(na):
                copy(a, 4 + j, (*chip, 1 - c), me).wait_recv()
            multiply(3 + 2 * j, (*chip, 1 - c), outs[0].at[slot((*chip, 1 - c))])
        for cp in first + passed:
            cp.wait_send()
        for cp in mine + stores:
            cp.wait()

    vmem = pl.BlockSpec(memory_space=pltpu.VMEM)
    outs = pl.pallas_call(
        body, name=name,
        out_shape=[jax.ShapeDtypeStruct((s, N_DEV * n), BF16)]
        + [jax.ShapeDtypeStruct((N_DEV,) + a.shape, a.dtype) for a in arrs],
        in_specs=[vmem] + [_ANY] * na, out_specs=[_ANY] * (1 + na),
        scratch_shapes=[pltpu.VMEM((2, d, n), BF16), pltpu.VMEM((2, s, n), BF16),
                        pltpu.SemaphoreType.DMA((7 * na,)), pltpu.SemaphoreType.DMA((7 * na,)),
                        pltpu.SemaphoreType.DMA, pltpu.SemaphoreType.DMA((2,)), pltpu.SemaphoreType.DMA((na,))],
        compiler_params=pltpu.CompilerParams(vmem_limit_bytes=VMEM_LIMIT),
    )(h, *arrs)
    return outs[0], outs[1], outs[2:]


_HBM = pl.BlockSpec(memory_space=pltpu.HBM)
_SEM = pl.BlockSpec(memory_space=pltpu.SEMAPHORE)
_DATAFLOW = pltpu.SideEffectType.DATAFLOW_SIDE_EFFECTING


def _peers_per_array(kind):
    return 1 if kind == "sibling" else 3


def _split_copies(kind, srcs, lands, send_sems, recv_sems):
    x, y, c = _place()
    per = _peers_per_array(kind)
    out = []
    for a in range(len(srcs)):
        if kind == "sibling":
            peers = [((x, y, 1 - c), srcs[a].at[:, pl.ds(1 - c, 1)], lands[a], lands[a])]
        else:
            peers = []
            for px, py in [(1 - x, y), (x, 1 - y), (1 - x, 1 - y)]:
                if kind == "gather":
                    views = (srcs[a], lands[a].at[4 * x + 2 * y + c], lands[a].at[4 * px + 2 * py + c])
                else:
                    views = (srcs[a].at[2 * px + py], lands[a].at[2 * x + y], lands[a].at[2 * px + py])
                peers.append(((px, py, c),) + views)
        for j, (peer, src, dst, arrives) in enumerate(peers):
            sems = dict(send_sem=send_sems.at[per * a + j], recv_sem=recv_sems.at[per * a + j],
                        device_id=peer, device_id_type=MESH)
            out.append((pltpu.make_async_remote_copy(src_ref=src, dst_ref=dst, **sems),
                        pltpu.make_async_remote_copy(src_ref=src, dst_ref=arrives, **sems)))
    return out


def split_start(kind, srcs, lands, deps, name):
    n = len(srcs)
    n_sems = _peers_per_array(kind) * n

    def body(*refs):
        send_sems, recv_sems = refs[2 * n + len(deps)], refs[2 * n + len(deps) + 1]
        for copy, _ in _split_copies(kind, refs[:n], refs[n:2 * n], send_sems, recv_sems):
            copy.start()
        token = refs[-1]
        token[...] = jnp.zeros_like(token)

    held = [pltpu.HBM(a.shape, a.dtype) for a in list(srcs) + list(lands)]
    outs = pl.pallas_call(
        body, name=name,
        out_shape=(pltpu.SemaphoreType.DMA((n_sems,)), pltpu.SemaphoreType.DMA((n_sems,)), *held,
                   jax.ShapeDtypeStruct((8, 128), F32)),
        in_specs=[_HBM] * (2 * n) + [_ANY] * len(deps),
        out_specs=(_SEM, _SEM, *([_HBM] * (2 * n)), pl.BlockSpec(memory_space=pltpu.VMEM)),
        input_output_aliases={i: 2 + i for i in range(2 * n)},
        compiler_params=pltpu.CompilerParams(has_side_effects=_DATAFLOW),
    )(*[pltpu.with_memory_space_constraint(a, pltpu.HBM) for a in list(srcs) + list(lands)], *deps)
    return outs[0], outs[1], list(outs[2:2 + n]), list(outs[2 + n:2 + 2 * n]), outs[-1]


def split_wait(kind, send_sems, recv_sems, srcs, lands, afters, name):
    n = len(srcs)

    def body(*refs):
        for _, arrival in _split_copies(kind, refs[:n], refs[n:2 * n], refs[2 * n], refs[2 * n + 1]):
            arrival.wait_send()
            arrival.wait_recv()

    outs = pl.pallas_call(
        body, name=name,
        out_shape=[pltpu.HBM(a.shape, a.dtype) for a in list(srcs) + list(lands)],
        in_specs=[_HBM] * (2 * n) + [_SEM, _SEM] + [_ANY] * len(afters),
        out_specs=[_HBM] * (2 * n),
        input_output_aliases={i: i for i in range(2 * n)},
        compiler_params=pltpu.CompilerParams(has_side_effects=_DATAFLOW),
    )(*srcs, *lands, send_sems, recv_sems, *afters)
    return list(outs[:n]), list(outs[n:])


def place_block(land, block, dev, name):
    r, c = block.shape
    tr = min(r, 512)

    def body(dev_ref, land_ref, b_ref, o_ref):
        del dev_ref, land_ref
        o_ref[...] = b_ref[...]

    return pl.pallas_call(
        body, name=name,
        grid_spec=pltpu.PrefetchScalarGridSpec(
            num_scalar_prefetch=1, grid=(r // tr,),
            in_specs=[_ANY, pl.BlockSpec((tr, c), lambda i, dev_ref: (i, 0))],
            out_specs=pl.BlockSpec((None, tr, c), lambda i, dev_ref: (dev_ref[0], i, 0))),
        out_shape=jax.ShapeDtypeStruct(land.shape, land.dtype),
        input_output_aliases={1: 0},
        compiler_params=_params("parallel"),
    )(dev, land, block)


def gather_finish(lands, name):
    n = len(lands)

    def body(*refs):
        outs = refs[n:2 * n]
        send_sems, recv_sems = refs[2 * n:]
        x, y, c = _place()
        cps = [pltpu.make_async_remote_copy(
            src_ref=outs[a].at[:, pl.ds(c, 1)], dst_ref=outs[a].at[:, pl.ds(c, 1)],
            send_sem=send_sems.at[a], recv_sem=recv_sems.at[a],
            device_id=(x, y, 1 - c), device_id_type=MESH) for a in range(n)]
        for cp in cps:
            cp.start()
        for cp in cps:
            cp.wait()

    return pl.pallas_call(
        body, name=name,
        out_shape=[jax.ShapeDtypeStruct(a.shape, a.dtype) for a in lands],
        in_specs=[_ANY] * n, out_specs=[_ANY] * n,
        input_output_aliases={i: i for i in range(n)},
        scratch_shapes=[pltpu.SemaphoreType.DMA((n,)), pltpu.SemaphoreType.DMA((n,))],
    )(*lands)


def pair_add(own, recv, core, name):
    _, _, r, c = own.shape
    tr = min(r, 512)

    def body(core_ref, own_ref, recv_ref, o_ref):
        del core_ref
        o_ref[...] = (own_ref[...].astype(F32) + recv_ref[...].astype(F32)).astype(BF16)

    return pl.pallas_call(
        body, name=name,
        grid_spec=pltpu.PrefetchScalarGridSpec(
            num_scalar_prefetch=1, grid=(4, r // tr),
            in_specs=[pl.BlockSpec((None, None, tr, c), lambda k, i, core_ref: (k, core_ref[0], i, 0)),
                      pl.BlockSpec((None, None, tr, c), lambda k, i, core_ref: (k, 0, i, 0))],
            out_specs=pl.BlockSpec((None, tr, c), lambda k, i, core_ref: (k, i, 0))),
        out_shape=jax.ShapeDtypeStruct((4, r, c), BF16),
        compiler_params=_params("parallel", "parallel"),
    )(core, own, recv)


def _adamw_math(w, g, m, v):
    m2 = ADAM_B1 * m + (1.0 - ADAM_B1) * g
    v2 = ADAM_B2 * v + (1.0 - ADAM_B2) * (g * g)
    m_hat = m2 / (1.0 - ADAM_B1 ** ADAM_STEP)
    v_hat = v2 / (1.0 - ADAM_B2 ** ADAM_STEP)
    delta = -ADAM_LR * (m_hat / (jnp.sqrt(v_hat) + ADAM_EPS) + ADAM_WD * w)
    return delta, m2, v2


def adamw_big(w, m, v, own, got, chip, name):
    r, c = w.shape
    tr = min(r, 256)

    def body(chip_ref, w_ref, m_ref, v_ref, p0, p1, p2, p3, g_ref, d_ref, m2_ref, v2_ref):
        del chip_ref
        g = ((p0[...].astype(F32) + p1[...].astype(F32)) + p2[...].astype(F32)) + p3[...].astype(F32)
        delta, m2, v2 = _adamw_math(w_ref[...], g, m_ref[...], v_ref[...])
        g_ref[...] = g
        d_ref[...] = delta
        m2_ref[...] = m2
        v2_ref[...] = v2

    row = pl.BlockSpec((tr, c), lambda i, chip_ref: (i, 0))

    def slab(flip):
        return pl.BlockSpec((None, tr, c), lambda i, chip_ref: (chip_ref[0] ^ flip, i, 0))

    return pl.pallas_call(
        body, name=name,
        grid_spec=pltpu.PrefetchScalarGridSpec(
            num_scalar_prefetch=1, grid=(r // tr,),
            in_specs=[row, row, row, slab(0), slab(1), slab(2), slab(3)],
            out_specs=[row] * 4),
        out_shape=[jax.ShapeDtypeStruct((r, c), F32)] * 4,
        compiler_params=_params("parallel"),
    )(chip, w, m, v, own, got, got, got)


def sum_devices(g8, name):
    def body(g_ref, o_ref):
        tot = g_ref[0]
        for k in range(1, N_DEV):
            tot = tot + g_ref[k]
        o_ref[...] = tot

    return pl.pallas_call(body, name=name, out_shape=jax.ShapeDtypeStruct(g8.shape[1:], F32))(g8)


def adamw_small(ws, gs, ms, vs, name):
    n = len(ws)

    def body(*refs):
        w_r, g_r, m_r, v_r = refs[:n], refs[n:2 * n], refs[2 * n:3 * n], refs[3 * n:4 * n]
        d_o, m_o, v_o = refs[4 * n:5 * n], refs[5 * n:6 * n], refs[6 * n:7 * n]
        for k in range(n):
            delta, m2, v2 = _adamw_math(w_r[k][...], g_r[k][...], m_r[k][...], v_r[k][...])
            d_o[k][...] = delta
            m_o[k][...] = m2
            v_o[k][...] = v2

    shapes = [jax.ShapeDtypeStruct(w.shape, F32) for w in ws]
    outs = pl.pallas_call(body, name=name, out_shape=shapes * 3)(*ws, *gs, *ms, *vs)
    return outs[:n], outs[n:2 * n], outs[2 * n:]


def _rows128(a):
    return a.reshape(-1, 128)


def _pad_rows(a, rows):
    return jnp.pad(a, ((0, rows - a.shape[0]), (0, 0)))


def kernel(x, ln_pre_even, w_in_even, pool_w, pool_scale, w_out_even, ln_post_even, ln_pre_odd, w_in_odd, sconv_w, dconv_w, dconv_b, cnorm_g, cnorm_b, w_out_odd, ln_post_odd, loss_target, m_ln_pre_even, m_w_in_even, m_pool_w, m_pool_scale, m_w_out_even, m_ln_post_even, m_ln_pre_odd, m_w_in_odd, m_sconv_w, m_dconv_w, m_dconv_b, m_cnorm_g, m_cnorm_b, m_w_out_odd, m_ln_post_odd, v_ln_pre_even, v_w_in_even, v_pool_w, v_pool_scale, v_w_out_even, v_ln_post_even, v_ln_pre_odd, v_w_in_odd, v_sconv_w, v_dconv_w, v_dconv_b, v_cnorm_g, v_cnorm_b, v_w_out_odd, v_ln_post_odd):
    xs = x[0]
    tgt = loss_target[0]
    s, d = xs.shape
    half = d // 2
    n_heads = half // HEAD_DIM
    ng = len(POOL_WINDOWS)
    cwp = half // ng
    dev = 4 * lax.axis_index("x") + 2 * lax.axis_index("y") + lax.axis_index("c")
    core = lax.axis_index("c").astype(jnp.int32).reshape(1)

    pr = pool_w.shape[2]
    cl = sconv_w.shape[2]
    small_parts = [(_rows128(ln_pre_odd), 8), (sconv_w[0], 8), (dconv_w[0], 32), (dconv_b, 8),
                   (cnorm_g, 8), (cnorm_b, 8), (_rows128(ln_post_odd), 8)]
    small_local = jnp.concatenate([_pad_rows(a, r) for a, r in small_parts], axis=0)
    h0 = rms_fwd(xs, ln_pre_even, "rms_pre_even")
    p0, g_wie, (g_pw, g_small) = in_proj_gathered(
        h0, w_in_even[0].astype(BF16), [pool_w[0].reshape(ng * pr, cwp).astype(BF16), small_local],
        "ag_in_proj_even")
    comm = _Exchanges(dev, core, d)
    token = comm.start_weights("out_even", [w_out_even[0].astype(BF16)], [p0])
    sb_dep = comm.start_weights("odd", [w_in_odd[0].astype(BF16), w_out_odd[0].astype(BF16)], [token])
    pool_full = g_pw.reshape(N_DEV, ng, pr, cwp).transpose(1, 0, 2, 3).reshape(ng, cwp, cwp)
    nl = ln_pre_odd.shape[1] // 128

    def chan(lo, rows):
        return g_small[:, lo:lo + rows].transpose(1, 0, 2).reshape(rows, N_DEV * cl)

    ln_pre_odd_f = g_small[:, 0:nl].reshape(1, d)
    sconv_f = chan(8, SCONV_K)
    dconv_f = chan(16, CONF_K)
    dconv_b_f = chan(48, 1)
    cnorm_g_f = chan(56, 1)
    cnorm_b_f = chan(64, 1)
    ln_post_odd_f = g_small[:, 72:72 + nl].reshape(1, d)

    loss_blk, grad_x, small_g = _fwd_bwd(
        xs, tgt, ln_pre_even, h0, p0, g_wie, pool_full, pool_scale, ln_post_even, ln_pre_odd_f,
        sconv_f, dconv_f, dconv_b_f, cnorm_g_f, cnorm_b_f, ln_post_odd_f, comm, sb_dep)
    small_w = [ln_pre_even, pool_scale, ln_post_even, ln_pre_odd, sconv_w[0], dconv_w[0], dconv_b, cnorm_g, cnorm_b, ln_post_odd]
    small_m = [m_ln_pre_even, m_pool_scale, m_ln_post_even, m_ln_pre_odd, m_sconv_w[0], m_dconv_w[0], m_dconv_b, m_cnorm_g, m_cnorm_b, m_ln_post_odd]
    small_v = [v_ln_pre_even, v_pool_scale, v_ln_post_even, v_ln_pre_odd, v_sconv_w[0], v_dconv_w[0], v_dconv_b, v_cnorm_g, v_cnorm_b, v_ln_post_odd]
    big = {"w_in_even": (w_in_even, m_w_in_even, v_w_in_even), "pool_w": (pool_w, m_pool_w, v_pool_w),
           "w_out_even": (w_out_even, m_w_out_even, v_w_out_even), "w_in_odd": (w_in_odd, m_w_in_odd, v_w_in_odd),
           "w_out_odd": (w_out_odd, m_w_out_odd, v_w_out_odd)}
    upd = comm.finish_updates(big, [grad_x])
    upd.update(comm.finish_updates(big, [grad_x]))
    sg, sd, sm, sv, loss = _update_small(small_g, loss_blk, small_w, small_m, small_v, dev, d, cl,
                                         deps=[upd["w_in_odd"][1], upd["w_out_even"][1]])
    upd.update(comm.finish_updates(big, sd))
    (g_wie_o, d_wie, m_wie, v_wie), (g_pw_o, d_pw, m_pw, v_pw) = upd["w_in_even"], upd["pool_w"]
    (g_woe_o, d_woe, m_woe, v_woe), (g_wio_o, d_wio, m_wio, v_wio) = upd["w_out_even"], upd["w_in_odd"]
    g_woo_o, d_woo, m_woo, v_woo = upd["w_out_odd"]

    def order(small, wie, pw, woe, wio, woo):
        return [small[0], wie, pw, small[1], woe, small[2], small[3], wio, small[4], small[5], small[6],
                small[7], small[8], woo, small[9]]

    grads = order(sg, g_wie_o, g_pw_o, g_woe_o, g_wio_o, g_woo_o)
    deltas = order(sd, d_wie, d_pw, d_woe, d_wio, d_woo)
    new_m = order(sm, m_wie, m_pw, m_woe, m_wio, m_woo)
    new_v = order(sv, v_wie, v_pw, v_woe, v_wio, v_woo)
    return (loss, grad_x[None], *grads, *deltas, *new_m, *new_v)


def _fwd_bwd(xs, tgt, ln_pre_even, h0, p0, g_wie, pool_full, pool_scale, ln_post_even, ln_pre_odd_f,
             sconv_f, dconv_f, dconv_b_f, cnorm_g_f, cnorm_b_f, ln_post_odd_f, comm, sb_dep):
    d = xs.shape[1]
    n_heads = d // 2 // HEAD_DIM
    ng, cwp = pool_full.shape[0], pool_full.shape[1]
    a0, sb_wts = sb_fwd(p0, n_heads, "sb_fwd", dep=sb_dep)
    y0 = even_mix_fwd(a0, p0, pool_full, pool_scale, "even_mix_fwd")
    (w_out_e,) = comm.weights("out_even", after=y0)
    w_out_e = w_out_e.reshape(1, d, d)
    o0 = mm_nn(y0, w_out_e, F32, "out_proj_even", tm=1024, tn=1024)
    x1 = postnorm_fwd(xs, o0, ln_post_even, "post_even")
    g_wio, w_out_o = comm.weights("odd", after=x1)
    w_out_o = w_out_o.reshape(1, d, d)
    h1 = rms_fwd(x1, ln_pre_odd_f, "rms_pre_odd")
    p1 = mm_nn(h1, g_wio, BF16, "in_proj_odd")
    y1, dc = odd_mix_fwd(p1, sconv_f, dconv_f, dconv_b_f, cnorm_g_f, cnorm_b_f, "odd_mix_fwd")
    o1 = mm_nn(y1, w_out_o, F32, "out_proj_odd", tm=1024, tn=1024)
    loss_blk, gx2, do1, dg_post_odd = final_fwd_bwd(x1, o1, ln_post_odd_f, tgt, "post_odd_loss")

    dw_out_o = mm_tn(y1, do1, 1, BF16, "dw_out_odd", tm=1024)
    dy1 = mm_nt(do1, w_out_o, BF16, "dy_odd", tn=1024)
    ddc, dg2, dgam, dbet = odd_bwd_ln(dy1, p1, dc, cnorm_g_f, cnorm_b_f, "odd_bwd_ln")
    dp1, dsconv, ddconv, ddconv_b = odd_bwd_conv(dy1, p1, ddc, dg2, sconv_f, dconv_f, "odd_bwd_conv")
    dw_in_o = mm_tn(h1, dp1, N_DEV, BF16, "dw_in_odd")
    dep = comm.reduce_begin({"w_out_odd": dw_out_o.reshape(N_DEV, d // N_DEV, d), "w_in_odd": dw_in_o}, "odd")
    dh1 = mm_nt(dp1, g_wio, F32, "dh_odd", dep=dep)
    dep = comm.reduce_send(after=dh1)
    gx1, dg_pre_odd = norm_bwd(dh1, x1, ln_pre_odd_f, gx2, F32, "pre_odd_bwd", dep=dep)

    do0, dg_post_even = norm_bwd(gx1, o0, ln_post_even, None, BF16, "post_even_bwd")
    dw_out_e = mm_tn(y0, do0, 1, BF16, "dw_out_even", tm=1024)
    dy0 = mm_nt(do0, w_out_e, BF16, "dy_even", tn=1024)
    da0, du0, dg0, dpool, dpool_scale = even_mix_bwd(dy0, a0, p0, pool_full, pool_scale, "even_mix_bwd")
    pr = cwp // N_DEV
    dpool_slabs = dpool.astype(BF16).reshape(ng, N_DEV, pr, cwp).transpose(1, 0, 2, 3).reshape(N_DEV, ng * pr, cwp)
    dep = comm.reduce_begin({"w_out_even": dw_out_e.reshape(N_DEV, d // N_DEV, d), "pool_w": dpool_slabs}, "even_out")
    dq0, dk0, dv0 = sb_bwd(p0, a0, sb_wts, da0, n_heads, "sb_bwd", dep=dep)
    dep = comm.reduce_send(after=dq0)
    dp0 = jnp.concatenate([dq0, dk0, dv0, du0, dg0], axis=1)
    dw_in_e = mm_tn(h0, dp0, N_DEV, BF16, "dw_in_even", dep=dep)
    comm.reduce_begin({"w_in_even": dw_in_e}, "even_in")
    dep = comm.reduce_send(after=dw_in_e)
    dh0 = mm_nt(dp0, g_wie, F32, "dh_even", dep=dep)
    dep = None
    grad_x, dg_pre_even = norm_bwd(dh0, xs, ln_pre_even, gx1, F32, "pre_even_bwd", dep=dep)
    small_g = [dg_pre_even, dpool_scale, dg_post_even, dg_pre_odd, dsconv, ddconv, ddconv_b, dgam, dbet, dg_post_odd]
    return loss_blk, grad_x, small_g


class _Exchanges:
    def __init__(self, dev, core, d):
        self.dev = dev.astype(jnp.int32).reshape(1)
        self.core = core
        self.chip = (dev // 2).astype(jnp.int32).reshape(1)
        self.d = d
        self.in_flight = {}
        self.to_sibling = None
        self.pending = []

    def start_weights(self, tag, blocks, afters):
        lands = [lax.empty((N_DEV,) + b.shape, b.dtype) for b in blocks]
        send, recv, srcs, lands, token = split_start("gather", blocks, lands, afters, "ag_start_" + tag)
        self.in_flight[tag] = (send, recv, srcs, lands)
        return token

    def weights(self, tag, after):
        send, recv, srcs, lands = self.in_flight.pop(tag)
        srcs, lands = split_wait("gather", send, recv, srcs, lands, [after], "ag_wait_" + tag)
        lands = [place_block(l, b, self.dev, "ag_own_%s_%d" % (tag, k)) for k, (l, b) in enumerate(zip(lands, srcs))]
        full = gather_finish([l.reshape((4, 2) + l.shape[1:]) for l in lands], "ag_finish_" + tag)
        return [f.reshape((N_DEV,) + f.shape[2:]) for f in full]

    def reduce_begin(self, partials, tag):
        names = list(partials)
        arrs = [partials[k].reshape((4, 2) + partials[k].shape[1:]) for k in names]
        lands = [lax.empty((4, 1) + a.shape[2:], a.dtype) for a in arrs]
        send, recv, srcs, lands, token = split_start("sibling", arrs, lands, [], "rs_sibling_start_" + tag)
        self.to_sibling = (tag, names, send, recv, srcs, lands)
        return token

    def reduce_send(self, after):
        tag, names, send, recv, srcs, lands = self.to_sibling
        srcs, lands = split_wait("sibling", send, recv, srcs, lands, [after], "rs_sibling_wait_" + tag)
        sums = [pair_add(o, r, self.core, "rs_pair_add_" + k) for k, o, r in zip(names, srcs, lands)]
        zones = [lax.empty(a.shape, a.dtype) for a in sums]
        send, recv, srcs, zones, token = split_start("scatter", sums, zones, [], "rs_start_" + tag)
        self.pending.append((tag, names, send, recv, srcs, zones))
        return token

    def finish_updates(self, big, afters):
        tag, names, send, recv, srcs, lands = self.pending.pop(0)
        srcs, lands = split_wait("scatter", send, recv, srcs, lands, afters, "rs_wait_" + tag)
        out = {}
        for name, own, got in zip(names, srcs, lands):
            w, m, v = big[name]
            shp = own.shape[1:]
            outs = adamw_big(w.reshape(shp), m.reshape(shp), v.reshape(shp), own, got, self.chip, "adamw_" + name)
            out[name] = [o.reshape(w.shape) for o in outs]
        return out


def _update_small(small_g, loss_blk, small_w, small_m, small_v, dev, d, cl, deps):
    packed = jnp.concatenate([_rows128(g) for g in small_g] + [loss_blk], axis=0)
    (g8,) = all_gather([packed], "ag_small_grads", deps)
    tot = sum_devices(g8, "sum_small_grads")
    loss = tot[packed.shape[0] - 8, 0]
    full_g = []
    lo = 0
    for g in small_g:
        rows = g.size // 128
        full_g.append(tot[lo:lo + rows].reshape(g.shape))
        lo += rows

    def mine(g, width):
        return lax.dynamic_slice_in_dim(g, dev * width, width, axis=g.ndim - 1)

    fg = full_g
    small_gl = [fg[0], fg[1], fg[2], mine(fg[3], d // N_DEV), mine(fg[4], cl), mine(fg[5], cl), mine(fg[6], cl),
                mine(fg[7], cl), mine(fg[8], cl), mine(fg[9], d // N_DEV)]
    sd, sm, sv = adamw_small(small_w, small_gl, small_m, small_v, "adamw_small")

    def like(k, a):
        return a[None] if k in (4, 5) else a

    sg = [like(k, a) for k, a in enumerate(small_gl)]
    sd = [like(k, a) for k, a in enumerate(sd)]
    sm = [like(k, a) for k, a in enumerate(sm)]
    sv = [like(k, a) for k, a in enumerate(sv)]
    return sg, sd, sm, sv, loss
```

```python
import functools
import math

import jax
import jax.numpy as jnp
from jax import lax
from jax.experimental import pallas as pl
from jax.experimental.pallas import tpu as pltpu

F32 = jnp.float32
BF16 = jnp.bfloat16
EPS = 1e-6
HEAD_DIM = 128
POOL_WINDOWS = (2, 4, 8, 16)
SCONV_K = 3
CONF_K = 31
HALO = 32
N_DEV = 8
VMEM_LIMIT = 56 * 1024 * 1024
MESH = pl.DeviceIdType.MESH

ADAM_LR = 0.001
ADAM_B1 = 0.9
ADAM_B2 = 0.999
ADAM_EPS = 1e-08
ADAM_WD = 0.01
ADAM_STEP = 10


def _params(*sem):
    return pltpu.CompilerParams(dimension_semantics=sem, vmem_limit_bytes=VMEM_LIMIT)


def _sigmoid(v):
    return 1.0 / (1.0 + jnp.exp(-v))


def _silu(v):
    return v * _sigmoid(v)


def _silu_and_grad(v):
    s = _sigmoid(v)
    return v * s, s * (1.0 + v * (1.0 - s))


def _rowsum8(v):
    r, c = v.shape
    return jnp.sum(v.reshape(r // 8, 8, c), axis=0)


def _tap_before(xx, i, rows):
    if i == 0:
        return xx[HALO:HALO + rows]
    return pltpu.roll(xx, i, 0)[HALO:HALO + rows]


def _tap_after(xx, i, rows):
    if i == 0:
        return xx[0:rows]
    return pltpu.roll(xx, xx.shape[0] - i, 0)[0:rows]


def rms_fwd(x, g, name, tm=256):
    s, d = x.shape

    def body(x_ref, g_ref, h_ref):
        xv = x_ref[...]
        r = lax.rsqrt(jnp.mean(xv * xv, axis=-1, keepdims=True) + EPS)
        h_ref[...] = (xv * r * g_ref[...]).astype(BF16)

    return pl.pallas_call(
        body, name=name, grid=(s // tm,),
        in_specs=[pl.BlockSpec((tm, d), lambda i: (i, 0)), pl.BlockSpec((1, d), lambda i: (0, 0))],
        out_specs=pl.BlockSpec((tm, d), lambda i: (i, 0)),
        out_shape=jax.ShapeDtypeStruct((s, d), BF16),
        compiler_params=_params("parallel"),
    )(x, g)


def postnorm_fwd(x, o, g, g_next, name, tm=256):
    s, d = x.shape

    def body(x_ref, o_ref, g_ref, gn_ref, y_ref, h_ref):
        ov = o_ref[...]
        r = lax.rsqrt(jnp.mean(ov * ov, axis=-1, keepdims=True) + EPS)
        y = x_ref[...] + ov * r * g_ref[...]
        y_ref[...] = y
        r2 = lax.rsqrt(jnp.mean(y * y, axis=-1, keepdims=True) + EPS)
        h_ref[...] = (y * r2 * gn_ref[...]).astype(BF16)

    row = pl.BlockSpec((tm, d), lambda i: (i, 0))
    vec = pl.BlockSpec((1, d), lambda i: (0, 0))
    return pl.pallas_call(
        body, name=name, grid=(s // tm,),
        in_specs=[row, row, vec, vec], out_specs=[row, row],
        out_shape=[jax.ShapeDtypeStruct((s, d), F32), jax.ShapeDtypeStruct((s, d), BF16)],
        compiler_params=_params("parallel"),
    )(x, o, g, g_next)


def final_fwd_bwd(x1, o, g, target, name, tm=256):
    s, d = x1.shape
    n = s // tm

    def body(x_ref, o_ref, g_ref, t_ref, loss_ref, gx_ref, do_ref, dg_ref, lacc, gacc):
        i = pl.program_id(0)

        @pl.when(i == 0)
        def _():
            lacc[...] = jnp.zeros_like(lacc)
            gacc[...] = jnp.zeros_like(gacc)

        ov = o_ref[...]
        gv = g_ref[...]
        r = lax.rsqrt(jnp.mean(ov * ov, axis=-1, keepdims=True) + EPS)
        oh = ov * r
        diff = x_ref[...] + oh * gv - t_ref[...]
        lacc[...] += _rowsum8(diff * diff)
        gx = diff * (1.0 / d)
        gx_ref[...] = gx
        gacc[...] += _rowsum8(gx * oh)
        dn = gx * gv
        do_ref[...] = (r * (dn - oh * jnp.mean(dn * oh, axis=-1, keepdims=True))).astype(BF16)

        @pl.when(i == n - 1)
        def _():
            tot = jnp.sum(jnp.sum(lacc[...], axis=0, keepdims=True), axis=1, keepdims=True)
            loss_ref[...] = jnp.broadcast_to(tot * (0.5 / d), loss_ref.shape)
            dg_ref[...] = jnp.sum(gacc[...], axis=0, keepdims=True)

    row = pl.BlockSpec((tm, d), lambda i: (i, 0))
    vec = pl.BlockSpec((1, d), lambda i: (0, 0))
    return pl.pallas_call(
        body, name=name, grid=(n,),
        in_specs=[row, row, vec, row],
        out_specs=[pl.BlockSpec((8, 128), lambda i: (0, 0)), row, row, vec],
        out_shape=[jax.ShapeDtypeStruct((8, 128), F32), jax.ShapeDtypeStruct((s, d), F32),
                   jax.ShapeDtypeStruct((s, d), BF16), jax.ShapeDtypeStruct((1, d), F32)],
        scratch_shapes=[pltpu.VMEM((8, d), F32), pltpu.VMEM((8, d), F32)],
        compiler_params=_params("arbitrary"),
    )(x1, o, g, target)


def _rms_bwd_rows(dyv, xv, gv):
    r = lax.rsqrt(jnp.mean(xv * xv, axis=-1, keepdims=True) + EPS)
    xh = xv * r
    dn = dyv * gv
    return r * (dn - xh * jnp.mean(dn * xh, axis=-1, keepdims=True)), _rowsum8(dyv * xh)


def norm_bwd(dy, inp, g, resid, name, inp2=None, g2=None, tm=256, dep=None):
    s, d = inp.shape
    n = s // tm
    chain = inp2 is not None

    def body(*refs):
        dy_ref, x_ref, g_ref, r_ref = refs[:4]
        outs = refs[-6:] if chain else refs[-3:]
        i = pl.program_id(0)

        @pl.when(i == 0)
        def _():
            for acc in outs[-2:] if chain else outs[-1:]:
                acc[...] = jnp.zeros_like(acc)

        if chain:
            x2_ref, g2_ref = refs[4:6]
            dx_ref, dg_ref, dx2_ref, dg2_ref, gacc, gacc2 = outs
        else:
            dx_ref, dg_ref, gacc = outs
        dx, dg_rows = _rms_bwd_rows(dy_ref[...].astype(F32), x_ref[...], g_ref[...])
        dx = dx + r_ref[...]
        dx_ref[...] = dx
        gacc[...] += dg_rows
        if chain:
            dx2, dg2_rows = _rms_bwd_rows(dx, x2_ref[...], g2_ref[...])
            dx2_ref[...] = dx2.astype(BF16)
            gacc2[...] += dg2_rows

        @pl.when(i == n - 1)
        def _():
            dg_ref[...] = jnp.sum(gacc[...], axis=0, keepdims=True)
            if chain:
                dg2_ref[...] = jnp.sum(gacc2[...], axis=0, keepdims=True)

    row = pl.BlockSpec((tm, d), lambda i: (i, 0))
    vec = pl.BlockSpec((1, d), lambda i: (0, 0))
    dep_args, dep_specs = _after(dep)
    extra = [inp2, g2] if chain else []
    return pl.pallas_call(
        body, name=name, grid=(n,),
        in_specs=[row, row, vec, row] + ([row, vec] if chain else []) + dep_specs,
        out_specs=[row, vec] * (2 if chain else 1),
        out_shape=[jax.ShapeDtypeStruct((s, d), F32), jax.ShapeDtypeStruct((1, d), F32)]
        + ([jax.ShapeDtypeStruct((s, d), BF16), jax.ShapeDtypeStruct((1, d), F32)] if chain else []),
        scratch_shapes=[pltpu.VMEM((8, d), F32)] * (2 if chain else 1),
        compiler_params=_params("arbitrary"),
    )(dy, inp, g, resid, *extra, *dep_args)


def _after(dep):
    if dep is None:
        return [], []
    return [dep], [pl.BlockSpec((8, 128), lambda *_: (0, 0))]


def mm_nn(a, w, out_dtype, name, tm=2048, tn=None, dep=None):
    m, k = a.shape
    tm = min(tm, m)
    ns, _, n = w.shape
    tn = n if tn is None else tn
    nj = n // tn
    dep_args, dep_specs = _after(dep)

    def body(a_ref, w_ref, *rest):
        o_ref = rest[-1]
        o_ref[...] = jnp.dot(a_ref[...], w_ref[0], preferred_element_type=F32).astype(out_dtype)

    return pl.pallas_call(
        body, name=name, grid=(ns, nj, m // tm),
        in_specs=[pl.BlockSpec((tm, k), lambda s, j, i: (i, 0)),
                  pl.BlockSpec((1, k, tn), lambda s, j, i: (s, 0, j))] + dep_specs,
        out_specs=pl.BlockSpec((tm, tn), lambda s, j, i: (i, s * nj + j)),
        out_shape=jax.ShapeDtypeStruct((m, ns * n), out_dtype),
        compiler_params=_params("parallel", "parallel", "parallel"),
    )(a, w, *dep_args)


def mm_nt(a, w, out_dtype, name, tm=1024, tn=None, dep=None):
    m = a.shape[0]
    tm = min(tm, m)
    ns, k, n = w.shape
    tn = n if tn is None else tn
    nj = n // tn
    steps = ns * nj
    dep_args, dep_specs = _after(dep)

    def body(a_ref, w_ref, *rest):
        o_ref, acc = rest[-2:]
        r = pl.program_id(1)

        @pl.when(r == 0)
        def _():
            acc[...] = jnp.zeros_like(acc)

        acc[...] += lax.dot_general(a_ref[...], w_ref[0], (((1,), (1,)), ((), ())),
                                    preferred_element_type=F32)

        @pl.when(r == steps - 1)
        def _():
            o_ref[...] = acc[...].astype(out_dtype)

    return pl.pallas_call(
        body, name=name, grid=(m // tm, steps),
        in_specs=[pl.BlockSpec((tm, tn), lambda i, r: (i, r)),
                  pl.BlockSpec((1, k, tn), lambda i, r: (r // nj, 0, r % nj))] + dep_specs,
        out_specs=pl.BlockSpec((tm, k), lambda i, r: (i, 0)),
        out_shape=jax.ShapeDtypeStruct((m, k), out_dtype),
        scratch_shapes=[pltpu.VMEM((tm, k), F32)],
        compiler_params=_params("parallel", "arbitrary"),
    )(a, w, *dep_args)


def mm_tn(a, b, ns, out_dtype, name, tk=1024, tm=2048, dep=None):
    m, k = a.shape
    tm = min(tm, m)
    n = b.shape[1] // ns
    steps = m // tm
    dep_args, dep_specs = _after(dep)

    def body(a_ref, b_ref, *rest):
        o_ref, acc = rest[-2:]
        r = pl.program_id(2)

        @pl.when(r == 0)
        def _():
            acc[...] = jnp.zeros_like(acc)

        acc[...] += lax.dot_general(a_ref[...], b_ref[...], (((0,), (0,)), ((), ())),
                                    preferred_element_type=F32)

        @pl.when(r == steps - 1)
        def _():
            o_ref[0] = acc[...].astype(out_dtype)

    return pl.pallas_call(
        body, name=name, grid=(ns, k // tk, steps),
        in_specs=[pl.BlockSpec((tm, tk), lambda s, j, r: (r, j)),
                  pl.BlockSpec((tm, n), lambda s, j, r: (r, s))] + dep_specs,
        out_specs=pl.BlockSpec((1, tk, n), lambda s, j, r: (s, j, 0)),
        out_shape=jax.ShapeDtypeStruct((ns, k, n), out_dtype),
        scratch_shapes=[pltpu.VMEM((tk, n), F32)],
        compiler_params=_params("parallel", "parallel", "arbitrary"),
    )(a, b, *dep_args)


SB_BLK = 128


LOG2E = 1.0 / math.log(2.0)


def _split_dot(v, tri):
    hi = pltpu.bitcast(pltpu.bitcast(v, jnp.uint32) & jnp.uint32(0xFFFF0000), F32)
    lo = (v - hi).astype(BF16)
    return (jnp.dot(hi.astype(BF16), tri, preferred_element_type=F32)
            + jnp.dot(lo, tri, preferred_element_type=F32))


def _sb_scores(z2, lim, dcol, tri_ex, masked):
    sp = jnp.log2(1.0 + jnp.exp2(-jnp.abs(z2)))
    lb = jnp.minimum(z2, 0.0) - sp
    l1m = lb - z2
    mask = None
    if masked:
        mask = dcol < lim
        l1m = jnp.where(mask, l1m, 0.0)
    return mask, lb, l1m, _split_dot(l1m, tri_ex)


def _sb_consts():
    row = lax.broadcasted_iota(jnp.int32, (SB_BLK, SB_BLK), 0)
    col = lax.broadcasted_iota(jnp.int32, (SB_BLK, SB_BLK), 1)
    tri_ex = jnp.where(row > col, 1.0, 0.0).astype(BF16)
    tri_in = jnp.where(row >= col, 1.0, 0.0).astype(BF16)
    return col - row, tri_ex, tri_in


def sb_fwd(p, n_heads, name, tq=256, nsub=4, dep=None):
    s = p.shape[0]
    h_n = n_heads
    b = SB_BLK
    nqs = tq // b
    tk = nsub * b
    scale = 1.0 / math.sqrt(HEAD_DIM)

    dep_args, dep_specs = _after(dep)

    def body(q_ref, k_ref, v_ref, *rest):
        o_ref, w_ref = rest[-2:]
        qi = pl.program_id(1)
        dcol, tri_ex, _ = _sb_consts()
        qv = [q_ref[qs * b:(qs + 1) * b, :] for qs in range(nqs)]
        n_groups = ((qi + 1) * nqs - 1) // nsub + 1

        def step(it, carry, masked):
            c1s, accs = carry
            g = n_groups - 1 - it
            off = pl.multiple_of(g * tk, tk)
            kg = k_ref[pl.ds(off, tk), :]
            vg = v_ref[pl.ds(off, tk), :]
            new_c1, new_acc = [], []
            for qs in range(nqs):
                qb = qi * nqs + qs
                z2 = lax.dot_general(qv[qs], kg, (((1,), (1,)), ((), ())),
                                     preferred_element_type=F32) * (scale * LOG2E)
                blocks = [_sb_scores(z2[:, j * b:(j + 1) * b], (qb - (g * nsub + j)) * b, dcol, tri_ex, masked)
                          for j in range(nsub)]
                run = c1s[qs]
                ws = [None] * nsub
                for j in reversed(range(nsub)):
                    mask, lb, l1m, ls_loc = blocks[j]
                    wj = jnp.exp2(lb + ls_loc + run)
                    ws[j] = (jnp.where(mask, wj, 0.0) if masked else wj).astype(BF16)
                    run = run + jnp.sum(l1m, axis=1, keepdims=True)
                w = jnp.concatenate(ws, axis=1)
                w_ref[0, g, qs * b:(qs + 1) * b, :] = w
                new_acc.append(accs[qs] + jnp.dot(w, vg, preferred_element_type=F32))
                new_c1.append(run)
            return tuple(new_c1), tuple(new_acc)

        init = (tuple(jnp.zeros((b, 1), F32) for _ in range(nqs)),
                tuple(jnp.zeros((b, HEAD_DIM), F32) for _ in range(nqs)))
        assert nqs == 2 and nsub % 2 == 0
        first = step(0, init, True)
        _, accs = lax.fori_loop(1, n_groups, functools.partial(step, masked=False), first)
        for qs in range(nqs):
            o_ref[qs * b:(qs + 1) * b, :] = accs[qs]

    return pl.pallas_call(
        body, name=name, grid=(h_n, s // tq),
        in_specs=[pl.BlockSpec((tq, HEAD_DIM), lambda h, i: (i, h)),
                  pl.BlockSpec((s, HEAD_DIM), lambda h, i: (0, h_n + h)),
                  pl.BlockSpec((s, HEAD_DIM), lambda h, i: (0, 2 * h_n + h))] + dep_specs,
        out_specs=[pl.BlockSpec((tq, HEAD_DIM), lambda h, i: (i, h)),
                   pl.BlockSpec((1, s // tk, tq, tk), lambda h, i: (h, 0, i, 0))],
        out_shape=[jax.ShapeDtypeStruct((s, h_n * HEAD_DIM), F32),
                   jax.ShapeDtypeStruct((h_n, s // tk, s, tk), BF16)],
        compiler_params=_params("parallel", "arbitrary"),
    )(p, p, p, *dep_args)


def sb_bwd(p, a, wts, da, n_heads, name, tq=256, dep=None):
    s = p.shape[0]
    h_n = n_heads
    nq = s // tq
    b = SB_BLK
    nqs = tq // b
    tk = wts.shape[3]
    nsub = tk // b
    scale = 1.0 / math.sqrt(HEAD_DIM)
    dep_args, dep_specs = _after(dep)

    def body(q_ref, k_ref, v_ref, a_ref, da_ref, w_ref, *rest):
        dq_ref, dk_ref, dv_ref, dk_acc, dv_acc = rest[-5:]
        qi = pl.program_id(1)

        @pl.when(qi == 0)
        def _():
            dk_acc[...] = jnp.zeros_like(dk_acc)
            dv_acc[...] = jnp.zeros_like(dv_acc)

        dcol, _, tri_in = _sb_consts()
        q_all = q_ref[...]
        do_all = da_ref[...]
        qv = [q_ref[qs * b:(qs + 1) * b, :] for qs in range(nqs)]
        dov = [da_ref[qs * b:(qs + 1) * b, :] for qs in range(nqs)]
        tots = [jnp.sum(dov[qs].astype(F32) * a_ref[qs * b:(qs + 1) * b, :], axis=1, keepdims=True)
                for qs in range(nqs)]
        n_groups = ((qi + 1) * nqs - 1) // nsub + 1

        def step(it, carry, masked):
            c2s, dqs = carry
            g = n_groups - 1 - it
            off = pl.multiple_of(g * tk, tk)
            kg = k_ref[pl.ds(off, tk), :]
            vg = v_ref[pl.ds(off, tk), :]
            w_all = w_ref[0, g]
            new_c2, new_dq, dz_rows = [], [], []
            for qs in range(nqs):
                qb = qi * nqs + qs
                z2 = lax.dot_general(qv[qs], kg, (((1,), (1,)), ((), ())),
                                     preferred_element_type=F32) * (-scale * LOG2E)
                dw = lax.dot_general(dov[qs], vg, (((1,), (1,)), ((), ())), preferred_element_type=F32)
                beta = 1.0 / (1.0 + jnp.exp2(z2))
                e = dw * w_all[qs * b:(qs + 1) * b, :].astype(F32)
                run2 = c2s[qs]
                dzs = [None] * nsub
                for j in reversed(range(nsub)):
                    cols = slice(j * b, (j + 1) * b)
                    later = _split_dot(e[:, cols], tri_in) + run2
                    bj = beta[:, cols]
                    dz = (e[:, cols] * (1.0 - bj) - bj * (tots[qs] - later)) * scale
                    if masked:
                        dz = jnp.where(dcol < (qb - (g * nsub + j)) * b, dz, 0.0)
                    dzs[j] = dz.astype(BF16)
                    run2 = run2 + jnp.sum(e[:, cols], axis=1, keepdims=True)
                dzq = jnp.concatenate(dzs, axis=1)
                new_dq.append(dqs[qs] + jnp.dot(dzq, kg, preferred_element_type=F32))
                new_c2.append(run2)
                dz_rows.append(dzq)
            dz_all = jnp.concatenate(dz_rows, axis=0)
            dk_acc[pl.ds(off, tk), :] += lax.dot_general(dz_all, q_all, (((0,), (0,)), ((), ())),
                                                         preferred_element_type=F32)
            dv_acc[pl.ds(off, tk), :] += lax.dot_general(w_all, do_all, (((0,), (0,)), ((), ())),
                                                         preferred_element_type=F32)
            return tuple(new_c2), tuple(new_dq)

        zeros = tuple(jnp.zeros((b, 1), F32) for _ in range(nqs))
        assert nqs == 2 and nsub % 2 == 0
        first = step(0, (zeros, tuple(jnp.zeros((b, HEAD_DIM), F32) for _ in range(nqs))), True)
        _, dqs = lax.fori_loop(1, n_groups, functools.partial(step, masked=False), first)
        for qs in range(nqs):
            dq_ref[qs * b:(qs + 1) * b, :] = dqs[qs].astype(BF16)

        @pl.when(qi == nq - 1)
        def _():
            dk_ref[...] = dk_acc[...].astype(BF16)
            dv_ref[...] = dv_acc[...].astype(BF16)

    blk = pl.BlockSpec((tq, HEAD_DIM), lambda h, i: (i, h))
    full = pl.BlockSpec((s, HEAD_DIM), lambda h, i: (0, h))
    return pl.pallas_call(
        body, name=name, grid=(h_n, nq),
        in_specs=[blk, pl.BlockSpec((s, HEAD_DIM), lambda h, i: (0, h_n + h)),
                  pl.BlockSpec((s, HEAD_DIM), lambda h, i: (0, 2 * h_n + h)), blk, blk,
                  pl.BlockSpec((1, s // tk, tq, tk), lambda h, i: (h, 0, i, 0))] + dep_specs,
        out_specs=[blk, full, full],
        out_shape=[jax.ShapeDtypeStruct((s, h_n * HEAD_DIM), BF16)] * 3,
        scratch_shapes=[pltpu.VMEM((s, HEAD_DIM), F32), pltpu.VMEM((s, HEAD_DIM), F32)],
        compiler_params=_params("parallel", "arbitrary"),
    )(p, p, p, a, da, wts, *dep_args)


def _pool_window(xx, win, r0, rc):
    cur = xx[HALO:HALO + rc]
    ws = cur
    for i in range(1, win):
        ws = ws + _tap_before(xx, i, rc)
    t_idx = r0 + lax.broadcasted_iota(jnp.int32, (rc, 1), 0)
    inv = 1.0 / jnp.minimum(win, t_idx + 1).astype(F32)
    return ws * inv - cur, inv


def even_mix_fwd(a, p, pool_w, pool_scale, name, rc=64):
    s = p.shape[0]
    ng = len(POOL_WINDOWS)
    cw = pool_w.shape[1]
    n_chunks = s // rc

    def body(a_ref, u_ref, g_ref, w_ref, sc_ref, y_ref, upad):
        j = pl.program_id(0)

        @pl.when(j < ng)
        def _():
            def chunk(ci, carry):
                rows = pl.ds(pl.multiple_of(ci * rc, rc), rc)
                y_ref[rows, :] = (a_ref[rows, :] * _silu(g_ref[rows, :].astype(F32))).astype(BF16)
                return carry

            lax.fori_loop(0, n_chunks, chunk, 0)

        for gi, win in enumerate(POOL_WINDOWS):
            @pl.when(j == ng + gi)
            def _(win=win):
                upad[0:HALO, :] = jnp.zeros((HALO, cw), F32)

                def fill(ci, carry):
                    r0 = pl.multiple_of(ci * rc, rc)
                    upad[pl.ds(pl.multiple_of(r0 + HALO, HALO), rc), :] = u_ref[pl.ds(r0, rc), :].astype(F32)
                    return carry

                lax.fori_loop(0, n_chunks, fill, 0)

                def chunk(ci, carry):
                    r0 = pl.multiple_of(ci * rc, rc)
                    rows = pl.ds(r0, rc)
                    pooled, _ = _pool_window(upad[pl.ds(r0, HALO + rc), :], win, r0, rc)
                    t = jnp.dot(pooled.astype(BF16), w_ref[0], preferred_element_type=F32)
                    y_ref[rows, :] = (t * sc_ref[...] * _silu(g_ref[rows, :].astype(F32))).astype(BF16)
                    return carry

                lax.fori_loop(0, n_chunks, chunk, 0)

    grp = lambda j: jnp.maximum(j - ng, 0)
    return pl.pallas_call(
        body, name=name, grid=(2 * ng,),
        in_specs=[pl.BlockSpec((s, cw), lambda j: (0, jnp.minimum(j, ng - 1))),
                  pl.BlockSpec((s, cw), lambda j: (0, 3 * ng + grp(j))),
                  pl.BlockSpec((s, cw), lambda j: (0, 4 * ng + j)),
                  pl.BlockSpec((1, cw, cw), lambda j: (grp(j), 0, 0)),
                  pl.BlockSpec((1, cw), lambda j: (0, grp(j)))],
        out_specs=pl.BlockSpec((s, cw), lambda j: (0, j)),
        out_shape=jax.ShapeDtypeStruct((s, 2 * ng * cw), BF16),
        scratch_shapes=[pltpu.VMEM((HALO + s, cw), F32)],
        compiler_params=_params("arbitrary"),
    )(a, p, p, pool_w, pool_scale)


def even_mix_bwd(dy, a, p, pool_w, pool_scale, name, rc=64):
    s = p.shape[0]
    ng = len(POOL_WINDOWS)
    cw = pool_w.shape[1]
    n_chunks = s // rc

    def body(dy_ref, a_ref, u_ref, g_ref, w_ref, sc_ref, da_ref, du_ref, dg_ref, dw_ref, dsc_ref,
             upad, rpad, dpl, dw_acc, dsc_acc):
        j = pl.program_id(0)

        @pl.when(j < ng)
        def _():
            def chunk(ci, carry):
                rows = pl.ds(pl.multiple_of(ci * rc, rc), rc)
                dyv = dy_ref[rows, :].astype(F32)
                sg, dsg = _silu_and_grad(g_ref[rows, :].astype(F32))
                da_ref[rows, :] = (dyv * sg).astype(BF16)
                dg_ref[rows, :] = (dyv * a_ref[rows, :] * dsg).astype(BF16)
                return carry

            lax.fori_loop(0, n_chunks, chunk, 0)

        for gi, win in enumerate(POOL_WINDOWS):
            @pl.when(j == ng + gi)
            def _(win=win):
                upad[0:HALO, :] = jnp.zeros((HALO, cw), F32)
                rpad[s:s + HALO, :] = jnp.zeros((HALO, cw), F32)
                dw_acc[...] = jnp.zeros_like(dw_acc)
                dsc_acc[...] = jnp.zeros_like(dsc_acc)

                def fill(ci, carry):
                    r0 = pl.multiple_of(ci * rc, rc)
                    upad[pl.ds(pl.multiple_of(r0 + HALO, HALO), rc), :] = u_ref[pl.ds(r0, rc), :].astype(F32)
                    return carry

                lax.fori_loop(0, n_chunks, fill, 0)

                def chunk(ci, carry):
                    r0 = pl.multiple_of(ci * rc, rc)
                    rows = pl.ds(r0, rc)
                    pooled, inv = _pool_window(upad[pl.ds(r0, HALO + rc), :], win, r0, rc)
                    pb = pooled.astype(BF16)
                    wv = w_ref[0]
                    t = jnp.dot(pb, wv, preferred_element_type=F32)
                    scv = sc_ref[...]
                    dyv = dy_ref[rows, :].astype(F32)
                    sg, dsg = _silu_and_grad(g_ref[rows, :].astype(F32))
                    dpo = dyv * sg
                    dg_ref[rows, :] = (dyv * t * scv * dsg).astype(BF16)
                    dsc_acc[...] += _rowsum8(dpo * t)
                    dtb = (dpo * scv).astype(BF16)
                    dw_acc[...] += lax.dot_general(pb, dtb, (((0,), (0,)), ((), ())),
                                                   preferred_element_type=F32)
                    dpooled = lax.dot_general(dtb, wv, (((1,), (1,)), ((), ())),
                                              preferred_element_type=F32)
                    dpl[rows, :] = dpooled
                    rpad[rows, :] = dpooled * inv
                    return carry

                lax.fori_loop(0, n_chunks, chunk, 0)

                def chunk2(ci, carry):
                    r0 = pl.multiple_of(ci * rc, rc)
                    rows = pl.ds(r0, rc)
                    xx = rpad[pl.ds(r0, rc + HALO), :]
                    fs = xx[0:rc]
                    for i in range(1, win):
                        fs = fs + _tap_after(xx, i, rc)
                    du_ref[rows, :] = (fs - dpl[rows, :]).astype(BF16)
                    return carry

                lax.fori_loop(0, n_chunks, chunk2, 0)
                dw_ref[0] = dw_acc[...]
                dsc_ref[...] = jnp.sum(dsc_acc[...], axis=0, keepdims=True)

    grp = lambda j: jnp.maximum(j - ng, 0)
    att = lambda j: jnp.minimum(j, ng - 1)
    return pl.pallas_call(
        body, name=name, grid=(2 * ng,),
        in_specs=[pl.BlockSpec((s, cw), lambda j: (0, j)),
                  pl.BlockSpec((s, cw), lambda j: (0, att(j))),
                  pl.BlockSpec((s, cw), lambda j: (0, 3 * ng + grp(j))),
                  pl.BlockSpec((s, cw), lambda j: (0, 4 * ng + j)),
                  pl.BlockSpec((1, cw, cw), lambda j: (grp(j), 0, 0)),
                  pl.BlockSpec((1, cw), lambda j: (0, grp(j)))],
        out_specs=[pl.BlockSpec((s, cw), lambda j: (0, att(j))),
                   pl.BlockSpec((s, cw), lambda j: (0, grp(j))),
                   pl.BlockSpec((s, cw), lambda j: (0, j)),
                   pl.BlockSpec((1, cw, cw), lambda j: (grp(j), 0, 0)),
                   pl.BlockSpec((1, cw), lambda j: (0, grp(j)))],
        out_shape=[jax.ShapeDtypeStruct((s, ng * cw), BF16), jax.ShapeDtypeStruct((s, ng * cw), BF16),
                   jax.ShapeDtypeStruct((s, 2 * ng * cw), BF16),
                   jax.ShapeDtypeStruct((ng, cw, cw), F32), jax.ShapeDtypeStruct((1, ng * cw), F32)],
        scratch_shapes=[pltpu.VMEM((HALO + s, cw), F32), pltpu.VMEM((s + HALO, cw), F32),
                        pltpu.VMEM((s, cw), F32), pltpu.VMEM((cw, cw), F32), pltpu.VMEM((8, cw), F32)],
        compiler_params=_params("arbitrary"),
    )(dy, a, p, p, pool_w, pool_scale)


def _halo_before(tm):
    return lambda i: jnp.maximum(i * (tm // HALO) - 1, 0)


def _halo_after(tm, s):
    return lambda i: jnp.minimum((i + 1) * (tm // HALO), s // HALO - 1)


def odd_mix_fwd(p, sconv_w, dconv_w, dconv_b, cnorm_g, cnorm_b, name, tm=128):
    s = p.shape[0]
    cw = sconv_w.shape[1]
    n = s // tm
    lanes = 128
    hb = _halo_before(tm)

    def body(hc_ref, hch_ref, bc_ref, cc_ref, cch_ref, ga_ref, gah_ref, gb_ref, gbh_ref, g1_ref, g2_ref,
             sw_ref, dw_ref, db_ref, gam_ref, bet_ref, y_ref, dc_ref):
        first = pl.program_id(0) == 0
        for l in range(cw // lanes):
            cols = slice(l * lanes, (l + 1) * lanes)
            mh = jnp.where(first, 0.0, cch_ref[:, cols].astype(F32) * hch_ref[:, cols].astype(F32))
            mm = cc_ref[:, cols].astype(F32) * hc_ref[:, cols].astype(F32)
            xx = jnp.concatenate([mh, mm], axis=0)
            cv = jnp.zeros((tm, lanes), F32)
            for k in range(SCONV_K):
                cv = cv + sw_ref[k:k + 1, cols] * _tap_before(xx, SCONV_K - 1 - k, tm)
            c_out = bc_ref[:, cols].astype(F32) * cv
            y_ref[:, cols] = (c_out * _silu(g1_ref[:, cols].astype(F32))).astype(BF16)
            dh = jnp.where(first, 0.0, gah_ref[:, cols].astype(F32) * _sigmoid(gbh_ref[:, cols].astype(F32)))
            dm = ga_ref[:, cols].astype(F32) * _sigmoid(gb_ref[:, cols].astype(F32))
            xx = jnp.concatenate([dh, dm], axis=0)
            acc = jnp.zeros((tm, lanes), F32) + db_ref[:, cols]
            for k in range(CONF_K):
                acc = acc + dw_ref[k:k + 1, cols] * _tap_before(xx, CONF_K - 1 - k, tm)
            dc_ref[:, cols] = acc
        rs = 32
        for r in range(tm // rs):
            rows = slice(r * rs, (r + 1) * rs)
            xv = dc_ref[rows, :]
            mu = jnp.mean(xv, axis=-1, keepdims=True)
            xc = xv - mu
            rstd = lax.rsqrt(jnp.mean(xc * xc, axis=-1, keepdims=True) + EPS)
            ln = xc * rstd * gam_ref[...] + bet_ref[...]
            y_ref[rows, cw:2 * cw] = (_silu(ln) * _silu(g2_ref[rows, :].astype(F32))).astype(BF16)

    main = lambda c: pl.BlockSpec((tm, cw), lambda i: (i, c))
    halo = lambda c: pl.BlockSpec((HALO, cw), lambda i: (hb(i), c))
    vec = lambda r: pl.BlockSpec((r, cw), lambda i: (0, 0))
    return pl.pallas_call(
        body, name=name, grid=(n,),
        in_specs=[main(0), halo(0), main(1), main(2), halo(2), main(3), halo(3), main(4), halo(4),
                  main(5), main(6), vec(SCONV_K), vec(CONF_K), vec(1), vec(1), vec(1)],
        out_specs=[pl.BlockSpec((tm, 2 * cw), lambda i: (i, 0)), pl.BlockSpec((tm, cw), lambda i: (i, 0))],
        out_shape=[jax.ShapeDtypeStruct((s, 2 * cw), BF16), jax.ShapeDtypeStruct((s, cw), F32)],
        compiler_params=_params("parallel"),
    )(p, p, p, p, p, p, p, p, p, p, p, sconv_w, dconv_w, dconv_b, cnorm_g, cnorm_b)


def odd_bwd_ln(dy, p, dc, cnorm_g, cnorm_b, name, tm=256):
    s = p.shape[0]
    cw = dc.shape[1]
    n = s // tm
    rs = 32

    def body(dy_ref, g2_ref, dc_ref, gam_ref, bet_ref, ddc_ref, dg_ref, dgam_ref, dbet_ref, gacc, bacc):
        i = pl.program_id(0)

        @pl.when(i == 0)
        def _():
            gacc[...] = jnp.zeros_like(gacc)
            bacc[...] = jnp.zeros_like(bacc)

        def chunk(ci, carry):
            rows = pl.ds(pl.multiple_of(ci * rs, rs), rs)
            xv = dc_ref[rows, :]
            mu = jnp.mean(xv, axis=-1, keepdims=True)
            xc = xv - mu
            rstd = lax.rsqrt(jnp.mean(xc * xc, axis=-1, keepdims=True) + EPS)
            xh = xc * rstd
            gam = gam_ref[...]
            sl, dsl = _silu_and_grad(xh * gam + bet_ref[...])
            sg, dsg = _silu_and_grad(g2_ref[rows, :].astype(F32))
            dyv = dy_ref[rows, :].astype(F32)
            dg_ref[rows, :] = (dyv * sl * dsg).astype(BF16)
            dln = dyv * sg * dsl
            gacc[...] += _rowsum8(dln * xh)
            bacc[...] += _rowsum8(dln)
            dxh = dln * gam
            ddc_ref[rows, :] = rstd * (dxh - jnp.mean(dxh, axis=-1, keepdims=True)
                                       - xh * jnp.mean(dxh * xh, axis=-1, keepdims=True))
            return carry

        lax.fori_loop(0, tm // rs, chunk, 0)

        @pl.when(i == n - 1)
        def _():
            dgam_ref[...] = jnp.sum(gacc[...], axis=0, keepdims=True)
            dbet_ref[...] = jnp.sum(bacc[...], axis=0, keepdims=True)

    vec = pl.BlockSpec((1, cw), lambda i: (0, 0))
    return pl.pallas_call(
        body, name=name, grid=(n,),
        in_specs=[pl.BlockSpec((tm, cw), lambda i: (i, 1)), pl.BlockSpec((tm, cw), lambda i: (i, 6)),
                  pl.BlockSpec((tm, cw), lambda i: (i, 0)), vec, vec],
        out_specs=[pl.BlockSpec((tm, cw), lambda i: (i, 0)), pl.BlockSpec((tm, cw), lambda i: (i, 0)), vec, vec],
        out_shape=[jax.ShapeDtypeStruct((s, cw), F32), jax.ShapeDtypeStruct((s, cw), BF16),
                   jax.ShapeDtypeStruct((1, cw), F32), jax.ShapeDtypeStruct((1, cw), F32)],
        scratch_shapes=[pltpu.VMEM((8, cw), F32), pltpu.VMEM((8, cw), F32)],
        compiler_params=_params("arbitrary"),
    )(dy, p, dc, cnorm_g, cnorm_b)


def odd_bwd_conv(dy, p, ddc, dg2, sconv_w, dconv_w, name, tm=128):
    s = p.shape[0]
    cw = ddc.shape[1]
    n = s // tm
    lanes = 128
    hb = _halo_before(tm)
    ha = _halo_after(tm, s)

    def body(dy_ref, dya_ref, g1_ref, g1a_ref, bc_ref, bca_ref, hc_ref, hch_ref, cc_ref, cch_ref,
             ddc_ref, ddca_ref, ga_ref, gah_ref, gb_ref, gbh_ref, dg2_ref, sw_ref, dw_ref,
             dp_ref, dsw_ref, ddw_ref, ddb_ref, sw_acc, dw_acc, db_acc):
        i = pl.program_id(0)
        first = i == 0
        last = i == n - 1

        @pl.when(first)
        def _():
            sw_acc[...] = jnp.zeros_like(sw_acc)
            dw_acc[...] = jnp.zeros_like(dw_acc)
            db_acc[...] = jnp.zeros_like(db_acc)

        for l in range(cw // lanes):
            cols = slice(l * lanes, (l + 1) * lanes)
            mh = jnp.where(first, 0.0, cch_ref[:, cols].astype(F32) * hch_ref[:, cols].astype(F32))
            hcv = hc_ref[:, cols].astype(F32)
            ccv = cc_ref[:, cols].astype(F32)
            xx = jnp.concatenate([mh, ccv * hcv], axis=0)
            taps = [_tap_before(xx, SCONV_K - 1 - k, tm) for k in range(SCONV_K)]
            cv = jnp.zeros((tm, lanes), F32)
            for k in range(SCONV_K):
                cv = cv + sw_ref[k:k + 1, cols] * taps[k]
            bcv = bc_ref[:, cols].astype(F32)
            dyv = dy_ref[:, cols].astype(F32)
            sg, dsg = _silu_and_grad(g1_ref[:, cols].astype(F32))
            dco = dyv * sg
            dp_ref[:, 5 * cw + l * lanes:5 * cw + (l + 1) * lanes] = (dyv * bcv * cv * dsg).astype(BF16)
            dp_ref[:, cw + l * lanes:cw + (l + 1) * lanes] = (dco * cv).astype(BF16)
            dcv = dco * bcv
            for k in range(SCONV_K):
                sw_acc[k * 8:(k + 1) * 8, cols] += _rowsum8(dcv * taps[k])
            dcv_a = jnp.where(last, 0.0, dya_ref[:, cols].astype(F32) * _silu(g1a_ref[:, cols].astype(F32))
                              * bca_ref[:, cols].astype(F32))
            xx = jnp.concatenate([dcv, dcv_a], axis=0)
            dm = jnp.zeros((tm, lanes), F32)
            for k in range(SCONV_K):
                dm = dm + sw_ref[k:k + 1, cols] * _tap_after(xx, SCONV_K - 1 - k, tm)
            dp_ref[:, l * lanes:(l + 1) * lanes] = (dm * ccv).astype(BF16)
            dp_ref[:, 2 * cw + l * lanes:2 * cw + (l + 1) * lanes] = (dm * hcv).astype(BF16)
            gav = ga_ref[:, cols].astype(F32)
            sb = _sigmoid(gb_ref[:, cols].astype(F32))
            dh = jnp.where(first, 0.0, gah_ref[:, cols].astype(F32) * _sigmoid(gbh_ref[:, cols].astype(F32)))
            xx = jnp.concatenate([dh, gav * sb], axis=0)
            ddcv = ddc_ref[:, cols]
            db_acc[:, cols] += _rowsum8(ddcv)
            for k in range(CONF_K):
                dw_acc[k * 8:(k + 1) * 8, cols] += _rowsum8(ddcv * _tap_before(xx, CONF_K - 1 - k, tm))
            ddc_a = jnp.where(last, 0.0, ddca_ref[:, cols])
            xx = jnp.concatenate([ddcv, ddc_a], axis=0)
            dgl = jnp.zeros((tm, lanes), F32)
            for k in range(CONF_K):
                dgl = dgl + dw_ref[k:k + 1, cols] * _tap_after(xx, CONF_K - 1 - k, tm)
            dp_ref[:, 3 * cw + l * lanes:3 * cw + (l + 1) * lanes] = (dgl * sb).astype(BF16)
            dp_ref[:, 4 * cw + l * lanes:4 * cw + (l + 1) * lanes] = (dgl * gav * sb * (1.0 - sb)).astype(BF16)
        dp_ref[:, 6 * cw:7 * cw] = dg2_ref[...]

        @pl.when(last)
        def _():
            for k in range(SCONV_K):
                dsw_ref[k:k + 1, :] = jnp.sum(sw_acc[k * 8:(k + 1) * 8, :], axis=0, keepdims=True)
            for k in range(CONF_K):
                ddw_ref[k:k + 1, :] = jnp.sum(dw_acc[k * 8:(k + 1) * 8, :], axis=0, keepdims=True)
            ddb_ref[...] = jnp.sum(db_acc[...], axis=0, keepdims=True)

    def main(c):
        return pl.BlockSpec((tm, cw), lambda i: (i, c))

    def before(c):
        return pl.BlockSpec((HALO, cw), lambda i: (hb(i), c))

    def after(c):
        return pl.BlockSpec((HALO, cw), lambda i: (ha(i), c))

    def vec(r):
        return pl.BlockSpec((r, cw), lambda i: (0, 0))

    return pl.pallas_call(
        body, name=name, grid=(n,),
        in_specs=[main(0), after(0), main(5), after(5), main(1), after(1), main(0), before(0), main(2), before(2),
                  main(0), after(0), main(3), before(3), main(4), before(4), main(0), vec(SCONV_K), vec(CONF_K)],
        out_specs=[pl.BlockSpec((tm, 7 * cw), lambda i: (i, 0)), vec(SCONV_K), vec(CONF_K), vec(1)],
        out_shape=[jax.ShapeDtypeStruct((s, 7 * cw), BF16), jax.ShapeDtypeStruct((SCONV_K, cw), F32),
                   jax.ShapeDtypeStruct((CONF_K, cw), F32), jax.ShapeDtypeStruct((1, cw), F32)],
        scratch_shapes=[pltpu.VMEM((8 * SCONV_K, cw), F32), pltpu.VMEM((8 * CONF_K, cw), F32),
                        pltpu.VMEM((8, cw), F32)],
        compiler_params=_params("arbitrary"),
    )(dy, dy, p, p, p, p, p, p, p, p, ddc, ddc, p, p, p, p, dg2, sconv_w, dconv_w)


_ANY = pl.BlockSpec(memory_space=pl.ANY)


def _place():
    return lax.axis_index("x"), lax.axis_index("y"), lax.axis_index("c")


def all_gather(arrs, name, deps=()):
    n = len(arrs)

    def body(*refs):
        ins, outs = refs[:n], refs[n + len(deps):2 * n + len(deps)]
        send_sems, recv_sems, local_sems = refs[-3:]
        x, y, c = _place()
        me, sibling = (x, y, c), (x, y, 1 - c)
        chips = [(1 - x, y), (x, 1 - y), (1 - x, 1 - y)]

        def copy(a, k, block, to, src=None):
            px, py, pc = block
            dst = outs[a].at[4 * px + 2 * py + pc]
            return pltpu.make_async_remote_copy(
                src_ref=dst if src is None else src, dst_ref=dst,
                send_sem=send_sems.at[7 * a + k], recv_sem=recv_sems.at[7 * a + k],
                device_id=to, device_id_type=MESH)

        mine = [pltpu.make_async_copy(ins[a], outs[a].at[4 * x + 2 * y + c], local_sems.at[a]) for a in range(n)]
        first = []
        for a in range(n):
            first.append(copy(a, 0, me, sibling, src=ins[a]))
            first += [copy(a, 1 + j, me, (*chip, c), src=ins[a]) for j, chip in enumerate(chips)]
        for cp in first + mine:
            cp.start()
        passed = []
        for a in range(n):
            for j, chip in enumerate(chips):
                copy(a, 1 + j, (*chip, c), me).wait_recv()
                cp = copy(a, 4 + j, (*chip, c), sibling)
                cp.start()
                passed.append(cp)
        for a in range(n):
            copy(a, 0, sibling, me).wait_recv()
            for j, chip in enumerate(chips):
                copy(a, 4 + j, (*chip, 1 - c), me).wait_recv()
        for cp in first + passed:
            cp.wait_send()
        for cp in mine:
            cp.wait()

    return pl.pallas_call(
        body, name=name,
        out_shape=[jax.ShapeDtypeStruct((N_DEV,) + a.shape, a.dtype) for a in arrs],
        in_specs=[_ANY] * (n + len(deps)), out_specs=[_ANY] * n,
        scratch_shapes=[pltpu.SemaphoreType.DMA((7 * n,)), pltpu.SemaphoreType.DMA((7 * n,)),
                        pltpu.SemaphoreType.DMA((n,))],
    )(*arrs, *deps)


def in_proj_gathered(h, w_own, extras, name, tm=512):
    s, d = h.shape
    n = w_own.shape[1]
    arrs = [w_own] + list(extras)
    na = len(arrs)

    def body(*refs):
        h_ref, ins = refs[0], refs[1:1 + na]
        p_ref, outs = refs[1 + na], refs[2 + na:2 + 2 * na]
        wbuf, obuf, send_sems, recv_sems, load_sem, store_sems, own_sems = refs[2 + 2 * na:]
        x, y, c = _place()
        me, sibling = (x, y, c), (x, y, 1 - c)
        chips = [(1 - x, y), (x, 1 - y), (1 - x, 1 - y)]

        def slot(block):
            return 4 * block[0] + 2 * block[1] + block[2]

        def copy(a, k, block, to, src=None):
            dst = outs[a].at[slot(block)]
            return pltpu.make_async_remote_copy(
                src_ref=dst if src is None else src, dst_ref=dst,
                send_sem=send_sems.at[7 * a + k], recv_sem=recv_sems.at[7 * a + k],
                device_id=to, device_id_type=MESH)

        first = []
        for a in range(na):
            first.append(copy(a, 0, me, sibling, src=ins[a]))
            first += [copy(a, 1 + j, me, (*chip, c), src=ins[a]) for j, chip in enumerate(chips)]
        for cp in first:
            cp.start()
        own = pltpu.make_async_copy(wbuf.at[0], outs[0].at[slot(me)], own_sems.at[0])
        mine = [pltpu.make_async_copy(ins[a], outs[a].at[slot(me)], own_sems.at[a]) for a in range(1, na)]
        stores = [None, None]

        def multiply(k, block, w_from):
            b = k % 2
            if k == 2:
                own.wait()
            load = pltpu.make_async_copy(w_from, wbuf.at[b], load_sem)
            load.start()
            if stores[b] is not None:
                stores[b].wait()
            load.wait()
            if k == 0:
                own.start()

            def chunk(i, carry):
                rows = pl.ds(pl.multiple_of(i * tm, tm), tm)
                obuf[b, rows, :] = jnp.dot(h_ref[rows, :], wbuf[b], preferred_element_type=F32).astype(BF16)
                return carry

            lax.fori_loop(0, s // tm, chunk, 0)
            stores[b] = pltpu.make_async_copy(
                obuf.at[b], p_ref.at[:, pl.ds(pl.multiple_of(slot(block) * n, 128), n)], store_sems.at[b])
            stores[b].start()

        multiply(0, me, ins[0])
        for a in range(na):
            copy(a, 0, sibling, me).wait_recv()
        multiply(1, sibling, outs[0].at[slot(sibling)])
        passed = []
        for j, chip in enumerate(chips):
            for a in range(na):
                copy(a, 1 + j, (*chip, c), me).wait_recv()
                cp = copy(a, 4 + j, (*chip, c), sibling)
                cp.start()
                passed.append(cp)
            multiply(2 + 2 * j, (*chip, c), outs[0].at[slot((*chip, c))])
            for a in range(na):
                copy(a, 4 + j, (*chip, 1 - c), me).wait_recv()
            multiply(3 + 2 * j, (*chip, 1 - c), outs[0].at[slot((*chip, 1 - c))])
        for cp in mine:
            cp.start()
        for cp in first + passed:
            cp.wait_send()
        for cp in mine + stores:
            cp.wait()

    vmem = pl.BlockSpec(memory_space=pltpu.VMEM)
    outs = pl.pallas_call(
        body, name=name,
        out_shape=[jax.ShapeDtypeStruct((s, N_DEV * n), BF16)]
        + [jax.ShapeDtypeStruct((N_DEV,) + a.shape, a.dtype) for a in arrs],
        in_specs=[vmem] + [_ANY] * na, out_specs=[_ANY] * (1 + na),
        scratch_shapes=[pltpu.VMEM((2, d, n), BF16), pltpu.VMEM((2, s, n), BF16),
                        pltpu.SemaphoreType.DMA((7 * na,)), pltpu.SemaphoreType.DMA((7 * na,)),
                        pltpu.SemaphoreType.DMA, pltpu.SemaphoreType.DMA((2,)), pltpu.SemaphoreType.DMA((na,))],
        compiler_params=pltpu.CompilerParams(vmem_limit_bytes=VMEM_LIMIT),
    )(h, *arrs)
    return outs[0], outs[1], outs[2:]


_HBM = pl.BlockSpec(memory_space=pltpu.HBM)
_SEM = pl.BlockSpec(memory_space=pltpu.SEMAPHORE)
_DATAFLOW = pltpu.SideEffectType.DATAFLOW_SIDE_EFFECTING


def _peers_per_array(kind):
    return 1 if kind == "sibling" else 3


def _split_copies(kind, srcs, lands, send_sems, recv_sems):
    x, y, c = _place()
    per = _peers_per_array(kind)
    out = []
    for a in range(len(srcs)):
        if kind == "sibling":
            peers = [((x, y, 1 - c), srcs[a].at[:, pl.ds(1 - c, 1)], lands[a], lands[a])]
        else:
            peers = []
            for px, py in [(1 - x, y), (x, 1 - y), (1 - x, 1 - y)]:
                if kind == "gather":
                    views = (srcs[a], lands[a].at[4 * x + 2 * y + c], lands[a].at[4 * px + 2 * py + c])
                else:
                    views = (srcs[a].at[2 * px + py], lands[a].at[2 * x + y], lands[a].at[2 * px + py])
                peers.append(((px, py, c),) + views)
        for j, (peer, src, dst, arrives) in enumerate(peers):
            sems = dict(send_sem=send_sems.at[per * a + j], recv_sem=recv_sems.at[per * a + j],
                        device_id=peer, device_id_type=MESH)
            out.append((pltpu.make_async_remote_copy(src_ref=src, dst_ref=dst, **sems),
                        pltpu.make_async_remote_copy(src_ref=src, dst_ref=arrives, **sems)))
    return out


def split_start(kind, srcs, lands, deps, name):
    n = len(srcs)
    n_sems = _peers_per_array(kind) * n

    def body(*refs):
        send_sems, recv_sems = refs[2 * n + len(deps)], refs[2 * n + len(deps) + 1]
        for copy, _ in _split_copies(kind, refs[:n], refs[n:2 * n], send_sems, recv_sems):
            copy.start()
        token = refs[-1]
        token[...] = jnp.zeros_like(token)

    held = [pltpu.HBM(a.shape, a.dtype) for a in list(srcs) + list(lands)]
    outs = pl.pallas_call(
        body, name=name,
        out_shape=(pltpu.SemaphoreType.DMA((n_sems,)), pltpu.SemaphoreType.DMA((n_sems,)), *held,
                   jax.ShapeDtypeStruct((8, 128), F32)),
        in_specs=[_HBM] * (2 * n) + [_ANY] * len(deps),
        out_specs=(_SEM, _SEM, *([_HBM] * (2 * n)), pl.BlockSpec(memory_space=pltpu.VMEM)),
        input_output_aliases={i: 2 + i for i in range(2 * n)},
        compiler_params=pltpu.CompilerParams(has_side_effects=_DATAFLOW),
    )(*[pltpu.with_memory_space_constraint(a, pltpu.HBM) for a in list(srcs) + list(lands)], *deps)
    return outs[0], outs[1], list(outs[2:2 + n]), list(outs[2 + n:2 + 2 * n]), outs[-1]


def split_wait(kind, send_sems, recv_sems, srcs, lands, afters, name):
    n = len(srcs)

    def body(*refs):
        for _, arrival in _split_copies(kind, refs[:n], refs[n:2 * n], refs[2 * n], refs[2 * n + 1]):
            arrival.wait_send()
            arrival.wait_recv()

    outs = pl.pallas_call(
        body, name=name,
        out_shape=[pltpu.HBM(a.shape, a.dtype) for a in list(srcs) + list(lands)],
        in_specs=[_HBM] * (2 * n) + [_SEM, _SEM] + [_ANY] * len(afters),
        out_specs=[_HBM] * (2 * n),
        input_output_aliases={i: i for i in range(2 * n)},
        compiler_params=pltpu.CompilerParams(has_side_effects=_DATAFLOW),
    )(*srcs, *lands, send_sems, recv_sems, *afters)
    return list(outs[:n]), list(outs[n:])


def place_block(land, block, dev, name):
    r, c = block.shape
    tr = min(r, 512)

    def body(dev_ref, land_ref, b_ref, o_ref):
        del dev_ref, land_ref
        o_ref[...] = b_ref[...]

    return pl.pallas_call(
        body, name=name,
        grid_spec=pltpu.PrefetchScalarGridSpec(
            num_scalar_prefetch=1, grid=(r // tr,),
            in_specs=[_ANY, pl.BlockSpec((tr, c), lambda i, dev_ref: (i, 0))],
            out_specs=pl.BlockSpec((None, tr, c), lambda i, dev_ref: (dev_ref[0], i, 0))),
        out_shape=jax.ShapeDtypeStruct(land.shape, land.dtype),
        input_output_aliases={1: 0},
        compiler_params=_params("parallel"),
    )(dev, land, block)


def gather_finish(lands, name):
    n = len(lands)

    def body(*refs):
        outs = refs[n:2 * n]
        send_sems, recv_sems = refs[2 * n:]
        x, y, c = _place()
        cps = [pltpu.make_async_remote_copy(
            src_ref=outs[a].at[:, pl.ds(c, 1)], dst_ref=outs[a].at[:, pl.ds(c, 1)],
            send_sem=send_sems.at[a], recv_sem=recv_sems.at[a],
            device_id=(x, y, 1 - c), device_id_type=MESH) for a in range(n)]
        for cp in cps:
            cp.start()
        for cp in cps:
            cp.wait()

    return pl.pallas_call(
        body, name=name,
        out_shape=[jax.ShapeDtypeStruct(a.shape, a.dtype) for a in lands],
        in_specs=[_ANY] * n, out_specs=[_ANY] * n,
        input_output_aliases={i: i for i in range(n)},
        scratch_shapes=[pltpu.SemaphoreType.DMA((n,)), pltpu.SemaphoreType.DMA((n,))],
    )(*lands)


def pair_add(own, recv, core, name):
    _, _, r, c = own.shape
    tr = min(r, 512)

    def body(core_ref, own_ref, recv_ref, o_ref):
        del core_ref
        o_ref[...] = (own_ref[...].astype(F32) + recv_ref[...].astype(F32)).astype(BF16)

    return pl.pallas_call(
        body, name=name,
        grid_spec=pltpu.PrefetchScalarGridSpec(
            num_scalar_prefetch=1, grid=(4, r // tr),
            in_specs=[pl.BlockSpec((None, None, tr, c), lambda k, i, core_ref: (k, core_ref[0], i, 0)),
                      pl.BlockSpec((None, None, tr, c), lambda k, i, core_ref: (k, 0, i, 0))],
            out_specs=pl.BlockSpec((None, tr, c), lambda k, i, core_ref: (k, i, 0))),
        out_shape=jax.ShapeDtypeStruct((4, r, c), BF16),
        compiler_params=_params("parallel", "parallel"),
    )(core, own, recv)


def _adamw_math(w, g, m, v):
    m2 = ADAM_B1 * m + (1.0 - ADAM_B1) * g
    v2 = ADAM_B2 * v + (1.0 - ADAM_B2) * (g * g)
    m_hat = m2 / (1.0 - ADAM_B1 ** ADAM_STEP)
    v_hat = v2 / (1.0 - ADAM_B2 ** ADAM_STEP)
    delta = -ADAM_LR * (m_hat / (jnp.sqrt(v_hat) + ADAM_EPS) + ADAM_WD * w)
    return delta, m2, v2


def adamw_big(w, m, v, own, got, chip, name):
    r, c = w.shape
    tr = min(r, 256)

    def body(chip_ref, w_ref, m_ref, v_ref, p0, p1, p2, p3, g_ref, d_ref, m2_ref, v2_ref):
        del chip_ref
        g = ((p0[...].astype(F32) + p1[...].astype(F32)) + p2[...].astype(F32)) + p3[...].astype(F32)
        delta, m2, v2 = _adamw_math(w_ref[...], g, m_ref[...], v_ref[...])
        g_ref[...] = g
        d_ref[...] = delta
        m2_ref[...] = m2
        v2_ref[...] = v2

    row = pl.BlockSpec((tr, c), lambda i, chip_ref: (i, 0))

    def slab(flip):
        return pl.BlockSpec((None, tr, c), lambda i, chip_ref: (chip_ref[0] ^ flip, i, 0))

    return pl.pallas_call(
        body, name=name,
        grid_spec=pltpu.PrefetchScalarGridSpec(
            num_scalar_prefetch=1, grid=(r // tr,),
            in_specs=[row, row, row, slab(0), slab(1), slab(2), slab(3)],
            out_specs=[row] * 4),
        out_shape=[jax.ShapeDtypeStruct((r, c), F32)] * 4,
        compiler_params=_params("parallel"),
    )(chip, w, m, v, own, got, got, got)


def sum_devices(g8, name):
    def body(g_ref, o_ref):
        tot = g_ref[0]
        for k in range(1, N_DEV):
            tot = tot + g_ref[k]
        o_ref[...] = tot

    return pl.pallas_call(body, name=name, out_shape=jax.ShapeDtypeStruct(g8.shape[1:], F32))(g8)


def adamw_small(ws, gs, ms, vs, name):
    n = len(ws)

    def body(*refs):
        w_r, g_r, m_r, v_r = refs[:n], refs[n:2 * n], refs[2 * n:3 * n], refs[3 * n:4 * n]
        d_o, m_o, v_o = refs[4 * n:5 * n], refs[5 * n:6 * n], refs[6 * n:7 * n]
        for k in range(n):
            delta, m2, v2 = _adamw_math(w_r[k][...], g_r[k][...], m_r[k][...], v_r[k][...])
            d_o[k][...] = delta
            m_o[k][...] = m2
            v_o[k][...] = v2

    shapes = [jax.ShapeDtypeStruct(w.shape, F32) for w in ws]
    outs = pl.pallas_call(body, name=name, out_shape=shapes * 3)(*ws, *gs, *ms, *vs)
    return outs[:n], outs[n:2 * n], outs[2 * n:]


def _rows128(a):
    return a.reshape(-1, 128)


def _pad_rows(a, rows):
    return jnp.pad(a, ((0, rows - a.shape[0]), (0, 0)))


def kernel(x, ln_pre_even, w_in_even, pool_w, pool_scale, w_out_even, ln_post_even, ln_pre_odd, w_in_odd, sconv_w, dconv_w, dconv_b, cnorm_g, cnorm_b, w_out_odd, ln_post_odd, loss_target, m_ln_pre_even, m_w_in_even, m_pool_w, m_pool_scale, m_w_out_even, m_ln_post_even, m_ln_pre_odd, m_w_in_odd, m_sconv_w, m_dconv_w, m_dconv_b, m_cnorm_g, m_cnorm_b, m_w_out_odd, m_ln_post_odd, v_ln_pre_even, v_w_in_even, v_pool_w, v_pool_scale, v_w_out_even, v_ln_post_even, v_ln_pre_odd, v_w_in_odd, v_sconv_w, v_dconv_w, v_dconv_b, v_cnorm_g, v_cnorm_b, v_w_out_odd, v_ln_post_odd):
    xs = x[0]
    tgt = loss_target[0]
    s, d = xs.shape
    half = d // 2
    n_heads = half // HEAD_DIM
    ng = len(POOL_WINDOWS)
    cwp = half // ng
    dev = 4 * lax.axis_index("x") + 2 * lax.axis_index("y") + lax.axis_index("c")
    core = lax.axis_index("c").astype(jnp.int32).reshape(1)

    pr = pool_w.shape[2]
    cl = sconv_w.shape[2]
    small_parts = [(_rows128(ln_pre_odd), 8), (sconv_w[0], 8), (dconv_w[0], 32), (dconv_b, 8),
                   (cnorm_g, 8), (cnorm_b, 8), (_rows128(ln_post_odd), 8)]
    small_local = jnp.concatenate([_pad_rows(a, r) for a, r in small_parts], axis=0)
    h0 = rms_fwd(xs, ln_pre_even, "rms_pre_even")
    p0, g_wie, (g_pw, g_small) = in_proj_gathered(
        h0, w_in_even[0].astype(BF16), [pool_w[0].reshape(ng * pr, cwp).astype(BF16), small_local],
        "ag_in_proj_even")
    comm = _Exchanges(dev, core, d)
    token = comm.start_weights("out_even", [w_out_even[0].astype(BF16)], [p0])
    sb_dep = comm.start_weights("odd", [w_in_odd[0].astype(BF16), w_out_odd[0].astype(BF16)], [token])
    pool_full = g_pw.reshape(N_DEV, ng, pr, cwp).transpose(1, 0, 2, 3).reshape(ng, cwp, cwp)
    nl = ln_pre_odd.shape[1] // 128

    def chan(lo, rows):
        return g_small[:, lo:lo + rows].transpose(1, 0, 2).reshape(rows, N_DEV * cl)

    ln_pre_odd_f = g_small[:, 0:nl].reshape(1, d)
    sconv_f = chan(8, SCONV_K)
    dconv_f = chan(16, CONF_K)
    dconv_b_f = chan(48, 1)
    cnorm_g_f = chan(56, 1)
    cnorm_b_f = chan(64, 1)
    ln_post_odd_f = g_small[:, 72:72 + nl].reshape(1, d)

    loss_blk, grad_x, small_g = _fwd_bwd(
        xs, tgt, ln_pre_even, h0, p0, g_wie, pool_full, pool_scale, ln_post_even, ln_pre_odd_f,
        sconv_f, dconv_f, dconv_b_f, cnorm_g_f, cnorm_b_f, ln_post_odd_f, comm, sb_dep)
    small_w = [ln_pre_even, pool_scale, ln_post_even, ln_pre_odd, sconv_w[0], dconv_w[0], dconv_b, cnorm_g, cnorm_b, ln_post_odd]
    small_m = [m_ln_pre_even, m_pool_scale, m_ln_post_even, m_ln_pre_odd, m_sconv_w[0], m_dconv_w[0], m_dconv_b, m_cnorm_g, m_cnorm_b, m_ln_post_odd]
    small_v = [v_ln_pre_even, v_pool_scale, v_ln_post_even, v_ln_pre_odd, v_sconv_w[0], v_dconv_w[0], v_dconv_b, v_cnorm_g, v_cnorm_b, v_ln_post_odd]
    big = {"w_in_even": (w_in_even, m_w_in_even, v_w_in_even), "pool_w": (pool_w, m_pool_w, v_pool_w),
           "w_out_even": (w_out_even, m_w_out_even, v_w_out_even), "w_in_odd": (w_in_odd, m_w_in_odd, v_w_in_odd),
           "w_out_odd": (w_out_odd, m_w_out_odd, v_w_out_odd)}
    upd = comm.finish_updates(big, [grad_x])
    upd.update(comm.finish_updates(big, [grad_x]))
    sg, sd, sm, sv, loss = _update_small(small_g, loss_blk, small_w, small_m, small_v, dev, d, cl,
                                         deps=[upd["w_in_odd"][1], upd["w_out_even"][1]])
    upd.update(comm.finish_updates(big, sd))
    (g_wie_o, d_wie, m_wie, v_wie), (g_pw_o, d_pw, m_pw, v_pw) = upd["w_in_even"], upd["pool_w"]
    (g_woe_o, d_woe, m_woe, v_woe), (g_wio_o, d_wio, m_wio, v_wio) = upd["w_out_even"], upd["w_in_odd"]
    g_woo_o, d_woo, m_woo, v_woo = upd["w_out_odd"]

    def order(small, wie, pw, woe, wio, woo):
        return [small[0], wie, pw, small[1], woe, small[2], small[3], wio, small[4], small[5], small[6],
                small[7], small[8], woo, small[9]]

    grads = order(sg, g_wie_o, g_pw_o, g_woe_o, g_wio_o, g_woo_o)
    deltas = order(sd, d_wie, d_pw, d_woe, d_wio, d_woo)
    new_m = order(sm, m_wie, m_pw, m_woe, m_wio, m_woo)
    new_v = order(sv, v_wie, v_pw, v_woe, v_wio, v_woo)
    return (loss, grad_x[None], *grads, *deltas, *new_m, *new_v)


def _fwd_bwd(xs, tgt, ln_pre_even, h0, p0, g_wie, pool_full, pool_scale, ln_post_even, ln_pre_odd_f,
             sconv_f, dconv_f, dconv_b_f, cnorm_g_f, cnorm_b_f, ln_post_odd_f, comm, sb_dep):
    d = xs.shape[1]
    n_heads = d // 2 // HEAD_DIM
    ng, cwp = pool_full.shape[0], pool_full.shape[1]
    a0, sb_wts = sb_fwd(p0, n_heads, "sb_fwd", dep=sb_dep)
    y0 = even_mix_fwd(a0, p0, pool_full, pool_scale, "even_mix_fwd")
    (w_out_e,) = comm.weights("out_even", after=y0)
    w_out_e = w_out_e.reshape(1, d, d)
    o0 = mm_nn(y0, w_out_e, F32, "out_proj_even", tm=1024, tn=1024)
    x1, h1 = postnorm_fwd(xs, o0, ln_post_even, ln_pre_odd_f, "post_even")
    g_wio, w_out_o = comm.weights("odd", after=x1)
    w_out_o = w_out_o.reshape(1, d, d)
    p1 = mm_nn(h1, g_wio, BF16, "in_proj_odd")
    y1, dc = odd_mix_fwd(p1, sconv_f, dconv_f, dconv_b_f, cnorm_g_f, cnorm_b_f, "odd_mix_fwd")
    o1 = mm_nn(y1, w_out_o, F32, "out_proj_odd", tm=1024, tn=1024)
    loss_blk, gx2, do1, dg_post_odd = final_fwd_bwd(x1, o1, ln_post_odd_f, tgt, "post_odd_loss")

    dw_out_o = mm_tn(y1, do1, 1, BF16, "dw_out_odd", tm=1024)
    dy1 = mm_nt(do1, w_out_o, BF16, "dy_odd", tn=1024)
    ddc, dg2, dgam, dbet = odd_bwd_ln(dy1, p1, dc, cnorm_g_f, cnorm_b_f, "odd_bwd_ln")
    dp1, dsconv, ddconv, ddconv_b = odd_bwd_conv(dy1, p1, ddc, dg2, sconv_f, dconv_f, "odd_bwd_conv")
    dw_in_o = mm_tn(h1, dp1, N_DEV, BF16, "dw_in_odd")
    dep = comm.reduce_begin({"w_out_odd": dw_out_o.reshape(N_DEV, d // N_DEV, d), "w_in_odd": dw_in_o}, "odd")
    dh1 = mm_nt(dp1, g_wio, F32, "dh_odd", dep=dep)
    dep = comm.reduce_send(after=dh1)
    gx1, dg_pre_odd, do0, dg_post_even = norm_bwd(dh1, x1, ln_pre_odd_f, gx2, "pre_odd_post_even_bwd",
                                                  inp2=o0, g2=ln_post_even, dep=dep)

    dw_out_e = mm_tn(y0, do0, 1, BF16, "dw_out_even", tm=1024)
    dy0 = mm_nt(do0, w_out_e, BF16, "dy_even", tn=1024)
    da0, du0, dg0, dpool, dpool_scale = even_mix_bwd(dy0, a0, p0, pool_full, pool_scale, "even_mix_bwd")
    pr = cwp // N_DEV
    dpool_slabs = dpool.astype(BF16).reshape(ng, N_DEV, pr, cwp).transpose(1, 0, 2, 3).reshape(N_DEV, ng * pr, cwp)
    dep = comm.reduce_begin({"w_out_even": dw_out_e.reshape(N_DEV, d // N_DEV, d), "pool_w": dpool_slabs}, "even_out")
    dq0, dk0, dv0 = sb_bwd(p0, a0, sb_wts, da0, n_heads, "sb_bwd", dep=dep)
    dep = comm.reduce_send(after=dq0)
    dp0 = jnp.concatenate([dq0, dk0, dv0, du0, dg0], axis=1)
    dw_in_e = mm_tn(h0, dp0, N_DEV, BF16, "dw_in_even", dep=dep)
    comm.reduce_begin({"w_in_even": dw_in_e}, "even_in")
    dep = comm.reduce_send(after=dw_in_e)
    dh0 = mm_nt(dp0, g_wie, F32, "dh_even", dep=dep)
    dep = None
    grad_x, dg_pre_even = norm_bwd(dh0, xs, ln_pre_even, gx1, "pre_even_bwd", dep=dep)
    small_g = [dg_pre_even, dpool_scale, dg_post_even, dg_pre_odd, dsconv, ddconv, ddconv_b, dgam, dbet, dg_post_odd]
    return loss_blk, grad_x, small_g


class _Exchanges:
    def __init__(self, dev, core, d):
        self.dev = dev.astype(jnp.int32).reshape(1)
        self.core = core
        self.chip = (dev // 2).astype(jnp.int32).reshape(1)
        self.d = d
        self.in_flight = {}
        self.to_sibling = None
        self.pending = []

    def start_weights(self, tag, blocks, afters):
        lands = [lax.empty((N_DEV,) + b.shape, b.dtype) for b in blocks]
        send, recv, srcs, lands, token = split_start("gather", blocks, lands, afters, "ag_start_" + tag)
        self.in_flight[tag] = (send, recv, srcs, lands)
        return token

    def weights(self, tag, after):
        send, recv, srcs, lands = self.in_flight.pop(tag)
        srcs, lands = split_wait("gather", send, recv, srcs, lands, [after], "ag_wait_" + tag)
        lands = [place_block(l, b, self.dev, "ag_own_%s_%d" % (tag, k)) for k, (l, b) in enumerate(zip(lands, srcs))]
        full = gather_finish([l.reshape((4, 2) + l.shape[1:]) for l in lands], "ag_finish_" + tag)
        return [f.reshape((N_DEV,) + f.shape[2:]) for f in full]

    def reduce_begin(self, partials, tag):
        names = list(partials)
        arrs = [partials[k].reshape((4, 2) + partials[k].shape[1:]) for k in names]
        lands = [lax.empty((4, 1) + a.shape[2:], a.dtype) for a in arrs]
        send, recv, srcs, lands, token = split_start("sibling", arrs, lands, [], "rs_sibling_start_" + tag)
        self.to_sibling = (tag, names, send, recv, srcs, lands)
        return token

    def reduce_send(self, after):
        tag, names, send, recv, srcs, lands = self.to_sibling
        srcs, lands = split_wait("sibling", send, recv, srcs, lands, [after], "rs_sibling_wait_" + tag)
        sums = [pair_add(o, r, self.core, "rs_pair_add_" + k) for k, o, r in zip(names, srcs, lands)]
        zones = [lax.empty(a.shape, a.dtype) for a in sums]
        send, recv, srcs, zones, token = split_start("scatter", sums, zones, [], "rs_start_" + tag)
        self.pending.append((tag, names, send, recv, srcs, zones))
        return token

    def finish_updates(self, big, afters):
        tag, names, send, recv, srcs, lands = self.pending.pop(0)
        srcs, lands = split_wait("scatter", send, recv, srcs, lands, afters, "rs_wait_" + tag)
        out = {}
        for name, own, got in zip(names, srcs, lands):
            w, m, v = big[name]
            shp = own.shape[1:]
            outs = adamw_big(w.reshape(shp), m.reshape(shp), v.reshape(shp), own, got, self.chip, "adamw_" + name)
            out[name] = [o.reshape(w.shape) for o in outs]
        return out


def _update_small(small_g, loss_blk, small_w, small_m, small_v, dev, d, cl, deps):
    packed = jnp.concatenate([_rows128(g) for g in small_g] + [loss_blk], axis=0)
    (g8,) = all_gather([packed], "ag_small_grads", deps)
    tot = sum_devices(g8, "sum_small_grads")
    loss = tot[packed.shape[0] - 8, 0]
    full_g = []
    lo = 0
    for g in small_g:
        rows = g.size // 128
        full_g.append(tot[lo:lo + rows].reshape(g.shape))
        lo += rows

    def mine(g, width):
        return lax.dynamic_slice_in_dim(g, dev * width, width, axis=g.ndim - 1)

    fg = full_g
    small_gl = [fg[0], fg[1], fg[2], mine(fg[3], d // N_DEV), mine(fg[4], cl), mine(fg[5], cl), mine(fg[6], cl),
                mine(fg[7], cl), mine(fg[8], cl), mine(fg[9], d // N_DEV)]
    sd, sm, sv = adamw_small(small_w, small_gl, small_m, small_v, "adamw_small")

    def like(k, a):
        return a[None] if k in (4, 5) else a

    sg = [like(k, a) for k, a in enumerate(small_gl)]
    sd = [like(k, a) for k, a in enumerate(sd)]
    sm = [like(k, a) for k, a in enumerate(sm)]
    sv = [like(k, a) for k, a in enumerate(sv)]
    return sg, sd, sm, sv, loss
```

```python
import functools
import math

import jax
import jax.numpy as jnp
from jax import lax
from jax.experimental import pallas as pl
from jax.experimental.pallas import tpu as pltpu

F32 = jnp.float32
BF16 = jnp.bfloat16
EPS = 1e-6
HEAD_DIM = 128
POOL_WINDOWS = (2, 4, 8, 16)
SCONV_K = 3
CONF_K = 31
HALO = 32
N_DEV = 8
VMEM_LIMIT = 56 * 1024 * 1024
MESH = pl.DeviceIdType.MESH

ADAM_LR = 0.001
ADAM_B1 = 0.9
ADAM_B2 = 0.999
ADAM_EPS = 1e-08
ADAM_WD = 0.01
ADAM_STEP = 10


def _params(*sem):
    return pltpu.CompilerParams(dimension_semantics=sem, vmem_limit_bytes=VMEM_LIMIT)


def _sigmoid(v):
    return 1.0 / (1.0 + jnp.exp(-v))


def _silu(v):
    return v * _sigmoid(v)


def _silu_and_grad(v):
    s = _sigmoid(v)
    return v * s, s * (1.0 + v * (1.0 - s))


def _rowsum8(v):
    r, c = v.shape
    return jnp.sum(v.reshape(r // 8, 8, c), axis=0)


def _tap_before(xx, i, rows):
    if i == 0:
        return xx[HALO:HALO + rows]
    return pltpu.roll(xx, i, 0)[HALO:HALO + rows]


def _tap_after(xx, i, rows):
    if i == 0:
        return xx[0:rows]
    return pltpu.roll(xx, xx.shape[0] - i, 0)[0:rows]


def rms_fwd(x, g, name, tm=256):
    s, d = x.shape

    def body(x_ref, g_ref, h_ref):
        xv = x_ref[...]
        r = lax.rsqrt(jnp.mean(xv * xv, axis=-1, keepdims=True) + EPS)
        h_ref[...] = (xv * r * g_ref[...]).astype(BF16)

    return pl.pallas_call(
        body, name=name, grid=(s // tm,),
        in_specs=[pl.BlockSpec((tm, d), lambda i: (i, 0)), pl.BlockSpec((1, d), lambda i: (0, 0))],
        out_specs=pl.BlockSpec((tm, d), lambda i: (i, 0)),
        out_shape=jax.ShapeDtypeStruct((s, d), BF16),
        compiler_params=_params("parallel"),
    )(x, g)


def postnorm_fwd(x, o, g, g_next, name, tm=256):
    s, d = x.shape

    def body(x_ref, o_ref, g_ref, gn_ref, y_ref, h_ref):
        ov = o_ref[...]
        r = lax.rsqrt(jnp.mean(ov * ov, axis=-1, keepdims=True) + EPS)
        y = x_ref[...] + ov * r * g_ref[...]
        y_ref[...] = y
        r2 = lax.rsqrt(jnp.mean(y * y, axis=-1, keepdims=True) + EPS)
        h_ref[...] = (y * r2 * gn_ref[...]).astype(BF16)

    row = pl.BlockSpec((tm, d), lambda i: (i, 0))
    vec = pl.BlockSpec((1, d), lambda i: (0, 0))
    return pl.pallas_call(
        body, name=name, grid=(s // tm,),
        in_specs=[row, row, vec, vec], out_specs=[row, row],
        out_shape=[jax.ShapeDtypeStruct((s, d), F32), jax.ShapeDtypeStruct((s, d), BF16)],
        compiler_params=_params("parallel"),
    )(x, o, g, g_next)


def final_fwd_bwd(x1, o, g, target, name, tm=256):
    s, d = x1.shape
    n = s // tm

    def body(x_ref, o_ref, g_ref, t_ref, loss_ref, gx_ref, do_ref, dg_ref, lacc, gacc):
        i = pl.program_id(0)

        @pl.when(i == 0)
        def _():
            lacc[...] = jnp.zeros_like(lacc)
            gacc[...] = jnp.zeros_like(gacc)

        ov = o_ref[...]
        gv = g_ref[...]
        r = lax.rsqrt(jnp.mean(ov * ov, axis=-1, keepdims=True) + EPS)
        oh = ov * r
        diff = x_ref[...] + oh * gv - t_ref[...]
        lacc[...] += _rowsum8(diff * diff)
        gx = diff * (1.0 / d)
        gx_ref[...] = gx
        gacc[...] += _rowsum8(gx * oh)
        dn = gx * gv
        do_ref[...] = (r * (dn - oh * jnp.mean(dn * oh, axis=-1, keepdims=True))).astype(BF16)

        @pl.when(i == n - 1)
        def _():
            tot = jnp.sum(jnp.sum(lacc[...], axis=0, keepdims=True), axis=1, keepdims=True)
            loss_ref[...] = jnp.broadcast_to(tot * (0.5 / d), loss_ref.shape)
            dg_ref[...] = jnp.sum(gacc[...], axis=0, keepdims=True)

    row = pl.BlockSpec((tm, d), lambda i: (i, 0))
    vec = pl.BlockSpec((1, d), lambda i: (0, 0))
    return pl.pallas_call(
        body, name=name, grid=(n,),
        in_specs=[row, row, vec, row],
        out_specs=[pl.BlockSpec((8, 128), lambda i: (0, 0)), row, row, vec],
        out_shape=[jax.ShapeDtypeStruct((8, 128), F32), jax.ShapeDtypeStruct((s, d), F32),
                   jax.ShapeDtypeStruct((s, d), BF16), jax.ShapeDtypeStruct((1, d), F32)],
        scratch_shapes=[pltpu.VMEM((8, d), F32), pltpu.VMEM((8, d), F32)],
        compiler_params=_params("arbitrary"),
    )(x1, o, g, target)


def _rms_bwd_rows(dyv, xv, gv):
    r = lax.rsqrt(jnp.mean(xv * xv, axis=-1, keepdims=True) + EPS)
    xh = xv * r
    dn = dyv * gv
    return r * (dn - xh * jnp.mean(dn * xh, axis=-1, keepdims=True)), _rowsum8(dyv * xh)


def norm_bwd(dy, inp, g, resid, name, inp2=None, g2=None, tm=256, dep=None):
    s, d = inp.shape
    n = s // tm
    chain = inp2 is not None

    def body(*refs):
        dy_ref, x_ref, g_ref, r_ref = refs[:4]
        outs = refs[-6:] if chain else refs[-3:]
        i = pl.program_id(0)

        @pl.when(i == 0)
        def _():
            for acc in outs[-2:] if chain else outs[-1:]:
                acc[...] = jnp.zeros_like(acc)

        if chain:
            x2_ref, g2_ref = refs[4:6]
            dx_ref, dg_ref, dx2_ref, dg2_ref, gacc, gacc2 = outs
        else:
            dx_ref, dg_ref, gacc = outs
        dx, dg_rows = _rms_bwd_rows(dy_ref[...].astype(F32), x_ref[...], g_ref[...])
        dx = dx + r_ref[...]
        dx_ref[...] = dx
        gacc[...] += dg_rows
        if chain:
            dx2, dg2_rows = _rms_bwd_rows(dx, x2_ref[...], g2_ref[...])
            dx2_ref[...] = dx2.astype(BF16)
            gacc2[...] += dg2_rows

        @pl.when(i == n - 1)
        def _():
            dg_ref[...] = jnp.sum(gacc[...], axis=0, keepdims=True)
            if chain:
                dg2_ref[...] = jnp.sum(gacc2[...], axis=0, keepdims=True)

    row = pl.BlockSpec((tm, d), lambda i: (i, 0))
    vec = pl.BlockSpec((1, d), lambda i: (0, 0))
    dep_args, dep_specs = _after(dep)
    extra = [inp2, g2] if chain else []
    return pl.pallas_call(
        body, name=name, grid=(n,),
        in_specs=[row, row, vec, row] + ([row, vec] if chain else []) + dep_specs,
        out_specs=[row, vec] * (2 if chain else 1),
        out_shape=[jax.ShapeDtypeStruct((s, d), F32), jax.ShapeDtypeStruct((1, d), F32)]
        + ([jax.ShapeDtypeStruct((s, d), BF16), jax.ShapeDtypeStruct((1, d), F32)] if chain else []),
        scratch_shapes=[pltpu.VMEM((8, d), F32)] * (2 if chain else 1),
        compiler_params=_params("arbitrary"),
    )(dy, inp, g, resid, *extra, *dep_args)


def _after(dep):
    if dep is None:
        return [], []
    return [dep], [pl.BlockSpec((8, 128), lambda *_: (0, 0))]


def mm_nn(a, w, out_dtype, name, tm=2048, tn=None, dep=None):
    m, k = a.shape
    tm = min(tm, m)
    ns, _, n = w.shape
    tn = n if tn is None else tn
    nj = n // tn
    dep_args, dep_specs = _after(dep)

    def body(a_ref, w_ref, *rest):
        o_ref = rest[-1]
        o_ref[...] = jnp.dot(a_ref[...], w_ref[0], preferred_element_type=F32).astype(out_dtype)

    return pl.pallas_call(
        body, name=name, grid=(ns, nj, m // tm),
        in_specs=[pl.BlockSpec((tm, k), lambda s, j, i: (i, 0)),
                  pl.BlockSpec((1, k, tn), lambda s, j, i: (s, 0, j))] + dep_specs,
        out_specs=pl.BlockSpec((tm, tn), lambda s, j, i: (i, s * nj + j)),
        out_shape=jax.ShapeDtypeStruct((m, ns * n), out_dtype),
        compiler_params=_params("parallel", "parallel", "parallel"),
    )(a, w, *dep_args)


def mm_nt(a, w, out_dtype, name, tm=1024, tn=None, dep=None):
    m = a.shape[0]
    tm = min(tm, m)
    ns, k, n = w.shape
    tn = n if tn is None else tn
    nj = n // tn
    steps = ns * nj
    dep_args, dep_specs = _after(dep)

    def body(a_ref, w_ref, *rest):
        o_ref, acc = rest[-2:]
        r = pl.program_id(1)

        @pl.when(r == 0)
        def _():
            acc[...] = jnp.zeros_like(acc)

        acc[...] += lax.dot_general(a_ref[...], w_ref[0], (((1,), (1,)), ((), ())),
                                    preferred_element_type=F32)

        @pl.when(r == steps - 1)
        def _():
            o_ref[...] = acc[...].astype(out_dtype)

    return pl.pallas_call(
        body, name=name, grid=(m // tm, steps),
        in_specs=[pl.BlockSpec((tm, tn), lambda i, r: (i, r)),
                  pl.BlockSpec((1, k, tn), lambda i, r: (r // nj, 0, r % nj))] + dep_specs,
        out_specs=pl.BlockSpec((tm, k), lambda i, r: (i, 0)),
        out_shape=jax.ShapeDtypeStruct((m, k), out_dtype),
        scratch_shapes=[pltpu.VMEM((tm, k), F32)],
        compiler_params=_params("parallel", "arbitrary"),
    )(a, w, *dep_args)


def mm_tn(a, b, ns, out_dtype, name, tk=1024, tm=2048, dep=None):
    m, k = a.shape
    tm = min(tm, m)
    n = b.shape[1] // ns
    steps = m // tm
    dep_args, dep_specs = _after(dep)

    def body(a_ref, b_ref, *rest):
        o_ref, acc = rest[-2:]
        r = pl.program_id(2)

        @pl.when(r == 0)
        def _():
            acc[...] = jnp.zeros_like(acc)

        acc[...] += lax.dot_general(a_ref[...], b_ref[...], (((0,), (0,)), ((), ())),
                                    preferred_element_type=F32)

        @pl.when(r == steps - 1)
        def _():
            o_ref[0] = acc[...].astype(out_dtype)

    return pl.pallas_call(
        body, name=name, grid=(ns, k // tk, steps),
        in_specs=[pl.BlockSpec((tm, tk), lambda s, j, r: (r, j)),
                  pl.BlockSpec((tm, n), lambda s, j, r: (r, s))] + dep_specs,
        out_specs=pl.BlockSpec((1, tk, n), lambda s, j, r: (s, j, 0)),
        out_shape=jax.ShapeDtypeStruct((ns, k, n), out_dtype),
        scratch_shapes=[pltpu.VMEM((tk, n), F32)],
        compiler_params=_params("parallel", "parallel", "arbitrary"),
    )(a, b, *dep_args)


SB_BLK = 128


LOG2E = 1.0 / math.log(2.0)


def _split_dot(v, tri):
    hi = pltpu.bitcast(pltpu.bitcast(v, jnp.uint32) & jnp.uint32(0xFFFF0000), F32)
    lo = (v - hi).astype(BF16)
    return (jnp.dot(hi.astype(BF16), tri, preferred_element_type=F32)
            + jnp.dot(lo, tri, preferred_element_type=F32))


def _sb_scores(z2, lim, dcol, tri_ex, masked):
    sp = jnp.log2(1.0 + jnp.exp2(-jnp.abs(z2)))
    lb = jnp.minimum(z2, 0.0) - sp
    l1m = lb - z2
    mask = None
    if masked:
        mask = dcol < lim
        l1m = jnp.where(mask, l1m, 0.0)
    return mask, lb, l1m, _split_dot(l1m, tri_ex)


def _sb_consts():
    row = lax.broadcasted_iota(jnp.int32, (SB_BLK, SB_BLK), 0)
    col = lax.broadcasted_iota(jnp.int32, (SB_BLK, SB_BLK), 1)
    tri_ex = jnp.where(row > col, 1.0, 0.0).astype(BF16)
    tri_in = jnp.where(row >= col, 1.0, 0.0).astype(BF16)
    return col - row, tri_ex, tri_in


def sb_fwd(p, n_heads, name, tq=256, nsub=4, dep=None):
    s = p.shape[0]
    h_n = n_heads
    b = SB_BLK
    nqs = tq // b
    tk = nsub * b
    scale = 1.0 / math.sqrt(HEAD_DIM)

    dep_args, dep_specs = _after(dep)

    def body(q_ref, k_ref, v_ref, *rest):
        o_ref, w_ref = rest[-2:]
        qi = pl.program_id(1)
        dcol, tri_ex, _ = _sb_consts()
        qv = [q_ref[qs * b:(qs + 1) * b, :] for qs in range(nqs)]
        n_groups = ((qi + 1) * nqs - 1) // nsub + 1

        def step(it, carry, masked):
            c1s, accs = carry
            g = n_groups - 1 - it
            off = pl.multiple_of(g * tk, tk)
            kg = k_ref[pl.ds(off, tk), :]
            vg = v_ref[pl.ds(off, tk), :]
            new_c1, new_acc = [], []
            for qs in range(nqs):
                qb = qi * nqs + qs
                z2 = lax.dot_general(qv[qs], kg, (((1,), (1,)), ((), ())),
                                     preferred_element_type=F32) * (scale * LOG2E)
                blocks = [_sb_scores(z2[:, j * b:(j + 1) * b], (qb - (g * nsub + j)) * b, dcol, tri_ex, masked)
                          for j in range(nsub)]
                run = c1s[qs]
                ws = [None] * nsub
                for j in reversed(range(nsub)):
                    mask, lb, l1m, ls_loc = blocks[j]
                    wj = jnp.exp2(lb + ls_loc + run)
                    ws[j] = (jnp.where(mask, wj, 0.0) if masked else wj).astype(BF16)
                    run = run + jnp.sum(l1m, axis=1, keepdims=True)
                w = jnp.concatenate(ws, axis=1)
                w_ref[0, g, qs * b:(qs + 1) * b, :] = w
                new_acc.append(accs[qs] + jnp.dot(w, vg, preferred_element_type=F32))
                new_c1.append(run)
            return tuple(new_c1), tuple(new_acc)

        init = (tuple(jnp.zeros((b, 1), F32) for _ in range(nqs)),
                tuple(jnp.zeros((b, HEAD_DIM), F32) for _ in range(nqs)))
        assert nqs == 2 and nsub % 2 == 0
        first = step(0, init, True)
        _, accs = lax.fori_loop(1, n_groups, functools.partial(step, masked=False), first)
        for qs in range(nqs):
            o_ref[qs * b:(qs + 1) * b, :] = accs[qs]

    return pl.pallas_call(
        body, name=name, grid=(h_n, s // tq),
        in_specs=[pl.BlockSpec((tq, HEAD_DIM), lambda h, i: (i, h)),
                  pl.BlockSpec((s, HEAD_DIM), lambda h, i: (0, h_n + h)),
                  pl.BlockSpec((s, HEAD_DIM), lambda h, i: (0, 2 * h_n + h))] + dep_specs,
        out_specs=[pl.BlockSpec((tq, HEAD_DIM), lambda h, i: (i, h)),
                   pl.BlockSpec((1, s // tk, tq, tk), lambda h, i: (h, 0, i, 0))],
        out_shape=[jax.ShapeDtypeStruct((s, h_n * HEAD_DIM), F32),
                   jax.ShapeDtypeStruct((h_n, s // tk, s, tk), BF16)],
        compiler_params=_params("parallel", "arbitrary"),
    )(p, p, p, *dep_args)


def sb_bwd(p, a, wts, da, n_heads, name, tq=256, dep=None):
    s = p.shape[0]
    h_n = n_heads
    nq = s // tq
    b = SB_BLK
    nqs = tq // b
    tk = wts.shape[3]
    nsub = tk // b
    scale = 1.0 / math.sqrt(HEAD_DIM)
    dep_args, dep_specs = _after(dep)

    def body(q_ref, k_ref, v_ref, a_ref, da_ref, w_ref, *rest):
        dq_ref, dk_ref, dv_ref, dk_acc, dv_acc = rest[-5:]
        qi = pl.program_id(1)

        @pl.when(qi == 0)
        def _():
            dk_acc[...] = jnp.zeros_like(dk_acc)
            dv_acc[...] = jnp.zeros_like(dv_acc)

        dcol, _, tri_in = _sb_consts()
        q_all = q_ref[...]
        do_all = da_ref[...]
        qv = [q_ref[qs * b:(qs + 1) * b, :] for qs in range(nqs)]
        dov = [da_ref[qs * b:(qs + 1) * b, :] for qs in range(nqs)]
        tots = [jnp.sum(dov[qs].astype(F32) * a_ref[qs * b:(qs + 1) * b, :], axis=1, keepdims=True)
                for qs in range(nqs)]
        n_groups = ((qi + 1) * nqs - 1) // nsub + 1

        def step(it, carry, masked):
            c2s, dqs = carry
            g = n_groups - 1 - it
            off = pl.multiple_of(g * tk, tk)
            kg = k_ref[pl.ds(off, tk), :]
            vg = v_ref[pl.ds(off, tk), :]
            w_all = w_ref[0, g]
            new_c2, new_dq, dz_rows = [], [], []
            for qs in range(nqs):
                qb = qi * nqs + qs
                z2 = lax.dot_general(qv[qs], kg, (((1,), (1,)), ((), ())),
                                     preferred_element_type=F32) * (-scale * LOG2E)
                dw = lax.dot_general(dov[qs], vg, (((1,), (1,)), ((), ())), preferred_element_type=F32)
                beta = 1.0 / (1.0 + jnp.exp2(z2))
                e = dw * w_all[qs * b:(qs + 1) * b, :].astype(F32)
                run2 = c2s[qs]
                dzs = [None] * nsub
                for j in reversed(range(nsub)):
                    cols = slice(j * b, (j + 1) * b)
                    later = _split_dot(e[:, cols], tri_in) + run2
                    bj = beta[:, cols]
                    dz = (e[:, cols] * (1.0 - bj) - bj * (tots[qs] - later)) * scale
                    if masked:
                        dz = jnp.where(dcol < (qb - (g * nsub + j)) * b, dz, 0.0)
                    dzs[j] = dz.astype(BF16)
                    run2 = run2 + jnp.sum(e[:, cols], axis=1, keepdims=True)
                dzq = jnp.concatenate(dzs, axis=1)
                new_dq.append(dqs[qs] + jnp.dot(dzq, kg, preferred_element_type=F32))
                new_c2.append(run2)
                dz_rows.append(dzq)
            dz_all = jnp.concatenate(dz_rows, axis=0)
            dk_acc[pl.ds(off, tk), :] += lax.dot_general(dz_all, q_all, (((0,), (0,)), ((), ())),
                                                         preferred_element_type=F32)
            dv_acc[pl.ds(off, tk), :] += lax.dot_general(w_all, do_all, (((0,), (0,)), ((), ())),
                                                         preferred_element_type=F32)
            return tuple(new_c2), tuple(new_dq)

        zeros = tuple(jnp.zeros((b, 1), F32) for _ in range(nqs))
        assert nqs == 2 and nsub % 2 == 0
        first = step(0, (zeros, tuple(jnp.zeros((b, HEAD_DIM), F32) for _ in range(nqs))), True)
        _, dqs = lax.fori_loop(1, n_groups, functools.partial(step, masked=False), first)
        for qs in range(nqs):
            dq_ref[qs * b:(qs + 1) * b, :] = dqs[qs].astype(BF16)

        @pl.when(qi == nq - 1)
        def _():
            dk_ref[...] = dk_acc[...].astype(BF16)
            dv_ref[...] = dv_acc[...].astype(BF16)

    blk = pl.BlockSpec((tq, HEAD_DIM), lambda h, i: (i, h))
    full = pl.BlockSpec((s, HEAD_DIM), lambda h, i: (0, h))
    return pl.pallas_call(
        body, name=name, grid=(h_n, nq),
        in_specs=[blk, pl.BlockSpec((s, HEAD_DIM), lambda h, i: (0, h_n + h)),
                  pl.BlockSpec((s, HEAD_DIM), lambda h, i: (0, 2 * h_n + h)), blk, blk,
                  pl.BlockSpec((1, s // tk, tq, tk), lambda h, i: (h, 0, i, 0))] + dep_specs,
        out_specs=[blk, full, full],
        out_shape=[jax.ShapeDtypeStruct((s, h_n * HEAD_DIM), BF16)] * 3,
        scratch_shapes=[pltpu.VMEM((s, HEAD_DIM), F32), pltpu.VMEM((s, HEAD_DIM), F32)],
        compiler_params=_params("parallel", "arbitrary"),
    )(p, p, p, a, da, wts, *dep_args)


def _pool_window(xx, win, r0, rc):
    cur = xx[HALO:HALO + rc]
    ws = cur
    for i in range(1, win):
        ws = ws + _tap_before(xx, i, rc)
    t_idx = r0 + lax.broadcasted_iota(jnp.int32, (rc, 1), 0)
    inv = 1.0 / jnp.minimum(win, t_idx + 1).astype(F32)
    return ws * inv - cur, inv


def even_mix_fwd(a, p, pool_w, pool_scale, name, rc=64):
    s = p.shape[0]
    ng = len(POOL_WINDOWS)
    cw = pool_w.shape[1]
    n_chunks = s // rc

    def body(a_ref, u_ref, g_ref, w_ref, sc_ref, y_ref, upad):
        j = pl.program_id(0)

        @pl.when(j < ng)
        def _():
            def chunk(ci, carry):
                rows = pl.ds(pl.multiple_of(ci * rc, rc), rc)
                y_ref[rows, :] = (a_ref[rows, :] * _silu(g_ref[rows, :].astype(F32))).astype(BF16)
                return carry

            lax.fori_loop(0, n_chunks, chunk, 0)

        for gi, win in enumerate(POOL_WINDOWS):
            @pl.when(j == ng + gi)
            def _(win=win):
                upad[0:HALO, :] = jnp.zeros((HALO, cw), F32)

                def fill(ci, carry):
                    r0 = pl.multiple_of(ci * rc, rc)
                    upad[pl.ds(pl.multiple_of(r0 + HALO, HALO), rc), :] = u_ref[pl.ds(r0, rc), :].astype(F32)
                    return carry

                lax.fori_loop(0, n_chunks, fill, 0)

                def chunk(ci, carry):
                    r0 = pl.multiple_of(ci * rc, rc)
                    rows = pl.ds(r0, rc)
                    pooled, _ = _pool_window(upad[pl.ds(r0, HALO + rc), :], win, r0, rc)
                    t = jnp.dot(pooled.astype(BF16), w_ref[0], preferred_element_type=F32)
                    y_ref[rows, :] = (t * sc_ref[...] * _silu(g_ref[rows, :].astype(F32))).astype(BF16)
                    return carry

                lax.fori_loop(0, n_chunks, chunk, 0)

    grp = lambda j: jnp.maximum(j - ng, 0)
    return pl.pallas_call(
        body, name=name, grid=(2 * ng,),
        in_specs=[pl.BlockSpec((s, cw), lambda j: (0, jnp.minimum(j, ng - 1))),
                  pl.BlockSpec((s, cw), lambda j: (0, 3 * ng + grp(j))),
                  pl.BlockSpec((s, cw), lambda j: (0, 4 * ng + j)),
                  pl.BlockSpec((1, cw, cw), lambda j: (grp(j), 0, 0)),
                  pl.BlockSpec((1, cw), lambda j: (0, grp(j)))],
        out_specs=pl.BlockSpec((s, cw), lambda j: (0, j)),
        out_shape=jax.ShapeDtypeStruct((s, 2 * ng * cw), BF16),
        scratch_shapes=[pltpu.VMEM((HALO + s, cw), F32)],
        compiler_params=_params("arbitrary"),
    )(a, p, p, pool_w, pool_scale)


def even_mix_bwd(dy, a, p, pool_w, pool_scale, name, rc=64):
    s = p.shape[0]
    ng = len(POOL_WINDOWS)
    cw = pool_w.shape[1]
    n_chunks = s // rc

    def body(dy_ref, a_ref, u_ref, g_ref, w_ref, sc_ref, da_ref, du_ref, dg_ref, dw_ref, dsc_ref,
             upad, rpad, dpl, dw_acc, dsc_acc):
        j = pl.program_id(0)

        @pl.when(j < ng)
        def _():
            def chunk(ci, carry):
                rows = pl.ds(pl.multiple_of(ci * rc, rc), rc)
                dyv = dy_ref[rows, :].astype(F32)
                sg, dsg = _silu_and_grad(g_ref[rows, :].astype(F32))
                da_ref[rows, :] = (dyv * sg).astype(BF16)
                dg_ref[rows, :] = (dyv * a_ref[rows, :] * dsg).astype(BF16)
                return carry

            lax.fori_loop(0, n_chunks, chunk, 0)

        for gi, win in enumerate(POOL_WINDOWS):
            @pl.when(j == ng + gi)
            def _(win=win):
                upad[0:HALO, :] = jnp.zeros((HALO, cw), F32)
                rpad[s:s + HALO, :] = jnp.zeros((HALO, cw), F32)
                dw_acc[...] = jnp.zeros_like(dw_acc)
                dsc_acc[...] = jnp.zeros_like(dsc_acc)

                def fill(ci, carry):
                    r0 = pl.multiple_of(ci * rc, rc)
                    upad[pl.ds(pl.multiple_of(r0 + HALO, HALO), rc), :] = u_ref[pl.ds(r0, rc), :].astype(F32)
                    return carry

                lax.fori_loop(0, n_chunks, fill, 0)

                def chunk(ci, carry):
                    r0 = pl.multiple_of(ci * rc, rc)
                    rows = pl.ds(r0, rc)
                    pooled, inv = _pool_window(upad[pl.ds(r0, HALO + rc), :], win, r0, rc)
                    pb = pooled.astype(BF16)
                    wv = w_ref[0]
                    t = jnp.dot(pb, wv, preferred_element_type=F32)
                    scv = sc_ref[...]
                    dyv = dy_ref[rows, :].astype(F32)
                    sg, dsg = _silu_and_grad(g_ref[rows, :].astype(F32))
                    dpo = dyv * sg
                    dg_ref[rows, :] = (dyv * t * scv * dsg).astype(BF16)
                    dsc_acc[...] += _rowsum8(dpo * t)
                    dtb = (dpo * scv).astype(BF16)
                    dw_acc[...] += lax.dot_general(pb, dtb, (((0,), (0,)), ((), ())),
                                                   preferred_element_type=F32)
                    dpooled = lax.dot_general(dtb, wv, (((1,), (1,)), ((), ())),
                                              preferred_element_type=F32)
                    dpl[rows, :] = dpooled
                    rpad[rows, :] = dpooled * inv
                    return carry

                lax.fori_loop(0, n_chunks, chunk, 0)

                def chunk2(ci, carry):
                    r0 = pl.multiple_of(ci * rc, rc)
                    rows = pl.ds(r0, rc)
                    xx = rpad[pl.ds(r0, rc + HALO), :]
                    fs = xx[0:rc]
                    for i in range(1, win):
                        fs = fs + _tap_after(xx, i, rc)
                    du_ref[rows, :] = (fs - dpl[rows, :]).astype(BF16)
                    return carry

                lax.fori_loop(0, n_chunks, chunk2, 0)
                dw_ref[0] = dw_acc[...]
                dsc_ref[...] = jnp.sum(dsc_acc[...], axis=0, keepdims=True)

    grp = lambda j: jnp.maximum(j - ng, 0)
    att = lambda j: jnp.minimum(j, ng - 1)
    return pl.pallas_call(
        body, name=name, grid=(2 * ng,),
        in_specs=[pl.BlockSpec((s, cw), lambda j: (0, j)),
                  pl.BlockSpec((s, cw), lambda j: (0, att(j))),
                  pl.BlockSpec((s, cw), lambda j: (0, 3 * ng + grp(j))),
                  pl.BlockSpec((s, cw), lambda j: (0, 4 * ng + j)),
                  pl.BlockSpec((1, cw, cw), lambda j: (grp(j), 0, 0)),
                  pl.BlockSpec((1, cw), lambda j: (0, grp(j)))],
        out_specs=[pl.BlockSpec((s, cw), lambda j: (0, att(j))),
                   pl.BlockSpec((s, cw), lambda j: (0, grp(j))),
                   pl.BlockSpec((s, cw), lambda j: (0, j)),
                   pl.BlockSpec((1, cw, cw), lambda j: (grp(j), 0, 0)),
                   pl.BlockSpec((1, cw), lambda j: (0, grp(j)))],
        out_shape=[jax.ShapeDtypeStruct((s, ng * cw), BF16), jax.ShapeDtypeStruct((s, ng * cw), BF16),
                   jax.ShapeDtypeStruct((s, 2 * ng * cw), BF16),
                   jax.ShapeDtypeStruct((ng, cw, cw), F32), jax.ShapeDtypeStruct((1, ng * cw), F32)],
        scratch_shapes=[pltpu.VMEM((HALO + s, cw), F32), pltpu.VMEM((s + HALO, cw), F32),
                        pltpu.VMEM((s, cw), F32), pltpu.VMEM((cw, cw), F32), pltpu.VMEM((8, cw), F32)],
        compiler_params=_params("arbitrary"),
    )(dy, a, p, p, pool_w, pool_scale)


def _halo_before(tm):
    return lambda i: jnp.maximum(i * (tm // HALO) - 1, 0)


def _halo_after(tm, s):
    return lambda i: jnp.minimum((i + 1) * (tm // HALO), s // HALO - 1)


def odd_mix_fwd(p, sconv_w, dconv_w, dconv_b, cnorm_g, cnorm_b, name, tm=128):
    s = p.shape[0]
    cw = sconv_w.shape[1]
    n = s // tm
    lanes = 128
    hb = _halo_before(tm)

    def body(hc_ref, hch_ref, bc_ref, cc_ref, cch_ref, ga_ref, gah_ref, gb_ref, gbh_ref, g1_ref, g2_ref,
             sw_ref, dw_ref, db_ref, gam_ref, bet_ref, y_ref, dc_ref):
        first = pl.program_id(0) == 0
        for l in range(cw // lanes):
            cols = slice(l * lanes, (l + 1) * lanes)
            mh = jnp.where(first, 0.0, cch_ref[:, cols].astype(F32) * hch_ref[:, cols].astype(F32))
            mm = cc_ref[:, cols].astype(F32) * hc_ref[:, cols].astype(F32)
            xx = jnp.concatenate([mh, mm], axis=0)
            cv = jnp.zeros((tm, lanes), F32)
            for k in range(SCONV_K):
                cv = cv + sw_ref[k:k + 1, cols] * _tap_before(xx, SCONV_K - 1 - k, tm)
            c_out = bc_ref[:, cols].astype(F32) * cv
            y_ref[:, cols] = (c_out * _silu(g1_ref[:, cols].astype(F32))).astype(BF16)
            dh = jnp.where(first, 0.0, gah_ref[:, cols].astype(F32) * _sigmoid(gbh_ref[:, cols].astype(F32)))
            dm = ga_ref[:, cols].astype(F32) * _sigmoid(gb_ref[:, cols].astype(F32))
            xx = jnp.concatenate([dh, dm], axis=0)
            acc = jnp.zeros((tm, lanes), F32) + db_ref[:, cols]
            for k in range(CONF_K):
                acc = acc + dw_ref[k:k + 1, cols] * _tap_before(xx, CONF_K - 1 - k, tm)
            dc_ref[:, cols] = acc
        rs = 32
        for r in range(tm // rs):
            rows = slice(r * rs, (r + 1) * rs)
            xv = dc_ref[rows, :]
            mu = jnp.mean(xv, axis=-1, keepdims=True)
            xc = xv - mu
            rstd = lax.rsqrt(jnp.mean(xc * xc, axis=-1, keepdims=True) + EPS)
            ln = xc * rstd * gam_ref[...] + bet_ref[...]
            y_ref[rows, cw:2 * cw] = (_silu(ln) * _silu(g2_ref[rows, :].astype(F32))).astype(BF16)

    main = lambda c: pl.BlockSpec((tm, cw), lambda i: (i, c))
    halo = lambda c: pl.BlockSpec((HALO, cw), lambda i: (hb(i), c))
    vec = lambda r: pl.BlockSpec((r, cw), lambda i: (0, 0))
    return pl.pallas_call(
        body, name=name, grid=(n,),
        in_specs=[main(0), halo(0), main(1), main(2), halo(2), main(3), halo(3), main(4), halo(4),
                  main(5), main(6), vec(SCONV_K), vec(CONF_K), vec(1), vec(1), vec(1)],
        out_specs=[pl.BlockSpec((tm, 2 * cw), lambda i: (i, 0)), pl.BlockSpec((tm, cw), lambda i: (i, 0))],
        out_shape=[jax.ShapeDtypeStruct((s, 2 * cw), BF16), jax.ShapeDtypeStruct((s, cw), F32)],
        compiler_params=_params("parallel"),
    )(p, p, p, p, p, p, p, p, p, p, p, sconv_w, dconv_w, dconv_b, cnorm_g, cnorm_b)


def odd_bwd_ln(dy, p, dc, cnorm_g, cnorm_b, name, tm=256):
    s = p.shape[0]
    cw = dc.shape[1]
    n = s // tm
    rs = 32

    def body(dy_ref, g2_ref, dc_ref, gam_ref, bet_ref, ddc_ref, dg_ref, dgam_ref, dbet_ref, gacc, bacc):
        i = pl.program_id(0)

        @pl.when(i == 0)
        def _():
            gacc[...] = jnp.zeros_like(gacc)
            bacc[...] = jnp.zeros_like(bacc)

        def chunk(ci, carry):
            rows = pl.ds(pl.multiple_of(ci * rs, rs), rs)
            xv = dc_ref[rows, :]
            mu = jnp.mean(xv, axis=-1, keepdims=True)
            xc = xv - mu
            rstd = lax.rsqrt(jnp.mean(xc * xc, axis=-1, keepdims=True) + EPS)
            xh = xc * rstd
            gam = gam_ref[...]
            sl, dsl = _silu_and_grad(xh * gam + bet_ref[...])
            sg, dsg = _silu_and_grad(g2_ref[rows, :].astype(F32))
            dyv = dy_ref[rows, :].astype(F32)
            dg_ref[rows, :] = (dyv * sl * dsg).astype(BF16)
            dln = dyv * sg * dsl
            gacc[...] += _rowsum8(dln * xh)
            bacc[...] += _rowsum8(dln)
            dxh = dln * gam
            ddc_ref[rows, :] = rstd * (dxh - jnp.mean(dxh, axis=-1, keepdims=True)
                                       - xh * jnp.mean(dxh * xh, axis=-1, keepdims=True))
            return carry

        lax.fori_loop(0, tm // rs, chunk, 0)

        @pl.when(i == n - 1)
        def _():
            dgam_ref[...] = jnp.sum(gacc[...], axis=0, keepdims=True)
            dbet_ref[...] = jnp.sum(bacc[...], axis=0, keepdims=True)

    vec = pl.BlockSpec((1, cw), lambda i: (0, 0))
    return pl.pallas_call(
        body, name=name, grid=(n,),
        in_specs=[pl.BlockSpec((tm, cw), lambda i: (i, 1)), pl.BlockSpec((tm, cw), lambda i: (i, 6)),
                  pl.BlockSpec((tm, cw), lambda i: (i, 0)), vec, vec],
        out_specs=[pl.BlockSpec((tm, cw), lambda i: (i, 0)), pl.BlockSpec((tm, cw), lambda i: (i, 0)), vec, vec],
        out_shape=[jax.ShapeDtypeStruct((s, cw), F32), jax.ShapeDtypeStruct((s, cw), BF16),
                   jax.ShapeDtypeStruct((1, cw), F32), jax.ShapeDtypeStruct((1, cw), F32)],
        scratch_shapes=[pltpu.VMEM((8, cw), F32), pltpu.VMEM((8, cw), F32)],
        compiler_params=_params("arbitrary"),
    )(dy, p, dc, cnorm_g, cnorm_b)


def odd_bwd_conv(dy, p, ddc, dg2, sconv_w, dconv_w, name, tm=128):
    s = p.shape[0]
    cw = ddc.shape[1]
    n = s // tm
    lanes = 128
    hb = _halo_before(tm)
    ha = _halo_after(tm, s)

    def body(dy_ref, dya_ref, g1_ref, g1a_ref, bc_ref, bca_ref, hc_ref, hch_ref, cc_ref, cch_ref,
             ddc_ref, ddca_ref, ga_ref, gah_ref, gb_ref, gbh_ref, dg2_ref, sw_ref, dw_ref,
             dp_ref, dsw_ref, ddw_ref, ddb_ref, sw_acc, dw_acc, db_acc):
        i = pl.program_id(0)
        first = i == 0
        last = i == n - 1

        @pl.when(first)
        def _():
            sw_acc[...] = jnp.zeros_like(sw_acc)
            dw_acc[...] = jnp.zeros_like(dw_acc)
            db_acc[...] = jnp.zeros_like(db_acc)

        for l in range(cw // lanes):
            cols = slice(l * lanes, (l + 1) * lanes)
            mh = jnp.where(first, 0.0, cch_ref[:, cols].astype(F32) * hch_ref[:, cols].astype(F32))
            hcv = hc_ref[:, cols].astype(F32)
            ccv = cc_ref[:, cols].astype(F32)
            xx = jnp.concatenate([mh, ccv * hcv], axis=0)
            taps = [_tap_before(xx, SCONV_K - 1 - k, tm) for k in range(SCONV_K)]
            cv = jnp.zeros((tm, lanes), F32)
            for k in range(SCONV_K):
                cv = cv + sw_ref[k:k + 1, cols] * taps[k]
            bcv = bc_ref[:, cols].astype(F32)
            dyv = dy_ref[:, cols].astype(F32)
            sg, dsg = _silu_and_grad(g1_ref[:, cols].astype(F32))
            dco = dyv * sg
            dp_ref[:, 5 * cw + l * lanes:5 * cw + (l + 1) * lanes] = (dyv * bcv * cv * dsg).astype(BF16)
            dp_ref[:, cw + l * lanes:cw + (l + 1) * lanes] = (dco * cv).astype(BF16)
            dcv = dco * bcv
            for k in range(SCONV_K):
                sw_acc[k * 8:(k + 1) * 8, cols] += _rowsum8(dcv * taps[k])
            dcv_a = jnp.where(last, 0.0, dya_ref[:, cols].astype(F32) * _silu(g1a_ref[:, cols].astype(F32))
                              * bca_ref[:, cols].astype(F32))
            xx = jnp.concatenate([dcv, dcv_a], axis=0)
            dm = jnp.zeros((tm, lanes), F32)
            for k in range(SCONV_K):
                dm = dm + sw_ref[k:k + 1, cols] * _tap_after(xx, SCONV_K - 1 - k, tm)
            dp_ref[:, l * lanes:(l + 1) * lanes] = (dm * ccv).astype(BF16)
            dp_ref[:, 2 * cw + l * lanes:2 * cw + (l + 1) * lanes] = (dm * hcv).astype(BF16)
            gav = ga_ref[:, cols].astype(F32)
            sb = _sigmoid(gb_ref[:, cols].astype(F32))
            dh = jnp.where(first, 0.0, gah_ref[:, cols].astype(F32) * _sigmoid(gbh_ref[:, cols].astype(F32)))
            xx = jnp.concatenate([dh, gav * sb], axis=0)
            ddcv = ddc_ref[:, cols]
            db_acc[:, cols] += _rowsum8(ddcv)
            for k in range(CONF_K):
                dw_acc[k * 8:(k + 1) * 8, cols] += _rowsum8(ddcv * _tap_before(xx, CONF_K - 1 - k, tm))
            ddc_a = jnp.where(last, 0.0, ddca_ref[:, cols])
            xx = jnp.concatenate([ddcv, ddc_a], axis=0)
            dgl = jnp.zeros((tm, lanes), F32)
            for k in range(CONF_K):
                dgl = dgl + dw_ref[k:k + 1, cols] * _tap_after(xx, CONF_K - 1 - k, tm)
            dp_ref[:, 3 * cw + l * lanes:3 * cw + (l + 1) * lanes] = (dgl * sb).astype(BF16)
            dp_ref[:, 4 * cw + l * lanes:4 * cw + (l + 1) * lanes] = (dgl * gav * sb * (1.0 - sb)).astype(BF16)
        dp_ref[:, 6 * cw:7 * cw] = dg2_ref[...]

        @pl.when(last)
        def _():
            for k in range(SCONV_K):
                dsw_ref[k:k + 1, :] = jnp.sum(sw_acc[k * 8:(k + 1) * 8, :], axis=0, keepdims=True)
            for k in range(CONF_K):
                ddw_ref[k:k + 1, :] = jnp.sum(dw_acc[k * 8:(k + 1) * 8, :], axis=0, keepdims=True)
            ddb_ref[...] = jnp.sum(db_acc[...], axis=0, keepdims=True)

    def main(c):
        return pl.BlockSpec((tm, cw), lambda i: (i, c))

    def before(c):
        return pl.BlockSpec((HALO, cw), lambda i: (hb(i), c))

    def after(c):
        return pl.BlockSpec((HALO, cw), lambda i: (ha(i), c))

    def vec(r):
        return pl.BlockSpec((r, cw), lambda i: (0, 0))

    return pl.pallas_call(
        body, name=name, grid=(n,),
        in_specs=[main(0), after(0), main(5), after(5), main(1), after(1), main(0), before(0), main(2), before(2),
                  main(0), after(0), main(3), before(3), main(4), before(4), main(0), vec(SCONV_K), vec(CONF_K)],
        out_specs=[pl.BlockSpec((tm, 7 * cw), lambda i: (i, 0)), vec(SCONV_K), vec(CONF_K), vec(1)],
        out_shape=[jax.ShapeDtypeStruct((s, 7 * cw), BF16), jax.ShapeDtypeStruct((SCONV_K, cw), F32),
                   jax.ShapeDtypeStruct((CONF_K, cw), F32), jax.ShapeDtypeStruct((1, cw), F32)],
        scratch_shapes=[pltpu.VMEM((8 * SCONV_K, cw), F32), pltpu.VMEM((8 * CONF_K, cw), F32),
                        pltpu.VMEM((8, cw), F32)],
        compiler_params=_params("arbitrary"),
    )(dy, dy, p, p, p, p, p, p, p, p, ddc, ddc, p, p, p, p, dg2, sconv_w, dconv_w)


_ANY = pl.BlockSpec(memory_space=pl.ANY)


def _place():
    return lax.axis_index("x"), lax.axis_index("y"), lax.axis_index("c")


def all_gather(arrs, name, deps=()):
    n = len(arrs)

    def body(*refs):
        ins, outs = refs[:n], refs[n + len(deps):2 * n + len(deps)]
        send_sems, recv_sems, local_sems = refs[-3:]
        x, y, c = _place()
        me, sibling = (x, y, c), (x, y, 1 - c)
        chips = [(1 - x, y), (x, 1 - y), (1 - x, 1 - y)]

        def copy(a, k, block, to, src=None):
            px, py, pc = block
            dst = outs[a].at[4 * px + 2 * py + pc]
            return pltpu.make_async_remote_copy(
                src_ref=dst if src is None else src, dst_ref=dst,
                send_sem=send_sems.at[7 * a + k], recv_sem=recv_sems.at[7 * a + k],
                device_id=to, device_id_type=MESH)

        mine = [pltpu.make_async_copy(ins[a], outs[a].at[4 * x + 2 * y + c], local_sems.at[a]) for a in range(n)]
        first = []
        for a in range(n):
            first.append(copy(a, 0, me, sibling, src=ins[a]))
            first += [copy(a, 1 + j, me, (*chip, c), src=ins[a]) for j, chip in enumerate(chips)]
        for cp in first + mine:
            cp.start()
        passed = []
        for a in range(n):
            for j, chip in enumerate(chips):
                copy(a, 1 + j, (*chip, c), me).wait_recv()
                cp = copy(a, 4 + j, (*chip, c), sibling)
                cp.start()
                passed.append(cp)
        for a in range(n):
            copy(a, 0, sibling, me).wait_recv()
            for j, chip in enumerate(chips):
                copy(a, 4 + j, (*chip, 1 - c), me).wait_recv()
        for cp in first + passed:
            cp.wait_send()
        for cp in mine:
            cp.wait()

    return pl.pallas_call(
        body, name=name,
        out_shape=[jax.ShapeDtypeStruct((N_DEV,) + a.shape, a.dtype) for a in arrs],
        in_specs=[_ANY] * (n + len(deps)), out_specs=[_ANY] * n,
        scratch_shapes=[pltpu.SemaphoreType.DMA((7 * n,)), pltpu.SemaphoreType.DMA((7 * n,)),
                        pltpu.SemaphoreType.DMA((n,))],
    )(*arrs, *deps)


def in_proj_gathered(h, w_own, extras, name, tm=512):
    s, d = h.shape
    n = w_own.shape[1]
    arrs = [w_own] + list(extras)
    na = len(arrs)

    def body(*refs):
        h_ref, ins = refs[0], refs[1:1 + na]
        p_ref, outs = refs[1 + na], refs[2 + na:2 + 2 * na]
        wbuf, obuf, send_sems, recv_sems, load_sem, store_sems, own_sems = refs[2 + 2 * na:]
        x, y, c = _place()
        me, sibling = (x, y, c), (x, y, 1 - c)
        chips = [(1 - x, y), (x, 1 - y), (1 - x, 1 - y)]

        def slot(block):
            return 4 * block[0] + 2 * block[1] + block[2]

        def copy(a, k, block, to, src=None):
            dst = outs[a].at[slot(block)]
            return pltpu.make_async_remote_copy(
                src_ref=dst if src is None else src, dst_ref=dst,
                send_sem=send_sems.at[7 * a + k], recv_sem=recv_sems.at[7 * a + k],
                device_id=to, device_id_type=MESH)

        first = []
        for a in range(na):
            first.append(copy(a, 0, me, sibling, src=ins[a]))
            first += [copy(a, 1 + j, me, (*chip, c), src=ins[a]) for j, chip in enumerate(chips)]
        for cp in first:
            cp.start()
        own = pltpu.make_async_copy(wbuf.at[0], outs[0].at[slot(me)], own_sems.at[0])
        mine = [pltpu.make_async_copy(ins[a], outs[a].at[slot(me)], own_sems.at[a]) for a in range(1, na)]
        stores = [None, None]

        def multiply(k, block, w_from):
            b = k % 2
            if k == 2:
                own.wait()
            load = pltpu.make_async_copy(w_from, wbuf.at[b], load_sem)
            load.start()
            if stores[b] is not None:
                stores[b].wait()
            load.wait()
            if k == 0:
                own.start()

            def chunk(i, carry):
                rows = pl.ds(pl.multiple_of(i * tm, tm), tm)
                obuf[b, rows, :] = jnp.dot(h_ref[rows, :], wbuf[b], preferred_element_type=F32).astype(BF16)
                return carry

            lax.fori_loop(0, s // tm, chunk, 0)
            stores[b] = pltpu.make_async_copy(
                obuf.at[b], p_ref.at[:, pl.ds(pl.multiple_of(slot(block) * n, 128), n)], store_sems.at[b])
            stores[b].start()

        def arrived(arrays, j, chip):
            for a in arrays:
                copy(a, 1 + j, (*chip, c), me).wait_recv()
                cp = copy(a, 4 + j, (*chip, c), sibling)
                cp.start()
                passed.append(cp)

        small = range(1, na)
        passed = []
        multiply(0, me, ins[0])
        copy(0, 0, sibling, me).wait_recv()
        multiply(1, sibling, outs[0].at[slot(sibling)])
        for j, chip in enumerate(chips):
            arrived([0], j, chip)
            multiply(2 + 2 * j, (*chip, c), outs[0].at[slot((*chip, c))])
            copy(0, 4 + j, (*chip, 1 - c), me).wait_recv()
            multiply(3 + 2 * j, (*chip, 1 - c), outs[0].at[slot((*chip, 1 - c))])
        for cp in mine:
            cp.start()
        for a in small:
            copy(a, 0, sibling, me).wait_recv()
        for j, chip in enumerate(chips):
            arrived(small, j, chip)
        for j, chip in enumerate(chips):
            for a in small:
                copy(a, 4 + j, (*chip, 1 - c), me).wait_recv()
        for cp in first + passed:
            cp.wait_send()
        for cp in mine + stores:
            cp.wait()

    vmem = pl.BlockSpec(memory_space=pltpu.VMEM)
    outs = pl.pallas_call(
        body, name=name,
        out_shape=[jax.ShapeDtypeStruct((s, N_DEV * n), BF16)]
        + [jax.ShapeDtypeStruct((N_DEV,) + a.shape, a.dtype) for a in arrs],
        in_specs=[vmem] + [_ANY] * na, out_specs=[_ANY] * (1 + na),
        scratch_shapes=[pltpu.VMEM((2, d, n), BF16), pltpu.VMEM((2, s, n), BF16),
                        pltpu.SemaphoreType.DMA((7 * na,)), pltpu.SemaphoreType.DMA((7 * na,)),
                        pltpu.SemaphoreType.DMA, pltpu.SemaphoreType.DMA((2,)), pltpu.SemaphoreType.DMA((na,))],
        compiler_params=pltpu.CompilerParams(vmem_limit_bytes=VMEM_LIMIT),
    )(h, *arrs)
    return outs[0], outs[1], outs[2:]


_HBM = pl.BlockSpec(memory_space=pltpu.HBM)
_SEM = pl.BlockSpec(memory_space=pltpu.SEMAPHORE)
_DATAFLOW = pltpu.SideEffectType.DATAFLOW_SIDE_EFFECTING


def _peers_per_array(kind):
    return 1 if kind == "sibling" else 3


def _split_copies(kind, srcs, lands, send_sems, recv_sems):
    x, y, c = _place()
    per = _peers_per_array(kind)
    out = []
    for a in range(len(srcs)):
        if kind == "sibling":
            peers = [((x, y, 1 - c), srcs[a].at[:, pl.ds(1 - c, 1)], lands[a], lands[a])]
        else:
            peers = []
            for px, py in [(1 - x, y), (x, 1 - y), (1 - x, 1 - y)]:
                if kind == "gather":
                    views = (srcs[a], lands[a].at[4 * x + 2 * y + c], lands[a].at[4 * px + 2 * py + c])
                else:
                    views = (srcs[a].at[2 * px + py], lands[a].at[2 * x + y], lands[a].at[2 * px + py])
                peers.append(((px, py, c),) + views)
        for j, (peer, src, dst, arrives) in enumerate(peers):
            sems = dict(send_sem=send_sems.at[per * a + j], recv_sem=recv_sems.at[per * a + j],
                        device_id=peer, device_id_type=MESH)
            out.append((pltpu.make_async_remote_copy(src_ref=src, dst_ref=dst, **sems),
                        pltpu.make_async_remote_copy(src_ref=src, dst_ref=arrives, **sems)))
    return out


def split_start(kind, srcs, lands, deps, name):
    n = len(srcs)
    n_sems = _peers_per_array(kind) * n

    def body(*refs):
        send_sems, recv_sems = refs[2 * n + len(deps)], refs[2 * n + len(deps) + 1]
        for copy, _ in _split_copies(kind, refs[:n], refs[n:2 * n], send_sems, recv_sems):
            copy.start()
        token = refs[-1]
        token[...] = jnp.zeros_like(token)

    held = [pltpu.HBM(a.shape, a.dtype) for a in list(srcs) + list(lands)]
    outs = pl.pallas_call(
        body, name=name,
        out_shape=(pltpu.SemaphoreType.DMA((n_sems,)), pltpu.SemaphoreType.DMA((n_sems,)), *held,
                   jax.ShapeDtypeStruct((8, 128), F32)),
        in_specs=[_HBM] * (2 * n) + [_ANY] * len(deps),
        out_specs=(_SEM, _SEM, *([_HBM] * (2 * n)), pl.BlockSpec(memory_space=pltpu.VMEM)),
        input_output_aliases={i: 2 + i for i in range(2 * n)},
        compiler_params=pltpu.CompilerParams(has_side_effects=_DATAFLOW),
    )(*[pltpu.with_memory_space_constraint(a, pltpu.HBM) for a in list(srcs) + list(lands)], *deps)
    return outs[0], outs[1], list(outs[2:2 + n]), list(outs[2 + n:2 + 2 * n]), outs[-1]


def split_wait(kind, send_sems, recv_sems, srcs, lands, afters, name):
    n = len(srcs)

    def body(*refs):
        for _, arrival in _split_copies(kind, refs[:n], refs[n:2 * n], refs[2 * n], refs[2 * n + 1]):
            arrival.wait_send()
            arrival.wait_recv()

    outs = pl.pallas_call(
        body, name=name,
        out_shape=[pltpu.HBM(a.shape, a.dtype) for a in list(srcs) + list(lands)],
        in_specs=[_HBM] * (2 * n) + [_SEM, _SEM] + [_ANY] * len(afters),
        out_specs=[_HBM] * (2 * n),
        input_output_aliases={i: i for i in range(2 * n)},
        compiler_params=pltpu.CompilerParams(has_side_effects=_DATAFLOW),
    )(*srcs, *lands, send_sems, recv_sems, *afters)
    return list(outs[:n]), list(outs[n:])


def place_block(land, block, dev, name):
    r, c = block.shape
    tr = min(r, 512)

    def body(dev_ref, land_ref, b_ref, o_ref):
        del dev_ref, land_ref
        o_ref[...] = b_ref[...]

    return pl.pallas_call(
        body, name=name,
        grid_spec=pltpu.PrefetchScalarGridSpec(
            num_scalar_prefetch=1, grid=(r // tr,),
            in_specs=[_ANY, pl.BlockSpec((tr, c), lambda i, dev_ref: (i, 0))],
            out_specs=pl.BlockSpec((None, tr, c), lambda i, dev_ref: (dev_ref[0], i, 0))),
        out_shape=jax.ShapeDtypeStruct(land.shape, land.dtype),
        input_output_aliases={1: 0},
        compiler_params=_params("parallel"),
    )(dev, land, block)


def gather_finish(lands, name):
    n = len(lands)

    def body(*refs):
        outs = refs[n:2 * n]
        send_sems, recv_sems = refs[2 * n:]
        x, y, c = _place()
        cps = [pltpu.make_async_remote_copy(
            src_ref=outs[a].at[:, pl.ds(c, 1)], dst_ref=outs[a].at[:, pl.ds(c, 1)],
            send_sem=send_sems.at[a], recv_sem=recv_sems.at[a],
            device_id=(x, y, 1 - c), device_id_type=MESH) for a in range(n)]
        for cp in cps:
            cp.start()
        for cp in cps:
            cp.wait()

    return pl.pallas_call(
        body, name=name,
        out_shape=[jax.ShapeDtypeStruct(a.shape, a.dtype) for a in lands],
        in_specs=[_ANY] * n, out_specs=[_ANY] * n,
        input_output_aliases={i: i for i in range(n)},
        scratch_shapes=[pltpu.SemaphoreType.DMA((n,)), pltpu.SemaphoreType.DMA((n,))],
    )(*lands)


def pair_add(own, recv, core, name):
    _, _, r, c = own.shape
    tr = min(r, 512)

    def body(core_ref, own_ref, recv_ref, o_ref):
        del core_ref
        o_ref[...] = (own_ref[...].astype(F32) + recv_ref[...].astype(F32)).astype(BF16)

    return pl.pallas_call(
        body, name=name,
        grid_spec=pltpu.PrefetchScalarGridSpec(
            num_scalar_prefetch=1, grid=(4, r // tr),
            in_specs=[pl.BlockSpec((None, None, tr, c), lambda k, i, core_ref: (k, core_ref[0], i, 0)),
                      pl.BlockSpec((None, None, tr, c), lambda k, i, core_ref: (k, 0, i, 0))],
            out_specs=pl.BlockSpec((None, tr, c), lambda k, i, core_ref: (k, i, 0))),
        out_shape=jax.ShapeDtypeStruct((4, r, c), BF16),
        compiler_params=_params("parallel", "parallel"),
    )(core, own, recv)


def _adamw_math(w, g, m, v):
    m2 = ADAM_B1 * m + (1.0 - ADAM_B1) * g
    v2 = ADAM_B2 * v + (1.0 - ADAM_B2) * (g * g)
    m_hat = m2 / (1.0 - ADAM_B1 ** ADAM_STEP)
    v_hat = v2 / (1.0 - ADAM_B2 ** ADAM_STEP)
    delta = -ADAM_LR * (m_hat / (jnp.sqrt(v_hat) + ADAM_EPS) + ADAM_WD * w)
    return delta, m2, v2


def adamw_big(w, m, v, own, got, chip, name):
    r, c = w.shape
    tr = min(r, 256)

    def body(chip_ref, w_ref, m_ref, v_ref, p0, p1, p2, p3, g_ref, d_ref, m2_ref, v2_ref):
        del chip_ref
        g = ((p0[...].astype(F32) + p1[...].astype(F32)) + p2[...].astype(F32)) + p3[...].astype(F32)
        delta, m2, v2 = _adamw_math(w_ref[...], g, m_ref[...], v_ref[...])
        g_ref[...] = g
        d_ref[...] = delta
        m2_ref[...] = m2
        v2_ref[...] = v2

    row = pl.BlockSpec((tr, c), lambda i, chip_ref: (i, 0))

    def slab(flip):
        return pl.BlockSpec((None, tr, c), lambda i, chip_ref: (chip_ref[0] ^ flip, i, 0))

    return pl.pallas_call(
        body, name=name,
        grid_spec=pltpu.PrefetchScalarGridSpec(
            num_scalar_prefetch=1, grid=(r // tr,),
            in_specs=[row, row, row, slab(0), slab(1), slab(2), slab(3)],
            out_specs=[row] * 4),
        out_shape=[jax.ShapeDtypeStruct((r, c), F32)] * 4,
        compiler_params=_params("parallel"),
    )(chip, w, m, v, own, got, got, got)


def sum_devices(g8, name):
    def body(g_ref, o_ref):
        tot = g_ref[0]
        for k in range(1, N_DEV):
            tot = tot + g_ref[k]
        o_ref[...] = tot

    return pl.pallas_call(body, name=name, out_shape=jax.ShapeDtypeStruct(g8.shape[1:], F32))(g8)


def adamw_small(ws, gs, ms, vs, name):
    n = len(ws)

    def body(*refs):
        w_r, g_r, m_r, v_r = refs[:n], refs[n:2 * n], refs[2 * n:3 * n], refs[3 * n:4 * n]
        d_o, m_o, v_o = refs[4 * n:5 * n], refs[5 * n:6 * n], refs[6 * n:7 * n]
        for k in range(n):
            delta, m2, v2 = _adamw_math(w_r[k][...], g_r[k][...], m_r[k][...], v_r[k][...])
            d_o[k][...] = delta
            m_o[k][...] = m2
            v_o[k][...] = v2

    shapes = [jax.ShapeDtypeStruct(w.shape, F32) for w in ws]
    outs = pl.pallas_call(body, name=name, out_shape=shapes * 3)(*ws, *gs, *ms, *vs)
    return outs[:n], outs[n:2 * n], outs[2 * n:]


def _rows128(a):
    return a.reshape(-1, 128)


def _pad_rows(a, rows):
    return jnp.pad(a, ((0, rows - a.shape[0]), (0, 0)))


def kernel(x, ln_pre_even, w_in_even, pool_w, pool_scale, w_out_even, ln_post_even, ln_pre_odd, w_in_odd, sconv_w, dconv_w, dconv_b, cnorm_g, cnorm_b, w_out_odd, ln_post_odd, loss_target, m_ln_pre_even, m_w_in_even, m_pool_w, m_pool_scale, m_w_out_even, m_ln_post_even, m_ln_pre_odd, m_w_in_odd, m_sconv_w, m_dconv_w, m_dconv_b, m_cnorm_g, m_cnorm_b, m_w_out_odd, m_ln_post_odd, v_ln_pre_even, v_w_in_even, v_pool_w, v_pool_scale, v_w_out_even, v_ln_post_even, v_ln_pre_odd, v_w_in_odd, v_sconv_w, v_dconv_w, v_dconv_b, v_cnorm_g, v_cnorm_b, v_w_out_odd, v_ln_post_odd):
    xs = x[0]
    tgt = loss_target[0]
    s, d = xs.shape
    half = d // 2
    n_heads = half // HEAD_DIM
    ng = len(POOL_WINDOWS)
    cwp = half // ng
    dev = 4 * lax.axis_index("x") + 2 * lax.axis_index("y") + lax.axis_index("c")
    core = lax.axis_index("c").astype(jnp.int32).reshape(1)

    pr = pool_w.shape[2]
    cl = sconv_w.shape[2]
    small_parts = [(_rows128(ln_pre_odd), 8), (sconv_w[0], 8), (dconv_w[0], 32), (dconv_b, 8),
                   (cnorm_g, 8), (cnorm_b, 8), (_rows128(ln_post_odd), 8)]
    small_local = jnp.concatenate([_pad_rows(a, r) for a, r in small_parts], axis=0)
    h0 = rms_fwd(xs, ln_pre_even, "rms_pre_even")
    p0, g_wie, (g_pw, g_small) = in_proj_gathered(
        h0, w_in_even[0].astype(BF16), [pool_w[0].reshape(ng * pr, cwp).astype(BF16), small_local],
        "ag_in_proj_even")
    comm = _Exchanges(dev, core, d)
    token = comm.start_weights("out_even", [w_out_even[0].astype(BF16)], [p0])
    sb_dep = comm.start_weights("odd", [w_in_odd[0].astype(BF16), w_out_odd[0].astype(BF16)], [token])
    pool_full = g_pw.reshape(N_DEV, ng, pr, cwp).transpose(1, 0, 2, 3).reshape(ng, cwp, cwp)
    nl = ln_pre_odd.shape[1] // 128

    def chan(lo, rows):
        return g_small[:, lo:lo + rows].transpose(1, 0, 2).reshape(rows, N_DEV * cl)

    ln_pre_odd_f = g_small[:, 0:nl].reshape(1, d)
    sconv_f = chan(8, SCONV_K)
    dconv_f = chan(16, CONF_K)
    dconv_b_f = chan(48, 1)
    cnorm_g_f = chan(56, 1)
    cnorm_b_f = chan(64, 1)
    ln_post_odd_f = g_small[:, 72:72 + nl].reshape(1, d)

    loss_blk, grad_x, small_g = _fwd_bwd(
        xs, tgt, ln_pre_even, h0, p0, g_wie, pool_full, pool_scale, ln_post_even, ln_pre_odd_f,
        sconv_f, dconv_f, dconv_b_f, cnorm_g_f, cnorm_b_f, ln_post_odd_f, comm, sb_dep)
    small_w = [ln_pre_even, pool_scale, ln_post_even, ln_pre_odd, sconv_w[0], dconv_w[0], dconv_b, cnorm_g, cnorm_b, ln_post_odd]
    small_m = [m_ln_pre_even, m_pool_scale, m_ln_post_even, m_ln_pre_odd, m_sconv_w[0], m_dconv_w[0], m_dconv_b, m_cnorm_g, m_cnorm_b, m_ln_post_odd]
    small_v = [v_ln_pre_even, v_pool_scale, v_ln_post_even, v_ln_pre_odd, v_sconv_w[0], v_dconv_w[0], v_dconv_b, v_cnorm_g, v_cnorm_b, v_ln_post_odd]
    big = {"w_in_even": (w_in_even, m_w_in_even, v_w_in_even), "pool_w": (pool_w, m_pool_w, v_pool_w),
           "w_out_even": (w_out_even, m_w_out_even, v_w_out_even), "w_in_odd": (w_in_odd, m_w_in_odd, v_w_in_odd),
           "w_out_odd": (w_out_odd, m_w_out_odd, v_w_out_odd)}
    upd = comm.finish_updates(big, [grad_x])
    upd.update(comm.finish_updates(big, [grad_x]))
    sg, sd, sm, sv, loss = _update_small(small_g, loss_blk, small_w, small_m, small_v, dev, d, cl,
                                         deps=[upd["w_in_odd"][1], upd["w_out_even"][1]])
    upd.update(comm.finish_updates(big, sd))
    (g_wie_o, d_wie, m_wie, v_wie), (g_pw_o, d_pw, m_pw, v_pw) = upd["w_in_even"], upd["pool_w"]
    (g_woe_o, d_woe, m_woe, v_woe), (g_wio_o, d_wio, m_wio, v_wio) = upd["w_out_even"], upd["w_in_odd"]
    g_woo_o, d_woo, m_woo, v_woo = upd["w_out_odd"]

    def order(small, wie, pw, woe, wio, woo):
        return [small[0], wie, pw, small[1], woe, small[2], small[3], wio, small[4], small[5], small[6],
                small[7], small[8], woo, small[9]]

    grads = order(sg, g_wie_o, g_pw_o, g_woe_o, g_wio_o, g_woo_o)
    deltas = order(sd, d_wie, d_pw, d_woe, d_wio, d_woo)
    new_m = order(sm, m_wie, m_pw, m_woe, m_wio, m_woo)
    new_v = order(sv, v_wie, v_pw, v_woe, v_wio, v_woo)
    return (loss, grad_x[None], *grads, *deltas, *new_m, *new_v)


def _fwd_bwd(xs, tgt, ln_pre_even, h0, p0, g_wie, pool_full, pool_scale, ln_post_even, ln_pre_odd_f,
             sconv_f, dconv_f, dconv_b_f, cnorm_g_f, cnorm_b_f, ln_post_odd_f, comm, sb_dep):
    d = xs.shape[1]
    n_heads = d // 2 // HEAD_DIM
    ng, cwp = pool_full.shape[0], pool_full.shape[1]
    a0, sb_wts = sb_fwd(p0, n_heads, "sb_fwd", dep=sb_dep)
    y0 = even_mix_fwd(a0, p0, pool_full, pool_scale, "even_mix_fwd")
    (w_out_e,) = comm.weights("out_even", after=y0)
    w_out_e = w_out_e.reshape(1, d, d)
    o0 = mm_nn(y0, w_out_e, F32, "out_proj_even", tm=1024, tn=1024)
    x1, h1 = postnorm_fwd(xs, o0, ln_post_even, ln_pre_odd_f, "post_even")
    g_wio, w_out_o = comm.weights("odd", after=x1)
    w_out_o = w_out_o.reshape(1, d, d)
    p1 = mm_nn(h1, g_wio, BF16, "in_proj_odd")
    y1, dc = odd_mix_fwd(p1, sconv_f, dconv_f, dconv_b_f, cnorm_g_f, cnorm_b_f, "odd_mix_fwd")
    o1 = mm_nn(y1, w_out_o, F32, "out_proj_odd", tm=1024, tn=1024)
    loss_blk, gx2, do1, dg_post_odd = final_fwd_bwd(x1, o1, ln_post_odd_f, tgt, "post_odd_loss")

    dw_out_o = mm_tn(y1, do1, 1, BF16, "dw_out_odd", tm=1024)
    dy1 = mm_nt(do1, w_out_o, BF16, "dy_odd", tn=1024)
    ddc, dg2, dgam, dbet = odd_bwd_ln(dy1, p1, dc, cnorm_g_f, cnorm_b_f, "odd_bwd_ln")
    dp1, dsconv, ddconv, ddconv_b = odd_bwd_conv(dy1, p1, ddc, dg2, sconv_f, dconv_f, "odd_bwd_conv")
    dw_in_o = mm_tn(h1, dp1, N_DEV, BF16, "dw_in_odd")
    dep = comm.reduce_begin({"w_out_odd": dw_out_o.reshape(N_DEV, d // N_DEV, d), "w_in_odd": dw_in_o}, "odd")
    dh1 = mm_nt(dp1, g_wio, F32, "dh_odd", dep=dep)
    dep = comm.reduce_send(after=dh1)
    gx1, dg_pre_odd, do0, dg_post_even = norm_bwd(dh1, x1, ln_pre_odd_f, gx2, "pre_odd_post_even_bwd",
                                                  inp2=o0, g2=ln_post_even, dep=dep)

    dw_out_e = mm_tn(y0, do0, 1, BF16, "dw_out_even", tm=1024)
    dy0 = mm_nt(do0, w_out_e, BF16, "dy_even", tn=1024)
    da0, du0, dg0, dpool, dpool_scale = even_mix_bwd(dy0, a0, p0, pool_full, pool_scale, "even_mix_bwd")
    pr = cwp // N_DEV
    dpool_slabs = dpool.astype(BF16).reshape(ng, N_DEV, pr, cwp).transpose(1, 0, 2, 3).reshape(N_DEV, ng * pr, cwp)
    dep = comm.reduce_begin({"w_out_even": dw_out_e.reshape(N_DEV, d // N_DEV, d), "pool_w": dpool_slabs}, "even_out")
    dq0, dk0, dv0 = sb_bwd(p0, a0, sb_wts, da0, n_heads, "sb_bwd", dep=dep)
    dep = comm.reduce_send(after=dq0)
    dp0 = jnp.concatenate([dq0, dk0, dv0, du0, dg0], axis=1)
    dw_in_e = mm_tn(h0, dp0, N_DEV, BF16, "dw_in_even", dep=dep)
    comm.reduce_begin({"w_in_even": dw_in_e}, "even_in")
    dep = comm.reduce_send(after=dw_in_e)
    dh0 = mm_nt(dp0, g_wie, F32, "dh_even", dep=dep)
    dep = None
    grad_x, dg_pre_even = norm_bwd(dh0, xs, ln_pre_even, gx1, "pre_even_bwd", dep=dep)
    small_g = [dg_pre_even, dpool_scale, dg_post_even, dg_pre_odd, dsconv, ddconv, ddconv_b, dgam, dbet, dg_post_odd]
    return loss_blk, grad_x, small_g


class _Exchanges:
    def __init__(self, dev, core, d):
        self.dev = dev.astype(jnp.int32).reshape(1)
        self.core = core
        self.chip = (dev // 2).astype(jnp.int32).reshape(1)
        self.d = d
        self.in_flight = {}
        self.to_sibling = None
        self.pending = []

    def start_weights(self, tag, blocks, afters):
        lands = [lax.empty((N_DEV,) + b.shape, b.dtype) for b in blocks]
        send, recv, srcs, lands, token = split_start("gather", blocks, lands, afters, "ag_start_" + tag)
        self.in_flight[tag] = (send, recv, srcs, lands)
        return token

    def weights(self, tag, after):
        send, recv, srcs, lands = self.in_flight.pop(tag)
        srcs, lands = split_wait("gather", send, recv, srcs, lands, [after], "ag_wait_" + tag)
        lands = [place_block(l, b, self.dev, "ag_own_%s_%d" % (tag, k)) for k, (l, b) in enumerate(zip(lands, srcs))]
        full = gather_finish([l.reshape((4, 2) + l.shape[1:]) for l in lands], "ag_finish_" + tag)
        return [f.reshape((N_DEV,) + f.shape[2:]) for f in full]

    def reduce_begin(self, partials, tag):
        names = list(partials)
        arrs = [partials[k].reshape((4, 2) + partials[k].shape[1:]) for k in names]
        lands = [lax.empty((4, 1) + a.shape[2:], a.dtype) for a in arrs]
        send, recv, srcs, lands, token = split_start("sibling", arrs, lands, [], "rs_sibling_start_" + tag)
        self.to_sibling = (tag, names, send, recv, srcs, lands)
        return token

    def reduce_send(self, after):
        tag, names, send, recv, srcs, lands = self.to_sibling
        srcs, lands = split_wait("sibling", send, recv, srcs, lands, [after], "rs_sibling_wait_" + tag)
        sums = [pair_add(o, r, self.core, "rs_pair_add_" + k) for k, o, r in zip(names, srcs, lands)]
        zones = [lax.empty(a.shape, a.dtype) for a in sums]
        send, recv, srcs, zones, token = split_start("scatter", sums, zones, [], "rs_start_" + tag)
        self.pending.append((tag, names, send, recv, srcs, zones))
        return token

    def finish_updates(self, big, afters):
        tag, names, send, recv, srcs, lands = self.pending.pop(0)
        srcs, lands = split_wait("scatter", send, recv, srcs, lands, afters, "rs_wait_" + tag)
        out = {}
        for name, own, got in zip(names, srcs, lands):
            w, m, v = big[name]
            shp = own.shape[1:]
            outs = adamw_big(w.reshape(shp), m.reshape(shp), v.reshape(shp), own, got, self.chip, "adamw_" + name)
            out[name] = [o.reshape(w.shape) for o in outs]
        return out


def _update_small(small_g, loss_blk, small_w, small_m, small_v, dev, d, cl, deps):
    packed = jnp.concatenate([_rows128(g) for g in small_g] + [loss_blk], axis=0)
    (g8,) = all_gather([packed], "ag_small_grads", deps)
    tot = sum_devices(g8, "sum_small_grads")
    loss = tot[packed.shape[0] - 8, 0]
    full_g = []
    lo = 0
    for g in small_g:
        rows = g.size // 128
        full_g.append(tot[lo:lo + rows].reshape(g.shape))
        lo += rows

    def mine(g, width):
        return lax.dynamic_slice_in_dim(g, dev * width, width, axis=g.ndim - 1)

    fg = full_g
    small_gl = [fg[0], fg[1], fg[2], mine(fg[3], d // N_DEV), mine(fg[4], cl), mine(fg[5], cl), mine(fg[6], cl),
                mine(fg[7], cl), mine(fg[8], cl), mine(fg[9], d // N_DEV)]
    sd, sm, sv = adamw_small(small_w, small_gl, small_m, small_v, "adamw_small")

    def like(k, a):
        return a[None] if k in (4, 5) else a

    sg = [like(k, a) for k, a in enumerate(small_gl)]
    sd = [like(k, a) for k, a in enumerate(sd)]
    sm = [like(k, a) for k, a in enumerate(sm)]
    sv = [like(k, a) for k, a in enumerate(sv)]
    return sg, sd, sm, sv, loss
```

```python
import functools
import math

import jax
import jax.numpy as jnp
from jax import lax
from jax.experimental import pallas as pl
from jax.experimental.pallas import tpu as pltpu

F32 = jnp.float32
BF16 = jnp.bfloat16
EPS = 1e-6
HEAD_DIM = 128
POOL_WINDOWS = (2, 4, 8, 16)
SCONV_K = 3
CONF_K = 31
HALO = 32
N_DEV = 8
VMEM_LIMIT = 56 * 1024 * 1024
MESH = pl.DeviceIdType.MESH

ADAM_LR = 0.001
ADAM_B1 = 0.9
ADAM_B2 = 0.999
ADAM_EPS = 1e-08
ADAM_WD = 0.01
ADAM_STEP = 10


def _params(*sem):
    return pltpu.CompilerParams(dimension_semantics=sem, vmem_limit_bytes=VMEM_LIMIT)


def _sigmoid(v):
    return 1.0 / (1.0 + jnp.exp(-v))


def _silu(v):
    return v * _sigmoid(v)


def _silu_and_grad(v):
    s = _sigmoid(v)
    return v * s, s * (1.0 + v * (1.0 - s))


def _rowsum8(v):
    r, c = v.shape
    return jnp.sum(v.reshape(r // 8, 8, c), axis=0)


SUBLANES = 8


class _Taps:
    def __init__(self, xx, rows, before):
        self.xx, self.rows, self.before, self.rotated = xx, rows, before, {}

    def __call__(self, i):
        r, q = i % SUBLANES, i // SUBLANES
        if r not in self.rotated:
            n = self.xx.shape[0]
            self.rotated[r] = self.xx if r == 0 else pltpu.roll(self.xx, r if self.before else n - r, 0)
        lo = HALO - SUBLANES * q if self.before else SUBLANES * q
        return self.rotated[r][lo:lo + self.rows]


def _window_sum(xx, win, before):
    n = xx.shape[0]
    acc = xx
    k = 1
    while k < win:
        acc = acc + pltpu.roll(acc, k if before else n - k, 0)
        k *= 2
    return acc


def rms_fwd(x, g, name, tm=256):
    s, d = x.shape

    def body(x_ref, g_ref, h_ref):
        xv = x_ref[...]
        r = lax.rsqrt(jnp.mean(xv * xv, axis=-1, keepdims=True) + EPS)
        h_ref[...] = (xv * r * g_ref[...]).astype(BF16)

    return pl.pallas_call(
        body, name=name, grid=(s // tm,),
        in_specs=[pl.BlockSpec((tm, d), lambda i: (i, 0)), pl.BlockSpec((1, d), lambda i: (0, 0))],
        out_specs=pl.BlockSpec((tm, d), lambda i: (i, 0)),
        out_shape=jax.ShapeDtypeStruct((s, d), BF16),
        compiler_params=_params("parallel"),
    )(x, g)


def postnorm_fwd(x, o, g, g_next, name, tm=256, dep=None):
    s, d = x.shape
    dep_args, dep_specs = _after(dep)

    def body(x_ref, o_ref, g_ref, gn_ref, *rest):
        y_ref, h_ref = rest[-2:]
        ov = o_ref[...]
        r = lax.rsqrt(jnp.mean(ov * ov, axis=-1, keepdims=True) + EPS)
        y = x_ref[...] + ov * r * g_ref[...]
        y_ref[...] = y
        r2 = lax.rsqrt(jnp.mean(y * y, axis=-1, keepdims=True) + EPS)
        h_ref[...] = (y * r2 * gn_ref[...]).astype(BF16)

    row = pl.BlockSpec((tm, d), lambda i: (i, 0))
    vec = pl.BlockSpec((1, d), lambda i: (0, 0))
    return pl.pallas_call(
        body, name=name, grid=(s // tm,),
        in_specs=[row, row, vec, vec] + dep_specs, out_specs=[row, row],
        out_shape=[jax.ShapeDtypeStruct((s, d), F32), jax.ShapeDtypeStruct((s, d), BF16)],
        compiler_params=_params("parallel"),
    )(x, o, g, g_next, *dep_args)


def final_fwd_bwd(x1, o, g, target, name, tm=256):
    s, d = x1.shape
    n = s // tm

    def body(x_ref, o_ref, g_ref, t_ref, loss_ref, gx_ref, do_ref, dg_ref, lacc, gacc):
        i = pl.program_id(0)

        @pl.when(i == 0)
        def _():
            lacc[...] = jnp.zeros_like(lacc)
            gacc[...] = jnp.zeros_like(gacc)

        ov = o_ref[...]
        gv = g_ref[...]
        r = lax.rsqrt(jnp.mean(ov * ov, axis=-1, keepdims=True) + EPS)
        oh = ov * r
        diff = x_ref[...] + oh * gv - t_ref[...]
        lacc[...] += _rowsum8(diff * diff)
        gx = diff * (1.0 / d)
        gx_ref[...] = gx
        gacc[...] += _rowsum8(gx * oh)
        dn = gx * gv
        do_ref[...] = (r * (dn - oh * jnp.mean(dn * oh, axis=-1, keepdims=True))).astype(BF16)

        @pl.when(i == n - 1)
        def _():
            tot = jnp.sum(jnp.sum(lacc[...], axis=0, keepdims=True), axis=1, keepdims=True)
            loss_ref[...] = jnp.broadcast_to(tot * (0.5 / d), loss_ref.shape)
            dg_ref[...] = jnp.sum(gacc[...], axis=0, keepdims=True)

    row = pl.BlockSpec((tm, d), lambda i: (i, 0))
    vec = pl.BlockSpec((1, d), lambda i: (0, 0))
    return pl.pallas_call(
        body, name=name, grid=(n,),
        in_specs=[row, row, vec, row],
        out_specs=[pl.BlockSpec((8, 128), lambda i: (0, 0)), row, row, vec],
        out_shape=[jax.ShapeDtypeStruct((8, 128), F32), jax.ShapeDtypeStruct((s, d), F32),
                   jax.ShapeDtypeStruct((s, d), BF16), jax.ShapeDtypeStruct((1, d), F32)],
        scratch_shapes=[pltpu.VMEM((8, d), F32), pltpu.VMEM((8, d), F32)],
        compiler_params=_params("arbitrary"),
    )(x1, o, g, target)


def _rms_bwd_rows(dyv, xv, gv):
    r = lax.rsqrt(jnp.mean(xv * xv, axis=-1, keepdims=True) + EPS)
    xh = xv * r
    dn = dyv * gv
    return r * (dn - xh * jnp.mean(dn * xh, axis=-1, keepdims=True)), _rowsum8(dyv * xh)


def norm_bwd(dy, inp, g, resid, name, inp2=None, g2=None, tm=256, dep=None):
    s, d = inp.shape
    n = s // tm
    chain = inp2 is not None

    def body(*refs):
        dy_ref, x_ref, g_ref, r_ref = refs[:4]
        outs = refs[-6:] if chain else refs[-3:]
        i = pl.program_id(0)

        @pl.when(i == 0)
        def _():
            for acc in outs[-2:] if chain else outs[-1:]:
                acc[...] = jnp.zeros_like(acc)

        if chain:
            x2_ref, g2_ref = refs[4:6]
            dx_ref, dg_ref, dx2_ref, dg2_ref, gacc, gacc2 = outs
        else:
            dx_ref, dg_ref, gacc = outs
        dx, dg_rows = _rms_bwd_rows(dy_ref[...].astype(F32), x_ref[...], g_ref[...])
        dx = dx + r_ref[...]
        dx_ref[...] = dx
        gacc[...] += dg_rows
        if chain:
            dx2, dg2_rows = _rms_bwd_rows(dx, x2_ref[...], g2_ref[...])
            dx2_ref[...] = dx2.astype(BF16)
            gacc2[...] += dg2_rows

        @pl.when(i == n - 1)
        def _():
            dg_ref[...] = jnp.sum(gacc[...], axis=0, keepdims=True)
            if chain:
                dg2_ref[...] = jnp.sum(gacc2[...], axis=0, keepdims=True)

    row = pl.BlockSpec((tm, d), lambda i: (i, 0))
    vec = pl.BlockSpec((1, d), lambda i: (0, 0))
    dep_args, dep_specs = _after(dep)
    extra = [inp2, g2] if chain else []
    return pl.pallas_call(
        body, name=name, grid=(n,),
        in_specs=[row, row, vec, row] + ([row, vec] if chain else []) + dep_specs,
        out_specs=[row, vec] * (2 if chain else 1),
        out_shape=[jax.ShapeDtypeStruct((s, d), F32), jax.ShapeDtypeStruct((1, d), F32)]
        + ([jax.ShapeDtypeStruct((s, d), BF16), jax.ShapeDtypeStruct((1, d), F32)] if chain else []),
        scratch_shapes=[pltpu.VMEM((8, d), F32)] * (2 if chain else 1),
        compiler_params=_params("arbitrary"),
    )(dy, inp, g, resid, *extra, *dep_args)


def _after(dep):
    if dep is None:
        return [], []
    return [dep], [pl.BlockSpec((8, 128), lambda *_: (0, 0))]


def mm_nn(a, w, out_dtype, name, tm=2048, tn=None, dep=None):
    m, k = a.shape
    tm = min(tm, m)
    ns, _, n = w.shape
    tn = n if tn is None else tn
    nj = n // tn
    dep_args, dep_specs = _after(dep)

    def body(a_ref, w_ref, *rest):
        o_ref = rest[-1]
        o_ref[...] = jnp.dot(a_ref[...], w_ref[0], preferred_element_type=F32).astype(out_dtype)

    return pl.pallas_call(
        body, name=name, grid=(ns, nj, m // tm),
        in_specs=[pl.BlockSpec((tm, k), lambda s, j, i: (i, 0)),
                  pl.BlockSpec((1, k, tn), lambda s, j, i: (s, 0, j))] + dep_specs,
        out_specs=pl.BlockSpec((tm, tn), lambda s, j, i: (i, s * nj + j)),
        out_shape=jax.ShapeDtypeStruct((m, ns * n), out_dtype),
        compiler_params=_params("parallel", "parallel", "parallel"),
    )(a, w, *dep_args)


def mm_nt(a, w, out_dtype, name, tm=1024, tn=None, dep=None):
    m = a.shape[0]
    tm = min(tm, m)
    ns, k, n = w.shape
    tn = n if tn is None else tn
    nj = n // tn
    steps = ns * nj
    dep_args, dep_specs = _after(dep)

    def body(a_ref, w_ref, *rest):
        o_ref, acc = rest[-2:]
        r = pl.program_id(1)

        @pl.when(r == 0)
        def _():
            acc[...] = jnp.zeros_like(acc)

        acc[...] += lax.dot_general(a_ref[...], w_ref[0], (((1,), (1,)), ((), ())),
                                    preferred_element_type=F32)

        @pl.when(r == steps - 1)
        def _():
            o_ref[...] = acc[...].astype(out_dtype)

    return pl.pallas_call(
        body, name=name, grid=(m // tm, steps),
        in_specs=[pl.BlockSpec((tm, tn), lambda i, r: (i, r)),
                  pl.BlockSpec((1, k, tn), lambda i, r: (r // nj, 0, r % nj))] + dep_specs,
        out_specs=pl.BlockSpec((tm, k), lambda i, r: (i, 0)),
        out_shape=jax.ShapeDtypeStruct((m, k), out_dtype),
        scratch_shapes=[pltpu.VMEM((tm, k), F32)],
        compiler_params=_params("parallel", "arbitrary"),
    )(a, w, *dep_args)


def mm_tn(a, b, ns, out_dtype, name, tk=1024, tm=2048, dep=None):
    m, k = a.shape
    tm = min(tm, m)
    n = b.shape[1] // ns
    steps = m // tm
    dep_args, dep_specs = _after(dep)

    def body(a_ref, b_ref, *rest):
        o_ref, acc = rest[-2:]
        r = pl.program_id(2)

        @pl.when(r == 0)
        def _():
            acc[...] = jnp.zeros_like(acc)

        acc[...] += lax.dot_general(a_ref[...], b_ref[...], (((0,), (0,)), ((), ())),
                                    preferred_element_type=F32)

        @pl.when(r == steps - 1)
        def _():
            o_ref[0] = acc[...].astype(out_dtype)

    return pl.pallas_call(
        body, name=name, grid=(ns, k // tk, steps),
        in_specs=[pl.BlockSpec((tm, tk), lambda s, j, r: (r, j)),
                  pl.BlockSpec((tm, n), lambda s, j, r: (r, s))] + dep_specs,
        out_specs=pl.BlockSpec((1, tk, n), lambda s, j, r: (s, j, 0)),
        out_shape=jax.ShapeDtypeStruct((ns, k, n), out_dtype),
        scratch_shapes=[pltpu.VMEM((tk, n), F32)],
        compiler_params=_params("parallel", "parallel", "arbitrary"),
    )(a, b, *dep_args)


SB_BLK = 128


LOG2E = 1.0 / math.log(2.0)


def _split_dot(v, tri):
    hi = pltpu.bitcast(pltpu.bitcast(v, jnp.uint32) & jnp.uint32(0xFFFF0000), F32)
    lo = (v - hi).astype(BF16)
    return (jnp.dot(hi.astype(BF16), tri, preferred_element_type=F32)
            + jnp.dot(lo, tri, preferred_element_type=F32))


def _sb_scores(z2, lim, dcol, tri_ex, masked):
    sp = jnp.log2(1.0 + jnp.exp2(-jnp.abs(z2)))
    lb = jnp.minimum(z2, 0.0) - sp
    l1m = lb - z2
    mask = None
    if masked:
        mask = dcol < lim
        l1m = jnp.where(mask, l1m, 0.0)
    return mask, lb, l1m, _split_dot(l1m, tri_ex)


def _sb_consts():
    row = lax.broadcasted_iota(jnp.int32, (SB_BLK, SB_BLK), 0)
    col = lax.broadcasted_iota(jnp.int32, (SB_BLK, SB_BLK), 1)
    tri_ex = jnp.where(row > col, 1.0, 0.0).astype(BF16)
    tri_in = jnp.where(row >= col, 1.0, 0.0).astype(BF16)
    return col - row, tri_ex, tri_in


def sb_fwd(p, n_heads, name, tq=256, nsub=4, dep=None):
    s = p.shape[0]
    h_n = n_heads
    b = SB_BLK
    nqs = tq // b
    tk = nsub * b
    scale = 1.0 / math.sqrt(HEAD_DIM)

    dep_args, dep_specs = _after(dep)

    def body(q_ref, k_ref, v_ref, *rest):
        o_ref, w_ref = rest[-2:]
        qi = pl.program_id(1)
        dcol, tri_ex, _ = _sb_consts()
        qv = [q_ref[qs * b:(qs + 1) * b, :] for qs in range(nqs)]
        n_groups = ((qi + 1) * nqs - 1) // nsub + 1

        def step(it, carry, masked):
            c1s, accs = carry
            g = n_groups - 1 - it
            off = pl.multiple_of(g * tk, tk)
            kg = k_ref[pl.ds(off, tk), :]
            vg = v_ref[pl.ds(off, tk), :]
            new_c1, new_acc = [], []
            for qs in range(nqs):
                qb = qi * nqs + qs
                z2 = lax.dot_general(qv[qs], kg, (((1,), (1,)), ((), ())),
                                     preferred_element_type=F32) * (scale * LOG2E)
                blocks = [_sb_scores(z2[:, j * b:(j + 1) * b], (qb - (g * nsub + j)) * b, dcol, tri_ex, masked)
                          for j in range(nsub)]
                run = c1s[qs]
                ws = [None] * nsub
                for j in reversed(range(nsub)):
                    mask, lb, l1m, ls_loc = blocks[j]
                    wj = jnp.exp2(lb + ls_loc + run)
                    ws[j] = (jnp.where(mask, wj, 0.0) if masked else wj).astype(BF16)
                    run = run + jnp.sum(l1m, axis=1, keepdims=True)
                w = jnp.concatenate(ws, axis=1)
                w_ref[0, g, qs * b:(qs + 1) * b, :] = w
                new_acc.append(accs[qs] + jnp.dot(w, vg, preferred_element_type=F32))
                new_c1.append(run)
            return tuple(new_c1), tuple(new_acc)

        init = (tuple(jnp.zeros((b, 1), F32) for _ in range(nqs)),
                tuple(jnp.zeros((b, HEAD_DIM), F32) for _ in range(nqs)))
        assert nqs == 2 and nsub % 2 == 0
        first = step(0, init, True)
        _, accs = lax.fori_loop(1, n_groups, functools.partial(step, masked=False), first)
        for qs in range(nqs):
            o_ref[qs * b:(qs + 1) * b, :] = accs[qs]

    return pl.pallas_call(
        body, name=name, grid=(h_n, s // tq),
        in_specs=[pl.BlockSpec((tq, HEAD_DIM), lambda h, i: (i, h)),
                  pl.BlockSpec((s, HEAD_DIM), lambda h, i: (0, h_n + h)),
                  pl.BlockSpec((s, HEAD_DIM), lambda h, i: (0, 2 * h_n + h))] + dep_specs,
        out_specs=[pl.BlockSpec((tq, HEAD_DIM), lambda h, i: (i, h)),
                   pl.BlockSpec((1, s // tk, tq, tk), lambda h, i: (h, 0, i, 0))],
        out_shape=[jax.ShapeDtypeStruct((s, h_n * HEAD_DIM), F32),
                   jax.ShapeDtypeStruct((h_n, s // tk, s, tk), BF16)],
        compiler_params=_params("parallel", "arbitrary"),
    )(p, p, p, *dep_args)


def sb_bwd(p, a, wts, da, n_heads, name, tq=256, dep=None):
    s = p.shape[0]
    h_n = n_heads
    nq = s // tq
    b = SB_BLK
    nqs = tq // b
    tk = wts.shape[3]
    nsub = tk // b
    scale = 1.0 / math.sqrt(HEAD_DIM)
    dep_args, dep_specs = _after(dep)

    def body(q_ref, k_ref, v_ref, a_ref, da_ref, w_ref, *rest):
        dq_ref, dk_ref, dv_ref, dk_acc, dv_acc = rest[-5:]
        qi = pl.program_id(1)

        @pl.when(qi == 0)
        def _():
            dk_acc[...] = jnp.zeros_like(dk_acc)
            dv_acc[...] = jnp.zeros_like(dv_acc)

        dcol, _, tri_in = _sb_consts()
        q_all = q_ref[...]
        do_all = da_ref[...]
        qv = [q_ref[qs * b:(qs + 1) * b, :] for qs in range(nqs)]
        dov = [da_ref[qs * b:(qs + 1) * b, :] for qs in range(nqs)]
        tots = [jnp.sum(dov[qs].astype(F32) * a_ref[qs * b:(qs + 1) * b, :], axis=1, keepdims=True)
                for qs in range(nqs)]
        n_groups = ((qi + 1) * nqs - 1) // nsub + 1

        def step(it, carry, masked):
            c2s, dqs = carry
            g = n_groups - 1 - it
            off = pl.multiple_of(g * tk, tk)
            kg = k_ref[pl.ds(off, tk), :]
            vg = v_ref[pl.ds(off, tk), :]
            w_all = w_ref[0, g]
            new_c2, new_dq, dz_rows = [], [], []
            for qs in range(nqs):
                qb = qi * nqs + qs
                z2 = lax.dot_general(qv[qs], kg, (((1,), (1,)), ((), ())),
                                     preferred_element_type=F32) * (-scale * LOG2E)
                dw = lax.dot_general(dov[qs], vg, (((1,), (1,)), ((), ())), preferred_element_type=F32)
                beta = 1.0 / (1.0 + jnp.exp2(z2))
                e = dw * w_all[qs * b:(qs + 1) * b, :].astype(F32)
                run2 = c2s[qs]
                dzs = [None] * nsub
                for j in reversed(range(nsub)):
                    cols = slice(j * b, (j + 1) * b)
                    later = _split_dot(e[:, cols], tri_in) + run2
                    bj = beta[:, cols]
                    dz = (e[:, cols] * (1.0 - bj) - bj * (tots[qs] - later)) * scale
                    if masked:
                        dz = jnp.where(dcol < (qb - (g * nsub + j)) * b, dz, 0.0)
                    dzs[j] = dz.astype(BF16)
                    run2 = run2 + jnp.sum(e[:, cols], axis=1, keepdims=True)
                dzq = jnp.concatenate(dzs, axis=1)
                new_dq.append(dqs[qs] + jnp.dot(dzq, kg, preferred_element_type=F32))
                new_c2.append(run2)
                dz_rows.append(dzq)
            dz_all = jnp.concatenate(dz_rows, axis=0)
            dk_acc[pl.ds(off, tk), :] += lax.dot_general(dz_all, q_all, (((0,), (0,)), ((), ())),
                                                         preferred_element_type=F32)
            dv_acc[pl.ds(off, tk), :] += lax.dot_general(w_all, do_all, (((0,), (0,)), ((), ())),
                                                         preferred_element_type=F32)
            return tuple(new_c2), tuple(new_dq)

        zeros = tuple(jnp.zeros((b, 1), F32) for _ in range(nqs))
        assert nqs == 2 and nsub % 2 == 0
        first = step(0, (zeros, tuple(jnp.zeros((b, HEAD_DIM), F32) for _ in range(nqs))), True)
        _, dqs = lax.fori_loop(1, n_groups, functools.partial(step, masked=False), first)
        for qs in range(nqs):
            dq_ref[qs * b:(qs + 1) * b, :] = dqs[qs].astype(BF16)

        @pl.when(qi == nq - 1)
        def _():
            dk_ref[...] = dk_acc[...].astype(BF16)
            dv_ref[...] = dv_acc[...].astype(BF16)

    blk = pl.BlockSpec((tq, HEAD_DIM), lambda h, i: (i, h))
    full = pl.BlockSpec((s, HEAD_DIM), lambda h, i: (0, h))
    return pl.pallas_call(
        body, name=name, grid=(h_n, nq),
        in_specs=[blk, pl.BlockSpec((s, HEAD_DIM), lambda h, i: (0, h_n + h)),
                  pl.BlockSpec((s, HEAD_DIM), lambda h, i: (0, 2 * h_n + h)), blk, blk,
                  pl.BlockSpec((1, s // tk, tq, tk), lambda h, i: (h, 0, i, 0))] + dep_specs,
        out_specs=[blk, full, full],
        out_shape=[jax.ShapeDtypeStruct((s, h_n * HEAD_DIM), BF16)] * 3,
        scratch_shapes=[pltpu.VMEM((s, HEAD_DIM), F32), pltpu.VMEM((s, HEAD_DIM), F32)],
        compiler_params=_params("parallel", "arbitrary"),
    )(p, p, p, a, da, wts, *dep_args)


def _pool_window(xx, win, r0, rc):
    cur = xx[HALO:HALO + rc]
    ws = _window_sum(xx, win, True)[HALO:HALO + rc]
    t_idx = r0 + lax.broadcasted_iota(jnp.int32, (rc, 1), 0)
    inv = 1.0 / jnp.minimum(win, t_idx + 1).astype(F32)
    return ws * inv - cur, inv


def even_mix_fwd(a, p, pool_w, pool_scale, name, rc=64, dep=None):
    s = p.shape[0]
    ng = len(POOL_WINDOWS)
    cw = pool_w.shape[1]
    n_chunks = s // rc
    dep_args, dep_specs = _after(dep)

    def body(a_ref, u_ref, g_ref, w_ref, sc_ref, *rest):
        y_ref, upad = rest[-2:]
        j = pl.program_id(0)

        @pl.when(j < ng)
        def _():
            def chunk(ci, carry):
                rows = pl.ds(pl.multiple_of(ci * rc, rc), rc)
                y_ref[rows, :] = (a_ref[rows, :] * _silu(g_ref[rows, :].astype(F32))).astype(BF16)
                return carry

            lax.fori_loop(0, n_chunks, chunk, 0)

        for gi, win in enumerate(POOL_WINDOWS):
            @pl.when(j == ng + gi)
            def _(win=win):
                upad[0:HALO, :] = jnp.zeros((HALO, cw), F32)

                def fill(ci, carry):
                    r0 = pl.multiple_of(ci * rc, rc)
                    upad[pl.ds(pl.multiple_of(r0 + HALO, HALO), rc), :] = u_ref[pl.ds(r0, rc), :].astype(F32)
                    return carry

                lax.fori_loop(0, n_chunks, fill, 0)

                def chunk(ci, carry):
                    r0 = pl.multiple_of(ci * rc, rc)
                    rows = pl.ds(r0, rc)
                    pooled, _ = _pool_window(upad[pl.ds(r0, HALO + rc), :], win, r0, rc)
                    t = jnp.dot(pooled.astype(BF16), w_ref[0], preferred_element_type=F32)
                    y_ref[rows, :] = (t * sc_ref[...] * _silu(g_ref[rows, :].astype(F32))).astype(BF16)
                    return carry

                lax.fori_loop(0, n_chunks, chunk, 0)

    grp = lambda j: jnp.maximum(j - ng, 0)
    return pl.pallas_call(
        body, name=name, grid=(2 * ng,),
        in_specs=[pl.BlockSpec((s, cw), lambda j: (0, jnp.minimum(j, ng - 1))),
                  pl.BlockSpec((s, cw), lambda j: (0, 3 * ng + grp(j))),
                  pl.BlockSpec((s, cw), lambda j: (0, 4 * ng + j)),
                  pl.BlockSpec((1, cw, cw), lambda j: (grp(j), 0, 0)),
                  pl.BlockSpec((1, cw), lambda j: (0, grp(j)))] + dep_specs,
        out_specs=pl.BlockSpec((s, cw), lambda j: (0, j)),
        out_shape=jax.ShapeDtypeStruct((s, 2 * ng * cw), BF16),
        scratch_shapes=[pltpu.VMEM((HALO + s, cw), F32)],
        compiler_params=_params("arbitrary"),
    )(a, p, p, pool_w, pool_scale, *dep_args)


def even_mix_bwd(dy, a, p, pool_w, pool_scale, name, rc=64):
    s = p.shape[0]
    ng = len(POOL_WINDOWS)
    cw = pool_w.shape[1]
    n_chunks = s // rc

    def body(dy_ref, a_ref, u_ref, g_ref, w_ref, sc_ref, da_ref, du_ref, dg_ref, dw_ref, dsc_ref,
             upad, rpad, dpl, dw_acc, dsc_acc):
        j = pl.program_id(0)

        @pl.when(j < ng)
        def _():
            def chunk(ci, carry):
                rows = pl.ds(pl.multiple_of(ci * rc, rc), rc)
                dyv = dy_ref[rows, :].astype(F32)
                sg, dsg = _silu_and_grad(g_ref[rows, :].astype(F32))
                da_ref[rows, :] = (dyv * sg).astype(BF16)
                dg_ref[rows, :] = (dyv * a_ref[rows, :] * dsg).astype(BF16)
                return carry

            lax.fori_loop(0, n_chunks, chunk, 0)

        for gi, win in enumerate(POOL_WINDOWS):
            @pl.when(j == ng + gi)
            def _(win=win):
                upad[0:HALO, :] = jnp.zeros((HALO, cw), F32)
                rpad[s:s + HALO, :] = jnp.zeros((HALO, cw), F32)
                dw_acc[...] = jnp.zeros_like(dw_acc)
                dsc_acc[...] = jnp.zeros_like(dsc_acc)

                def fill(ci, carry):
                    r0 = pl.multiple_of(ci * rc, rc)
                    upad[pl.ds(pl.multiple_of(r0 + HALO, HALO), rc), :] = u_ref[pl.ds(r0, rc), :].astype(F32)
                    return carry

                lax.fori_loop(0, n_chunks, fill, 0)

                def chunk(ci, carry):
                    r0 = pl.multiple_of(ci * rc, rc)
                    rows = pl.ds(r0, rc)
                    pooled, inv = _pool_window(upad[pl.ds(r0, HALO + rc), :], win, r0, rc)
                    pb = pooled.astype(BF16)
                    wv = w_ref[0]
                    t = jnp.dot(pb, wv, preferred_element_type=F32)
                    scv = sc_ref[...]
                    dyv = dy_ref[rows, :].astype(F32)
                    sg, dsg = _silu_and_grad(g_ref[rows, :].astype(F32))
                    dpo = dyv * sg
                    dg_ref[rows, :] = (dyv * t * scv * dsg).astype(BF16)
                    dsc_acc[...] += _rowsum8(dpo * t)
                    dtb = (dpo * scv).astype(BF16)
                    dw_acc[...] += lax.dot_general(pb, dtb, (((0,), (0,)), ((), ())),
                                                   preferred_element_type=F32)
                    dpooled = lax.dot_general(dtb, wv, (((1,), (1,)), ((), ())),
                                              preferred_element_type=F32)
                    dpl[rows, :] = dpooled
                    rpad[rows, :] = dpooled * inv
                    return carry

                lax.fori_loop(0, n_chunks, chunk, 0)

                def chunk2(ci, carry):
                    r0 = pl.multiple_of(ci * rc, rc)
                    rows = pl.ds(r0, rc)
                    xx = rpad[pl.ds(r0, rc + HALO), :]
                    fs = _window_sum(xx, win, False)[0:rc]
                    du_ref[rows, :] = (fs - dpl[rows, :]).astype(BF16)
                    return carry

                lax.fori_loop(0, n_chunks, chunk2, 0)
                dw_ref[0] = dw_acc[...]
                dsc_ref[...] = jnp.sum(dsc_acc[...], axis=0, keepdims=True)

    grp = lambda j: jnp.maximum(j - ng, 0)
    att = lambda j: jnp.minimum(j, ng - 1)
    return pl.pallas_call(
        body, name=name, grid=(2 * ng,),
        in_specs=[pl.BlockSpec((s, cw), lambda j: (0, j)),
                  pl.BlockSpec((s, cw), lambda j: (0, att(j))),
                  pl.BlockSpec((s, cw), lambda j: (0, 3 * ng + grp(j))),
                  pl.BlockSpec((s, cw), lambda j: (0, 4 * ng + j)),
                  pl.BlockSpec((1, cw, cw), lambda j: (grp(j), 0, 0)),
                  pl.BlockSpec((1, cw), lambda j: (0, grp(j)))],
        out_specs=[pl.BlockSpec((s, cw), lambda j: (0, att(j))),
                   pl.BlockSpec((s, cw), lambda j: (0, grp(j))),
                   pl.BlockSpec((s, cw), lambda j: (0, j)),
                   pl.BlockSpec((1, cw, cw), lambda j: (grp(j), 0, 0)),
                   pl.BlockSpec((1, cw), lambda j: (0, grp(j)))],
        out_shape=[jax.ShapeDtypeStruct((s, ng * cw), BF16), jax.ShapeDtypeStruct((s, ng * cw), BF16),
                   jax.ShapeDtypeStruct((s, 2 * ng * cw), BF16),
                   jax.ShapeDtypeStruct((ng, cw, cw), F32), jax.ShapeDtypeStruct((1, ng * cw), F32)],
        scratch_shapes=[pltpu.VMEM((HALO + s, cw), F32), pltpu.VMEM((s + HALO, cw), F32),
                        pltpu.VMEM((s, cw), F32), pltpu.VMEM((cw, cw), F32), pltpu.VMEM((8, cw), F32)],
        compiler_params=_params("arbitrary"),
    )(dy, a, p, p, pool_w, pool_scale)


def _halo_before(tm):
    return lambda i: jnp.maximum(i * (tm // HALO) - 1, 0)


def _halo_after(tm, s):
    return lambda i: jnp.minimum((i + 1) * (tm // HALO), s // HALO - 1)


def odd_mix_fwd(p, sconv_w, dconv_w, dconv_b, cnorm_g, cnorm_b, name, tm=128):
    s = p.shape[0]
    cw = sconv_w.shape[1]
    n = s // tm
    lanes = 128
    hb = _halo_before(tm)

    def body(hc_ref, hch_ref, bc_ref, cc_ref, cch_ref, ga_ref, gah_ref, gb_ref, gbh_ref, g1_ref, g2_ref,
             sw_ref, dw_ref, db_ref, gam_ref, bet_ref, y_ref, dc_ref):
        first = pl.program_id(0) == 0
        for l in range(cw // lanes):
            cols = slice(l * lanes, (l + 1) * lanes)
            mh = jnp.where(first, 0.0, cch_ref[:, cols].astype(F32) * hch_ref[:, cols].astype(F32))
            mm = cc_ref[:, cols].astype(F32) * hc_ref[:, cols].astype(F32)
            xx = jnp.concatenate([mh, mm], axis=0)
            tap = _Taps(xx, tm, True)
            cv = jnp.zeros((tm, lanes), F32)
            for k in range(SCONV_K):
                cv = cv + sw_ref[k:k + 1, cols] * tap(SCONV_K - 1 - k)
            c_out = bc_ref[:, cols].astype(F32) * cv
            y_ref[:, cols] = (c_out * _silu(g1_ref[:, cols].astype(F32))).astype(BF16)
            dh = jnp.where(first, 0.0, gah_ref[:, cols].astype(F32) * _sigmoid(gbh_ref[:, cols].astype(F32)))
            dm = ga_ref[:, cols].astype(F32) * _sigmoid(gb_ref[:, cols].astype(F32))
            xx = jnp.concatenate([dh, dm], axis=0)
            tap = _Taps(xx, tm, True)
            acc = jnp.zeros((tm, lanes), F32) + db_ref[:, cols]
            for k in range(CONF_K):
                acc = acc + dw_ref[k:k + 1, cols] * tap(CONF_K - 1 - k)
            dc_ref[:, cols] = acc
        rs = 32
        for r in range(tm // rs):
            rows = slice(r * rs, (r + 1) * rs)
            xv = dc_ref[rows, :]
            mu = jnp.mean(xv, axis=-1, keepdims=True)
            xc = xv - mu
            rstd = lax.rsqrt(jnp.mean(xc * xc, axis=-1, keepdims=True) + EPS)
            ln = xc * rstd * gam_ref[...] + bet_ref[...]
            y_ref[rows, cw:2 * cw] = (_silu(ln) * _silu(g2_ref[rows, :].astype(F32))).astype(BF16)

    main = lambda c: pl.BlockSpec((tm, cw), lambda i: (i, c))
    halo = lambda c: pl.BlockSpec((HALO, cw), lambda i: (hb(i), c))
    vec = lambda r: pl.BlockSpec((r, cw), lambda i: (0, 0))
    return pl.pallas_call(
        body, name=name, grid=(n,),
        in_specs=[main(0), halo(0), main(1), main(2), halo(2), main(3), halo(3), main(4), halo(4),
                  main(5), main(6), vec(SCONV_K), vec(CONF_K), vec(1), vec(1), vec(1)],
        out_specs=[pl.BlockSpec((tm, 2 * cw), lambda i: (i, 0)), pl.BlockSpec((tm, cw), lambda i: (i, 0))],
        out_shape=[jax.ShapeDtypeStruct((s, 2 * cw), BF16), jax.ShapeDtypeStruct((s, cw), F32)],
        compiler_params=_params("parallel"),
    )(p, p, p, p, p, p, p, p, p, p, p, sconv_w, dconv_w, dconv_b, cnorm_g, cnorm_b)


def odd_bwd_ln(dy, p, dc, cnorm_g, cnorm_b, name, tm=256):
    s = p.shape[0]
    cw = dc.shape[1]
    n = s // tm
    rs = 32

    def body(dy_ref, g2_ref, dc_ref, gam_ref, bet_ref, ddc_ref, dg_ref, dgam_ref, dbet_ref, gacc, bacc):
        i = pl.program_id(0)

        @pl.when(i == 0)
        def _():
            gacc[...] = jnp.zeros_like(gacc)
            bacc[...] = jnp.zeros_like(bacc)

        def chunk(ci, carry):
            rows = pl.ds(pl.multiple_of(ci * rs, rs), rs)
            xv = dc_ref[rows, :]
            mu = jnp.mean(xv, axis=-1, keepdims=True)
            xc = xv - mu
            rstd = lax.rsqrt(jnp.mean(xc * xc, axis=-1, keepdims=True) + EPS)
            xh = xc * rstd
            gam = gam_ref[...]
            sl, dsl = _silu_and_grad(xh * gam + bet_ref[...])
            sg, dsg = _silu_and_grad(g2_ref[rows, :].astype(F32))
            dyv = dy_ref[rows, :].astype(F32)
            dg_ref[rows, :] = (dyv * sl * dsg).astype(BF16)
            dln = dyv * sg * dsl
            gacc[...] += _rowsum8(dln * xh)
            bacc[...] += _rowsum8(dln)
            dxh = dln * gam
            ddc_ref[rows, :] = rstd * (dxh - jnp.mean(dxh, axis=-1, keepdims=True)
                                       - xh * jnp.mean(dxh * xh, axis=-1, keepdims=True))
            return carry

        lax.fori_loop(0, tm // rs, chunk, 0)

        @pl.when(i == n - 1)
        def _():
            dgam_ref[...] = jnp.sum(gacc[...], axis=0, keepdims=True)
            dbet_ref[...] = jnp.sum(bacc[...], axis=0, keepdims=True)

    vec = pl.BlockSpec((1, cw), lambda i: (0, 0))
    return pl.pallas_call(
        body, name=name, grid=(n,),
        in_specs=[pl.BlockSpec((tm, cw), lambda i: (i, 1)), pl.BlockSpec((tm, cw), lambda i: (i, 6)),
                  pl.BlockSpec((tm, cw), lambda i: (i, 0)), vec, vec],
        out_specs=[pl.BlockSpec((tm, cw), lambda i: (i, 0)), pl.BlockSpec((tm, cw), lambda i: (i, 0)), vec, vec],
        out_shape=[jax.ShapeDtypeStruct((s, cw), F32), jax.ShapeDtypeStruct((s, cw), BF16),
                   jax.ShapeDtypeStruct((1, cw), F32), jax.ShapeDtypeStruct((1, cw), F32)],
        scratch_shapes=[pltpu.VMEM((8, cw), F32), pltpu.VMEM((8, cw), F32)],
        compiler_params=_params("arbitrary"),
    )(dy, p, dc, cnorm_g, cnorm_b)


def odd_bwd_conv(dy, p, ddc, dg2, sconv_w, dconv_w, name, tm=128):
    s = p.shape[0]
    cw = ddc.shape[1]
    n = s // tm
    lanes = 128
    hb = _halo_before(tm)
    ha = _halo_after(tm, s)

    def body(dy_ref, dya_ref, g1_ref, g1a_ref, bc_ref, bca_ref, hc_ref, hch_ref, cc_ref, cch_ref,
             ddc_ref, ddca_ref, ga_ref, gah_ref, gb_ref, gbh_ref, dg2_ref, sw_ref, dw_ref,
             dp_ref, dsw_ref, ddw_ref, ddb_ref, sw_acc, dw_acc, db_acc):
        i = pl.program_id(0)
        first = i == 0
        last = i == n - 1

        @pl.when(first)
        def _():
            sw_acc[...] = jnp.zeros_like(sw_acc)
            dw_acc[...] = jnp.zeros_like(dw_acc)
            db_acc[...] = jnp.zeros_like(db_acc)

        for l in range(cw // lanes):
            cols = slice(l * lanes, (l + 1) * lanes)
            mh = jnp.where(first, 0.0, cch_ref[:, cols].astype(F32) * hch_ref[:, cols].astype(F32))
            hcv = hc_ref[:, cols].astype(F32)
            ccv = cc_ref[:, cols].astype(F32)
            xx = jnp.concatenate([mh, ccv * hcv], axis=0)
            tap = _Taps(xx, tm, True)
            taps = [tap(SCONV_K - 1 - k) for k in range(SCONV_K)]
            cv = jnp.zeros((tm, lanes), F32)
            for k in range(SCONV_K):
                cv = cv + sw_ref[k:k + 1, cols] * taps[k]
            bcv = bc_ref[:, cols].astype(F32)
            dyv = dy_ref[:, cols].astype(F32)
            sg, dsg = _silu_and_grad(g1_ref[:, cols].astype(F32))
            dco = dyv * sg
            dp_ref[:, 5 * cw + l * lanes:5 * cw + (l + 1) * lanes] = (dyv * bcv * cv * dsg).astype(BF16)
            dp_ref[:, cw + l * lanes:cw + (l + 1) * lanes] = (dco * cv).astype(BF16)
            dcv = dco * bcv
            for k in range(SCONV_K):
                sw_acc[k * 8:(k + 1) * 8, cols] += _rowsum8(dcv * taps[k])
            dcv_a = jnp.where(last, 0.0, dya_ref[:, cols].astype(F32) * _silu(g1a_ref[:, cols].astype(F32))
                              * bca_ref[:, cols].astype(F32))
            xx = jnp.concatenate([dcv, dcv_a], axis=0)
            tap = _Taps(xx, tm, False)
            dm = jnp.zeros((tm, lanes), F32)
            for k in range(SCONV_K):
                dm = dm + sw_ref[k:k + 1, cols] * tap(SCONV_K - 1 - k)
            dp_ref[:, l * lanes:(l + 1) * lanes] = (dm * ccv).astype(BF16)
            dp_ref[:, 2 * cw + l * lanes:2 * cw + (l + 1) * lanes] = (dm * hcv).astype(BF16)
            gav = ga_ref[:, cols].astype(F32)
            sb = _sigmoid(gb_ref[:, cols].astype(F32))
            dh = jnp.where(first, 0.0, gah_ref[:, cols].astype(F32) * _sigmoid(gbh_ref[:, cols].astype(F32)))
            xx = jnp.concatenate([dh, gav * sb], axis=0)
            ddcv = ddc_ref[:, cols]
            db_acc[:, cols] += _rowsum8(ddcv)
            tap = _Taps(xx, tm, True)
            for k in range(CONF_K):
                dw_acc[k * 8:(k + 1) * 8, cols] += _rowsum8(ddcv * tap(CONF_K - 1 - k))
            ddc_a = jnp.where(last, 0.0, ddca_ref[:, cols])
            xx = jnp.concatenate([ddcv, ddc_a], axis=0)
            tap = _Taps(xx, tm, False)
            dgl = jnp.zeros((tm, lanes), F32)
            for k in range(CONF_K):
                dgl = dgl + dw_ref[k:k + 1, cols] * tap(CONF_K - 1 - k)
            dp_ref[:, 3 * cw + l * lanes:3 * cw + (l + 1) * lanes] = (dgl * sb).astype(BF16)
            dp_ref[:, 4 * cw + l * lanes:4 * cw + (l + 1) * lanes] = (dgl * gav * sb * (1.0 - sb)).astype(BF16)
        dp_ref[:, 6 * cw:7 * cw] = dg2_ref[...]

        @pl.when(last)
        def _():
            for k in range(SCONV_K):
                dsw_ref[k:k + 1, :] = jnp.sum(sw_acc[k * 8:(k + 1) * 8, :], axis=0, keepdims=True)
            for k in range(CONF_K):
                ddw_ref[k:k + 1, :] = jnp.sum(dw_acc[k * 8:(k + 1) * 8, :], axis=0, keepdims=True)
            ddb_ref[...] = jnp.sum(db_acc[...], axis=0, keepdims=True)

    def main(c):
        return pl.BlockSpec((tm, cw), lambda i: (i, c))

    def before(c):
        return pl.BlockSpec((HALO, cw), lambda i: (hb(i), c))

    def after(c):
        return pl.BlockSpec((HALO, cw), lambda i: (ha(i), c))

    def vec(r):
        return pl.BlockSpec((r, cw), lambda i: (0, 0))

    return pl.pallas_call(
        body, name=name, grid=(n,),
        in_specs=[main(0), after(0), main(5), after(5), main(1), after(1), main(0), before(0), main(2), before(2),
                  main(0), after(0), main(3), before(3), main(4), before(4), main(0), vec(SCONV_K), vec(CONF_K)],
        out_specs=[pl.BlockSpec((tm, 7 * cw), lambda i: (i, 0)), vec(SCONV_K), vec(CONF_K), vec(1)],
        out_shape=[jax.ShapeDtypeStruct((s, 7 * cw), BF16), jax.ShapeDtypeStruct((SCONV_K, cw), F32),
                   jax.ShapeDtypeStruct((CONF_K, cw), F32), jax.ShapeDtypeStruct((1, cw), F32)],
        scratch_shapes=[pltpu.VMEM((8 * SCONV_K, cw), F32), pltpu.VMEM((8 * CONF_K, cw), F32),
                        pltpu.VMEM((8, cw), F32)],
        compiler_params=_params("arbitrary"),
    )(dy, dy, p, p, p, p, p, p, p, p, ddc, ddc, p, p, p, p, dg2, sconv_w, dconv_w)


_ANY = pl.BlockSpec(memory_space=pl.ANY)


def _place():
    return lax.axis_index("x"), lax.axis_index("y"), lax.axis_index("c")


def all_gather(arrs, name, deps=()):
    n = len(arrs)

    def body(*refs):
        ins, outs = refs[:n], refs[n + len(deps):2 * n + len(deps)]
        send_sems, recv_sems, local_sems = refs[-3:]
        x, y, c = _place()
        me, sibling = (x, y, c), (x, y, 1 - c)
        chips = [(1 - x, y), (x, 1 - y), (1 - x, 1 - y)]

        def copy(a, k, block, to, src=None):
            px, py, pc = block
            dst = outs[a].at[4 * px + 2 * py + pc]
            return pltpu.make_async_remote_copy(
                src_ref=dst if src is None else src, dst_ref=dst,
                send_sem=send_sems.at[7 * a + k], recv_sem=recv_sems.at[7 * a + k],
                device_id=to, device_id_type=MESH)

        mine = [pltpu.make_async_copy(ins[a], outs[a].at[4 * x + 2 * y + c], local_sems.at[a]) for a in range(n)]
        first = []
        for a in range(n):
            first.append(copy(a, 0, me, sibling, src=ins[a]))
            first += [copy(a, 1 + j, me, (*chip, c), src=ins[a]) for j, chip in enumerate(chips)]
        for cp in first + mine:
            cp.start()
        passed = []
        for a in range(n):
            for j, chip in enumerate(chips):
                copy(a, 1 + j, (*chip, c), me).wait_recv()
                cp = copy(a, 4 + j, (*chip, c), sibling)
                cp.start()
                passed.append(cp)
        for a in range(n):
            copy(a, 0, sibling, me).wait_recv()
            for j, chip in enumerate(chips):
                copy(a, 4 + j, (*chip, 1 - c), me).wait_recv()
        for cp in first + passed:
            cp.wait_send()
        for cp in mine:
            cp.wait()

    return pl.pallas_call(
        body, name=name,
        out_shape=[jax.ShapeDtypeStruct((N_DEV,) + a.shape, a.dtype) for a in arrs],
        in_specs=[_ANY] * (n + len(deps)), out_specs=[_ANY] * n,
        scratch_shapes=[pltpu.SemaphoreType.DMA((7 * n,)), pltpu.SemaphoreType.DMA((7 * n,)),
                        pltpu.SemaphoreType.DMA((n,))],
    )(*arrs, *deps)


def in_proj_gathered(h, w_own, extras, name, tm=512):
    s, d = h.shape
    n = w_own.shape[1]
    arrs = [w_own] + list(extras)
    na = len(arrs)

    def body(*refs):
        h_ref, ins = refs[0], refs[1:1 + na]
        p_ref, outs = refs[1 + na], refs[2 + na:2 + 2 * na]
        wbuf, obuf, send_sems, recv_sems, load_sem, store_sems, own_sems = refs[2 + 2 * na:]
        x, y, c = _place()
        me, sibling = (x, y, c), (x, y, 1 - c)
        chips = [(1 - x, y), (x, 1 - y), (1 - x, 1 - y)]

        def slot(block):
            return 4 * block[0] + 2 * block[1] + block[2]

        def copy(a, k, block, to, src=None):
            dst = outs[a].at[slot(block)]
            return pltpu.make_async_remote_copy(
                src_ref=dst if src is None else src, dst_ref=dst,
                send_sem=send_sems.at[7 * a + k], recv_sem=recv_sems.at[7 * a + k],
                device_id=to, device_id_type=MESH)

        first = []
        for a in range(na):
            first.append(copy(a, 0, me, sibling, src=ins[a]))
            first += [copy(a, 1 + j, me, (*chip, c), src=ins[a]) for j, chip in enumerate(chips)]
        for cp in first:
            cp.start()
        own = pltpu.make_async_copy(wbuf.at[0], outs[0].at[slot(me)], own_sems.at[0])
        mine = [pltpu.make_async_copy(ins[a], outs[a].at[slot(me)], own_sems.at[a]) for a in range(1, na)]
        stores = [None, None]

        def multiply(k, block, w_from):
            b = k % 2
            if k == 2:
                own.wait()
            load = pltpu.make_async_copy(w_from, wbuf.at[b], load_sem)
            load.start()
            if stores[b] is not None:
                stores[b].wait()
            load.wait()
            if k == 0:
                own.start()

            def chunk(i, carry):
                rows = pl.ds(pl.multiple_of(i * tm, tm), tm)
                obuf[b, rows, :] = jnp.dot(h_ref[rows, :], wbuf[b], preferred_element_type=F32).astype(BF16)
                return carry

            lax.fori_loop(0, s // tm, chunk, 0)
            stores[b] = pltpu.make_async_copy(
                obuf.at[b], p_ref.at[:, pl.ds(pl.multiple_of(slot(block) * n, 128), n)], store_sems.at[b])
            stores[b].start()

        def arrived(arrays, j, chip):
            for a in arrays:
                copy(a, 1 + j, (*chip, c), me).wait_recv()
                cp = copy(a, 4 + j, (*chip, c), sibling)
                cp.start()
                passed.append(cp)

        small = range(1, na)
        passed = []
        multiply(0, me, ins[0])
        copy(0, 0, sibling, me).wait_recv()
        multiply(1, sibling, outs[0].at[slot(sibling)])
        for j, chip in enumerate(chips):
            arrived([0], j, chip)
            multiply(2 + 2 * j, (*chip, c), outs[0].at[slot((*chip, c))])
            copy(0, 4 + j, (*chip, 1 - c), me).wait_recv()
            multiply(3 + 2 * j, (*chip, 1 - c), outs[0].at[slot((*chip, 1 - c))])
        for cp in mine:
            cp.start()
        for a in small:
            copy(a, 0, sibling, me).wait_recv()
        for j, chip in enumerate(chips):
            arrived(small, j, chip)
        for j, chip in enumerate(chips):
            for a in small:
                copy(a, 4 + j, (*chip, 1 - c), me).wait_recv()
        for cp in first + passed:
            cp.wait_send()
        for cp in mine + stores:
            cp.wait()

    vmem = pl.BlockSpec(memory_space=pltpu.VMEM)
    outs = pl.pallas_call(
        body, name=name,
        out_shape=[jax.ShapeDtypeStruct((s, N_DEV * n), BF16)]
        + [jax.ShapeDtypeStruct((N_DEV,) + a.shape, a.dtype) for a in arrs],
        in_specs=[vmem] + [_ANY] * na, out_specs=[_ANY] * (1 + na),
        scratch_shapes=[pltpu.VMEM((2, d, n), BF16), pltpu.VMEM((2, s, n), BF16),
                        pltpu.SemaphoreType.DMA((7 * na,)), pltpu.SemaphoreType.DMA((7 * na,)),
                        pltpu.SemaphoreType.DMA, pltpu.SemaphoreType.DMA((2,)), pltpu.SemaphoreType.DMA((na,))],
        compiler_params=pltpu.CompilerParams(vmem_limit_bytes=VMEM_LIMIT),
    )(h, *arrs)
    return outs[0], outs[1], outs[2:]


_HBM = pl.BlockSpec(memory_space=pltpu.HBM)
_SEM = pl.BlockSpec(memory_space=pltpu.SEMAPHORE)
_DATAFLOW = pltpu.SideEffectType.DATAFLOW_SIDE_EFFECTING


def _peers_per_array(kind):
    return 1 if kind in ("sibling", "halves") else 3


def _split_copies(kind, srcs, lands, send_sems, recv_sems):
    x, y, c = _place()
    per = _peers_per_array(kind)
    out = []
    for a in range(len(lands)):
        if kind == "sibling":
            peers = [((x, y, 1 - c), srcs[a].at[:, pl.ds(1 - c, 1)], lands[a], lands[a])]
        elif kind == "halves":
            mine, its = lands[a].at[:, pl.ds(c, 1)], lands[a].at[:, pl.ds(1 - c, 1)]
            peers = [((x, y, 1 - c), mine, mine, its)]
        else:
            peers = []
            for px, py in [(1 - x, y), (x, 1 - y), (1 - x, 1 - y)]:
                if kind == "gather":
                    views = (srcs[a], lands[a].at[4 * x + 2 * y + c], lands[a].at[4 * px + 2 * py + c])
                else:
                    views = (srcs[a].at[2 * px + py], lands[a].at[2 * x + y], lands[a].at[2 * px + py])
                peers.append(((px, py, c),) + views)
        for j, (peer, src, dst, arrives) in enumerate(peers):
            sems = dict(send_sem=send_sems.at[per * a + j], recv_sem=recv_sems.at[per * a + j],
                        device_id=peer, device_id_type=MESH)
            out.append((pltpu.make_async_remote_copy(src_ref=src, dst_ref=dst, **sems),
                        pltpu.make_async_remote_copy(src_ref=src, dst_ref=arrives, **sems)))
    return out


def split_start(kind, srcs, lands, deps, name):
    ns, nl = len(srcs), len(lands)
    n_sems = _peers_per_array(kind) * nl
    held = list(srcs) + list(lands)

    def body(*refs):
        send_sems, recv_sems = refs[len(held) + len(deps)], refs[len(held) + len(deps) + 1]
        for copy, _ in _split_copies(kind, refs[:ns], refs[ns:ns + nl], send_sems, recv_sems):
            copy.start()
        token = refs[-1]
        token[...] = jnp.zeros_like(token)

    outs = pl.pallas_call(
        body, name=name,
        out_shape=(pltpu.SemaphoreType.DMA((n_sems,)), pltpu.SemaphoreType.DMA((n_sems,)),
                   *[pltpu.HBM(a.shape, a.dtype) for a in held], jax.ShapeDtypeStruct((8, 128), F32)),
        in_specs=[_HBM] * len(held) + [_ANY] * len(deps),
        out_specs=(_SEM, _SEM, *([_HBM] * len(held)), pl.BlockSpec(memory_space=pltpu.VMEM)),
        input_output_aliases={i: 2 + i for i in range(len(held))},
        compiler_params=pltpu.CompilerParams(has_side_effects=_DATAFLOW),
    )(*[pltpu.with_memory_space_constraint(a, pltpu.HBM) for a in held], *deps)
    return outs[0], outs[1], list(outs[2:2 + ns]), list(outs[2 + ns:2 + ns + nl]), outs[-1]


def split_wait(kind, send_sems, recv_sems, srcs, lands, afters, name):
    ns, nl = len(srcs), len(lands)
    held = list(srcs) + list(lands)

    def body(*refs):
        for _, arrival in _split_copies(kind, refs[:ns], refs[ns:ns + nl], refs[ns + nl], refs[ns + nl + 1]):
            arrival.wait_send()
            arrival.wait_recv()

    outs = pl.pallas_call(
        body, name=name,
        out_shape=[pltpu.HBM(a.shape, a.dtype) for a in held],
        in_specs=[_HBM] * len(held) + [_SEM, _SEM] + [_ANY] * len(afters),
        out_specs=[_HBM] * len(held),
        input_output_aliases={i: i for i in range(len(held))},
        compiler_params=pltpu.CompilerParams(has_side_effects=_DATAFLOW),
    )(*held, send_sems, recv_sems, *afters)
    return list(outs[:ns]), list(outs[ns:])


def place_block(land, block, dev, name):
    r, c = block.shape
    tr = min(r, 512)

    def body(dev_ref, land_ref, b_ref, o_ref):
        del dev_ref, land_ref
        o_ref[...] = b_ref[...]

    return pl.pallas_call(
        body, name=name,
        grid_spec=pltpu.PrefetchScalarGridSpec(
            num_scalar_prefetch=1, grid=(r // tr,),
            in_specs=[_ANY, pl.BlockSpec((tr, c), lambda i, dev_ref: (i, 0))],
            out_specs=pl.BlockSpec((None, tr, c), lambda i, dev_ref: (dev_ref[0], i, 0))),
        out_shape=jax.ShapeDtypeStruct(land.shape, land.dtype),
        input_output_aliases={1: 0},
        compiler_params=_params("parallel"),
    )(dev, land, block)


def pair_add(own, recv, core, name):
    _, _, r, c = own.shape
    tr = min(r, 512)

    def body(core_ref, own_ref, recv_ref, o_ref):
        del core_ref
        o_ref[...] = (own_ref[...].astype(F32) + recv_ref[...].astype(F32)).astype(BF16)

    return pl.pallas_call(
        body, name=name,
        grid_spec=pltpu.PrefetchScalarGridSpec(
            num_scalar_prefetch=1, grid=(4, r // tr),
            in_specs=[pl.BlockSpec((None, None, tr, c), lambda k, i, core_ref: (k, core_ref[0], i, 0)),
                      pl.BlockSpec((None, None, tr, c), lambda k, i, core_ref: (k, 0, i, 0))],
            out_specs=pl.BlockSpec((None, tr, c), lambda k, i, core_ref: (k, i, 0))),
        out_shape=jax.ShapeDtypeStruct((4, r, c), BF16),
        compiler_params=_params("parallel", "parallel"),
    )(core, own, recv)


def _adamw_math(w, g, m, v):
    m2 = ADAM_B1 * m + (1.0 - ADAM_B1) * g
    v2 = ADAM_B2 * v + (1.0 - ADAM_B2) * (g * g)
    m_hat = m2 / (1.0 - ADAM_B1 ** ADAM_STEP)
    v_hat = v2 / (1.0 - ADAM_B2 ** ADAM_STEP)
    delta = -ADAM_LR * (m_hat / (jnp.sqrt(v_hat) + ADAM_EPS) + ADAM_WD * w)
    return delta, m2, v2


def adamw_big(w, m, v, own, got, chip, name):
    r, c = w.shape
    tr = min(r, 256)

    def body(chip_ref, w_ref, m_ref, v_ref, p0, p1, p2, p3, g_ref, d_ref, m2_ref, v2_ref):
        del chip_ref
        g = ((p0[...].astype(F32) + p1[...].astype(F32)) + p2[...].astype(F32)) + p3[...].astype(F32)
        delta, m2, v2 = _adamw_math(w_ref[...], g, m_ref[...], v_ref[...])
        g_ref[...] = g
        d_ref[...] = delta
        m2_ref[...] = m2
        v2_ref[...] = v2

    row = pl.BlockSpec((tr, c), lambda i, chip_ref: (i, 0))

    def slab(flip):
        return pl.BlockSpec((None, tr, c), lambda i, chip_ref: (chip_ref[0] ^ flip, i, 0))

    return pl.pallas_call(
        body, name=name,
        grid_spec=pltpu.PrefetchScalarGridSpec(
            num_scalar_prefetch=1, grid=(r // tr,),
            in_specs=[row, row, row, slab(0), slab(1), slab(2), slab(3)],
            out_specs=[row] * 4),
        out_shape=[jax.ShapeDtypeStruct((r, c), F32)] * 4,
        compiler_params=_params("parallel"),
    )(chip, w, m, v, own, got, got, got)


def sum_devices(g8, name):
    def body(g_ref, o_ref):
        tot = g_ref[0]
        for k in range(1, N_DEV):
            tot = tot + g_ref[k]
        o_ref[...] = tot

    return pl.pallas_call(body, name=name, out_shape=jax.ShapeDtypeStruct(g8.shape[1:], F32))(g8)


def adamw_small(ws, gs, ms, vs, name):
    n = len(ws)

    def body(*refs):
        w_r, g_r, m_r, v_r = refs[:n], refs[n:2 * n], refs[2 * n:3 * n], refs[3 * n:4 * n]
        d_o, m_o, v_o = refs[4 * n:5 * n], refs[5 * n:6 * n], refs[6 * n:7 * n]
        for k in range(n):
            delta, m2, v2 = _adamw_math(w_r[k][...], g_r[k][...], m_r[k][...], v_r[k][...])
            d_o[k][...] = delta
            m_o[k][...] = m2
            v_o[k][...] = v2

    shapes = [jax.ShapeDtypeStruct(w.shape, F32) for w in ws]
    outs = pl.pallas_call(body, name=name, out_shape=shapes * 3)(*ws, *gs, *ms, *vs)
    return outs[:n], outs[n:2 * n], outs[2 * n:]


def _rows128(a):
    return a.reshape(-1, 128)


def _pad_rows(a, rows):
    return jnp.pad(a, ((0, rows - a.shape[0]), (0, 0)))


def kernel(x, ln_pre_even, w_in_even, pool_w, pool_scale, w_out_even, ln_post_even, ln_pre_odd, w_in_odd, sconv_w, dconv_w, dconv_b, cnorm_g, cnorm_b, w_out_odd, ln_post_odd, loss_target, m_ln_pre_even, m_w_in_even, m_pool_w, m_pool_scale, m_w_out_even, m_ln_post_even, m_ln_pre_odd, m_w_in_odd, m_sconv_w, m_dconv_w, m_dconv_b, m_cnorm_g, m_cnorm_b, m_w_out_odd, m_ln_post_odd, v_ln_pre_even, v_w_in_even, v_pool_w, v_pool_scale, v_w_out_even, v_ln_post_even, v_ln_pre_odd, v_w_in_odd, v_sconv_w, v_dconv_w, v_dconv_b, v_cnorm_g, v_cnorm_b, v_w_out_odd, v_ln_post_odd):
    xs = x[0]
    tgt = loss_target[0]
    s, d = xs.shape
    half = d // 2
    n_heads = half // HEAD_DIM
    ng = len(POOL_WINDOWS)
    cwp = half // ng
    dev = 4 * lax.axis_index("x") + 2 * lax.axis_index("y") + lax.axis_index("c")
    core = lax.axis_index("c").astype(jnp.int32).reshape(1)

    pr = pool_w.shape[2]
    cl = sconv_w.shape[2]
    small_parts = [(_rows128(ln_pre_odd), 8), (sconv_w[0], 8), (dconv_w[0], 32), (dconv_b, 8),
                   (cnorm_g, 8), (cnorm_b, 8), (_rows128(ln_post_odd), 8)]
    small_local = jnp.concatenate([_pad_rows(a, r) for a, r in small_parts], axis=0)
    h0 = rms_fwd(xs, ln_pre_even, "rms_pre_even")
    p0, g_wie, (g_pw, g_small) = in_proj_gathered(
        h0, w_in_even[0].astype(BF16), [pool_w[0].reshape(ng * pr, cwp).astype(BF16), small_local],
        "ag_in_proj_even")
    comm = _Exchanges(dev, core, d)
    token = comm.start_weights("out_even", [w_out_even[0].astype(BF16)], [p0])
    sb_dep = comm.start_weights("odd", [w_in_odd[0].astype(BF16), w_out_odd[0].astype(BF16)], [token])
    pool_full = g_pw.reshape(N_DEV, ng, pr, cwp).transpose(1, 0, 2, 3).reshape(ng, cwp, cwp)
    nl = ln_pre_odd.shape[1] // 128

    def chan(lo, rows):
        return g_small[:, lo:lo + rows].transpose(1, 0, 2).reshape(rows, N_DEV * cl)

    ln_pre_odd_f = g_small[:, 0:nl].reshape(1, d)
    sconv_f = chan(8, SCONV_K)
    dconv_f = chan(16, CONF_K)
    dconv_b_f = chan(48, 1)
    cnorm_g_f = chan(56, 1)
    cnorm_b_f = chan(64, 1)
    ln_post_odd_f = g_small[:, 72:72 + nl].reshape(1, d)

    loss_blk, grad_x, small_g = _fwd_bwd(
        xs, tgt, ln_pre_even, h0, p0, g_wie, pool_full, pool_scale, ln_post_even, ln_pre_odd_f,
        sconv_f, dconv_f, dconv_b_f, cnorm_g_f, cnorm_b_f, ln_post_odd_f, comm, sb_dep)
    small_w = [ln_pre_even, pool_scale, ln_post_even, ln_pre_odd, sconv_w[0], dconv_w[0], dconv_b, cnorm_g, cnorm_b, ln_post_odd]
    small_m = [m_ln_pre_even, m_pool_scale, m_ln_post_even, m_ln_pre_odd, m_sconv_w[0], m_dconv_w[0], m_dconv_b, m_cnorm_g, m_cnorm_b, m_ln_post_odd]
    small_v = [v_ln_pre_even, v_pool_scale, v_ln_post_even, v_ln_pre_odd, v_sconv_w[0], v_dconv_w[0], v_dconv_b, v_cnorm_g, v_cnorm_b, v_ln_post_odd]
    big = {"w_in_even": (w_in_even, m_w_in_even, v_w_in_even), "pool_w": (pool_w, m_pool_w, v_pool_w),
           "w_out_even": (w_out_even, m_w_out_even, v_w_out_even), "w_in_odd": (w_in_odd, m_w_in_odd, v_w_in_odd),
           "w_out_odd": (w_out_odd, m_w_out_odd, v_w_out_odd)}
    upd = comm.finish_updates(big, [grad_x])
    upd.update(comm.finish_updates(big, [grad_x]))
    sg, sd, sm, sv, loss = _update_small(small_g, loss_blk, small_w, small_m, small_v, dev, d, cl,
                                         deps=[upd["w_in_odd"][1], upd["w_out_even"][1]])
    upd.update(comm.finish_updates(big, sd))
    (g_wie_o, d_wie, m_wie, v_wie), (g_pw_o, d_pw, m_pw, v_pw) = upd["w_in_even"], upd["pool_w"]
    (g_woe_o, d_woe, m_woe, v_woe), (g_wio_o, d_wio, m_wio, v_wio) = upd["w_out_even"], upd["w_in_odd"]
    g_woo_o, d_woo, m_woo, v_woo = upd["w_out_odd"]

    def order(small, wie, pw, woe, wio, woo):
        return [small[0], wie, pw, small[1], woe, small[2], small[3], wio, small[4], small[5], small[6],
                small[7], small[8], woo, small[9]]

    grads = order(sg, g_wie_o, g_pw_o, g_woe_o, g_wio_o, g_woo_o)
    deltas = order(sd, d_wie, d_pw, d_woe, d_wio, d_woo)
    new_m = order(sm, m_wie, m_pw, m_woe, m_wio, m_woo)
    new_v = order(sv, v_wie, v_pw, v_woe, v_wio, v_woo)
    return (loss, grad_x[None], *grads, *deltas, *new_m, *new_v)


def _fwd_bwd(xs, tgt, ln_pre_even, h0, p0, g_wie, pool_full, pool_scale, ln_post_even, ln_pre_odd_f,
             sconv_f, dconv_f, dconv_b_f, cnorm_g_f, cnorm_b_f, ln_post_odd_f, comm, sb_dep):
    d = xs.shape[1]
    n_heads = d // 2 // HEAD_DIM
    ng, cwp = pool_full.shape[0], pool_full.shape[1]
    a0, sb_wts = sb_fwd(p0, n_heads, "sb_fwd", dep=sb_dep)
    dep = comm.weights_arrived("out_even", after=a0)
    y0 = even_mix_fwd(a0, p0, pool_full, pool_scale, "even_mix_fwd", dep=dep)
    (w_out_e,) = comm.weights("out_even", after=y0)
    w_out_e = w_out_e.reshape(1, d, d)
    o0 = mm_nn(y0, w_out_e, F32, "out_proj_even", tm=1024, tn=1024)
    dep = comm.weights_arrived("odd", after=o0)
    x1, h1 = postnorm_fwd(xs, o0, ln_post_even, ln_pre_odd_f, "post_even", dep=dep)
    g_wio, w_out_o = comm.weights("odd", after=x1)
    w_out_o = w_out_o.reshape(1, d, d)
    p1 = mm_nn(h1, g_wio, BF16, "in_proj_odd")
    y1, dc = odd_mix_fwd(p1, sconv_f, dconv_f, dconv_b_f, cnorm_g_f, cnorm_b_f, "odd_mix_fwd")
    o1 = mm_nn(y1, w_out_o, F32, "out_proj_odd", tm=1024, tn=1024)
    loss_blk, gx2, do1, dg_post_odd = final_fwd_bwd(x1, o1, ln_post_odd_f, tgt, "post_odd_loss")

    dw_out_o = mm_tn(y1, do1, 1, BF16, "dw_out_odd", tm=1024)
    dy1 = mm_nt(do1, w_out_o, BF16, "dy_odd", tn=1024)
    ddc, dg2, dgam, dbet = odd_bwd_ln(dy1, p1, dc, cnorm_g_f, cnorm_b_f, "odd_bwd_ln")
    dp1, dsconv, ddconv, ddconv_b = odd_bwd_conv(dy1, p1, ddc, dg2, sconv_f, dconv_f, "odd_bwd_conv")
    dw_in_o = mm_tn(h1, dp1, N_DEV, BF16, "dw_in_odd")
    dep = comm.reduce_begin({"w_out_odd": dw_out_o.reshape(N_DEV, d // N_DEV, d), "w_in_odd": dw_in_o}, "odd")
    dh1 = mm_nt(dp1, g_wio, F32, "dh_odd", dep=dep)
    dep = comm.reduce_send(after=dh1)
    gx1, dg_pre_odd, do0, dg_post_even = norm_bwd(dh1, x1, ln_pre_odd_f, gx2, "pre_odd_post_even_bwd",
                                                  inp2=o0, g2=ln_post_even, dep=dep)

    dw_out_e = mm_tn(y0, do0, 1, BF16, "dw_out_even", tm=1024)
    dy0 = mm_nt(do0, w_out_e, BF16, "dy_even", tn=1024)
    da0, du0, dg0, dpool, dpool_scale = even_mix_bwd(dy0, a0, p0, pool_full, pool_scale, "even_mix_bwd")
    pr = cwp // N_DEV
    dpool_slabs = dpool.astype(BF16).reshape(ng, N_DEV, pr, cwp).transpose(1, 0, 2, 3).reshape(N_DEV, ng * pr, cwp)
    dep = comm.reduce_begin({"w_out_even": dw_out_e.reshape(N_DEV, d // N_DEV, d), "pool_w": dpool_slabs}, "even_out")
    dq0, dk0, dv0 = sb_bwd(p0, a0, sb_wts, da0, n_heads, "sb_bwd", dep=dep)
    dep = comm.reduce_send(after=dq0)
    dp0 = jnp.concatenate([dq0, dk0, dv0, du0, dg0], axis=1)
    dw_in_e = mm_tn(h0, dp0, N_DEV, BF16, "dw_in_even", dep=dep)
    comm.reduce_begin({"w_in_even": dw_in_e}, "even_in")
    dep = comm.reduce_send(after=dw_in_e)
    dh0 = mm_nt(dp0, g_wie, F32, "dh_even", dep=dep)
    dep = None
    grad_x, dg_pre_even = norm_bwd(dh0, xs, ln_pre_even, gx1, "pre_even_bwd", dep=dep)
    small_g = [dg_pre_even, dpool_scale, dg_post_even, dg_pre_odd, dsconv, ddconv, ddconv_b, dgam, dbet, dg_post_odd]
    return loss_blk, grad_x, small_g


class _Exchanges:
    def __init__(self, dev, core, d):
        self.dev = dev.astype(jnp.int32).reshape(1)
        self.core = core
        self.chip = (dev // 2).astype(jnp.int32).reshape(1)
        self.d = d
        self.in_flight = {}
        self.to_sibling = None
        self.pending = []

    def start_weights(self, tag, blocks, afters):
        lands = [lax.empty((N_DEV,) + b.shape, b.dtype) for b in blocks]
        send, recv, srcs, lands, token = split_start("gather", blocks, lands, afters, "ag_start_" + tag)
        self.in_flight[tag] = (send, recv, srcs, lands)
        return token

    def weights_arrived(self, tag, after):
        send, recv, srcs, lands = self.in_flight.pop(tag)
        srcs, lands = split_wait("gather", send, recv, srcs, lands, [after], "ag_wait_" + tag)
        lands = [place_block(l, b, self.dev, "ag_own_%s_%d" % (tag, k)) for k, (l, b) in enumerate(zip(lands, srcs))]
        lands = [l.reshape((4, 2) + l.shape[1:]) for l in lands]
        send, recv, _, lands, token = split_start("halves", [], lands, [], "ag_sibling_start_" + tag)
        self.in_flight[tag] = (send, recv, lands)
        return token

    def weights(self, tag, after):
        send, recv, lands = self.in_flight.pop(tag)
        _, lands = split_wait("halves", send, recv, [], lands, [after], "ag_sibling_wait_" + tag)
        return [l.reshape((N_DEV,) + l.shape[2:]) for l in lands]

    def reduce_begin(self, partials, tag):
        names = list(partials)
        arrs = [partials[k].reshape((4, 2) + partials[k].shape[1:]) for k in names]
        lands = [lax.empty((4, 1) + a.shape[2:], a.dtype) for a in arrs]
        send, recv, srcs, lands, token = split_start("sibling", arrs, lands, [], "rs_sibling_start_" + tag)
        self.to_sibling = (tag, names, send, recv, srcs, lands)
        return token

    def reduce_send(self, after):
        tag, names, send, recv, srcs, lands = self.to_sibling
        srcs, lands = split_wait("sibling", send, recv, srcs, lands, [after], "rs_sibling_wait_" + tag)
        sums = [pair_add(o, r, self.core, "rs_pair_add_" + k) for k, o, r in zip(names, srcs, lands)]
        zones = [lax.empty(a.shape, a.dtype) for a in sums]
        send, recv, srcs, zones, token = split_start("scatter", sums, zones, [], "rs_start_" + tag)
        self.pending.append((tag, names, send, recv, srcs, zones))
        return token

    def finish_updates(self, big, afters):
        tag, names, send, recv, srcs, lands = self.pending.pop(0)
        srcs, lands = split_wait("scatter", send, recv, srcs, lands, afters, "rs_wait_" + tag)
        out = {}
        for name, own, got in zip(names, srcs, lands):
            w, m, v = big[name]
            shp = own.shape[1:]
            outs = adamw_big(w.reshape(shp), m.reshape(shp), v.reshape(shp), own, got, self.chip, "adamw_" + name)
            out[name] = [o.reshape(w.shape) for o in outs]
        return out


def _update_small(small_g, loss_blk, small_w, small_m, small_v, dev, d, cl, deps):
    packed = jnp.concatenate([_rows128(g) for g in small_g] + [loss_blk], axis=0)
    (g8,) = all_gather([packed], "ag_small_grads", deps)
    tot = sum_devices(g8, "sum_small_grads")
    loss = tot[packed.shape[0] - 8, 0]
    full_g = []
    lo = 0
    for g in small_g:
        rows = g.size // 128
        full_g.append(tot[lo:lo + rows].reshape(g.shape))
        lo += rows

    def mine(g, width):
        return lax.dynamic_slice_in_dim(g, dev * width, width, axis=g.ndim - 1)

    fg = full_g
    small_gl = [fg[0], fg[1], fg[2], mine(fg[3], d // N_DEV), mine(fg[4], cl), mine(fg[5], cl), mine(fg[6], cl),
                mine(fg[7], cl), mine(fg[8], cl), mine(fg[9], d // N_DEV)]
    sd, sm, sv = adamw_small(small_w, small_gl, small_m, small_v, "adamw_small")

    def like(k, a):
        return a[None] if k in (4, 5) else a

    sg = [like(k, a) for k, a in enumerate(small_gl)]
    sd = [like(k, a) for k, a in enumerate(sd)]
    sm = [like(k, a) for k, a in enumerate(sm)]
    sv = [like(k, a) for k, a in enumerate(sv)]
    return sg, sd, sm, sv, loss
```

```python
import functools
import math

import jax
import jax.numpy as jnp
from jax import lax
from jax.experimental import pallas as pl
from jax.experimental.pallas import tpu as pltpu

F32 = jnp.float32
BF16 = jnp.bfloat16
EPS = 1e-6
HEAD_DIM = 128
POOL_WINDOWS = (2, 4, 8, 16)
SCONV_K = 3
CONF_K = 31
HALO = 32
N_DEV = 8
VMEM_LIMIT = 56 * 1024 * 1024
MESH = pl.DeviceIdType.MESH

ADAM_LR = 0.001
ADAM_B1 = 0.9
ADAM_B2 = 0.999
ADAM_EPS = 1e-08
ADAM_WD = 0.01
ADAM_STEP = 10


def _params(*sem):
    return pltpu.CompilerParams(dimension_semantics=sem, vmem_limit_bytes=VMEM_LIMIT)


def _sigmoid(v):
    return 1.0 / (1.0 + jnp.exp(-v))


def _silu(v):
    return v * _sigmoid(v)


def _silu_and_grad(v):
    s = _sigmoid(v)
    return v * s, s * (1.0 + v * (1.0 - s))


def _rowsum8(v):
    r, c = v.shape
    return jnp.sum(v.reshape(r // 8, 8, c), axis=0)


SUBLANES = 8


class _Taps:
    def __init__(self, xx, rows, before):
        self.xx, self.rows, self.before, self.rotated = xx, rows, before, {}

    def __call__(self, i):
        r, q = i % SUBLANES, i // SUBLANES
        if r not in self.rotated:
            n = self.xx.shape[0]
            self.rotated[r] = self.xx if r == 0 else pltpu.roll(self.xx, r if self.before else n - r, 0)
        lo = HALO - SUBLANES * q if self.before else SUBLANES * q
        return self.rotated[r][lo:lo + self.rows]


def _window_sum(xx, win, before):
    n = xx.shape[0]
    acc = xx
    k = 1
    while k < win:
        acc = acc + pltpu.roll(acc, k if before else n - k, 0)
        k *= 2
    return acc


def rms_fwd(x, g, name, tm=256):
    s, d = x.shape

    def body(x_ref, g_ref, h_ref):
        xv = x_ref[...]
        r = lax.rsqrt(jnp.mean(xv * xv, axis=-1, keepdims=True) + EPS)
        h_ref[...] = (xv * r * g_ref[...]).astype(BF16)

    return pl.pallas_call(
        body, name=name, grid=(s // tm,),
        in_specs=[pl.BlockSpec((tm, d), lambda i: (i, 0)), pl.BlockSpec((1, d), lambda i: (0, 0))],
        out_specs=pl.BlockSpec((tm, d), lambda i: (i, 0)),
        out_shape=jax.ShapeDtypeStruct((s, d), BF16),
        compiler_params=_params("parallel"),
    )(x, g)


def postnorm_fwd(x, o, g, g_next, name, tm=256, dep=None):
    s, d = x.shape
    dep_args, dep_specs = _after(dep)

    def body(x_ref, o_ref, g_ref, gn_ref, *rest):
        y_ref, h_ref = rest[-2:]
        ov = o_ref[...]
        r = lax.rsqrt(jnp.mean(ov * ov, axis=-1, keepdims=True) + EPS)
        y = x_ref[...] + ov * r * g_ref[...]
        y_ref[...] = y
        r2 = lax.rsqrt(jnp.mean(y * y, axis=-1, keepdims=True) + EPS)
        h_ref[...] = (y * r2 * gn_ref[...]).astype(BF16)

    row = pl.BlockSpec((tm, d), lambda i: (i, 0))
    vec = pl.BlockSpec((1, d), lambda i: (0, 0))
    return pl.pallas_call(
        body, name=name, grid=(s // tm,),
        in_specs=[row, row, vec, vec] + dep_specs, out_specs=[row, row],
        out_shape=[jax.ShapeDtypeStruct((s, d), F32), jax.ShapeDtypeStruct((s, d), BF16)],
        compiler_params=_params("parallel"),
    )(x, o, g, g_next, *dep_args)


def final_fwd_bwd(x1, o, g, target, name, tm=256):
    s, d = x1.shape
    n = s // tm

    def body(x_ref, o_ref, g_ref, t_ref, loss_ref, gx_ref, do_ref, dg_ref, lacc, gacc):
        i = pl.program_id(0)

        @pl.when(i == 0)
        def _():
            lacc[...] = jnp.zeros_like(lacc)
            gacc[...] = jnp.zeros_like(gacc)

        ov = o_ref[...]
        gv = g_ref[...]
        r = lax.rsqrt(jnp.mean(ov * ov, axis=-1, keepdims=True) + EPS)
        oh = ov * r
        diff = x_ref[...] + oh * gv - t_ref[...]
        lacc[...] += _rowsum8(diff * diff)
        gx = diff * (1.0 / d)
        gx_ref[...] = gx
        gacc[...] += _rowsum8(gx * oh)
        dn = gx * gv
        do_ref[...] = (r * (dn - oh * jnp.mean(dn * oh, axis=-1, keepdims=True))).astype(BF16)

        @pl.when(i == n - 1)
        def _():
            tot = jnp.sum(jnp.sum(lacc[...], axis=0, keepdims=True), axis=1, keepdims=True)
            loss_ref[...] = jnp.broadcast_to(tot * (0.5 / d), loss_ref.shape)
            dg_ref[...] = jnp.sum(gacc[...], axis=0, keepdims=True)

    row = pl.BlockSpec((tm, d), lambda i: (i, 0))
    vec = pl.BlockSpec((1, d), lambda i: (0, 0))
    return pl.pallas_call(
        body, name=name, grid=(n,),
        in_specs=[row, row, vec, row],
        out_specs=[pl.BlockSpec((8, 128), lambda i: (0, 0)), row, row, vec],
        out_shape=[jax.ShapeDtypeStruct((8, 128), F32), jax.ShapeDtypeStruct((s, d), F32),
                   jax.ShapeDtypeStruct((s, d), BF16), jax.ShapeDtypeStruct((1, d), F32)],
        scratch_shapes=[pltpu.VMEM((8, d), F32), pltpu.VMEM((8, d), F32)],
        compiler_params=_params("arbitrary"),
    )(x1, o, g, target)


def _rms_bwd_rows(dyv, xv, gv):
    r = lax.rsqrt(jnp.mean(xv * xv, axis=-1, keepdims=True) + EPS)
    xh = xv * r
    dn = dyv * gv
    return r * (dn - xh * jnp.mean(dn * xh, axis=-1, keepdims=True)), _rowsum8(dyv * xh)


def norm_bwd(dy, inp, g, resid, name, inp2=None, g2=None, tm=256, dep=None):
    s, d = inp.shape
    n = s // tm
    chain = inp2 is not None

    def body(*refs):
        dy_ref, x_ref, g_ref, r_ref = refs[:4]
        outs = refs[-6:] if chain else refs[-3:]
        i = pl.program_id(0)

        @pl.when(i == 0)
        def _():
            for acc in outs[-2:] if chain else outs[-1:]:
                acc[...] = jnp.zeros_like(acc)

        if chain:
            x2_ref, g2_ref = refs[4:6]
            dx_ref, dg_ref, dx2_ref, dg2_ref, gacc, gacc2 = outs
        else:
            dx_ref, dg_ref, gacc = outs
        dx, dg_rows = _rms_bwd_rows(dy_ref[...].astype(F32), x_ref[...], g_ref[...])
        dx = dx + r_ref[...]
        dx_ref[...] = dx
        gacc[...] += dg_rows
        if chain:
            dx2, dg2_rows = _rms_bwd_rows(dx, x2_ref[...], g2_ref[...])
            dx2_ref[...] = dx2.astype(BF16)
            gacc2[...] += dg2_rows

        @pl.when(i == n - 1)
        def _():
            dg_ref[...] = jnp.sum(gacc[...], axis=0, keepdims=True)
            if chain:
                dg2_ref[...] = jnp.sum(gacc2[...], axis=0, keepdims=True)

    row = pl.BlockSpec((tm, d), lambda i: (i, 0))
    vec = pl.BlockSpec((1, d), lambda i: (0, 0))
    dep_args, dep_specs = _after(dep)
    extra = [inp2, g2] if chain else []
    return pl.pallas_call(
        body, name=name, grid=(n,),
        in_specs=[row, row, vec, row] + ([row, vec] if chain else []) + dep_specs,
        out_specs=[row, vec] * (2 if chain else 1),
        out_shape=[jax.ShapeDtypeStruct((s, d), F32), jax.ShapeDtypeStruct((1, d), F32)]
        + ([jax.ShapeDtypeStruct((s, d), BF16), jax.ShapeDtypeStruct((1, d), F32)] if chain else []),
        scratch_shapes=[pltpu.VMEM((8, d), F32)] * (2 if chain else 1),
        compiler_params=_params("arbitrary"),
    )(dy, inp, g, resid, *extra, *dep_args)


def _after(dep):
    if dep is None:
        return [], []
    return [dep], [pl.BlockSpec((8, 128), lambda *_: (0, 0))]


def mm_nn(a, w, out_dtype, name, tm=2048, tn=None, dep=None):
    m, k = a.shape
    tm = min(tm, m)
    ns, _, n = w.shape
    tn = n if tn is None else tn
    nj = n // tn
    dep_args, dep_specs = _after(dep)

    def body(a_ref, w_ref, *rest):
        o_ref = rest[-1]
        o_ref[...] = jnp.dot(a_ref[...], w_ref[0], preferred_element_type=F32).astype(out_dtype)

    return pl.pallas_call(
        body, name=name, grid=(ns, nj, m // tm),
        in_specs=[pl.BlockSpec((tm, k), lambda s, j, i: (i, 0)),
                  pl.BlockSpec((1, k, tn), lambda s, j, i: (s, 0, j))] + dep_specs,
        out_specs=pl.BlockSpec((tm, tn), lambda s, j, i: (i, s * nj + j)),
        out_shape=jax.ShapeDtypeStruct((m, ns * n), out_dtype),
        compiler_params=_params("parallel", "parallel", "parallel"),
    )(a, w, *dep_args)


def mm_nt(a, w, out_dtype, name, tm=1024, tn=None, dep=None):
    m = a.shape[0]
    tm = min(tm, m)
    ns, k, n = w.shape
    tn = n if tn is None else tn
    nj = n // tn
    steps = ns * nj
    dep_args, dep_specs = _after(dep)

    def body(a_ref, w_ref, *rest):
        o_ref, acc = rest[-2:]
        r = pl.program_id(1)

        @pl.when(r == 0)
        def _():
            acc[...] = jnp.zeros_like(acc)

        acc[...] += lax.dot_general(a_ref[...], w_ref[0], (((1,), (1,)), ((), ())),
                                    preferred_element_type=F32)

        @pl.when(r == steps - 1)
        def _():
            o_ref[...] = acc[...].astype(out_dtype)

    return pl.pallas_call(
        body, name=name, grid=(m // tm, steps),
        in_specs=[pl.BlockSpec((tm, tn), lambda i, r: (i, r)),
                  pl.BlockSpec((1, k, tn), lambda i, r: (r // nj, 0, r % nj))] + dep_specs,
        out_specs=pl.BlockSpec((tm, k), lambda i, r: (i, 0)),
        out_shape=jax.ShapeDtypeStruct((m, k), out_dtype),
        scratch_shapes=[pltpu.VMEM((tm, k), F32)],
        compiler_params=_params("parallel", "arbitrary"),
    )(a, w, *dep_args)


def mm_tn(a, b, ns, out_dtype, name, tk=1024, tm=2048, dep=None):
    m, k = a.shape
    tm = min(tm, m)
    n = b.shape[1] // ns
    steps = m // tm
    dep_args, dep_specs = _after(dep)

    def body(a_ref, b_ref, *rest):
        o_ref, acc = rest[-2:]
        r = pl.program_id(2)

        @pl.when(r == 0)
        def _():
            acc[...] = jnp.zeros_like(acc)

        acc[...] += lax.dot_general(a_ref[...], b_ref[...], (((0,), (0,)), ((), ())),
                                    preferred_element_type=F32)

        @pl.when(r == steps - 1)
        def _():
            o_ref[0] = acc[...].astype(out_dtype)

    return pl.pallas_call(
        body, name=name, grid=(ns, k // tk, steps),
        in_specs=[pl.BlockSpec((tm, tk), lambda s, j, r: (r, j)),
                  pl.BlockSpec((tm, n), lambda s, j, r: (r, s))] + dep_specs,
        out_specs=pl.BlockSpec((1, tk, n), lambda s, j, r: (s, j, 0)),
        out_shape=jax.ShapeDtypeStruct((ns, k, n), out_dtype),
        scratch_shapes=[pltpu.VMEM((tk, n), F32)],
        compiler_params=_params("parallel", "parallel", "arbitrary"),
    )(a, b, *dep_args)


SB_BLK = 128


LOG2E = 1.0 / math.log(2.0)


def _split_dot(v, tri2):
    hi = pltpu.bitcast(pltpu.bitcast(v, jnp.uint32) & jnp.uint32(0xFFFF0000), F32)
    lo = (v - hi).astype(BF16)
    return jnp.dot(jnp.concatenate([hi.astype(BF16), lo], axis=1), tri2, preferred_element_type=F32)


def _sb_scores(z2, lim, dcol, tri_ex, masked):
    sp = jnp.log2(1.0 + jnp.exp2(-jnp.abs(z2)))
    lb = jnp.minimum(z2, 0.0) - sp
    l1m = lb - z2
    mask = None
    if masked:
        mask = dcol < lim
        l1m = jnp.where(mask, l1m, 0.0)
    return mask, lb, l1m, _split_dot(l1m, tri_ex)


def _sb_consts():
    row = lax.broadcasted_iota(jnp.int32, (SB_BLK, SB_BLK), 0)
    col = lax.broadcasted_iota(jnp.int32, (SB_BLK, SB_BLK), 1)
    tri_ex = jnp.where(row > col, 1.0, 0.0).astype(BF16)
    tri_in = jnp.where(row >= col, 1.0, 0.0).astype(BF16)
    return col - row, jnp.concatenate([tri_ex, tri_ex], axis=0), jnp.concatenate([tri_in, tri_in], axis=0)


def sb_fwd(p, n_heads, name, tq=256, nsub=4, dep=None):
    s = p.shape[0]
    h_n = n_heads
    b = SB_BLK
    nqs = tq // b
    tk = nsub * b
    scale = 1.0 / math.sqrt(HEAD_DIM)

    dep_args, dep_specs = _after(dep)

    def body(q_ref, k_ref, v_ref, *rest):
        o_ref, w_ref = rest[-2:]
        qi = pl.program_id(1)
        dcol, tri_ex, _ = _sb_consts()
        qv = [q_ref[qs * b:(qs + 1) * b, :] for qs in range(nqs)]
        n_groups = ((qi + 1) * nqs - 1) // nsub + 1

        def step(it, carry, masked):
            c1s, accs = carry
            g = n_groups - 1 - it
            off = pl.multiple_of(g * tk, tk)
            kg = k_ref[pl.ds(off, tk), :]
            vg = v_ref[pl.ds(off, tk), :]
            new_c1, new_acc = [], []
            for qs in range(nqs):
                qb = qi * nqs + qs
                z2 = lax.dot_general(qv[qs], kg, (((1,), (1,)), ((), ())),
                                     preferred_element_type=F32) * (scale * LOG2E)
                blocks = [_sb_scores(z2[:, j * b:(j + 1) * b], (qb - (g * nsub + j)) * b, dcol, tri_ex, masked)
                          for j in range(nsub)]
                run = c1s[qs]
                ws = [None] * nsub
                for j in reversed(range(nsub)):
                    mask, lb, l1m, ls_loc = blocks[j]
                    wj = jnp.exp2(lb + ls_loc + run)
                    ws[j] = (jnp.where(mask, wj, 0.0) if masked else wj).astype(BF16)
                    run = run + jnp.sum(l1m, axis=1, keepdims=True)
                w = jnp.concatenate(ws, axis=1)
                w_ref[0, g, qs * b:(qs + 1) * b, :] = w
                new_acc.append(accs[qs] + jnp.dot(w, vg, preferred_element_type=F32))
                new_c1.append(run)
            return tuple(new_c1), tuple(new_acc)

        init = (tuple(jnp.zeros((b, 1), F32) for _ in range(nqs)),
                tuple(jnp.zeros((b, HEAD_DIM), F32) for _ in range(nqs)))
        assert nqs == 2 and nsub % 2 == 0
        first = step(0, init, True)
        _, accs = lax.fori_loop(1, n_groups, functools.partial(step, masked=False), first)
        for qs in range(nqs):
            o_ref[qs * b:(qs + 1) * b, :] = accs[qs]

    return pl.pallas_call(
        body, name=name, grid=(h_n, s // tq),
        in_specs=[pl.BlockSpec((tq, HEAD_DIM), lambda h, i: (i, h)),
                  pl.BlockSpec((s, HEAD_DIM), lambda h, i: (0, h_n + h)),
                  pl.BlockSpec((s, HEAD_DIM), lambda h, i: (0, 2 * h_n + h))] + dep_specs,
        out_specs=[pl.BlockSpec((tq, HEAD_DIM), lambda h, i: (i, h)),
                   pl.BlockSpec((1, s // tk, tq, tk), lambda h, i: (h, 0, i, 0))],
        out_shape=[jax.ShapeDtypeStruct((s, h_n * HEAD_DIM), F32),
                   jax.ShapeDtypeStruct((h_n, s // tk, s, tk), BF16)],
        compiler_params=_params("parallel", "arbitrary"),
    )(p, p, p, *dep_args)


def sb_bwd(p, a, wts, da, n_heads, name, tq=256, dep=None):
    s = p.shape[0]
    h_n = n_heads
    nq = s // tq
    b = SB_BLK
    nqs = tq // b
    tk = wts.shape[3]
    nsub = tk // b
    scale = 1.0 / math.sqrt(HEAD_DIM)
    dep_args, dep_specs = _after(dep)

    def body(q_ref, k_ref, v_ref, a_ref, da_ref, w_ref, *rest):
        dq_ref, dk_ref, dv_ref, dk_acc, dv_acc = rest[-5:]
        qi = pl.program_id(1)

        @pl.when(qi == 0)
        def _():
            dk_acc[...] = jnp.zeros_like(dk_acc)
            dv_acc[...] = jnp.zeros_like(dv_acc)

        dcol, _, tri_in = _sb_consts()
        q_all = q_ref[...]
        do_all = da_ref[...]
        qv = [q_ref[qs * b:(qs + 1) * b, :] for qs in range(nqs)]
        dov = [da_ref[qs * b:(qs + 1) * b, :] for qs in range(nqs)]
        tots = [jnp.sum(dov[qs].astype(F32) * a_ref[qs * b:(qs + 1) * b, :], axis=1, keepdims=True)
                for qs in range(nqs)]
        n_groups = ((qi + 1) * nqs - 1) // nsub + 1

        def step(it, carry, masked):
            c2s, dqs = carry
            g = n_groups - 1 - it
            off = pl.multiple_of(g * tk, tk)
            kg = k_ref[pl.ds(off, tk), :]
            vg = v_ref[pl.ds(off, tk), :]
            w_all = w_ref[0, g]
            new_c2, new_dq, dz_rows = [], [], []
            for qs in range(nqs):
                qb = qi * nqs + qs
                z2 = lax.dot_general(qv[qs], kg, (((1,), (1,)), ((), ())),
                                     preferred_element_type=F32) * (-scale * LOG2E)
                dw = lax.dot_general(dov[qs], vg, (((1,), (1,)), ((), ())), preferred_element_type=F32)
                beta = 1.0 / (1.0 + jnp.exp2(z2))
                e = dw * w_all[qs * b:(qs + 1) * b, :].astype(F32)
                run2 = c2s[qs]
                dzs = [None] * nsub
                for j in reversed(range(nsub)):
                    cols = slice(j * b, (j + 1) * b)
                    later = _split_dot(e[:, cols], tri_in) + run2
                    bj = beta[:, cols]
                    dz = (e[:, cols] * (1.0 - bj) - bj * (tots[qs] - later)) * scale
                    if masked:
                        dz = jnp.where(dcol < (qb - (g * nsub + j)) * b, dz, 0.0)
                    dzs[j] = dz.astype(BF16)
                    run2 = run2 + jnp.sum(e[:, cols], axis=1, keepdims=True)
                dzq = jnp.concatenate(dzs, axis=1)
                new_dq.append(dqs[qs] + jnp.dot(dzq, kg, preferred_element_type=F32))
                new_c2.append(run2)
                dz_rows.append(dzq)
            dz_all = jnp.concatenate(dz_rows, axis=0)
            dk_acc[pl.ds(off, tk), :] += lax.dot_general(dz_all, q_all, (((0,), (0,)), ((), ())),
                                                         preferred_element_type=F32)
            dv_acc[pl.ds(off, tk), :] += lax.dot_general(w_all, do_all, (((0,), (0,)), ((), ())),
                                                         preferred_element_type=F32)
            return tuple(new_c2), tuple(new_dq)

        zeros = tuple(jnp.zeros((b, 1), F32) for _ in range(nqs))
        assert nqs == 2 and nsub % 2 == 0
        first = step(0, (zeros, tuple(jnp.zeros((b, HEAD_DIM), F32) for _ in range(nqs))), True)
        _, dqs = lax.fori_loop(1, n_groups, functools.partial(step, masked=False), first)
        for qs in range(nqs):
            dq_ref[qs * b:(qs + 1) * b, :] = dqs[qs].astype(BF16)

        @pl.when(qi == nq - 1)
        def _():
            dk_ref[...] = dk_acc[...].astype(BF16)
            dv_ref[...] = dv_acc[...].astype(BF16)

    blk = pl.BlockSpec((tq, HEAD_DIM), lambda h, i: (i, h))
    full = pl.BlockSpec((s, HEAD_DIM), lambda h, i: (0, h))
    return pl.pallas_call(
        body, name=name, grid=(h_n, nq),
        in_specs=[blk, pl.BlockSpec((s, HEAD_DIM), lambda h, i: (0, h_n + h)),
                  pl.BlockSpec((s, HEAD_DIM), lambda h, i: (0, 2 * h_n + h)), blk, blk,
                  pl.BlockSpec((1, s // tk, tq, tk), lambda h, i: (h, 0, i, 0))] + dep_specs,
        out_specs=[blk, full, full],
        out_shape=[jax.ShapeDtypeStruct((s, h_n * HEAD_DIM), BF16)] * 3,
        scratch_shapes=[pltpu.VMEM((s, HEAD_DIM), F32), pltpu.VMEM((s, HEAD_DIM), F32)],
        compiler_params=_params("parallel", "arbitrary"),
    )(p, p, p, a, da, wts, *dep_args)


def _pool_window(xx, win, r0, rc):
    cur = xx[HALO:HALO + rc]
    ws = _window_sum(xx, win, True)[HALO:HALO + rc]
    t_idx = r0 + lax.broadcasted_iota(jnp.int32, (rc, 1), 0)
    inv = 1.0 / jnp.minimum(win, t_idx + 1).astype(F32)
    return ws * inv - cur, inv


def even_mix_fwd(a, p, pool_w, pool_scale, name, rc=64, dep=None):
    s = p.shape[0]
    ng = len(POOL_WINDOWS)
    cw = pool_w.shape[1]
    n_chunks = s // rc
    dep_args, dep_specs = _after(dep)

    def body(a_ref, u_ref, g_ref, w_ref, sc_ref, *rest):
        y_ref, upad = rest[-2:]
        j = pl.program_id(0)

        @pl.when(j < ng)
        def _():
            def chunk(ci, carry):
                rows = pl.ds(pl.multiple_of(ci * rc, rc), rc)
                y_ref[rows, :] = (a_ref[rows, :] * _silu(g_ref[rows, :].astype(F32))).astype(BF16)
                return carry

            lax.fori_loop(0, n_chunks, chunk, 0)

        for gi, win in enumerate(POOL_WINDOWS):
            @pl.when(j == ng + gi)
            def _(win=win):
                upad[0:HALO, :] = jnp.zeros((HALO, cw), F32)

                def fill(ci, carry):
                    r0 = pl.multiple_of(ci * rc, rc)
                    upad[pl.ds(pl.multiple_of(r0 + HALO, HALO), rc), :] = u_ref[pl.ds(r0, rc), :].astype(F32)
                    return carry

                lax.fori_loop(0, n_chunks, fill, 0)

                def chunk(ci, carry):
                    r0 = pl.multiple_of(ci * rc, rc)
                    rows = pl.ds(r0, rc)
                    pooled, _ = _pool_window(upad[pl.ds(r0, HALO + rc), :], win, r0, rc)
                    t = jnp.dot(pooled.astype(BF16), w_ref[0], preferred_element_type=F32)
                    y_ref[rows, :] = (t * sc_ref[...] * _silu(g_ref[rows, :].astype(F32))).astype(BF16)
                    return carry

                lax.fori_loop(0, n_chunks, chunk, 0)

    grp = lambda j: jnp.maximum(j - ng, 0)
    return pl.pallas_call(
        body, name=name, grid=(2 * ng,),
        in_specs=[pl.BlockSpec((s, cw), lambda j: (0, jnp.minimum(j, ng - 1))),
                  pl.BlockSpec((s, cw), lambda j: (0, 3 * ng + grp(j))),
                  pl.BlockSpec((s, cw), lambda j: (0, 4 * ng + j)),
                  pl.BlockSpec((1, cw, cw), lambda j: (grp(j), 0, 0)),
                  pl.BlockSpec((1, cw), lambda j: (0, grp(j)))] + dep_specs,
        out_specs=pl.BlockSpec((s, cw), lambda j: (0, j)),
        out_shape=jax.ShapeDtypeStruct((s, 2 * ng * cw), BF16),
        scratch_shapes=[pltpu.VMEM((HALO + s, cw), F32)],
        compiler_params=_params("arbitrary"),
    )(a, p, p, pool_w, pool_scale, *dep_args)


def even_mix_bwd(dy, a, p, pool_w, pool_scale, name, rc=64):
    s = p.shape[0]
    ng = len(POOL_WINDOWS)
    cw = pool_w.shape[1]
    n_chunks = s // rc

    def body(dy_ref, a_ref, u_ref, g_ref, w_ref, sc_ref, da_ref, du_ref, dg_ref, dw_ref, dsc_ref,
             upad, rpad, dpl, dw_acc, dsc_acc):
        j = pl.program_id(0)

        @pl.when(j < ng)
        def _():
            def chunk(ci, carry):
                rows = pl.ds(pl.multiple_of(ci * rc, rc), rc)
                dyv = dy_ref[rows, :].astype(F32)
                sg, dsg = _silu_and_grad(g_ref[rows, :].astype(F32))
                da_ref[rows, :] = (dyv * sg).astype(BF16)
                dg_ref[rows, :] = (dyv * a_ref[rows, :] * dsg).astype(BF16)
                return carry

            lax.fori_loop(0, n_chunks, chunk, 0)

        for gi, win in enumerate(POOL_WINDOWS):
            @pl.when(j == ng + gi)
            def _(win=win):
                upad[0:HALO, :] = jnp.zeros((HALO, cw), F32)
                rpad[s:s + HALO, :] = jnp.zeros((HALO, cw), F32)
                dw_acc[...] = jnp.zeros_like(dw_acc)
                dsc_acc[...] = jnp.zeros_like(dsc_acc)

                def fill(ci, carry):
                    r0 = pl.multiple_of(ci * rc, rc)
                    upad[pl.ds(pl.multiple_of(r0 + HALO, HALO), rc), :] = u_ref[pl.ds(r0, rc), :].astype(F32)
                    return carry

                lax.fori_loop(0, n_chunks, fill, 0)

                def chunk(ci, carry):
                    r0 = pl.multiple_of(ci * rc, rc)
                    rows = pl.ds(r0, rc)
                    pooled, inv = _pool_window(upad[pl.ds(r0, HALO + rc), :], win, r0, rc)
                    pb = pooled.astype(BF16)
                    wv = w_ref[0]
                    t = jnp.dot(pb, wv, preferred_element_type=F32)
                    scv = sc_ref[...]
                    dyv = dy_ref[rows, :].astype(F32)
                    sg, dsg = _silu_and_grad(g_ref[rows, :].astype(F32))
                    dpo = dyv * sg
                    dg_ref[rows, :] = (dyv * t * scv * dsg).astype(BF16)
                    dsc_acc[...] += _rowsum8(dpo * t)
                    dtb = (dpo * scv).astype(BF16)
                    dw_acc[...] += lax.dot_general(pb, dtb, (((0,), (0,)), ((), ())),
                                                   preferred_element_type=F32)
                    dpooled = lax.dot_general(dtb, wv, (((1,), (1,)), ((), ())),
                                              preferred_element_type=F32)
                    dpl[rows, :] = dpooled
                    rpad[rows, :] = dpooled * inv
                    return carry

                lax.fori_loop(0, n_chunks, chunk, 0)

                def chunk2(ci, carry):
                    r0 = pl.multiple_of(ci * rc, rc)
                    rows = pl.ds(r0, rc)
                    xx = rpad[pl.ds(r0, rc + HALO), :]
                    fs = _window_sum(xx, win, False)[0:rc]
                    du_ref[rows, :] = (fs - dpl[rows, :]).astype(BF16)
                    return carry

                lax.fori_loop(0, n_chunks, chunk2, 0)
                dw_ref[0] = dw_acc[...]
                dsc_ref[...] = jnp.sum(dsc_acc[...], axis=0, keepdims=True)

    grp = lambda j: jnp.maximum(j - ng, 0)
    att = lambda j: jnp.minimum(j, ng - 1)
    return pl.pallas_call(
        body, name=name, grid=(2 * ng,),
        in_specs=[pl.BlockSpec((s, cw), lambda j: (0, j)),
                  pl.BlockSpec((s, cw), lambda j: (0, att(j))),
                  pl.BlockSpec((s, cw), lambda j: (0, 3 * ng + grp(j))),
                  pl.BlockSpec((s, cw), lambda j: (0, 4 * ng + j)),
                  pl.BlockSpec((1, cw, cw), lambda j: (grp(j), 0, 0)),
                  pl.BlockSpec((1, cw), lambda j: (0, grp(j)))],
        out_specs=[pl.BlockSpec((s, cw), lambda j: (0, att(j))),
                   pl.BlockSpec((s, cw), lambda j: (0, grp(j))),
                   pl.BlockSpec((s, cw), lambda j: (0, j)),
                   pl.BlockSpec((1, cw, cw), lambda j: (grp(j), 0, 0)),
                   pl.BlockSpec((1, cw), lambda j: (0, grp(j)))],
        out_shape=[jax.ShapeDtypeStruct((s, ng * cw), BF16), jax.ShapeDtypeStruct((s, ng * cw), BF16),
                   jax.ShapeDtypeStruct((s, 2 * ng * cw), BF16),
                   jax.ShapeDtypeStruct((ng, cw, cw), F32), jax.ShapeDtypeStruct((1, ng * cw), F32)],
        scratch_shapes=[pltpu.VMEM((HALO + s, cw), F32), pltpu.VMEM((s + HALO, cw), F32),
                        pltpu.VMEM((s, cw), F32), pltpu.VMEM((cw, cw), F32), pltpu.VMEM((8, cw), F32)],
        compiler_params=_params("arbitrary"),
    )(dy, a, p, p, pool_w, pool_scale)


def _halo_before(tm):
    return lambda i: jnp.maximum(i * (tm // HALO) - 1, 0)


def _halo_after(tm, s):
    return lambda i: jnp.minimum((i + 1) * (tm // HALO), s // HALO - 1)


def odd_mix_fwd(p, sconv_w, dconv_w, dconv_b, cnorm_g, cnorm_b, name, tm=128):
    s = p.shape[0]
    cw = sconv_w.shape[1]
    n = s // tm
    lanes = 128
    hb = _halo_before(tm)

    def body(hc_ref, hch_ref, bc_ref, cc_ref, cch_ref, ga_ref, gah_ref, gb_ref, gbh_ref, g1_ref, g2_ref,
             sw_ref, dw_ref, db_ref, gam_ref, bet_ref, y_ref, dc_ref):
        first = pl.program_id(0) == 0
        for l in range(cw // lanes):
            cols = slice(l * lanes, (l + 1) * lanes)
            mh = jnp.where(first, 0.0, cch_ref[:, cols].astype(F32) * hch_ref[:, cols].astype(F32))
            mm = cc_ref[:, cols].astype(F32) * hc_ref[:, cols].astype(F32)
            xx = jnp.concatenate([mh, mm], axis=0)
            tap = _Taps(xx, tm, True)
            cv = jnp.zeros((tm, lanes), F32)
            for k in range(SCONV_K):
                cv = cv + sw_ref[k:k + 1, cols] * tap(SCONV_K - 1 - k)
            c_out = bc_ref[:, cols].astype(F32) * cv
            y_ref[:, cols] = (c_out * _silu(g1_ref[:, cols].astype(F32))).astype(BF16)
            dh = jnp.where(first, 0.0, gah_ref[:, cols].astype(F32) * _sigmoid(gbh_ref[:, cols].astype(F32)))
            dm = ga_ref[:, cols].astype(F32) * _sigmoid(gb_ref[:, cols].astype(F32))
            xx = jnp.concatenate([dh, dm], axis=0)
            tap = _Taps(xx, tm, True)
            acc = jnp.zeros((tm, lanes), F32) + db_ref[:, cols]
            for k in range(CONF_K):
                acc = acc + dw_ref[k:k + 1, cols] * tap(CONF_K - 1 - k)
            dc_ref[:, cols] = acc
        rs = 32
        for r in range(tm // rs):
            rows = slice(r * rs, (r + 1) * rs)
            xv = dc_ref[rows, :]
            mu = jnp.mean(xv, axis=-1, keepdims=True)
            xc = xv - mu
            rstd = lax.rsqrt(jnp.mean(xc * xc, axis=-1, keepdims=True) + EPS)
            ln = xc * rstd * gam_ref[...] + bet_ref[...]
            y_ref[rows, cw:2 * cw] = (_silu(ln) * _silu(g2_ref[rows, :].astype(F32))).astype(BF16)

    main = lambda c: pl.BlockSpec((tm, cw), lambda i: (i, c))
    halo = lambda c: pl.BlockSpec((HALO, cw), lambda i: (hb(i), c))
    vec = lambda r: pl.BlockSpec((r, cw), lambda i: (0, 0))
    return pl.pallas_call(
        body, name=name, grid=(n,),
        in_specs=[main(0), halo(0), main(1), main(2), halo(2), main(3), halo(3), main(4), halo(4),
                  main(5), main(6), vec(SCONV_K), vec(CONF_K), vec(1), vec(1), vec(1)],
        out_specs=[pl.BlockSpec((tm, 2 * cw), lambda i: (i, 0)), pl.BlockSpec((tm, cw), lambda i: (i, 0))],
        out_shape=[jax.ShapeDtypeStruct((s, 2 * cw), BF16), jax.ShapeDtypeStruct((s, cw), F32)],
        compiler_params=_params("parallel"),
    )(p, p, p, p, p, p, p, p, p, p, p, sconv_w, dconv_w, dconv_b, cnorm_g, cnorm_b)


def odd_bwd_ln(dy, p, dc, cnorm_g, cnorm_b, name, tm=256):
    s = p.shape[0]
    cw = dc.shape[1]
    n = s // tm
    rs = 32

    def body(dy_ref, g2_ref, dc_ref, gam_ref, bet_ref, ddc_ref, dg_ref, dgam_ref, dbet_ref, gacc, bacc):
        i = pl.program_id(0)

        @pl.when(i == 0)
        def _():
            gacc[...] = jnp.zeros_like(gacc)
            bacc[...] = jnp.zeros_like(bacc)

        def chunk(ci, carry):
            rows = pl.ds(pl.multiple_of(ci * rs, rs), rs)
            xv = dc_ref[rows, :]
            mu = jnp.mean(xv, axis=-1, keepdims=True)
            xc = xv - mu
            rstd = lax.rsqrt(jnp.mean(xc * xc, axis=-1, keepdims=True) + EPS)
            xh = xc * rstd
            gam = gam_ref[...]
            sl, dsl = _silu_and_grad(xh * gam + bet_ref[...])
            sg, dsg = _silu_and_grad(g2_ref[rows, :].astype(F32))
            dyv = dy_ref[rows, :].astype(F32)
            dg_ref[rows, :] = (dyv * sl * dsg).astype(BF16)
            dln = dyv * sg * dsl
            gacc[...] += _rowsum8(dln * xh)
            bacc[...] += _rowsum8(dln)
            dxh = dln * gam
            ddc_ref[rows, :] = rstd * (dxh - jnp.mean(dxh, axis=-1, keepdims=True)
                                       - xh * jnp.mean(dxh * xh, axis=-1, keepdims=True))
            return carry

        lax.fori_loop(0, tm // rs, chunk, 0)

        @pl.when(i == n - 1)
        def _():
            dgam_ref[...] = jnp.sum(gacc[...], axis=0, keepdims=True)
            dbet_ref[...] = jnp.sum(bacc[...], axis=0, keepdims=True)

    vec = pl.BlockSpec((1, cw), lambda i: (0, 0))
    return pl.pallas_call(
        body, name=name, grid=(n,),
        in_specs=[pl.BlockSpec((tm, cw), lambda i: (i, 1)), pl.BlockSpec((tm, cw), lambda i: (i, 6)),
                  pl.BlockSpec((tm, cw), lambda i: (i, 0)), vec, vec],
        out_specs=[pl.BlockSpec((tm, cw), lambda i: (i, 0)), pl.BlockSpec((tm, cw), lambda i: (i, 0)), vec, vec],
        out_shape=[jax.ShapeDtypeStruct((s, cw), F32), jax.ShapeDtypeStruct((s, cw), BF16),
                   jax.ShapeDtypeStruct((1, cw), F32), jax.ShapeDtypeStruct((1, cw), F32)],
        scratch_shapes=[pltpu.VMEM((8, cw), F32), pltpu.VMEM((8, cw), F32)],
        compiler_params=_params("arbitrary"),
    )(dy, p, dc, cnorm_g, cnorm_b)


def odd_bwd_conv(dy, p, ddc, dg2, sconv_w, dconv_w, name, tm=128):
    s = p.shape[0]
    cw = ddc.shape[1]
    n = s // tm
    lanes = 128
    hb = _halo_before(tm)
    ha = _halo_after(tm, s)

    def body(dy_ref, dya_ref, g1_ref, g1a_ref, bc_ref, bca_ref, hc_ref, hch_ref, cc_ref, cch_ref,
             ddc_ref, ddca_ref, ga_ref, gah_ref, gb_ref, gbh_ref, dg2_ref, sw_ref, dw_ref,
             dp_ref, dsw_ref, ddw_ref, ddb_ref, sw_acc, dw_acc, db_acc):
        i = pl.program_id(0)
        first = i == 0
        last = i == n - 1

        @pl.when(first)
        def _():
            sw_acc[...] = jnp.zeros_like(sw_acc)
            dw_acc[...] = jnp.zeros_like(dw_acc)
            db_acc[...] = jnp.zeros_like(db_acc)

        for l in range(cw // lanes):
            cols = slice(l * lanes, (l + 1) * lanes)
            mh = jnp.where(first, 0.0, cch_ref[:, cols].astype(F32) * hch_ref[:, cols].astype(F32))
            hcv = hc_ref[:, cols].astype(F32)
            ccv = cc_ref[:, cols].astype(F32)
            xx = jnp.concatenate([mh, ccv * hcv], axis=0)
            tap = _Taps(xx, tm, True)
            taps = [tap(SCONV_K - 1 - k) for k in range(SCONV_K)]
            cv = jnp.zeros((tm, lanes), F32)
            for k in range(SCONV_K):
                cv = cv + sw_ref[k:k + 1, cols] * taps[k]
            bcv = bc_ref[:, cols].astype(F32)
            dyv = dy_ref[:, cols].astype(F32)
            sg, dsg = _silu_and_grad(g1_ref[:, cols].astype(F32))
            dco = dyv * sg
            dp_ref[:, 5 * cw + l * lanes:5 * cw + (l + 1) * lanes] = (dyv * bcv * cv * dsg).astype(BF16)
            dp_ref[:, cw + l * lanes:cw + (l + 1) * lanes] = (dco * cv).astype(BF16)
            dcv = dco * bcv
            for k in range(SCONV_K):
                sw_acc[k * 8:(k + 1) * 8, cols] += _rowsum8(dcv * taps[k])
            dcv_a = jnp.where(last, 0.0, dya_ref[:, cols].astype(F32) * _silu(g1a_ref[:, cols].astype(F32))
                              * bca_ref[:, cols].astype(F32))
            xx = jnp.concatenate([dcv, dcv_a], axis=0)
            tap = _Taps(xx, tm, False)
            dm = jnp.zeros((tm, lanes), F32)
            for k in range(SCONV_K):
                dm = dm + sw_ref[k:k + 1, cols] * tap(SCONV_K - 1 - k)
            dp_ref[:, l * lanes:(l + 1) * lanes] = (dm * ccv).astype(BF16)
            dp_ref[:, 2 * cw + l * lanes:2 * cw + (l + 1) * lanes] = (dm * hcv).astype(BF16)
            gav = ga_ref[:, cols].astype(F32)
            sb = _sigmoid(gb_ref[:, cols].astype(F32))
            dh = jnp.where(first, 0.0, gah_ref[:, cols].astype(F32) * _sigmoid(gbh_ref[:, cols].astype(F32)))
            xx = jnp.concatenate([dh, gav * sb], axis=0)
            ddcv = ddc_ref[:, cols]
            db_acc[:, cols] += _rowsum8(ddcv)
            tap = _Taps(xx, tm, True)
            for k in range(CONF_K):
                dw_acc[k * 8:(k + 1) * 8, cols] += _rowsum8(ddcv * tap(CONF_K - 1 - k))
            ddc_a = jnp.where(last, 0.0, ddca_ref[:, cols])
            xx = jnp.concatenate([ddcv, ddc_a], axis=0)
            tap = _Taps(xx, tm, False)
            dgl = jnp.zeros((tm, lanes), F32)
            for k in range(CONF_K):
                dgl = dgl + dw_ref[k:k + 1, cols] * tap(CONF_K - 1 - k)
            dp_ref[:, 3 * cw + l * lanes:3 * cw + (l + 1) * lanes] = (dgl * sb).astype(BF16)
            dp_ref[:, 4 * cw + l * lanes:4 * cw + (l + 1) * lanes] = (dgl * gav * sb * (1.0 - sb)).astype(BF16)
        dp_ref[:, 6 * cw:7 * cw] = dg2_ref[...]

        @pl.when(last)
        def _():
            for k in range(SCONV_K):
                dsw_ref[k:k + 1, :] = jnp.sum(sw_acc[k * 8:(k + 1) * 8, :], axis=0, keepdims=True)
            for k in range(CONF_K):
                ddw_ref[k:k + 1, :] = jnp.sum(dw_acc[k * 8:(k + 1) * 8, :], axis=0, keepdims=True)
            ddb_ref[...] = jnp.sum(db_acc[...], axis=0, keepdims=True)

    def main(c):
        return pl.BlockSpec((tm, cw), lambda i: (i, c))

    def before(c):
        return pl.BlockSpec((HALO, cw), lambda i: (hb(i), c))

    def after(c):
        return pl.BlockSpec((HALO, cw), lambda i: (ha(i), c))

    def vec(r):
        return pl.BlockSpec((r, cw), lambda i: (0, 0))

    return pl.pallas_call(
        body, name=name, grid=(n,),
        in_specs=[main(0), after(0), main(5), after(5), main(1), after(1), main(0), before(0), main(2), before(2),
                  main(0), after(0), main(3), before(3), main(4), before(4), main(0), vec(SCONV_K), vec(CONF_K)],
        out_specs=[pl.BlockSpec((tm, 7 * cw), lambda i: (i, 0)), vec(SCONV_K), vec(CONF_K), vec(1)],
        out_shape=[jax.ShapeDtypeStruct((s, 7 * cw), BF16), jax.ShapeDtypeStruct((SCONV_K, cw), F32),
                   jax.ShapeDtypeStruct((CONF_K, cw), F32), jax.ShapeDtypeStruct((1, cw), F32)],
        scratch_shapes=[pltpu.VMEM((8 * SCONV_K, cw), F32), pltpu.VMEM((8 * CONF_K, cw), F32),
                        pltpu.VMEM((8, cw), F32)],
        compiler_params=_params("arbitrary"),
    )(dy, dy, p, p, p, p, p, p, p, p, ddc, ddc, p, p, p, p, dg2, sconv_w, dconv_w)


_ANY = pl.BlockSpec(memory_space=pl.ANY)


def _place():
    return lax.axis_index("x"), lax.axis_index("y"), lax.axis_index("c")


def all_gather(arrs, name, deps=()):
    n = len(arrs)

    def body(*refs):
        ins, outs = refs[:n], refs[n + len(deps):2 * n + len(deps)]
        send_sems, recv_sems, local_sems = refs[-3:]
        x, y, c = _place()
        me, sibling = (x, y, c), (x, y, 1 - c)
        chips = [(1 - x, y), (x, 1 - y), (1 - x, 1 - y)]

        def copy(a, k, block, to, src=None):
            px, py, pc = block
            dst = outs[a].at[4 * px + 2 * py + pc]
            return pltpu.make_async_remote_copy(
                src_ref=dst if src is None else src, dst_ref=dst,
                send_sem=send_sems.at[7 * a + k], recv_sem=recv_sems.at[7 * a + k],
                device_id=to, device_id_type=MESH)

        mine = [pltpu.make_async_copy(ins[a], outs[a].at[4 * x + 2 * y + c], local_sems.at[a]) for a in range(n)]
        first = []
        for a in range(n):
            first.append(copy(a, 0, me, sibling, src=ins[a]))
            first += [copy(a, 1 + j, me, (*chip, c), src=ins[a]) for j, chip in enumerate(chips)]
        for cp in first + mine:
            cp.start()
        passed = []
        for a in range(n):
            for j, chip in enumerate(chips):
                copy(a, 1 + j, (*chip, c), me).wait_recv()
                cp = copy(a, 4 + j, (*chip, c), sibling)
                cp.start()
                passed.append(cp)
        for a in range(n):
            copy(a, 0, sibling, me).wait_recv()
            for j, chip in enumerate(chips):
                copy(a, 4 + j, (*chip, 1 - c), me).wait_recv()
        for cp in first + passed:
            cp.wait_send()
        for cp in mine:
            cp.wait()

    return pl.pallas_call(
        body, name=name,
        out_shape=[jax.ShapeDtypeStruct((N_DEV,) + a.shape, a.dtype) for a in arrs],
        in_specs=[_ANY] * (n + len(deps)), out_specs=[_ANY] * n,
        scratch_shapes=[pltpu.SemaphoreType.DMA((7 * n,)), pltpu.SemaphoreType.DMA((7 * n,)),
                        pltpu.SemaphoreType.DMA((n,))],
    )(*arrs, *deps)


def in_proj_gathered(h, w_own, extras, name, tm=512):
    s, d = h.shape
    n = w_own.shape[1]
    arrs = [w_own] + list(extras)
    na = len(arrs)

    def body(*refs):
        h_ref, ins = refs[0], refs[1:1 + na]
        p_ref, outs = refs[1 + na], refs[2 + na:2 + 2 * na]
        wbuf, obuf, send_sems, recv_sems, load_sem, store_sems, own_sems = refs[2 + 2 * na:]
        x, y, c = _place()
        me, sibling = (x, y, c), (x, y, 1 - c)
        chips = [(1 - x, y), (x, 1 - y), (1 - x, 1 - y)]

        def slot(block):
            return 4 * block[0] + 2 * block[1] + block[2]

        def copy(a, k, block, to, src=None):
            dst = outs[a].at[slot(block)]
            return pltpu.make_async_remote_copy(
                src_ref=dst if src is None else src, dst_ref=dst,
                send_sem=send_sems.at[7 * a + k], recv_sem=recv_sems.at[7 * a + k],
                device_id=to, device_id_type=MESH)

        first = []
        for a in range(na):
            first.append(copy(a, 0, me, sibling, src=ins[a]))
            first += [copy(a, 1 + j, me, (*chip, c), src=ins[a]) for j, chip in enumerate(chips)]
        for cp in first:
            cp.start()
        own = pltpu.make_async_copy(wbuf.at[0], outs[0].at[slot(me)], own_sems.at[0])
        mine = [pltpu.make_async_copy(ins[a], outs[a].at[slot(me)], own_sems.at[a]) for a in range(1, na)]
        stores = [None, None]

        def multiply(k, block, w_from):
            b = k % 2
            if k == 2:
                own.wait()
            load = pltpu.make_async_copy(w_from, wbuf.at[b], load_sem)
            load.start()
            if stores[b] is not None:
                stores[b].wait()
            load.wait()
            if k == 0:
                own.start()

            def chunk(i, carry):
                rows = pl.ds(pl.multiple_of(i * tm, tm), tm)
                obuf[b, rows, :] = jnp.dot(h_ref[rows, :], wbuf[b], preferred_element_type=F32).astype(BF16)
                return carry

            lax.fori_loop(0, s // tm, chunk, 0)
            stores[b] = pltpu.make_async_copy(
                obuf.at[b], p_ref.at[:, pl.ds(pl.multiple_of(slot(block) * n, 128), n)], store_sems.at[b])
            stores[b].start()

        def arrived(arrays, j, chip):
            for a in arrays:
                copy(a, 1 + j, (*chip, c), me).wait_recv()
                cp = copy(a, 4 + j, (*chip, c), sibling)
                cp.start()
                passed.append(cp)

        small = range(1, na)
        passed = []
        multiply(0, me, ins[0])
        copy(0, 0, sibling, me).wait_recv()
        multiply(1, sibling, outs[0].at[slot(sibling)])
        for j, chip in enumerate(chips):
            arrived([0], j, chip)
            multiply(2 + 2 * j, (*chip, c), outs[0].at[slot((*chip, c))])
            copy(0, 4 + j, (*chip, 1 - c), me).wait_recv()
            multiply(3 + 2 * j, (*chip, 1 - c), outs[0].at[slot((*chip, 1 - c))])
        for cp in mine:
            cp.start()
        for a in small:
            copy(a, 0, sibling, me).wait_recv()
        for j, chip in enumerate(chips):
            arrived(small, j, chip)
        for j, chip in enumerate(chips):
            for a in small:
                copy(a, 4 + j, (*chip, 1 - c), me).wait_recv()
        for cp in first + passed:
            cp.wait_send()
        for cp in mine + stores:
            cp.wait()

    vmem = pl.BlockSpec(memory_space=pltpu.VMEM)
    outs = pl.pallas_call(
        body, name=name,
        out_shape=[jax.ShapeDtypeStruct((s, N_DEV * n), BF16)]
        + [jax.ShapeDtypeStruct((N_DEV,) + a.shape, a.dtype) for a in arrs],
        in_specs=[vmem] + [_ANY] * na, out_specs=[_ANY] * (1 + na),
        scratch_shapes=[pltpu.VMEM((2, d, n), BF16), pltpu.VMEM((2, s, n), BF16),
                        pltpu.SemaphoreType.DMA((7 * na,)), pltpu.SemaphoreType.DMA((7 * na,)),
                        pltpu.SemaphoreType.DMA, pltpu.SemaphoreType.DMA((2,)), pltpu.SemaphoreType.DMA((na,))],
        compiler_params=pltpu.CompilerParams(vmem_limit_bytes=VMEM_LIMIT),
    )(h, *arrs)
    return outs[0], outs[1], outs[2:]


_HBM = pl.BlockSpec(memory_space=pltpu.HBM)
_SEM = pl.BlockSpec(memory_space=pltpu.SEMAPHORE)
_DATAFLOW = pltpu.SideEffectType.DATAFLOW_SIDE_EFFECTING


def _peers_per_array(kind):
    return 1 if kind in ("sibling", "halves") else 3


def _split_copies(kind, srcs, lands, send_sems, recv_sems):
    x, y, c = _place()
    per = _peers_per_array(kind)
    out = []
    for a in range(len(lands)):
        if kind == "sibling":
            peers = [((x, y, 1 - c), srcs[a].at[:, pl.ds(1 - c, 1)], lands[a], lands[a])]
        elif kind == "halves":
            mine, its = lands[a].at[:, pl.ds(c, 1)], lands[a].at[:, pl.ds(1 - c, 1)]
            peers = [((x, y, 1 - c), mine, mine, its)]
        else:
            peers = []
            for px, py in [(1 - x, y), (x, 1 - y), (1 - x, 1 - y)]:
                if kind == "gather":
                    views = (srcs[a], lands[a].at[4 * x + 2 * y + c], lands[a].at[4 * px + 2 * py + c])
                else:
                    views = (srcs[a].at[2 * px + py], lands[a].at[2 * x + y], lands[a].at[2 * px + py])
                peers.append(((px, py, c),) + views)
        for j, (peer, src, dst, arrives) in enumerate(peers):
            sems = dict(send_sem=send_sems.at[per * a + j], recv_sem=recv_sems.at[per * a + j],
                        device_id=peer, device_id_type=MESH)
            out.append((pltpu.make_async_remote_copy(src_ref=src, dst_ref=dst, **sems),
                        pltpu.make_async_remote_copy(src_ref=src, dst_ref=arrives, **sems)))
    return out


def split_start(kind, srcs, lands, deps, name):
    ns, nl = len(srcs), len(lands)
    n_sems = _peers_per_array(kind) * nl
    held = list(srcs) + list(lands)

    def body(*refs):
        send_sems, recv_sems = refs[len(held) + len(deps)], refs[len(held) + len(deps) + 1]
        for copy, _ in _split_copies(kind, refs[:ns], refs[ns:ns + nl], send_sems, recv_sems):
            copy.start()
        token = refs[-1]
        token[...] = jnp.zeros_like(token)

    outs = pl.pallas_call(
        body, name=name,
        out_shape=(pltpu.SemaphoreType.DMA((n_sems,)), pltpu.SemaphoreType.DMA((n_sems,)),
                   *[pltpu.HBM(a.shape, a.dtype) for a in held], jax.ShapeDtypeStruct((8, 128), F32)),
        in_specs=[_HBM] * len(held) + [_ANY] * len(deps),
        out_specs=(_SEM, _SEM, *([_HBM] * len(held)), pl.BlockSpec(memory_space=pltpu.VMEM)),
        input_output_aliases={i: 2 + i for i in range(len(held))},
        compiler_params=pltpu.CompilerParams(has_side_effects=_DATAFLOW),
    )(*[pltpu.with_memory_space_constraint(a, pltpu.HBM) for a in held], *deps)
    return outs[0], outs[1], list(outs[2:2 + ns]), list(outs[2 + ns:2 + ns + nl]), outs[-1]


def split_wait(kind, send_sems, recv_sems, srcs, lands, afters, name):
    ns, nl = len(srcs), len(lands)
    held = list(srcs) + list(lands)

    def body(*refs):
        for _, arrival in _split_copies(kind, refs[:ns], refs[ns:ns + nl], refs[ns + nl], refs[ns + nl + 1]):
            arrival.wait_send()
            arrival.wait_recv()

    outs = pl.pallas_call(
        body, name=name,
        out_shape=[pltpu.HBM(a.shape, a.dtype) for a in held],
        in_specs=[_HBM] * len(held) + [_SEM, _SEM] + [_ANY] * len(afters),
        out_specs=[_HBM] * len(held),
        input_output_aliases={i: i for i in range(len(held))},
        compiler_params=pltpu.CompilerParams(has_side_effects=_DATAFLOW),
    )(*held, send_sems, recv_sems, *afters)
    return list(outs[:ns]), list(outs[ns:])


def place_block(land, block, dev, name):
    r, c = block.shape
    tr = min(r, 512)

    def body(dev_ref, land_ref, b_ref, o_ref):
        del dev_ref, land_ref
        o_ref[...] = b_ref[...]

    return pl.pallas_call(
        body, name=name,
        grid_spec=pltpu.PrefetchScalarGridSpec(
            num_scalar_prefetch=1, grid=(r // tr,),
            in_specs=[_ANY, pl.BlockSpec((tr, c), lambda i, dev_ref: (i, 0))],
            out_specs=pl.BlockSpec((None, tr, c), lambda i, dev_ref: (dev_ref[0], i, 0))),
        out_shape=jax.ShapeDtypeStruct(land.shape, land.dtype),
        input_output_aliases={1: 0},
        compiler_params=_params("parallel"),
    )(dev, land, block)


def pair_add(own, recv, core, name):
    _, _, r, c = own.shape
    tr = min(r, 512)

    def body(core_ref, own_ref, recv_ref, o_ref):
        del core_ref
        o_ref[...] = (own_ref[...].astype(F32) + recv_ref[...].astype(F32)).astype(BF16)

    return pl.pallas_call(
        body, name=name,
        grid_spec=pltpu.PrefetchScalarGridSpec(
            num_scalar_prefetch=1, grid=(4, r // tr),
            in_specs=[pl.BlockSpec((None, None, tr, c), lambda k, i, core_ref: (k, core_ref[0], i, 0)),
                      pl.BlockSpec((None, None, tr, c), lambda k, i, core_ref: (k, 0, i, 0))],
            out_specs=pl.BlockSpec((None, tr, c), lambda k, i, core_ref: (k, i, 0))),
        out_shape=jax.ShapeDtypeStruct((4, r, c), BF16),
        compiler_params=_params("parallel", "parallel"),
    )(core, own, recv)


def _adamw_math(w, g, m, v):
    m2 = ADAM_B1 * m + (1.0 - ADAM_B1) * g
    v2 = ADAM_B2 * v + (1.0 - ADAM_B2) * (g * g)
    m_hat = m2 / (1.0 - ADAM_B1 ** ADAM_STEP)
    v_hat = v2 / (1.0 - ADAM_B2 ** ADAM_STEP)
    delta = -ADAM_LR * (m_hat / (jnp.sqrt(v_hat) + ADAM_EPS) + ADAM_WD * w)
    return delta, m2, v2


def adamw_big(w, m, v, own, got, chip, name):
    r, c = w.shape
    tr = min(r, 256)

    def body(chip_ref, w_ref, m_ref, v_ref, p0, p1, p2, p3, g_ref, d_ref, m2_ref, v2_ref):
        del chip_ref
        g = ((p0[...].astype(F32) + p1[...].astype(F32)) + p2[...].astype(F32)) + p3[...].astype(F32)
        delta, m2, v2 = _adamw_math(w_ref[...], g, m_ref[...], v_ref[...])
        g_ref[...] = g
        d_ref[...] = delta
        m2_ref[...] = m2
        v2_ref[...] = v2

    row = pl.BlockSpec((tr, c), lambda i, chip_ref: (i, 0))

    def slab(flip):
        return pl.BlockSpec((None, tr, c), lambda i, chip_ref: (chip_ref[0] ^ flip, i, 0))

    return pl.pallas_call(
        body, name=name,
        grid_spec=pltpu.PrefetchScalarGridSpec(
            num_scalar_prefetch=1, grid=(r // tr,),
            in_specs=[row, row, row, slab(0), slab(1), slab(2), slab(3)],
            out_specs=[row] * 4),
        out_shape=[jax.ShapeDtypeStruct((r, c), F32)] * 4,
        compiler_params=_params("parallel"),
    )(chip, w, m, v, own, got, got, got)


def sum_devices(g8, name):
    def body(g_ref, o_ref):
        tot = g_ref[0]
        for k in range(1, N_DEV):
            tot = tot + g_ref[k]
        o_ref[...] = tot

    return pl.pallas_call(body, name=name, out_shape=jax.ShapeDtypeStruct(g8.shape[1:], F32))(g8)


def adamw_small(ws, gs, ms, vs, name):
    n = len(ws)

    def body(*refs):
        w_r, g_r, m_r, v_r = refs[:n], refs[n:2 * n], refs[2 * n:3 * n], refs[3 * n:4 * n]
        d_o, m_o, v_o = refs[4 * n:5 * n], refs[5 * n:6 * n], refs[6 * n:7 * n]
        for k in range(n):
            delta, m2, v2 = _adamw_math(w_r[k][...], g_r[k][...], m_r[k][...], v_r[k][...])
            d_o[k][...] = delta
            m_o[k][...] = m2
            v_o[k][...] = v2

    shapes = [jax.ShapeDtypeStruct(w.shape, F32) for w in ws]
    outs = pl.pallas_call(body, name=name, out_shape=shapes * 3)(*ws, *gs, *ms, *vs)
    return outs[:n], outs[n:2 * n], outs[2 * n:]


def _rows128(a):
    return a.reshape(-1, 128)


def _pad_rows(a, rows):
    return jnp.pad(a, ((0, rows - a.shape[0]), (0, 0)))


def kernel(x, ln_pre_even, w_in_even, pool_w, pool_scale, w_out_even, ln_post_even, ln_pre_odd, w_in_odd, sconv_w, dconv_w, dconv_b, cnorm_g, cnorm_b, w_out_odd, ln_post_odd, loss_target, m_ln_pre_even, m_w_in_even, m_pool_w, m_pool_scale, m_w_out_even, m_ln_post_even, m_ln_pre_odd, m_w_in_odd, m_sconv_w, m_dconv_w, m_dconv_b, m_cnorm_g, m_cnorm_b, m_w_out_odd, m_ln_post_odd, v_ln_pre_even, v_w_in_even, v_pool_w, v_pool_scale, v_w_out_even, v_ln_post_even, v_ln_pre_odd, v_w_in_odd, v_sconv_w, v_dconv_w, v_dconv_b, v_cnorm_g, v_cnorm_b, v_w_out_odd, v_ln_post_odd):
    xs = x[0]
    tgt = loss_target[0]
    s, d = xs.shape
    half = d // 2
    n_heads = half // HEAD_DIM
    ng = len(POOL_WINDOWS)
    cwp = half // ng
    dev = 4 * lax.axis_index("x") + 2 * lax.axis_index("y") + lax.axis_index("c")
    core = lax.axis_index("c").astype(jnp.int32).reshape(1)

    pr = pool_w.shape[2]
    cl = sconv_w.shape[2]
    small_parts = [(_rows128(ln_pre_odd), 8), (sconv_w[0], 8), (dconv_w[0], 32), (dconv_b, 8),
                   (cnorm_g, 8), (cnorm_b, 8), (_rows128(ln_post_odd), 8)]
    small_local = jnp.concatenate([_pad_rows(a, r) for a, r in small_parts], axis=0)
    h0 = rms_fwd(xs, ln_pre_even, "rms_pre_even")
    p0, g_wie, (g_pw, g_small) = in_proj_gathered(
        h0, w_in_even[0].astype(BF16), [pool_w[0].reshape(ng * pr, cwp).astype(BF16), small_local],
        "ag_in_proj_even")
    comm = _Exchanges(dev, core, d)
    token = comm.start_weights("out_even", [w_out_even[0].astype(BF16)], [p0])
    sb_dep = comm.start_weights("odd", [w_in_odd[0].astype(BF16), w_out_odd[0].astype(BF16)], [token])
    pool_full = g_pw.reshape(N_DEV, ng, pr, cwp).transpose(1, 0, 2, 3).reshape(ng, cwp, cwp)
    nl = ln_pre_odd.shape[1] // 128

    def chan(lo, rows):
        return g_small[:, lo:lo + rows].transpose(1, 0, 2).reshape(rows, N_DEV * cl)

    ln_pre_odd_f = g_small[:, 0:nl].reshape(1, d)
    sconv_f = chan(8, SCONV_K)
    dconv_f = chan(16, CONF_K)
    dconv_b_f = chan(48, 1)
    cnorm_g_f = chan(56, 1)
    cnorm_b_f = chan(64, 1)
    ln_post_odd_f = g_small[:, 72:72 + nl].reshape(1, d)

    loss_blk, grad_x, small_g = _fwd_bwd(
        xs, tgt, ln_pre_even, h0, p0, g_wie, pool_full, pool_scale, ln_post_even, ln_pre_odd_f,
        sconv_f, dconv_f, dconv_b_f, cnorm_g_f, cnorm_b_f, ln_post_odd_f, comm, sb_dep)
    small_w = [ln_pre_even, pool_scale, ln_post_even, ln_pre_odd, sconv_w[0], dconv_w[0], dconv_b, cnorm_g, cnorm_b, ln_post_odd]
    small_m = [m_ln_pre_even, m_pool_scale, m_ln_post_even, m_ln_pre_odd, m_sconv_w[0], m_dconv_w[0], m_dconv_b, m_cnorm_g, m_cnorm_b, m_ln_post_odd]
    small_v = [v_ln_pre_even, v_pool_scale, v_ln_post_even, v_ln_pre_odd, v_sconv_w[0], v_dconv_w[0], v_dconv_b, v_cnorm_g, v_cnorm_b, v_ln_post_odd]
    big = {"w_in_even": (w_in_even, m_w_in_even, v_w_in_even), "pool_w": (pool_w, m_pool_w, v_pool_w),
           "w_out_even": (w_out_even, m_w_out_even, v_w_out_even), "w_in_odd": (w_in_odd, m_w_in_odd, v_w_in_odd),
           "w_out_odd": (w_out_odd, m_w_out_odd, v_w_out_odd)}
    upd = comm.finish_updates(big, [grad_x])
    upd.update(comm.finish_updates(big, [grad_x]))
    sg, sd, sm, sv, loss = _update_small(small_g, loss_blk, small_w, small_m, small_v, dev, d, cl,
                                         deps=[upd["w_in_odd"][1], upd["w_out_even"][1]])
    upd.update(comm.finish_updates(big, sd))
    (g_wie_o, d_wie, m_wie, v_wie), (g_pw_o, d_pw, m_pw, v_pw) = upd["w_in_even"], upd["pool_w"]
    (g_woe_o, d_woe, m_woe, v_woe), (g_wio_o, d_wio, m_wio, v_wio) = upd["w_out_even"], upd["w_in_odd"]
    g_woo_o, d_woo, m_woo, v_woo = upd["w_out_odd"]

    def order(small, wie, pw, woe, wio, woo):
        return [small[0], wie, pw, small[1], woe, small[2], small[3], wio, small[4], small[5], small[6],
                small[7], small[8], woo, small[9]]

    grads = order(sg, g_wie_o, g_pw_o, g_woe_o, g_wio_o, g_woo_o)
    deltas = order(sd, d_wie, d_pw, d_woe, d_wio, d_woo)
    new_m = order(sm, m_wie, m_pw, m_woe, m_wio, m_woo)
    new_v = order(sv, v_wie, v_pw, v_woe, v_wio, v_woo)
    return (loss, grad_x[None], *grads, *deltas, *new_m, *new_v)


def _fwd_bwd(xs, tgt, ln_pre_even, h0, p0, g_wie, pool_full, pool_scale, ln_post_even, ln_pre_odd_f,
             sconv_f, dconv_f, dconv_b_f, cnorm_g_f, cnorm_b_f, ln_post_odd_f, comm, sb_dep):
    d = xs.shape[1]
    n_heads = d // 2 // HEAD_DIM
    ng, cwp = pool_full.shape[0], pool_full.shape[1]
    a0, sb_wts = sb_fwd(p0, n_heads, "sb_fwd", dep=sb_dep)
    dep = comm.weights_arrived("out_even", after=a0)
    y0 = even_mix_fwd(a0, p0, pool_full, pool_scale, "even_mix_fwd", dep=dep)
    (w_out_e,) = comm.weights("out_even", after=y0)
    w_out_e = w_out_e.reshape(1, d, d)
    o0 = mm_nn(y0, w_out_e, F32, "out_proj_even", tm=1024, tn=1024)
    dep = comm.weights_arrived("odd", after=o0)
    x1, h1 = postnorm_fwd(xs, o0, ln_post_even, ln_pre_odd_f, "post_even", dep=dep)
    g_wio, w_out_o = comm.weights("odd", after=x1)
    w_out_o = w_out_o.reshape(1, d, d)
    p1 = mm_nn(h1, g_wio, BF16, "in_proj_odd")
    y1, dc = odd_mix_fwd(p1, sconv_f, dconv_f, dconv_b_f, cnorm_g_f, cnorm_b_f, "odd_mix_fwd")
    o1 = mm_nn(y1, w_out_o, F32, "out_proj_odd", tm=1024, tn=1024)
    loss_blk, gx2, do1, dg_post_odd = final_fwd_bwd(x1, o1, ln_post_odd_f, tgt, "post_odd_loss")

    dw_out_o = mm_tn(y1, do1, 1, BF16, "dw_out_odd", tm=1024)
    dy1 = mm_nt(do1, w_out_o, BF16, "dy_odd", tn=1024)
    ddc, dg2, dgam, dbet = odd_bwd_ln(dy1, p1, dc, cnorm_g_f, cnorm_b_f, "odd_bwd_ln")
    dp1, dsconv, ddconv, ddconv_b = odd_bwd_conv(dy1, p1, ddc, dg2, sconv_f, dconv_f, "odd_bwd_conv")
    dw_in_o = mm_tn(h1, dp1, N_DEV, BF16, "dw_in_odd")
    dep = comm.reduce_begin({"w_out_odd": dw_out_o.reshape(N_DEV, d // N_DEV, d), "w_in_odd": dw_in_o}, "odd")
    dh1 = mm_nt(dp1, g_wio, F32, "dh_odd", dep=dep)
    dep = comm.reduce_send(after=dh1)
    gx1, dg_pre_odd, do0, dg_post_even = norm_bwd(dh1, x1, ln_pre_odd_f, gx2, "pre_odd_post_even_bwd",
                                                  inp2=o0, g2=ln_post_even, dep=dep)

    dw_out_e = mm_tn(y0, do0, 1, BF16, "dw_out_even", tm=1024)
    dy0 = mm_nt(do0, w_out_e, BF16, "dy_even", tn=1024)
    da0, du0, dg0, dpool, dpool_scale = even_mix_bwd(dy0, a0, p0, pool_full, pool_scale, "even_mix_bwd")
    pr = cwp // N_DEV
    dpool_slabs = dpool.astype(BF16).reshape(ng, N_DEV, pr, cwp).transpose(1, 0, 2, 3).reshape(N_DEV, ng * pr, cwp)
    dep = comm.reduce_begin({"w_out_even": dw_out_e.reshape(N_DEV, d // N_DEV, d), "pool_w": dpool_slabs}, "even_out")
    dq0, dk0, dv0 = sb_bwd(p0, a0, sb_wts, da0, n_heads, "sb_bwd", dep=dep)
    dep = comm.reduce_send(after=dq0)
    dp0 = jnp.concatenate([dq0, dk0, dv0, du0, dg0], axis=1)
    dw_in_e = mm_tn(h0, dp0, N_DEV, BF16, "dw_in_even", dep=dep)
    dep = comm.reduce_begin({"w_in_even": dw_in_e}, "even_in")
    dep = comm.reduce_send(after=dep)
    dh0 = mm_nt(dp0, g_wie, F32, "dh_even", dep=dep)
    dep = None
    grad_x, dg_pre_even = norm_bwd(dh0, xs, ln_pre_even, gx1, "pre_even_bwd", dep=dep)
    small_g = [dg_pre_even, dpool_scale, dg_post_even, dg_pre_odd, dsconv, ddconv, ddconv_b, dgam, dbet, dg_post_odd]
    return loss_blk, grad_x, small_g


class _Exchanges:
    def __init__(self, dev, core, d):
        self.dev = dev.astype(jnp.int32).reshape(1)
        self.core = core
        self.chip = (dev // 2).astype(jnp.int32).reshape(1)
        self.d = d
        self.in_flight = {}
        self.to_sibling = None
        self.pending = []

    def start_weights(self, tag, blocks, afters):
        lands = [lax.empty((N_DEV,) + b.shape, b.dtype) for b in blocks]
        send, recv, srcs, lands, token = split_start("gather", blocks, lands, afters, "ag_start_" + tag)
        self.in_flight[tag] = (send, recv, srcs, lands)
        return token

    def weights_arrived(self, tag, after):
        send, recv, srcs, lands = self.in_flight.pop(tag)
        srcs, lands = split_wait("gather", send, recv, srcs, lands, [after], "ag_wait_" + tag)
        lands = [place_block(l, b, self.dev, "ag_own_%s_%d" % (tag, k)) for k, (l, b) in enumerate(zip(lands, srcs))]
        lands = [l.reshape((4, 2) + l.shape[1:]) for l in lands]
        send, recv, _, lands, token = split_start("halves", [], lands, [], "ag_sibling_start_" + tag)
        self.in_flight[tag] = (send, recv, lands)
        return token

    def weights(self, tag, after):
        send, recv, lands = self.in_flight.pop(tag)
        _, lands = split_wait("halves", send, recv, [], lands, [after], "ag_sibling_wait_" + tag)
        return [l.reshape((N_DEV,) + l.shape[2:]) for l in lands]

    def reduce_begin(self, partials, tag):
        names = list(partials)
        arrs = [partials[k].reshape((4, 2) + partials[k].shape[1:]) for k in names]
        lands = [lax.empty((4, 1) + a.shape[2:], a.dtype) for a in arrs]
        send, recv, srcs, lands, token = split_start("sibling", arrs, lands, [], "rs_sibling_start_" + tag)
        self.to_sibling = (tag, names, send, recv, srcs, lands)
        return token

    def reduce_send(self, after):
        tag, names, send, recv, srcs, lands = self.to_sibling
        srcs, lands = split_wait("sibling", send, recv, srcs, lands, [after], "rs_sibling_wait_" + tag)
        sums = [pair_add(o, r, self.core, "rs_pair_add_" + k) for k, o, r in zip(names, srcs, lands)]
        zones = [lax.empty(a.shape, a.dtype) for a in sums]
        send, recv, srcs, zones, token = split_start("scatter", sums, zones, [], "rs_start_" + tag)
        self.pending.append((tag, names, send, recv, srcs, zones))
        return token

    def finish_updates(self, big, afters):
        tag, names, send, recv, srcs, lands = self.pending.pop(0)
        srcs, lands = split_wait("scatter", send, recv, srcs, lands, afters, "rs_wait_" + tag)
        out = {}
        for name, own, got in zip(names, srcs, lands):
            w, m, v = big[name]
            shp = own.shape[1:]
            outs = adamw_big(w.reshape(shp), m.reshape(shp), v.reshape(shp), own, got, self.chip, "adamw_" + name)
            out[name] = [o.reshape(w.shape) for o in outs]
        return out


def _update_small(small_g, loss_blk, small_w, small_m, small_v, dev, d, cl, deps):
    packed = jnp.concatenate([_rows128(g) for g in small_g] + [loss_blk], axis=0)
    (g8,) = all_gather([packed], "ag_small_grads", deps)
    tot = sum_devices(g8, "sum_small_grads")
    loss = tot[packed.shape[0] - 8, 0]
    full_g = []
    lo = 0
    for g in small_g:
        rows = g.size // 128
        full_g.append(tot[lo:lo + rows].reshape(g.shape))
        lo += rows

    def mine(g, width):
        return lax.dynamic_slice_in_dim(g, dev * width, width, axis=g.ndim - 1)

    fg = full_g
    small_gl = [fg[0], fg[1], fg[2], mine(fg[3], d // N_DEV), mine(fg[4], cl), mine(fg[5], cl), mine(fg[6], cl),
                mine(fg[7], cl), mine(fg[8], cl), mine(fg[9], d // N_DEV)]
    sd, sm, sv = adamw_small(small_w, small_gl, small_m, small_v, "adamw_small")

    def like(k, a):
        return a[None] if k in (4, 5) else a

    sg = [like(k, a) for k, a in enumerate(small_gl)]
    sd = [like(k, a) for k, a in enumerate(sd)]
    sm = [like(k, a) for k, a in enumerate(sm)]
    sv = [like(k, a) for k, a in enumerate(sv)]
    return sg, sd, sm, sv, loss
```

```python
import functools
import math

import jax
import jax.numpy as jnp
from jax import lax
from jax.experimental import pallas as pl
from jax.experimental.pallas import tpu as pltpu

F32 = jnp.float32
BF16 = jnp.bfloat16
EPS = 1e-6
HEAD_DIM = 128
POOL_WINDOWS = (2, 4, 8, 16)
SCONV_K = 3
CONF_K = 31
HALO = 32
N_DEV = 8
VMEM_LIMIT = 56 * 1024 * 1024
MESH = pl.DeviceIdType.MESH

ADAM_LR = 0.001
ADAM_B1 = 0.9
ADAM_B2 = 0.999
ADAM_EPS = 1e-08
ADAM_WD = 0.01
ADAM_STEP = 10


def _params(*sem):
    return pltpu.CompilerParams(dimension_semantics=sem, vmem_limit_bytes=VMEM_LIMIT)


def _sigmoid(v):
    return 1.0 / (1.0 + jnp.exp(-v))


def _silu(v):
    return v * _sigmoid(v)


def _silu_and_grad(v):
    s = _sigmoid(v)
    return v * s, s * (1.0 + v * (1.0 - s))


def _rowsum8(v):
    r, c = v.shape
    return jnp.sum(v.reshape(r // 8, 8, c), axis=0)


SUBLANES = 8


class _Taps:
    def __init__(self, xx, rows, before):
        self.xx, self.rows, self.before, self.rotated = xx, rows, before, {}

    def __call__(self, i):
        r, q = i % SUBLANES, i // SUBLANES
        if r not in self.rotated:
            n = self.xx.shape[0]
            self.rotated[r] = self.xx if r == 0 else pltpu.roll(self.xx, r if self.before else n - r, 0)
        lo = HALO - SUBLANES * q if self.before else SUBLANES * q
        return self.rotated[r][lo:lo + self.rows]


def _window_sum(xx, win, before):
    n = xx.shape[0]
    acc = xx
    k = 1
    while k < win:
        acc = acc + pltpu.roll(acc, k if before else n - k, 0)
        k *= 2
    return acc


def postnorm_fwd(x, o, g, g_next, name, tm=256, dep=None):
    s, d = x.shape
    dep_args, dep_specs = _after(dep)

    def body(x_ref, o_ref, g_ref, gn_ref, *rest):
        y_ref, h_ref = rest[-2:]
        ov = o_ref[...]
        r = lax.rsqrt(jnp.mean(ov * ov, axis=-1, keepdims=True) + EPS)
        y = x_ref[...] + ov * r * g_ref[...]
        y_ref[...] = y
        r2 = lax.rsqrt(jnp.mean(y * y, axis=-1, keepdims=True) + EPS)
        h_ref[...] = (y * r2 * gn_ref[...]).astype(BF16)

    row = pl.BlockSpec((tm, d), lambda i: (i, 0))
    vec = pl.BlockSpec((1, d), lambda i: (0, 0))
    return pl.pallas_call(
        body, name=name, grid=(s // tm,),
        in_specs=[row, row, vec, vec] + dep_specs, out_specs=[row, row],
        out_shape=[jax.ShapeDtypeStruct((s, d), F32), jax.ShapeDtypeStruct((s, d), BF16)],
        compiler_params=_params("parallel"),
    )(x, o, g, g_next, *dep_args)


def final_fwd_bwd(x1, o, g, target, name, tm=256):
    s, d = x1.shape
    n = s // tm

    def body(x_ref, o_ref, g_ref, t_ref, loss_ref, gx_ref, do_ref, dg_ref, lacc, gacc):
        i = pl.program_id(0)

        @pl.when(i == 0)
        def _():
            lacc[...] = jnp.zeros_like(lacc)
            gacc[...] = jnp.zeros_like(gacc)

        ov = o_ref[...]
        gv = g_ref[...]
        r = lax.rsqrt(jnp.mean(ov * ov, axis=-1, keepdims=True) + EPS)
        oh = ov * r
        diff = x_ref[...] + oh * gv - t_ref[...]
        lacc[...] += _rowsum8(diff * diff)
        gx = diff * (1.0 / d)
        gx_ref[...] = gx
        gacc[...] += _rowsum8(gx * oh)
        dn = gx * gv
        do_ref[...] = (r * (dn - oh * jnp.mean(dn * oh, axis=-1, keepdims=True))).astype(BF16)

        @pl.when(i == n - 1)
        def _():
            tot = jnp.sum(jnp.sum(lacc[...], axis=0, keepdims=True), axis=1, keepdims=True)
            loss_ref[...] = jnp.broadcast_to(tot * (0.5 / d), loss_ref.shape)
            dg_ref[...] = jnp.sum(gacc[...], axis=0, keepdims=True)

    row = pl.BlockSpec((tm, d), lambda i: (i, 0))
    vec = pl.BlockSpec((1, d), lambda i: (0, 0))
    return pl.pallas_call(
        body, name=name, grid=(n,),
        in_specs=[row, row, vec, row],
        out_specs=[pl.BlockSpec((8, 128), lambda i: (0, 0)), row, row, vec],
        out_shape=[jax.ShapeDtypeStruct((8, 128), F32), jax.ShapeDtypeStruct((s, d), F32),
                   jax.ShapeDtypeStruct((s, d), BF16), jax.ShapeDtypeStruct((1, d), F32)],
        scratch_shapes=[pltpu.VMEM((8, d), F32), pltpu.VMEM((8, d), F32)],
        compiler_params=_params("arbitrary"),
    )(x1, o, g, target)


def _rms_bwd_rows(dyv, xv, gv):
    r = lax.rsqrt(jnp.mean(xv * xv, axis=-1, keepdims=True) + EPS)
    xh = xv * r
    dn = dyv * gv
    return r * (dn - xh * jnp.mean(dn * xh, axis=-1, keepdims=True)), _rowsum8(dyv * xh)


def norm_bwd(dy, inp, g, resid, name, inp2=None, g2=None, tm=256, dep=None):
    s, d = inp.shape
    n = s // tm
    chain = inp2 is not None

    def body(*refs):
        dy_ref, x_ref, g_ref, r_ref = refs[:4]
        outs = refs[-6:] if chain else refs[-3:]
        i = pl.program_id(0)

        @pl.when(i == 0)
        def _():
            for acc in outs[-2:] if chain else outs[-1:]:
                acc[...] = jnp.zeros_like(acc)

        if chain:
            x2_ref, g2_ref = refs[4:6]
            dx_ref, dg_ref, dx2_ref, dg2_ref, gacc, gacc2 = outs
        else:
            dx_ref, dg_ref, gacc = outs
        dx, dg_rows = _rms_bwd_rows(dy_ref[...].astype(F32), x_ref[...], g_ref[...])
        dx = dx + r_ref[...]
        dx_ref[...] = dx
        gacc[...] += dg_rows
        if chain:
            dx2, dg2_rows = _rms_bwd_rows(dx, x2_ref[...], g2_ref[...])
            dx2_ref[...] = dx2.astype(BF16)
            gacc2[...] += dg2_rows

        @pl.when(i == n - 1)
        def _():
            dg_ref[...] = jnp.sum(gacc[...], axis=0, keepdims=True)
            if chain:
                dg2_ref[...] = jnp.sum(gacc2[...], axis=0, keepdims=True)

    row = pl.BlockSpec((tm, d), lambda i: (i, 0))
    vec = pl.BlockSpec((1, d), lambda i: (0, 0))
    dep_args, dep_specs = _after(dep)
    extra = [inp2, g2] if chain else []
    return pl.pallas_call(
        body, name=name, grid=(n,),
        in_specs=[row, row, vec, row] + ([row, vec] if chain else []) + dep_specs,
        out_specs=[row, vec] * (2 if chain else 1),
        out_shape=[jax.ShapeDtypeStruct((s, d), F32), jax.ShapeDtypeStruct((1, d), F32)]
        + ([jax.ShapeDtypeStruct((s, d), BF16), jax.ShapeDtypeStruct((1, d), F32)] if chain else []),
        scratch_shapes=[pltpu.VMEM((8, d), F32)] * (2 if chain else 1),
        compiler_params=_params("arbitrary"),
    )(dy, inp, g, resid, *extra, *dep_args)


def _after(dep):
    if dep is None:
        return [], []
    return [dep], [pl.BlockSpec((8, 128), lambda *_: (0, 0))]


def mm_nn(a, w, out_dtype, name, tm=2048, tn=None, dep=None):
    m, k = a.shape
    tm = min(tm, m)
    ns, _, n = w.shape
    tn = n if tn is None else tn
    nj = n // tn
    dep_args, dep_specs = _after(dep)

    def body(a_ref, w_ref, *rest):
        o_ref = rest[-1]
        o_ref[...] = jnp.dot(a_ref[...], w_ref[0], preferred_element_type=F32).astype(out_dtype)

    return pl.pallas_call(
        body, name=name, grid=(ns, nj, m // tm),
        in_specs=[pl.BlockSpec((tm, k), lambda s, j, i: (i, 0)),
                  pl.BlockSpec((1, k, tn), lambda s, j, i: (s, 0, j))] + dep_specs,
        out_specs=pl.BlockSpec((tm, tn), lambda s, j, i: (i, s * nj + j)),
        out_shape=jax.ShapeDtypeStruct((m, ns * n), out_dtype),
        compiler_params=_params("parallel", "parallel", "parallel"),
    )(a, w, *dep_args)


def mm_nt(a, w, out_dtype, name, tm=1024, tn=None, dep=None):
    m = a.shape[0]
    tm = min(tm, m)
    ns, k, n = w.shape
    tn = n if tn is None else tn
    nj = n // tn
    steps = ns * nj
    dep_args, dep_specs = _after(dep)

    def body(a_ref, w_ref, *rest):
        o_ref, acc = rest[-2:]
        r = pl.program_id(1)

        @pl.when(r == 0)
        def _():
            acc[...] = jnp.zeros_like(acc)

        acc[...] += lax.dot_general(a_ref[...], w_ref[0], (((1,), (1,)), ((), ())),
                                    preferred_element_type=F32)

        @pl.when(r == steps - 1)
        def _():
            o_ref[...] = acc[...].astype(out_dtype)

    return pl.pallas_call(
        body, name=name, grid=(m // tm, steps),
        in_specs=[pl.BlockSpec((tm, tn), lambda i, r: (i, r)),
                  pl.BlockSpec((1, k, tn), lambda i, r: (r // nj, 0, r % nj))] + dep_specs,
        out_specs=pl.BlockSpec((tm, k), lambda i, r: (i, 0)),
        out_shape=jax.ShapeDtypeStruct((m, k), out_dtype),
        scratch_shapes=[pltpu.VMEM((tm, k), F32)],
        compiler_params=_params("parallel", "arbitrary"),
    )(a, w, *dep_args)


def mm_tn(a, b, ns, out_dtype, name, tk=1024, tm=2048, dep=None):
    m, k = a.shape
    tm = min(tm, m)
    n = b.shape[1] // ns
    steps = m // tm
    dep_args, dep_specs = _after(dep)

    def body(a_ref, b_ref, *rest):
        o_ref, acc = rest[-2:]
        r = pl.program_id(2)

        @pl.when(r == 0)
        def _():
            acc[...] = jnp.zeros_like(acc)

        acc[...] += lax.dot_general(a_ref[...], b_ref[...], (((0,), (0,)), ((), ())),
                                    preferred_element_type=F32)

        @pl.when(r == steps - 1)
        def _():
            o_ref[0] = acc[...].astype(out_dtype)

    return pl.pallas_call(
        body, name=name, grid=(ns, k // tk, steps),
        in_specs=[pl.BlockSpec((tm, tk), lambda s, j, r: (r, j)),
                  pl.BlockSpec((tm, n), lambda s, j, r: (r, s))] + dep_specs,
        out_specs=pl.BlockSpec((1, tk, n), lambda s, j, r: (s, j, 0)),
        out_shape=jax.ShapeDtypeStruct((ns, k, n), out_dtype),
        scratch_shapes=[pltpu.VMEM((tk, n), F32)],
        compiler_params=_params("parallel", "parallel", "arbitrary"),
    )(a, b, *dep_args)


SB_BLK = 128


LOG2E = 1.0 / math.log(2.0)


def _split_dot(v, tri2):
    hi = pltpu.bitcast(pltpu.bitcast(v, jnp.uint32) & jnp.uint32(0xFFFF0000), F32)
    lo = (v - hi).astype(BF16)
    return jnp.dot(jnp.concatenate([hi.astype(BF16), lo], axis=1), tri2, preferred_element_type=F32)


def _sb_scores(z2, lim, dcol, tri_ex, masked):
    sp = jnp.log2(1.0 + jnp.exp2(-jnp.abs(z2)))
    lb = jnp.minimum(z2, 0.0) - sp
    l1m = lb - z2
    mask = None
    if masked:
        mask = dcol < lim
        l1m = jnp.where(mask, l1m, 0.0)
    return mask, lb, l1m, _split_dot(l1m, tri_ex)


def _sb_consts():
    row = lax.broadcasted_iota(jnp.int32, (SB_BLK, SB_BLK), 0)
    col = lax.broadcasted_iota(jnp.int32, (SB_BLK, SB_BLK), 1)
    tri_ex = jnp.where(row > col, 1.0, 0.0).astype(BF16)
    tri_in = jnp.where(row >= col, 1.0, 0.0).astype(BF16)
    return col - row, jnp.concatenate([tri_ex, tri_ex], axis=0), jnp.concatenate([tri_in, tri_in], axis=0)


def sb_fwd(p, n_heads, name, tq=256, nsub=4, dep=None):
    s = p.shape[0]
    h_n = n_heads
    b = SB_BLK
    nqs = tq // b
    tk = nsub * b
    scale = 1.0 / math.sqrt(HEAD_DIM)

    dep_args, dep_specs = _after(dep)

    def body(q_ref, k_ref, v_ref, *rest):
        o_ref, w_ref = rest[-2:]
        qi = pl.program_id(1)
        dcol, tri_ex, _ = _sb_consts()
        qv = [q_ref[qs * b:(qs + 1) * b, :] for qs in range(nqs)]
        n_groups = ((qi + 1) * nqs - 1) // nsub + 1

        def step(it, carry, masked):
            c1s, accs = carry
            g = n_groups - 1 - it
            off = pl.multiple_of(g * tk, tk)
            kg = k_ref[pl.ds(off, tk), :]
            vg = v_ref[pl.ds(off, tk), :]
            new_c1, new_acc = [], []
            for qs in range(nqs):
                qb = qi * nqs + qs
                z2 = lax.dot_general(qv[qs], kg, (((1,), (1,)), ((), ())),
                                     preferred_element_type=F32) * (scale * LOG2E)
                blocks = [_sb_scores(z2[:, j * b:(j + 1) * b], (qb - (g * nsub + j)) * b, dcol, tri_ex, masked)
                          for j in range(nsub)]
                run = c1s[qs]
                ws = [None] * nsub
                for j in reversed(range(nsub)):
                    mask, lb, l1m, ls_loc = blocks[j]
                    wj = jnp.exp2(lb + ls_loc + run)
                    ws[j] = (jnp.where(mask, wj, 0.0) if masked else wj).astype(BF16)
                    run = run + jnp.sum(l1m, axis=1, keepdims=True)
                w = jnp.concatenate(ws, axis=1)
                w_ref[0, g, qs * b:(qs + 1) * b, :] = w
                new_acc.append(accs[qs] + jnp.dot(w, vg, preferred_element_type=F32))
                new_c1.append(run)
            return tuple(new_c1), tuple(new_acc)

        init = (tuple(jnp.zeros((b, 1), F32) for _ in range(nqs)),
                tuple(jnp.zeros((b, HEAD_DIM), F32) for _ in range(nqs)))
        assert nqs == 2 and nsub % 2 == 0
        first = step(0, init, True)
        _, accs = lax.fori_loop(1, n_groups, functools.partial(step, masked=False), first)
        for qs in range(nqs):
            o_ref[qs * b:(qs + 1) * b, :] = accs[qs]

    return pl.pallas_call(
        body, name=name, grid=(h_n, s // tq),
        in_specs=[pl.BlockSpec((tq, HEAD_DIM), lambda h, i: (i, h)),
                  pl.BlockSpec((s, HEAD_DIM), lambda h, i: (0, h_n + h)),
                  pl.BlockSpec((s, HEAD_DIM), lambda h, i: (0, 2 * h_n + h))] + dep_specs,
        out_specs=[pl.BlockSpec((tq, HEAD_DIM), lambda h, i: (i, h)),
                   pl.BlockSpec((1, s // tk, tq, tk), lambda h, i: (h, 0, i, 0))],
        out_shape=[jax.ShapeDtypeStruct((s, h_n * HEAD_DIM), F32),
                   jax.ShapeDtypeStruct((h_n, s // tk, s, tk), BF16)],
        compiler_params=_params("parallel", "arbitrary"),
    )(p, p, p, *dep_args)


def sb_bwd(p, a, wts, da, n_heads, name, tq=256, dep=None):
    s = p.shape[0]
    h_n = n_heads
    nq = s // tq
    b = SB_BLK
    nqs = tq // b
    tk = wts.shape[3]
    nsub = tk // b
    scale = 1.0 / math.sqrt(HEAD_DIM)
    dep_args, dep_specs = _after(dep)

    def body(q_ref, k_ref, v_ref, a_ref, da_ref, w_ref, *rest):
        dq_ref, dk_ref, dv_ref, dk_acc, dv_acc = rest[-5:]
        qi = pl.program_id(1)

        @pl.when(qi == 0)
        def _():
            dk_acc[...] = jnp.zeros_like(dk_acc)
            dv_acc[...] = jnp.zeros_like(dv_acc)

        dcol, _, tri_in = _sb_consts()
        q_all = q_ref[...]
        do_all = da_ref[...]
        qv = [q_ref[qs * b:(qs + 1) * b, :] for qs in range(nqs)]
        dov = [da_ref[qs * b:(qs + 1) * b, :] for qs in range(nqs)]
        tots = [jnp.sum(dov[qs].astype(F32) * a_ref[qs * b:(qs + 1) * b, :], axis=1, keepdims=True)
                for qs in range(nqs)]
        n_groups = ((qi + 1) * nqs - 1) // nsub + 1

        def step(it, carry, masked):
            c2s, dqs = carry
            g = n_groups - 1 - it
            off = pl.multiple_of(g * tk, tk)
            kg = k_ref[pl.ds(off, tk), :]
            vg = v_ref[pl.ds(off, tk), :]
            w_all = w_ref[0, g]
            new_c2, new_dq, dz_rows = [], [], []
            for qs in range(nqs):
                qb = qi * nqs + qs
                z2 = lax.dot_general(qv[qs], kg, (((1,), (1,)), ((), ())),
                                     preferred_element_type=F32) * (-scale * LOG2E)
                dw = lax.dot_general(dov[qs], vg, (((1,), (1,)), ((), ())), preferred_element_type=F32)
                beta = 1.0 / (1.0 + jnp.exp2(z2))
                e = dw * w_all[qs * b:(qs + 1) * b, :].astype(F32)
                run2 = c2s[qs]
                dzs = [None] * nsub
                for j in reversed(range(nsub)):
                    cols = slice(j * b, (j + 1) * b)
                    later = _split_dot(e[:, cols], tri_in) + run2
                    bj = beta[:, cols]
                    dz = (e[:, cols] * (1.0 - bj) - bj * (tots[qs] - later)) * scale
                    if masked:
                        dz = jnp.where(dcol < (qb - (g * nsub + j)) * b, dz, 0.0)
                    dzs[j] = dz.astype(BF16)
                    run2 = run2 + jnp.sum(e[:, cols], axis=1, keepdims=True)
                dzq = jnp.concatenate(dzs, axis=1)
                new_dq.append(dqs[qs] + jnp.dot(dzq, kg, preferred_element_type=F32))
                new_c2.append(run2)
                dz_rows.append(dzq)
            dz_all = jnp.concatenate(dz_rows, axis=0)
            dk_acc[pl.ds(off, tk), :] += lax.dot_general(dz_all, q_all, (((0,), (0,)), ((), ())),
                                                         preferred_element_type=F32)
            dv_acc[pl.ds(off, tk), :] += lax.dot_general(w_all, do_all, (((0,), (0,)), ((), ())),
                                                         preferred_element_type=F32)
            return tuple(new_c2), tuple(new_dq)

        zeros = tuple(jnp.zeros((b, 1), F32) for _ in range(nqs))
        assert nqs == 2 and nsub % 2 == 0
        first = step(0, (zeros, tuple(jnp.zeros((b, HEAD_DIM), F32) for _ in range(nqs))), True)
        _, dqs = lax.fori_loop(1, n_groups, functools.partial(step, masked=False), first)
        for qs in range(nqs):
            dq_ref[qs * b:(qs + 1) * b, :] = dqs[qs].astype(BF16)

        @pl.when(qi == nq - 1)
        def _():
            dk_ref[...] = dk_acc[...].astype(BF16)
            dv_ref[...] = dv_acc[...].astype(BF16)

    blk = pl.BlockSpec((tq, HEAD_DIM), lambda h, i: (i, h))
    full = pl.BlockSpec((s, HEAD_DIM), lambda h, i: (0, h))
    return pl.pallas_call(
        body, name=name, grid=(h_n, nq),
        in_specs=[blk, pl.BlockSpec((s, HEAD_DIM), lambda h, i: (0, h_n + h)),
                  pl.BlockSpec((s, HEAD_DIM), lambda h, i: (0, 2 * h_n + h)), blk, blk,
                  pl.BlockSpec((1, s // tk, tq, tk), lambda h, i: (h, 0, i, 0))] + dep_specs,
        out_specs=[blk, full, full],
        out_shape=[jax.ShapeDtypeStruct((s, h_n * HEAD_DIM), BF16)] * 3,
        scratch_shapes=[pltpu.VMEM((s, HEAD_DIM), F32), pltpu.VMEM((s, HEAD_DIM), F32)],
        compiler_params=_params("parallel", "arbitrary"),
    )(p, p, p, a, da, wts, *dep_args)


def _pool_window(xx, win, r0, rc):
    cur = xx[HALO:HALO + rc]
    ws = _window_sum(xx, win, True)[HALO:HALO + rc]
    t_idx = r0 + lax.broadcasted_iota(jnp.int32, (rc, 1), 0)
    inv = 1.0 / jnp.minimum(win, t_idx + 1).astype(F32)
    return ws * inv - cur, inv


def even_mix_fwd(a, p, pool_w, pool_scale, name, rc=64, dep=None):
    s = p.shape[0]
    ng = len(POOL_WINDOWS)
    cw = pool_w.shape[1]
    n_chunks = s // rc
    dep_args, dep_specs = _after(dep)

    def body(a_ref, u_ref, g_ref, w_ref, sc_ref, *rest):
        y_ref, upad = rest[-2:]
        j = pl.program_id(0)

        @pl.when(j < ng)
        def _():
            def chunk(ci, carry):
                rows = pl.ds(pl.multiple_of(ci * rc, rc), rc)
                y_ref[rows, :] = (a_ref[rows, :] * _silu(g_ref[rows, :].astype(F32))).astype(BF16)
                return carry

            lax.fori_loop(0, n_chunks, chunk, 0)

        for gi, win in enumerate(POOL_WINDOWS):
            @pl.when(j == ng + gi)
            def _(win=win):
                upad[0:HALO, :] = jnp.zeros((HALO, cw), F32)

                def fill(ci, carry):
                    r0 = pl.multiple_of(ci * rc, rc)
                    upad[pl.ds(pl.multiple_of(r0 + HALO, HALO), rc), :] = u_ref[pl.ds(r0, rc), :].astype(F32)
                    return carry

                lax.fori_loop(0, n_chunks, fill, 0)

                def chunk(ci, carry):
                    r0 = pl.multiple_of(ci * rc, rc)
                    rows = pl.ds(r0, rc)
                    pooled, _ = _pool_window(upad[pl.ds(r0, HALO + rc), :], win, r0, rc)
                    t = jnp.dot(pooled.astype(BF16), w_ref[0], preferred_element_type=F32)
                    y_ref[rows, :] = (t * sc_ref[...] * _silu(g_ref[rows, :].astype(F32))).astype(BF16)
                    return carry

                lax.fori_loop(0, n_chunks, chunk, 0)

    grp = lambda j: jnp.maximum(j - ng, 0)
    return pl.pallas_call(
        body, name=name, grid=(2 * ng,),
        in_specs=[pl.BlockSpec((s, cw), lambda j: (0, jnp.minimum(j, ng - 1))),
                  pl.BlockSpec((s, cw), lambda j: (0, 3 * ng + grp(j))),
                  pl.BlockSpec((s, cw), lambda j: (0, 4 * ng + j)),
                  pl.BlockSpec((1, cw, cw), lambda j: (grp(j), 0, 0)),
                  pl.BlockSpec((1, cw), lambda j: (0, grp(j)))] + dep_specs,
        out_specs=pl.BlockSpec((s, cw), lambda j: (0, j)),
        out_shape=jax.ShapeDtypeStruct((s, 2 * ng * cw), BF16),
        scratch_shapes=[pltpu.VMEM((HALO + s, cw), F32)],
        compiler_params=_params("arbitrary"),
    )(a, p, p, pool_w, pool_scale, *dep_args)


def even_mix_bwd(dy, a, p, pool_w, pool_scale, name, rc=64):
    s = p.shape[0]
    ng = len(POOL_WINDOWS)
    cw = pool_w.shape[1]
    n_chunks = s // rc

    def body(dy_ref, a_ref, u_ref, g_ref, w_ref, sc_ref, da_ref, du_ref, dg_ref, dw_ref, dsc_ref,
             upad, rpad, dpl, dw_acc, dsc_acc):
        j = pl.program_id(0)

        @pl.when(j < ng)
        def _():
            def chunk(ci, carry):
                rows = pl.ds(pl.multiple_of(ci * rc, rc), rc)
                dyv = dy_ref[rows, :].astype(F32)
                sg, dsg = _silu_and_grad(g_ref[rows, :].astype(F32))
                da_ref[rows, :] = (dyv * sg).astype(BF16)
                dg_ref[rows, :] = (dyv * a_ref[rows, :] * dsg).astype(BF16)
                return carry

            lax.fori_loop(0, n_chunks, chunk, 0)

        for gi, win in enumerate(POOL_WINDOWS):
            @pl.when(j == ng + gi)
            def _(win=win):
                upad[0:HALO, :] = jnp.zeros((HALO, cw), F32)
                rpad[s:s + HALO, :] = jnp.zeros((HALO, cw), F32)
                dw_acc[...] = jnp.zeros_like(dw_acc)
                dsc_acc[...] = jnp.zeros_like(dsc_acc)

                def fill(ci, carry):
                    r0 = pl.multiple_of(ci * rc, rc)
                    upad[pl.ds(pl.multiple_of(r0 + HALO, HALO), rc), :] = u_ref[pl.ds(r0, rc), :].astype(F32)
                    return carry

                lax.fori_loop(0, n_chunks, fill, 0)

                def chunk(ci, carry):
                    r0 = pl.multiple_of(ci * rc, rc)
                    rows = pl.ds(r0, rc)
                    pooled, inv = _pool_window(upad[pl.ds(r0, HALO + rc), :], win, r0, rc)
                    pb = pooled.astype(BF16)
                    wv = w_ref[0]
                    t = jnp.dot(pb, wv, preferred_element_type=F32)
                    scv = sc_ref[...]
                    dyv = dy_ref[rows, :].astype(F32)
                    sg, dsg = _silu_and_grad(g_ref[rows, :].astype(F32))
                    dpo = dyv * sg
                    dg_ref[rows, :] = (dyv * t * scv * dsg).astype(BF16)
                    dsc_acc[...] += _rowsum8(dpo * t)
                    dtb = (dpo * scv).astype(BF16)
                    dw_acc[...] += lax.dot_general(pb, dtb, (((0,), (0,)), ((), ())),
                                                   preferred_element_type=F32)
                    dpooled = lax.dot_general(dtb, wv, (((1,), (1,)), ((), ())),
                                              preferred_element_type=F32)
                    dpl[rows, :] = dpooled
                    rpad[rows, :] = dpooled * inv
                    return carry

                lax.fori_loop(0, n_chunks, chunk, 0)

                def chunk2(ci, carry):
                    r0 = pl.multiple_of(ci * rc, rc)
                    rows = pl.ds(r0, rc)
                    xx = rpad[pl.ds(r0, rc + HALO), :]
                    fs = _window_sum(xx, win, False)[0:rc]
                    du_ref[rows, :] = (fs - dpl[rows, :]).astype(BF16)
                    return carry

                lax.fori_loop(0, n_chunks, chunk2, 0)
                dw_ref[0] = dw_acc[...]
                dsc_ref[...] = jnp.sum(dsc_acc[...], axis=0, keepdims=True)

    grp = lambda j: jnp.maximum(j - ng, 0)
    att = lambda j: jnp.minimum(j, ng - 1)
    return pl.pallas_call(
        body, name=name, grid=(2 * ng,),
        in_specs=[pl.BlockSpec((s, cw), lambda j: (0, j)),
                  pl.BlockSpec((s, cw), lambda j: (0, att(j))),
                  pl.BlockSpec((s, cw), lambda j: (0, 3 * ng + grp(j))),
                  pl.BlockSpec((s, cw), lambda j: (0, 4 * ng + j)),
                  pl.BlockSpec((1, cw, cw), lambda j: (grp(j), 0, 0)),
                  pl.BlockSpec((1, cw), lambda j: (0, grp(j)))],
        out_specs=[pl.BlockSpec((s, cw), lambda j: (0, att(j))),
                   pl.BlockSpec((s, cw), lambda j: (0, grp(j))),
                   pl.BlockSpec((s, cw), lambda j: (0, j)),
                   pl.BlockSpec((1, cw, cw), lambda j: (grp(j), 0, 0)),
                   pl.BlockSpec((1, cw), lambda j: (0, grp(j)))],
        out_shape=[jax.ShapeDtypeStruct((s, ng * cw), BF16), jax.ShapeDtypeStruct((s, ng * cw), BF16),
                   jax.ShapeDtypeStruct((s, 2 * ng * cw), BF16),
                   jax.ShapeDtypeStruct((ng, cw, cw), F32), jax.ShapeDtypeStruct((1, ng * cw), F32)],
        scratch_shapes=[pltpu.VMEM((HALO + s, cw), F32), pltpu.VMEM((s + HALO, cw), F32),
                        pltpu.VMEM((s, cw), F32), pltpu.VMEM((cw, cw), F32), pltpu.VMEM((8, cw), F32)],
        compiler_params=_params("arbitrary"),
    )(dy, a, p, p, pool_w, pool_scale)


def _halo_before(tm):
    return lambda i: jnp.maximum(i * (tm // HALO) - 1, 0)


def _halo_after(tm, s):
    return lambda i: jnp.minimum((i + 1) * (tm // HALO), s // HALO - 1)


def odd_mix_fwd(p, sconv_w, dconv_w, dconv_b, cnorm_g, cnorm_b, name, tm=128):
    s = p.shape[0]
    cw = sconv_w.shape[1]
    n = s // tm
    lanes = 128
    hb = _halo_before(tm)

    def body(hc_ref, hch_ref, bc_ref, cc_ref, cch_ref, ga_ref, gah_ref, gb_ref, gbh_ref, g1_ref, g2_ref,
             sw_ref, dw_ref, db_ref, gam_ref, bet_ref, y_ref, dc_ref):
        first = pl.program_id(0) == 0
        for l in range(cw // lanes):
            cols = slice(l * lanes, (l + 1) * lanes)
            mh = jnp.where(first, 0.0, cch_ref[:, cols].astype(F32) * hch_ref[:, cols].astype(F32))
            mm = cc_ref[:, cols].astype(F32) * hc_ref[:, cols].astype(F32)
            xx = jnp.concatenate([mh, mm], axis=0)
            tap = _Taps(xx, tm, True)
            cv = jnp.zeros((tm, lanes), F32)
            for k in range(SCONV_K):
                cv = cv + sw_ref[k:k + 1, cols] * tap(SCONV_K - 1 - k)
            c_out = bc_ref[:, cols].astype(F32) * cv
            y_ref[:, cols] = (c_out * _silu(g1_ref[:, cols].astype(F32))).astype(BF16)
            dh = jnp.where(first, 0.0, gah_ref[:, cols].astype(F32) * _sigmoid(gbh_ref[:, cols].astype(F32)))
            dm = ga_ref[:, cols].astype(F32) * _sigmoid(gb_ref[:, cols].astype(F32))
            xx = jnp.concatenate([dh, dm], axis=0)
            tap = _Taps(xx, tm, True)
            acc = jnp.zeros((tm, lanes), F32) + db_ref[:, cols]
            for k in range(CONF_K):
                acc = acc + dw_ref[k:k + 1, cols] * tap(CONF_K - 1 - k)
            dc_ref[:, cols] = acc
        rs = 32
        for r in range(tm // rs):
            rows = slice(r * rs, (r + 1) * rs)
            xv = dc_ref[rows, :]
            mu = jnp.mean(xv, axis=-1, keepdims=True)
            xc = xv - mu
            rstd = lax.rsqrt(jnp.mean(xc * xc, axis=-1, keepdims=True) + EPS)
            ln = xc * rstd * gam_ref[...] + bet_ref[...]
            y_ref[rows, cw:2 * cw] = (_silu(ln) * _silu(g2_ref[rows, :].astype(F32))).astype(BF16)

    main = lambda c: pl.BlockSpec((tm, cw), lambda i: (i, c))
    halo = lambda c: pl.BlockSpec((HALO, cw), lambda i: (hb(i), c))
    vec = lambda r: pl.BlockSpec((r, cw), lambda i: (0, 0))
    return pl.pallas_call(
        body, name=name, grid=(n,),
        in_specs=[main(0), halo(0), main(1), main(2), halo(2), main(3), halo(3), main(4), halo(4),
                  main(5), main(6), vec(SCONV_K), vec(CONF_K), vec(1), vec(1), vec(1)],
        out_specs=[pl.BlockSpec((tm, 2 * cw), lambda i: (i, 0)), pl.BlockSpec((tm, cw), lambda i: (i, 0))],
        out_shape=[jax.ShapeDtypeStruct((s, 2 * cw), BF16), jax.ShapeDtypeStruct((s, cw), F32)],
        compiler_params=_params("parallel"),
    )(p, p, p, p, p, p, p, p, p, p, p, sconv_w, dconv_w, dconv_b, cnorm_g, cnorm_b)


def odd_bwd_ln(dy, p, dc, cnorm_g, cnorm_b, name, tm=256):
    s = p.shape[0]
    cw = dc.shape[1]
    n = s // tm
    rs = 32

    def body(dy_ref, g2_ref, dc_ref, gam_ref, bet_ref, ddc_ref, dg_ref, dgam_ref, dbet_ref, gacc, bacc):
        i = pl.program_id(0)

        @pl.when(i == 0)
        def _():
            gacc[...] = jnp.zeros_like(gacc)
            bacc[...] = jnp.zeros_like(bacc)

        def chunk(ci, carry):
            rows = pl.ds(pl.multiple_of(ci * rs, rs), rs)
            xv = dc_ref[rows, :]
            mu = jnp.mean(xv, axis=-1, keepdims=True)
            xc = xv - mu
            rstd = lax.rsqrt(jnp.mean(xc * xc, axis=-1, keepdims=True) + EPS)
            xh = xc * rstd
            gam = gam_ref[...]
            sl, dsl = _silu_and_grad(xh * gam + bet_ref[...])
            sg, dsg = _silu_and_grad(g2_ref[rows, :].astype(F32))
            dyv = dy_ref[rows, :].astype(F32)
            dg_ref[rows, :] = (dyv * sl * dsg).astype(BF16)
            dln = dyv * sg * dsl
            gacc[...] += _rowsum8(dln * xh)
            bacc[...] += _rowsum8(dln)
            dxh = dln * gam
            ddc_ref[rows, :] = rstd * (dxh - jnp.mean(dxh, axis=-1, keepdims=True)
                                       - xh * jnp.mean(dxh * xh, axis=-1, keepdims=True))
            return carry

        lax.fori_loop(0, tm // rs, chunk, 0)

        @pl.when(i == n - 1)
        def _():
            dgam_ref[...] = jnp.sum(gacc[...], axis=0, keepdims=True)
            dbet_ref[...] = jnp.sum(bacc[...], axis=0, keepdims=True)

    vec = pl.BlockSpec((1, cw), lambda i: (0, 0))
    return pl.pallas_call(
        body, name=name, grid=(n,),
        in_specs=[pl.BlockSpec((tm, cw), lambda i: (i, 1)), pl.BlockSpec((tm, cw), lambda i: (i, 6)),
                  pl.BlockSpec((tm, cw), lambda i: (i, 0)), vec, vec],
        out_specs=[pl.BlockSpec((tm, cw), lambda i: (i, 0)), pl.BlockSpec((tm, cw), lambda i: (i, 0)), vec, vec],
        out_shape=[jax.ShapeDtypeStruct((s, cw), F32), jax.ShapeDtypeStruct((s, cw), BF16),
                   jax.ShapeDtypeStruct((1, cw), F32), jax.ShapeDtypeStruct((1, cw), F32)],
        scratch_shapes=[pltpu.VMEM((8, cw), F32), pltpu.VMEM((8, cw), F32)],
        compiler_params=_params("arbitrary"),
    )(dy, p, dc, cnorm_g, cnorm_b)


def odd_bwd_conv(dy, p, ddc, dg2, sconv_w, dconv_w, name, tm=128):
    s = p.shape[0]
    cw = ddc.shape[1]
    n = s // tm
    lanes = 128
    hb = _halo_before(tm)
    ha = _halo_after(tm, s)

    def body(dy_ref, dya_ref, g1_ref, g1a_ref, bc_ref, bca_ref, hc_ref, hch_ref, cc_ref, cch_ref,
             ddc_ref, ddca_ref, ga_ref, gah_ref, gb_ref, gbh_ref, dg2_ref, sw_ref, dw_ref,
             dp_ref, dsw_ref, ddw_ref, ddb_ref, sw_acc, dw_acc, db_acc):
        i = pl.program_id(0)
        first = i == 0
        last = i == n - 1

        @pl.when(first)
        def _():
            sw_acc[...] = jnp.zeros_like(sw_acc)
            dw_acc[...] = jnp.zeros_like(dw_acc)
            db_acc[...] = jnp.zeros_like(db_acc)

        for l in range(cw // lanes):
            cols = slice(l * lanes, (l + 1) * lanes)
            mh = jnp.where(first, 0.0, cch_ref[:, cols].astype(F32) * hch_ref[:, cols].astype(F32))
            hcv = hc_ref[:, cols].astype(F32)
            ccv = cc_ref[:, cols].astype(F32)
            xx = jnp.concatenate([mh, ccv * hcv], axis=0)
            tap = _Taps(xx, tm, True)
            taps = [tap(SCONV_K - 1 - k) for k in range(SCONV_K)]
            cv = jnp.zeros((tm, lanes), F32)
            for k in range(SCONV_K):
                cv = cv + sw_ref[k:k + 1, cols] * taps[k]
            bcv = bc_ref[:, cols].astype(F32)
            dyv = dy_ref[:, cols].astype(F32)
            sg, dsg = _silu_and_grad(g1_ref[:, cols].astype(F32))
            dco = dyv * sg
            dp_ref[:, 5 * cw + l * lanes:5 * cw + (l + 1) * lanes] = (dyv * bcv * cv * dsg).astype(BF16)
            dp_ref[:, cw + l * lanes:cw + (l + 1) * lanes] = (dco * cv).astype(BF16)
            dcv = dco * bcv
            for k in range(SCONV_K):
                sw_acc[k * 8:(k + 1) * 8, cols] += _rowsum8(dcv * taps[k])
            dcv_a = jnp.where(last, 0.0, dya_ref[:, cols].astype(F32) * _silu(g1a_ref[:, cols].astype(F32))
                              * bca_ref[:, cols].astype(F32))
            xx = jnp.concatenate([dcv, dcv_a], axis=0)
            tap = _Taps(xx, tm, False)
            dm = jnp.zeros((tm, lanes), F32)
            for k in range(SCONV_K):
                dm = dm + sw_ref[k:k + 1, cols] * tap(SCONV_K - 1 - k)
            dp_ref[:, l * lanes:(l + 1) * lanes] = (dm * ccv).astype(BF16)
            dp_ref[:, 2 * cw + l * lanes:2 * cw + (l + 1) * lanes] = (dm * hcv).astype(BF16)
            gav = ga_ref[:, cols].astype(F32)
            sb = _sigmoid(gb_ref[:, cols].astype(F32))
            dh = jnp.where(first, 0.0, gah_ref[:, cols].astype(F32) * _sigmoid(gbh_ref[:, cols].astype(F32)))
            xx = jnp.concatenate([dh, gav * sb], axis=0)
            ddcv = ddc_ref[:, cols]
            db_acc[:, cols] += _rowsum8(ddcv)
            tap = _Taps(xx, tm, True)
            for k in range(CONF_K):
                dw_acc[k * 8:(k + 1) * 8, cols] += _rowsum8(ddcv * tap(CONF_K - 1 - k))
            ddc_a = jnp.where(last, 0.0, ddca_ref[:, cols])
            xx = jnp.concatenate([ddcv, ddc_a], axis=0)
            tap = _Taps(xx, tm, False)
            dgl = jnp.zeros((tm, lanes), F32)
            for k in range(CONF_K):
                dgl = dgl + dw_ref[k:k + 1, cols] * tap(CONF_K - 1 - k)
            dp_ref[:, 3 * cw + l * lanes:3 * cw + (l + 1) * lanes] = (dgl * sb).astype(BF16)
            dp_ref[:, 4 * cw + l * lanes:4 * cw + (l + 1) * lanes] = (dgl * gav * sb * (1.0 - sb)).astype(BF16)
        dp_ref[:, 6 * cw:7 * cw] = dg2_ref[...]

        @pl.when(last)
        def _():
            for k in range(SCONV_K):
                dsw_ref[k:k + 1, :] = jnp.sum(sw_acc[k * 8:(k + 1) * 8, :], axis=0, keepdims=True)
            for k in range(CONF_K):
                ddw_ref[k:k + 1, :] = jnp.sum(dw_acc[k * 8:(k + 1) * 8, :], axis=0, keepdims=True)
            ddb_ref[...] = jnp.sum(db_acc[...], axis=0, keepdims=True)

    def main(c):
        return pl.BlockSpec((tm, cw), lambda i: (i, c))

    def before(c):
        return pl.BlockSpec((HALO, cw), lambda i: (hb(i), c))

    def after(c):
        return pl.BlockSpec((HALO, cw), lambda i: (ha(i), c))

    def vec(r):
        return pl.BlockSpec((r, cw), lambda i: (0, 0))

    return pl.pallas_call(
        body, name=name, grid=(n,),
        in_specs=[main(0), after(0), main(5), after(5), main(1), after(1), main(0), before(0), main(2), before(2),
                  main(0), after(0), main(3), before(3), main(4), before(4), main(0), vec(SCONV_K), vec(CONF_K)],
        out_specs=[pl.BlockSpec((tm, 7 * cw), lambda i: (i, 0)), vec(SCONV_K), vec(CONF_K), vec(1)],
        out_shape=[jax.ShapeDtypeStruct((s, 7 * cw), BF16), jax.ShapeDtypeStruct((SCONV_K, cw), F32),
                   jax.ShapeDtypeStruct((CONF_K, cw), F32), jax.ShapeDtypeStruct((1, cw), F32)],
        scratch_shapes=[pltpu.VMEM((8 * SCONV_K, cw), F32), pltpu.VMEM((8 * CONF_K, cw), F32),
                        pltpu.VMEM((8, cw), F32)],
        compiler_params=_params("arbitrary"),
    )(dy, dy, p, p, p, p, p, p, p, p, ddc, ddc, p, p, p, p, dg2, sconv_w, dconv_w)


_ANY = pl.BlockSpec(memory_space=pl.ANY)


def _place():
    return lax.axis_index("x"), lax.axis_index("y"), lax.axis_index("c")


def all_gather(arrs, name, deps=()):
    n = len(arrs)

    def body(*refs):
        ins, outs = refs[:n], refs[n + len(deps):2 * n + len(deps)]
        send_sems, recv_sems, local_sems = refs[-3:]
        x, y, c = _place()
        me, sibling = (x, y, c), (x, y, 1 - c)
        chips = [(1 - x, y), (x, 1 - y), (1 - x, 1 - y)]

        def copy(a, k, block, to, src=None):
            px, py, pc = block
            dst = outs[a].at[4 * px + 2 * py + pc]
            return pltpu.make_async_remote_copy(
                src_ref=dst if src is None else src, dst_ref=dst,
                send_sem=send_sems.at[7 * a + k], recv_sem=recv_sems.at[7 * a + k],
                device_id=to, device_id_type=MESH)

        mine = [pltpu.make_async_copy(ins[a], outs[a].at[4 * x + 2 * y + c], local_sems.at[a]) for a in range(n)]
        first = []
        for a in range(n):
            first.append(copy(a, 0, me, sibling, src=ins[a]))
            first += [copy(a, 1 + j, me, (*chip, c), src=ins[a]) for j, chip in enumerate(chips)]
        for cp in first + mine:
            cp.start()
        passed = []
        for a in range(n):
            for j, chip in enumerate(chips):
                copy(a, 1 + j, (*chip, c), me).wait_recv()
                cp = copy(a, 4 + j, (*chip, c), sibling)
                cp.start()
                passed.append(cp)
        for a in range(n):
            copy(a, 0, sibling, me).wait_recv()
            for j, chip in enumerate(chips):
                copy(a, 4 + j, (*chip, 1 - c), me).wait_recv()
        for cp in first + passed:
            cp.wait_send()
        for cp in mine:
            cp.wait()

    return pl.pallas_call(
        body, name=name,
        out_shape=[jax.ShapeDtypeStruct((N_DEV,) + a.shape, a.dtype) for a in arrs],
        in_specs=[_ANY] * (n + len(deps)), out_specs=[_ANY] * n,
        scratch_shapes=[pltpu.SemaphoreType.DMA((7 * n,)), pltpu.SemaphoreType.DMA((7 * n,)),
                        pltpu.SemaphoreType.DMA((n,))],
    )(*arrs, *deps)


def in_proj_gathered(xs, g, w_own, extras, name, tm=512):
    s, d = xs.shape
    n = w_own.shape[1]
    arrs = [w_own] + list(extras)
    na = len(arrs)
    tr = 256

    def body(*refs):
        x_ref, g_ref, ins = refs[0], refs[1], refs[2:2 + na]
        h_out, p_ref, outs = refs[2 + na], refs[3 + na], refs[4 + na:4 + 2 * na]
        h_ref, wbuf, obuf, send_sems, recv_sems, load_sem, store_sems, own_sems, h_sem = refs[4 + 2 * na:]
        x, y, c = _place()
        me, sibling = (x, y, c), (x, y, 1 - c)
        chips = [(1 - x, y), (x, 1 - y), (1 - x, 1 - y)]

        def slot(block):
            return 4 * block[0] + 2 * block[1] + block[2]

        def copy(a, k, block, to, src=None):
            dst = outs[a].at[slot(block)]
            return pltpu.make_async_remote_copy(
                src_ref=dst if src is None else src, dst_ref=dst,
                send_sem=send_sems.at[7 * a + k], recv_sem=recv_sems.at[7 * a + k],
                device_id=to, device_id_type=MESH)

        first = []
        for a in range(na):
            first.append(copy(a, 0, me, sibling, src=ins[a]))
            first += [copy(a, 1 + j, me, (*chip, c), src=ins[a]) for j, chip in enumerate(chips)]
        for cp in first:
            cp.start()
        own = pltpu.make_async_copy(wbuf.at[0], outs[0].at[slot(me)], own_sems.at[0])
        mine = [pltpu.make_async_copy(ins[a], outs[a].at[slot(me)], own_sems.at[a]) for a in range(1, na)]
        stores = [None, None]

        def norm(i, carry):
            rows = pl.ds(pl.multiple_of(i * tr, tr), tr)
            xv = x_ref[rows, :]
            r = lax.rsqrt(jnp.mean(xv * xv, axis=-1, keepdims=True) + EPS)
            h_ref[rows, :] = (xv * r * g_ref[...]).astype(BF16)
            return carry

        lax.fori_loop(0, s // tr, norm, 0)
        h_store = pltpu.make_async_copy(h_ref, h_out, h_sem)
        h_store.start()

        def multiply(k, block, w_from):
            b = k % 2
            if k == 2:
                own.wait()
            load = pltpu.make_async_copy(w_from, wbuf.at[b], load_sem)
            load.start()
            if stores[b] is not None:
                stores[b].wait()
            load.wait()
            if k == 0:
                own.start()

            def chunk(i, carry):
                rows = pl.ds(pl.multiple_of(i * tm, tm), tm)
                obuf[b, rows, :] = jnp.dot(h_ref[rows, :], wbuf[b], preferred_element_type=F32).astype(BF16)
                return carry

            lax.fori_loop(0, s // tm, chunk, 0)
            stores[b] = pltpu.make_async_copy(
                obuf.at[b], p_ref.at[:, pl.ds(pl.multiple_of(slot(block) * n, 128), n)], store_sems.at[b])
            stores[b].start()

        def arrived(arrays, j, chip):
            for a in arrays:
                copy(a, 1 + j, (*chip, c), me).wait_recv()
                cp = copy(a, 4 + j, (*chip, c), sibling)
                cp.start()
                passed.append(cp)

        small = range(1, na)
        passed = []
        multiply(0, me, ins[0])
        copy(0, 0, sibling, me).wait_recv()
        multiply(1, sibling, outs[0].at[slot(sibling)])
        for j, chip in enumerate(chips):
            arrived([0], j, chip)
            multiply(2 + 2 * j, (*chip, c), outs[0].at[slot((*chip, c))])
            copy(0, 4 + j, (*chip, 1 - c), me).wait_recv()
            multiply(3 + 2 * j, (*chip, 1 - c), outs[0].at[slot((*chip, 1 - c))])
        for cp in mine:
            cp.start()
        for a in small:
            copy(a, 0, sibling, me).wait_recv()
        for j, chip in enumerate(chips):
            arrived(small, j, chip)
        for j, chip in enumerate(chips):
            for a in small:
                copy(a, 4 + j, (*chip, 1 - c), me).wait_recv()
        for cp in first + passed:
            cp.wait_send()
        for cp in mine + stores + [h_store]:
            cp.wait()

    vmem = pl.BlockSpec(memory_space=pltpu.VMEM)
    outs = pl.pallas_call(
        body, name=name,
        out_shape=[jax.ShapeDtypeStruct((s, d), BF16), jax.ShapeDtypeStruct((s, N_DEV * n), BF16)]
        + [jax.ShapeDtypeStruct((N_DEV,) + a.shape, a.dtype) for a in arrs],
        in_specs=[vmem, vmem] + [_ANY] * na, out_specs=[_ANY] * (2 + na),
        scratch_shapes=[pltpu.VMEM((s, d), BF16), pltpu.VMEM((2, d, n), BF16), pltpu.VMEM((2, s, n), BF16),
                        pltpu.SemaphoreType.DMA((7 * na,)), pltpu.SemaphoreType.DMA((7 * na,)),
                        pltpu.SemaphoreType.DMA, pltpu.SemaphoreType.DMA((2,)), pltpu.SemaphoreType.DMA((na,)),
                        pltpu.SemaphoreType.DMA],
        compiler_params=pltpu.CompilerParams(vmem_limit_bytes=VMEM_LIMIT),
    )(xs, g, *arrs)
    return outs[0], outs[1], outs[2], outs[3:]


_HBM = pl.BlockSpec(memory_space=pltpu.HBM)
_SEM = pl.BlockSpec(memory_space=pltpu.SEMAPHORE)
_DATAFLOW = pltpu.SideEffectType.DATAFLOW_SIDE_EFFECTING


def _peers_per_array(kind):
    return 1 if kind in ("sibling", "halves") else 3


def _split_copies(kind, srcs, lands, send_sems, recv_sems):
    x, y, c = _place()
    per = _peers_per_array(kind)
    out = []
    for a in range(len(lands)):
        if kind == "sibling":
            peers = [((x, y, 1 - c), srcs[a].at[:, pl.ds(1 - c, 1)], lands[a], lands[a])]
        elif kind == "halves":
            mine, its = lands[a].at[:, pl.ds(c, 1)], lands[a].at[:, pl.ds(1 - c, 1)]
            peers = [((x, y, 1 - c), mine, mine, its)]
        else:
            peers = []
            for px, py in [(1 - x, y), (x, 1 - y), (1 - x, 1 - y)]:
                if kind == "gather":
                    views = (srcs[a], lands[a].at[4 * x + 2 * y + c], lands[a].at[4 * px + 2 * py + c])
                else:
                    views = (srcs[a].at[2 * px + py], lands[a].at[2 * x + y], lands[a].at[2 * px + py])
                peers.append(((px, py, c),) + views)
        for j, (peer, src, dst, arrives) in enumerate(peers):
            sems = dict(send_sem=send_sems.at[per * a + j], recv_sem=recv_sems.at[per * a + j],
                        device_id=peer, device_id_type=MESH)
            out.append((pltpu.make_async_remote_copy(src_ref=src, dst_ref=dst, **sems),
                        pltpu.make_async_remote_copy(src_ref=src, dst_ref=arrives, **sems)))
    return out


def split_start(kind, srcs, lands, deps, name):
    ns, nl = len(srcs), len(lands)
    n_sems = _peers_per_array(kind) * nl
    held = list(srcs) + list(lands)

    def body(*refs):
        send_sems, recv_sems = refs[len(held) + len(deps)], refs[len(held) + len(deps) + 1]
        for copy, _ in _split_copies(kind, refs[:ns], refs[ns:ns + nl], send_sems, recv_sems):
            copy.start()
        token = refs[-1]
        token[...] = jnp.zeros_like(token)

    outs = pl.pallas_call(
        body, name=name,
        out_shape=(pltpu.SemaphoreType.DMA((n_sems,)), pltpu.SemaphoreType.DMA((n_sems,)),
                   *[pltpu.HBM(a.shape, a.dtype) for a in held], jax.ShapeDtypeStruct((8, 128), F32)),
        in_specs=[_HBM] * len(held) + [_ANY] * len(deps),
        out_specs=(_SEM, _SEM, *([_HBM] * len(held)), pl.BlockSpec(memory_space=pltpu.VMEM)),
        input_output_aliases={i: 2 + i for i in range(len(held))},
        compiler_params=pltpu.CompilerParams(has_side_effects=_DATAFLOW),
    )(*[pltpu.with_memory_space_constraint(a, pltpu.HBM) for a in held], *deps)
    return outs[0], outs[1], list(outs[2:2 + ns]), list(outs[2 + ns:2 + ns + nl]), outs[-1]


def split_wait(kind, send_sems, recv_sems, srcs, lands, afters, name):
    ns, nl = len(srcs), len(lands)
    held = list(srcs) + list(lands)

    def body(*refs):
        for _, arrival in _split_copies(kind, refs[:ns], refs[ns:ns + nl], refs[ns + nl], refs[ns + nl + 1]):
            arrival.wait_send()
            arrival.wait_recv()

    outs = pl.pallas_call(
        body, name=name,
        out_shape=[pltpu.HBM(a.shape, a.dtype) for a in held],
        in_specs=[_HBM] * len(held) + [_SEM, _SEM] + [_ANY] * len(afters),
        out_specs=[_HBM] * len(held),
        input_output_aliases={i: i for i in range(len(held))},
        compiler_params=pltpu.CompilerParams(has_side_effects=_DATAFLOW),
    )(*held, send_sems, recv_sems, *afters)
    return list(outs[:ns]), list(outs[ns:])


def place_block(land, block, dev, name):
    r, c = block.shape
    tr = min(r, 512)

    def body(dev_ref, land_ref, b_ref, o_ref):
        del dev_ref, land_ref
        o_ref[...] = b_ref[...]

    return pl.pallas_call(
        body, name=name,
        grid_spec=pltpu.PrefetchScalarGridSpec(
            num_scalar_prefetch=1, grid=(r // tr,),
            in_specs=[_ANY, pl.BlockSpec((tr, c), lambda i, dev_ref: (i, 0))],
            out_specs=pl.BlockSpec((None, tr, c), lambda i, dev_ref: (dev_ref[0], i, 0))),
        out_shape=jax.ShapeDtypeStruct(land.shape, land.dtype),
        input_output_aliases={1: 0},
        compiler_params=_params("parallel"),
    )(dev, land, block)


def pair_add(own, recv, core, name):
    _, _, r, c = own.shape
    tr = min(r, 512)

    def body(core_ref, own_ref, recv_ref, o_ref):
        del core_ref
        o_ref[...] = (own_ref[...].astype(F32) + recv_ref[...].astype(F32)).astype(BF16)

    return pl.pallas_call(
        body, name=name,
        grid_spec=pltpu.PrefetchScalarGridSpec(
            num_scalar_prefetch=1, grid=(4, r // tr),
            in_specs=[pl.BlockSpec((None, None, tr, c), lambda k, i, core_ref: (k, core_ref[0], i, 0)),
                      pl.BlockSpec((None, None, tr, c), lambda k, i, core_ref: (k, 0, i, 0))],
            out_specs=pl.BlockSpec((None, tr, c), lambda k, i, core_ref: (k, i, 0))),
        out_shape=jax.ShapeDtypeStruct((4, r, c), BF16),
        compiler_params=_params("parallel", "parallel"),
    )(core, own, recv)


def _adamw_math(w, g, m, v):
    m2 = ADAM_B1 * m + (1.0 - ADAM_B1) * g
    v2 = ADAM_B2 * v + (1.0 - ADAM_B2) * (g * g)
    m_hat = m2 / (1.0 - ADAM_B1 ** ADAM_STEP)
    v_hat = v2 / (1.0 - ADAM_B2 ** ADAM_STEP)
    delta = -ADAM_LR * (m_hat / (jnp.sqrt(v_hat) + ADAM_EPS) + ADAM_WD * w)
    return delta, m2, v2


def adamw_big(w, m, v, own, got, chip, name):
    r, c = w.shape
    tr = min(r, 256)

    def body(chip_ref, w_ref, m_ref, v_ref, p0, p1, p2, p3, g_ref, d_ref, m2_ref, v2_ref):
        del chip_ref
        g = ((p0[...].astype(F32) + p1[...].astype(F32)) + p2[...].astype(F32)) + p3[...].astype(F32)
        delta, m2, v2 = _adamw_math(w_ref[...], g, m_ref[...], v_ref[...])
        g_ref[...] = g
        d_ref[...] = delta
        m2_ref[...] = m2
        v2_ref[...] = v2

    row = pl.BlockSpec((tr, c), lambda i, chip_ref: (i, 0))

    def slab(flip):
        return pl.BlockSpec((None, tr, c), lambda i, chip_ref: (chip_ref[0] ^ flip, i, 0))

    return pl.pallas_call(
        body, name=name,
        grid_spec=pltpu.PrefetchScalarGridSpec(
            num_scalar_prefetch=1, grid=(r // tr,),
            in_specs=[row, row, row, slab(0), slab(1), slab(2), slab(3)],
            out_specs=[row] * 4),
        out_shape=[jax.ShapeDtypeStruct((r, c), F32)] * 4,
        compiler_params=_params("parallel"),
    )(chip, w, m, v, own, got, got, got)


def sum_devices(g8, name):
    def body(g_ref, o_ref):
        tot = g_ref[0]
        for k in range(1, N_DEV):
            tot = tot + g_ref[k]
        o_ref[...] = tot

    return pl.pallas_call(body, name=name, out_shape=jax.ShapeDtypeStruct(g8.shape[1:], F32))(g8)


def adamw_small(ws, gs, ms, vs, name):
    n = len(ws)

    def body(*refs):
        w_r, g_r, m_r, v_r = refs[:n], refs[n:2 * n], refs[2 * n:3 * n], refs[3 * n:4 * n]
        d_o, m_o, v_o = refs[4 * n:5 * n], refs[5 * n:6 * n], refs[6 * n:7 * n]
        for k in range(n):
            delta, m2, v2 = _adamw_math(w_r[k][...], g_r[k][...], m_r[k][...], v_r[k][...])
            d_o[k][...] = delta
            m_o[k][...] = m2
            v_o[k][...] = v2

    shapes = [jax.ShapeDtypeStruct(w.shape, F32) for w in ws]
    outs = pl.pallas_call(body, name=name, out_shape=shapes * 3)(*ws, *gs, *ms, *vs)
    return outs[:n], outs[n:2 * n], outs[2 * n:]


def _rows128(a):
    return a.reshape(-1, 128)


def _pad_rows(a, rows):
    return jnp.pad(a, ((0, rows - a.shape[0]), (0, 0)))


def kernel(x, ln_pre_even, w_in_even, pool_w, pool_scale, w_out_even, ln_post_even, ln_pre_odd, w_in_odd, sconv_w, dconv_w, dconv_b, cnorm_g, cnorm_b, w_out_odd, ln_post_odd, loss_target, m_ln_pre_even, m_w_in_even, m_pool_w, m_pool_scale, m_w_out_even, m_ln_post_even, m_ln_pre_odd, m_w_in_odd, m_sconv_w, m_dconv_w, m_dconv_b, m_cnorm_g, m_cnorm_b, m_w_out_odd, m_ln_post_odd, v_ln_pre_even, v_w_in_even, v_pool_w, v_pool_scale, v_w_out_even, v_ln_post_even, v_ln_pre_odd, v_w_in_odd, v_sconv_w, v_dconv_w, v_dconv_b, v_cnorm_g, v_cnorm_b, v_w_out_odd, v_ln_post_odd):
    xs = x[0]
    tgt = loss_target[0]
    s, d = xs.shape
    half = d // 2
    n_heads = half // HEAD_DIM
    ng = len(POOL_WINDOWS)
    cwp = half // ng
    dev = 4 * lax.axis_index("x") + 2 * lax.axis_index("y") + lax.axis_index("c")
    core = lax.axis_index("c").astype(jnp.int32).reshape(1)

    pr = pool_w.shape[2]
    cl = sconv_w.shape[2]
    small_parts = [(_rows128(ln_pre_odd), 8), (sconv_w[0], 8), (dconv_w[0], 32), (dconv_b, 8),
                   (cnorm_g, 8), (cnorm_b, 8), (_rows128(ln_post_odd), 8)]
    small_local = jnp.concatenate([_pad_rows(a, r) for a, r in small_parts], axis=0)
    h0, p0, g_wie, (g_pw, g_small) = in_proj_gathered(
        xs, ln_pre_even, w_in_even[0].astype(BF16), [pool_w[0].reshape(ng * pr, cwp).astype(BF16), small_local],
        "ag_in_proj_even")
    comm = _Exchanges(dev, core, d)
    token = comm.start_weights("out_even", [w_out_even[0].astype(BF16)], [p0])
    sb_dep = comm.start_weights("odd", [w_in_odd[0].astype(BF16), w_out_odd[0].astype(BF16)], [token])
    pool_full = g_pw.reshape(N_DEV, ng, pr, cwp).transpose(1, 0, 2, 3).reshape(ng, cwp, cwp)
    nl = ln_pre_odd.shape[1] // 128

    def chan(lo, rows):
        return g_small[:, lo:lo + rows].transpose(1, 0, 2).reshape(rows, N_DEV * cl)

    ln_pre_odd_f = g_small[:, 0:nl].reshape(1, d)
    sconv_f = chan(8, SCONV_K)
    dconv_f = chan(16, CONF_K)
    dconv_b_f = chan(48, 1)
    cnorm_g_f = chan(56, 1)
    cnorm_b_f = chan(64, 1)
    ln_post_odd_f = g_small[:, 72:72 + nl].reshape(1, d)

    loss_blk, grad_x, small_g = _fwd_bwd(
        xs, tgt, ln_pre_even, h0, p0, g_wie, pool_full, pool_scale, ln_post_even, ln_pre_odd_f,
        sconv_f, dconv_f, dconv_b_f, cnorm_g_f, cnorm_b_f, ln_post_odd_f, comm, sb_dep)
    small_w = [ln_pre_even, pool_scale, ln_post_even, ln_pre_odd, sconv_w[0], dconv_w[0], dconv_b, cnorm_g, cnorm_b, ln_post_odd]
    small_m = [m_ln_pre_even, m_pool_scale, m_ln_post_even, m_ln_pre_odd, m_sconv_w[0], m_dconv_w[0], m_dconv_b, m_cnorm_g, m_cnorm_b, m_ln_post_odd]
    small_v = [v_ln_pre_even, v_pool_scale, v_ln_post_even, v_ln_pre_odd, v_sconv_w[0], v_dconv_w[0], v_dconv_b, v_cnorm_g, v_cnorm_b, v_ln_post_odd]
    big = {"w_in_even": (w_in_even, m_w_in_even, v_w_in_even), "pool_w": (pool_w, m_pool_w, v_pool_w),
           "w_out_even": (w_out_even, m_w_out_even, v_w_out_even), "w_in_odd": (w_in_odd, m_w_in_odd, v_w_in_odd),
           "w_out_odd": (w_out_odd, m_w_out_odd, v_w_out_odd)}
    upd = comm.finish_updates(big, [grad_x])
    upd.update(comm.finish_updates(big, [grad_x]))
    sg, sd, sm, sv, loss = _update_small(small_g, loss_blk, small_w, small_m, small_v, dev, d, cl,
                                         deps=[upd["w_in_odd"][1], upd["w_out_even"][1]])
    upd.update(comm.finish_updates(big, sd))
    (g_wie_o, d_wie, m_wie, v_wie), (g_pw_o, d_pw, m_pw, v_pw) = upd["w_in_even"], upd["pool_w"]
    (g_woe_o, d_woe, m_woe, v_woe), (g_wio_o, d_wio, m_wio, v_wio) = upd["w_out_even"], upd["w_in_odd"]
    g_woo_o, d_woo, m_woo, v_woo = upd["w_out_odd"]

    def order(small, wie, pw, woe, wio, woo):
        return [small[0], wie, pw, small[1], woe, small[2], small[3], wio, small[4], small[5], small[6],
                small[7], small[8], woo, small[9]]

    grads = order(sg, g_wie_o, g_pw_o, g_woe_o, g_wio_o, g_woo_o)
    deltas = order(sd, d_wie, d_pw, d_woe, d_wio, d_woo)
    new_m = order(sm, m_wie, m_pw, m_woe, m_wio, m_woo)
    new_v = order(sv, v_wie, v_pw, v_woe, v_wio, v_woo)
    return (loss, grad_x[None], *grads, *deltas, *new_m, *new_v)


def _fwd_bwd(xs, tgt, ln_pre_even, h0, p0, g_wie, pool_full, pool_scale, ln_post_even, ln_pre_odd_f,
             sconv_f, dconv_f, dconv_b_f, cnorm_g_f, cnorm_b_f, ln_post_odd_f, comm, sb_dep):
    d = xs.shape[1]
    n_heads = d // 2 // HEAD_DIM
    ng, cwp = pool_full.shape[0], pool_full.shape[1]
    a0, sb_wts = sb_fwd(p0, n_heads, "sb_fwd", dep=sb_dep)
    dep = comm.weights_arrived("out_even", after=a0)
    y0 = even_mix_fwd(a0, p0, pool_full, pool_scale, "even_mix_fwd", dep=dep)
    (w_out_e,) = comm.weights("out_even", after=y0)
    w_out_e = w_out_e.reshape(1, d, d)
    o0 = mm_nn(y0, w_out_e, F32, "out_proj_even", tm=1024)
    dep = comm.weights_arrived("odd", after=o0)
    x1, h1 = postnorm_fwd(xs, o0, ln_post_even, ln_pre_odd_f, "post_even", dep=dep)
    g_wio, w_out_o = comm.weights("odd", after=x1)
    w_out_o = w_out_o.reshape(1, d, d)
    p1 = mm_nn(h1, g_wio, BF16, "in_proj_odd")
    y1, dc = odd_mix_fwd(p1, sconv_f, dconv_f, dconv_b_f, cnorm_g_f, cnorm_b_f, "odd_mix_fwd")
    o1 = mm_nn(y1, w_out_o, F32, "out_proj_odd", tm=1024)
    loss_blk, gx2, do1, dg_post_odd = final_fwd_bwd(x1, o1, ln_post_odd_f, tgt, "post_odd_loss")

    dw_out_o = mm_tn(y1, do1, 1, BF16, "dw_out_odd")
    dy1 = mm_nt(do1, w_out_o, BF16, "dy_odd")
    ddc, dg2, dgam, dbet = odd_bwd_ln(dy1, p1, dc, cnorm_g_f, cnorm_b_f, "odd_bwd_ln")
    dp1, dsconv, ddconv, ddconv_b = odd_bwd_conv(dy1, p1, ddc, dg2, sconv_f, dconv_f, "odd_bwd_conv")
    dw_in_o = mm_tn(h1, dp1, N_DEV, BF16, "dw_in_odd")
    dep = comm.reduce_begin({"w_out_odd": dw_out_o.reshape(N_DEV, d // N_DEV, d), "w_in_odd": dw_in_o}, "odd")
    dh1 = mm_nt(dp1, g_wio, F32, "dh_odd", dep=dep)
    dep = comm.reduce_send(after=dh1)
    gx1, dg_pre_odd, do0, dg_post_even = norm_bwd(dh1, x1, ln_pre_odd_f, gx2, "pre_odd_post_even_bwd",
                                                  inp2=o0, g2=ln_post_even, dep=dep)

    dw_out_e = mm_tn(y0, do0, 1, BF16, "dw_out_even")
    dy0 = mm_nt(do0, w_out_e, BF16, "dy_even")
    da0, du0, dg0, dpool, dpool_scale = even_mix_bwd(dy0, a0, p0, pool_full, pool_scale, "even_mix_bwd")
    pr = cwp // N_DEV
    dpool_slabs = dpool.astype(BF16).reshape(ng, N_DEV, pr, cwp).transpose(1, 0, 2, 3).reshape(N_DEV, ng * pr, cwp)
    dep = comm.reduce_begin({"w_out_even": dw_out_e.reshape(N_DEV, d // N_DEV, d), "pool_w": dpool_slabs}, "even_out")
    dq0, dk0, dv0 = sb_bwd(p0, a0, sb_wts, da0, n_heads, "sb_bwd", dep=dep)
    dep = comm.reduce_send(after=dq0)
    dp0 = jnp.concatenate([dq0, dk0, dv0, du0, dg0], axis=1)
    dw_in_e = mm_tn(h0, dp0, N_DEV, BF16, "dw_in_even", dep=dep)
    dep = comm.reduce_begin({"w_in_even": dw_in_e}, "even_in")
    dep = comm.reduce_send(after=dep)
    dh0 = mm_nt(dp0, g_wie, F32, "dh_even", dep=dep)
    dep = None
    grad_x, dg_pre_even = norm_bwd(dh0, xs, ln_pre_even, gx1, "pre_even_bwd", dep=dep)
    small_g = [dg_pre_even, dpool_scale, dg_post_even, dg_pre_odd, dsconv, ddconv, ddconv_b, dgam, dbet, dg_post_odd]
    return loss_blk, grad_x, small_g


class _Exchanges:
    def __init__(self, dev, core, d):
        self.dev = dev.astype(jnp.int32).reshape(1)
        self.core = core
        self.chip = (dev // 2).astype(jnp.int32).reshape(1)
        self.d = d
        self.in_flight = {}
        self.to_sibling = None
        self.pending = []

    def start_weights(self, tag, blocks, afters):
        lands = [lax.empty((N_DEV,) + b.shape, b.dtype) for b in blocks]
        send, recv, srcs, lands, token = split_start("gather", blocks, lands, afters, "ag_start_" + tag)
        self.in_flight[tag] = (send, recv, srcs, lands)
        return token

    def weights_arrived(self, tag, after):
        send, recv, srcs, lands = self.in_flight.pop(tag)
        srcs, lands = split_wait("gather", send, recv, srcs, lands, [after], "ag_wait_" + tag)
        lands = [place_block(l, b, self.dev, "ag_own_%s_%d" % (tag, k)) for k, (l, b) in enumerate(zip(lands, srcs))]
        lands = [l.reshape((4, 2) + l.shape[1:]) for l in lands]
        send, recv, _, lands, token = split_start("halves", [], lands, [], "ag_sibling_start_" + tag)
        self.in_flight[tag] = (send, recv, lands)
        return token

    def weights(self, tag, after):
        send, recv, lands = self.in_flight.pop(tag)
        _, lands = split_wait("halves", send, recv, [], lands, [after], "ag_sibling_wait_" + tag)
        return [l.reshape((N_DEV,) + l.shape[2:]) for l in lands]

    def reduce_begin(self, partials, tag):
        names = list(partials)
        arrs = [partials[k].reshape((4, 2) + partials[k].shape[1:]) for k in names]
        lands = [lax.empty((4, 1) + a.shape[2:], a.dtype) for a in arrs]
        send, recv, srcs, lands, token = split_start("sibling", arrs, lands, [], "rs_sibling_start_" + tag)
        self.to_sibling = (tag, names, send, recv, srcs, lands)
        return token

    def reduce_send(self, after):
        tag, names, send, recv, srcs, lands = self.to_sibling
        srcs, lands = split_wait("sibling", send, recv, srcs, lands, [after], "rs_sibling_wait_" + tag)
        sums = [pair_add(o, r, self.core, "rs_pair_add_" + k) for k, o, r in zip(names, srcs, lands)]
        zones = [lax.empty(a.shape, a.dtype) for a in sums]
        send, recv, srcs, zones, token = split_start("scatter", sums, zones, [], "rs_start_" + tag)
        self.pending.append((tag, names, send, recv, srcs, zones))
        return token

    def finish_updates(self, big, afters):
        tag, names, send, recv, srcs, lands = self.pending.pop(0)
        srcs, lands = split_wait("scatter", send, recv, srcs, lands, afters, "rs_wait_" + tag)
        out = {}
        for name, own, got in zip(names, srcs, lands):
            w, m, v = big[name]
            shp = own.shape[1:]
            outs = adamw_big(w.reshape(shp), m.reshape(shp), v.reshape(shp), own, got, self.chip, "adamw_" + name)
            out[name] = [o.reshape(w.shape) for o in outs]
        return out


def _update_small(small_g, loss_blk, small_w, small_m, small_v, dev, d, cl, deps):
    packed = jnp.concatenate([_rows128(g) for g in small_g] + [loss_blk], axis=0)
    (g8,) = all_gather([packed], "ag_small_grads", deps)
    tot = sum_devices(g8, "sum_small_grads")
    loss = tot[packed.shape[0] - 8, 0]
    full_g = []
    lo = 0
    for g in small_g:
        rows = g.size // 128
        full_g.append(tot[lo:lo + rows].reshape(g.shape))
        lo += rows

    def mine(g, width):
        return lax.dynamic_slice_in_dim(g, dev * width, width, axis=g.ndim - 1)

    fg = full_g
    small_gl = [fg[0], fg[1], fg[2], mine(fg[3], d // N_DEV), mine(fg[4], cl), mine(fg[5], cl), mine(fg[6], cl),
                mine(fg[7], cl), mine(fg[8], cl), mine(fg[9], d // N_DEV)]
    sd, sm, sv = adamw_small(small_w, small_gl, small_m, small_v, "adamw_small")

    def like(k, a):
        return a[None] if k in (4, 5) else a

    sg = [like(k, a) for k, a in enumerate(small_gl)]
    sd = [like(k, a) for k, a in enumerate(sd)]
    sm = [like(k, a) for k, a in enumerate(sm)]
    sv = [like(k, a) for k, a in enumerate(sv)]
    return sg, sd, sm, sv, loss
```

```python
import functools
import math

import jax
import jax.numpy as jnp
from jax import lax
from jax.experimental import pallas as pl
from jax.experimental.pallas import tpu as pltpu

F32 = jnp.float32
BF16 = jnp.bfloat16
EPS = 1e-6
HEAD_DIM = 128
POOL_WINDOWS = (2, 4, 8, 16)
SCONV_K = 3
CONF_K = 31
HALO = 32
N_DEV = 8
VMEM_LIMIT = 56 * 1024 * 1024
MESH = pl.DeviceIdType.MESH

ADAM_LR = 0.001
ADAM_B1 = 0.9
ADAM_B2 = 0.999
ADAM_EPS = 1e-08
ADAM_WD = 0.01
ADAM_STEP = 10


def _params(*sem):
    return pltpu.CompilerParams(dimension_semantics=sem, vmem_limit_bytes=VMEM_LIMIT)


def _sigmoid(v):
    return 1.0 / (1.0 + jnp.exp(-v))


def _silu(v):
    return v * _sigmoid(v)


def _silu_and_grad(v):
    s = _sigmoid(v)
    return v * s, s * (1.0 + v * (1.0 - s))


def _rowsum8(v):
    r, c = v.shape
    return jnp.sum(v.reshape(r // 8, 8, c), axis=0)


SUBLANES = 8


class _Taps:
    def __init__(self, xx, rows, before):
        self.xx, self.rows, self.before, self.rotated = xx, rows, before, {}

    def __call__(self, i):
        r, q = i % SUBLANES, i // SUBLANES
        if r not in self.rotated:
            n = self.xx.shape[0]
            self.rotated[r] = self.xx if r == 0 else pltpu.roll(self.xx, r if self.before else n - r, 0)
        lo = HALO - SUBLANES * q if self.before else SUBLANES * q
        return self.rotated[r][lo:lo + self.rows]


def _window_sum(xx, win, before):
    n = xx.shape[0]
    acc = xx
    k = 1
    while k < win:
        acc = acc + pltpu.roll(acc, k if before else n - k, 0)
        k *= 2
    return acc


def postnorm_fwd(x, o, g, g_next, name, tm=256, dep=None):
    s, d = x.shape
    dep_args, dep_specs = _after(dep)

    def body(x_ref, o_ref, g_ref, gn_ref, *rest):
        y_ref, h_ref = rest[-2:]
        ov = o_ref[...]
        r = lax.rsqrt(jnp.mean(ov * ov, axis=-1, keepdims=True) + EPS)
        y = x_ref[...] + ov * r * g_ref[...]
        y_ref[...] = y
        r2 = lax.rsqrt(jnp.mean(y * y, axis=-1, keepdims=True) + EPS)
        h_ref[...] = (y * r2 * gn_ref[...]).astype(BF16)

    row = pl.BlockSpec((tm, d), lambda i: (i, 0))
    vec = pl.BlockSpec((1, d), lambda i: (0, 0))
    return pl.pallas_call(
        body, name=name, grid=(s // tm,),
        in_specs=[row, row, vec, vec] + dep_specs, out_specs=[row, row],
        out_shape=[jax.ShapeDtypeStruct((s, d), F32), jax.ShapeDtypeStruct((s, d), BF16)],
        compiler_params=_params("parallel"),
    )(x, o, g, g_next, *dep_args)


def final_fwd_bwd(x1, o, g, target, name, tm=256):
    s, d = x1.shape
    n = s // tm

    def body(x_ref, o_ref, g_ref, t_ref, loss_ref, gx_ref, do_ref, dg_ref, lacc, gacc):
        i = pl.program_id(0)

        @pl.when(i == 0)
        def _():
            lacc[...] = jnp.zeros_like(lacc)
            gacc[...] = jnp.zeros_like(gacc)

        ov = o_ref[...]
        gv = g_ref[...]
        r = lax.rsqrt(jnp.mean(ov * ov, axis=-1, keepdims=True) + EPS)
        oh = ov * r
        diff = x_ref[...] + oh * gv - t_ref[...]
        lacc[...] += _rowsum8(diff * diff)
        gx = diff * (1.0 / d)
        gx_ref[...] = gx
        gacc[...] += _rowsum8(gx * oh)
        dn = gx * gv
        do_ref[...] = (r * (dn - oh * jnp.mean(dn * oh, axis=-1, keepdims=True))).astype(BF16)

        @pl.when(i == n - 1)
        def _():
            tot = jnp.sum(jnp.sum(lacc[...], axis=0, keepdims=True), axis=1, keepdims=True)
            loss_ref[...] = jnp.broadcast_to(tot * (0.5 / d), loss_ref.shape)
            dg_ref[...] = jnp.sum(gacc[...], axis=0, keepdims=True)

    row = pl.BlockSpec((tm, d), lambda i: (i, 0))
    vec = pl.BlockSpec((1, d), lambda i: (0, 0))
    return pl.pallas_call(
        body, name=name, grid=(n,),
        in_specs=[row, row, vec, row],
        out_specs=[pl.BlockSpec((8, 128), lambda i: (0, 0)), row, row, vec],
        out_shape=[jax.ShapeDtypeStruct((8, 128), F32), jax.ShapeDtypeStruct((s, d), F32),
                   jax.ShapeDtypeStruct((s, d), BF16), jax.ShapeDtypeStruct((1, d), F32)],
        scratch_shapes=[pltpu.VMEM((8, d), F32), pltpu.VMEM((8, d), F32)],
        compiler_params=_params("arbitrary"),
    )(x1, o, g, target)


def _rms_bwd_rows(dyv, xv, gv):
    r = lax.rsqrt(jnp.mean(xv * xv, axis=-1, keepdims=True) + EPS)
    xh = xv * r
    dn = dyv * gv
    return r * (dn - xh * jnp.mean(dn * xh, axis=-1, keepdims=True)), _rowsum8(dyv * xh)


def norm_bwd(dy, inp, g, resid, name, inp2=None, g2=None, tm=256, dep=None):
    s, d = inp.shape
    n = s // tm
    chain = inp2 is not None

    def body(*refs):
        dy_ref, x_ref, g_ref, r_ref = refs[:4]
        outs = refs[-6:] if chain else refs[-3:]
        i = pl.program_id(0)

        @pl.when(i == 0)
        def _():
            for acc in outs[-2:] if chain else outs[-1:]:
                acc[...] = jnp.zeros_like(acc)

        if chain:
            x2_ref, g2_ref = refs[4:6]
            dx_ref, dg_ref, dx2_ref, dg2_ref, gacc, gacc2 = outs
        else:
            dx_ref, dg_ref, gacc = outs
        dx, dg_rows = _rms_bwd_rows(dy_ref[...].astype(F32), x_ref[...], g_ref[...])
        dx = dx + r_ref[...]
        dx_ref[...] = dx
        gacc[...] += dg_rows
        if chain:
            dx2, dg2_rows = _rms_bwd_rows(dx, x2_ref[...], g2_ref[...])
            dx2_ref[...] = dx2.astype(BF16)
            gacc2[...] += dg2_rows

        @pl.when(i == n - 1)
        def _():
            dg_ref[...] = jnp.sum(gacc[...], axis=0, keepdims=True)
            if chain:
                dg2_ref[...] = jnp.sum(gacc2[...], axis=0, keepdims=True)

    row = pl.BlockSpec((tm, d), lambda i: (i, 0))
    vec = pl.BlockSpec((1, d), lambda i: (0, 0))
    dep_args, dep_specs = _after(dep)
    extra = [inp2, g2] if chain else []
    return pl.pallas_call(
        body, name=name, grid=(n,),
        in_specs=[row, row, vec, row] + ([row, vec] if chain else []) + dep_specs,
        out_specs=[row, vec] * (2 if chain else 1),
        out_shape=[jax.ShapeDtypeStruct((s, d), F32), jax.ShapeDtypeStruct((1, d), F32)]
        + ([jax.ShapeDtypeStruct((s, d), BF16), jax.ShapeDtypeStruct((1, d), F32)] if chain else []),
        scratch_shapes=[pltpu.VMEM((8, d), F32)] * (2 if chain else 1),
        compiler_params=_params("arbitrary"),
    )(dy, inp, g, resid, *extra, *dep_args)


def _after(dep):
    if dep is None:
        return [], []
    return [dep], [pl.BlockSpec((8, 128), lambda *_: (0, 0))]


def mm_nn(a, w, out_dtype, name, tm=2048, tn=None, dep=None):
    m, k = a.shape
    tm = min(tm, m)
    ns, _, n = w.shape
    tn = n if tn is None else tn
    nj = n // tn
    dep_args, dep_specs = _after(dep)

    def body(a_ref, w_ref, *rest):
        o_ref = rest[-1]
        o_ref[...] = jnp.dot(a_ref[...], w_ref[0], preferred_element_type=F32).astype(out_dtype)

    return pl.pallas_call(
        body, name=name, grid=(ns, nj, m // tm),
        in_specs=[pl.BlockSpec((tm, k), lambda s, j, i: (i, 0)),
                  pl.BlockSpec((1, k, tn), lambda s, j, i: (s, 0, j))] + dep_specs,
        out_specs=pl.BlockSpec((tm, tn), lambda s, j, i: (i, s * nj + j)),
        out_shape=jax.ShapeDtypeStruct((m, ns * n), out_dtype),
        compiler_params=_params("parallel", "parallel", "parallel"),
    )(a, w, *dep_args)


def mm_nt(a, w, out_dtype, name, tm=1024, tn=None, dep=None):
    m = a.shape[0]
    tm = min(tm, m)
    ns, k, n = w.shape
    tn = n if tn is None else tn
    nj = n // tn
    steps = ns * nj
    dep_args, dep_specs = _after(dep)

    def body(a_ref, w_ref, *rest):
        o_ref, acc = rest[-2:]
        r = pl.program_id(1)

        @pl.when(r == 0)
        def _():
            acc[...] = jnp.zeros_like(acc)

        acc[...] += lax.dot_general(a_ref[...], w_ref[0], (((1,), (1,)), ((), ())),
                                    preferred_element_type=F32)

        @pl.when(r == steps - 1)
        def _():
            o_ref[...] = acc[...].astype(out_dtype)

    return pl.pallas_call(
        body, name=name, grid=(m // tm, steps),
        in_specs=[pl.BlockSpec((tm, tn), lambda i, r: (i, r)),
                  pl.BlockSpec((1, k, tn), lambda i, r: (r // nj, 0, r % nj))] + dep_specs,
        out_specs=pl.BlockSpec((tm, k), lambda i, r: (i, 0)),
        out_shape=jax.ShapeDtypeStruct((m, k), out_dtype),
        scratch_shapes=[pltpu.VMEM((tm, k), F32)],
        compiler_params=_params("parallel", "arbitrary"),
    )(a, w, *dep_args)


def mm_tn(a, b, ns, out_dtype, name, tk=1024, tm=2048, dep=None, pick=None):
    m, k = a.shape
    tm = min(tm, m)
    step, offset = (1, None) if pick is None else pick
    n = b.shape[1] // (ns * step)
    steps = m // tm
    dep_args, dep_specs = _after(dep)
    n_pre = 0 if pick is None else 1

    def b_block(s, j, r, *pre):
        return (r, s if pick is None else step * s + pre[0][0])

    def body(*refs):
        a_ref, b_ref = refs[n_pre:n_pre + 2]
        o_ref, acc = refs[-2:]
        r = pl.program_id(2)

        @pl.when(r == 0)
        def _():
            acc[...] = jnp.zeros_like(acc)

        acc[...] += lax.dot_general(a_ref[...], b_ref[...], (((0,), (0,)), ((), ())),
                                    preferred_element_type=F32)

        @pl.when(r == steps - 1)
        def _():
            o_ref[0] = acc[...].astype(out_dtype)

    return pl.pallas_call(
        body, name=name,
        grid_spec=pltpu.PrefetchScalarGridSpec(
            num_scalar_prefetch=n_pre, grid=(ns, k // tk, steps),
            in_specs=[pl.BlockSpec((tm, tk), lambda s, j, r, *pre: (r, j)),
                      pl.BlockSpec((tm, n), b_block)] + dep_specs,
            out_specs=pl.BlockSpec((1, tk, n), lambda s, j, r, *pre: (s, j, 0)),
            scratch_shapes=[pltpu.VMEM((tk, n), F32)]),
        out_shape=jax.ShapeDtypeStruct((ns, k, n), out_dtype),
        compiler_params=_params("parallel", "parallel", "arbitrary"),
    )(*([] if pick is None else [offset]), a, b, *dep_args)


SB_BLK = 128


LOG2E = 1.0 / math.log(2.0)


def _split_dot(v, tri2):
    hi = pltpu.bitcast(pltpu.bitcast(v, jnp.uint32) & jnp.uint32(0xFFFF0000), F32)
    lo = (v - hi).astype(BF16)
    return jnp.dot(jnp.concatenate([hi.astype(BF16), lo], axis=1), tri2, preferred_element_type=F32)


def _sb_scores(z2, lim, dcol, tri_ex, masked):
    sp = jnp.log2(1.0 + jnp.exp2(-jnp.abs(z2)))
    lb = jnp.minimum(z2, 0.0) - sp
    l1m = lb - z2
    mask = None
    if masked:
        mask = dcol < lim
        l1m = jnp.where(mask, l1m, 0.0)
    return mask, lb, l1m, _split_dot(l1m, tri_ex)


def _sb_consts():
    row = lax.broadcasted_iota(jnp.int32, (SB_BLK, SB_BLK), 0)
    col = lax.broadcasted_iota(jnp.int32, (SB_BLK, SB_BLK), 1)
    tri_ex = jnp.where(row > col, 1.0, 0.0).astype(BF16)
    tri_in = jnp.where(row >= col, 1.0, 0.0).astype(BF16)
    return col - row, jnp.concatenate([tri_ex, tri_ex], axis=0), jnp.concatenate([tri_in, tri_in], axis=0)


def sb_fwd(p, n_heads, name, tq=256, nsub=4, dep=None):
    s = p.shape[0]
    h_n = n_heads
    b = SB_BLK
    nqs = tq // b
    tk = nsub * b
    scale = 1.0 / math.sqrt(HEAD_DIM)

    dep_args, dep_specs = _after(dep)

    def body(q_ref, k_ref, v_ref, *rest):
        o_ref, w_ref = rest[-2:]
        qi = pl.program_id(1)
        dcol, tri_ex, _ = _sb_consts()
        qv = [q_ref[qs * b:(qs + 1) * b, :] for qs in range(nqs)]
        n_groups = ((qi + 1) * nqs - 1) // nsub + 1

        def step(it, carry, masked):
            c1s, accs = carry
            g = n_groups - 1 - it
            off = pl.multiple_of(g * tk, tk)
            kg = k_ref[pl.ds(off, tk), :]
            vg = v_ref[pl.ds(off, tk), :]
            new_c1, new_acc = [], []
            for qs in range(nqs):
                qb = qi * nqs + qs
                z2 = lax.dot_general(qv[qs], kg, (((1,), (1,)), ((), ())),
                                     preferred_element_type=F32) * (scale * LOG2E)
                blocks = [_sb_scores(z2[:, j * b:(j + 1) * b], (qb - (g * nsub + j)) * b, dcol, tri_ex, masked)
                          for j in range(nsub)]
                run = c1s[qs]
                ws = [None] * nsub
                for j in reversed(range(nsub)):
                    mask, lb, l1m, ls_loc = blocks[j]
                    wj = jnp.exp2(lb + ls_loc + run)
                    ws[j] = (jnp.where(mask, wj, 0.0) if masked else wj).astype(BF16)
                    run = run + jnp.sum(l1m, axis=1, keepdims=True)
                w = jnp.concatenate(ws, axis=1)
                w_ref[0, g, qs * b:(qs + 1) * b, :] = w
                new_acc.append(accs[qs] + jnp.dot(w, vg, preferred_element_type=F32))
                new_c1.append(run)
            return tuple(new_c1), tuple(new_acc)

        init = (tuple(jnp.zeros((b, 1), F32) for _ in range(nqs)),
                tuple(jnp.zeros((b, HEAD_DIM), F32) for _ in range(nqs)))
        assert nqs == 2 and nsub % 2 == 0
        first = step(0, init, True)
        _, accs = lax.fori_loop(1, n_groups, functools.partial(step, masked=False), first)
        for qs in range(nqs):
            o_ref[qs * b:(qs + 1) * b, :] = accs[qs]

    return pl.pallas_call(
        body, name=name, grid=(h_n, s // tq),
        in_specs=[pl.BlockSpec((tq, HEAD_DIM), lambda h, i: (i, h)),
                  pl.BlockSpec((s, HEAD_DIM), lambda h, i: (0, h_n + h)),
                  pl.BlockSpec((s, HEAD_DIM), lambda h, i: (0, 2 * h_n + h))] + dep_specs,
        out_specs=[pl.BlockSpec((tq, HEAD_DIM), lambda h, i: (i, h)),
                   pl.BlockSpec((1, s // tk, tq, tk), lambda h, i: (h, 0, i, 0))],
        out_shape=[jax.ShapeDtypeStruct((s, h_n * HEAD_DIM), F32),
                   jax.ShapeDtypeStruct((h_n, s // tk, s, tk), BF16)],
        compiler_params=_params("parallel", "arbitrary"),
    )(p, p, p, *dep_args)


def sb_bwd(p, a, wts, da, n_heads, name, tq=256, dep=None):
    s = p.shape[0]
    h_n = n_heads
    nq = s // tq
    b = SB_BLK
    nqs = tq // b
    tk = wts.shape[3]
    nsub = tk // b
    scale = 1.0 / math.sqrt(HEAD_DIM)
    dep_args, dep_specs = _after(dep)

    def body(q_ref, k_ref, v_ref, a_ref, da_ref, w_ref, *rest):
        dq_ref, dk_ref, dv_ref, dk_acc, dv_acc = rest[-5:]
        qi = pl.program_id(1)

        @pl.when(qi == 0)
        def _():
            dk_acc[...] = jnp.zeros_like(dk_acc)
            dv_acc[...] = jnp.zeros_like(dv_acc)

        dcol, _, tri_in = _sb_consts()
        q_all = q_ref[...]
        do_all = da_ref[...]
        qv = [q_ref[qs * b:(qs + 1) * b, :] for qs in range(nqs)]
        dov = [da_ref[qs * b:(qs + 1) * b, :] for qs in range(nqs)]
        tots = [jnp.sum(dov[qs].astype(F32) * a_ref[qs * b:(qs + 1) * b, :], axis=1, keepdims=True)
                for qs in range(nqs)]
        n_groups = ((qi + 1) * nqs - 1) // nsub + 1

        def step(it, carry, masked):
            c2s, dqs = carry
            g = n_groups - 1 - it
            off = pl.multiple_of(g * tk, tk)
            kg = k_ref[pl.ds(off, tk), :]
            vg = v_ref[pl.ds(off, tk), :]
            w_all = w_ref[0, g]
            new_c2, new_dq, dz_rows = [], [], []
            for qs in range(nqs):
                qb = qi * nqs + qs
                z2 = lax.dot_general(qv[qs], kg, (((1,), (1,)), ((), ())),
                                     preferred_element_type=F32) * (-scale * LOG2E)
                dw = lax.dot_general(dov[qs], vg, (((1,), (1,)), ((), ())), preferred_element_type=F32)
                beta = 1.0 / (1.0 + jnp.exp2(z2))
                e = dw * w_all[qs * b:(qs + 1) * b, :].astype(F32)
                run2 = c2s[qs]
                dzs = [None] * nsub
                for j in reversed(range(nsub)):
                    cols = slice(j * b, (j + 1) * b)
                    later = _split_dot(e[:, cols], tri_in) + run2
                    bj = beta[:, cols]
                    dz = (e[:, cols] * (1.0 - bj) - bj * (tots[qs] - later)) * scale
                    if masked:
                        dz = jnp.where(dcol < (qb - (g * nsub + j)) * b, dz, 0.0)
                    dzs[j] = dz.astype(BF16)
                    run2 = run2 + jnp.sum(e[:, cols], axis=1, keepdims=True)
                dzq = jnp.concatenate(dzs, axis=1)
                new_dq.append(dqs[qs] + jnp.dot(dzq, kg, preferred_element_type=F32))
                new_c2.append(run2)
                dz_rows.append(dzq)
            dz_all = jnp.concatenate(dz_rows, axis=0)
            dk_acc[pl.ds(off, tk), :] += lax.dot_general(dz_all, q_all, (((0,), (0,)), ((), ())),
                                                         preferred_element_type=F32)
            dv_acc[pl.ds(off, tk), :] += lax.dot_general(w_all, do_all, (((0,), (0,)), ((), ())),
                                                         preferred_element_type=F32)
            return tuple(new_c2), tuple(new_dq)

        zeros = tuple(jnp.zeros((b, 1), F32) for _ in range(nqs))
        assert nqs == 2 and nsub % 2 == 0
        first = step(0, (zeros, tuple(jnp.zeros((b, HEAD_DIM), F32) for _ in range(nqs))), True)
        _, dqs = lax.fori_loop(1, n_groups, functools.partial(step, masked=False), first)
        for qs in range(nqs):
            dq_ref[qs * b:(qs + 1) * b, :] = dqs[qs].astype(BF16)

        @pl.when(qi == nq - 1)
        def _():
            dk_ref[...] = dk_acc[...].astype(BF16)
            dv_ref[...] = dv_acc[...].astype(BF16)

    blk = pl.BlockSpec((tq, HEAD_DIM), lambda h, i: (i, h))
    full = pl.BlockSpec((s, HEAD_DIM), lambda h, i: (0, h))
    return pl.pallas_call(
        body, name=name, grid=(h_n, nq),
        in_specs=[blk, pl.BlockSpec((s, HEAD_DIM), lambda h, i: (0, h_n + h)),
                  pl.BlockSpec((s, HEAD_DIM), lambda h, i: (0, 2 * h_n + h)), blk, blk,
                  pl.BlockSpec((1, s // tk, tq, tk), lambda h, i: (h, 0, i, 0))] + dep_specs,
        out_specs=[blk, full, full],
        out_shape=[jax.ShapeDtypeStruct((s, h_n * HEAD_DIM), BF16)] * 3,
        scratch_shapes=[pltpu.VMEM((s, HEAD_DIM), F32), pltpu.VMEM((s, HEAD_DIM), F32)],
        compiler_params=_params("parallel", "arbitrary"),
    )(p, p, p, a, da, wts, *dep_args)


def _pool_window(xx, win, r0, rc):
    cur = xx[HALO:HALO + rc]
    ws = _window_sum(xx, win, True)[HALO:HALO + rc]
    t_idx = r0 + lax.broadcasted_iota(jnp.int32, (rc, 1), 0)
    inv = 1.0 / jnp.minimum(win, t_idx + 1).astype(F32)
    return ws * inv - cur, inv


def even_mix_fwd(a, p, pool_w, pool_scale, name, rc=64, dep=None):
    s = p.shape[0]
    ng = len(POOL_WINDOWS)
    cw = pool_w.shape[1]
    n_chunks = s // rc
    dep_args, dep_specs = _after(dep)

    def body(a_ref, u_ref, g_ref, w_ref, sc_ref, *rest):
        y_ref, upad = rest[-2:]
        j = pl.program_id(0)

        @pl.when(j < ng)
        def _():
            def chunk(ci, carry):
                rows = pl.ds(pl.multiple_of(ci * rc, rc), rc)
                y_ref[rows, :] = (a_ref[rows, :] * _silu(g_ref[rows, :].astype(F32))).astype(BF16)
                return carry

            lax.fori_loop(0, n_chunks, chunk, 0)

        for gi, win in enumerate(POOL_WINDOWS):
            @pl.when(j == ng + gi)
            def _(win=win):
                upad[0:HALO, :] = jnp.zeros((HALO, cw), F32)

                def fill(ci, carry):
                    r0 = pl.multiple_of(ci * rc, rc)
                    upad[pl.ds(pl.multiple_of(r0 + HALO, HALO), rc), :] = u_ref[pl.ds(r0, rc), :].astype(F32)
                    return carry

                lax.fori_loop(0, n_chunks, fill, 0)

                def chunk(ci, carry):
                    r0 = pl.multiple_of(ci * rc, rc)
                    rows = pl.ds(r0, rc)
                    pooled, _ = _pool_window(upad[pl.ds(r0, HALO + rc), :], win, r0, rc)
                    t = jnp.dot(pooled.astype(BF16), w_ref[0], preferred_element_type=F32)
                    y_ref[rows, :] = (t * sc_ref[...] * _silu(g_ref[rows, :].astype(F32))).astype(BF16)
                    return carry

                lax.fori_loop(0, n_chunks, chunk, 0)

    grp = lambda j: jnp.maximum(j - ng, 0)
    return pl.pallas_call(
        body, name=name, grid=(2 * ng,),
        in_specs=[pl.BlockSpec((s, cw), lambda j: (0, jnp.minimum(j, ng - 1))),
                  pl.BlockSpec((s, cw), lambda j: (0, 3 * ng + grp(j))),
                  pl.BlockSpec((s, cw), lambda j: (0, 4 * ng + j)),
                  pl.BlockSpec((1, cw, cw), lambda j: (grp(j), 0, 0)),
                  pl.BlockSpec((1, cw), lambda j: (0, grp(j)))] + dep_specs,
        out_specs=pl.BlockSpec((s, cw), lambda j: (0, j)),
        out_shape=jax.ShapeDtypeStruct((s, 2 * ng * cw), BF16),
        scratch_shapes=[pltpu.VMEM((HALO + s, cw), F32)],
        compiler_params=_params("arbitrary"),
    )(a, p, p, pool_w, pool_scale, *dep_args)


def even_mix_bwd(dy, a, p, pool_w, pool_scale, name, rc=64):
    s = p.shape[0]
    ng = len(POOL_WINDOWS)
    cw = pool_w.shape[1]
    n_chunks = s // rc

    def body(dy_ref, a_ref, u_ref, g_ref, w_ref, sc_ref, da_ref, du_ref, dg_ref, dw_ref, dsc_ref,
             upad, rpad, dpl, dw_acc, dsc_acc):
        j = pl.program_id(0)

        @pl.when(j < ng)
        def _():
            def chunk(ci, carry):
                rows = pl.ds(pl.multiple_of(ci * rc, rc), rc)
                dyv = dy_ref[rows, :].astype(F32)
                sg, dsg = _silu_and_grad(g_ref[rows, :].astype(F32))
                da_ref[rows, :] = (dyv * sg).astype(BF16)
                dg_ref[rows, :] = (dyv * a_ref[rows, :] * dsg).astype(BF16)
                return carry

            lax.fori_loop(0, n_chunks, chunk, 0)

        for gi, win in enumerate(POOL_WINDOWS):
            @pl.when(j == ng + gi)
            def _(win=win):
                upad[0:HALO, :] = jnp.zeros((HALO, cw), F32)
                rpad[s:s + HALO, :] = jnp.zeros((HALO, cw), F32)
                dw_acc[...] = jnp.zeros_like(dw_acc)
                dsc_acc[...] = jnp.zeros_like(dsc_acc)

                def fill(ci, carry):
                    r0 = pl.multiple_of(ci * rc, rc)
                    upad[pl.ds(pl.multiple_of(r0 + HALO, HALO), rc), :] = u_ref[pl.ds(r0, rc), :].astype(F32)
                    return carry

                lax.fori_loop(0, n_chunks, fill, 0)

                def chunk(ci, carry):
                    r0 = pl.multiple_of(ci * rc, rc)
                    rows = pl.ds(r0, rc)
                    pooled, inv = _pool_window(upad[pl.ds(r0, HALO + rc), :], win, r0, rc)
                    pb = pooled.astype(BF16)
                    wv = w_ref[0]
                    t = jnp.dot(pb, wv, preferred_element_type=F32)
                    scv = sc_ref[...]
                    dyv = dy_ref[rows, :].astype(F32)
                    sg, dsg = _silu_and_grad(g_ref[rows, :].astype(F32))
                    dpo = dyv * sg
                    dg_ref[rows, :] = (dyv * t * scv * dsg).astype(BF16)
                    dsc_acc[...] += _rowsum8(dpo * t)
                    dtb = (dpo * scv).astype(BF16)
                    dw_acc[...] += lax.dot_general(pb, dtb, (((0,), (0,)), ((), ())),
                                                   preferred_element_type=F32)
                    dpooled = lax.dot_general(dtb, wv, (((1,), (1,)), ((), ())),
                                              preferred_element_type=F32)
                    dpl[rows, :] = dpooled
                    rpad[rows, :] = dpooled * inv
                    return carry

                lax.fori_loop(0, n_chunks, chunk, 0)

                def chunk2(ci, carry):
                    r0 = pl.multiple_of(ci * rc, rc)
                    rows = pl.ds(r0, rc)
                    xx = rpad[pl.ds(r0, rc + HALO), :]
                    fs = _window_sum(xx, win, False)[0:rc]
                    du_ref[rows, :] = (fs - dpl[rows, :]).astype(BF16)
                    return carry

                lax.fori_loop(0, n_chunks, chunk2, 0)
                dw_ref[0] = dw_acc[...]
                dsc_ref[...] = jnp.sum(dsc_acc[...], axis=0, keepdims=True)

    grp = lambda j: jnp.maximum(j - ng, 0)
    att = lambda j: jnp.minimum(j, ng - 1)
    return pl.pallas_call(
        body, name=name, grid=(2 * ng,),
        in_specs=[pl.BlockSpec((s, cw), lambda j: (0, j)),
                  pl.BlockSpec((s, cw), lambda j: (0, att(j))),
                  pl.BlockSpec((s, cw), lambda j: (0, 3 * ng + grp(j))),
                  pl.BlockSpec((s, cw), lambda j: (0, 4 * ng + j)),
                  pl.BlockSpec((1, cw, cw), lambda j: (grp(j), 0, 0)),
                  pl.BlockSpec((1, cw), lambda j: (0, grp(j)))],
        out_specs=[pl.BlockSpec((s, cw), lambda j: (0, att(j))),
                   pl.BlockSpec((s, cw), lambda j: (0, grp(j))),
                   pl.BlockSpec((s, cw), lambda j: (0, j)),
                   pl.BlockSpec((1, cw, cw), lambda j: (grp(j), 0, 0)),
                   pl.BlockSpec((1, cw), lambda j: (0, grp(j)))],
        out_shape=[jax.ShapeDtypeStruct((s, ng * cw), BF16), jax.ShapeDtypeStruct((s, ng * cw), BF16),
                   jax.ShapeDtypeStruct((s, 2 * ng * cw), BF16),
                   jax.ShapeDtypeStruct((ng, cw, cw), F32), jax.ShapeDtypeStruct((1, ng * cw), F32)],
        scratch_shapes=[pltpu.VMEM((HALO + s, cw), F32), pltpu.VMEM((s + HALO, cw), F32),
                        pltpu.VMEM((s, cw), F32), pltpu.VMEM((cw, cw), F32), pltpu.VMEM((8, cw), F32)],
        compiler_params=_params("arbitrary"),
    )(dy, a, p, p, pool_w, pool_scale)


def _halo_before(tm):
    return lambda i: jnp.maximum(i * (tm // HALO) - 1, 0)


def _halo_after(tm, s):
    return lambda i: jnp.minimum((i + 1) * (tm // HALO), s // HALO - 1)


def odd_mix_fwd(p, sconv_w, dconv_w, dconv_b, cnorm_g, cnorm_b, name, tm=128):
    s = p.shape[0]
    cw = sconv_w.shape[1]
    n = s // tm
    lanes = 128
    hb = _halo_before(tm)

    def body(hc_ref, hch_ref, bc_ref, cc_ref, cch_ref, ga_ref, gah_ref, gb_ref, gbh_ref, g1_ref, g2_ref,
             sw_ref, dw_ref, db_ref, gam_ref, bet_ref, y_ref, dc_ref):
        first = pl.program_id(0) == 0
        for l in range(cw // lanes):
            cols = slice(l * lanes, (l + 1) * lanes)
            mh = jnp.where(first, 0.0, cch_ref[:, cols].astype(F32) * hch_ref[:, cols].astype(F32))
            mm = cc_ref[:, cols].astype(F32) * hc_ref[:, cols].astype(F32)
            xx = jnp.concatenate([mh, mm], axis=0)
            tap = _Taps(xx, tm, True)
            cv = jnp.zeros((tm, lanes), F32)
            for k in range(SCONV_K):
                cv = cv + sw_ref[k:k + 1, cols] * tap(SCONV_K - 1 - k)
            c_out = bc_ref[:, cols].astype(F32) * cv
            y_ref[:, cols] = (c_out * _silu(g1_ref[:, cols].astype(F32))).astype(BF16)
            dh = jnp.where(first, 0.0, gah_ref[:, cols].astype(F32) * _sigmoid(gbh_ref[:, cols].astype(F32)))
            dm = ga_ref[:, cols].astype(F32) * _sigmoid(gb_ref[:, cols].astype(F32))
            xx = jnp.concatenate([dh, dm], axis=0)
            tap = _Taps(xx, tm, True)
            acc = jnp.zeros((tm, lanes), F32) + db_ref[:, cols]
            for k in range(CONF_K):
                acc = acc + dw_ref[k:k + 1, cols] * tap(CONF_K - 1 - k)
            dc_ref[:, cols] = acc
        rs = 32
        for r in range(tm // rs):
            rows = slice(r * rs, (r + 1) * rs)
            xv = dc_ref[rows, :]
            mu = jnp.mean(xv, axis=-1, keepdims=True)
            xc = xv - mu
            rstd = lax.rsqrt(jnp.mean(xc * xc, axis=-1, keepdims=True) + EPS)
            ln = xc * rstd * gam_ref[...] + bet_ref[...]
            y_ref[rows, cw:2 * cw] = (_silu(ln) * _silu(g2_ref[rows, :].astype(F32))).astype(BF16)

    main = lambda c: pl.BlockSpec((tm, cw), lambda i: (i, c))
    halo = lambda c: pl.BlockSpec((HALO, cw), lambda i: (hb(i), c))
    vec = lambda r: pl.BlockSpec((r, cw), lambda i: (0, 0))
    return pl.pallas_call(
        body, name=name, grid=(n,),
        in_specs=[main(0), halo(0), main(1), main(2), halo(2), main(3), halo(3), main(4), halo(4),
                  main(5), main(6), vec(SCONV_K), vec(CONF_K), vec(1), vec(1), vec(1)],
        out_specs=[pl.BlockSpec((tm, 2 * cw), lambda i: (i, 0)), pl.BlockSpec((tm, cw), lambda i: (i, 0))],
        out_shape=[jax.ShapeDtypeStruct((s, 2 * cw), BF16), jax.ShapeDtypeStruct((s, cw), F32)],
        compiler_params=_params("parallel"),
    )(p, p, p, p, p, p, p, p, p, p, p, sconv_w, dconv_w, dconv_b, cnorm_g, cnorm_b)


def odd_bwd_ln(dy, p, dc, cnorm_g, cnorm_b, name, tm=256):
    s = p.shape[0]
    cw = dc.shape[1]
    n = s // tm
    rs = 32

    def body(dy_ref, g2_ref, dc_ref, gam_ref, bet_ref, ddc_ref, dg_ref, dgam_ref, dbet_ref, gacc, bacc):
        i = pl.program_id(0)

        @pl.when(i == 0)
        def _():
            gacc[...] = jnp.zeros_like(gacc)
            bacc[...] = jnp.zeros_like(bacc)

        def chunk(ci, carry):
            rows = pl.ds(pl.multiple_of(ci * rs, rs), rs)
            xv = dc_ref[rows, :]
            mu = jnp.mean(xv, axis=-1, keepdims=True)
            xc = xv - mu
            rstd = lax.rsqrt(jnp.mean(xc * xc, axis=-1, keepdims=True) + EPS)
            xh = xc * rstd
            gam = gam_ref[...]
            sl, dsl = _silu_and_grad(xh * gam + bet_ref[...])
            sg, dsg = _silu_and_grad(g2_ref[rows, :].astype(F32))
            dyv = dy_ref[rows, :].astype(F32)
            dg_ref[rows, :] = (dyv * sl * dsg).astype(BF16)
            dln = dyv * sg * dsl
            gacc[...] += _rowsum8(dln * xh)
            bacc[...] += _rowsum8(dln)
            dxh = dln * gam
            ddc_ref[rows, :] = rstd * (dxh - jnp.mean(dxh, axis=-1, keepdims=True)
                                       - xh * jnp.mean(dxh * xh, axis=-1, keepdims=True))
            return carry

        lax.fori_loop(0, tm // rs, chunk, 0)

        @pl.when(i == n - 1)
        def _():
            dgam_ref[...] = jnp.sum(gacc[...], axis=0, keepdims=True)
            dbet_ref[...] = jnp.sum(bacc[...], axis=0, keepdims=True)

    vec = pl.BlockSpec((1, cw), lambda i: (0, 0))
    return pl.pallas_call(
        body, name=name, grid=(n,),
        in_specs=[pl.BlockSpec((tm, cw), lambda i: (i, 1)), pl.BlockSpec((tm, cw), lambda i: (i, 6)),
                  pl.BlockSpec((tm, cw), lambda i: (i, 0)), vec, vec],
        out_specs=[pl.BlockSpec((tm, cw), lambda i: (i, 0)), pl.BlockSpec((tm, cw), lambda i: (i, 0)), vec, vec],
        out_shape=[jax.ShapeDtypeStruct((s, cw), F32), jax.ShapeDtypeStruct((s, cw), BF16),
                   jax.ShapeDtypeStruct((1, cw), F32), jax.ShapeDtypeStruct((1, cw), F32)],
        scratch_shapes=[pltpu.VMEM((8, cw), F32), pltpu.VMEM((8, cw), F32)],
        compiler_params=_params("arbitrary"),
    )(dy, p, dc, cnorm_g, cnorm_b)


def odd_bwd_conv(dy, p, ddc, dg2, sconv_w, dconv_w, name, tm=128):
    s = p.shape[0]
    cw = ddc.shape[1]
    n = s // tm
    lanes = 128
    hb = _halo_before(tm)
    ha = _halo_after(tm, s)

    def body(dy_ref, dya_ref, g1_ref, g1a_ref, bc_ref, bca_ref, hc_ref, hch_ref, cc_ref, cch_ref,
             ddc_ref, ddca_ref, ga_ref, gah_ref, gb_ref, gbh_ref, dg2_ref, sw_ref, dw_ref,
             dp_ref, dsw_ref, ddw_ref, ddb_ref, sw_acc, dw_acc, db_acc):
        i = pl.program_id(0)
        first = i == 0
        last = i == n - 1

        @pl.when(first)
        def _():
            sw_acc[...] = jnp.zeros_like(sw_acc)
            dw_acc[...] = jnp.zeros_like(dw_acc)
            db_acc[...] = jnp.zeros_like(db_acc)

        for l in range(cw // lanes):
            cols = slice(l * lanes, (l + 1) * lanes)
            mh = jnp.where(first, 0.0, cch_ref[:, cols].astype(F32) * hch_ref[:, cols].astype(F32))
            hcv = hc_ref[:, cols].astype(F32)
            ccv = cc_ref[:, cols].astype(F32)
            xx = jnp.concatenate([mh, ccv * hcv], axis=0)
            tap = _Taps(xx, tm, True)
            taps = [tap(SCONV_K - 1 - k) for k in range(SCONV_K)]
            cv = jnp.zeros((tm, lanes), F32)
            for k in range(SCONV_K):
                cv = cv + sw_ref[k:k + 1, cols] * taps[k]
            bcv = bc_ref[:, cols].astype(F32)
            dyv = dy_ref[:, cols].astype(F32)
            sg, dsg = _silu_and_grad(g1_ref[:, cols].astype(F32))
            dco = dyv * sg
            dp_ref[:, 5 * cw + l * lanes:5 * cw + (l + 1) * lanes] = (dyv * bcv * cv * dsg).astype(BF16)
            dp_ref[:, cw + l * lanes:cw + (l + 1) * lanes] = (dco * cv).astype(BF16)
            dcv = dco * bcv
            for k in range(SCONV_K):
                sw_acc[k * 8:(k + 1) * 8, cols] += _rowsum8(dcv * taps[k])
            dcv_a = jnp.where(last, 0.0, dya_ref[:, cols].astype(F32) * _silu(g1a_ref[:, cols].astype(F32))
                              * bca_ref[:, cols].astype(F32))
            xx = jnp.concatenate([dcv, dcv_a], axis=0)
            tap = _Taps(xx, tm, False)
            dm = jnp.zeros((tm, lanes), F32)
            for k in range(SCONV_K):
                dm = dm + sw_ref[k:k + 1, cols] * tap(SCONV_K - 1 - k)
            dp_ref[:, l * lanes:(l + 1) * lanes] = (dm * ccv).astype(BF16)
            dp_ref[:, 2 * cw + l * lanes:2 * cw + (l + 1) * lanes] = (dm * hcv).astype(BF16)
            gav = ga_ref[:, cols].astype(F32)
            sb = _sigmoid(gb_ref[:, cols].astype(F32))
            dh = jnp.where(first, 0.0, gah_ref[:, cols].astype(F32) * _sigmoid(gbh_ref[:, cols].astype(F32)))
            xx = jnp.concatenate([dh, gav * sb], axis=0)
            ddcv = ddc_ref[:, cols]
            db_acc[:, cols] += _rowsum8(ddcv)
            tap = _Taps(xx, tm, True)
            for k in range(CONF_K):
                dw_acc[k * 8:(k + 1) * 8, cols] += _rowsum8(ddcv * tap(CONF_K - 1 - k))
            ddc_a = jnp.where(last, 0.0, ddca_ref[:, cols])
            xx = jnp.concatenate([ddcv, ddc_a], axis=0)
            tap = _Taps(xx, tm, False)
            dgl = jnp.zeros((tm, lanes), F32)
            for k in range(CONF_K):
                dgl = dgl + dw_ref[k:k + 1, cols] * tap(CONF_K - 1 - k)
            dp_ref[:, 3 * cw + l * lanes:3 * cw + (l + 1) * lanes] = (dgl * sb).astype(BF16)
            dp_ref[:, 4 * cw + l * lanes:4 * cw + (l + 1) * lanes] = (dgl * gav * sb * (1.0 - sb)).astype(BF16)
        dp_ref[:, 6 * cw:7 * cw] = dg2_ref[...]

        @pl.when(last)
        def _():
            for k in range(SCONV_K):
                dsw_ref[k:k + 1, :] = jnp.sum(sw_acc[k * 8:(k + 1) * 8, :], axis=0, keepdims=True)
            for k in range(CONF_K):
                ddw_ref[k:k + 1, :] = jnp.sum(dw_acc[k * 8:(k + 1) * 8, :], axis=0, keepdims=True)
            ddb_ref[...] = jnp.sum(db_acc[...], axis=0, keepdims=True)

    def main(c):
        return pl.BlockSpec((tm, cw), lambda i: (i, c))

    def before(c):
        return pl.BlockSpec((HALO, cw), lambda i: (hb(i), c))

    def after(c):
        return pl.BlockSpec((HALO, cw), lambda i: (ha(i), c))

    def vec(r):
        return pl.BlockSpec((r, cw), lambda i: (0, 0))

    return pl.pallas_call(
        body, name=name, grid=(n,),
        in_specs=[main(0), after(0), main(5), after(5), main(1), after(1), main(0), before(0), main(2), before(2),
                  main(0), after(0), main(3), before(3), main(4), before(4), main(0), vec(SCONV_K), vec(CONF_K)],
        out_specs=[pl.BlockSpec((tm, 7 * cw), lambda i: (i, 0)), vec(SCONV_K), vec(CONF_K), vec(1)],
        out_shape=[jax.ShapeDtypeStruct((s, 7 * cw), BF16), jax.ShapeDtypeStruct((SCONV_K, cw), F32),
                   jax.ShapeDtypeStruct((CONF_K, cw), F32), jax.ShapeDtypeStruct((1, cw), F32)],
        scratch_shapes=[pltpu.VMEM((8 * SCONV_K, cw), F32), pltpu.VMEM((8 * CONF_K, cw), F32),
                        pltpu.VMEM((8, cw), F32)],
        compiler_params=_params("arbitrary"),
    )(dy, dy, p, p, p, p, p, p, p, p, ddc, ddc, p, p, p, p, dg2, sconv_w, dconv_w)


_ANY = pl.BlockSpec(memory_space=pl.ANY)


def _place():
    return lax.axis_index("x"), lax.axis_index("y"), lax.axis_index("c")


def all_gather(arrs, name, deps=()):
    n = len(arrs)

    def body(*refs):
        ins, outs = refs[:n], refs[n + len(deps):2 * n + len(deps)]
        send_sems, recv_sems, local_sems = refs[-3:]
        x, y, c = _place()
        me, sibling = (x, y, c), (x, y, 1 - c)
        chips = [(1 - x, y), (x, 1 - y), (1 - x, 1 - y)]

        def copy(a, k, block, to, src=None):
            px, py, pc = block
            dst = outs[a].at[4 * px + 2 * py + pc]
            return pltpu.make_async_remote_copy(
                src_ref=dst if src is None else src, dst_ref=dst,
                send_sem=send_sems.at[7 * a + k], recv_sem=recv_sems.at[7 * a + k],
                device_id=to, device_id_type=MESH)

        mine = [pltpu.make_async_copy(ins[a], outs[a].at[4 * x + 2 * y + c], local_sems.at[a]) for a in range(n)]
        first = []
        for a in range(n):
            first.append(copy(a, 0, me, sibling, src=ins[a]))
            first += [copy(a, 1 + j, me, (*chip, c), src=ins[a]) for j, chip in enumerate(chips)]
        for cp in first + mine:
            cp.start()
        passed = []
        for a in range(n):
            for j, chip in enumerate(chips):
                copy(a, 1 + j, (*chip, c), me).wait_recv()
                cp = copy(a, 4 + j, (*chip, c), sibling)
                cp.start()
                passed.append(cp)
        for a in range(n):
            copy(a, 0, sibling, me).wait_recv()
            for j, chip in enumerate(chips):
                copy(a, 4 + j, (*chip, 1 - c), me).wait_recv()
        for cp in first + passed:
            cp.wait_send()
        for cp in mine:
            cp.wait()

    return pl.pallas_call(
        body, name=name,
        out_shape=[jax.ShapeDtypeStruct((N_DEV,) + a.shape, a.dtype) for a in arrs],
        in_specs=[_ANY] * (n + len(deps)), out_specs=[_ANY] * n,
        scratch_shapes=[pltpu.SemaphoreType.DMA((7 * n,)), pltpu.SemaphoreType.DMA((7 * n,)),
                        pltpu.SemaphoreType.DMA((n,))],
    )(*arrs, *deps)


def in_proj_gathered(xs, g, w_own, extras, name, tm=512):
    s, d = xs.shape
    n = w_own.shape[1]
    arrs = [w_own] + list(extras)
    na = len(arrs)
    tr = 256

    def body(*refs):
        x_ref, g_ref, ins = refs[0], refs[1], refs[2:2 + na]
        h_out, p_ref, outs = refs[2 + na], refs[3 + na], refs[4 + na:4 + 2 * na]
        h_ref, wbuf, obuf, send_sems, recv_sems, load_sem, store_sems, own_sems, h_sem = refs[4 + 2 * na:]
        x, y, c = _place()
        me, sibling = (x, y, c), (x, y, 1 - c)
        chips = [(1 - x, y), (x, 1 - y), (1 - x, 1 - y)]

        def slot(block):
            return 4 * block[0] + 2 * block[1] + block[2]

        def copy(a, k, block, to, src=None):
            dst = outs[a].at[slot(block)]
            return pltpu.make_async_remote_copy(
                src_ref=dst if src is None else src, dst_ref=dst,
                send_sem=send_sems.at[7 * a + k], recv_sem=recv_sems.at[7 * a + k],
                device_id=to, device_id_type=MESH)

        first = []
        for a in range(na):
            first.append(copy(a, 0, me, sibling, src=ins[a]))
            first += [copy(a, 1 + j, me, (*chip, c), src=ins[a]) for j, chip in enumerate(chips)]
        for cp in first:
            cp.start()
        own = pltpu.make_async_copy(wbuf.at[0], outs[0].at[slot(me)], own_sems.at[0])
        mine = [pltpu.make_async_copy(ins[a], outs[a].at[slot(me)], own_sems.at[a]) for a in range(1, na)]
        stores = [None, None]

        def norm(i, carry):
            rows = pl.ds(pl.multiple_of(i * tr, tr), tr)
            xv = x_ref[rows, :]
            r = lax.rsqrt(jnp.mean(xv * xv, axis=-1, keepdims=True) + EPS)
            h_ref[rows, :] = (xv * r * g_ref[...]).astype(BF16)
            return carry

        lax.fori_loop(0, s // tr, norm, 0)
        h_store = pltpu.make_async_copy(h_ref, h_out, h_sem)
        h_store.start()

        def multiply(k, block, w_from):
            b = k % 2
            if k == 2:
                own.wait()
            load = pltpu.make_async_copy(w_from, wbuf.at[b], load_sem)
            load.start()
            if stores[b] is not None:
                stores[b].wait()
            load.wait()
            if k == 0:
                own.start()

            def chunk(i, carry):
                rows = pl.ds(pl.multiple_of(i * tm, tm), tm)
                obuf[b, rows, :] = jnp.dot(h_ref[rows, :], wbuf[b], preferred_element_type=F32).astype(BF16)
                return carry

            lax.fori_loop(0, s // tm, chunk, 0)
            stores[b] = pltpu.make_async_copy(
                obuf.at[b], p_ref.at[:, pl.ds(pl.multiple_of(slot(block) * n, 128), n)], store_sems.at[b])
            stores[b].start()

        def arrived(arrays, j, chip):
            for a in arrays:
                copy(a, 1 + j, (*chip, c), me).wait_recv()
                cp = copy(a, 4 + j, (*chip, c), sibling)
                cp.start()
                passed.append(cp)

        small = range(1, na)
        passed = []
        multiply(0, me, ins[0])
        copy(0, 0, sibling, me).wait_recv()
        multiply(1, sibling, outs[0].at[slot(sibling)])
        for j, chip in enumerate(chips):
            arrived([0], j, chip)
            multiply(2 + 2 * j, (*chip, c), outs[0].at[slot((*chip, c))])
            copy(0, 4 + j, (*chip, 1 - c), me).wait_recv()
            multiply(3 + 2 * j, (*chip, 1 - c), outs[0].at[slot((*chip, 1 - c))])
        for cp in mine:
            cp.start()
        for a in small:
            copy(a, 0, sibling, me).wait_recv()
        for j, chip in enumerate(chips):
            arrived(small, j, chip)
        for j, chip in enumerate(chips):
            for a in small:
                copy(a, 4 + j, (*chip, 1 - c), me).wait_recv()
        for cp in first + passed:
            cp.wait_send()
        for cp in mine + stores + [h_store]:
            cp.wait()

    vmem = pl.BlockSpec(memory_space=pltpu.VMEM)
    outs = pl.pallas_call(
        body, name=name,
        out_shape=[jax.ShapeDtypeStruct((s, d), BF16), jax.ShapeDtypeStruct((s, N_DEV * n), BF16)]
        + [jax.ShapeDtypeStruct((N_DEV,) + a.shape, a.dtype) for a in arrs],
        in_specs=[vmem, vmem] + [_ANY] * na, out_specs=[_ANY] * (2 + na),
        scratch_shapes=[pltpu.VMEM((s, d), BF16), pltpu.VMEM((2, d, n), BF16), pltpu.VMEM((2, s, n), BF16),
                        pltpu.SemaphoreType.DMA((7 * na,)), pltpu.SemaphoreType.DMA((7 * na,)),
                        pltpu.SemaphoreType.DMA, pltpu.SemaphoreType.DMA((2,)), pltpu.SemaphoreType.DMA((na,)),
                        pltpu.SemaphoreType.DMA],
        compiler_params=pltpu.CompilerParams(vmem_limit_bytes=VMEM_LIMIT),
    )(xs, g, *arrs)
    return outs[0], outs[1], outs[2], outs[3:]


_HBM = pl.BlockSpec(memory_space=pltpu.HBM)
_SEM = pl.BlockSpec(memory_space=pltpu.SEMAPHORE)
_DATAFLOW = pltpu.SideEffectType.DATAFLOW_SIDE_EFFECTING


def _peers_per_array(kind):
    return 1 if kind in ("sibling", "halves") else 3


def _split_copies(kind, srcs, lands, send_sems, recv_sems):
    x, y, c = _place()
    per = _peers_per_array(kind)
    out = []
    for a in range(len(lands)):
        if kind == "sibling":
            part = srcs[a] if srcs[a].shape[1] == 1 else srcs[a].at[:, pl.ds(1 - c, 1)]
            peers = [((x, y, 1 - c), part, lands[a], lands[a])]
        elif kind == "halves":
            mine, its = lands[a].at[:, pl.ds(c, 1)], lands[a].at[:, pl.ds(1 - c, 1)]
            peers = [((x, y, 1 - c), mine, mine, its)]
        else:
            peers = []
            for px, py in [(1 - x, y), (x, 1 - y), (1 - x, 1 - y)]:
                if kind == "gather":
                    views = (srcs[a], lands[a].at[4 * x + 2 * y + c], lands[a].at[4 * px + 2 * py + c])
                else:
                    views = (srcs[a].at[2 * px + py], lands[a].at[2 * x + y], lands[a].at[2 * px + py])
                peers.append(((px, py, c),) + views)
        for j, (peer, src, dst, arrives) in enumerate(peers):
            sems = dict(send_sem=send_sems.at[per * a + j], recv_sem=recv_sems.at[per * a + j],
                        device_id=peer, device_id_type=MESH)
            out.append((pltpu.make_async_remote_copy(src_ref=src, dst_ref=dst, **sems),
                        pltpu.make_async_remote_copy(src_ref=src, dst_ref=arrives, **sems)))
    return out


def split_start(kind, srcs, lands, deps, name):
    ns, nl = len(srcs), len(lands)
    n_sems = _peers_per_array(kind) * nl
    held = list(srcs) + list(lands)

    def body(*refs):
        send_sems, recv_sems = refs[len(held) + len(deps)], refs[len(held) + len(deps) + 1]
        for copy, _ in _split_copies(kind, refs[:ns], refs[ns:ns + nl], send_sems, recv_sems):
            copy.start()
        token = refs[-1]
        token[...] = jnp.zeros_like(token)

    outs = pl.pallas_call(
        body, name=name,
        out_shape=(pltpu.SemaphoreType.DMA((n_sems,)), pltpu.SemaphoreType.DMA((n_sems,)),
                   *[pltpu.HBM(a.shape, a.dtype) for a in held], jax.ShapeDtypeStruct((8, 128), F32)),
        in_specs=[_HBM] * len(held) + [_ANY] * len(deps),
        out_specs=(_SEM, _SEM, *([_HBM] * len(held)), pl.BlockSpec(memory_space=pltpu.VMEM)),
        input_output_aliases={i: 2 + i for i in range(len(held))},
        compiler_params=pltpu.CompilerParams(has_side_effects=_DATAFLOW),
    )(*[pltpu.with_memory_space_constraint(a, pltpu.HBM) for a in held], *deps)
    return outs[0], outs[1], list(outs[2:2 + ns]), list(outs[2 + ns:2 + ns + nl]), outs[-1]


def split_wait(kind, send_sems, recv_sems, srcs, lands, afters, name):
    ns, nl = len(srcs), len(lands)
    held = list(srcs) + list(lands)

    def body(*refs):
        for _, arrival in _split_copies(kind, refs[:ns], refs[ns:ns + nl], refs[ns + nl], refs[ns + nl + 1]):
            arrival.wait_send()
            arrival.wait_recv()

    outs = pl.pallas_call(
        body, name=name,
        out_shape=[pltpu.HBM(a.shape, a.dtype) for a in held],
        in_specs=[_HBM] * len(held) + [_SEM, _SEM] + [_ANY] * len(afters),
        out_specs=[_HBM] * len(held),
        input_output_aliases={i: i for i in range(len(held))},
        compiler_params=pltpu.CompilerParams(has_side_effects=_DATAFLOW),
    )(*held, send_sems, recv_sems, *afters)
    return list(outs[:ns]), list(outs[ns:])


def place_block(land, block, dev, name):
    r, c = block.shape
    tr = min(r, 512)

    def body(dev_ref, land_ref, b_ref, o_ref):
        del dev_ref, land_ref
        o_ref[...] = b_ref[...]

    return pl.pallas_call(
        body, name=name,
        grid_spec=pltpu.PrefetchScalarGridSpec(
            num_scalar_prefetch=1, grid=(r // tr,),
            in_specs=[_ANY, pl.BlockSpec((tr, c), lambda i, dev_ref: (i, 0))],
            out_specs=pl.BlockSpec((None, tr, c), lambda i, dev_ref: (dev_ref[0], i, 0))),
        out_shape=jax.ShapeDtypeStruct(land.shape, land.dtype),
        input_output_aliases={1: 0},
        compiler_params=_params("parallel"),
    )(dev, land, block)


def pair_add(own, recv, core, name):
    _, _, r, c = own.shape
    tr = min(r, 512)

    def body(core_ref, own_ref, recv_ref, o_ref):
        del core_ref
        o_ref[...] = (own_ref[...].astype(F32) + recv_ref[...].astype(F32)).astype(BF16)

    return pl.pallas_call(
        body, name=name,
        grid_spec=pltpu.PrefetchScalarGridSpec(
            num_scalar_prefetch=1, grid=(4, r // tr),
            in_specs=[pl.BlockSpec((None, None, tr, c), lambda k, i, core_ref: (k, core_ref[0], i, 0)),
                      pl.BlockSpec((None, None, tr, c), lambda k, i, core_ref: (k, 0, i, 0))],
            out_specs=pl.BlockSpec((None, tr, c), lambda k, i, core_ref: (k, i, 0))),
        out_shape=jax.ShapeDtypeStruct((4, r, c), BF16),
        compiler_params=_params("parallel", "parallel"),
    )(core, own, recv)


def _adamw_math(w, g, m, v):
    m2 = ADAM_B1 * m + (1.0 - ADAM_B1) * g
    v2 = ADAM_B2 * v + (1.0 - ADAM_B2) * (g * g)
    m_hat = m2 / (1.0 - ADAM_B1 ** ADAM_STEP)
    v_hat = v2 / (1.0 - ADAM_B2 ** ADAM_STEP)
    delta = -ADAM_LR * (m_hat / (jnp.sqrt(v_hat) + ADAM_EPS) + ADAM_WD * w)
    return delta, m2, v2


def adamw_big(w, m, v, own, got, chip, name):
    r, c = w.shape
    tr = min(r, 256)

    def body(chip_ref, w_ref, m_ref, v_ref, p0, p1, p2, p3, g_ref, d_ref, m2_ref, v2_ref):
        del chip_ref
        g = ((p0[...].astype(F32) + p1[...].astype(F32)) + p2[...].astype(F32)) + p3[...].astype(F32)
        delta, m2, v2 = _adamw_math(w_ref[...], g, m_ref[...], v_ref[...])
        g_ref[...] = g
        d_ref[...] = delta
        m2_ref[...] = m2
        v2_ref[...] = v2

    row = pl.BlockSpec((tr, c), lambda i, chip_ref: (i, 0))

    def slab(flip):
        return pl.BlockSpec((None, tr, c), lambda i, chip_ref: (chip_ref[0] ^ flip, i, 0))

    return pl.pallas_call(
        body, name=name,
        grid_spec=pltpu.PrefetchScalarGridSpec(
            num_scalar_prefetch=1, grid=(r // tr,),
            in_specs=[row, row, row, slab(0), slab(1), slab(2), slab(3)],
            out_specs=[row] * 4),
        out_shape=[jax.ShapeDtypeStruct((r, c), F32)] * 4,
        compiler_params=_params("parallel"),
    )(chip, w, m, v, own, got, got, got)


def sum_devices(g8, name):
    def body(g_ref, o_ref):
        tot = g_ref[0]
        for k in range(1, N_DEV):
            tot = tot + g_ref[k]
        o_ref[...] = tot

    return pl.pallas_call(body, name=name, out_shape=jax.ShapeDtypeStruct(g8.shape[1:], F32))(g8)


def adamw_small(ws, gs, ms, vs, name):
    n = len(ws)

    def body(*refs):
        w_r, g_r, m_r, v_r = refs[:n], refs[n:2 * n], refs[2 * n:3 * n], refs[3 * n:4 * n]
        d_o, m_o, v_o = refs[4 * n:5 * n], refs[5 * n:6 * n], refs[6 * n:7 * n]
        for k in range(n):
            delta, m2, v2 = _adamw_math(w_r[k][...], g_r[k][...], m_r[k][...], v_r[k][...])
            d_o[k][...] = delta
            m_o[k][...] = m2
            v_o[k][...] = v2

    shapes = [jax.ShapeDtypeStruct(w.shape, F32) for w in ws]
    outs = pl.pallas_call(body, name=name, out_shape=shapes * 3)(*ws, *gs, *ms, *vs)
    return outs[:n], outs[n:2 * n], outs[2 * n:]


def _rows128(a):
    return a.reshape(-1, 128)


def _pad_rows(a, rows):
    return jnp.pad(a, ((0, rows - a.shape[0]), (0, 0)))


def kernel(x, ln_pre_even, w_in_even, pool_w, pool_scale, w_out_even, ln_post_even, ln_pre_odd, w_in_odd, sconv_w, dconv_w, dconv_b, cnorm_g, cnorm_b, w_out_odd, ln_post_odd, loss_target, m_ln_pre_even, m_w_in_even, m_pool_w, m_pool_scale, m_w_out_even, m_ln_post_even, m_ln_pre_odd, m_w_in_odd, m_sconv_w, m_dconv_w, m_dconv_b, m_cnorm_g, m_cnorm_b, m_w_out_odd, m_ln_post_odd, v_ln_pre_even, v_w_in_even, v_pool_w, v_pool_scale, v_w_out_even, v_ln_post_even, v_ln_pre_odd, v_w_in_odd, v_sconv_w, v_dconv_w, v_dconv_b, v_cnorm_g, v_cnorm_b, v_w_out_odd, v_ln_post_odd):
    xs = x[0]
    tgt = loss_target[0]
    s, d = xs.shape
    half = d // 2
    n_heads = half // HEAD_DIM
    ng = len(POOL_WINDOWS)
    cwp = half // ng
    dev = 4 * lax.axis_index("x") + 2 * lax.axis_index("y") + lax.axis_index("c")
    core = lax.axis_index("c").astype(jnp.int32).reshape(1)

    pr = pool_w.shape[2]
    cl = sconv_w.shape[2]
    small_parts = [(_rows128(ln_pre_odd), 8), (sconv_w[0], 8), (dconv_w[0], 32), (dconv_b, 8),
                   (cnorm_g, 8), (cnorm_b, 8), (_rows128(ln_post_odd), 8)]
    small_local = jnp.concatenate([_pad_rows(a, r) for a, r in small_parts], axis=0)
    h0, p0, g_wie, (g_pw, g_small) = in_proj_gathered(
        xs, ln_pre_even, w_in_even[0].astype(BF16), [pool_w[0].reshape(ng * pr, cwp).astype(BF16), small_local],
        "ag_in_proj_even")
    comm = _Exchanges(dev, core, d)
    token = comm.start_weights("out_even", [w_out_even[0].astype(BF16)], [p0])
    sb_dep = comm.start_weights("odd", [w_in_odd[0].astype(BF16), w_out_odd[0].astype(BF16)], [token])
    pool_full = g_pw.reshape(N_DEV, ng, pr, cwp).transpose(1, 0, 2, 3).reshape(ng, cwp, cwp)
    nl = ln_pre_odd.shape[1] // 128

    def chan(lo, rows):
        return g_small[:, lo:lo + rows].transpose(1, 0, 2).reshape(rows, N_DEV * cl)

    ln_pre_odd_f = g_small[:, 0:nl].reshape(1, d)
    sconv_f = chan(8, SCONV_K)
    dconv_f = chan(16, CONF_K)
    dconv_b_f = chan(48, 1)
    cnorm_g_f = chan(56, 1)
    cnorm_b_f = chan(64, 1)
    ln_post_odd_f = g_small[:, 72:72 + nl].reshape(1, d)

    loss_blk, grad_x, small_g = _fwd_bwd(
        xs, tgt, ln_pre_even, h0, p0, g_wie, pool_full, pool_scale, ln_post_even, ln_pre_odd_f,
        sconv_f, dconv_f, dconv_b_f, cnorm_g_f, cnorm_b_f, ln_post_odd_f, comm, sb_dep)
    small_w = [ln_pre_even, pool_scale, ln_post_even, ln_pre_odd, sconv_w[0], dconv_w[0], dconv_b, cnorm_g, cnorm_b, ln_post_odd]
    small_m = [m_ln_pre_even, m_pool_scale, m_ln_post_even, m_ln_pre_odd, m_sconv_w[0], m_dconv_w[0], m_dconv_b, m_cnorm_g, m_cnorm_b, m_ln_post_odd]
    small_v = [v_ln_pre_even, v_pool_scale, v_ln_post_even, v_ln_pre_odd, v_sconv_w[0], v_dconv_w[0], v_dconv_b, v_cnorm_g, v_cnorm_b, v_ln_post_odd]
    big = {"w_in_even": (w_in_even, m_w_in_even, v_w_in_even), "pool_w": (pool_w, m_pool_w, v_pool_w),
           "w_out_even": (w_out_even, m_w_out_even, v_w_out_even), "w_in_odd": (w_in_odd, m_w_in_odd, v_w_in_odd),
           "w_out_odd": (w_out_odd, m_w_out_odd, v_w_out_odd)}
    upd = comm.finish_updates(big, [grad_x])
    upd.update(comm.finish_updates(big, [grad_x]))
    sg, sd, sm, sv, loss = _update_small(small_g, loss_blk, small_w, small_m, small_v, dev, d, cl,
                                         deps=[upd["w_in_odd"][1], upd["w_out_even"][1]])
    upd.update(comm.finish_updates(big, sd))
    (g_wie_o, d_wie, m_wie, v_wie), (g_pw_o, d_pw, m_pw, v_pw) = upd["w_in_even"], upd["pool_w"]
    (g_woe_o, d_woe, m_woe, v_woe), (g_wio_o, d_wio, m_wio, v_wio) = upd["w_out_even"], upd["w_in_odd"]
    g_woo_o, d_woo, m_woo, v_woo = upd["w_out_odd"]

    def order(small, wie, pw, woe, wio, woo):
        return [small[0], wie, pw, small[1], woe, small[2], small[3], wio, small[4], small[5], small[6],
                small[7], small[8], woo, small[9]]

    grads = order(sg, g_wie_o, g_pw_o, g_woe_o, g_wio_o, g_woo_o)
    deltas = order(sd, d_wie, d_pw, d_woe, d_wio, d_woo)
    new_m = order(sm, m_wie, m_pw, m_woe, m_wio, m_woo)
    new_v = order(sv, v_wie, v_pw, v_woe, v_wio, v_woo)
    return (loss, grad_x[None], *grads, *deltas, *new_m, *new_v)


def _fwd_bwd(xs, tgt, ln_pre_even, h0, p0, g_wie, pool_full, pool_scale, ln_post_even, ln_pre_odd_f,
             sconv_f, dconv_f, dconv_b_f, cnorm_g_f, cnorm_b_f, ln_post_odd_f, comm, sb_dep):
    d = xs.shape[1]
    n_heads = d // 2 // HEAD_DIM
    ng, cwp = pool_full.shape[0], pool_full.shape[1]
    a0, sb_wts = sb_fwd(p0, n_heads, "sb_fwd", dep=sb_dep)
    dep = comm.weights_arrived("out_even", after=a0)
    y0 = even_mix_fwd(a0, p0, pool_full, pool_scale, "even_mix_fwd", dep=dep)
    (w_out_e,) = comm.weights("out_even", after=y0)
    w_out_e = w_out_e.reshape(1, d, d)
    o0 = mm_nn(y0, w_out_e, F32, "out_proj_even", tm=1024)
    dep = comm.weights_arrived("odd", after=o0)
    x1, h1 = postnorm_fwd(xs, o0, ln_post_even, ln_pre_odd_f, "post_even", dep=dep)
    g_wio, w_out_o = comm.weights("odd", after=x1)
    w_out_o = w_out_o.reshape(1, d, d)
    p1 = mm_nn(h1, g_wio, BF16, "in_proj_odd")
    y1, dc = odd_mix_fwd(p1, sconv_f, dconv_f, dconv_b_f, cnorm_g_f, cnorm_b_f, "odd_mix_fwd")
    o1 = mm_nn(y1, w_out_o, F32, "out_proj_odd", tm=1024)
    loss_blk, gx2, do1, dg_post_odd = final_fwd_bwd(x1, o1, ln_post_odd_f, tgt, "post_odd_loss")

    dw_out_o = mm_tn(y1, do1, 1, BF16, "dw_out_odd")
    dy1 = mm_nt(do1, w_out_o, BF16, "dy_odd")
    ddc, dg2, dgam, dbet = odd_bwd_ln(dy1, p1, dc, cnorm_g_f, cnorm_b_f, "odd_bwd_ln")
    dp1, dsconv, ddconv, ddconv_b = odd_bwd_conv(dy1, p1, ddc, dg2, sconv_f, dconv_f, "odd_bwd_conv")
    dw_in_o = mm_tn(h1, dp1, N_DEV, BF16, "dw_in_odd")
    dep = comm.reduce_begin({"w_out_odd": dw_out_o.reshape(N_DEV, d // N_DEV, d), "w_in_odd": dw_in_o}, "odd")
    dh1 = mm_nt(dp1, g_wio, F32, "dh_odd", dep=dep)
    dep = comm.reduce_send(after=dh1)
    gx1, dg_pre_odd, do0, dg_post_even = norm_bwd(dh1, x1, ln_pre_odd_f, gx2, "pre_odd_post_even_bwd",
                                                  inp2=o0, g2=ln_post_even, dep=dep)

    dw_out_e = mm_tn(y0, do0, 1, BF16, "dw_out_even")
    dy0 = mm_nt(do0, w_out_e, BF16, "dy_even")
    da0, du0, dg0, dpool, dpool_scale = even_mix_bwd(dy0, a0, p0, pool_full, pool_scale, "even_mix_bwd")
    pr = cwp // N_DEV
    dpool_slabs = dpool.astype(BF16).reshape(ng, N_DEV, pr, cwp).transpose(1, 0, 2, 3).reshape(N_DEV, ng * pr, cwp)
    dep = comm.reduce_begin({"w_out_even": dw_out_e.reshape(N_DEV, d // N_DEV, d), "pool_w": dpool_slabs}, "even_out")
    dq0, dk0, dv0 = sb_bwd(p0, a0, sb_wts, da0, n_heads, "sb_bwd", dep=dep)
    dep = comm.reduce_send(after=dq0)
    dp0 = jnp.concatenate([dq0, dk0, dv0, du0, dg0], axis=1)
    dw_sibling = mm_tn(h0, dp0, N_DEV // 2, BF16, "dw_in_even_sibling", dep=dep, pick=(2, 1 - comm.core))
    dep = comm.reduce_begin({"w_in_even": dw_sibling}, "even_in", sibling_part=True)
    dw_own = mm_tn(h0, dp0, N_DEV // 2, BF16, "dw_in_even_own", dep=dep, pick=(2, comm.core))
    dep = comm.reduce_send(after=dw_own, own_part={"w_in_even": dw_own})
    dh0 = mm_nt(dp0, g_wie, F32, "dh_even", dep=dep)
    dep = None
    grad_x, dg_pre_even = norm_bwd(dh0, xs, ln_pre_even, gx1, "pre_even_bwd", dep=dep)
    small_g = [dg_pre_even, dpool_scale, dg_post_even, dg_pre_odd, dsconv, ddconv, ddconv_b, dgam, dbet, dg_post_odd]
    return loss_blk, grad_x, small_g


class _Exchanges:
    def __init__(self, dev, core, d):
        self.dev = dev.astype(jnp.int32).reshape(1)
        self.core = core
        self.chip = (dev // 2).astype(jnp.int32).reshape(1)
        self.d = d
        self.in_flight = {}
        self.to_sibling = None
        self.pending = []

    def start_weights(self, tag, blocks, afters):
        lands = [lax.empty((N_DEV,) + b.shape, b.dtype) for b in blocks]
        send, recv, srcs, lands, token = split_start("gather", blocks, lands, afters, "ag_start_" + tag)
        self.in_flight[tag] = (send, recv, srcs, lands)
        return token

    def weights_arrived(self, tag, after):
        send, recv, srcs, lands = self.in_flight.pop(tag)
        srcs, lands = split_wait("gather", send, recv, srcs, lands, [after], "ag_wait_" + tag)
        lands = [place_block(l, b, self.dev, "ag_own_%s_%d" % (tag, k)) for k, (l, b) in enumerate(zip(lands, srcs))]
        lands = [l.reshape((4, 2) + l.shape[1:]) for l in lands]
        send, recv, _, lands, token = split_start("halves", [], lands, [], "ag_sibling_start_" + tag)
        self.in_flight[tag] = (send, recv, lands)
        return token

    def weights(self, tag, after):
        send, recv, lands = self.in_flight.pop(tag)
        _, lands = split_wait("halves", send, recv, [], lands, [after], "ag_sibling_wait_" + tag)
        return [l.reshape((N_DEV,) + l.shape[2:]) for l in lands]

    def reduce_begin(self, partials, tag, sibling_part=False):
        names = list(partials)
        arrs = [partials[k].reshape((4, 1 if sibling_part else 2) + partials[k].shape[1:]) for k in names]
        lands = [lax.empty((4, 1) + a.shape[2:], a.dtype) for a in arrs]
        send, recv, srcs, lands, token = split_start("sibling", arrs, lands, [], "rs_sibling_start_" + tag)
        self.to_sibling = (tag, names, send, recv, srcs, lands)
        return token

    def reduce_send(self, after, own_part=None):
        tag, names, send, recv, srcs, lands = self.to_sibling
        srcs, lands = split_wait("sibling", send, recv, srcs, lands, [after], "rs_sibling_wait_" + tag)
        which = self.core
        if own_part is not None:
            srcs = [own_part[k].reshape((4, 1) + own_part[k].shape[1:]) for k in names]
            which = jnp.zeros((1,), jnp.int32)
        sums = [pair_add(o, r, which, "rs_pair_add_" + k) for k, o, r in zip(names, srcs, lands)]
        zones = [lax.empty(a.shape, a.dtype) for a in sums]
        send, recv, srcs, zones, token = split_start("scatter", sums, zones, [], "rs_start_" + tag)
        self.pending.append((tag, names, send, recv, srcs, zones))
        return token

    def finish_updates(self, big, afters):
        tag, names, send, recv, srcs, lands = self.pending.pop(0)
        srcs, lands = split_wait("scatter", send, recv, srcs, lands, afters, "rs_wait_" + tag)
        out = {}
        for name, own, got in zip(names, srcs, lands):
            w, m, v = big[name]
            shp = own.shape[1:]
            outs = adamw_big(w.reshape(shp), m.reshape(shp), v.reshape(shp), own, got, self.chip, "adamw_" + name)
            out[name] = [o.reshape(w.shape) for o in outs]
        return out


def _update_small(small_g, loss_blk, small_w, small_m, small_v, dev, d, cl, deps):
    packed = jnp.concatenate([_rows128(g) for g in small_g] + [loss_blk], axis=0)
    (g8,) = all_gather([packed], "ag_small_grads", deps)
    tot = sum_devices(g8, "sum_small_grads")
    loss = tot[packed.shape[0] - 8, 0]
    full_g = []
    lo = 0
    for g in small_g:
        rows = g.size // 128
        full_g.append(tot[lo:lo + rows].reshape(g.shape))
        lo += rows

    def mine(g, width):
        return lax.dynamic_slice_in_dim(g, dev * width, width, axis=g.ndim - 1)

    fg = full_g
    small_gl = [fg[0], fg[1], fg[2], mine(fg[3], d // N_DEV), mine(fg[4], cl), mine(fg[5], cl), mine(fg[6], cl),
                mine(fg[7], cl), mine(fg[8], cl), mine(fg[9], d // N_DEV)]
    sd, sm, sv = adamw_small(small_w, small_gl, small_m, small_v, "adamw_small")

    def like(k, a):
        return a[None] if k in (4, 5) else a

    sg = [like(k, a) for k, a in enumerate(small_gl)]
    sd = [like(k, a) for k, a in enumerate(sd)]
    sm = [like(k, a) for k, a in enumerate(sm)]
    sv = [like(k, a) for k, a in enumerate(sv)]
    return sg, sd, sm, sv, loss
```

```python
import functools
import math

import jax
import jax.numpy as jnp
from jax import lax
from jax.experimental import pallas as pl
from jax.experimental.pallas import tpu as pltpu

F32 = jnp.float32
BF16 = jnp.bfloat16
EPS = 1e-6
HEAD_DIM = 128
POOL_WINDOWS = (2, 4, 8, 16)
SCONV_K = 3
CONF_K = 31
HALO = 32
N_DEV = 8
VMEM_LIMIT = 56 * 1024 * 1024
MESH = pl.DeviceIdType.MESH

ADAM_LR = 0.001
ADAM_B1 = 0.9
ADAM_B2 = 0.999
ADAM_EPS = 1e-08
ADAM_WD = 0.01
ADAM_STEP = 10


def _params(*sem):
    return pltpu.CompilerParams(dimension_semantics=sem, vmem_limit_bytes=VMEM_LIMIT)


def _sigmoid(v):
    return 1.0 / (1.0 + jnp.exp(-v))


def _silu(v):
    return v * _sigmoid(v)


def _silu_and_grad(v):
    s = _sigmoid(v)
    return v * s, s * (1.0 + v * (1.0 - s))


def _rowsum8(v):
    r, c = v.shape
    return jnp.sum(v.reshape(r // 8, 8, c), axis=0)


SUBLANES = 8


class _Taps:
    def __init__(self, xx, rows, before):
        self.xx, self.rows, self.before, self.rotated = xx, rows, before, {}

    def __call__(self, i):
        r, q = i % SUBLANES, i // SUBLANES
        if r not in self.rotated:
            n = self.xx.shape[0]
            self.rotated[r] = self.xx if r == 0 else pltpu.roll(self.xx, r if self.before else n - r, 0)
        lo = HALO - SUBLANES * q if self.before else SUBLANES * q
        return self.rotated[r][lo:lo + self.rows]


def _window_sum(xx, win, before):
    n = xx.shape[0]
    acc = xx
    k = 1
    while k < win:
        acc = acc + pltpu.roll(acc, k if before else n - k, 0)
        k *= 2
    return acc


def postnorm_fwd(x, o, g, g_next, name, tm=256, dep=None):
    s, d = x.shape
    dep_args, dep_specs = _after(dep)

    def body(x_ref, o_ref, g_ref, gn_ref, *rest):
        y_ref, h_ref = rest[-2:]
        ov = o_ref[...]
        r = lax.rsqrt(jnp.mean(ov * ov, axis=-1, keepdims=True) + EPS)
        y = x_ref[...] + ov * r * g_ref[...]
        y_ref[...] = y
        r2 = lax.rsqrt(jnp.mean(y * y, axis=-1, keepdims=True) + EPS)
        h_ref[...] = (y * r2 * gn_ref[...]).astype(BF16)

    row = pl.BlockSpec((tm, d), lambda i: (i, 0))
    vec = pl.BlockSpec((1, d), lambda i: (0, 0))
    return pl.pallas_call(
        body, name=name, grid=(s // tm,),
        in_specs=[row, row, vec, vec] + dep_specs, out_specs=[row, row],
        out_shape=[jax.ShapeDtypeStruct((s, d), F32), jax.ShapeDtypeStruct((s, d), BF16)],
        compiler_params=_params("parallel"),
    )(x, o, g, g_next, *dep_args)


def final_fwd_bwd(x1, o, g, target, name, tm=256):
    s, d = x1.shape
    n = s // tm

    def body(x_ref, o_ref, g_ref, t_ref, loss_ref, gx_ref, do_ref, dg_ref, lacc, gacc):
        i = pl.program_id(0)

        @pl.when(i == 0)
        def _():
            lacc[...] = jnp.zeros_like(lacc)
            gacc[...] = jnp.zeros_like(gacc)

        ov = o_ref[...]
        gv = g_ref[...]
        r = lax.rsqrt(jnp.mean(ov * ov, axis=-1, keepdims=True) + EPS)
        oh = ov * r
        diff = x_ref[...] + oh * gv - t_ref[...]
        lacc[...] += _rowsum8(diff * diff)
        gx = diff * (1.0 / d)
        gx_ref[...] = gx
        gacc[...] += _rowsum8(gx * oh)
        dn = gx * gv
        do_ref[...] = (r * (dn - oh * jnp.mean(dn * oh, axis=-1, keepdims=True))).astype(BF16)

        @pl.when(i == n - 1)
        def _():
            tot = jnp.sum(jnp.sum(lacc[...], axis=0, keepdims=True), axis=1, keepdims=True)
            loss_ref[...] = jnp.broadcast_to(tot * (0.5 / d), loss_ref.shape)
            dg_ref[...] = jnp.sum(gacc[...], axis=0, keepdims=True)

    row = pl.BlockSpec((tm, d), lambda i: (i, 0))
    vec = pl.BlockSpec((1, d), lambda i: (0, 0))
    return pl.pallas_call(
        body, name=name, grid=(n,),
        in_specs=[row, row, vec, row],
        out_specs=[pl.BlockSpec((8, 128), lambda i: (0, 0)), row, row, vec],
        out_shape=[jax.ShapeDtypeStruct((8, 128), F32), jax.ShapeDtypeStruct((s, d), F32),
                   jax.ShapeDtypeStruct((s, d), BF16), jax.ShapeDtypeStruct((1, d), F32)],
        scratch_shapes=[pltpu.VMEM((8, d), F32), pltpu.VMEM((8, d), F32)],
        compiler_params=_params("arbitrary"),
    )(x1, o, g, target)


def _rms_bwd_rows(dyv, xv, gv):
    r = lax.rsqrt(jnp.mean(xv * xv, axis=-1, keepdims=True) + EPS)
    xh = xv * r
    dn = dyv * gv
    return r * (dn - xh * jnp.mean(dn * xh, axis=-1, keepdims=True)), _rowsum8(dyv * xh)


def norm_bwd(dy, inp, g, resid, name, inp2=None, g2=None, tm=256, dep=None):
    s, d = inp.shape
    n = s // tm
    chain = inp2 is not None

    def body(*refs):
        dy_ref, x_ref, g_ref, r_ref = refs[:4]
        outs = refs[-6:] if chain else refs[-3:]
        i = pl.program_id(0)

        @pl.when(i == 0)
        def _():
            for acc in outs[-2:] if chain else outs[-1:]:
                acc[...] = jnp.zeros_like(acc)

        if chain:
            x2_ref, g2_ref = refs[4:6]
            dx_ref, dg_ref, dx2_ref, dg2_ref, gacc, gacc2 = outs
        else:
            dx_ref, dg_ref, gacc = outs
        dx, dg_rows = _rms_bwd_rows(dy_ref[...].astype(F32), x_ref[...], g_ref[...])
        dx = dx + r_ref[...]
        dx_ref[...] = dx
        gacc[...] += dg_rows
        if chain:
            dx2, dg2_rows = _rms_bwd_rows(dx, x2_ref[...], g2_ref[...])
            dx2_ref[...] = dx2.astype(BF16)
            gacc2[...] += dg2_rows

        @pl.when(i == n - 1)
        def _():
            dg_ref[...] = jnp.sum(gacc[...], axis=0, keepdims=True)
            if chain:
                dg2_ref[...] = jnp.sum(gacc2[...], axis=0, keepdims=True)

    row = pl.BlockSpec((tm, d), lambda i: (i, 0))
    vec = pl.BlockSpec((1, d), lambda i: (0, 0))
    dep_args, dep_specs = _after(dep)
    extra = [inp2, g2] if chain else []
    return pl.pallas_call(
        body, name=name, grid=(n,),
        in_specs=[row, row, vec, row] + ([row, vec] if chain else []) + dep_specs,
        out_specs=[row, vec] * (2 if chain else 1),
        out_shape=[jax.ShapeDtypeStruct((s, d), F32), jax.ShapeDtypeStruct((1, d), F32)]
        + ([jax.ShapeDtypeStruct((s, d), BF16), jax.ShapeDtypeStruct((1, d), F32)] if chain else []),
        scratch_shapes=[pltpu.VMEM((8, d), F32)] * (2 if chain else 1),
        compiler_params=_params("arbitrary"),
    )(dy, inp, g, resid, *extra, *dep_args)


def _after(dep):
    if dep is None:
        return [], []
    return [dep], [pl.BlockSpec((8, 128), lambda *_: (0, 0))]


def mm_nn(a, w, out_dtype, name, tm=2048, tn=None, dep=None):
    m, k = a.shape
    tm = min(tm, m)
    ns, _, n = w.shape
    tn = n if tn is None else tn
    nj = n // tn
    dep_args, dep_specs = _after(dep)

    def body(a_ref, w_ref, *rest):
        o_ref = rest[-1]
        o_ref[...] = jnp.dot(a_ref[...], w_ref[0], preferred_element_type=F32).astype(out_dtype)

    return pl.pallas_call(
        body, name=name, grid=(ns, nj, m // tm),
        in_specs=[pl.BlockSpec((tm, k), lambda s, j, i: (i, 0)),
                  pl.BlockSpec((1, k, tn), lambda s, j, i: (s, 0, j))] + dep_specs,
        out_specs=pl.BlockSpec((tm, tn), lambda s, j, i: (i, s * nj + j)),
        out_shape=jax.ShapeDtypeStruct((m, ns * n), out_dtype),
        compiler_params=_params("parallel", "parallel", "parallel"),
    )(a, w, *dep_args)


def mm_nt(a, w, out_dtype, name, tm=1024, tn=None, dep=None):
    m = a.shape[0]
    tm = min(tm, m)
    ns, k, n = w.shape
    tn = n if tn is None else tn
    nj = n // tn
    steps = ns * nj
    dep_args, dep_specs = _after(dep)

    def body(a_ref, w_ref, *rest):
        o_ref, acc = rest[-2:]
        r = pl.program_id(1)

        @pl.when(r == 0)
        def _():
            acc[...] = jnp.zeros_like(acc)

        acc[...] += lax.dot_general(a_ref[...], w_ref[0], (((1,), (1,)), ((), ())),
                                    preferred_element_type=F32)

        @pl.when(r == steps - 1)
        def _():
            o_ref[...] = acc[...].astype(out_dtype)

    return pl.pallas_call(
        body, name=name, grid=(m // tm, steps),
        in_specs=[pl.BlockSpec((tm, tn), lambda i, r: (i, r)),
                  pl.BlockSpec((1, k, tn), lambda i, r: (r // nj, 0, r % nj))] + dep_specs,
        out_specs=pl.BlockSpec((tm, k), lambda i, r: (i, 0)),
        out_shape=jax.ShapeDtypeStruct((m, k), out_dtype),
        scratch_shapes=[pltpu.VMEM((tm, k), F32)],
        compiler_params=_params("parallel", "arbitrary"),
    )(a, w, *dep_args)


def mm_tn(a, b, ns, out_dtype, name, tk=1024, tm=2048, dep=None, pick=None):
    m, k = a.shape
    tm = min(tm, m)
    step, offset = (1, None) if pick is None else pick
    n = b.shape[1] // (ns * step)
    steps = m // tm
    dep_args, dep_specs = _after(dep)
    n_pre = 0 if pick is None else 1

    def b_block(s, j, r, *pre):
        return (r, s if pick is None else step * s + pre[0][0])

    def body(*refs):
        a_ref, b_ref = refs[n_pre:n_pre + 2]
        o_ref, acc = refs[-2:]
        r = pl.program_id(2)

        @pl.when(r == 0)
        def _():
            acc[...] = jnp.zeros_like(acc)

        acc[...] += lax.dot_general(a_ref[...], b_ref[...], (((0,), (0,)), ((), ())),
                                    preferred_element_type=F32)

        @pl.when(r == steps - 1)
        def _():
            o_ref[0] = acc[...].astype(out_dtype)

    return pl.pallas_call(
        body, name=name,
        grid_spec=pltpu.PrefetchScalarGridSpec(
            num_scalar_prefetch=n_pre, grid=(ns, k // tk, steps),
            in_specs=[pl.BlockSpec((tm, tk), lambda s, j, r, *pre: (r, j)),
                      pl.BlockSpec((tm, n), b_block)] + dep_specs,
            out_specs=pl.BlockSpec((1, tk, n), lambda s, j, r, *pre: (s, j, 0)),
            scratch_shapes=[pltpu.VMEM((tk, n), F32)]),
        out_shape=jax.ShapeDtypeStruct((ns, k, n), out_dtype),
        compiler_params=_params("parallel", "parallel", "arbitrary"),
    )(*([] if pick is None else [offset]), a, b, *dep_args)


SB_BLK = 128


LOG2E = 1.0 / math.log(2.0)


def _split_dot(v, tri2):
    hi = pltpu.bitcast(pltpu.bitcast(v, jnp.uint32) & jnp.uint32(0xFFFF0000), F32)
    lo = (v - hi).astype(BF16)
    return jnp.dot(jnp.concatenate([hi.astype(BF16), lo], axis=1), tri2, preferred_element_type=F32)


def _sb_scores(z2, lim, dcol, tri_ex, masked):
    sp = jnp.log2(1.0 + jnp.exp2(-jnp.abs(z2)))
    lb = jnp.minimum(z2, 0.0) - sp
    l1m = lb - z2
    mask = None
    if masked:
        mask = dcol < lim
        l1m = jnp.where(mask, l1m, 0.0)
    return mask, lb, l1m, _split_dot(l1m, tri_ex)


def _sb_consts():
    row = lax.broadcasted_iota(jnp.int32, (SB_BLK, SB_BLK), 0)
    col = lax.broadcasted_iota(jnp.int32, (SB_BLK, SB_BLK), 1)
    tri_ex = jnp.where(row > col, 1.0, 0.0).astype(BF16)
    tri_in = jnp.where(row >= col, 1.0, 0.0).astype(BF16)
    return col - row, jnp.concatenate([tri_ex, tri_ex], axis=0), jnp.concatenate([tri_in, tri_in], axis=0)


def sb_fwd(p, n_heads, name, tq=256, nsub=4, dep=None):
    s = p.shape[0]
    h_n = n_heads
    b = SB_BLK
    nqs = tq // b
    tk = nsub * b
    scale = 1.0 / math.sqrt(HEAD_DIM)

    dep_args, dep_specs = _after(dep)

    def body(q_ref, k_ref, v_ref, *rest):
        o_ref, w_ref = rest[-2:]
        qi = pl.program_id(1)
        dcol, tri_ex, _ = _sb_consts()
        qv = [q_ref[qs * b:(qs + 1) * b, :] for qs in range(nqs)]
        n_groups = ((qi + 1) * nqs - 1) // nsub + 1

        def step(it, carry, masked):
            c1s, accs = carry
            g = n_groups - 1 - it
            off = pl.multiple_of(g * tk, tk)
            kg = k_ref[pl.ds(off, tk), :]
            vg = v_ref[pl.ds(off, tk), :]
            new_c1, new_acc = [], []
            for qs in range(nqs):
                qb = qi * nqs + qs
                z2 = lax.dot_general(qv[qs], kg, (((1,), (1,)), ((), ())),
                                     preferred_element_type=F32) * (scale * LOG2E)
                blocks = [_sb_scores(z2[:, j * b:(j + 1) * b], (qb - (g * nsub + j)) * b, dcol, tri_ex, masked)
                          for j in range(nsub)]
                run = c1s[qs]
                ws = [None] * nsub
                for j in reversed(range(nsub)):
                    mask, lb, l1m, ls_loc = blocks[j]
                    wj = jnp.exp2(lb + ls_loc + run)
                    ws[j] = (jnp.where(mask, wj, 0.0) if masked else wj).astype(BF16)
                    run = run + jnp.sum(l1m, axis=1, keepdims=True)
                w = jnp.concatenate(ws, axis=1)
                w_ref[0, g, qs * b:(qs + 1) * b, :] = w
                new_acc.append(accs[qs] + jnp.dot(w, vg, preferred_element_type=F32))
                new_c1.append(run)
            return tuple(new_c1), tuple(new_acc)

        init = (tuple(jnp.zeros((b, 1), F32) for _ in range(nqs)),
                tuple(jnp.zeros((b, HEAD_DIM), F32) for _ in range(nqs)))
        assert nqs == 2 and nsub % 2 == 0
        first = step(0, init, True)
        _, accs = lax.fori_loop(1, n_groups, functools.partial(step, masked=False), first)
        for qs in range(nqs):
            o_ref[qs * b:(qs + 1) * b, :] = accs[qs]

    return pl.pallas_call(
        body, name=name, grid=(h_n, s // tq),
        in_specs=[pl.BlockSpec((tq, HEAD_DIM), lambda h, i: (i, h)),
                  pl.BlockSpec((s, HEAD_DIM), lambda h, i: (0, h_n + h)),
                  pl.BlockSpec((s, HEAD_DIM), lambda h, i: (0, 2 * h_n + h))] + dep_specs,
        out_specs=[pl.BlockSpec((tq, HEAD_DIM), lambda h, i: (i, h)),
                   pl.BlockSpec((1, s // tk, tq, tk), lambda h, i: (h, 0, i, 0))],
        out_shape=[jax.ShapeDtypeStruct((s, h_n * HEAD_DIM), F32),
                   jax.ShapeDtypeStruct((h_n, s // tk, s, tk), BF16)],
        compiler_params=_params("parallel", "arbitrary"),
    )(p, p, p, *dep_args)


def sb_bwd(p, a, wts, da, n_heads, name, tq=256, dep=None):
    s = p.shape[0]
    h_n = n_heads
    nq = s // tq
    b = SB_BLK
    nqs = tq // b
    tk = wts.shape[3]
    nsub = tk // b
    scale = 1.0 / math.sqrt(HEAD_DIM)
    dep_args, dep_specs = _after(dep)

    def body(q_ref, k_ref, v_ref, a_ref, da_ref, w_ref, *rest):
        dq_ref, dk_ref, dv_ref, dk_acc, dv_acc = rest[-5:]
        qi = pl.program_id(1)

        @pl.when(qi == 0)
        def _():
            dk_acc[...] = jnp.zeros_like(dk_acc)
            dv_acc[...] = jnp.zeros_like(dv_acc)

        dcol, _, tri_in = _sb_consts()
        q_all = q_ref[...]
        do_all = da_ref[...]
        qv = [q_ref[qs * b:(qs + 1) * b, :] for qs in range(nqs)]
        dov = [da_ref[qs * b:(qs + 1) * b, :] for qs in range(nqs)]
        tots = [jnp.sum(dov[qs].astype(F32) * a_ref[qs * b:(qs + 1) * b, :], axis=1, keepdims=True)
                for qs in range(nqs)]
        n_groups = ((qi + 1) * nqs - 1) // nsub + 1

        def step(it, carry, masked):
            c2s, dqs = carry
            g = n_groups - 1 - it
            off = pl.multiple_of(g * tk, tk)
            kg = k_ref[pl.ds(off, tk), :]
            vg = v_ref[pl.ds(off, tk), :]
            w_all = w_ref[0, g]
            new_c2, new_dq, dz_rows = [], [], []
            for qs in range(nqs):
                qb = qi * nqs + qs
                z2 = lax.dot_general(qv[qs], kg, (((1,), (1,)), ((), ())),
                                     preferred_element_type=F32) * (-scale * LOG2E)
                dw = lax.dot_general(dov[qs], vg, (((1,), (1,)), ((), ())), preferred_element_type=F32)
                beta = 1.0 / (1.0 + jnp.exp2(z2))
                e = dw * w_all[qs * b:(qs + 1) * b, :].astype(F32)
                run2 = c2s[qs]
                dzs = [None] * nsub
                for j in reversed(range(nsub)):
                    cols = slice(j * b, (j + 1) * b)
                    later = _split_dot(e[:, cols], tri_in) + run2
                    bj = beta[:, cols]
                    dz = (e[:, cols] * (1.0 - bj) - bj * (tots[qs] - later)) * scale
                    if masked:
                        dz = jnp.where(dcol < (qb - (g * nsub + j)) * b, dz, 0.0)
                    dzs[j] = dz.astype(BF16)
                    run2 = run2 + jnp.sum(e[:, cols], axis=1, keepdims=True)
                dzq = jnp.concatenate(dzs, axis=1)
                new_dq.append(dqs[qs] + jnp.dot(dzq, kg, preferred_element_type=F32))
                new_c2.append(run2)
                dz_rows.append(dzq)
            dz_all = jnp.concatenate(dz_rows, axis=0)
            dk_acc[pl.ds(off, tk), :] += lax.dot_general(dz_all, q_all, (((0,), (0,)), ((), ())),
                                                         preferred_element_type=F32)
            dv_acc[pl.ds(off, tk), :] += lax.dot_general(w_all, do_all, (((0,), (0,)), ((), ())),
                                                         preferred_element_type=F32)
            return tuple(new_c2), tuple(new_dq)

        zeros = tuple(jnp.zeros((b, 1), F32) for _ in range(nqs))
        assert nqs == 2 and nsub % 2 == 0
        first = step(0, (zeros, tuple(jnp.zeros((b, HEAD_DIM), F32) for _ in range(nqs))), True)
        _, dqs = lax.fori_loop(1, n_groups, functools.partial(step, masked=False), first)
        for qs in range(nqs):
            dq_ref[qs * b:(qs + 1) * b, :] = dqs[qs].astype(BF16)

        @pl.when(qi == nq - 1)
        def _():
            dk_ref[...] = dk_acc[...].astype(BF16)
            dv_ref[...] = dv_acc[...].astype(BF16)

    blk = pl.BlockSpec((tq, HEAD_DIM), lambda h, i: (i, h))
    full = pl.BlockSpec((s, HEAD_DIM), lambda h, i: (0, h))
    return pl.pallas_call(
        body, name=name, grid=(h_n, nq),
        in_specs=[blk, pl.BlockSpec((s, HEAD_DIM), lambda h, i: (0, h_n + h)),
                  pl.BlockSpec((s, HEAD_DIM), lambda h, i: (0, 2 * h_n + h)), blk, blk,
                  pl.BlockSpec((1, s // tk, tq, tk), lambda h, i: (h, 0, i, 0))] + dep_specs,
        out_specs=[blk, full, full],
        out_shape=[jax.ShapeDtypeStruct((s, h_n * HEAD_DIM), BF16)] * 3,
        scratch_shapes=[pltpu.VMEM((s, HEAD_DIM), F32), pltpu.VMEM((s, HEAD_DIM), F32)],
        compiler_params=_params("parallel", "arbitrary"),
    )(p, p, p, a, da, wts, *dep_args)


def _pool_window(xx, win, r0, rc):
    cur = xx[HALO:HALO + rc]
    ws = _window_sum(xx, win, True)[HALO:HALO + rc]
    t_idx = r0 + lax.broadcasted_iota(jnp.int32, (rc, 1), 0)
    inv = 1.0 / jnp.minimum(win, t_idx + 1).astype(F32)
    return ws * inv - cur, inv


def even_mix_fwd(a, p, pool_w, pool_scale, name, rc=64, dep=None):
    s = p.shape[0]
    ng = len(POOL_WINDOWS)
    cw = pool_w.shape[1]
    n_chunks = s // rc
    dep_args, dep_specs = _after(dep)

    def body(a_ref, u_ref, g_ref, w_ref, sc_ref, *rest):
        y_ref, upad = rest[-2:]
        j = pl.program_id(0)

        @pl.when(j < ng)
        def _():
            def chunk(ci, carry):
                rows = pl.ds(pl.multiple_of(ci * rc, rc), rc)
                y_ref[rows, :] = (a_ref[rows, :] * _silu(g_ref[rows, :].astype(F32))).astype(BF16)
                return carry

            lax.fori_loop(0, n_chunks, chunk, 0)

        for gi, win in enumerate(POOL_WINDOWS):
            @pl.when(j == ng + gi)
            def _(win=win):
                upad[0:HALO, :] = jnp.zeros((HALO, cw), F32)

                def fill(ci, carry):
                    r0 = pl.multiple_of(ci * rc, rc)
                    upad[pl.ds(pl.multiple_of(r0 + HALO, HALO), rc), :] = u_ref[pl.ds(r0, rc), :].astype(F32)
                    return carry

                lax.fori_loop(0, n_chunks, fill, 0)

                def chunk(ci, carry):
                    r0 = pl.multiple_of(ci * rc, rc)
                    rows = pl.ds(r0, rc)
                    pooled, _ = _pool_window(upad[pl.ds(r0, HALO + rc), :], win, r0, rc)
                    t = jnp.dot(pooled.astype(BF16), w_ref[0], preferred_element_type=F32)
                    y_ref[rows, :] = (t * sc_ref[...] * _silu(g_ref[rows, :].astype(F32))).astype(BF16)
                    return carry

                lax.fori_loop(0, n_chunks, chunk, 0)

    grp = lambda j: jnp.maximum(j - ng, 0)
    return pl.pallas_call(
        body, name=name, grid=(2 * ng,),
        in_specs=[pl.BlockSpec((s, cw), lambda j: (0, jnp.minimum(j, ng - 1))),
                  pl.BlockSpec((s, cw), lambda j: (0, 3 * ng + grp(j))),
                  pl.BlockSpec((s, cw), lambda j: (0, 4 * ng + j)),
                  pl.BlockSpec((1, cw, cw), lambda j: (grp(j), 0, 0)),
                  pl.BlockSpec((1, cw), lambda j: (0, grp(j)))] + dep_specs,
        out_specs=pl.BlockSpec((s, cw), lambda j: (0, j)),
        out_shape=jax.ShapeDtypeStruct((s, 2 * ng * cw), BF16),
        scratch_shapes=[pltpu.VMEM((HALO + s, cw), F32)],
        compiler_params=_params("arbitrary"),
    )(a, p, p, pool_w, pool_scale, *dep_args)


def even_mix_bwd(dy, a, p, pool_w, pool_scale, name, rc=64):
    s = p.shape[0]
    ng = len(POOL_WINDOWS)
    cw = pool_w.shape[1]
    n_chunks = s // rc

    def body(dy_ref, a_ref, u_ref, g_ref, w_ref, sc_ref, da_ref, du_ref, dg_ref, dw_ref, dsc_ref,
             upad, rpad, dpl, dw_acc, dsc_acc):
        j = pl.program_id(0)

        @pl.when(j < ng)
        def _():
            def chunk(ci, carry):
                rows = pl.ds(pl.multiple_of(ci * rc, rc), rc)
                dyv = dy_ref[rows, :].astype(F32)
                sg, dsg = _silu_and_grad(g_ref[rows, :].astype(F32))
                da_ref[rows, :] = (dyv * sg).astype(BF16)
                dg_ref[rows, :] = (dyv * a_ref[rows, :] * dsg).astype(BF16)
                return carry

            lax.fori_loop(0, n_chunks, chunk, 0)

        for gi, win in enumerate(POOL_WINDOWS):
            @pl.when(j == ng + gi)
            def _(win=win):
                upad[0:HALO, :] = jnp.zeros((HALO, cw), F32)
                rpad[s:s + HALO, :] = jnp.zeros((HALO, cw), F32)
                dw_acc[...] = jnp.zeros_like(dw_acc)
                dsc_acc[...] = jnp.zeros_like(dsc_acc)

                def fill(ci, carry):
                    r0 = pl.multiple_of(ci * rc, rc)
                    upad[pl.ds(pl.multiple_of(r0 + HALO, HALO), rc), :] = u_ref[pl.ds(r0, rc), :].astype(F32)
                    return carry

                lax.fori_loop(0, n_chunks, fill, 0)

                def chunk(ci, carry):
                    r0 = pl.multiple_of(ci * rc, rc)
                    rows = pl.ds(r0, rc)
                    pooled, inv = _pool_window(upad[pl.ds(r0, HALO + rc), :], win, r0, rc)
                    pb = pooled.astype(BF16)
                    wv = w_ref[0]
                    t = jnp.dot(pb, wv, preferred_element_type=F32)
                    scv = sc_ref[...]
                    dyv = dy_ref[rows, :].astype(F32)
                    sg, dsg = _silu_and_grad(g_ref[rows, :].astype(F32))
                    dpo = dyv * sg
                    dg_ref[rows, :] = (dyv * t * scv * dsg).astype(BF16)
                    dsc_acc[...] += _rowsum8(dpo * t)
                    dtb = (dpo * scv).astype(BF16)
                    dw_acc[...] += lax.dot_general(pb, dtb, (((0,), (0,)), ((), ())),
                                                   preferred_element_type=F32)
                    dpooled = lax.dot_general(dtb, wv, (((1,), (1,)), ((), ())),
                                              preferred_element_type=F32)
                    dpl[rows, :] = dpooled
                    rpad[rows, :] = dpooled * inv
                    return carry

                lax.fori_loop(0, n_chunks, chunk, 0)

                def chunk2(ci, carry):
                    r0 = pl.multiple_of(ci * rc, rc)
                    rows = pl.ds(r0, rc)
                    xx = rpad[pl.ds(r0, rc + HALO), :]
                    fs = _window_sum(xx, win, False)[0:rc]
                    du_ref[rows, :] = (fs - dpl[rows, :]).astype(BF16)
                    return carry

                lax.fori_loop(0, n_chunks, chunk2, 0)
                dw_ref[0] = dw_acc[...]
                dsc_ref[...] = jnp.sum(dsc_acc[...], axis=0, keepdims=True)

    grp = lambda j: jnp.maximum(j - ng, 0)
    att = lambda j: jnp.minimum(j, ng - 1)
    return pl.pallas_call(
        body, name=name, grid=(2 * ng,),
        in_specs=[pl.BlockSpec((s, cw), lambda j: (0, j)),
                  pl.BlockSpec((s, cw), lambda j: (0, att(j))),
                  pl.BlockSpec((s, cw), lambda j: (0, 3 * ng + grp(j))),
                  pl.BlockSpec((s, cw), lambda j: (0, 4 * ng + j)),
                  pl.BlockSpec((1, cw, cw), lambda j: (grp(j), 0, 0)),
                  pl.BlockSpec((1, cw), lambda j: (0, grp(j)))],
        out_specs=[pl.BlockSpec((s, cw), lambda j: (0, att(j))),
                   pl.BlockSpec((s, cw), lambda j: (0, grp(j))),
                   pl.BlockSpec((s, cw), lambda j: (0, j)),
                   pl.BlockSpec((1, cw, cw), lambda j: (grp(j), 0, 0)),
                   pl.BlockSpec((1, cw), lambda j: (0, grp(j)))],
        out_shape=[jax.ShapeDtypeStruct((s, ng * cw), BF16), jax.ShapeDtypeStruct((s, ng * cw), BF16),
                   jax.ShapeDtypeStruct((s, 2 * ng * cw), BF16),
                   jax.ShapeDtypeStruct((ng, cw, cw), F32), jax.ShapeDtypeStruct((1, ng * cw), F32)],
        scratch_shapes=[pltpu.VMEM((HALO + s, cw), F32), pltpu.VMEM((s + HALO, cw), F32),
                        pltpu.VMEM((s, cw), F32), pltpu.VMEM((cw, cw), F32), pltpu.VMEM((8, cw), F32)],
        compiler_params=_params("arbitrary"),
    )(dy, a, p, p, pool_w, pool_scale)


def _halo_before(tm):
    return lambda i: jnp.maximum(i * (tm // HALO) - 1, 0)


def _halo_after(tm, s):
    return lambda i: jnp.minimum((i + 1) * (tm // HALO), s // HALO - 1)


def odd_mix_fwd(p, sconv_w, dconv_w, dconv_b, cnorm_g, cnorm_b, name, tm=128):
    s = p.shape[0]
    cw = sconv_w.shape[1]
    n = s // tm
    lanes = 128
    hb = _halo_before(tm)

    def body(hc_ref, hch_ref, bc_ref, cc_ref, cch_ref, ga_ref, gah_ref, gb_ref, gbh_ref, g1_ref, g2_ref,
             sw_ref, dw_ref, db_ref, gam_ref, bet_ref, y_ref, dc_ref):
        first = pl.program_id(0) == 0
        for l in range(cw // lanes):
            cols = slice(l * lanes, (l + 1) * lanes)
            mh = jnp.where(first, 0.0, cch_ref[:, cols].astype(F32) * hch_ref[:, cols].astype(F32))
            mm = cc_ref[:, cols].astype(F32) * hc_ref[:, cols].astype(F32)
            xx = jnp.concatenate([mh, mm], axis=0)
            tap = _Taps(xx, tm, True)
            cv = jnp.zeros((tm, lanes), F32)
            for k in range(SCONV_K):
                cv = cv + sw_ref[k:k + 1, cols] * tap(SCONV_K - 1 - k)
            c_out = bc_ref[:, cols].astype(F32) * cv
            y_ref[:, cols] = (c_out * _silu(g1_ref[:, cols].astype(F32))).astype(BF16)
            dh = jnp.where(first, 0.0, gah_ref[:, cols].astype(F32) * _sigmoid(gbh_ref[:, cols].astype(F32)))
            dm = ga_ref[:, cols].astype(F32) * _sigmoid(gb_ref[:, cols].astype(F32))
            xx = jnp.concatenate([dh, dm], axis=0)
            tap = _Taps(xx, tm, True)
            acc = jnp.zeros((tm, lanes), F32) + db_ref[:, cols]
            for k in range(CONF_K):
                acc = acc + dw_ref[k:k + 1, cols] * tap(CONF_K - 1 - k)
            dc_ref[:, cols] = acc
        rs = 32
        for r in range(tm // rs):
            rows = slice(r * rs, (r + 1) * rs)
            xv = dc_ref[rows, :]
            mu = jnp.mean(xv, axis=-1, keepdims=True)
            xc = xv - mu
            rstd = lax.rsqrt(jnp.mean(xc * xc, axis=-1, keepdims=True) + EPS)
            ln = xc * rstd * gam_ref[...] + bet_ref[...]
            y_ref[rows, cw:2 * cw] = (_silu(ln) * _silu(g2_ref[rows, :].astype(F32))).astype(BF16)

    main = lambda c: pl.BlockSpec((tm, cw), lambda i: (i, c))
    halo = lambda c: pl.BlockSpec((HALO, cw), lambda i: (hb(i), c))
    vec = lambda r: pl.BlockSpec((r, cw), lambda i: (0, 0))
    return pl.pallas_call(
        body, name=name, grid=(n,),
        in_specs=[main(0), halo(0), main(1), main(2), halo(2), main(3), halo(3), main(4), halo(4),
                  main(5), main(6), vec(SCONV_K), vec(CONF_K), vec(1), vec(1), vec(1)],
        out_specs=[pl.BlockSpec((tm, 2 * cw), lambda i: (i, 0)), pl.BlockSpec((tm, cw), lambda i: (i, 0))],
        out_shape=[jax.ShapeDtypeStruct((s, 2 * cw), BF16), jax.ShapeDtypeStruct((s, cw), F32)],
        compiler_params=_params("parallel"),
    )(p, p, p, p, p, p, p, p, p, p, p, sconv_w, dconv_w, dconv_b, cnorm_g, cnorm_b)


def odd_bwd_ln(dy, p, dc, cnorm_g, cnorm_b, name, tm=256):
    s = p.shape[0]
    cw = dc.shape[1]
    n = s // tm
    rs = 32

    def body(dy_ref, g2_ref, dc_ref, gam_ref, bet_ref, ddc_ref, dg_ref, dgam_ref, dbet_ref, gacc, bacc):
        i = pl.program_id(0)

        @pl.when(i == 0)
        def _():
            gacc[...] = jnp.zeros_like(gacc)
            bacc[...] = jnp.zeros_like(bacc)

        def chunk(ci, carry):
            rows = pl.ds(pl.multiple_of(ci * rs, rs), rs)
            xv = dc_ref[rows, :]
            mu = jnp.mean(xv, axis=-1, keepdims=True)
            xc = xv - mu
            rstd = lax.rsqrt(jnp.mean(xc * xc, axis=-1, keepdims=True) + EPS)
            xh = xc * rstd
            gam = gam_ref[...]
            sl, dsl = _silu_and_grad(xh * gam + bet_ref[...])
            sg, dsg = _silu_and_grad(g2_ref[rows, :].astype(F32))
            dyv = dy_ref[rows, :].astype(F32)
            dg_ref[rows, :] = (dyv * sl * dsg).astype(BF16)
            dln = dyv * sg * dsl
            gacc[...] += _rowsum8(dln * xh)
            bacc[...] += _rowsum8(dln)
            dxh = dln * gam
            ddc_ref[rows, :] = rstd * (dxh - jnp.mean(dxh, axis=-1, keepdims=True)
                                       - xh * jnp.mean(dxh * xh, axis=-1, keepdims=True))
            return carry

        lax.fori_loop(0, tm // rs, chunk, 0)

        @pl.when(i == n - 1)
        def _():
            dgam_ref[...] = jnp.sum(gacc[...], axis=0, keepdims=True)
            dbet_ref[...] = jnp.sum(bacc[...], axis=0, keepdims=True)

    vec = pl.BlockSpec((1, cw), lambda i: (0, 0))
    return pl.pallas_call(
        body, name=name, grid=(n,),
        in_specs=[pl.BlockSpec((tm, cw), lambda i: (i, 1)), pl.BlockSpec((tm, cw), lambda i: (i, 6)),
                  pl.BlockSpec((tm, cw), lambda i: (i, 0)), vec, vec],
        out_specs=[pl.BlockSpec((tm, cw), lambda i: (i, 0)), pl.BlockSpec((tm, cw), lambda i: (i, 0)), vec, vec],
        out_shape=[jax.ShapeDtypeStruct((s, cw), F32), jax.ShapeDtypeStruct((s, cw), BF16),
                   jax.ShapeDtypeStruct((1, cw), F32), jax.ShapeDtypeStruct((1, cw), F32)],
        scratch_shapes=[pltpu.VMEM((8, cw), F32), pltpu.VMEM((8, cw), F32)],
        compiler_params=_params("arbitrary"),
    )(dy, p, dc, cnorm_g, cnorm_b)


def odd_bwd_conv(dy, p, ddc, dg2, sconv_w, dconv_w, name, tm=128):
    s = p.shape[0]
    cw = ddc.shape[1]
    n = s // tm
    lanes = 128
    hb = _halo_before(tm)
    ha = _halo_after(tm, s)

    def body(dy_ref, dya_ref, g1_ref, g1a_ref, bc_ref, bca_ref, hc_ref, hch_ref, cc_ref, cch_ref,
             ddc_ref, ddca_ref, ga_ref, gah_ref, gb_ref, gbh_ref, dg2_ref, sw_ref, dw_ref,
             dp_ref, dsw_ref, ddw_ref, ddb_ref, sw_acc, dw_acc, db_acc):
        i = pl.program_id(0)
        first = i == 0
        last = i == n - 1

        @pl.when(first)
        def _():
            sw_acc[...] = jnp.zeros_like(sw_acc)
            dw_acc[...] = jnp.zeros_like(dw_acc)
            db_acc[...] = jnp.zeros_like(db_acc)

        for l in range(cw // lanes):
            cols = slice(l * lanes, (l + 1) * lanes)
            mh = jnp.where(first, 0.0, cch_ref[:, cols].astype(F32) * hch_ref[:, cols].astype(F32))
            hcv = hc_ref[:, cols].astype(F32)
            ccv = cc_ref[:, cols].astype(F32)
            xx = jnp.concatenate([mh, ccv * hcv], axis=0)
            tap = _Taps(xx, tm, True)
            taps = [tap(SCONV_K - 1 - k) for k in range(SCONV_K)]
            cv = jnp.zeros((tm, lanes), F32)
            for k in range(SCONV_K):
                cv = cv + sw_ref[k:k + 1, cols] * taps[k]
            bcv = bc_ref[:, cols].astype(F32)
            dyv = dy_ref[:, cols].astype(F32)
            sg, dsg = _silu_and_grad(g1_ref[:, cols].astype(F32))
            dco = dyv * sg
            dp_ref[:, 5 * cw + l * lanes:5 * cw + (l + 1) * lanes] = (dyv * bcv * cv * dsg).astype(BF16)
            dp_ref[:, cw + l * lanes:cw + (l + 1) * lanes] = (dco * cv).astype(BF16)
            dcv = dco * bcv
            for k in range(SCONV_K):
                sw_acc[k * 8:(k + 1) * 8, cols] += _rowsum8(dcv * taps[k])
            dcv_a = jnp.where(last, 0.0, dya_ref[:, cols].astype(F32) * _silu(g1a_ref[:, cols].astype(F32))
                              * bca_ref[:, cols].astype(F32))
            xx = jnp.concatenate([dcv, dcv_a], axis=0)
            tap = _Taps(xx, tm, False)
            dm = jnp.zeros((tm, lanes), F32)
            for k in range(SCONV_K):
                dm = dm + sw_ref[k:k + 1, cols] * tap(SCONV_K - 1 - k)
            dp_ref[:, l * lanes:(l + 1) * lanes] = (dm * ccv).astype(BF16)
            dp_ref[:, 2 * cw + l * lanes:2 * cw + (l + 1) * lanes] = (dm * hcv).astype(BF16)
            gav = ga_ref[:, cols].astype(F32)
            sb = _sigmoid(gb_ref[:, cols].astype(F32))
            dh = jnp.where(first, 0.0, gah_ref[:, cols].astype(F32) * _sigmoid(gbh_ref[:, cols].astype(F32)))
            xx = jnp.concatenate([dh, gav * sb], axis=0)
            ddcv = ddc_ref[:, cols]
            db_acc[:, cols] += _rowsum8(ddcv)
            tap = _Taps(xx, tm, True)
            for k in range(CONF_K):
                dw_acc[k * 8:(k + 1) * 8, cols] += _rowsum8(ddcv * tap(CONF_K - 1 - k))
            ddc_a = jnp.where(last, 0.0, ddca_ref[:, cols])
            xx = jnp.concatenate([ddcv, ddc_a], axis=0)
            tap = _Taps(xx, tm, False)
            dgl = jnp.zeros((tm, lanes), F32)
            for k in range(CONF_K):
                dgl = dgl + dw_ref[k:k + 1, cols] * tap(CONF_K - 1 - k)
            dp_ref[:, 3 * cw + l * lanes:3 * cw + (l + 1) * lanes] = (dgl * sb).astype(BF16)
            dp_ref[:, 4 * cw + l * lanes:4 * cw + (l + 1) * lanes] = (dgl * gav * sb * (1.0 - sb)).astype(BF16)
        dp_ref[:, 6 * cw:7 * cw] = dg2_ref[...]

        @pl.when(last)
        def _():
            for k in range(SCONV_K):
                dsw_ref[k:k + 1, :] = jnp.sum(sw_acc[k * 8:(k + 1) * 8, :], axis=0, keepdims=True)
            for k in range(CONF_K):
                ddw_ref[k:k + 1, :] = jnp.sum(dw_acc[k * 8:(k + 1) * 8, :], axis=0, keepdims=True)
            ddb_ref[...] = jnp.sum(db_acc[...], axis=0, keepdims=True)

    def main(c):
        return pl.BlockSpec((tm, cw), lambda i: (i, c))

    def before(c):
        return pl.BlockSpec((HALO, cw), lambda i: (hb(i), c))

    def after(c):
        return pl.BlockSpec((HALO, cw), lambda i: (ha(i), c))

    def vec(r):
        return pl.BlockSpec((r, cw), lambda i: (0, 0))

    return pl.pallas_call(
        body, name=name, grid=(n,),
        in_specs=[main(0), after(0), main(5), after(5), main(1), after(1), main(0), before(0), main(2), before(2),
                  main(0), after(0), main(3), before(3), main(4), before(4), main(0), vec(SCONV_K), vec(CONF_K)],
        out_specs=[pl.BlockSpec((tm, 7 * cw), lambda i: (i, 0)), vec(SCONV_K), vec(CONF_K), vec(1)],
        out_shape=[jax.ShapeDtypeStruct((s, 7 * cw), BF16), jax.ShapeDtypeStruct((SCONV_K, cw), F32),
                   jax.ShapeDtypeStruct((CONF_K, cw), F32), jax.ShapeDtypeStruct((1, cw), F32)],
        scratch_shapes=[pltpu.VMEM((8 * SCONV_K, cw), F32), pltpu.VMEM((8 * CONF_K, cw), F32),
                        pltpu.VMEM((8, cw), F32)],
        compiler_params=_params("arbitrary"),
    )(dy, dy, p, p, p, p, p, p, p, p, ddc, ddc, p, p, p, p, dg2, sconv_w, dconv_w)


_ANY = pl.BlockSpec(memory_space=pl.ANY)


def _place():
    return lax.axis_index("x"), lax.axis_index("y"), lax.axis_index("c")


def all_gather(arrs, name, deps=()):
    n = len(arrs)

    def body(*refs):
        ins, outs = refs[:n], refs[n + len(deps):2 * n + len(deps)]
        send_sems, recv_sems, local_sems = refs[-3:]
        x, y, c = _place()
        me, sibling = (x, y, c), (x, y, 1 - c)
        chips = [(1 - x, y), (x, 1 - y), (1 - x, 1 - y)]

        def copy(a, k, block, to, src=None):
            px, py, pc = block
            dst = outs[a].at[4 * px + 2 * py + pc]
            return pltpu.make_async_remote_copy(
                src_ref=dst if src is None else src, dst_ref=dst,
                send_sem=send_sems.at[7 * a + k], recv_sem=recv_sems.at[7 * a + k],
                device_id=to, device_id_type=MESH)

        mine = [pltpu.make_async_copy(ins[a], outs[a].at[4 * x + 2 * y + c], local_sems.at[a]) for a in range(n)]
        first = []
        for a in range(n):
            first.append(copy(a, 0, me, sibling, src=ins[a]))
            first += [copy(a, 1 + j, me, (*chip, c), src=ins[a]) for j, chip in enumerate(chips)]
        for cp in first + mine:
            cp.start()
        passed = []
        for a in range(n):
            for j, chip in enumerate(chips):
                copy(a, 1 + j, (*chip, c), me).wait_recv()
                cp = copy(a, 4 + j, (*chip, c), sibling)
                cp.start()
                passed.append(cp)
        for a in range(n):
            copy(a, 0, sibling, me).wait_recv()
            for j, chip in enumerate(chips):
                copy(a, 4 + j, (*chip, 1 - c), me).wait_recv()
        for cp in first + passed:
            cp.wait_send()
        for cp in mine:
            cp.wait()

    return pl.pallas_call(
        body, name=name,
        out_shape=[jax.ShapeDtypeStruct((N_DEV,) + a.shape, a.dtype) for a in arrs],
        in_specs=[_ANY] * (n + len(deps)), out_specs=[_ANY] * n,
        scratch_shapes=[pltpu.SemaphoreType.DMA((7 * n,)), pltpu.SemaphoreType.DMA((7 * n,)),
                        pltpu.SemaphoreType.DMA((n,))],
    )(*arrs, *deps)


def in_proj_gathered(xs, g, w_own, extras, name, tm=512):
    s, d = xs.shape
    n = w_own.shape[1]
    arrs = [w_own] + list(extras)
    na = len(arrs)
    tr = 256

    def body(*refs):
        x_ref, g_ref, ins = refs[0], refs[1], refs[2:2 + na]
        h_out, p_ref, outs = refs[2 + na], refs[3 + na], refs[4 + na:4 + 2 * na]
        h_ref, wbuf, obuf, send_sems, recv_sems, load_sem, store_sems, own_sems, h_sem = refs[4 + 2 * na:]
        x, y, c = _place()
        me, sibling = (x, y, c), (x, y, 1 - c)
        x_first = c == 0
        near = (jnp.where(x_first, 1 - x, x), jnp.where(x_first, y, 1 - y))
        far = (jnp.where(x_first, x, 1 - x), jnp.where(x_first, 1 - y, y))
        diag = (1 - x, 1 - y)
        k_near, k_far = jnp.where(x_first, 1, 2), jnp.where(x_first, 2, 1)
        f_near, f_far = k_near + 3, k_far + 3

        def slot(block):
            return 4 * block[0] + 2 * block[1] + block[2]

        def copy(a, k, block, to, src=None):
            dst = outs[a].at[slot(block)]
            return pltpu.make_async_remote_copy(
                src_ref=dst if src is None else src, dst_ref=dst,
                send_sem=send_sems.at[7 * a + k], recv_sem=recv_sems.at[7 * a + k],
                device_id=to, device_id_type=MESH)

        first = []
        for a in range(na):
            first += [copy(a, 0, me, sibling, src=ins[a]), copy(a, 1, me, (1 - x, y, c), src=ins[a]),
                      copy(a, 2, me, (x, 1 - y, c), src=ins[a])]
        for cp in first:
            cp.start()
        own = pltpu.make_async_copy(wbuf.at[0], outs[0].at[slot(me)], own_sems.at[0])
        mine = [pltpu.make_async_copy(ins[a], outs[a].at[slot(me)], own_sems.at[a]) for a in range(1, na)]
        stores = [None, None]

        def norm(i, carry):
            rows = pl.ds(pl.multiple_of(i * tr, tr), tr)
            xv = x_ref[rows, :]
            r = lax.rsqrt(jnp.mean(xv * xv, axis=-1, keepdims=True) + EPS)
            h_ref[rows, :] = (xv * r * g_ref[...]).astype(BF16)
            return carry

        lax.fori_loop(0, s // tr, norm, 0)
        h_store = pltpu.make_async_copy(h_ref, h_out, h_sem)
        h_store.start()

        def multiply(k, block, w_from):
            b = k % 2
            if k == 2:
                own.wait()
            load = pltpu.make_async_copy(w_from, wbuf.at[b], load_sem)
            load.start()
            if stores[b] is not None:
                stores[b].wait()
            load.wait()
            if k == 0:
                own.start()

            def chunk(i, carry):
                rows = pl.ds(pl.multiple_of(i * tm, tm), tm)
                obuf[b, rows, :] = jnp.dot(h_ref[rows, :], wbuf[b], preferred_element_type=F32).astype(BF16)
                return carry

            lax.fori_loop(0, s // tm, chunk, 0)
            stores[b] = pltpu.make_async_copy(
                obuf.at[b], p_ref.at[:, pl.ds(pl.multiple_of(slot(block) * n, 128), n)], store_sems.at[b])
            stores[b].start()

        passed = []

        def arrive(a, k, block):
            copy(a, k, block, me).wait_recv()

        def pass_on(a, k, block, to):
            cp = copy(a, k, block, to)
            cp.start()
            passed.append(cp)

        def gather(a, use):
            use(0, me)
            arrive(a, 0, sibling)
            use(1, sibling)
            arrive(a, k_near, (*near, c))
            pass_on(a, 3, (*near, c), (*far, c))
            pass_on(a, f_near, (*near, c), sibling)
            use(2, (*near, c))
            arrive(a, f_far, (*far, 1 - c))
            use(3, (*far, 1 - c))
            arrive(a, k_far, (*far, c))
            pass_on(a, f_far, (*far, c), sibling)
            use(4, (*far, c))
            arrive(a, f_near, (*near, 1 - c))
            use(5, (*near, 1 - c))
            arrive(a, 3, (*diag, c))
            pass_on(a, 6, (*diag, c), sibling)
            use(6, (*diag, c))
            arrive(a, 6, (*diag, 1 - c))
            use(7, (*diag, 1 - c))

        gather(0, lambda k, block: multiply(k, block, ins[0] if k == 0 else outs[0].at[slot(block)]))
        for cp in mine:
            cp.start()
        for a in range(1, na):
            gather(a, lambda k, block: None)
        for cp in first + passed:
            cp.wait_send()
        for cp in mine + stores + [h_store]:
            cp.wait()

    vmem = pl.BlockSpec(memory_space=pltpu.VMEM)
    outs = pl.pallas_call(
        body, name=name,
        out_shape=[jax.ShapeDtypeStruct((s, d), BF16), jax.ShapeDtypeStruct((s, N_DEV * n), BF16)]
        + [jax.ShapeDtypeStruct((N_DEV,) + a.shape, a.dtype) for a in arrs],
        in_specs=[vmem, vmem] + [_ANY] * na, out_specs=[_ANY] * (2 + na),
        scratch_shapes=[pltpu.VMEM((s, d), BF16), pltpu.VMEM((2, d, n), BF16), pltpu.VMEM((2, s, n), BF16),
                        pltpu.SemaphoreType.DMA((7 * na,)), pltpu.SemaphoreType.DMA((7 * na,)),
                        pltpu.SemaphoreType.DMA, pltpu.SemaphoreType.DMA((2,)), pltpu.SemaphoreType.DMA((na,)),
                        pltpu.SemaphoreType.DMA],
        compiler_params=pltpu.CompilerParams(vmem_limit_bytes=VMEM_LIMIT),
    )(xs, g, *arrs)
    return outs[0], outs[1], outs[2], outs[3:]


_HBM = pl.BlockSpec(memory_space=pltpu.HBM)
_SEM = pl.BlockSpec(memory_space=pltpu.SEMAPHORE)
_DATAFLOW = pltpu.SideEffectType.DATAFLOW_SIDE_EFFECTING


def _peers_per_array(kind):
    return 1 if kind in ("sibling", "halves") else 3


def _split_copies(kind, srcs, lands, send_sems, recv_sems):
    x, y, c = _place()
    per = _peers_per_array(kind)
    out = []
    for a in range(len(lands)):
        if kind == "sibling":
            part = srcs[a] if srcs[a].shape[1] == 1 else srcs[a].at[:, pl.ds(1 - c, 1)]
            peers = [((x, y, 1 - c), part, lands[a], lands[a])]
        elif kind == "halves":
            mine, its = lands[a].at[:, pl.ds(c, 1)], lands[a].at[:, pl.ds(1 - c, 1)]
            peers = [((x, y, 1 - c), mine, mine, its)]
        else:
            peers = []
            for px, py in [(1 - x, y), (x, 1 - y), (1 - x, 1 - y)]:
                if kind == "gather":
                    views = (srcs[a], lands[a].at[4 * x + 2 * y + c], lands[a].at[4 * px + 2 * py + c])
                else:
                    views = (srcs[a].at[2 * px + py], lands[a].at[2 * x + y], lands[a].at[2 * px + py])
                peers.append(((px, py, c),) + views)
        for j, (peer, src, dst, arrives) in enumerate(peers):
            sems = dict(send_sem=send_sems.at[per * a + j], recv_sem=recv_sems.at[per * a + j],
                        device_id=peer, device_id_type=MESH)
            out.append((pltpu.make_async_remote_copy(src_ref=src, dst_ref=dst, **sems),
                        pltpu.make_async_remote_copy(src_ref=src, dst_ref=arrives, **sems)))
    return out


def split_start(kind, srcs, lands, deps, name):
    ns, nl = len(srcs), len(lands)
    n_sems = _peers_per_array(kind) * nl
    held = list(srcs) + list(lands)

    def body(*refs):
        send_sems, recv_sems = refs[len(held) + len(deps)], refs[len(held) + len(deps) + 1]
        for copy, _ in _split_copies(kind, refs[:ns], refs[ns:ns + nl], send_sems, recv_sems):
            copy.start()
        token = refs[-1]
        token[...] = jnp.zeros_like(token)

    outs = pl.pallas_call(
        body, name=name,
        out_shape=(pltpu.SemaphoreType.DMA((n_sems,)), pltpu.SemaphoreType.DMA((n_sems,)),
                   *[pltpu.HBM(a.shape, a.dtype) for a in held], jax.ShapeDtypeStruct((8, 128), F32)),
        in_specs=[_HBM] * len(held) + [_ANY] * len(deps),
        out_specs=(_SEM, _SEM, *([_HBM] * len(held)), pl.BlockSpec(memory_space=pltpu.VMEM)),
        input_output_aliases={i: 2 + i for i in range(len(held))},
        compiler_params=pltpu.CompilerParams(has_side_effects=_DATAFLOW),
    )(*[pltpu.with_memory_space_constraint(a, pltpu.HBM) for a in held], *deps)
    return outs[0], outs[1], list(outs[2:2 + ns]), list(outs[2 + ns:2 + ns + nl]), outs[-1]


def split_wait(kind, send_sems, recv_sems, srcs, lands, afters, name):
    ns, nl = len(srcs), len(lands)
    held = list(srcs) + list(lands)

    def body(*refs):
        for _, arrival in _split_copies(kind, refs[:ns], refs[ns:ns + nl], refs[ns + nl], refs[ns + nl + 1]):
            arrival.wait_send()
            arrival.wait_recv()

    outs = pl.pallas_call(
        body, name=name,
        out_shape=[pltpu.HBM(a.shape, a.dtype) for a in held],
        in_specs=[_HBM] * len(held) + [_SEM, _SEM] + [_ANY] * len(afters),
        out_specs=[_HBM] * len(held),
        input_output_aliases={i: i for i in range(len(held))},
        compiler_params=pltpu.CompilerParams(has_side_effects=_DATAFLOW),
    )(*held, send_sems, recv_sems, *afters)
    return list(outs[:ns]), list(outs[ns:])


def place_block(land, block, dev, name):
    r, c = block.shape
    tr = min(r, 512)

    def body(dev_ref, land_ref, b_ref, o_ref):
        del dev_ref, land_ref
        o_ref[...] = b_ref[...]

    return pl.pallas_call(
        body, name=name,
        grid_spec=pltpu.PrefetchScalarGridSpec(
            num_scalar_prefetch=1, grid=(r // tr,),
            in_specs=[_ANY, pl.BlockSpec((tr, c), lambda i, dev_ref: (i, 0))],
            out_specs=pl.BlockSpec((None, tr, c), lambda i, dev_ref: (dev_ref[0], i, 0))),
        out_shape=jax.ShapeDtypeStruct(land.shape, land.dtype),
        input_output_aliases={1: 0},
        compiler_params=_params("parallel"),
    )(dev, land, block)


def pair_add(own, recv, core, name):
    _, _, r, c = own.shape
    tr = min(r, 512)

    def body(core_ref, own_ref, recv_ref, o_ref):
        del core_ref
        o_ref[...] = (own_ref[...].astype(F32) + recv_ref[...].astype(F32)).astype(BF16)

    return pl.pallas_call(
        body, name=name,
        grid_spec=pltpu.PrefetchScalarGridSpec(
            num_scalar_prefetch=1, grid=(4, r // tr),
            in_specs=[pl.BlockSpec((None, None, tr, c), lambda k, i, core_ref: (k, core_ref[0], i, 0)),
                      pl.BlockSpec((None, None, tr, c), lambda k, i, core_ref: (k, 0, i, 0))],
            out_specs=pl.BlockSpec((None, tr, c), lambda k, i, core_ref: (k, i, 0))),
        out_shape=jax.ShapeDtypeStruct((4, r, c), BF16),
        compiler_params=_params("parallel", "parallel"),
    )(core, own, recv)


def _adamw_math(w, g, m, v):
    m2 = ADAM_B1 * m + (1.0 - ADAM_B1) * g
    v2 = ADAM_B2 * v + (1.0 - ADAM_B2) * (g * g)
    m_hat = m2 / (1.0 - ADAM_B1 ** ADAM_STEP)
    v_hat = v2 / (1.0 - ADAM_B2 ** ADAM_STEP)
    delta = -ADAM_LR * (m_hat / (jnp.sqrt(v_hat) + ADAM_EPS) + ADAM_WD * w)
    return delta, m2, v2


def adamw_big(w, m, v, own, got, chip, name):
    r, c = w.shape
    tr = min(r, 256)

    def body(chip_ref, w_ref, m_ref, v_ref, p0, p1, p2, p3, g_ref, d_ref, m2_ref, v2_ref):
        del chip_ref
        g = ((p0[...].astype(F32) + p1[...].astype(F32)) + p2[...].astype(F32)) + p3[...].astype(F32)
        delta, m2, v2 = _adamw_math(w_ref[...], g, m_ref[...], v_ref[...])
        g_ref[...] = g
        d_ref[...] = delta
        m2_ref[...] = m2
        v2_ref[...] = v2

    row = pl.BlockSpec((tr, c), lambda i, chip_ref: (i, 0))

    def slab(flip):
        return pl.BlockSpec((None, tr, c), lambda i, chip_ref: (chip_ref[0] ^ flip, i, 0))

    return pl.pallas_call(
        body, name=name,
        grid_spec=pltpu.PrefetchScalarGridSpec(
            num_scalar_prefetch=1, grid=(r // tr,),
            in_specs=[row, row, row, slab(0), slab(1), slab(2), slab(3)],
            out_specs=[row] * 4),
        out_shape=[jax.ShapeDtypeStruct((r, c), F32)] * 4,
        compiler_params=_params("parallel"),
    )(chip, w, m, v, own, got, got, got)


def sum_devices(g8, name):
    def body(g_ref, o_ref):
        tot = g_ref[0]
        for k in range(1, N_DEV):
            tot = tot + g_ref[k]
        o_ref[...] = tot

    return pl.pallas_call(body, name=name, out_shape=jax.ShapeDtypeStruct(g8.shape[1:], F32))(g8)


def adamw_small(ws, gs, ms, vs, name):
    n = len(ws)

    def body(*refs):
        w_r, g_r, m_r, v_r = refs[:n], refs[n:2 * n], refs[2 * n:3 * n], refs[3 * n:4 * n]
        d_o, m_o, v_o = refs[4 * n:5 * n], refs[5 * n:6 * n], refs[6 * n:7 * n]
        for k in range(n):
            delta, m2, v2 = _adamw_math(w_r[k][...], g_r[k][...], m_r[k][...], v_r[k][...])
            d_o[k][...] = delta
            m_o[k][...] = m2
            v_o[k][...] = v2

    shapes = [jax.ShapeDtypeStruct(w.shape, F32) for w in ws]
    outs = pl.pallas_call(body, name=name, out_shape=shapes * 3)(*ws, *gs, *ms, *vs)
    return outs[:n], outs[n:2 * n], outs[2 * n:]


def _rows128(a):
    return a.reshape(-1, 128)


def _pad_rows(a, rows):
    return jnp.pad(a, ((0, rows - a.shape[0]), (0, 0)))


def kernel(x, ln_pre_even, w_in_even, pool_w, pool_scale, w_out_even, ln_post_even, ln_pre_odd, w_in_odd, sconv_w, dconv_w, dconv_b, cnorm_g, cnorm_b, w_out_odd, ln_post_odd, loss_target, m_ln_pre_even, m_w_in_even, m_pool_w, m_pool_scale, m_w_out_even, m_ln_post_even, m_ln_pre_odd, m_w_in_odd, m_sconv_w, m_dconv_w, m_dconv_b, m_cnorm_g, m_cnorm_b, m_w_out_odd, m_ln_post_odd, v_ln_pre_even, v_w_in_even, v_pool_w, v_pool_scale, v_w_out_even, v_ln_post_even, v_ln_pre_odd, v_w_in_odd, v_sconv_w, v_dconv_w, v_dconv_b, v_cnorm_g, v_cnorm_b, v_w_out_odd, v_ln_post_odd):
    xs = x[0]
    tgt = loss_target[0]
    s, d = xs.shape
    half = d // 2
    n_heads = half // HEAD_DIM
    ng = len(POOL_WINDOWS)
    cwp = half // ng
    dev = 4 * lax.axis_index("x") + 2 * lax.axis_index("y") + lax.axis_index("c")
    core = lax.axis_index("c").astype(jnp.int32).reshape(1)

    pr = pool_w.shape[2]
    cl = sconv_w.shape[2]
    small_parts = [(_rows128(ln_pre_odd), 8), (sconv_w[0], 8), (dconv_w[0], 32), (dconv_b, 8),
                   (cnorm_g, 8), (cnorm_b, 8), (_rows128(ln_post_odd), 8)]
    small_local = jnp.concatenate([_pad_rows(a, r) for a, r in small_parts], axis=0)
    h0, p0, g_wie, (g_pw, g_small) = in_proj_gathered(
        xs, ln_pre_even, w_in_even[0].astype(BF16), [pool_w[0].reshape(ng * pr, cwp).astype(BF16), small_local],
        "ag_in_proj_even")
    comm = _Exchanges(dev, core, d)
    token = comm.start_weights("out_even", [w_out_even[0].astype(BF16)], [p0])
    sb_dep = comm.start_weights("odd", [w_in_odd[0].astype(BF16), w_out_odd[0].astype(BF16)], [token])
    pool_full = g_pw.reshape(N_DEV, ng, pr, cwp).transpose(1, 0, 2, 3).reshape(ng, cwp, cwp)
    nl = ln_pre_odd.shape[1] // 128

    def chan(lo, rows):
        return g_small[:, lo:lo + rows].transpose(1, 0, 2).reshape(rows, N_DEV * cl)

    ln_pre_odd_f = g_small[:, 0:nl].reshape(1, d)
    sconv_f = chan(8, SCONV_K)
    dconv_f = chan(16, CONF_K)
    dconv_b_f = chan(48, 1)
    cnorm_g_f = chan(56, 1)
    cnorm_b_f = chan(64, 1)
    ln_post_odd_f = g_small[:, 72:72 + nl].reshape(1, d)

    loss_blk, grad_x, small_g = _fwd_bwd(
        xs, tgt, ln_pre_even, h0, p0, g_wie, pool_full, pool_scale, ln_post_even, ln_pre_odd_f,
        sconv_f, dconv_f, dconv_b_f, cnorm_g_f, cnorm_b_f, ln_post_odd_f, comm, sb_dep)
    small_w = [ln_pre_even, pool_scale, ln_post_even, ln_pre_odd, sconv_w[0], dconv_w[0], dconv_b, cnorm_g, cnorm_b, ln_post_odd]
    small_m = [m_ln_pre_even, m_pool_scale, m_ln_post_even, m_ln_pre_odd, m_sconv_w[0], m_dconv_w[0], m_dconv_b, m_cnorm_g, m_cnorm_b, m_ln_post_odd]
    small_v = [v_ln_pre_even, v_pool_scale, v_ln_post_even, v_ln_pre_odd, v_sconv_w[0], v_dconv_w[0], v_dconv_b, v_cnorm_g, v_cnorm_b, v_ln_post_odd]
    big = {"w_in_even": (w_in_even, m_w_in_even, v_w_in_even), "pool_w": (pool_w, m_pool_w, v_pool_w),
           "w_out_even": (w_out_even, m_w_out_even, v_w_out_even), "w_in_odd": (w_in_odd, m_w_in_odd, v_w_in_odd),
           "w_out_odd": (w_out_odd, m_w_out_odd, v_w_out_odd)}
    upd = comm.finish_updates(big, [grad_x])
    upd.update(comm.finish_updates(big, [grad_x]))
    sg, sd, sm, sv, loss = _update_small(small_g, loss_blk, small_w, small_m, small_v, dev, d, cl,
                                         deps=[upd["w_in_odd"][1], upd["w_out_even"][1]])
    upd.update(comm.finish_updates(big, sd))
    (g_wie_o, d_wie, m_wie, v_wie), (g_pw_o, d_pw, m_pw, v_pw) = upd["w_in_even"], upd["pool_w"]
    (g_woe_o, d_woe, m_woe, v_woe), (g_wio_o, d_wio, m_wio, v_wio) = upd["w_out_even"], upd["w_in_odd"]
    g_woo_o, d_woo, m_woo, v_woo = upd["w_out_odd"]

    def order(small, wie, pw, woe, wio, woo):
        return [small[0], wie, pw, small[1], woe, small[2], small[3], wio, small[4], small[5], small[6],
                small[7], small[8], woo, small[9]]

    grads = order(sg, g_wie_o, g_pw_o, g_woe_o, g_wio_o, g_woo_o)
    deltas = order(sd, d_wie, d_pw, d_woe, d_wio, d_woo)
    new_m = order(sm, m_wie, m_pw, m_woe, m_wio, m_woo)
    new_v = order(sv, v_wie, v_pw, v_woe, v_wio, v_woo)
    return (loss, grad_x[None], *grads, *deltas, *new_m, *new_v)


def _fwd_bwd(xs, tgt, ln_pre_even, h0, p0, g_wie, pool_full, pool_scale, ln_post_even, ln_pre_odd_f,
             sconv_f, dconv_f, dconv_b_f, cnorm_g_f, cnorm_b_f, ln_post_odd_f, comm, sb_dep):
    d = xs.shape[1]
    n_heads = d // 2 // HEAD_DIM
    ng, cwp = pool_full.shape[0], pool_full.shape[1]
    a0, sb_wts = sb_fwd(p0, n_heads, "sb_fwd", dep=sb_dep)
    dep = comm.weights_arrived("out_even", after=a0)
    y0 = even_mix_fwd(a0, p0, pool_full, pool_scale, "even_mix_fwd", dep=dep)
    (w_out_e,) = comm.weights("out_even", after=y0)
    w_out_e = w_out_e.reshape(1, d, d)
    o0 = mm_nn(y0, w_out_e, F32, "out_proj_even", tm=1024)
    dep = comm.weights_arrived("odd", after=o0)
    x1, h1 = postnorm_fwd(xs, o0, ln_post_even, ln_pre_odd_f, "post_even", dep=dep)
    g_wio, w_out_o = comm.weights("odd", after=x1)
    w_out_o = w_out_o.reshape(1, d, d)
    p1 = mm_nn(h1, g_wio, BF16, "in_proj_odd")
    y1, dc = odd_mix_fwd(p1, sconv_f, dconv_f, dconv_b_f, cnorm_g_f, cnorm_b_f, "odd_mix_fwd")
    o1 = mm_nn(y1, w_out_o, F32, "out_proj_odd", tm=1024)
    loss_blk, gx2, do1, dg_post_odd = final_fwd_bwd(x1, o1, ln_post_odd_f, tgt, "post_odd_loss")

    dw_out_o = mm_tn(y1, do1, 1, BF16, "dw_out_odd")
    dy1 = mm_nt(do1, w_out_o, BF16, "dy_odd")
    ddc, dg2, dgam, dbet = odd_bwd_ln(dy1, p1, dc, cnorm_g_f, cnorm_b_f, "odd_bwd_ln")
    dp1, dsconv, ddconv, ddconv_b = odd_bwd_conv(dy1, p1, ddc, dg2, sconv_f, dconv_f, "odd_bwd_conv")
    dw_in_o = mm_tn(h1, dp1, N_DEV, BF16, "dw_in_odd")
    dep = comm.reduce_begin({"w_out_odd": dw_out_o.reshape(N_DEV, d // N_DEV, d), "w_in_odd": dw_in_o}, "odd")
    dh1 = mm_nt(dp1, g_wio, F32, "dh_odd", dep=dep)
    dep = comm.reduce_send(after=dh1)
    gx1, dg_pre_odd, do0, dg_post_even = norm_bwd(dh1, x1, ln_pre_odd_f, gx2, "pre_odd_post_even_bwd",
                                                  inp2=o0, g2=ln_post_even, dep=dep)

    dw_out_e = mm_tn(y0, do0, 1, BF16, "dw_out_even")
    dy0 = mm_nt(do0, w_out_e, BF16, "dy_even")
    da0, du0, dg0, dpool, dpool_scale = even_mix_bwd(dy0, a0, p0, pool_full, pool_scale, "even_mix_bwd")
    pr = cwp // N_DEV
    dpool_slabs = dpool.astype(BF16).reshape(ng, N_DEV, pr, cwp).transpose(1, 0, 2, 3).reshape(N_DEV, ng * pr, cwp)
    dep = comm.reduce_begin({"w_out_even": dw_out_e.reshape(N_DEV, d // N_DEV, d), "pool_w": dpool_slabs}, "even_out")
    dq0, dk0, dv0 = sb_bwd(p0, a0, sb_wts, da0, n_heads, "sb_bwd", dep=dep)
    dep = comm.reduce_send(after=dq0)
    dp0 = jnp.concatenate([dq0, dk0, dv0, du0, dg0], axis=1)
    dw_sibling = mm_tn(h0, dp0, N_DEV // 2, BF16, "dw_in_even_sibling", dep=dep, pick=(2, 1 - comm.core))
    dep = comm.reduce_begin({"w_in_even": dw_sibling}, "even_in", sibling_part=True)
    dw_own = mm_tn(h0, dp0, N_DEV // 2, BF16, "dw_in_even_own", dep=dep, pick=(2, comm.core))
    dep = comm.reduce_send(after=dw_own, own_part={"w_in_even": dw_own})
    dh0 = mm_nt(dp0, g_wie, F32, "dh_even", dep=dep)
    dep = None
    grad_x, dg_pre_even = norm_bwd(dh0, xs, ln_pre_even, gx1, "pre_even_bwd", dep=dep)
    small_g = [dg_pre_even, dpool_scale, dg_post_even, dg_pre_odd, dsconv, ddconv, ddconv_b, dgam, dbet, dg_post_odd]
    return loss_blk, grad_x, small_g


class _Exchanges:
    def __init__(self, dev, core, d):
        self.dev = dev.astype(jnp.int32).reshape(1)
        self.core = core
        self.chip = (dev // 2).astype(jnp.int32).reshape(1)
        self.d = d
        self.in_flight = {}
        self.to_sibling = None
        self.pending = []

    def start_weights(self, tag, blocks, afters):
        lands = [lax.empty((N_DEV,) + b.shape, b.dtype) for b in blocks]
        send, recv, srcs, lands, token = split_start("gather", blocks, lands, afters, "ag_start_" + tag)
        self.in_flight[tag] = (send, recv, srcs, lands)
        return token

    def weights_arrived(self, tag, after):
        send, recv, srcs, lands = self.in_flight.pop(tag)
        srcs, lands = split_wait("gather", send, recv, srcs, lands, [after], "ag_wait_" + tag)
        lands = [place_block(l, b, self.dev, "ag_own_%s_%d" % (tag, k)) for k, (l, b) in enumerate(zip(lands, srcs))]
        lands = [l.reshape((4, 2) + l.shape[1:]) for l in lands]
        send, recv, _, lands, token = split_start("halves", [], lands, [], "ag_sibling_start_" + tag)
        self.in_flight[tag] = (send, recv, lands)
        return token

    def weights(self, tag, after):
        send, recv, lands = self.in_flight.pop(tag)
        _, lands = split_wait("halves", send, recv, [], lands, [after], "ag_sibling_wait_" + tag)
        return [l.reshape((N_DEV,) + l.shape[2:]) for l in lands]

    def reduce_begin(self, partials, tag, sibling_part=False):
        names = list(partials)
        arrs = [partials[k].reshape((4, 1 if sibling_part else 2) + partials[k].shape[1:]) for k in names]
        lands = [lax.empty((4, 1) + a.shape[2:], a.dtype) for a in arrs]
        send, recv, srcs, lands, token = split_start("sibling", arrs, lands, [], "rs_sibling_start_" + tag)
        self.to_sibling = (tag, names, send, recv, srcs, lands)
        return token

    def reduce_send(self, after, own_part=None):
        tag, names, send, recv, srcs, lands = self.to_sibling
        srcs, lands = split_wait("sibling", send, recv, srcs, lands, [after], "rs_sibling_wait_" + tag)
        which = self.core
        if own_part is not None:
            srcs = [own_part[k].reshape((4, 1) + own_part[k].shape[1:]) for k in names]
            which = jnp.zeros((1,), jnp.int32)
        sums = [pair_add(o, r, which, "rs_pair_add_" + k) for k, o, r in zip(names, srcs, lands)]
        zones = [lax.empty(a.shape, a.dtype) for a in sums]
        send, recv, srcs, zones, token = split_start("scatter", sums, zones, [], "rs_start_" + tag)
        self.pending.append((tag, names, send, recv, srcs, zones))
        return token

    def finish_updates(self, big, afters):
        tag, names, send, recv, srcs, lands = self.pending.pop(0)
        srcs, lands = split_wait("scatter", send, recv, srcs, lands, afters, "rs_wait_" + tag)
        out = {}
        for name, own, got in zip(names, srcs, lands):
            w, m, v = big[name]
            shp = own.shape[1:]
            outs = adamw_big(w.reshape(shp), m.reshape(shp), v.reshape(shp), own, got, self.chip, "adamw_" + name)
            out[name] = [o.reshape(w.shape) for o in outs]
        return out


def _update_small(small_g, loss_blk, small_w, small_m, small_v, dev, d, cl, deps):
    packed = jnp.concatenate([_rows128(g) for g in small_g] + [loss_blk], axis=0)
    (g8,) = all_gather([packed], "ag_small_grads", deps)
    tot = sum_devices(g8, "sum_small_grads")
    loss = tot[packed.shape[0] - 8, 0]
    full_g = []
    lo = 0
    for g in small_g:
        rows = g.size // 128
        full_g.append(tot[lo:lo + rows].reshape(g.shape))
        lo += rows

    def mine(g, width):
        return lax.dynamic_slice_in_dim(g, dev * width, width, axis=g.ndim - 1)

    fg = full_g
    small_gl = [fg[0], fg[1], fg[2], mine(fg[3], d // N_DEV), mine(fg[4], cl), mine(fg[5], cl), mine(fg[6], cl),
                mine(fg[7], cl), mine(fg[8], cl), mine(fg[9], d // N_DEV)]
    sd, sm, sv = adamw_small(small_w, small_gl, small_m, small_v, "adamw_small")

    def like(k, a):
        return a[None] if k in (4, 5) else a

    sg = [like(k, a) for k, a in enumerate(small_gl)]
    sd = [like(k, a) for k, a in enumerate(sd)]
    sm = [like(k, a) for k, a in enumerate(sm)]
    sv = [like(k, a) for k, a in enumerate(sv)]
    return sg, sd, sm, sv, loss
```

```python
import functools
import math

import jax
import jax.numpy as jnp
from jax import lax
from jax.experimental import pallas as pl
from jax.experimental.pallas import tpu as pltpu

F32 = jnp.float32
BF16 = jnp.bfloat16
EPS = 1e-6
HEAD_DIM = 128
POOL_WINDOWS = (2, 4, 8, 16)
SCONV_K = 3
CONF_K = 31
HALO = 32
N_DEV = 8
VMEM_LIMIT = 56 * 1024 * 1024
MESH = pl.DeviceIdType.MESH

ADAM_LR = 0.001
ADAM_B1 = 0.9
ADAM_B2 = 0.999
ADAM_EPS = 1e-08
ADAM_WD = 0.01
ADAM_STEP = 10


def _params(*sem):
    return pltpu.CompilerParams(dimension_semantics=sem, vmem_limit_bytes=VMEM_LIMIT)


def _sigmoid(v):
    return 1.0 / (1.0 + jnp.exp(-v))


def _silu(v):
    return v * _sigmoid(v)


def _silu_and_grad(v):
    s = _sigmoid(v)
    return v * s, s * (1.0 + v * (1.0 - s))


def _rowsum8(v):
    r, c = v.shape
    return jnp.sum(v.reshape(r // 8, 8, c), axis=0)


SUBLANES = 8


class _Taps:
    def __init__(self, xx, rows, before):
        self.xx, self.rows, self.before, self.rotated = xx, rows, before, {}

    def __call__(self, i):
        r, q = i % SUBLANES, i // SUBLANES
        if r not in self.rotated:
            n = self.xx.shape[0]
            self.rotated[r] = self.xx if r == 0 else pltpu.roll(self.xx, r if self.before else n - r, 0)
        lo = HALO - SUBLANES * q if self.before else SUBLANES * q
        return self.rotated[r][lo:lo + self.rows]


def _window_sum(xx, win, before):
    n = xx.shape[0]
    acc = xx
    k = 1
    while k < win:
        acc = acc + pltpu.roll(acc, k if before else n - k, 0)
        k *= 2
    return acc


def postnorm_fwd(x, o, g, g_next, name, tm=256, dep=None):
    s, d = x.shape
    dep_args, dep_specs = _after(dep)

    def body(x_ref, o_ref, g_ref, gn_ref, *rest):
        y_ref, h_ref = rest[-2:]
        ov = o_ref[...]
        r = lax.rsqrt(jnp.mean(ov * ov, axis=-1, keepdims=True) + EPS)
        y = x_ref[...] + ov * r * g_ref[...]
        y_ref[...] = y
        r2 = lax.rsqrt(jnp.mean(y * y, axis=-1, keepdims=True) + EPS)
        h_ref[...] = (y * r2 * gn_ref[...]).astype(BF16)

    row = pl.BlockSpec((tm, d), lambda i: (i, 0))
    vec = pl.BlockSpec((1, d), lambda i: (0, 0))
    return pl.pallas_call(
        body, name=name, grid=(s // tm,),
        in_specs=[row, row, vec, vec] + dep_specs, out_specs=[row, row],
        out_shape=[jax.ShapeDtypeStruct((s, d), F32), jax.ShapeDtypeStruct((s, d), BF16)],
        compiler_params=_params("parallel"),
    )(x, o, g, g_next, *dep_args)


def final_fwd_bwd(x1, o, g, target, name, tm=256):
    s, d = x1.shape
    n = s // tm

    def body(x_ref, o_ref, g_ref, t_ref, loss_ref, gx_ref, do_ref, dg_ref, lacc, gacc):
        i = pl.program_id(0)

        @pl.when(i == 0)
        def _():
            lacc[...] = jnp.zeros_like(lacc)
            gacc[...] = jnp.zeros_like(gacc)

        ov = o_ref[...]
        gv = g_ref[...]
        r = lax.rsqrt(jnp.mean(ov * ov, axis=-1, keepdims=True) + EPS)
        oh = ov * r
        diff = x_ref[...] + oh * gv - t_ref[...]
        lacc[...] += _rowsum8(diff * diff)
        gx = diff * (1.0 / d)
        gx_ref[...] = gx
        gacc[...] += _rowsum8(gx * oh)
        dn = gx * gv
        do_ref[...] = (r * (dn - oh * jnp.mean(dn * oh, axis=-1, keepdims=True))).astype(BF16)

        @pl.when(i == n - 1)
        def _():
            tot = jnp.sum(jnp.sum(lacc[...], axis=0, keepdims=True), axis=1, keepdims=True)
            loss_ref[...] = jnp.broadcast_to(tot * (0.5 / d), loss_ref.shape)
            dg_ref[...] = jnp.sum(gacc[...], axis=0, keepdims=True)

    row = pl.BlockSpec((tm, d), lambda i: (i, 0))
    vec = pl.BlockSpec((1, d), lambda i: (0, 0))
    return pl.pallas_call(
        body, name=name, grid=(n,),
        in_specs=[row, row, vec, row],
        out_specs=[pl.BlockSpec((8, 128), lambda i: (0, 0)), row, row, vec],
        out_shape=[jax.ShapeDtypeStruct((8, 128), F32), jax.ShapeDtypeStruct((s, d), F32),
                   jax.ShapeDtypeStruct((s, d), BF16), jax.ShapeDtypeStruct((1, d), F32)],
        scratch_shapes=[pltpu.VMEM((8, d), F32), pltpu.VMEM((8, d), F32)],
        compiler_params=_params("arbitrary"),
    )(x1, o, g, target)


def _rms_bwd_rows(dyv, xv, gv):
    r = lax.rsqrt(jnp.mean(xv * xv, axis=-1, keepdims=True) + EPS)
    xh = xv * r
    dn = dyv * gv
    return r * (dn - xh * jnp.mean(dn * xh, axis=-1, keepdims=True)), _rowsum8(dyv * xh)


def norm_bwd(dy, inp, g, resid, name, inp2=None, g2=None, tm=256, dep=None):
    s, d = inp.shape
    n = s // tm
    chain = inp2 is not None

    def body(*refs):
        dy_ref, x_ref, g_ref, r_ref = refs[:4]
        outs = refs[-6:] if chain else refs[-3:]
        i = pl.program_id(0)

        @pl.when(i == 0)
        def _():
            for acc in outs[-2:] if chain else outs[-1:]:
                acc[...] = jnp.zeros_like(acc)

        if chain:
            x2_ref, g2_ref = refs[4:6]
            dx_ref, dg_ref, dx2_ref, dg2_ref, gacc, gacc2 = outs
        else:
            dx_ref, dg_ref, gacc = outs
        dx, dg_rows = _rms_bwd_rows(dy_ref[...].astype(F32), x_ref[...], g_ref[...])
        dx = dx + r_ref[...]
        dx_ref[...] = dx
        gacc[...] += dg_rows
        if chain:
            dx2, dg2_rows = _rms_bwd_rows(dx, x2_ref[...], g2_ref[...])
            dx2_ref[...] = dx2.astype(BF16)
            gacc2[...] += dg2_rows

        @pl.when(i == n - 1)
        def _():
            dg_ref[...] = jnp.sum(gacc[...], axis=0, keepdims=True)
            if chain:
                dg2_ref[...] = jnp.sum(gacc2[...], axis=0, keepdims=True)

    row = pl.BlockSpec((tm, d), lambda i: (i, 0))
    vec = pl.BlockSpec((1, d), lambda i: (0, 0))
    dep_args, dep_specs = _after(dep)
    extra = [inp2, g2] if chain else []
    return pl.pallas_call(
        body, name=name, grid=(n,),
        in_specs=[row, row, vec, row] + ([row, vec] if chain else []) + dep_specs,
        out_specs=[row, vec] * (2 if chain else 1),
        out_shape=[jax.ShapeDtypeStruct((s, d), F32), jax.ShapeDtypeStruct((1, d), F32)]
        + ([jax.ShapeDtypeStruct((s, d), BF16), jax.ShapeDtypeStruct((1, d), F32)] if chain else []),
        scratch_shapes=[pltpu.VMEM((8, d), F32)] * (2 if chain else 1),
        compiler_params=_params("arbitrary"),
    )(dy, inp, g, resid, *extra, *dep_args)


def _after(dep):
    if dep is None:
        return [], []
    return [dep], [pl.BlockSpec((8, 128), lambda *_: (0, 0))]


def mm_nn(a, w, out_dtype, name, tm=2048, tn=None, dep=None):
    m, k = a.shape
    tm = min(tm, m)
    ns, _, n = w.shape
    tn = n if tn is None else tn
    nj = n // tn
    dep_args, dep_specs = _after(dep)

    def body(a_ref, w_ref, *rest):
        o_ref = rest[-1]
        o_ref[...] = jnp.dot(a_ref[...], w_ref[0], preferred_element_type=F32).astype(out_dtype)

    return pl.pallas_call(
        body, name=name, grid=(ns, nj, m // tm),
        in_specs=[pl.BlockSpec((tm, k), lambda s, j, i: (i, 0)),
                  pl.BlockSpec((1, k, tn), lambda s, j, i: (s, 0, j))] + dep_specs,
        out_specs=pl.BlockSpec((tm, tn), lambda s, j, i: (i, s * nj + j)),
        out_shape=jax.ShapeDtypeStruct((m, ns * n), out_dtype),
        compiler_params=_params("parallel", "parallel", "parallel"),
    )(a, w, *dep_args)


def mm_nt(a, w, out_dtype, name, tm=1024, tn=None, dep=None):
    m = a.shape[0]
    tm = min(tm, m)
    ns, k, n = w.shape
    tn = n if tn is None else tn
    nj = n // tn
    steps = ns * nj
    dep_args, dep_specs = _after(dep)

    def body(a_ref, w_ref, *rest):
        o_ref, acc = rest[-2:]
        r = pl.program_id(1)

        @pl.when(r == 0)
        def _():
            acc[...] = jnp.zeros_like(acc)

        acc[...] += lax.dot_general(a_ref[...], w_ref[0], (((1,), (1,)), ((), ())),
                                    preferred_element_type=F32)

        @pl.when(r == steps - 1)
        def _():
            o_ref[...] = acc[...].astype(out_dtype)

    return pl.pallas_call(
        body, name=name, grid=(m // tm, steps),
        in_specs=[pl.BlockSpec((tm, tn), lambda i, r: (i, r)),
                  pl.BlockSpec((1, k, tn), lambda i, r: (r // nj, 0, r % nj))] + dep_specs,
        out_specs=pl.BlockSpec((tm, k), lambda i, r: (i, 0)),
        out_shape=jax.ShapeDtypeStruct((m, k), out_dtype),
        scratch_shapes=[pltpu.VMEM((tm, k), F32)],
        compiler_params=_params("parallel", "arbitrary"),
    )(a, w, *dep_args)


def mm_tn(a, b, ns, out_dtype, name, tk=1024, tm=2048, dep=None, pick=None):
    m, k = a.shape
    tm = min(tm, m)
    step, offset = (1, None) if pick is None else pick
    n = b.shape[1] // (ns * step)
    steps = m // tm
    dep_args, dep_specs = _after(dep)
    n_pre = 0 if pick is None else 1

    def b_block(s, j, r, *pre):
        return (r, s if pick is None else step * s + pre[0][0])

    def body(*refs):
        a_ref, b_ref = refs[n_pre:n_pre + 2]
        o_ref, acc = refs[-2:]
        r = pl.program_id(2)

        @pl.when(r == 0)
        def _():
            acc[...] = jnp.zeros_like(acc)

        acc[...] += lax.dot_general(a_ref[...], b_ref[...], (((0,), (0,)), ((), ())),
                                    preferred_element_type=F32)

        @pl.when(r == steps - 1)
        def _():
            o_ref[0] = acc[...].astype(out_dtype)

    return pl.pallas_call(
        body, name=name,
        grid_spec=pltpu.PrefetchScalarGridSpec(
            num_scalar_prefetch=n_pre, grid=(ns, k // tk, steps),
            in_specs=[pl.BlockSpec((tm, tk), lambda s, j, r, *pre: (r, j)),
                      pl.BlockSpec((tm, n), b_block)] + dep_specs,
            out_specs=pl.BlockSpec((1, tk, n), lambda s, j, r, *pre: (s, j, 0)),
            scratch_shapes=[pltpu.VMEM((tk, n), F32)]),
        out_shape=jax.ShapeDtypeStruct((ns, k, n), out_dtype),
        compiler_params=_params("parallel", "parallel", "arbitrary"),
    )(*([] if pick is None else [offset]), a, b, *dep_args)


SB_BLK = 128


LOG2E = 1.0 / math.log(2.0)


def _split_dot(v, tri2):
    hi = pltpu.bitcast(pltpu.bitcast(v, jnp.uint32) & jnp.uint32(0xFFFF0000), F32)
    lo = (v - hi).astype(BF16)
    return jnp.dot(jnp.concatenate([hi.astype(BF16), lo], axis=1), tri2, preferred_element_type=F32)


def _sb_scores(z2, lim, dcol, tri_ex, masked):
    sp = jnp.log2(1.0 + jnp.exp2(-jnp.abs(z2)))
    lb = jnp.minimum(z2, 0.0) - sp
    l1m = lb - z2
    mask = None
    if masked:
        mask = dcol < lim
        l1m = jnp.where(mask, l1m, 0.0)
    return mask, lb, l1m, _split_dot(l1m, tri_ex)


def _sb_consts():
    row = lax.broadcasted_iota(jnp.int32, (SB_BLK, SB_BLK), 0)
    col = lax.broadcasted_iota(jnp.int32, (SB_BLK, SB_BLK), 1)
    tri_ex = jnp.where(row > col, 1.0, 0.0).astype(BF16)
    tri_in = jnp.where(row >= col, 1.0, 0.0).astype(BF16)
    return col - row, jnp.concatenate([tri_ex, tri_ex], axis=0), jnp.concatenate([tri_in, tri_in], axis=0)


def sb_fwd(p, n_heads, name, tq=256, nsub=4, dep=None):
    s = p.shape[0]
    h_n = n_heads
    b = SB_BLK
    nqs = tq // b
    tk = nsub * b
    scale = 1.0 / math.sqrt(HEAD_DIM)

    dep_args, dep_specs = _after(dep)

    def body(q_ref, k_ref, v_ref, *rest):
        o_ref, w_ref = rest[-2:]
        qi = pl.program_id(1)
        dcol, tri_ex, _ = _sb_consts()
        qv = [q_ref[qs * b:(qs + 1) * b, :] for qs in range(nqs)]
        n_groups = ((qi + 1) * nqs - 1) // nsub + 1

        def step(it, carry, masked):
            c1s, accs = carry
            g = n_groups - 1 - it
            off = pl.multiple_of(g * tk, tk)
            kg = k_ref[pl.ds(off, tk), :]
            vg = v_ref[pl.ds(off, tk), :]
            new_c1, new_acc = [], []
            for qs in range(nqs):
                qb = qi * nqs + qs
                z2 = lax.dot_general(qv[qs], kg, (((1,), (1,)), ((), ())),
                                     preferred_element_type=F32) * (scale * LOG2E)
                blocks = [_sb_scores(z2[:, j * b:(j + 1) * b], (qb - (g * nsub + j)) * b, dcol, tri_ex, masked)
                          for j in range(nsub)]
                run = c1s[qs]
                ws = [None] * nsub
                for j in reversed(range(nsub)):
                    mask, lb, l1m, ls_loc = blocks[j]
                    wj = jnp.exp2(lb + ls_loc + run)
                    ws[j] = (jnp.where(mask, wj, 0.0) if masked else wj).astype(BF16)
                    run = run + jnp.sum(l1m, axis=1, keepdims=True)
                w = jnp.concatenate(ws, axis=1)
                w_ref[0, g, qs * b:(qs + 1) * b, :] = w
                new_acc.append(accs[qs] + jnp.dot(w, vg, preferred_element_type=F32))
                new_c1.append(run)
            return tuple(new_c1), tuple(new_acc)

        init = (tuple(jnp.zeros((b, 1), F32) for _ in range(nqs)),
                tuple(jnp.zeros((b, HEAD_DIM), F32) for _ in range(nqs)))
        assert nqs == 2 and nsub % 2 == 0
        first = step(0, init, True)
        _, accs = lax.fori_loop(1, n_groups, functools.partial(step, masked=False), first)
        for qs in range(nqs):
            o_ref[qs * b:(qs + 1) * b, :] = accs[qs]

    return pl.pallas_call(
        body, name=name, grid=(h_n, s // tq),
        in_specs=[pl.BlockSpec((tq, HEAD_DIM), lambda h, i: (i, h)),
                  pl.BlockSpec((s, HEAD_DIM), lambda h, i: (0, h_n + h)),
                  pl.BlockSpec((s, HEAD_DIM), lambda h, i: (0, 2 * h_n + h))] + dep_specs,
        out_specs=[pl.BlockSpec((tq, HEAD_DIM), lambda h, i: (i, h)),
                   pl.BlockSpec((1, s // tk, tq, tk), lambda h, i: (h, 0, i, 0))],
        out_shape=[jax.ShapeDtypeStruct((s, h_n * HEAD_DIM), F32),
                   jax.ShapeDtypeStruct((h_n, s // tk, s, tk), BF16)],
        compiler_params=_params("parallel", "arbitrary"),
    )(p, p, p, *dep_args)


def sb_bwd(p, a, wts, da, n_heads, name, tq=256, dep=None):
    s = p.shape[0]
    h_n = n_heads
    nq = s // tq
    b = SB_BLK
    nqs = tq // b
    tk = wts.shape[3]
    nsub = tk // b
    scale = 1.0 / math.sqrt(HEAD_DIM)
    dep_args, dep_specs = _after(dep)

    def body(q_ref, k_ref, v_ref, a_ref, da_ref, w_ref, *rest):
        dq_ref, dk_ref, dv_ref, dk_acc, dv_acc = rest[-5:]
        qi = pl.program_id(1)

        @pl.when(qi == 0)
        def _():
            dk_acc[...] = jnp.zeros_like(dk_acc)
            dv_acc[...] = jnp.zeros_like(dv_acc)

        dcol, _, tri_in = _sb_consts()
        q_all = q_ref[...]
        do_all = da_ref[...]
        qv = [q_ref[qs * b:(qs + 1) * b, :] for qs in range(nqs)]
        dov = [da_ref[qs * b:(qs + 1) * b, :] for qs in range(nqs)]
        tots = [jnp.sum(dov[qs].astype(F32) * a_ref[qs * b:(qs + 1) * b, :], axis=1, keepdims=True)
                for qs in range(nqs)]
        n_groups = ((qi + 1) * nqs - 1) // nsub + 1

        def step(it, carry, masked):
            c2s, dqs = carry
            g = n_groups - 1 - it
            off = pl.multiple_of(g * tk, tk)
            kg = k_ref[pl.ds(off, tk), :]
            vg = v_ref[pl.ds(off, tk), :]
            w_all = w_ref[0, g]
            new_c2, new_dq, dz_rows = [], [], []
            for qs in range(nqs):
                qb = qi * nqs + qs
                z2 = lax.dot_general(qv[qs], kg, (((1,), (1,)), ((), ())),
                                     preferred_element_type=F32) * (-scale * LOG2E)
                dw = lax.dot_general(dov[qs], vg, (((1,), (1,)), ((), ())), preferred_element_type=F32)
                beta = 1.0 / (1.0 + jnp.exp2(z2))
                e = dw * w_all[qs * b:(qs + 1) * b, :].astype(F32)
                run2 = c2s[qs]
                dzs = [None] * nsub
                for j in reversed(range(nsub)):
                    cols = slice(j * b, (j + 1) * b)
                    later = _split_dot(e[:, cols], tri_in) + run2
                    bj = beta[:, cols]
                    dz = (e[:, cols] * (1.0 - bj) - bj * (tots[qs] - later)) * scale
                    if masked:
                        dz = jnp.where(dcol < (qb - (g * nsub + j)) * b, dz, 0.0)
                    dzs[j] = dz.astype(BF16)
                    run2 = run2 + jnp.sum(e[:, cols], axis=1, keepdims=True)
                dzq = jnp.concatenate(dzs, axis=1)
                new_dq.append(dqs[qs] + jnp.dot(dzq, kg, preferred_element_type=F32))
                new_c2.append(run2)
                dz_rows.append(dzq)
            dz_all = jnp.concatenate(dz_rows, axis=0)
            dk_acc[pl.ds(off, tk), :] += lax.dot_general(dz_all, q_all, (((0,), (0,)), ((), ())),
                                                         preferred_element_type=F32)
            dv_acc[pl.ds(off, tk), :] += lax.dot_general(w_all, do_all, (((0,), (0,)), ((), ())),
                                                         preferred_element_type=F32)
            return tuple(new_c2), tuple(new_dq)

        zeros = tuple(jnp.zeros((b, 1), F32) for _ in range(nqs))
        assert nqs == 2 and nsub % 2 == 0
        first = step(0, (zeros, tuple(jnp.zeros((b, HEAD_DIM), F32) for _ in range(nqs))), True)
        _, dqs = lax.fori_loop(1, n_groups, functools.partial(step, masked=False), first)
        for qs in range(nqs):
            dq_ref[qs * b:(qs + 1) * b, :] = dqs[qs].astype(BF16)

        @pl.when(qi == nq - 1)
        def _():
            dk_ref[...] = dk_acc[...].astype(BF16)
            dv_ref[...] = dv_acc[...].astype(BF16)

    blk = pl.BlockSpec((tq, HEAD_DIM), lambda h, i: (i, h))
    full = pl.BlockSpec((s, HEAD_DIM), lambda h, i: (0, h))
    return pl.pallas_call(
        body, name=name, grid=(h_n, nq),
        in_specs=[blk, pl.BlockSpec((s, HEAD_DIM), lambda h, i: (0, h_n + h)),
                  pl.BlockSpec((s, HEAD_DIM), lambda h, i: (0, 2 * h_n + h)), blk, blk,
                  pl.BlockSpec((1, s // tk, tq, tk), lambda h, i: (h, 0, i, 0))] + dep_specs,
        out_specs=[blk, full, full],
        out_shape=[jax.ShapeDtypeStruct((s, h_n * HEAD_DIM), BF16)] * 3,
        scratch_shapes=[pltpu.VMEM((s, HEAD_DIM), F32), pltpu.VMEM((s, HEAD_DIM), F32)],
        compiler_params=_params("parallel", "arbitrary"),
    )(p, p, p, a, da, wts, *dep_args)


def _pool_window(xx, win, r0, rc):
    cur = xx[HALO:HALO + rc]
    ws = _window_sum(xx, win, True)[HALO:HALO + rc]
    t_idx = r0 + lax.broadcasted_iota(jnp.int32, (rc, 1), 0)
    inv = 1.0 / jnp.minimum(win, t_idx + 1).astype(F32)
    return ws * inv - cur, inv


def even_mix_fwd(a, p, pool_w, pool_scale, name, rc=64, dep=None):
    s = p.shape[0]
    ng = len(POOL_WINDOWS)
    cw = pool_w.shape[1]
    n_chunks = s // rc
    dep_args, dep_specs = _after(dep)

    def body(a_ref, u_ref, g_ref, w_ref, sc_ref, *rest):
        y_ref, upad = rest[-2:]
        j = pl.program_id(0)

        @pl.when(j < ng)
        def _():
            def chunk(ci, carry):
                rows = pl.ds(pl.multiple_of(ci * rc, rc), rc)
                y_ref[rows, :] = (a_ref[rows, :] * _silu(g_ref[rows, :].astype(F32))).astype(BF16)
                return carry

            lax.fori_loop(0, n_chunks, chunk, 0)

        for gi, win in enumerate(POOL_WINDOWS):
            @pl.when(j == ng + gi)
            def _(win=win):
                upad[0:HALO, :] = jnp.zeros((HALO, cw), F32)

                def fill(ci, carry):
                    r0 = pl.multiple_of(ci * rc, rc)
                    upad[pl.ds(pl.multiple_of(r0 + HALO, HALO), rc), :] = u_ref[pl.ds(r0, rc), :].astype(F32)
                    return carry

                lax.fori_loop(0, n_chunks, fill, 0)

                def chunk(ci, carry):
                    r0 = pl.multiple_of(ci * rc, rc)
                    rows = pl.ds(r0, rc)
                    pooled, _ = _pool_window(upad[pl.ds(r0, HALO + rc), :], win, r0, rc)
                    t = jnp.dot(pooled.astype(BF16), w_ref[0], preferred_element_type=F32)
                    y_ref[rows, :] = (t * sc_ref[...] * _silu(g_ref[rows, :].astype(F32))).astype(BF16)
                    return carry

                lax.fori_loop(0, n_chunks, chunk, 0)

    grp = lambda j: jnp.maximum(j - ng, 0)
    return pl.pallas_call(
        body, name=name, grid=(2 * ng,),
        in_specs=[pl.BlockSpec((s, cw), lambda j: (0, jnp.minimum(j, ng - 1))),
                  pl.BlockSpec((s, cw), lambda j: (0, 3 * ng + grp(j))),
                  pl.BlockSpec((s, cw), lambda j: (0, 4 * ng + j)),
                  pl.BlockSpec((1, cw, cw), lambda j: (grp(j), 0, 0)),
                  pl.BlockSpec((1, cw), lambda j: (0, grp(j)))] + dep_specs,
        out_specs=pl.BlockSpec((s, cw), lambda j: (0, j)),
        out_shape=jax.ShapeDtypeStruct((s, 2 * ng * cw), BF16),
        scratch_shapes=[pltpu.VMEM((HALO + s, cw), F32)],
        compiler_params=_params("arbitrary"),
    )(a, p, p, pool_w, pool_scale, *dep_args)


def even_mix_bwd(dy, a, p, pool_w, pool_scale, name, rc=64):
    s = p.shape[0]
    ng = len(POOL_WINDOWS)
    cw = pool_w.shape[1]
    n_chunks = s // rc

    def body(dy_ref, a_ref, u_ref, g_ref, w_ref, sc_ref, da_ref, du_ref, dg_ref, dw_ref, dsc_ref,
             upad, rpad, dpl, dw_acc, dsc_acc):
        j = pl.program_id(0)

        @pl.when(j < ng)
        def _():
            def chunk(ci, carry):
                rows = pl.ds(pl.multiple_of(ci * rc, rc), rc)
                dyv = dy_ref[rows, :].astype(F32)
                sg, dsg = _silu_and_grad(g_ref[rows, :].astype(F32))
                da_ref[rows, :] = (dyv * sg).astype(BF16)
                dg_ref[rows, :] = (dyv * a_ref[rows, :] * dsg).astype(BF16)
                return carry

            lax.fori_loop(0, n_chunks, chunk, 0)

        for gi, win in enumerate(POOL_WINDOWS):
            @pl.when(j == ng + gi)
            def _(win=win):
                upad[0:HALO, :] = jnp.zeros((HALO, cw), F32)
                rpad[s:s + HALO, :] = jnp.zeros((HALO, cw), F32)
                dw_acc[...] = jnp.zeros_like(dw_acc)
                dsc_acc[...] = jnp.zeros_like(dsc_acc)

                def fill(ci, carry):
                    r0 = pl.multiple_of(ci * rc, rc)
                    upad[pl.ds(pl.multiple_of(r0 + HALO, HALO), rc), :] = u_ref[pl.ds(r0, rc), :].astype(F32)
                    return carry

                lax.fori_loop(0, n_chunks, fill, 0)

                def chunk(ci, carry):
                    r0 = pl.multiple_of(ci * rc, rc)
                    rows = pl.ds(r0, rc)
                    pooled, inv = _pool_window(upad[pl.ds(r0, HALO + rc), :], win, r0, rc)
                    pb = pooled.astype(BF16)
                    wv = w_ref[0]
                    t = jnp.dot(pb, wv, preferred_element_type=F32)
                    scv = sc_ref[...]
                    dyv = dy_ref[rows, :].astype(F32)
                    sg, dsg = _silu_and_grad(g_ref[rows, :].astype(F32))
                    dpo = dyv * sg
                    dg_ref[rows, :] = (dyv * t * scv * dsg).astype(BF16)
                    dsc_acc[...] += _rowsum8(dpo * t)
                    dtb = (dpo * scv).astype(BF16)
                    dw_acc[...] += lax.dot_general(pb, dtb, (((0,), (0,)), ((), ())),
                                                   preferred_element_type=F32)
                    dpooled = lax.dot_general(dtb, wv, (((1,), (1,)), ((), ())),
                                              preferred_element_type=F32)
                    dpl[rows, :] = dpooled
                    rpad[rows, :] = dpooled * inv
                    return carry

                lax.fori_loop(0, n_chunks, chunk, 0)

                def chunk2(ci, carry):
                    r0 = pl.multiple_of(ci * rc, rc)
                    rows = pl.ds(r0, rc)
                    xx = rpad[pl.ds(r0, rc + HALO), :]
                    fs = _window_sum(xx, win, False)[0:rc]
                    du_ref[rows, :] = (fs - dpl[rows, :]).astype(BF16)
                    return carry

                lax.fori_loop(0, n_chunks, chunk2, 0)
                dw_ref[0] = dw_acc[...]
                dsc_ref[...] = jnp.sum(dsc_acc[...], axis=0, keepdims=True)

    grp = lambda j: jnp.maximum(j - ng, 0)
    att = lambda j: jnp.minimum(j, ng - 1)
    return pl.pallas_call(
        body, name=name, grid=(2 * ng,),
        in_specs=[pl.BlockSpec((s, cw), lambda j: (0, j)),
                  pl.BlockSpec((s, cw), lambda j: (0, att(j))),
                  pl.BlockSpec((s, cw), lambda j: (0, 3 * ng + grp(j))),
                  pl.BlockSpec((s, cw), lambda j: (0, 4 * ng + j)),
                  pl.BlockSpec((1, cw, cw), lambda j: (grp(j), 0, 0)),
                  pl.BlockSpec((1, cw), lambda j: (0, grp(j)))],
        out_specs=[pl.BlockSpec((s, cw), lambda j: (0, att(j))),
                   pl.BlockSpec((s, cw), lambda j: (0, grp(j))),
                   pl.BlockSpec((s, cw), lambda j: (0, j)),
                   pl.BlockSpec((1, cw, cw), lambda j: (grp(j), 0, 0)),
                   pl.BlockSpec((1, cw), lambda j: (0, grp(j)))],
        out_shape=[jax.ShapeDtypeStruct((s, ng * cw), BF16), jax.ShapeDtypeStruct((s, ng * cw), BF16),
                   jax.ShapeDtypeStruct((s, 2 * ng * cw), BF16),
                   jax.ShapeDtypeStruct((ng, cw, cw), F32), jax.ShapeDtypeStruct((1, ng * cw), F32)],
        scratch_shapes=[pltpu.VMEM((HALO + s, cw), F32), pltpu.VMEM((s + HALO, cw), F32),
                        pltpu.VMEM((s, cw), F32), pltpu.VMEM((cw, cw), F32), pltpu.VMEM((8, cw), F32)],
        compiler_params=_params("arbitrary"),
    )(dy, a, p, p, pool_w, pool_scale)


def _halo_before(tm):
    return lambda i: jnp.maximum(i * (tm // HALO) - 1, 0)


def _halo_after(tm, s):
    return lambda i: jnp.minimum((i + 1) * (tm // HALO), s // HALO - 1)


def odd_mix_fwd(p, sconv_w, dconv_w, dconv_b, cnorm_g, cnorm_b, name, tm=128):
    s = p.shape[0]
    cw = sconv_w.shape[1]
    n = s // tm
    lanes = 128
    hb = _halo_before(tm)

    def body(hc_ref, hch_ref, bc_ref, cc_ref, cch_ref, ga_ref, gah_ref, gb_ref, gbh_ref, g1_ref, g2_ref,
             sw_ref, dw_ref, db_ref, gam_ref, bet_ref, y_ref, dc_ref):
        first = pl.program_id(0) == 0
        for l in range(cw // lanes):
            cols = slice(l * lanes, (l + 1) * lanes)
            mh = jnp.where(first, 0.0, cch_ref[:, cols].astype(F32) * hch_ref[:, cols].astype(F32))
            mm = cc_ref[:, cols].astype(F32) * hc_ref[:, cols].astype(F32)
            xx = jnp.concatenate([mh, mm], axis=0)
            tap = _Taps(xx, tm, True)
            cv = jnp.zeros((tm, lanes), F32)
            for k in range(SCONV_K):
                cv = cv + sw_ref[k:k + 1, cols] * tap(SCONV_K - 1 - k)
            c_out = bc_ref[:, cols].astype(F32) * cv
            y_ref[:, cols] = (c_out * _silu(g1_ref[:, cols].astype(F32))).astype(BF16)
            dh = jnp.where(first, 0.0, gah_ref[:, cols].astype(F32) * _sigmoid(gbh_ref[:, cols].astype(F32)))
            dm = ga_ref[:, cols].astype(F32) * _sigmoid(gb_ref[:, cols].astype(F32))
            xx = jnp.concatenate([dh, dm], axis=0)
            tap = _Taps(xx, tm, True)
            acc = jnp.zeros((tm, lanes), F32) + db_ref[:, cols]
            for k in range(CONF_K):
                acc = acc + dw_ref[k:k + 1, cols] * tap(CONF_K - 1 - k)
            dc_ref[:, cols] = acc
        rs = 32
        for r in range(tm // rs):
            rows = slice(r * rs, (r + 1) * rs)
            xv = dc_ref[rows, :]
            mu = jnp.mean(xv, axis=-1, keepdims=True)
            xc = xv - mu
            rstd = lax.rsqrt(jnp.mean(xc * xc, axis=-1, keepdims=True) + EPS)
            ln = xc * rstd * gam_ref[...] + bet_ref[...]
            y_ref[rows, cw:2 * cw] = (_silu(ln) * _silu(g2_ref[rows, :].astype(F32))).astype(BF16)

    main = lambda c: pl.BlockSpec((tm, cw), lambda i: (i, c))
    halo = lambda c: pl.BlockSpec((HALO, cw), lambda i: (hb(i), c))
    vec = lambda r: pl.BlockSpec((r, cw), lambda i: (0, 0))
    return pl.pallas_call(
        body, name=name, grid=(n,),
        in_specs=[main(0), halo(0), main(1), main(2), halo(2), main(3), halo(3), main(4), halo(4),
                  main(5), main(6), vec(SCONV_K), vec(CONF_K), vec(1), vec(1), vec(1)],
        out_specs=[pl.BlockSpec((tm, 2 * cw), lambda i: (i, 0)), pl.BlockSpec((tm, cw), lambda i: (i, 0))],
        out_shape=[jax.ShapeDtypeStruct((s, 2 * cw), BF16), jax.ShapeDtypeStruct((s, cw), F32)],
        compiler_params=_params("parallel"),
    )(p, p, p, p, p, p, p, p, p, p, p, sconv_w, dconv_w, dconv_b, cnorm_g, cnorm_b)


def odd_bwd_ln(dy, p, dc, cnorm_g, cnorm_b, name, tm=256):
    s = p.shape[0]
    cw = dc.shape[1]
    n = s // tm
    rs = 32

    def body(dy_ref, g2_ref, dc_ref, gam_ref, bet_ref, ddc_ref, dg_ref, dgam_ref, dbet_ref, gacc, bacc):
        i = pl.program_id(0)

        @pl.when(i == 0)
        def _():
            gacc[...] = jnp.zeros_like(gacc)
            bacc[...] = jnp.zeros_like(bacc)

        def chunk(ci, carry):
            rows = pl.ds(pl.multiple_of(ci * rs, rs), rs)
            xv = dc_ref[rows, :]
            mu = jnp.mean(xv, axis=-1, keepdims=True)
            xc = xv - mu
            rstd = lax.rsqrt(jnp.mean(xc * xc, axis=-1, keepdims=True) + EPS)
            xh = xc * rstd
            gam = gam_ref[...]
            sl, dsl = _silu_and_grad(xh * gam + bet_ref[...])
            sg, dsg = _silu_and_grad(g2_ref[rows, :].astype(F32))
            dyv = dy_ref[rows, :].astype(F32)
            dg_ref[rows, :] = (dyv * sl * dsg).astype(BF16)
            dln = dyv * sg * dsl
            gacc[...] += _rowsum8(dln * xh)
            bacc[...] += _rowsum8(dln)
            dxh = dln * gam
            ddc_ref[rows, :] = rstd * (dxh - jnp.mean(dxh, axis=-1, keepdims=True)
                                       - xh * jnp.mean(dxh * xh, axis=-1, keepdims=True))
            return carry

        lax.fori_loop(0, tm // rs, chunk, 0)

        @pl.when(i == n - 1)
        def _():
            dgam_ref[...] = jnp.sum(gacc[...], axis=0, keepdims=True)
            dbet_ref[...] = jnp.sum(bacc[...], axis=0, keepdims=True)

    vec = pl.BlockSpec((1, cw), lambda i: (0, 0))
    return pl.pallas_call(
        body, name=name, grid=(n,),
        in_specs=[pl.BlockSpec((tm, cw), lambda i: (i, 1)), pl.BlockSpec((tm, cw), lambda i: (i, 6)),
                  pl.BlockSpec((tm, cw), lambda i: (i, 0)), vec, vec],
        out_specs=[pl.BlockSpec((tm, cw), lambda i: (i, 0)), pl.BlockSpec((tm, cw), lambda i: (i, 0)), vec, vec],
        out_shape=[jax.ShapeDtypeStruct((s, cw), F32), jax.ShapeDtypeStruct((s, cw), BF16),
                   jax.ShapeDtypeStruct((1, cw), F32), jax.ShapeDtypeStruct((1, cw), F32)],
        scratch_shapes=[pltpu.VMEM((8, cw), F32), pltpu.VMEM((8, cw), F32)],
        compiler_params=_params("arbitrary"),
    )(dy, p, dc, cnorm_g, cnorm_b)


def odd_bwd_conv(dy, p, ddc, dg2, sconv_w, dconv_w, name, tm=128):
    s = p.shape[0]
    cw = ddc.shape[1]
    n = s // tm
    lanes = 128
    hb = _halo_before(tm)
    ha = _halo_after(tm, s)

    def body(dy_ref, dya_ref, g1_ref, g1a_ref, bc_ref, bca_ref, hc_ref, hch_ref, cc_ref, cch_ref,
             ddc_ref, ddca_ref, ga_ref, gah_ref, gb_ref, gbh_ref, dg2_ref, sw_ref, dw_ref,
             dp_ref, dsw_ref, ddw_ref, ddb_ref, sw_acc, dw_acc, db_acc):
        i = pl.program_id(0)
        first = i == 0
        last = i == n - 1

        @pl.when(first)
        def _():
            sw_acc[...] = jnp.zeros_like(sw_acc)
            dw_acc[...] = jnp.zeros_like(dw_acc)
            db_acc[...] = jnp.zeros_like(db_acc)

        for l in range(cw // lanes):
            cols = slice(l * lanes, (l + 1) * lanes)
            mh = jnp.where(first, 0.0, cch_ref[:, cols].astype(F32) * hch_ref[:, cols].astype(F32))
            hcv = hc_ref[:, cols].astype(F32)
            ccv = cc_ref[:, cols].astype(F32)
            xx = jnp.concatenate([mh, ccv * hcv], axis=0)
            tap = _Taps(xx, tm, True)
            taps = [tap(SCONV_K - 1 - k) for k in range(SCONV_K)]
            cv = jnp.zeros((tm, lanes), F32)
            for k in range(SCONV_K):
                cv = cv + sw_ref[k:k + 1, cols] * taps[k]
            bcv = bc_ref[:, cols].astype(F32)
            dyv = dy_ref[:, cols].astype(F32)
            sg, dsg = _silu_and_grad(g1_ref[:, cols].astype(F32))
            dco = dyv * sg
            dp_ref[:, 5 * cw + l * lanes:5 * cw + (l + 1) * lanes] = (dyv * bcv * cv * dsg).astype(BF16)
            dp_ref[:, cw + l * lanes:cw + (l + 1) * lanes] = (dco * cv).astype(BF16)
            dcv = dco * bcv
            for k in range(SCONV_K):
                sw_acc[k * 8:(k + 1) * 8, cols] += _rowsum8(dcv * taps[k])
            dcv_a = jnp.where(last, 0.0, dya_ref[:, cols].astype(F32) * _silu(g1a_ref[:, cols].astype(F32))
                              * bca_ref[:, cols].astype(F32))
            xx = jnp.concatenate([dcv, dcv_a], axis=0)
            tap = _Taps(xx, tm, False)
            dm = jnp.zeros((tm, lanes), F32)
            for k in range(SCONV_K):
                dm = dm + sw_ref[k:k + 1, cols] * tap(SCONV_K - 1 - k)
            dp_ref[:, l * lanes:(l + 1) * lanes] = (dm * ccv).astype(BF16)
            dp_ref[:, 2 * cw + l * lanes:2 * cw + (l + 1) * lanes] = (dm * hcv).astype(BF16)
            gav = ga_ref[:, cols].astype(F32)
            sb = _sigmoid(gb_ref[:, cols].astype(F32))
            dh = jnp.where(first, 0.0, gah_ref[:, cols].astype(F32) * _sigmoid(gbh_ref[:, cols].astype(F32)))
            xx = jnp.concatenate([dh, gav * sb], axis=0)
            ddcv = ddc_ref[:, cols]
            db_acc[:, cols] += _rowsum8(ddcv)
            tap = _Taps(xx, tm, True)
            for k in range(CONF_K):
                dw_acc[k * 8:(k + 1) * 8, cols] += _rowsum8(ddcv * tap(CONF_K - 1 - k))
            ddc_a = jnp.where(last, 0.0, ddca_ref[:, cols])
            xx = jnp.concatenate([ddcv, ddc_a], axis=0)
            tap = _Taps(xx, tm, False)
            dgl = jnp.zeros((tm, lanes), F32)
            for k in range(CONF_K):
                dgl = dgl + dw_ref[k:k + 1, cols] * tap(CONF_K - 1 - k)
            dp_ref[:, 3 * cw + l * lanes:3 * cw + (l + 1) * lanes] = (dgl * sb).astype(BF16)
            dp_ref[:, 4 * cw + l * lanes:4 * cw + (l + 1) * lanes] = (dgl * gav * sb * (1.0 - sb)).astype(BF16)
        dp_ref[:, 6 * cw:7 * cw] = dg2_ref[...]

        @pl.when(last)
        def _():
            for k in range(SCONV_K):
                dsw_ref[k:k + 1, :] = jnp.sum(sw_acc[k * 8:(k + 1) * 8, :], axis=0, keepdims=True)
            for k in range(CONF_K):
                ddw_ref[k:k + 1, :] = jnp.sum(dw_acc[k * 8:(k + 1) * 8, :], axis=0, keepdims=True)
            ddb_ref[...] = jnp.sum(db_acc[...], axis=0, keepdims=True)

    def main(c):
        return pl.BlockSpec((tm, cw), lambda i: (i, c))

    def before(c):
        return pl.BlockSpec((HALO, cw), lambda i: (hb(i), c))

    def after(c):
        return pl.BlockSpec((HALO, cw), lambda i: (ha(i), c))

    def vec(r):
        return pl.BlockSpec((r, cw), lambda i: (0, 0))

    return pl.pallas_call(
        body, name=name, grid=(n,),
        in_specs=[main(0), after(0), main(5), after(5), main(1), after(1), main(0), before(0), main(2), before(2),
                  main(0), after(0), main(3), before(3), main(4), before(4), main(0), vec(SCONV_K), vec(CONF_K)],
        out_specs=[pl.BlockSpec((tm, 7 * cw), lambda i: (i, 0)), vec(SCONV_K), vec(CONF_K), vec(1)],
        out_shape=[jax.ShapeDtypeStruct((s, 7 * cw), BF16), jax.ShapeDtypeStruct((SCONV_K, cw), F32),
                   jax.ShapeDtypeStruct((CONF_K, cw), F32), jax.ShapeDtypeStruct((1, cw), F32)],
        scratch_shapes=[pltpu.VMEM((8 * SCONV_K, cw), F32), pltpu.VMEM((8 * CONF_K, cw), F32),
                        pltpu.VMEM((8, cw), F32)],
        compiler_params=_params("arbitrary"),
    )(dy, dy, p, p, p, p, p, p, p, p, ddc, ddc, p, p, p, p, dg2, sconv_w, dconv_w)


_ANY = pl.BlockSpec(memory_space=pl.ANY)


def _place():
    return lax.axis_index("x"), lax.axis_index("y"), lax.axis_index("c")


def all_gather(arrs, name, deps=()):
    n = len(arrs)

    def body(*refs):
        ins, outs = refs[:n], refs[n + len(deps):2 * n + len(deps)]
        send_sems, recv_sems, local_sems = refs[-3:]
        x, y, c = _place()
        me, sibling = (x, y, c), (x, y, 1 - c)
        chips = [(1 - x, y), (x, 1 - y), (1 - x, 1 - y)]

        def copy(a, k, block, to, src=None):
            px, py, pc = block
            dst = outs[a].at[4 * px + 2 * py + pc]
            return pltpu.make_async_remote_copy(
                src_ref=dst if src is None else src, dst_ref=dst,
                send_sem=send_sems.at[7 * a + k], recv_sem=recv_sems.at[7 * a + k],
                device_id=to, device_id_type=MESH)

        mine = [pltpu.make_async_copy(ins[a], outs[a].at[4 * x + 2 * y + c], local_sems.at[a]) for a in range(n)]
        first = []
        for a in range(n):
            first.append(copy(a, 0, me, sibling, src=ins[a]))
            first += [copy(a, 1 + j, me, (*chip, c), src=ins[a]) for j, chip in enumerate(chips)]
        for cp in first + mine:
            cp.start()
        passed = []
        for a in range(n):
            for j, chip in enumerate(chips):
                copy(a, 1 + j, (*chip, c), me).wait_recv()
                cp = copy(a, 4 + j, (*chip, c), sibling)
                cp.start()
                passed.append(cp)
        for a in range(n):
            copy(a, 0, sibling, me).wait_recv()
            for j, chip in enumerate(chips):
                copy(a, 4 + j, (*chip, 1 - c), me).wait_recv()
        for cp in first + passed:
            cp.wait_send()
        for cp in mine:
            cp.wait()

    return pl.pallas_call(
        body, name=name,
        out_shape=[jax.ShapeDtypeStruct((N_DEV,) + a.shape, a.dtype) for a in arrs],
        in_specs=[_ANY] * (n + len(deps)), out_specs=[_ANY] * n,
        scratch_shapes=[pltpu.SemaphoreType.DMA((7 * n,)), pltpu.SemaphoreType.DMA((7 * n,)),
                        pltpu.SemaphoreType.DMA((n,))],
    )(*arrs, *deps)


def in_proj_gathered(xs, g, w_own, extras, name, tm=512):
    s, d = xs.shape
    n = w_own.shape[1]
    arrs = [w_own] + list(extras)
    na = len(arrs)
    tr = 256

    def body(*refs):
        x_ref, g_ref, ins = refs[0], refs[1], refs[2:2 + na]
        h_out, p_ref, outs = refs[2 + na], refs[3 + na], refs[4 + na:4 + 2 * na]
        (h_ref, xbuf, wbuf, obuf, send_sems, recv_sems, load_sem, store_sems, own_sems, h_sem,
         x_sems) = refs[4 + 2 * na:]
        x, y, c = _place()
        me, sibling = (x, y, c), (x, y, 1 - c)
        x_first = c == 0
        near = (jnp.where(x_first, 1 - x, x), jnp.where(x_first, y, 1 - y))
        far = (jnp.where(x_first, x, 1 - x), jnp.where(x_first, 1 - y, y))
        diag = (1 - x, 1 - y)
        k_near, k_far = jnp.where(x_first, 1, 2), jnp.where(x_first, 2, 1)
        f_near, f_far = k_near + 3, k_far + 3

        def slot(block):
            return 4 * block[0] + 2 * block[1] + block[2]

        def copy(a, k, block, to, src=None):
            dst = outs[a].at[slot(block)]
            return pltpu.make_async_remote_copy(
                src_ref=dst if src is None else src, dst_ref=dst,
                send_sem=send_sems.at[7 * a + k], recv_sem=recv_sems.at[7 * a + k],
                device_id=to, device_id_type=MESH)

        first = []
        for a in range(na):
            first += [copy(a, 0, me, sibling, src=ins[a]), copy(a, 1, me, (1 - x, y, c), src=ins[a]),
                      copy(a, 2, me, (x, 1 - y, c), src=ins[a])]
        for cp in first:
            cp.start()
        own = pltpu.make_async_copy(wbuf.at[0], outs[0].at[slot(me)], own_sems.at[0])
        mine = [pltpu.make_async_copy(ins[a], outs[a].at[slot(me)], own_sems.at[a]) for a in range(1, na)]
        stores = [None, None]

        def x_load(i):
            return pltpu.make_async_copy(x_ref.at[pl.ds(i * tr, tr), :], xbuf.at[i % 2], x_sems.at[i % 2])

        x_load(0).start()
        for i in range(s // tr):
            if i + 1 < s // tr:
                x_load(i + 1).start()
            x_load(i).wait()
            xv = xbuf[i % 2]
            r = lax.rsqrt(jnp.mean(xv * xv, axis=-1, keepdims=True) + EPS)
            h_ref[i * tr:(i + 1) * tr, :] = (xv * r * g_ref[...]).astype(BF16)
        h_store = pltpu.make_async_copy(h_ref, h_out, h_sem)
        h_store.start()

        def multiply(k, block, w_from):
            b = k % 2
            if k == 2:
                own.wait()
            load = pltpu.make_async_copy(w_from, wbuf.at[b], load_sem)
            load.start()
            if stores[b] is not None:
                stores[b].wait()
            load.wait()
            if k == 0:
                own.start()

            def chunk(i, carry):
                rows = pl.ds(pl.multiple_of(i * tm, tm), tm)
                obuf[b, rows, :] = jnp.dot(h_ref[rows, :], wbuf[b], preferred_element_type=F32).astype(BF16)
                return carry

            lax.fori_loop(0, s // tm, chunk, 0)
            stores[b] = pltpu.make_async_copy(
                obuf.at[b], p_ref.at[:, pl.ds(pl.multiple_of(slot(block) * n, 128), n)], store_sems.at[b])
            stores[b].start()

        passed = []

        def arrive(a, k, block):
            copy(a, k, block, me).wait_recv()

        def pass_on(a, k, block, to):
            cp = copy(a, k, block, to)
            cp.start()
            passed.append(cp)

        def gather(a, use):
            use(0, me)
            arrive(a, 0, sibling)
            use(1, sibling)
            arrive(a, k_near, (*near, c))
            pass_on(a, 3, (*near, c), (*far, c))
            pass_on(a, f_near, (*near, c), sibling)
            use(2, (*near, c))
            arrive(a, f_far, (*far, 1 - c))
            use(3, (*far, 1 - c))
            arrive(a, k_far, (*far, c))
            pass_on(a, f_far, (*far, c), sibling)
            use(4, (*far, c))
            arrive(a, f_near, (*near, 1 - c))
            use(5, (*near, 1 - c))
            arrive(a, 3, (*diag, c))
            pass_on(a, 6, (*diag, c), sibling)
            use(6, (*diag, c))
            arrive(a, 6, (*diag, 1 - c))
            use(7, (*diag, 1 - c))

        gather(0, lambda k, block: multiply(k, block, ins[0] if k == 0 else outs[0].at[slot(block)]))
        for cp in mine:
            cp.start()
        for a in range(1, na):
            gather(a, lambda k, block: None)
        for cp in first + passed:
            cp.wait_send()
        for cp in mine + stores + [h_store]:
            cp.wait()

    vmem = pl.BlockSpec(memory_space=pltpu.VMEM)
    outs = pl.pallas_call(
        body, name=name,
        out_shape=[jax.ShapeDtypeStruct((s, d), BF16), jax.ShapeDtypeStruct((s, N_DEV * n), BF16)]
        + [jax.ShapeDtypeStruct((N_DEV,) + a.shape, a.dtype) for a in arrs],
        in_specs=[_ANY, vmem] + [_ANY] * na, out_specs=[_ANY] * (2 + na),
        scratch_shapes=[pltpu.VMEM((s, d), BF16), pltpu.VMEM((2, tr, d), F32), pltpu.VMEM((2, d, n), BF16),
                        pltpu.VMEM((2, s, n), BF16),
                        pltpu.SemaphoreType.DMA((7 * na,)), pltpu.SemaphoreType.DMA((7 * na,)),
                        pltpu.SemaphoreType.DMA, pltpu.SemaphoreType.DMA((2,)), pltpu.SemaphoreType.DMA((na,)),
                        pltpu.SemaphoreType.DMA, pltpu.SemaphoreType.DMA((2,))],
        compiler_params=pltpu.CompilerParams(vmem_limit_bytes=VMEM_LIMIT),
    )(xs, g, *arrs)
    return outs[0], outs[1], outs[2], outs[3:]


_HBM = pl.BlockSpec(memory_space=pltpu.HBM)
_SEM = pl.BlockSpec(memory_space=pltpu.SEMAPHORE)
_DATAFLOW = pltpu.SideEffectType.DATAFLOW_SIDE_EFFECTING


def _peers_per_array(kind):
    return 1 if kind in ("sibling", "halves") else 3


def _split_copies(kind, srcs, lands, send_sems, recv_sems):
    x, y, c = _place()
    per = _peers_per_array(kind)
    out = []
    for a in range(len(lands)):
        if kind == "sibling":
            part = srcs[a] if srcs[a].shape[1] == 1 else srcs[a].at[:, pl.ds(1 - c, 1)]
            peers = [((x, y, 1 - c), part, lands[a], lands[a])]
        elif kind == "halves":
            mine, its = lands[a].at[:, pl.ds(c, 1)], lands[a].at[:, pl.ds(1 - c, 1)]
            peers = [((x, y, 1 - c), mine, mine, its)]
        else:
            peers = []
            for px, py in [(1 - x, y), (x, 1 - y), (1 - x, 1 - y)]:
                if kind == "gather":
                    views = (srcs[a], lands[a].at[4 * x + 2 * y + c], lands[a].at[4 * px + 2 * py + c])
                else:
                    views = (srcs[a].at[2 * px + py], lands[a].at[2 * x + y], lands[a].at[2 * px + py])
                peers.append(((px, py, c),) + views)
        for j, (peer, src, dst, arrives) in enumerate(peers):
            sems = dict(send_sem=send_sems.at[per * a + j], recv_sem=recv_sems.at[per * a + j],
                        device_id=peer, device_id_type=MESH)
            out.append((pltpu.make_async_remote_copy(src_ref=src, dst_ref=dst, **sems),
                        pltpu.make_async_remote_copy(src_ref=src, dst_ref=arrives, **sems)))
    return out


def split_start(kind, srcs, lands, deps, name):
    ns, nl = len(srcs), len(lands)
    n_sems = _peers_per_array(kind) * nl
    held = list(srcs) + list(lands)

    def body(*refs):
        send_sems, recv_sems = refs[len(held) + len(deps)], refs[len(held) + len(deps) + 1]
        for copy, _ in _split_copies(kind, refs[:ns], refs[ns:ns + nl], send_sems, recv_sems):
            copy.start()
        token = refs[-1]
        token[...] = jnp.zeros_like(token)

    outs = pl.pallas_call(
        body, name=name,
        out_shape=(pltpu.SemaphoreType.DMA((n_sems,)), pltpu.SemaphoreType.DMA((n_sems,)),
                   *[pltpu.HBM(a.shape, a.dtype) for a in held], jax.ShapeDtypeStruct((8, 128), F32)),
        in_specs=[_HBM] * len(held) + [_ANY] * len(deps),
        out_specs=(_SEM, _SEM, *([_HBM] * len(held)), pl.BlockSpec(memory_space=pltpu.VMEM)),
        input_output_aliases={i: 2 + i for i in range(len(held))},
        compiler_params=pltpu.CompilerParams(has_side_effects=_DATAFLOW),
    )(*[pltpu.with_memory_space_constraint(a, pltpu.HBM) for a in held], *deps)
    return outs[0], outs[1], list(outs[2:2 + ns]), list(outs[2 + ns:2 + ns + nl]), outs[-1]


def split_wait(kind, send_sems, recv_sems, srcs, lands, afters, name):
    ns, nl = len(srcs), len(lands)
    held = list(srcs) + list(lands)

    def body(*refs):
        for _, arrival in _split_copies(kind, refs[:ns], refs[ns:ns + nl], refs[ns + nl], refs[ns + nl + 1]):
            arrival.wait_send()
            arrival.wait_recv()

    outs = pl.pallas_call(
        body, name=name,
        out_shape=[pltpu.HBM(a.shape, a.dtype) for a in held],
        in_specs=[_HBM] * len(held) + [_SEM, _SEM] + [_ANY] * len(afters),
        out_specs=[_HBM] * len(held),
        input_output_aliases={i: i for i in range(len(held))},
        compiler_params=pltpu.CompilerParams(has_side_effects=_DATAFLOW),
    )(*held, send_sems, recv_sems, *afters)
    return list(outs[:ns]), list(outs[ns:])


def place_block(land, block, dev, name):
    r, c = block.shape
    tr = min(r, 512)

    def body(dev_ref, land_ref, b_ref, o_ref):
        del dev_ref, land_ref
        o_ref[...] = b_ref[...]

    return pl.pallas_call(
        body, name=name,
        grid_spec=pltpu.PrefetchScalarGridSpec(
            num_scalar_prefetch=1, grid=(r // tr,),
            in_specs=[_ANY, pl.BlockSpec((tr, c), lambda i, dev_ref: (i, 0))],
            out_specs=pl.BlockSpec((None, tr, c), lambda i, dev_ref: (dev_ref[0], i, 0))),
        out_shape=jax.ShapeDtypeStruct(land.shape, land.dtype),
        input_output_aliases={1: 0},
        compiler_params=_params("parallel"),
    )(dev, land, block)


def pair_add(own, recv, core, name):
    _, _, r, c = own.shape
    tr = min(r, 512)

    def body(core_ref, own_ref, recv_ref, o_ref):
        del core_ref
        o_ref[...] = (own_ref[...].astype(F32) + recv_ref[...].astype(F32)).astype(BF16)

    return pl.pallas_call(
        body, name=name,
        grid_spec=pltpu.PrefetchScalarGridSpec(
            num_scalar_prefetch=1, grid=(4, r // tr),
            in_specs=[pl.BlockSpec((None, None, tr, c), lambda k, i, core_ref: (k, core_ref[0], i, 0)),
                      pl.BlockSpec((None, None, tr, c), lambda k, i, core_ref: (k, 0, i, 0))],
            out_specs=pl.BlockSpec((None, tr, c), lambda k, i, core_ref: (k, i, 0))),
        out_shape=jax.ShapeDtypeStruct((4, r, c), BF16),
        compiler_params=_params("parallel", "parallel"),
    )(core, own, recv)


def _adamw_math(w, g, m, v):
    m2 = ADAM_B1 * m + (1.0 - ADAM_B1) * g
    v2 = ADAM_B2 * v + (1.0 - ADAM_B2) * (g * g)
    m_hat = m2 / (1.0 - ADAM_B1 ** ADAM_STEP)
    v_hat = v2 / (1.0 - ADAM_B2 ** ADAM_STEP)
    delta = -ADAM_LR * (m_hat / (jnp.sqrt(v_hat) + ADAM_EPS) + ADAM_WD * w)
    return delta, m2, v2


def adamw_big(w, m, v, own, got, chip, name):
    r, c = w.shape
    tr = min(r, 256)

    def body(chip_ref, w_ref, m_ref, v_ref, p0, p1, p2, p3, g_ref, d_ref, m2_ref, v2_ref):
        del chip_ref
        g = ((p0[...].astype(F32) + p1[...].astype(F32)) + p2[...].astype(F32)) + p3[...].astype(F32)
        delta, m2, v2 = _adamw_math(w_ref[...], g, m_ref[...], v_ref[...])
        g_ref[...] = g
        d_ref[...] = delta
        m2_ref[...] = m2
        v2_ref[...] = v2

    row = pl.BlockSpec((tr, c), lambda i, chip_ref: (i, 0))

    def slab(flip):
        return pl.BlockSpec((None, tr, c), lambda i, chip_ref: (chip_ref[0] ^ flip, i, 0))

    return pl.pallas_call(
        body, name=name,
        grid_spec=pltpu.PrefetchScalarGridSpec(
            num_scalar_prefetch=1, grid=(r // tr,),
            in_specs=[row, row, row, slab(0), slab(1), slab(2), slab(3)],
            out_specs=[row] * 4),
        out_shape=[jax.ShapeDtypeStruct((r, c), F32)] * 4,
        compiler_params=_params("parallel"),
    )(chip, w, m, v, own, got, got, got)


def sum_devices(g8, name):
    def body(g_ref, o_ref):
        tot = g_ref[0]
        for k in range(1, N_DEV):
            tot = tot + g_ref[k]
        o_ref[...] = tot

    return pl.pallas_call(body, name=name, out_shape=jax.ShapeDtypeStruct(g8.shape[1:], F32))(g8)


def adamw_small(ws, gs, ms, vs, name):
    n = len(ws)

    def body(*refs):
        w_r, g_r, m_r, v_r = refs[:n], refs[n:2 * n], refs[2 * n:3 * n], refs[3 * n:4 * n]
        d_o, m_o, v_o = refs[4 * n:5 * n], refs[5 * n:6 * n], refs[6 * n:7 * n]
        for k in range(n):
            delta, m2, v2 = _adamw_math(w_r[k][...], g_r[k][...], m_r[k][...], v_r[k][...])
            d_o[k][...] = delta
            m_o[k][...] = m2
            v_o[k][...] = v2

    shapes = [jax.ShapeDtypeStruct(w.shape, F32) for w in ws]
    outs = pl.pallas_call(body, name=name, out_shape=shapes * 3)(*ws, *gs, *ms, *vs)
    return outs[:n], outs[n:2 * n], outs[2 * n:]


def _rows128(a):
    return a.reshape(-1, 128)


def _pad_rows(a, rows):
    return jnp.pad(a, ((0, rows - a.shape[0]), (0, 0)))


def kernel(x, ln_pre_even, w_in_even, pool_w, pool_scale, w_out_even, ln_post_even, ln_pre_odd, w_in_odd, sconv_w, dconv_w, dconv_b, cnorm_g, cnorm_b, w_out_odd, ln_post_odd, loss_target, m_ln_pre_even, m_w_in_even, m_pool_w, m_pool_scale, m_w_out_even, m_ln_post_even, m_ln_pre_odd, m_w_in_odd, m_sconv_w, m_dconv_w, m_dconv_b, m_cnorm_g, m_cnorm_b, m_w_out_odd, m_ln_post_odd, v_ln_pre_even, v_w_in_even, v_pool_w, v_pool_scale, v_w_out_even, v_ln_post_even, v_ln_pre_odd, v_w_in_odd, v_sconv_w, v_dconv_w, v_dconv_b, v_cnorm_g, v_cnorm_b, v_w_out_odd, v_ln_post_odd):
    xs = x[0]
    tgt = loss_target[0]
    s, d = xs.shape
    half = d // 2
    n_heads = half // HEAD_DIM
    ng = len(POOL_WINDOWS)
    cwp = half // ng
    dev = 4 * lax.axis_index("x") + 2 * lax.axis_index("y") + lax.axis_index("c")
    core = lax.axis_index("c").astype(jnp.int32).reshape(1)

    pr = pool_w.shape[2]
    cl = sconv_w.shape[2]
    small_parts = [(_rows128(ln_pre_odd), 8), (sconv_w[0], 8), (dconv_w[0], 32), (dconv_b, 8),
                   (cnorm_g, 8), (cnorm_b, 8), (_rows128(ln_post_odd), 8)]
    small_local = jnp.concatenate([_pad_rows(a, r) for a, r in small_parts], axis=0)
    h0, p0, g_wie, (g_pw, g_small) = in_proj_gathered(
        xs, ln_pre_even, w_in_even[0].astype(BF16), [pool_w[0].reshape(ng * pr, cwp).astype(BF16), small_local],
        "ag_in_proj_even")
    comm = _Exchanges(dev, core, d)
    token = comm.start_weights("out_even", [w_out_even[0].astype(BF16)], [p0])
    sb_dep = comm.start_weights("odd", [w_in_odd[0].astype(BF16), w_out_odd[0].astype(BF16)], [token])
    pool_full = g_pw.reshape(N_DEV, ng, pr, cwp).transpose(1, 0, 2, 3).reshape(ng, cwp, cwp)
    nl = ln_pre_odd.shape[1] // 128

    def chan(lo, rows):
        return g_small[:, lo:lo + rows].transpose(1, 0, 2).reshape(rows, N_DEV * cl)

    ln_pre_odd_f = g_small[:, 0:nl].reshape(1, d)
    sconv_f = chan(8, SCONV_K)
    dconv_f = chan(16, CONF_K)
    dconv_b_f = chan(48, 1)
    cnorm_g_f = chan(56, 1)
    cnorm_b_f = chan(64, 1)
    ln_post_odd_f = g_small[:, 72:72 + nl].reshape(1, d)

    loss_blk, grad_x, small_g = _fwd_bwd(
        xs, tgt, ln_pre_even, h0, p0, g_wie, pool_full, pool_scale, ln_post_even, ln_pre_odd_f,
        sconv_f, dconv_f, dconv_b_f, cnorm_g_f, cnorm_b_f, ln_post_odd_f, comm, sb_dep)
    small_w = [ln_pre_even, pool_scale, ln_post_even, ln_pre_odd, sconv_w[0], dconv_w[0], dconv_b, cnorm_g, cnorm_b, ln_post_odd]
    small_m = [m_ln_pre_even, m_pool_scale, m_ln_post_even, m_ln_pre_odd, m_sconv_w[0], m_dconv_w[0], m_dconv_b, m_cnorm_g, m_cnorm_b, m_ln_post_odd]
    small_v = [v_ln_pre_even, v_pool_scale, v_ln_post_even, v_ln_pre_odd, v_sconv_w[0], v_dconv_w[0], v_dconv_b, v_cnorm_g, v_cnorm_b, v_ln_post_odd]
    big = {"w_in_even": (w_in_even, m_w_in_even, v_w_in_even), "pool_w": (pool_w, m_pool_w, v_pool_w),
           "w_out_even": (w_out_even, m_w_out_even, v_w_out_even), "w_in_odd": (w_in_odd, m_w_in_odd, v_w_in_odd),
           "w_out_odd": (w_out_odd, m_w_out_odd, v_w_out_odd)}
    upd = comm.finish_updates(big, [grad_x])
    upd.update(comm.finish_updates(big, [grad_x]))
    sg, sd, sm, sv, loss = _update_small(small_g, loss_blk, small_w, small_m, small_v, dev, d, cl,
                                         deps=[upd["w_in_odd"][1], upd["w_out_even"][1]])
    upd.update(comm.finish_updates(big, sd))
    (g_wie_o, d_wie, m_wie, v_wie), (g_pw_o, d_pw, m_pw, v_pw) = upd["w_in_even"], upd["pool_w"]
    (g_woe_o, d_woe, m_woe, v_woe), (g_wio_o, d_wio, m_wio, v_wio) = upd["w_out_even"], upd["w_in_odd"]
    g_woo_o, d_woo, m_woo, v_woo = upd["w_out_odd"]

    def order(small, wie, pw, woe, wio, woo):
        return [small[0], wie, pw, small[1], woe, small[2], small[3], wio, small[4], small[5], small[6],
                small[7], small[8], woo, small[9]]

    grads = order(sg, g_wie_o, g_pw_o, g_woe_o, g_wio_o, g_woo_o)
    deltas = order(sd, d_wie, d_pw, d_woe, d_wio, d_woo)
    new_m = order(sm, m_wie, m_pw, m_woe, m_wio, m_woo)
    new_v = order(sv, v_wie, v_pw, v_woe, v_wio, v_woo)
    return (loss, grad_x[None], *grads, *deltas, *new_m, *new_v)


def _fwd_bwd(xs, tgt, ln_pre_even, h0, p0, g_wie, pool_full, pool_scale, ln_post_even, ln_pre_odd_f,
             sconv_f, dconv_f, dconv_b_f, cnorm_g_f, cnorm_b_f, ln_post_odd_f, comm, sb_dep):
    d = xs.shape[1]
    n_heads = d // 2 // HEAD_DIM
    ng, cwp = pool_full.shape[0], pool_full.shape[1]
    a0, sb_wts = sb_fwd(p0, n_heads, "sb_fwd", dep=sb_dep)
    dep = comm.weights_arrived("out_even", after=a0)
    y0 = even_mix_fwd(a0, p0, pool_full, pool_scale, "even_mix_fwd", dep=dep)
    (w_out_e,) = comm.weights("out_even", after=y0)
    w_out_e = w_out_e.reshape(1, d, d)
    o0 = mm_nn(y0, w_out_e, F32, "out_proj_even", tm=1024)
    dep = comm.weights_arrived("odd", after=o0)
    x1, h1 = postnorm_fwd(xs, o0, ln_post_even, ln_pre_odd_f, "post_even", dep=dep)
    g_wio, w_out_o = comm.weights("odd", after=x1)
    w_out_o = w_out_o.reshape(1, d, d)
    p1 = mm_nn(h1, g_wio, BF16, "in_proj_odd")
    y1, dc = odd_mix_fwd(p1, sconv_f, dconv_f, dconv_b_f, cnorm_g_f, cnorm_b_f, "odd_mix_fwd")
    o1 = mm_nn(y1, w_out_o, F32, "out_proj_odd", tm=1024)
    loss_blk, gx2, do1, dg_post_odd = final_fwd_bwd(x1, o1, ln_post_odd_f, tgt, "post_odd_loss")

    dw_out_o = mm_tn(y1, do1, 1, BF16, "dw_out_odd")
    dy1 = mm_nt(do1, w_out_o, BF16, "dy_odd")
    ddc, dg2, dgam, dbet = odd_bwd_ln(dy1, p1, dc, cnorm_g_f, cnorm_b_f, "odd_bwd_ln")
    dp1, dsconv, ddconv, ddconv_b = odd_bwd_conv(dy1, p1, ddc, dg2, sconv_f, dconv_f, "odd_bwd_conv")
    dw_in_o = mm_tn(h1, dp1, N_DEV, BF16, "dw_in_odd")
    dep = comm.reduce_begin({"w_out_odd": dw_out_o.reshape(N_DEV, d // N_DEV, d), "w_in_odd": dw_in_o}, "odd")
    dh1 = mm_nt(dp1, g_wio, F32, "dh_odd", dep=dep)
    dep = comm.reduce_send(after=dh1)
    gx1, dg_pre_odd, do0, dg_post_even = norm_bwd(dh1, x1, ln_pre_odd_f, gx2, "pre_odd_post_even_bwd",
                                                  inp2=o0, g2=ln_post_even, dep=dep)

    dw_out_e = mm_tn(y0, do0, 1, BF16, "dw_out_even")
    dy0 = mm_nt(do0, w_out_e, BF16, "dy_even")
    da0, du0, dg0, dpool, dpool_scale = even_mix_bwd(dy0, a0, p0, pool_full, pool_scale, "even_mix_bwd")
    pr = cwp // N_DEV
    dpool_slabs = dpool.astype(BF16).reshape(ng, N_DEV, pr, cwp).transpose(1, 0, 2, 3).reshape(N_DEV, ng * pr, cwp)
    dep = comm.reduce_begin({"w_out_even": dw_out_e.reshape(N_DEV, d // N_DEV, d), "pool_w": dpool_slabs}, "even_out")
    dq0, dk0, dv0 = sb_bwd(p0, a0, sb_wts, da0, n_heads, "sb_bwd", dep=dep)
    dep = comm.reduce_send(after=dq0)
    dp0 = jnp.concatenate([dq0, dk0, dv0, du0, dg0], axis=1)
    dw_sibling = mm_tn(h0, dp0, N_DEV // 2, BF16, "dw_in_even_sibling", dep=dep, pick=(2, 1 - comm.core))
    dep = comm.reduce_begin({"w_in_even": dw_sibling}, "even_in", sibling_part=True)
    dw_own = mm_tn(h0, dp0, N_DEV // 2, BF16, "dw_in_even_own", dep=dep, pick=(2, comm.core))
    dep = comm.reduce_send(after=dw_own, own_part={"w_in_even": dw_own})
    dh0 = mm_nt(dp0, g_wie, F32, "dh_even", dep=dep)
    dep = None
    grad_x, dg_pre_even = norm_bwd(dh0, xs, ln_pre_even, gx1, "pre_even_bwd", dep=dep)
    small_g = [dg_pre_even, dpool_scale, dg_post_even, dg_pre_odd, dsconv, ddconv, ddconv_b, dgam, dbet, dg_post_odd]
    return loss_blk, grad_x, small_g


class _Exchanges:
    def __init__(self, dev, core, d):
        self.dev = dev.astype(jnp.int32).reshape(1)
        self.core = core
        self.chip = (dev // 2).astype(jnp.int32).reshape(1)
        self.d = d
        self.in_flight = {}
        self.to_sibling = None
        self.pending = []

    def start_weights(self, tag, blocks, afters):
        lands = [lax.empty((N_DEV,) + b.shape, b.dtype) for b in blocks]
        send, recv, srcs, lands, token = split_start("gather", blocks, lands, afters, "ag_start_" + tag)
        self.in_flight[tag] = (send, recv, srcs, lands)
        return token

    def weights_arrived(self, tag, after):
        send, recv, srcs, lands = self.in_flight.pop(tag)
        srcs, lands = split_wait("gather", send, recv, srcs, lands, [after], "ag_wait_" + tag)
        lands = [place_block(l, b, self.dev, "ag_own_%s_%d" % (tag, k)) for k, (l, b) in enumerate(zip(lands, srcs))]
        lands = [l.reshape((4, 2) + l.shape[1:]) for l in lands]
        send, recv, _, lands, token = split_start("halves", [], lands, [], "ag_sibling_start_" + tag)
        self.in_flight[tag] = (send, recv, lands)
        return token

    def weights(self, tag, after):
        send, recv, lands = self.in_flight.pop(tag)
        _, lands = split_wait("halves", send, recv, [], lands, [after], "ag_sibling_wait_" + tag)
        return [l.reshape((N_DEV,) + l.shape[2:]) for l in lands]

    def reduce_begin(self, partials, tag, sibling_part=False):
        names = list(partials)
        arrs = [partials[k].reshape((4, 1 if sibling_part else 2) + partials[k].shape[1:]) for k in names]
        lands = [lax.empty((4, 1) + a.shape[2:], a.dtype) for a in arrs]
        send, recv, srcs, lands, token = split_start("sibling", arrs, lands, [], "rs_sibling_start_" + tag)
        self.to_sibling = (tag, names, send, recv, srcs, lands)
        return token

    def reduce_send(self, after, own_part=None):
        tag, names, send, recv, srcs, lands = self.to_sibling
        srcs, lands = split_wait("sibling", send, recv, srcs, lands, [after], "rs_sibling_wait_" + tag)
        which = self.core
        if own_part is not None:
            srcs = [own_part[k].reshape((4, 1) + own_part[k].shape[1:]) for k in names]
            which = jnp.zeros((1,), jnp.int32)
        sums = [pair_add(o, r, which, "rs_pair_add_" + k) for k, o, r in zip(names, srcs, lands)]
        zones = [lax.empty(a.shape, a.dtype) for a in sums]
        send, recv, srcs, zones, token = split_start("scatter", sums, zones, [], "rs_start_" + tag)
        self.pending.append((tag, names, send, recv, srcs, zones))
        return token

    def finish_updates(self, big, afters):
        tag, names, send, recv, srcs, lands = self.pending.pop(0)
        srcs, lands = split_wait("scatter", send, recv, srcs, lands, afters, "rs_wait_" + tag)
        out = {}
        for name, own, got in zip(names, srcs, lands):
            w, m, v = big[name]
            shp = own.shape[1:]
            outs = adamw_big(w.reshape(shp), m.reshape(shp), v.reshape(shp), own, got, self.chip, "adamw_" + name)
            out[name] = [o.reshape(w.shape) for o in outs]
        return out


def _update_small(small_g, loss_blk, small_w, small_m, small_v, dev, d, cl, deps):
    packed = jnp.concatenate([_rows128(g) for g in small_g] + [loss_blk], axis=0)
    (g8,) = all_gather([packed], "ag_small_grads", deps)
    tot = sum_devices(g8, "sum_small_grads")
    loss = tot[packed.shape[0] - 8, 0]
    full_g = []
    lo = 0
    for g in small_g:
        rows = g.size // 128
        full_g.append(tot[lo:lo + rows].reshape(g.shape))
        lo += rows

    def mine(g, width):
        return lax.dynamic_slice_in_dim(g, dev * width, width, axis=g.ndim - 1)

    fg = full_g
    small_gl = [fg[0], fg[1], fg[2], mine(fg[3], d // N_DEV), mine(fg[4], cl), mine(fg[5], cl), mine(fg[6], cl),
                mine(fg[7], cl), mine(fg[8], cl), mine(fg[9], d // N_DEV)]
    sd, sm, sv = adamw_small(small_w, small_gl, small_m, small_v, "adamw_small")

    def like(k, a):
        return a[None] if k in (4, 5) else a

    sg = [like(k, a) for k, a in enumerate(small_gl)]
    sd = [like(k, a) for k, a in enumerate(sd)]
    sm = [like(k, a) for k, a in enumerate(sm)]
    sv = [like(k, a) for k, a in enumerate(sv)]
    return sg, sd, sm, sv, loss
```

```python
import functools
import math

import jax
import jax.numpy as jnp
from jax import lax
from jax.experimental import pallas as pl
from jax.experimental.pallas import tpu as pltpu

F32 = jnp.float32
BF16 = jnp.bfloat16
EPS = 1e-6
HEAD_DIM = 128
POOL_WINDOWS = (2, 4, 8, 16)
SCONV_K = 3
CONF_K = 31
HALO = 32
N_DEV = 8
VMEM_LIMIT = 56 * 1024 * 1024
MESH = pl.DeviceIdType.MESH

ADAM_LR = 0.001
ADAM_B1 = 0.9
ADAM_B2 = 0.999
ADAM_EPS = 1e-08
ADAM_WD = 0.01
ADAM_STEP = 10


def _params(*sem):
    return pltpu.CompilerParams(dimension_semantics=sem, vmem_limit_bytes=VMEM_LIMIT)


def _sigmoid(v):
    return 1.0 / (1.0 + jnp.exp(-v))


def _silu(v):
    return v * _sigmoid(v)


def _silu_and_grad(v):
    s = _sigmoid(v)
    return v * s, s * (1.0 + v * (1.0 - s))


def _rowsum8(v):
    r, c = v.shape
    return jnp.sum(v.reshape(r // 8, 8, c), axis=0)


SUBLANES = 8


class _Taps:
    def __init__(self, xx, rows, before):
        self.xx, self.rows, self.before, self.rotated = xx, rows, before, {}

    def __call__(self, i):
        r, q = i % SUBLANES, i // SUBLANES
        if r not in self.rotated:
            n = self.xx.shape[0]
            self.rotated[r] = self.xx if r == 0 else pltpu.roll(self.xx, r if self.before else n - r, 0)
        lo = HALO - SUBLANES * q if self.before else SUBLANES * q
        return self.rotated[r][lo:lo + self.rows]


def _window_sum(xx, win, before):
    n = xx.shape[0]
    acc = xx
    k = 1
    while k < win:
        acc = acc + pltpu.roll(acc, k if before else n - k, 0)
        k *= 2
    return acc


def postnorm_fwd(x, o, g, g_next, name, tm=256, dep=None):
    s, d = x.shape
    dep_args, dep_specs = _after(dep)

    def body(x_ref, o_ref, g_ref, gn_ref, *rest):
        y_ref, h_ref = rest[-2:]
        ov = o_ref[...]
        r = lax.rsqrt(jnp.mean(ov * ov, axis=-1, keepdims=True) + EPS)
        y = x_ref[...] + ov * r * g_ref[...]
        y_ref[...] = y
        r2 = lax.rsqrt(jnp.mean(y * y, axis=-1, keepdims=True) + EPS)
        h_ref[...] = (y * r2 * gn_ref[...]).astype(BF16)

    row = pl.BlockSpec((tm, d), lambda i: (i, 0))
    vec = pl.BlockSpec((1, d), lambda i: (0, 0))
    return pl.pallas_call(
        body, name=name, grid=(s // tm,),
        in_specs=[row, row, vec, vec] + dep_specs, out_specs=[row, row],
        out_shape=[jax.ShapeDtypeStruct((s, d), F32), jax.ShapeDtypeStruct((s, d), BF16)],
        compiler_params=_params("parallel"),
    )(x, o, g, g_next, *dep_args)


def final_fwd_bwd(x1, o, g, target, name, tm=256):
    s, d = x1.shape
    n = s // tm

    def body(x_ref, o_ref, g_ref, t_ref, loss_ref, gx_ref, do_ref, dg_ref, lacc, gacc):
        i = pl.program_id(0)

        @pl.when(i == 0)
        def _():
            lacc[...] = jnp.zeros_like(lacc)
            gacc[...] = jnp.zeros_like(gacc)

        ov = o_ref[...]
        gv = g_ref[...]
        r = lax.rsqrt(jnp.mean(ov * ov, axis=-1, keepdims=True) + EPS)
        oh = ov * r
        diff = x_ref[...] + oh * gv - t_ref[...]
        lacc[...] += _rowsum8(diff * diff)
        gx = diff * (1.0 / d)
        gx_ref[...] = gx
        gacc[...] += _rowsum8(gx * oh)
        dn = gx * gv
        do_ref[...] = (r * (dn - oh * jnp.mean(dn * oh, axis=-1, keepdims=True))).astype(BF16)

        @pl.when(i == n - 1)
        def _():
            tot = jnp.sum(jnp.sum(lacc[...], axis=0, keepdims=True), axis=1, keepdims=True)
            loss_ref[...] = jnp.broadcast_to(tot * (0.5 / d), loss_ref.shape)
            dg_ref[...] = jnp.sum(gacc[...], axis=0, keepdims=True)

    row = pl.BlockSpec((tm, d), lambda i: (i, 0))
    vec = pl.BlockSpec((1, d), lambda i: (0, 0))
    return pl.pallas_call(
        body, name=name, grid=(n,),
        in_specs=[row, row, vec, row],
        out_specs=[pl.BlockSpec((8, 128), lambda i: (0, 0)), row, row, vec],
        out_shape=[jax.ShapeDtypeStruct((8, 128), F32), jax.ShapeDtypeStruct((s, d), F32),
                   jax.ShapeDtypeStruct((s, d), BF16), jax.ShapeDtypeStruct((1, d), F32)],
        scratch_shapes=[pltpu.VMEM((8, d), F32), pltpu.VMEM((8, d), F32)],
        compiler_params=_params("arbitrary"),
    )(x1, o, g, target)


def _rms_bwd_rows(dyv, xv, gv):
    r = lax.rsqrt(jnp.mean(xv * xv, axis=-1, keepdims=True) + EPS)
    xh = xv * r
    dn = dyv * gv
    return r * (dn - xh * jnp.mean(dn * xh, axis=-1, keepdims=True)), _rowsum8(dyv * xh)


def norm_bwd(dy, inp, g, resid, name, inp2=None, g2=None, tm=256, dep=None):
    s, d = inp.shape
    n = s // tm
    chain = inp2 is not None

    def body(*refs):
        dy_ref, x_ref, g_ref, r_ref = refs[:4]
        outs = refs[-6:] if chain else refs[-3:]
        i = pl.program_id(0)

        @pl.when(i == 0)
        def _():
            for acc in outs[-2:] if chain else outs[-1:]:
                acc[...] = jnp.zeros_like(acc)

        if chain:
            x2_ref, g2_ref = refs[4:6]
            dx_ref, dg_ref, dx2_ref, dg2_ref, gacc, gacc2 = outs
        else:
            dx_ref, dg_ref, gacc = outs
        dx, dg_rows = _rms_bwd_rows(dy_ref[...].astype(F32), x_ref[...], g_ref[...])
        dx = dx + r_ref[...]
        dx_ref[...] = dx
        gacc[...] += dg_rows
        if chain:
            dx2, dg2_rows = _rms_bwd_rows(dx, x2_ref[...], g2_ref[...])
            dx2_ref[...] = dx2.astype(BF16)
            gacc2[...] += dg2_rows

        @pl.when(i == n - 1)
        def _():
            dg_ref[...] = jnp.sum(gacc[...], axis=0, keepdims=True)
            if chain:
                dg2_ref[...] = jnp.sum(gacc2[...], axis=0, keepdims=True)

    row = pl.BlockSpec((tm, d), lambda i: (i, 0))
    vec = pl.BlockSpec((1, d), lambda i: (0, 0))
    dep_args, dep_specs = _after(dep)
    extra = [inp2, g2] if chain else []
    return pl.pallas_call(
        body, name=name, grid=(n,),
        in_specs=[row, row, vec, row] + ([row, vec] if chain else []) + dep_specs,
        out_specs=[row, vec] * (2 if chain else 1),
        out_shape=[jax.ShapeDtypeStruct((s, d), F32), jax.ShapeDtypeStruct((1, d), F32)]
        + ([jax.ShapeDtypeStruct((s, d), BF16), jax.ShapeDtypeStruct((1, d), F32)] if chain else []),
        scratch_shapes=[pltpu.VMEM((8, d), F32)] * (2 if chain else 1),
        compiler_params=_params("arbitrary"),
    )(dy, inp, g, resid, *extra, *dep_args)


def _after(dep):
    if dep is None:
        return [], []
    return [dep], [pl.BlockSpec((8, 128), lambda *_: (0, 0))]


def mm_nn(a, w, out_dtype, name, tm=2048, tn=None, dep=None):
    m, k = a.shape
    tm = min(tm, m)
    ns, _, n = w.shape
    tn = n if tn is None else tn
    nj = n // tn
    dep_args, dep_specs = _after(dep)

    def body(a_ref, w_ref, *rest):
        o_ref = rest[-1]
        o_ref[...] = jnp.dot(a_ref[...], w_ref[0], preferred_element_type=F32).astype(out_dtype)

    return pl.pallas_call(
        body, name=name, grid=(ns, nj, m // tm),
        in_specs=[pl.BlockSpec((tm, k), lambda s, j, i: (i, 0)),
                  pl.BlockSpec((1, k, tn), lambda s, j, i: (s, 0, j))] + dep_specs,
        out_specs=pl.BlockSpec((tm, tn), lambda s, j, i: (i, s * nj + j)),
        out_shape=jax.ShapeDtypeStruct((m, ns * n), out_dtype),
        compiler_params=_params("parallel", "parallel", "parallel"),
    )(a, w, *dep_args)


def mm_nt(a, w, out_dtype, name, tm=1024, tn=None, dep=None):
    m = a.shape[0]
    tm = min(tm, m)
    ns, k, n = w.shape
    tn = n if tn is None else tn
    nj = n // tn
    steps = ns * nj
    dep_args, dep_specs = _after(dep)

    def body(a_ref, w_ref, *rest):
        o_ref, acc = rest[-2:]
        r = pl.program_id(1)

        @pl.when(r == 0)
        def _():
            acc[...] = jnp.zeros_like(acc)

        acc[...] += lax.dot_general(a_ref[...], w_ref[0], (((1,), (1,)), ((), ())),
                                    preferred_element_type=F32)

        @pl.when(r == steps - 1)
        def _():
            o_ref[...] = acc[...].astype(out_dtype)

    return pl.pallas_call(
        body, name=name, grid=(m // tm, steps),
        in_specs=[pl.BlockSpec((tm, tn), lambda i, r: (i, r)),
                  pl.BlockSpec((1, k, tn), lambda i, r: (r // nj, 0, r % nj))] + dep_specs,
        out_specs=pl.BlockSpec((tm, k), lambda i, r: (i, 0)),
        out_shape=jax.ShapeDtypeStruct((m, k), out_dtype),
        scratch_shapes=[pltpu.VMEM((tm, k), F32)],
        compiler_params=_params("parallel", "arbitrary"),
    )(a, w, *dep_args)


def mm_tn(a, b, ns, out_dtype, name, tk=1024, tm=2048, dep=None, pick=None):
    m, k = a.shape
    tm = min(tm, m)
    step, offset = (1, None) if pick is None else pick
    n = b.shape[1] // (ns * step)
    steps = m // tm
    dep_args, dep_specs = _after(dep)
    n_pre = 0 if pick is None else 1

    def b_block(s, j, r, *pre):
        return (r, s if pick is None else step * s + pre[0][0])

    def body(*refs):
        a_ref, b_ref = refs[n_pre:n_pre + 2]
        o_ref, acc = refs[-2:]
        r = pl.program_id(2)

        @pl.when(r == 0)
        def _():
            acc[...] = jnp.zeros_like(acc)

        acc[...] += lax.dot_general(a_ref[...], b_ref[...], (((0,), (0,)), ((), ())),
                                    preferred_element_type=F32)

        @pl.when(r == steps - 1)
        def _():
            o_ref[0] = acc[...].astype(out_dtype)

    return pl.pallas_call(
        body, name=name,
        grid_spec=pltpu.PrefetchScalarGridSpec(
            num_scalar_prefetch=n_pre, grid=(ns, k // tk, steps),
            in_specs=[pl.BlockSpec((tm, tk), lambda s, j, r, *pre: (r, j)),
                      pl.BlockSpec((tm, n), b_block)] + dep_specs,
            out_specs=pl.BlockSpec((1, tk, n), lambda s, j, r, *pre: (s, j, 0)),
            scratch_shapes=[pltpu.VMEM((tk, n), F32)]),
        out_shape=jax.ShapeDtypeStruct((ns, k, n), out_dtype),
        compiler_params=_params("parallel", "parallel", "arbitrary"),
    )(*([] if pick is None else [offset]), a, b, *dep_args)


SB_BLK = 128


LOG2E = 1.0 / math.log(2.0)


def _split_dot(v, tri2):
    hi = pltpu.bitcast(pltpu.bitcast(v, jnp.uint32) & jnp.uint32(0xFFFF0000), F32)
    lo = (v - hi).astype(BF16)
    return jnp.dot(jnp.concatenate([hi.astype(BF16), lo], axis=1), tri2, preferred_element_type=F32)


def _sb_scores(z2, lim, dcol, tri_ex, masked):
    sp = jnp.log2(1.0 + jnp.exp2(-jnp.abs(z2)))
    lb = jnp.minimum(z2, 0.0) - sp
    l1m = lb - z2
    mask = None
    if masked:
        mask = dcol < lim
        l1m = jnp.where(mask, l1m, 0.0)
    return mask, lb, l1m, _split_dot(l1m, tri_ex)


def _sb_consts():
    row = lax.broadcasted_iota(jnp.int32, (SB_BLK, SB_BLK), 0)
    col = lax.broadcasted_iota(jnp.int32, (SB_BLK, SB_BLK), 1)
    tri_ex = jnp.where(row > col, 1.0, 0.0).astype(BF16)
    tri_in = jnp.where(row >= col, 1.0, 0.0).astype(BF16)
    return col - row, jnp.concatenate([tri_ex, tri_ex], axis=0), jnp.concatenate([tri_in, tri_in], axis=0)


def sb_fwd(p, n_heads, name, tq=512, nsub=4, dep=None):
    s = p.shape[0]
    h_n = n_heads
    b = SB_BLK
    nqs = tq // b
    tk = nsub * b
    scale = 1.0 / math.sqrt(HEAD_DIM)

    dep_args, dep_specs = _after(dep)

    def body(q_ref, k_ref, v_ref, *rest):
        o_ref, w_ref = rest[-2:]
        qi = pl.program_id(1)
        dcol, tri_ex, _ = _sb_consts()
        qv = [q_ref[qs * b:(qs + 1) * b, :] for qs in range(nqs)]
        n_groups = ((qi + 1) * nqs - 1) // nsub + 1

        def step(it, carry, masked):
            c1s, accs = carry
            g = n_groups - 1 - it
            off = pl.multiple_of(g * tk, tk)
            kg = k_ref[pl.ds(off, tk), :]
            vg = v_ref[pl.ds(off, tk), :]
            new_c1, new_acc = [], []
            for qs in range(nqs):
                qb = qi * nqs + qs
                z2 = lax.dot_general(qv[qs], kg, (((1,), (1,)), ((), ())),
                                     preferred_element_type=F32) * (scale * LOG2E)
                blocks = [_sb_scores(z2[:, j * b:(j + 1) * b], (qb - (g * nsub + j)) * b, dcol, tri_ex, masked)
                          for j in range(nsub)]
                run = c1s[qs]
                ws = [None] * nsub
                for j in reversed(range(nsub)):
                    mask, lb, l1m, ls_loc = blocks[j]
                    wj = jnp.exp2(lb + ls_loc + run)
                    ws[j] = (jnp.where(mask, wj, 0.0) if masked else wj).astype(BF16)
                    run = run + jnp.sum(l1m, axis=1, keepdims=True)
                w = jnp.concatenate(ws, axis=1)
                w_ref[0, g, qs * b:(qs + 1) * b, :] = w
                new_acc.append(accs[qs] + jnp.dot(w, vg, preferred_element_type=F32))
                new_c1.append(run)
            return tuple(new_c1), tuple(new_acc)

        init = (tuple(jnp.zeros((b, 1), F32) for _ in range(nqs)),
                tuple(jnp.zeros((b, HEAD_DIM), F32) for _ in range(nqs)))
        assert all(((i + 1) * nqs - 1) // nsub * nsub <= i * nqs for i in range(s // tq))
        first = step(0, init, True)
        _, accs = lax.fori_loop(1, n_groups, functools.partial(step, masked=False), first)
        for qs in range(nqs):
            o_ref[qs * b:(qs + 1) * b, :] = accs[qs]

    return pl.pallas_call(
        body, name=name, grid=(h_n, s // tq),
        in_specs=[pl.BlockSpec((tq, HEAD_DIM), lambda h, i: (i, h)),
                  pl.BlockSpec((s, HEAD_DIM), lambda h, i: (0, h_n + h)),
                  pl.BlockSpec((s, HEAD_DIM), lambda h, i: (0, 2 * h_n + h))] + dep_specs,
        out_specs=[pl.BlockSpec((tq, HEAD_DIM), lambda h, i: (i, h)),
                   pl.BlockSpec((1, s // tk, tq, tk), lambda h, i: (h, 0, i, 0))],
        out_shape=[jax.ShapeDtypeStruct((s, h_n * HEAD_DIM), F32),
                   jax.ShapeDtypeStruct((h_n, s // tk, s, tk), BF16)],
        compiler_params=_params("parallel", "arbitrary"),
    )(p, p, p, *dep_args)


def sb_bwd(p, a, wts, da, n_heads, name, tq=512, dep=None):
    s = p.shape[0]
    h_n = n_heads
    nq = s // tq
    b = SB_BLK
    nqs = tq // b
    tk = wts.shape[3]
    nsub = tk // b
    scale = 1.0 / math.sqrt(HEAD_DIM)
    dep_args, dep_specs = _after(dep)

    def body(q_ref, k_ref, v_ref, a_ref, da_ref, w_ref, *rest):
        dq_ref, dk_ref, dv_ref, dk_acc, dv_acc = rest[-5:]
        qi = pl.program_id(1)

        @pl.when(qi == 0)
        def _():
            dk_acc[...] = jnp.zeros_like(dk_acc)
            dv_acc[...] = jnp.zeros_like(dv_acc)

        dcol, _, tri_in = _sb_consts()
        q_all = q_ref[...]
        do_all = da_ref[...]
        qv = [q_ref[qs * b:(qs + 1) * b, :] for qs in range(nqs)]
        dov = [da_ref[qs * b:(qs + 1) * b, :] for qs in range(nqs)]
        tots = [jnp.sum(dov[qs].astype(F32) * a_ref[qs * b:(qs + 1) * b, :], axis=1, keepdims=True)
                for qs in range(nqs)]
        n_groups = ((qi + 1) * nqs - 1) // nsub + 1

        def step(it, carry, masked):
            c2s, dqs = carry
            g = n_groups - 1 - it
            off = pl.multiple_of(g * tk, tk)
            kg = k_ref[pl.ds(off, tk), :]
            vg = v_ref[pl.ds(off, tk), :]
            w_all = w_ref[0, g]
            new_c2, new_dq, dz_rows = [], [], []
            for qs in range(nqs):
                qb = qi * nqs + qs
                z2 = lax.dot_general(qv[qs], kg, (((1,), (1,)), ((), ())),
                                     preferred_element_type=F32) * (-scale * LOG2E)
                dw = lax.dot_general(dov[qs], vg, (((1,), (1,)), ((), ())), preferred_element_type=F32)
                beta = 1.0 / (1.0 + jnp.exp2(z2))
                e = dw * w_all[qs * b:(qs + 1) * b, :].astype(F32)
                run2 = c2s[qs]
                dzs = [None] * nsub
                for j in reversed(range(nsub)):
                    cols = slice(j * b, (j + 1) * b)
                    later = _split_dot(e[:, cols], tri_in) + run2
                    bj = beta[:, cols]
                    dz = (e[:, cols] * (1.0 - bj) - bj * (tots[qs] - later)) * scale
                    if masked:
                        dz = jnp.where(dcol < (qb - (g * nsub + j)) * b, dz, 0.0)
                    dzs[j] = dz.astype(BF16)
                    run2 = run2 + jnp.sum(e[:, cols], axis=1, keepdims=True)
                dzq = jnp.concatenate(dzs, axis=1)
                new_dq.append(dqs[qs] + jnp.dot(dzq, kg, preferred_element_type=F32))
                new_c2.append(run2)
                dz_rows.append(dzq)
            dz_all = jnp.concatenate(dz_rows, axis=0)
            dk_acc[pl.ds(off, tk), :] += lax.dot_general(dz_all, q_all, (((0,), (0,)), ((), ())),
                                                         preferred_element_type=F32)
            dv_acc[pl.ds(off, tk), :] += lax.dot_general(w_all, do_all, (((0,), (0,)), ((), ())),
                                                         preferred_element_type=F32)
            return tuple(new_c2), tuple(new_dq)

        zeros = tuple(jnp.zeros((b, 1), F32) for _ in range(nqs))
        assert all(((i + 1) * nqs - 1) // nsub * nsub <= i * nqs for i in range(s // tq))
        first = step(0, (zeros, tuple(jnp.zeros((b, HEAD_DIM), F32) for _ in range(nqs))), True)
        _, dqs = lax.fori_loop(1, n_groups, functools.partial(step, masked=False), first)
        for qs in range(nqs):
            dq_ref[qs * b:(qs + 1) * b, :] = dqs[qs].astype(BF16)

        @pl.when(qi == nq - 1)
        def _():
            dk_ref[...] = dk_acc[...].astype(BF16)
            dv_ref[...] = dv_acc[...].astype(BF16)

    blk = pl.BlockSpec((tq, HEAD_DIM), lambda h, i: (i, h))
    full = pl.BlockSpec((s, HEAD_DIM), lambda h, i: (0, h))
    return pl.pallas_call(
        body, name=name, grid=(h_n, nq),
        in_specs=[blk, pl.BlockSpec((s, HEAD_DIM), lambda h, i: (0, h_n + h)),
                  pl.BlockSpec((s, HEAD_DIM), lambda h, i: (0, 2 * h_n + h)), blk, blk,
                  pl.BlockSpec((1, s // tk, tq, tk), lambda h, i: (h, 0, i, 0))] + dep_specs,
        out_specs=[blk, full, full],
        out_shape=[jax.ShapeDtypeStruct((s, h_n * HEAD_DIM), BF16)] * 3,
        scratch_shapes=[pltpu.VMEM((s, HEAD_DIM), F32), pltpu.VMEM((s, HEAD_DIM), F32)],
        compiler_params=_params("parallel", "arbitrary"),
    )(p, p, p, a, da, wts, *dep_args)


def _pool_window(xx, win, r0, rc):
    cur = xx[HALO:HALO + rc]
    ws = _window_sum(xx, win, True)[HALO:HALO + rc]
    t_idx = r0 + lax.broadcasted_iota(jnp.int32, (rc, 1), 0)
    inv = 1.0 / jnp.minimum(win, t_idx + 1).astype(F32)
    return ws * inv - cur, inv


def even_mix_fwd(a, p, pool_w, pool_scale, name, rc=64, dep=None):
    s = p.shape[0]
    ng = len(POOL_WINDOWS)
    cw = pool_w.shape[1]
    n_chunks = s // rc
    dep_args, dep_specs = _after(dep)

    def body(a_ref, u_ref, g_ref, w_ref, sc_ref, *rest):
        y_ref, upad = rest[-2:]
        j = pl.program_id(0)

        @pl.when(j < ng)
        def _():
            def chunk(ci, carry):
                rows = pl.ds(pl.multiple_of(ci * rc, rc), rc)
                y_ref[rows, :] = (a_ref[rows, :] * _silu(g_ref[rows, :].astype(F32))).astype(BF16)
                return carry

            lax.fori_loop(0, n_chunks, chunk, 0)

        for gi, win in enumerate(POOL_WINDOWS):
            @pl.when(j == ng + gi)
            def _(win=win):
                upad[0:HALO, :] = jnp.zeros((HALO, cw), F32)

                def fill(ci, carry):
                    r0 = pl.multiple_of(ci * rc, rc)
                    upad[pl.ds(pl.multiple_of(r0 + HALO, HALO), rc), :] = u_ref[pl.ds(r0, rc), :].astype(F32)
                    return carry

                lax.fori_loop(0, n_chunks, fill, 0)

                def chunk(ci, carry):
                    r0 = pl.multiple_of(ci * rc, rc)
                    rows = pl.ds(r0, rc)
                    pooled, _ = _pool_window(upad[pl.ds(r0, HALO + rc), :], win, r0, rc)
                    t = jnp.dot(pooled.astype(BF16), w_ref[0], preferred_element_type=F32)
                    y_ref[rows, :] = (t * sc_ref[...] * _silu(g_ref[rows, :].astype(F32))).astype(BF16)
                    return carry

                lax.fori_loop(0, n_chunks, chunk, 0)

    grp = lambda j: jnp.maximum(j - ng, 0)
    return pl.pallas_call(
        body, name=name, grid=(2 * ng,),
        in_specs=[pl.BlockSpec((s, cw), lambda j: (0, jnp.minimum(j, ng - 1))),
                  pl.BlockSpec((s, cw), lambda j: (0, 3 * ng + grp(j))),
                  pl.BlockSpec((s, cw), lambda j: (0, 4 * ng + j)),
                  pl.BlockSpec((1, cw, cw), lambda j: (grp(j), 0, 0)),
                  pl.BlockSpec((1, cw), lambda j: (0, grp(j)))] + dep_specs,
        out_specs=pl.BlockSpec((s, cw), lambda j: (0, j)),
        out_shape=jax.ShapeDtypeStruct((s, 2 * ng * cw), BF16),
        scratch_shapes=[pltpu.VMEM((HALO + s, cw), F32)],
        compiler_params=_params("arbitrary"),
    )(a, p, p, pool_w, pool_scale, *dep_args)


def even_mix_bwd(dy, a, p, pool_w, pool_scale, name, rc=64):
    s = p.shape[0]
    ng = len(POOL_WINDOWS)
    cw = pool_w.shape[1]
    n_chunks = s // rc

    def body(dy_ref, a_ref, u_ref, g_ref, w_ref, sc_ref, da_ref, du_ref, dg_ref, dw_ref, dsc_ref,
             upad, rpad, dpl, dw_acc, dsc_acc):
        j = pl.program_id(0)

        @pl.when(j < ng)
        def _():
            def chunk(ci, carry):
                rows = pl.ds(pl.multiple_of(ci * rc, rc), rc)
                dyv = dy_ref[rows, :].astype(F32)
                sg, dsg = _silu_and_grad(g_ref[rows, :].astype(F32))
                da_ref[rows, :] = (dyv * sg).astype(BF16)
                dg_ref[rows, :] = (dyv * a_ref[rows, :] * dsg).astype(BF16)
                return carry

            lax.fori_loop(0, n_chunks, chunk, 0)

        for gi, win in enumerate(POOL_WINDOWS):
            @pl.when(j == ng + gi)
            def _(win=win):
                upad[0:HALO, :] = jnp.zeros((HALO, cw), F32)
                rpad[s:s + HALO, :] = jnp.zeros((HALO, cw), F32)
                dw_acc[...] = jnp.zeros_like(dw_acc)
                dsc_acc[...] = jnp.zeros_like(dsc_acc)

                def fill(ci, carry):
                    r0 = pl.multiple_of(ci * rc, rc)
                    upad[pl.ds(pl.multiple_of(r0 + HALO, HALO), rc), :] = u_ref[pl.ds(r0, rc), :].astype(F32)
                    return carry

                lax.fori_loop(0, n_chunks, fill, 0)

                def chunk(ci, carry):
                    r0 = pl.multiple_of(ci * rc, rc)
                    rows = pl.ds(r0, rc)
                    pooled, inv = _pool_window(upad[pl.ds(r0, HALO + rc), :], win, r0, rc)
                    pb = pooled.astype(BF16)
                    wv = w_ref[0]
                    t = jnp.dot(pb, wv, preferred_element_type=F32)
                    scv = sc_ref[...]
                    dyv = dy_ref[rows, :].astype(F32)
                    sg, dsg = _silu_and_grad(g_ref[rows, :].astype(F32))
                    dpo = dyv * sg
                    dg_ref[rows, :] = (dyv * t * scv * dsg).astype(BF16)
                    dsc_acc[...] += _rowsum8(dpo * t)
                    dtb = (dpo * scv).astype(BF16)
                    dw_acc[...] += lax.dot_general(pb, dtb, (((0,), (0,)), ((), ())),
                                                   preferred_element_type=F32)
                    dpooled = lax.dot_general(dtb, wv, (((1,), (1,)), ((), ())),
                                              preferred_element_type=F32)
                    dpl[rows, :] = dpooled
                    rpad[rows, :] = dpooled * inv
                    return carry

                lax.fori_loop(0, n_chunks, chunk, 0)

                def chunk2(ci, carry):
                    r0 = pl.multiple_of(ci * rc, rc)
                    rows = pl.ds(r0, rc)
                    xx = rpad[pl.ds(r0, rc + HALO), :]
                    fs = _window_sum(xx, win, False)[0:rc]
                    du_ref[rows, :] = (fs - dpl[rows, :]).astype(BF16)
                    return carry

                lax.fori_loop(0, n_chunks, chunk2, 0)
                dw_ref[0] = dw_acc[...]
                dsc_ref[...] = jnp.sum(dsc_acc[...], axis=0, keepdims=True)

    grp = lambda j: jnp.maximum(j - ng, 0)
    att = lambda j: jnp.minimum(j, ng - 1)
    return pl.pallas_call(
        body, name=name, grid=(2 * ng,),
        in_specs=[pl.BlockSpec((s, cw), lambda j: (0, j)),
                  pl.BlockSpec((s, cw), lambda j: (0, att(j))),
                  pl.BlockSpec((s, cw), lambda j: (0, 3 * ng + grp(j))),
                  pl.BlockSpec((s, cw), lambda j: (0, 4 * ng + j)),
                  pl.BlockSpec((1, cw, cw), lambda j: (grp(j), 0, 0)),
                  pl.BlockSpec((1, cw), lambda j: (0, grp(j)))],
        out_specs=[pl.BlockSpec((s, cw), lambda j: (0, att(j))),
                   pl.BlockSpec((s, cw), lambda j: (0, grp(j))),
                   pl.BlockSpec((s, cw), lambda j: (0, j)),
                   pl.BlockSpec((1, cw, cw), lambda j: (grp(j), 0, 0)),
                   pl.BlockSpec((1, cw), lambda j: (0, grp(j)))],
        out_shape=[jax.ShapeDtypeStruct((s, ng * cw), BF16), jax.ShapeDtypeStruct((s, ng * cw), BF16),
                   jax.ShapeDtypeStruct((s, 2 * ng * cw), BF16),
                   jax.ShapeDtypeStruct((ng, cw, cw), F32), jax.ShapeDtypeStruct((1, ng * cw), F32)],
        scratch_shapes=[pltpu.VMEM((HALO + s, cw), F32), pltpu.VMEM((s + HALO, cw), F32),
                        pltpu.VMEM((s, cw), F32), pltpu.VMEM((cw, cw), F32), pltpu.VMEM((8, cw), F32)],
        compiler_params=_params("arbitrary"),
    )(dy, a, p, p, pool_w, pool_scale)


def _halo_before(tm):
    return lambda i: jnp.maximum(i * (tm // HALO) - 1, 0)


def _halo_after(tm, s):
    return lambda i: jnp.minimum((i + 1) * (tm // HALO), s // HALO - 1)


def odd_mix_fwd(p, sconv_w, dconv_w, dconv_b, cnorm_g, cnorm_b, name, tm=128):
    s = p.shape[0]
    cw = sconv_w.shape[1]
    n = s // tm
    lanes = 128
    hb = _halo_before(tm)

    def body(hc_ref, hch_ref, bc_ref, cc_ref, cch_ref, ga_ref, gah_ref, gb_ref, gbh_ref, g1_ref, g2_ref,
             sw_ref, dw_ref, db_ref, gam_ref, bet_ref, y_ref, dc_ref):
        first = pl.program_id(0) == 0
        for l in range(cw // lanes):
            cols = slice(l * lanes, (l + 1) * lanes)
            mh = jnp.where(first, 0.0, cch_ref[:, cols].astype(F32) * hch_ref[:, cols].astype(F32))
            mm = cc_ref[:, cols].astype(F32) * hc_ref[:, cols].astype(F32)
            xx = jnp.concatenate([mh, mm], axis=0)
            tap = _Taps(xx, tm, True)
            cv = jnp.zeros((tm, lanes), F32)
            for k in range(SCONV_K):
                cv = cv + sw_ref[k:k + 1, cols] * tap(SCONV_K - 1 - k)
            c_out = bc_ref[:, cols].astype(F32) * cv
            y_ref[:, cols] = (c_out * _silu(g1_ref[:, cols].astype(F32))).astype(BF16)
            dh = jnp.where(first, 0.0, gah_ref[:, cols].astype(F32) * _sigmoid(gbh_ref[:, cols].astype(F32)))
            dm = ga_ref[:, cols].astype(F32) * _sigmoid(gb_ref[:, cols].astype(F32))
            xx = jnp.concatenate([dh, dm], axis=0)
            tap = _Taps(xx, tm, True)
            acc = jnp.zeros((tm, lanes), F32) + db_ref[:, cols]
            for k in range(CONF_K):
                acc = acc + dw_ref[k:k + 1, cols] * tap(CONF_K - 1 - k)
            dc_ref[:, cols] = acc
        rs = 32
        for r in range(tm // rs):
            rows = slice(r * rs, (r + 1) * rs)
            xv = dc_ref[rows, :]
            mu = jnp.mean(xv, axis=-1, keepdims=True)
            xc = xv - mu
            rstd = lax.rsqrt(jnp.mean(xc * xc, axis=-1, keepdims=True) + EPS)
            ln = xc * rstd * gam_ref[...] + bet_ref[...]
            y_ref[rows, cw:2 * cw] = (_silu(ln) * _silu(g2_ref[rows, :].astype(F32))).astype(BF16)

    main = lambda c: pl.BlockSpec((tm, cw), lambda i: (i, c))
    halo = lambda c: pl.BlockSpec((HALO, cw), lambda i: (hb(i), c))
    vec = lambda r: pl.BlockSpec((r, cw), lambda i: (0, 0))
    return pl.pallas_call(
        body, name=name, grid=(n,),
        in_specs=[main(0), halo(0), main(1), main(2), halo(2), main(3), halo(3), main(4), halo(4),
                  main(5), main(6), vec(SCONV_K), vec(CONF_K), vec(1), vec(1), vec(1)],
        out_specs=[pl.BlockSpec((tm, 2 * cw), lambda i: (i, 0)), pl.BlockSpec((tm, cw), lambda i: (i, 0))],
        out_shape=[jax.ShapeDtypeStruct((s, 2 * cw), BF16), jax.ShapeDtypeStruct((s, cw), F32)],
        compiler_params=_params("parallel"),
    )(p, p, p, p, p, p, p, p, p, p, p, sconv_w, dconv_w, dconv_b, cnorm_g, cnorm_b)


def odd_bwd_ln(dy, p, dc, cnorm_g, cnorm_b, name, tm=256):
    s = p.shape[0]
    cw = dc.shape[1]
    n = s // tm
    rs = 32

    def body(dy_ref, g2_ref, dc_ref, gam_ref, bet_ref, ddc_ref, dg_ref, dgam_ref, dbet_ref, gacc, bacc):
        i = pl.program_id(0)

        @pl.when(i == 0)
        def _():
            gacc[...] = jnp.zeros_like(gacc)
            bacc[...] = jnp.zeros_like(bacc)

        def chunk(ci, carry):
            rows = pl.ds(pl.multiple_of(ci * rs, rs), rs)
            xv = dc_ref[rows, :]
            mu = jnp.mean(xv, axis=-1, keepdims=True)
            xc = xv - mu
            rstd = lax.rsqrt(jnp.mean(xc * xc, axis=-1, keepdims=True) + EPS)
            xh = xc * rstd
            gam = gam_ref[...]
            sl, dsl = _silu_and_grad(xh * gam + bet_ref[...])
            sg, dsg = _silu_and_grad(g2_ref[rows, :].astype(F32))
            dyv = dy_ref[rows, :].astype(F32)
            dg_ref[rows, :] = (dyv * sl * dsg).astype(BF16)
            dln = dyv * sg * dsl
            gacc[...] += _rowsum8(dln * xh)
            bacc[...] += _rowsum8(dln)
            dxh = dln * gam
            ddc_ref[rows, :] = rstd * (dxh - jnp.mean(dxh, axis=-1, keepdims=True)
                                       - xh * jnp.mean(dxh * xh, axis=-1, keepdims=True))
            return carry

        lax.fori_loop(0, tm // rs, chunk, 0)

        @pl.when(i == n - 1)
        def _():
            dgam_ref[...] = jnp.sum(gacc[...], axis=0, keepdims=True)
            dbet_ref[...] = jnp.sum(bacc[...], axis=0, keepdims=True)

    vec = pl.BlockSpec((1, cw), lambda i: (0, 0))
    return pl.pallas_call(
        body, name=name, grid=(n,),
        in_specs=[pl.BlockSpec((tm, cw), lambda i: (i, 1)), pl.BlockSpec((tm, cw), lambda i: (i, 6)),
                  pl.BlockSpec((tm, cw), lambda i: (i, 0)), vec, vec],
        out_specs=[pl.BlockSpec((tm, cw), lambda i: (i, 0)), pl.BlockSpec((tm, cw), lambda i: (i, 0)), vec, vec],
        out_shape=[jax.ShapeDtypeStruct((s, cw), F32), jax.ShapeDtypeStruct((s, cw), BF16),
                   jax.ShapeDtypeStruct((1, cw), F32), jax.ShapeDtypeStruct((1, cw), F32)],
        scratch_shapes=[pltpu.VMEM((8, cw), F32), pltpu.VMEM((8, cw), F32)],
        compiler_params=_params("arbitrary"),
    )(dy, p, dc, cnorm_g, cnorm_b)


def odd_bwd_conv(dy, p, ddc, dg2, sconv_w, dconv_w, name, tm=128):
    s = p.shape[0]
    cw = ddc.shape[1]
    n = s // tm
    lanes = 128
    hb = _halo_before(tm)
    ha = _halo_after(tm, s)

    def body(dy_ref, dya_ref, g1_ref, g1a_ref, bc_ref, bca_ref, hc_ref, hch_ref, cc_ref, cch_ref,
             ddc_ref, ddca_ref, ga_ref, gah_ref, gb_ref, gbh_ref, dg2_ref, sw_ref, dw_ref,
             dp_ref, dsw_ref, ddw_ref, ddb_ref, sw_acc, dw_acc, db_acc):
        i = pl.program_id(0)
        first = i == 0
        last = i == n - 1

        @pl.when(first)
        def _():
            sw_acc[...] = jnp.zeros_like(sw_acc)
            dw_acc[...] = jnp.zeros_like(dw_acc)
            db_acc[...] = jnp.zeros_like(db_acc)

        for l in range(cw // lanes):
            cols = slice(l * lanes, (l + 1) * lanes)
            mh = jnp.where(first, 0.0, cch_ref[:, cols].astype(F32) * hch_ref[:, cols].astype(F32))
            hcv = hc_ref[:, cols].astype(F32)
            ccv = cc_ref[:, cols].astype(F32)
            xx = jnp.concatenate([mh, ccv * hcv], axis=0)
            tap = _Taps(xx, tm, True)
            taps = [tap(SCONV_K - 1 - k) for k in range(SCONV_K)]
            cv = jnp.zeros((tm, lanes), F32)
            for k in range(SCONV_K):
                cv = cv + sw_ref[k:k + 1, cols] * taps[k]
            bcv = bc_ref[:, cols].astype(F32)
            dyv = dy_ref[:, cols].astype(F32)
            sg, dsg = _silu_and_grad(g1_ref[:, cols].astype(F32))
            dco = dyv * sg
            dp_ref[:, 5 * cw + l * lanes:5 * cw + (l + 1) * lanes] = (dyv * bcv * cv * dsg).astype(BF16)
            dp_ref[:, cw + l * lanes:cw + (l + 1) * lanes] = (dco * cv).astype(BF16)
            dcv = dco * bcv
            for k in range(SCONV_K):
                sw_acc[k * 8:(k + 1) * 8, cols] += _rowsum8(dcv * taps[k])
            dcv_a = jnp.where(last, 0.0, dya_ref[:, cols].astype(F32) * _silu(g1a_ref[:, cols].astype(F32))
                              * bca_ref[:, cols].astype(F32))
            xx = jnp.concatenate([dcv, dcv_a], axis=0)
            tap = _Taps(xx, tm, False)
            dm = jnp.zeros((tm, lanes), F32)
            for k in range(SCONV_K):
                dm = dm + sw_ref[k:k + 1, cols] * tap(SCONV_K - 1 - k)
            dp_ref[:, l * lanes:(l + 1) * lanes] = (dm * ccv).astype(BF16)
            dp_ref[:, 2 * cw + l * lanes:2 * cw + (l + 1) * lanes] = (dm * hcv).astype(BF16)
            gav = ga_ref[:, cols].astype(F32)
            sb = _sigmoid(gb_ref[:, cols].astype(F32))
            dh = jnp.where(first, 0.0, gah_ref[:, cols].astype(F32) * _sigmoid(gbh_ref[:, cols].astype(F32)))
            xx = jnp.concatenate([dh, gav * sb], axis=0)
            ddcv = ddc_ref[:, cols]
            db_acc[:, cols] += _rowsum8(ddcv)
            tap = _Taps(xx, tm, True)
            for k in range(CONF_K):
                dw_acc[k * 8:(k + 1) * 8, cols] += _rowsum8(ddcv * tap(CONF_K - 1 - k))
            ddc_a = jnp.where(last, 0.0, ddca_ref[:, cols])
            xx = jnp.concatenate([ddcv, ddc_a], axis=0)
            tap = _Taps(xx, tm, False)
            dgl = jnp.zeros((tm, lanes), F32)
            for k in range(CONF_K):
                dgl = dgl + dw_ref[k:k + 1, cols] * tap(CONF_K - 1 - k)
            dp_ref[:, 3 * cw + l * lanes:3 * cw + (l + 1) * lanes] = (dgl * sb).astype(BF16)
            dp_ref[:, 4 * cw + l * lanes:4 * cw + (l + 1) * lanes] = (dgl * gav * sb * (1.0 - sb)).astype(BF16)
        dp_ref[:, 6 * cw:7 * cw] = dg2_ref[...]

        @pl.when(last)
        def _():
            for k in range(SCONV_K):
                dsw_ref[k:k + 1, :] = jnp.sum(sw_acc[k * 8:(k + 1) * 8, :], axis=0, keepdims=True)
            for k in range(CONF_K):
                ddw_ref[k:k + 1, :] = jnp.sum(dw_acc[k * 8:(k + 1) * 8, :], axis=0, keepdims=True)
            ddb_ref[...] = jnp.sum(db_acc[...], axis=0, keepdims=True)

    def main(c):
        return pl.BlockSpec((tm, cw), lambda i: (i, c))

    def before(c):
        return pl.BlockSpec((HALO, cw), lambda i: (hb(i), c))

    def after(c):
        return pl.BlockSpec((HALO, cw), lambda i: (ha(i), c))

    def vec(r):
        return pl.BlockSpec((r, cw), lambda i: (0, 0))

    return pl.pallas_call(
        body, name=name, grid=(n,),
        in_specs=[main(0), after(0), main(5), after(5), main(1), after(1), main(0), before(0), main(2), before(2),
                  main(0), after(0), main(3), before(3), main(4), before(4), main(0), vec(SCONV_K), vec(CONF_K)],
        out_specs=[pl.BlockSpec((tm, 7 * cw), lambda i: (i, 0)), vec(SCONV_K), vec(CONF_K), vec(1)],
        out_shape=[jax.ShapeDtypeStruct((s, 7 * cw), BF16), jax.ShapeDtypeStruct((SCONV_K, cw), F32),
                   jax.ShapeDtypeStruct((CONF_K, cw), F32), jax.ShapeDtypeStruct((1, cw), F32)],
        scratch_shapes=[pltpu.VMEM((8 * SCONV_K, cw), F32), pltpu.VMEM((8 * CONF_K, cw), F32),
                        pltpu.VMEM((8, cw), F32)],
        compiler_params=_params("arbitrary"),
    )(dy, dy, p, p, p, p, p, p, p, p, ddc, ddc, p, p, p, p, dg2, sconv_w, dconv_w)


_ANY = pl.BlockSpec(memory_space=pl.ANY)


def _place():
    return lax.axis_index("x"), lax.axis_index("y"), lax.axis_index("c")


def all_gather(arrs, name, deps=()):
    n = len(arrs)

    def body(*refs):
        ins, outs = refs[:n], refs[n + len(deps):2 * n + len(deps)]
        send_sems, recv_sems, local_sems = refs[-3:]
        x, y, c = _place()
        me, sibling = (x, y, c), (x, y, 1 - c)
        chips = [(1 - x, y), (x, 1 - y), (1 - x, 1 - y)]

        def copy(a, k, block, to, src=None):
            px, py, pc = block
            dst = outs[a].at[4 * px + 2 * py + pc]
            return pltpu.make_async_remote_copy(
                src_ref=dst if src is None else src, dst_ref=dst,
                send_sem=send_sems.at[7 * a + k], recv_sem=recv_sems.at[7 * a + k],
                device_id=to, device_id_type=MESH)

        mine = [pltpu.make_async_copy(ins[a], outs[a].at[4 * x + 2 * y + c], local_sems.at[a]) for a in range(n)]
        first = []
        for a in range(n):
            first.append(copy(a, 0, me, sibling, src=ins[a]))
            first += [copy(a, 1 + j, me, (*chip, c), src=ins[a]) for j, chip in enumerate(chips)]
        for cp in first + mine:
            cp.start()
        passed = []
        for a in range(n):
            for j, chip in enumerate(chips):
                copy(a, 1 + j, (*chip, c), me).wait_recv()
                cp = copy(a, 4 + j, (*chip, c), sibling)
                cp.start()
                passed.append(cp)
        for a in range(n):
            copy(a, 0, sibling, me).wait_recv()
            for j, chip in enumerate(chips):
                copy(a, 4 + j, (*chip, 1 - c), me).wait_recv()
        for cp in first + passed:
            cp.wait_send()
        for cp in mine:
            cp.wait()

    return pl.pallas_call(
        body, name=name,
        out_shape=[jax.ShapeDtypeStruct((N_DEV,) + a.shape, a.dtype) for a in arrs],
        in_specs=[_ANY] * (n + len(deps)), out_specs=[_ANY] * n,
        scratch_shapes=[pltpu.SemaphoreType.DMA((7 * n,)), pltpu.SemaphoreType.DMA((7 * n,)),
                        pltpu.SemaphoreType.DMA((n,))],
    )(*arrs, *deps)


def in_proj_gathered(xs, g, w_own, extras, name, tm=512):
    s, d = xs.shape
    n = w_own.shape[1]
    arrs = [w_own] + list(extras)
    na = len(arrs)
    tr = 256

    def body(*refs):
        x_ref, g_ref, ins = refs[0], refs[1], refs[2:2 + na]
        h_out, p_ref, outs = refs[2 + na], refs[3 + na], refs[4 + na:4 + 2 * na]
        (h_ref, xbuf, wbuf, obuf, send_sems, recv_sems, load_sem, store_sems, own_sems, h_sem,
         x_sems) = refs[4 + 2 * na:]
        x, y, c = _place()
        me, sibling = (x, y, c), (x, y, 1 - c)
        x_first = c == 0
        near = (jnp.where(x_first, 1 - x, x), jnp.where(x_first, y, 1 - y))
        far = (jnp.where(x_first, x, 1 - x), jnp.where(x_first, 1 - y, y))
        diag = (1 - x, 1 - y)
        k_near, k_far = jnp.where(x_first, 1, 2), jnp.where(x_first, 2, 1)
        f_near, f_far = k_near + 3, k_far + 3

        def slot(block):
            return 4 * block[0] + 2 * block[1] + block[2]

        def copy(a, k, block, to, src=None):
            dst = outs[a].at[slot(block)]
            return pltpu.make_async_remote_copy(
                src_ref=dst if src is None else src, dst_ref=dst,
                send_sem=send_sems.at[7 * a + k], recv_sem=recv_sems.at[7 * a + k],
                device_id=to, device_id_type=MESH)

        first = []
        for a in range(na):
            first += [copy(a, 0, me, sibling, src=ins[a]), copy(a, 1, me, (1 - x, y, c), src=ins[a]),
                      copy(a, 2, me, (x, 1 - y, c), src=ins[a])]
        for cp in first:
            cp.start()
        own = pltpu.make_async_copy(wbuf.at[0], outs[0].at[slot(me)], own_sems.at[0])
        mine = [pltpu.make_async_copy(ins[a], outs[a].at[slot(me)], own_sems.at[a]) for a in range(1, na)]
        stores = [None, None]

        def x_load(i):
            return pltpu.make_async_copy(x_ref.at[pl.ds(i * tr, tr), :], xbuf.at[i % 2], x_sems.at[i % 2])

        x_load(0).start()
        for i in range(s // tr):
            if i + 1 < s // tr:
                x_load(i + 1).start()
            x_load(i).wait()
            xv = xbuf[i % 2]
            r = lax.rsqrt(jnp.mean(xv * xv, axis=-1, keepdims=True) + EPS)
            h_ref[i * tr:(i + 1) * tr, :] = (xv * r * g_ref[...]).astype(BF16)
        h_store = pltpu.make_async_copy(h_ref, h_out, h_sem)
        h_store.start()

        def multiply(k, block, w_from):
            b = k % 2
            if k == 2:
                own.wait()
            load = pltpu.make_async_copy(w_from, wbuf.at[b], load_sem)
            load.start()
            if stores[b] is not None:
                stores[b].wait()
            load.wait()
            if k == 0:
                own.start()

            def chunk(i, carry):
                rows = pl.ds(pl.multiple_of(i * tm, tm), tm)
                obuf[b, rows, :] = jnp.dot(h_ref[rows, :], wbuf[b], preferred_element_type=F32).astype(BF16)
                return carry

            lax.fori_loop(0, s // tm, chunk, 0)
            stores[b] = pltpu.make_async_copy(
                obuf.at[b], p_ref.at[:, pl.ds(pl.multiple_of(slot(block) * n, 128), n)], store_sems.at[b])
            stores[b].start()

        passed = []

        def arrive(a, k, block):
            copy(a, k, block, me).wait_recv()

        def pass_on(a, k, block, to):
            cp = copy(a, k, block, to)
            cp.start()
            passed.append(cp)

        def gather(a, use):
            use(0, me)
            arrive(a, 0, sibling)
            use(1, sibling)
            arrive(a, k_near, (*near, c))
            pass_on(a, 3, (*near, c), (*far, c))
            pass_on(a, f_near, (*near, c), sibling)
            use(2, (*near, c))
            arrive(a, f_far, (*far, 1 - c))
            use(3, (*far, 1 - c))
            arrive(a, k_far, (*far, c))
            pass_on(a, f_far, (*far, c), sibling)
            use(4, (*far, c))
            arrive(a, f_near, (*near, 1 - c))
            use(5, (*near, 1 - c))
            arrive(a, 3, (*diag, c))
            pass_on(a, 6, (*diag, c), sibling)
            use(6, (*diag, c))
            arrive(a, 6, (*diag, 1 - c))
            use(7, (*diag, 1 - c))

        gather(0, lambda k, block: multiply(k, block, ins[0] if k == 0 else outs[0].at[slot(block)]))
        for cp in mine:
            cp.start()
        for a in range(1, na):
            gather(a, lambda k, block: None)
        for cp in first + passed:
            cp.wait_send()
        for cp in mine + stores + [h_store]:
            cp.wait()

    vmem = pl.BlockSpec(memory_space=pltpu.VMEM)
    outs = pl.pallas_call(
        body, name=name,
        out_shape=[jax.ShapeDtypeStruct((s, d), BF16), jax.ShapeDtypeStruct((s, N_DEV * n), BF16)]
        + [jax.ShapeDtypeStruct((N_DEV,) + a.shape, a.dtype) for a in arrs],
        in_specs=[_ANY, vmem] + [_ANY] * na, out_specs=[_ANY] * (2 + na),
        scratch_shapes=[pltpu.VMEM((s, d), BF16), pltpu.VMEM((2, tr, d), F32), pltpu.VMEM((2, d, n), BF16),
                        pltpu.VMEM((2, s, n), BF16),
                        pltpu.SemaphoreType.DMA((7 * na,)), pltpu.SemaphoreType.DMA((7 * na,)),
                        pltpu.SemaphoreType.DMA, pltpu.SemaphoreType.DMA((2,)), pltpu.SemaphoreType.DMA((na,)),
                        pltpu.SemaphoreType.DMA, pltpu.SemaphoreType.DMA((2,))],
        compiler_params=pltpu.CompilerParams(vmem_limit_bytes=VMEM_LIMIT),
    )(xs, g, *arrs)
    return outs[0], outs[1], outs[2], outs[3:]


_HBM = pl.BlockSpec(memory_space=pltpu.HBM)
_SEM = pl.BlockSpec(memory_space=pltpu.SEMAPHORE)
_DATAFLOW = pltpu.SideEffectType.DATAFLOW_SIDE_EFFECTING


def _peers_per_array(kind):
    return 1 if kind in ("sibling", "halves") else 3


def _split_copies(kind, srcs, lands, send_sems, recv_sems):
    x, y, c = _place()
    per = _peers_per_array(kind)
    out = []
    for a in range(len(lands)):
        if kind == "sibling":
            part = srcs[a] if srcs[a].shape[1] == 1 else srcs[a].at[:, pl.ds(1 - c, 1)]
            peers = [((x, y, 1 - c), part, lands[a], lands[a])]
        elif kind == "halves":
            mine, its = lands[a].at[:, pl.ds(c, 1)], lands[a].at[:, pl.ds(1 - c, 1)]
            peers = [((x, y, 1 - c), mine, mine, its)]
        else:
            peers = []
            for px, py in [(1 - x, y), (x, 1 - y), (1 - x, 1 - y)]:
                if kind == "gather":
                    views = (srcs[a], lands[a].at[4 * x + 2 * y + c], lands[a].at[4 * px + 2 * py + c])
                else:
                    views = (srcs[a].at[2 * px + py], lands[a].at[2 * x + y], lands[a].at[2 * px + py])
                peers.append(((px, py, c),) + views)
        for j, (peer, src, dst, arrives) in enumerate(peers):
            sems = dict(send_sem=send_sems.at[per * a + j], recv_sem=recv_sems.at[per * a + j],
                        device_id=peer, device_id_type=MESH)
            out.append((pltpu.make_async_remote_copy(src_ref=src, dst_ref=dst, **sems),
                        pltpu.make_async_remote_copy(src_ref=src, dst_ref=arrives, **sems)))
    return out


def split_start(kind, srcs, lands, deps, name):
    ns, nl = len(srcs), len(lands)
    n_sems = _peers_per_array(kind) * nl
    held = list(srcs) + list(lands)

    def body(*refs):
        send_sems, recv_sems = refs[len(held) + len(deps)], refs[len(held) + len(deps) + 1]
        for copy, _ in _split_copies(kind, refs[:ns], refs[ns:ns + nl], send_sems, recv_sems):
            copy.start()
        token = refs[-1]
        token[...] = jnp.zeros_like(token)

    outs = pl.pallas_call(
        body, name=name,
        out_shape=(pltpu.SemaphoreType.DMA((n_sems,)), pltpu.SemaphoreType.DMA((n_sems,)),
                   *[pltpu.HBM(a.shape, a.dtype) for a in held], jax.ShapeDtypeStruct((8, 128), F32)),
        in_specs=[_HBM] * len(held) + [_ANY] * len(deps),
        out_specs=(_SEM, _SEM, *([_HBM] * len(held)), pl.BlockSpec(memory_space=pltpu.VMEM)),
        input_output_aliases={i: 2 + i for i in range(len(held))},
        compiler_params=pltpu.CompilerParams(has_side_effects=_DATAFLOW),
    )(*[pltpu.with_memory_space_constraint(a, pltpu.HBM) for a in held], *deps)
    return outs[0], outs[1], list(outs[2:2 + ns]), list(outs[2 + ns:2 + ns + nl]), outs[-1]


def split_wait(kind, send_sems, recv_sems, srcs, lands, afters, name):
    ns, nl = len(srcs), len(lands)
    held = list(srcs) + list(lands)

    def body(*refs):
        for _, arrival in _split_copies(kind, refs[:ns], refs[ns:ns + nl], refs[ns + nl], refs[ns + nl + 1]):
            arrival.wait_send()
            arrival.wait_recv()

    outs = pl.pallas_call(
        body, name=name,
        out_shape=[pltpu.HBM(a.shape, a.dtype) for a in held],
        in_specs=[_HBM] * len(held) + [_SEM, _SEM] + [_ANY] * len(afters),
        out_specs=[_HBM] * len(held),
        input_output_aliases={i: i for i in range(len(held))},
        compiler_params=pltpu.CompilerParams(has_side_effects=_DATAFLOW),
    )(*held, send_sems, recv_sems, *afters)
    return list(outs[:ns]), list(outs[ns:])


def place_block(land, block, dev, name):
    r, c = block.shape
    tr = min(r, 512)

    def body(dev_ref, land_ref, b_ref, o_ref):
        del dev_ref, land_ref
        o_ref[...] = b_ref[...]

    return pl.pallas_call(
        body, name=name,
        grid_spec=pltpu.PrefetchScalarGridSpec(
            num_scalar_prefetch=1, grid=(r // tr,),
            in_specs=[_ANY, pl.BlockSpec((tr, c), lambda i, dev_ref: (i, 0))],
            out_specs=pl.BlockSpec((None, tr, c), lambda i, dev_ref: (dev_ref[0], i, 0))),
        out_shape=jax.ShapeDtypeStruct(land.shape, land.dtype),
        input_output_aliases={1: 0},
        compiler_params=_params("parallel"),
    )(dev, land, block)


def pair_add(own, recv, core, name):
    _, _, r, c = own.shape
    tr = min(r, 512)

    def body(core_ref, own_ref, recv_ref, o_ref):
        del core_ref
        o_ref[...] = (own_ref[...].astype(F32) + recv_ref[...].astype(F32)).astype(BF16)

    return pl.pallas_call(
        body, name=name,
        grid_spec=pltpu.PrefetchScalarGridSpec(
            num_scalar_prefetch=1, grid=(4, r // tr),
            in_specs=[pl.BlockSpec((None, None, tr, c), lambda k, i, core_ref: (k, core_ref[0], i, 0)),
                      pl.BlockSpec((None, None, tr, c), lambda k, i, core_ref: (k, 0, i, 0))],
            out_specs=pl.BlockSpec((None, tr, c), lambda k, i, core_ref: (k, i, 0))),
        out_shape=jax.ShapeDtypeStruct((4, r, c), BF16),
        compiler_params=_params("parallel", "parallel"),
    )(core, own, recv)


def _adamw_math(w, g, m, v):
    m2 = ADAM_B1 * m + (1.0 - ADAM_B1) * g
    v2 = ADAM_B2 * v + (1.0 - ADAM_B2) * (g * g)
    m_hat = m2 / (1.0 - ADAM_B1 ** ADAM_STEP)
    v_hat = v2 / (1.0 - ADAM_B2 ** ADAM_STEP)
    delta = -ADAM_LR * (m_hat / (jnp.sqrt(v_hat) + ADAM_EPS) + ADAM_WD * w)
    return delta, m2, v2


def adamw_big(w, m, v, own, got, chip, name):
    r, c = w.shape
    tr = min(r, 256)

    def body(chip_ref, w_ref, m_ref, v_ref, p0, p1, p2, p3, g_ref, d_ref, m2_ref, v2_ref):
        del chip_ref
        g = ((p0[...].astype(F32) + p1[...].astype(F32)) + p2[...].astype(F32)) + p3[...].astype(F32)
        delta, m2, v2 = _adamw_math(w_ref[...], g, m_ref[...], v_ref[...])
        g_ref[...] = g
        d_ref[...] = delta
        m2_ref[...] = m2
        v2_ref[...] = v2

    row = pl.BlockSpec((tr, c), lambda i, chip_ref: (i, 0))

    def slab(flip):
        return pl.BlockSpec((None, tr, c), lambda i, chip_ref: (chip_ref[0] ^ flip, i, 0))

    return pl.pallas_call(
        body, name=name,
        grid_spec=pltpu.PrefetchScalarGridSpec(
            num_scalar_prefetch=1, grid=(r // tr,),
            in_specs=[row, row, row, slab(0), slab(1), slab(2), slab(3)],
            out_specs=[row] * 4),
        out_shape=[jax.ShapeDtypeStruct((r, c), F32)] * 4,
        compiler_params=_params("parallel"),
    )(chip, w, m, v, own, got, got, got)


def sum_devices(g8, name):
    def body(g_ref, o_ref):
        tot = g_ref[0]
        for k in range(1, N_DEV):
            tot = tot + g_ref[k]
        o_ref[...] = tot

    return pl.pallas_call(body, name=name, out_shape=jax.ShapeDtypeStruct(g8.shape[1:], F32))(g8)


def adamw_small(ws, gs, ms, vs, name):
    n = len(ws)

    def body(*refs):
        w_r, g_r, m_r, v_r = refs[:n], refs[n:2 * n], refs[2 * n:3 * n], refs[3 * n:4 * n]
        d_o, m_o, v_o = refs[4 * n:5 * n], refs[5 * n:6 * n], refs[6 * n:7 * n]
        for k in range(n):
            delta, m2, v2 = _adamw_math(w_r[k][...], g_r[k][...], m_r[k][...], v_r[k][...])
            d_o[k][...] = delta
            m_o[k][...] = m2
            v_o[k][...] = v2

    shapes = [jax.ShapeDtypeStruct(w.shape, F32) for w in ws]
    outs = pl.pallas_call(body, name=name, out_shape=shapes * 3)(*ws, *gs, *ms, *vs)
    return outs[:n], outs[n:2 * n], outs[2 * n:]


def _rows128(a):
    return a.reshape(-1, 128)


def _pad_rows(a, rows):
    return jnp.pad(a, ((0, rows - a.shape[0]), (0, 0)))


def kernel(x, ln_pre_even, w_in_even, pool_w, pool_scale, w_out_even, ln_post_even, ln_pre_odd, w_in_odd, sconv_w, dconv_w, dconv_b, cnorm_g, cnorm_b, w_out_odd, ln_post_odd, loss_target, m_ln_pre_even, m_w_in_even, m_pool_w, m_pool_scale, m_w_out_even, m_ln_post_even, m_ln_pre_odd, m_w_in_odd, m_sconv_w, m_dconv_w, m_dconv_b, m_cnorm_g, m_cnorm_b, m_w_out_odd, m_ln_post_odd, v_ln_pre_even, v_w_in_even, v_pool_w, v_pool_scale, v_w_out_even, v_ln_post_even, v_ln_pre_odd, v_w_in_odd, v_sconv_w, v_dconv_w, v_dconv_b, v_cnorm_g, v_cnorm_b, v_w_out_odd, v_ln_post_odd):
    xs = x[0]
    tgt = loss_target[0]
    s, d = xs.shape
    half = d // 2
    n_heads = half // HEAD_DIM
    ng = len(POOL_WINDOWS)
    cwp = half // ng
    dev = 4 * lax.axis_index("x") + 2 * lax.axis_index("y") + lax.axis_index("c")
    core = lax.axis_index("c").astype(jnp.int32).reshape(1)

    pr = pool_w.shape[2]
    cl = sconv_w.shape[2]
    small_parts = [(_rows128(ln_pre_odd), 8), (sconv_w[0], 8), (dconv_w[0], 32), (dconv_b, 8),
                   (cnorm_g, 8), (cnorm_b, 8), (_rows128(ln_post_odd), 8)]
    small_local = jnp.concatenate([_pad_rows(a, r) for a, r in small_parts], axis=0)
    h0, p0, g_wie, (g_pw, g_small) = in_proj_gathered(
        xs, ln_pre_even, w_in_even[0].astype(BF16), [pool_w[0].reshape(ng * pr, cwp).astype(BF16), small_local],
        "ag_in_proj_even")
    comm = _Exchanges(dev, core, d)
    token = comm.start_weights("out_even", [w_out_even[0].astype(BF16)], [p0])
    sb_dep = comm.start_weights("odd", [w_in_odd[0].astype(BF16), w_out_odd[0].astype(BF16)], [token])
    pool_full = g_pw.reshape(N_DEV, ng, pr, cwp).transpose(1, 0, 2, 3).reshape(ng, cwp, cwp)
    nl = ln_pre_odd.shape[1] // 128

    def chan(lo, rows):
        return g_small[:, lo:lo + rows].transpose(1, 0, 2).reshape(rows, N_DEV * cl)

    ln_pre_odd_f = g_small[:, 0:nl].reshape(1, d)
    sconv_f = chan(8, SCONV_K)
    dconv_f = chan(16, CONF_K)
    dconv_b_f = chan(48, 1)
    cnorm_g_f = chan(56, 1)
    cnorm_b_f = chan(64, 1)
    ln_post_odd_f = g_small[:, 72:72 + nl].reshape(1, d)

    loss_blk, grad_x, small_g = _fwd_bwd(
        xs, tgt, ln_pre_even, h0, p0, g_wie, pool_full, pool_scale, ln_post_even, ln_pre_odd_f,
        sconv_f, dconv_f, dconv_b_f, cnorm_g_f, cnorm_b_f, ln_post_odd_f, comm, sb_dep)
    small_w = [ln_pre_even, pool_scale, ln_post_even, ln_pre_odd, sconv_w[0], dconv_w[0], dconv_b, cnorm_g, cnorm_b, ln_post_odd]
    small_m = [m_ln_pre_even, m_pool_scale, m_ln_post_even, m_ln_pre_odd, m_sconv_w[0], m_dconv_w[0], m_dconv_b, m_cnorm_g, m_cnorm_b, m_ln_post_odd]
    small_v = [v_ln_pre_even, v_pool_scale, v_ln_post_even, v_ln_pre_odd, v_sconv_w[0], v_dconv_w[0], v_dconv_b, v_cnorm_g, v_cnorm_b, v_ln_post_odd]
    big = {"w_in_even": (w_in_even, m_w_in_even, v_w_in_even), "pool_w": (pool_w, m_pool_w, v_pool_w),
           "w_out_even": (w_out_even, m_w_out_even, v_w_out_even), "w_in_odd": (w_in_odd, m_w_in_odd, v_w_in_odd),
           "w_out_odd": (w_out_odd, m_w_out_odd, v_w_out_odd)}
    upd = comm.finish_updates(big, [grad_x])
    upd.update(comm.finish_updates(big, [grad_x]))
    sg, sd, sm, sv, loss = _update_small(small_g, loss_blk, small_w, small_m, small_v, dev, d, cl,
                                         deps=[upd["w_in_odd"][1], upd["w_out_even"][1]])
    upd.update(comm.finish_updates(big, sd))
    (g_wie_o, d_wie, m_wie, v_wie), (g_pw_o, d_pw, m_pw, v_pw) = upd["w_in_even"], upd["pool_w"]
    (g_woe_o, d_woe, m_woe, v_woe), (g_wio_o, d_wio, m_wio, v_wio) = upd["w_out_even"], upd["w_in_odd"]
    g_woo_o, d_woo, m_woo, v_woo = upd["w_out_odd"]

    def order(small, wie, pw, woe, wio, woo):
        return [small[0], wie, pw, small[1], woe, small[2], small[3], wio, small[4], small[5], small[6],
                small[7], small[8], woo, small[9]]

    grads = order(sg, g_wie_o, g_pw_o, g_woe_o, g_wio_o, g_woo_o)
    deltas = order(sd, d_wie, d_pw, d_woe, d_wio, d_woo)
    new_m = order(sm, m_wie, m_pw, m_woe, m_wio, m_woo)
    new_v = order(sv, v_wie, v_pw, v_woe, v_wio, v_woo)
    return (loss, grad_x[None], *grads, *deltas, *new_m, *new_v)


def _fwd_bwd(xs, tgt, ln_pre_even, h0, p0, g_wie, pool_full, pool_scale, ln_post_even, ln_pre_odd_f,
             sconv_f, dconv_f, dconv_b_f, cnorm_g_f, cnorm_b_f, ln_post_odd_f, comm, sb_dep):
    d = xs.shape[1]
    n_heads = d // 2 // HEAD_DIM
    ng, cwp = pool_full.shape[0], pool_full.shape[1]
    a0, sb_wts = sb_fwd(p0, n_heads, "sb_fwd", dep=sb_dep)
    dep = comm.weights_arrived("out_even", after=a0)
    y0 = even_mix_fwd(a0, p0, pool_full, pool_scale, "even_mix_fwd", dep=dep)
    (w_out_e,) = comm.weights("out_even", after=y0)
    w_out_e = w_out_e.reshape(1, d, d)
    o0 = mm_nn(y0, w_out_e, F32, "out_proj_even", tm=1024)
    dep = comm.weights_arrived("odd", after=o0)
    x1, h1 = postnorm_fwd(xs, o0, ln_post_even, ln_pre_odd_f, "post_even", dep=dep)
    g_wio, w_out_o = comm.weights("odd", after=x1)
    w_out_o = w_out_o.reshape(1, d, d)
    p1 = mm_nn(h1, g_wio, BF16, "in_proj_odd")
    y1, dc = odd_mix_fwd(p1, sconv_f, dconv_f, dconv_b_f, cnorm_g_f, cnorm_b_f, "odd_mix_fwd")
    o1 = mm_nn(y1, w_out_o, F32, "out_proj_odd", tm=1024)
    loss_blk, gx2, do1, dg_post_odd = final_fwd_bwd(x1, o1, ln_post_odd_f, tgt, "post_odd_loss")

    dw_out_o = mm_tn(y1, do1, 1, BF16, "dw_out_odd")
    dy1 = mm_nt(do1, w_out_o, BF16, "dy_odd")
    ddc, dg2, dgam, dbet = odd_bwd_ln(dy1, p1, dc, cnorm_g_f, cnorm_b_f, "odd_bwd_ln")
    dp1, dsconv, ddconv, ddconv_b = odd_bwd_conv(dy1, p1, ddc, dg2, sconv_f, dconv_f, "odd_bwd_conv")
    dw_in_o = mm_tn(h1, dp1, N_DEV, BF16, "dw_in_odd")
    dep = comm.reduce_begin({"w_out_odd": dw_out_o.reshape(N_DEV, d // N_DEV, d), "w_in_odd": dw_in_o}, "odd")
    dh1 = mm_nt(dp1, g_wio, F32, "dh_odd", dep=dep)
    dep = comm.reduce_send(after=dh1)
    gx1, dg_pre_odd, do0, dg_post_even = norm_bwd(dh1, x1, ln_pre_odd_f, gx2, "pre_odd_post_even_bwd",
                                                  inp2=o0, g2=ln_post_even, dep=dep)

    dw_out_e = mm_tn(y0, do0, 1, BF16, "dw_out_even")
    dy0 = mm_nt(do0, w_out_e, BF16, "dy_even")
    da0, du0, dg0, dpool, dpool_scale = even_mix_bwd(dy0, a0, p0, pool_full, pool_scale, "even_mix_bwd")
    pr = cwp // N_DEV
    dpool_slabs = dpool.astype(BF16).reshape(ng, N_DEV, pr, cwp).transpose(1, 0, 2, 3).reshape(N_DEV, ng * pr, cwp)
    dep = comm.reduce_begin({"w_out_even": dw_out_e.reshape(N_DEV, d // N_DEV, d), "pool_w": dpool_slabs}, "even_out")
    dq0, dk0, dv0 = sb_bwd(p0, a0, sb_wts, da0, n_heads, "sb_bwd", dep=dep)
    dep = comm.reduce_send(after=dq0)
    dp0 = jnp.concatenate([dq0, dk0, dv0, du0, dg0], axis=1)
    dw_sibling = mm_tn(h0, dp0, N_DEV // 2, BF16, "dw_in_even_sibling", dep=dep, pick=(2, 1 - comm.core))
    dep = comm.reduce_begin({"w_in_even": dw_sibling}, "even_in", sibling_part=True)
    dw_own = mm_tn(h0, dp0, N_DEV // 2, BF16, "dw_in_even_own", dep=dep, pick=(2, comm.core))
    dep = comm.reduce_send(after=dw_own, own_part={"w_in_even": dw_own})
    dh0 = mm_nt(dp0, g_wie, F32, "dh_even", dep=dep)
    dep = None
    grad_x, dg_pre_even = norm_bwd(dh0, xs, ln_pre_even, gx1, "pre_even_bwd", dep=dep)
    small_g = [dg_pre_even, dpool_scale, dg_post_even, dg_pre_odd, dsconv, ddconv, ddconv_b, dgam, dbet, dg_post_odd]
    return loss_blk, grad_x, small_g


class _Exchanges:
    def __init__(self, dev, core, d):
        self.dev = dev.astype(jnp.int32).reshape(1)
        self.core = core
        self.chip = (dev // 2).astype(jnp.int32).reshape(1)
        self.d = d
        self.in_flight = {}
        self.to_sibling = None
        self.pending = []

    def start_weights(self, tag, blocks, afters):
        lands = [lax.empty((N_DEV,) + b.shape, b.dtype) for b in blocks]
        send, recv, srcs, lands, token = split_start("gather", blocks, lands, afters, "ag_start_" + tag)
        self.in_flight[tag] = (send, recv, srcs, lands)
        return token

    def weights_arrived(self, tag, after):
        send, recv, srcs, lands = self.in_flight.pop(tag)
        srcs, lands = split_wait("gather", send, recv, srcs, lands, [after], "ag_wait_" + tag)
        lands = [place_block(l, b, self.dev, "ag_own_%s_%d" % (tag, k)) for k, (l, b) in enumerate(zip(lands, srcs))]
        lands = [l.reshape((4, 2) + l.shape[1:]) for l in lands]
        send, recv, _, lands, token = split_start("halves", [], lands, [], "ag_sibling_start_" + tag)
        self.in_flight[tag] = (send, recv, lands)
        return token

    def weights(self, tag, after):
        send, recv, lands = self.in_flight.pop(tag)
        _, lands = split_wait("halves", send, recv, [], lands, [after], "ag_sibling_wait_" + tag)
        return [l.reshape((N_DEV,) + l.shape[2:]) for l in lands]

    def reduce_begin(self, partials, tag, sibling_part=False):
        names = list(partials)
        arrs = [partials[k].reshape((4, 1 if sibling_part else 2) + partials[k].shape[1:]) for k in names]
        lands = [lax.empty((4, 1) + a.shape[2:], a.dtype) for a in arrs]
        send, recv, srcs, lands, token = split_start("sibling", arrs, lands, [], "rs_sibling_start_" + tag)
        self.to_sibling = (tag, names, send, recv, srcs, lands)
        return token

    def reduce_send(self, after, own_part=None):
        tag, names, send, recv, srcs, lands = self.to_sibling
        srcs, lands = split_wait("sibling", send, recv, srcs, lands, [after], "rs_sibling_wait_" + tag)
        which = self.core
        if own_part is not None:
            srcs = [own_part[k].reshape((4, 1) + own_part[k].shape[1:]) for k in names]
            which = jnp.zeros((1,), jnp.int32)
        sums = [pair_add(o, r, which, "rs_pair_add_" + k) for k, o, r in zip(names, srcs, lands)]
        zones = [lax.empty(a.shape, a.dtype) for a in sums]
        send, recv, srcs, zones, token = split_start("scatter", sums, zones, [], "rs_start_" + tag)
        self.pending.append((tag, names, send, recv, srcs, zones))
        return token

    def finish_updates(self, big, afters):
        tag, names, send, recv, srcs, lands = self.pending.pop(0)
        srcs, lands = split_wait("scatter", send, recv, srcs, lands, afters, "rs_wait_" + tag)
        out = {}
        for name, own, got in zip(names, srcs, lands):
            w, m, v = big[name]
            shp = own.shape[1:]
            outs = adamw_big(w.reshape(shp), m.reshape(shp), v.reshape(shp), own, got, self.chip, "adamw_" + name)
            out[name] = [o.reshape(w.shape) for o in outs]
        return out


def _update_small(small_g, loss_blk, small_w, small_m, small_v, dev, d, cl, deps):
    packed = jnp.concatenate([_rows128(g) for g in small_g] + [loss_blk], axis=0)
    (g8,) = all_gather([packed], "ag_small_grads", deps)
    tot = sum_devices(g8, "sum_small_grads")
    loss = tot[packed.shape[0] - 8, 0]
    full_g = []
    lo = 0
    for g in small_g:
        rows = g.size // 128
        full_g.append(tot[lo:lo + rows].reshape(g.shape))
        lo += rows

    def mine(g, width):
        return lax.dynamic_slice_in_dim(g, dev * width, width, axis=g.ndim - 1)

    fg = full_g
    small_gl = [fg[0], fg[1], fg[2], mine(fg[3], d // N_DEV), mine(fg[4], cl), mine(fg[5], cl), mine(fg[6], cl),
                mine(fg[7], cl), mine(fg[8], cl), mine(fg[9], d // N_DEV)]
    sd, sm, sv = adamw_small(small_w, small_gl, small_m, small_v, "adamw_small")

    def like(k, a):
        return a[None] if k in (4, 5) else a

    sg = [like(k, a) for k, a in enumerate(small_gl)]
    sd = [like(k, a) for k, a in enumerate(sd)]
    sm = [like(k, a) for k, a in enumerate(sm)]
    sv = [like(k, a) for k, a in enumerate(sv)]
    return sg, sd, sm, sv, loss
```

```python
import functools
import math

import jax
import jax.numpy as jnp
from jax import lax
from jax.experimental import pallas as pl
from jax.experimental.pallas import tpu as pltpu

F32 = jnp.float32
BF16 = jnp.bfloat16
EPS = 1e-6
HEAD_DIM = 128
POOL_WINDOWS = (2, 4, 8, 16)
SCONV_K = 3
CONF_K = 31
HALO = 32
N_DEV = 8
VMEM_LIMIT = 56 * 1024 * 1024
MESH = pl.DeviceIdType.MESH

ADAM_LR = 0.001
ADAM_B1 = 0.9
ADAM_B2 = 0.999
ADAM_EPS = 1e-08
ADAM_WD = 0.01
ADAM_STEP = 10


def _params(*sem):
    return pltpu.CompilerParams(dimension_semantics=sem, vmem_limit_bytes=VMEM_LIMIT)


def _sigmoid(v):
    return 1.0 / (1.0 + jnp.exp(-v))


def _silu(v):
    return v * _sigmoid(v)


def _silu_and_grad(v):
    s = _sigmoid(v)
    return v * s, s * (1.0 + v * (1.0 - s))


def _rowsum8(v):
    r, c = v.shape
    return jnp.sum(v.reshape(r // 8, 8, c), axis=0)


SUBLANES = 8


class _Taps:
    def __init__(self, xx, rows, before):
        self.xx, self.rows, self.before, self.rotated = xx, rows, before, {}

    def __call__(self, i):
        r, q = i % SUBLANES, i // SUBLANES
        if r not in self.rotated:
            n = self.xx.shape[0]
            self.rotated[r] = self.xx if r == 0 else pltpu.roll(self.xx, r if self.before else n - r, 0)
        lo = HALO - SUBLANES * q if self.before else SUBLANES * q
        return self.rotated[r][lo:lo + self.rows]


def _window_sum(xx, win, before):
    n = xx.shape[0]
    acc = xx
    k = 1
    while k < win:
        acc = acc + pltpu.roll(acc, k if before else n - k, 0)
        k *= 2
    return acc


def postnorm_fwd(x, o, g, g_next, name, tm=256, dep=None):
    s, d = x.shape
    dep_args, dep_specs = _after(dep)

    def body(x_ref, o_ref, g_ref, gn_ref, *rest):
        y_ref, h_ref = rest[-2:]
        ov = o_ref[...]
        r = lax.rsqrt(jnp.mean(ov * ov, axis=-1, keepdims=True) + EPS)
        y = x_ref[...] + ov * r * g_ref[...]
        y_ref[...] = y
        r2 = lax.rsqrt(jnp.mean(y * y, axis=-1, keepdims=True) + EPS)
        h_ref[...] = (y * r2 * gn_ref[...]).astype(BF16)

    row = pl.BlockSpec((tm, d), lambda i: (i, 0))
    vec = pl.BlockSpec((1, d), lambda i: (0, 0))
    return pl.pallas_call(
        body, name=name, grid=(s // tm,),
        in_specs=[row, row, vec, vec] + dep_specs, out_specs=[row, row],
        out_shape=[jax.ShapeDtypeStruct((s, d), F32), jax.ShapeDtypeStruct((s, d), BF16)],
        compiler_params=_params("parallel"),
    )(x, o, g, g_next, *dep_args)


def final_fwd_bwd(x1, o, g, target, name, tm=256):
    s, d = x1.shape
    n = s // tm

    def body(x_ref, o_ref, g_ref, t_ref, loss_ref, gx_ref, do_ref, dg_ref, lacc, gacc):
        i = pl.program_id(0)

        @pl.when(i == 0)
        def _():
            lacc[...] = jnp.zeros_like(lacc)
            gacc[...] = jnp.zeros_like(gacc)

        ov = o_ref[...]
        gv = g_ref[...]
        r = lax.rsqrt(jnp.mean(ov * ov, axis=-1, keepdims=True) + EPS)
        oh = ov * r
        diff = x_ref[...] + oh * gv - t_ref[...]
        lacc[...] += _rowsum8(diff * diff)
        gx = diff * (1.0 / d)
        gx_ref[...] = gx
        gacc[...] += _rowsum8(gx * oh)
        dn = gx * gv
        do_ref[...] = (r * (dn - oh * jnp.mean(dn * oh, axis=-1, keepdims=True))).astype(BF16)

        @pl.when(i == n - 1)
        def _():
            tot = jnp.sum(jnp.sum(lacc[...], axis=0, keepdims=True), axis=1, keepdims=True)
            loss_ref[...] = jnp.broadcast_to(tot * (0.5 / d), loss_ref.shape)
            dg_ref[...] = jnp.sum(gacc[...], axis=0, keepdims=True)

    row = pl.BlockSpec((tm, d), lambda i: (i, 0))
    vec = pl.BlockSpec((1, d), lambda i: (0, 0))
    return pl.pallas_call(
        body, name=name, grid=(n,),
        in_specs=[row, row, vec, row],
        out_specs=[pl.BlockSpec((8, 128), lambda i: (0, 0)), row, row, vec],
        out_shape=[jax.ShapeDtypeStruct((8, 128), F32), jax.ShapeDtypeStruct((s, d), F32),
                   jax.ShapeDtypeStruct((s, d), BF16), jax.ShapeDtypeStruct((1, d), F32)],
        scratch_shapes=[pltpu.VMEM((8, d), F32), pltpu.VMEM((8, d), F32)],
        compiler_params=_params("arbitrary"),
    )(x1, o, g, target)


def _rms_bwd_rows(dyv, xv, gv):
    r = lax.rsqrt(jnp.mean(xv * xv, axis=-1, keepdims=True) + EPS)
    xh = xv * r
    dn = dyv * gv
    return r * (dn - xh * jnp.mean(dn * xh, axis=-1, keepdims=True)), _rowsum8(dyv * xh)


def norm_bwd(dy, inp, g, resid, name, inp2=None, g2=None, tm=256, dep=None):
    s, d = inp.shape
    n = s // tm
    chain = inp2 is not None

    def body(*refs):
        dy_ref, x_ref, g_ref, r_ref = refs[:4]
        outs = refs[-6:] if chain else refs[-3:]
        i = pl.program_id(0)

        @pl.when(i == 0)
        def _():
            for acc in outs[-2:] if chain else outs[-1:]:
                acc[...] = jnp.zeros_like(acc)

        if chain:
            x2_ref, g2_ref = refs[4:6]
            dx_ref, dg_ref, dx2_ref, dg2_ref, gacc, gacc2 = outs
        else:
            dx_ref, dg_ref, gacc = outs
        dx, dg_rows = _rms_bwd_rows(dy_ref[...].astype(F32), x_ref[...], g_ref[...])
        dx = dx + r_ref[...]
        dx_ref[...] = dx
        gacc[...] += dg_rows
        if chain:
            dx2, dg2_rows = _rms_bwd_rows(dx, x2_ref[...], g2_ref[...])
            dx2_ref[...] = dx2.astype(BF16)
            gacc2[...] += dg2_rows

        @pl.when(i == n - 1)
        def _():
            dg_ref[...] = jnp.sum(gacc[...], axis=0, keepdims=True)
            if chain:
                dg2_ref[...] = jnp.sum(gacc2[...], axis=0, keepdims=True)

    row = pl.BlockSpec((tm, d), lambda i: (i, 0))
    vec = pl.BlockSpec((1, d), lambda i: (0, 0))
    dep_args, dep_specs = _after(dep)
    extra = [inp2, g2] if chain else []
    return pl.pallas_call(
        body, name=name, grid=(n,),
        in_specs=[row, row, vec, row] + ([row, vec] if chain else []) + dep_specs,
        out_specs=[row, vec] * (2 if chain else 1),
        out_shape=[jax.ShapeDtypeStruct((s, d), F32), jax.ShapeDtypeStruct((1, d), F32)]
        + ([jax.ShapeDtypeStruct((s, d), BF16), jax.ShapeDtypeStruct((1, d), F32)] if chain else []),
        scratch_shapes=[pltpu.VMEM((8, d), F32)] * (2 if chain else 1),
        compiler_params=_params("arbitrary"),
    )(dy, inp, g, resid, *extra, *dep_args)


def _after(dep):
    if dep is None:
        return [], []
    return [dep], [pl.BlockSpec((8, 128), lambda *_: (0, 0))]


def _lane_concat(ref, count):
    return ref[0] if count == 1 else jnp.concatenate([ref[i] for i in range(count)], axis=1)


def mm_nn(a, w, out_dtype, name, tm=2048, tn=None, dep=None, group=1):
    m, k = a.shape
    tm = min(tm, m)
    ns, _, n = w.shape
    tn = n if tn is None else tn
    nj = n // tn
    assert group == 1 or nj == 1
    dep_args, dep_specs = _after(dep)

    def body(a_ref, w_ref, *rest):
        o_ref = rest[-1]
        o_ref[...] = jnp.dot(a_ref[...], _lane_concat(w_ref, group), preferred_element_type=F32).astype(out_dtype)

    return pl.pallas_call(
        body, name=name, grid=(ns // group, nj, m // tm),
        in_specs=[pl.BlockSpec((tm, k), lambda s, j, i: (i, 0)),
                  pl.BlockSpec((group, k, tn), lambda s, j, i: (s, 0, j))] + dep_specs,
        out_specs=pl.BlockSpec((tm, group * tn), lambda s, j, i: (i, s * nj + j)),
        out_shape=jax.ShapeDtypeStruct((m, ns * n), out_dtype),
        compiler_params=_params("parallel", "parallel", "parallel"),
    )(a, w, *dep_args)


def mm_nt(a, w, out_dtype, name, tm=1024, tn=None, dep=None, group=1):
    m = a.shape[0]
    tm = min(tm, m)
    ns, k, n = w.shape
    tn = n if tn is None else tn
    nj = n // tn
    assert group == 1 or nj == 1
    steps = ns * nj // group
    dep_args, dep_specs = _after(dep)

    def body(a_ref, w_ref, *rest):
        o_ref, acc = rest[-2:]
        r = pl.program_id(1)

        @pl.when(r == 0)
        def _():
            acc[...] = jnp.zeros_like(acc)

        acc[...] += lax.dot_general(a_ref[...], _lane_concat(w_ref, group), (((1,), (1,)), ((), ())),
                                    preferred_element_type=F32)

        @pl.when(r == steps - 1)
        def _():
            o_ref[...] = acc[...].astype(out_dtype)

    return pl.pallas_call(
        body, name=name, grid=(m // tm, steps),
        in_specs=[pl.BlockSpec((tm, group * tn), lambda i, r: (i, r)),
                  pl.BlockSpec((group, k, tn), lambda i, r: (r // nj, 0, r % nj))] + dep_specs,
        out_specs=pl.BlockSpec((tm, k), lambda i, r: (i, 0)),
        out_shape=jax.ShapeDtypeStruct((m, k), out_dtype),
        scratch_shapes=[pltpu.VMEM((tm, k), F32)],
        compiler_params=_params("parallel", "arbitrary"),
    )(a, w, *dep_args)


def mm_tn(a, b, ns, out_dtype, name, tk=1024, tm=2048, dep=None, pick=None, group=1):
    m, k = a.shape
    tm = min(tm, m)
    step, offset = (1, None) if pick is None else pick
    assert group == 1 or pick is None
    n = b.shape[1] // (ns * step)
    steps = m // tm
    dep_args, dep_specs = _after(dep)
    n_pre = 0 if pick is None else 1

    def b_block(s, j, r, *pre):
        return (r, s if pick is None else step * s + pre[0][0])

    def body(*refs):
        a_ref, b_ref = refs[n_pre:n_pre + 2]
        o_ref, acc = refs[-2:]
        r = pl.program_id(2)

        @pl.when(r == 0)
        def _():
            acc[...] = jnp.zeros_like(acc)

        acc[...] += lax.dot_general(a_ref[...], b_ref[...], (((0,), (0,)), ((), ())),
                                    preferred_element_type=F32)

        @pl.when(r == steps - 1)
        def _():
            for i in range(group):
                o_ref[i] = acc[:, i * n:(i + 1) * n].astype(out_dtype)

    return pl.pallas_call(
        body, name=name,
        grid_spec=pltpu.PrefetchScalarGridSpec(
            num_scalar_prefetch=n_pre, grid=(ns // group, k // tk, steps),
            in_specs=[pl.BlockSpec((tm, tk), lambda s, j, r, *pre: (r, j)),
                      pl.BlockSpec((tm, group * n), b_block)] + dep_specs,
            out_specs=pl.BlockSpec((group, tk, n), lambda s, j, r, *pre: (s, j, 0)),
            scratch_shapes=[pltpu.VMEM((tk, group * n), F32)]),
        out_shape=jax.ShapeDtypeStruct((ns, k, n), out_dtype),
        compiler_params=_params("parallel", "parallel", "arbitrary"),
    )(*([] if pick is None else [offset]), a, b, *dep_args)


SB_BLK = 128


LOG2E = 1.0 / math.log(2.0)


def _split_dot(v, tri2):
    hi = pltpu.bitcast(pltpu.bitcast(v, jnp.uint32) & jnp.uint32(0xFFFF0000), F32)
    lo = (v - hi).astype(BF16)
    return jnp.dot(jnp.concatenate([hi.astype(BF16), lo], axis=1), tri2, preferred_element_type=F32)


def _sb_scores(z2, lim, dcol, tri_ex, masked):
    sp = jnp.log2(1.0 + jnp.exp2(-jnp.abs(z2)))
    lb = jnp.minimum(z2, 0.0) - sp
    l1m = lb - z2
    mask = None
    if masked:
        mask = dcol < lim
        l1m = jnp.where(mask, l1m, 0.0)
    return mask, lb, l1m, _split_dot(l1m, tri_ex)


def _sb_consts():
    row = lax.broadcasted_iota(jnp.int32, (SB_BLK, SB_BLK), 0)
    col = lax.broadcasted_iota(jnp.int32, (SB_BLK, SB_BLK), 1)
    tri_ex = jnp.where(row > col, 1.0, 0.0).astype(BF16)
    tri_in = jnp.where(row >= col, 1.0, 0.0).astype(BF16)
    return col - row, jnp.concatenate([tri_ex, tri_ex], axis=0), jnp.concatenate([tri_in, tri_in], axis=0)


def sb_fwd(p, n_heads, name, tq=512, nsub=8, dep=None):
    s = p.shape[0]
    h_n = n_heads
    b = SB_BLK
    nqs = tq // b
    tk = nsub * b
    scale = 1.0 / math.sqrt(HEAD_DIM)

    dep_args, dep_specs = _after(dep)

    def body(q_ref, k_ref, v_ref, *rest):
        o_ref, w_ref = rest[-2:]
        qi = pl.program_id(1)
        dcol, tri_ex, _ = _sb_consts()
        qv = [q_ref[qs * b:(qs + 1) * b, :] for qs in range(nqs)]
        n_groups = ((qi + 1) * nqs - 1) // nsub + 1

        def step(it, carry, masked):
            c1s, accs = carry
            g = n_groups - 1 - it
            off = pl.multiple_of(g * tk, tk)
            kg = k_ref[pl.ds(off, tk), :]
            vg = v_ref[pl.ds(off, tk), :]
            new_c1, new_acc = [], []
            for qs in range(nqs):
                qb = qi * nqs + qs
                z2 = lax.dot_general(qv[qs], kg, (((1,), (1,)), ((), ())),
                                     preferred_element_type=F32) * (scale * LOG2E)
                blocks = [_sb_scores(z2[:, j * b:(j + 1) * b], (qb - (g * nsub + j)) * b, dcol, tri_ex, masked)
                          for j in range(nsub)]
                run = c1s[qs]
                ws = [None] * nsub
                for j in reversed(range(nsub)):
                    mask, lb, l1m, ls_loc = blocks[j]
                    wj = jnp.exp2(lb + ls_loc + run)
                    ws[j] = (jnp.where(mask, wj, 0.0) if masked else wj).astype(BF16)
                    run = run + jnp.sum(l1m, axis=1, keepdims=True)
                w = jnp.concatenate(ws, axis=1)
                w_ref[0, g, qs * b:(qs + 1) * b, :] = w
                new_acc.append(accs[qs] + jnp.dot(w, vg, preferred_element_type=F32))
                new_c1.append(run)
            return tuple(new_c1), tuple(new_acc)

        init = (tuple(jnp.zeros((b, 1), F32) for _ in range(nqs)),
                tuple(jnp.zeros((b, HEAD_DIM), F32) for _ in range(nqs)))
        assert all(((i + 1) * nqs - 1) // nsub * nsub <= i * nqs for i in range(s // tq))
        first = step(0, init, True)
        _, accs = lax.fori_loop(1, n_groups, functools.partial(step, masked=False), first)
        for qs in range(nqs):
            o_ref[qs * b:(qs + 1) * b, :] = accs[qs]

    return pl.pallas_call(
        body, name=name, grid=(h_n, s // tq),
        in_specs=[pl.BlockSpec((tq, HEAD_DIM), lambda h, i: (i, h)),
                  pl.BlockSpec((s, HEAD_DIM), lambda h, i: (0, h_n + h)),
                  pl.BlockSpec((s, HEAD_DIM), lambda h, i: (0, 2 * h_n + h))] + dep_specs,
        out_specs=[pl.BlockSpec((tq, HEAD_DIM), lambda h, i: (i, h)),
                   pl.BlockSpec((1, s // tk, tq, tk), lambda h, i: (h, 0, i, 0))],
        out_shape=[jax.ShapeDtypeStruct((s, h_n * HEAD_DIM), F32),
                   jax.ShapeDtypeStruct((h_n, s // tk, s, tk), BF16)],
        compiler_params=_params("parallel", "arbitrary"),
    )(p, p, p, *dep_args)


def sb_bwd(p, a, wts, da, n_heads, name, tq=512, dep=None):
    s = p.shape[0]
    h_n = n_heads
    nq = s // tq
    b = SB_BLK
    nqs = tq // b
    tk = wts.shape[3]
    nsub = tk // b
    scale = 1.0 / math.sqrt(HEAD_DIM)
    dep_args, dep_specs = _after(dep)

    def body(q_ref, k_ref, v_ref, a_ref, da_ref, w_ref, *rest):
        dq_ref, dk_ref, dv_ref, dk_acc, dv_acc = rest[-5:]
        qi = pl.program_id(1)

        @pl.when(qi == 0)
        def _():
            dk_acc[...] = jnp.zeros_like(dk_acc)
            dv_acc[...] = jnp.zeros_like(dv_acc)

        dcol, _, tri_in = _sb_consts()
        q_all = q_ref[...]
        do_all = da_ref[...]
        qv = [q_ref[qs * b:(qs + 1) * b, :] for qs in range(nqs)]
        dov = [da_ref[qs * b:(qs + 1) * b, :] for qs in range(nqs)]
        tots = [jnp.sum(dov[qs].astype(F32) * a_ref[qs * b:(qs + 1) * b, :], axis=1, keepdims=True)
                for qs in range(nqs)]
        n_groups = ((qi + 1) * nqs - 1) // nsub + 1

        def step(it, carry, masked):
            c2s, dqs = carry
            g = n_groups - 1 - it
            off = pl.multiple_of(g * tk, tk)
            kg = k_ref[pl.ds(off, tk), :]
            vg = v_ref[pl.ds(off, tk), :]
            w_all = w_ref[0, g]
            new_c2, new_dq, dz_rows = [], [], []
            for qs in range(nqs):
                qb = qi * nqs + qs
                z2 = lax.dot_general(qv[qs], kg, (((1,), (1,)), ((), ())),
                                     preferred_element_type=F32) * (-scale * LOG2E)
                dw = lax.dot_general(dov[qs], vg, (((1,), (1,)), ((), ())), preferred_element_type=F32)
                beta = 1.0 / (1.0 + jnp.exp2(z2))
                e = dw * w_all[qs * b:(qs + 1) * b, :].astype(F32)
                run2 = c2s[qs]
                dzs = [None] * nsub
                for j in reversed(range(nsub)):
                    cols = slice(j * b, (j + 1) * b)
                    later = _split_dot(e[:, cols], tri_in) + run2
                    bj = beta[:, cols]
                    dz = (e[:, cols] * (1.0 - bj) - bj * (tots[qs] - later)) * scale
                    if masked:
                        dz = jnp.where(dcol < (qb - (g * nsub + j)) * b, dz, 0.0)
                    dzs[j] = dz.astype(BF16)
                    run2 = run2 + jnp.sum(e[:, cols], axis=1, keepdims=True)
                dzq = jnp.concatenate(dzs, axis=1)
                new_dq.append(dqs[qs] + jnp.dot(dzq, kg, preferred_element_type=F32))
                new_c2.append(run2)
                dz_rows.append(dzq)
            dz_all = jnp.concatenate(dz_rows, axis=0)
            dk_acc[pl.ds(off, tk), :] += lax.dot_general(dz_all, q_all, (((0,), (0,)), ((), ())),
                                                         preferred_element_type=F32)
            dv_acc[pl.ds(off, tk), :] += lax.dot_general(w_all, do_all, (((0,), (0,)), ((), ())),
                                                         preferred_element_type=F32)
            return tuple(new_c2), tuple(new_dq)

        zeros = tuple(jnp.zeros((b, 1), F32) for _ in range(nqs))
        assert all(((i + 1) * nqs - 1) // nsub * nsub <= i * nqs for i in range(s // tq))
        first = step(0, (zeros, tuple(jnp.zeros((b, HEAD_DIM), F32) for _ in range(nqs))), True)
        _, dqs = lax.fori_loop(1, n_groups, functools.partial(step, masked=False), first)
        for qs in range(nqs):
            dq_ref[qs * b:(qs + 1) * b, :] = dqs[qs].astype(BF16)

        @pl.when(qi == nq - 1)
        def _():
            dk_ref[...] = dk_acc[...].astype(BF16)
            dv_ref[...] = dv_acc[...].astype(BF16)

    blk = pl.BlockSpec((tq, HEAD_DIM), lambda h, i: (i, h))
    full = pl.BlockSpec((s, HEAD_DIM), lambda h, i: (0, h))
    return pl.pallas_call(
        body, name=name, grid=(h_n, nq),
        in_specs=[blk, pl.BlockSpec((s, HEAD_DIM), lambda h, i: (0, h_n + h)),
                  pl.BlockSpec((s, HEAD_DIM), lambda h, i: (0, 2 * h_n + h)), blk, blk,
                  pl.BlockSpec((1, s // tk, tq, tk), lambda h, i: (h, 0, i, 0))] + dep_specs,
        out_specs=[blk, full, full],
        out_shape=[jax.ShapeDtypeStruct((s, h_n * HEAD_DIM), BF16)] * 3,
        scratch_shapes=[pltpu.VMEM((s, HEAD_DIM), F32), pltpu.VMEM((s, HEAD_DIM), F32)],
        compiler_params=_params("parallel", "arbitrary"),
    )(p, p, p, a, da, wts, *dep_args)


def _pool_window(xx, win, r0, rc):
    cur = xx[HALO:HALO + rc]
    ws = _window_sum(xx, win, True)[HALO:HALO + rc]
    t_idx = r0 + lax.broadcasted_iota(jnp.int32, (rc, 1), 0)
    inv = 1.0 / jnp.minimum(win, t_idx + 1).astype(F32)
    return ws * inv - cur, inv


def even_mix_fwd(a, p, pool_w, pool_scale, name, rc=64, dep=None):
    s = p.shape[0]
    ng = len(POOL_WINDOWS)
    cw = pool_w.shape[1]
    n_chunks = s // rc
    dep_args, dep_specs = _after(dep)

    def body(a_ref, u_ref, g_ref, w_ref, sc_ref, *rest):
        y_ref, upad = rest[-2:]
        j = pl.program_id(0)

        @pl.when(j < ng)
        def _():
            def chunk(ci, carry):
                rows = pl.ds(pl.multiple_of(ci * rc, rc), rc)
                y_ref[rows, :] = (a_ref[rows, :] * _silu(g_ref[rows, :].astype(F32))).astype(BF16)
                return carry

            lax.fori_loop(0, n_chunks, chunk, 0)

        for gi, win in enumerate(POOL_WINDOWS):
            @pl.when(j == ng + gi)
            def _(win=win):
                upad[0:HALO, :] = jnp.zeros((HALO, cw), F32)

                def fill(ci, carry):
                    r0 = pl.multiple_of(ci * rc, rc)
                    upad[pl.ds(pl.multiple_of(r0 + HALO, HALO), rc), :] = u_ref[pl.ds(r0, rc), :].astype(F32)
                    return carry

                lax.fori_loop(0, n_chunks, fill, 0)

                def chunk(ci, carry):
                    r0 = pl.multiple_of(ci * rc, rc)
                    rows = pl.ds(r0, rc)
                    pooled, _ = _pool_window(upad[pl.ds(r0, HALO + rc), :], win, r0, rc)
                    t = jnp.dot(pooled.astype(BF16), w_ref[0], preferred_element_type=F32)
                    y_ref[rows, :] = (t * sc_ref[...] * _silu(g_ref[rows, :].astype(F32))).astype(BF16)
                    return carry

                lax.fori_loop(0, n_chunks, chunk, 0)

    grp = lambda j: jnp.maximum(j - ng, 0)
    return pl.pallas_call(
        body, name=name, grid=(2 * ng,),
        in_specs=[pl.BlockSpec((s, cw), lambda j: (0, jnp.minimum(j, ng - 1))),
                  pl.BlockSpec((s, cw), lambda j: (0, 3 * ng + grp(j))),
                  pl.BlockSpec((s, cw), lambda j: (0, 4 * ng + j)),
                  pl.BlockSpec((1, cw, cw), lambda j: (grp(j), 0, 0)),
                  pl.BlockSpec((1, cw), lambda j: (0, grp(j)))] + dep_specs,
        out_specs=pl.BlockSpec((s, cw), lambda j: (0, j)),
        out_shape=jax.ShapeDtypeStruct((s, 2 * ng * cw), BF16),
        scratch_shapes=[pltpu.VMEM((HALO + s, cw), F32)],
        compiler_params=_params("arbitrary"),
    )(a, p, p, pool_w, pool_scale, *dep_args)


def even_mix_bwd(dy, a, p, pool_w, pool_scale, name, rc=64):
    s = p.shape[0]
    ng = len(POOL_WINDOWS)
    cw = pool_w.shape[1]
    n_chunks = s // rc

    def body(dy_ref, a_ref, u_ref, g_ref, w_ref, sc_ref, da_ref, du_ref, dg_ref, dw_ref, dsc_ref,
             upad, rpad, dpl, dw_acc, dsc_acc):
        j = pl.program_id(0)

        @pl.when(j < ng)
        def _():
            def chunk(ci, carry):
                rows = pl.ds(pl.multiple_of(ci * rc, rc), rc)
                dyv = dy_ref[rows, :].astype(F32)
                sg, dsg = _silu_and_grad(g_ref[rows, :].astype(F32))
                da_ref[rows, :] = (dyv * sg).astype(BF16)
                dg_ref[rows, :] = (dyv * a_ref[rows, :] * dsg).astype(BF16)
                return carry

            lax.fori_loop(0, n_chunks, chunk, 0)

        for gi, win in enumerate(POOL_WINDOWS):
            @pl.when(j == ng + gi)
            def _(win=win):
                upad[0:HALO, :] = jnp.zeros((HALO, cw), F32)
                rpad[s:s + HALO, :] = jnp.zeros((HALO, cw), F32)
                dw_acc[...] = jnp.zeros_like(dw_acc)
                dsc_acc[...] = jnp.zeros_like(dsc_acc)

                def fill(ci, carry):
                    r0 = pl.multiple_of(ci * rc, rc)
                    upad[pl.ds(pl.multiple_of(r0 + HALO, HALO), rc), :] = u_ref[pl.ds(r0, rc), :].astype(F32)
                    return carry

                lax.fori_loop(0, n_chunks, fill, 0)

                def chunk(ci, carry):
                    r0 = pl.multiple_of(ci * rc, rc)
                    rows = pl.ds(r0, rc)
                    pooled, inv = _pool_window(upad[pl.ds(r0, HALO + rc), :], win, r0, rc)
                    pb = pooled.astype(BF16)
                    wv = w_ref[0]
                    t = jnp.dot(pb, wv, preferred_element_type=F32)
                    scv = sc_ref[...]
                    dyv = dy_ref[rows, :].astype(F32)
                    sg, dsg = _silu_and_grad(g_ref[rows, :].astype(F32))
                    dpo = dyv * sg
                    dg_ref[rows, :] = (dyv * t * scv * dsg).astype(BF16)
                    dsc_acc[...] += _rowsum8(dpo * t)
                    dtb = (dpo * scv).astype(BF16)
                    dw_acc[...] += lax.dot_general(pb, dtb, (((0,), (0,)), ((), ())),
                                                   preferred_element_type=F32)
                    dpooled = lax.dot_general(dtb, wv, (((1,), (1,)), ((), ())),
                                              preferred_element_type=F32)
                    dpl[rows, :] = dpooled
                    rpad[rows, :] = dpooled * inv
                    return carry

                lax.fori_loop(0, n_chunks, chunk, 0)

                def chunk2(ci, carry):
                    r0 = pl.multiple_of(ci * rc, rc)
                    rows = pl.ds(r0, rc)
                    xx = rpad[pl.ds(r0, rc + HALO), :]
                    fs = _window_sum(xx, win, False)[0:rc]
                    du_ref[rows, :] = (fs - dpl[rows, :]).astype(BF16)
                    return carry

                lax.fori_loop(0, n_chunks, chunk2, 0)
                dw_ref[0] = dw_acc[...]
                dsc_ref[...] = jnp.sum(dsc_acc[...], axis=0, keepdims=True)

    grp = lambda j: jnp.maximum(j - ng, 0)
    att = lambda j: jnp.minimum(j, ng - 1)
    return pl.pallas_call(
        body, name=name, grid=(2 * ng,),
        in_specs=[pl.BlockSpec((s, cw), lambda j: (0, j)),
                  pl.BlockSpec((s, cw), lambda j: (0, att(j))),
                  pl.BlockSpec((s, cw), lambda j: (0, 3 * ng + grp(j))),
                  pl.BlockSpec((s, cw), lambda j: (0, 4 * ng + j)),
                  pl.BlockSpec((1, cw, cw), lambda j: (grp(j), 0, 0)),
                  pl.BlockSpec((1, cw), lambda j: (0, grp(j)))],
        out_specs=[pl.BlockSpec((s, cw), lambda j: (0, att(j))),
                   pl.BlockSpec((s, cw), lambda j: (0, grp(j))),
                   pl.BlockSpec((s, cw), lambda j: (0, j)),
                   pl.BlockSpec((1, cw, cw), lambda j: (grp(j), 0, 0)),
                   pl.BlockSpec((1, cw), lambda j: (0, grp(j)))],
        out_shape=[jax.ShapeDtypeStruct((s, ng * cw), BF16), jax.ShapeDtypeStruct((s, ng * cw), BF16),
                   jax.ShapeDtypeStruct((s, 2 * ng * cw), BF16),
                   jax.ShapeDtypeStruct((ng, cw, cw), F32), jax.ShapeDtypeStruct((1, ng * cw), F32)],
        scratch_shapes=[pltpu.VMEM((HALO + s, cw), F32), pltpu.VMEM((s + HALO, cw), F32),
                        pltpu.VMEM((s, cw), F32), pltpu.VMEM((cw, cw), F32), pltpu.VMEM((8, cw), F32)],
        compiler_params=_params("arbitrary"),
    )(dy, a, p, p, pool_w, pool_scale)


def _halo_before(tm):
    return lambda i: jnp.maximum(i * (tm // HALO) - 1, 0)


def _halo_after(tm, s):
    return lambda i: jnp.minimum((i + 1) * (tm // HALO), s // HALO - 1)


def odd_mix_fwd(p, sconv_w, dconv_w, dconv_b, cnorm_g, cnorm_b, name, tm=128):
    s = p.shape[0]
    cw = sconv_w.shape[1]
    n = s // tm
    lanes = 128
    hb = _halo_before(tm)

    def body(hc_ref, hch_ref, bc_ref, cc_ref, cch_ref, ga_ref, gah_ref, gb_ref, gbh_ref, g1_ref, g2_ref,
             sw_ref, dw_ref, db_ref, gam_ref, bet_ref, y_ref, dc_ref):
        first = pl.program_id(0) == 0
        for l in range(cw // lanes):
            cols = slice(l * lanes, (l + 1) * lanes)
            mh = jnp.where(first, 0.0, cch_ref[:, cols].astype(F32) * hch_ref[:, cols].astype(F32))
            mm = cc_ref[:, cols].astype(F32) * hc_ref[:, cols].astype(F32)
            xx = jnp.concatenate([mh, mm], axis=0)
            tap = _Taps(xx, tm, True)
            cv = jnp.zeros((tm, lanes), F32)
            for k in range(SCONV_K):
                cv = cv + sw_ref[k:k + 1, cols] * tap(SCONV_K - 1 - k)
            c_out = bc_ref[:, cols].astype(F32) * cv
            y_ref[:, cols] = (c_out * _silu(g1_ref[:, cols].astype(F32))).astype(BF16)
            dh = jnp.where(first, 0.0, gah_ref[:, cols].astype(F32) * _sigmoid(gbh_ref[:, cols].astype(F32)))
            dm = ga_ref[:, cols].astype(F32) * _sigmoid(gb_ref[:, cols].astype(F32))
            xx = jnp.concatenate([dh, dm], axis=0)
            tap = _Taps(xx, tm, True)
            acc = jnp.zeros((tm, lanes), F32) + db_ref[:, cols]
            for k in range(CONF_K):
                acc = acc + dw_ref[k:k + 1, cols] * tap(CONF_K - 1 - k)
            dc_ref[:, cols] = acc
        rs = 32
        for r in range(tm // rs):
            rows = slice(r * rs, (r + 1) * rs)
            xv = dc_ref[rows, :]
            mu = jnp.mean(xv, axis=-1, keepdims=True)
            xc = xv - mu
            rstd = lax.rsqrt(jnp.mean(xc * xc, axis=-1, keepdims=True) + EPS)
            ln = xc * rstd * gam_ref[...] + bet_ref[...]
            y_ref[rows, cw:2 * cw] = (_silu(ln) * _silu(g2_ref[rows, :].astype(F32))).astype(BF16)

    main = lambda c: pl.BlockSpec((tm, cw), lambda i: (i, c))
    halo = lambda c: pl.BlockSpec((HALO, cw), lambda i: (hb(i), c))
    vec = lambda r: pl.BlockSpec((r, cw), lambda i: (0, 0))
    return pl.pallas_call(
        body, name=name, grid=(n,),
        in_specs=[main(0), halo(0), main(1), main(2), halo(2), main(3), halo(3), main(4), halo(4),
                  main(5), main(6), vec(SCONV_K), vec(CONF_K), vec(1), vec(1), vec(1)],
        out_specs=[pl.BlockSpec((tm, 2 * cw), lambda i: (i, 0)), pl.BlockSpec((tm, cw), lambda i: (i, 0))],
        out_shape=[jax.ShapeDtypeStruct((s, 2 * cw), BF16), jax.ShapeDtypeStruct((s, cw), F32)],
        compiler_params=_params("parallel"),
    )(p, p, p, p, p, p, p, p, p, p, p, sconv_w, dconv_w, dconv_b, cnorm_g, cnorm_b)


def odd_bwd_ln(dy, p, dc, cnorm_g, cnorm_b, name, tm=256):
    s = p.shape[0]
    cw = dc.shape[1]
    n = s // tm
    rs = 32

    def body(dy_ref, g2_ref, dc_ref, gam_ref, bet_ref, ddc_ref, dg_ref, dgam_ref, dbet_ref, gacc, bacc):
        i = pl.program_id(0)

        @pl.when(i == 0)
        def _():
            gacc[...] = jnp.zeros_like(gacc)
            bacc[...] = jnp.zeros_like(bacc)

        def chunk(ci, carry):
            rows = pl.ds(pl.multiple_of(ci * rs, rs), rs)
            xv = dc_ref[rows, :]
            mu = jnp.mean(xv, axis=-1, keepdims=True)
            xc = xv - mu
            rstd = lax.rsqrt(jnp.mean(xc * xc, axis=-1, keepdims=True) + EPS)
            xh = xc * rstd
            gam = gam_ref[...]
            sl, dsl = _silu_and_grad(xh * gam + bet_ref[...])
            sg, dsg = _silu_and_grad(g2_ref[rows, :].astype(F32))
            dyv = dy_ref[rows, :].astype(F32)
            dg_ref[rows, :] = (dyv * sl * dsg).astype(BF16)
            dln = dyv * sg * dsl
            gacc[...] += _rowsum8(dln * xh)
            bacc[...] += _rowsum8(dln)
            dxh = dln * gam
            ddc_ref[rows, :] = rstd * (dxh - jnp.mean(dxh, axis=-1, keepdims=True)
                                       - xh * jnp.mean(dxh * xh, axis=-1, keepdims=True))
            return carry

        lax.fori_loop(0, tm // rs, chunk, 0)

        @pl.when(i == n - 1)
        def _():
            dgam_ref[...] = jnp.sum(gacc[...], axis=0, keepdims=True)
            dbet_ref[...] = jnp.sum(bacc[...], axis=0, keepdims=True)

    vec = pl.BlockSpec((1, cw), lambda i: (0, 0))
    return pl.pallas_call(
        body, name=name, grid=(n,),
        in_specs=[pl.BlockSpec((tm, cw), lambda i: (i, 1)), pl.BlockSpec((tm, cw), lambda i: (i, 6)),
                  pl.BlockSpec((tm, cw), lambda i: (i, 0)), vec, vec],
        out_specs=[pl.BlockSpec((tm, cw), lambda i: (i, 0)), pl.BlockSpec((tm, cw), lambda i: (i, 0)), vec, vec],
        out_shape=[jax.ShapeDtypeStruct((s, cw), F32), jax.ShapeDtypeStruct((s, cw), BF16),
                   jax.ShapeDtypeStruct((1, cw), F32), jax.ShapeDtypeStruct((1, cw), F32)],
        scratch_shapes=[pltpu.VMEM((8, cw), F32), pltpu.VMEM((8, cw), F32)],
        compiler_params=_params("arbitrary"),
    )(dy, p, dc, cnorm_g, cnorm_b)


def odd_bwd_conv(dy, p, ddc, dg2, sconv_w, dconv_w, name, tm=128):
    s = p.shape[0]
    cw = ddc.shape[1]
    n = s // tm
    lanes = 128
    hb = _halo_before(tm)
    ha = _halo_after(tm, s)

    def body(dy_ref, dya_ref, g1_ref, g1a_ref, bc_ref, bca_ref, hc_ref, hch_ref, cc_ref, cch_ref,
             ddc_ref, ddca_ref, ga_ref, gah_ref, gb_ref, gbh_ref, dg2_ref, sw_ref, dw_ref,
             dp_ref, dsw_ref, ddw_ref, ddb_ref, sw_acc, dw_acc, db_acc):
        i = pl.program_id(0)
        first = i == 0
        last = i == n - 1

        @pl.when(first)
        def _():
            sw_acc[...] = jnp.zeros_like(sw_acc)
            dw_acc[...] = jnp.zeros_like(dw_acc)
            db_acc[...] = jnp.zeros_like(db_acc)

        for l in range(cw // lanes):
            cols = slice(l * lanes, (l + 1) * lanes)
            mh = jnp.where(first, 0.0, cch_ref[:, cols].astype(F32) * hch_ref[:, cols].astype(F32))
            hcv = hc_ref[:, cols].astype(F32)
            ccv = cc_ref[:, cols].astype(F32)
            xx = jnp.concatenate([mh, ccv * hcv], axis=0)
            tap = _Taps(xx, tm, True)
            taps = [tap(SCONV_K - 1 - k) for k in range(SCONV_K)]
            cv = jnp.zeros((tm, lanes), F32)
            for k in range(SCONV_K):
                cv = cv + sw_ref[k:k + 1, cols] * taps[k]
            bcv = bc_ref[:, cols].astype(F32)
            dyv = dy_ref[:, cols].astype(F32)
            sg, dsg = _silu_and_grad(g1_ref[:, cols].astype(F32))
            dco = dyv * sg
            dp_ref[:, 5 * cw + l * lanes:5 * cw + (l + 1) * lanes] = (dyv * bcv * cv * dsg).astype(BF16)
            dp_ref[:, cw + l * lanes:cw + (l + 1) * lanes] = (dco * cv).astype(BF16)
            dcv = dco * bcv
            for k in range(SCONV_K):
                sw_acc[k * 8:(k + 1) * 8, cols] += _rowsum8(dcv * taps[k])
            dcv_a = jnp.where(last, 0.0, dya_ref[:, cols].astype(F32) * _silu(g1a_ref[:, cols].astype(F32))
                              * bca_ref[:, cols].astype(F32))
            xx = jnp.concatenate([dcv, dcv_a], axis=0)
            tap = _Taps(xx, tm, False)
            dm = jnp.zeros((tm, lanes), F32)
            for k in range(SCONV_K):
                dm = dm + sw_ref[k:k + 1, cols] * tap(SCONV_K - 1 - k)
            dp_ref[:, l * lanes:(l + 1) * lanes] = (dm * ccv).astype(BF16)
            dp_ref[:, 2 * cw + l * lanes:2 * cw + (l + 1) * lanes] = (dm * hcv).astype(BF16)
            gav = ga_ref[:, cols].astype(F32)
            sb = _sigmoid(gb_ref[:, cols].astype(F32))
            dh = jnp.where(first, 0.0, gah_ref[:, cols].astype(F32) * _sigmoid(gbh_ref[:, cols].astype(F32)))
            xx = jnp.concatenate([dh, gav * sb], axis=0)
            ddcv = ddc_ref[:, cols]
            db_acc[:, cols] += _rowsum8(ddcv)
            tap = _Taps(xx, tm, True)
            for k in range(CONF_K):
                dw_acc[k * 8:(k + 1) * 8, cols] += _rowsum8(ddcv * tap(CONF_K - 1 - k))
            ddc_a = jnp.where(last, 0.0, ddca_ref[:, cols])
            xx = jnp.concatenate([ddcv, ddc_a], axis=0)
            tap = _Taps(xx, tm, False)
            dgl = jnp.zeros((tm, lanes), F32)
            for k in range(CONF_K):
                dgl = dgl + dw_ref[k:k + 1, cols] * tap(CONF_K - 1 - k)
            dp_ref[:, 3 * cw + l * lanes:3 * cw + (l + 1) * lanes] = (dgl * sb).astype(BF16)
            dp_ref[:, 4 * cw + l * lanes:4 * cw + (l + 1) * lanes] = (dgl * gav * sb * (1.0 - sb)).astype(BF16)
        dp_ref[:, 6 * cw:7 * cw] = dg2_ref[...]

        @pl.when(last)
        def _():
            for k in range(SCONV_K):
                dsw_ref[k:k + 1, :] = jnp.sum(sw_acc[k * 8:(k + 1) * 8, :], axis=0, keepdims=True)
            for k in range(CONF_K):
                ddw_ref[k:k + 1, :] = jnp.sum(dw_acc[k * 8:(k + 1) * 8, :], axis=0, keepdims=True)
            ddb_ref[...] = jnp.sum(db_acc[...], axis=0, keepdims=True)

    def main(c):
        return pl.BlockSpec((tm, cw), lambda i: (i, c))

    def before(c):
        return pl.BlockSpec((HALO, cw), lambda i: (hb(i), c))

    def after(c):
        return pl.BlockSpec((HALO, cw), lambda i: (ha(i), c))

    def vec(r):
        return pl.BlockSpec((r, cw), lambda i: (0, 0))

    return pl.pallas_call(
        body, name=name, grid=(n,),
        in_specs=[main(0), after(0), main(5), after(5), main(1), after(1), main(0), before(0), main(2), before(2),
                  main(0), after(0), main(3), before(3), main(4), before(4), main(0), vec(SCONV_K), vec(CONF_K)],
        out_specs=[pl.BlockSpec((tm, 7 * cw), lambda i: (i, 0)), vec(SCONV_K), vec(CONF_K), vec(1)],
        out_shape=[jax.ShapeDtypeStruct((s, 7 * cw), BF16), jax.ShapeDtypeStruct((SCONV_K, cw), F32),
                   jax.ShapeDtypeStruct((CONF_K, cw), F32), jax.ShapeDtypeStruct((1, cw), F32)],
        scratch_shapes=[pltpu.VMEM((8 * SCONV_K, cw), F32), pltpu.VMEM((8 * CONF_K, cw), F32),
                        pltpu.VMEM((8, cw), F32)],
        compiler_params=_params("arbitrary"),
    )(dy, dy, p, p, p, p, p, p, p, p, ddc, ddc, p, p, p, p, dg2, sconv_w, dconv_w)


_ANY = pl.BlockSpec(memory_space=pl.ANY)


def _place():
    return lax.axis_index("x"), lax.axis_index("y"), lax.axis_index("c")


def all_gather(arrs, name, deps=()):
    n = len(arrs)

    def body(*refs):
        ins, outs = refs[:n], refs[n + len(deps):2 * n + len(deps)]
        send_sems, recv_sems, local_sems = refs[-3:]
        x, y, c = _place()
        me, sibling = (x, y, c), (x, y, 1 - c)
        chips = [(1 - x, y), (x, 1 - y), (1 - x, 1 - y)]

        def copy(a, k, block, to, src=None):
            px, py, pc = block
            dst = outs[a].at[4 * px + 2 * py + pc]
            return pltpu.make_async_remote_copy(
                src_ref=dst if src is None else src, dst_ref=dst,
                send_sem=send_sems.at[7 * a + k], recv_sem=recv_sems.at[7 * a + k],
                device_id=to, device_id_type=MESH)

        mine = [pltpu.make_async_copy(ins[a], outs[a].at[4 * x + 2 * y + c], local_sems.at[a]) for a in range(n)]
        first = []
        for a in range(n):
            first.append(copy(a, 0, me, sibling, src=ins[a]))
            first += [copy(a, 1 + j, me, (*chip, c), src=ins[a]) for j, chip in enumerate(chips)]
        for cp in first + mine:
            cp.start()
        passed = []
        for a in range(n):
            for j, chip in enumerate(chips):
                copy(a, 1 + j, (*chip, c), me).wait_recv()
                cp = copy(a, 4 + j, (*chip, c), sibling)
                cp.start()
                passed.append(cp)
        for a in range(n):
            copy(a, 0, sibling, me).wait_recv()
            for j, chip in enumerate(chips):
                copy(a, 4 + j, (*chip, 1 - c), me).wait_recv()
        for cp in first + passed:
            cp.wait_send()
        for cp in mine:
            cp.wait()

    return pl.pallas_call(
        body, name=name,
        out_shape=[jax.ShapeDtypeStruct((N_DEV,) + a.shape, a.dtype) for a in arrs],
        in_specs=[_ANY] * (n + len(deps)), out_specs=[_ANY] * n,
        scratch_shapes=[pltpu.SemaphoreType.DMA((7 * n,)), pltpu.SemaphoreType.DMA((7 * n,)),
                        pltpu.SemaphoreType.DMA((n,))],
    )(*arrs, *deps)


def in_proj_gathered(xs, g, w_own, extras, name, tm=512):
    s, d = xs.shape
    n = w_own.shape[1]
    arrs = [w_own] + list(extras)
    na = len(arrs)
    tr = 256

    def body(*refs):
        x_ref, g_ref, ins = refs[0], refs[1], refs[2:2 + na]
        h_out, p_ref, outs = refs[2 + na], refs[3 + na], refs[4 + na:4 + 2 * na]
        (h_ref, xbuf, wbuf, obuf, send_sems, recv_sems, load_sem, store_sems, own_sems, h_sem,
         x_sems) = refs[4 + 2 * na:]
        x, y, c = _place()
        me, sibling = (x, y, c), (x, y, 1 - c)
        x_first = c == 0
        near = (jnp.where(x_first, 1 - x, x), jnp.where(x_first, y, 1 - y))
        far = (jnp.where(x_first, x, 1 - x), jnp.where(x_first, 1 - y, y))
        diag = (1 - x, 1 - y)
        k_near, k_far = jnp.where(x_first, 1, 2), jnp.where(x_first, 2, 1)
        f_near, f_far = k_near + 3, k_far + 3

        def slot(block):
            return 4 * block[0] + 2 * block[1] + block[2]

        def copy(a, k, block, to, src=None):
            dst = outs[a].at[slot(block)]
            return pltpu.make_async_remote_copy(
                src_ref=dst if src is None else src, dst_ref=dst,
                send_sem=send_sems.at[7 * a + k], recv_sem=recv_sems.at[7 * a + k],
                device_id=to, device_id_type=MESH)

        first = []
        for a in range(na):
            first += [copy(a, 0, me, sibling, src=ins[a]), copy(a, 1, me, (1 - x, y, c), src=ins[a]),
                      copy(a, 2, me, (x, 1 - y, c), src=ins[a])]
        for cp in first:
            cp.start()
        own = pltpu.make_async_copy(wbuf.at[0], outs[0].at[slot(me)], own_sems.at[0])
        mine = [pltpu.make_async_copy(ins[a], outs[a].at[slot(me)], own_sems.at[a]) for a in range(1, na)]
        stores = [None, None]

        def x_load(i):
            return pltpu.make_async_copy(x_ref.at[pl.ds(i * tr, tr), :], xbuf.at[i % 2], x_sems.at[i % 2])

        x_load(0).start()
        for i in range(s // tr):
            if i + 1 < s // tr:
                x_load(i + 1).start()
            x_load(i).wait()
            xv = xbuf[i % 2]
            r = lax.rsqrt(jnp.mean(xv * xv, axis=-1, keepdims=True) + EPS)
            h_ref[i * tr:(i + 1) * tr, :] = (xv * r * g_ref[...]).astype(BF16)
        h_store = pltpu.make_async_copy(h_ref, h_out, h_sem)
        h_store.start()

        def multiply(k, block, w_from):
            b = k % 2
            if k == 2:
                own.wait()
            load = pltpu.make_async_copy(w_from, wbuf.at[b], load_sem)
            load.start()
            if stores[b] is not None:
                stores[b].wait()
            load.wait()
            if k == 0:
                own.start()

            def chunk(i, carry):
                rows = pl.ds(pl.multiple_of(i * tm, tm), tm)
                obuf[b, rows, :] = jnp.dot(h_ref[rows, :], wbuf[b], preferred_element_type=F32).astype(BF16)
                return carry

            lax.fori_loop(0, s // tm, chunk, 0)
            stores[b] = pltpu.make_async_copy(
                obuf.at[b], p_ref.at[:, pl.ds(pl.multiple_of(slot(block) * n, 128), n)], store_sems.at[b])
            stores[b].start()

        passed = []

        def arrive(a, k, block):
            copy(a, k, block, me).wait_recv()

        def pass_on(a, k, block, to):
            cp = copy(a, k, block, to)
            cp.start()
            passed.append(cp)

        def gather(a, use):
            use(0, me)
            arrive(a, 0, sibling)
            use(1, sibling)
            arrive(a, k_near, (*near, c))
            pass_on(a, 3, (*near, c), (*far, c))
            pass_on(a, f_near, (*near, c), sibling)
            use(2, (*near, c))
            arrive(a, f_far, (*far, 1 - c))
            use(3, (*far, 1 - c))
            arrive(a, k_far, (*far, c))
            pass_on(a, f_far, (*far, c), sibling)
            use(4, (*far, c))
            arrive(a, f_near, (*near, 1 - c))
            use(5, (*near, 1 - c))
            arrive(a, 3, (*diag, c))
            pass_on(a, 6, (*diag, c), sibling)
            use(6, (*diag, c))
            arrive(a, 6, (*diag, 1 - c))
            use(7, (*diag, 1 - c))

        gather(0, lambda k, block: multiply(k, block, ins[0] if k == 0 else outs[0].at[slot(block)]))
        for cp in mine:
            cp.start()
        for a in range(1, na):
            gather(a, lambda k, block: None)
        for cp in first + passed:
            cp.wait_send()
        for cp in mine + stores + [h_store]:
            cp.wait()

    vmem = pl.BlockSpec(memory_space=pltpu.VMEM)
    outs = pl.pallas_call(
        body, name=name,
        out_shape=[jax.ShapeDtypeStruct((s, d), BF16), jax.ShapeDtypeStruct((s, N_DEV * n), BF16)]
        + [jax.ShapeDtypeStruct((N_DEV,) + a.shape, a.dtype) for a in arrs],
        in_specs=[_ANY, vmem] + [_ANY] * na, out_specs=[_ANY] * (2 + na),
        scratch_shapes=[pltpu.VMEM((s, d), BF16), pltpu.VMEM((2, tr, d), F32), pltpu.VMEM((2, d, n), BF16),
                        pltpu.VMEM((2, s, n), BF16),
                        pltpu.SemaphoreType.DMA((7 * na,)), pltpu.SemaphoreType.DMA((7 * na,)),
                        pltpu.SemaphoreType.DMA, pltpu.SemaphoreType.DMA((2,)), pltpu.SemaphoreType.DMA((na,)),
                        pltpu.SemaphoreType.DMA, pltpu.SemaphoreType.DMA((2,))],
        compiler_params=pltpu.CompilerParams(vmem_limit_bytes=VMEM_LIMIT),
    )(xs, g, *arrs)
    return outs[0], outs[1], outs[2], outs[3:]


_HBM = pl.BlockSpec(memory_space=pltpu.HBM)
_SEM = pl.BlockSpec(memory_space=pltpu.SEMAPHORE)
_DATAFLOW = pltpu.SideEffectType.DATAFLOW_SIDE_EFFECTING


def _peers_per_array(kind):
    return 1 if kind in ("sibling", "halves") else 3


def _split_copies(kind, srcs, lands, send_sems, recv_sems):
    x, y, c = _place()
    per = _peers_per_array(kind)
    out = []
    for a in range(len(lands)):
        if kind == "sibling":
            part = srcs[a] if srcs[a].shape[1] == 1 else srcs[a].at[:, pl.ds(1 - c, 1)]
            peers = [((x, y, 1 - c), part, lands[a], lands[a])]
        elif kind == "halves":
            mine, its = lands[a].at[:, pl.ds(c, 1)], lands[a].at[:, pl.ds(1 - c, 1)]
            peers = [((x, y, 1 - c), mine, mine, its)]
        else:
            peers = []
            for px, py in [(1 - x, y), (x, 1 - y), (1 - x, 1 - y)]:
                if kind == "gather":
                    views = (srcs[a], lands[a].at[4 * x + 2 * y + c], lands[a].at[4 * px + 2 * py + c])
                else:
                    views = (srcs[a].at[2 * px + py], lands[a].at[2 * x + y], lands[a].at[2 * px + py])
                peers.append(((px, py, c),) + views)
        for j, (peer, src, dst, arrives) in enumerate(peers):
            sems = dict(send_sem=send_sems.at[per * a + j], recv_sem=recv_sems.at[per * a + j],
                        device_id=peer, device_id_type=MESH)
            out.append((pltpu.make_async_remote_copy(src_ref=src, dst_ref=dst, **sems),
                        pltpu.make_async_remote_copy(src_ref=src, dst_ref=arrives, **sems)))
    return out


def split_start(kind, srcs, lands, deps, name):
    ns, nl = len(srcs), len(lands)
    n_sems = _peers_per_array(kind) * nl
    held = list(srcs) + list(lands)

    def body(*refs):
        send_sems, recv_sems = refs[len(held) + len(deps)], refs[len(held) + len(deps) + 1]
        for copy, _ in _split_copies(kind, refs[:ns], refs[ns:ns + nl], send_sems, recv_sems):
            copy.start()
        token = refs[-1]
        token[...] = jnp.zeros_like(token)

    outs = pl.pallas_call(
        body, name=name,
        out_shape=(pltpu.SemaphoreType.DMA((n_sems,)), pltpu.SemaphoreType.DMA((n_sems,)),
                   *[pltpu.HBM(a.shape, a.dtype) for a in held], jax.ShapeDtypeStruct((8, 128), F32)),
        in_specs=[_HBM] * len(held) + [_ANY] * len(deps),
        out_specs=(_SEM, _SEM, *([_HBM] * len(held)), pl.BlockSpec(memory_space=pltpu.VMEM)),
        input_output_aliases={i: 2 + i for i in range(len(held))},
        compiler_params=pltpu.CompilerParams(has_side_effects=_DATAFLOW),
    )(*[pltpu.with_memory_space_constraint(a, pltpu.HBM) for a in held], *deps)
    return outs[0], outs[1], list(outs[2:2 + ns]), list(outs[2 + ns:2 + ns + nl]), outs[-1]


def split_wait(kind, send_sems, recv_sems, srcs, lands, afters, name):
    ns, nl = len(srcs), len(lands)
    held = list(srcs) + list(lands)

    def body(*refs):
        for _, arrival in _split_copies(kind, refs[:ns], refs[ns:ns + nl], refs[ns + nl], refs[ns + nl + 1]):
            arrival.wait_send()
            arrival.wait_recv()

    outs = pl.pallas_call(
        body, name=name,
        out_shape=[pltpu.HBM(a.shape, a.dtype) for a in held],
        in_specs=[_HBM] * len(held) + [_SEM, _SEM] + [_ANY] * len(afters),
        out_specs=[_HBM] * len(held),
        input_output_aliases={i: i for i in range(len(held))},
        compiler_params=pltpu.CompilerParams(has_side_effects=_DATAFLOW),
    )(*held, send_sems, recv_sems, *afters)
    return list(outs[:ns]), list(outs[ns:])


def place_block(land, block, dev, name):
    r, c = block.shape
    tr = min(r, 512)

    def body(dev_ref, land_ref, b_ref, o_ref):
        del dev_ref, land_ref
        o_ref[...] = b_ref[...]

    return pl.pallas_call(
        body, name=name,
        grid_spec=pltpu.PrefetchScalarGridSpec(
            num_scalar_prefetch=1, grid=(r // tr,),
            in_specs=[_ANY, pl.BlockSpec((tr, c), lambda i, dev_ref: (i, 0))],
            out_specs=pl.BlockSpec((None, tr, c), lambda i, dev_ref: (dev_ref[0], i, 0))),
        out_shape=jax.ShapeDtypeStruct(land.shape, land.dtype),
        input_output_aliases={1: 0},
        compiler_params=_params("parallel"),
    )(dev, land, block)


def pair_add(own, recv, core, name):
    _, _, r, c = own.shape
    tr = min(r, 512)

    def body(core_ref, own_ref, recv_ref, o_ref):
        del core_ref
        o_ref[...] = (own_ref[...].astype(F32) + recv_ref[...].astype(F32)).astype(BF16)

    return pl.pallas_call(
        body, name=name,
        grid_spec=pltpu.PrefetchScalarGridSpec(
            num_scalar_prefetch=1, grid=(4, r // tr),
            in_specs=[pl.BlockSpec((None, None, tr, c), lambda k, i, core_ref: (k, core_ref[0], i, 0)),
                      pl.BlockSpec((None, None, tr, c), lambda k, i, core_ref: (k, 0, i, 0))],
            out_specs=pl.BlockSpec((None, tr, c), lambda k, i, core_ref: (k, i, 0))),
        out_shape=jax.ShapeDtypeStruct((4, r, c), BF16),
        compiler_params=_params("parallel", "parallel"),
    )(core, own, recv)


def _adamw_math(w, g, m, v):
    m2 = ADAM_B1 * m + (1.0 - ADAM_B1) * g
    v2 = ADAM_B2 * v + (1.0 - ADAM_B2) * (g * g)
    m_hat = m2 / (1.0 - ADAM_B1 ** ADAM_STEP)
    v_hat = v2 / (1.0 - ADAM_B2 ** ADAM_STEP)
    delta = -ADAM_LR * (m_hat / (jnp.sqrt(v_hat) + ADAM_EPS) + ADAM_WD * w)
    return delta, m2, v2


def adamw_big(w, m, v, own, got, chip, name):
    r, c = w.shape
    tr = min(r, 256)

    def body(chip_ref, w_ref, m_ref, v_ref, p0, p1, p2, p3, g_ref, d_ref, m2_ref, v2_ref):
        del chip_ref
        g = ((p0[...].astype(F32) + p1[...].astype(F32)) + p2[...].astype(F32)) + p3[...].astype(F32)
        delta, m2, v2 = _adamw_math(w_ref[...], g, m_ref[...], v_ref[...])
        g_ref[...] = g
        d_ref[...] = delta
        m2_ref[...] = m2
        v2_ref[...] = v2

    row = pl.BlockSpec((tr, c), lambda i, chip_ref: (i, 0))

    def slab(flip):
        return pl.BlockSpec((None, tr, c), lambda i, chip_ref: (chip_ref[0] ^ flip, i, 0))

    return pl.pallas_call(
        body, name=name,
        grid_spec=pltpu.PrefetchScalarGridSpec(
            num_scalar_prefetch=1, grid=(r // tr,),
            in_specs=[row, row, row, slab(0), slab(1), slab(2), slab(3)],
            out_specs=[row] * 4),
        out_shape=[jax.ShapeDtypeStruct((r, c), F32)] * 4,
        compiler_params=_params("parallel"),
    )(chip, w, m, v, own, got, got, got)


def sum_devices(g8, name):
    def body(g_ref, o_ref):
        tot = g_ref[0]
        for k in range(1, N_DEV):
            tot = tot + g_ref[k]
        o_ref[...] = tot

    return pl.pallas_call(body, name=name, out_shape=jax.ShapeDtypeStruct(g8.shape[1:], F32))(g8)


def adamw_small(ws, gs, ms, vs, name):
    n = len(ws)

    def body(*refs):
        w_r, g_r, m_r, v_r = refs[:n], refs[n:2 * n], refs[2 * n:3 * n], refs[3 * n:4 * n]
        d_o, m_o, v_o = refs[4 * n:5 * n], refs[5 * n:6 * n], refs[6 * n:7 * n]
        for k in range(n):
            delta, m2, v2 = _adamw_math(w_r[k][...], g_r[k][...], m_r[k][...], v_r[k][...])
            d_o[k][...] = delta
            m_o[k][...] = m2
            v_o[k][...] = v2

    shapes = [jax.ShapeDtypeStruct(w.shape, F32) for w in ws]
    outs = pl.pallas_call(body, name=name, out_shape=shapes * 3)(*ws, *gs, *ms, *vs)
    return outs[:n], outs[n:2 * n], outs[2 * n:]


def _rows128(a):
    return a.reshape(-1, 128)


def _pad_rows(a, rows):
    return jnp.pad(a, ((0, rows - a.shape[0]), (0, 0)))


def kernel(x, ln_pre_even, w_in_even, pool_w, pool_scale, w_out_even, ln_post_even, ln_pre_odd, w_in_odd, sconv_w, dconv_w, dconv_b, cnorm_g, cnorm_b, w_out_odd, ln_post_odd, loss_target, m_ln_pre_even, m_w_in_even, m_pool_w, m_pool_scale, m_w_out_even, m_ln_post_even, m_ln_pre_odd, m_w_in_odd, m_sconv_w, m_dconv_w, m_dconv_b, m_cnorm_g, m_cnorm_b, m_w_out_odd, m_ln_post_odd, v_ln_pre_even, v_w_in_even, v_pool_w, v_pool_scale, v_w_out_even, v_ln_post_even, v_ln_pre_odd, v_w_in_odd, v_sconv_w, v_dconv_w, v_dconv_b, v_cnorm_g, v_cnorm_b, v_w_out_odd, v_ln_post_odd):
    xs = x[0]
    tgt = loss_target[0]
    s, d = xs.shape
    half = d // 2
    n_heads = half // HEAD_DIM
    ng = len(POOL_WINDOWS)
    cwp = half // ng
    dev = 4 * lax.axis_index("x") + 2 * lax.axis_index("y") + lax.axis_index("c")
    core = lax.axis_index("c").astype(jnp.int32).reshape(1)

    pr = pool_w.shape[2]
    cl = sconv_w.shape[2]
    small_parts = [(_rows128(ln_pre_odd), 8), (sconv_w[0], 8), (dconv_w[0], 32), (dconv_b, 8),
                   (cnorm_g, 8), (cnorm_b, 8), (_rows128(ln_post_odd), 8)]
    small_local = jnp.concatenate([_pad_rows(a, r) for a, r in small_parts], axis=0)
    h0, p0, g_wie, (g_pw, g_small) = in_proj_gathered(
        xs, ln_pre_even, w_in_even[0].astype(BF16), [pool_w[0].reshape(ng * pr, cwp).astype(BF16), small_local],
        "ag_in_proj_even")
    comm = _Exchanges(dev, core, d)
    token = comm.start_weights("out_even", [w_out_even[0].astype(BF16)], [p0])
    sb_dep = comm.start_weights("odd", [w_in_odd[0].astype(BF16), w_out_odd[0].astype(BF16)], [token])
    pool_full = g_pw.reshape(N_DEV, ng, pr, cwp).transpose(1, 0, 2, 3).reshape(ng, cwp, cwp)
    nl = ln_pre_odd.shape[1] // 128

    def chan(lo, rows):
        return g_small[:, lo:lo + rows].transpose(1, 0, 2).reshape(rows, N_DEV * cl)

    ln_pre_odd_f = g_small[:, 0:nl].reshape(1, d)
    sconv_f = chan(8, SCONV_K)
    dconv_f = chan(16, CONF_K)
    dconv_b_f = chan(48, 1)
    cnorm_g_f = chan(56, 1)
    cnorm_b_f = chan(64, 1)
    ln_post_odd_f = g_small[:, 72:72 + nl].reshape(1, d)

    loss_blk, grad_x, small_g = _fwd_bwd(
        xs, tgt, ln_pre_even, h0, p0, g_wie, pool_full, pool_scale, ln_post_even, ln_pre_odd_f,
        sconv_f, dconv_f, dconv_b_f, cnorm_g_f, cnorm_b_f, ln_post_odd_f, comm, sb_dep)
    small_w = [ln_pre_even, pool_scale, ln_post_even, ln_pre_odd, sconv_w[0], dconv_w[0], dconv_b, cnorm_g, cnorm_b, ln_post_odd]
    small_m = [m_ln_pre_even, m_pool_scale, m_ln_post_even, m_ln_pre_odd, m_sconv_w[0], m_dconv_w[0], m_dconv_b, m_cnorm_g, m_cnorm_b, m_ln_post_odd]
    small_v = [v_ln_pre_even, v_pool_scale, v_ln_post_even, v_ln_pre_odd, v_sconv_w[0], v_dconv_w[0], v_dconv_b, v_cnorm_g, v_cnorm_b, v_ln_post_odd]
    big = {"w_in_even": (w_in_even, m_w_in_even, v_w_in_even), "pool_w": (pool_w, m_pool_w, v_pool_w),
           "w_out_even": (w_out_even, m_w_out_even, v_w_out_even), "w_in_odd": (w_in_odd, m_w_in_odd, v_w_in_odd),
           "w_out_odd": (w_out_odd, m_w_out_odd, v_w_out_odd)}
    upd = comm.finish_updates(big, [grad_x])
    upd.update(comm.finish_updates(big, [grad_x]))
    sg, sd, sm, sv, loss = _update_small(small_g, loss_blk, small_w, small_m, small_v, dev, d, cl,
                                         deps=[upd["w_in_odd"][1], upd["w_out_even"][1]])
    upd.update(comm.finish_updates(big, sd))
    (g_wie_o, d_wie, m_wie, v_wie), (g_pw_o, d_pw, m_pw, v_pw) = upd["w_in_even"], upd["pool_w"]
    (g_woe_o, d_woe, m_woe, v_woe), (g_wio_o, d_wio, m_wio, v_wio) = upd["w_out_even"], upd["w_in_odd"]
    g_woo_o, d_woo, m_woo, v_woo = upd["w_out_odd"]

    def order(small, wie, pw, woe, wio, woo):
        return [small[0], wie, pw, small[1], woe, small[2], small[3], wio, small[4], small[5], small[6],
                small[7], small[8], woo, small[9]]

    grads = order(sg, g_wie_o, g_pw_o, g_woe_o, g_wio_o, g_woo_o)
    deltas = order(sd, d_wie, d_pw, d_woe, d_wio, d_woo)
    new_m = order(sm, m_wie, m_pw, m_woe, m_wio, m_woo)
    new_v = order(sv, v_wie, v_pw, v_woe, v_wio, v_woo)
    return (loss, grad_x[None], *grads, *deltas, *new_m, *new_v)


def _fwd_bwd(xs, tgt, ln_pre_even, h0, p0, g_wie, pool_full, pool_scale, ln_post_even, ln_pre_odd_f,
             sconv_f, dconv_f, dconv_b_f, cnorm_g_f, cnorm_b_f, ln_post_odd_f, comm, sb_dep):
    d = xs.shape[1]
    n_heads = d // 2 // HEAD_DIM
    ng, cwp = pool_full.shape[0], pool_full.shape[1]
    a0, sb_wts = sb_fwd(p0, n_heads, "sb_fwd", dep=sb_dep)
    dep = comm.weights_arrived("out_even", after=a0)
    y0 = even_mix_fwd(a0, p0, pool_full, pool_scale, "even_mix_fwd", dep=dep)
    (w_out_e,) = comm.weights("out_even", after=y0)
    w_out_e = w_out_e.reshape(1, d, d)
    o0 = mm_nn(y0, w_out_e, F32, "out_proj_even", tm=1024)
    dep = comm.weights_arrived("odd", after=o0)
    x1, h1 = postnorm_fwd(xs, o0, ln_post_even, ln_pre_odd_f, "post_even", dep=dep)
    g_wio, w_out_o = comm.weights("odd", after=x1)
    w_out_o = w_out_o.reshape(1, d, d)
    p1 = mm_nn(h1, g_wio, BF16, "in_proj_odd", group=2)
    y1, dc = odd_mix_fwd(p1, sconv_f, dconv_f, dconv_b_f, cnorm_g_f, cnorm_b_f, "odd_mix_fwd")
    o1 = mm_nn(y1, w_out_o, F32, "out_proj_odd", tm=1024)
    loss_blk, gx2, do1, dg_post_odd = final_fwd_bwd(x1, o1, ln_post_odd_f, tgt, "post_odd_loss")

    dw_out_o = mm_tn(y1, do1, 1, BF16, "dw_out_odd")
    dy1 = mm_nt(do1, w_out_o, BF16, "dy_odd")
    ddc, dg2, dgam, dbet = odd_bwd_ln(dy1, p1, dc, cnorm_g_f, cnorm_b_f, "odd_bwd_ln")
    dp1, dsconv, ddconv, ddconv_b = odd_bwd_conv(dy1, p1, ddc, dg2, sconv_f, dconv_f, "odd_bwd_conv")
    dw_in_o = mm_tn(h1, dp1, N_DEV, BF16, "dw_in_odd", group=2)
    dep = comm.reduce_begin({"w_out_odd": dw_out_o.reshape(N_DEV, d // N_DEV, d), "w_in_odd": dw_in_o}, "odd")
    dh1 = mm_nt(dp1, g_wio, F32, "dh_odd", dep=dep, group=2)
    dep = comm.reduce_send(after=dh1)
    gx1, dg_pre_odd, do0, dg_post_even = norm_bwd(dh1, x1, ln_pre_odd_f, gx2, "pre_odd_post_even_bwd",
                                                  inp2=o0, g2=ln_post_even, dep=dep)

    dw_out_e = mm_tn(y0, do0, 1, BF16, "dw_out_even")
    dy0 = mm_nt(do0, w_out_e, BF16, "dy_even")
    da0, du0, dg0, dpool, dpool_scale = even_mix_bwd(dy0, a0, p0, pool_full, pool_scale, "even_mix_bwd")
    pr = cwp // N_DEV
    dpool_slabs = dpool.astype(BF16).reshape(ng, N_DEV, pr, cwp).transpose(1, 0, 2, 3).reshape(N_DEV, ng * pr, cwp)
    dep = comm.reduce_begin({"w_out_even": dw_out_e.reshape(N_DEV, d // N_DEV, d), "pool_w": dpool_slabs}, "even_out")
    dq0, dk0, dv0 = sb_bwd(p0, a0, sb_wts, da0, n_heads, "sb_bwd", dep=dep)
    dep = comm.reduce_send(after=dq0)
    dp0 = jnp.concatenate([dq0, dk0, dv0, du0, dg0], axis=1)
    dw_sibling = mm_tn(h0, dp0, N_DEV // 2, BF16, "dw_in_even_sibling", dep=dep, pick=(2, 1 - comm.core))
    dep = comm.reduce_begin({"w_in_even": dw_sibling}, "even_in", sibling_part=True)
    dw_own = mm_tn(h0, dp0, N_DEV // 2, BF16, "dw_in_even_own", dep=dep, pick=(2, comm.core))
    dep = comm.reduce_send(after=dw_own, own_part={"w_in_even": dw_own})
    dh0 = mm_nt(dp0, g_wie, F32, "dh_even", dep=dep)
    dep = None
    grad_x, dg_pre_even = norm_bwd(dh0, xs, ln_pre_even, gx1, "pre_even_bwd", dep=dep)
    small_g = [dg_pre_even, dpool_scale, dg_post_even, dg_pre_odd, dsconv, ddconv, ddconv_b, dgam, dbet, dg_post_odd]
    return loss_blk, grad_x, small_g


class _Exchanges:
    def __init__(self, dev, core, d):
        self.dev = dev.astype(jnp.int32).reshape(1)
        self.core = core
        self.chip = (dev // 2).astype(jnp.int32).reshape(1)
        self.d = d
        self.in_flight = {}
        self.to_sibling = None
        self.pending = []

    def start_weights(self, tag, blocks, afters):
        lands = [lax.empty((N_DEV,) + b.shape, b.dtype) for b in blocks]
        send, recv, srcs, lands, token = split_start("gather", blocks, lands, afters, "ag_start_" + tag)
        self.in_flight[tag] = (send, recv, srcs, lands)
        return token

    def weights_arrived(self, tag, after):
        send, recv, srcs, lands = self.in_flight.pop(tag)
        srcs, lands = split_wait("gather", send, recv, srcs, lands, [after], "ag_wait_" + tag)
        lands = [place_block(l, b, self.dev, "ag_own_%s_%d" % (tag, k)) for k, (l, b) in enumerate(zip(lands, srcs))]
        lands = [l.reshape((4, 2) + l.shape[1:]) for l in lands]
        send, recv, _, lands, token = split_start("halves", [], lands, [], "ag_sibling_start_" + tag)
        self.in_flight[tag] = (send, recv, lands)
        return token

    def weights(self, tag, after):
        send, recv, lands = self.in_flight.pop(tag)
        _, lands = split_wait("halves", send, recv, [], lands, [after], "ag_sibling_wait_" + tag)
        return [l.reshape((N_DEV,) + l.shape[2:]) for l in lands]

    def reduce_begin(self, partials, tag, sibling_part=False):
        names = list(partials)
        arrs = [partials[k].reshape((4, 1 if sibling_part else 2) + partials[k].shape[1:]) for k in names]
        lands = [lax.empty((4, 1) + a.shape[2:], a.dtype) for a in arrs]
        send, recv, srcs, lands, token = split_start("sibling", arrs, lands, [], "rs_sibling_start_" + tag)
        self.to_sibling = (tag, names, send, recv, srcs, lands)
        return token

    def reduce_send(self, after, own_part=None):
        tag, names, send, recv, srcs, lands = self.to_sibling
        srcs, lands = split_wait("sibling", send, recv, srcs, lands, [after], "rs_sibling_wait_" + tag)
        which = self.core
        if own_part is not None:
            srcs = [own_part[k].reshape((4, 1) + own_part[k].shape[1:]) for k in names]
            which = jnp.zeros((1,), jnp.int32)
        sums = [pair_add(o, r, which, "rs_pair_add_" + k) for k, o, r in zip(names, srcs, lands)]
        zones = [lax.empty(a.shape, a.dtype) for a in sums]
        send, recv, srcs, zones, token = split_start("scatter", sums, zones, [], "rs_start_" + tag)
        self.pending.append((tag, names, send, recv, srcs, zones))
        return token

    def finish_updates(self, big, afters):
        tag, names, send, recv, srcs, lands = self.pending.pop(0)
        srcs, lands = split_wait("scatter", send, recv, srcs, lands, afters, "rs_wait_" + tag)
        out = {}
        for name, own, got in zip(names, srcs, lands):
            w, m, v = big[name]
            shp = own.shape[1:]
            outs = adamw_big(w.reshape(shp), m.reshape(shp), v.reshape(shp), own, got, self.chip, "adamw_" + name)
            out[name] = [o.reshape(w.shape) for o in outs]
        return out


def _update_small(small_g, loss_blk, small_w, small_m, small_v, dev, d, cl, deps):
    packed = jnp.concatenate([_rows128(g) for g in small_g] + [loss_blk], axis=0)
    (g8,) = all_gather([packed], "ag_small_grads", deps)
    tot = sum_devices(g8, "sum_small_grads")
    loss = tot[packed.shape[0] - 8, 0]
    full_g = []
    lo = 0
    for g in small_g:
        rows = g.size // 128
        full_g.append(tot[lo:lo + rows].reshape(g.shape))
        lo += rows

    def mine(g, width):
        return lax.dynamic_slice_in_dim(g, dev * width, width, axis=g.ndim - 1)

    fg = full_g
    small_gl = [fg[0], fg[1], fg[2], mine(fg[3], d // N_DEV), mine(fg[4], cl), mine(fg[5], cl), mine(fg[6], cl),
                mine(fg[7], cl), mine(fg[8], cl), mine(fg[9], d // N_DEV)]
    sd, sm, sv = adamw_small(small_w, small_gl, small_m, small_v, "adamw_small")

    def like(k, a):
        return a[None] if k in (4, 5) else a

    sg = [like(k, a) for k, a in enumerate(small_gl)]
    sd = [like(k, a) for k, a in enumerate(sd)]
    sm = [like(k, a) for k, a in enumerate(sm)]
    sv = [like(k, a) for k, a in enumerate(sv)]
    return sg, sd, sm, sv, loss
```

```python
import functools
import math

import jax
import jax.numpy as jnp
from jax import lax
from jax.experimental import pallas as pl
from jax.experimental.pallas import tpu as pltpu

F32 = jnp.float32
BF16 = jnp.bfloat16
EPS = 1e-6
HEAD_DIM = 128
POOL_WINDOWS = (2, 4, 8, 16)
SCONV_K = 3
CONF_K = 31
HALO = 32
N_DEV = 8
VMEM_LIMIT = 56 * 1024 * 1024
MESH = pl.DeviceIdType.MESH

ADAM_LR = 0.001
ADAM_B1 = 0.9
ADAM_B2 = 0.999
ADAM_EPS = 1e-08
ADAM_WD = 0.01
ADAM_STEP = 10


def _params(*sem):
    return pltpu.CompilerParams(dimension_semantics=sem, vmem_limit_bytes=VMEM_LIMIT)


def _sigmoid(v):
    return 1.0 / (1.0 + jnp.exp(-v))


def _silu(v):
    return v * _sigmoid(v)


def _silu_and_grad(v):
    s = _sigmoid(v)
    return v * s, s * (1.0 + v * (1.0 - s))


def _rowsum8(v):
    r, c = v.shape
    return jnp.sum(v.reshape(r // 8, 8, c), axis=0)


SUBLANES = 8


class _Taps:
    def __init__(self, xx, rows, before):
        self.xx, self.rows, self.before, self.rotated = xx, rows, before, {}

    def __call__(self, i):
        r, q = i % SUBLANES, i // SUBLANES
        if r not in self.rotated:
            n = self.xx.shape[0]
            self.rotated[r] = self.xx if r == 0 else pltpu.roll(self.xx, r if self.before else n - r, 0)
        lo = HALO - SUBLANES * q if self.before else SUBLANES * q
        return self.rotated[r][lo:lo + self.rows]


def _window_sum(xx, win, before):
    n = xx.shape[0]
    acc = xx
    k = 1
    while k < win:
        acc = acc + pltpu.roll(acc, k if before else n - k, 0)
        k *= 2
    return acc


def postnorm_fwd(x, o, g, g_next, name, tm=256, dep=None):
    s, d = x.shape
    dep_args, dep_specs = _after(dep)

    def body(x_ref, o_ref, g_ref, gn_ref, *rest):
        y_ref, h_ref = rest[-2:]
        ov = o_ref[...].astype(F32)
        r = lax.rsqrt(jnp.mean(ov * ov, axis=-1, keepdims=True) + EPS)
        y = x_ref[...] + ov * r * g_ref[...]
        y_ref[...] = y
        r2 = lax.rsqrt(jnp.mean(y * y, axis=-1, keepdims=True) + EPS)
        h_ref[...] = (y * r2 * gn_ref[...]).astype(BF16)

    row = pl.BlockSpec((tm, d), lambda i: (i, 0))
    vec = pl.BlockSpec((1, d), lambda i: (0, 0))
    return pl.pallas_call(
        body, name=name, grid=(s // tm,),
        in_specs=[row, row, vec, vec] + dep_specs, out_specs=[row, row],
        out_shape=[jax.ShapeDtypeStruct((s, d), F32), jax.ShapeDtypeStruct((s, d), BF16)],
        compiler_params=_params("parallel"),
    )(x, o, g, g_next, *dep_args)


def final_fwd_bwd(x1, o, g, target, name, tm=256):
    s, d = x1.shape
    n = s // tm

    def body(x_ref, o_ref, g_ref, t_ref, loss_ref, gx_ref, do_ref, dg_ref, lacc, gacc):
        i = pl.program_id(0)

        @pl.when(i == 0)
        def _():
            lacc[...] = jnp.zeros_like(lacc)
            gacc[...] = jnp.zeros_like(gacc)

        ov = o_ref[...].astype(F32)
        gv = g_ref[...]
        r = lax.rsqrt(jnp.mean(ov * ov, axis=-1, keepdims=True) + EPS)
        oh = ov * r
        diff = x_ref[...] + oh * gv - t_ref[...]
        lacc[...] += _rowsum8(diff * diff)
        gx = diff * (1.0 / d)
        gx_ref[...] = gx
        gacc[...] += _rowsum8(gx * oh)
        dn = gx * gv
        do_ref[...] = (r * (dn - oh * jnp.mean(dn * oh, axis=-1, keepdims=True))).astype(BF16)

        @pl.when(i == n - 1)
        def _():
            tot = jnp.sum(jnp.sum(lacc[...], axis=0, keepdims=True), axis=1, keepdims=True)
            loss_ref[...] = jnp.broadcast_to(tot * (0.5 / d), loss_ref.shape)
            dg_ref[...] = jnp.sum(gacc[...], axis=0, keepdims=True)

    row = pl.BlockSpec((tm, d), lambda i: (i, 0))
    vec = pl.BlockSpec((1, d), lambda i: (0, 0))
    return pl.pallas_call(
        body, name=name, grid=(n,),
        in_specs=[row, row, vec, row],
        out_specs=[pl.BlockSpec((8, 128), lambda i: (0, 0)), row, row, vec],
        out_shape=[jax.ShapeDtypeStruct((8, 128), F32), jax.ShapeDtypeStruct((s, d), F32),
                   jax.ShapeDtypeStruct((s, d), BF16), jax.ShapeDtypeStruct((1, d), F32)],
        scratch_shapes=[pltpu.VMEM((8, d), F32), pltpu.VMEM((8, d), F32)],
        compiler_params=_params("arbitrary"),
    )(x1, o, g, target)


def _rms_bwd_rows(dyv, xv, gv):
    r = lax.rsqrt(jnp.mean(xv * xv, axis=-1, keepdims=True) + EPS)
    xh = xv * r
    dn = dyv * gv
    return r * (dn - xh * jnp.mean(dn * xh, axis=-1, keepdims=True)), _rowsum8(dyv * xh)


def norm_bwd(dy, inp, g, resid, name, inp2=None, g2=None, tm=256, dep=None):
    s, d = inp.shape
    n = s // tm
    chain = inp2 is not None

    def body(*refs):
        dy_ref, x_ref, g_ref, r_ref = refs[:4]
        outs = refs[-6:] if chain else refs[-3:]
        i = pl.program_id(0)

        @pl.when(i == 0)
        def _():
            for acc in outs[-2:] if chain else outs[-1:]:
                acc[...] = jnp.zeros_like(acc)

        if chain:
            x2_ref, g2_ref = refs[4:6]
            dx_ref, dg_ref, dx2_ref, dg2_ref, gacc, gacc2 = outs
        else:
            dx_ref, dg_ref, gacc = outs
        dx, dg_rows = _rms_bwd_rows(dy_ref[...].astype(F32), x_ref[...], g_ref[...])
        dx = dx + r_ref[...]
        dx_ref[...] = dx
        gacc[...] += dg_rows
        if chain:
            dx2, dg2_rows = _rms_bwd_rows(dx, x2_ref[...].astype(F32), g2_ref[...])
            dx2_ref[...] = dx2.astype(BF16)
            gacc2[...] += dg2_rows

        @pl.when(i == n - 1)
        def _():
            dg_ref[...] = jnp.sum(gacc[...], axis=0, keepdims=True)
            if chain:
                dg2_ref[...] = jnp.sum(gacc2[...], axis=0, keepdims=True)

    row = pl.BlockSpec((tm, d), lambda i: (i, 0))
    vec = pl.BlockSpec((1, d), lambda i: (0, 0))
    dep_args, dep_specs = _after(dep)
    extra = [inp2, g2] if chain else []
    return pl.pallas_call(
        body, name=name, grid=(n,),
        in_specs=[row, row, vec, row] + ([row, vec] if chain else []) + dep_specs,
        out_specs=[row, vec] * (2 if chain else 1),
        out_shape=[jax.ShapeDtypeStruct((s, d), F32), jax.ShapeDtypeStruct((1, d), F32)]
        + ([jax.ShapeDtypeStruct((s, d), BF16), jax.ShapeDtypeStruct((1, d), F32)] if chain else []),
        scratch_shapes=[pltpu.VMEM((8, d), F32)] * (2 if chain else 1),
        compiler_params=_params("arbitrary"),
    )(dy, inp, g, resid, *extra, *dep_args)


def _after(dep):
    if dep is None:
        return [], []
    return [dep], [pl.BlockSpec((8, 128), lambda *_: (0, 0))]


def _lane_concat(ref, count):
    return ref[0] if count == 1 else jnp.concatenate([ref[i] for i in range(count)], axis=1)


def mm_nn(a, w, out_dtype, name, tm=2048, tn=None, dep=None, group=1):
    m, k = a.shape
    tm = min(tm, m)
    ns, _, n = w.shape
    tn = n if tn is None else tn
    nj = n // tn
    assert group == 1 or nj == 1
    dep_args, dep_specs = _after(dep)

    def body(a_ref, w_ref, *rest):
        o_ref = rest[-1]
        o_ref[...] = jnp.dot(a_ref[...], _lane_concat(w_ref, group), preferred_element_type=F32).astype(out_dtype)

    return pl.pallas_call(
        body, name=name, grid=(ns // group, nj, m // tm),
        in_specs=[pl.BlockSpec((tm, k), lambda s, j, i: (i, 0)),
                  pl.BlockSpec((group, k, tn), lambda s, j, i: (s, 0, j))] + dep_specs,
        out_specs=pl.BlockSpec((tm, group * tn), lambda s, j, i: (i, s * nj + j)),
        out_shape=jax.ShapeDtypeStruct((m, ns * n), out_dtype),
        compiler_params=_params("parallel", "parallel", "parallel"),
    )(a, w, *dep_args)


def mm_nt(a, w, out_dtype, name, tm=1024, tn=None, dep=None, group=1):
    m = a.shape[0]
    tm = min(tm, m)
    ns, k, n = w.shape
    tn = n if tn is None else tn
    nj = n // tn
    assert group == 1 or nj == 1
    steps = ns * nj // group
    dep_args, dep_specs = _after(dep)

    def body(a_ref, w_ref, *rest):
        o_ref, acc = rest[-2:]
        r = pl.program_id(1)

        @pl.when(r == 0)
        def _():
            acc[...] = jnp.zeros_like(acc)

        acc[...] += lax.dot_general(a_ref[...], _lane_concat(w_ref, group), (((1,), (1,)), ((), ())),
                                    preferred_element_type=F32)

        @pl.when(r == steps - 1)
        def _():
            o_ref[...] = acc[...].astype(out_dtype)

    return pl.pallas_call(
        body, name=name, grid=(m // tm, steps),
        in_specs=[pl.BlockSpec((tm, group * tn), lambda i, r: (i, r)),
                  pl.BlockSpec((group, k, tn), lambda i, r: (r // nj, 0, r % nj))] + dep_specs,
        out_specs=pl.BlockSpec((tm, k), lambda i, r: (i, 0)),
        out_shape=jax.ShapeDtypeStruct((m, k), out_dtype),
        scratch_shapes=[pltpu.VMEM((tm, k), F32)],
        compiler_params=_params("parallel", "arbitrary"),
    )(a, w, *dep_args)


def mm_tn(a, b, ns, out_dtype, name, tk=1024, tm=2048, dep=None, pick=None, group=1):
    m, k = a.shape
    tm = min(tm, m)
    step, offset = (1, None) if pick is None else pick
    assert group == 1 or pick is None
    n = b.shape[1] // (ns * step)
    steps = m // tm
    dep_args, dep_specs = _after(dep)
    n_pre = 0 if pick is None else 1

    def b_block(s, j, r, *pre):
        return (r, s if pick is None else step * s + pre[0][0])

    def body(*refs):
        a_ref, b_ref = refs[n_pre:n_pre + 2]
        o_ref, acc = refs[-2:]
        r = pl.program_id(2)

        @pl.when(r == 0)
        def _():
            acc[...] = jnp.zeros_like(acc)

        acc[...] += lax.dot_general(a_ref[...], b_ref[...], (((0,), (0,)), ((), ())),
                                    preferred_element_type=F32)

        @pl.when(r == steps - 1)
        def _():
            for i in range(group):
                o_ref[i] = acc[:, i * n:(i + 1) * n].astype(out_dtype)

    return pl.pallas_call(
        body, name=name,
        grid_spec=pltpu.PrefetchScalarGridSpec(
            num_scalar_prefetch=n_pre, grid=(ns // group, k // tk, steps),
            in_specs=[pl.BlockSpec((tm, tk), lambda s, j, r, *pre: (r, j)),
                      pl.BlockSpec((tm, group * n), b_block)] + dep_specs,
            out_specs=pl.BlockSpec((group, tk, n), lambda s, j, r, *pre: (s, j, 0)),
            scratch_shapes=[pltpu.VMEM((tk, group * n), F32)]),
        out_shape=jax.ShapeDtypeStruct((ns, k, n), out_dtype),
        compiler_params=_params("parallel", "parallel", "arbitrary"),
    )(*([] if pick is None else [offset]), a, b, *dep_args)


SB_BLK = 128


LOG2E = 1.0 / math.log(2.0)


def _split_dot(v, tri2):
    hi = pltpu.bitcast(pltpu.bitcast(v, jnp.uint32) & jnp.uint32(0xFFFF0000), F32)
    lo = (v - hi).astype(BF16)
    return jnp.dot(jnp.concatenate([hi.astype(BF16), lo], axis=1), tri2, preferred_element_type=F32)


def _sb_scores(z2, lim, dcol, tri_ex, masked):
    sp = jnp.log2(1.0 + jnp.exp2(-jnp.abs(z2)))
    lb = jnp.minimum(z2, 0.0) - sp
    l1m = lb - z2
    mask = None
    if masked:
        mask = dcol < lim
        l1m = jnp.where(mask, l1m, 0.0)
    return mask, lb, l1m, _split_dot(l1m, tri_ex)


def _sb_consts():
    row = lax.broadcasted_iota(jnp.int32, (SB_BLK, SB_BLK), 0)
    col = lax.broadcasted_iota(jnp.int32, (SB_BLK, SB_BLK), 1)
    tri_ex = jnp.where(row > col, 1.0, 0.0).astype(BF16)
    tri_in = jnp.where(row >= col, 1.0, 0.0).astype(BF16)
    return col - row, jnp.concatenate([tri_ex, tri_ex], axis=0), jnp.concatenate([tri_in, tri_in], axis=0)


def sb_fwd(p, n_heads, name, tq=512, nsub=8, dep=None):
    s = p.shape[0]
    h_n = n_heads
    b = SB_BLK
    nqs = tq // b
    tk = nsub * b
    scale = 1.0 / math.sqrt(HEAD_DIM)

    dep_args, dep_specs = _after(dep)

    def body(q_ref, k_ref, v_ref, *rest):
        o_ref, w_ref = rest[-2:]
        qi = pl.program_id(1)
        dcol, tri_ex, _ = _sb_consts()
        qv = [q_ref[qs * b:(qs + 1) * b, :] for qs in range(nqs)]
        n_groups = ((qi + 1) * nqs - 1) // nsub + 1

        def step(it, carry, masked):
            c1s, accs = carry
            g = n_groups - 1 - it
            off = pl.multiple_of(g * tk, tk)
            kg = k_ref[pl.ds(off, tk), :]
            vg = v_ref[pl.ds(off, tk), :]
            new_c1, new_acc = [], []
            for qs in range(nqs):
                qb = qi * nqs + qs
                z2 = lax.dot_general(qv[qs], kg, (((1,), (1,)), ((), ())),
                                     preferred_element_type=F32) * (scale * LOG2E)
                blocks = [_sb_scores(z2[:, j * b:(j + 1) * b], (qb - (g * nsub + j)) * b, dcol, tri_ex, masked)
                          for j in range(nsub)]
                run = c1s[qs]
                ws = [None] * nsub
                for j in reversed(range(nsub)):
                    mask, lb, l1m, ls_loc = blocks[j]
                    wj = jnp.exp2(lb + ls_loc + run)
                    ws[j] = (jnp.where(mask, wj, 0.0) if masked else wj).astype(BF16)
                    run = run + jnp.sum(l1m, axis=1, keepdims=True)
                w = jnp.concatenate(ws, axis=1)
                w_ref[0, g, qs * b:(qs + 1) * b, :] = w
                new_acc.append(accs[qs] + jnp.dot(w, vg, preferred_element_type=F32))
                new_c1.append(run)
            return tuple(new_c1), tuple(new_acc)

        init = (tuple(jnp.zeros((b, 1), F32) for _ in range(nqs)),
                tuple(jnp.zeros((b, HEAD_DIM), F32) for _ in range(nqs)))
        assert all(((i + 1) * nqs - 1) // nsub * nsub <= i * nqs for i in range(s // tq))
        first = step(0, init, True)
        _, accs = lax.fori_loop(1, n_groups, functools.partial(step, masked=False), first)
        for qs in range(nqs):
            o_ref[qs * b:(qs + 1) * b, :] = accs[qs]

    return pl.pallas_call(
        body, name=name, grid=(h_n, s // tq),
        in_specs=[pl.BlockSpec((tq, HEAD_DIM), lambda h, i: (i, h)),
                  pl.BlockSpec((s, HEAD_DIM), lambda h, i: (0, h_n + h)),
                  pl.BlockSpec((s, HEAD_DIM), lambda h, i: (0, 2 * h_n + h))] + dep_specs,
        out_specs=[pl.BlockSpec((tq, HEAD_DIM), lambda h, i: (i, h)),
                   pl.BlockSpec((1, s // tk, tq, tk), lambda h, i: (h, 0, i, 0))],
        out_shape=[jax.ShapeDtypeStruct((s, h_n * HEAD_DIM), F32),
                   jax.ShapeDtypeStruct((h_n, s // tk, s, tk), BF16)],
        compiler_params=_params("parallel", "arbitrary"),
    )(p, p, p, *dep_args)


def sb_bwd(p, a, wts, da, n_heads, name, tq=512, dep=None):
    s = p.shape[0]
    h_n = n_heads
    nq = s // tq
    b = SB_BLK
    nqs = tq // b
    tk = wts.shape[3]
    nsub = tk // b
    scale = 1.0 / math.sqrt(HEAD_DIM)
    dep_args, dep_specs = _after(dep)

    def body(q_ref, k_ref, v_ref, a_ref, da_ref, w_ref, *rest):
        dq_ref, dk_ref, dv_ref, dk_acc, dv_acc = rest[-5:]
        qi = pl.program_id(1)

        @pl.when(qi == 0)
        def _():
            dk_acc[...] = jnp.zeros_like(dk_acc)
            dv_acc[...] = jnp.zeros_like(dv_acc)

        dcol, _, tri_in = _sb_consts()
        q_all = q_ref[...]
        do_all = da_ref[...]
        qv = [q_ref[qs * b:(qs + 1) * b, :] for qs in range(nqs)]
        dov = [da_ref[qs * b:(qs + 1) * b, :] for qs in range(nqs)]
        tots = [jnp.sum(dov[qs].astype(F32) * a_ref[qs * b:(qs + 1) * b, :], axis=1, keepdims=True)
                for qs in range(nqs)]
        n_groups = ((qi + 1) * nqs - 1) // nsub + 1

        def step(it, carry, masked):
            c2s, dqs = carry
            g = n_groups - 1 - it
            off = pl.multiple_of(g * tk, tk)
            kg = k_ref[pl.ds(off, tk), :]
            vg = v_ref[pl.ds(off, tk), :]
            w_all = w_ref[0, g]
            new_c2, new_dq, dz_rows = [], [], []
            for qs in range(nqs):
                qb = qi * nqs + qs
                z2 = lax.dot_general(qv[qs], kg, (((1,), (1,)), ((), ())),
                                     preferred_element_type=F32) * (-scale * LOG2E)
                dw = lax.dot_general(dov[qs], vg, (((1,), (1,)), ((), ())), preferred_element_type=F32)
                beta = 1.0 / (1.0 + jnp.exp2(z2))
                e = dw * w_all[qs * b:(qs + 1) * b, :].astype(F32)
                run2 = c2s[qs]
                dzs = [None] * nsub
                for j in reversed(range(nsub)):
                    cols = slice(j * b, (j + 1) * b)
                    later = _split_dot(e[:, cols], tri_in) + run2
                    bj = beta[:, cols]
                    dz = (e[:, cols] * (1.0 - bj) - bj * (tots[qs] - later)) * scale
                    if masked:
                        dz = jnp.where(dcol < (qb - (g * nsub + j)) * b, dz, 0.0)
                    dzs[j] = dz.astype(BF16)
                    run2 = run2 + jnp.sum(e[:, cols], axis=1, keepdims=True)
                dzq = jnp.concatenate(dzs, axis=1)
                new_dq.append(dqs[qs] + jnp.dot(dzq, kg, preferred_element_type=F32))
                new_c2.append(run2)
                dz_rows.append(dzq)
            dz_all = jnp.concatenate(dz_rows, axis=0)
            dk_acc[pl.ds(off, tk), :] += lax.dot_general(dz_all, q_all, (((0,), (0,)), ((), ())),
                                                         preferred_element_type=F32)
            dv_acc[pl.ds(off, tk), :] += lax.dot_general(w_all, do_all, (((0,), (0,)), ((), ())),
                                                         preferred_element_type=F32)
            return tuple(new_c2), tuple(new_dq)

        zeros = tuple(jnp.zeros((b, 1), F32) for _ in range(nqs))
        assert all(((i + 1) * nqs - 1) // nsub * nsub <= i * nqs for i in range(s // tq))
        first = step(0, (zeros, tuple(jnp.zeros((b, HEAD_DIM), F32) for _ in range(nqs))), True)
        _, dqs = lax.fori_loop(1, n_groups, functools.partial(step, masked=False), first)
        for qs in range(nqs):
            dq_ref[qs * b:(qs + 1) * b, :] = dqs[qs].astype(BF16)

        @pl.when(qi == nq - 1)
        def _():
            dk_ref[...] = dk_acc[...].astype(BF16)
            dv_ref[...] = dv_acc[...].astype(BF16)

    blk = pl.BlockSpec((tq, HEAD_DIM), lambda h, i: (i, h))
    full = pl.BlockSpec((s, HEAD_DIM), lambda h, i: (0, h))
    return pl.pallas_call(
        body, name=name, grid=(h_n, nq),
        in_specs=[blk, pl.BlockSpec((s, HEAD_DIM), lambda h, i: (0, h_n + h)),
                  pl.BlockSpec((s, HEAD_DIM), lambda h, i: (0, 2 * h_n + h)), blk, blk,
                  pl.BlockSpec((1, s // tk, tq, tk), lambda h, i: (h, 0, i, 0))] + dep_specs,
        out_specs=[blk, full, full],
        out_shape=[jax.ShapeDtypeStruct((s, h_n * HEAD_DIM), BF16)] * 3,
        scratch_shapes=[pltpu.VMEM((s, HEAD_DIM), F32), pltpu.VMEM((s, HEAD_DIM), F32)],
        compiler_params=_params("parallel", "arbitrary"),
    )(p, p, p, a, da, wts, *dep_args)


def _pool_window(xx, win, r0, rc):
    cur = xx[HALO:HALO + rc]
    ws = _window_sum(xx, win, True)[HALO:HALO + rc]
    t_idx = r0 + lax.broadcasted_iota(jnp.int32, (rc, 1), 0)
    inv = 1.0 / jnp.minimum(win, t_idx + 1).astype(F32)
    return ws * inv - cur, inv


def even_mix_fwd(a, p, pool_w, pool_scale, name, rc=64, dep=None):
    s = p.shape[0]
    ng = len(POOL_WINDOWS)
    cw = pool_w.shape[1]
    n_chunks = s // rc
    dep_args, dep_specs = _after(dep)

    def body(a_ref, u_ref, g_ref, w_ref, sc_ref, *rest):
        y_ref, upad = rest[-2:]
        j = pl.program_id(0)

        @pl.when(j < ng)
        def _():
            def chunk(ci, carry):
                rows = pl.ds(pl.multiple_of(ci * rc, rc), rc)
                y_ref[rows, :] = (a_ref[rows, :] * _silu(g_ref[rows, :].astype(F32))).astype(BF16)
                return carry

            lax.fori_loop(0, n_chunks, chunk, 0)

        for gi, win in enumerate(POOL_WINDOWS):
            @pl.when(j == ng + gi)
            def _(win=win):
                upad[0:HALO, :] = jnp.zeros((HALO, cw), F32)

                def fill(ci, carry):
                    r0 = pl.multiple_of(ci * rc, rc)
                    upad[pl.ds(pl.multiple_of(r0 + HALO, HALO), rc), :] = u_ref[pl.ds(r0, rc), :].astype(F32)
                    return carry

                lax.fori_loop(0, n_chunks, fill, 0)

                def chunk(ci, carry):
                    r0 = pl.multiple_of(ci * rc, rc)
                    rows = pl.ds(r0, rc)
                    pooled, _ = _pool_window(upad[pl.ds(r0, HALO + rc), :], win, r0, rc)
                    t = jnp.dot(pooled.astype(BF16), w_ref[0], preferred_element_type=F32)
                    y_ref[rows, :] = (t * sc_ref[...] * _silu(g_ref[rows, :].astype(F32))).astype(BF16)
                    return carry

                lax.fori_loop(0, n_chunks, chunk, 0)

    grp = lambda j: jnp.maximum(j - ng, 0)
    return pl.pallas_call(
        body, name=name, grid=(2 * ng,),
        in_specs=[pl.BlockSpec((s, cw), lambda j: (0, jnp.minimum(j, ng - 1))),
                  pl.BlockSpec((s, cw), lambda j: (0, 3 * ng + grp(j))),
                  pl.BlockSpec((s, cw), lambda j: (0, 4 * ng + j)),
                  pl.BlockSpec((1, cw, cw), lambda j: (grp(j), 0, 0)),
                  pl.BlockSpec((1, cw), lambda j: (0, grp(j)))] + dep_specs,
        out_specs=pl.BlockSpec((s, cw), lambda j: (0, j)),
        out_shape=jax.ShapeDtypeStruct((s, 2 * ng * cw), BF16),
        scratch_shapes=[pltpu.VMEM((HALO + s, cw), F32)],
        compiler_params=_params("arbitrary"),
    )(a, p, p, pool_w, pool_scale, *dep_args)


def even_mix_bwd(dy, a, p, pool_w, pool_scale, name, rc=64):
    s = p.shape[0]
    ng = len(POOL_WINDOWS)
    cw = pool_w.shape[1]
    n_chunks = s // rc

    def body(dy_ref, a_ref, u_ref, g_ref, w_ref, sc_ref, da_ref, du_ref, dg_ref, dw_ref, dsc_ref,
             upad, rpad, dpl, dw_acc, dsc_acc):
        j = pl.program_id(0)

        @pl.when(j < ng)
        def _():
            def chunk(ci, carry):
                rows = pl.ds(pl.multiple_of(ci * rc, rc), rc)
                dyv = dy_ref[rows, :].astype(F32)
                sg, dsg = _silu_and_grad(g_ref[rows, :].astype(F32))
                da_ref[rows, :] = (dyv * sg).astype(BF16)
                dg_ref[rows, :] = (dyv * a_ref[rows, :] * dsg).astype(BF16)
                return carry

            lax.fori_loop(0, n_chunks, chunk, 0)

        for gi, win in enumerate(POOL_WINDOWS):
            @pl.when(j == ng + gi)
            def _(win=win):
                upad[0:HALO, :] = jnp.zeros((HALO, cw), F32)
                rpad[s:s + HALO, :] = jnp.zeros((HALO, cw), F32)
                dw_acc[...] = jnp.zeros_like(dw_acc)
                dsc_acc[...] = jnp.zeros_like(dsc_acc)

                def fill(ci, carry):
                    r0 = pl.multiple_of(ci * rc, rc)
                    upad[pl.ds(pl.multiple_of(r0 + HALO, HALO), rc), :] = u_ref[pl.ds(r0, rc), :].astype(F32)
                    return carry

                lax.fori_loop(0, n_chunks, fill, 0)

                def chunk(ci, carry):
                    r0 = pl.multiple_of(ci * rc, rc)
                    rows = pl.ds(r0, rc)
                    pooled, inv = _pool_window(upad[pl.ds(r0, HALO + rc), :], win, r0, rc)
                    pb = pooled.astype(BF16)
                    wv = w_ref[0]
                    t = jnp.dot(pb, wv, preferred_element_type=F32)
                    scv = sc_ref[...]
                    dyv = dy_ref[rows, :].astype(F32)
                    sg, dsg = _silu_and_grad(g_ref[rows, :].astype(F32))
                    dpo = dyv * sg
                    dg_ref[rows, :] = (dyv * t * scv * dsg).astype(BF16)
                    dsc_acc[...] += _rowsum8(dpo * t)
                    dtb = (dpo * scv).astype(BF16)
                    dw_acc[...] += lax.dot_general(pb, dtb, (((0,), (0,)), ((), ())),
                                                   preferred_element_type=F32)
                    dpooled = lax.dot_general(dtb, wv, (((1,), (1,)), ((), ())),
                                              preferred_element_type=F32)
                    dpl[rows, :] = dpooled
                    rpad[rows, :] = dpooled * inv
                    return carry

                lax.fori_loop(0, n_chunks, chunk, 0)

                def chunk2(ci, carry):
                    r0 = pl.multiple_of(ci * rc, rc)
                    rows = pl.ds(r0, rc)
                    xx = rpad[pl.ds(r0, rc + HALO), :]
                    fs = _window_sum(xx, win, False)[0:rc]
                    du_ref[rows, :] = (fs - dpl[rows, :]).astype(BF16)
                    return carry

                lax.fori_loop(0, n_chunks, chunk2, 0)
                dw_ref[0] = dw_acc[...]
                dsc_ref[...] = jnp.sum(dsc_acc[...], axis=0, keepdims=True)

    grp = lambda j: jnp.maximum(j - ng, 0)
    att = lambda j: jnp.minimum(j, ng - 1)
    return pl.pallas_call(
        body, name=name, grid=(2 * ng,),
        in_specs=[pl.BlockSpec((s, cw), lambda j: (0, j)),
                  pl.BlockSpec((s, cw), lambda j: (0, att(j))),
                  pl.BlockSpec((s, cw), lambda j: (0, 3 * ng + grp(j))),
                  pl.BlockSpec((s, cw), lambda j: (0, 4 * ng + j)),
                  pl.BlockSpec((1, cw, cw), lambda j: (grp(j), 0, 0)),
                  pl.BlockSpec((1, cw), lambda j: (0, grp(j)))],
        out_specs=[pl.BlockSpec((s, cw), lambda j: (0, att(j))),
                   pl.BlockSpec((s, cw), lambda j: (0, grp(j))),
                   pl.BlockSpec((s, cw), lambda j: (0, j)),
                   pl.BlockSpec((1, cw, cw), lambda j: (grp(j), 0, 0)),
                   pl.BlockSpec((1, cw), lambda j: (0, grp(j)))],
        out_shape=[jax.ShapeDtypeStruct((s, ng * cw), BF16), jax.ShapeDtypeStruct((s, ng * cw), BF16),
                   jax.ShapeDtypeStruct((s, 2 * ng * cw), BF16),
                   jax.ShapeDtypeStruct((ng, cw, cw), F32), jax.ShapeDtypeStruct((1, ng * cw), F32)],
        scratch_shapes=[pltpu.VMEM((HALO + s, cw), F32), pltpu.VMEM((s + HALO, cw), F32),
                        pltpu.VMEM((s, cw), F32), pltpu.VMEM((cw, cw), F32), pltpu.VMEM((8, cw), F32)],
        compiler_params=_params("arbitrary"),
    )(dy, a, p, p, pool_w, pool_scale)


def _halo_before(tm):
    return lambda i: jnp.maximum(i * (tm // HALO) - 1, 0)


def _halo_after(tm, s):
    return lambda i: jnp.minimum((i + 1) * (tm // HALO), s // HALO - 1)


def odd_mix_fwd(p, sconv_w, dconv_w, dconv_b, cnorm_g, cnorm_b, name, tm=128):
    s = p.shape[0]
    cw = sconv_w.shape[1]
    n = s // tm
    lanes = 128
    hb = _halo_before(tm)

    def body(hc_ref, hch_ref, bc_ref, cc_ref, cch_ref, ga_ref, gah_ref, gb_ref, gbh_ref, g1_ref, g2_ref,
             sw_ref, dw_ref, db_ref, gam_ref, bet_ref, y_ref, dc_ref):
        first = pl.program_id(0) == 0
        for l in range(cw // lanes):
            cols = slice(l * lanes, (l + 1) * lanes)
            mh = jnp.where(first, 0.0, cch_ref[:, cols].astype(F32) * hch_ref[:, cols].astype(F32))
            mm = cc_ref[:, cols].astype(F32) * hc_ref[:, cols].astype(F32)
            xx = jnp.concatenate([mh, mm], axis=0)
            tap = _Taps(xx, tm, True)
            cv = jnp.zeros((tm, lanes), F32)
            for k in range(SCONV_K):
                cv = cv + sw_ref[k:k + 1, cols] * tap(SCONV_K - 1 - k)
            c_out = bc_ref[:, cols].astype(F32) * cv
            y_ref[:, cols] = (c_out * _silu(g1_ref[:, cols].astype(F32))).astype(BF16)
            dh = jnp.where(first, 0.0, gah_ref[:, cols].astype(F32) * _sigmoid(gbh_ref[:, cols].astype(F32)))
            dm = ga_ref[:, cols].astype(F32) * _sigmoid(gb_ref[:, cols].astype(F32))
            xx = jnp.concatenate([dh, dm], axis=0)
            tap = _Taps(xx, tm, True)
            acc = jnp.zeros((tm, lanes), F32) + db_ref[:, cols]
            for k in range(CONF_K):
                acc = acc + dw_ref[k:k + 1, cols] * tap(CONF_K - 1 - k)
            dc_ref[:, cols] = acc
        rs = 32
        for r in range(tm // rs):
            rows = slice(r * rs, (r + 1) * rs)
            xv = dc_ref[rows, :]
            mu = jnp.mean(xv, axis=-1, keepdims=True)
            xc = xv - mu
            rstd = lax.rsqrt(jnp.mean(xc * xc, axis=-1, keepdims=True) + EPS)
            ln = xc * rstd * gam_ref[...] + bet_ref[...]
            y_ref[rows, cw:2 * cw] = (_silu(ln) * _silu(g2_ref[rows, :].astype(F32))).astype(BF16)

    main = lambda c: pl.BlockSpec((tm, cw), lambda i: (i, c))
    halo = lambda c: pl.BlockSpec((HALO, cw), lambda i: (hb(i), c))
    vec = lambda r: pl.BlockSpec((r, cw), lambda i: (0, 0))
    return pl.pallas_call(
        body, name=name, grid=(n,),
        in_specs=[main(0), halo(0), main(1), main(2), halo(2), main(3), halo(3), main(4), halo(4),
                  main(5), main(6), vec(SCONV_K), vec(CONF_K), vec(1), vec(1), vec(1)],
        out_specs=[pl.BlockSpec((tm, 2 * cw), lambda i: (i, 0)), pl.BlockSpec((tm, cw), lambda i: (i, 0))],
        out_shape=[jax.ShapeDtypeStruct((s, 2 * cw), BF16), jax.ShapeDtypeStruct((s, cw), F32)],
        compiler_params=_params("parallel"),
    )(p, p, p, p, p, p, p, p, p, p, p, sconv_w, dconv_w, dconv_b, cnorm_g, cnorm_b)


def odd_bwd_ln(dy, p, dc, cnorm_g, cnorm_b, name, tm=256):
    s = p.shape[0]
    cw = dc.shape[1]
    n = s // tm
    rs = 32

    def body(dy_ref, g2_ref, dc_ref, gam_ref, bet_ref, ddc_ref, dg_ref, dgam_ref, dbet_ref, gacc, bacc):
        i = pl.program_id(0)

        @pl.when(i == 0)
        def _():
            gacc[...] = jnp.zeros_like(gacc)
            bacc[...] = jnp.zeros_like(bacc)

        def chunk(ci, carry):
            rows = pl.ds(pl.multiple_of(ci * rs, rs), rs)
            xv = dc_ref[rows, :]
            mu = jnp.mean(xv, axis=-1, keepdims=True)
            xc = xv - mu
            rstd = lax.rsqrt(jnp.mean(xc * xc, axis=-1, keepdims=True) + EPS)
            xh = xc * rstd
            gam = gam_ref[...]
            sl, dsl = _silu_and_grad(xh * gam + bet_ref[...])
            sg, dsg = _silu_and_grad(g2_ref[rows, :].astype(F32))
            dyv = dy_ref[rows, :].astype(F32)
            dg_ref[rows, :] = (dyv * sl * dsg).astype(BF16)
            dln = dyv * sg * dsl
            gacc[...] += _rowsum8(dln * xh)
            bacc[...] += _rowsum8(dln)
            dxh = dln * gam
            ddc_ref[rows, :] = rstd * (dxh - jnp.mean(dxh, axis=-1, keepdims=True)
                                       - xh * jnp.mean(dxh * xh, axis=-1, keepdims=True))
            return carry

        lax.fori_loop(0, tm // rs, chunk, 0)

        @pl.when(i == n - 1)
        def _():
            dgam_ref[...] = jnp.sum(gacc[...], axis=0, keepdims=True)
            dbet_ref[...] = jnp.sum(bacc[...], axis=0, keepdims=True)

    vec = pl.BlockSpec((1, cw), lambda i: (0, 0))
    return pl.pallas_call(
        body, name=name, grid=(n,),
        in_specs=[pl.BlockSpec((tm, cw), lambda i: (i, 1)), pl.BlockSpec((tm, cw), lambda i: (i, 6)),
                  pl.BlockSpec((tm, cw), lambda i: (i, 0)), vec, vec],
        out_specs=[pl.BlockSpec((tm, cw), lambda i: (i, 0)), pl.BlockSpec((tm, cw), lambda i: (i, 0)), vec, vec],
        out_shape=[jax.ShapeDtypeStruct((s, cw), F32), jax.ShapeDtypeStruct((s, cw), BF16),
                   jax.ShapeDtypeStruct((1, cw), F32), jax.ShapeDtypeStruct((1, cw), F32)],
        scratch_shapes=[pltpu.VMEM((8, cw), F32), pltpu.VMEM((8, cw), F32)],
        compiler_params=_params("arbitrary"),
    )(dy, p, dc, cnorm_g, cnorm_b)


def odd_bwd_conv(dy, p, ddc, dg2, sconv_w, dconv_w, name, tm=128):
    s = p.shape[0]
    cw = ddc.shape[1]
    n = s // tm
    lanes = 128
    hb = _halo_before(tm)
    ha = _halo_after(tm, s)

    def body(dy_ref, dya_ref, g1_ref, g1a_ref, bc_ref, bca_ref, hc_ref, hch_ref, cc_ref, cch_ref,
             ddc_ref, ddca_ref, ga_ref, gah_ref, gb_ref, gbh_ref, dg2_ref, sw_ref, dw_ref,
             dp_ref, dsw_ref, ddw_ref, ddb_ref, sw_acc, dw_acc, db_acc):
        i = pl.program_id(0)
        first = i == 0
        last = i == n - 1

        @pl.when(first)
        def _():
            sw_acc[...] = jnp.zeros_like(sw_acc)
            dw_acc[...] = jnp.zeros_like(dw_acc)
            db_acc[...] = jnp.zeros_like(db_acc)

        for l in range(cw // lanes):
            cols = slice(l * lanes, (l + 1) * lanes)
            mh = jnp.where(first, 0.0, cch_ref[:, cols].astype(F32) * hch_ref[:, cols].astype(F32))
            hcv = hc_ref[:, cols].astype(F32)
            ccv = cc_ref[:, cols].astype(F32)
            xx = jnp.concatenate([mh, ccv * hcv], axis=0)
            tap = _Taps(xx, tm, True)
            taps = [tap(SCONV_K - 1 - k) for k in range(SCONV_K)]
            cv = jnp.zeros((tm, lanes), F32)
            for k in range(SCONV_K):
                cv = cv + sw_ref[k:k + 1, cols] * taps[k]
            bcv = bc_ref[:, cols].astype(F32)
            dyv = dy_ref[:, cols].astype(F32)
            sg, dsg = _silu_and_grad(g1_ref[:, cols].astype(F32))
            dco = dyv * sg
            dp_ref[:, 5 * cw + l * lanes:5 * cw + (l + 1) * lanes] = (dyv * bcv * cv * dsg).astype(BF16)
            dp_ref[:, cw + l * lanes:cw + (l + 1) * lanes] = (dco * cv).astype(BF16)
            dcv = dco * bcv
            for k in range(SCONV_K):
                sw_acc[k * 8:(k + 1) * 8, cols] += _rowsum8(dcv * taps[k])
            dcv_a = jnp.where(last, 0.0, dya_ref[:, cols].astype(F32) * _silu(g1a_ref[:, cols].astype(F32))
                              * bca_ref[:, cols].astype(F32))
            xx = jnp.concatenate([dcv, dcv_a], axis=0)
            tap = _Taps(xx, tm, False)
            dm = jnp.zeros((tm, lanes), F32)
            for k in range(SCONV_K):
                dm = dm + sw_ref[k:k + 1, cols] * tap(SCONV_K - 1 - k)
            dp_ref[:, l * lanes:(l + 1) * lanes] = (dm * ccv).astype(BF16)
            dp_ref[:, 2 * cw + l * lanes:2 * cw + (l + 1) * lanes] = (dm * hcv).astype(BF16)
            gav = ga_ref[:, cols].astype(F32)
            sb = _sigmoid(gb_ref[:, cols].astype(F32))
            dh = jnp.where(first, 0.0, gah_ref[:, cols].astype(F32) * _sigmoid(gbh_ref[:, cols].astype(F32)))
            xx = jnp.concatenate([dh, gav * sb], axis=0)
            ddcv = ddc_ref[:, cols]
            db_acc[:, cols] += _rowsum8(ddcv)
            tap = _Taps(xx, tm, True)
            for k in range(CONF_K):
                dw_acc[k * 8:(k + 1) * 8, cols] += _rowsum8(ddcv * tap(CONF_K - 1 - k))
            ddc_a = jnp.where(last, 0.0, ddca_ref[:, cols])
            xx = jnp.concatenate([ddcv, ddc_a], axis=0)
            tap = _Taps(xx, tm, False)
            dgl = jnp.zeros((tm, lanes), F32)
            for k in range(CONF_K):
                dgl = dgl + dw_ref[k:k + 1, cols] * tap(CONF_K - 1 - k)
            dp_ref[:, 3 * cw + l * lanes:3 * cw + (l + 1) * lanes] = (dgl * sb).astype(BF16)
            dp_ref[:, 4 * cw + l * lanes:4 * cw + (l + 1) * lanes] = (dgl * gav * sb * (1.0 - sb)).astype(BF16)
        dp_ref[:, 6 * cw:7 * cw] = dg2_ref[...]

        @pl.when(last)
        def _():
            for k in range(SCONV_K):
                dsw_ref[k:k + 1, :] = jnp.sum(sw_acc[k * 8:(k + 1) * 8, :], axis=0, keepdims=True)
            for k in range(CONF_K):
                ddw_ref[k:k + 1, :] = jnp.sum(dw_acc[k * 8:(k + 1) * 8, :], axis=0, keepdims=True)
            ddb_ref[...] = jnp.sum(db_acc[...], axis=0, keepdims=True)

    def main(c):
        return pl.BlockSpec((tm, cw), lambda i: (i, c))

    def before(c):
        return pl.BlockSpec((HALO, cw), lambda i: (hb(i), c))

    def after(c):
        return pl.BlockSpec((HALO, cw), lambda i: (ha(i), c))

    def vec(r):
        return pl.BlockSpec((r, cw), lambda i: (0, 0))

    return pl.pallas_call(
        body, name=name, grid=(n,),
        in_specs=[main(0), after(0), main(5), after(5), main(1), after(1), main(0), before(0), main(2), before(2),
                  main(0), after(0), main(3), before(3), main(4), before(4), main(0), vec(SCONV_K), vec(CONF_K)],
        out_specs=[pl.BlockSpec((tm, 7 * cw), lambda i: (i, 0)), vec(SCONV_K), vec(CONF_K), vec(1)],
        out_shape=[jax.ShapeDtypeStruct((s, 7 * cw), BF16), jax.ShapeDtypeStruct((SCONV_K, cw), F32),
                   jax.ShapeDtypeStruct((CONF_K, cw), F32), jax.ShapeDtypeStruct((1, cw), F32)],
        scratch_shapes=[pltpu.VMEM((8 * SCONV_K, cw), F32), pltpu.VMEM((8 * CONF_K, cw), F32),
                        pltpu.VMEM((8, cw), F32)],
        compiler_params=_params("arbitrary"),
    )(dy, dy, p, p, p, p, p, p, p, p, ddc, ddc, p, p, p, p, dg2, sconv_w, dconv_w)


_ANY = pl.BlockSpec(memory_space=pl.ANY)


def _place():
    return lax.axis_index("x"), lax.axis_index("y"), lax.axis_index("c")


def all_gather(arrs, name, deps=()):
    n = len(arrs)

    def body(*refs):
        ins, outs = refs[:n], refs[n + len(deps):2 * n + len(deps)]
        send_sems, recv_sems, local_sems = refs[-3:]
        x, y, c = _place()
        me, sibling = (x, y, c), (x, y, 1 - c)
        chips = [(1 - x, y), (x, 1 - y), (1 - x, 1 - y)]

        def copy(a, k, block, to, src=None):
            px, py, pc = block
            dst = outs[a].at[4 * px + 2 * py + pc]
            return pltpu.make_async_remote_copy(
                src_ref=dst if src is None else src, dst_ref=dst,
                send_sem=send_sems.at[7 * a + k], recv_sem=recv_sems.at[7 * a + k],
                device_id=to, device_id_type=MESH)

        mine = [pltpu.make_async_copy(ins[a], outs[a].at[4 * x + 2 * y + c], local_sems.at[a]) for a in range(n)]
        first = []
        for a in range(n):
            first.append(copy(a, 0, me, sibling, src=ins[a]))
            first += [copy(a, 1 + j, me, (*chip, c), src=ins[a]) for j, chip in enumerate(chips)]
        for cp in first + mine:
            cp.start()
        passed = []
        for a in range(n):
            for j, chip in enumerate(chips):
                copy(a, 1 + j, (*chip, c), me).wait_recv()
                cp = copy(a, 4 + j, (*chip, c), sibling)
                cp.start()
                passed.append(cp)
        for a in range(n):
            copy(a, 0, sibling, me).wait_recv()
            for j, chip in enumerate(chips):
                copy(a, 4 + j, (*chip, 1 - c), me).wait_recv()
        for cp in first + passed:
            cp.wait_send()
        for cp in mine:
            cp.wait()

    return pl.pallas_call(
        body, name=name,
        out_shape=[jax.ShapeDtypeStruct((N_DEV,) + a.shape, a.dtype) for a in arrs],
        in_specs=[_ANY] * (n + len(deps)), out_specs=[_ANY] * n,
        scratch_shapes=[pltpu.SemaphoreType.DMA((7 * n,)), pltpu.SemaphoreType.DMA((7 * n,)),
                        pltpu.SemaphoreType.DMA((n,))],
    )(*arrs, *deps)


def in_proj_gathered(xs, g, w_own, extras, name, tm=512):
    s, d = xs.shape
    n = w_own.shape[1]
    arrs = [w_own] + list(extras)
    na = len(arrs)
    tr = 256

    def body(*refs):
        x_ref, g_ref, ins = refs[0], refs[1], refs[2:2 + na]
        h_out, p_ref, outs = refs[2 + na], refs[3 + na], refs[4 + na:4 + 2 * na]
        (h_ref, xbuf, wbuf, obuf, send_sems, recv_sems, load_sem, store_sems, own_sems, h_sem,
         x_sems) = refs[4 + 2 * na:]
        x, y, c = _place()
        me, sibling = (x, y, c), (x, y, 1 - c)
        x_first = c == 0
        near = (jnp.where(x_first, 1 - x, x), jnp.where(x_first, y, 1 - y))
        far = (jnp.where(x_first, x, 1 - x), jnp.where(x_first, 1 - y, y))
        diag = (1 - x, 1 - y)
        k_near, k_far = jnp.where(x_first, 1, 2), jnp.where(x_first, 2, 1)
        f_near, f_far = k_near + 3, k_far + 3

        def slot(block):
            return 4 * block[0] + 2 * block[1] + block[2]

        def copy(a, k, block, to, src=None):
            dst = outs[a].at[slot(block)]
            return pltpu.make_async_remote_copy(
                src_ref=dst if src is None else src, dst_ref=dst,
                send_sem=send_sems.at[7 * a + k], recv_sem=recv_sems.at[7 * a + k],
                device_id=to, device_id_type=MESH)

        first = []
        for a in range(na):
            first += [copy(a, 0, me, sibling, src=ins[a]), copy(a, 1, me, (1 - x, y, c), src=ins[a]),
                      copy(a, 2, me, (x, 1 - y, c), src=ins[a])]
        for cp in first:
            cp.start()
        own = pltpu.make_async_copy(wbuf.at[0], outs[0].at[slot(me)], own_sems.at[0])
        mine = [pltpu.make_async_copy(ins[a], outs[a].at[slot(me)], own_sems.at[a]) for a in range(1, na)]
        stores = [None, None]

        def x_load(i):
            return pltpu.make_async_copy(x_ref.at[pl.ds(i * tr, tr), :], xbuf.at[i % 2], x_sems.at[i % 2])

        x_load(0).start()
        for i in range(s // tr):
            if i + 1 < s // tr:
                x_load(i + 1).start()
            x_load(i).wait()
            xv = xbuf[i % 2]
            r = lax.rsqrt(jnp.mean(xv * xv, axis=-1, keepdims=True) + EPS)
            h_ref[i * tr:(i + 1) * tr, :] = (xv * r * g_ref[...]).astype(BF16)
        h_store = pltpu.make_async_copy(h_ref, h_out, h_sem)
        h_store.start()

        def multiply(k, block, w_from):
            b = k % 2
            if k == 2:
                own.wait()
            load = pltpu.make_async_copy(w_from, wbuf.at[b], load_sem)
            load.start()
            if stores[b] is not None:
                stores[b].wait()
            load.wait()
            if k == 0:
                own.start()

            def chunk(i, carry):
                rows = pl.ds(pl.multiple_of(i * tm, tm), tm)
                obuf[b, rows, :] = jnp.dot(h_ref[rows, :], wbuf[b], preferred_element_type=F32).astype(BF16)
                return carry

            lax.fori_loop(0, s // tm, chunk, 0)
            stores[b] = pltpu.make_async_copy(
                obuf.at[b], p_ref.at[:, pl.ds(pl.multiple_of(slot(block) * n, 128), n)], store_sems.at[b])
            stores[b].start()

        passed = []

        def arrive(a, k, block):
            copy(a, k, block, me).wait_recv()

        def pass_on(a, k, block, to):
            cp = copy(a, k, block, to)
            cp.start()
            passed.append(cp)

        def gather(a, use):
            use(0, me)
            arrive(a, 0, sibling)
            use(1, sibling)
            arrive(a, k_near, (*near, c))
            pass_on(a, 3, (*near, c), (*far, c))
            pass_on(a, f_near, (*near, c), sibling)
            use(2, (*near, c))
            arrive(a, f_far, (*far, 1 - c))
            use(3, (*far, 1 - c))
            arrive(a, k_far, (*far, c))
            pass_on(a, f_far, (*far, c), sibling)
            use(4, (*far, c))
            arrive(a, f_near, (*near, 1 - c))
            use(5, (*near, 1 - c))
            arrive(a, 3, (*diag, c))
            pass_on(a, 6, (*diag, c), sibling)
            use(6, (*diag, c))
            arrive(a, 6, (*diag, 1 - c))
            use(7, (*diag, 1 - c))

        gather(0, lambda k, block: multiply(k, block, ins[0] if k == 0 else outs[0].at[slot(block)]))
        for cp in mine:
            cp.start()
        for a in range(1, na):
            gather(a, lambda k, block: None)
        for cp in first + passed:
            cp.wait_send()
        for cp in mine + stores + [h_store]:
            cp.wait()

    vmem = pl.BlockSpec(memory_space=pltpu.VMEM)
    outs = pl.pallas_call(
        body, name=name,
        out_shape=[jax.ShapeDtypeStruct((s, d), BF16), jax.ShapeDtypeStruct((s, N_DEV * n), BF16)]
        + [jax.ShapeDtypeStruct((N_DEV,) + a.shape, a.dtype) for a in arrs],
        in_specs=[_ANY, vmem] + [_ANY] * na, out_specs=[_ANY] * (2 + na),
        scratch_shapes=[pltpu.VMEM((s, d), BF16), pltpu.VMEM((2, tr, d), F32), pltpu.VMEM((2, d, n), BF16),
                        pltpu.VMEM((2, s, n), BF16),
                        pltpu.SemaphoreType.DMA((7 * na,)), pltpu.SemaphoreType.DMA((7 * na,)),
                        pltpu.SemaphoreType.DMA, pltpu.SemaphoreType.DMA((2,)), pltpu.SemaphoreType.DMA((na,)),
                        pltpu.SemaphoreType.DMA, pltpu.SemaphoreType.DMA((2,))],
        compiler_params=pltpu.CompilerParams(vmem_limit_bytes=VMEM_LIMIT),
    )(xs, g, *arrs)
    return outs[0], outs[1], outs[2], outs[3:]


_HBM = pl.BlockSpec(memory_space=pltpu.HBM)
_SEM = pl.BlockSpec(memory_space=pltpu.SEMAPHORE)
_DATAFLOW = pltpu.SideEffectType.DATAFLOW_SIDE_EFFECTING


def _peers_per_array(kind):
    return 1 if kind in ("sibling", "halves") else 3


def _split_copies(kind, srcs, lands, send_sems, recv_sems):
    x, y, c = _place()
    per = _peers_per_array(kind)
    out = []
    for a in range(len(lands)):
        if kind == "sibling":
            part = srcs[a] if srcs[a].shape[1] == 1 else srcs[a].at[:, pl.ds(1 - c, 1)]
            peers = [((x, y, 1 - c), part, lands[a], lands[a])]
        elif kind == "halves":
            mine, its = lands[a].at[:, pl.ds(c, 1)], lands[a].at[:, pl.ds(1 - c, 1)]
            peers = [((x, y, 1 - c), mine, mine, its)]
        else:
            peers = []
            for px, py in [(1 - x, y), (x, 1 - y), (1 - x, 1 - y)]:
                if kind == "gather":
                    views = (srcs[a], lands[a].at[4 * x + 2 * y + c], lands[a].at[4 * px + 2 * py + c])
                else:
                    views = (srcs[a].at[2 * px + py], lands[a].at[2 * x + y], lands[a].at[2 * px + py])
                peers.append(((px, py, c),) + views)
        for j, (peer, src, dst, arrives) in enumerate(peers):
            sems = dict(send_sem=send_sems.at[per * a + j], recv_sem=recv_sems.at[per * a + j],
                        device_id=peer, device_id_type=MESH)
            out.append((pltpu.make_async_remote_copy(src_ref=src, dst_ref=dst, **sems),
                        pltpu.make_async_remote_copy(src_ref=src, dst_ref=arrives, **sems)))
    return out


def split_start(kind, srcs, lands, deps, name):
    ns, nl = len(srcs), len(lands)
    n_sems = _peers_per_array(kind) * nl
    held = list(srcs) + list(lands)

    def body(*refs):
        send_sems, recv_sems = refs[len(held) + len(deps)], refs[len(held) + len(deps) + 1]
        for copy, _ in _split_copies(kind, refs[:ns], refs[ns:ns + nl], send_sems, recv_sems):
            copy.start()
        token = refs[-1]
        token[...] = jnp.zeros_like(token)

    outs = pl.pallas_call(
        body, name=name,
        out_shape=(pltpu.SemaphoreType.DMA((n_sems,)), pltpu.SemaphoreType.DMA((n_sems,)),
                   *[pltpu.HBM(a.shape, a.dtype) for a in held], jax.ShapeDtypeStruct((8, 128), F32)),
        in_specs=[_HBM] * len(held) + [_ANY] * len(deps),
        out_specs=(_SEM, _SEM, *([_HBM] * len(held)), pl.BlockSpec(memory_space=pltpu.VMEM)),
        input_output_aliases={i: 2 + i for i in range(len(held))},
        compiler_params=pltpu.CompilerParams(has_side_effects=_DATAFLOW),
    )(*[pltpu.with_memory_space_constraint(a, pltpu.HBM) for a in held], *deps)
    return outs[0], outs[1], list(outs[2:2 + ns]), list(outs[2 + ns:2 + ns + nl]), outs[-1]


def split_wait(kind, send_sems, recv_sems, srcs, lands, afters, name):
    ns, nl = len(srcs), len(lands)
    held = list(srcs) + list(lands)

    def body(*refs):
        for _, arrival in _split_copies(kind, refs[:ns], refs[ns:ns + nl], refs[ns + nl], refs[ns + nl + 1]):
            arrival.wait_send()
            arrival.wait_recv()

    outs = pl.pallas_call(
        body, name=name,
        out_shape=[pltpu.HBM(a.shape, a.dtype) for a in held],
        in_specs=[_HBM] * len(held) + [_SEM, _SEM] + [_ANY] * len(afters),
        out_specs=[_HBM] * len(held),
        input_output_aliases={i: i for i in range(len(held))},
        compiler_params=pltpu.CompilerParams(has_side_effects=_DATAFLOW),
    )(*held, send_sems, recv_sems, *afters)
    return list(outs[:ns]), list(outs[ns:])


def place_block(land, block, dev, name):
    r, c = block.shape
    tr = min(r, 512)

    def body(dev_ref, land_ref, b_ref, o_ref):
        del dev_ref, land_ref
        o_ref[...] = b_ref[...]

    return pl.pallas_call(
        body, name=name,
        grid_spec=pltpu.PrefetchScalarGridSpec(
            num_scalar_prefetch=1, grid=(r // tr,),
            in_specs=[_ANY, pl.BlockSpec((tr, c), lambda i, dev_ref: (i, 0))],
            out_specs=pl.BlockSpec((None, tr, c), lambda i, dev_ref: (dev_ref[0], i, 0))),
        out_shape=jax.ShapeDtypeStruct(land.shape, land.dtype),
        input_output_aliases={1: 0},
        compiler_params=_params("parallel"),
    )(dev, land, block)


def pair_add(own, recv, core, name):
    _, _, r, c = own.shape
    tr = min(r, 512)

    def body(core_ref, own_ref, recv_ref, o_ref):
        del core_ref
        o_ref[...] = (own_ref[...].astype(F32) + recv_ref[...].astype(F32)).astype(BF16)

    return pl.pallas_call(
        body, name=name,
        grid_spec=pltpu.PrefetchScalarGridSpec(
            num_scalar_prefetch=1, grid=(4, r // tr),
            in_specs=[pl.BlockSpec((None, None, tr, c), lambda k, i, core_ref: (k, core_ref[0], i, 0)),
                      pl.BlockSpec((None, None, tr, c), lambda k, i, core_ref: (k, 0, i, 0))],
            out_specs=pl.BlockSpec((None, tr, c), lambda k, i, core_ref: (k, i, 0))),
        out_shape=jax.ShapeDtypeStruct((4, r, c), BF16),
        compiler_params=_params("parallel", "parallel"),
    )(core, own, recv)


def _adamw_math(w, g, m, v):
    m2 = ADAM_B1 * m + (1.0 - ADAM_B1) * g
    v2 = ADAM_B2 * v + (1.0 - ADAM_B2) * (g * g)
    m_hat = m2 / (1.0 - ADAM_B1 ** ADAM_STEP)
    v_hat = v2 / (1.0 - ADAM_B2 ** ADAM_STEP)
    delta = -ADAM_LR * (m_hat / (jnp.sqrt(v_hat) + ADAM_EPS) + ADAM_WD * w)
    return delta, m2, v2


def adamw_big(w, m, v, own, got, chip, name):
    r, c = w.shape
    tr = min(r, 256)

    def body(chip_ref, w_ref, m_ref, v_ref, p0, p1, p2, p3, g_ref, d_ref, m2_ref, v2_ref):
        del chip_ref
        g = ((p0[...].astype(F32) + p1[...].astype(F32)) + p2[...].astype(F32)) + p3[...].astype(F32)
        delta, m2, v2 = _adamw_math(w_ref[...], g, m_ref[...], v_ref[...])
        g_ref[...] = g
        d_ref[...] = delta
        m2_ref[...] = m2
        v2_ref[...] = v2

    row = pl.BlockSpec((tr, c), lambda i, chip_ref: (i, 0))

    def slab(flip):
        return pl.BlockSpec((None, tr, c), lambda i, chip_ref: (chip_ref[0] ^ flip, i, 0))

    return pl.pallas_call(
        body, name=name,
        grid_spec=pltpu.PrefetchScalarGridSpec(
            num_scalar_prefetch=1, grid=(r // tr,),
            in_specs=[row, row, row, slab(0), slab(1), slab(2), slab(3)],
            out_specs=[row] * 4),
        out_shape=[jax.ShapeDtypeStruct((r, c), F32)] * 4,
        compiler_params=_params("parallel"),
    )(chip, w, m, v, own, got, got, got)


def sum_devices(g8, name):
    def body(g_ref, o_ref):
        tot = g_ref[0]
        for k in range(1, N_DEV):
            tot = tot + g_ref[k]
        o_ref[...] = tot

    return pl.pallas_call(body, name=name, out_shape=jax.ShapeDtypeStruct(g8.shape[1:], F32))(g8)


def adamw_small(ws, gs, ms, vs, name):
    n = len(ws)

    def body(*refs):
        w_r, g_r, m_r, v_r = refs[:n], refs[n:2 * n], refs[2 * n:3 * n], refs[3 * n:4 * n]
        d_o, m_o, v_o = refs[4 * n:5 * n], refs[5 * n:6 * n], refs[6 * n:7 * n]
        for k in range(n):
            delta, m2, v2 = _adamw_math(w_r[k][...], g_r[k][...], m_r[k][...], v_r[k][...])
            d_o[k][...] = delta
            m_o[k][...] = m2
            v_o[k][...] = v2

    shapes = [jax.ShapeDtypeStruct(w.shape, F32) for w in ws]
    outs = pl.pallas_call(body, name=name, out_shape=shapes * 3)(*ws, *gs, *ms, *vs)
    return outs[:n], outs[n:2 * n], outs[2 * n:]


def _rows128(a):
    return a.reshape(-1, 128)


def _pad_rows(a, rows):
    return jnp.pad(a, ((0, rows - a.shape[0]), (0, 0)))


def kernel(x, ln_pre_even, w_in_even, pool_w, pool_scale, w_out_even, ln_post_even, ln_pre_odd, w_in_odd, sconv_w, dconv_w, dconv_b, cnorm_g, cnorm_b, w_out_odd, ln_post_odd, loss_target, m_ln_pre_even, m_w_in_even, m_pool_w, m_pool_scale, m_w_out_even, m_ln_post_even, m_ln_pre_odd, m_w_in_odd, m_sconv_w, m_dconv_w, m_dconv_b, m_cnorm_g, m_cnorm_b, m_w_out_odd, m_ln_post_odd, v_ln_pre_even, v_w_in_even, v_pool_w, v_pool_scale, v_w_out_even, v_ln_post_even, v_ln_pre_odd, v_w_in_odd, v_sconv_w, v_dconv_w, v_dconv_b, v_cnorm_g, v_cnorm_b, v_w_out_odd, v_ln_post_odd):
    xs = x[0]
    tgt = loss_target[0]
    s, d = xs.shape
    half = d // 2
    n_heads = half // HEAD_DIM
    ng = len(POOL_WINDOWS)
    cwp = half // ng
    dev = 4 * lax.axis_index("x") + 2 * lax.axis_index("y") + lax.axis_index("c")
    core = lax.axis_index("c").astype(jnp.int32).reshape(1)

    pr = pool_w.shape[2]
    cl = sconv_w.shape[2]
    small_parts = [(_rows128(ln_pre_odd), 8), (sconv_w[0], 8), (dconv_w[0], 32), (dconv_b, 8),
                   (cnorm_g, 8), (cnorm_b, 8), (_rows128(ln_post_odd), 8)]
    small_local = jnp.concatenate([_pad_rows(a, r) for a, r in small_parts], axis=0)
    h0, p0, g_wie, (g_pw, g_small) = in_proj_gathered(
        xs, ln_pre_even, w_in_even[0].astype(BF16), [pool_w[0].reshape(ng * pr, cwp).astype(BF16), small_local],
        "ag_in_proj_even")
    comm = _Exchanges(dev, core, d)
    token = comm.start_weights("out_even", [w_out_even[0].astype(BF16)], [p0])
    sb_dep = comm.start_weights("odd", [w_in_odd[0].astype(BF16), w_out_odd[0].astype(BF16)], [token])
    pool_full = g_pw.reshape(N_DEV, ng, pr, cwp).transpose(1, 0, 2, 3).reshape(ng, cwp, cwp)
    nl = ln_pre_odd.shape[1] // 128

    def chan(lo, rows):
        return g_small[:, lo:lo + rows].transpose(1, 0, 2).reshape(rows, N_DEV * cl)

    ln_pre_odd_f = g_small[:, 0:nl].reshape(1, d)
    sconv_f = chan(8, SCONV_K)
    dconv_f = chan(16, CONF_K)
    dconv_b_f = chan(48, 1)
    cnorm_g_f = chan(56, 1)
    cnorm_b_f = chan(64, 1)
    ln_post_odd_f = g_small[:, 72:72 + nl].reshape(1, d)

    loss_blk, grad_x, small_g = _fwd_bwd(
        xs, tgt, ln_pre_even, h0, p0, g_wie, pool_full, pool_scale, ln_post_even, ln_pre_odd_f,
        sconv_f, dconv_f, dconv_b_f, cnorm_g_f, cnorm_b_f, ln_post_odd_f, comm, sb_dep)
    small_w = [ln_pre_even, pool_scale, ln_post_even, ln_pre_odd, sconv_w[0], dconv_w[0], dconv_b, cnorm_g, cnorm_b, ln_post_odd]
    small_m = [m_ln_pre_even, m_pool_scale, m_ln_post_even, m_ln_pre_odd, m_sconv_w[0], m_dconv_w[0], m_dconv_b, m_cnorm_g, m_cnorm_b, m_ln_post_odd]
    small_v = [v_ln_pre_even, v_pool_scale, v_ln_post_even, v_ln_pre_odd, v_sconv_w[0], v_dconv_w[0], v_dconv_b, v_cnorm_g, v_cnorm_b, v_ln_post_odd]
    big = {"w_in_even": (w_in_even, m_w_in_even, v_w_in_even), "pool_w": (pool_w, m_pool_w, v_pool_w),
           "w_out_even": (w_out_even, m_w_out_even, v_w_out_even), "w_in_odd": (w_in_odd, m_w_in_odd, v_w_in_odd),
           "w_out_odd": (w_out_odd, m_w_out_odd, v_w_out_odd)}
    upd = comm.finish_updates(big, [grad_x])
    upd.update(comm.finish_updates(big, [grad_x]))
    sg, sd, sm, sv, loss = _update_small(small_g, loss_blk, small_w, small_m, small_v, dev, d, cl,
                                         deps=[upd["w_in_odd"][1], upd["w_out_even"][1]])
    upd.update(comm.finish_updates(big, sd))
    (g_wie_o, d_wie, m_wie, v_wie), (g_pw_o, d_pw, m_pw, v_pw) = upd["w_in_even"], upd["pool_w"]
    (g_woe_o, d_woe, m_woe, v_woe), (g_wio_o, d_wio, m_wio, v_wio) = upd["w_out_even"], upd["w_in_odd"]
    g_woo_o, d_woo, m_woo, v_woo = upd["w_out_odd"]

    def order(small, wie, pw, woe, wio, woo):
        return [small[0], wie, pw, small[1], woe, small[2], small[3], wio, small[4], small[5], small[6],
                small[7], small[8], woo, small[9]]

    grads = order(sg, g_wie_o, g_pw_o, g_woe_o, g_wio_o, g_woo_o)
    deltas = order(sd, d_wie, d_pw, d_woe, d_wio, d_woo)
    new_m = order(sm, m_wie, m_pw, m_woe, m_wio, m_woo)
    new_v = order(sv, v_wie, v_pw, v_woe, v_wio, v_woo)
    return (loss, grad_x[None], *grads, *deltas, *new_m, *new_v)


def _fwd_bwd(xs, tgt, ln_pre_even, h0, p0, g_wie, pool_full, pool_scale, ln_post_even, ln_pre_odd_f,
             sconv_f, dconv_f, dconv_b_f, cnorm_g_f, cnorm_b_f, ln_post_odd_f, comm, sb_dep):
    d = xs.shape[1]
    n_heads = d // 2 // HEAD_DIM
    ng, cwp = pool_full.shape[0], pool_full.shape[1]
    a0, sb_wts = sb_fwd(p0, n_heads, "sb_fwd", dep=sb_dep)
    dep = comm.weights_arrived("out_even", after=a0)
    y0 = even_mix_fwd(a0, p0, pool_full, pool_scale, "even_mix_fwd", dep=dep)
    (w_out_e,) = comm.weights("out_even", after=y0)
    w_out_e = w_out_e.reshape(1, d, d)
    o0 = mm_nn(y0, w_out_e, BF16, "out_proj_even", tn=512)
    dep = comm.weights_arrived("odd", after=o0)
    x1, h1 = postnorm_fwd(xs, o0, ln_post_even, ln_pre_odd_f, "post_even", dep=dep)
    g_wio, w_out_o = comm.weights("odd", after=x1)
    w_out_o = w_out_o.reshape(1, d, d)
    p1 = mm_nn(h1, g_wio, BF16, "in_proj_odd", group=2)
    y1, dc = odd_mix_fwd(p1, sconv_f, dconv_f, dconv_b_f, cnorm_g_f, cnorm_b_f, "odd_mix_fwd")
    o1 = mm_nn(y1, w_out_o, BF16, "out_proj_odd", tn=512)
    loss_blk, gx2, do1, dg_post_odd = final_fwd_bwd(x1, o1, ln_post_odd_f, tgt, "post_odd_loss")

    dw_out_o = mm_tn(y1, do1, 1, BF16, "dw_out_odd")
    dy1 = mm_nt(do1, w_out_o, BF16, "dy_odd")
    ddc, dg2, dgam, dbet = odd_bwd_ln(dy1, p1, dc, cnorm_g_f, cnorm_b_f, "odd_bwd_ln")
    dp1, dsconv, ddconv, ddconv_b = odd_bwd_conv(dy1, p1, ddc, dg2, sconv_f, dconv_f, "odd_bwd_conv")
    dw_in_o = mm_tn(h1, dp1, N_DEV, BF16, "dw_in_odd", group=2)
    dep = comm.reduce_begin({"w_out_odd": dw_out_o.reshape(N_DEV, d // N_DEV, d), "w_in_odd": dw_in_o}, "odd")
    dh1 = mm_nt(dp1, g_wio, BF16, "dh_odd", dep=dep, group=2)
    dep = comm.reduce_send(after=dh1)
    gx1, dg_pre_odd, do0, dg_post_even = norm_bwd(dh1, x1, ln_pre_odd_f, gx2, "pre_odd_post_even_bwd",
                                                  inp2=o0, g2=ln_post_even, dep=dep)

    dw_out_e = mm_tn(y0, do0, 1, BF16, "dw_out_even")
    dy0 = mm_nt(do0, w_out_e, BF16, "dy_even")
    da0, du0, dg0, dpool, dpool_scale = even_mix_bwd(dy0, a0, p0, pool_full, pool_scale, "even_mix_bwd")
    pr = cwp // N_DEV
    dpool_slabs = dpool.astype(BF16).reshape(ng, N_DEV, pr, cwp).transpose(1, 0, 2, 3).reshape(N_DEV, ng * pr, cwp)
    dep = comm.reduce_begin({"w_out_even": dw_out_e.reshape(N_DEV, d // N_DEV, d), "pool_w": dpool_slabs}, "even_out")
    dq0, dk0, dv0 = sb_bwd(p0, a0, sb_wts, da0, n_heads, "sb_bwd", dep=dep)
    dep = comm.reduce_send(after=dq0)
    dp0 = jnp.concatenate([dq0, dk0, dv0, du0, dg0], axis=1)
    dw_sibling = mm_tn(h0, dp0, N_DEV // 2, BF16, "dw_in_even_sibling", dep=dep, pick=(2, 1 - comm.core))
    dep = comm.reduce_begin({"w_in_even": dw_sibling}, "even_in", sibling_part=True)
    dw_own = mm_tn(h0, dp0, N_DEV // 2, BF16, "dw_in_even_own", dep=dep, pick=(2, comm.core))
    dep = comm.reduce_send(after=dw_own, own_part={"w_in_even": dw_own})
    dh0 = mm_nt(dp0, g_wie, BF16, "dh_even", dep=dep, group=2)
    dep = None
    grad_x, dg_pre_even = norm_bwd(dh0, xs, ln_pre_even, gx1, "pre_even_bwd", dep=dep)
    small_g = [dg_pre_even, dpool_scale, dg_post_even, dg_pre_odd, dsconv, ddconv, ddconv_b, dgam, dbet, dg_post_odd]
    return loss_blk, grad_x, small_g


class _Exchanges:
    def __init__(self, dev, core, d):
        self.dev = dev.astype(jnp.int32).reshape(1)
        self.core = core
        self.chip = (dev // 2).astype(jnp.int32).reshape(1)
        self.d = d
        self.in_flight = {}
        self.to_sibling = None
        self.pending = []

    def start_weights(self, tag, blocks, afters):
        lands = [lax.empty((N_DEV,) + b.shape, b.dtype) for b in blocks]
        send, recv, srcs, lands, token = split_start("gather", blocks, lands, afters, "ag_start_" + tag)
        self.in_flight[tag] = (send, recv, srcs, lands)
        return token

    def weights_arrived(self, tag, after):
        send, recv, srcs, lands = self.in_flight.pop(tag)
        srcs, lands = split_wait("gather", send, recv, srcs, lands, [after], "ag_wait_" + tag)
        lands = [place_block(l, b, self.dev, "ag_own_%s_%d" % (tag, k)) for k, (l, b) in enumerate(zip(lands, srcs))]
        lands = [l.reshape((4, 2) + l.shape[1:]) for l in lands]
        send, recv, _, lands, token = split_start("halves", [], lands, [], "ag_sibling_start_" + tag)
        self.in_flight[tag] = (send, recv, lands)
        return token

    def weights(self, tag, after):
        send, recv, lands = self.in_flight.pop(tag)
        _, lands = split_wait("halves", send, recv, [], lands, [after], "ag_sibling_wait_" + tag)
        return [l.reshape((N_DEV,) + l.shape[2:]) for l in lands]

    def reduce_begin(self, partials, tag, sibling_part=False):
        names = list(partials)
        arrs = [partials[k].reshape((4, 1 if sibling_part else 2) + partials[k].shape[1:]) for k in names]
        lands = [lax.empty((4, 1) + a.shape[2:], a.dtype) for a in arrs]
        send, recv, srcs, lands, token = split_start("sibling", arrs, lands, [], "rs_sibling_start_" + tag)
        self.to_sibling = (tag, names, send, recv, srcs, lands)
        return token

    def reduce_send(self, after, own_part=None):
        tag, names, send, recv, srcs, lands = self.to_sibling
        srcs, lands = split_wait("sibling", send, recv, srcs, lands, [after], "rs_sibling_wait_" + tag)
        which = self.core
        if own_part is not None:
            srcs = [own_part[k].reshape((4, 1) + own_part[k].shape[1:]) for k in names]
            which = jnp.zeros((1,), jnp.int32)
        sums = [pair_add(o, r, which, "rs_pair_add_" + k) for k, o, r in zip(names, srcs, lands)]
        zones = [lax.empty(a.shape, a.dtype) for a in sums]
        send, recv, srcs, zones, token = split_start("scatter", sums, zones, [], "rs_start_" + tag)
        self.pending.append((tag, names, send, recv, srcs, zones))
        return token

    def finish_updates(self, big, afters):
        tag, names, send, recv, srcs, lands = self.pending.pop(0)
        srcs, lands = split_wait("scatter", send, recv, srcs, lands, afters, "rs_wait_" + tag)
        out = {}
        for name, own, got in zip(names, srcs, lands):
            w, m, v = big[name]
            shp = own.shape[1:]
            outs = adamw_big(w.reshape(shp), m.reshape(shp), v.reshape(shp), own, got, self.chip, "adamw_" + name)
            out[name] = [o.reshape(w.shape) for o in outs]
        return out


def _update_small(small_g, loss_blk, small_w, small_m, small_v, dev, d, cl, deps):
    packed = jnp.concatenate([_rows128(g) for g in small_g] + [loss_blk], axis=0)
    (g8,) = all_gather([packed], "ag_small_grads", deps)
    tot = sum_devices(g8, "sum_small_grads")
    loss = tot[packed.shape[0] - 8, 0]
    full_g = []
    lo = 0
    for g in small_g:
        rows = g.size // 128
        full_g.append(tot[lo:lo + rows].reshape(g.shape))
        lo += rows

    def mine(g, width):
        return lax.dynamic_slice_in_dim(g, dev * width, width, axis=g.ndim - 1)

    fg = full_g
    small_gl = [fg[0], fg[1], fg[2], mine(fg[3], d // N_DEV), mine(fg[4], cl), mine(fg[5], cl), mine(fg[6], cl),
                mine(fg[7], cl), mine(fg[8], cl), mine(fg[9], d // N_DEV)]
    sd, sm, sv = adamw_small(small_w, small_gl, small_m, small_v, "adamw_small")

    def like(k, a):
        return a[None] if k in (4, 5) else a

    sg = [like(k, a) for k, a in enumerate(small_gl)]
    sd = [like(k, a) for k, a in enumerate(sd)]
    sm = [like(k, a) for k, a in enumerate(sm)]
    sv = [like(k, a) for k, a in enumerate(sv)]
    return sg, sd, sm, sv, loss
```

```python
import functools
import math

import jax
import jax.numpy as jnp
from jax import lax
from jax.experimental import pallas as pl
from jax.experimental.pallas import tpu as pltpu

F32 = jnp.float32
BF16 = jnp.bfloat16
EPS = 1e-6
HEAD_DIM = 128
POOL_WINDOWS = (2, 4, 8, 16)
SCONV_K = 3
CONF_K = 31
HALO = 32
N_DEV = 8
VMEM_LIMIT = 56 * 1024 * 1024
MESH = pl.DeviceIdType.MESH

ADAM_LR = 0.001
ADAM_B1 = 0.9
ADAM_B2 = 0.999
ADAM_EPS = 1e-08
ADAM_WD = 0.01
ADAM_STEP = 10


def _params(*sem):
    return pltpu.CompilerParams(dimension_semantics=sem, vmem_limit_bytes=VMEM_LIMIT)


def _sigmoid(v):
    return 1.0 / (1.0 + jnp.exp(-v))


def _silu(v):
    return v * _sigmoid(v)


def _silu_and_grad(v):
    s = _sigmoid(v)
    return v * s, s * (1.0 + v * (1.0 - s))


def _rowsum8(v):
    r, c = v.shape
    return jnp.sum(v.reshape(r // 8, 8, c), axis=0)


SUBLANES = 8


class _Taps:
    def __init__(self, xx, rows, before):
        self.xx, self.rows, self.before, self.rotated = xx, rows, before, {}

    def __call__(self, i):
        r, q = i % SUBLANES, i // SUBLANES
        if r not in self.rotated:
            n = self.xx.shape[0]
            self.rotated[r] = self.xx if r == 0 else pltpu.roll(self.xx, r if self.before else n - r, 0)
        lo = HALO - SUBLANES * q if self.before else SUBLANES * q
        return self.rotated[r][lo:lo + self.rows]


def _window_sum(xx, win, before):
    n = xx.shape[0]
    acc = xx
    k = 1
    while k < win:
        acc = acc + pltpu.roll(acc, k if before else n - k, 0)
        k *= 2
    return acc


def postnorm_fwd(x, o, g, g_next, name, tm=256, dep=None):
    s, d = x.shape
    dep_args, dep_specs = _after(dep)

    def body(x_ref, o_ref, g_ref, gn_ref, *rest):
        y_ref, h_ref = rest[-2:]
        ov = o_ref[...].astype(F32)
        r = lax.rsqrt(jnp.mean(ov * ov, axis=-1, keepdims=True) + EPS)
        y = x_ref[...] + ov * r * g_ref[...]
        y_ref[...] = y
        r2 = lax.rsqrt(jnp.mean(y * y, axis=-1, keepdims=True) + EPS)
        h_ref[...] = (y * r2 * gn_ref[...]).astype(BF16)

    row = pl.BlockSpec((tm, d), lambda i: (i, 0))
    vec = pl.BlockSpec((1, d), lambda i: (0, 0))
    return pl.pallas_call(
        body, name=name, grid=(s // tm,),
        in_specs=[row, row, vec, vec] + dep_specs, out_specs=[row, row],
        out_shape=[jax.ShapeDtypeStruct((s, d), F32), jax.ShapeDtypeStruct((s, d), BF16)],
        compiler_params=_params("parallel"),
    )(x, o, g, g_next, *dep_args)


def final_fwd_bwd(x1, o, g, target, name, tm=256):
    s, d = x1.shape
    n = s // tm

    def body(x_ref, o_ref, g_ref, t_ref, loss_ref, gx_ref, do_ref, dg_ref, lacc, gacc):
        i = pl.program_id(0)

        @pl.when(i == 0)
        def _():
            lacc[...] = jnp.zeros_like(lacc)
            gacc[...] = jnp.zeros_like(gacc)

        ov = o_ref[...].astype(F32)
        gv = g_ref[...]
        r = lax.rsqrt(jnp.mean(ov * ov, axis=-1, keepdims=True) + EPS)
        oh = ov * r
        diff = x_ref[...] + oh * gv - t_ref[...]
        lacc[...] += _rowsum8(diff * diff)
        gx = diff * (1.0 / d)
        gx_ref[...] = gx
        gacc[...] += _rowsum8(gx * oh)
        dn = gx * gv
        do_ref[...] = (r * (dn - oh * jnp.mean(dn * oh, axis=-1, keepdims=True))).astype(BF16)

        @pl.when(i == n - 1)
        def _():
            tot = jnp.sum(jnp.sum(lacc[...], axis=0, keepdims=True), axis=1, keepdims=True)
            loss_ref[...] = jnp.broadcast_to(tot * (0.5 / d), loss_ref.shape)
            dg_ref[...] = jnp.sum(gacc[...], axis=0, keepdims=True)

    row = pl.BlockSpec((tm, d), lambda i: (i, 0))
    vec = pl.BlockSpec((1, d), lambda i: (0, 0))
    return pl.pallas_call(
        body, name=name, grid=(n,),
        in_specs=[row, row, vec, row],
        out_specs=[pl.BlockSpec((8, 128), lambda i: (0, 0)), row, row, vec],
        out_shape=[jax.ShapeDtypeStruct((8, 128), F32), jax.ShapeDtypeStruct((s, d), F32),
                   jax.ShapeDtypeStruct((s, d), BF16), jax.ShapeDtypeStruct((1, d), F32)],
        scratch_shapes=[pltpu.VMEM((8, d), F32), pltpu.VMEM((8, d), F32)],
        compiler_params=_params("arbitrary"),
    )(x1, o, g, target)


def _rms_bwd_rows(dyv, xv, gv):
    r = lax.rsqrt(jnp.mean(xv * xv, axis=-1, keepdims=True) + EPS)
    xh = xv * r
    dn = dyv * gv
    return r * (dn - xh * jnp.mean(dn * xh, axis=-1, keepdims=True)), _rowsum8(dyv * xh)


def norm_bwd(dy, inp, g, resid, name, inp2=None, g2=None, tm=256, dep=None):
    s, d = inp.shape
    n = s // tm
    chain = inp2 is not None

    def body(*refs):
        dy_ref, x_ref, g_ref, r_ref = refs[:4]
        outs = refs[-6:] if chain else refs[-3:]
        i = pl.program_id(0)

        @pl.when(i == 0)
        def _():
            for acc in outs[-2:] if chain else outs[-1:]:
                acc[...] = jnp.zeros_like(acc)

        if chain:
            x2_ref, g2_ref = refs[4:6]
            dx_ref, dg_ref, dx2_ref, dg2_ref, gacc, gacc2 = outs
        else:
            dx_ref, dg_ref, gacc = outs
        dx, dg_rows = _rms_bwd_rows(dy_ref[...].astype(F32), x_ref[...], g_ref[...])
        dx = dx + r_ref[...]
        dx_ref[...] = dx
        gacc[...] += dg_rows
        if chain:
            dx2, dg2_rows = _rms_bwd_rows(dx, x2_ref[...].astype(F32), g2_ref[...])
            dx2_ref[...] = dx2.astype(BF16)
            gacc2[...] += dg2_rows

        @pl.when(i == n - 1)
        def _():
            dg_ref[...] = jnp.sum(gacc[...], axis=0, keepdims=True)
            if chain:
                dg2_ref[...] = jnp.sum(gacc2[...], axis=0, keepdims=True)

    row = pl.BlockSpec((tm, d), lambda i: (i, 0))
    vec = pl.BlockSpec((1, d), lambda i: (0, 0))
    dep_args, dep_specs = _after(dep)
    extra = [inp2, g2] if chain else []
    return pl.pallas_call(
        body, name=name, grid=(n,),
        in_specs=[row, row, vec, row] + ([row, vec] if chain else []) + dep_specs,
        out_specs=[row, vec] * (2 if chain else 1),
        out_shape=[jax.ShapeDtypeStruct((s, d), F32), jax.ShapeDtypeStruct((1, d), F32)]
        + ([jax.ShapeDtypeStruct((s, d), BF16), jax.ShapeDtypeStruct((1, d), F32)] if chain else []),
        scratch_shapes=[pltpu.VMEM((8, d), F32)] * (2 if chain else 1),
        compiler_params=_params("arbitrary"),
    )(dy, inp, g, resid, *extra, *dep_args)


def _after(dep):
    if dep is None:
        return [], []
    return [dep], [pl.BlockSpec((8, 128), lambda *_: (0, 0))]


def _lane_concat(ref, count):
    return ref[0] if count == 1 else jnp.concatenate([ref[i] for i in range(count)], axis=1)


def mm_nn(a, w, out_dtype, name, tm=2048, tn=None, dep=None, group=1):
    m, k = a.shape
    tm = min(tm, m)
    ns, _, n = w.shape
    tn = n if tn is None else tn
    nj = n // tn
    assert group == 1 or nj == 1
    dep_args, dep_specs = _after(dep)

    def body(a_ref, w_ref, *rest):
        o_ref = rest[-1]
        o_ref[...] = jnp.dot(a_ref[...], _lane_concat(w_ref, group), preferred_element_type=F32).astype(out_dtype)

    return pl.pallas_call(
        body, name=name, grid=(ns // group, nj, m // tm),
        in_specs=[pl.BlockSpec((tm, k), lambda s, j, i: (i, 0)),
                  pl.BlockSpec((group, k, tn), lambda s, j, i: (s, 0, j))] + dep_specs,
        out_specs=pl.BlockSpec((tm, group * tn), lambda s, j, i: (i, s * nj + j)),
        out_shape=jax.ShapeDtypeStruct((m, ns * n), out_dtype),
        compiler_params=_params("parallel", "parallel", "parallel"),
    )(a, w, *dep_args)


def mm_nt(a, w, out_dtype, name, tm=1024, tn=None, dep=None, group=1):
    m = a.shape[0]
    tm = min(tm, m)
    ns, k, n = w.shape
    tn = n if tn is None else tn
    nj = n // tn
    assert group == 1 or nj == 1
    steps = ns * nj // group
    dep_args, dep_specs = _after(dep)

    def body(a_ref, w_ref, *rest):
        o_ref, acc = rest[-2:]
        r = pl.program_id(1)

        @pl.when(r == 0)
        def _():
            acc[...] = jnp.zeros_like(acc)

        acc[...] += lax.dot_general(a_ref[...], _lane_concat(w_ref, group), (((1,), (1,)), ((), ())),
                                    preferred_element_type=F32)

        @pl.when(r == steps - 1)
        def _():
            o_ref[...] = acc[...].astype(out_dtype)

    return pl.pallas_call(
        body, name=name, grid=(m // tm, steps),
        in_specs=[pl.BlockSpec((tm, group * tn), lambda i, r: (i, r)),
                  pl.BlockSpec((group, k, tn), lambda i, r: (r // nj, 0, r % nj))] + dep_specs,
        out_specs=pl.BlockSpec((tm, k), lambda i, r: (i, 0)),
        out_shape=jax.ShapeDtypeStruct((m, k), out_dtype),
        scratch_shapes=[pltpu.VMEM((tm, k), F32)],
        compiler_params=_params("parallel", "arbitrary"),
    )(a, w, *dep_args)


def mm_tn(a, b, ns, out_dtype, name, tk=1024, tm=2048, dep=None, pick=None, group=1):
    m, k = a.shape
    tm = min(tm, m)
    step, offset = (1, None) if pick is None else pick
    assert group == 1 or pick is None
    n = b.shape[1] // (ns * step)
    steps = m // tm
    dep_args, dep_specs = _after(dep)
    n_pre = 0 if pick is None else 1

    def b_block(s, j, r, *pre):
        return (r, s if pick is None else step * s + pre[0][0])

    def body(*refs):
        a_ref, b_ref = refs[n_pre:n_pre + 2]
        o_ref, acc = refs[-2:]
        r = pl.program_id(2)

        @pl.when(r == 0)
        def _():
            acc[...] = jnp.zeros_like(acc)

        acc[...] += lax.dot_general(a_ref[...], b_ref[...], (((0,), (0,)), ((), ())),
                                    preferred_element_type=F32)

        @pl.when(r == steps - 1)
        def _():
            for i in range(group):
                o_ref[i] = acc[:, i * n:(i + 1) * n].astype(out_dtype)

    return pl.pallas_call(
        body, name=name,
        grid_spec=pltpu.PrefetchScalarGridSpec(
            num_scalar_prefetch=n_pre, grid=(ns // group, k // tk, steps),
            in_specs=[pl.BlockSpec((tm, tk), lambda s, j, r, *pre: (r, j)),
                      pl.BlockSpec((tm, group * n), b_block)] + dep_specs,
            out_specs=pl.BlockSpec((group, tk, n), lambda s, j, r, *pre: (s, j, 0)),
            scratch_shapes=[pltpu.VMEM((tk, group * n), F32)]),
        out_shape=jax.ShapeDtypeStruct((ns, k, n), out_dtype),
        compiler_params=_params("parallel", "parallel", "arbitrary"),
    )(*([] if pick is None else [offset]), a, b, *dep_args)


SB_BLK = 128


LOG2E = 1.0 / math.log(2.0)


def _split_dot(v, tri2):
    hi = pltpu.bitcast(pltpu.bitcast(v, jnp.uint32) & jnp.uint32(0xFFFF0000), F32)
    lo = (v - hi).astype(BF16)
    return jnp.dot(jnp.concatenate([hi.astype(BF16), lo], axis=1), tri2, preferred_element_type=F32)


def _sb_scores(z2, lim, dcol, tri_ex, masked):
    sp = jnp.log2(1.0 + jnp.exp2(-jnp.abs(z2)))
    lb = jnp.minimum(z2, 0.0) - sp
    l1m = lb - z2
    mask = None
    if masked:
        mask = dcol < lim
        l1m = jnp.where(mask, l1m, 0.0)
    return mask, lb, l1m, _split_dot(l1m, tri_ex)


def _sb_consts():
    row = lax.broadcasted_iota(jnp.int32, (SB_BLK, SB_BLK), 0)
    col = lax.broadcasted_iota(jnp.int32, (SB_BLK, SB_BLK), 1)
    tri_ex = jnp.where(row > col, 1.0, 0.0).astype(BF16)
    tri_in = jnp.where(row >= col, 1.0, 0.0).astype(BF16)
    return col - row, jnp.concatenate([tri_ex, tri_ex], axis=0), jnp.concatenate([tri_in, tri_in], axis=0)


def sb_fwd(p, n_heads, name, tq=512, nsub=8, dep=None):
    s = p.shape[0]
    h_n = n_heads
    b = SB_BLK
    nqs = tq // b
    tk = nsub * b
    scale = 1.0 / math.sqrt(HEAD_DIM)

    dep_args, dep_specs = _after(dep)

    def body(q_ref, k_ref, v_ref, *rest):
        o_ref, w_ref = rest[-2:]
        qi = pl.program_id(1)
        dcol, tri_ex, _ = _sb_consts()
        qv = [q_ref[qs * b:(qs + 1) * b, :] for qs in range(nqs)]
        n_groups = ((qi + 1) * nqs - 1) // nsub + 1

        def step(it, carry, masked):
            c1s, accs = carry
            g = n_groups - 1 - it
            off = pl.multiple_of(g * tk, tk)
            kg = k_ref[pl.ds(off, tk), :]
            vg = v_ref[pl.ds(off, tk), :]
            new_c1, new_acc = [], []
            for qs in range(nqs):
                qb = qi * nqs + qs
                z2 = lax.dot_general(qv[qs], kg, (((1,), (1,)), ((), ())),
                                     preferred_element_type=F32) * (scale * LOG2E)
                blocks = [_sb_scores(z2[:, j * b:(j + 1) * b], (qb - (g * nsub + j)) * b, dcol, tri_ex, masked)
                          for j in range(nsub)]
                run = c1s[qs]
                ws = [None] * nsub
                for j in reversed(range(nsub)):
                    mask, lb, l1m, ls_loc = blocks[j]
                    wj = jnp.exp2(lb + ls_loc + run)
                    ws[j] = (jnp.where(mask, wj, 0.0) if masked else wj).astype(BF16)
                    run = run + jnp.sum(l1m, axis=1, keepdims=True)
                w = jnp.concatenate(ws, axis=1)
                w_ref[0, g, qs * b:(qs + 1) * b, :] = w
                new_acc.append(accs[qs] + jnp.dot(w, vg, preferred_element_type=F32))
                new_c1.append(run)
            return tuple(new_c1), tuple(new_acc)

        init = (tuple(jnp.zeros((b, 1), F32) for _ in range(nqs)),
                tuple(jnp.zeros((b, HEAD_DIM), F32) for _ in range(nqs)))
        assert all(((i + 1) * nqs - 1) // nsub * nsub <= i * nqs for i in range(s // tq))
        first = step(0, init, True)
        _, accs = lax.fori_loop(1, n_groups, functools.partial(step, masked=False), first)
        for qs in range(nqs):
            o_ref[qs * b:(qs + 1) * b, :] = accs[qs]

    return pl.pallas_call(
        body, name=name, grid=(h_n, s // tq),
        in_specs=[pl.BlockSpec((tq, HEAD_DIM), lambda h, i: (i, h)),
                  pl.BlockSpec((s, HEAD_DIM), lambda h, i: (0, h_n + h)),
                  pl.BlockSpec((s, HEAD_DIM), lambda h, i: (0, 2 * h_n + h))] + dep_specs,
        out_specs=[pl.BlockSpec((tq, HEAD_DIM), lambda h, i: (i, h)),
                   pl.BlockSpec((1, s // tk, tq, tk), lambda h, i: (h, 0, i, 0))],
        out_shape=[jax.ShapeDtypeStruct((s, h_n * HEAD_DIM), F32),
                   jax.ShapeDtypeStruct((h_n, s // tk, s, tk), BF16)],
        compiler_params=_params("parallel", "arbitrary"),
    )(p, p, p, *dep_args)


def sb_bwd(p, a, wts, da, n_heads, name, tq=512, dep=None):
    s = p.shape[0]
    h_n = n_heads
    nq = s // tq
    b = SB_BLK
    nqs = tq // b
    tk = wts.shape[3]
    nsub = tk // b
    scale = 1.0 / math.sqrt(HEAD_DIM)
    dep_args, dep_specs = _after(dep)

    def body(q_ref, k_ref, v_ref, a_ref, da_ref, w_ref, *rest):
        dq_ref, dk_ref, dv_ref, dk_acc, dv_acc = rest[-5:]
        qi = pl.program_id(1)

        @pl.when(qi == 0)
        def _():
            dk_acc[...] = jnp.zeros_like(dk_acc)
            dv_acc[...] = jnp.zeros_like(dv_acc)

        dcol, _, tri_in = _sb_consts()
        q_all = q_ref[...]
        do_all = da_ref[...]
        qv = [q_ref[qs * b:(qs + 1) * b, :] for qs in range(nqs)]
        dov = [da_ref[qs * b:(qs + 1) * b, :] for qs in range(nqs)]
        tots = [jnp.sum(dov[qs].astype(F32) * a_ref[qs * b:(qs + 1) * b, :], axis=1, keepdims=True)
                for qs in range(nqs)]
        n_groups = ((qi + 1) * nqs - 1) // nsub + 1

        def step(it, carry, masked):
            c2s, dqs = carry
            g = n_groups - 1 - it
            off = pl.multiple_of(g * tk, tk)
            kg = k_ref[pl.ds(off, tk), :]
            vg = v_ref[pl.ds(off, tk), :]
            w_all = w_ref[0, g]
            new_c2, new_dq, dz_rows = [], [], []
            for qs in range(nqs):
                qb = qi * nqs + qs
                z2 = lax.dot_general(qv[qs], kg, (((1,), (1,)), ((), ())),
                                     preferred_element_type=F32) * (-scale * LOG2E)
                dw = lax.dot_general(dov[qs], vg, (((1,), (1,)), ((), ())), preferred_element_type=F32)
                beta = 1.0 / (1.0 + jnp.exp2(z2))
                e = dw * w_all[qs * b:(qs + 1) * b, :].astype(F32)
                run2 = c2s[qs]
                dzs = [None] * nsub
                for j in reversed(range(nsub)):
                    cols = slice(j * b, (j + 1) * b)
                    later = _split_dot(e[:, cols], tri_in) + run2
                    bj = beta[:, cols]
                    dz = (e[:, cols] * (1.0 - bj) - bj * (tots[qs] - later)) * scale
                    if masked:
                        dz = jnp.where(dcol < (qb - (g * nsub + j)) * b, dz, 0.0)
                    dzs[j] = dz.astype(BF16)
                    run2 = run2 + jnp.sum(e[:, cols], axis=1, keepdims=True)
                dzq = jnp.concatenate(dzs, axis=1)
                new_dq.append(dqs[qs] + jnp.dot(dzq, kg, preferred_element_type=F32))
                new_c2.append(run2)
                dz_rows.append(dzq)
            dz_all = jnp.concatenate(dz_rows, axis=0)
            dk_acc[pl.ds(off, tk), :] += lax.dot_general(dz_all, q_all, (((0,), (0,)), ((), ())),
                                                         preferred_element_type=F32)
            dv_acc[pl.ds(off, tk), :] += lax.dot_general(w_all, do_all, (((0,), (0,)), ((), ())),
                                                         preferred_element_type=F32)
            return tuple(new_c2), tuple(new_dq)

        zeros = tuple(jnp.zeros((b, 1), F32) for _ in range(nqs))
        assert all(((i + 1) * nqs - 1) // nsub * nsub <= i * nqs for i in range(s // tq))
        first = step(0, (zeros, tuple(jnp.zeros((b, HEAD_DIM), F32) for _ in range(nqs))), True)
        _, dqs = lax.fori_loop(1, n_groups, functools.partial(step, masked=False), first)
        for qs in range(nqs):
            dq_ref[qs * b:(qs + 1) * b, :] = dqs[qs].astype(BF16)

        @pl.when(qi == nq - 1)
        def _():
            dk_ref[...] = dk_acc[...].astype(BF16)
            dv_ref[...] = dv_acc[...].astype(BF16)

    blk = pl.BlockSpec((tq, HEAD_DIM), lambda h, i: (i, h))
    full = pl.BlockSpec((s, HEAD_DIM), lambda h, i: (0, h))
    return pl.pallas_call(
        body, name=name, grid=(h_n, nq),
        in_specs=[blk, pl.BlockSpec((s, HEAD_DIM), lambda h, i: (0, h_n + h)),
                  pl.BlockSpec((s, HEAD_DIM), lambda h, i: (0, 2 * h_n + h)), blk, blk,
                  pl.BlockSpec((1, s // tk, tq, tk), lambda h, i: (h, 0, i, 0))] + dep_specs,
        out_specs=[blk, full, full],
        out_shape=[jax.ShapeDtypeStruct((s, h_n * HEAD_DIM), BF16)] * 3,
        scratch_shapes=[pltpu.VMEM((s, HEAD_DIM), F32), pltpu.VMEM((s, HEAD_DIM), F32)],
        compiler_params=_params("parallel", "arbitrary"),
    )(p, p, p, a, da, wts, *dep_args)


def _pool_window(xx, win, r0, rc):
    cur = xx[HALO:HALO + rc]
    ws = _window_sum(xx, win, True)[HALO:HALO + rc]
    t_idx = r0 + lax.broadcasted_iota(jnp.int32, (rc, 1), 0)
    inv = 1.0 / jnp.minimum(win, t_idx + 1).astype(F32)
    return ws * inv - cur, inv


def even_mix_fwd(a, p, pool_w, pool_scale, name, rc=64, dep=None):
    s = p.shape[0]
    ng = len(POOL_WINDOWS)
    cw = pool_w.shape[1]
    n_chunks = s // rc
    dep_args, dep_specs = _after(dep)

    def body(a_ref, u_ref, g_ref, w_ref, sc_ref, *rest):
        y_ref, upad = rest[-2:]
        j = pl.program_id(0)

        @pl.when(j < ng)
        def _():
            def chunk(ci, carry):
                rows = pl.ds(pl.multiple_of(ci * rc, rc), rc)
                y_ref[rows, :] = (a_ref[rows, :] * _silu(g_ref[rows, :].astype(F32))).astype(BF16)
                return carry

            lax.fori_loop(0, n_chunks, chunk, 0)

        for gi, win in enumerate(POOL_WINDOWS):
            @pl.when(j == ng + gi)
            def _(win=win):
                upad[0:HALO, :] = jnp.zeros((HALO, cw), F32)

                def fill(ci, carry):
                    r0 = pl.multiple_of(ci * rc, rc)
                    upad[pl.ds(pl.multiple_of(r0 + HALO, HALO), rc), :] = u_ref[pl.ds(r0, rc), :].astype(F32)
                    return carry

                lax.fori_loop(0, n_chunks, fill, 0)

                def chunk(ci, carry):
                    r0 = pl.multiple_of(ci * rc, rc)
                    rows = pl.ds(r0, rc)
                    pooled, _ = _pool_window(upad[pl.ds(r0, HALO + rc), :], win, r0, rc)
                    t = jnp.dot(pooled.astype(BF16), w_ref[0], preferred_element_type=F32)
                    y_ref[rows, :] = (t * sc_ref[...] * _silu(g_ref[rows, :].astype(F32))).astype(BF16)
                    return carry

                lax.fori_loop(0, n_chunks, chunk, 0)

    grp = lambda j: jnp.maximum(j - ng, 0)
    return pl.pallas_call(
        body, name=name, grid=(2 * ng,),
        in_specs=[pl.BlockSpec((s, cw), lambda j: (0, jnp.minimum(j, ng - 1))),
                  pl.BlockSpec((s, cw), lambda j: (0, 3 * ng + grp(j))),
                  pl.BlockSpec((s, cw), lambda j: (0, 4 * ng + j)),
                  pl.BlockSpec((1, cw, cw), lambda j: (grp(j), 0, 0)),
                  pl.BlockSpec((1, cw), lambda j: (0, grp(j)))] + dep_specs,
        out_specs=pl.BlockSpec((s, cw), lambda j: (0, j)),
        out_shape=jax.ShapeDtypeStruct((s, 2 * ng * cw), BF16),
        scratch_shapes=[pltpu.VMEM((HALO + s, cw), F32)],
        compiler_params=_params("arbitrary"),
    )(a, p, p, pool_w, pool_scale, *dep_args)


def even_mix_bwd(dy, a, p, pool_w, pool_scale, name, rc=64):
    s = p.shape[0]
    ng = len(POOL_WINDOWS)
    cw = pool_w.shape[1]
    n_chunks = s // rc

    def body(dy_ref, a_ref, u_ref, g_ref, w_ref, sc_ref, da_ref, du_ref, dg_ref, dw_ref, dsc_ref,
             upad, rpad, dpl, dw_acc, dsc_acc):
        j = pl.program_id(0)

        @pl.when(j < ng)
        def _():
            def chunk(ci, carry):
                rows = pl.ds(pl.multiple_of(ci * rc, rc), rc)
                dyv = dy_ref[rows, :].astype(F32)
                sg, dsg = _silu_and_grad(g_ref[rows, :].astype(F32))
                da_ref[rows, :] = (dyv * sg).astype(BF16)
                dg_ref[rows, :] = (dyv * a_ref[rows, :] * dsg).astype(BF16)
                return carry

            lax.fori_loop(0, n_chunks, chunk, 0)

        for gi, win in enumerate(POOL_WINDOWS):
            @pl.when(j == ng + gi)
            def _(win=win):
                upad[0:HALO, :] = jnp.zeros((HALO, cw), F32)
                rpad[s:s + HALO, :] = jnp.zeros((HALO, cw), F32)
                dw_acc[...] = jnp.zeros_like(dw_acc)
                dsc_acc[...] = jnp.zeros_like(dsc_acc)

                def fill(ci, carry):
                    r0 = pl.multiple_of(ci * rc, rc)
                    upad[pl.ds(pl.multiple_of(r0 + HALO, HALO), rc), :] = u_ref[pl.ds(r0, rc), :].astype(F32)
                    return carry

                lax.fori_loop(0, n_chunks, fill, 0)

                def chunk(ci, carry):
                    r0 = pl.multiple_of(ci * rc, rc)
                    rows = pl.ds(r0, rc)
                    pooled, inv = _pool_window(upad[pl.ds(r0, HALO + rc), :], win, r0, rc)
                    pb = pooled.astype(BF16)
                    wv = w_ref[0]
                    t = jnp.dot(pb, wv, preferred_element_type=F32)
                    scv = sc_ref[...]
                    dyv = dy_ref[rows, :].astype(F32)
                    sg, dsg = _silu_and_grad(g_ref[rows, :].astype(F32))
                    dpo = dyv * sg
                    dg_ref[rows, :] = (dyv * t * scv * dsg).astype(BF16)
                    dsc_acc[...] += _rowsum8(dpo * t)
                    dtb = (dpo * scv).astype(BF16)
                    dw_acc[...] += lax.dot_general(pb, dtb, (((0,), (0,)), ((), ())),
                                                   preferred_element_type=F32)
                    dpooled = lax.dot_general(dtb, wv, (((1,), (1,)), ((), ())),
                                              preferred_element_type=F32)
                    dpl[rows, :] = dpooled
                    rpad[rows, :] = dpooled * inv
                    return carry

                lax.fori_loop(0, n_chunks, chunk, 0)

                def chunk2(ci, carry):
                    r0 = pl.multiple_of(ci * rc, rc)
                    rows = pl.ds(r0, rc)
                    xx = rpad[pl.ds(r0, rc + HALO), :]
                    fs = _window_sum(xx, win, False)[0:rc]
                    du_ref[rows, :] = (fs - dpl[rows, :]).astype(BF16)
                    return carry

                lax.fori_loop(0, n_chunks, chunk2, 0)
                dw_ref[0] = dw_acc[...]
                dsc_ref[...] = jnp.sum(dsc_acc[...], axis=0, keepdims=True)

    grp = lambda j: jnp.maximum(j - ng, 0)
    att = lambda j: jnp.minimum(j, ng - 1)
    return pl.pallas_call(
        body, name=name, grid=(2 * ng,),
        in_specs=[pl.BlockSpec((s, cw), lambda j: (0, j)),
                  pl.BlockSpec((s, cw), lambda j: (0, att(j))),
                  pl.BlockSpec((s, cw), lambda j: (0, 3 * ng + grp(j))),
                  pl.BlockSpec((s, cw), lambda j: (0, 4 * ng + j)),
                  pl.BlockSpec((1, cw, cw), lambda j: (grp(j), 0, 0)),
                  pl.BlockSpec((1, cw), lambda j: (0, grp(j)))],
        out_specs=[pl.BlockSpec((s, cw), lambda j: (0, att(j))),
                   pl.BlockSpec((s, cw), lambda j: (0, grp(j))),
                   pl.BlockSpec((s, cw), lambda j: (0, j)),
                   pl.BlockSpec((1, cw, cw), lambda j: (grp(j), 0, 0)),
                   pl.BlockSpec((1, cw), lambda j: (0, grp(j)))],
        out_shape=[jax.ShapeDtypeStruct((s, ng * cw), BF16), jax.ShapeDtypeStruct((s, ng * cw), BF16),
                   jax.ShapeDtypeStruct((s, 2 * ng * cw), BF16),
                   jax.ShapeDtypeStruct((ng, cw, cw), F32), jax.ShapeDtypeStruct((1, ng * cw), F32)],
        scratch_shapes=[pltpu.VMEM((HALO + s, cw), F32), pltpu.VMEM((s + HALO, cw), F32),
                        pltpu.VMEM((s, cw), F32), pltpu.VMEM((cw, cw), F32), pltpu.VMEM((8, cw), F32)],
        compiler_params=_params("arbitrary"),
    )(dy, a, p, p, pool_w, pool_scale)


def _halo_before(tm):
    return lambda i: jnp.maximum(i * (tm // HALO) - 1, 0)


def _halo_after(tm, s):
    return lambda i: jnp.minimum((i + 1) * (tm // HALO), s // HALO - 1)


def odd_mix_fwd(p, sconv_w, dconv_w, dconv_b, cnorm_g, cnorm_b, name, tm=128, dep=None):
    s = p.shape[0]
    cw = sconv_w.shape[1]
    n = s // tm
    lanes = 128
    hb = _halo_before(tm)

    dep_args, dep_specs = _after(dep)

    def body(hc_ref, hch_ref, bc_ref, cc_ref, cch_ref, ga_ref, gah_ref, gb_ref, gbh_ref, g1_ref, g2_ref,
             sw_ref, dw_ref, db_ref, gam_ref, bet_ref, *rest):
        y_ref, dc_ref = rest[-2:]
        first = pl.program_id(0) == 0
        for l in range(cw // lanes):
            cols = slice(l * lanes, (l + 1) * lanes)
            mh = jnp.where(first, 0.0, cch_ref[:, cols].astype(F32) * hch_ref[:, cols].astype(F32))
            mm = cc_ref[:, cols].astype(F32) * hc_ref[:, cols].astype(F32)
            xx = jnp.concatenate([mh, mm], axis=0)
            tap = _Taps(xx, tm, True)
            cv = jnp.zeros((tm, lanes), F32)
            for k in range(SCONV_K):
                cv = cv + sw_ref[k:k + 1, cols] * tap(SCONV_K - 1 - k)
            c_out = bc_ref[:, cols].astype(F32) * cv
            y_ref[:, cols] = (c_out * _silu(g1_ref[:, cols].astype(F32))).astype(BF16)
            dh = jnp.where(first, 0.0, gah_ref[:, cols].astype(F32) * _sigmoid(gbh_ref[:, cols].astype(F32)))
            dm = ga_ref[:, cols].astype(F32) * _sigmoid(gb_ref[:, cols].astype(F32))
            xx = jnp.concatenate([dh, dm], axis=0)
            tap = _Taps(xx, tm, True)
            acc = jnp.zeros((tm, lanes), F32) + db_ref[:, cols]
            for k in range(CONF_K):
                acc = acc + dw_ref[k:k + 1, cols] * tap(CONF_K - 1 - k)
            dc_ref[:, cols] = acc
        rs = 32
        for r in range(tm // rs):
            rows = slice(r * rs, (r + 1) * rs)
            xv = dc_ref[rows, :]
            mu = jnp.mean(xv, axis=-1, keepdims=True)
            xc = xv - mu
            rstd = lax.rsqrt(jnp.mean(xc * xc, axis=-1, keepdims=True) + EPS)
            ln = xc * rstd * gam_ref[...] + bet_ref[...]
            y_ref[rows, cw:2 * cw] = (_silu(ln) * _silu(g2_ref[rows, :].astype(F32))).astype(BF16)

    main = lambda c: pl.BlockSpec((tm, cw), lambda i: (i, c))
    halo = lambda c: pl.BlockSpec((HALO, cw), lambda i: (hb(i), c))
    vec = lambda r: pl.BlockSpec((r, cw), lambda i: (0, 0))
    return pl.pallas_call(
        body, name=name, grid=(n,),
        in_specs=[main(0), halo(0), main(1), main(2), halo(2), main(3), halo(3), main(4), halo(4),
                  main(5), main(6), vec(SCONV_K), vec(CONF_K), vec(1), vec(1), vec(1)] + dep_specs,
        out_specs=[pl.BlockSpec((tm, 2 * cw), lambda i: (i, 0)), pl.BlockSpec((tm, cw), lambda i: (i, 0))],
        out_shape=[jax.ShapeDtypeStruct((s, 2 * cw), BF16), jax.ShapeDtypeStruct((s, cw), F32)],
        compiler_params=_params("parallel"),
    )(p, p, p, p, p, p, p, p, p, p, p, sconv_w, dconv_w, dconv_b, cnorm_g, cnorm_b, *dep_args)


def odd_bwd_ln(dy, p, dc, cnorm_g, cnorm_b, name, tm=256):
    s = p.shape[0]
    cw = dc.shape[1]
    n = s // tm
    rs = 32

    def body(dy_ref, g2_ref, dc_ref, gam_ref, bet_ref, ddc_ref, dg_ref, dgam_ref, dbet_ref, gacc, bacc):
        i = pl.program_id(0)

        @pl.when(i == 0)
        def _():
            gacc[...] = jnp.zeros_like(gacc)
            bacc[...] = jnp.zeros_like(bacc)

        def chunk(ci, carry):
            rows = pl.ds(pl.multiple_of(ci * rs, rs), rs)
            xv = dc_ref[rows, :]
            mu = jnp.mean(xv, axis=-1, keepdims=True)
            xc = xv - mu
            rstd = lax.rsqrt(jnp.mean(xc * xc, axis=-1, keepdims=True) + EPS)
            xh = xc * rstd
            gam = gam_ref[...]
            sl, dsl = _silu_and_grad(xh * gam + bet_ref[...])
            sg, dsg = _silu_and_grad(g2_ref[rows, :].astype(F32))
            dyv = dy_ref[rows, :].astype(F32)
            dg_ref[rows, :] = (dyv * sl * dsg).astype(BF16)
            dln = dyv * sg * dsl
            gacc[...] += _rowsum8(dln * xh)
            bacc[...] += _rowsum8(dln)
            dxh = dln * gam
            ddc_ref[rows, :] = rstd * (dxh - jnp.mean(dxh, axis=-1, keepdims=True)
                                       - xh * jnp.mean(dxh * xh, axis=-1, keepdims=True))
            return carry

        lax.fori_loop(0, tm // rs, chunk, 0)

        @pl.when(i == n - 1)
        def _():
            dgam_ref[...] = jnp.sum(gacc[...], axis=0, keepdims=True)
            dbet_ref[...] = jnp.sum(bacc[...], axis=0, keepdims=True)

    vec = pl.BlockSpec((1, cw), lambda i: (0, 0))
    return pl.pallas_call(
        body, name=name, grid=(n,),
        in_specs=[pl.BlockSpec((tm, cw), lambda i: (i, 1)), pl.BlockSpec((tm, cw), lambda i: (i, 6)),
                  pl.BlockSpec((tm, cw), lambda i: (i, 0)), vec, vec],
        out_specs=[pl.BlockSpec((tm, cw), lambda i: (i, 0)), pl.BlockSpec((tm, cw), lambda i: (i, 0)), vec, vec],
        out_shape=[jax.ShapeDtypeStruct((s, cw), F32), jax.ShapeDtypeStruct((s, cw), BF16),
                   jax.ShapeDtypeStruct((1, cw), F32), jax.ShapeDtypeStruct((1, cw), F32)],
        scratch_shapes=[pltpu.VMEM((8, cw), F32), pltpu.VMEM((8, cw), F32)],
        compiler_params=_params("arbitrary"),
    )(dy, p, dc, cnorm_g, cnorm_b)


def odd_bwd_conv(dy, p, ddc, dg2, sconv_w, dconv_w, name, tm=128):
    s = p.shape[0]
    cw = ddc.shape[1]
    n = s // tm
    lanes = 128
    hb = _halo_before(tm)
    ha = _halo_after(tm, s)

    def body(dy_ref, dya_ref, g1_ref, g1a_ref, bc_ref, bca_ref, hc_ref, hch_ref, cc_ref, cch_ref,
             ddc_ref, ddca_ref, ga_ref, gah_ref, gb_ref, gbh_ref, dg2_ref, sw_ref, dw_ref,
             dp_ref, dsw_ref, ddw_ref, ddb_ref, sw_acc, dw_acc, db_acc):
        i = pl.program_id(0)
        first = i == 0
        last = i == n - 1

        @pl.when(first)
        def _():
            sw_acc[...] = jnp.zeros_like(sw_acc)
            dw_acc[...] = jnp.zeros_like(dw_acc)
            db_acc[...] = jnp.zeros_like(db_acc)

        for l in range(cw // lanes):
            cols = slice(l * lanes, (l + 1) * lanes)
            mh = jnp.where(first, 0.0, cch_ref[:, cols].astype(F32) * hch_ref[:, cols].astype(F32))
            hcv = hc_ref[:, cols].astype(F32)
            ccv = cc_ref[:, cols].astype(F32)
            xx = jnp.concatenate([mh, ccv * hcv], axis=0)
            tap = _Taps(xx, tm, True)
            taps = [tap(SCONV_K - 1 - k) for k in range(SCONV_K)]
            cv = jnp.zeros((tm, lanes), F32)
            for k in range(SCONV_K):
                cv = cv + sw_ref[k:k + 1, cols] * taps[k]
            bcv = bc_ref[:, cols].astype(F32)
            dyv = dy_ref[:, cols].astype(F32)
            sg, dsg = _silu_and_grad(g1_ref[:, cols].astype(F32))
            dco = dyv * sg
            dp_ref[:, 5 * cw + l * lanes:5 * cw + (l + 1) * lanes] = (dyv * bcv * cv * dsg).astype(BF16)
            dp_ref[:, cw + l * lanes:cw + (l + 1) * lanes] = (dco * cv).astype(BF16)
            dcv = dco * bcv
            for k in range(SCONV_K):
                sw_acc[k * 8:(k + 1) * 8, cols] += _rowsum8(dcv * taps[k])
            dcv_a = jnp.where(last, 0.0, dya_ref[:, cols].astype(F32) * _silu(g1a_ref[:, cols].astype(F32))
                              * bca_ref[:, cols].astype(F32))
            xx = jnp.concatenate([dcv, dcv_a], axis=0)
            tap = _Taps(xx, tm, False)
            dm = jnp.zeros((tm, lanes), F32)
            for k in range(SCONV_K):
                dm = dm + sw_ref[k:k + 1, cols] * tap(SCONV_K - 1 - k)
            dp_ref[:, l * lanes:(l + 1) * lanes] = (dm * ccv).astype(BF16)
            dp_ref[:, 2 * cw + l * lanes:2 * cw + (l + 1) * lanes] = (dm * hcv).astype(BF16)
            gav = ga_ref[:, cols].astype(F32)
            sb = _sigmoid(gb_ref[:, cols].astype(F32))
            dh = jnp.where(first, 0.0, gah_ref[:, cols].astype(F32) * _sigmoid(gbh_ref[:, cols].astype(F32)))
            xx = jnp.concatenate([dh, gav * sb], axis=0)
            ddcv = ddc_ref[:, cols]
            db_acc[:, cols] += _rowsum8(ddcv)
            tap = _Taps(xx, tm, True)
            for k in range(CONF_K):
                dw_acc[k * 8:(k + 1) * 8, cols] += _rowsum8(ddcv * tap(CONF_K - 1 - k))
            ddc_a = jnp.where(last, 0.0, ddca_ref[:, cols])
            xx = jnp.concatenate([ddcv, ddc_a], axis=0)
            tap = _Taps(xx, tm, False)
            dgl = jnp.zeros((tm, lanes), F32)
            for k in range(CONF_K):
                dgl = dgl + dw_ref[k:k + 1, cols] * tap(CONF_K - 1 - k)
            dp_ref[:, 3 * cw + l * lanes:3 * cw + (l + 1) * lanes] = (dgl * sb).astype(BF16)
            dp_ref[:, 4 * cw + l * lanes:4 * cw + (l + 1) * lanes] = (dgl * gav * sb * (1.0 - sb)).astype(BF16)
        dp_ref[:, 6 * cw:7 * cw] = dg2_ref[...]

        @pl.when(last)
        def _():
            for k in range(SCONV_K):
                dsw_ref[k:k + 1, :] = jnp.sum(sw_acc[k * 8:(k + 1) * 8, :], axis=0, keepdims=True)
            for k in range(CONF_K):
                ddw_ref[k:k + 1, :] = jnp.sum(dw_acc[k * 8:(k + 1) * 8, :], axis=0, keepdims=True)
            ddb_ref[...] = jnp.sum(db_acc[...], axis=0, keepdims=True)

    def main(c):
        return pl.BlockSpec((tm, cw), lambda i: (i, c))

    def before(c):
        return pl.BlockSpec((HALO, cw), lambda i: (hb(i), c))

    def after(c):
        return pl.BlockSpec((HALO, cw), lambda i: (ha(i), c))

    def vec(r):
        return pl.BlockSpec((r, cw), lambda i: (0, 0))

    return pl.pallas_call(
        body, name=name, grid=(n,),
        in_specs=[main(0), after(0), main(5), after(5), main(1), after(1), main(0), before(0), main(2), before(2),
                  main(0), after(0), main(3), before(3), main(4), before(4), main(0), vec(SCONV_K), vec(CONF_K)],
        out_specs=[pl.BlockSpec((tm, 7 * cw), lambda i: (i, 0)), vec(SCONV_K), vec(CONF_K), vec(1)],
        out_shape=[jax.ShapeDtypeStruct((s, 7 * cw), BF16), jax.ShapeDtypeStruct((SCONV_K, cw), F32),
                   jax.ShapeDtypeStruct((CONF_K, cw), F32), jax.ShapeDtypeStruct((1, cw), F32)],
        scratch_shapes=[pltpu.VMEM((8 * SCONV_K, cw), F32), pltpu.VMEM((8 * CONF_K, cw), F32),
                        pltpu.VMEM((8, cw), F32)],
        compiler_params=_params("arbitrary"),
    )(dy, dy, p, p, p, p, p, p, p, p, ddc, ddc, p, p, p, p, dg2, sconv_w, dconv_w)


_ANY = pl.BlockSpec(memory_space=pl.ANY)


def _place():
    return lax.axis_index("x"), lax.axis_index("y"), lax.axis_index("c")


def all_gather(arrs, name, deps=()):
    n = len(arrs)

    def body(*refs):
        ins, outs = refs[:n], refs[n + len(deps):2 * n + len(deps)]
        send_sems, recv_sems, local_sems = refs[-3:]
        x, y, c = _place()
        me, sibling = (x, y, c), (x, y, 1 - c)
        chips = [(1 - x, y), (x, 1 - y), (1 - x, 1 - y)]

        def copy(a, k, block, to, src=None):
            px, py, pc = block
            dst = outs[a].at[4 * px + 2 * py + pc]
            return pltpu.make_async_remote_copy(
                src_ref=dst if src is None else src, dst_ref=dst,
                send_sem=send_sems.at[7 * a + k], recv_sem=recv_sems.at[7 * a + k],
                device_id=to, device_id_type=MESH)

        mine = [pltpu.make_async_copy(ins[a], outs[a].at[4 * x + 2 * y + c], local_sems.at[a]) for a in range(n)]
        first = []
        for a in range(n):
            first.append(copy(a, 0, me, sibling, src=ins[a]))
            first += [copy(a, 1 + j, me, (*chip, c), src=ins[a]) for j, chip in enumerate(chips)]
        for cp in first + mine:
            cp.start()
        passed = []
        for a in range(n):
            for j, chip in enumerate(chips):
                copy(a, 1 + j, (*chip, c), me).wait_recv()
                cp = copy(a, 4 + j, (*chip, c), sibling)
                cp.start()
                passed.append(cp)
        for a in range(n):
            copy(a, 0, sibling, me).wait_recv()
            for j, chip in enumerate(chips):
                copy(a, 4 + j, (*chip, 1 - c), me).wait_recv()
        for cp in first + passed:
            cp.wait_send()
        for cp in mine:
            cp.wait()

    return pl.pallas_call(
        body, name=name,
        out_shape=[jax.ShapeDtypeStruct((N_DEV,) + a.shape, a.dtype) for a in arrs],
        in_specs=[_ANY] * (n + len(deps)), out_specs=[_ANY] * n,
        scratch_shapes=[pltpu.SemaphoreType.DMA((7 * n,)), pltpu.SemaphoreType.DMA((7 * n,)),
                        pltpu.SemaphoreType.DMA((n,))],
    )(*arrs, *deps)


def in_proj_gathered(xs, g, w_own, extras, name, tm=512):
    s, d = xs.shape
    n = w_own.shape[1]
    arrs = [w_own] + list(extras)
    na = len(arrs)
    tr = 256

    def body(*refs):
        x_ref, g_ref, ins = refs[0], refs[1], refs[2:2 + na]
        h_out, p_ref, outs = refs[2 + na], refs[3 + na], refs[4 + na:4 + 2 * na]
        (h_ref, xbuf, wbuf, obuf, send_sems, recv_sems, load_sem, store_sems, own_sems, h_sem,
         x_sems) = refs[4 + 2 * na:]
        x, y, c = _place()
        me, sibling = (x, y, c), (x, y, 1 - c)
        x_first = c == 0
        near = (jnp.where(x_first, 1 - x, x), jnp.where(x_first, y, 1 - y))
        far = (jnp.where(x_first, x, 1 - x), jnp.where(x_first, 1 - y, y))
        diag = (1 - x, 1 - y)
        k_near, k_far = jnp.where(x_first, 1, 2), jnp.where(x_first, 2, 1)
        f_near, f_far = k_near + 3, k_far + 3

        def slot(block):
            return 4 * block[0] + 2 * block[1] + block[2]

        def copy(a, k, block, to, src=None):
            dst = outs[a].at[slot(block)]
            return pltpu.make_async_remote_copy(
                src_ref=dst if src is None else src, dst_ref=dst,
                send_sem=send_sems.at[7 * a + k], recv_sem=recv_sems.at[7 * a + k],
                device_id=to, device_id_type=MESH)

        first = []
        for a in range(na):
            first += [copy(a, 0, me, sibling, src=ins[a]), copy(a, 1, me, (1 - x, y, c), src=ins[a]),
                      copy(a, 2, me, (x, 1 - y, c), src=ins[a])]
        for cp in first:
            cp.start()
        own = pltpu.make_async_copy(wbuf.at[0], outs[0].at[slot(me)], own_sems.at[0])
        mine = [pltpu.make_async_copy(ins[a], outs[a].at[slot(me)], own_sems.at[a]) for a in range(1, na)]
        stores = [None, None]

        def x_load(i):
            return pltpu.make_async_copy(x_ref.at[pl.ds(i * tr, tr), :], xbuf.at[i % 2], x_sems.at[i % 2])

        x_load(0).start()
        for i in range(s // tr):
            if i + 1 < s // tr:
                x_load(i + 1).start()
            x_load(i).wait()
            xv = xbuf[i % 2]
            r = lax.rsqrt(jnp.mean(xv * xv, axis=-1, keepdims=True) + EPS)
            h_ref[i * tr:(i + 1) * tr, :] = (xv * r * g_ref[...]).astype(BF16)
        h_store = pltpu.make_async_copy(h_ref, h_out, h_sem)
        h_store.start()

        def multiply(k, block, w_from):
            b = k % 2
            if k == 2:
                own.wait()
            load = pltpu.make_async_copy(w_from, wbuf.at[b], load_sem)
            load.start()
            if stores[b] is not None:
                stores[b].wait()
            load.wait()
            if k == 0:
                own.start()

            def chunk(i, carry):
                rows = pl.ds(pl.multiple_of(i * tm, tm), tm)
                obuf[b, rows, :] = jnp.dot(h_ref[rows, :], wbuf[b], preferred_element_type=F32).astype(BF16)
                return carry

            lax.fori_loop(0, s // tm, chunk, 0)
            stores[b] = pltpu.make_async_copy(
                obuf.at[b], p_ref.at[:, pl.ds(pl.multiple_of(slot(block) * n, 128), n)], store_sems.at[b])
            stores[b].start()

        passed = []

        def arrive(a, k, block):
            copy(a, k, block, me).wait_recv()

        def pass_on(a, k, block, to):
            cp = copy(a, k, block, to)
            cp.start()
            passed.append(cp)

        def gather(a, use):
            use(0, me)
            arrive(a, 0, sibling)
            use(1, sibling)
            arrive(a, k_near, (*near, c))
            pass_on(a, 3, (*near, c), (*far, c))
            pass_on(a, f_near, (*near, c), sibling)
            use(2, (*near, c))
            arrive(a, f_far, (*far, 1 - c))
            use(3, (*far, 1 - c))
            arrive(a, k_far, (*far, c))
            pass_on(a, f_far, (*far, c), sibling)
            use(4, (*far, c))
            arrive(a, f_near, (*near, 1 - c))
            use(5, (*near, 1 - c))
            arrive(a, 3, (*diag, c))
            pass_on(a, 6, (*diag, c), sibling)
            use(6, (*diag, c))
            arrive(a, 6, (*diag, 1 - c))
            use(7, (*diag, 1 - c))

        gather(0, lambda k, block: multiply(k, block, ins[0] if k == 0 else outs[0].at[slot(block)]))
        for cp in mine:
            cp.start()
        for a in range(1, na):
            gather(a, lambda k, block: None)
        for cp in first + passed:
            cp.wait_send()
        for cp in mine + stores + [h_store]:
            cp.wait()

    vmem = pl.BlockSpec(memory_space=pltpu.VMEM)
    outs = pl.pallas_call(
        body, name=name,
        out_shape=[jax.ShapeDtypeStruct((s, d), BF16), jax.ShapeDtypeStruct((s, N_DEV * n), BF16)]
        + [jax.ShapeDtypeStruct((N_DEV,) + a.shape, a.dtype) for a in arrs],
        in_specs=[_ANY, vmem] + [_ANY] * na, out_specs=[_ANY] * (2 + na),
        scratch_shapes=[pltpu.VMEM((s, d), BF16), pltpu.VMEM((2, tr, d), F32), pltpu.VMEM((2, d, n), BF16),
                        pltpu.VMEM((2, s, n), BF16),
                        pltpu.SemaphoreType.DMA((7 * na,)), pltpu.SemaphoreType.DMA((7 * na,)),
                        pltpu.SemaphoreType.DMA, pltpu.SemaphoreType.DMA((2,)), pltpu.SemaphoreType.DMA((na,)),
                        pltpu.SemaphoreType.DMA, pltpu.SemaphoreType.DMA((2,))],
        compiler_params=pltpu.CompilerParams(vmem_limit_bytes=VMEM_LIMIT),
    )(xs, g, *arrs)
    return outs[0], outs[1], outs[2], outs[3:]


_HBM = pl.BlockSpec(memory_space=pltpu.HBM)
_SEM = pl.BlockSpec(memory_space=pltpu.SEMAPHORE)
_DATAFLOW = pltpu.SideEffectType.DATAFLOW_SIDE_EFFECTING


def _peers_per_array(kind):
    return 1 if kind in ("sibling", "halves") else 3


def _split_copies(kind, srcs, lands, send_sems, recv_sems):
    x, y, c = _place()
    per = _peers_per_array(kind)
    out = []
    for a in range(len(lands)):
        if kind == "sibling":
            part = srcs[a] if srcs[a].shape[1] == 1 else srcs[a].at[:, pl.ds(1 - c, 1)]
            peers = [((x, y, 1 - c), part, lands[a], lands[a])]
        elif kind == "halves":
            mine, its = lands[a].at[:, pl.ds(c, 1)], lands[a].at[:, pl.ds(1 - c, 1)]
            peers = [((x, y, 1 - c), mine, mine, its)]
        else:
            peers = []
            for px, py in [(1 - x, y), (x, 1 - y), (1 - x, 1 - y)]:
                if kind == "gather":
                    views = (srcs[a], lands[a].at[4 * x + 2 * y + c], lands[a].at[4 * px + 2 * py + c])
                else:
                    views = (srcs[a].at[2 * px + py], lands[a].at[2 * x + y], lands[a].at[2 * px + py])
                peers.append(((px, py, c),) + views)
        for j, (peer, src, dst, arrives) in enumerate(peers):
            sems = dict(send_sem=send_sems.at[per * a + j], recv_sem=recv_sems.at[per * a + j],
                        device_id=peer, device_id_type=MESH)
            out.append((pltpu.make_async_remote_copy(src_ref=src, dst_ref=dst, **sems),
                        pltpu.make_async_remote_copy(src_ref=src, dst_ref=arrives, **sems)))
    return out


def split_start(kind, srcs, lands, deps, name):
    ns, nl = len(srcs), len(lands)
    n_sems = _peers_per_array(kind) * nl
    held = list(srcs) + list(lands)

    def body(*refs):
        send_sems, recv_sems = refs[len(held) + len(deps)], refs[len(held) + len(deps) + 1]
        for copy, _ in _split_copies(kind, refs[:ns], refs[ns:ns + nl], send_sems, recv_sems):
            copy.start()
        token = refs[-1]
        token[...] = jnp.zeros_like(token)

    outs = pl.pallas_call(
        body, name=name,
        out_shape=(pltpu.SemaphoreType.DMA((n_sems,)), pltpu.SemaphoreType.DMA((n_sems,)),
                   *[pltpu.HBM(a.shape, a.dtype) for a in held], jax.ShapeDtypeStruct((8, 128), F32)),
        in_specs=[_HBM] * len(held) + [_ANY] * len(deps),
        out_specs=(_SEM, _SEM, *([_HBM] * len(held)), pl.BlockSpec(memory_space=pltpu.VMEM)),
        input_output_aliases={i: 2 + i for i in range(len(held))},
        compiler_params=pltpu.CompilerParams(has_side_effects=_DATAFLOW),
    )(*[pltpu.with_memory_space_constraint(a, pltpu.HBM) for a in held], *deps)
    return outs[0], outs[1], list(outs[2:2 + ns]), list(outs[2 + ns:2 + ns + nl]), outs[-1]


def split_wait(kind, send_sems, recv_sems, srcs, lands, afters, name):
    ns, nl = len(srcs), len(lands)
    held = list(srcs) + list(lands)

    def body(*refs):
        for _, arrival in _split_copies(kind, refs[:ns], refs[ns:ns + nl], refs[ns + nl], refs[ns + nl + 1]):
            arrival.wait_send()
            arrival.wait_recv()

    outs = pl.pallas_call(
        body, name=name,
        out_shape=[pltpu.HBM(a.shape, a.dtype) for a in held],
        in_specs=[_HBM] * len(held) + [_SEM, _SEM] + [_ANY] * len(afters),
        out_specs=[_HBM] * len(held),
        input_output_aliases={i: i for i in range(len(held))},
        compiler_params=pltpu.CompilerParams(has_side_effects=_DATAFLOW),
    )(*held, send_sems, recv_sems, *afters)
    return list(outs[:ns]), list(outs[ns:])


def place_block(land, block, dev, name):
    r, c = block.shape
    tr = min(r, 512)

    def body(dev_ref, land_ref, b_ref, o_ref):
        del dev_ref, land_ref
        o_ref[...] = b_ref[...]

    return pl.pallas_call(
        body, name=name,
        grid_spec=pltpu.PrefetchScalarGridSpec(
            num_scalar_prefetch=1, grid=(r // tr,),
            in_specs=[_ANY, pl.BlockSpec((tr, c), lambda i, dev_ref: (i, 0))],
            out_specs=pl.BlockSpec((None, tr, c), lambda i, dev_ref: (dev_ref[0], i, 0))),
        out_shape=jax.ShapeDtypeStruct(land.shape, land.dtype),
        input_output_aliases={1: 0},
        compiler_params=_params("parallel"),
    )(dev, land, block)


def pair_add(own, recv, core, name):
    _, _, r, c = own.shape
    tr = min(r, 512)

    def body(core_ref, own_ref, recv_ref, o_ref):
        del core_ref
        o_ref[...] = (own_ref[...].astype(F32) + recv_ref[...].astype(F32)).astype(BF16)

    return pl.pallas_call(
        body, name=name,
        grid_spec=pltpu.PrefetchScalarGridSpec(
            num_scalar_prefetch=1, grid=(4, r // tr),
            in_specs=[pl.BlockSpec((None, None, tr, c), lambda k, i, core_ref: (k, core_ref[0], i, 0)),
                      pl.BlockSpec((None, None, tr, c), lambda k, i, core_ref: (k, 0, i, 0))],
            out_specs=pl.BlockSpec((None, tr, c), lambda k, i, core_ref: (k, i, 0))),
        out_shape=jax.ShapeDtypeStruct((4, r, c), BF16),
        compiler_params=_params("parallel", "parallel"),
    )(core, own, recv)


def _adamw_math(w, g, m, v):
    m2 = ADAM_B1 * m + (1.0 - ADAM_B1) * g
    v2 = ADAM_B2 * v + (1.0 - ADAM_B2) * (g * g)
    m_hat = m2 / (1.0 - ADAM_B1 ** ADAM_STEP)
    v_hat = v2 / (1.0 - ADAM_B2 ** ADAM_STEP)
    delta = -ADAM_LR * (m_hat / (jnp.sqrt(v_hat) + ADAM_EPS) + ADAM_WD * w)
    return delta, m2, v2


def adamw_big(w, m, v, own, got, chip, name):
    r, c = w.shape
    tr = min(r, 256)

    def body(chip_ref, w_ref, m_ref, v_ref, p0, p1, p2, p3, g_ref, d_ref, m2_ref, v2_ref):
        del chip_ref
        g = ((p0[...].astype(F32) + p1[...].astype(F32)) + p2[...].astype(F32)) + p3[...].astype(F32)
        delta, m2, v2 = _adamw_math(w_ref[...], g, m_ref[...], v_ref[...])
        g_ref[...] = g
        d_ref[...] = delta
        m2_ref[...] = m2
        v2_ref[...] = v2

    row = pl.BlockSpec((tr, c), lambda i, chip_ref: (i, 0))

    def slab(flip):
        return pl.BlockSpec((None, tr, c), lambda i, chip_ref: (chip_ref[0] ^ flip, i, 0))

    return pl.pallas_call(
        body, name=name,
        grid_spec=pltpu.PrefetchScalarGridSpec(
            num_scalar_prefetch=1, grid=(r // tr,),
            in_specs=[row, row, row, slab(0), slab(1), slab(2), slab(3)],
            out_specs=[row] * 4),
        out_shape=[jax.ShapeDtypeStruct((r, c), F32)] * 4,
        compiler_params=_params("parallel"),
    )(chip, w, m, v, own, got, got, got)


def sum_devices(g8, name):
    def body(g_ref, o_ref):
        tot = g_ref[0]
        for k in range(1, N_DEV):
            tot = tot + g_ref[k]
        o_ref[...] = tot

    return pl.pallas_call(body, name=name, out_shape=jax.ShapeDtypeStruct(g8.shape[1:], F32))(g8)


def adamw_small(ws, gs, ms, vs, name):
    n = len(ws)

    def body(*refs):
        w_r, g_r, m_r, v_r = refs[:n], refs[n:2 * n], refs[2 * n:3 * n], refs[3 * n:4 * n]
        d_o, m_o, v_o = refs[4 * n:5 * n], refs[5 * n:6 * n], refs[6 * n:7 * n]
        for k in range(n):
            delta, m2, v2 = _adamw_math(w_r[k][...], g_r[k][...], m_r[k][...], v_r[k][...])
            d_o[k][...] = delta
            m_o[k][...] = m2
            v_o[k][...] = v2

    shapes = [jax.ShapeDtypeStruct(w.shape, F32) for w in ws]
    outs = pl.pallas_call(body, name=name, out_shape=shapes * 3)(*ws, *gs, *ms, *vs)
    return outs[:n], outs[n:2 * n], outs[2 * n:]


def _rows128(a):
    return a.reshape(-1, 128)


def _pad_rows(a, rows):
    return jnp.pad(a, ((0, rows - a.shape[0]), (0, 0)))


def kernel(x, ln_pre_even, w_in_even, pool_w, pool_scale, w_out_even, ln_post_even, ln_pre_odd, w_in_odd, sconv_w, dconv_w, dconv_b, cnorm_g, cnorm_b, w_out_odd, ln_post_odd, loss_target, m_ln_pre_even, m_w_in_even, m_pool_w, m_pool_scale, m_w_out_even, m_ln_post_even, m_ln_pre_odd, m_w_in_odd, m_sconv_w, m_dconv_w, m_dconv_b, m_cnorm_g, m_cnorm_b, m_w_out_odd, m_ln_post_odd, v_ln_pre_even, v_w_in_even, v_pool_w, v_pool_scale, v_w_out_even, v_ln_post_even, v_ln_pre_odd, v_w_in_odd, v_sconv_w, v_dconv_w, v_dconv_b, v_cnorm_g, v_cnorm_b, v_w_out_odd, v_ln_post_odd):
    xs = x[0]
    tgt = loss_target[0]
    s, d = xs.shape
    half = d // 2
    n_heads = half // HEAD_DIM
    ng = len(POOL_WINDOWS)
    cwp = half // ng
    dev = 4 * lax.axis_index("x") + 2 * lax.axis_index("y") + lax.axis_index("c")
    core = lax.axis_index("c").astype(jnp.int32).reshape(1)

    pr = pool_w.shape[2]
    cl = sconv_w.shape[2]
    small_parts = [(_rows128(ln_pre_odd), 8), (sconv_w[0], 8), (dconv_w[0], 32), (dconv_b, 8),
                   (cnorm_g, 8), (cnorm_b, 8), (_rows128(ln_post_odd), 8)]
    small_local = jnp.concatenate([_pad_rows(a, r) for a, r in small_parts], axis=0)
    h0, p0, g_wie, (g_pw, g_small) = in_proj_gathered(
        xs, ln_pre_even, w_in_even[0].astype(BF16), [pool_w[0].reshape(ng * pr, cwp).astype(BF16), small_local],
        "ag_in_proj_even")
    comm = _Exchanges(dev, core, d)
    token = comm.start_weights("out_even", [w_out_even[0].astype(BF16)], [p0])
    token = comm.start_weights("in_odd", [w_in_odd[0].astype(BF16)], [token])
    sb_dep = comm.start_weights("out_odd", [w_out_odd[0].astype(BF16)], [token])
    pool_full = g_pw.reshape(N_DEV, ng, pr, cwp).transpose(1, 0, 2, 3).reshape(ng, cwp, cwp)
    nl = ln_pre_odd.shape[1] // 128

    def chan(lo, rows):
        return g_small[:, lo:lo + rows].transpose(1, 0, 2).reshape(rows, N_DEV * cl)

    ln_pre_odd_f = g_small[:, 0:nl].reshape(1, d)
    sconv_f = chan(8, SCONV_K)
    dconv_f = chan(16, CONF_K)
    dconv_b_f = chan(48, 1)
    cnorm_g_f = chan(56, 1)
    cnorm_b_f = chan(64, 1)
    ln_post_odd_f = g_small[:, 72:72 + nl].reshape(1, d)

    loss_blk, grad_x, small_g = _fwd_bwd(
        xs, tgt, ln_pre_even, h0, p0, g_wie, pool_full, pool_scale, ln_post_even, ln_pre_odd_f,
        sconv_f, dconv_f, dconv_b_f, cnorm_g_f, cnorm_b_f, ln_post_odd_f, comm, sb_dep)
    small_w = [ln_pre_even, pool_scale, ln_post_even, ln_pre_odd, sconv_w[0], dconv_w[0], dconv_b, cnorm_g, cnorm_b, ln_post_odd]
    small_m = [m_ln_pre_even, m_pool_scale, m_ln_post_even, m_ln_pre_odd, m_sconv_w[0], m_dconv_w[0], m_dconv_b, m_cnorm_g, m_cnorm_b, m_ln_post_odd]
    small_v = [v_ln_pre_even, v_pool_scale, v_ln_post_even, v_ln_pre_odd, v_sconv_w[0], v_dconv_w[0], v_dconv_b, v_cnorm_g, v_cnorm_b, v_ln_post_odd]
    big = {"w_in_even": (w_in_even, m_w_in_even, v_w_in_even), "pool_w": (pool_w, m_pool_w, v_pool_w),
           "w_out_even": (w_out_even, m_w_out_even, v_w_out_even), "w_in_odd": (w_in_odd, m_w_in_odd, v_w_in_odd),
           "w_out_odd": (w_out_odd, m_w_out_odd, v_w_out_odd)}
    upd = comm.finish_updates(big, [grad_x])
    upd.update(comm.finish_updates(big, [grad_x]))
    sg, sd, sm, sv, loss = _update_small(small_g, loss_blk, small_w, small_m, small_v, dev, d, cl,
                                         deps=[upd["w_in_odd"][1], upd["w_out_even"][1]])
    upd.update(comm.finish_updates(big, sd))
    (g_wie_o, d_wie, m_wie, v_wie), (g_pw_o, d_pw, m_pw, v_pw) = upd["w_in_even"], upd["pool_w"]
    (g_woe_o, d_woe, m_woe, v_woe), (g_wio_o, d_wio, m_wio, v_wio) = upd["w_out_even"], upd["w_in_odd"]
    g_woo_o, d_woo, m_woo, v_woo = upd["w_out_odd"]

    def order(small, wie, pw, woe, wio, woo):
        return [small[0], wie, pw, small[1], woe, small[2], small[3], wio, small[4], small[5], small[6],
                small[7], small[8], woo, small[9]]

    grads = order(sg, g_wie_o, g_pw_o, g_woe_o, g_wio_o, g_woo_o)
    deltas = order(sd, d_wie, d_pw, d_woe, d_wio, d_woo)
    new_m = order(sm, m_wie, m_pw, m_woe, m_wio, m_woo)
    new_v = order(sv, v_wie, v_pw, v_woe, v_wio, v_woo)
    return (loss, grad_x[None], *grads, *deltas, *new_m, *new_v)


def _fwd_bwd(xs, tgt, ln_pre_even, h0, p0, g_wie, pool_full, pool_scale, ln_post_even, ln_pre_odd_f,
             sconv_f, dconv_f, dconv_b_f, cnorm_g_f, cnorm_b_f, ln_post_odd_f, comm, sb_dep):
    d = xs.shape[1]
    n_heads = d // 2 // HEAD_DIM
    ng, cwp = pool_full.shape[0], pool_full.shape[1]
    a0, sb_wts = sb_fwd(p0, n_heads, "sb_fwd", dep=sb_dep)
    dep = comm.weights_arrived("out_even", after=a0)
    y0 = even_mix_fwd(a0, p0, pool_full, pool_scale, "even_mix_fwd", dep=dep)
    (w_out_e,) = comm.weights("out_even", after=y0)
    w_out_e = w_out_e.reshape(1, d, d)
    o0 = mm_nn(y0, w_out_e, BF16, "out_proj_even", tn=512)
    dep = comm.weights_arrived("in_odd", after=o0)
    x1, h1 = postnorm_fwd(xs, o0, ln_post_even, ln_pre_odd_f, "post_even", dep=dep)
    (g_wio,) = comm.weights("in_odd", after=x1)
    p1 = mm_nn(h1, g_wio, BF16, "in_proj_odd", group=2)
    dep = comm.weights_arrived("out_odd", after=p1)
    y1, dc = odd_mix_fwd(p1, sconv_f, dconv_f, dconv_b_f, cnorm_g_f, cnorm_b_f, "odd_mix_fwd", dep=dep)
    (w_out_o,) = comm.weights("out_odd", after=y1)
    w_out_o = w_out_o.reshape(1, d, d)
    o1 = mm_nn(y1, w_out_o, BF16, "out_proj_odd", tn=512)
    loss_blk, gx2, do1, dg_post_odd = final_fwd_bwd(x1, o1, ln_post_odd_f, tgt, "post_odd_loss")

    dw_out_o = mm_tn(y1, do1, 1, BF16, "dw_out_odd")
    dy1 = mm_nt(do1, w_out_o, BF16, "dy_odd")
    ddc, dg2, dgam, dbet = odd_bwd_ln(dy1, p1, dc, cnorm_g_f, cnorm_b_f, "odd_bwd_ln")
    dp1, dsconv, ddconv, ddconv_b = odd_bwd_conv(dy1, p1, ddc, dg2, sconv_f, dconv_f, "odd_bwd_conv")
    dw_in_o = mm_tn(h1, dp1, N_DEV, BF16, "dw_in_odd", group=2)
    dep = comm.reduce_begin({"w_out_odd": dw_out_o.reshape(N_DEV, d // N_DEV, d), "w_in_odd": dw_in_o}, "odd")
    dh1 = mm_nt(dp1, g_wio, BF16, "dh_odd", dep=dep, group=2)
    dep = comm.reduce_send(after=dh1)
    gx1, dg_pre_odd, do0, dg_post_even = norm_bwd(dh1, x1, ln_pre_odd_f, gx2, "pre_odd_post_even_bwd",
                                                  inp2=o0, g2=ln_post_even, dep=dep)

    dw_out_e = mm_tn(y0, do0, 1, BF16, "dw_out_even")
    dy0 = mm_nt(do0, w_out_e, BF16, "dy_even")
    da0, du0, dg0, dpool, dpool_scale = even_mix_bwd(dy0, a0, p0, pool_full, pool_scale, "even_mix_bwd")
    pr = cwp // N_DEV
    dpool_slabs = dpool.astype(BF16).reshape(ng, N_DEV, pr, cwp).transpose(1, 0, 2, 3).reshape(N_DEV, ng * pr, cwp)
    dep = comm.reduce_begin({"w_out_even": dw_out_e.reshape(N_DEV, d // N_DEV, d), "pool_w": dpool_slabs}, "even_out")
    dq0, dk0, dv0 = sb_bwd(p0, a0, sb_wts, da0, n_heads, "sb_bwd", dep=dep)
    dep = comm.reduce_send(after=dq0)
    dp0 = jnp.concatenate([dq0, dk0, dv0, du0, dg0], axis=1)
    dw_sibling = mm_tn(h0, dp0, N_DEV // 2, BF16, "dw_in_even_sibling", dep=dep, pick=(2, 1 - comm.core))
    dep = comm.reduce_begin({"w_in_even": dw_sibling}, "even_in", sibling_part=True)
    dw_own = mm_tn(h0, dp0, N_DEV // 2, BF16, "dw_in_even_own", dep=dep, pick=(2, comm.core))
    dep = comm.reduce_send(after=dw_own, own_part={"w_in_even": dw_own})
    dh0 = mm_nt(dp0, g_wie, BF16, "dh_even", dep=dep, group=2)
    dep = None
    grad_x, dg_pre_even = norm_bwd(dh0, xs, ln_pre_even, gx1, "pre_even_bwd", dep=dep)
    small_g = [dg_pre_even, dpool_scale, dg_post_even, dg_pre_odd, dsconv, ddconv, ddconv_b, dgam, dbet, dg_post_odd]
    return loss_blk, grad_x, small_g


class _Exchanges:
    def __init__(self, dev, core, d):
        self.dev = dev.astype(jnp.int32).reshape(1)
        self.core = core
        self.chip = (dev // 2).astype(jnp.int32).reshape(1)
        self.d = d
        self.in_flight = {}
        self.to_sibling = None
        self.pending = []

    def start_weights(self, tag, blocks, afters):
        lands = [lax.empty((N_DEV,) + b.shape, b.dtype) for b in blocks]
        send, recv, srcs, lands, token = split_start("gather", blocks, lands, afters, "ag_start_" + tag)
        self.in_flight[tag] = (send, recv, srcs, lands)
        return token

    def weights_arrived(self, tag, after):
        send, recv, srcs, lands = self.in_flight.pop(tag)
        srcs, lands = split_wait("gather", send, recv, srcs, lands, [after], "ag_wait_" + tag)
        lands = [place_block(l, b, self.dev, "ag_own_%s_%d" % (tag, k)) for k, (l, b) in enumerate(zip(lands, srcs))]
        lands = [l.reshape((4, 2) + l.shape[1:]) for l in lands]
        send, recv, _, lands, token = split_start("halves", [], lands, [], "ag_sibling_start_" + tag)
        self.in_flight[tag] = (send, recv, lands)
        return token

    def weights(self, tag, after):
        send, recv, lands = self.in_flight.pop(tag)
        _, lands = split_wait("halves", send, recv, [], lands, [after], "ag_sibling_wait_" + tag)
        return [l.reshape((N_DEV,) + l.shape[2:]) for l in lands]

    def reduce_begin(self, partials, tag, sibling_part=False):
        names = list(partials)
        arrs = [partials[k].reshape((4, 1 if sibling_part else 2) + partials[k].shape[1:]) for k in names]
        lands = [lax.empty((4, 1) + a.shape[2:], a.dtype) for a in arrs]
        send, recv, srcs, lands, token = split_start("sibling", arrs, lands, [], "rs_sibling_start_" + tag)
        self.to_sibling = (tag, names, send, recv, srcs, lands)
        return token

    def reduce_send(self, after, own_part=None):
        tag, names, send, recv, srcs, lands = self.to_sibling
        srcs, lands = split_wait("sibling", send, recv, srcs, lands, [after], "rs_sibling_wait_" + tag)
        which = self.core
        if own_part is not None:
            srcs = [own_part[k].reshape((4, 1) + own_part[k].shape[1:]) for k in names]
            which = jnp.zeros((1,), jnp.int32)
        sums = [pair_add(o, r, which, "rs_pair_add_" + k) for k, o, r in zip(names, srcs, lands)]
        zones = [lax.empty(a.shape, a.dtype) for a in sums]
        send, recv, srcs, zones, token = split_start("scatter", sums, zones, [], "rs_start_" + tag)
        self.pending.append((tag, names, send, recv, srcs, zones))
        return token

    def finish_updates(self, big, afters):
        tag, names, send, recv, srcs, lands = self.pending.pop(0)
        srcs, lands = split_wait("scatter", send, recv, srcs, lands, afters, "rs_wait_" + tag)
        out = {}
        for name, own, got in zip(names, srcs, lands):
            w, m, v = big[name]
            shp = own.shape[1:]
            outs = adamw_big(w.reshape(shp), m.reshape(shp), v.reshape(shp), own, got, self.chip, "adamw_" + name)
            out[name] = [o.reshape(w.shape) for o in outs]
        return out


def _update_small(small_g, loss_blk, small_w, small_m, small_v, dev, d, cl, deps):
    packed = jnp.concatenate([_rows128(g) for g in small_g] + [loss_blk], axis=0)
    (g8,) = all_gather([packed], "ag_small_grads", deps)
    tot = sum_devices(g8, "sum_small_grads")
    loss = tot[packed.shape[0] - 8, 0]
    full_g = []
    lo = 0
    for g in small_g:
        rows = g.size // 128
        full_g.append(tot[lo:lo + rows].reshape(g.shape))
        lo += rows

    def mine(g, width):
        return lax.dynamic_slice_in_dim(g, dev * width, width, axis=g.ndim - 1)

    fg = full_g
    small_gl = [fg[0], fg[1], fg[2], mine(fg[3], d // N_DEV), mine(fg[4], cl), mine(fg[5], cl), mine(fg[6], cl),
                mine(fg[7], cl), mine(fg[8], cl), mine(fg[9], d // N_DEV)]
    sd, sm, sv = adamw_small(small_w, small_gl, small_m, small_v, "adamw_small")

    def like(k, a):
        return a[None] if k in (4, 5) else a

    sg = [like(k, a) for k, a in enumerate(small_gl)]
    sd = [like(k, a) for k, a in enumerate(sd)]
    sm = [like(k, a) for k, a in enumerate(sm)]
    sv = [like(k, a) for k, a in enumerate(sv)]
    return sg, sd, sm, sv, loss
```

```python
import functools
import math

import jax
import jax.numpy as jnp
from jax import lax
from jax.experimental import pallas as pl
from jax.experimental.pallas import tpu as pltpu

F32 = jnp.float32
BF16 = jnp.bfloat16
EPS = 1e-6
HEAD_DIM = 128
POOL_WINDOWS = (2, 4, 8, 16)
SCONV_K = 3
CONF_K = 31
HALO = 32
N_DEV = 8
VMEM_LIMIT = 56 * 1024 * 1024
MESH = pl.DeviceIdType.MESH

ADAM_LR = 0.001
ADAM_B1 = 0.9
ADAM_B2 = 0.999
ADAM_EPS = 1e-08
ADAM_WD = 0.01
ADAM_STEP = 10


def _params(*sem):
    return pltpu.CompilerParams(dimension_semantics=sem, vmem_limit_bytes=VMEM_LIMIT)


def _sigmoid(v):
    return 1.0 / (1.0 + jnp.exp(-v))


def _silu(v):
    return v * _sigmoid(v)


def _silu_and_grad(v):
    s = _sigmoid(v)
    return v * s, s * (1.0 + v * (1.0 - s))


def _rowsum8(v):
    r, c = v.shape
    return jnp.sum(v.reshape(r // 8, 8, c), axis=0)


SUBLANES = 8


class _Taps:
    def __init__(self, xx, rows, before):
        self.xx, self.rows, self.before, self.rotated = xx, rows, before, {}

    def __call__(self, i):
        r, q = i % SUBLANES, i // SUBLANES
        if r not in self.rotated:
            n = self.xx.shape[0]
            self.rotated[r] = self.xx if r == 0 else pltpu.roll(self.xx, r if self.before else n - r, 0)
        lo = HALO - SUBLANES * q if self.before else SUBLANES * q
        return self.rotated[r][lo:lo + self.rows]


def _window_sum(xx, win, before):
    n = xx.shape[0]
    acc = xx
    k = 1
    while k < win:
        acc = acc + pltpu.roll(acc, k if before else n - k, 0)
        k *= 2
    return acc


def postnorm_fwd(x, o, g, g_next, name, tm=512, dep=None):
    s, d = x.shape
    dep_args, dep_specs = _after(dep)

    def body(x_ref, o_ref, g_ref, gn_ref, *rest):
        y_ref, h_ref = rest[-2:]
        ov = o_ref[...].astype(F32)
        r = lax.rsqrt(jnp.mean(ov * ov, axis=-1, keepdims=True) + EPS)
        y = x_ref[...] + ov * r * g_ref[...]
        y_ref[...] = y
        r2 = lax.rsqrt(jnp.mean(y * y, axis=-1, keepdims=True) + EPS)
        h_ref[...] = (y * r2 * gn_ref[...]).astype(BF16)

    row = pl.BlockSpec((tm, d), lambda i: (i, 0))
    vec = pl.BlockSpec((1, d), lambda i: (0, 0))
    return pl.pallas_call(
        body, name=name, grid=(s // tm,),
        in_specs=[row, row, vec, vec] + dep_specs, out_specs=[row, row],
        out_shape=[jax.ShapeDtypeStruct((s, d), F32), jax.ShapeDtypeStruct((s, d), BF16)],
        compiler_params=_params("parallel"),
    )(x, o, g, g_next, *dep_args)


def final_fwd_bwd(x1, o, g, target, name, tm=512):
    s, d = x1.shape
    n = s // tm

    def body(x_ref, o_ref, g_ref, t_ref, loss_ref, gx_ref, do_ref, dg_ref, lacc, gacc):
        i = pl.program_id(0)

        @pl.when(i == 0)
        def _():
            lacc[...] = jnp.zeros_like(lacc)
            gacc[...] = jnp.zeros_like(gacc)

        ov = o_ref[...].astype(F32)
        gv = g_ref[...]
        r = lax.rsqrt(jnp.mean(ov * ov, axis=-1, keepdims=True) + EPS)
        oh = ov * r
        diff = x_ref[...] + oh * gv - t_ref[...]
        lacc[...] += _rowsum8(diff * diff)
        gx = diff * (1.0 / d)
        gx_ref[...] = gx
        gacc[...] += _rowsum8(gx * oh)
        dn = gx * gv
        do_ref[...] = (r * (dn - oh * jnp.mean(dn * oh, axis=-1, keepdims=True))).astype(BF16)

        @pl.when(i == n - 1)
        def _():
            tot = jnp.sum(jnp.sum(lacc[...], axis=0, keepdims=True), axis=1, keepdims=True)
            loss_ref[...] = jnp.broadcast_to(tot * (0.5 / d), loss_ref.shape)
            dg_ref[...] = jnp.sum(gacc[...], axis=0, keepdims=True)

    row = pl.BlockSpec((tm, d), lambda i: (i, 0))
    vec = pl.BlockSpec((1, d), lambda i: (0, 0))
    return pl.pallas_call(
        body, name=name, grid=(n,),
        in_specs=[row, row, vec, row],
        out_specs=[pl.BlockSpec((8, 128), lambda i: (0, 0)), row, row, vec],
        out_shape=[jax.ShapeDtypeStruct((8, 128), F32), jax.ShapeDtypeStruct((s, d), F32),
                   jax.ShapeDtypeStruct((s, d), BF16), jax.ShapeDtypeStruct((1, d), F32)],
        scratch_shapes=[pltpu.VMEM((8, d), F32), pltpu.VMEM((8, d), F32)],
        compiler_params=_params("arbitrary"),
    )(x1, o, g, target)


def _rms_bwd_rows(dyv, xv, gv):
    r = lax.rsqrt(jnp.mean(xv * xv, axis=-1, keepdims=True) + EPS)
    xh = xv * r
    dn = dyv * gv
    return r * (dn - xh * jnp.mean(dn * xh, axis=-1, keepdims=True)), _rowsum8(dyv * xh)


def norm_bwd(dy, inp, g, resid, name, inp2=None, g2=None, tm=256, dep=None):
    s, d = inp.shape
    n = s // tm
    chain = inp2 is not None

    def body(*refs):
        dy_ref, x_ref, g_ref, r_ref = refs[:4]
        outs = refs[-6:] if chain else refs[-3:]
        i = pl.program_id(0)

        @pl.when(i == 0)
        def _():
            for acc in outs[-2:] if chain else outs[-1:]:
                acc[...] = jnp.zeros_like(acc)

        if chain:
            x2_ref, g2_ref = refs[4:6]
            dx_ref, dg_ref, dx2_ref, dg2_ref, gacc, gacc2 = outs
        else:
            dx_ref, dg_ref, gacc = outs
        dx, dg_rows = _rms_bwd_rows(dy_ref[...].astype(F32), x_ref[...], g_ref[...])
        dx = dx + r_ref[...]
        dx_ref[...] = dx
        gacc[...] += dg_rows
        if chain:
            dx2, dg2_rows = _rms_bwd_rows(dx, x2_ref[...].astype(F32), g2_ref[...])
            dx2_ref[...] = dx2.astype(BF16)
            gacc2[...] += dg2_rows

        @pl.when(i == n - 1)
        def _():
            dg_ref[...] = jnp.sum(gacc[...], axis=0, keepdims=True)
            if chain:
                dg2_ref[...] = jnp.sum(gacc2[...], axis=0, keepdims=True)

    row = pl.BlockSpec((tm, d), lambda i: (i, 0))
    vec = pl.BlockSpec((1, d), lambda i: (0, 0))
    dep_args, dep_specs = _after(dep)
    extra = [inp2, g2] if chain else []
    return pl.pallas_call(
        body, name=name, grid=(n,),
        in_specs=[row, row, vec, row] + ([row, vec] if chain else []) + dep_specs,
        out_specs=[row, vec] * (2 if chain else 1),
        out_shape=[jax.ShapeDtypeStruct((s, d), F32), jax.ShapeDtypeStruct((1, d), F32)]
        + ([jax.ShapeDtypeStruct((s, d), BF16), jax.ShapeDtypeStruct((1, d), F32)] if chain else []),
        scratch_shapes=[pltpu.VMEM((8, d), F32)] * (2 if chain else 1),
        compiler_params=_params("arbitrary"),
    )(dy, inp, g, resid, *extra, *dep_args)


def _after(dep):
    if dep is None:
        return [], []
    return [dep], [pl.BlockSpec((8, 128), lambda *_: (0, 0))]


def _lane_concat(ref, count):
    return ref[0] if count == 1 else jnp.concatenate([ref[i] for i in range(count)], axis=1)


def mm_nn(a, w, out_dtype, name, tm=2048, tn=None, dep=None, group=1):
    m, k = a.shape
    tm = min(tm, m)
    ns, _, n = w.shape
    tn = n if tn is None else tn
    nj = n // tn
    assert group == 1 or nj == 1
    dep_args, dep_specs = _after(dep)

    def body(a_ref, w_ref, *rest):
        o_ref = rest[-1]
        o_ref[...] = jnp.dot(a_ref[...], _lane_concat(w_ref, group), preferred_element_type=F32).astype(out_dtype)

    return pl.pallas_call(
        body, name=name, grid=(ns // group, nj, m // tm),
        in_specs=[pl.BlockSpec((tm, k), lambda s, j, i: (i, 0)),
                  pl.BlockSpec((group, k, tn), lambda s, j, i: (s, 0, j))] + dep_specs,
        out_specs=pl.BlockSpec((tm, group * tn), lambda s, j, i: (i, s * nj + j)),
        out_shape=jax.ShapeDtypeStruct((m, ns * n), out_dtype),
        compiler_params=_params("parallel", "parallel", "parallel"),
    )(a, w, *dep_args)


def mm_nt(a, w, out_dtype, name, tm=1024, tn=None, dep=None, group=1):
    m = a.shape[0]
    tm = min(tm, m)
    ns, k, n = w.shape
    tn = n if tn is None else tn
    nj = n // tn
    assert group == 1 or nj == 1
    steps = ns * nj // group
    dep_args, dep_specs = _after(dep)

    def body(a_ref, w_ref, *rest):
        o_ref, acc = rest[-2:]
        r = pl.program_id(1)

        @pl.when(r == 0)
        def _():
            acc[...] = jnp.zeros_like(acc)

        acc[...] += lax.dot_general(a_ref[...], _lane_concat(w_ref, group), (((1,), (1,)), ((), ())),
                                    preferred_element_type=F32)

        @pl.when(r == steps - 1)
        def _():
            o_ref[...] = acc[...].astype(out_dtype)

    return pl.pallas_call(
        body, name=name, grid=(m // tm, steps),
        in_specs=[pl.BlockSpec((tm, group * tn), lambda i, r: (i, r)),
                  pl.BlockSpec((group, k, tn), lambda i, r: (r // nj, 0, r % nj))] + dep_specs,
        out_specs=pl.BlockSpec((tm, k), lambda i, r: (i, 0)),
        out_shape=jax.ShapeDtypeStruct((m, k), out_dtype),
        scratch_shapes=[pltpu.VMEM((tm, k), F32)],
        compiler_params=_params("parallel", "arbitrary"),
    )(a, w, *dep_args)


def mm_tn(a, b, ns, out_dtype, name, tk=1024, tm=2048, dep=None, pick=None, group=1):
    m, k = a.shape
    tm = min(tm, m)
    step, offset = (1, None) if pick is None else pick
    assert group == 1 or pick is None
    n = b.shape[1] // (ns * step)
    steps = m // tm
    dep_args, dep_specs = _after(dep)
    n_pre = 0 if pick is None else 1

    def b_block(s, j, r, *pre):
        return (r, s if pick is None else step * s + pre[0][0])

    def body(*refs):
        a_ref, b_ref = refs[n_pre:n_pre + 2]
        o_ref, acc = refs[-2:]
        r = pl.program_id(2)

        @pl.when(r == 0)
        def _():
            acc[...] = jnp.zeros_like(acc)

        acc[...] += lax.dot_general(a_ref[...], b_ref[...], (((0,), (0,)), ((), ())),
                                    preferred_element_type=F32)

        @pl.when(r == steps - 1)
        def _():
            for i in range(group):
                o_ref[i] = acc[:, i * n:(i + 1) * n].astype(out_dtype)

    return pl.pallas_call(
        body, name=name,
        grid_spec=pltpu.PrefetchScalarGridSpec(
            num_scalar_prefetch=n_pre, grid=(ns // group, k // tk, steps),
            in_specs=[pl.BlockSpec((tm, tk), lambda s, j, r, *pre: (r, j)),
                      pl.BlockSpec((tm, group * n), b_block)] + dep_specs,
            out_specs=pl.BlockSpec((group, tk, n), lambda s, j, r, *pre: (s, j, 0)),
            scratch_shapes=[pltpu.VMEM((tk, group * n), F32)]),
        out_shape=jax.ShapeDtypeStruct((ns, k, n), out_dtype),
        compiler_params=_params("parallel", "parallel", "arbitrary"),
    )(*([] if pick is None else [offset]), a, b, *dep_args)


SB_BLK = 128


LOG2E = 1.0 / math.log(2.0)


def _split_dot(v, tri2):
    hi = pltpu.bitcast(pltpu.bitcast(v, jnp.uint32) & jnp.uint32(0xFFFF0000), F32)
    lo = (v - hi).astype(BF16)
    return jnp.dot(jnp.concatenate([hi.astype(BF16), lo], axis=1), tri2, preferred_element_type=F32)


def _sb_scores(z2, lim, dcol, tri_ex, masked):
    sp = jnp.log2(1.0 + jnp.exp2(-jnp.abs(z2)))
    lb = jnp.minimum(z2, 0.0) - sp
    l1m = lb - z2
    mask = None
    if masked:
        mask = dcol < lim
        l1m = jnp.where(mask, l1m, 0.0)
    return mask, lb, l1m, _split_dot(l1m, tri_ex)


def _sb_consts():
    row = lax.broadcasted_iota(jnp.int32, (SB_BLK, SB_BLK), 0)
    col = lax.broadcasted_iota(jnp.int32, (SB_BLK, SB_BLK), 1)
    tri_ex = jnp.where(row > col, 1.0, 0.0).astype(BF16)
    tri_in = jnp.where(row >= col, 1.0, 0.0).astype(BF16)
    return col - row, jnp.concatenate([tri_ex, tri_ex], axis=0), jnp.concatenate([tri_in, tri_in], axis=0)


def sb_fwd(p, n_heads, name, tq=512, nsub=8, dep=None):
    s = p.shape[0]
    h_n = n_heads
    b = SB_BLK
    nqs = tq // b
    tk = nsub * b
    scale = 1.0 / math.sqrt(HEAD_DIM)

    dep_args, dep_specs = _after(dep)

    def body(q_ref, k_ref, v_ref, *rest):
        o_ref, w_ref = rest[-2:]
        qi = pl.program_id(1)
        dcol, tri_ex, _ = _sb_consts()
        qv = [q_ref[qs * b:(qs + 1) * b, :] for qs in range(nqs)]
        n_groups = ((qi + 1) * nqs - 1) // nsub + 1

        def step(it, carry, masked):
            c1s, accs = carry
            g = n_groups - 1 - it
            off = pl.multiple_of(g * tk, tk)
            kg = k_ref[pl.ds(off, tk), :]
            vg = v_ref[pl.ds(off, tk), :]
            new_c1, new_acc = [], []
            for qs in range(nqs):
                qb = qi * nqs + qs
                z2 = lax.dot_general(qv[qs], kg, (((1,), (1,)), ((), ())),
                                     preferred_element_type=F32) * (scale * LOG2E)
                blocks = [_sb_scores(z2[:, j * b:(j + 1) * b], (qb - (g * nsub + j)) * b, dcol, tri_ex, masked)
                          for j in range(nsub)]
                run = c1s[qs]
                ws = [None] * nsub
                for j in reversed(range(nsub)):
                    mask, lb, l1m, ls_loc = blocks[j]
                    wj = jnp.exp2(lb + ls_loc + run)
                    ws[j] = (jnp.where(mask, wj, 0.0) if masked else wj).astype(BF16)
                    run = run + jnp.sum(l1m, axis=1, keepdims=True)
                w = jnp.concatenate(ws, axis=1)
                w_ref[0, g, qs * b:(qs + 1) * b, :] = w
                new_acc.append(accs[qs] + jnp.dot(w, vg, preferred_element_type=F32))
                new_c1.append(run)
            return tuple(new_c1), tuple(new_acc)

        init = (tuple(jnp.zeros((b, 1), F32) for _ in range(nqs)),
                tuple(jnp.zeros((b, HEAD_DIM), F32) for _ in range(nqs)))
        assert all(((i + 1) * nqs - 1) // nsub * nsub <= i * nqs for i in range(s // tq))
        first = step(0, init, True)
        _, accs = lax.fori_loop(1, n_groups, functools.partial(step, masked=False), first)
        for qs in range(nqs):
            o_ref[qs * b:(qs + 1) * b, :] = accs[qs]

    return pl.pallas_call(
        body, name=name, grid=(h_n, s // tq),
        in_specs=[pl.BlockSpec((tq, HEAD_DIM), lambda h, i: (i, h)),
                  pl.BlockSpec((s, HEAD_DIM), lambda h, i: (0, h_n + h)),
                  pl.BlockSpec((s, HEAD_DIM), lambda h, i: (0, 2 * h_n + h))] + dep_specs,
        out_specs=[pl.BlockSpec((tq, HEAD_DIM), lambda h, i: (i, h)),
                   pl.BlockSpec((1, s // tk, tq, tk), lambda h, i: (h, 0, i, 0))],
        out_shape=[jax.ShapeDtypeStruct((s, h_n * HEAD_DIM), F32),
                   jax.ShapeDtypeStruct((h_n, s // tk, s, tk), BF16)],
        compiler_params=_params("parallel", "arbitrary"),
    )(p, p, p, *dep_args)


def sb_bwd(p, a, wts, da, n_heads, name, tq=512, dep=None):
    s = p.shape[0]
    h_n = n_heads
    nq = s // tq
    b = SB_BLK
    nqs = tq // b
    tk = wts.shape[3]
    nsub = tk // b
    scale = 1.0 / math.sqrt(HEAD_DIM)
    dep_args, dep_specs = _after(dep)

    def body(q_ref, k_ref, v_ref, a_ref, da_ref, w_ref, *rest):
        dq_ref, dk_ref, dv_ref, dk_acc, dv_acc = rest[-5:]
        qi = pl.program_id(1)

        @pl.when(qi == 0)
        def _():
            dk_acc[...] = jnp.zeros_like(dk_acc)
            dv_acc[...] = jnp.zeros_like(dv_acc)

        dcol, _, tri_in = _sb_consts()
        q_all = q_ref[...]
        do_all = da_ref[...]
        qv = [q_ref[qs * b:(qs + 1) * b, :] for qs in range(nqs)]
        dov = [da_ref[qs * b:(qs + 1) * b, :] for qs in range(nqs)]
        tots = [jnp.sum(dov[qs].astype(F32) * a_ref[qs * b:(qs + 1) * b, :], axis=1, keepdims=True)
                for qs in range(nqs)]
        n_groups = ((qi + 1) * nqs - 1) // nsub + 1

        def step(it, carry, masked):
            c2s, dqs = carry
            g = n_groups - 1 - it
            off = pl.multiple_of(g * tk, tk)
            kg = k_ref[pl.ds(off, tk), :]
            vg = v_ref[pl.ds(off, tk), :]
            w_all = w_ref[0, g]
            new_c2, new_dq, dz_rows = [], [], []
            for qs in range(nqs):
                qb = qi * nqs + qs
                z2 = lax.dot_general(qv[qs], kg, (((1,), (1,)), ((), ())),
                                     preferred_element_type=F32) * (-scale * LOG2E)
                dw = lax.dot_general(dov[qs], vg, (((1,), (1,)), ((), ())), preferred_element_type=F32)
                beta = 1.0 / (1.0 + jnp.exp2(z2))
                e = dw * w_all[qs * b:(qs + 1) * b, :].astype(F32)
                run2 = c2s[qs]
                dzs = [None] * nsub
                for j in reversed(range(nsub)):
                    cols = slice(j * b, (j + 1) * b)
                    later = _split_dot(e[:, cols], tri_in) + run2
                    bj = beta[:, cols]
                    dz = (e[:, cols] * (1.0 - bj) - bj * (tots[qs] - later)) * scale
                    if masked:
                        dz = jnp.where(dcol < (qb - (g * nsub + j)) * b, dz, 0.0)
                    dzs[j] = dz.astype(BF16)
                    run2 = run2 + jnp.sum(e[:, cols], axis=1, keepdims=True)
                dzq = jnp.concatenate(dzs, axis=1)
                new_dq.append(dqs[qs] + jnp.dot(dzq, kg, preferred_element_type=F32))
                new_c2.append(run2)
                dz_rows.append(dzq)
            dz_all = jnp.concatenate(dz_rows, axis=0)
            dk_acc[pl.ds(off, tk), :] += lax.dot_general(dz_all, q_all, (((0,), (0,)), ((), ())),
                                                         preferred_element_type=F32)
            dv_acc[pl.ds(off, tk), :] += lax.dot_general(w_all, do_all, (((0,), (0,)), ((), ())),
                                                         preferred_element_type=F32)
            return tuple(new_c2), tuple(new_dq)

        zeros = tuple(jnp.zeros((b, 1), F32) for _ in range(nqs))
        assert all(((i + 1) * nqs - 1) // nsub * nsub <= i * nqs for i in range(s // tq))
        first = step(0, (zeros, tuple(jnp.zeros((b, HEAD_DIM), F32) for _ in range(nqs))), True)
        _, dqs = lax.fori_loop(1, n_groups, functools.partial(step, masked=False), first)
        for qs in range(nqs):
            dq_ref[qs * b:(qs + 1) * b, :] = dqs[qs].astype(BF16)

        @pl.when(qi == nq - 1)
        def _():
            dk_ref[...] = dk_acc[...].astype(BF16)
            dv_ref[...] = dv_acc[...].astype(BF16)

    blk = pl.BlockSpec((tq, HEAD_DIM), lambda h, i: (i, h))
    full = pl.BlockSpec((s, HEAD_DIM), lambda h, i: (0, h))
    return pl.pallas_call(
        body, name=name, grid=(h_n, nq),
        in_specs=[blk, pl.BlockSpec((s, HEAD_DIM), lambda h, i: (0, h_n + h)),
                  pl.BlockSpec((s, HEAD_DIM), lambda h, i: (0, 2 * h_n + h)), blk, blk,
                  pl.BlockSpec((1, s // tk, tq, tk), lambda h, i: (h, 0, i, 0))] + dep_specs,
        out_specs=[blk, full, full],
        out_shape=[jax.ShapeDtypeStruct((s, h_n * HEAD_DIM), BF16)] * 3,
        scratch_shapes=[pltpu.VMEM((s, HEAD_DIM), F32), pltpu.VMEM((s, HEAD_DIM), F32)],
        compiler_params=_params("parallel", "arbitrary"),
    )(p, p, p, a, da, wts, *dep_args)


def _pool_window(xx, win, r0, rc):
    cur = xx[HALO:HALO + rc]
    ws = _window_sum(xx, win, True)[HALO:HALO + rc]
    t_idx = r0 + lax.broadcasted_iota(jnp.int32, (rc, 1), 0)
    inv = 1.0 / jnp.minimum(win, t_idx + 1).astype(F32)
    return ws * inv - cur, inv


def even_mix_fwd(a, p, pool_w, pool_scale, name, rc=512, dep=None):
    s = p.shape[0]
    ng = len(POOL_WINDOWS)
    cw = pool_w.shape[1]
    n_chunks = s // rc
    dep_args, dep_specs = _after(dep)

    def body(a_ref, u_ref, g_ref, w_ref, sc_ref, *rest):
        y_ref, upad = rest[-2:]
        j = pl.program_id(0)

        @pl.when(j < ng)
        def _():
            def chunk(ci, carry):
                rows = pl.ds(pl.multiple_of(ci * rc, rc), rc)
                y_ref[rows, :] = (a_ref[rows, :] * _silu(g_ref[rows, :].astype(F32))).astype(BF16)
                return carry

            lax.fori_loop(0, n_chunks, chunk, 0)

        for gi, win in enumerate(POOL_WINDOWS):
            @pl.when(j == ng + gi)
            def _(win=win):
                upad[0:HALO, :] = jnp.zeros((HALO, cw), F32)

                def fill(ci, carry):
                    r0 = pl.multiple_of(ci * rc, rc)
                    upad[pl.ds(pl.multiple_of(r0 + HALO, HALO), rc), :] = u_ref[pl.ds(r0, rc), :].astype(F32)
                    return carry

                lax.fori_loop(0, n_chunks, fill, 0)

                def chunk(ci, carry):
                    r0 = pl.multiple_of(ci * rc, rc)
                    rows = pl.ds(r0, rc)
                    pooled, _ = _pool_window(upad[pl.ds(r0, HALO + rc), :], win, r0, rc)
                    t = jnp.dot(pooled.astype(BF16), w_ref[0], preferred_element_type=F32)
                    y_ref[rows, :] = (t * sc_ref[...] * _silu(g_ref[rows, :].astype(F32))).astype(BF16)
                    return carry

                lax.fori_loop(0, n_chunks, chunk, 0)

    grp = lambda j: jnp.maximum(j - ng, 0)
    return pl.pallas_call(
        body, name=name, grid=(2 * ng,),
        in_specs=[pl.BlockSpec((s, cw), lambda j: (0, jnp.minimum(j, ng - 1))),
                  pl.BlockSpec((s, cw), lambda j: (0, 3 * ng + grp(j))),
                  pl.BlockSpec((s, cw), lambda j: (0, 4 * ng + j)),
                  pl.BlockSpec((1, cw, cw), lambda j: (grp(j), 0, 0)),
                  pl.BlockSpec((1, cw), lambda j: (0, grp(j)))] + dep_specs,
        out_specs=pl.BlockSpec((s, cw), lambda j: (0, j)),
        out_shape=jax.ShapeDtypeStruct((s, 2 * ng * cw), BF16),
        scratch_shapes=[pltpu.VMEM((HALO + s, cw), F32)],
        compiler_params=_params("arbitrary"),
    )(a, p, p, pool_w, pool_scale, *dep_args)


def even_mix_bwd(dy, a, p, pool_w, pool_scale, name, rc=512):
    s = p.shape[0]
    ng = len(POOL_WINDOWS)
    cw = pool_w.shape[1]
    n_chunks = s // rc

    def body(dy_ref, a_ref, u_ref, g_ref, w_ref, sc_ref, da_ref, du_ref, dg_ref, dw_ref, dsc_ref,
             upad, rpad, dpl, dw_acc, dsc_acc):
        j = pl.program_id(0)

        @pl.when(j < ng)
        def _():
            def chunk(ci, carry):
                rows = pl.ds(pl.multiple_of(ci * rc, rc), rc)
                dyv = dy_ref[rows, :].astype(F32)
                sg, dsg = _silu_and_grad(g_ref[rows, :].astype(F32))
                da_ref[rows, :] = (dyv * sg).astype(BF16)
                dg_ref[rows, :] = (dyv * a_ref[rows, :] * dsg).astype(BF16)
                return carry

            lax.fori_loop(0, n_chunks, chunk, 0)

        for gi, win in enumerate(POOL_WINDOWS):
            @pl.when(j == ng + gi)
            def _(win=win):
                upad[0:HALO, :] = jnp.zeros((HALO, cw), F32)
                rpad[s:s + HALO, :] = jnp.zeros((HALO, cw), F32)
                dw_acc[...] = jnp.zeros_like(dw_acc)
                dsc_acc[...] = jnp.zeros_like(dsc_acc)

                def fill(ci, carry):
                    r0 = pl.multiple_of(ci * rc, rc)
                    upad[pl.ds(pl.multiple_of(r0 + HALO, HALO), rc), :] = u_ref[pl.ds(r0, rc), :].astype(F32)
                    return carry

                lax.fori_loop(0, n_chunks, fill, 0)

                def chunk(ci, carry):
                    r0 = pl.multiple_of(ci * rc, rc)
                    rows = pl.ds(r0, rc)
                    pooled, inv = _pool_window(upad[pl.ds(r0, HALO + rc), :], win, r0, rc)
                    pb = pooled.astype(BF16)
                    wv = w_ref[0]
                    t = jnp.dot(pb, wv, preferred_element_type=F32)
                    scv = sc_ref[...]
                    dyv = dy_ref[rows, :].astype(F32)
                    sg, dsg = _silu_and_grad(g_ref[rows, :].astype(F32))
                    dpo = dyv * sg
                    dg_ref[rows, :] = (dyv * t * scv * dsg).astype(BF16)
                    dsc_acc[...] += _rowsum8(dpo * t)
                    dtb = (dpo * scv).astype(BF16)
                    dw_acc[...] += lax.dot_general(pb, dtb, (((0,), (0,)), ((), ())),
                                                   preferred_element_type=F32)
                    dpooled = lax.dot_general(dtb, wv, (((1,), (1,)), ((), ())),
                                              preferred_element_type=F32)
                    dpl[rows, :] = dpooled
                    rpad[rows, :] = dpooled * inv
                    return carry

                lax.fori_loop(0, n_chunks, chunk, 0)

                def chunk2(ci, carry):
                    r0 = pl.multiple_of(ci * rc, rc)
                    rows = pl.ds(r0, rc)
                    xx = rpad[pl.ds(r0, rc + HALO), :]
                    fs = _window_sum(xx, win, False)[0:rc]
                    du_ref[rows, :] = (fs - dpl[rows, :]).astype(BF16)
                    return carry

                lax.fori_loop(0, n_chunks, chunk2, 0)
                dw_ref[0] = dw_acc[...]
                dsc_ref[...] = jnp.sum(dsc_acc[...], axis=0, keepdims=True)

    grp = lambda j: jnp.maximum(j - ng, 0)
    att = lambda j: jnp.minimum(j, ng - 1)
    return pl.pallas_call(
        body, name=name, grid=(2 * ng,),
        in_specs=[pl.BlockSpec((s, cw), lambda j: (0, j)),
                  pl.BlockSpec((s, cw), lambda j: (0, att(j))),
                  pl.BlockSpec((s, cw), lambda j: (0, 3 * ng + grp(j))),
                  pl.BlockSpec((s, cw), lambda j: (0, 4 * ng + j)),
                  pl.BlockSpec((1, cw, cw), lambda j: (grp(j), 0, 0)),
                  pl.BlockSpec((1, cw), lambda j: (0, grp(j)))],
        out_specs=[pl.BlockSpec((s, cw), lambda j: (0, att(j))),
                   pl.BlockSpec((s, cw), lambda j: (0, grp(j))),
                   pl.BlockSpec((s, cw), lambda j: (0, j)),
                   pl.BlockSpec((1, cw, cw), lambda j: (grp(j), 0, 0)),
                   pl.BlockSpec((1, cw), lambda j: (0, grp(j)))],
        out_shape=[jax.ShapeDtypeStruct((s, ng * cw), BF16), jax.ShapeDtypeStruct((s, ng * cw), BF16),
                   jax.ShapeDtypeStruct((s, 2 * ng * cw), BF16),
                   jax.ShapeDtypeStruct((ng, cw, cw), F32), jax.ShapeDtypeStruct((1, ng * cw), F32)],
        scratch_shapes=[pltpu.VMEM((HALO + s, cw), F32), pltpu.VMEM((s + HALO, cw), F32),
                        pltpu.VMEM((s, cw), F32), pltpu.VMEM((cw, cw), F32), pltpu.VMEM((8, cw), F32)],
        compiler_params=_params("arbitrary"),
    )(dy, a, p, p, pool_w, pool_scale)


def _halo_before(tm):
    return lambda i: jnp.maximum(i * (tm // HALO) - 1, 0)


def _halo_after(tm, s):
    return lambda i: jnp.minimum((i + 1) * (tm // HALO), s // HALO - 1)


def odd_mix_fwd(p, sconv_w, dconv_w, dconv_b, cnorm_g, cnorm_b, name, tm=128, dep=None):
    s = p.shape[0]
    cw = sconv_w.shape[1]
    n = s // tm
    lanes = 128
    hb = _halo_before(tm)

    dep_args, dep_specs = _after(dep)

    def body(hc_ref, hch_ref, bc_ref, cc_ref, cch_ref, ga_ref, gah_ref, gb_ref, gbh_ref, g1_ref, g2_ref,
             sw_ref, dw_ref, db_ref, gam_ref, bet_ref, *rest):
        y_ref, dc_ref = rest[-2:]
        first = pl.program_id(0) == 0
        for l in range(cw // lanes):
            cols = slice(l * lanes, (l + 1) * lanes)
            mh = jnp.where(first, 0.0, cch_ref[:, cols].astype(F32) * hch_ref[:, cols].astype(F32))
            mm = cc_ref[:, cols].astype(F32) * hc_ref[:, cols].astype(F32)
            xx = jnp.concatenate([mh, mm], axis=0)
            tap = _Taps(xx, tm, True)
            cv = jnp.zeros((tm, lanes), F32)
            for k in range(SCONV_K):
                cv = cv + sw_ref[k:k + 1, cols] * tap(SCONV_K - 1 - k)
            c_out = bc_ref[:, cols].astype(F32) * cv
            y_ref[:, cols] = (c_out * _silu(g1_ref[:, cols].astype(F32))).astype(BF16)
            dh = jnp.where(first, 0.0, gah_ref[:, cols].astype(F32) * _sigmoid(gbh_ref[:, cols].astype(F32)))
            dm = ga_ref[:, cols].astype(F32) * _sigmoid(gb_ref[:, cols].astype(F32))
            xx = jnp.concatenate([dh, dm], axis=0)
            tap = _Taps(xx, tm, True)
            acc = jnp.zeros((tm, lanes), F32) + db_ref[:, cols]
            for k in range(CONF_K):
                acc = acc + dw_ref[k:k + 1, cols] * tap(CONF_K - 1 - k)
            dc_ref[:, cols] = acc
        rs = 32
        for r in range(tm // rs):
            rows = slice(r * rs, (r + 1) * rs)
            xv = dc_ref[rows, :]
            mu = jnp.mean(xv, axis=-1, keepdims=True)
            xc = xv - mu
            rstd = lax.rsqrt(jnp.mean(xc * xc, axis=-1, keepdims=True) + EPS)
            ln = xc * rstd * gam_ref[...] + bet_ref[...]
            y_ref[rows, cw:2 * cw] = (_silu(ln) * _silu(g2_ref[rows, :].astype(F32))).astype(BF16)

    main = lambda c: pl.BlockSpec((tm, cw), lambda i: (i, c))
    halo = lambda c: pl.BlockSpec((HALO, cw), lambda i: (hb(i), c))
    vec = lambda r: pl.BlockSpec((r, cw), lambda i: (0, 0))
    return pl.pallas_call(
        body, name=name, grid=(n,),
        in_specs=[main(0), halo(0), main(1), main(2), halo(2), main(3), halo(3), main(4), halo(4),
                  main(5), main(6), vec(SCONV_K), vec(CONF_K), vec(1), vec(1), vec(1)] + dep_specs,
        out_specs=[pl.BlockSpec((tm, 2 * cw), lambda i: (i, 0)), pl.BlockSpec((tm, cw), lambda i: (i, 0))],
        out_shape=[jax.ShapeDtypeStruct((s, 2 * cw), BF16), jax.ShapeDtypeStruct((s, cw), F32)],
        compiler_params=_params("parallel"),
    )(p, p, p, p, p, p, p, p, p, p, p, sconv_w, dconv_w, dconv_b, cnorm_g, cnorm_b, *dep_args)


def odd_bwd_ln(dy, p, dc, cnorm_g, cnorm_b, name, tm=256):
    s = p.shape[0]
    cw = dc.shape[1]
    n = s // tm
    rs = 32

    def body(dy_ref, g2_ref, dc_ref, gam_ref, bet_ref, ddc_ref, dg_ref, dgam_ref, dbet_ref, gacc, bacc):
        i = pl.program_id(0)

        @pl.when(i == 0)
        def _():
            gacc[...] = jnp.zeros_like(gacc)
            bacc[...] = jnp.zeros_like(bacc)

        def chunk(ci, carry):
            rows = pl.ds(pl.multiple_of(ci * rs, rs), rs)
            xv = dc_ref[rows, :]
            mu = jnp.mean(xv, axis=-1, keepdims=True)
            xc = xv - mu
            rstd = lax.rsqrt(jnp.mean(xc * xc, axis=-1, keepdims=True) + EPS)
            xh = xc * rstd
            gam = gam_ref[...]
            sl, dsl = _silu_and_grad(xh * gam + bet_ref[...])
            sg, dsg = _silu_and_grad(g2_ref[rows, :].astype(F32))
            dyv = dy_ref[rows, :].astype(F32)
            dg_ref[rows, :] = (dyv * sl * dsg).astype(BF16)
            dln = dyv * sg * dsl
            gacc[...] += _rowsum8(dln * xh)
            bacc[...] += _rowsum8(dln)
            dxh = dln * gam
            ddc_ref[rows, :] = rstd * (dxh - jnp.mean(dxh, axis=-1, keepdims=True)
                                       - xh * jnp.mean(dxh * xh, axis=-1, keepdims=True))
            return carry

        lax.fori_loop(0, tm // rs, chunk, 0)

        @pl.when(i == n - 1)
        def _():
            dgam_ref[...] = jnp.sum(gacc[...], axis=0, keepdims=True)
            dbet_ref[...] = jnp.sum(bacc[...], axis=0, keepdims=True)

    vec = pl.BlockSpec((1, cw), lambda i: (0, 0))
    return pl.pallas_call(
        body, name=name, grid=(n,),
        in_specs=[pl.BlockSpec((tm, cw), lambda i: (i, 1)), pl.BlockSpec((tm, cw), lambda i: (i, 6)),
                  pl.BlockSpec((tm, cw), lambda i: (i, 0)), vec, vec],
        out_specs=[pl.BlockSpec((tm, cw), lambda i: (i, 0)), pl.BlockSpec((tm, cw), lambda i: (i, 0)), vec, vec],
        out_shape=[jax.ShapeDtypeStruct((s, cw), F32), jax.ShapeDtypeStruct((s, cw), BF16),
                   jax.ShapeDtypeStruct((1, cw), F32), jax.ShapeDtypeStruct((1, cw), F32)],
        scratch_shapes=[pltpu.VMEM((8, cw), F32), pltpu.VMEM((8, cw), F32)],
        compiler_params=_params("arbitrary"),
    )(dy, p, dc, cnorm_g, cnorm_b)


def odd_bwd_conv(dy, p, ddc, dg2, sconv_w, dconv_w, name, tm=128):
    s = p.shape[0]
    cw = ddc.shape[1]
    n = s // tm
    lanes = 128
    hb = _halo_before(tm)
    ha = _halo_after(tm, s)

    def body(dy_ref, dya_ref, g1_ref, g1a_ref, bc_ref, bca_ref, hc_ref, hch_ref, cc_ref, cch_ref,
             ddc_ref, ddca_ref, ga_ref, gah_ref, gb_ref, gbh_ref, dg2_ref, sw_ref, dw_ref,
             dp_ref, dsw_ref, ddw_ref, ddb_ref, sw_acc, dw_acc, db_acc):
        i = pl.program_id(0)
        first = i == 0
        last = i == n - 1

        @pl.when(first)
        def _():
            sw_acc[...] = jnp.zeros_like(sw_acc)
            dw_acc[...] = jnp.zeros_like(dw_acc)
            db_acc[...] = jnp.zeros_like(db_acc)

        for l in range(cw // lanes):
            cols = slice(l * lanes, (l + 1) * lanes)
            mh = jnp.where(first, 0.0, cch_ref[:, cols].astype(F32) * hch_ref[:, cols].astype(F32))
            hcv = hc_ref[:, cols].astype(F32)
            ccv = cc_ref[:, cols].astype(F32)
            xx = jnp.concatenate([mh, ccv * hcv], axis=0)
            tap = _Taps(xx, tm, True)
            taps = [tap(SCONV_K - 1 - k) for k in range(SCONV_K)]
            cv = jnp.zeros((tm, lanes), F32)
            for k in range(SCONV_K):
                cv = cv + sw_ref[k:k + 1, cols] * taps[k]
            bcv = bc_ref[:, cols].astype(F32)
            dyv = dy_ref[:, cols].astype(F32)
            sg, dsg = _silu_and_grad(g1_ref[:, cols].astype(F32))
            dco = dyv * sg
            dp_ref[:, 5 * cw + l * lanes:5 * cw + (l + 1) * lanes] = (dyv * bcv * cv * dsg).astype(BF16)
            dp_ref[:, cw + l * lanes:cw + (l + 1) * lanes] = (dco * cv).astype(BF16)
            dcv = dco * bcv
            for k in range(SCONV_K):
                sw_acc[k * 8:(k + 1) * 8, cols] += _rowsum8(dcv * taps[k])
            dcv_a = jnp.where(last, 0.0, dya_ref[:, cols].astype(F32) * _silu(g1a_ref[:, cols].astype(F32))
                              * bca_ref[:, cols].astype(F32))
            xx = jnp.concatenate([dcv, dcv_a], axis=0)
            tap = _Taps(xx, tm, False)
            dm = jnp.zeros((tm, lanes), F32)
            for k in range(SCONV_K):
                dm = dm + sw_ref[k:k + 1, cols] * tap(SCONV_K - 1 - k)
            dp_ref[:, l * lanes:(l + 1) * lanes] = (dm * ccv).astype(BF16)
            dp_ref[:, 2 * cw + l * lanes:2 * cw + (l + 1) * lanes] = (dm * hcv).astype(BF16)
            gav = ga_ref[:, cols].astype(F32)
            sb = _sigmoid(gb_ref[:, cols].astype(F32))
            dh = jnp.where(first, 0.0, gah_ref[:, cols].astype(F32) * _sigmoid(gbh_ref[:, cols].astype(F32)))
            xx = jnp.concatenate([dh, gav * sb], axis=0)
            ddcv = ddc_ref[:, cols]
            db_acc[:, cols] += _rowsum8(ddcv)
            tap = _Taps(xx, tm, True)
            for k in range(CONF_K):
                dw_acc[k * 8:(k + 1) * 8, cols] += _rowsum8(ddcv * tap(CONF_K - 1 - k))
            ddc_a = jnp.where(last, 0.0, ddca_ref[:, cols])
            xx = jnp.concatenate([ddcv, ddc_a], axis=0)
            tap = _Taps(xx, tm, False)
            dgl = jnp.zeros((tm, lanes), F32)
            for k in range(CONF_K):
                dgl = dgl + dw_ref[k:k + 1, cols] * tap(CONF_K - 1 - k)
            dp_ref[:, 3 * cw + l * lanes:3 * cw + (l + 1) * lanes] = (dgl * sb).astype(BF16)
            dp_ref[:, 4 * cw + l * lanes:4 * cw + (l + 1) * lanes] = (dgl * gav * sb * (1.0 - sb)).astype(BF16)
        dp_ref[:, 6 * cw:7 * cw] = dg2_ref[...]

        @pl.when(last)
        def _():
            for k in range(SCONV_K):
                dsw_ref[k:k + 1, :] = jnp.sum(sw_acc[k * 8:(k + 1) * 8, :], axis=0, keepdims=True)
            for k in range(CONF_K):
                ddw_ref[k:k + 1, :] = jnp.sum(dw_acc[k * 8:(k + 1) * 8, :], axis=0, keepdims=True)
            ddb_ref[...] = jnp.sum(db_acc[...], axis=0, keepdims=True)

    def main(c):
        return pl.BlockSpec((tm, cw), lambda i: (i, c))

    def before(c):
        return pl.BlockSpec((HALO, cw), lambda i: (hb(i), c))

    def after(c):
        return pl.BlockSpec((HALO, cw), lambda i: (ha(i), c))

    def vec(r):
        return pl.BlockSpec((r, cw), lambda i: (0, 0))

    return pl.pallas_call(
        body, name=name, grid=(n,),
        in_specs=[main(0), after(0), main(5), after(5), main(1), after(1), main(0), before(0), main(2), before(2),
                  main(0), after(0), main(3), before(3), main(4), before(4), main(0), vec(SCONV_K), vec(CONF_K)],
        out_specs=[pl.BlockSpec((tm, 7 * cw), lambda i: (i, 0)), vec(SCONV_K), vec(CONF_K), vec(1)],
        out_shape=[jax.ShapeDtypeStruct((s, 7 * cw), BF16), jax.ShapeDtypeStruct((SCONV_K, cw), F32),
                   jax.ShapeDtypeStruct((CONF_K, cw), F32), jax.ShapeDtypeStruct((1, cw), F32)],
        scratch_shapes=[pltpu.VMEM((8 * SCONV_K, cw), F32), pltpu.VMEM((8 * CONF_K, cw), F32),
                        pltpu.VMEM((8, cw), F32)],
        compiler_params=_params("arbitrary"),
    )(dy, dy, p, p, p, p, p, p, p, p, ddc, ddc, p, p, p, p, dg2, sconv_w, dconv_w)


_ANY = pl.BlockSpec(memory_space=pl.ANY)


def _place():
    return lax.axis_index("x"), lax.axis_index("y"), lax.axis_index("c")


def all_gather(arrs, name, deps=()):
    n = len(arrs)

    def body(*refs):
        ins, outs = refs[:n], refs[n + len(deps):2 * n + len(deps)]
        send_sems, recv_sems, local_sems = refs[-3:]
        x, y, c = _place()
        me, sibling = (x, y, c), (x, y, 1 - c)
        chips = [(1 - x, y), (x, 1 - y), (1 - x, 1 - y)]

        def copy(a, k, block, to, src=None):
            px, py, pc = block
            dst = outs[a].at[4 * px + 2 * py + pc]
            return pltpu.make_async_remote_copy(
                src_ref=dst if src is None else src, dst_ref=dst,
                send_sem=send_sems.at[7 * a + k], recv_sem=recv_sems.at[7 * a + k],
                device_id=to, device_id_type=MESH)

        mine = [pltpu.make_async_copy(ins[a], outs[a].at[4 * x + 2 * y + c], local_sems.at[a]) for a in range(n)]
        first = []
        for a in range(n):
            first.append(copy(a, 0, me, sibling, src=ins[a]))
            first += [copy(a, 1 + j, me, (*chip, c), src=ins[a]) for j, chip in enumerate(chips)]
        for cp in first + mine:
            cp.start()
        passed = []
        for a in range(n):
            for j, chip in enumerate(chips):
                copy(a, 1 + j, (*chip, c), me).wait_recv()
                cp = copy(a, 4 + j, (*chip, c), sibling)
                cp.start()
                passed.append(cp)
        for a in range(n):
            copy(a, 0, sibling, me).wait_recv()
            for j, chip in enumerate(chips):
                copy(a, 4 + j, (*chip, 1 - c), me).wait_recv()
        for cp in first + passed:
            cp.wait_send()
        for cp in mine:
            cp.wait()

    return pl.pallas_call(
        body, name=name,
        out_shape=[jax.ShapeDtypeStruct((N_DEV,) + a.shape, a.dtype) for a in arrs],
        in_specs=[_ANY] * (n + len(deps)), out_specs=[_ANY] * n,
        scratch_shapes=[pltpu.SemaphoreType.DMA((7 * n,)), pltpu.SemaphoreType.DMA((7 * n,)),
                        pltpu.SemaphoreType.DMA((n,))],
    )(*arrs, *deps)


def in_proj_gathered(xs, g, w_own, extras, name, tm=512):
    s, d = xs.shape
    n = w_own.shape[1]
    arrs = [w_own] + list(extras)
    na = len(arrs)
    tr = 256

    def body(*refs):
        x_ref, g_ref, ins = refs[0], refs[1], refs[2:2 + na]
        h_out, p_ref, outs = refs[2 + na], refs[3 + na], refs[4 + na:4 + 2 * na]
        (h_ref, xbuf, wbuf, obuf, send_sems, recv_sems, load_sem, store_sems, own_sems, h_sem,
         x_sems) = refs[4 + 2 * na:]
        x, y, c = _place()
        me, sibling = (x, y, c), (x, y, 1 - c)
        x_first = c == 0
        near = (jnp.where(x_first, 1 - x, x), jnp.where(x_first, y, 1 - y))
        far = (jnp.where(x_first, x, 1 - x), jnp.where(x_first, 1 - y, y))
        diag = (1 - x, 1 - y)
        k_near, k_far = jnp.where(x_first, 1, 2), jnp.where(x_first, 2, 1)
        f_near, f_far = k_near + 3, k_far + 3

        def slot(block):
            return 4 * block[0] + 2 * block[1] + block[2]

        def copy(a, k, block, to, src=None):
            dst = outs[a].at[slot(block)]
            return pltpu.make_async_remote_copy(
                src_ref=dst if src is None else src, dst_ref=dst,
                send_sem=send_sems.at[7 * a + k], recv_sem=recv_sems.at[7 * a + k],
                device_id=to, device_id_type=MESH)

        first = []
        for a in range(na):
            first += [copy(a, 0, me, sibling, src=ins[a]), copy(a, 1, me, (1 - x, y, c), src=ins[a]),
                      copy(a, 2, me, (x, 1 - y, c), src=ins[a])]
        for cp in first:
            cp.start()
        own = pltpu.make_async_copy(wbuf.at[0], outs[0].at[slot(me)], own_sems.at[0])
        mine = [pltpu.make_async_copy(ins[a], outs[a].at[slot(me)], own_sems.at[a]) for a in range(1, na)]
        stores = [None, None]

        def x_load(i):
            return pltpu.make_async_copy(x_ref.at[pl.ds(i * tr, tr), :], xbuf.at[i % 2], x_sems.at[i % 2])

        x_load(0).start()
        for i in range(s // tr):
            if i + 1 < s // tr:
                x_load(i + 1).start()
            x_load(i).wait()
            xv = xbuf[i % 2]
            r = lax.rsqrt(jnp.mean(xv * xv, axis=-1, keepdims=True) + EPS)
            h_ref[i * tr:(i + 1) * tr, :] = (xv * r * g_ref[...]).astype(BF16)
        h_store = pltpu.make_async_copy(h_ref, h_out, h_sem)
        h_store.start()

        def multiply(k, block, w_from):
            b = k % 2
            if k == 2:
                own.wait()
            load = pltpu.make_async_copy(w_from, wbuf.at[b], load_sem)
            load.start()
            if stores[b] is not None:
                stores[b].wait()
            load.wait()
            if k == 0:
                own.start()

            def chunk(i, carry):
                rows = pl.ds(pl.multiple_of(i * tm, tm), tm)
                obuf[b, rows, :] = jnp.dot(h_ref[rows, :], wbuf[b], preferred_element_type=F32).astype(BF16)
                return carry

            lax.fori_loop(0, s // tm, chunk, 0)
            stores[b] = pltpu.make_async_copy(
                obuf.at[b], p_ref.at[:, pl.ds(pl.multiple_of(slot(block) * n, 128), n)], store_sems.at[b])
            stores[b].start()

        passed = []

        def arrive(a, k, block):
            copy(a, k, block, me).wait_recv()

        def pass_on(a, k, block, to):
            cp = copy(a, k, block, to)
            cp.start()
            passed.append(cp)

        def gather(a, use):
            use(0, me)
            arrive(a, 0, sibling)
            use(1, sibling)
            arrive(a, k_near, (*near, c))
            pass_on(a, 3, (*near, c), (*far, c))
            pass_on(a, f_near, (*near, c), sibling)
            use(2, (*near, c))
            arrive(a, f_far, (*far, 1 - c))
            use(3, (*far, 1 - c))
            arrive(a, k_far, (*far, c))
            pass_on(a, f_far, (*far, c), sibling)
            use(4, (*far, c))
            arrive(a, f_near, (*near, 1 - c))
            use(5, (*near, 1 - c))
            arrive(a, 3, (*diag, c))
            pass_on(a, 6, (*diag, c), sibling)
            use(6, (*diag, c))
            arrive(a, 6, (*diag, 1 - c))
            use(7, (*diag, 1 - c))

        gather(0, lambda k, block: multiply(k, block, ins[0] if k == 0 else outs[0].at[slot(block)]))
        for cp in mine:
            cp.start()
        for a in range(1, na):
            gather(a, lambda k, block: None)
        for cp in first + passed:
            cp.wait_send()
        for cp in mine + stores + [h_store]:
            cp.wait()

    vmem = pl.BlockSpec(memory_space=pltpu.VMEM)
    outs = pl.pallas_call(
        body, name=name,
        out_shape=[jax.ShapeDtypeStruct((s, d), BF16), jax.ShapeDtypeStruct((s, N_DEV * n), BF16)]
        + [jax.ShapeDtypeStruct((N_DEV,) + a.shape, a.dtype) for a in arrs],
        in_specs=[_ANY, vmem] + [_ANY] * na, out_specs=[_ANY] * (2 + na),
        scratch_shapes=[pltpu.VMEM((s, d), BF16), pltpu.VMEM((2, tr, d), F32), pltpu.VMEM((2, d, n), BF16),
                        pltpu.VMEM((2, s, n), BF16),
                        pltpu.SemaphoreType.DMA((7 * na,)), pltpu.SemaphoreType.DMA((7 * na,)),
                        pltpu.SemaphoreType.DMA, pltpu.SemaphoreType.DMA((2,)), pltpu.SemaphoreType.DMA((na,)),
                        pltpu.SemaphoreType.DMA, pltpu.SemaphoreType.DMA((2,))],
        compiler_params=pltpu.CompilerParams(vmem_limit_bytes=VMEM_LIMIT),
    )(xs, g, *arrs)
    return outs[0], outs[1], outs[2], outs[3:]


_HBM = pl.BlockSpec(memory_space=pltpu.HBM)
_SEM = pl.BlockSpec(memory_space=pltpu.SEMAPHORE)
_DATAFLOW = pltpu.SideEffectType.DATAFLOW_SIDE_EFFECTING


def _peers_per_array(kind):
    return 1 if kind in ("sibling", "halves") else 3


def _split_copies(kind, srcs, lands, send_sems, recv_sems):
    x, y, c = _place()
    per = _peers_per_array(kind)
    out = []
    for a in range(len(lands)):
        if kind == "sibling":
            part = srcs[a] if srcs[a].shape[1] == 1 else srcs[a].at[:, pl.ds(1 - c, 1)]
            peers = [((x, y, 1 - c), part, lands[a], lands[a])]
        elif kind == "halves":
            mine, its = lands[a].at[:, pl.ds(c, 1)], lands[a].at[:, pl.ds(1 - c, 1)]
            peers = [((x, y, 1 - c), mine, mine, its)]
        else:
            peers = []
            for px, py in [(1 - x, y), (x, 1 - y), (1 - x, 1 - y)]:
                if kind == "gather":
                    views = (srcs[a], lands[a].at[4 * x + 2 * y + c], lands[a].at[4 * px + 2 * py + c])
                else:
                    views = (srcs[a].at[2 * px + py], lands[a].at[2 * x + y], lands[a].at[2 * px + py])
                peers.append(((px, py, c),) + views)
        for j, (peer, src, dst, arrives) in enumerate(peers):
            sems = dict(send_sem=send_sems.at[per * a + j], recv_sem=recv_sems.at[per * a + j],
                        device_id=peer, device_id_type=MESH)
            out.append((pltpu.make_async_remote_copy(src_ref=src, dst_ref=dst, **sems),
                        pltpu.make_async_remote_copy(src_ref=src, dst_ref=arrives, **sems)))
    return out


def split_start(kind, srcs, lands, deps, name):
    ns, nl = len(srcs), len(lands)
    n_sems = _peers_per_array(kind) * nl
    held = list(srcs) + list(lands)

    def body(*refs):
        send_sems, recv_sems = refs[len(held) + len(deps)], refs[len(held) + len(deps) + 1]
        for copy, _ in _split_copies(kind, refs[:ns], refs[ns:ns + nl], send_sems, recv_sems):
            copy.start()
        token = refs[-1]
        token[...] = jnp.zeros_like(token)

    outs = pl.pallas_call(
        body, name=name,
        out_shape=(pltpu.SemaphoreType.DMA((n_sems,)), pltpu.SemaphoreType.DMA((n_sems,)),
                   *[pltpu.HBM(a.shape, a.dtype) for a in held], jax.ShapeDtypeStruct((8, 128), F32)),
        in_specs=[_HBM] * len(held) + [_ANY] * len(deps),
        out_specs=(_SEM, _SEM, *([_HBM] * len(held)), pl.BlockSpec(memory_space=pltpu.VMEM)),
        input_output_aliases={i: 2 + i for i in range(len(held))},
        compiler_params=pltpu.CompilerParams(has_side_effects=_DATAFLOW),
    )(*[pltpu.with_memory_space_constraint(a, pltpu.HBM) for a in held], *deps)
    return outs[0], outs[1], list(outs[2:2 + ns]), list(outs[2 + ns:2 + ns + nl]), outs[-1]


def split_wait(kind, send_sems, recv_sems, srcs, lands, afters, name):
    ns, nl = len(srcs), len(lands)
    held = list(srcs) + list(lands)

    def body(*refs):
        for _, arrival in _split_copies(kind, refs[:ns], refs[ns:ns + nl], refs[ns + nl], refs[ns + nl + 1]):
            arrival.wait_send()
            arrival.wait_recv()

    outs = pl.pallas_call(
        body, name=name,
        out_shape=[pltpu.HBM(a.shape, a.dtype) for a in held],
        in_specs=[_HBM] * len(held) + [_SEM, _SEM] + [_ANY] * len(afters),
        out_specs=[_HBM] * len(held),
        input_output_aliases={i: i for i in range(len(held))},
        compiler_params=pltpu.CompilerParams(has_side_effects=_DATAFLOW),
    )(*held, send_sems, recv_sems, *afters)
    return list(outs[:ns]), list(outs[ns:])


def place_block(land, block, dev, name):
    r, c = block.shape
    tr = min(r, 512)

    def body(dev_ref, land_ref, b_ref, o_ref):
        del dev_ref, land_ref
        o_ref[...] = b_ref[...]

    return pl.pallas_call(
        body, name=name,
        grid_spec=pltpu.PrefetchScalarGridSpec(
            num_scalar_prefetch=1, grid=(r // tr,),
            in_specs=[_ANY, pl.BlockSpec((tr, c), lambda i, dev_ref: (i, 0))],
            out_specs=pl.BlockSpec((None, tr, c), lambda i, dev_ref: (dev_ref[0], i, 0))),
        out_shape=jax.ShapeDtypeStruct(land.shape, land.dtype),
        input_output_aliases={1: 0},
        compiler_params=_params("parallel"),
    )(dev, land, block)


def pair_add(own, recv, core, name):
    _, _, r, c = own.shape
    tr = min(r, 512)

    def body(core_ref, own_ref, recv_ref, o_ref):
        del core_ref
        o_ref[...] = (own_ref[...].astype(F32) + recv_ref[...].astype(F32)).astype(BF16)

    return pl.pallas_call(
        body, name=name,
        grid_spec=pltpu.PrefetchScalarGridSpec(
            num_scalar_prefetch=1, grid=(4, r // tr),
            in_specs=[pl.BlockSpec((None, None, tr, c), lambda k, i, core_ref: (k, core_ref[0], i, 0)),
                      pl.BlockSpec((None, None, tr, c), lambda k, i, core_ref: (k, 0, i, 0))],
            out_specs=pl.BlockSpec((None, tr, c), lambda k, i, core_ref: (k, i, 0))),
        out_shape=jax.ShapeDtypeStruct((4, r, c), BF16),
        compiler_params=_params("parallel", "parallel"),
    )(core, own, recv)


def _adamw_math(w, g, m, v):
    m2 = ADAM_B1 * m + (1.0 - ADAM_B1) * g
    v2 = ADAM_B2 * v + (1.0 - ADAM_B2) * (g * g)
    m_hat = m2 / (1.0 - ADAM_B1 ** ADAM_STEP)
    v_hat = v2 / (1.0 - ADAM_B2 ** ADAM_STEP)
    delta = -ADAM_LR * (m_hat / (jnp.sqrt(v_hat) + ADAM_EPS) + ADAM_WD * w)
    return delta, m2, v2


def adamw_big(w, m, v, own, got, chip, name):
    r, c = w.shape
    tr = min(r, 256)

    def body(chip_ref, w_ref, m_ref, v_ref, p0, p1, p2, p3, g_ref, d_ref, m2_ref, v2_ref):
        del chip_ref
        g = ((p0[...].astype(F32) + p1[...].astype(F32)) + p2[...].astype(F32)) + p3[...].astype(F32)
        delta, m2, v2 = _adamw_math(w_ref[...], g, m_ref[...], v_ref[...])
        g_ref[...] = g
        d_ref[...] = delta
        m2_ref[...] = m2
        v2_ref[...] = v2

    row = pl.BlockSpec((tr, c), lambda i, chip_ref: (i, 0))

    def slab(flip):
        return pl.BlockSpec((None, tr, c), lambda i, chip_ref: (chip_ref[0] ^ flip, i, 0))

    return pl.pallas_call(
        body, name=name,
        grid_spec=pltpu.PrefetchScalarGridSpec(
            num_scalar_prefetch=1, grid=(r // tr,),
            in_specs=[row, row, row, slab(0), slab(1), slab(2), slab(3)],
            out_specs=[row] * 4),
        out_shape=[jax.ShapeDtypeStruct((r, c), F32)] * 4,
        compiler_params=_params("parallel"),
    )(chip, w, m, v, own, got, got, got)


def sum_devices(g8, name):
    def body(g_ref, o_ref):
        tot = g_ref[0]
        for k in range(1, N_DEV):
            tot = tot + g_ref[k]
        o_ref[...] = tot

    return pl.pallas_call(body, name=name, out_shape=jax.ShapeDtypeStruct(g8.shape[1:], F32))(g8)


def adamw_small(ws, gs, ms, vs, name):
    n = len(ws)

    def body(*refs):
        w_r, g_r, m_r, v_r = refs[:n], refs[n:2 * n], refs[2 * n:3 * n], refs[3 * n:4 * n]
        d_o, m_o, v_o = refs[4 * n:5 * n], refs[5 * n:6 * n], refs[6 * n:7 * n]
        for k in range(n):
            delta, m2, v2 = _adamw_math(w_r[k][...], g_r[k][...], m_r[k][...], v_r[k][...])
            d_o[k][...] = delta
            m_o[k][...] = m2
            v_o[k][...] = v2

    shapes = [jax.ShapeDtypeStruct(w.shape, F32) for w in ws]
    outs = pl.pallas_call(body, name=name, out_shape=shapes * 3)(*ws, *gs, *ms, *vs)
    return outs[:n], outs[n:2 * n], outs[2 * n:]


def _rows128(a):
    return a.reshape(-1, 128)


def _pad_rows(a, rows):
    return jnp.pad(a, ((0, rows - a.shape[0]), (0, 0)))


def kernel(x, ln_pre_even, w_in_even, pool_w, pool_scale, w_out_even, ln_post_even, ln_pre_odd, w_in_odd, sconv_w, dconv_w, dconv_b, cnorm_g, cnorm_b, w_out_odd, ln_post_odd, loss_target, m_ln_pre_even, m_w_in_even, m_pool_w, m_pool_scale, m_w_out_even, m_ln_post_even, m_ln_pre_odd, m_w_in_odd, m_sconv_w, m_dconv_w, m_dconv_b, m_cnorm_g, m_cnorm_b, m_w_out_odd, m_ln_post_odd, v_ln_pre_even, v_w_in_even, v_pool_w, v_pool_scale, v_w_out_even, v_ln_post_even, v_ln_pre_odd, v_w_in_odd, v_sconv_w, v_dconv_w, v_dconv_b, v_cnorm_g, v_cnorm_b, v_w_out_odd, v_ln_post_odd):
    xs = x[0]
    tgt = loss_target[0]
    s, d = xs.shape
    half = d // 2
    n_heads = half // HEAD_DIM
    ng = len(POOL_WINDOWS)
    cwp = half // ng
    dev = 4 * lax.axis_index("x") + 2 * lax.axis_index("y") + lax.axis_index("c")
    core = lax.axis_index("c").astype(jnp.int32).reshape(1)

    pr = pool_w.shape[2]
    cl = sconv_w.shape[2]
    small_parts = [(_rows128(ln_pre_odd), 8), (sconv_w[0], 8), (dconv_w[0], 32), (dconv_b, 8),
                   (cnorm_g, 8), (cnorm_b, 8), (_rows128(ln_post_odd), 8)]
    small_local = jnp.concatenate([_pad_rows(a, r) for a, r in small_parts], axis=0)
    h0, p0, g_wie, (g_pw, g_small) = in_proj_gathered(
        xs, ln_pre_even, w_in_even[0].astype(BF16), [pool_w[0].reshape(ng * pr, cwp).astype(BF16), small_local],
        "ag_in_proj_even")
    comm = _Exchanges(dev, core, d)
    token = comm.start_weights("out_even", [w_out_even[0].astype(BF16)], [p0])
    token = comm.start_weights("in_odd", [w_in_odd[0].astype(BF16)], [token])
    sb_dep = comm.start_weights("out_odd", [w_out_odd[0].astype(BF16)], [token])
    pool_full = g_pw.reshape(N_DEV, ng, pr, cwp).transpose(1, 0, 2, 3).reshape(ng, cwp, cwp)
    nl = ln_pre_odd.shape[1] // 128

    def chan(lo, rows):
        return g_small[:, lo:lo + rows].transpose(1, 0, 2).reshape(rows, N_DEV * cl)

    ln_pre_odd_f = g_small[:, 0:nl].reshape(1, d)
    sconv_f = chan(8, SCONV_K)
    dconv_f = chan(16, CONF_K)
    dconv_b_f = chan(48, 1)
    cnorm_g_f = chan(56, 1)
    cnorm_b_f = chan(64, 1)
    ln_post_odd_f = g_small[:, 72:72 + nl].reshape(1, d)

    loss_blk, grad_x, small_g = _fwd_bwd(
        xs, tgt, ln_pre_even, h0, p0, g_wie, pool_full, pool_scale, ln_post_even, ln_pre_odd_f,
        sconv_f, dconv_f, dconv_b_f, cnorm_g_f, cnorm_b_f, ln_post_odd_f, comm, sb_dep)
    small_w = [ln_pre_even, pool_scale, ln_post_even, ln_pre_odd, sconv_w[0], dconv_w[0], dconv_b, cnorm_g, cnorm_b, ln_post_odd]
    small_m = [m_ln_pre_even, m_pool_scale, m_ln_post_even, m_ln_pre_odd, m_sconv_w[0], m_dconv_w[0], m_dconv_b, m_cnorm_g, m_cnorm_b, m_ln_post_odd]
    small_v = [v_ln_pre_even, v_pool_scale, v_ln_post_even, v_ln_pre_odd, v_sconv_w[0], v_dconv_w[0], v_dconv_b, v_cnorm_g, v_cnorm_b, v_ln_post_odd]
    big = {"w_in_even": (w_in_even, m_w_in_even, v_w_in_even), "pool_w": (pool_w, m_pool_w, v_pool_w),
           "w_out_even": (w_out_even, m_w_out_even, v_w_out_even), "w_in_odd": (w_in_odd, m_w_in_odd, v_w_in_odd),
           "w_out_odd": (w_out_odd, m_w_out_odd, v_w_out_odd)}
    upd = comm.finish_updates(big, [grad_x])
    upd.update(comm.finish_updates(big, [grad_x]))
    sg, sd, sm, sv, loss = _update_small(small_g, loss_blk, small_w, small_m, small_v, dev, d, cl,
                                         deps=[upd["w_in_odd"][1], upd["w_out_even"][1]])
    upd.update(comm.finish_updates(big, sd))
    (g_wie_o, d_wie, m_wie, v_wie), (g_pw_o, d_pw, m_pw, v_pw) = upd["w_in_even"], upd["pool_w"]
    (g_woe_o, d_woe, m_woe, v_woe), (g_wio_o, d_wio, m_wio, v_wio) = upd["w_out_even"], upd["w_in_odd"]
    g_woo_o, d_woo, m_woo, v_woo = upd["w_out_odd"]

    def order(small, wie, pw, woe, wio, woo):
        return [small[0], wie, pw, small[1], woe, small[2], small[3], wio, small[4], small[5], small[6],
                small[7], small[8], woo, small[9]]

    grads = order(sg, g_wie_o, g_pw_o, g_woe_o, g_wio_o, g_woo_o)
    deltas = order(sd, d_wie, d_pw, d_woe, d_wio, d_woo)
    new_m = order(sm, m_wie, m_pw, m_woe, m_wio, m_woo)
    new_v = order(sv, v_wie, v_pw, v_woe, v_wio, v_woo)
    return (loss, grad_x[None], *grads, *deltas, *new_m, *new_v)


def _fwd_bwd(xs, tgt, ln_pre_even, h0, p0, g_wie, pool_full, pool_scale, ln_post_even, ln_pre_odd_f,
             sconv_f, dconv_f, dconv_b_f, cnorm_g_f, cnorm_b_f, ln_post_odd_f, comm, sb_dep):
    d = xs.shape[1]
    n_heads = d // 2 // HEAD_DIM
    ng, cwp = pool_full.shape[0], pool_full.shape[1]
    a0, sb_wts = sb_fwd(p0, n_heads, "sb_fwd", dep=sb_dep)
    dep = comm.weights_arrived("out_even", after=a0)
    y0 = even_mix_fwd(a0, p0, pool_full, pool_scale, "even_mix_fwd", dep=dep)
    (w_out_e,) = comm.weights("out_even", after=y0)
    w_out_e = w_out_e.reshape(1, d, d)
    o0 = mm_nn(y0, w_out_e, BF16, "out_proj_even", tn=512)
    dep = comm.weights_arrived("in_odd", after=o0)
    x1, h1 = postnorm_fwd(xs, o0, ln_post_even, ln_pre_odd_f, "post_even", dep=dep)
    (g_wio,) = comm.weights("in_odd", after=x1)
    p1 = mm_nn(h1, g_wio, BF16, "in_proj_odd", group=2)
    dep = comm.weights_arrived("out_odd", after=p1)
    y1, dc = odd_mix_fwd(p1, sconv_f, dconv_f, dconv_b_f, cnorm_g_f, cnorm_b_f, "odd_mix_fwd", dep=dep)
    (w_out_o,) = comm.weights("out_odd", after=y1)
    w_out_o = w_out_o.reshape(1, d, d)
    o1 = mm_nn(y1, w_out_o, BF16, "out_proj_odd", tn=512)
    loss_blk, gx2, do1, dg_post_odd = final_fwd_bwd(x1, o1, ln_post_odd_f, tgt, "post_odd_loss")

    dw_out_o = mm_tn(y1, do1, 1, BF16, "dw_out_odd")
    dy1 = mm_nt(do1, w_out_o, BF16, "dy_odd")
    ddc, dg2, dgam, dbet = odd_bwd_ln(dy1, p1, dc, cnorm_g_f, cnorm_b_f, "odd_bwd_ln")
    dp1, dsconv, ddconv, ddconv_b = odd_bwd_conv(dy1, p1, ddc, dg2, sconv_f, dconv_f, "odd_bwd_conv")
    dw_in_o = mm_tn(h1, dp1, N_DEV, BF16, "dw_in_odd", group=2)
    dep = comm.reduce_begin({"w_out_odd": dw_out_o.reshape(N_DEV, d // N_DEV, d), "w_in_odd": dw_in_o}, "odd")
    dh1 = mm_nt(dp1, g_wio, BF16, "dh_odd", dep=dep, group=2)
    dep = comm.reduce_send(after=dh1)
    gx1, dg_pre_odd, do0, dg_post_even = norm_bwd(dh1, x1, ln_pre_odd_f, gx2, "pre_odd_post_even_bwd",
                                                  inp2=o0, g2=ln_post_even, dep=dep)

    dw_out_e = mm_tn(y0, do0, 1, BF16, "dw_out_even")
    dy0 = mm_nt(do0, w_out_e, BF16, "dy_even")
    da0, du0, dg0, dpool, dpool_scale = even_mix_bwd(dy0, a0, p0, pool_full, pool_scale, "even_mix_bwd")
    pr = cwp // N_DEV
    dpool_slabs = dpool.astype(BF16).reshape(ng, N_DEV, pr, cwp).transpose(1, 0, 2, 3).reshape(N_DEV, ng * pr, cwp)
    dep = comm.reduce_begin({"w_out_even": dw_out_e.reshape(N_DEV, d // N_DEV, d), "pool_w": dpool_slabs}, "even_out")
    dq0, dk0, dv0 = sb_bwd(p0, a0, sb_wts, da0, n_heads, "sb_bwd", dep=dep)
    dep = comm.reduce_send(after=dq0)
    dp0 = jnp.concatenate([dq0, dk0, dv0, du0, dg0], axis=1)
    dw_sibling = mm_tn(h0, dp0, N_DEV // 2, BF16, "dw_in_even_sibling", dep=dep, pick=(2, 1 - comm.core))
    dep = comm.reduce_begin({"w_in_even": dw_sibling}, "even_in", sibling_part=True)
    dw_own = mm_tn(h0, dp0, N_DEV // 2, BF16, "dw_in_even_own", dep=dep, pick=(2, comm.core))
    dep = comm.reduce_send(after=dw_own, own_part={"w_in_even": dw_own})
    dh0 = mm_nt(dp0, g_wie, BF16, "dh_even", dep=dep, group=2)
    dep = None
    grad_x, dg_pre_even = norm_bwd(dh0, xs, ln_pre_even, gx1, "pre_even_bwd", tm=512, dep=dep)
    small_g = [dg_pre_even, dpool_scale, dg_post_even, dg_pre_odd, dsconv, ddconv, ddconv_b, dgam, dbet, dg_post_odd]
    return loss_blk, grad_x, small_g


class _Exchanges:
    def __init__(self, dev, core, d):
        self.dev = dev.astype(jnp.int32).reshape(1)
        self.core = core
        self.chip = (dev // 2).astype(jnp.int32).reshape(1)
        self.d = d
        self.in_flight = {}
        self.to_sibling = None
        self.pending = []

    def start_weights(self, tag, blocks, afters):
        lands = [lax.empty((N_DEV,) + b.shape, b.dtype) for b in blocks]
        send, recv, srcs, lands, token = split_start("gather", blocks, lands, afters, "ag_start_" + tag)
        self.in_flight[tag] = (send, recv, srcs, lands)
        return token

    def weights_arrived(self, tag, after):
        send, recv, srcs, lands = self.in_flight.pop(tag)
        srcs, lands = split_wait("gather", send, recv, srcs, lands, [after], "ag_wait_" + tag)
        lands = [place_block(l, b, self.dev, "ag_own_%s_%d" % (tag, k)) for k, (l, b) in enumerate(zip(lands, srcs))]
        lands = [l.reshape((4, 2) + l.shape[1:]) for l in lands]
        send, recv, _, lands, token = split_start("halves", [], lands, [], "ag_sibling_start_" + tag)
        self.in_flight[tag] = (send, recv, lands)
        return token

    def weights(self, tag, after):
        send, recv, lands = self.in_flight.pop(tag)
        _, lands = split_wait("halves", send, recv, [], lands, [after], "ag_sibling_wait_" + tag)
        return [l.reshape((N_DEV,) + l.shape[2:]) for l in lands]

    def reduce_begin(self, partials, tag, sibling_part=False):
        names = list(partials)
        arrs = [partials[k].reshape((4, 1 if sibling_part else 2) + partials[k].shape[1:]) for k in names]
        lands = [lax.empty((4, 1) + a.shape[2:], a.dtype) for a in arrs]
        send, recv, srcs, lands, token = split_start("sibling", arrs, lands, [], "rs_sibling_start_" + tag)
        self.to_sibling = (tag, names, send, recv, srcs, lands)
        return token

    def reduce_send(self, after, own_part=None):
        tag, names, send, recv, srcs, lands = self.to_sibling
        srcs, lands = split_wait("sibling", send, recv, srcs, lands, [after], "rs_sibling_wait_" + tag)
        which = self.core
        if own_part is not None:
            srcs = [own_part[k].reshape((4, 1) + own_part[k].shape[1:]) for k in names]
            which = jnp.zeros((1,), jnp.int32)
        sums = [pair_add(o, r, which, "rs_pair_add_" + k) for k, o, r in zip(names, srcs, lands)]
        zones = [lax.empty(a.shape, a.dtype) for a in sums]
        send, recv, srcs, zones, token = split_start("scatter", sums, zones, [], "rs_start_" + tag)
        self.pending.append((tag, names, send, recv, srcs, zones))
        return token

    def finish_updates(self, big, afters):
        tag, names, send, recv, srcs, lands = self.pending.pop(0)
        srcs, lands = split_wait("scatter", send, recv, srcs, lands, afters, "rs_wait_" + tag)
        out = {}
        for name, own, got in zip(names, srcs, lands):
            w, m, v = big[name]
            shp = own.shape[1:]
            outs = adamw_big(w.reshape(shp), m.reshape(shp), v.reshape(shp), own, got, self.chip, "adamw_" + name)
            out[name] = [o.reshape(w.shape) for o in outs]
        return out


def _update_small(small_g, loss_blk, small_w, small_m, small_v, dev, d, cl, deps):
    packed = jnp.concatenate([_rows128(g) for g in small_g] + [loss_blk], axis=0)
    (g8,) = all_gather([packed], "ag_small_grads", deps)
    tot = sum_devices(g8, "sum_small_grads")
    loss = tot[packed.shape[0] - 8, 0]
    full_g = []
    lo = 0
    for g in small_g:
        rows = g.size // 128
        full_g.append(tot[lo:lo + rows].reshape(g.shape))
        lo += rows

    def mine(g, width):
        return lax.dynamic_slice_in_dim(g, dev * width, width, axis=g.ndim - 1)

    fg = full_g
    small_gl = [fg[0], fg[1], fg[2], mine(fg[3], d // N_DEV), mine(fg[4], cl), mine(fg[5], cl), mine(fg[6], cl),
                mine(fg[7], cl), mine(fg[8], cl), mine(fg[9], d // N_DEV)]
    sd, sm, sv = adamw_small(small_w, small_gl, small_m, small_v, "adamw_small")

    def like(k, a):
        return a[None] if k in (4, 5) else a

    sg = [like(k, a) for k, a in enumerate(small_gl)]
    sd = [like(k, a) for k, a in enumerate(sd)]
    sm = [like(k, a) for k, a in enumerate(sm)]
    sv = [like(k, a) for k, a in enumerate(sv)]
    return sg, sd, sm, sv, loss
```

```python
import functools
import math

import jax
import jax.numpy as jnp
from jax import lax
from jax.experimental import pallas as pl
from jax.experimental.pallas import tpu as pltpu

F32 = jnp.float32
BF16 = jnp.bfloat16
EPS = 1e-6
HEAD_DIM = 128
POOL_WINDOWS = (2, 4, 8, 16)
SCONV_K = 3
CONF_K = 31
HALO = 32
N_DEV = 8
VMEM_LIMIT = 56 * 1024 * 1024
MESH = pl.DeviceIdType.MESH

ADAM_LR = 0.001
ADAM_B1 = 0.9
ADAM_B2 = 0.999
ADAM_EPS = 1e-08
ADAM_WD = 0.01
ADAM_STEP = 10


def _params(*sem):
    return pltpu.CompilerParams(dimension_semantics=sem, vmem_limit_bytes=VMEM_LIMIT)


def _sigmoid(v):
    return 1.0 / (1.0 + jnp.exp(-v))


def _silu(v):
    return v * _sigmoid(v)


def _silu_and_grad(v):
    s = _sigmoid(v)
    return v * s, s * (1.0 + v * (1.0 - s))


def _rowsum8(v):
    r, c = v.shape
    return jnp.sum(v.reshape(r // 8, 8, c), axis=0)


SUBLANES = 8


class _Taps:
    def __init__(self, xx, rows, before):
        self.xx, self.rows, self.before, self.rotated = xx, rows, before, {}

    def __call__(self, i):
        r, q = i % SUBLANES, i // SUBLANES
        if r not in self.rotated:
            n = self.xx.shape[0]
            self.rotated[r] = self.xx if r == 0 else pltpu.roll(self.xx, r if self.before else n - r, 0)
        lo = HALO - SUBLANES * q if self.before else SUBLANES * q
        return self.rotated[r][lo:lo + self.rows]


def _window_sum(xx, win, before):
    n = xx.shape[0]
    acc = xx
    k = 1
    while k < win:
        acc = acc + pltpu.roll(acc, k if before else n - k, 0)
        k *= 2
    return acc


def postnorm_fwd(x, o, g, g_next, name, tm=512, dep=None):
    s, d = x.shape
    dep_args, dep_specs = _after(dep)

    def body(x_ref, o_ref, g_ref, gn_ref, *rest):
        y_ref, h_ref = rest[-2:]
        ov = o_ref[...].astype(F32)
        r = lax.rsqrt(jnp.mean(ov * ov, axis=-1, keepdims=True) + EPS)
        y = x_ref[...] + ov * r * g_ref[...]
        y_ref[...] = y
        r2 = lax.rsqrt(jnp.mean(y * y, axis=-1, keepdims=True) + EPS)
        h_ref[...] = (y * r2 * gn_ref[...]).astype(BF16)

    row = pl.BlockSpec((tm, d), lambda i: (i, 0))
    vec = pl.BlockSpec((1, d), lambda i: (0, 0))
    return pl.pallas_call(
        body, name=name, grid=(s // tm,),
        in_specs=[row, row, vec, vec] + dep_specs, out_specs=[row, row],
        out_shape=[jax.ShapeDtypeStruct((s, d), F32), jax.ShapeDtypeStruct((s, d), BF16)],
        compiler_params=_params("parallel"),
    )(x, o, g, g_next, *dep_args)


def final_fwd_bwd(x1, o, g, target, name, tm=512):
    s, d = x1.shape
    n = s // tm

    def body(x_ref, o_ref, g_ref, t_ref, loss_ref, gx_ref, do_ref, dg_ref, lacc, gacc):
        i = pl.program_id(0)

        @pl.when(i == 0)
        def _():
            lacc[...] = jnp.zeros_like(lacc)
            gacc[...] = jnp.zeros_like(gacc)

        ov = o_ref[...].astype(F32)
        gv = g_ref[...]
        r = lax.rsqrt(jnp.mean(ov * ov, axis=-1, keepdims=True) + EPS)
        oh = ov * r
        diff = x_ref[...] + oh * gv - t_ref[...]
        lacc[...] += _rowsum8(diff * diff)
        gx = diff * (1.0 / d)
        gx_ref[...] = gx
        gacc[...] += _rowsum8(gx * oh)
        dn = gx * gv
        do_ref[...] = (r * (dn - oh * jnp.mean(dn * oh, axis=-1, keepdims=True))).astype(BF16)

        @pl.when(i == n - 1)
        def _():
            tot = jnp.sum(jnp.sum(lacc[...], axis=0, keepdims=True), axis=1, keepdims=True)
            loss_ref[...] = jnp.broadcast_to(tot * (0.5 / d), loss_ref.shape)
            dg_ref[...] = jnp.sum(gacc[...], axis=0, keepdims=True)

    row = pl.BlockSpec((tm, d), lambda i: (i, 0))
    vec = pl.BlockSpec((1, d), lambda i: (0, 0))
    return pl.pallas_call(
        body, name=name, grid=(n,),
        in_specs=[row, row, vec, row],
        out_specs=[pl.BlockSpec((8, 128), lambda i: (0, 0)), row, row, vec],
        out_shape=[jax.ShapeDtypeStruct((8, 128), F32), jax.ShapeDtypeStruct((s, d), F32),
                   jax.ShapeDtypeStruct((s, d), BF16), jax.ShapeDtypeStruct((1, d), F32)],
        scratch_shapes=[pltpu.VMEM((8, d), F32), pltpu.VMEM((8, d), F32)],
        compiler_params=_params("arbitrary"),
    )(x1, o, g, target)


def _rms_bwd_rows(dyv, xv, gv):
    r = lax.rsqrt(jnp.mean(xv * xv, axis=-1, keepdims=True) + EPS)
    xh = xv * r
    dn = dyv * gv
    return r * (dn - xh * jnp.mean(dn * xh, axis=-1, keepdims=True)), _rowsum8(dyv * xh)


def norm_bwd(dy, inp, g, resid, name, inp2=None, g2=None, tm=256, dep=None):
    s, d = inp.shape
    n = s // tm
    chain = inp2 is not None

    def body(*refs):
        dy_ref, x_ref, g_ref, r_ref = refs[:4]
        outs = refs[-6:] if chain else refs[-3:]
        i = pl.program_id(0)

        @pl.when(i == 0)
        def _():
            for acc in outs[-2:] if chain else outs[-1:]:
                acc[...] = jnp.zeros_like(acc)

        if chain:
            x2_ref, g2_ref = refs[4:6]
            dx_ref, dg_ref, dx2_ref, dg2_ref, gacc, gacc2 = outs
        else:
            dx_ref, dg_ref, gacc = outs
        dx, dg_rows = _rms_bwd_rows(dy_ref[...].astype(F32), x_ref[...], g_ref[...])
        dx = dx + r_ref[...]
        dx_ref[...] = dx
        gacc[...] += dg_rows
        if chain:
            dx2, dg2_rows = _rms_bwd_rows(dx, x2_ref[...].astype(F32), g2_ref[...])
            dx2_ref[...] = dx2.astype(BF16)
            gacc2[...] += dg2_rows

        @pl.when(i == n - 1)
        def _():
            dg_ref[...] = jnp.sum(gacc[...], axis=0, keepdims=True)
            if chain:
                dg2_ref[...] = jnp.sum(gacc2[...], axis=0, keepdims=True)

    row = pl.BlockSpec((tm, d), lambda i: (i, 0))
    vec = pl.BlockSpec((1, d), lambda i: (0, 0))
    dep_args, dep_specs = _after(dep)
    extra = [inp2, g2] if chain else []
    return pl.pallas_call(
        body, name=name, grid=(n,),
        in_specs=[row, row, vec, row] + ([row, vec] if chain else []) + dep_specs,
        out_specs=[row, vec] * (2 if chain else 1),
        out_shape=[jax.ShapeDtypeStruct((s, d), F32), jax.ShapeDtypeStruct((1, d), F32)]
        + ([jax.ShapeDtypeStruct((s, d), BF16), jax.ShapeDtypeStruct((1, d), F32)] if chain else []),
        scratch_shapes=[pltpu.VMEM((8, d), F32)] * (2 if chain else 1),
        compiler_params=_params("arbitrary"),
    )(dy, inp, g, resid, *extra, *dep_args)


def _after(dep):
    if dep is None:
        return [], []
    return [dep], [pl.BlockSpec((8, 128), lambda *_: (0, 0))]


def _lane_concat(ref, count):
    return ref[0] if count == 1 else jnp.concatenate([ref[i] for i in range(count)], axis=1)


def mm_nn(a, w, out_dtype, name, tm=2048, tn=None, dep=None, group=1):
    m, k = a.shape
    tm = min(tm, m)
    ns, _, n = w.shape
    tn = n if tn is None else tn
    nj = n // tn
    assert group == 1 or nj == 1
    dep_args, dep_specs = _after(dep)

    def body(a_ref, w_ref, *rest):
        o_ref = rest[-1]
        o_ref[...] = jnp.dot(a_ref[...], _lane_concat(w_ref, group), preferred_element_type=F32).astype(out_dtype)

    return pl.pallas_call(
        body, name=name, grid=(ns // group, nj, m // tm),
        in_specs=[pl.BlockSpec((tm, k), lambda s, j, i: (i, 0)),
                  pl.BlockSpec((group, k, tn), lambda s, j, i: (s, 0, j))] + dep_specs,
        out_specs=pl.BlockSpec((tm, group * tn), lambda s, j, i: (i, s * nj + j)),
        out_shape=jax.ShapeDtypeStruct((m, ns * n), out_dtype),
        compiler_params=_params("parallel", "parallel", "parallel"),
    )(a, w, *dep_args)


def mm_nt(a, w, out_dtype, name, tm=1024, tn=None, dep=None, group=1):
    m = a.shape[0]
    tm = min(tm, m)
    ns, k, n = w.shape
    tn = n if tn is None else tn
    nj = n // tn
    assert group == 1 or nj == 1
    steps = ns * nj // group
    dep_args, dep_specs = _after(dep)

    def body(a_ref, w_ref, *rest):
        o_ref, acc = rest[-2:]
        r = pl.program_id(1)

        @pl.when(r == 0)
        def _():
            acc[...] = jnp.zeros_like(acc)

        acc[...] += lax.dot_general(a_ref[...], _lane_concat(w_ref, group), (((1,), (1,)), ((), ())),
                                    preferred_element_type=F32)

        @pl.when(r == steps - 1)
        def _():
            o_ref[...] = acc[...].astype(out_dtype)

    return pl.pallas_call(
        body, name=name, grid=(m // tm, steps),
        in_specs=[pl.BlockSpec((tm, group * tn), lambda i, r: (i, r)),
                  pl.BlockSpec((group, k, tn), lambda i, r: (r // nj, 0, r % nj))] + dep_specs,
        out_specs=pl.BlockSpec((tm, k), lambda i, r: (i, 0)),
        out_shape=jax.ShapeDtypeStruct((m, k), out_dtype),
        scratch_shapes=[pltpu.VMEM((tm, k), F32)],
        compiler_params=_params("parallel", "arbitrary"),
    )(a, w, *dep_args)


def mm_tn(a, b, ns, out_dtype, name, tk=1024, tm=2048, dep=None, pick=None, group=1):
    m, k = a.shape
    tm = min(tm, m)
    step, offset = (1, None) if pick is None else pick
    assert group == 1 or pick is None
    n = b.shape[1] // (ns * step)
    steps = m // tm
    dep_args, dep_specs = _after(dep)
    n_pre = 0 if pick is None else 1

    def b_block(s, j, r, *pre):
        return (r, s if pick is None else step * s + pre[0][0])

    def body(*refs):
        a_ref, b_ref = refs[n_pre:n_pre + 2]
        o_ref, acc = refs[-2:]
        r = pl.program_id(2)

        @pl.when(r == 0)
        def _():
            acc[...] = jnp.zeros_like(acc)

        acc[...] += lax.dot_general(a_ref[...], b_ref[...], (((0,), (0,)), ((), ())),
                                    preferred_element_type=F32)

        @pl.when(r == steps - 1)
        def _():
            for i in range(group):
                o_ref[i] = acc[:, i * n:(i + 1) * n].astype(out_dtype)

    return pl.pallas_call(
        body, name=name,
        grid_spec=pltpu.PrefetchScalarGridSpec(
            num_scalar_prefetch=n_pre, grid=(ns // group, k // tk, steps),
            in_specs=[pl.BlockSpec((tm, tk), lambda s, j, r, *pre: (r, j)),
                      pl.BlockSpec((tm, group * n), b_block)] + dep_specs,
            out_specs=pl.BlockSpec((group, tk, n), lambda s, j, r, *pre: (s, j, 0)),
            scratch_shapes=[pltpu.VMEM((tk, group * n), F32)]),
        out_shape=jax.ShapeDtypeStruct((ns, k, n), out_dtype),
        compiler_params=_params("parallel", "parallel", "arbitrary"),
    )(*([] if pick is None else [offset]), a, b, *dep_args)


SB_BLK = 128


LOG2E = 1.0 / math.log(2.0)


def _split_dot(v, tri2):
    hi = pltpu.bitcast(pltpu.bitcast(v, jnp.uint32) & jnp.uint32(0xFFFF0000), F32)
    lo = (v - hi).astype(BF16)
    return jnp.dot(jnp.concatenate([hi.astype(BF16), lo], axis=1), tri2, preferred_element_type=F32)


def _sb_scores(z2, lim, dcol, tri_ex, masked):
    sp = jnp.log2(1.0 + jnp.exp2(-jnp.abs(z2)))
    lb = jnp.minimum(z2, 0.0) - sp
    l1m = lb - z2
    mask = None
    if masked:
        mask = dcol < lim
        l1m = jnp.where(mask, l1m, 0.0)
    return mask, lb, l1m, _split_dot(l1m, tri_ex)


def _sb_consts():
    row = lax.broadcasted_iota(jnp.int32, (SB_BLK, SB_BLK), 0)
    col = lax.broadcasted_iota(jnp.int32, (SB_BLK, SB_BLK), 1)
    tri_ex = jnp.where(row > col, 1.0, 0.0).astype(BF16)
    tri_in = jnp.where(row >= col, 1.0, 0.0).astype(BF16)
    return col - row, jnp.concatenate([tri_ex, tri_ex], axis=0), jnp.concatenate([tri_in, tri_in], axis=0)


def sb_fwd(p, n_heads, name, tq=1024, nsub=8, dep=None):
    s = p.shape[0]
    h_n = n_heads
    b = SB_BLK
    nqs = tq // b
    tk = nsub * b
    scale = 1.0 / math.sqrt(HEAD_DIM)

    dep_args, dep_specs = _after(dep)

    def body(q_ref, k_ref, v_ref, *rest):
        o_ref, w_ref = rest[-2:]
        qi = pl.program_id(1)
        dcol, tri_ex, _ = _sb_consts()
        qv = [q_ref[qs * b:(qs + 1) * b, :] for qs in range(nqs)]
        n_groups = ((qi + 1) * nqs - 1) // nsub + 1

        def step(it, carry, masked):
            c1s, accs = carry
            g = n_groups - 1 - it
            off = pl.multiple_of(g * tk, tk)
            kg = k_ref[pl.ds(off, tk), :]
            vg = v_ref[pl.ds(off, tk), :]
            new_c1, new_acc = [], []
            for qs in range(nqs):
                qb = qi * nqs + qs
                z2 = lax.dot_general(qv[qs], kg, (((1,), (1,)), ((), ())),
                                     preferred_element_type=F32) * (scale * LOG2E)
                blocks = [_sb_scores(z2[:, j * b:(j + 1) * b], (qb - (g * nsub + j)) * b, dcol, tri_ex, masked)
                          for j in range(nsub)]
                run = c1s[qs]
                ws = [None] * nsub
                for j in reversed(range(nsub)):
                    mask, lb, l1m, ls_loc = blocks[j]
                    wj = jnp.exp2(lb + ls_loc + run)
                    ws[j] = (jnp.where(mask, wj, 0.0) if masked else wj).astype(BF16)
                    run = run + jnp.sum(l1m, axis=1, keepdims=True)
                w = jnp.concatenate(ws, axis=1)
                w_ref[0, g, qs * b:(qs + 1) * b, :] = w
                new_acc.append(accs[qs] + jnp.dot(w, vg, preferred_element_type=F32))
                new_c1.append(run)
            return tuple(new_c1), tuple(new_acc)

        init = (tuple(jnp.zeros((b, 1), F32) for _ in range(nqs)),
                tuple(jnp.zeros((b, HEAD_DIM), F32) for _ in range(nqs)))
        assert all(((i + 1) * nqs - 1) // nsub * nsub <= i * nqs for i in range(s // tq))
        first = step(0, init, True)
        _, accs = lax.fori_loop(1, n_groups, functools.partial(step, masked=False), first)
        for qs in range(nqs):
            o_ref[qs * b:(qs + 1) * b, :] = accs[qs]

    return pl.pallas_call(
        body, name=name, grid=(h_n, s // tq),
        in_specs=[pl.BlockSpec((tq, HEAD_DIM), lambda h, i: (i, h)),
                  pl.BlockSpec((s, HEAD_DIM), lambda h, i: (0, h_n + h)),
                  pl.BlockSpec((s, HEAD_DIM), lambda h, i: (0, 2 * h_n + h))] + dep_specs,
        out_specs=[pl.BlockSpec((tq, HEAD_DIM), lambda h, i: (i, h)),
                   pl.BlockSpec((1, s // tk, tq, tk), lambda h, i: (h, 0, i, 0))],
        out_shape=[jax.ShapeDtypeStruct((s, h_n * HEAD_DIM), F32),
                   jax.ShapeDtypeStruct((h_n, s // tk, s, tk), BF16)],
        compiler_params=_params("parallel", "arbitrary"),
    )(p, p, p, *dep_args)


def sb_bwd(p, a, wts, da, n_heads, name, tq=1024, dep=None):
    s = p.shape[0]
    h_n = n_heads
    nq = s // tq
    b = SB_BLK
    nqs = tq // b
    tk = wts.shape[3]
    nsub = tk // b
    scale = 1.0 / math.sqrt(HEAD_DIM)
    dep_args, dep_specs = _after(dep)

    def body(q_ref, k_ref, v_ref, a_ref, da_ref, w_ref, *rest):
        dq_ref, dk_ref, dv_ref, dk_acc, dv_acc = rest[-5:]
        qi = pl.program_id(1)

        @pl.when(qi == 0)
        def _():
            dk_acc[...] = jnp.zeros_like(dk_acc)
            dv_acc[...] = jnp.zeros_like(dv_acc)

        dcol, _, tri_in = _sb_consts()
        q_all = q_ref[...]
        do_all = da_ref[...]
        qv = [q_ref[qs * b:(qs + 1) * b, :] for qs in range(nqs)]
        dov = [da_ref[qs * b:(qs + 1) * b, :] for qs in range(nqs)]
        tots = [jnp.sum(dov[qs].astype(F32) * a_ref[qs * b:(qs + 1) * b, :], axis=1, keepdims=True)
                for qs in range(nqs)]
        n_groups = ((qi + 1) * nqs - 1) // nsub + 1

        def step(it, carry, masked):
            c2s, dqs = carry
            g = n_groups - 1 - it
            off = pl.multiple_of(g * tk, tk)
            kg = k_ref[pl.ds(off, tk), :]
            vg = v_ref[pl.ds(off, tk), :]
            w_all = w_ref[0, g]
            new_c2, new_dq, dz_rows = [], [], []
            for qs in range(nqs):
                qb = qi * nqs + qs
                z2 = lax.dot_general(qv[qs], kg, (((1,), (1,)), ((), ())),
                                     preferred_element_type=F32) * (-scale * LOG2E)
                dw = lax.dot_general(dov[qs], vg, (((1,), (1,)), ((), ())), preferred_element_type=F32)
                beta = 1.0 / (1.0 + jnp.exp2(z2))
                e = dw * w_all[qs * b:(qs + 1) * b, :].astype(F32)
                run2 = c2s[qs]
                dzs = [None] * nsub
                for j in reversed(range(nsub)):
                    cols = slice(j * b, (j + 1) * b)
                    later = _split_dot(e[:, cols], tri_in) + run2
                    bj = beta[:, cols]
                    dz = (e[:, cols] * (1.0 - bj) - bj * (tots[qs] - later)) * scale
                    if masked:
                        dz = jnp.where(dcol < (qb - (g * nsub + j)) * b, dz, 0.0)
                    dzs[j] = dz.astype(BF16)
                    run2 = run2 + jnp.sum(e[:, cols], axis=1, keepdims=True)
                dzq = jnp.concatenate(dzs, axis=1)
                new_dq.append(dqs[qs] + jnp.dot(dzq, kg, preferred_element_type=F32))
                new_c2.append(run2)
                dz_rows.append(dzq)
            dz_all = jnp.concatenate(dz_rows, axis=0)
            dk_acc[pl.ds(off, tk), :] += lax.dot_general(dz_all, q_all, (((0,), (0,)), ((), ())),
                                                         preferred_element_type=F32)
            dv_acc[pl.ds(off, tk), :] += lax.dot_general(w_all, do_all, (((0,), (0,)), ((), ())),
                                                         preferred_element_type=F32)
            return tuple(new_c2), tuple(new_dq)

        zeros = tuple(jnp.zeros((b, 1), F32) for _ in range(nqs))
        assert all(((i + 1) * nqs - 1) // nsub * nsub <= i * nqs for i in range(s // tq))
        first = step(0, (zeros, tuple(jnp.zeros((b, HEAD_DIM), F32) for _ in range(nqs))), True)
        _, dqs = lax.fori_loop(1, n_groups, functools.partial(step, masked=False), first)
        for qs in range(nqs):
            dq_ref[qs * b:(qs + 1) * b, :] = dqs[qs].astype(BF16)

        @pl.when(qi == nq - 1)
        def _():
            dk_ref[...] = dk_acc[...].astype(BF16)
            dv_ref[...] = dv_acc[...].astype(BF16)

    blk = pl.BlockSpec((tq, HEAD_DIM), lambda h, i: (i, h))
    full = pl.BlockSpec((s, HEAD_DIM), lambda h, i: (0, h))
    return pl.pallas_call(
        body, name=name, grid=(h_n, nq),
        in_specs=[blk, pl.BlockSpec((s, HEAD_DIM), lambda h, i: (0, h_n + h)),
                  pl.BlockSpec((s, HEAD_DIM), lambda h, i: (0, 2 * h_n + h)), blk, blk,
                  pl.BlockSpec((1, s // tk, tq, tk), lambda h, i: (h, 0, i, 0))] + dep_specs,
        out_specs=[blk, full, full],
        out_shape=[jax.ShapeDtypeStruct((s, h_n * HEAD_DIM), BF16)] * 3,
        scratch_shapes=[pltpu.VMEM((s, HEAD_DIM), F32), pltpu.VMEM((s, HEAD_DIM), F32)],
        compiler_params=_params("parallel", "arbitrary"),
    )(p, p, p, a, da, wts, *dep_args)


def _pool_window(xx, win, r0, rc):
    cur = xx[HALO:HALO + rc]
    ws = _window_sum(xx, win, True)[HALO:HALO + rc]
    t_idx = r0 + lax.broadcasted_iota(jnp.int32, (rc, 1), 0)
    inv = 1.0 / jnp.minimum(win, t_idx + 1).astype(F32)
    return ws * inv - cur, inv


def even_mix_fwd(a, p, pool_w, pool_scale, name, rc=512, dep=None):
    s = p.shape[0]
    ng = len(POOL_WINDOWS)
    cw = pool_w.shape[1]
    n_chunks = s // rc
    dep_args, dep_specs = _after(dep)

    def body(a_ref, u_ref, g_ref, w_ref, sc_ref, *rest):
        y_ref, upad = rest[-2:]
        j = pl.program_id(0)

        @pl.when(j < ng)
        def _():
            def chunk(ci, carry):
                rows = pl.ds(pl.multiple_of(ci * rc, rc), rc)
                y_ref[rows, :] = (a_ref[rows, :] * _silu(g_ref[rows, :].astype(F32))).astype(BF16)
                return carry

            lax.fori_loop(0, n_chunks, chunk, 0)

        for gi, win in enumerate(POOL_WINDOWS):
            @pl.when(j == ng + gi)
            def _(win=win):
                upad[0:HALO, :] = jnp.zeros((HALO, cw), F32)

                def fill(ci, carry):
                    r0 = pl.multiple_of(ci * rc, rc)
                    upad[pl.ds(pl.multiple_of(r0 + HALO, HALO), rc), :] = u_ref[pl.ds(r0, rc), :].astype(F32)
                    return carry

                lax.fori_loop(0, n_chunks, fill, 0)

                def chunk(ci, carry):
                    r0 = pl.multiple_of(ci * rc, rc)
                    rows = pl.ds(r0, rc)
                    pooled, _ = _pool_window(upad[pl.ds(r0, HALO + rc), :], win, r0, rc)
                    t = jnp.dot(pooled.astype(BF16), w_ref[0], preferred_element_type=F32)
                    y_ref[rows, :] = (t * sc_ref[...] * _silu(g_ref[rows, :].astype(F32))).astype(BF16)
                    return carry

                lax.fori_loop(0, n_chunks, chunk, 0)

    grp = lambda j: jnp.maximum(j - ng, 0)
    return pl.pallas_call(
        body, name=name, grid=(2 * ng,),
        in_specs=[pl.BlockSpec((s, cw), lambda j: (0, jnp.minimum(j, ng - 1))),
                  pl.BlockSpec((s, cw), lambda j: (0, 3 * ng + grp(j))),
                  pl.BlockSpec((s, cw), lambda j: (0, 4 * ng + j)),
                  pl.BlockSpec((1, cw, cw), lambda j: (grp(j), 0, 0)),
                  pl.BlockSpec((1, cw), lambda j: (0, grp(j)))] + dep_specs,
        out_specs=pl.BlockSpec((s, cw), lambda j: (0, j)),
        out_shape=jax.ShapeDtypeStruct((s, 2 * ng * cw), BF16),
        scratch_shapes=[pltpu.VMEM((HALO + s, cw), F32)],
        compiler_params=_params("arbitrary"),
    )(a, p, p, pool_w, pool_scale, *dep_args)


def even_mix_bwd(dy, a, p, pool_w, pool_scale, name, rc=512):
    s = p.shape[0]
    ng = len(POOL_WINDOWS)
    cw = pool_w.shape[1]
    n_chunks = s // rc

    def body(dy_ref, a_ref, u_ref, g_ref, w_ref, sc_ref, da_ref, du_ref, dg_ref, dw_ref, dsc_ref,
             upad, rpad, dpl, dw_acc, dsc_acc):
        j = pl.program_id(0)

        @pl.when(j < ng)
        def _():
            def chunk(ci, carry):
                rows = pl.ds(pl.multiple_of(ci * rc, rc), rc)
                dyv = dy_ref[rows, :].astype(F32)
                sg, dsg = _silu_and_grad(g_ref[rows, :].astype(F32))
                da_ref[rows, :] = (dyv * sg).astype(BF16)
                dg_ref[rows, :] = (dyv * a_ref[rows, :] * dsg).astype(BF16)
                return carry

            lax.fori_loop(0, n_chunks, chunk, 0)

        for gi, win in enumerate(POOL_WINDOWS):
            @pl.when(j == ng + gi)
            def _(win=win):
                upad[0:HALO, :] = jnp.zeros((HALO, cw), F32)
                rpad[s:s + HALO, :] = jnp.zeros((HALO, cw), F32)
                dw_acc[...] = jnp.zeros_like(dw_acc)
                dsc_acc[...] = jnp.zeros_like(dsc_acc)

                def fill(ci, carry):
                    r0 = pl.multiple_of(ci * rc, rc)
                    upad[pl.ds(pl.multiple_of(r0 + HALO, HALO), rc), :] = u_ref[pl.ds(r0, rc), :].astype(F32)
                    return carry

                lax.fori_loop(0, n_chunks, fill, 0)

                def chunk(ci, carry):
                    r0 = pl.multiple_of(ci * rc, rc)
                    rows = pl.ds(r0, rc)
                    pooled, inv = _pool_window(upad[pl.ds(r0, HALO + rc), :], win, r0, rc)
                    pb = pooled.astype(BF16)
                    wv = w_ref[0]
                    t = jnp.dot(pb, wv, preferred_element_type=F32)
                    scv = sc_ref[...]
                    dyv = dy_ref[rows, :].astype(F32)
                    sg, dsg = _silu_and_grad(g_ref[rows, :].astype(F32))
                    dpo = dyv * sg
                    dg_ref[rows, :] = (dyv * t * scv * dsg).astype(BF16)
                    dsc_acc[...] += _rowsum8(dpo * t)
                    dtb = (dpo * scv).astype(BF16)
                    dw_acc[...] += lax.dot_general(pb, dtb, (((0,), (0,)), ((), ())),
                                                   preferred_element_type=F32)
                    dpooled = lax.dot_general(dtb, wv, (((1,), (1,)), ((), ())),
                                              preferred_element_type=F32)
                    dpl[rows, :] = dpooled
                    rpad[rows, :] = dpooled * inv
                    return carry

                lax.fori_loop(0, n_chunks, chunk, 0)

                def chunk2(ci, carry):
                    r0 = pl.multiple_of(ci * rc, rc)
                    rows = pl.ds(r0, rc)
                    xx = rpad[pl.ds(r0, rc + HALO), :]
                    fs = _window_sum(xx, win, False)[0:rc]
                    du_ref[rows, :] = (fs - dpl[rows, :]).astype(BF16)
                    return carry

                lax.fori_loop(0, n_chunks, chunk2, 0)
                dw_ref[0] = dw_acc[...]
                dsc_ref[...] = jnp.sum(dsc_acc[...], axis=0, keepdims=True)

    grp = lambda j: jnp.maximum(j - ng, 0)
    att = lambda j: jnp.minimum(j, ng - 1)
    return pl.pallas_call(
        body, name=name, grid=(2 * ng,),
        in_specs=[pl.BlockSpec((s, cw), lambda j: (0, j)),
                  pl.BlockSpec((s, cw), lambda j: (0, att(j))),
                  pl.BlockSpec((s, cw), lambda j: (0, 3 * ng + grp(j))),
                  pl.BlockSpec((s, cw), lambda j: (0, 4 * ng + j)),
                  pl.BlockSpec((1, cw, cw), lambda j: (grp(j), 0, 0)),
                  pl.BlockSpec((1, cw), lambda j: (0, grp(j)))],
        out_specs=[pl.BlockSpec((s, cw), lambda j: (0, att(j))),
                   pl.BlockSpec((s, cw), lambda j: (0, grp(j))),
                   pl.BlockSpec((s, cw), lambda j: (0, j)),
                   pl.BlockSpec((1, cw, cw), lambda j: (grp(j), 0, 0)),
                   pl.BlockSpec((1, cw), lambda j: (0, grp(j)))],
        out_shape=[jax.ShapeDtypeStruct((s, ng * cw), BF16), jax.ShapeDtypeStruct((s, ng * cw), BF16),
                   jax.ShapeDtypeStruct((s, 2 * ng * cw), BF16),
                   jax.ShapeDtypeStruct((ng, cw, cw), F32), jax.ShapeDtypeStruct((1, ng * cw), F32)],
        scratch_shapes=[pltpu.VMEM((HALO + s, cw), F32), pltpu.VMEM((s + HALO, cw), F32),
                        pltpu.VMEM((s, cw), F32), pltpu.VMEM((cw, cw), F32), pltpu.VMEM((8, cw), F32)],
        compiler_params=_params("arbitrary"),
    )(dy, a, p, p, pool_w, pool_scale)


def _halo_before(tm):
    return lambda i: jnp.maximum(i * (tm // HALO) - 1, 0)


def _halo_after(tm, s):
    return lambda i: jnp.minimum((i + 1) * (tm // HALO), s // HALO - 1)


def odd_mix_fwd(p, sconv_w, dconv_w, dconv_b, cnorm_g, cnorm_b, name, tm=128, dep=None):
    s = p.shape[0]
    cw = sconv_w.shape[1]
    n = s // tm
    lanes = 128
    hb = _halo_before(tm)

    dep_args, dep_specs = _after(dep)

    def body(hc_ref, hch_ref, bc_ref, cc_ref, cch_ref, ga_ref, gah_ref, gb_ref, gbh_ref, g1_ref, g2_ref,
             sw_ref, dw_ref, db_ref, gam_ref, bet_ref, *rest):
        y_ref, dc_ref = rest[-2:]
        first = pl.program_id(0) == 0
        for l in range(cw // lanes):
            cols = slice(l * lanes, (l + 1) * lanes)
            mh = jnp.where(first, 0.0, cch_ref[:, cols].astype(F32) * hch_ref[:, cols].astype(F32))
            mm = cc_ref[:, cols].astype(F32) * hc_ref[:, cols].astype(F32)
            xx = jnp.concatenate([mh, mm], axis=0)
            tap = _Taps(xx, tm, True)
            cv = jnp.zeros((tm, lanes), F32)
            for k in range(SCONV_K):
                cv = cv + sw_ref[k:k + 1, cols] * tap(SCONV_K - 1 - k)
            c_out = bc_ref[:, cols].astype(F32) * cv
            y_ref[:, cols] = (c_out * _silu(g1_ref[:, cols].astype(F32))).astype(BF16)
            dh = jnp.where(first, 0.0, gah_ref[:, cols].astype(F32) * _sigmoid(gbh_ref[:, cols].astype(F32)))
            dm = ga_ref[:, cols].astype(F32) * _sigmoid(gb_ref[:, cols].astype(F32))
            xx = jnp.concatenate([dh, dm], axis=0)
            tap = _Taps(xx, tm, True)
            acc = jnp.zeros((tm, lanes), F32) + db_ref[:, cols]
            for k in range(CONF_K):
                acc = acc + dw_ref[k:k + 1, cols] * tap(CONF_K - 1 - k)
            dc_ref[:, cols] = acc
        rs = 32
        for r in range(tm // rs):
            rows = slice(r * rs, (r + 1) * rs)
            xv = dc_ref[rows, :]
            mu = jnp.mean(xv, axis=-1, keepdims=True)
            xc = xv - mu
            rstd = lax.rsqrt(jnp.mean(xc * xc, axis=-1, keepdims=True) + EPS)
            ln = xc * rstd * gam_ref[...] + bet_ref[...]
            y_ref[rows, cw:2 * cw] = (_silu(ln) * _silu(g2_ref[rows, :].astype(F32))).astype(BF16)

    main = lambda c: pl.BlockSpec((tm, cw), lambda i: (i, c))
    halo = lambda c: pl.BlockSpec((HALO, cw), lambda i: (hb(i), c))
    vec = lambda r: pl.BlockSpec((r, cw), lambda i: (0, 0))
    return pl.pallas_call(
        body, name=name, grid=(n,),
        in_specs=[main(0), halo(0), main(1), main(2), halo(2), main(3), halo(3), main(4), halo(4),
                  main(5), main(6), vec(SCONV_K), vec(CONF_K), vec(1), vec(1), vec(1)] + dep_specs,
        out_specs=[pl.BlockSpec((tm, 2 * cw), lambda i: (i, 0)), pl.BlockSpec((tm, cw), lambda i: (i, 0))],
        out_shape=[jax.ShapeDtypeStruct((s, 2 * cw), BF16), jax.ShapeDtypeStruct((s, cw), F32)],
        compiler_params=_params("parallel"),
    )(p, p, p, p, p, p, p, p, p, p, p, sconv_w, dconv_w, dconv_b, cnorm_g, cnorm_b, *dep_args)


def odd_bwd_ln(dy, p, dc, cnorm_g, cnorm_b, name, tm=256):
    s = p.shape[0]
    cw = dc.shape[1]
    n = s // tm
    rs = 32

    def body(dy_ref, g2_ref, dc_ref, gam_ref, bet_ref, ddc_ref, dg_ref, dgam_ref, dbet_ref, gacc, bacc):
        i = pl.program_id(0)

        @pl.when(i == 0)
        def _():
            gacc[...] = jnp.zeros_like(gacc)
            bacc[...] = jnp.zeros_like(bacc)

        def chunk(ci, carry):
            rows = pl.ds(pl.multiple_of(ci * rs, rs), rs)
            xv = dc_ref[rows, :]
            mu = jnp.mean(xv, axis=-1, keepdims=True)
            xc = xv - mu
            rstd = lax.rsqrt(jnp.mean(xc * xc, axis=-1, keepdims=True) + EPS)
            xh = xc * rstd
            gam = gam_ref[...]
            sl, dsl = _silu_and_grad(xh * gam + bet_ref[...])
            sg, dsg = _silu_and_grad(g2_ref[rows, :].astype(F32))
            dyv = dy_ref[rows, :].astype(F32)
            dg_ref[rows, :] = (dyv * sl * dsg).astype(BF16)
            dln = dyv * sg * dsl
            gacc[...] += _rowsum8(dln * xh)
            bacc[...] += _rowsum8(dln)
            dxh = dln * gam
            ddc_ref[rows, :] = rstd * (dxh - jnp.mean(dxh, axis=-1, keepdims=True)
                                       - xh * jnp.mean(dxh * xh, axis=-1, keepdims=True))
            return carry

        lax.fori_loop(0, tm // rs, chunk, 0)

        @pl.when(i == n - 1)
        def _():
            dgam_ref[...] = jnp.sum(gacc[...], axis=0, keepdims=True)
            dbet_ref[...] = jnp.sum(bacc[...], axis=0, keepdims=True)

    vec = pl.BlockSpec((1, cw), lambda i: (0, 0))
    return pl.pallas_call(
        body, name=name, grid=(n,),
        in_specs=[pl.BlockSpec((tm, cw), lambda i: (i, 1)), pl.BlockSpec((tm, cw), lambda i: (i, 6)),
                  pl.BlockSpec((tm, cw), lambda i: (i, 0)), vec, vec],
        out_specs=[pl.BlockSpec((tm, cw), lambda i: (i, 0)), pl.BlockSpec((tm, cw), lambda i: (i, 0)), vec, vec],
        out_shape=[jax.ShapeDtypeStruct((s, cw), F32), jax.ShapeDtypeStruct((s, cw), BF16),
                   jax.ShapeDtypeStruct((1, cw), F32), jax.ShapeDtypeStruct((1, cw), F32)],
        scratch_shapes=[pltpu.VMEM((8, cw), F32), pltpu.VMEM((8, cw), F32)],
        compiler_params=_params("arbitrary"),
    )(dy, p, dc, cnorm_g, cnorm_b)


def odd_bwd_conv(dy, p, ddc, dg2, sconv_w, dconv_w, name, tm=128):
    s = p.shape[0]
    cw = ddc.shape[1]
    n = s // tm
    lanes = 128
    hb = _halo_before(tm)
    ha = _halo_after(tm, s)

    def body(dy_ref, dya_ref, g1_ref, g1a_ref, bc_ref, bca_ref, hc_ref, hch_ref, cc_ref, cch_ref,
             ddc_ref, ddca_ref, ga_ref, gah_ref, gb_ref, gbh_ref, dg2_ref, sw_ref, dw_ref,
             dp_ref, dsw_ref, ddw_ref, ddb_ref, sw_acc, dw_acc, db_acc):
        i = pl.program_id(0)
        first = i == 0
        last = i == n - 1

        @pl.when(first)
        def _():
            sw_acc[...] = jnp.zeros_like(sw_acc)
            dw_acc[...] = jnp.zeros_like(dw_acc)
            db_acc[...] = jnp.zeros_like(db_acc)

        for l in range(cw // lanes):
            cols = slice(l * lanes, (l + 1) * lanes)
            mh = jnp.where(first, 0.0, cch_ref[:, cols].astype(F32) * hch_ref[:, cols].astype(F32))
            hcv = hc_ref[:, cols].astype(F32)
            ccv = cc_ref[:, cols].astype(F32)
            xx = jnp.concatenate([mh, ccv * hcv], axis=0)
            tap = _Taps(xx, tm, True)
            taps = [tap(SCONV_K - 1 - k) for k in range(SCONV_K)]
            cv = jnp.zeros((tm, lanes), F32)
            for k in range(SCONV_K):
                cv = cv + sw_ref[k:k + 1, cols] * taps[k]
            bcv = bc_ref[:, cols].astype(F32)
            dyv = dy_ref[:, cols].astype(F32)
            sg, dsg = _silu_and_grad(g1_ref[:, cols].astype(F32))
            dco = dyv * sg
            dp_ref[:, 5 * cw + l * lanes:5 * cw + (l + 1) * lanes] = (dyv * bcv * cv * dsg).astype(BF16)
            dp_ref[:, cw + l * lanes:cw + (l + 1) * lanes] = (dco * cv).astype(BF16)
            dcv = dco * bcv
            for k in range(SCONV_K):
                sw_acc[k * 8:(k + 1) * 8, cols] += _rowsum8(dcv * taps[k])
            dcv_a = jnp.where(last, 0.0, dya_ref[:, cols].astype(F32) * _silu(g1a_ref[:, cols].astype(F32))
                              * bca_ref[:, cols].astype(F32))
            xx = jnp.concatenate([dcv, dcv_a], axis=0)
            tap = _Taps(xx, tm, False)
            dm = jnp.zeros((tm, lanes), F32)
            for k in range(SCONV_K):
                dm = dm + sw_ref[k:k + 1, cols] * tap(SCONV_K - 1 - k)
            dp_ref[:, l * lanes:(l + 1) * lanes] = (dm * ccv).astype(BF16)
            dp_ref[:, 2 * cw + l * lanes:2 * cw + (l + 1) * lanes] = (dm * hcv).astype(BF16)
            gav = ga_ref[:, cols].astype(F32)
            sb = _sigmoid(gb_ref[:, cols].astype(F32))
            dh = jnp.where(first, 0.0, gah_ref[:, cols].astype(F32) * _sigmoid(gbh_ref[:, cols].astype(F32)))
            xx = jnp.concatenate([dh, gav * sb], axis=0)
            ddcv = ddc_ref[:, cols]
            db_acc[:, cols] += _rowsum8(ddcv)
            tap = _Taps(xx, tm, True)
            for k in range(CONF_K):
                dw_acc[k * 8:(k + 1) * 8, cols] += _rowsum8(ddcv * tap(CONF_K - 1 - k))
            ddc_a = jnp.where(last, 0.0, ddca_ref[:, cols])
            xx = jnp.concatenate([ddcv, ddc_a], axis=0)
            tap = _Taps(xx, tm, False)
            dgl = jnp.zeros((tm, lanes), F32)
            for k in range(CONF_K):
                dgl = dgl + dw_ref[k:k + 1, cols] * tap(CONF_K - 1 - k)
            dp_ref[:, 3 * cw + l * lanes:3 * cw + (l + 1) * lanes] = (dgl * sb).astype(BF16)
            dp_ref[:, 4 * cw + l * lanes:4 * cw + (l + 1) * lanes] = (dgl * gav * sb * (1.0 - sb)).astype(BF16)
        dp_ref[:, 6 * cw:7 * cw] = dg2_ref[...]

        @pl.when(last)
        def _():
            for k in range(SCONV_K):
                dsw_ref[k:k + 1, :] = jnp.sum(sw_acc[k * 8:(k + 1) * 8, :], axis=0, keepdims=True)
            for k in range(CONF_K):
                ddw_ref[k:k + 1, :] = jnp.sum(dw_acc[k * 8:(k + 1) * 8, :], axis=0, keepdims=True)
            ddb_ref[...] = jnp.sum(db_acc[...], axis=0, keepdims=True)

    def main(c):
        return pl.BlockSpec((tm, cw), lambda i: (i, c))

    def before(c):
        return pl.BlockSpec((HALO, cw), lambda i: (hb(i), c))

    def after(c):
        return pl.BlockSpec((HALO, cw), lambda i: (ha(i), c))

    def vec(r):
        return pl.BlockSpec((r, cw), lambda i: (0, 0))

    return pl.pallas_call(
        body, name=name, grid=(n,),
        in_specs=[main(0), after(0), main(5), after(5), main(1), after(1), main(0), before(0), main(2), before(2),
                  main(0), after(0), main(3), before(3), main(4), before(4), main(0), vec(SCONV_K), vec(CONF_K)],
        out_specs=[pl.BlockSpec((tm, 7 * cw), lambda i: (i, 0)), vec(SCONV_K), vec(CONF_K), vec(1)],
        out_shape=[jax.ShapeDtypeStruct((s, 7 * cw), BF16), jax.ShapeDtypeStruct((SCONV_K, cw), F32),
                   jax.ShapeDtypeStruct((CONF_K, cw), F32), jax.ShapeDtypeStruct((1, cw), F32)],
        scratch_shapes=[pltpu.VMEM((8 * SCONV_K, cw), F32), pltpu.VMEM((8 * CONF_K, cw), F32),
                        pltpu.VMEM((8, cw), F32)],
        compiler_params=_params("arbitrary"),
    )(dy, dy, p, p, p, p, p, p, p, p, ddc, ddc, p, p, p, p, dg2, sconv_w, dconv_w)


_ANY = pl.BlockSpec(memory_space=pl.ANY)


def _place():
    return lax.axis_index("x"), lax.axis_index("y"), lax.axis_index("c")


def all_gather(arrs, name, deps=()):
    n = len(arrs)

    def body(*refs):
        ins, outs = refs[:n], refs[n + len(deps):2 * n + len(deps)]
        send_sems, recv_sems, local_sems = refs[-3:]
        x, y, c = _place()
        me, sibling = (x, y, c), (x, y, 1 - c)
        chips = [(1 - x, y), (x, 1 - y), (1 - x, 1 - y)]

        def copy(a, k, block, to, src=None):
            px, py, pc = block
            dst = outs[a].at[4 * px + 2 * py + pc]
            return pltpu.make_async_remote_copy(
                src_ref=dst if src is None else src, dst_ref=dst,
                send_sem=send_sems.at[7 * a + k], recv_sem=recv_sems.at[7 * a + k],
                device_id=to, device_id_type=MESH)

        mine = [pltpu.make_async_copy(ins[a], outs[a].at[4 * x + 2 * y + c], local_sems.at[a]) for a in range(n)]
        first = []
        for a in range(n):
            first.append(copy(a, 0, me, sibling, src=ins[a]))
            first += [copy(a, 1 + j, me, (*chip, c), src=ins[a]) for j, chip in enumerate(chips)]
        for cp in first + mine:
            cp.start()
        passed = []
        for a in range(n):
            for j, chip in enumerate(chips):
                copy(a, 1 + j, (*chip, c), me).wait_recv()
                cp = copy(a, 4 + j, (*chip, c), sibling)
                cp.start()
                passed.append(cp)
        for a in range(n):
            copy(a, 0, sibling, me).wait_recv()
            for j, chip in enumerate(chips):
                copy(a, 4 + j, (*chip, 1 - c), me).wait_recv()
        for cp in first + passed:
            cp.wait_send()
        for cp in mine:
            cp.wait()

    return pl.pallas_call(
        body, name=name,
        out_shape=[jax.ShapeDtypeStruct((N_DEV,) + a.shape, a.dtype) for a in arrs],
        in_specs=[_ANY] * (n + len(deps)), out_specs=[_ANY] * n,
        scratch_shapes=[pltpu.SemaphoreType.DMA((7 * n,)), pltpu.SemaphoreType.DMA((7 * n,)),
                        pltpu.SemaphoreType.DMA((n,))],
    )(*arrs, *deps)


def in_proj_gathered(xs, g, w_own, extras, name, tm=512):
    s, d = xs.shape
    n = w_own.shape[1]
    arrs = [w_own] + list(extras)
    na = len(arrs)
    tr = 256

    def body(*refs):
        x_ref, g_ref, ins = refs[0], refs[1], refs[2:2 + na]
        h_out, p_ref, outs = refs[2 + na], refs[3 + na], refs[4 + na:4 + 2 * na]
        (h_ref, xbuf, wbuf, obuf, send_sems, recv_sems, load_sem, store_sems, own_sems, h_sem,
         x_sems) = refs[4 + 2 * na:]
        x, y, c = _place()
        me, sibling = (x, y, c), (x, y, 1 - c)
        x_first = c == 0
        near = (jnp.where(x_first, 1 - x, x), jnp.where(x_first, y, 1 - y))
        far = (jnp.where(x_first, x, 1 - x), jnp.where(x_first, 1 - y, y))
        diag = (1 - x, 1 - y)
        k_near, k_far = jnp.where(x_first, 1, 2), jnp.where(x_first, 2, 1)
        f_near, f_far = k_near + 3, k_far + 3

        def slot(block):
            return 4 * block[0] + 2 * block[1] + block[2]

        def copy(a, k, block, to, src=None):
            dst = outs[a].at[slot(block)]
            return pltpu.make_async_remote_copy(
                src_ref=dst if src is None else src, dst_ref=dst,
                send_sem=send_sems.at[7 * a + k], recv_sem=recv_sems.at[7 * a + k],
                device_id=to, device_id_type=MESH)

        first = []
        for a in range(na):
            first += [copy(a, 0, me, sibling, src=ins[a]), copy(a, 1, me, (1 - x, y, c), src=ins[a]),
                      copy(a, 2, me, (x, 1 - y, c), src=ins[a])]
        for cp in first:
            cp.start()
        own = pltpu.make_async_copy(wbuf.at[0], outs[0].at[slot(me)], own_sems.at[0])
        mine = [pltpu.make_async_copy(ins[a], outs[a].at[slot(me)], own_sems.at[a]) for a in range(1, na)]
        stores = [None, None]

        def x_load(i):
            return pltpu.make_async_copy(x_ref.at[pl.ds(i * tr, tr), :], xbuf.at[i % 2], x_sems.at[i % 2])

        x_load(0).start()
        for i in range(s // tr):
            if i + 1 < s // tr:
                x_load(i + 1).start()
            x_load(i).wait()
            xv = xbuf[i % 2]
            r = lax.rsqrt(jnp.mean(xv * xv, axis=-1, keepdims=True) + EPS)
            h_ref[i * tr:(i + 1) * tr, :] = (xv * r * g_ref[...]).astype(BF16)
        h_store = pltpu.make_async_copy(h_ref, h_out, h_sem)
        h_store.start()

        def multiply(k, block, w_from):
            b = k % 2
            if k == 2:
                own.wait()
            load = pltpu.make_async_copy(w_from, wbuf.at[b], load_sem)
            load.start()
            if stores[b] is not None:
                stores[b].wait()
            load.wait()
            if k == 0:
                own.start()

            def chunk(i, carry):
                rows = pl.ds(pl.multiple_of(i * tm, tm), tm)
                obuf[b, rows, :] = jnp.dot(h_ref[rows, :], wbuf[b], preferred_element_type=F32).astype(BF16)
                return carry

            lax.fori_loop(0, s // tm, chunk, 0)
            stores[b] = pltpu.make_async_copy(
                obuf.at[b], p_ref.at[:, pl.ds(pl.multiple_of(slot(block) * n, 128), n)], store_sems.at[b])
            stores[b].start()

        passed = []

        def arrive(a, k, block):
            copy(a, k, block, me).wait_recv()

        def pass_on(a, k, block, to):
            cp = copy(a, k, block, to)
            cp.start()
            passed.append(cp)

        def gather(a, use):
            use(0, me)
            arrive(a, 0, sibling)
            use(1, sibling)
            arrive(a, k_near, (*near, c))
            pass_on(a, 3, (*near, c), (*far, c))
            pass_on(a, f_near, (*near, c), sibling)
            use(2, (*near, c))
            arrive(a, f_far, (*far, 1 - c))
            use(3, (*far, 1 - c))
            arrive(a, k_far, (*far, c))
            pass_on(a, f_far, (*far, c), sibling)
            use(4, (*far, c))
            arrive(a, f_near, (*near, 1 - c))
            use(5, (*near, 1 - c))
            arrive(a, 3, (*diag, c))
            pass_on(a, 6, (*diag, c), sibling)
            use(6, (*diag, c))
            arrive(a, 6, (*diag, 1 - c))
            use(7, (*diag, 1 - c))

        gather(0, lambda k, block: multiply(k, block, ins[0] if k == 0 else outs[0].at[slot(block)]))
        for cp in mine:
            cp.start()
        for a in range(1, na):
            gather(a, lambda k, block: None)
        for cp in first + passed:
            cp.wait_send()
        for cp in mine + stores + [h_store]:
            cp.wait()

    vmem = pl.BlockSpec(memory_space=pltpu.VMEM)
    outs = pl.pallas_call(
        body, name=name,
        out_shape=[jax.ShapeDtypeStruct((s, d), BF16), jax.ShapeDtypeStruct((s, N_DEV * n), BF16)]
        + [jax.ShapeDtypeStruct((N_DEV,) + a.shape, a.dtype) for a in arrs],
        in_specs=[_ANY, vmem] + [_ANY] * na, out_specs=[_ANY] * (2 + na),
        scratch_shapes=[pltpu.VMEM((s, d), BF16), pltpu.VMEM((2, tr, d), F32), pltpu.VMEM((2, d, n), BF16),
                        pltpu.VMEM((2, s, n), BF16),
                        pltpu.SemaphoreType.DMA((7 * na,)), pltpu.SemaphoreType.DMA((7 * na,)),
                        pltpu.SemaphoreType.DMA, pltpu.SemaphoreType.DMA((2,)), pltpu.SemaphoreType.DMA((na,)),
                        pltpu.SemaphoreType.DMA, pltpu.SemaphoreType.DMA((2,))],
        compiler_params=pltpu.CompilerParams(vmem_limit_bytes=VMEM_LIMIT),
    )(xs, g, *arrs)
    return outs[0], outs[1], outs[2], outs[3:]


_HBM = pl.BlockSpec(memory_space=pltpu.HBM)
_SEM = pl.BlockSpec(memory_space=pltpu.SEMAPHORE)
_DATAFLOW = pltpu.SideEffectType.DATAFLOW_SIDE_EFFECTING


def _peers_per_array(kind):
    return 1 if kind in ("sibling", "halves") else 3


def _split_copies(kind, srcs, lands, send_sems, recv_sems):
    x, y, c = _place()
    per = _peers_per_array(kind)
    out = []
    for a in range(len(lands)):
        if kind == "sibling":
            part = srcs[a] if srcs[a].shape[1] == 1 else srcs[a].at[:, pl.ds(1 - c, 1)]
            peers = [((x, y, 1 - c), part, lands[a], lands[a])]
        elif kind == "halves":
            mine, its = lands[a].at[:, pl.ds(c, 1)], lands[a].at[:, pl.ds(1 - c, 1)]
            peers = [((x, y, 1 - c), mine, mine, its)]
        else:
            peers = []
            for px, py in [(1 - x, y), (x, 1 - y), (1 - x, 1 - y)]:
                if kind == "gather":
                    views = (srcs[a], lands[a].at[4 * x + 2 * y + c], lands[a].at[4 * px + 2 * py + c])
                else:
                    views = (srcs[a].at[2 * px + py], lands[a].at[2 * x + y], lands[a].at[2 * px + py])
                peers.append(((px, py, c),) + views)
        for j, (peer, src, dst, arrives) in enumerate(peers):
            sems = dict(send_sem=send_sems.at[per * a + j], recv_sem=recv_sems.at[per * a + j],
                        device_id=peer, device_id_type=MESH)
            out.append((pltpu.make_async_remote_copy(src_ref=src, dst_ref=dst, **sems),
                        pltpu.make_async_remote_copy(src_ref=src, dst_ref=arrives, **sems)))
    return out


def split_start(kind, srcs, lands, deps, name):
    ns, nl = len(srcs), len(lands)
    n_sems = _peers_per_array(kind) * nl
    held = list(srcs) + list(lands)

    def body(*refs):
        send_sems, recv_sems = refs[len(held) + len(deps)], refs[len(held) + len(deps) + 1]
        for copy, _ in _split_copies(kind, refs[:ns], refs[ns:ns + nl], send_sems, recv_sems):
            copy.start()
        token = refs[-1]
        token[...] = jnp.zeros_like(token)

    outs = pl.pallas_call(
        body, name=name,
        out_shape=(pltpu.SemaphoreType.DMA((n_sems,)), pltpu.SemaphoreType.DMA((n_sems,)),
                   *[pltpu.HBM(a.shape, a.dtype) for a in held], jax.ShapeDtypeStruct((8, 128), F32)),
        in_specs=[_HBM] * len(held) + [_ANY] * len(deps),
        out_specs=(_SEM, _SEM, *([_HBM] * len(held)), pl.BlockSpec(memory_space=pltpu.VMEM)),
        input_output_aliases={i: 2 + i for i in range(len(held))},
        compiler_params=pltpu.CompilerParams(has_side_effects=_DATAFLOW),
    )(*[pltpu.with_memory_space_constraint(a, pltpu.HBM) for a in held], *deps)
    return outs[0], outs[1], list(outs[2:2 + ns]), list(outs[2 + ns:2 + ns + nl]), outs[-1]


def split_wait(kind, send_sems, recv_sems, srcs, lands, afters, name):
    ns, nl = len(srcs), len(lands)
    held = list(srcs) + list(lands)

    def body(*refs):
        for _, arrival in _split_copies(kind, refs[:ns], refs[ns:ns + nl], refs[ns + nl], refs[ns + nl + 1]):
            arrival.wait_send()
            arrival.wait_recv()

    outs = pl.pallas_call(
        body, name=name,
        out_shape=[pltpu.HBM(a.shape, a.dtype) for a in held],
        in_specs=[_HBM] * len(held) + [_SEM, _SEM] + [_ANY] * len(afters),
        out_specs=[_HBM] * len(held),
        input_output_aliases={i: i for i in range(len(held))},
        compiler_params=pltpu.CompilerParams(has_side_effects=_DATAFLOW),
    )(*held, send_sems, recv_sems, *afters)
    return list(outs[:ns]), list(outs[ns:])


def place_block(land, block, dev, name):
    r, c = block.shape
    tr = min(r, 512)

    def body(dev_ref, land_ref, b_ref, o_ref):
        del dev_ref, land_ref
        o_ref[...] = b_ref[...]

    return pl.pallas_call(
        body, name=name,
        grid_spec=pltpu.PrefetchScalarGridSpec(
            num_scalar_prefetch=1, grid=(r // tr,),
            in_specs=[_ANY, pl.BlockSpec((tr, c), lambda i, dev_ref: (i, 0))],
            out_specs=pl.BlockSpec((None, tr, c), lambda i, dev_ref: (dev_ref[0], i, 0))),
        out_shape=jax.ShapeDtypeStruct(land.shape, land.dtype),
        input_output_aliases={1: 0},
        compiler_params=_params("parallel"),
    )(dev, land, block)


def pair_add(own, recv, core, name):
    _, _, r, c = own.shape
    tr = min(r, 512)

    def body(core_ref, own_ref, recv_ref, o_ref):
        del core_ref
        o_ref[...] = (own_ref[...].astype(F32) + recv_ref[...].astype(F32)).astype(BF16)

    return pl.pallas_call(
        body, name=name,
        grid_spec=pltpu.PrefetchScalarGridSpec(
            num_scalar_prefetch=1, grid=(4, r // tr),
            in_specs=[pl.BlockSpec((None, None, tr, c), lambda k, i, core_ref: (k, core_ref[0], i, 0)),
                      pl.BlockSpec((None, None, tr, c), lambda k, i, core_ref: (k, 0, i, 0))],
            out_specs=pl.BlockSpec((None, tr, c), lambda k, i, core_ref: (k, i, 0))),
        out_shape=jax.ShapeDtypeStruct((4, r, c), BF16),
        compiler_params=_params("parallel", "parallel"),
    )(core, own, recv)


def _adamw_math(w, g, m, v):
    m2 = ADAM_B1 * m + (1.0 - ADAM_B1) * g
    v2 = ADAM_B2 * v + (1.0 - ADAM_B2) * (g * g)
    m_hat = m2 / (1.0 - ADAM_B1 ** ADAM_STEP)
    v_hat = v2 / (1.0 - ADAM_B2 ** ADAM_STEP)
    delta = -ADAM_LR * (m_hat / (jnp.sqrt(v_hat) + ADAM_EPS) + ADAM_WD * w)
    return delta, m2, v2


def adamw_big(w, m, v, own, got, chip, name):
    r, c = w.shape
    tr = min(r, 256)

    def body(chip_ref, w_ref, m_ref, v_ref, p0, p1, p2, p3, g_ref, d_ref, m2_ref, v2_ref):
        del chip_ref
        g = ((p0[...].astype(F32) + p1[...].astype(F32)) + p2[...].astype(F32)) + p3[...].astype(F32)
        delta, m2, v2 = _adamw_math(w_ref[...], g, m_ref[...], v_ref[...])
        g_ref[...] = g
        d_ref[...] = delta
        m2_ref[...] = m2
        v2_ref[...] = v2

    row = pl.BlockSpec((tr, c), lambda i, chip_ref: (i, 0))

    def slab(flip):
        return pl.BlockSpec((None, tr, c), lambda i, chip_ref: (chip_ref[0] ^ flip, i, 0))

    return pl.pallas_call(
        body, name=name,
        grid_spec=pltpu.PrefetchScalarGridSpec(
            num_scalar_prefetch=1, grid=(r // tr,),
            in_specs=[row, row, row, slab(0), slab(1), slab(2), slab(3)],
            out_specs=[row] * 4),
        out_shape=[jax.ShapeDtypeStruct((r, c), F32)] * 4,
        compiler_params=_params("parallel"),
    )(chip, w, m, v, own, got, got, got)


def sum_devices(g8, name):
    def body(g_ref, o_ref):
        tot = g_ref[0]
        for k in range(1, N_DEV):
            tot = tot + g_ref[k]
        o_ref[...] = tot

    return pl.pallas_call(body, name=name, out_shape=jax.ShapeDtypeStruct(g8.shape[1:], F32))(g8)


def adamw_small(ws, gs, ms, vs, name):
    n = len(ws)

    def body(*refs):
        w_r, g_r, m_r, v_r = refs[:n], refs[n:2 * n], refs[2 * n:3 * n], refs[3 * n:4 * n]
        d_o, m_o, v_o = refs[4 * n:5 * n], refs[5 * n:6 * n], refs[6 * n:7 * n]
        for k in range(n):
            delta, m2, v2 = _adamw_math(w_r[k][...], g_r[k][...], m_r[k][...], v_r[k][...])
            d_o[k][...] = delta
            m_o[k][...] = m2
            v_o[k][...] = v2

    shapes = [jax.ShapeDtypeStruct(w.shape, F32) for w in ws]
    outs = pl.pallas_call(body, name=name, out_shape=shapes * 3)(*ws, *gs, *ms, *vs)
    return outs[:n], outs[n:2 * n], outs[2 * n:]


def _rows128(a):
    return a.reshape(-1, 128)


def _pad_rows(a, rows):
    return jnp.pad(a, ((0, rows - a.shape[0]), (0, 0)))


def kernel(x, ln_pre_even, w_in_even, pool_w, pool_scale, w_out_even, ln_post_even, ln_pre_odd, w_in_odd, sconv_w, dconv_w, dconv_b, cnorm_g, cnorm_b, w_out_odd, ln_post_odd, loss_target, m_ln_pre_even, m_w_in_even, m_pool_w, m_pool_scale, m_w_out_even, m_ln_post_even, m_ln_pre_odd, m_w_in_odd, m_sconv_w, m_dconv_w, m_dconv_b, m_cnorm_g, m_cnorm_b, m_w_out_odd, m_ln_post_odd, v_ln_pre_even, v_w_in_even, v_pool_w, v_pool_scale, v_w_out_even, v_ln_post_even, v_ln_pre_odd, v_w_in_odd, v_sconv_w, v_dconv_w, v_dconv_b, v_cnorm_g, v_cnorm_b, v_w_out_odd, v_ln_post_odd):
    xs = x[0]
    tgt = loss_target[0]
    s, d = xs.shape
    half = d // 2
    n_heads = half // HEAD_DIM
    ng = len(POOL_WINDOWS)
    cwp = half // ng
    dev = 4 * lax.axis_index("x") + 2 * lax.axis_index("y") + lax.axis_index("c")
    core = lax.axis_index("c").astype(jnp.int32).reshape(1)

    pr = pool_w.shape[2]
    cl = sconv_w.shape[2]
    small_parts = [(_rows128(ln_pre_odd), 8), (sconv_w[0], 8), (dconv_w[0], 32), (dconv_b, 8),
                   (cnorm_g, 8), (cnorm_b, 8), (_rows128(ln_post_odd), 8)]
    small_local = jnp.concatenate([_pad_rows(a, r) for a, r in small_parts], axis=0)
    h0, p0, g_wie, (g_pw, g_small) = in_proj_gathered(
        xs, ln_pre_even, w_in_even[0].astype(BF16), [pool_w[0].reshape(ng * pr, cwp).astype(BF16), small_local],
        "ag_in_proj_even")
    comm = _Exchanges(dev, core, d)
    token = comm.start_weights("out_even", [w_out_even[0].astype(BF16)], [p0])
    token = comm.start_weights("in_odd", [w_in_odd[0].astype(BF16)], [token])
    sb_dep = comm.start_weights("out_odd", [w_out_odd[0].astype(BF16)], [token])
    pool_full = g_pw.reshape(N_DEV, ng, pr, cwp).transpose(1, 0, 2, 3).reshape(ng, cwp, cwp)
    nl = ln_pre_odd.shape[1] // 128

    def chan(lo, rows):
        return g_small[:, lo:lo + rows].transpose(1, 0, 2).reshape(rows, N_DEV * cl)

    ln_pre_odd_f = g_small[:, 0:nl].reshape(1, d)
    sconv_f = chan(8, SCONV_K)
    dconv_f = chan(16, CONF_K)
    dconv_b_f = chan(48, 1)
    cnorm_g_f = chan(56, 1)
    cnorm_b_f = chan(64, 1)
    ln_post_odd_f = g_small[:, 72:72 + nl].reshape(1, d)

    loss_blk, grad_x, small_g = _fwd_bwd(
        xs, tgt, ln_pre_even, h0, p0, g_wie, pool_full, pool_scale, ln_post_even, ln_pre_odd_f,
        sconv_f, dconv_f, dconv_b_f, cnorm_g_f, cnorm_b_f, ln_post_odd_f, comm, sb_dep)
    small_w = [ln_pre_even, pool_scale, ln_post_even, ln_pre_odd, sconv_w[0], dconv_w[0], dconv_b, cnorm_g, cnorm_b, ln_post_odd]
    small_m = [m_ln_pre_even, m_pool_scale, m_ln_post_even, m_ln_pre_odd, m_sconv_w[0], m_dconv_w[0], m_dconv_b, m_cnorm_g, m_cnorm_b, m_ln_post_odd]
    small_v = [v_ln_pre_even, v_pool_scale, v_ln_post_even, v_ln_pre_odd, v_sconv_w[0], v_dconv_w[0], v_dconv_b, v_cnorm_g, v_cnorm_b, v_ln_post_odd]
    big = {"w_in_even": (w_in_even, m_w_in_even, v_w_in_even), "pool_w": (pool_w, m_pool_w, v_pool_w),
           "w_out_even": (w_out_even, m_w_out_even, v_w_out_even), "w_in_odd": (w_in_odd, m_w_in_odd, v_w_in_odd),
           "w_out_odd": (w_out_odd, m_w_out_odd, v_w_out_odd)}
    upd = comm.finish_updates(big, [grad_x])
    upd.update(comm.finish_updates(big, [grad_x]))
    sg, sd, sm, sv, loss = _update_small(small_g, loss_blk, small_w, small_m, small_v, dev, d, cl,
                                         deps=[upd["w_in_odd"][1], upd["w_out_even"][1]])
    upd.update(comm.finish_updates(big, sd))
    (g_wie_o, d_wie, m_wie, v_wie), (g_pw_o, d_pw, m_pw, v_pw) = upd["w_in_even"], upd["pool_w"]
    (g_woe_o, d_woe, m_woe, v_woe), (g_wio_o, d_wio, m_wio, v_wio) = upd["w_out_even"], upd["w_in_odd"]
    g_woo_o, d_woo, m_woo, v_woo = upd["w_out_odd"]

    def order(small, wie, pw, woe, wio, woo):
        return [small[0], wie, pw, small[1], woe, small[2], small[3], wio, small[4], small[5], small[6],
                small[7], small[8], woo, small[9]]

    grads = order(sg, g_wie_o, g_pw_o, g_woe_o, g_wio_o, g_woo_o)
    deltas = order(sd, d_wie, d_pw, d_woe, d_wio, d_woo)
    new_m = order(sm, m_wie, m_pw, m_woe, m_wio, m_woo)
    new_v = order(sv, v_wie, v_pw, v_woe, v_wio, v_woo)
    return (loss, grad_x[None], *grads, *deltas, *new_m, *new_v)


def _fwd_bwd(xs, tgt, ln_pre_even, h0, p0, g_wie, pool_full, pool_scale, ln_post_even, ln_pre_odd_f,
             sconv_f, dconv_f, dconv_b_f, cnorm_g_f, cnorm_b_f, ln_post_odd_f, comm, sb_dep):
    d = xs.shape[1]
    n_heads = d // 2 // HEAD_DIM
    ng, cwp = pool_full.shape[0], pool_full.shape[1]
    a0, sb_wts = sb_fwd(p0, n_heads, "sb_fwd", dep=sb_dep)
    dep = comm.weights_arrived("out_even", after=a0)
    y0 = even_mix_fwd(a0, p0, pool_full, pool_scale, "even_mix_fwd", dep=dep)
    (w_out_e,) = comm.weights("out_even", after=y0)
    w_out_e = w_out_e.reshape(1, d, d)
    o0 = mm_nn(y0, w_out_e, BF16, "out_proj_even", tn=512)
    dep = comm.weights_arrived("in_odd", after=o0)
    x1, h1 = postnorm_fwd(xs, o0, ln_post_even, ln_pre_odd_f, "post_even", dep=dep)
    (g_wio,) = comm.weights("in_odd", after=x1)
    p1 = mm_nn(h1, g_wio, BF16, "in_proj_odd", group=2)
    dep = comm.weights_arrived("out_odd", after=p1)
    y1, dc = odd_mix_fwd(p1, sconv_f, dconv_f, dconv_b_f, cnorm_g_f, cnorm_b_f, "odd_mix_fwd", dep=dep)
    (w_out_o,) = comm.weights("out_odd", after=y1)
    w_out_o = w_out_o.reshape(1, d, d)
    o1 = mm_nn(y1, w_out_o, BF16, "out_proj_odd", tn=512)
    loss_blk, gx2, do1, dg_post_odd = final_fwd_bwd(x1, o1, ln_post_odd_f, tgt, "post_odd_loss")

    dw_out_o = mm_tn(y1, do1, 1, BF16, "dw_out_odd")
    dy1 = mm_nt(do1, w_out_o, BF16, "dy_odd")
    ddc, dg2, dgam, dbet = odd_bwd_ln(dy1, p1, dc, cnorm_g_f, cnorm_b_f, "odd_bwd_ln")
    dp1, dsconv, ddconv, ddconv_b = odd_bwd_conv(dy1, p1, ddc, dg2, sconv_f, dconv_f, "odd_bwd_conv")
    dw_in_o = mm_tn(h1, dp1, N_DEV, BF16, "dw_in_odd", group=2)
    dep = comm.reduce_begin({"w_out_odd": dw_out_o.reshape(N_DEV, d // N_DEV, d), "w_in_odd": dw_in_o}, "odd")
    dh1 = mm_nt(dp1, g_wio, BF16, "dh_odd", dep=dep, group=2)
    dep = comm.reduce_send(after=dh1)
    gx1, dg_pre_odd, do0, dg_post_even = norm_bwd(dh1, x1, ln_pre_odd_f, gx2, "pre_odd_post_even_bwd",
                                                  inp2=o0, g2=ln_post_even, dep=dep)

    dw_out_e = mm_tn(y0, do0, 1, BF16, "dw_out_even")
    dy0 = mm_nt(do0, w_out_e, BF16, "dy_even")
    da0, du0, dg0, dpool, dpool_scale = even_mix_bwd(dy0, a0, p0, pool_full, pool_scale, "even_mix_bwd")
    pr = cwp // N_DEV
    dpool_slabs = dpool.astype(BF16).reshape(ng, N_DEV, pr, cwp).transpose(1, 0, 2, 3).reshape(N_DEV, ng * pr, cwp)
    dep = comm.reduce_begin({"w_out_even": dw_out_e.reshape(N_DEV, d // N_DEV, d), "pool_w": dpool_slabs}, "even_out")
    dq0, dk0, dv0 = sb_bwd(p0, a0, sb_wts, da0, n_heads, "sb_bwd", dep=dep)
    dep = comm.reduce_send(after=dq0)
    dp0 = jnp.concatenate([dq0, dk0, dv0, du0, dg0], axis=1)
    dw_sibling = mm_tn(h0, dp0, N_DEV // 2, BF16, "dw_in_even_sibling", dep=dep, pick=(2, 1 - comm.core))
    dep = comm.reduce_begin({"w_in_even": dw_sibling}, "even_in", sibling_part=True)
    dw_own = mm_tn(h0, dp0, N_DEV // 2, BF16, "dw_in_even_own", dep=dep, pick=(2, comm.core))
    dep = comm.reduce_send(after=dw_own, own_part={"w_in_even": dw_own})
    dh0 = mm_nt(dp0, g_wie, BF16, "dh_even", dep=dep, group=2)
    dep = None
    grad_x, dg_pre_even = norm_bwd(dh0, xs, ln_pre_even, gx1, "pre_even_bwd", tm=512, dep=dep)
    small_g = [dg_pre_even, dpool_scale, dg_post_even, dg_pre_odd, dsconv, ddconv, ddconv_b, dgam, dbet, dg_post_odd]
    return loss_blk, grad_x, small_g


class _Exchanges:
    def __init__(self, dev, core, d):
        self.dev = dev.astype(jnp.int32).reshape(1)
        self.core = core
        self.chip = (dev // 2).astype(jnp.int32).reshape(1)
        self.d = d
        self.in_flight = {}
        self.to_sibling = None
        self.pending = []

    def start_weights(self, tag, blocks, afters):
        lands = [lax.empty((N_DEV,) + b.shape, b.dtype) for b in blocks]
        send, recv, srcs, lands, token = split_start("gather", blocks, lands, afters, "ag_start_" + tag)
        self.in_flight[tag] = (send, recv, srcs, lands)
        return token

    def weights_arrived(self, tag, after):
        send, recv, srcs, lands = self.in_flight.pop(tag)
        srcs, lands = split_wait("gather", send, recv, srcs, lands, [after], "ag_wait_" + tag)
        lands = [place_block(l, b, self.dev, "ag_own_%s_%d" % (tag, k)) for k, (l, b) in enumerate(zip(lands, srcs))]
        lands = [l.reshape((4, 2) + l.shape[1:]) for l in lands]
        send, recv, _, lands, token = split_start("halves", [], lands, [], "ag_sibling_start_" + tag)
        self.in_flight[tag] = (send, recv, lands)
        return token

    def weights(self, tag, after):
        send, recv, lands = self.in_flight.pop(tag)
        _, lands = split_wait("halves", send, recv, [], lands, [after], "ag_sibling_wait_" + tag)
        return [l.reshape((N_DEV,) + l.shape[2:]) for l in lands]

    def reduce_begin(self, partials, tag, sibling_part=False):
        names = list(partials)
        arrs = [partials[k].reshape((4, 1 if sibling_part else 2) + partials[k].shape[1:]) for k in names]
        lands = [lax.empty((4, 1) + a.shape[2:], a.dtype) for a in arrs]
        send, recv, srcs, lands, token = split_start("sibling", arrs, lands, [], "rs_sibling_start_" + tag)
        self.to_sibling = (tag, names, send, recv, srcs, lands)
        return token

    def reduce_send(self, after, own_part=None):
        tag, names, send, recv, srcs, lands = self.to_sibling
        srcs, lands = split_wait("sibling", send, recv, srcs, lands, [after], "rs_sibling_wait_" + tag)
        which = self.core
        if own_part is not None:
            srcs = [own_part[k].reshape((4, 1) + own_part[k].shape[1:]) for k in names]
            which = jnp.zeros((1,), jnp.int32)
        sums = [pair_add(o, r, which, "rs_pair_add_" + k) for k, o, r in zip(names, srcs, lands)]
        zones = [lax.empty(a.shape, a.dtype) for a in sums]
        send, recv, srcs, zones, token = split_start("scatter", sums, zones, [], "rs_start_" + tag)
        self.pending.append((tag, names, send, recv, srcs, zones))
        return token

    def finish_updates(self, big, afters):
        tag, names, send, recv, srcs, lands = self.pending.pop(0)
        srcs, lands = split_wait("scatter", send, recv, srcs, lands, afters, "rs_wait_" + tag)
        out = {}
        for name, own, got in zip(names, srcs, lands):
            w, m, v = big[name]
            shp = own.shape[1:]
            outs = adamw_big(w.reshape(shp), m.reshape(shp), v.reshape(shp), own, got, self.chip, "adamw_" + name)
            out[name] = [o.reshape(w.shape) for o in outs]
        return out


def _update_small(small_g, loss_blk, small_w, small_m, small_v, dev, d, cl, deps):
    packed = jnp.concatenate([_rows128(g) for g in small_g] + [loss_blk], axis=0)
    (g8,) = all_gather([packed], "ag_small_grads", deps)
    tot = sum_devices(g8, "sum_small_grads")
    loss = tot[packed.shape[0] - 8, 0]
    full_g = []
    lo = 0
    for g in small_g:
        rows = g.size // 128
        full_g.append(tot[lo:lo + rows].reshape(g.shape))
        lo += rows

    def mine(g, width):
        return lax.dynamic_slice_in_dim(g, dev * width, width, axis=g.ndim - 1)

    fg = full_g
    small_gl = [fg[0], fg[1], fg[2], mine(fg[3], d // N_DEV), mine(fg[4], cl), mine(fg[5], cl), mine(fg[6], cl),
                mine(fg[7], cl), mine(fg[8], cl), mine(fg[9], d // N_DEV)]
    sd, sm, sv = adamw_small(small_w, small_gl, small_m, small_v, "adamw_small")

    def like(k, a):
        return a[None] if k in (4, 5) else a

    sg = [like(k, a) for k, a in enumerate(small_gl)]
    sd = [like(k, a) for k, a in enumerate(sd)]
    sm = [like(k, a) for k, a in enumerate(sm)]
    sv = [like(k, a) for k, a in enumerate(sv)]
    return sg, sd, sm, sv, loss
```

```python
import functools
import math

import jax
import jax.numpy as jnp
from jax import lax
from jax.experimental import pallas as pl
from jax.experimental.pallas import tpu as pltpu

F32 = jnp.float32
BF16 = jnp.bfloat16
EPS = 1e-6
HEAD_DIM = 128
POOL_WINDOWS = (2, 4, 8, 16)
SCONV_K = 3
CONF_K = 31
HALO = 32
N_DEV = 8
VMEM_LIMIT = 56 * 1024 * 1024
MESH = pl.DeviceIdType.MESH

ADAM_LR = 0.001
ADAM_B1 = 0.9
ADAM_B2 = 0.999
ADAM_EPS = 1e-08
ADAM_WD = 0.01
ADAM_STEP = 10


def _params(*sem):
    return pltpu.CompilerParams(dimension_semantics=sem, vmem_limit_bytes=VMEM_LIMIT)


def _sigmoid(v):
    return 1.0 / (1.0 + jnp.exp(-v))


def _silu(v):
    return v * _sigmoid(v)


def _silu_and_grad(v):
    s = _sigmoid(v)
    return v * s, s * (1.0 + v * (1.0 - s))


def _rowsum8(v):
    r, c = v.shape
    return jnp.sum(v.reshape(r // 8, 8, c), axis=0)


SUBLANES = 8


class _Taps:
    def __init__(self, xx, rows, before):
        self.xx, self.rows, self.before, self.rotated = xx, rows, before, {}

    def __call__(self, i):
        r, q = i % SUBLANES, i // SUBLANES
        if r not in self.rotated:
            n = self.xx.shape[0]
            self.rotated[r] = self.xx if r == 0 else pltpu.roll(self.xx, r if self.before else n - r, 0)
        lo = HALO - SUBLANES * q if self.before else SUBLANES * q
        return self.rotated[r][lo:lo + self.rows]


def _window_sum(xx, win, before):
    n = xx.shape[0]
    acc = xx
    k = 1
    while k < win:
        acc = acc + pltpu.roll(acc, k if before else n - k, 0)
        k *= 2
    return acc


def postnorm_fwd(x, o, g, g_next, name, tm=512, dep=None):
    s, d = x.shape
    dep_args, dep_specs = _after(dep)

    def body(x_ref, o_ref, g_ref, gn_ref, *rest):
        y_ref, h_ref = rest[-2:]
        ov = o_ref[...].astype(F32)
        r = lax.rsqrt(jnp.mean(ov * ov, axis=-1, keepdims=True) + EPS)
        y = x_ref[...] + ov * r * g_ref[...]
        y_ref[...] = y
        r2 = lax.rsqrt(jnp.mean(y * y, axis=-1, keepdims=True) + EPS)
        h_ref[...] = (y * r2 * gn_ref[...]).astype(BF16)

    row = pl.BlockSpec((tm, d), lambda i: (i, 0))
    vec = pl.BlockSpec((1, d), lambda i: (0, 0))
    return pl.pallas_call(
        body, name=name, grid=(s // tm,),
        in_specs=[row, row, vec, vec] + dep_specs, out_specs=[row, row],
        out_shape=[jax.ShapeDtypeStruct((s, d), F32), jax.ShapeDtypeStruct((s, d), BF16)],
        compiler_params=_params("parallel"),
    )(x, o, g, g_next, *dep_args)


def final_fwd_bwd(x1, o, g, target, name, tm=512):
    s, d = x1.shape
    n = s // tm

    def body(x_ref, o_ref, g_ref, t_ref, loss_ref, gx_ref, do_ref, dg_ref, lacc, gacc):
        i = pl.program_id(0)

        @pl.when(i == 0)
        def _():
            lacc[...] = jnp.zeros_like(lacc)
            gacc[...] = jnp.zeros_like(gacc)

        ov = o_ref[...].astype(F32)
        gv = g_ref[...]
        r = lax.rsqrt(jnp.mean(ov * ov, axis=-1, keepdims=True) + EPS)
        oh = ov * r
        diff = x_ref[...] + oh * gv - t_ref[...]
        lacc[...] += _rowsum8(diff * diff)
        gx = diff * (1.0 / d)
        gx_ref[...] = gx
        gacc[...] += _rowsum8(gx * oh)
        dn = gx * gv
        do_ref[...] = (r * (dn - oh * jnp.mean(dn * oh, axis=-1, keepdims=True))).astype(BF16)

        @pl.when(i == n - 1)
        def _():
            tot = jnp.sum(jnp.sum(lacc[...], axis=0, keepdims=True), axis=1, keepdims=True)
            loss_ref[...] = jnp.broadcast_to(tot * (0.5 / d), loss_ref.shape)
            dg_ref[...] = jnp.sum(gacc[...], axis=0, keepdims=True)

    row = pl.BlockSpec((tm, d), lambda i: (i, 0))
    vec = pl.BlockSpec((1, d), lambda i: (0, 0))
    return pl.pallas_call(
        body, name=name, grid=(n,),
        in_specs=[row, row, vec, row],
        out_specs=[pl.BlockSpec((8, 128), lambda i: (0, 0)), row, row, vec],
        out_shape=[jax.ShapeDtypeStruct((8, 128), F32), jax.ShapeDtypeStruct((s, d), F32),
                   jax.ShapeDtypeStruct((s, d), BF16), jax.ShapeDtypeStruct((1, d), F32)],
        scratch_shapes=[pltpu.VMEM((8, d), F32), pltpu.VMEM((8, d), F32)],
        compiler_params=_params("arbitrary"),
    )(x1, o, g, target)


def _rms_bwd_rows(dyv, xv, gv):
    r = lax.rsqrt(jnp.mean(xv * xv, axis=-1, keepdims=True) + EPS)
    xh = xv * r
    dn = dyv * gv
    return r * (dn - xh * jnp.mean(dn * xh, axis=-1, keepdims=True)), _rowsum8(dyv * xh)


def norm_bwd(dy, inp, g, resid, name, inp2=None, g2=None, tm=256, dep=None):
    s, d = inp.shape
    n = s // tm
    chain = inp2 is not None

    def body(*refs):
        dy_ref, x_ref, g_ref, r_ref = refs[:4]
        outs = refs[-6:] if chain else refs[-3:]
        i = pl.program_id(0)

        @pl.when(i == 0)
        def _():
            for acc in outs[-2:] if chain else outs[-1:]:
                acc[...] = jnp.zeros_like(acc)

        if chain:
            x2_ref, g2_ref = refs[4:6]
            dx_ref, dg_ref, dx2_ref, dg2_ref, gacc, gacc2 = outs
        else:
            dx_ref, dg_ref, gacc = outs
        dx, dg_rows = _rms_bwd_rows(dy_ref[...].astype(F32), x_ref[...], g_ref[...])
        dx = dx + r_ref[...]
        dx_ref[...] = dx
        gacc[...] += dg_rows
        if chain:
            dx2, dg2_rows = _rms_bwd_rows(dx, x2_ref[...].astype(F32), g2_ref[...])
            dx2_ref[...] = dx2.astype(BF16)
            gacc2[...] += dg2_rows

        @pl.when(i == n - 1)
        def _():
            dg_ref[...] = jnp.sum(gacc[...], axis=0, keepdims=True)
            if chain:
                dg2_ref[...] = jnp.sum(gacc2[...], axis=0, keepdims=True)

    row = pl.BlockSpec((tm, d), lambda i: (i, 0))
    vec = pl.BlockSpec((1, d), lambda i: (0, 0))
    dep_args, dep_specs = _after(dep)
    extra = [inp2, g2] if chain else []
    return pl.pallas_call(
        body, name=name, grid=(n,),
        in_specs=[row, row, vec, row] + ([row, vec] if chain else []) + dep_specs,
        out_specs=[row, vec] * (2 if chain else 1),
        out_shape=[jax.ShapeDtypeStruct((s, d), F32), jax.ShapeDtypeStruct((1, d), F32)]
        + ([jax.ShapeDtypeStruct((s, d), BF16), jax.ShapeDtypeStruct((1, d), F32)] if chain else []),
        scratch_shapes=[pltpu.VMEM((8, d), F32)] * (2 if chain else 1),
        compiler_params=_params("arbitrary"),
    )(dy, inp, g, resid, *extra, *dep_args)


def _after(dep):
    if dep is None:
        return [], []
    return [dep], [pl.BlockSpec((8, 128), lambda *_: (0, 0))]


def _lane_concat(ref, count):
    return ref[0] if count == 1 else jnp.concatenate([ref[i] for i in range(count)], axis=1)


def mm_nn(a, w, out_dtype, name, tm=2048, tn=None, dep=None, group=1):
    m, k = a.shape
    tm = min(tm, m)
    ns, _, n = w.shape
    tn = n if tn is None else tn
    nj = n // tn
    assert group == 1 or nj == 1
    dep_args, dep_specs = _after(dep)

    def body(a_ref, w_ref, *rest):
        o_ref = rest[-1]
        o_ref[...] = jnp.dot(a_ref[...], _lane_concat(w_ref, group), preferred_element_type=F32).astype(out_dtype)

    return pl.pallas_call(
        body, name=name, grid=(ns // group, nj, m // tm),
        in_specs=[pl.BlockSpec((tm, k), lambda s, j, i: (i, 0)),
                  pl.BlockSpec((group, k, tn), lambda s, j, i: (s, 0, j))] + dep_specs,
        out_specs=pl.BlockSpec((tm, group * tn), lambda s, j, i: (i, s * nj + j)),
        out_shape=jax.ShapeDtypeStruct((m, ns * n), out_dtype),
        compiler_params=_params("parallel", "parallel", "parallel"),
    )(a, w, *dep_args)


def mm_nt(a, w, out_dtype, name, tm=1024, tn=None, dep=None, group=1):
    m = a.shape[0]
    tm = min(tm, m)
    ns, k, n = w.shape
    tn = n if tn is None else tn
    nj = n // tn
    assert group == 1 or nj == 1
    steps = ns * nj // group
    dep_args, dep_specs = _after(dep)

    def body(a_ref, w_ref, *rest):
        o_ref, acc = rest[-2:]
        r = pl.program_id(1)

        @pl.when(r == 0)
        def _():
            acc[...] = jnp.zeros_like(acc)

        acc[...] += lax.dot_general(a_ref[...], _lane_concat(w_ref, group), (((1,), (1,)), ((), ())),
                                    preferred_element_type=F32)

        @pl.when(r == steps - 1)
        def _():
            o_ref[...] = acc[...].astype(out_dtype)

    return pl.pallas_call(
        body, name=name, grid=(m // tm, steps),
        in_specs=[pl.BlockSpec((tm, group * tn), lambda i, r: (i, r)),
                  pl.BlockSpec((group, k, tn), lambda i, r: (r // nj, 0, r % nj))] + dep_specs,
        out_specs=pl.BlockSpec((tm, k), lambda i, r: (i, 0)),
        out_shape=jax.ShapeDtypeStruct((m, k), out_dtype),
        scratch_shapes=[pltpu.VMEM((tm, k), F32)],
        compiler_params=_params("parallel", "arbitrary"),
    )(a, w, *dep_args)


def mm_tn(a, b, ns, out_dtype, name, tk=1024, tm=2048, dep=None, pick=None, group=1):
    m, k = a.shape
    tm = min(tm, m)
    step, offset = (1, None) if pick is None else pick
    assert group == 1 or pick is None
    n = b.shape[1] // (ns * step)
    steps = m // tm
    dep_args, dep_specs = _after(dep)
    n_pre = 0 if pick is None else 1

    def b_block(s, j, r, *pre):
        return (r, s if pick is None else step * s + pre[0][0])

    def body(*refs):
        a_ref, b_ref = refs[n_pre:n_pre + 2]
        o_ref, acc = refs[-2:]
        r = pl.program_id(2)

        @pl.when(r == 0)
        def _():
            acc[...] = jnp.zeros_like(acc)

        acc[...] += lax.dot_general(a_ref[...], b_ref[...], (((0,), (0,)), ((), ())),
                                    preferred_element_type=F32)

        @pl.when(r == steps - 1)
        def _():
            for i in range(group):
                o_ref[i] = acc[:, i * n:(i + 1) * n].astype(out_dtype)

    return pl.pallas_call(
        body, name=name,
        grid_spec=pltpu.PrefetchScalarGridSpec(
            num_scalar_prefetch=n_pre, grid=(ns // group, k // tk, steps),
            in_specs=[pl.BlockSpec((tm, tk), lambda s, j, r, *pre: (r, j)),
                      pl.BlockSpec((tm, group * n), b_block)] + dep_specs,
            out_specs=pl.BlockSpec((group, tk, n), lambda s, j, r, *pre: (s, j, 0)),
            scratch_shapes=[pltpu.VMEM((tk, group * n), F32)]),
        out_shape=jax.ShapeDtypeStruct((ns, k, n), out_dtype),
        compiler_params=_params("parallel", "parallel", "arbitrary"),
    )(*([] if pick is None else [offset]), a, b, *dep_args)


SB_BLK = 128


LOG2E = 1.0 / math.log(2.0)


def _split_dot(v, tri2):
    hi = pltpu.bitcast(pltpu.bitcast(v, jnp.uint32) & jnp.uint32(0xFFFF0000), F32)
    lo = (v - hi).astype(BF16)
    return jnp.dot(jnp.concatenate([hi.astype(BF16), lo], axis=1), tri2, preferred_element_type=F32)


def _sb_scores(z2, lim, dcol, tri_ex, masked):
    sp = jnp.log2(1.0 + jnp.exp2(-jnp.abs(z2)))
    lb = jnp.minimum(z2, 0.0) - sp
    l1m = lb - z2
    mask = None
    if masked:
        mask = dcol < lim
        l1m = jnp.where(mask, l1m, 0.0)
    return mask, lb, l1m, _split_dot(l1m, tri_ex)


def _sb_consts():
    row = lax.broadcasted_iota(jnp.int32, (SB_BLK, SB_BLK), 0)
    col = lax.broadcasted_iota(jnp.int32, (SB_BLK, SB_BLK), 1)
    tri_ex = jnp.where(row > col, 1.0, 0.0).astype(BF16)
    tri_in = jnp.where(row >= col, 1.0, 0.0).astype(BF16)
    return col - row, jnp.concatenate([tri_ex, tri_ex], axis=0), jnp.concatenate([tri_in, tri_in], axis=0)


def sb_fwd(p, n_heads, name, tq=1024, nsub=8, dep=None):
    s = p.shape[0]
    h_n = n_heads
    b = SB_BLK
    nqs = tq // b
    tk = nsub * b
    scale = 1.0 / math.sqrt(HEAD_DIM)

    dep_args, dep_specs = _after(dep)

    def body(q_ref, k_ref, v_ref, *rest):
        o_ref, w_ref = rest[-2:]
        qi = pl.program_id(1)
        dcol, tri_ex, _ = _sb_consts()
        qv = [q_ref[qs * b:(qs + 1) * b, :] for qs in range(nqs)]
        n_groups = ((qi + 1) * nqs - 1) // nsub + 1

        def step(it, carry, masked):
            c1s, accs = carry
            g = n_groups - 1 - it
            off = pl.multiple_of(g * tk, tk)
            kg = k_ref[pl.ds(off, tk), :]
            vg = v_ref[pl.ds(off, tk), :]
            new_c1, new_acc = [], []
            for qs in range(nqs):
                qb = qi * nqs + qs
                z2 = lax.dot_general(qv[qs], kg, (((1,), (1,)), ((), ())),
                                     preferred_element_type=F32) * (scale * LOG2E)
                blocks = [_sb_scores(z2[:, j * b:(j + 1) * b], (qb - (g * nsub + j)) * b, dcol, tri_ex, masked)
                          for j in range(nsub)]
                run = c1s[qs]
                ws = [None] * nsub
                for j in reversed(range(nsub)):
                    mask, lb, l1m, ls_loc = blocks[j]
                    wj = jnp.exp2(lb + ls_loc + run)
                    ws[j] = (jnp.where(mask, wj, 0.0) if masked else wj).astype(BF16)
                    run = run + jnp.sum(l1m, axis=1, keepdims=True)
                w = jnp.concatenate(ws, axis=1)
                w_ref[0, g, qs * b:(qs + 1) * b, :] = w
                new_acc.append(accs[qs] + jnp.dot(w, vg, preferred_element_type=F32))
                new_c1.append(run)
            return tuple(new_c1), tuple(new_acc)

        init = (tuple(jnp.zeros((b, 1), F32) for _ in range(nqs)),
                tuple(jnp.zeros((b, HEAD_DIM), F32) for _ in range(nqs)))
        assert all(((i + 1) * nqs - 1) // nsub * nsub <= i * nqs for i in range(s // tq))
        first = step(0, init, True)
        _, accs = lax.fori_loop(1, n_groups, functools.partial(step, masked=False), first)
        for qs in range(nqs):
            o_ref[qs * b:(qs + 1) * b, :] = accs[qs]

    return pl.pallas_call(
        body, name=name, grid=(h_n, s // tq),
        in_specs=[pl.BlockSpec((tq, HEAD_DIM), lambda h, i: (i, h)),
                  pl.BlockSpec((s, HEAD_DIM), lambda h, i: (0, h_n + h)),
                  pl.BlockSpec((s, HEAD_DIM), lambda h, i: (0, 2 * h_n + h))] + dep_specs,
        out_specs=[pl.BlockSpec((tq, HEAD_DIM), lambda h, i: (i, h)),
                   pl.BlockSpec((1, s // tk, tq, tk), lambda h, i: (h, 0, i, 0))],
        out_shape=[jax.ShapeDtypeStruct((s, h_n * HEAD_DIM), F32),
                   jax.ShapeDtypeStruct((h_n, s // tk, s, tk), BF16)],
        compiler_params=_params("parallel", "arbitrary"),
    )(p, p, p, *dep_args)


def sb_bwd(p, a, wts, da, n_heads, name, tq=1024, dep=None):
    s = p.shape[0]
    h_n = n_heads
    nq = s // tq
    b = SB_BLK
    nqs = tq // b
    tk = wts.shape[3]
    nsub = tk // b
    scale = 1.0 / math.sqrt(HEAD_DIM)
    dep_args, dep_specs = _after(dep)

    def body(q_ref, k_ref, v_ref, a_ref, da_ref, w_ref, *rest):
        dq_ref, dk_ref, dv_ref, dk_acc, dv_acc = rest[-5:]
        qi = pl.program_id(1)

        @pl.when(qi == 0)
        def _():
            dk_acc[...] = jnp.zeros_like(dk_acc)
            dv_acc[...] = jnp.zeros_like(dv_acc)

        dcol, _, tri_in = _sb_consts()
        q_all = q_ref[...]
        do_all = da_ref[...]
        qv = [q_ref[qs * b:(qs + 1) * b, :] for qs in range(nqs)]
        dov = [da_ref[qs * b:(qs + 1) * b, :] for qs in range(nqs)]
        tots = [jnp.sum(dov[qs].astype(F32) * a_ref[qs * b:(qs + 1) * b, :], axis=1, keepdims=True)
                for qs in range(nqs)]
        n_groups = ((qi + 1) * nqs - 1) // nsub + 1

        def step(it, carry, masked):
            c2s, dqs = carry
            g = n_groups - 1 - it
            off = pl.multiple_of(g * tk, tk)
            kg = k_ref[pl.ds(off, tk), :]
            vg = v_ref[pl.ds(off, tk), :]
            w_all = w_ref[0, g]
            new_c2, new_dq, dz_rows = [], [], []
            for qs in range(nqs):
                qb = qi * nqs + qs
                z2 = lax.dot_general(qv[qs], kg, (((1,), (1,)), ((), ())),
                                     preferred_element_type=F32) * (-scale * LOG2E)
                dw = lax.dot_general(dov[qs], vg, (((1,), (1,)), ((), ())), preferred_element_type=F32)
                beta = 1.0 / (1.0 + jnp.exp2(z2))
                e = dw * w_all[qs * b:(qs + 1) * b, :].astype(F32)
                run2 = c2s[qs]
                dzs = [None] * nsub
                for j in reversed(range(nsub)):
                    cols = slice(j * b, (j + 1) * b)
                    later = _split_dot(e[:, cols], tri_in) + run2
                    bj = beta[:, cols]
                    dz = (e[:, cols] * (1.0 - bj) - bj * (tots[qs] - later)) * scale
                    if masked:
                        dz = jnp.where(dcol < (qb - (g * nsub + j)) * b, dz, 0.0)
                    dzs[j] = dz.astype(BF16)
                    run2 = run2 + jnp.sum(e[:, cols], axis=1, keepdims=True)
                dzq = jnp.concatenate(dzs, axis=1)
                new_dq.append(dqs[qs] + jnp.dot(dzq, kg, preferred_element_type=F32))
                new_c2.append(run2)
                dz_rows.append(dzq)
            dz_all = jnp.concatenate(dz_rows, axis=0)
            dk_acc[pl.ds(off, tk), :] += lax.dot_general(dz_all, q_all, (((0,), (0,)), ((), ())),
                                                         preferred_element_type=F32)
            dv_acc[pl.ds(off, tk), :] += lax.dot_general(w_all, do_all, (((0,), (0,)), ((), ())),
                                                         preferred_element_type=F32)
            return tuple(new_c2), tuple(new_dq)

        zeros = tuple(jnp.zeros((b, 1), F32) for _ in range(nqs))
        assert all(((i + 1) * nqs - 1) // nsub * nsub <= i * nqs for i in range(s // tq))
        first = step(0, (zeros, tuple(jnp.zeros((b, HEAD_DIM), F32) for _ in range(nqs))), True)
        _, dqs = lax.fori_loop(1, n_groups, functools.partial(step, masked=False), first)
        for qs in range(nqs):
            dq_ref[qs * b:(qs + 1) * b, :] = dqs[qs].astype(BF16)

        @pl.when(qi == nq - 1)
        def _():
            dk_ref[...] = dk_acc[...].astype(BF16)
            dv_ref[...] = dv_acc[...].astype(BF16)

    blk = pl.BlockSpec((tq, HEAD_DIM), lambda h, i: (i, h))
    full = pl.BlockSpec((s, HEAD_DIM), lambda h, i: (0, h))
    return pl.pallas_call(
        body, name=name, grid=(h_n, nq),
        in_specs=[blk, pl.BlockSpec((s, HEAD_DIM), lambda h, i: (0, h_n + h)),
                  pl.BlockSpec((s, HEAD_DIM), lambda h, i: (0, 2 * h_n + h)), blk, blk,
                  pl.BlockSpec((1, s // tk, tq, tk), lambda h, i: (h, 0, i, 0))] + dep_specs,
        out_specs=[blk, full, full],
        out_shape=[jax.ShapeDtypeStruct((s, h_n * HEAD_DIM), BF16)] * 3,
        scratch_shapes=[pltpu.VMEM((s, HEAD_DIM), F32), pltpu.VMEM((s, HEAD_DIM), F32)],
        compiler_params=_params("parallel", "arbitrary"),
    )(p, p, p, a, da, wts, *dep_args)


def _pool_window(xx, win, r0, rc):
    cur = xx[HALO:HALO + rc]
    ws = _window_sum(xx, win, True)[HALO:HALO + rc]
    t_idx = r0 + lax.broadcasted_iota(jnp.int32, (rc, 1), 0)
    inv = 1.0 / jnp.minimum(win, t_idx + 1).astype(F32)
    return ws * inv - cur, inv


def even_mix_fwd(a, p, pool_w, pool_scale, name, rc=512, dep=None):
    s = p.shape[0]
    ng = len(POOL_WINDOWS)
    cw = pool_w.shape[1]
    n_chunks = s // rc
    dep_args, dep_specs = _after(dep)

    def body(a_ref, u_ref, g_ref, w_ref, sc_ref, *rest):
        y_ref, upad = rest[-2:]
        j = pl.program_id(0)

        @pl.when(j < ng)
        def _():
            def chunk(ci, carry):
                rows = pl.ds(pl.multiple_of(ci * rc, rc), rc)
                y_ref[rows, :] = (a_ref[rows, :] * _silu(g_ref[rows, :].astype(F32))).astype(BF16)
                return carry

            lax.fori_loop(0, n_chunks, chunk, 0)

        for gi, win in enumerate(POOL_WINDOWS):
            @pl.when(j == ng + gi)
            def _(win=win):
                upad[0:HALO, :] = jnp.zeros((HALO, cw), F32)

                def fill(ci, carry):
                    r0 = pl.multiple_of(ci * rc, rc)
                    upad[pl.ds(pl.multiple_of(r0 + HALO, HALO), rc), :] = u_ref[pl.ds(r0, rc), :].astype(F32)
                    return carry

                lax.fori_loop(0, n_chunks, fill, 0)

                def chunk(ci, carry):
                    r0 = pl.multiple_of(ci * rc, rc)
                    rows = pl.ds(r0, rc)
                    pooled, _ = _pool_window(upad[pl.ds(r0, HALO + rc), :], win, r0, rc)
                    t = jnp.dot(pooled.astype(BF16), w_ref[0], preferred_element_type=F32)
                    y_ref[rows, :] = (t * sc_ref[...] * _silu(g_ref[rows, :].astype(F32))).astype(BF16)
                    return carry

                lax.fori_loop(0, n_chunks, chunk, 0)

    grp = lambda j: jnp.maximum(j - ng, 0)
    return pl.pallas_call(
        body, name=name, grid=(2 * ng,),
        in_specs=[pl.BlockSpec((s, cw), lambda j: (0, jnp.minimum(j, ng - 1))),
                  pl.BlockSpec((s, cw), lambda j: (0, 3 * ng + grp(j))),
                  pl.BlockSpec((s, cw), lambda j: (0, 4 * ng + j)),
                  pl.BlockSpec((1, cw, cw), lambda j: (grp(j), 0, 0)),
                  pl.BlockSpec((1, cw), lambda j: (0, grp(j)))] + dep_specs,
        out_specs=pl.BlockSpec((s, cw), lambda j: (0, j)),
        out_shape=jax.ShapeDtypeStruct((s, 2 * ng * cw), BF16),
        scratch_shapes=[pltpu.VMEM((HALO + s, cw), F32)],
        compiler_params=_params("arbitrary"),
    )(a, p, p, pool_w, pool_scale, *dep_args)


def even_mix_bwd(dy, a, p, pool_w, pool_scale, name, rc=512):
    s = p.shape[0]
    ng = len(POOL_WINDOWS)
    cw = pool_w.shape[1]
    n_chunks = s // rc

    def body(dy_ref, a_ref, u_ref, g_ref, w_ref, sc_ref, da_ref, du_ref, dg_ref, dw_ref, dsc_ref,
             upad, rpad, dpl, dw_acc, dsc_acc):
        j = pl.program_id(0)

        @pl.when(j < ng)
        def _():
            def chunk(ci, carry):
                rows = pl.ds(pl.multiple_of(ci * rc, rc), rc)
                dyv = dy_ref[rows, :].astype(F32)
                sg, dsg = _silu_and_grad(g_ref[rows, :].astype(F32))
                da_ref[rows, :] = (dyv * sg).astype(BF16)
                dg_ref[rows, :] = (dyv * a_ref[rows, :] * dsg).astype(BF16)
                return carry

            lax.fori_loop(0, n_chunks, chunk, 0)

        for gi, win in enumerate(POOL_WINDOWS):
            @pl.when(j == ng + gi)
            def _(win=win):
                upad[0:HALO, :] = jnp.zeros((HALO, cw), F32)
                rpad[s:s + HALO, :] = jnp.zeros((HALO, cw), F32)
                dw_acc[...] = jnp.zeros_like(dw_acc)
                dsc_acc[...] = jnp.zeros_like(dsc_acc)

                def fill(ci, carry):
                    r0 = pl.multiple_of(ci * rc, rc)
                    upad[pl.ds(pl.multiple_of(r0 + HALO, HALO), rc), :] = u_ref[pl.ds(r0, rc), :].astype(F32)
                    return carry

                lax.fori_loop(0, n_chunks, fill, 0)

                def chunk(ci, carry):
                    r0 = pl.multiple_of(ci * rc, rc)
                    rows = pl.ds(r0, rc)
                    pooled, inv = _pool_window(upad[pl.ds(r0, HALO + rc), :], win, r0, rc)
                    pb = pooled.astype(BF16)
                    wv = w_ref[0]
                    t = jnp.dot(pb, wv, preferred_element_type=F32)
                    scv = sc_ref[...]
                    dyv = dy_ref[rows, :].astype(F32)
                    sg, dsg = _silu_and_grad(g_ref[rows, :].astype(F32))
                    dpo = dyv * sg
                    dg_ref[rows, :] = (dyv * t * scv * dsg).astype(BF16)
                    dsc_acc[...] += _rowsum8(dpo * t)
                    dtb = (dpo * scv).astype(BF16)
                    dw_acc[...] += lax.dot_general(pb, dtb, (((0,), (0,)), ((), ())),
                                                   preferred_element_type=F32)
                    dpooled = lax.dot_general(dtb, wv, (((1,), (1,)), ((), ())),
                                              preferred_element_type=F32)
                    dpl[rows, :] = dpooled
                    rpad[rows, :] = dpooled * inv
                    return carry

                lax.fori_loop(0, n_chunks, chunk, 0)

                def chunk2(ci, carry):
                    r0 = pl.multiple_of(ci * rc, rc)
                    rows = pl.ds(r0, rc)
                    xx = rpad[pl.ds(r0, rc + HALO), :]
                    fs = _window_sum(xx, win, False)[0:rc]
                    du_ref[rows, :] = (fs - dpl[rows, :]).astype(BF16)
                    return carry

                lax.fori_loop(0, n_chunks, chunk2, 0)
                dw_ref[0] = dw_acc[...]
                dsc_ref[...] = jnp.sum(dsc_acc[...], axis=0, keepdims=True)

    grp = lambda j: jnp.maximum(j - ng, 0)
    att = lambda j: jnp.minimum(j, ng - 1)
    return pl.pallas_call(
        body, name=name, grid=(2 * ng,),
        in_specs=[pl.BlockSpec((s, cw), lambda j: (0, j)),
                  pl.BlockSpec((s, cw), lambda j: (0, att(j))),
                  pl.BlockSpec((s, cw), lambda j: (0, 3 * ng + grp(j))),
                  pl.BlockSpec((s, cw), lambda j: (0, 4 * ng + j)),
                  pl.BlockSpec((1, cw, cw), lambda j: (grp(j), 0, 0)),
                  pl.BlockSpec((1, cw), lambda j: (0, grp(j)))],
        out_specs=[pl.BlockSpec((s, cw), lambda j: (0, att(j))),
                   pl.BlockSpec((s, cw), lambda j: (0, grp(j))),
                   pl.BlockSpec((s, cw), lambda j: (0, j)),
                   pl.BlockSpec((1, cw, cw), lambda j: (grp(j), 0, 0)),
                   pl.BlockSpec((1, cw), lambda j: (0, grp(j)))],
        out_shape=[jax.ShapeDtypeStruct((s, ng * cw), BF16), jax.ShapeDtypeStruct((s, ng * cw), BF16),
                   jax.ShapeDtypeStruct((s, 2 * ng * cw), BF16),
                   jax.ShapeDtypeStruct((ng, cw, cw), F32), jax.ShapeDtypeStruct((1, ng * cw), F32)],
        scratch_shapes=[pltpu.VMEM((HALO + s, cw), F32), pltpu.VMEM((s + HALO, cw), F32),
                        pltpu.VMEM((s, cw), F32), pltpu.VMEM((cw, cw), F32), pltpu.VMEM((8, cw), F32)],
        compiler_params=_params("arbitrary"),
    )(dy, a, p, p, pool_w, pool_scale)


def _halo_before(tm):
    return lambda i: jnp.maximum(i * (tm // HALO) - 1, 0)


def _halo_after(tm, s):
    return lambda i: jnp.minimum((i + 1) * (tm // HALO), s // HALO - 1)


def odd_mix_fwd(p, sconv_w, dconv_w, dconv_b, cnorm_g, cnorm_b, name, tm=128, dep=None):
    s = p.shape[0]
    cw = sconv_w.shape[1]
    n = s // tm
    lanes = 128
    hb = _halo_before(tm)

    dep_args, dep_specs = _after(dep)

    def body(hc_ref, hch_ref, bc_ref, cc_ref, cch_ref, ga_ref, gah_ref, gb_ref, gbh_ref, g1_ref, g2_ref,
             sw_ref, dw_ref, db_ref, gam_ref, bet_ref, *rest):
        y_ref, dc_ref = rest[-2:]
        first = pl.program_id(0) == 0
        for l in range(cw // lanes):
            cols = slice(l * lanes, (l + 1) * lanes)
            mh = jnp.where(first, 0.0, cch_ref[:, cols].astype(F32) * hch_ref[:, cols].astype(F32))
            mm = cc_ref[:, cols].astype(F32) * hc_ref[:, cols].astype(F32)
            xx = jnp.concatenate([mh, mm], axis=0)
            tap = _Taps(xx, tm, True)
            cv = jnp.zeros((tm, lanes), F32)
            for k in range(SCONV_K):
                cv = cv + sw_ref[k:k + 1, cols] * tap(SCONV_K - 1 - k)
            c_out = bc_ref[:, cols].astype(F32) * cv
            y_ref[:, cols] = (c_out * _silu(g1_ref[:, cols].astype(F32))).astype(BF16)
            dh = jnp.where(first, 0.0, gah_ref[:, cols].astype(F32) * _sigmoid(gbh_ref[:, cols].astype(F32)))
            dm = ga_ref[:, cols].astype(F32) * _sigmoid(gb_ref[:, cols].astype(F32))
            xx = jnp.concatenate([dh, dm], axis=0)
            tap = _Taps(xx, tm, True)
            acc = jnp.zeros((tm, lanes), F32) + db_ref[:, cols]
            for k in range(CONF_K):
                acc = acc + dw_ref[k:k + 1, cols] * tap(CONF_K - 1 - k)
            dc_ref[:, cols] = acc
        rs = 32
        for r in range(tm // rs):
            rows = slice(r * rs, (r + 1) * rs)
            xv = dc_ref[rows, :]
            mu = jnp.mean(xv, axis=-1, keepdims=True)
            xc = xv - mu
            rstd = lax.rsqrt(jnp.mean(xc * xc, axis=-1, keepdims=True) + EPS)
            ln = xc * rstd * gam_ref[...] + bet_ref[...]
            y_ref[rows, cw:2 * cw] = (_silu(ln) * _silu(g2_ref[rows, :].astype(F32))).astype(BF16)

    main = lambda c: pl.BlockSpec((tm, cw), lambda i: (i, c))
    halo = lambda c: pl.BlockSpec((HALO, cw), lambda i: (hb(i), c))
    vec = lambda r: pl.BlockSpec((r, cw), lambda i: (0, 0))
    return pl.pallas_call(
        body, name=name, grid=(n,),
        in_specs=[main(0), halo(0), main(1), main(2), halo(2), main(3), halo(3), main(4), halo(4),
                  main(5), main(6), vec(SCONV_K), vec(CONF_K), vec(1), vec(1), vec(1)] + dep_specs,
        out_specs=[pl.BlockSpec((tm, 2 * cw), lambda i: (i, 0)), pl.BlockSpec((tm, cw), lambda i: (i, 0))],
        out_shape=[jax.ShapeDtypeStruct((s, 2 * cw), BF16), jax.ShapeDtypeStruct((s, cw), F32)],
        compiler_params=_params("parallel"),
    )(p, p, p, p, p, p, p, p, p, p, p, sconv_w, dconv_w, dconv_b, cnorm_g, cnorm_b, *dep_args)


def odd_bwd_ln(dy, p, dc, cnorm_g, cnorm_b, name, tm=256):
    s = p.shape[0]
    cw = dc.shape[1]
    n = s // tm
    rs = 32

    def body(dy_ref, g2_ref, dc_ref, gam_ref, bet_ref, ddc_ref, dg_ref, dgam_ref, dbet_ref, gacc, bacc):
        i = pl.program_id(0)

        @pl.when(i == 0)
        def _():
            gacc[...] = jnp.zeros_like(gacc)
            bacc[...] = jnp.zeros_like(bacc)

        def chunk(ci, carry):
            rows = pl.ds(pl.multiple_of(ci * rs, rs), rs)
            xv = dc_ref[rows, :]
            mu = jnp.mean(xv, axis=-1, keepdims=True)
            xc = xv - mu
            rstd = lax.rsqrt(jnp.mean(xc * xc, axis=-1, keepdims=True) + EPS)
            xh = xc * rstd
            gam = gam_ref[...]
            sl, dsl = _silu_and_grad(xh * gam + bet_ref[...])
            sg, dsg = _silu_and_grad(g2_ref[rows, :].astype(F32))
            dyv = dy_ref[rows, :].astype(F32)
            dg_ref[rows, :] = (dyv * sl * dsg).astype(BF16)
            dln = dyv * sg * dsl
            gacc[...] += _rowsum8(dln * xh)
            bacc[...] += _rowsum8(dln)
            dxh = dln * gam
            ddc_ref[rows, :] = rstd * (dxh - jnp.mean(dxh, axis=-1, keepdims=True)
                                       - xh * jnp.mean(dxh * xh, axis=-1, keepdims=True))
            return carry

        lax.fori_loop(0, tm // rs, chunk, 0)

        @pl.when(i == n - 1)
        def _():
            dgam_ref[...] = jnp.sum(gacc[...], axis=0, keepdims=True)
            dbet_ref[...] = jnp.sum(bacc[...], axis=0, keepdims=True)

    vec = pl.BlockSpec((1, cw), lambda i: (0, 0))
    return pl.pallas_call(
        body, name=name, grid=(n,),
        in_specs=[pl.BlockSpec((tm, cw), lambda i: (i, 1)), pl.BlockSpec((tm, cw), lambda i: (i, 6)),
                  pl.BlockSpec((tm, cw), lambda i: (i, 0)), vec, vec],
        out_specs=[pl.BlockSpec((tm, cw), lambda i: (i, 0)), pl.BlockSpec((tm, cw), lambda i: (i, 0)), vec, vec],
        out_shape=[jax.ShapeDtypeStruct((s, cw), F32), jax.ShapeDtypeStruct((s, cw), BF16),
                   jax.ShapeDtypeStruct((1, cw), F32), jax.ShapeDtypeStruct((1, cw), F32)],
        scratch_shapes=[pltpu.VMEM((8, cw), F32), pltpu.VMEM((8, cw), F32)],
        compiler_params=_params("arbitrary"),
    )(dy, p, dc, cnorm_g, cnorm_b)


def odd_bwd_conv(dy, p, ddc, dg2, sconv_w, dconv_w, name, tm=128):
    s = p.shape[0]
    cw = ddc.shape[1]
    n = s // tm
    lanes = 128
    hb = _halo_before(tm)
    ha = _halo_after(tm, s)

    def body(dy_ref, dya_ref, g1_ref, g1a_ref, bc_ref, bca_ref, hc_ref, hch_ref, cc_ref, cch_ref,
             ddc_ref, ddca_ref, ga_ref, gah_ref, gb_ref, gbh_ref, dg2_ref, sw_ref, dw_ref,
             dp_ref, dsw_ref, ddw_ref, ddb_ref, sw_acc, dw_acc, db_acc):
        i = pl.program_id(0)
        first = i == 0
        last = i == n - 1

        @pl.when(first)
        def _():
            sw_acc[...] = jnp.zeros_like(sw_acc)
            dw_acc[...] = jnp.zeros_like(dw_acc)
            db_acc[...] = jnp.zeros_like(db_acc)

        for l in range(cw // lanes):
            cols = slice(l * lanes, (l + 1) * lanes)
            mh = jnp.where(first, 0.0, cch_ref[:, cols].astype(F32) * hch_ref[:, cols].astype(F32))
            hcv = hc_ref[:, cols].astype(F32)
            ccv = cc_ref[:, cols].astype(F32)
            xx = jnp.concatenate([mh, ccv * hcv], axis=0)
            tap = _Taps(xx, tm, True)
            taps = [tap(SCONV_K - 1 - k) for k in range(SCONV_K)]
            cv = jnp.zeros((tm, lanes), F32)
            for k in range(SCONV_K):
                cv = cv + sw_ref[k:k + 1, cols] * taps[k]
            bcv = bc_ref[:, cols].astype(F32)
            dyv = dy_ref[:, cols].astype(F32)
            sg, dsg = _silu_and_grad(g1_ref[:, cols].astype(F32))
            dco = dyv * sg
            dp_ref[:, 5 * cw + l * lanes:5 * cw + (l + 1) * lanes] = (dyv * bcv * cv * dsg).astype(BF16)
            dp_ref[:, cw + l * lanes:cw + (l + 1) * lanes] = (dco * cv).astype(BF16)
            dcv = dco * bcv
            for k in range(SCONV_K):
                sw_acc[k * 8:(k + 1) * 8, cols] += _rowsum8(dcv * taps[k])
            dcv_a = jnp.where(last, 0.0, dya_ref[:, cols].astype(F32) * _silu(g1a_ref[:, cols].astype(F32))
                              * bca_ref[:, cols].astype(F32))
            xx = jnp.concatenate([dcv, dcv_a], axis=0)
            tap = _Taps(xx, tm, False)
            dm = jnp.zeros((tm, lanes), F32)
            for k in range(SCONV_K):
                dm = dm + sw_ref[k:k + 1, cols] * tap(SCONV_K - 1 - k)
            dp_ref[:, l * lanes:(l + 1) * lanes] = (dm * ccv).astype(BF16)
            dp_ref[:, 2 * cw + l * lanes:2 * cw + (l + 1) * lanes] = (dm * hcv).astype(BF16)
            gav = ga_ref[:, cols].astype(F32)
            sb = _sigmoid(gb_ref[:, cols].astype(F32))
            dh = jnp.where(first, 0.0, gah_ref[:, cols].astype(F32) * _sigmoid(gbh_ref[:, cols].astype(F32)))
            xx = jnp.concatenate([dh, gav * sb], axis=0)
            ddcv = ddc_ref[:, cols]
            db_acc[:, cols] += _rowsum8(ddcv)
            tap = _Taps(xx, tm, True)
            for k in range(CONF_K):
                dw_acc[k * 8:(k + 1) * 8, cols] += _rowsum8(ddcv * tap(CONF_K - 1 - k))
            ddc_a = jnp.where(last, 0.0, ddca_ref[:, cols])
            xx = jnp.concatenate([ddcv, ddc_a], axis=0)
            tap = _Taps(xx, tm, False)
            dgl = jnp.zeros((tm, lanes), F32)
            for k in range(CONF_K):
                dgl = dgl + dw_ref[k:k + 1, cols] * tap(CONF_K - 1 - k)
            dp_ref[:, 3 * cw + l * lanes:3 * cw + (l + 1) * lanes] = (dgl * sb).astype(BF16)
            dp_ref[:, 4 * cw + l * lanes:4 * cw + (l + 1) * lanes] = (dgl * gav * sb * (1.0 - sb)).astype(BF16)
        dp_ref[:, 6 * cw:7 * cw] = dg2_ref[...]

        @pl.when(last)
        def _():
            for k in range(SCONV_K):
                dsw_ref[k:k + 1, :] = jnp.sum(sw_acc[k * 8:(k + 1) * 8, :], axis=0, keepdims=True)
            for k in range(CONF_K):
                ddw_ref[k:k + 1, :] = jnp.sum(dw_acc[k * 8:(k + 1) * 8, :], axis=0, keepdims=True)
            ddb_ref[...] = jnp.sum(db_acc[...], axis=0, keepdims=True)

    def main(c):
        return pl.BlockSpec((tm, cw), lambda i: (i, c))

    def before(c):
        return pl.BlockSpec((HALO, cw), lambda i: (hb(i), c))

    def after(c):
        return pl.BlockSpec((HALO, cw), lambda i: (ha(i), c))

    def vec(r):
        return pl.BlockSpec((r, cw), lambda i: (0, 0))

    return pl.pallas_call(
        body, name=name, grid=(n,),
        in_specs=[main(0), after(0), main(5), after(5), main(1), after(1), main(0), before(0), main(2), before(2),
                  main(0), after(0), main(3), before(3), main(4), before(4), main(0), vec(SCONV_K), vec(CONF_K)],
        out_specs=[pl.BlockSpec((tm, 7 * cw), lambda i: (i, 0)), vec(SCONV_K), vec(CONF_K), vec(1)],
        out_shape=[jax.ShapeDtypeStruct((s, 7 * cw), BF16), jax.ShapeDtypeStruct((SCONV_K, cw), F32),
                   jax.ShapeDtypeStruct((CONF_K, cw), F32), jax.ShapeDtypeStruct((1, cw), F32)],
        scratch_shapes=[pltpu.VMEM((8 * SCONV_K, cw), F32), pltpu.VMEM((8 * CONF_K, cw), F32),
                        pltpu.VMEM((8, cw), F32)],
        compiler_params=_params("arbitrary"),
    )(dy, dy, p, p, p, p, p, p, p, p, ddc, ddc, p, p, p, p, dg2, sconv_w, dconv_w)


_ANY = pl.BlockSpec(memory_space=pl.ANY)


def _place():
    return lax.axis_index("x"), lax.axis_index("y"), lax.axis_index("c")


def all_gather(arrs, name, deps=()):
    n = len(arrs)

    def body(*refs):
        ins, outs = refs[:n], refs[n + len(deps):2 * n + len(deps)]
        send_sems, recv_sems, local_sems = refs[-3:]
        x, y, c = _place()
        me, sibling = (x, y, c), (x, y, 1 - c)
        chips = [(1 - x, y), (x, 1 - y), (1 - x, 1 - y)]

        def copy(a, k, block, to, src=None):
            px, py, pc = block
            dst = outs[a].at[4 * px + 2 * py + pc]
            return pltpu.make_async_remote_copy(
                src_ref=dst if src is None else src, dst_ref=dst,
                send_sem=send_sems.at[7 * a + k], recv_sem=recv_sems.at[7 * a + k],
                device_id=to, device_id_type=MESH)

        mine = [pltpu.make_async_copy(ins[a], outs[a].at[4 * x + 2 * y + c], local_sems.at[a]) for a in range(n)]
        first = []
        for a in range(n):
            first.append(copy(a, 0, me, sibling, src=ins[a]))
            first += [copy(a, 1 + j, me, (*chip, c), src=ins[a]) for j, chip in enumerate(chips)]
        for cp in first + mine:
            cp.start()
        passed = []
        for a in range(n):
            for j, chip in enumerate(chips):
                copy(a, 1 + j, (*chip, c), me).wait_recv()
                cp = copy(a, 4 + j, (*chip, c), sibling)
                cp.start()
                passed.append(cp)
        for a in range(n):
            copy(a, 0, sibling, me).wait_recv()
            for j, chip in enumerate(chips):
                copy(a, 4 + j, (*chip, 1 - c), me).wait_recv()
        for cp in first + passed:
            cp.wait_send()
        for cp in mine:
            cp.wait()

    return pl.pallas_call(
        body, name=name,
        out_shape=[jax.ShapeDtypeStruct((N_DEV,) + a.shape, a.dtype) for a in arrs],
        in_specs=[_ANY] * (n + len(deps)), out_specs=[_ANY] * n,
        scratch_shapes=[pltpu.SemaphoreType.DMA((7 * n,)), pltpu.SemaphoreType.DMA((7 * n,)),
                        pltpu.SemaphoreType.DMA((n,))],
    )(*arrs, *deps)


def in_proj_gathered(xs, g, w_own, extras, name, tm=512):
    s, d = xs.shape
    n = w_own.shape[1]
    arrs = [w_own] + list(extras)
    na = len(arrs)
    tr = 256

    def body(*refs):
        x_ref, g_ref, ins = refs[0], refs[1], refs[2:2 + na]
        h_out, p_ref, outs = refs[2 + na], refs[3 + na], refs[4 + na:4 + 2 * na]
        (h_ref, xbuf, wbuf, obuf, send_sems, recv_sems, load_sem, store_sems, own_sems, h_sem,
         x_sems) = refs[4 + 2 * na:]
        x, y, c = _place()
        me, sibling = (x, y, c), (x, y, 1 - c)
        x_first = c == 0
        near = (jnp.where(x_first, 1 - x, x), jnp.where(x_first, y, 1 - y))
        far = (jnp.where(x_first, x, 1 - x), jnp.where(x_first, 1 - y, y))
        diag = (1 - x, 1 - y)
        k_near, k_far = jnp.where(x_first, 1, 2), jnp.where(x_first, 2, 1)
        f_near, f_far = k_near + 3, k_far + 3

        def slot(block):
            return 4 * block[0] + 2 * block[1] + block[2]

        def copy(a, k, block, to, src=None):
            dst = outs[a].at[slot(block)]
            return pltpu.make_async_remote_copy(
                src_ref=dst if src is None else src, dst_ref=dst,
                send_sem=send_sems.at[7 * a + k], recv_sem=recv_sems.at[7 * a + k],
                device_id=to, device_id_type=MESH)

        first = []
        for a in range(na):
            first += [copy(a, 0, me, sibling, src=ins[a]), copy(a, 1, me, (1 - x, y, c), src=ins[a]),
                      copy(a, 2, me, (x, 1 - y, c), src=ins[a])]
        for cp in first:
            cp.start()
        own = pltpu.make_async_copy(wbuf.at[0], outs[0].at[slot(me)], own_sems.at[0])
        mine = [pltpu.make_async_copy(ins[a], outs[a].at[slot(me)], own_sems.at[a]) for a in range(1, na)]
        stores = [None, None]

        def x_load(i):
            return pltpu.make_async_copy(x_ref.at[pl.ds(i * tr, tr), :], xbuf.at[i % 2], x_sems.at[i % 2])

        x_load(0).start()
        for i in range(s // tr):
            if i + 1 < s // tr:
                x_load(i + 1).start()
            x_load(i).wait()
            xv = xbuf[i % 2]
            r = lax.rsqrt(jnp.mean(xv * xv, axis=-1, keepdims=True) + EPS)
            h_ref[i * tr:(i + 1) * tr, :] = (xv * r * g_ref[...]).astype(BF16)
        h_store = pltpu.make_async_copy(h_ref, h_out, h_sem)
        h_store.start()

        def multiply(k, block, w_from):
            b = k % 2
            if k == 2:
                own.wait()
            load = pltpu.make_async_copy(w_from, wbuf.at[b], load_sem)
            load.start()
            if stores[b] is not None:
                stores[b].wait()
            load.wait()
            if k == 0:
                own.start()

            def chunk(i, carry):
                rows = pl.ds(pl.multiple_of(i * tm, tm), tm)
                obuf[b, rows, :] = jnp.dot(h_ref[rows, :], wbuf[b], preferred_element_type=F32).astype(BF16)
                return carry

            lax.fori_loop(0, s // tm, chunk, 0)
            stores[b] = pltpu.make_async_copy(
                obuf.at[b], p_ref.at[:, pl.ds(pl.multiple_of(slot(block) * n, 128), n)], store_sems.at[b])
            stores[b].start()

        passed = []

        def arrive(a, k, block):
            copy(a, k, block, me).wait_recv()

        def pass_on(a, k, block, to):
            cp = copy(a, k, block, to)
            cp.start()
            passed.append(cp)

        def gather(arrays, use):
            def arrive_all(k, block):
                for a in arrays:
                    arrive(a, k, block)

            def pass_all(k, block, to):
                for a in arrays:
                    pass_on(a, k, block, to)

            use(0, me)
            arrive_all(0, sibling)
            use(1, sibling)
            arrive_all(k_near, (*near, c))
            pass_all(3, (*near, c), (*far, c))
            pass_all(f_near, (*near, c), sibling)
            use(2, (*near, c))
            arrive_all(f_far, (*far, 1 - c))
            use(3, (*far, 1 - c))
            arrive_all(k_far, (*far, c))
            pass_all(f_far, (*far, c), sibling)
            use(4, (*far, c))
            arrive_all(f_near, (*near, 1 - c))
            use(5, (*near, 1 - c))
            arrive_all(3, (*diag, c))
            pass_all(6, (*diag, c), sibling)
            use(6, (*diag, c))
            arrive_all(6, (*diag, 1 - c))
            use(7, (*diag, 1 - c))

        gather(range(na), lambda k, block: multiply(k, block, ins[0] if k == 0 else outs[0].at[slot(block)]))
        for cp in mine:
            cp.start()
        for cp in first + passed:
            cp.wait_send()
        for cp in mine + stores + [h_store]:
            cp.wait()

    vmem = pl.BlockSpec(memory_space=pltpu.VMEM)
    outs = pl.pallas_call(
        body, name=name,
        out_shape=[jax.ShapeDtypeStruct((s, d), BF16), jax.ShapeDtypeStruct((s, N_DEV * n), BF16)]
        + [jax.ShapeDtypeStruct((N_DEV,) + a.shape, a.dtype) for a in arrs],
        in_specs=[_ANY, vmem] + [_ANY] * na, out_specs=[_ANY] * (2 + na),
        scratch_shapes=[pltpu.VMEM((s, d), BF16), pltpu.VMEM((2, tr, d), F32), pltpu.VMEM((2, d, n), BF16),
                        pltpu.VMEM((2, s, n), BF16),
                        pltpu.SemaphoreType.DMA((7 * na,)), pltpu.SemaphoreType.DMA((7 * na,)),
                        pltpu.SemaphoreType.DMA, pltpu.SemaphoreType.DMA((2,)), pltpu.SemaphoreType.DMA((na,)),
                        pltpu.SemaphoreType.DMA, pltpu.SemaphoreType.DMA((2,))],
        compiler_params=pltpu.CompilerParams(vmem_limit_bytes=VMEM_LIMIT),
    )(xs, g, *arrs)
    return outs[0], outs[1], outs[2], outs[3:]


_HBM = pl.BlockSpec(memory_space=pltpu.HBM)
_SEM = pl.BlockSpec(memory_space=pltpu.SEMAPHORE)
_DATAFLOW = pltpu.SideEffectType.DATAFLOW_SIDE_EFFECTING


def _peers_per_array(kind):
    return 1 if kind in ("sibling", "halves") else 3


def _split_copies(kind, srcs, lands, send_sems, recv_sems):
    x, y, c = _place()
    per = _peers_per_array(kind)
    out = []
    for a in range(len(lands)):
        if kind == "sibling":
            part = srcs[a] if srcs[a].shape[1] == 1 else srcs[a].at[:, pl.ds(1 - c, 1)]
            peers = [((x, y, 1 - c), part, lands[a], lands[a])]
        elif kind == "halves":
            mine, its = lands[a].at[:, pl.ds(c, 1)], lands[a].at[:, pl.ds(1 - c, 1)]
            peers = [((x, y, 1 - c), mine, mine, its)]
        else:
            peers = []
            for px, py in [(1 - x, y), (x, 1 - y), (1 - x, 1 - y)]:
                if kind == "gather":
                    views = (srcs[a], lands[a].at[4 * x + 2 * y + c], lands[a].at[4 * px + 2 * py + c])
                else:
                    views = (srcs[a].at[2 * px + py], lands[a].at[2 * x + y], lands[a].at[2 * px + py])
                peers.append(((px, py, c),) + views)
        for j, (peer, src, dst, arrives) in enumerate(peers):
            sems = dict(send_sem=send_sems.at[per * a + j], recv_sem=recv_sems.at[per * a + j],
                        device_id=peer, device_id_type=MESH)
            out.append((pltpu.make_async_remote_copy(src_ref=src, dst_ref=dst, **sems),
                        pltpu.make_async_remote_copy(src_ref=src, dst_ref=arrives, **sems)))
    return out


def split_start(kind, srcs, lands, deps, name):
    ns, nl = len(srcs), len(lands)
    n_sems = _peers_per_array(kind) * nl
    held = list(srcs) + list(lands)

    def body(*refs):
        send_sems, recv_sems = refs[len(held) + len(deps)], refs[len(held) + len(deps) + 1]
        for copy, _ in _split_copies(kind, refs[:ns], refs[ns:ns + nl], send_sems, recv_sems):
            copy.start()
        token = refs[-1]
        token[...] = jnp.zeros_like(token)

    outs = pl.pallas_call(
        body, name=name,
        out_shape=(pltpu.SemaphoreType.DMA((n_sems,)), pltpu.SemaphoreType.DMA((n_sems,)),
                   *[pltpu.HBM(a.shape, a.dtype) for a in held], jax.ShapeDtypeStruct((8, 128), F32)),
        in_specs=[_HBM] * len(held) + [_ANY] * len(deps),
        out_specs=(_SEM, _SEM, *([_HBM] * len(held)), pl.BlockSpec(memory_space=pltpu.VMEM)),
        input_output_aliases={i: 2 + i for i in range(len(held))},
        compiler_params=pltpu.CompilerParams(has_side_effects=_DATAFLOW),
    )(*[pltpu.with_memory_space_constraint(a, pltpu.HBM) for a in held], *deps)
    return outs[0], outs[1], list(outs[2:2 + ns]), list(outs[2 + ns:2 + ns + nl]), outs[-1]


def split_wait(kind, send_sems, recv_sems, srcs, lands, afters, name):
    ns, nl = len(srcs), len(lands)
    held = list(srcs) + list(lands)

    def body(*refs):
        for _, arrival in _split_copies(kind, refs[:ns], refs[ns:ns + nl], refs[ns + nl], refs[ns + nl + 1]):
            arrival.wait_send()
            arrival.wait_recv()

    outs = pl.pallas_call(
        body, name=name,
        out_shape=[pltpu.HBM(a.shape, a.dtype) for a in held],
        in_specs=[_HBM] * len(held) + [_SEM, _SEM] + [_ANY] * len(afters),
        out_specs=[_HBM] * len(held),
        input_output_aliases={i: i for i in range(len(held))},
        compiler_params=pltpu.CompilerParams(has_side_effects=_DATAFLOW),
    )(*held, send_sems, recv_sems, *afters)
    return list(outs[:ns]), list(outs[ns:])


def place_block(land, block, dev, name):
    r, c = block.shape
    tr = min(r, 512)

    def body(dev_ref, land_ref, b_ref, o_ref):
        del dev_ref, land_ref
        o_ref[...] = b_ref[...]

    return pl.pallas_call(
        body, name=name,
        grid_spec=pltpu.PrefetchScalarGridSpec(
            num_scalar_prefetch=1, grid=(r // tr,),
            in_specs=[_ANY, pl.BlockSpec((tr, c), lambda i, dev_ref: (i, 0))],
            out_specs=pl.BlockSpec((None, tr, c), lambda i, dev_ref: (dev_ref[0], i, 0))),
        out_shape=jax.ShapeDtypeStruct(land.shape, land.dtype),
        input_output_aliases={1: 0},
        compiler_params=_params("parallel"),
    )(dev, land, block)


def pair_add(own, recv, core, name):
    _, _, r, c = own.shape
    tr = min(r, 512)

    def body(core_ref, own_ref, recv_ref, o_ref):
        del core_ref
        o_ref[...] = (own_ref[...].astype(F32) + recv_ref[...].astype(F32)).astype(BF16)

    return pl.pallas_call(
        body, name=name,
        grid_spec=pltpu.PrefetchScalarGridSpec(
            num_scalar_prefetch=1, grid=(4, r // tr),
            in_specs=[pl.BlockSpec((None, None, tr, c), lambda k, i, core_ref: (k, core_ref[0], i, 0)),
                      pl.BlockSpec((None, None, tr, c), lambda k, i, core_ref: (k, 0, i, 0))],
            out_specs=pl.BlockSpec((None, tr, c), lambda k, i, core_ref: (k, i, 0))),
        out_shape=jax.ShapeDtypeStruct((4, r, c), BF16),
        compiler_params=_params("parallel", "parallel"),
    )(core, own, recv)


def _adamw_math(w, g, m, v):
    m2 = ADAM_B1 * m + (1.0 - ADAM_B1) * g
    v2 = ADAM_B2 * v + (1.0 - ADAM_B2) * (g * g)
    m_hat = m2 / (1.0 - ADAM_B1 ** ADAM_STEP)
    v_hat = v2 / (1.0 - ADAM_B2 ** ADAM_STEP)
    delta = -ADAM_LR * (m_hat / (jnp.sqrt(v_hat) + ADAM_EPS) + ADAM_WD * w)
    return delta, m2, v2


def adamw_big(w, m, v, own, got, chip, name):
    r, c = w.shape
    tr = min(r, 256)

    def body(chip_ref, w_ref, m_ref, v_ref, p0, p1, p2, p3, g_ref, d_ref, m2_ref, v2_ref):
        del chip_ref
        g = ((p0[...].astype(F32) + p1[...].astype(F32)) + p2[...].astype(F32)) + p3[...].astype(F32)
        delta, m2, v2 = _adamw_math(w_ref[...], g, m_ref[...], v_ref[...])
        g_ref[...] = g
        d_ref[...] = delta
        m2_ref[...] = m2
        v2_ref[...] = v2

    row = pl.BlockSpec((tr, c), lambda i, chip_ref: (i, 0))

    def slab(flip):
        return pl.BlockSpec((None, tr, c), lambda i, chip_ref: (chip_ref[0] ^ flip, i, 0))

    return pl.pallas_call(
        body, name=name,
        grid_spec=pltpu.PrefetchScalarGridSpec(
            num_scalar_prefetch=1, grid=(r // tr,),
            in_specs=[row, row, row, slab(0), slab(1), slab(2), slab(3)],
            out_specs=[row] * 4),
        out_shape=[jax.ShapeDtypeStruct((r, c), F32)] * 4,
        compiler_params=_params("parallel"),
    )(chip, w, m, v, own, got, got, got)


def sum_devices(g8, name):
    def body(g_ref, o_ref):
        tot = g_ref[0]
        for k in range(1, N_DEV):
            tot = tot + g_ref[k]
        o_ref[...] = tot

    return pl.pallas_call(body, name=name, out_shape=jax.ShapeDtypeStruct(g8.shape[1:], F32))(g8)


def adamw_small(ws, gs, ms, vs, name):
    n = len(ws)

    def body(*refs):
        w_r, g_r, m_r, v_r = refs[:n], refs[n:2 * n], refs[2 * n:3 * n], refs[3 * n:4 * n]
        d_o, m_o, v_o = refs[4 * n:5 * n], refs[5 * n:6 * n], refs[6 * n:7 * n]
        for k in range(n):
            delta, m2, v2 = _adamw_math(w_r[k][...], g_r[k][...], m_r[k][...], v_r[k][...])
            d_o[k][...] = delta
            m_o[k][...] = m2
            v_o[k][...] = v2

    shapes = [jax.ShapeDtypeStruct(w.shape, F32) for w in ws]
    outs = pl.pallas_call(body, name=name, out_shape=shapes * 3)(*ws, *gs, *ms, *vs)
    return outs[:n], outs[n:2 * n], outs[2 * n:]


def _rows128(a):
    return a.reshape(-1, 128)


def _pad_rows(a, rows):
    return jnp.pad(a, ((0, rows - a.shape[0]), (0, 0)))


def kernel(x, ln_pre_even, w_in_even, pool_w, pool_scale, w_out_even, ln_post_even, ln_pre_odd, w_in_odd, sconv_w, dconv_w, dconv_b, cnorm_g, cnorm_b, w_out_odd, ln_post_odd, loss_target, m_ln_pre_even, m_w_in_even, m_pool_w, m_pool_scale, m_w_out_even, m_ln_post_even, m_ln_pre_odd, m_w_in_odd, m_sconv_w, m_dconv_w, m_dconv_b, m_cnorm_g, m_cnorm_b, m_w_out_odd, m_ln_post_odd, v_ln_pre_even, v_w_in_even, v_pool_w, v_pool_scale, v_w_out_even, v_ln_post_even, v_ln_pre_odd, v_w_in_odd, v_sconv_w, v_dconv_w, v_dconv_b, v_cnorm_g, v_cnorm_b, v_w_out_odd, v_ln_post_odd):
    xs = x[0]
    tgt = loss_target[0]
    s, d = xs.shape
    half = d // 2
    n_heads = half // HEAD_DIM
    ng = len(POOL_WINDOWS)
    cwp = half // ng
    dev = 4 * lax.axis_index("x") + 2 * lax.axis_index("y") + lax.axis_index("c")
    core = lax.axis_index("c").astype(jnp.int32).reshape(1)

    pr = pool_w.shape[2]
    cl = sconv_w.shape[2]
    small_parts = [(_rows128(ln_pre_odd), 8), (sconv_w[0], 8), (dconv_w[0], 32), (dconv_b, 8),
                   (cnorm_g, 8), (cnorm_b, 8), (_rows128(ln_post_odd), 8)]
    small_local = jnp.concatenate([_pad_rows(a, r) for a, r in small_parts], axis=0)
    h0, p0, g_wie, (g_pw, g_small) = in_proj_gathered(
        xs, ln_pre_even, w_in_even[0].astype(BF16), [pool_w[0].reshape(ng * pr, cwp).astype(BF16), small_local],
        "ag_in_proj_even")
    comm = _Exchanges(dev, core, d)
    token = comm.start_weights("out_even", [w_out_even[0].astype(BF16)], [p0])
    token = comm.start_weights("in_odd", [w_in_odd[0].astype(BF16)], [token])
    sb_dep = comm.start_weights("out_odd", [w_out_odd[0].astype(BF16)], [token])
    pool_full = g_pw.reshape(N_DEV, ng, pr, cwp).transpose(1, 0, 2, 3).reshape(ng, cwp, cwp)
    nl = ln_pre_odd.shape[1] // 128

    def chan(lo, rows):
        return g_small[:, lo:lo + rows].transpose(1, 0, 2).reshape(rows, N_DEV * cl)

    ln_pre_odd_f = g_small[:, 0:nl].reshape(1, d)
    sconv_f = chan(8, SCONV_K)
    dconv_f = chan(16, CONF_K)
    dconv_b_f = chan(48, 1)
    cnorm_g_f = chan(56, 1)
    cnorm_b_f = chan(64, 1)
    ln_post_odd_f = g_small[:, 72:72 + nl].reshape(1, d)

    loss_blk, grad_x, small_g = _fwd_bwd(
        xs, tgt, ln_pre_even, h0, p0, g_wie, pool_full, pool_scale, ln_post_even, ln_pre_odd_f,
        sconv_f, dconv_f, dconv_b_f, cnorm_g_f, cnorm_b_f, ln_post_odd_f, comm, sb_dep)
    small_w = [ln_pre_even, pool_scale, ln_post_even, ln_pre_odd, sconv_w[0], dconv_w[0], dconv_b, cnorm_g, cnorm_b, ln_post_odd]
    small_m = [m_ln_pre_even, m_pool_scale, m_ln_post_even, m_ln_pre_odd, m_sconv_w[0], m_dconv_w[0], m_dconv_b, m_cnorm_g, m_cnorm_b, m_ln_post_odd]
    small_v = [v_ln_pre_even, v_pool_scale, v_ln_post_even, v_ln_pre_odd, v_sconv_w[0], v_dconv_w[0], v_dconv_b, v_cnorm_g, v_cnorm_b, v_ln_post_odd]
    big = {"w_in_even": (w_in_even, m_w_in_even, v_w_in_even), "pool_w": (pool_w, m_pool_w, v_pool_w),
           "w_out_even": (w_out_even, m_w_out_even, v_w_out_even), "w_in_odd": (w_in_odd, m_w_in_odd, v_w_in_odd),
           "w_out_odd": (w_out_odd, m_w_out_odd, v_w_out_odd)}
    upd = comm.finish_updates(big, [grad_x])
    upd.update(comm.finish_updates(big, [grad_x]))
    sg, sd, sm, sv, loss = _update_small(small_g, loss_blk, small_w, small_m, small_v, dev, d, cl,
                                         deps=[upd["w_in_odd"][1], upd["w_out_even"][1]])
    upd.update(comm.finish_updates(big, sd))
    (g_wie_o, d_wie, m_wie, v_wie), (g_pw_o, d_pw, m_pw, v_pw) = upd["w_in_even"], upd["pool_w"]
    (g_woe_o, d_woe, m_woe, v_woe), (g_wio_o, d_wio, m_wio, v_wio) = upd["w_out_even"], upd["w_in_odd"]
    g_woo_o, d_woo, m_woo, v_woo = upd["w_out_odd"]

    def order(small, wie, pw, woe, wio, woo):
        return [small[0], wie, pw, small[1], woe, small[2], small[3], wio, small[4], small[5], small[6],
                small[7], small[8], woo, small[9]]

    grads = order(sg, g_wie_o, g_pw_o, g_woe_o, g_wio_o, g_woo_o)
    deltas = order(sd, d_wie, d_pw, d_woe, d_wio, d_woo)
    new_m = order(sm, m_wie, m_pw, m_woe, m_wio, m_woo)
    new_v = order(sv, v_wie, v_pw, v_woe, v_wio, v_woo)
    return (loss, grad_x[None], *grads, *deltas, *new_m, *new_v)


def _fwd_bwd(xs, tgt, ln_pre_even, h0, p0, g_wie, pool_full, pool_scale, ln_post_even, ln_pre_odd_f,
             sconv_f, dconv_f, dconv_b_f, cnorm_g_f, cnorm_b_f, ln_post_odd_f, comm, sb_dep):
    d = xs.shape[1]
    n_heads = d // 2 // HEAD_DIM
    ng, cwp = pool_full.shape[0], pool_full.shape[1]
    a0, sb_wts = sb_fwd(p0, n_heads, "sb_fwd", dep=sb_dep)
    dep = comm.weights_arrived("out_even", after=a0)
    y0 = even_mix_fwd(a0, p0, pool_full, pool_scale, "even_mix_fwd", dep=dep)
    (w_out_e,) = comm.weights("out_even", after=y0)
    w_out_e = w_out_e.reshape(1, d, d)
    o0 = mm_nn(y0, w_out_e, BF16, "out_proj_even", tn=512)
    dep = comm.weights_arrived("in_odd", after=o0)
    x1, h1 = postnorm_fwd(xs, o0, ln_post_even, ln_pre_odd_f, "post_even", dep=dep)
    (g_wio,) = comm.weights("in_odd", after=x1)
    p1 = mm_nn(h1, g_wio, BF16, "in_proj_odd", group=2)
    dep = comm.weights_arrived("out_odd", after=p1)
    y1, dc = odd_mix_fwd(p1, sconv_f, dconv_f, dconv_b_f, cnorm_g_f, cnorm_b_f, "odd_mix_fwd", dep=dep)
    (w_out_o,) = comm.weights("out_odd", after=y1)
    w_out_o = w_out_o.reshape(1, d, d)
    o1 = mm_nn(y1, w_out_o, BF16, "out_proj_odd", tn=512)
    loss_blk, gx2, do1, dg_post_odd = final_fwd_bwd(x1, o1, ln_post_odd_f, tgt, "post_odd_loss")

    dw_out_o = mm_tn(y1, do1, 1, BF16, "dw_out_odd")
    dy1 = mm_nt(do1, w_out_o, BF16, "dy_odd")
    ddc, dg2, dgam, dbet = odd_bwd_ln(dy1, p1, dc, cnorm_g_f, cnorm_b_f, "odd_bwd_ln")
    dp1, dsconv, ddconv, ddconv_b = odd_bwd_conv(dy1, p1, ddc, dg2, sconv_f, dconv_f, "odd_bwd_conv")
    dw_in_o = mm_tn(h1, dp1, N_DEV, BF16, "dw_in_odd", group=2)
    dep = comm.reduce_begin({"w_out_odd": dw_out_o.reshape(N_DEV, d // N_DEV, d), "w_in_odd": dw_in_o}, "odd")
    dh1 = mm_nt(dp1, g_wio, BF16, "dh_odd", dep=dep, group=2)
    dep = comm.reduce_send(after=dh1)
    gx1, dg_pre_odd, do0, dg_post_even = norm_bwd(dh1, x1, ln_pre_odd_f, gx2, "pre_odd_post_even_bwd",
                                                  inp2=o0, g2=ln_post_even, dep=dep)

    dw_out_e = mm_tn(y0, do0, 1, BF16, "dw_out_even")
    dy0 = mm_nt(do0, w_out_e, BF16, "dy_even")
    da0, du0, dg0, dpool, dpool_scale = even_mix_bwd(dy0, a0, p0, pool_full, pool_scale, "even_mix_bwd")
    pr = cwp // N_DEV
    dpool_slabs = dpool.astype(BF16).reshape(ng, N_DEV, pr, cwp).transpose(1, 0, 2, 3).reshape(N_DEV, ng * pr, cwp)
    dep = comm.reduce_begin({"w_out_even": dw_out_e.reshape(N_DEV, d // N_DEV, d), "pool_w": dpool_slabs}, "even_out")
    dq0, dk0, dv0 = sb_bwd(p0, a0, sb_wts, da0, n_heads, "sb_bwd", dep=dep)
    dep = comm.reduce_send(after=dq0)
    dp0 = jnp.concatenate([dq0, dk0, dv0, du0, dg0], axis=1)
    dw_sibling = mm_tn(h0, dp0, N_DEV // 2, BF16, "dw_in_even_sibling", dep=dep, pick=(2, 1 - comm.core))
    dep = comm.reduce_begin({"w_in_even": dw_sibling}, "even_in", sibling_part=True)
    dw_own = mm_tn(h0, dp0, N_DEV // 2, BF16, "dw_in_even_own", dep=dep, pick=(2, comm.core))
    dep = comm.reduce_send(after=dw_own, own_part={"w_in_even": dw_own})
    dh0 = mm_nt(dp0, g_wie, BF16, "dh_even", dep=dep, group=2)
    dep = None
    grad_x, dg_pre_even = norm_bwd(dh0, xs, ln_pre_even, gx1, "pre_even_bwd", tm=512, dep=dep)
    small_g = [dg_pre_even, dpool_scale, dg_post_even, dg_pre_odd, dsconv, ddconv, ddconv_b, dgam, dbet, dg_post_odd]
    return loss_blk, grad_x, small_g


class _Exchanges:
    def __init__(self, dev, core, d):
        self.dev = dev.astype(jnp.int32).reshape(1)
        self.core = core
        self.chip = (dev // 2).astype(jnp.int32).reshape(1)
        self.d = d
        self.in_flight = {}
        self.to_sibling = None
        self.pending = []

    def start_weights(self, tag, blocks, afters):
        lands = [lax.empty((N_DEV,) + b.shape, b.dtype) for b in blocks]
        send, recv, srcs, lands, token = split_start("gather", blocks, lands, afters, "ag_start_" + tag)
        self.in_flight[tag] = (send, recv, srcs, lands)
        return token

    def weights_arrived(self, tag, after):
        send, recv, srcs, lands = self.in_flight.pop(tag)
        srcs, lands = split_wait("gather", send, recv, srcs, lands, [after], "ag_wait_" + tag)
        lands = [place_block(l, b, self.dev, "ag_own_%s_%d" % (tag, k)) for k, (l, b) in enumerate(zip(lands, srcs))]
        lands = [l.reshape((4, 2) + l.shape[1:]) for l in lands]
        send, recv, _, lands, token = split_start("halves", [], lands, [], "ag_sibling_start_" + tag)
        self.in_flight[tag] = (send, recv, lands)
        return token

    def weights(self, tag, after):
        send, recv, lands = self.in_flight.pop(tag)
        _, lands = split_wait("halves", send, recv, [], lands, [after], "ag_sibling_wait_" + tag)
        return [l.reshape((N_DEV,) + l.shape[2:]) for l in lands]

    def reduce_begin(self, partials, tag, sibling_part=False):
        names = list(partials)
        arrs = [partials[k].reshape((4, 1 if sibling_part else 2) + partials[k].shape[1:]) for k in names]
        lands = [lax.empty((4, 1) + a.shape[2:], a.dtype) for a in arrs]
        send, recv, srcs, lands, token = split_start("sibling", arrs, lands, [], "rs_sibling_start_" + tag)
        self.to_sibling = (tag, names, send, recv, srcs, lands)
        return token

    def reduce_send(self, after, own_part=None):
        tag, names, send, recv, srcs, lands = self.to_sibling
        srcs, lands = split_wait("sibling", send, recv, srcs, lands, [after], "rs_sibling_wait_" + tag)
        which = self.core
        if own_part is not None:
            srcs = [own_part[k].reshape((4, 1) + own_part[k].shape[1:]) for k in names]
            which = jnp.zeros((1,), jnp.int32)
        sums = [pair_add(o, r, which, "rs_pair_add_" + k) for k, o, r in zip(names, srcs, lands)]
        zones = [lax.empty(a.shape, a.dtype) for a in sums]
        send, recv, srcs, zones, token = split_start("scatter", sums, zones, [], "rs_start_" + tag)
        self.pending.append((tag, names, send, recv, srcs, zones))
        return token

    def finish_updates(self, big, afters):
        tag, names, send, recv, srcs, lands = self.pending.pop(0)
        srcs, lands = split_wait("scatter", send, recv, srcs, lands, afters, "rs_wait_" + tag)
        out = {}
        for name, own, got in zip(names, srcs, lands):
            w, m, v = big[name]
            shp = own.shape[1:]
            outs = adamw_big(w.reshape(shp), m.reshape(shp), v.reshape(shp), own, got, self.chip, "adamw_" + name)
            out[name] = [o.reshape(w.shape) for o in outs]
        return out


def _update_small(small_g, loss_blk, small_w, small_m, small_v, dev, d, cl, deps):
    packed = jnp.concatenate([_rows128(g) for g in small_g] + [loss_blk], axis=0)
    (g8,) = all_gather([packed], "ag_small_grads", deps)
    tot = sum_devices(g8, "sum_small_grads")
    loss = tot[packed.shape[0] - 8, 0]
    full_g = []
    lo = 0
    for g in small_g:
        rows = g.size // 128
        full_g.append(tot[lo:lo + rows].reshape(g.shape))
        lo += rows

    def mine(g, width):
        return lax.dynamic_slice_in_dim(g, dev * width, width, axis=g.ndim - 1)

    fg = full_g
    small_gl = [fg[0], fg[1], fg[2], mine(fg[3], d // N_DEV), mine(fg[4], cl), mine(fg[5], cl), mine(fg[6], cl),
                mine(fg[7], cl), mine(fg[8], cl), mine(fg[9], d // N_DEV)]
    sd, sm, sv = adamw_small(small_w, small_gl, small_m, small_v, "adamw_small")

    def like(k, a):
        return a[None] if k in (4, 5) else a

    sg = [like(k, a) for k, a in enumerate(small_gl)]
    sd = [like(k, a) for k, a in enumerate(sd)]
    sm = [like(k, a) for k, a in enumerate(sm)]
    sv = [like(k, a) for k, a in enumerate(sv)]
    return sg, sd, sm, sv, loss
```

```python
import functools
import math

import jax
import jax.numpy as jnp
from jax import lax
from jax.experimental import pallas as pl
from jax.experimental.pallas import tpu as pltpu

F32 = jnp.float32
BF16 = jnp.bfloat16
EPS = 1e-6
HEAD_DIM = 128
POOL_WINDOWS = (2, 4, 8, 16)
SCONV_K = 3
CONF_K = 31
HALO = 32
N_DEV = 8
VMEM_LIMIT = 56 * 1024 * 1024
MESH = pl.DeviceIdType.MESH

ADAM_LR = 0.001
ADAM_B1 = 0.9
ADAM_B2 = 0.999
ADAM_EPS = 1e-08
ADAM_WD = 0.01
ADAM_STEP = 10


def _params(*sem):
    return pltpu.CompilerParams(dimension_semantics=sem, vmem_limit_bytes=VMEM_LIMIT)


def _sigmoid(v):
    return 1.0 / (1.0 + jnp.exp(-v))


def _silu(v):
    return v * _sigmoid(v)


def _silu_and_grad(v):
    s = _sigmoid(v)
    return v * s, s * (1.0 + v * (1.0 - s))


def _rowsum8(v):
    r, c = v.shape
    return jnp.sum(v.reshape(r // 8, 8, c), axis=0)


SUBLANES = 8


class _Taps:
    def __init__(self, xx, rows, before):
        self.xx, self.rows, self.before, self.rotated = xx, rows, before, {}

    def __call__(self, i):
        r, q = i % SUBLANES, i // SUBLANES
        if r not in self.rotated:
            n = self.xx.shape[0]
            self.rotated[r] = self.xx if r == 0 else pltpu.roll(self.xx, r if self.before else n - r, 0)
        lo = HALO - SUBLANES * q if self.before else SUBLANES * q
        return self.rotated[r][lo:lo + self.rows]


def _window_sum(xx, win, before):
    n = xx.shape[0]
    acc = xx
    k = 1
    while k < win:
        acc = acc + pltpu.roll(acc, k if before else n - k, 0)
        k *= 2
    return acc


def postnorm_fwd(x, o, g, g_next, name, tm=512, dep=None):
    s, d = x.shape
    dep_args, dep_specs = _after(dep)

    def body(x_ref, o_ref, g_ref, gn_ref, *rest):
        y_ref, h_ref = rest[-2:]
        ov = o_ref[...].astype(F32)
        r = lax.rsqrt(jnp.mean(ov * ov, axis=-1, keepdims=True) + EPS)
        y = x_ref[...] + ov * r * g_ref[...]
        y_ref[...] = y
        r2 = lax.rsqrt(jnp.mean(y * y, axis=-1, keepdims=True) + EPS)
        h_ref[...] = (y * r2 * gn_ref[...]).astype(BF16)

    row = pl.BlockSpec((tm, d), lambda i: (i, 0))
    vec = pl.BlockSpec((1, d), lambda i: (0, 0))
    return pl.pallas_call(
        body, name=name, grid=(s // tm,),
        in_specs=[row, row, vec, vec] + dep_specs, out_specs=[row, row],
        out_shape=[jax.ShapeDtypeStruct((s, d), F32), jax.ShapeDtypeStruct((s, d), BF16)],
        compiler_params=_params("parallel"),
    )(x, o, g, g_next, *dep_args)


def final_fwd_bwd(x1, o, g, target, name, tm=512):
    s, d = x1.shape
    n = s // tm

    def body(x_ref, o_ref, g_ref, t_ref, loss_ref, gx_ref, do_ref, dg_ref, lacc, gacc):
        i = pl.program_id(0)

        @pl.when(i == 0)
        def _():
            lacc[...] = jnp.zeros_like(lacc)
            gacc[...] = jnp.zeros_like(gacc)

        ov = o_ref[...].astype(F32)
        gv = g_ref[...]
        r = lax.rsqrt(jnp.mean(ov * ov, axis=-1, keepdims=True) + EPS)
        oh = ov * r
        diff = x_ref[...] + oh * gv - t_ref[...]
        lacc[...] += _rowsum8(diff * diff)
        gx = diff * (1.0 / d)
        gx_ref[...] = gx
        gacc[...] += _rowsum8(gx * oh)
        dn = gx * gv
        do_ref[...] = (r * (dn - oh * jnp.mean(dn * oh, axis=-1, keepdims=True))).astype(BF16)

        @pl.when(i == n - 1)
        def _():
            tot = jnp.sum(jnp.sum(lacc[...], axis=0, keepdims=True), axis=1, keepdims=True)
            loss_ref[...] = jnp.broadcast_to(tot * (0.5 / d), loss_ref.shape)
            dg_ref[...] = jnp.sum(gacc[...], axis=0, keepdims=True)

    row = pl.BlockSpec((tm, d), lambda i: (i, 0))
    vec = pl.BlockSpec((1, d), lambda i: (0, 0))
    return pl.pallas_call(
        body, name=name, grid=(n,),
        in_specs=[row, row, vec, row],
        out_specs=[pl.BlockSpec((8, 128), lambda i: (0, 0)), row, row, vec],
        out_shape=[jax.ShapeDtypeStruct((8, 128), F32), jax.ShapeDtypeStruct((s, d), F32),
                   jax.ShapeDtypeStruct((s, d), BF16), jax.ShapeDtypeStruct((1, d), F32)],
        scratch_shapes=[pltpu.VMEM((8, d), F32), pltpu.VMEM((8, d), F32)],
        compiler_params=_params("arbitrary"),
    )(x1, o, g, target)


def _rms_bwd_rows(dyv, xv, gv):
    r = lax.rsqrt(jnp.mean(xv * xv, axis=-1, keepdims=True) + EPS)
    xh = xv * r
    dn = dyv * gv
    return r * (dn - xh * jnp.mean(dn * xh, axis=-1, keepdims=True)), _rowsum8(dyv * xh)


def norm_bwd(dy, inp, g, resid, name, inp2=None, g2=None, tm=256, dep=None):
    s, d = inp.shape
    n = s // tm
    chain = inp2 is not None

    def body(*refs):
        dy_ref, x_ref, g_ref, r_ref = refs[:4]
        outs = refs[-6:] if chain else refs[-3:]
        i = pl.program_id(0)

        @pl.when(i == 0)
        def _():
            for acc in outs[-2:] if chain else outs[-1:]:
                acc[...] = jnp.zeros_like(acc)

        if chain:
            x2_ref, g2_ref = refs[4:6]
            dx_ref, dg_ref, dx2_ref, dg2_ref, gacc, gacc2 = outs
        else:
            dx_ref, dg_ref, gacc = outs
        dx, dg_rows = _rms_bwd_rows(dy_ref[...].astype(F32), x_ref[...], g_ref[...])
        dx = dx + r_ref[...]
        dx_ref[...] = dx
        gacc[...] += dg_rows
        if chain:
            dx2, dg2_rows = _rms_bwd_rows(dx, x2_ref[...].astype(F32), g2_ref[...])
            dx2_ref[...] = dx2.astype(BF16)
            gacc2[...] += dg2_rows

        @pl.when(i == n - 1)
        def _():
            dg_ref[...] = jnp.sum(gacc[...], axis=0, keepdims=True)
            if chain:
                dg2_ref[...] = jnp.sum(gacc2[...], axis=0, keepdims=True)

    row = pl.BlockSpec((tm, d), lambda i: (i, 0))
    vec = pl.BlockSpec((1, d), lambda i: (0, 0))
    dep_args, dep_specs = _after(dep)
    extra = [inp2, g2] if chain else []
    return pl.pallas_call(
        body, name=name, grid=(n,),
        in_specs=[row, row, vec, row] + ([row, vec] if chain else []) + dep_specs,
        out_specs=[row, vec] * (2 if chain else 1),
        out_shape=[jax.ShapeDtypeStruct((s, d), F32), jax.ShapeDtypeStruct((1, d), F32)]
        + ([jax.ShapeDtypeStruct((s, d), BF16), jax.ShapeDtypeStruct((1, d), F32)] if chain else []),
        scratch_shapes=[pltpu.VMEM((8, d), F32)] * (2 if chain else 1),
        compiler_params=_params("arbitrary"),
    )(dy, inp, g, resid, *extra, *dep_args)


def _after(dep):
    if dep is None:
        return [], []
    return [dep], [pl.BlockSpec((8, 128), lambda *_: (0, 0))]


def _lane_concat(ref, count):
    return ref[0] if count == 1 else jnp.concatenate([ref[i] for i in range(count)], axis=1)


def mm_nn(a, w, out_dtype, name, tm=2048, tn=None, dep=None, group=1):
    m, k = a.shape
    tm = min(tm, m)
    ns, _, n = w.shape
    tn = n if tn is None else tn
    nj = n // tn
    assert group == 1 or nj == 1
    dep_args, dep_specs = _after(dep)

    def body(a_ref, w_ref, *rest):
        o_ref = rest[-1]
        o_ref[...] = jnp.dot(a_ref[...], _lane_concat(w_ref, group), preferred_element_type=F32).astype(out_dtype)

    return pl.pallas_call(
        body, name=name, grid=(ns // group, nj, m // tm),
        in_specs=[pl.BlockSpec((tm, k), lambda s, j, i: (i, 0)),
                  pl.BlockSpec((group, k, tn), lambda s, j, i: (s, 0, j))] + dep_specs,
        out_specs=pl.BlockSpec((tm, group * tn), lambda s, j, i: (i, s * nj + j)),
        out_shape=jax.ShapeDtypeStruct((m, ns * n), out_dtype),
        compiler_params=_params("parallel", "parallel", "parallel"),
    )(a, w, *dep_args)


def mm_nt(a, w, out_dtype, name, tm=1024, tn=None, dep=None, group=1):
    m = a.shape[0]
    tm = min(tm, m)
    ns, k, n = w.shape
    tn = n if tn is None else tn
    nj = n // tn
    assert group == 1 or nj == 1
    steps = ns * nj // group
    dep_args, dep_specs = _after(dep)

    def body(a_ref, w_ref, *rest):
        o_ref, acc = rest[-2:]
        r = pl.program_id(1)

        @pl.when(r == 0)
        def _():
            acc[...] = jnp.zeros_like(acc)

        acc[...] += lax.dot_general(a_ref[...], _lane_concat(w_ref, group), (((1,), (1,)), ((), ())),
                                    preferred_element_type=F32)

        @pl.when(r == steps - 1)
        def _():
            o_ref[...] = acc[...].astype(out_dtype)

    return pl.pallas_call(
        body, name=name, grid=(m // tm, steps),
        in_specs=[pl.BlockSpec((tm, group * tn), lambda i, r: (i, r)),
                  pl.BlockSpec((group, k, tn), lambda i, r: (r // nj, 0, r % nj))] + dep_specs,
        out_specs=pl.BlockSpec((tm, k), lambda i, r: (i, 0)),
        out_shape=jax.ShapeDtypeStruct((m, k), out_dtype),
        scratch_shapes=[pltpu.VMEM((tm, k), F32)],
        compiler_params=_params("parallel", "arbitrary"),
    )(a, w, *dep_args)


def mm_tn(a, b, ns, out_dtype, name, tk=1024, tm=2048, dep=None, pick=None, group=1):
    m, k = a.shape
    tm = min(tm, m)
    step, offset = (1, None) if pick is None else pick
    assert group == 1 or pick is None
    n = b.shape[1] // (ns * step)
    steps = m // tm
    dep_args, dep_specs = _after(dep)
    n_pre = 0 if pick is None else 1

    def b_block(s, j, r, *pre):
        return (r, s if pick is None else step * s + pre[0][0])

    def body(*refs):
        a_ref, b_ref = refs[n_pre:n_pre + 2]
        o_ref, acc = refs[-2:]
        r = pl.program_id(2)

        @pl.when(r == 0)
        def _():
            acc[...] = jnp.zeros_like(acc)

        acc[...] += lax.dot_general(a_ref[...], b_ref[...], (((0,), (0,)), ((), ())),
                                    preferred_element_type=F32)

        @pl.when(r == steps - 1)
        def _():
            for i in range(group):
                o_ref[i] = acc[:, i * n:(i + 1) * n].astype(out_dtype)

    return pl.pallas_call(
        body, name=name,
        grid_spec=pltpu.PrefetchScalarGridSpec(
            num_scalar_prefetch=n_pre, grid=(ns // group, k // tk, steps),
            in_specs=[pl.BlockSpec((tm, tk), lambda s, j, r, *pre: (r, j)),
                      pl.BlockSpec((tm, group * n), b_block)] + dep_specs,
            out_specs=pl.BlockSpec((group, tk, n), lambda s, j, r, *pre: (s, j, 0)),
            scratch_shapes=[pltpu.VMEM((tk, group * n), F32)]),
        out_shape=jax.ShapeDtypeStruct((ns, k, n), out_dtype),
        compiler_params=_params("parallel", "parallel", "arbitrary"),
    )(*([] if pick is None else [offset]), a, b, *dep_args)


SB_BLK = 128


LOG2E = 1.0 / math.log(2.0)


def _split_dot(v, tri2):
    hi = pltpu.bitcast(pltpu.bitcast(v, jnp.uint32) & jnp.uint32(0xFFFF0000), F32)
    lo = (v - hi).astype(BF16)
    return jnp.dot(jnp.concatenate([hi.astype(BF16), lo], axis=1), tri2, preferred_element_type=F32)


def _sb_scores(z2, lim, dcol, tri_ex, masked):
    sp = jnp.log2(1.0 + jnp.exp2(-jnp.abs(z2)))
    lb = jnp.minimum(z2, 0.0) - sp
    l1m = lb - z2
    mask = None
    if masked:
        mask = dcol < lim
        l1m = jnp.where(mask, l1m, 0.0)
    return mask, lb, l1m, _split_dot(l1m, tri_ex)


def _sb_consts():
    row = lax.broadcasted_iota(jnp.int32, (SB_BLK, SB_BLK), 0)
    col = lax.broadcasted_iota(jnp.int32, (SB_BLK, SB_BLK), 1)
    tri_ex = jnp.where(row > col, 1.0, 0.0).astype(BF16)
    tri_in = jnp.where(row >= col, 1.0, 0.0).astype(BF16)
    return col - row, jnp.concatenate([tri_ex, tri_ex], axis=0), jnp.concatenate([tri_in, tri_in], axis=0)


def sb_fwd(p, n_heads, name, tq=1024, nsub=8, dep=None):
    s = p.shape[0]
    h_n = n_heads
    b = SB_BLK
    nqs = tq // b
    tk = nsub * b
    scale = 1.0 / math.sqrt(HEAD_DIM)

    dep_args, dep_specs = _after(dep)

    def body(q_ref, k_ref, v_ref, *rest):
        o_ref, w_ref = rest[-2:]
        qi = pl.program_id(1)
        dcol, tri_ex, _ = _sb_consts()
        qv = [q_ref[qs * b:(qs + 1) * b, :] for qs in range(nqs)]
        n_groups = ((qi + 1) * nqs - 1) // nsub + 1

        def step(it, carry, masked):
            c1s, accs = carry
            g = n_groups - 1 - it
            off = pl.multiple_of(g * tk, tk)
            kg = k_ref[pl.ds(off, tk), :]
            vg = v_ref[pl.ds(off, tk), :]
            new_c1, new_acc = [], []
            for qs in range(nqs):
                qb = qi * nqs + qs
                z2 = lax.dot_general(qv[qs], kg, (((1,), (1,)), ((), ())),
                                     preferred_element_type=F32) * (scale * LOG2E)
                blocks = [_sb_scores(z2[:, j * b:(j + 1) * b], (qb - (g * nsub + j)) * b, dcol, tri_ex, masked)
                          for j in range(nsub)]
                run = c1s[qs]
                ws = [None] * nsub
                for j in reversed(range(nsub)):
                    mask, lb, l1m, ls_loc = blocks[j]
                    wj = jnp.exp2(lb + ls_loc + run)
                    ws[j] = (jnp.where(mask, wj, 0.0) if masked else wj).astype(BF16)
                    run = run + jnp.sum(l1m, axis=1, keepdims=True)
                w = jnp.concatenate(ws, axis=1)
                w_ref[0, g, qs * b:(qs + 1) * b, :] = w
                new_acc.append(accs[qs] + jnp.dot(w, vg, preferred_element_type=F32))
                new_c1.append(run)
            return tuple(new_c1), tuple(new_acc)

        init = (tuple(jnp.zeros((b, 1), F32) for _ in range(nqs)),
                tuple(jnp.zeros((b, HEAD_DIM), F32) for _ in range(nqs)))
        assert all(((i + 1) * nqs - 1) // nsub * nsub <= i * nqs for i in range(s // tq))
        first = step(0, init, True)
        _, accs = lax.fori_loop(1, n_groups, functools.partial(step, masked=False), first)
        for qs in range(nqs):
            o_ref[qs * b:(qs + 1) * b, :] = accs[qs]

    return pl.pallas_call(
        body, name=name, grid=(h_n, s // tq),
        in_specs=[pl.BlockSpec((tq, HEAD_DIM), lambda h, i: (i, h)),
                  pl.BlockSpec((s, HEAD_DIM), lambda h, i: (0, h_n + h)),
                  pl.BlockSpec((s, HEAD_DIM), lambda h, i: (0, 2 * h_n + h))] + dep_specs,
        out_specs=[pl.BlockSpec((tq, HEAD_DIM), lambda h, i: (i, h)),
                   pl.BlockSpec((1, s // tk, tq, tk), lambda h, i: (h, 0, i, 0))],
        out_shape=[jax.ShapeDtypeStruct((s, h_n * HEAD_DIM), F32),
                   jax.ShapeDtypeStruct((h_n, s // tk, s, tk), BF16)],
        compiler_params=_params("parallel", "arbitrary"),
    )(p, p, p, *dep_args)


def sb_bwd(p, a, wts, da, n_heads, name, tq=1024, dep=None):
    s = p.shape[0]
    h_n = n_heads
    nq = s // tq
    b = SB_BLK
    nqs = tq // b
    tk = wts.shape[3]
    nsub = tk // b
    scale = 1.0 / math.sqrt(HEAD_DIM)
    dep_args, dep_specs = _after(dep)

    def body(q_ref, k_ref, v_ref, a_ref, da_ref, w_ref, *rest):
        dq_ref, dk_ref, dv_ref, dk_acc, dv_acc = rest[-5:]
        qi = pl.program_id(1)

        @pl.when(qi == 0)
        def _():
            dk_acc[...] = jnp.zeros_like(dk_acc)
            dv_acc[...] = jnp.zeros_like(dv_acc)

        dcol, _, tri_in = _sb_consts()
        q_all = q_ref[...]
        do_all = da_ref[...]
        qv = [q_ref[qs * b:(qs + 1) * b, :] for qs in range(nqs)]
        dov = [da_ref[qs * b:(qs + 1) * b, :] for qs in range(nqs)]
        tots = [jnp.sum(dov[qs].astype(F32) * a_ref[qs * b:(qs + 1) * b, :], axis=1, keepdims=True)
                for qs in range(nqs)]
        n_groups = ((qi + 1) * nqs - 1) // nsub + 1

        def step(it, carry, masked):
            c2s, dqs = carry
            g = n_groups - 1 - it
            off = pl.multiple_of(g * tk, tk)
            kg = k_ref[pl.ds(off, tk), :]
            vg = v_ref[pl.ds(off, tk), :]
            w_all = w_ref[0, g]
            new_c2, new_dq, dz_rows = [], [], []
            for qs in range(nqs):
                qb = qi * nqs + qs
                z2 = lax.dot_general(qv[qs], kg, (((1,), (1,)), ((), ())),
                                     preferred_element_type=F32) * (-scale * LOG2E)
                dw = lax.dot_general(dov[qs], vg, (((1,), (1,)), ((), ())), preferred_element_type=F32)
                beta = 1.0 / (1.0 + jnp.exp2(z2))
                e = dw * w_all[qs * b:(qs + 1) * b, :].astype(F32)
                run2 = c2s[qs]
                dzs = [None] * nsub
                for j in reversed(range(nsub)):
                    cols = slice(j * b, (j + 1) * b)
                    later = _split_dot(e[:, cols], tri_in) + run2
                    bj = beta[:, cols]
                    dz = (e[:, cols] * (1.0 - bj) - bj * (tots[qs] - later)) * scale
                    if masked:
                        dz = jnp.where(dcol < (qb - (g * nsub + j)) * b, dz, 0.0)
                    dzs[j] = dz.astype(BF16)
                    run2 = run2 + jnp.sum(e[:, cols], axis=1, keepdims=True)
                dzq = jnp.concatenate(dzs, axis=1)
                new_dq.append(dqs[qs] + jnp.dot(dzq, kg, preferred_element_type=F32))
                new_c2.append(run2)
                dz_rows.append(dzq)
            dz_all = jnp.concatenate(dz_rows, axis=0)
            dk_acc[pl.ds(off, tk), :] += lax.dot_general(dz_all, q_all, (((0,), (0,)), ((), ())),
                                                         preferred_element_type=F32)
            dv_acc[pl.ds(off, tk), :] += lax.dot_general(w_all, do_all, (((0,), (0,)), ((), ())),
                                                         preferred_element_type=F32)
            return tuple(new_c2), tuple(new_dq)

        zeros = tuple(jnp.zeros((b, 1), F32) for _ in range(nqs))
        assert all(((i + 1) * nqs - 1) // nsub * nsub <= i * nqs for i in range(s // tq))
        first = step(0, (zeros, tuple(jnp.zeros((b, HEAD_DIM), F32) for _ in range(nqs))), True)
        _, dqs = lax.fori_loop(1, n_groups, functools.partial(step, masked=False), first)
        for qs in range(nqs):
            dq_ref[qs * b:(qs + 1) * b, :] = dqs[qs].astype(BF16)

        @pl.when(qi == nq - 1)
        def _():
            dk_ref[...] = dk_acc[...].astype(BF16)
            dv_ref[...] = dv_acc[...].astype(BF16)

    blk = pl.BlockSpec((tq, HEAD_DIM), lambda h, i: (i, h))
    full = pl.BlockSpec((s, HEAD_DIM), lambda h, i: (0, h))
    return pl.pallas_call(
        body, name=name, grid=(h_n, nq),
        in_specs=[blk, pl.BlockSpec((s, HEAD_DIM), lambda h, i: (0, h_n + h)),
                  pl.BlockSpec((s, HEAD_DIM), lambda h, i: (0, 2 * h_n + h)), blk, blk,
                  pl.BlockSpec((1, s // tk, tq, tk), lambda h, i: (h, 0, i, 0))] + dep_specs,
        out_specs=[blk, full, full],
        out_shape=[jax.ShapeDtypeStruct((s, h_n * HEAD_DIM), BF16)] * 3,
        scratch_shapes=[pltpu.VMEM((s, HEAD_DIM), F32), pltpu.VMEM((s, HEAD_DIM), F32)],
        compiler_params=_params("parallel", "arbitrary"),
    )(p, p, p, a, da, wts, *dep_args)


def _pool_window(xx, win, r0, rc):
    cur = xx[HALO:HALO + rc]
    ws = _window_sum(xx, win, True)[HALO:HALO + rc]
    t_idx = r0 + lax.broadcasted_iota(jnp.int32, (rc, 1), 0)
    inv = 1.0 / jnp.minimum(win, t_idx + 1).astype(F32)
    return ws * inv - cur, inv


def even_mix_fwd(a, p, pool_w, pool_scale, name, rc=512, dep=None):
    s = p.shape[0]
    ng = len(POOL_WINDOWS)
    cw = pool_w.shape[1]
    n_chunks = s // rc
    dep_args, dep_specs = _after(dep)

    def body(a_ref, u_ref, g_ref, w_ref, sc_ref, *rest):
        y_ref, upad = rest[-2:]
        j = pl.program_id(0)

        @pl.when(j < ng)
        def _():
            def chunk(ci, carry):
                rows = pl.ds(pl.multiple_of(ci * rc, rc), rc)
                y_ref[rows, :] = (a_ref[rows, :] * _silu(g_ref[rows, :].astype(F32))).astype(BF16)
                return carry

            lax.fori_loop(0, n_chunks, chunk, 0)

        for gi, win in enumerate(POOL_WINDOWS):
            @pl.when(j == ng + gi)
            def _(win=win):
                upad[0:HALO, :] = jnp.zeros((HALO, cw), F32)

                def fill(ci, carry):
                    r0 = pl.multiple_of(ci * rc, rc)
                    upad[pl.ds(pl.multiple_of(r0 + HALO, HALO), rc), :] = u_ref[pl.ds(r0, rc), :].astype(F32)
                    return carry

                lax.fori_loop(0, n_chunks, fill, 0)

                def chunk(ci, carry):
                    r0 = pl.multiple_of(ci * rc, rc)
                    rows = pl.ds(r0, rc)
                    pooled, _ = _pool_window(upad[pl.ds(r0, HALO + rc), :], win, r0, rc)
                    t = jnp.dot(pooled.astype(BF16), w_ref[0], preferred_element_type=F32)
                    y_ref[rows, :] = (t * sc_ref[...] * _silu(g_ref[rows, :].astype(F32))).astype(BF16)
                    return carry

                lax.fori_loop(0, n_chunks, chunk, 0)

    grp = lambda j: jnp.maximum(j - ng, 0)
    return pl.pallas_call(
        body, name=name, grid=(2 * ng,),
        in_specs=[pl.BlockSpec((s, cw), lambda j: (0, jnp.minimum(j, ng - 1))),
                  pl.BlockSpec((s, cw), lambda j: (0, 3 * ng + grp(j))),
                  pl.BlockSpec((s, cw), lambda j: (0, 4 * ng + j)),
                  pl.BlockSpec((1, cw, cw), lambda j: (grp(j), 0, 0)),
                  pl.BlockSpec((1, cw), lambda j: (0, grp(j)))] + dep_specs,
        out_specs=pl.BlockSpec((s, cw), lambda j: (0, j)),
        out_shape=jax.ShapeDtypeStruct((s, 2 * ng * cw), BF16),
        scratch_shapes=[pltpu.VMEM((HALO + s, cw), F32)],
        compiler_params=_params("arbitrary"),
    )(a, p, p, pool_w, pool_scale, *dep_args)


def even_mix_bwd(dy, a, p, pool_w, pool_scale, name, rc=512):
    s = p.shape[0]
    ng = len(POOL_WINDOWS)
    cw = pool_w.shape[1]
    n_chunks = s // rc

    def body(dy_ref, a_ref, u_ref, g_ref, w_ref, sc_ref, da_ref, du_ref, dg_ref, dw_ref, dsc_ref,
             upad, rpad, dpl, dw_acc, dsc_acc):
        j = pl.program_id(0)

        @pl.when(j < ng)
        def _():
            def chunk(ci, carry):
                rows = pl.ds(pl.multiple_of(ci * rc, rc), rc)
                dyv = dy_ref[rows, :].astype(F32)
                sg, dsg = _silu_and_grad(g_ref[rows, :].astype(F32))
                da_ref[rows, :] = (dyv * sg).astype(BF16)
                dg_ref[rows, :] = (dyv * a_ref[rows, :] * dsg).astype(BF16)
                return carry

            lax.fori_loop(0, n_chunks, chunk, 0)

        for gi, win in enumerate(POOL_WINDOWS):
            @pl.when(j == ng + gi)
            def _(win=win):
                upad[0:HALO, :] = jnp.zeros((HALO, cw), F32)
                rpad[s:s + HALO, :] = jnp.zeros((HALO, cw), F32)
                dw_acc[...] = jnp.zeros_like(dw_acc)
                dsc_acc[...] = jnp.zeros_like(dsc_acc)

                def fill(ci, carry):
                    r0 = pl.multiple_of(ci * rc, rc)
                    upad[pl.ds(pl.multiple_of(r0 + HALO, HALO), rc), :] = u_ref[pl.ds(r0, rc), :].astype(F32)
                    return carry

                lax.fori_loop(0, n_chunks, fill, 0)

                def chunk(ci, carry):
                    r0 = pl.multiple_of(ci * rc, rc)
                    rows = pl.ds(r0, rc)
                    pooled, inv = _pool_window(upad[pl.ds(r0, HALO + rc), :], win, r0, rc)
                    pb = pooled.astype(BF16)
                    wv = w_ref[0]
                    t = jnp.dot(pb, wv, preferred_element_type=F32)
                    scv = sc_ref[...]
                    dyv = dy_ref[rows, :].astype(F32)
                    sg, dsg = _silu_and_grad(g_ref[rows, :].astype(F32))
                    dpo = dyv * sg
                    dg_ref[rows, :] = (dyv * t * scv * dsg).astype(BF16)
                    dsc_acc[...] += _rowsum8(dpo * t)
                    dtb = (dpo * scv).astype(BF16)
                    dw_acc[...] += lax.dot_general(pb, dtb, (((0,), (0,)), ((), ())),
                                                   preferred_element_type=F32)
                    dpooled = lax.dot_general(dtb, wv, (((1,), (1,)), ((), ())),
                                              preferred_element_type=F32)
                    dpl[rows, :] = dpooled
                    rpad[rows, :] = dpooled * inv
                    return carry

                lax.fori_loop(0, n_chunks, chunk, 0)

                def chunk2(ci, carry):
                    r0 = pl.multiple_of(ci * rc, rc)
                    rows = pl.ds(r0, rc)
                    xx = rpad[pl.ds(r0, rc + HALO), :]
                    fs = _window_sum(xx, win, False)[0:rc]
                    du_ref[rows, :] = (fs - dpl[rows, :]).astype(BF16)
                    return carry

                lax.fori_loop(0, n_chunks, chunk2, 0)
                dw_ref[0] = dw_acc[...]
                dsc_ref[...] = jnp.sum(dsc_acc[...], axis=0, keepdims=True)

    grp = lambda j: jnp.maximum(j - ng, 0)
    att = lambda j: jnp.minimum(j, ng - 1)
    return pl.pallas_call(
        body, name=name, grid=(2 * ng,),
        in_specs=[pl.BlockSpec((s, cw), lambda j: (0, j)),
                  pl.BlockSpec((s, cw), lambda j: (0, att(j))),
                  pl.BlockSpec((s, cw), lambda j: (0, 3 * ng + grp(j))),
                  pl.BlockSpec((s, cw), lambda j: (0, 4 * ng + j)),
                  pl.BlockSpec((1, cw, cw), lambda j: (grp(j), 0, 0)),
                  pl.BlockSpec((1, cw), lambda j: (0, grp(j)))],
        out_specs=[pl.BlockSpec((s, cw), lambda j: (0, att(j))),
                   pl.BlockSpec((s, cw), lambda j: (0, grp(j))),
                   pl.BlockSpec((s, cw), lambda j: (0, j)),
                   pl.BlockSpec((1, cw, cw), lambda j: (grp(j), 0, 0)),
                   pl.BlockSpec((1, cw), lambda j: (0, grp(j)))],
        out_shape=[jax.ShapeDtypeStruct((s, ng * cw), BF16), jax.ShapeDtypeStruct((s, ng * cw), BF16),
                   jax.ShapeDtypeStruct((s, 2 * ng * cw), BF16),
                   jax.ShapeDtypeStruct((ng, cw, cw), F32), jax.ShapeDtypeStruct((1, ng * cw), F32)],
        scratch_shapes=[pltpu.VMEM((HALO + s, cw), F32), pltpu.VMEM((s + HALO, cw), F32),
                        pltpu.VMEM((s, cw), F32), pltpu.VMEM((cw, cw), F32), pltpu.VMEM((8, cw), F32)],
        compiler_params=_params("arbitrary"),
    )(dy, a, p, p, pool_w, pool_scale)


def _halo_before(tm):
    return lambda i: jnp.maximum(i * (tm // HALO) - 1, 0)


def _halo_after(tm, s):
    return lambda i: jnp.minimum((i + 1) * (tm // HALO), s // HALO - 1)


def odd_mix_fwd(p, sconv_w, dconv_w, dconv_b, cnorm_g, cnorm_b, name, tm=128, dep=None):
    s = p.shape[0]
    cw = sconv_w.shape[1]
    n = s // tm
    lanes = 128
    hb = _halo_before(tm)

    dep_args, dep_specs = _after(dep)

    def body(hc_ref, hch_ref, bc_ref, cc_ref, cch_ref, ga_ref, gah_ref, gb_ref, gbh_ref, g1_ref, g2_ref,
             sw_ref, dw_ref, db_ref, gam_ref, bet_ref, *rest):
        y_ref, dc_ref = rest[-2:]
        first = pl.program_id(0) == 0
        for l in range(cw // lanes):
            cols = slice(l * lanes, (l + 1) * lanes)
            mh = jnp.where(first, 0.0, cch_ref[:, cols].astype(F32) * hch_ref[:, cols].astype(F32))
            mm = cc_ref[:, cols].astype(F32) * hc_ref[:, cols].astype(F32)
            xx = jnp.concatenate([mh, mm], axis=0)
            tap = _Taps(xx, tm, True)
            cv = jnp.zeros((tm, lanes), F32)
            for k in range(SCONV_K):
                cv = cv + sw_ref[k:k + 1, cols] * tap(SCONV_K - 1 - k)
            c_out = bc_ref[:, cols].astype(F32) * cv
            y_ref[:, cols] = (c_out * _silu(g1_ref[:, cols].astype(F32))).astype(BF16)
            dh = jnp.where(first, 0.0, gah_ref[:, cols].astype(F32) * _sigmoid(gbh_ref[:, cols].astype(F32)))
            dm = ga_ref[:, cols].astype(F32) * _sigmoid(gb_ref[:, cols].astype(F32))
            xx = jnp.concatenate([dh, dm], axis=0)
            tap = _Taps(xx, tm, True)
            acc = jnp.zeros((tm, lanes), F32) + db_ref[:, cols]
            for k in range(CONF_K):
                acc = acc + dw_ref[k:k + 1, cols] * tap(CONF_K - 1 - k)
            dc_ref[:, cols] = acc
        rs = 32
        for r in range(tm // rs):
            rows = slice(r * rs, (r + 1) * rs)
            xv = dc_ref[rows, :]
            mu = jnp.mean(xv, axis=-1, keepdims=True)
            xc = xv - mu
            rstd = lax.rsqrt(jnp.mean(xc * xc, axis=-1, keepdims=True) + EPS)
            ln = xc * rstd * gam_ref[...] + bet_ref[...]
            y_ref[rows, cw:2 * cw] = (_silu(ln) * _silu(g2_ref[rows, :].astype(F32))).astype(BF16)

    main = lambda c: pl.BlockSpec((tm, cw), lambda i: (i, c))
    halo = lambda c: pl.BlockSpec((HALO, cw), lambda i: (hb(i), c))
    vec = lambda r: pl.BlockSpec((r, cw), lambda i: (0, 0))
    return pl.pallas_call(
        body, name=name, grid=(n,),
        in_specs=[main(0), halo(0), main(1), main(2), halo(2), main(3), halo(3), main(4), halo(4),
                  main(5), main(6), vec(SCONV_K), vec(CONF_K), vec(1), vec(1), vec(1)] + dep_specs,
        out_specs=[pl.BlockSpec((tm, 2 * cw), lambda i: (i, 0)), pl.BlockSpec((tm, cw), lambda i: (i, 0))],
        out_shape=[jax.ShapeDtypeStruct((s, 2 * cw), BF16), jax.ShapeDtypeStruct((s, cw), F32)],
        compiler_params=_params("parallel"),
    )(p, p, p, p, p, p, p, p, p, p, p, sconv_w, dconv_w, dconv_b, cnorm_g, cnorm_b, *dep_args)


def odd_bwd_ln(dy, p, dc, cnorm_g, cnorm_b, name, tm=256):
    s = p.shape[0]
    cw = dc.shape[1]
    n = s // tm
    rs = 32

    def body(dy_ref, g2_ref, dc_ref, gam_ref, bet_ref, ddc_ref, dg_ref, dgam_ref, dbet_ref, gacc, bacc):
        i = pl.program_id(0)

        @pl.when(i == 0)
        def _():
            gacc[...] = jnp.zeros_like(gacc)
            bacc[...] = jnp.zeros_like(bacc)

        def chunk(ci, carry):
            rows = pl.ds(pl.multiple_of(ci * rs, rs), rs)
            xv = dc_ref[rows, :]
            mu = jnp.mean(xv, axis=-1, keepdims=True)
            xc = xv - mu
            rstd = lax.rsqrt(jnp.mean(xc * xc, axis=-1, keepdims=True) + EPS)
            xh = xc * rstd
            gam = gam_ref[...]
            sl, dsl = _silu_and_grad(xh * gam + bet_ref[...])
            sg, dsg = _silu_and_grad(g2_ref[rows, :].astype(F32))
            dyv = dy_ref[rows, :].astype(F32)
            dg_ref[rows, :] = (dyv * sl * dsg).astype(BF16)
            dln = dyv * sg * dsl
            gacc[...] += _rowsum8(dln * xh)
            bacc[...] += _rowsum8(dln)
            dxh = dln * gam
            ddc_ref[rows, :] = rstd * (dxh - jnp.mean(dxh, axis=-1, keepdims=True)
                                       - xh * jnp.mean(dxh * xh, axis=-1, keepdims=True))
            return carry

        lax.fori_loop(0, tm // rs, chunk, 0)

        @pl.when(i == n - 1)
        def _():
            dgam_ref[...] = jnp.sum(gacc[...], axis=0, keepdims=True)
            dbet_ref[...] = jnp.sum(bacc[...], axis=0, keepdims=True)

    vec = pl.BlockSpec((1, cw), lambda i: (0, 0))
    return pl.pallas_call(
        body, name=name, grid=(n,),
        in_specs=[pl.BlockSpec((tm, cw), lambda i: (i, 1)), pl.BlockSpec((tm, cw), lambda i: (i, 6)),
                  pl.BlockSpec((tm, cw), lambda i: (i, 0)), vec, vec],
        out_specs=[pl.BlockSpec((tm, cw), lambda i: (i, 0)), pl.BlockSpec((tm, cw), lambda i: (i, 0)), vec, vec],
        out_shape=[jax.ShapeDtypeStruct((s, cw), F32), jax.ShapeDtypeStruct((s, cw), BF16),
                   jax.ShapeDtypeStruct((1, cw), F32), jax.ShapeDtypeStruct((1, cw), F32)],
        scratch_shapes=[pltpu.VMEM((8, cw), F32), pltpu.VMEM((8, cw), F32)],
        compiler_params=_params("arbitrary"),
    )(dy, p, dc, cnorm_g, cnorm_b)


def odd_bwd_conv(dy, p, ddc, dg2, sconv_w, dconv_w, name, tm=128):
    s = p.shape[0]
    cw = ddc.shape[1]
    n = s // tm
    lanes = 128
    hb = _halo_before(tm)
    ha = _halo_after(tm, s)

    def body(dy_ref, dya_ref, g1_ref, g1a_ref, bc_ref, bca_ref, hc_ref, hch_ref, cc_ref, cch_ref,
             ddc_ref, ddca_ref, ga_ref, gah_ref, gb_ref, gbh_ref, dg2_ref, sw_ref, dw_ref,
             dp_ref, dsw_ref, ddw_ref, ddb_ref, sw_acc, dw_acc, db_acc):
        i = pl.program_id(0)
        first = i == 0
        last = i == n - 1

        @pl.when(first)
        def _():
            sw_acc[...] = jnp.zeros_like(sw_acc)
            dw_acc[...] = jnp.zeros_like(dw_acc)
            db_acc[...] = jnp.zeros_like(db_acc)

        for l in range(cw // lanes):
            cols = slice(l * lanes, (l + 1) * lanes)
            mh = jnp.where(first, 0.0, cch_ref[:, cols].astype(F32) * hch_ref[:, cols].astype(F32))
            hcv = hc_ref[:, cols].astype(F32)
            ccv = cc_ref[:, cols].astype(F32)
            xx = jnp.concatenate([mh, ccv * hcv], axis=0)
            tap = _Taps(xx, tm, True)
            taps = [tap(SCONV_K - 1 - k) for k in range(SCONV_K)]
            cv = jnp.zeros((tm, lanes), F32)
            for k in range(SCONV_K):
                cv = cv + sw_ref[k:k + 1, cols] * taps[k]
            bcv = bc_ref[:, cols].astype(F32)
            dyv = dy_ref[:, cols].astype(F32)
            sg, dsg = _silu_and_grad(g1_ref[:, cols].astype(F32))
            dco = dyv * sg
            dp_ref[:, 5 * cw + l * lanes:5 * cw + (l + 1) * lanes] = (dyv * bcv * cv * dsg).astype(BF16)
            dp_ref[:, cw + l * lanes:cw + (l + 1) * lanes] = (dco * cv).astype(BF16)
            dcv = dco * bcv
            for k in range(SCONV_K):
                sw_acc[k * 8:(k + 1) * 8, cols] += _rowsum8(dcv * taps[k])
            dcv_a = jnp.where(last, 0.0, dya_ref[:, cols].astype(F32) * _silu(g1a_ref[:, cols].astype(F32))
                              * bca_ref[:, cols].astype(F32))
            xx = jnp.concatenate([dcv, dcv_a], axis=0)
            tap = _Taps(xx, tm, False)
            dm = jnp.zeros((tm, lanes), F32)
            for k in range(SCONV_K):
                dm = dm + sw_ref[k:k + 1, cols] * tap(SCONV_K - 1 - k)
            dp_ref[:, l * lanes:(l + 1) * lanes] = (dm * ccv).astype(BF16)
            dp_ref[:, 2 * cw + l * lanes:2 * cw + (l + 1) * lanes] = (dm * hcv).astype(BF16)
            gav = ga_ref[:, cols].astype(F32)
            sb = _sigmoid(gb_ref[:, cols].astype(F32))
            dh = jnp.where(first, 0.0, gah_ref[:, cols].astype(F32) * _sigmoid(gbh_ref[:, cols].astype(F32)))
            xx = jnp.concatenate([dh, gav * sb], axis=0)
            ddcv = ddc_ref[:, cols]
            db_acc[:, cols] += _rowsum8(ddcv)
            tap = _Taps(xx, tm, True)
            for k in range(CONF_K):
                dw_acc[k * 8:(k + 1) * 8, cols] += _rowsum8(ddcv * tap(CONF_K - 1 - k))
            ddc_a = jnp.where(last, 0.0, ddca_ref[:, cols])
            xx = jnp.concatenate([ddcv, ddc_a], axis=0)
            tap = _Taps(xx, tm, False)
            dgl = jnp.zeros((tm, lanes), F32)
            for k in range(CONF_K):
                dgl = dgl + dw_ref[k:k + 1, cols] * tap(CONF_K - 1 - k)
            dp_ref[:, 3 * cw + l * lanes:3 * cw + (l + 1) * lanes] = (dgl * sb).astype(BF16)
            dp_ref[:, 4 * cw + l * lanes:4 * cw + (l + 1) * lanes] = (dgl * gav * sb * (1.0 - sb)).astype(BF16)
        dp_ref[:, 6 * cw:7 * cw] = dg2_ref[...]

        @pl.when(last)
        def _():
            for k in range(SCONV_K):
                dsw_ref[k:k + 1, :] = jnp.sum(sw_acc[k * 8:(k + 1) * 8, :], axis=0, keepdims=True)
            for k in range(CONF_K):
                ddw_ref[k:k + 1, :] = jnp.sum(dw_acc[k * 8:(k + 1) * 8, :], axis=0, keepdims=True)
            ddb_ref[...] = jnp.sum(db_acc[...], axis=0, keepdims=True)

    def main(c):
        return pl.BlockSpec((tm, cw), lambda i: (i, c))

    def before(c):
        return pl.BlockSpec((HALO, cw), lambda i: (hb(i), c))

    def after(c):
        return pl.BlockSpec((HALO, cw), lambda i: (ha(i), c))

    def vec(r):
        return pl.BlockSpec((r, cw), lambda i: (0, 0))

    return pl.pallas_call(
        body, name=name, grid=(n,),
        in_specs=[main(0), after(0), main(5), after(5), main(1), after(1), main(0), before(0), main(2), before(2),
                  main(0), after(0), main(3), before(3), main(4), before(4), main(0), vec(SCONV_K), vec(CONF_K)],
        out_specs=[pl.BlockSpec((tm, 7 * cw), lambda i: (i, 0)), vec(SCONV_K), vec(CONF_K), vec(1)],
        out_shape=[jax.ShapeDtypeStruct((s, 7 * cw), BF16), jax.ShapeDtypeStruct((SCONV_K, cw), F32),
                   jax.ShapeDtypeStruct((CONF_K, cw), F32), jax.ShapeDtypeStruct((1, cw), F32)],
        scratch_shapes=[pltpu.VMEM((8 * SCONV_K, cw), F32), pltpu.VMEM((8 * CONF_K, cw), F32),
                        pltpu.VMEM((8, cw), F32)],
        compiler_params=_params("arbitrary"),
    )(dy, dy, p, p, p, p, p, p, p, p, ddc, ddc, p, p, p, p, dg2, sconv_w, dconv_w)


_ANY = pl.BlockSpec(memory_space=pl.ANY)


def _place():
    return lax.axis_index("x"), lax.axis_index("y"), lax.axis_index("c")


def all_gather(arrs, name, deps=()):
    n = len(arrs)

    def body(*refs):
        ins, outs = refs[:n], refs[n + len(deps):2 * n + len(deps)]
        send_sems, recv_sems, local_sems = refs[-3:]
        x, y, c = _place()
        me, sibling = (x, y, c), (x, y, 1 - c)
        chips = [(1 - x, y), (x, 1 - y), (1 - x, 1 - y)]

        def copy(a, k, block, to, src=None):
            px, py, pc = block
            dst = outs[a].at[4 * px + 2 * py + pc]
            return pltpu.make_async_remote_copy(
                src_ref=dst if src is None else src, dst_ref=dst,
                send_sem=send_sems.at[7 * a + k], recv_sem=recv_sems.at[7 * a + k],
                device_id=to, device_id_type=MESH)

        mine = [pltpu.make_async_copy(ins[a], outs[a].at[4 * x + 2 * y + c], local_sems.at[a]) for a in range(n)]
        first = []
        for a in range(n):
            first.append(copy(a, 0, me, sibling, src=ins[a]))
            first += [copy(a, 1 + j, me, (*chip, c), src=ins[a]) for j, chip in enumerate(chips)]
        for cp in first + mine:
            cp.start()
        passed = []
        for a in range(n):
            for j, chip in enumerate(chips):
                copy(a, 1 + j, (*chip, c), me).wait_recv()
                cp = copy(a, 4 + j, (*chip, c), sibling)
                cp.start()
                passed.append(cp)
        for a in range(n):
            copy(a, 0, sibling, me).wait_recv()
            for j, chip in enumerate(chips):
                copy(a, 4 + j, (*chip, 1 - c), me).wait_recv()
        for cp in first + passed:
            cp.wait_send()
        for cp in mine:
            cp.wait()

    return pl.pallas_call(
        body, name=name,
        out_shape=[jax.ShapeDtypeStruct((N_DEV,) + a.shape, a.dtype) for a in arrs],
        in_specs=[_ANY] * (n + len(deps)), out_specs=[_ANY] * n,
        scratch_shapes=[pltpu.SemaphoreType.DMA((7 * n,)), pltpu.SemaphoreType.DMA((7 * n,)),
                        pltpu.SemaphoreType.DMA((n,))],
    )(*arrs, *deps)


def in_proj_gathered(xs, g, w_own, extras, name, tm=1024):
    s, d = xs.shape
    n = w_own.shape[1]
    tm = min(tm, s)
    arrs = [w_own] + list(extras)
    na = len(arrs)
    tr = 256

    def body(*refs):
        x_ref, g_ref, ins = refs[0], refs[1], refs[2:2 + na]
        h_out, p_ref, outs = refs[2 + na], refs[3 + na], refs[4 + na:4 + 2 * na]
        (h_ref, xbuf, wbuf, obuf, send_sems, recv_sems, load_sem, store_sems, own_sems, h_sem,
         x_sems) = refs[4 + 2 * na:]
        x, y, c = _place()
        me, sibling = (x, y, c), (x, y, 1 - c)
        x_first = c == 0
        near = (jnp.where(x_first, 1 - x, x), jnp.where(x_first, y, 1 - y))
        far = (jnp.where(x_first, x, 1 - x), jnp.where(x_first, 1 - y, y))
        diag = (1 - x, 1 - y)
        k_near, k_far = jnp.where(x_first, 1, 2), jnp.where(x_first, 2, 1)
        f_near, f_far = k_near + 3, k_far + 3

        def slot(block):
            return 4 * block[0] + 2 * block[1] + block[2]

        def copy(a, k, block, to, src=None):
            dst = outs[a].at[slot(block)]
            return pltpu.make_async_remote_copy(
                src_ref=dst if src is None else src, dst_ref=dst,
                send_sem=send_sems.at[7 * a + k], recv_sem=recv_sems.at[7 * a + k],
                device_id=to, device_id_type=MESH)

        first = []
        for a in range(na):
            first += [copy(a, 0, me, sibling, src=ins[a]), copy(a, 1, me, (1 - x, y, c), src=ins[a]),
                      copy(a, 2, me, (x, 1 - y, c), src=ins[a])]
        for cp in first:
            cp.start()
        own = pltpu.make_async_copy(wbuf.at[0], outs[0].at[slot(me)], own_sems.at[0])
        mine = [pltpu.make_async_copy(ins[a], outs[a].at[slot(me)], own_sems.at[a]) for a in range(1, na)]
        stores = [None, None]

        def x_load(i):
            return pltpu.make_async_copy(x_ref.at[pl.ds(i * tr, tr), :], xbuf.at[i % 2], x_sems.at[i % 2])

        x_load(0).start()
        for i in range(s // tr):
            if i + 1 < s // tr:
                x_load(i + 1).start()
            x_load(i).wait()
            xv = xbuf[i % 2]
            r = lax.rsqrt(jnp.mean(xv * xv, axis=-1, keepdims=True) + EPS)
            h_ref[i * tr:(i + 1) * tr, :] = (xv * r * g_ref[...]).astype(BF16)
        h_store = pltpu.make_async_copy(h_ref, h_out, h_sem)
        h_store.start()

        def multiply(k, block, w_from):
            b = k % 2
            if k == 2:
                own.wait()
            load = pltpu.make_async_copy(w_from, wbuf.at[b], load_sem)
            load.start()
            if stores[b] is not None:
                stores[b].wait()
            load.wait()
            if k == 0:
                own.start()

            def chunk(i, carry):
                rows = pl.ds(pl.multiple_of(i * tm, tm), tm)
                obuf[b, rows, :] = jnp.dot(h_ref[rows, :], wbuf[b], preferred_element_type=F32).astype(BF16)
                return carry

            lax.fori_loop(0, s // tm, chunk, 0)
            stores[b] = pltpu.make_async_copy(
                obuf.at[b], p_ref.at[:, pl.ds(pl.multiple_of(slot(block) * n, 128), n)], store_sems.at[b])
            stores[b].start()

        passed = []

        def arrive(a, k, block):
            copy(a, k, block, me).wait_recv()

        def pass_on(a, k, block, to):
            cp = copy(a, k, block, to)
            cp.start()
            passed.append(cp)

        def gather(arrays, use):
            def arrive_all(k, block):
                for a in arrays:
                    arrive(a, k, block)

            def pass_all(k, block, to):
                for a in arrays:
                    pass_on(a, k, block, to)

            use(0, me)
            arrive_all(0, sibling)
            use(1, sibling)
            arrive_all(k_near, (*near, c))
            pass_all(3, (*near, c), (*far, c))
            pass_all(f_near, (*near, c), sibling)
            use(2, (*near, c))
            arrive_all(f_far, (*far, 1 - c))
            use(3, (*far, 1 - c))
            arrive_all(k_far, (*far, c))
            pass_all(f_far, (*far, c), sibling)
            use(4, (*far, c))
            arrive_all(f_near, (*near, 1 - c))
            use(5, (*near, 1 - c))
            arrive_all(3, (*diag, c))
            pass_all(6, (*diag, c), sibling)
            use(6, (*diag, c))
            arrive_all(6, (*diag, 1 - c))
            use(7, (*diag, 1 - c))

        gather(range(na), lambda k, block: multiply(k, block, ins[0] if k == 0 else outs[0].at[slot(block)]))
        for cp in mine:
            cp.start()
        for cp in first + passed:
            cp.wait_send()
        for cp in mine + stores + [h_store]:
            cp.wait()

    vmem = pl.BlockSpec(memory_space=pltpu.VMEM)
    outs = pl.pallas_call(
        body, name=name,
        out_shape=[jax.ShapeDtypeStruct((s, d), BF16), jax.ShapeDtypeStruct((s, N_DEV * n), BF16)]
        + [jax.ShapeDtypeStruct((N_DEV,) + a.shape, a.dtype) for a in arrs],
        in_specs=[_ANY, vmem] + [_ANY] * na, out_specs=[_ANY] * (2 + na),
        scratch_shapes=[pltpu.VMEM((s, d), BF16), pltpu.VMEM((2, tr, d), F32), pltpu.VMEM((2, d, n), BF16),
                        pltpu.VMEM((2, s, n), BF16),
                        pltpu.SemaphoreType.DMA((7 * na,)), pltpu.SemaphoreType.DMA((7 * na,)),
                        pltpu.SemaphoreType.DMA, pltpu.SemaphoreType.DMA((2,)), pltpu.SemaphoreType.DMA((na,)),
                        pltpu.SemaphoreType.DMA, pltpu.SemaphoreType.DMA((2,))],
        compiler_params=pltpu.CompilerParams(vmem_limit_bytes=VMEM_LIMIT),
    )(xs, g, *arrs)
    return outs[0], outs[1], outs[2], outs[3:]


_HBM = pl.BlockSpec(memory_space=pltpu.HBM)
_SEM = pl.BlockSpec(memory_space=pltpu.SEMAPHORE)
_DATAFLOW = pltpu.SideEffectType.DATAFLOW_SIDE_EFFECTING


def _peers_per_array(kind):
    return 1 if kind in ("sibling", "halves") else 3


def _split_copies(kind, srcs, lands, send_sems, recv_sems):
    x, y, c = _place()
    per = _peers_per_array(kind)
    out = []
    for a in range(len(lands)):
        if kind == "sibling":
            part = srcs[a] if srcs[a].shape[1] == 1 else srcs[a].at[:, pl.ds(1 - c, 1)]
            peers = [((x, y, 1 - c), part, lands[a], lands[a])]
        elif kind == "halves":
            mine, its = lands[a].at[:, pl.ds(c, 1)], lands[a].at[:, pl.ds(1 - c, 1)]
            peers = [((x, y, 1 - c), mine, mine, its)]
        else:
            peers = []
            for px, py in [(1 - x, y), (x, 1 - y), (1 - x, 1 - y)]:
                if kind == "gather":
                    views = (srcs[a], lands[a].at[4 * x + 2 * y + c], lands[a].at[4 * px + 2 * py + c])
                else:
                    views = (srcs[a].at[2 * px + py], lands[a].at[2 * x + y], lands[a].at[2 * px + py])
                peers.append(((px, py, c),) + views)
        for j, (peer, src, dst, arrives) in enumerate(peers):
            sems = dict(send_sem=send_sems.at[per * a + j], recv_sem=recv_sems.at[per * a + j],
                        device_id=peer, device_id_type=MESH)
            out.append((pltpu.make_async_remote_copy(src_ref=src, dst_ref=dst, **sems),
                        pltpu.make_async_remote_copy(src_ref=src, dst_ref=arrives, **sems)))
    return out


def split_start(kind, srcs, lands, deps, name):
    ns, nl = len(srcs), len(lands)
    n_sems = _peers_per_array(kind) * nl
    held = list(srcs) + list(lands)

    def body(*refs):
        send_sems, recv_sems = refs[len(held) + len(deps)], refs[len(held) + len(deps) + 1]
        for copy, _ in _split_copies(kind, refs[:ns], refs[ns:ns + nl], send_sems, recv_sems):
            copy.start()
        token = refs[-1]
        token[...] = jnp.zeros_like(token)

    outs = pl.pallas_call(
        body, name=name,
        out_shape=(pltpu.SemaphoreType.DMA((n_sems,)), pltpu.SemaphoreType.DMA((n_sems,)),
                   *[pltpu.HBM(a.shape, a.dtype) for a in held], jax.ShapeDtypeStruct((8, 128), F32)),
        in_specs=[_HBM] * len(held) + [_ANY] * len(deps),
        out_specs=(_SEM, _SEM, *([_HBM] * len(held)), pl.BlockSpec(memory_space=pltpu.VMEM)),
        input_output_aliases={i: 2 + i for i in range(len(held))},
        compiler_params=pltpu.CompilerParams(has_side_effects=_DATAFLOW),
    )(*[pltpu.with_memory_space_constraint(a, pltpu.HBM) for a in held], *deps)
    return outs[0], outs[1], list(outs[2:2 + ns]), list(outs[2 + ns:2 + ns + nl]), outs[-1]


def split_wait(kind, send_sems, recv_sems, srcs, lands, afters, name):
    ns, nl = len(srcs), len(lands)
    held = list(srcs) + list(lands)

    def body(*refs):
        for _, arrival in _split_copies(kind, refs[:ns], refs[ns:ns + nl], refs[ns + nl], refs[ns + nl + 1]):
            arrival.wait_send()
            arrival.wait_recv()

    outs = pl.pallas_call(
        body, name=name,
        out_shape=[pltpu.HBM(a.shape, a.dtype) for a in held],
        in_specs=[_HBM] * len(held) + [_SEM, _SEM] + [_ANY] * len(afters),
        out_specs=[_HBM] * len(held),
        input_output_aliases={i: i for i in range(len(held))},
        compiler_params=pltpu.CompilerParams(has_side_effects=_DATAFLOW),
    )(*held, send_sems, recv_sems, *afters)
    return list(outs[:ns]), list(outs[ns:])


def place_block(land, block, dev, name):
    r, c = block.shape
    tr = min(r, 512)

    def body(dev_ref, land_ref, b_ref, o_ref):
        del dev_ref, land_ref
        o_ref[...] = b_ref[...]

    return pl.pallas_call(
        body, name=name,
        grid_spec=pltpu.PrefetchScalarGridSpec(
            num_scalar_prefetch=1, grid=(r // tr,),
            in_specs=[_ANY, pl.BlockSpec((tr, c), lambda i, dev_ref: (i, 0))],
            out_specs=pl.BlockSpec((None, tr, c), lambda i, dev_ref: (dev_ref[0], i, 0))),
        out_shape=jax.ShapeDtypeStruct(land.shape, land.dtype),
        input_output_aliases={1: 0},
        compiler_params=_params("parallel"),
    )(dev, land, block)


def pair_add(own, recv, core, name):
    _, _, r, c = own.shape
    tr = min(r, 2048)

    def body(core_ref, own_ref, recv_ref, o_ref):
        del core_ref
        o_ref[...] = (own_ref[...].astype(F32) + recv_ref[...].astype(F32)).astype(BF16)

    return pl.pallas_call(
        body, name=name,
        grid_spec=pltpu.PrefetchScalarGridSpec(
            num_scalar_prefetch=1, grid=(4, r // tr),
            in_specs=[pl.BlockSpec((None, None, tr, c), lambda k, i, core_ref: (k, core_ref[0], i, 0)),
                      pl.BlockSpec((None, None, tr, c), lambda k, i, core_ref: (k, 0, i, 0))],
            out_specs=pl.BlockSpec((None, tr, c), lambda k, i, core_ref: (k, i, 0))),
        out_shape=jax.ShapeDtypeStruct((4, r, c), BF16),
        compiler_params=_params("parallel", "parallel"),
    )(core, own, recv)


def _adamw_math(w, g, m, v):
    m2 = ADAM_B1 * m + (1.0 - ADAM_B1) * g
    v2 = ADAM_B2 * v + (1.0 - ADAM_B2) * (g * g)
    m_hat = m2 / (1.0 - ADAM_B1 ** ADAM_STEP)
    v_hat = v2 / (1.0 - ADAM_B2 ** ADAM_STEP)
    delta = -ADAM_LR * (m_hat / (jnp.sqrt(v_hat) + ADAM_EPS) + ADAM_WD * w)
    return delta, m2, v2


def adamw_big(w, m, v, own, got, chip, name):
    r, c = w.shape
    tr = min(r, 512)

    def body(chip_ref, w_ref, m_ref, v_ref, p0, p1, p2, p3, g_ref, d_ref, m2_ref, v2_ref):
        del chip_ref
        g = ((p0[...].astype(F32) + p1[...].astype(F32)) + p2[...].astype(F32)) + p3[...].astype(F32)
        delta, m2, v2 = _adamw_math(w_ref[...], g, m_ref[...], v_ref[...])
        g_ref[...] = g
        d_ref[...] = delta
        m2_ref[...] = m2
        v2_ref[...] = v2

    row = pl.BlockSpec((tr, c), lambda i, chip_ref: (i, 0))

    def slab(flip):
        return pl.BlockSpec((None, tr, c), lambda i, chip_ref: (chip_ref[0] ^ flip, i, 0))

    return pl.pallas_call(
        body, name=name,
        grid_spec=pltpu.PrefetchScalarGridSpec(
            num_scalar_prefetch=1, grid=(r // tr,),
            in_specs=[row, row, row, slab(0), slab(1), slab(2), slab(3)],
            out_specs=[row] * 4),
        out_shape=[jax.ShapeDtypeStruct((r, c), F32)] * 4,
        compiler_params=_params("parallel"),
    )(chip, w, m, v, own, got, got, got)


def sum_devices(g8, name):
    def body(g_ref, o_ref):
        tot = g_ref[0]
        for k in range(1, N_DEV):
            tot = tot + g_ref[k]
        o_ref[...] = tot

    return pl.pallas_call(body, name=name, out_shape=jax.ShapeDtypeStruct(g8.shape[1:], F32))(g8)


def adamw_small(ws, gs, ms, vs, name):
    n = len(ws)

    def body(*refs):
        w_r, g_r, m_r, v_r = refs[:n], refs[n:2 * n], refs[2 * n:3 * n], refs[3 * n:4 * n]
        d_o, m_o, v_o = refs[4 * n:5 * n], refs[5 * n:6 * n], refs[6 * n:7 * n]
        for k in range(n):
            delta, m2, v2 = _adamw_math(w_r[k][...], g_r[k][...], m_r[k][...], v_r[k][...])
            d_o[k][...] = delta
            m_o[k][...] = m2
            v_o[k][...] = v2

    shapes = [jax.ShapeDtypeStruct(w.shape, F32) for w in ws]
    outs = pl.pallas_call(body, name=name, out_shape=shapes * 3)(*ws, *gs, *ms, *vs)
    return outs[:n], outs[n:2 * n], outs[2 * n:]


def _rows128(a):
    return a.reshape(-1, 128)


def _pad_rows(a, rows):
    return jnp.pad(a, ((0, rows - a.shape[0]), (0, 0)))


def kernel(x, ln_pre_even, w_in_even, pool_w, pool_scale, w_out_even, ln_post_even, ln_pre_odd, w_in_odd, sconv_w, dconv_w, dconv_b, cnorm_g, cnorm_b, w_out_odd, ln_post_odd, loss_target, m_ln_pre_even, m_w_in_even, m_pool_w, m_pool_scale, m_w_out_even, m_ln_post_even, m_ln_pre_odd, m_w_in_odd, m_sconv_w, m_dconv_w, m_dconv_b, m_cnorm_g, m_cnorm_b, m_w_out_odd, m_ln_post_odd, v_ln_pre_even, v_w_in_even, v_pool_w, v_pool_scale, v_w_out_even, v_ln_post_even, v_ln_pre_odd, v_w_in_odd, v_sconv_w, v_dconv_w, v_dconv_b, v_cnorm_g, v_cnorm_b, v_w_out_odd, v_ln_post_odd):
    xs = x[0]
    tgt = loss_target[0]
    s, d = xs.shape
    half = d // 2
    n_heads = half // HEAD_DIM
    ng = len(POOL_WINDOWS)
    cwp = half // ng
    dev = 4 * lax.axis_index("x") + 2 * lax.axis_index("y") + lax.axis_index("c")
    core = lax.axis_index("c").astype(jnp.int32).reshape(1)

    pr = pool_w.shape[2]
    cl = sconv_w.shape[2]
    small_parts = [(_rows128(ln_pre_odd), 8), (sconv_w[0], 8), (dconv_w[0], 32), (dconv_b, 8),
                   (cnorm_g, 8), (cnorm_b, 8), (_rows128(ln_post_odd), 8)]
    small_local = jnp.concatenate([_pad_rows(a, r) for a, r in small_parts], axis=0)
    h0, p0, g_wie, (g_pw, g_small) = in_proj_gathered(
        xs, ln_pre_even, w_in_even[0].astype(BF16), [pool_w[0].reshape(ng * pr, cwp).astype(BF16), small_local],
        "ag_in_proj_even")
    comm = _Exchanges(dev, core, d)
    token = comm.start_weights("out_even", [w_out_even[0].astype(BF16)], [p0])
    token = comm.start_weights("in_odd", [w_in_odd[0].astype(BF16)], [token])
    sb_dep = comm.start_weights("out_odd", [w_out_odd[0].astype(BF16)], [token])
    pool_full = g_pw.reshape(N_DEV, ng, pr, cwp).transpose(1, 0, 2, 3).reshape(ng, cwp, cwp)
    nl = ln_pre_odd.shape[1] // 128

    def chan(lo, rows):
        return g_small[:, lo:lo + rows].transpose(1, 0, 2).reshape(rows, N_DEV * cl)

    ln_pre_odd_f = g_small[:, 0:nl].reshape(1, d)
    sconv_f = chan(8, SCONV_K)
    dconv_f = chan(16, CONF_K)
    dconv_b_f = chan(48, 1)
    cnorm_g_f = chan(56, 1)
    cnorm_b_f = chan(64, 1)
    ln_post_odd_f = g_small[:, 72:72 + nl].reshape(1, d)

    loss_blk, grad_x, small_g = _fwd_bwd(
        xs, tgt, ln_pre_even, h0, p0, g_wie, pool_full, pool_scale, ln_post_even, ln_pre_odd_f,
        sconv_f, dconv_f, dconv_b_f, cnorm_g_f, cnorm_b_f, ln_post_odd_f, comm, sb_dep)
    small_w = [ln_pre_even, pool_scale, ln_post_even, ln_pre_odd, sconv_w[0], dconv_w[0], dconv_b, cnorm_g, cnorm_b, ln_post_odd]
    small_m = [m_ln_pre_even, m_pool_scale, m_ln_post_even, m_ln_pre_odd, m_sconv_w[0], m_dconv_w[0], m_dconv_b, m_cnorm_g, m_cnorm_b, m_ln_post_odd]
    small_v = [v_ln_pre_even, v_pool_scale, v_ln_post_even, v_ln_pre_odd, v_sconv_w[0], v_dconv_w[0], v_dconv_b, v_cnorm_g, v_cnorm_b, v_ln_post_odd]
    big = {"w_in_even": (w_in_even, m_w_in_even, v_w_in_even), "pool_w": (pool_w, m_pool_w, v_pool_w),
           "w_out_even": (w_out_even, m_w_out_even, v_w_out_even), "w_in_odd": (w_in_odd, m_w_in_odd, v_w_in_odd),
           "w_out_odd": (w_out_odd, m_w_out_odd, v_w_out_odd)}
    upd = comm.finish_updates(big, [grad_x])
    upd.update(comm.finish_updates(big, [grad_x]))
    sg, sd, sm, sv, loss = _update_small(small_g, loss_blk, small_w, small_m, small_v, dev, d, cl,
                                         deps=[upd["w_in_odd"][1], upd["w_out_even"][1]])
    upd.update(comm.finish_updates(big, sd))
    (g_wie_o, d_wie, m_wie, v_wie), (g_pw_o, d_pw, m_pw, v_pw) = upd["w_in_even"], upd["pool_w"]
    (g_woe_o, d_woe, m_woe, v_woe), (g_wio_o, d_wio, m_wio, v_wio) = upd["w_out_even"], upd["w_in_odd"]
    g_woo_o, d_woo, m_woo, v_woo = upd["w_out_odd"]

    def order(small, wie, pw, woe, wio, woo):
        return [small[0], wie, pw, small[1], woe, small[2], small[3], wio, small[4], small[5], small[6],
                small[7], small[8], woo, small[9]]

    grads = order(sg, g_wie_o, g_pw_o, g_woe_o, g_wio_o, g_woo_o)
    deltas = order(sd, d_wie, d_pw, d_woe, d_wio, d_woo)
    new_m = order(sm, m_wie, m_pw, m_woe, m_wio, m_woo)
    new_v = order(sv, v_wie, v_pw, v_woe, v_wio, v_woo)
    return (loss, grad_x[None], *grads, *deltas, *new_m, *new_v)


def _fwd_bwd(xs, tgt, ln_pre_even, h0, p0, g_wie, pool_full, pool_scale, ln_post_even, ln_pre_odd_f,
             sconv_f, dconv_f, dconv_b_f, cnorm_g_f, cnorm_b_f, ln_post_odd_f, comm, sb_dep):
    d = xs.shape[1]
    n_heads = d // 2 // HEAD_DIM
    ng, cwp = pool_full.shape[0], pool_full.shape[1]
    a0, sb_wts = sb_fwd(p0, n_heads, "sb_fwd", dep=sb_dep)
    dep = comm.weights_arrived("out_even", after=a0)
    y0 = even_mix_fwd(a0, p0, pool_full, pool_scale, "even_mix_fwd", dep=dep)
    (w_out_e,) = comm.weights("out_even", after=y0)
    w_out_e = w_out_e.reshape(1, d, d)
    o0 = mm_nn(y0, w_out_e, BF16, "out_proj_even", tn=512)
    dep = comm.weights_arrived("in_odd", after=o0)
    x1, h1 = postnorm_fwd(xs, o0, ln_post_even, ln_pre_odd_f, "post_even", dep=dep)
    (g_wio,) = comm.weights("in_odd", after=x1)
    p1 = mm_nn(h1, g_wio, BF16, "in_proj_odd", group=2)
    dep = comm.weights_arrived("out_odd", after=p1)
    y1, dc = odd_mix_fwd(p1, sconv_f, dconv_f, dconv_b_f, cnorm_g_f, cnorm_b_f, "odd_mix_fwd", dep=dep)
    (w_out_o,) = comm.weights("out_odd", after=y1)
    w_out_o = w_out_o.reshape(1, d, d)
    o1 = mm_nn(y1, w_out_o, BF16, "out_proj_odd", tn=512)
    loss_blk, gx2, do1, dg_post_odd = final_fwd_bwd(x1, o1, ln_post_odd_f, tgt, "post_odd_loss")

    dw_out_o = mm_tn(y1, do1, 1, BF16, "dw_out_odd")
    dy1 = mm_nt(do1, w_out_o, BF16, "dy_odd")
    ddc, dg2, dgam, dbet = odd_bwd_ln(dy1, p1, dc, cnorm_g_f, cnorm_b_f, "odd_bwd_ln")
    dp1, dsconv, ddconv, ddconv_b = odd_bwd_conv(dy1, p1, ddc, dg2, sconv_f, dconv_f, "odd_bwd_conv")
    dw_in_o = mm_tn(h1, dp1, N_DEV, BF16, "dw_in_odd", group=2)
    dep = comm.reduce_begin({"w_out_odd": dw_out_o.reshape(N_DEV, d // N_DEV, d), "w_in_odd": dw_in_o}, "odd")
    dh1 = mm_nt(dp1, g_wio, BF16, "dh_odd", dep=dep, group=2)
    dep = comm.reduce_send(after=dh1)
    gx1, dg_pre_odd, do0, dg_post_even = norm_bwd(dh1, x1, ln_pre_odd_f, gx2, "pre_odd_post_even_bwd",
                                                  inp2=o0, g2=ln_post_even, dep=dep)

    dw_out_e = mm_tn(y0, do0, 1, BF16, "dw_out_even")
    dy0 = mm_nt(do0, w_out_e, BF16, "dy_even")
    da0, du0, dg0, dpool, dpool_scale = even_mix_bwd(dy0, a0, p0, pool_full, pool_scale, "even_mix_bwd")
    pr = cwp // N_DEV
    dpool_slabs = dpool.astype(BF16).reshape(ng, N_DEV, pr, cwp).transpose(1, 0, 2, 3).reshape(N_DEV, ng * pr, cwp)
    dep = comm.reduce_begin({"w_out_even": dw_out_e.reshape(N_DEV, d // N_DEV, d), "pool_w": dpool_slabs}, "even_out")
    dq0, dk0, dv0 = sb_bwd(p0, a0, sb_wts, da0, n_heads, "sb_bwd", dep=dep)
    dep = comm.reduce_send(after=dq0)
    dp0 = jnp.concatenate([dq0, dk0, dv0, du0, dg0], axis=1)
    dw_sibling = mm_tn(h0, dp0, N_DEV // 2, BF16, "dw_in_even_sibling", dep=dep, pick=(2, 1 - comm.core))
    dep = comm.reduce_begin({"w_in_even": dw_sibling}, "even_in", sibling_part=True)
    dw_own = mm_tn(h0, dp0, N_DEV // 2, BF16, "dw_in_even_own", dep=dep, pick=(2, comm.core))
    dep = comm.reduce_send(after=dw_own, own_part={"w_in_even": dw_own})
    dh0 = mm_nt(dp0, g_wie, BF16, "dh_even", dep=dep, group=2)
    dep = None
    grad_x, dg_pre_even = norm_bwd(dh0, xs, ln_pre_even, gx1, "pre_even_bwd", tm=512, dep=dep)
    small_g = [dg_pre_even, dpool_scale, dg_post_even, dg_pre_odd, dsconv, ddconv, ddconv_b, dgam, dbet, dg_post_odd]
    return loss_blk, grad_x, small_g


class _Exchanges:
    def __init__(self, dev, core, d):
        self.dev = dev.astype(jnp.int32).reshape(1)
        self.core = core
        self.chip = (dev // 2).astype(jnp.int32).reshape(1)
        self.d = d
        self.in_flight = {}
        self.to_sibling = None
        self.pending = []

    def start_weights(self, tag, blocks, afters):
        lands = [lax.empty((N_DEV,) + b.shape, b.dtype) for b in blocks]
        send, recv, srcs, lands, token = split_start("gather", blocks, lands, afters, "ag_start_" + tag)
        self.in_flight[tag] = (send, recv, srcs, lands)
        return token

    def weights_arrived(self, tag, after):
        send, recv, srcs, lands = self.in_flight.pop(tag)
        srcs, lands = split_wait("gather", send, recv, srcs, lands, [after], "ag_wait_" + tag)
        lands = [place_block(l, b, self.dev, "ag_own_%s_%d" % (tag, k)) for k, (l, b) in enumerate(zip(lands, srcs))]
        lands = [l.reshape((4, 2) + l.shape[1:]) for l in lands]
        send, recv, _, lands, token = split_start("halves", [], lands, [], "ag_sibling_start_" + tag)
        self.in_flight[tag] = (send, recv, lands)
        return token

    def weights(self, tag, after):
        send, recv, lands = self.in_flight.pop(tag)
        _, lands = split_wait("halves", send, recv, [], lands, [after], "ag_sibling_wait_" + tag)
        return [l.reshape((N_DEV,) + l.shape[2:]) for l in lands]

    def reduce_begin(self, partials, tag, sibling_part=False):
        names = list(partials)
        arrs = [partials[k].reshape((4, 1 if sibling_part else 2) + partials[k].shape[1:]) for k in names]
        lands = [lax.empty((4, 1) + a.shape[2:], a.dtype) for a in arrs]
        send, recv, srcs, lands, token = split_start("sibling", arrs, lands, [], "rs_sibling_start_" + tag)
        self.to_sibling = (tag, names, send, recv, srcs, lands)
        return token

    def reduce_send(self, after, own_part=None):
        tag, names, send, recv, srcs, lands = self.to_sibling
        srcs, lands = split_wait("sibling", send, recv, srcs, lands, [after], "rs_sibling_wait_" + tag)
        which = self.core
        if own_part is not None:
            srcs = [own_part[k].reshape((4, 1) + own_part[k].shape[1:]) for k in names]
            which = jnp.zeros((1,), jnp.int32)
        sums = [pair_add(o, r, which, "rs_pair_add_" + k) for k, o, r in zip(names, srcs, lands)]
        zones = [lax.empty(a.shape, a.dtype) for a in sums]
        send, recv, srcs, zones, token = split_start("scatter", sums, zones, [], "rs_start_" + tag)
        self.pending.append((tag, names, send, recv, srcs, zones))
        return token

    def finish_updates(self, big, afters):
        tag, names, send, recv, srcs, lands = self.pending.pop(0)
        srcs, lands = split_wait("scatter", send, recv, srcs, lands, afters, "rs_wait_" + tag)
        out = {}
        for name, own, got in zip(names, srcs, lands):
            w, m, v = big[name]
            shp = own.shape[1:]
            outs = adamw_big(w.reshape(shp), m.reshape(shp), v.reshape(shp), own, got, self.chip, "adamw_" + name)
            out[name] = [o.reshape(w.shape) for o in outs]
        return out


def _update_small(small_g, loss_blk, small_w, small_m, small_v, dev, d, cl, deps):
    packed = jnp.concatenate([_rows128(g) for g in small_g] + [loss_blk], axis=0)
    (g8,) = all_gather([packed], "ag_small_grads", deps)
    tot = sum_devices(g8, "sum_small_grads")
    loss = tot[packed.shape[0] - 8, 0]
    full_g = []
    lo = 0
    for g in small_g:
        rows = g.size // 128
        full_g.append(tot[lo:lo + rows].reshape(g.shape))
        lo += rows

    def mine(g, width):
        return lax.dynamic_slice_in_dim(g, dev * width, width, axis=g.ndim - 1)

    fg = full_g
    small_gl = [fg[0], fg[1], fg[2], mine(fg[3], d // N_DEV), mine(fg[4], cl), mine(fg[5], cl), mine(fg[6], cl),
                mine(fg[7], cl), mine(fg[8], cl), mine(fg[9], d // N_DEV)]
    sd, sm, sv = adamw_small(small_w, small_gl, small_m, small_v, "adamw_small")

    def like(k, a):
        return a[None] if k in (4, 5) else a

    sg = [like(k, a) for k, a in enumerate(small_gl)]
    sd = [like(k, a) for k, a in enumerate(sd)]
    sm = [like(k, a) for k, a in enumerate(sm)]
    sv = [like(k, a) for k, a in enumerate(sv)]
    return sg, sd, sm, sv, loss
```

```python
import functools
import math

import jax
import jax.numpy as jnp
from jax import lax
from jax.experimental import pallas as pl
from jax.experimental.pallas import tpu as pltpu

F32 = jnp.float32
BF16 = jnp.bfloat16
EPS = 1e-6
HEAD_DIM = 128
POOL_WINDOWS = (2, 4, 8, 16)
SCONV_K = 3
CONF_K = 31
HALO = 32
N_DEV = 8
VMEM_LIMIT = 56 * 1024 * 1024
MESH = pl.DeviceIdType.MESH

ADAM_LR = 0.001
ADAM_B1 = 0.9
ADAM_B2 = 0.999
ADAM_EPS = 1e-08
ADAM_WD = 0.01
ADAM_STEP = 10


def _params(*sem):
    return pltpu.CompilerParams(dimension_semantics=sem, vmem_limit_bytes=VMEM_LIMIT)


def _sigmoid(v):
    return 1.0 / (1.0 + jnp.exp(-v))


def _silu(v):
    return v * _sigmoid(v)


def _silu_and_grad(v):
    s = _sigmoid(v)
    return v * s, s * (1.0 + v * (1.0 - s))


def _rowsum8(v):
    r, c = v.shape
    return jnp.sum(v.reshape(r // 8, 8, c), axis=0)


SUBLANES = 8


class _Taps:
    def __init__(self, xx, rows, before):
        self.xx, self.rows, self.before, self.rotated = xx, rows, before, {}

    def __call__(self, i):
        r, q = i % SUBLANES, i // SUBLANES
        if r not in self.rotated:
            n = self.xx.shape[0]
            self.rotated[r] = self.xx if r == 0 else pltpu.roll(self.xx, r if self.before else n - r, 0)
        lo = HALO - SUBLANES * q if self.before else SUBLANES * q
        return self.rotated[r][lo:lo + self.rows]


def _window_sum(xx, win, before):
    n = xx.shape[0]
    acc = xx
    k = 1
    while k < win:
        acc = acc + pltpu.roll(acc, k if before else n - k, 0)
        k *= 2
    return acc


def postnorm_fwd(x, o, g, g_next, name, tm=512, dep=None):
    s, d = x.shape
    dep_args, dep_specs = _after(dep)

    def body(x_ref, o_ref, g_ref, gn_ref, *rest):
        y_ref, h_ref = rest[-2:]
        ov = o_ref[...].astype(F32)
        r = lax.rsqrt(jnp.mean(ov * ov, axis=-1, keepdims=True) + EPS)
        y = x_ref[...] + ov * r * g_ref[...]
        y_ref[...] = y
        r2 = lax.rsqrt(jnp.mean(y * y, axis=-1, keepdims=True) + EPS)
        h_ref[...] = (y * r2 * gn_ref[...]).astype(BF16)

    row = pl.BlockSpec((tm, d), lambda i: (i, 0))
    vec = pl.BlockSpec((1, d), lambda i: (0, 0))
    return pl.pallas_call(
        body, name=name, grid=(s // tm,),
        in_specs=[row, row, vec, vec] + dep_specs, out_specs=[row, row],
        out_shape=[jax.ShapeDtypeStruct((s, d), F32), jax.ShapeDtypeStruct((s, d), BF16)],
        compiler_params=_params("parallel"),
    )(x, o, g, g_next, *dep_args)


def final_fwd_bwd(x1, o, g, target, name, tm=512):
    s, d = x1.shape
    n = s // tm

    def body(x_ref, o_ref, g_ref, t_ref, loss_ref, gx_ref, do_ref, dg_ref, lacc, gacc):
        i = pl.program_id(0)

        @pl.when(i == 0)
        def _():
            lacc[...] = jnp.zeros_like(lacc)
            gacc[...] = jnp.zeros_like(gacc)

        ov = o_ref[...].astype(F32)
        gv = g_ref[...]
        r = lax.rsqrt(jnp.mean(ov * ov, axis=-1, keepdims=True) + EPS)
        oh = ov * r
        diff = x_ref[...] + oh * gv - t_ref[...]
        lacc[...] += _rowsum8(diff * diff)
        gx = diff * (1.0 / d)
        gx_ref[...] = gx
        gacc[...] += _rowsum8(gx * oh)
        dn = gx * gv
        do_ref[...] = (r * (dn - oh * jnp.mean(dn * oh, axis=-1, keepdims=True))).astype(BF16)

        @pl.when(i == n - 1)
        def _():
            tot = jnp.sum(jnp.sum(lacc[...], axis=0, keepdims=True), axis=1, keepdims=True)
            loss_ref[...] = jnp.broadcast_to(tot * (0.5 / d), loss_ref.shape)
            dg_ref[...] = jnp.sum(gacc[...], axis=0, keepdims=True)

    row = pl.BlockSpec((tm, d), lambda i: (i, 0))
    vec = pl.BlockSpec((1, d), lambda i: (0, 0))
    return pl.pallas_call(
        body, name=name, grid=(n,),
        in_specs=[row, row, vec, row],
        out_specs=[pl.BlockSpec((8, 128), lambda i: (0, 0)), row, row, vec],
        out_shape=[jax.ShapeDtypeStruct((8, 128), F32), jax.ShapeDtypeStruct((s, d), F32),
                   jax.ShapeDtypeStruct((s, d), BF16), jax.ShapeDtypeStruct((1, d), F32)],
        scratch_shapes=[pltpu.VMEM((8, d), F32), pltpu.VMEM((8, d), F32)],
        compiler_params=_params("arbitrary"),
    )(x1, o, g, target)


def _rms_bwd_rows(dyv, xv, gv):
    r = lax.rsqrt(jnp.mean(xv * xv, axis=-1, keepdims=True) + EPS)
    xh = xv * r
    dn = dyv * gv
    return r * (dn - xh * jnp.mean(dn * xh, axis=-1, keepdims=True)), _rowsum8(dyv * xh)


def norm_bwd(dy, inp, g, resid, name, inp2=None, g2=None, tm=256, dep=None):
    s, d = inp.shape
    n = s // tm
    chain = inp2 is not None

    def body(*refs):
        dy_ref, x_ref, g_ref, r_ref = refs[:4]
        outs = refs[-6:] if chain else refs[-3:]
        i = pl.program_id(0)

        @pl.when(i == 0)
        def _():
            for acc in outs[-2:] if chain else outs[-1:]:
                acc[...] = jnp.zeros_like(acc)

        if chain:
            x2_ref, g2_ref = refs[4:6]
            dx_ref, dg_ref, dx2_ref, dg2_ref, gacc, gacc2 = outs
        else:
            dx_ref, dg_ref, gacc = outs
        dx, dg_rows = _rms_bwd_rows(dy_ref[...].astype(F32), x_ref[...], g_ref[...])
        dx = dx + r_ref[...]
        dx_ref[...] = dx
        gacc[...] += dg_rows
        if chain:
            dx2, dg2_rows = _rms_bwd_rows(dx, x2_ref[...].astype(F32), g2_ref[...])
            dx2_ref[...] = dx2.astype(BF16)
            gacc2[...] += dg2_rows

        @pl.when(i == n - 1)
        def _():
            dg_ref[...] = jnp.sum(gacc[...], axis=0, keepdims=True)
            if chain:
                dg2_ref[...] = jnp.sum(gacc2[...], axis=0, keepdims=True)

    row = pl.BlockSpec((tm, d), lambda i: (i, 0))
    vec = pl.BlockSpec((1, d), lambda i: (0, 0))
    dep_args, dep_specs = _after(dep)
    extra = [inp2, g2] if chain else []
    return pl.pallas_call(
        body, name=name, grid=(n,),
        in_specs=[row, row, vec, row] + ([row, vec] if chain else []) + dep_specs,
        out_specs=[row, vec] * (2 if chain else 1),
        out_shape=[jax.ShapeDtypeStruct((s, d), F32), jax.ShapeDtypeStruct((1, d), F32)]
        + ([jax.ShapeDtypeStruct((s, d), BF16), jax.ShapeDtypeStruct((1, d), F32)] if chain else []),
        scratch_shapes=[pltpu.VMEM((8, d), F32)] * (2 if chain else 1),
        compiler_params=_params("arbitrary"),
    )(dy, inp, g, resid, *extra, *dep_args)


def _after(dep):
    if dep is None:
        return [], []
    return [dep], [pl.BlockSpec((8, 128), lambda *_: (0, 0))]


def _lane_concat(ref, count):
    return ref[0] if count == 1 else jnp.concatenate([ref[i] for i in range(count)], axis=1)


def mm_nn(a, w, out_dtype, name, tm=2048, tn=None, dep=None, group=1):
    m, k = a.shape
    tm = min(tm, m)
    ns, _, n = w.shape
    tn = n if tn is None else tn
    nj = n // tn
    assert group == 1 or nj == 1
    dep_args, dep_specs = _after(dep)

    def body(a_ref, w_ref, *rest):
        o_ref = rest[-1]
        o_ref[...] = jnp.dot(a_ref[...], _lane_concat(w_ref, group), preferred_element_type=F32).astype(out_dtype)

    return pl.pallas_call(
        body, name=name, grid=(ns // group, nj, m // tm),
        in_specs=[pl.BlockSpec((tm, k), lambda s, j, i: (i, 0)),
                  pl.BlockSpec((group, k, tn), lambda s, j, i: (s, 0, j))] + dep_specs,
        out_specs=pl.BlockSpec((tm, group * tn), lambda s, j, i: (i, s * nj + j)),
        out_shape=jax.ShapeDtypeStruct((m, ns * n), out_dtype),
        compiler_params=_params("parallel", "parallel", "parallel"),
    )(a, w, *dep_args)


def mm_nt(a, w, out_dtype, name, tm=1024, tn=None, dep=None, group=1):
    m = a.shape[0]
    tm = min(tm, m)
    ns, k, n = w.shape
    tn = n if tn is None else tn
    nj = n // tn
    assert group == 1 or nj == 1
    steps = ns * nj // group
    dep_args, dep_specs = _after(dep)

    def body(a_ref, w_ref, *rest):
        o_ref, acc = rest[-2:]
        r = pl.program_id(1)

        @pl.when(r == 0)
        def _():
            acc[...] = jnp.zeros_like(acc)

        acc[...] += lax.dot_general(a_ref[...], _lane_concat(w_ref, group), (((1,), (1,)), ((), ())),
                                    preferred_element_type=F32)

        @pl.when(r == steps - 1)
        def _():
            o_ref[...] = acc[...].astype(out_dtype)

    return pl.pallas_call(
        body, name=name, grid=(m // tm, steps),
        in_specs=[pl.BlockSpec((tm, group * tn), lambda i, r: (i, r)),
                  pl.BlockSpec((group, k, tn), lambda i, r: (r // nj, 0, r % nj))] + dep_specs,
        out_specs=pl.BlockSpec((tm, k), lambda i, r: (i, 0)),
        out_shape=jax.ShapeDtypeStruct((m, k), out_dtype),
        scratch_shapes=[pltpu.VMEM((tm, k), F32)],
        compiler_params=_params("parallel", "arbitrary"),
    )(a, w, *dep_args)


def mm_tn(a, b, ns, out_dtype, name, tk=1024, tm=2048, dep=None, pick=None, group=1):
    m, k = a.shape
    tm = min(tm, m)
    step, offset = (1, None) if pick is None else pick
    assert group == 1 or pick is None
    n = b.shape[1] // (ns * step)
    steps = m // tm
    dep_args, dep_specs = _after(dep)
    n_pre = 0 if pick is None else 1

    def b_block(s, j, r, *pre):
        return (r, s if pick is None else step * s + pre[0][0])

    def body(*refs):
        a_ref, b_ref = refs[n_pre:n_pre + 2]
        o_ref, acc = refs[-2:]
        r = pl.program_id(2)

        @pl.when(r == 0)
        def _():
            acc[...] = jnp.zeros_like(acc)

        acc[...] += lax.dot_general(a_ref[...], b_ref[...], (((0,), (0,)), ((), ())),
                                    preferred_element_type=F32)

        @pl.when(r == steps - 1)
        def _():
            for i in range(group):
                o_ref[i] = acc[:, i * n:(i + 1) * n].astype(out_dtype)

    return pl.pallas_call(
        body, name=name,
        grid_spec=pltpu.PrefetchScalarGridSpec(
            num_scalar_prefetch=n_pre, grid=(ns // group, k // tk, steps),
            in_specs=[pl.BlockSpec((tm, tk), lambda s, j, r, *pre: (r, j)),
                      pl.BlockSpec((tm, group * n), b_block)] + dep_specs,
            out_specs=pl.BlockSpec((group, tk, n), lambda s, j, r, *pre: (s, j, 0)),
            scratch_shapes=[pltpu.VMEM((tk, group * n), F32)]),
        out_shape=jax.ShapeDtypeStruct((ns, k, n), out_dtype),
        compiler_params=_params("parallel", "parallel", "arbitrary"),
    )(*([] if pick is None else [offset]), a, b, *dep_args)


SB_BLK = 128


LOG2E = 1.0 / math.log(2.0)


def _split_dot(v, tri2):
    hi = pltpu.bitcast(pltpu.bitcast(v, jnp.uint32) & jnp.uint32(0xFFFF0000), F32)
    lo = (v - hi).astype(BF16)
    return jnp.dot(jnp.concatenate([hi.astype(BF16), lo], axis=1), tri2, preferred_element_type=F32)


def _sb_scores(z2, lim, dcol, tri_ex, masked):
    sp = jnp.log2(1.0 + jnp.exp2(-jnp.abs(z2)))
    lb = jnp.minimum(z2, 0.0) - sp
    l1m = lb - z2
    mask = None
    if masked:
        mask = dcol < lim
        l1m = jnp.where(mask, l1m, 0.0)
    return mask, lb, l1m, _split_dot(l1m, tri_ex)


def _sb_consts():
    row = lax.broadcasted_iota(jnp.int32, (SB_BLK, SB_BLK), 0)
    col = lax.broadcasted_iota(jnp.int32, (SB_BLK, SB_BLK), 1)
    tri_ex = jnp.where(row > col, 1.0, 0.0).astype(BF16)
    tri_in = jnp.where(row >= col, 1.0, 0.0).astype(BF16)
    return col - row, jnp.concatenate([tri_ex, tri_ex], axis=0), jnp.concatenate([tri_in, tri_in], axis=0)


def sb_fwd(p, n_heads, name, tq=1024, nsub=8, dep=None):
    s = p.shape[0]
    h_n = n_heads
    b = SB_BLK
    nqs = tq // b
    tk = nsub * b
    scale = 1.0 / math.sqrt(HEAD_DIM)

    dep_args, dep_specs = _after(dep)

    def body(q_ref, k_ref, v_ref, *rest):
        o_ref, w_ref = rest[-2:]
        qi = pl.program_id(1)
        dcol, tri_ex, _ = _sb_consts()
        qv = [q_ref[qs * b:(qs + 1) * b, :] for qs in range(nqs)]
        n_groups = ((qi + 1) * nqs - 1) // nsub + 1

        def step(it, carry, masked):
            c1s, accs = carry
            g = n_groups - 1 - it
            off = pl.multiple_of(g * tk, tk)
            kg = k_ref[pl.ds(off, tk), :]
            vg = v_ref[pl.ds(off, tk), :]
            new_c1, new_acc = [], []
            for qs in range(nqs):
                qb = qi * nqs + qs
                z2 = lax.dot_general(qv[qs], kg, (((1,), (1,)), ((), ())),
                                     preferred_element_type=F32) * (scale * LOG2E)
                blocks = [_sb_scores(z2[:, j * b:(j + 1) * b], (qb - (g * nsub + j)) * b, dcol, tri_ex, masked)
                          for j in range(nsub)]
                run = c1s[qs]
                ws = [None] * nsub
                for j in reversed(range(nsub)):
                    mask, lb, l1m, ls_loc = blocks[j]
                    wj = jnp.exp2(lb + ls_loc + run)
                    ws[j] = (jnp.where(mask, wj, 0.0) if masked else wj).astype(BF16)
                    run = run + jnp.sum(l1m, axis=1, keepdims=True)
                w = jnp.concatenate(ws, axis=1)
                w_ref[0, g, qs * b:(qs + 1) * b, :] = w
                new_acc.append(accs[qs] + jnp.dot(w, vg, preferred_element_type=F32))
                new_c1.append(run)
            return tuple(new_c1), tuple(new_acc)

        init = (tuple(jnp.zeros((b, 1), F32) for _ in range(nqs)),
                tuple(jnp.zeros((b, HEAD_DIM), F32) for _ in range(nqs)))
        assert all(((i + 1) * nqs - 1) // nsub * nsub <= i * nqs for i in range(s // tq))
        first = step(0, init, True)
        _, accs = lax.fori_loop(1, n_groups, functools.partial(step, masked=False), first)
        for qs in range(nqs):
            o_ref[qs * b:(qs + 1) * b, :] = accs[qs]

    return pl.pallas_call(
        body, name=name, grid=(h_n, s // tq),
        in_specs=[pl.BlockSpec((tq, HEAD_DIM), lambda h, i: (i, h)),
                  pl.BlockSpec((s, HEAD_DIM), lambda h, i: (0, h_n + h)),
                  pl.BlockSpec((s, HEAD_DIM), lambda h, i: (0, 2 * h_n + h))] + dep_specs,
        out_specs=[pl.BlockSpec((tq, HEAD_DIM), lambda h, i: (i, h)),
                   pl.BlockSpec((1, s // tk, tq, tk), lambda h, i: (h, 0, i, 0))],
        out_shape=[jax.ShapeDtypeStruct((s, h_n * HEAD_DIM), F32),
                   jax.ShapeDtypeStruct((h_n, s // tk, s, tk), BF16)],
        compiler_params=_params("parallel", "arbitrary"),
    )(p, p, p, *dep_args)


def sb_bwd(p, a, wts, da, n_heads, name, tq=1024, dep=None):
    s = p.shape[0]
    h_n = n_heads
    nq = s // tq
    b = SB_BLK
    nqs = tq // b
    tk = wts.shape[3]
    nsub = tk // b
    scale = 1.0 / math.sqrt(HEAD_DIM)
    dep_args, dep_specs = _after(dep)

    def body(q_ref, k_ref, v_ref, a_ref, da_ref, w_ref, *rest):
        dq_ref, dk_ref, dv_ref, dk_acc, dv_acc = rest[-5:]
        qi = pl.program_id(1)

        @pl.when(qi == 0)
        def _():
            dk_acc[...] = jnp.zeros_like(dk_acc)
            dv_acc[...] = jnp.zeros_like(dv_acc)

        dcol, _, tri_in = _sb_consts()
        q_all = q_ref[...]
        do_all = da_ref[...]
        qv = [q_ref[qs * b:(qs + 1) * b, :] for qs in range(nqs)]
        dov = [da_ref[qs * b:(qs + 1) * b, :] for qs in range(nqs)]
        tots = [jnp.sum(dov[qs].astype(F32) * a_ref[qs * b:(qs + 1) * b, :], axis=1, keepdims=True)
                for qs in range(nqs)]
        n_groups = ((qi + 1) * nqs - 1) // nsub + 1

        def step(it, carry, masked):
            c2s, dqs = carry
            g = n_groups - 1 - it
            off = pl.multiple_of(g * tk, tk)
            kg = k_ref[pl.ds(off, tk), :]
            vg = v_ref[pl.ds(off, tk), :]
            w_all = w_ref[0, g]
            new_c2, new_dq, dz_rows = [], [], []
            for qs in range(nqs):
                qb = qi * nqs + qs
                z2 = lax.dot_general(qv[qs], kg, (((1,), (1,)), ((), ())),
                                     preferred_element_type=F32) * (-scale * LOG2E)
                dw = lax.dot_general(dov[qs], vg, (((1,), (1,)), ((), ())), preferred_element_type=F32)
                beta = 1.0 / (1.0 + jnp.exp2(z2))
                e = dw * w_all[qs * b:(qs + 1) * b, :].astype(F32)
                run2 = c2s[qs]
                dzs = [None] * nsub
                for j in reversed(range(nsub)):
                    cols = slice(j * b, (j + 1) * b)
                    later = _split_dot(e[:, cols], tri_in) + run2
                    bj = beta[:, cols]
                    dz = (e[:, cols] * (1.0 - bj) - bj * (tots[qs] - later)) * scale
                    if masked:
                        dz = jnp.where(dcol < (qb - (g * nsub + j)) * b, dz, 0.0)
                    dzs[j] = dz.astype(BF16)
                    run2 = run2 + jnp.sum(e[:, cols], axis=1, keepdims=True)
                dzq = jnp.concatenate(dzs, axis=1)
                new_dq.append(dqs[qs] + jnp.dot(dzq, kg, preferred_element_type=F32))
                new_c2.append(run2)
                dz_rows.append(dzq)
            dz_all = jnp.concatenate(dz_rows, axis=0)
            dk_acc[pl.ds(off, tk), :] += lax.dot_general(dz_all, q_all, (((0,), (0,)), ((), ())),
                                                         preferred_element_type=F32)
            dv_acc[pl.ds(off, tk), :] += lax.dot_general(w_all, do_all, (((0,), (0,)), ((), ())),
                                                         preferred_element_type=F32)
            return tuple(new_c2), tuple(new_dq)

        zeros = tuple(jnp.zeros((b, 1), F32) for _ in range(nqs))
        assert all(((i + 1) * nqs - 1) // nsub * nsub <= i * nqs for i in range(s // tq))
        first = step(0, (zeros, tuple(jnp.zeros((b, HEAD_DIM), F32) for _ in range(nqs))), True)
        _, dqs = lax.fori_loop(1, n_groups, functools.partial(step, masked=False), first)
        for qs in range(nqs):
            dq_ref[qs * b:(qs + 1) * b, :] = dqs[qs].astype(BF16)

        @pl.when(qi == nq - 1)
        def _():
            dk_ref[...] = dk_acc[...].astype(BF16)
            dv_ref[...] = dv_acc[...].astype(BF16)

    blk = pl.BlockSpec((tq, HEAD_DIM), lambda h, i: (i, h))
    full = pl.BlockSpec((s, HEAD_DIM), lambda h, i: (0, h))
    return pl.pallas_call(
        body, name=name, grid=(h_n, nq),
        in_specs=[blk, pl.BlockSpec((s, HEAD_DIM), lambda h, i: (0, h_n + h)),
                  pl.BlockSpec((s, HEAD_DIM), lambda h, i: (0, 2 * h_n + h)), blk, blk,
                  pl.BlockSpec((1, s // tk, tq, tk), lambda h, i: (h, 0, i, 0))] + dep_specs,
        out_specs=[blk, full, full],
        out_shape=[jax.ShapeDtypeStruct((s, h_n * HEAD_DIM), BF16)] * 3,
        scratch_shapes=[pltpu.VMEM((s, HEAD_DIM), F32), pltpu.VMEM((s, HEAD_DIM), F32)],
        compiler_params=_params("parallel", "arbitrary"),
    )(p, p, p, a, da, wts, *dep_args)


def _pool_window(xx, win, r0, rc):
    cur = xx[HALO:HALO + rc]
    ws = _window_sum(xx, win, True)[HALO:HALO + rc]
    t_idx = r0 + lax.broadcasted_iota(jnp.int32, (rc, 1), 0)
    inv = 1.0 / jnp.minimum(win, t_idx + 1).astype(F32)
    return ws * inv - cur, inv


def even_mix_fwd(a, p, pool_w, pool_scale, name, rc=512, dep=None):
    s = p.shape[0]
    ng = len(POOL_WINDOWS)
    cw = pool_w.shape[1]
    n_chunks = s // rc
    dep_args, dep_specs = _after(dep)

    def body(a_ref, u_ref, g_ref, w_ref, sc_ref, *rest):
        y_ref, upad = rest[-2:]
        j = pl.program_id(0)

        @pl.when(j < ng)
        def _():
            def chunk(ci, carry):
                rows = pl.ds(pl.multiple_of(ci * rc, rc), rc)
                y_ref[rows, :] = (a_ref[rows, :] * _silu(g_ref[rows, :].astype(F32))).astype(BF16)
                return carry

            lax.fori_loop(0, n_chunks, chunk, 0)

        for gi, win in enumerate(POOL_WINDOWS):
            @pl.when(j == ng + gi)
            def _(win=win):
                upad[0:HALO, :] = jnp.zeros((HALO, cw), F32)

                def fill(ci, carry):
                    r0 = pl.multiple_of(ci * rc, rc)
                    upad[pl.ds(pl.multiple_of(r0 + HALO, HALO), rc), :] = u_ref[pl.ds(r0, rc), :].astype(F32)
                    return carry

                lax.fori_loop(0, n_chunks, fill, 0)

                def chunk(ci, carry):
                    r0 = pl.multiple_of(ci * rc, rc)
                    rows = pl.ds(r0, rc)
                    pooled, _ = _pool_window(upad[pl.ds(r0, HALO + rc), :], win, r0, rc)
                    t = jnp.dot(pooled.astype(BF16), w_ref[0], preferred_element_type=F32)
                    y_ref[rows, :] = (t * sc_ref[...] * _silu(g_ref[rows, :].astype(F32))).astype(BF16)
                    return carry

                lax.fori_loop(0, n_chunks, chunk, 0)

    grp = lambda j: jnp.maximum(j - ng, 0)
    return pl.pallas_call(
        body, name=name, grid=(2 * ng,),
        in_specs=[pl.BlockSpec((s, cw), lambda j: (0, jnp.minimum(j, ng - 1))),
                  pl.BlockSpec((s, cw), lambda j: (0, 3 * ng + grp(j))),
                  pl.BlockSpec((s, cw), lambda j: (0, 4 * ng + j)),
                  pl.BlockSpec((1, cw, cw), lambda j: (grp(j), 0, 0)),
                  pl.BlockSpec((1, cw), lambda j: (0, grp(j)))] + dep_specs,
        out_specs=pl.BlockSpec((s, cw), lambda j: (0, j)),
        out_shape=jax.ShapeDtypeStruct((s, 2 * ng * cw), BF16),
        scratch_shapes=[pltpu.VMEM((HALO + s, cw), F32)],
        compiler_params=_params("arbitrary"),
    )(a, p, p, pool_w, pool_scale, *dep_args)


def even_mix_bwd(dy, a, p, pool_w, pool_scale, name, rc=512):
    s = p.shape[0]
    ng = len(POOL_WINDOWS)
    cw = pool_w.shape[1]
    n_chunks = s // rc

    def body(dy_ref, a_ref, u_ref, g_ref, w_ref, sc_ref, da_ref, du_ref, dg_ref, dw_ref, dsc_ref,
             upad, rpad, dpl, dw_acc, dsc_acc):
        j = pl.program_id(0)

        @pl.when(j < ng)
        def _():
            def chunk(ci, carry):
                rows = pl.ds(pl.multiple_of(ci * rc, rc), rc)
                dyv = dy_ref[rows, :].astype(F32)
                sg, dsg = _silu_and_grad(g_ref[rows, :].astype(F32))
                da_ref[rows, :] = (dyv * sg).astype(BF16)
                dg_ref[rows, :] = (dyv * a_ref[rows, :] * dsg).astype(BF16)
                return carry

            lax.fori_loop(0, n_chunks, chunk, 0)

        for gi, win in enumerate(POOL_WINDOWS):
            @pl.when(j == ng + gi)
            def _(win=win):
                upad[0:HALO, :] = jnp.zeros((HALO, cw), F32)
                rpad[s:s + HALO, :] = jnp.zeros((HALO, cw), F32)
                dw_acc[...] = jnp.zeros_like(dw_acc)
                dsc_acc[...] = jnp.zeros_like(dsc_acc)

                def fill(ci, carry):
                    r0 = pl.multiple_of(ci * rc, rc)
                    upad[pl.ds(pl.multiple_of(r0 + HALO, HALO), rc), :] = u_ref[pl.ds(r0, rc), :].astype(F32)
                    return carry

                lax.fori_loop(0, n_chunks, fill, 0)

                def chunk(ci, carry):
                    r0 = pl.multiple_of(ci * rc, rc)
                    rows = pl.ds(r0, rc)
                    pooled, inv = _pool_window(upad[pl.ds(r0, HALO + rc), :], win, r0, rc)
                    pb = pooled.astype(BF16)
                    wv = w_ref[0]
                    t = jnp.dot(pb, wv, preferred_element_type=F32)
                    scv = sc_ref[...]
                    dyv = dy_ref[rows, :].astype(F32)
                    sg, dsg = _silu_and_grad(g_ref[rows, :].astype(F32))
                    dpo = dyv * sg
                    dg_ref[rows, :] = (dyv * t * scv * dsg).astype(BF16)
                    dsc_acc[...] += _rowsum8(dpo * t)
                    dtb = (dpo * scv).astype(BF16)
                    dw_acc[...] += lax.dot_general(pb, dtb, (((0,), (0,)), ((), ())),
                                                   preferred_element_type=F32)
                    dpooled = lax.dot_general(dtb, wv, (((1,), (1,)), ((), ())),
                                              preferred_element_type=F32)
                    dpl[rows, :] = dpooled
                    rpad[rows, :] = dpooled * inv
                    return carry

                lax.fori_loop(0, n_chunks, chunk, 0)

                def chunk2(ci, carry):
                    r0 = pl.multiple_of(ci * rc, rc)
                    rows = pl.ds(r0, rc)
                    xx = rpad[pl.ds(r0, rc + HALO), :]
                    fs = _window_sum(xx, win, False)[0:rc]
                    du_ref[rows, :] = (fs - dpl[rows, :]).astype(BF16)
                    return carry

                lax.fori_loop(0, n_chunks, chunk2, 0)
                dw_ref[0] = dw_acc[...]
                dsc_ref[...] = jnp.sum(dsc_acc[...], axis=0, keepdims=True)

    grp = lambda j: jnp.maximum(j - ng, 0)
    att = lambda j: jnp.minimum(j, ng - 1)
    return pl.pallas_call(
        body, name=name, grid=(2 * ng,),
        in_specs=[pl.BlockSpec((s, cw), lambda j: (0, j)),
                  pl.BlockSpec((s, cw), lambda j: (0, att(j))),
                  pl.BlockSpec((s, cw), lambda j: (0, 3 * ng + grp(j))),
                  pl.BlockSpec((s, cw), lambda j: (0, 4 * ng + j)),
                  pl.BlockSpec((1, cw, cw), lambda j: (grp(j), 0, 0)),
                  pl.BlockSpec((1, cw), lambda j: (0, grp(j)))],
        out_specs=[pl.BlockSpec((s, cw), lambda j: (0, att(j))),
                   pl.BlockSpec((s, cw), lambda j: (0, grp(j))),
                   pl.BlockSpec((s, cw), lambda j: (0, j)),
                   pl.BlockSpec((1, cw, cw), lambda j: (grp(j), 0, 0)),
                   pl.BlockSpec((1, cw), lambda j: (0, grp(j)))],
        out_shape=[jax.ShapeDtypeStruct((s, ng * cw), BF16), jax.ShapeDtypeStruct((s, ng * cw), BF16),
                   jax.ShapeDtypeStruct((s, 2 * ng * cw), BF16),
                   jax.ShapeDtypeStruct((ng, cw, cw), F32), jax.ShapeDtypeStruct((1, ng * cw), F32)],
        scratch_shapes=[pltpu.VMEM((HALO + s, cw), F32), pltpu.VMEM((s + HALO, cw), F32),
                        pltpu.VMEM((s, cw), F32), pltpu.VMEM((cw, cw), F32), pltpu.VMEM((8, cw), F32)],
        compiler_params=_params("arbitrary"),
    )(dy, a, p, p, pool_w, pool_scale)


def _halo_before(tm):
    return lambda i: jnp.maximum(i * (tm // HALO) - 1, 0)


def _halo_after(tm, s):
    return lambda i: jnp.minimum((i + 1) * (tm // HALO), s // HALO - 1)


def odd_mix_fwd(p, sconv_w, dconv_w, dconv_b, cnorm_g, cnorm_b, name, tm=128, dep=None):
    s = p.shape[0]
    cw = sconv_w.shape[1]
    n = s // tm
    lanes = 128
    hb = _halo_before(tm)

    dep_args, dep_specs = _after(dep)

    def body(hc_ref, hch_ref, bc_ref, cc_ref, cch_ref, ga_ref, gah_ref, gb_ref, gbh_ref, g1_ref, g2_ref,
             sw_ref, dw_ref, db_ref, gam_ref, bet_ref, *rest):
        y_ref, dc_ref = rest[-2:]
        first = pl.program_id(0) == 0
        for l in range(cw // lanes):
            cols = slice(l * lanes, (l + 1) * lanes)
            mh = jnp.where(first, 0.0, cch_ref[:, cols].astype(F32) * hch_ref[:, cols].astype(F32))
            mm = cc_ref[:, cols].astype(F32) * hc_ref[:, cols].astype(F32)
            xx = jnp.concatenate([mh, mm], axis=0)
            tap = _Taps(xx, tm, True)
            cv = jnp.zeros((tm, lanes), F32)
            for k in range(SCONV_K):
                cv = cv + sw_ref[k:k + 1, cols] * tap(SCONV_K - 1 - k)
            c_out = bc_ref[:, cols].astype(F32) * cv
            y_ref[:, cols] = (c_out * _silu(g1_ref[:, cols].astype(F32))).astype(BF16)
            dh = jnp.where(first, 0.0, gah_ref[:, cols].astype(F32) * _sigmoid(gbh_ref[:, cols].astype(F32)))
            dm = ga_ref[:, cols].astype(F32) * _sigmoid(gb_ref[:, cols].astype(F32))
            xx = jnp.concatenate([dh, dm], axis=0)
            tap = _Taps(xx, tm, True)
            acc = jnp.zeros((tm, lanes), F32) + db_ref[:, cols]
            for k in range(CONF_K):
                acc = acc + dw_ref[k:k + 1, cols] * tap(CONF_K - 1 - k)
            dc_ref[:, cols] = acc
        rs = 64
        for r in range(tm // rs):
            rows = slice(r * rs, (r + 1) * rs)
            xv = dc_ref[rows, :]
            mu = jnp.mean(xv, axis=-1, keepdims=True)
            xc = xv - mu
            rstd = lax.rsqrt(jnp.mean(xc * xc, axis=-1, keepdims=True) + EPS)
            ln = xc * rstd * gam_ref[...] + bet_ref[...]
            y_ref[rows, cw:2 * cw] = (_silu(ln) * _silu(g2_ref[rows, :].astype(F32))).astype(BF16)

    main = lambda c: pl.BlockSpec((tm, cw), lambda i: (i, c))
    halo = lambda c: pl.BlockSpec((HALO, cw), lambda i: (hb(i), c))
    vec = lambda r: pl.BlockSpec((r, cw), lambda i: (0, 0))
    return pl.pallas_call(
        body, name=name, grid=(n,),
        in_specs=[main(0), halo(0), main(1), main(2), halo(2), main(3), halo(3), main(4), halo(4),
                  main(5), main(6), vec(SCONV_K), vec(CONF_K), vec(1), vec(1), vec(1)] + dep_specs,
        out_specs=[pl.BlockSpec((tm, 2 * cw), lambda i: (i, 0)), pl.BlockSpec((tm, cw), lambda i: (i, 0))],
        out_shape=[jax.ShapeDtypeStruct((s, 2 * cw), BF16), jax.ShapeDtypeStruct((s, cw), F32)],
        compiler_params=_params("parallel"),
    )(p, p, p, p, p, p, p, p, p, p, p, sconv_w, dconv_w, dconv_b, cnorm_g, cnorm_b, *dep_args)


def odd_bwd_ln(dy, p, dc, cnorm_g, cnorm_b, name, tm=256):
    s = p.shape[0]
    cw = dc.shape[1]
    n = s // tm
    rs = 128

    def body(dy_ref, g2_ref, dc_ref, gam_ref, bet_ref, ddc_ref, dg_ref, dgam_ref, dbet_ref, gacc, bacc):
        i = pl.program_id(0)

        @pl.when(i == 0)
        def _():
            gacc[...] = jnp.zeros_like(gacc)
            bacc[...] = jnp.zeros_like(bacc)

        def chunk(ci, carry):
            rows = pl.ds(pl.multiple_of(ci * rs, rs), rs)
            xv = dc_ref[rows, :]
            mu = jnp.mean(xv, axis=-1, keepdims=True)
            xc = xv - mu
            rstd = lax.rsqrt(jnp.mean(xc * xc, axis=-1, keepdims=True) + EPS)
            xh = xc * rstd
            gam = gam_ref[...]
            sl, dsl = _silu_and_grad(xh * gam + bet_ref[...])
            sg, dsg = _silu_and_grad(g2_ref[rows, :].astype(F32))
            dyv = dy_ref[rows, :].astype(F32)
            dg_ref[rows, :] = (dyv * sl * dsg).astype(BF16)
            dln = dyv * sg * dsl
            gacc[...] += _rowsum8(dln * xh)
            bacc[...] += _rowsum8(dln)
            dxh = dln * gam
            ddc_ref[rows, :] = rstd * (dxh - jnp.mean(dxh, axis=-1, keepdims=True)
                                       - xh * jnp.mean(dxh * xh, axis=-1, keepdims=True))
            return carry

        lax.fori_loop(0, tm // rs, chunk, 0)

        @pl.when(i == n - 1)
        def _():
            dgam_ref[...] = jnp.sum(gacc[...], axis=0, keepdims=True)
            dbet_ref[...] = jnp.sum(bacc[...], axis=0, keepdims=True)

    vec = pl.BlockSpec((1, cw), lambda i: (0, 0))
    return pl.pallas_call(
        body, name=name, grid=(n,),
        in_specs=[pl.BlockSpec((tm, cw), lambda i: (i, 1)), pl.BlockSpec((tm, cw), lambda i: (i, 6)),
                  pl.BlockSpec((tm, cw), lambda i: (i, 0)), vec, vec],
        out_specs=[pl.BlockSpec((tm, cw), lambda i: (i, 0)), pl.BlockSpec((tm, cw), lambda i: (i, 0)), vec, vec],
        out_shape=[jax.ShapeDtypeStruct((s, cw), F32), jax.ShapeDtypeStruct((s, cw), BF16),
                   jax.ShapeDtypeStruct((1, cw), F32), jax.ShapeDtypeStruct((1, cw), F32)],
        scratch_shapes=[pltpu.VMEM((8, cw), F32), pltpu.VMEM((8, cw), F32)],
        compiler_params=_params("arbitrary"),
    )(dy, p, dc, cnorm_g, cnorm_b)


def odd_bwd_conv(dy, p, ddc, dg2, sconv_w, dconv_w, name, tm=128):
    s = p.shape[0]
    cw = ddc.shape[1]
    n = s // tm
    lanes = 128
    hb = _halo_before(tm)
    ha = _halo_after(tm, s)

    def body(dy_ref, dya_ref, g1_ref, g1a_ref, bc_ref, bca_ref, hc_ref, hch_ref, cc_ref, cch_ref,
             ddc_ref, ddca_ref, ga_ref, gah_ref, gb_ref, gbh_ref, dg2_ref, sw_ref, dw_ref,
             dp_ref, dsw_ref, ddw_ref, ddb_ref, sw_acc, dw_acc, db_acc):
        i = pl.program_id(0)
        first = i == 0
        last = i == n - 1

        @pl.when(first)
        def _():
            sw_acc[...] = jnp.zeros_like(sw_acc)
            dw_acc[...] = jnp.zeros_like(dw_acc)
            db_acc[...] = jnp.zeros_like(db_acc)

        for l in range(cw // lanes):
            cols = slice(l * lanes, (l + 1) * lanes)
            mh = jnp.where(first, 0.0, cch_ref[:, cols].astype(F32) * hch_ref[:, cols].astype(F32))
            hcv = hc_ref[:, cols].astype(F32)
            ccv = cc_ref[:, cols].astype(F32)
            xx = jnp.concatenate([mh, ccv * hcv], axis=0)
            tap = _Taps(xx, tm, True)
            taps = [tap(SCONV_K - 1 - k) for k in range(SCONV_K)]
            cv = jnp.zeros((tm, lanes), F32)
            for k in range(SCONV_K):
                cv = cv + sw_ref[k:k + 1, cols] * taps[k]
            bcv = bc_ref[:, cols].astype(F32)
            dyv = dy_ref[:, cols].astype(F32)
            sg, dsg = _silu_and_grad(g1_ref[:, cols].astype(F32))
            dco = dyv * sg
            dp_ref[:, 5 * cw + l * lanes:5 * cw + (l + 1) * lanes] = (dyv * bcv * cv * dsg).astype(BF16)
            dp_ref[:, cw + l * lanes:cw + (l + 1) * lanes] = (dco * cv).astype(BF16)
            dcv = dco * bcv
            for k in range(SCONV_K):
                sw_acc[k * 8:(k + 1) * 8, cols] += _rowsum8(dcv * taps[k])
            dcv_a = jnp.where(last, 0.0, dya_ref[:, cols].astype(F32) * _silu(g1a_ref[:, cols].astype(F32))
                              * bca_ref[:, cols].astype(F32))
            xx = jnp.concatenate([dcv, dcv_a], axis=0)
            tap = _Taps(xx, tm, False)
            dm = jnp.zeros((tm, lanes), F32)
            for k in range(SCONV_K):
                dm = dm + sw_ref[k:k + 1, cols] * tap(SCONV_K - 1 - k)
            dp_ref[:, l * lanes:(l + 1) * lanes] = (dm * ccv).astype(BF16)
            dp_ref[:, 2 * cw + l * lanes:2 * cw + (l + 1) * lanes] = (dm * hcv).astype(BF16)
            gav = ga_ref[:, cols].astype(F32)
            sb = _sigmoid(gb_ref[:, cols].astype(F32))
            dh = jnp.where(first, 0.0, gah_ref[:, cols].astype(F32) * _sigmoid(gbh_ref[:, cols].astype(F32)))
            xx = jnp.concatenate([dh, gav * sb], axis=0)
            ddcv = ddc_ref[:, cols]
            db_acc[:, cols] += _rowsum8(ddcv)
            tap = _Taps(xx, tm, True)
            for k in range(CONF_K):
                dw_acc[k * 8:(k + 1) * 8, cols] += _rowsum8(ddcv * tap(CONF_K - 1 - k))
            ddc_a = jnp.where(last, 0.0, ddca_ref[:, cols])
            xx = jnp.concatenate([ddcv, ddc_a], axis=0)
            tap = _Taps(xx, tm, False)
            dgl = jnp.zeros((tm, lanes), F32)
            for k in range(CONF_K):
                dgl = dgl + dw_ref[k:k + 1, cols] * tap(CONF_K - 1 - k)
            dp_ref[:, 3 * cw + l * lanes:3 * cw + (l + 1) * lanes] = (dgl * sb).astype(BF16)
            dp_ref[:, 4 * cw + l * lanes:4 * cw + (l + 1) * lanes] = (dgl * gav * sb * (1.0 - sb)).astype(BF16)
        dp_ref[:, 6 * cw:7 * cw] = dg2_ref[...]

        @pl.when(last)
        def _():
            for k in range(SCONV_K):
                dsw_ref[k:k + 1, :] = jnp.sum(sw_acc[k * 8:(k + 1) * 8, :], axis=0, keepdims=True)
            for k in range(CONF_K):
                ddw_ref[k:k + 1, :] = jnp.sum(dw_acc[k * 8:(k + 1) * 8, :], axis=0, keepdims=True)
            ddb_ref[...] = jnp.sum(db_acc[...], axis=0, keepdims=True)

    def main(c):
        return pl.BlockSpec((tm, cw), lambda i: (i, c))

    def before(c):
        return pl.BlockSpec((HALO, cw), lambda i: (hb(i), c))

    def after(c):
        return pl.BlockSpec((HALO, cw), lambda i: (ha(i), c))

    def vec(r):
        return pl.BlockSpec((r, cw), lambda i: (0, 0))

    return pl.pallas_call(
        body, name=name, grid=(n,),
        in_specs=[main(0), after(0), main(5), after(5), main(1), after(1), main(0), before(0), main(2), before(2),
                  main(0), after(0), main(3), before(3), main(4), before(4), main(0), vec(SCONV_K), vec(CONF_K)],
        out_specs=[pl.BlockSpec((tm, 7 * cw), lambda i: (i, 0)), vec(SCONV_K), vec(CONF_K), vec(1)],
        out_shape=[jax.ShapeDtypeStruct((s, 7 * cw), BF16), jax.ShapeDtypeStruct((SCONV_K, cw), F32),
                   jax.ShapeDtypeStruct((CONF_K, cw), F32), jax.ShapeDtypeStruct((1, cw), F32)],
        scratch_shapes=[pltpu.VMEM((8 * SCONV_K, cw), F32), pltpu.VMEM((8 * CONF_K, cw), F32),
                        pltpu.VMEM((8, cw), F32)],
        compiler_params=_params("arbitrary"),
    )(dy, dy, p, p, p, p, p, p, p, p, ddc, ddc, p, p, p, p, dg2, sconv_w, dconv_w)


_ANY = pl.BlockSpec(memory_space=pl.ANY)


def _place():
    return lax.axis_index("x"), lax.axis_index("y"), lax.axis_index("c")


def all_gather(arrs, name, deps=()):
    n = len(arrs)

    def body(*refs):
        ins, outs = refs[:n], refs[n + len(deps):2 * n + len(deps)]
        send_sems, recv_sems, local_sems = refs[-3:]
        x, y, c = _place()
        me, sibling = (x, y, c), (x, y, 1 - c)
        chips = [(1 - x, y), (x, 1 - y), (1 - x, 1 - y)]

        def copy(a, k, block, to, src=None):
            px, py, pc = block
            dst = outs[a].at[4 * px + 2 * py + pc]
            return pltpu.make_async_remote_copy(
                src_ref=dst if src is None else src, dst_ref=dst,
                send_sem=send_sems.at[7 * a + k], recv_sem=recv_sems.at[7 * a + k],
                device_id=to, device_id_type=MESH)

        mine = [pltpu.make_async_copy(ins[a], outs[a].at[4 * x + 2 * y + c], local_sems.at[a]) for a in range(n)]
        first = []
        for a in range(n):
            first.append(copy(a, 0, me, sibling, src=ins[a]))
            first += [copy(a, 1 + j, me, (*chip, c), src=ins[a]) for j, chip in enumerate(chips)]
        for cp in first + mine:
            cp.start()
        passed = []
        for a in range(n):
            for j, chip in enumerate(chips):
                copy(a, 1 + j, (*chip, c), me).wait_recv()
                cp = copy(a, 4 + j, (*chip, c), sibling)
                cp.start()
                passed.append(cp)
        for a in range(n):
            copy(a, 0, sibling, me).wait_recv()
            for j, chip in enumerate(chips):
                copy(a, 4 + j, (*chip, 1 - c), me).wait_recv()
        for cp in first + passed:
            cp.wait_send()
        for cp in mine:
            cp.wait()

    return pl.pallas_call(
        body, name=name,
        out_shape=[jax.ShapeDtypeStruct((N_DEV,) + a.shape, a.dtype) for a in arrs],
        in_specs=[_ANY] * (n + len(deps)), out_specs=[_ANY] * n,
        scratch_shapes=[pltpu.SemaphoreType.DMA((7 * n,)), pltpu.SemaphoreType.DMA((7 * n,)),
                        pltpu.SemaphoreType.DMA((n,))],
    )(*arrs, *deps)


def in_proj_gathered(xs, g, w_own, extras, name, tm=1024):
    s, d = xs.shape
    n = w_own.shape[1]
    tm = min(tm, s)
    arrs = [w_own] + list(extras)
    na = len(arrs)
    tr = 256

    def body(*refs):
        x_ref, g_ref, ins = refs[0], refs[1], refs[2:2 + na]
        h_out, p_ref, outs = refs[2 + na], refs[3 + na], refs[4 + na:4 + 2 * na]
        (h_ref, xbuf, wbuf, obuf, send_sems, recv_sems, load_sem, store_sems, own_sems, h_sem,
         x_sems) = refs[4 + 2 * na:]
        x, y, c = _place()
        me, sibling = (x, y, c), (x, y, 1 - c)
        x_first = c == 0
        near = (jnp.where(x_first, 1 - x, x), jnp.where(x_first, y, 1 - y))
        far = (jnp.where(x_first, x, 1 - x), jnp.where(x_first, 1 - y, y))
        diag = (1 - x, 1 - y)
        k_near, k_far = jnp.where(x_first, 1, 2), jnp.where(x_first, 2, 1)
        f_near, f_far = k_near + 3, k_far + 3

        def slot(block):
            return 4 * block[0] + 2 * block[1] + block[2]

        def copy(a, k, block, to, src=None):
            dst = outs[a].at[slot(block)]
            return pltpu.make_async_remote_copy(
                src_ref=dst if src is None else src, dst_ref=dst,
                send_sem=send_sems.at[7 * a + k], recv_sem=recv_sems.at[7 * a + k],
                device_id=to, device_id_type=MESH)

        first = []
        for a in range(na):
            first += [copy(a, 0, me, sibling, src=ins[a]), copy(a, 1, me, (1 - x, y, c), src=ins[a]),
                      copy(a, 2, me, (x, 1 - y, c), src=ins[a])]
        for cp in first:
            cp.start()
        own = pltpu.make_async_copy(wbuf.at[0], outs[0].at[slot(me)], own_sems.at[0])
        mine = [pltpu.make_async_copy(ins[a], outs[a].at[slot(me)], own_sems.at[a]) for a in range(1, na)]
        stores = [None, None]

        def x_load(i):
            return pltpu.make_async_copy(x_ref.at[pl.ds(i * tr, tr), :], xbuf.at[i % 2], x_sems.at[i % 2])

        x_load(0).start()
        for i in range(s // tr):
            if i + 1 < s // tr:
                x_load(i + 1).start()
            x_load(i).wait()
            xv = xbuf[i % 2]
            r = lax.rsqrt(jnp.mean(xv * xv, axis=-1, keepdims=True) + EPS)
            h_ref[i * tr:(i + 1) * tr, :] = (xv * r * g_ref[...]).astype(BF16)
        h_store = pltpu.make_async_copy(h_ref, h_out, h_sem)
        h_store.start()

        def multiply(k, block, w_from):
            b = k % 2
            if k == 2:
                own.wait()
            load = pltpu.make_async_copy(w_from, wbuf.at[b], load_sem)
            load.start()
            if stores[b] is not None:
                stores[b].wait()
            load.wait()
            if k == 0:
                own.start()

            def chunk(i, carry):
                rows = pl.ds(pl.multiple_of(i * tm, tm), tm)
                obuf[b, rows, :] = jnp.dot(h_ref[rows, :], wbuf[b], preferred_element_type=F32).astype(BF16)
                return carry

            lax.fori_loop(0, s // tm, chunk, 0)
            stores[b] = pltpu.make_async_copy(
                obuf.at[b], p_ref.at[:, pl.ds(pl.multiple_of(slot(block) * n, 128), n)], store_sems.at[b])
            stores[b].start()

        passed = []

        def arrive(a, k, block):
            copy(a, k, block, me).wait_recv()

        def pass_on(a, k, block, to):
            cp = copy(a, k, block, to)
            cp.start()
            passed.append(cp)

        def gather(arrays, use):
            def arrive_all(k, block):
                for a in arrays:
                    arrive(a, k, block)

            def pass_all(k, block, to):
                for a in arrays:
                    pass_on(a, k, block, to)

            use(0, me)
            arrive_all(0, sibling)
            use(1, sibling)
            arrive_all(k_near, (*near, c))
            pass_all(3, (*near, c), (*far, c))
            pass_all(f_near, (*near, c), sibling)
            use(2, (*near, c))
            arrive_all(f_far, (*far, 1 - c))
            use(3, (*far, 1 - c))
            arrive_all(k_far, (*far, c))
            pass_all(f_far, (*far, c), sibling)
            use(4, (*far, c))
            arrive_all(f_near, (*near, 1 - c))
            use(5, (*near, 1 - c))
            arrive_all(3, (*diag, c))
            pass_all(6, (*diag, c), sibling)
            use(6, (*diag, c))
            arrive_all(6, (*diag, 1 - c))
            use(7, (*diag, 1 - c))

        gather(range(na), lambda k, block: multiply(k, block, ins[0] if k == 0 else outs[0].at[slot(block)]))
        for cp in mine:
            cp.start()
        for cp in first + passed:
            cp.wait_send()
        for cp in mine + stores + [h_store]:
            cp.wait()

    vmem = pl.BlockSpec(memory_space=pltpu.VMEM)
    outs = pl.pallas_call(
        body, name=name,
        out_shape=[jax.ShapeDtypeStruct((s, d), BF16), jax.ShapeDtypeStruct((s, N_DEV * n), BF16)]
        + [jax.ShapeDtypeStruct((N_DEV,) + a.shape, a.dtype) for a in arrs],
        in_specs=[_ANY, vmem] + [_ANY] * na, out_specs=[_ANY] * (2 + na),
        scratch_shapes=[pltpu.VMEM((s, d), BF16), pltpu.VMEM((2, tr, d), F32), pltpu.VMEM((2, d, n), BF16),
                        pltpu.VMEM((2, s, n), BF16),
                        pltpu.SemaphoreType.DMA((7 * na,)), pltpu.SemaphoreType.DMA((7 * na,)),
                        pltpu.SemaphoreType.DMA, pltpu.SemaphoreType.DMA((2,)), pltpu.SemaphoreType.DMA((na,)),
                        pltpu.SemaphoreType.DMA, pltpu.SemaphoreType.DMA((2,))],
        compiler_params=pltpu.CompilerParams(vmem_limit_bytes=VMEM_LIMIT),
    )(xs, g, *arrs)
    return outs[0], outs[1], outs[2], outs[3:]


_HBM = pl.BlockSpec(memory_space=pltpu.HBM)
_SEM = pl.BlockSpec(memory_space=pltpu.SEMAPHORE)
_DATAFLOW = pltpu.SideEffectType.DATAFLOW_SIDE_EFFECTING


def _peers_per_array(kind):
    return 1 if kind in ("sibling", "halves") else 3


def _split_copies(kind, srcs, lands, send_sems, recv_sems):
    x, y, c = _place()
    per = _peers_per_array(kind)
    out = []
    for a in range(len(lands)):
        if kind == "sibling":
            part = srcs[a] if srcs[a].shape[1] == 1 else srcs[a].at[:, pl.ds(1 - c, 1)]
            peers = [((x, y, 1 - c), part, lands[a], lands[a])]
        elif kind == "halves":
            mine, its = lands[a].at[:, pl.ds(c, 1)], lands[a].at[:, pl.ds(1 - c, 1)]
            peers = [((x, y, 1 - c), mine, mine, its)]
        else:
            peers = []
            for px, py in [(1 - x, y), (x, 1 - y), (1 - x, 1 - y)]:
                if kind == "gather":
                    views = (srcs[a], lands[a].at[4 * x + 2 * y + c], lands[a].at[4 * px + 2 * py + c])
                else:
                    views = (srcs[a].at[2 * px + py], lands[a].at[2 * x + y], lands[a].at[2 * px + py])
                peers.append(((px, py, c),) + views)
        for j, (peer, src, dst, arrives) in enumerate(peers):
            sems = dict(send_sem=send_sems.at[per * a + j], recv_sem=recv_sems.at[per * a + j],
                        device_id=peer, device_id_type=MESH)
            out.append((pltpu.make_async_remote_copy(src_ref=src, dst_ref=dst, **sems),
                        pltpu.make_async_remote_copy(src_ref=src, dst_ref=arrives, **sems)))
    return out


def split_start(kind, srcs, lands, deps, name):
    ns, nl = len(srcs), len(lands)
    n_sems = _peers_per_array(kind) * nl
    held = list(srcs) + list(lands)

    def body(*refs):
        send_sems, recv_sems = refs[len(held) + len(deps)], refs[len(held) + len(deps) + 1]
        for copy, _ in _split_copies(kind, refs[:ns], refs[ns:ns + nl], send_sems, recv_sems):
            copy.start()
        token = refs[-1]
        token[...] = jnp.zeros_like(token)

    outs = pl.pallas_call(
        body, name=name,
        out_shape=(pltpu.SemaphoreType.DMA((n_sems,)), pltpu.SemaphoreType.DMA((n_sems,)),
                   *[pltpu.HBM(a.shape, a.dtype) for a in held], jax.ShapeDtypeStruct((8, 128), F32)),
        in_specs=[_HBM] * len(held) + [_ANY] * len(deps),
        out_specs=(_SEM, _SEM, *([_HBM] * len(held)), pl.BlockSpec(memory_space=pltpu.VMEM)),
        input_output_aliases={i: 2 + i for i in range(len(held))},
        compiler_params=pltpu.CompilerParams(has_side_effects=_DATAFLOW),
    )(*[pltpu.with_memory_space_constraint(a, pltpu.HBM) for a in held], *deps)
    return outs[0], outs[1], list(outs[2:2 + ns]), list(outs[2 + ns:2 + ns + nl]), outs[-1]


def split_wait(kind, send_sems, recv_sems, srcs, lands, afters, name):
    ns, nl = len(srcs), len(lands)
    held = list(srcs) + list(lands)

    def body(*refs):
        for _, arrival in _split_copies(kind, refs[:ns], refs[ns:ns + nl], refs[ns + nl], refs[ns + nl + 1]):
            arrival.wait_send()
            arrival.wait_recv()

    outs = pl.pallas_call(
        body, name=name,
        out_shape=[pltpu.HBM(a.shape, a.dtype) for a in held],
        in_specs=[_HBM] * len(held) + [_SEM, _SEM] + [_ANY] * len(afters),
        out_specs=[_HBM] * len(held),
        input_output_aliases={i: i for i in range(len(held))},
        compiler_params=pltpu.CompilerParams(has_side_effects=_DATAFLOW),
    )(*held, send_sems, recv_sems, *afters)
    return list(outs[:ns]), list(outs[ns:])


def place_block(land, block, dev, name):
    r, c = block.shape
    tr = min(r, 512)

    def body(dev_ref, land_ref, b_ref, o_ref):
        del dev_ref, land_ref
        o_ref[...] = b_ref[...]

    return pl.pallas_call(
        body, name=name,
        grid_spec=pltpu.PrefetchScalarGridSpec(
            num_scalar_prefetch=1, grid=(r // tr,),
            in_specs=[_ANY, pl.BlockSpec((tr, c), lambda i, dev_ref: (i, 0))],
            out_specs=pl.BlockSpec((None, tr, c), lambda i, dev_ref: (dev_ref[0], i, 0))),
        out_shape=jax.ShapeDtypeStruct(land.shape, land.dtype),
        input_output_aliases={1: 0},
        compiler_params=_params("parallel"),
    )(dev, land, block)


def pair_add(own, recv, core, name):
    _, _, r, c = own.shape
    tr = min(r, 2048)

    def body(core_ref, own_ref, recv_ref, o_ref):
        del core_ref
        o_ref[...] = (own_ref[...].astype(F32) + recv_ref[...].astype(F32)).astype(BF16)

    return pl.pallas_call(
        body, name=name,
        grid_spec=pltpu.PrefetchScalarGridSpec(
            num_scalar_prefetch=1, grid=(4, r // tr),
            in_specs=[pl.BlockSpec((None, None, tr, c), lambda k, i, core_ref: (k, core_ref[0], i, 0)),
                      pl.BlockSpec((None, None, tr, c), lambda k, i, core_ref: (k, 0, i, 0))],
            out_specs=pl.BlockSpec((None, tr, c), lambda k, i, core_ref: (k, i, 0))),
        out_shape=jax.ShapeDtypeStruct((4, r, c), BF16),
        compiler_params=_params("parallel", "parallel"),
    )(core, own, recv)


def _adamw_math(w, g, m, v):
    m2 = ADAM_B1 * m + (1.0 - ADAM_B1) * g
    v2 = ADAM_B2 * v + (1.0 - ADAM_B2) * (g * g)
    m_hat = m2 / (1.0 - ADAM_B1 ** ADAM_STEP)
    v_hat = v2 / (1.0 - ADAM_B2 ** ADAM_STEP)
    delta = -ADAM_LR * (m_hat / (jnp.sqrt(v_hat) + ADAM_EPS) + ADAM_WD * w)
    return delta, m2, v2


def adamw_big(w, m, v, own, got, chip, name):
    r, c = w.shape
    tr = min(r, 512)

    def body(chip_ref, w_ref, m_ref, v_ref, p0, p1, p2, p3, g_ref, d_ref, m2_ref, v2_ref):
        del chip_ref
        g = ((p0[...].astype(F32) + p1[...].astype(F32)) + p2[...].astype(F32)) + p3[...].astype(F32)
        delta, m2, v2 = _adamw_math(w_ref[...], g, m_ref[...], v_ref[...])
        g_ref[...] = g
        d_ref[...] = delta
        m2_ref[...] = m2
        v2_ref[...] = v2

    row = pl.BlockSpec((tr, c), lambda i, chip_ref: (i, 0))

    def slab(flip):
        return pl.BlockSpec((None, tr, c), lambda i, chip_ref: (chip_ref[0] ^ flip, i, 0))

    return pl.pallas_call(
        body, name=name,
        grid_spec=pltpu.PrefetchScalarGridSpec(
            num_scalar_prefetch=1, grid=(r // tr,),
            in_specs=[row, row, row, slab(0), slab(1), slab(2), slab(3)],
            out_specs=[row] * 4),
        out_shape=[jax.ShapeDtypeStruct((r, c), F32)] * 4,
        compiler_params=_params("parallel"),
    )(chip, w, m, v, own, got, got, got)


def sum_devices(g8, name):
    def body(g_ref, o_ref):
        tot = g_ref[0]
        for k in range(1, N_DEV):
            tot = tot + g_ref[k]
        o_ref[...] = tot

    return pl.pallas_call(body, name=name, out_shape=jax.ShapeDtypeStruct(g8.shape[1:], F32))(g8)


def adamw_small(ws, gs, ms, vs, name):
    n = len(ws)

    def body(*refs):
        w_r, g_r, m_r, v_r = refs[:n], refs[n:2 * n], refs[2 * n:3 * n], refs[3 * n:4 * n]
        d_o, m_o, v_o = refs[4 * n:5 * n], refs[5 * n:6 * n], refs[6 * n:7 * n]
        for k in range(n):
            delta, m2, v2 = _adamw_math(w_r[k][...], g_r[k][...], m_r[k][...], v_r[k][...])
            d_o[k][...] = delta
            m_o[k][...] = m2
            v_o[k][...] = v2

    shapes = [jax.ShapeDtypeStruct(w.shape, F32) for w in ws]
    outs = pl.pallas_call(body, name=name, out_shape=shapes * 3)(*ws, *gs, *ms, *vs)
    return outs[:n], outs[n:2 * n], outs[2 * n:]


def _rows128(a):
    return a.reshape(-1, 128)


def _pad_rows(a, rows):
    return jnp.pad(a, ((0, rows - a.shape[0]), (0, 0)))


def kernel(x, ln_pre_even, w_in_even, pool_w, pool_scale, w_out_even, ln_post_even, ln_pre_odd, w_in_odd, sconv_w, dconv_w, dconv_b, cnorm_g, cnorm_b, w_out_odd, ln_post_odd, loss_target, m_ln_pre_even, m_w_in_even, m_pool_w, m_pool_scale, m_w_out_even, m_ln_post_even, m_ln_pre_odd, m_w_in_odd, m_sconv_w, m_dconv_w, m_dconv_b, m_cnorm_g, m_cnorm_b, m_w_out_odd, m_ln_post_odd, v_ln_pre_even, v_w_in_even, v_pool_w, v_pool_scale, v_w_out_even, v_ln_post_even, v_ln_pre_odd, v_w_in_odd, v_sconv_w, v_dconv_w, v_dconv_b, v_cnorm_g, v_cnorm_b, v_w_out_odd, v_ln_post_odd):
    xs = x[0]
    tgt = loss_target[0]
    s, d = xs.shape
    half = d // 2
    n_heads = half // HEAD_DIM
    ng = len(POOL_WINDOWS)
    cwp = half // ng
    dev = 4 * lax.axis_index("x") + 2 * lax.axis_index("y") + lax.axis_index("c")
    core = lax.axis_index("c").astype(jnp.int32).reshape(1)

    pr = pool_w.shape[2]
    cl = sconv_w.shape[2]
    small_parts = [(_rows128(ln_pre_odd), 8), (sconv_w[0], 8), (dconv_w[0], 32), (dconv_b, 8),
                   (cnorm_g, 8), (cnorm_b, 8), (_rows128(ln_post_odd), 8)]
    small_local = jnp.concatenate([_pad_rows(a, r) for a, r in small_parts], axis=0)
    h0, p0, g_wie, (g_pw, g_small) = in_proj_gathered(
        xs, ln_pre_even, w_in_even[0].astype(BF16), [pool_w[0].reshape(ng * pr, cwp).astype(BF16), small_local],
        "ag_in_proj_even")
    comm = _Exchanges(dev, core, d)
    token = comm.start_weights("out_even", [w_out_even[0].astype(BF16)], [p0])
    token = comm.start_weights("in_odd", [w_in_odd[0].astype(BF16)], [token])
    sb_dep = comm.start_weights("out_odd", [w_out_odd[0].astype(BF16)], [token])
    pool_full = g_pw.reshape(N_DEV, ng, pr, cwp).transpose(1, 0, 2, 3).reshape(ng, cwp, cwp)
    nl = ln_pre_odd.shape[1] // 128

    def chan(lo, rows):
        return g_small[:, lo:lo + rows].transpose(1, 0, 2).reshape(rows, N_DEV * cl)

    ln_pre_odd_f = g_small[:, 0:nl].reshape(1, d)
    sconv_f = chan(8, SCONV_K)
    dconv_f = chan(16, CONF_K)
    dconv_b_f = chan(48, 1)
    cnorm_g_f = chan(56, 1)
    cnorm_b_f = chan(64, 1)
    ln_post_odd_f = g_small[:, 72:72 + nl].reshape(1, d)

    loss_blk, grad_x, small_g = _fwd_bwd(
        xs, tgt, ln_pre_even, h0, p0, g_wie, pool_full, pool_scale, ln_post_even, ln_pre_odd_f,
        sconv_f, dconv_f, dconv_b_f, cnorm_g_f, cnorm_b_f, ln_post_odd_f, comm, sb_dep)
    small_w = [ln_pre_even, pool_scale, ln_post_even, ln_pre_odd, sconv_w[0], dconv_w[0], dconv_b, cnorm_g, cnorm_b, ln_post_odd]
    small_m = [m_ln_pre_even, m_pool_scale, m_ln_post_even, m_ln_pre_odd, m_sconv_w[0], m_dconv_w[0], m_dconv_b, m_cnorm_g, m_cnorm_b, m_ln_post_odd]
    small_v = [v_ln_pre_even, v_pool_scale, v_ln_post_even, v_ln_pre_odd, v_sconv_w[0], v_dconv_w[0], v_dconv_b, v_cnorm_g, v_cnorm_b, v_ln_post_odd]
    big = {"w_in_even": (w_in_even, m_w_in_even, v_w_in_even), "pool_w": (pool_w, m_pool_w, v_pool_w),
           "w_out_even": (w_out_even, m_w_out_even, v_w_out_even), "w_in_odd": (w_in_odd, m_w_in_odd, v_w_in_odd),
           "w_out_odd": (w_out_odd, m_w_out_odd, v_w_out_odd)}
    upd = comm.finish_updates(big, [grad_x])
    upd.update(comm.finish_updates(big, [grad_x]))
    sg, sd, sm, sv, loss = _update_small(small_g, loss_blk, small_w, small_m, small_v, dev, d, cl,
                                         deps=[upd["w_in_odd"][1], upd["w_out_even"][1]])
    upd.update(comm.finish_updates(big, sd))
    (g_wie_o, d_wie, m_wie, v_wie), (g_pw_o, d_pw, m_pw, v_pw) = upd["w_in_even"], upd["pool_w"]
    (g_woe_o, d_woe, m_woe, v_woe), (g_wio_o, d_wio, m_wio, v_wio) = upd["w_out_even"], upd["w_in_odd"]
    g_woo_o, d_woo, m_woo, v_woo = upd["w_out_odd"]

    def order(small, wie, pw, woe, wio, woo):
        return [small[0], wie, pw, small[1], woe, small[2], small[3], wio, small[4], small[5], small[6],
                small[7], small[8], woo, small[9]]

    grads = order(sg, g_wie_o, g_pw_o, g_woe_o, g_wio_o, g_woo_o)
    deltas = order(sd, d_wie, d_pw, d_woe, d_wio, d_woo)
    new_m = order(sm, m_wie, m_pw, m_woe, m_wio, m_woo)
    new_v = order(sv, v_wie, v_pw, v_woe, v_wio, v_woo)
    return (loss, grad_x[None], *grads, *deltas, *new_m, *new_v)


def _fwd_bwd(xs, tgt, ln_pre_even, h0, p0, g_wie, pool_full, pool_scale, ln_post_even, ln_pre_odd_f,
             sconv_f, dconv_f, dconv_b_f, cnorm_g_f, cnorm_b_f, ln_post_odd_f, comm, sb_dep):
    d = xs.shape[1]
    n_heads = d // 2 // HEAD_DIM
    ng, cwp = pool_full.shape[0], pool_full.shape[1]
    a0, sb_wts = sb_fwd(p0, n_heads, "sb_fwd", dep=sb_dep)
    dep = comm.weights_arrived("out_even", after=a0)
    y0 = even_mix_fwd(a0, p0, pool_full, pool_scale, "even_mix_fwd", dep=dep)
    (w_out_e,) = comm.weights("out_even", after=y0)
    w_out_e = w_out_e.reshape(1, d, d)
    o0 = mm_nn(y0, w_out_e, BF16, "out_proj_even", tn=512)
    dep = comm.weights_arrived("in_odd", after=o0)
    x1, h1 = postnorm_fwd(xs, o0, ln_post_even, ln_pre_odd_f, "post_even", dep=dep)
    (g_wio,) = comm.weights("in_odd", after=x1)
    p1 = mm_nn(h1, g_wio, BF16, "in_proj_odd", group=2)
    dep = comm.weights_arrived("out_odd", after=p1)
    y1, dc = odd_mix_fwd(p1, sconv_f, dconv_f, dconv_b_f, cnorm_g_f, cnorm_b_f, "odd_mix_fwd", dep=dep)
    (w_out_o,) = comm.weights("out_odd", after=y1)
    w_out_o = w_out_o.reshape(1, d, d)
    o1 = mm_nn(y1, w_out_o, BF16, "out_proj_odd", tn=512)
    loss_blk, gx2, do1, dg_post_odd = final_fwd_bwd(x1, o1, ln_post_odd_f, tgt, "post_odd_loss")

    dw_out_o = mm_tn(y1, do1, 1, BF16, "dw_out_odd")
    dy1 = mm_nt(do1, w_out_o, BF16, "dy_odd")
    ddc, dg2, dgam, dbet = odd_bwd_ln(dy1, p1, dc, cnorm_g_f, cnorm_b_f, "odd_bwd_ln")
    dp1, dsconv, ddconv, ddconv_b = odd_bwd_conv(dy1, p1, ddc, dg2, sconv_f, dconv_f, "odd_bwd_conv")
    dw_in_o = mm_tn(h1, dp1, N_DEV, BF16, "dw_in_odd", group=2)
    dep = comm.reduce_begin({"w_out_odd": dw_out_o.reshape(N_DEV, d // N_DEV, d), "w_in_odd": dw_in_o}, "odd")
    dh1 = mm_nt(dp1, g_wio, BF16, "dh_odd", dep=dep, group=2)
    dep = comm.reduce_send(after=dh1)
    gx1, dg_pre_odd, do0, dg_post_even = norm_bwd(dh1, x1, ln_pre_odd_f, gx2, "pre_odd_post_even_bwd",
                                                  inp2=o0, g2=ln_post_even, dep=dep)

    dw_out_e = mm_tn(y0, do0, 1, BF16, "dw_out_even")
    dy0 = mm_nt(do0, w_out_e, BF16, "dy_even")
    da0, du0, dg0, dpool, dpool_scale = even_mix_bwd(dy0, a0, p0, pool_full, pool_scale, "even_mix_bwd")
    pr = cwp // N_DEV
    dpool_slabs = dpool.astype(BF16).reshape(ng, N_DEV, pr, cwp).transpose(1, 0, 2, 3).reshape(N_DEV, ng * pr, cwp)
    dep = comm.reduce_begin({"w_out_even": dw_out_e.reshape(N_DEV, d // N_DEV, d), "pool_w": dpool_slabs}, "even_out")
    dq0, dk0, dv0 = sb_bwd(p0, a0, sb_wts, da0, n_heads, "sb_bwd", dep=dep)
    dep = comm.reduce_send(after=dq0)
    dp0 = jnp.concatenate([dq0, dk0, dv0, du0, dg0], axis=1)
    dw_sibling = mm_tn(h0, dp0, N_DEV // 2, BF16, "dw_in_even_sibling", dep=dep, pick=(2, 1 - comm.core))
    dep = comm.reduce_begin({"w_in_even": dw_sibling}, "even_in", sibling_part=True)
    dw_own = mm_tn(h0, dp0, N_DEV // 2, BF16, "dw_in_even_own", dep=dep, pick=(2, comm.core))
    dep = comm.reduce_send(after=dw_own, own_part={"w_in_even": dw_own})
    dh0 = mm_nt(dp0, g_wie, BF16, "dh_even", dep=dep, group=2)
    dep = None
    grad_x, dg_pre_even = norm_bwd(dh0, xs, ln_pre_even, gx1, "pre_even_bwd", tm=512, dep=dep)
    small_g = [dg_pre_even, dpool_scale, dg_post_even, dg_pre_odd, dsconv, ddconv, ddconv_b, dgam, dbet, dg_post_odd]
    return loss_blk, grad_x, small_g


class _Exchanges:
    def __init__(self, dev, core, d):
        self.dev = dev.astype(jnp.int32).reshape(1)
        self.core = core
        self.chip = (dev // 2).astype(jnp.int32).reshape(1)
        self.d = d
        self.in_flight = {}
        self.to_sibling = None
        self.pending = []

    def start_weights(self, tag, blocks, afters):
        lands = [lax.empty((N_DEV,) + b.shape, b.dtype) for b in blocks]
        send, recv, srcs, lands, token = split_start("gather", blocks, lands, afters, "ag_start_" + tag)
        self.in_flight[tag] = (send, recv, srcs, lands)
        return token

    def weights_arrived(self, tag, after):
        send, recv, srcs, lands = self.in_flight.pop(tag)
        srcs, lands = split_wait("gather", send, recv, srcs, lands, [after], "ag_wait_" + tag)
        lands = [place_block(l, b, self.dev, "ag_own_%s_%d" % (tag, k)) for k, (l, b) in enumerate(zip(lands, srcs))]
        lands = [l.reshape((4, 2) + l.shape[1:]) for l in lands]
        send, recv, _, lands, token = split_start("halves", [], lands, [], "ag_sibling_start_" + tag)
        self.in_flight[tag] = (send, recv, lands)
        return token

    def weights(self, tag, after):
        send, recv, lands = self.in_flight.pop(tag)
        _, lands = split_wait("halves", send, recv, [], lands, [after], "ag_sibling_wait_" + tag)
        return [l.reshape((N_DEV,) + l.shape[2:]) for l in lands]

    def reduce_begin(self, partials, tag, sibling_part=False):
        names = list(partials)
        arrs = [partials[k].reshape((4, 1 if sibling_part else 2) + partials[k].shape[1:]) for k in names]
        lands = [lax.empty((4, 1) + a.shape[2:], a.dtype) for a in arrs]
        send, recv, srcs, lands, token = split_start("sibling", arrs, lands, [], "rs_sibling_start_" + tag)
        self.to_sibling = (tag, names, send, recv, srcs, lands)
        return token

    def reduce_send(self, after, own_part=None):
        tag, names, send, recv, srcs, lands = self.to_sibling
        srcs, lands = split_wait("sibling", send, recv, srcs, lands, [after], "rs_sibling_wait_" + tag)
        which = self.core
        if own_part is not None:
            srcs = [own_part[k].reshape((4, 1) + own_part[k].shape[1:]) for k in names]
            which = jnp.zeros((1,), jnp.int32)
        sums = [pair_add(o, r, which, "rs_pair_add_" + k) for k, o, r in zip(names, srcs, lands)]
        zones = [lax.empty(a.shape, a.dtype) for a in sums]
        send, recv, srcs, zones, token = split_start("scatter", sums, zones, [], "rs_start_" + tag)
        self.pending.append((tag, names, send, recv, srcs, zones))
        return token

    def finish_updates(self, big, afters):
        tag, names, send, recv, srcs, lands = self.pending.pop(0)
        srcs, lands = split_wait("scatter", send, recv, srcs, lands, afters, "rs_wait_" + tag)
        out = {}
        for name, own, got in zip(names, srcs, lands):
            w, m, v = big[name]
            shp = own.shape[1:]
            outs = adamw_big(w.reshape(shp), m.reshape(shp), v.reshape(shp), own, got, self.chip, "adamw_" + name)
            out[name] = [o.reshape(w.shape) for o in outs]
        return out


def _update_small(small_g, loss_blk, small_w, small_m, small_v, dev, d, cl, deps):
    packed = jnp.concatenate([_rows128(g) for g in small_g] + [loss_blk], axis=0)
    (g8,) = all_gather([packed], "ag_small_grads", deps)
    tot = sum_devices(g8, "sum_small_grads")
    loss = tot[packed.shape[0] - 8, 0]
    full_g = []
    lo = 0
    for g in small_g:
        rows = g.size // 128
        full_g.append(tot[lo:lo + rows].reshape(g.shape))
        lo += rows

    def mine(g, width):
        return lax.dynamic_slice_in_dim(g, dev * width, width, axis=g.ndim - 1)

    fg = full_g
    small_gl = [fg[0], fg[1], fg[2], mine(fg[3], d // N_DEV), mine(fg[4], cl), mine(fg[5], cl), mine(fg[6], cl),
                mine(fg[7], cl), mine(fg[8], cl), mine(fg[9], d // N_DEV)]
    sd, sm, sv = adamw_small(small_w, small_gl, small_m, small_v, "adamw_small")

    def like(k, a):
        return a[None] if k in (4, 5) else a

    sg = [like(k, a) for k, a in enumerate(small_gl)]
    sd = [like(k, a) for k, a in enumerate(sd)]
    sm = [like(k, a) for k, a in enumerate(sm)]
    sv = [like(k, a) for k, a in enumerate(sv)]
    return sg, sd, sm, sv, loss
```

```python
import functools
import math

import jax
import jax.numpy as jnp
from jax import lax
from jax.experimental import pallas as pl
from jax.experimental.pallas import tpu as pltpu

F32 = jnp.float32
BF16 = jnp.bfloat16
EPS = 1e-6
HEAD_DIM = 128
POOL_WINDOWS = (2, 4, 8, 16)
SCONV_K = 3
CONF_K = 31
HALO = 32
N_DEV = 8
VMEM_LIMIT = 56 * 1024 * 1024
MESH = pl.DeviceIdType.MESH

ADAM_LR = 0.001
ADAM_B1 = 0.9
ADAM_B2 = 0.999
ADAM_EPS = 1e-08
ADAM_WD = 0.01
ADAM_STEP = 10


def _params(*sem):
    return pltpu.CompilerParams(dimension_semantics=sem, vmem_limit_bytes=VMEM_LIMIT)


def _sigmoid(v):
    return 1.0 / (1.0 + jnp.exp(-v))


def _silu(v):
    return v * _sigmoid(v)


def _silu_and_grad(v):
    s = _sigmoid(v)
    return v * s, s * (1.0 + v * (1.0 - s))


def _rowsum8(v):
    r, c = v.shape
    return jnp.sum(v.reshape(r // 8, 8, c), axis=0)


SUBLANES = 8


class _Taps:
    def __init__(self, xx, rows, before):
        self.xx, self.rows, self.before, self.rotated = xx, rows, before, {}

    def __call__(self, i):
        r, q = i % SUBLANES, i // SUBLANES
        if r not in self.rotated:
            n = self.xx.shape[0]
            self.rotated[r] = self.xx if r == 0 else pltpu.roll(self.xx, r if self.before else n - r, 0)
        lo = HALO - SUBLANES * q if self.before else SUBLANES * q
        return self.rotated[r][lo:lo + self.rows]


def _window_sum(xx, win, before):
    n = xx.shape[0]
    acc = xx
    k = 1
    while k < win:
        acc = acc + pltpu.roll(acc, k if before else n - k, 0)
        k *= 2
    return acc


def postnorm_fwd(x, o, g, g_next, name, tm=512, dep=None):
    s, d = x.shape
    dep_args, dep_specs = _after(dep)

    def body(x_ref, o_ref, g_ref, gn_ref, *rest):
        y_ref, h_ref = rest[-2:]
        ov = o_ref[...].astype(F32)
        r = lax.rsqrt(jnp.mean(ov * ov, axis=-1, keepdims=True) + EPS)
        y = x_ref[...] + ov * r * g_ref[...]
        y_ref[...] = y
        r2 = lax.rsqrt(jnp.mean(y * y, axis=-1, keepdims=True) + EPS)
        h_ref[...] = (y * r2 * gn_ref[...]).astype(BF16)

    row = pl.BlockSpec((tm, d), lambda i: (i, 0))
    vec = pl.BlockSpec((1, d), lambda i: (0, 0))
    return pl.pallas_call(
        body, name=name, grid=(s // tm,),
        in_specs=[row, row, vec, vec] + dep_specs, out_specs=[row, row],
        out_shape=[jax.ShapeDtypeStruct((s, d), F32), jax.ShapeDtypeStruct((s, d), BF16)],
        compiler_params=_params("parallel"),
    )(x, o, g, g_next, *dep_args)


def final_fwd_bwd(x1, o, g, target, name, tm=512):
    s, d = x1.shape
    n = s // tm

    def body(x_ref, o_ref, g_ref, t_ref, loss_ref, gx_ref, do_ref, dg_ref, lacc, gacc):
        i = pl.program_id(0)

        @pl.when(i == 0)
        def _():
            lacc[...] = jnp.zeros_like(lacc)
            gacc[...] = jnp.zeros_like(gacc)

        ov = o_ref[...].astype(F32)
        gv = g_ref[...]
        r = lax.rsqrt(jnp.mean(ov * ov, axis=-1, keepdims=True) + EPS)
        oh = ov * r
        diff = x_ref[...] + oh * gv - t_ref[...]
        lacc[...] += _rowsum8(diff * diff)
        gx = diff * (1.0 / d)
        gx_ref[...] = gx
        gacc[...] += _rowsum8(gx * oh)
        dn = gx * gv
        do_ref[...] = (r * (dn - oh * jnp.mean(dn * oh, axis=-1, keepdims=True))).astype(BF16)

        @pl.when(i == n - 1)
        def _():
            tot = jnp.sum(jnp.sum(lacc[...], axis=0, keepdims=True), axis=1, keepdims=True)
            loss_ref[...] = jnp.broadcast_to(tot * (0.5 / d), loss_ref.shape)
            dg_ref[...] = jnp.sum(gacc[...], axis=0, keepdims=True)

    row = pl.BlockSpec((tm, d), lambda i: (i, 0))
    vec = pl.BlockSpec((1, d), lambda i: (0, 0))
    return pl.pallas_call(
        body, name=name, grid=(n,),
        in_specs=[row, row, vec, row],
        out_specs=[pl.BlockSpec((8, 128), lambda i: (0, 0)), row, row, vec],
        out_shape=[jax.ShapeDtypeStruct((8, 128), F32), jax.ShapeDtypeStruct((s, d), F32),
                   jax.ShapeDtypeStruct((s, d), BF16), jax.ShapeDtypeStruct((1, d), F32)],
        scratch_shapes=[pltpu.VMEM((8, d), F32), pltpu.VMEM((8, d), F32)],
        compiler_params=_params("arbitrary"),
    )(x1, o, g, target)


def _rms_bwd_rows(dyv, xv, gv):
    r = lax.rsqrt(jnp.mean(xv * xv, axis=-1, keepdims=True) + EPS)
    xh = xv * r
    dn = dyv * gv
    return r * (dn - xh * jnp.mean(dn * xh, axis=-1, keepdims=True)), _rowsum8(dyv * xh)


def norm_bwd(dy, inp, g, resid, name, inp2=None, g2=None, tm=256, dep=None):
    s, d = inp.shape
    n = s // tm
    chain = inp2 is not None

    def body(*refs):
        dy_ref, x_ref, g_ref, r_ref = refs[:4]
        outs = refs[-6:] if chain else refs[-3:]
        i = pl.program_id(0)

        @pl.when(i == 0)
        def _():
            for acc in outs[-2:] if chain else outs[-1:]:
                acc[...] = jnp.zeros_like(acc)

        if chain:
            x2_ref, g2_ref = refs[4:6]
            dx_ref, dg_ref, dx2_ref, dg2_ref, gacc, gacc2 = outs
        else:
            dx_ref, dg_ref, gacc = outs
        dx, dg_rows = _rms_bwd_rows(dy_ref[...].astype(F32), x_ref[...], g_ref[...])
        dx = dx + r_ref[...]
        dx_ref[...] = dx
        gacc[...] += dg_rows
        if chain:
            dx2, dg2_rows = _rms_bwd_rows(dx, x2_ref[...].astype(F32), g2_ref[...])
            dx2_ref[...] = dx2.astype(BF16)
            gacc2[...] += dg2_rows

        @pl.when(i == n - 1)
        def _():
            dg_ref[...] = jnp.sum(gacc[...], axis=0, keepdims=True)
            if chain:
                dg2_ref[...] = jnp.sum(gacc2[...], axis=0, keepdims=True)

    row = pl.BlockSpec((tm, d), lambda i: (i, 0))
    vec = pl.BlockSpec((1, d), lambda i: (0, 0))
    dep_args, dep_specs = _after(dep)
    extra = [inp2, g2] if chain else []
    return pl.pallas_call(
        body, name=name, grid=(n,),
        in_specs=[row, row, vec, row] + ([row, vec] if chain else []) + dep_specs,
        out_specs=[row, vec] * (2 if chain else 1),
        out_shape=[jax.ShapeDtypeStruct((s, d), F32), jax.ShapeDtypeStruct((1, d), F32)]
        + ([jax.ShapeDtypeStruct((s, d), BF16), jax.ShapeDtypeStruct((1, d), F32)] if chain else []),
        scratch_shapes=[pltpu.VMEM((8, d), F32)] * (2 if chain else 1),
        compiler_params=_params("arbitrary"),
    )(dy, inp, g, resid, *extra, *dep_args)


def _after(dep):
    if dep is None:
        return [], []
    return [dep], [pl.BlockSpec((8, 128), lambda *_: (0, 0))]


def _lane_concat(ref, count):
    return ref[0] if count == 1 else jnp.concatenate([ref[i] for i in range(count)], axis=1)


def mm_nn(a, w, out_dtype, name, tm=2048, tn=None, dep=None, group=1):
    m, k = a.shape
    tm = min(tm, m)
    ns, _, n = w.shape
    tn = n if tn is None else tn
    nj = n // tn
    assert group == 1 or nj == 1
    dep_args, dep_specs = _after(dep)

    def body(a_ref, w_ref, *rest):
        o_ref = rest[-1]
        o_ref[...] = jnp.dot(a_ref[...], _lane_concat(w_ref, group), preferred_element_type=F32).astype(out_dtype)

    return pl.pallas_call(
        body, name=name, grid=(ns // group, nj, m // tm),
        in_specs=[pl.BlockSpec((tm, k), lambda s, j, i: (i, 0)),
                  pl.BlockSpec((group, k, tn), lambda s, j, i: (s, 0, j))] + dep_specs,
        out_specs=pl.BlockSpec((tm, group * tn), lambda s, j, i: (i, s * nj + j)),
        out_shape=jax.ShapeDtypeStruct((m, ns * n), out_dtype),
        compiler_params=_params("parallel", "parallel", "parallel"),
    )(a, w, *dep_args)


def mm_nt(a, w, out_dtype, name, tm=1024, tn=None, dep=None, group=1):
    m = a.shape[0]
    tm = min(tm, m)
    ns, k, n = w.shape
    tn = n if tn is None else tn
    nj = n // tn
    assert group == 1 or nj == 1
    steps = ns * nj // group
    dep_args, dep_specs = _after(dep)

    def body(a_ref, w_ref, *rest):
        o_ref, acc = rest[-2:]
        r = pl.program_id(1)

        @pl.when(r == 0)
        def _():
            acc[...] = jnp.zeros_like(acc)

        acc[...] += lax.dot_general(a_ref[...], _lane_concat(w_ref, group), (((1,), (1,)), ((), ())),
                                    preferred_element_type=F32)

        @pl.when(r == steps - 1)
        def _():
            o_ref[...] = acc[...].astype(out_dtype)

    return pl.pallas_call(
        body, name=name, grid=(m // tm, steps),
        in_specs=[pl.BlockSpec((tm, group * tn), lambda i, r: (i, r)),
                  pl.BlockSpec((group, k, tn), lambda i, r: (r // nj, 0, r % nj))] + dep_specs,
        out_specs=pl.BlockSpec((tm, k), lambda i, r: (i, 0)),
        out_shape=jax.ShapeDtypeStruct((m, k), out_dtype),
        scratch_shapes=[pltpu.VMEM((tm, k), F32)],
        compiler_params=_params("parallel", "arbitrary"),
    )(a, w, *dep_args)


def mm_tn(a, b, ns, out_dtype, name, tk=1024, tm=2048, dep=None, pick=None, group=1):
    m, k = a.shape
    tm = min(tm, m)
    step, offset = (1, None) if pick is None else pick
    assert group == 1 or pick is None
    n = b.shape[1] // (ns * step)
    steps = m // tm
    dep_args, dep_specs = _after(dep)
    n_pre = 0 if pick is None else 1

    def b_block(s, j, r, *pre):
        return (r, s if pick is None else step * s + pre[0][0])

    def body(*refs):
        a_ref, b_ref = refs[n_pre:n_pre + 2]
        o_ref, acc = refs[-2:]
        r = pl.program_id(2)

        @pl.when(r == 0)
        def _():
            acc[...] = jnp.zeros_like(acc)

        acc[...] += lax.dot_general(a_ref[...], b_ref[...], (((0,), (0,)), ((), ())),
                                    preferred_element_type=F32)

        @pl.when(r == steps - 1)
        def _():
            for i in range(group):
                o_ref[i] = acc[:, i * n:(i + 1) * n].astype(out_dtype)

    return pl.pallas_call(
        body, name=name,
        grid_spec=pltpu.PrefetchScalarGridSpec(
            num_scalar_prefetch=n_pre, grid=(ns // group, k // tk, steps),
            in_specs=[pl.BlockSpec((tm, tk), lambda s, j, r, *pre: (r, j)),
                      pl.BlockSpec((tm, group * n), b_block)] + dep_specs,
            out_specs=pl.BlockSpec((group, tk, n), lambda s, j, r, *pre: (s, j, 0)),
            scratch_shapes=[pltpu.VMEM((tk, group * n), F32)]),
        out_shape=jax.ShapeDtypeStruct((ns, k, n), out_dtype),
        compiler_params=_params("parallel", "parallel", "arbitrary"),
    )(*([] if pick is None else [offset]), a, b, *dep_args)


SB_BLK = 128


LOG2E = 1.0 / math.log(2.0)


def _split_dot(v, tri2):
    hi = pltpu.bitcast(pltpu.bitcast(v, jnp.uint32) & jnp.uint32(0xFFFF0000), F32)
    lo = (v - hi).astype(BF16)
    return jnp.dot(jnp.concatenate([hi.astype(BF16), lo], axis=1), tri2, preferred_element_type=F32)


def _sb_scores(z2, lim, dcol, tri_ex, masked):
    sp = jnp.log2(1.0 + jnp.exp2(-jnp.abs(z2)))
    lb = jnp.minimum(z2, 0.0) - sp
    l1m = lb - z2
    mask = None
    if masked:
        mask = dcol < lim
        l1m = jnp.where(mask, l1m, 0.0)
    return mask, lb, l1m, _split_dot(l1m, tri_ex)


def _sb_consts():
    row = lax.broadcasted_iota(jnp.int32, (SB_BLK, SB_BLK), 0)
    col = lax.broadcasted_iota(jnp.int32, (SB_BLK, SB_BLK), 1)
    tri_ex = jnp.where(row > col, 1.0, 0.0).astype(BF16)
    tri_in = jnp.where(row >= col, 1.0, 0.0).astype(BF16)
    return col - row, jnp.concatenate([tri_ex, tri_ex], axis=0), jnp.concatenate([tri_in, tri_in], axis=0)


def sb_fwd(p, n_heads, name, tq=1024, nsub=8, dep=None):
    s = p.shape[0]
    h_n = n_heads
    b = SB_BLK
    nqs = tq // b
    tk = nsub * b
    scale = 1.0 / math.sqrt(HEAD_DIM)

    dep_args, dep_specs = _after(dep)

    def body(q_ref, k_ref, v_ref, *rest):
        o_ref, w_ref = rest[-2:]
        qi = pl.program_id(1)
        dcol, tri_ex, _ = _sb_consts()
        qv = [q_ref[qs * b:(qs + 1) * b, :] for qs in range(nqs)]
        n_groups = ((qi + 1) * nqs - 1) // nsub + 1

        def step(it, carry, masked):
            c1s, accs = carry
            g = n_groups - 1 - it
            off = pl.multiple_of(g * tk, tk)
            kg = k_ref[pl.ds(off, tk), :]
            vg = v_ref[pl.ds(off, tk), :]
            new_c1, new_acc = [], []
            for qs in range(nqs):
                qb = qi * nqs + qs
                square = masked and nqs == nsub
                nk = qs + 1 if square else nsub
                kq, vq = kg[:nk * b], vg[:nk * b]
                z2 = lax.dot_general(qv[qs], kq, (((1,), (1,)), ((), ())),
                                     preferred_element_type=F32) * (scale * LOG2E)
                blocks = [_sb_scores(z2[:, j * b:(j + 1) * b], (qb - (g * nsub + j)) * b, dcol, tri_ex,
                                     masked and (j == qs or not square)) for j in range(nk)]
                run = c1s[qs]
                ws = [None] * nk
                for j in reversed(range(nk)):
                    mask, lb, l1m, ls_loc = blocks[j]
                    wj = jnp.exp2(lb + ls_loc + run)
                    ws[j] = (wj if mask is None else jnp.where(mask, wj, 0.0)).astype(BF16)
                    run = run + jnp.sum(l1m, axis=1, keepdims=True)
                w = jnp.concatenate(ws, axis=1)
                w_ref[0, g, qs * b:(qs + 1) * b, 0:nk * b] = w
                new_acc.append(accs[qs] + jnp.dot(w, vq, preferred_element_type=F32))
                new_c1.append(run)
            return tuple(new_c1), tuple(new_acc)

        init = (tuple(jnp.zeros((b, 1), F32) for _ in range(nqs)),
                tuple(jnp.zeros((b, HEAD_DIM), F32) for _ in range(nqs)))
        assert all(((i + 1) * nqs - 1) // nsub * nsub <= i * nqs for i in range(s // tq))
        first = step(0, init, True)
        _, accs = lax.fori_loop(1, n_groups, functools.partial(step, masked=False), first)
        for qs in range(nqs):
            o_ref[qs * b:(qs + 1) * b, :] = accs[qs]

    return pl.pallas_call(
        body, name=name, grid=(h_n, s // tq),
        in_specs=[pl.BlockSpec((tq, HEAD_DIM), lambda h, i: (i, h)),
                  pl.BlockSpec((s, HEAD_DIM), lambda h, i: (0, h_n + h)),
                  pl.BlockSpec((s, HEAD_DIM), lambda h, i: (0, 2 * h_n + h))] + dep_specs,
        out_specs=[pl.BlockSpec((tq, HEAD_DIM), lambda h, i: (i, h)),
                   pl.BlockSpec((1, s // tk, tq, tk), lambda h, i: (h, 0, i, 0))],
        out_shape=[jax.ShapeDtypeStruct((s, h_n * HEAD_DIM), F32),
                   jax.ShapeDtypeStruct((h_n, s // tk, s, tk), BF16)],
        compiler_params=_params("parallel", "arbitrary"),
    )(p, p, p, *dep_args)


def sb_bwd(p, a, wts, da, n_heads, name, tq=1024, dep=None):
    s = p.shape[0]
    h_n = n_heads
    nq = s // tq
    b = SB_BLK
    nqs = tq // b
    tk = wts.shape[3]
    nsub = tk // b
    scale = 1.0 / math.sqrt(HEAD_DIM)
    dep_args, dep_specs = _after(dep)

    def body(q_ref, k_ref, v_ref, a_ref, da_ref, w_ref, *rest):
        dq_ref, dk_ref, dv_ref, dk_acc, dv_acc = rest[-5:]
        qi = pl.program_id(1)

        @pl.when(qi == 0)
        def _():
            dk_acc[...] = jnp.zeros_like(dk_acc)
            dv_acc[...] = jnp.zeros_like(dv_acc)

        dcol, _, tri_in = _sb_consts()
        q_all = q_ref[...]
        do_all = da_ref[...]
        qv = [q_ref[qs * b:(qs + 1) * b, :] for qs in range(nqs)]
        dov = [da_ref[qs * b:(qs + 1) * b, :] for qs in range(nqs)]
        tots = [jnp.sum(dov[qs].astype(F32) * a_ref[qs * b:(qs + 1) * b, :], axis=1, keepdims=True)
                for qs in range(nqs)]
        n_groups = ((qi + 1) * nqs - 1) // nsub + 1

        def step(it, carry, masked):
            c2s, dqs = carry
            g = n_groups - 1 - it
            off = pl.multiple_of(g * tk, tk)
            kg = k_ref[pl.ds(off, tk), :]
            vg = v_ref[pl.ds(off, tk), :]
            square = masked and nqs == nsub
            new_c2, new_dq, dz_rows, w_rows = [], [], [], []
            for qs in range(nqs):
                qb = qi * nqs + qs
                nk = qs + 1 if square else nsub
                kq, vq = kg[:nk * b], vg[:nk * b]
                z2 = lax.dot_general(qv[qs], kq, (((1,), (1,)), ((), ())),
                                     preferred_element_type=F32) * (-scale * LOG2E)
                dw = lax.dot_general(dov[qs], vq, (((1,), (1,)), ((), ())), preferred_element_type=F32)
                beta = 1.0 / (1.0 + jnp.exp2(z2))
                wq = w_ref[0, g, qs * b:(qs + 1) * b, 0:nk * b]
                e = dw * wq.astype(F32)
                run2 = c2s[qs]
                dzs = [None] * nk
                for j in reversed(range(nk)):
                    cols = slice(j * b, (j + 1) * b)
                    later = _split_dot(e[:, cols], tri_in) + run2
                    bj = beta[:, cols]
                    dz = (e[:, cols] * (1.0 - bj) - bj * (tots[qs] - later)) * scale
                    if masked and (j == qs or not square):
                        dz = jnp.where(dcol < (qb - (g * nsub + j)) * b, dz, 0.0)
                    dzs[j] = dz.astype(BF16)
                    run2 = run2 + jnp.sum(e[:, cols], axis=1, keepdims=True)
                dzq = jnp.concatenate(dzs, axis=1)
                new_dq.append(dqs[qs] + jnp.dot(dzq, kq, preferred_element_type=F32))
                new_c2.append(run2)
                pad = [jnp.zeros((b, (nsub - nk) * b), BF16)] if nk < nsub else []
                dz_rows.append(jnp.concatenate([dzq] + pad, axis=1))
                w_rows.append(jnp.concatenate([wq] + pad, axis=1))
            dz_all = jnp.concatenate(dz_rows, axis=0)
            w_all = jnp.concatenate(w_rows, axis=0)
            dk_acc[pl.ds(off, tk), :] += lax.dot_general(dz_all, q_all, (((0,), (0,)), ((), ())),
                                                         preferred_element_type=F32)
            dv_acc[pl.ds(off, tk), :] += lax.dot_general(w_all, do_all, (((0,), (0,)), ((), ())),
                                                         preferred_element_type=F32)
            return tuple(new_c2), tuple(new_dq)

        zeros = tuple(jnp.zeros((b, 1), F32) for _ in range(nqs))
        assert all(((i + 1) * nqs - 1) // nsub * nsub <= i * nqs for i in range(s // tq))
        first = step(0, (zeros, tuple(jnp.zeros((b, HEAD_DIM), F32) for _ in range(nqs))), True)
        _, dqs = lax.fori_loop(1, n_groups, functools.partial(step, masked=False), first)
        for qs in range(nqs):
            dq_ref[qs * b:(qs + 1) * b, :] = dqs[qs].astype(BF16)

        @pl.when(qi == nq - 1)
        def _():
            dk_ref[...] = dk_acc[...].astype(BF16)
            dv_ref[...] = dv_acc[...].astype(BF16)

    blk = pl.BlockSpec((tq, HEAD_DIM), lambda h, i: (i, h))
    full = pl.BlockSpec((s, HEAD_DIM), lambda h, i: (0, h))
    return pl.pallas_call(
        body, name=name, grid=(h_n, nq),
        in_specs=[blk, pl.BlockSpec((s, HEAD_DIM), lambda h, i: (0, h_n + h)),
                  pl.BlockSpec((s, HEAD_DIM), lambda h, i: (0, 2 * h_n + h)), blk, blk,
                  pl.BlockSpec((1, s // tk, tq, tk), lambda h, i: (h, 0, i, 0))] + dep_specs,
        out_specs=[blk, full, full],
        out_shape=[jax.ShapeDtypeStruct((s, h_n * HEAD_DIM), BF16)] * 3,
        scratch_shapes=[pltpu.VMEM((s, HEAD_DIM), F32), pltpu.VMEM((s, HEAD_DIM), F32)],
        compiler_params=_params("parallel", "arbitrary"),
    )(p, p, p, a, da, wts, *dep_args)


def _pool_window(xx, win, r0, rc):
    cur = xx[HALO:HALO + rc]
    ws = _window_sum(xx, win, True)[HALO:HALO + rc]
    t_idx = r0 + lax.broadcasted_iota(jnp.int32, (rc, 1), 0)
    inv = 1.0 / jnp.minimum(win, t_idx + 1).astype(F32)
    return ws * inv - cur, inv


def even_mix_fwd(a, p, pool_w, pool_scale, name, rc=512, dep=None):
    s = p.shape[0]
    ng = len(POOL_WINDOWS)
    cw = pool_w.shape[1]
    n_chunks = s // rc
    dep_args, dep_specs = _after(dep)

    def body(a_ref, u_ref, g_ref, w_ref, sc_ref, *rest):
        y_ref, upad = rest[-2:]
        j = pl.program_id(0)

        @pl.when(j < ng)
        def _():
            def chunk(ci, carry):
                rows = pl.ds(pl.multiple_of(ci * rc, rc), rc)
                y_ref[rows, :] = (a_ref[rows, :] * _silu(g_ref[rows, :].astype(F32))).astype(BF16)
                return carry

            lax.fori_loop(0, n_chunks, chunk, 0)

        for gi, win in enumerate(POOL_WINDOWS):
            @pl.when(j == ng + gi)
            def _(win=win):
                upad[0:HALO, :] = jnp.zeros((HALO, cw), F32)

                def fill(ci, carry):
                    r0 = pl.multiple_of(ci * rc, rc)
                    upad[pl.ds(pl.multiple_of(r0 + HALO, HALO), rc), :] = u_ref[pl.ds(r0, rc), :].astype(F32)
                    return carry

                lax.fori_loop(0, n_chunks, fill, 0)

                def chunk(ci, carry):
                    r0 = pl.multiple_of(ci * rc, rc)
                    rows = pl.ds(r0, rc)
                    pooled, _ = _pool_window(upad[pl.ds(r0, HALO + rc), :], win, r0, rc)
                    t = jnp.dot(pooled.astype(BF16), w_ref[0], preferred_element_type=F32)
                    y_ref[rows, :] = (t * sc_ref[...] * _silu(g_ref[rows, :].astype(F32))).astype(BF16)
                    return carry

                lax.fori_loop(0, n_chunks, chunk, 0)

    grp = lambda j: jnp.maximum(j - ng, 0)
    return pl.pallas_call(
        body, name=name, grid=(2 * ng,),
        in_specs=[pl.BlockSpec((s, cw), lambda j: (0, jnp.minimum(j, ng - 1))),
                  pl.BlockSpec((s, cw), lambda j: (0, 3 * ng + grp(j))),
                  pl.BlockSpec((s, cw), lambda j: (0, 4 * ng + j)),
                  pl.BlockSpec((1, cw, cw), lambda j: (grp(j), 0, 0)),
                  pl.BlockSpec((1, cw), lambda j: (0, grp(j)))] + dep_specs,
        out_specs=pl.BlockSpec((s, cw), lambda j: (0, j)),
        out_shape=jax.ShapeDtypeStruct((s, 2 * ng * cw), BF16),
        scratch_shapes=[pltpu.VMEM((HALO + s, cw), F32)],
        compiler_params=_params("arbitrary"),
    )(a, p, p, pool_w, pool_scale, *dep_args)


def even_mix_bwd(dy, a, p, pool_w, pool_scale, name, rc=512):
    s = p.shape[0]
    ng = len(POOL_WINDOWS)
    cw = pool_w.shape[1]
    n_chunks = s // rc

    def body(dy_ref, a_ref, u_ref, g_ref, w_ref, sc_ref, da_ref, du_ref, dg_ref, dw_ref, dsc_ref,
             upad, rpad, dpl, dw_acc, dsc_acc):
        j = pl.program_id(0)

        @pl.when(j < ng)
        def _():
            def chunk(ci, carry):
                rows = pl.ds(pl.multiple_of(ci * rc, rc), rc)
                dyv = dy_ref[rows, :].astype(F32)
                sg, dsg = _silu_and_grad(g_ref[rows, :].astype(F32))
                da_ref[rows, :] = (dyv * sg).astype(BF16)
                dg_ref[rows, :] = (dyv * a_ref[rows, :] * dsg).astype(BF16)
                return carry

            lax.fori_loop(0, n_chunks, chunk, 0)

        for gi, win in enumerate(POOL_WINDOWS):
            @pl.when(j == ng + gi)
            def _(win=win):
                upad[0:HALO, :] = jnp.zeros((HALO, cw), F32)
                rpad[s:s + HALO, :] = jnp.zeros((HALO, cw), F32)
                dw_acc[...] = jnp.zeros_like(dw_acc)
                dsc_acc[...] = jnp.zeros_like(dsc_acc)

                def fill(ci, carry):
                    r0 = pl.multiple_of(ci * rc, rc)
                    upad[pl.ds(pl.multiple_of(r0 + HALO, HALO), rc), :] = u_ref[pl.ds(r0, rc), :].astype(F32)
                    return carry

                lax.fori_loop(0, n_chunks, fill, 0)

                def chunk(ci, carry):
                    r0 = pl.multiple_of(ci * rc, rc)
                    rows = pl.ds(r0, rc)
                    pooled, inv = _pool_window(upad[pl.ds(r0, HALO + rc), :], win, r0, rc)
                    pb = pooled.astype(BF16)
                    wv = w_ref[0]
                    t = jnp.dot(pb, wv, preferred_element_type=F32)
                    scv = sc_ref[...]
                    dyv = dy_ref[rows, :].astype(F32)
                    sg, dsg = _silu_and_grad(g_ref[rows, :].astype(F32))
                    dpo = dyv * sg
                    dg_ref[rows, :] = (dyv * t * scv * dsg).astype(BF16)
                    dsc_acc[...] += _rowsum8(dpo * t)
                    dtb = (dpo * scv).astype(BF16)
                    dw_acc[...] += lax.dot_general(pb, dtb, (((0,), (0,)), ((), ())),
                                                   preferred_element_type=F32)
                    dpooled = lax.dot_general(dtb, wv, (((1,), (1,)), ((), ())),
                                              preferred_element_type=F32)
                    dpl[rows, :] = dpooled
                    rpad[rows, :] = dpooled * inv
                    return carry

                lax.fori_loop(0, n_chunks, chunk, 0)

                def chunk2(ci, carry):
                    r0 = pl.multiple_of(ci * rc, rc)
                    rows = pl.ds(r0, rc)
                    xx = rpad[pl.ds(r0, rc + HALO), :]
                    fs = _window_sum(xx, win, False)[0:rc]
                    du_ref[rows, :] = (fs - dpl[rows, :]).astype(BF16)
                    return carry

                lax.fori_loop(0, n_chunks, chunk2, 0)
                dw_ref[0] = dw_acc[...]
                dsc_ref[...] = jnp.sum(dsc_acc[...], axis=0, keepdims=True)

    grp = lambda j: jnp.maximum(j - ng, 0)
    att = lambda j: jnp.minimum(j, ng - 1)
    return pl.pallas_call(
        body, name=name, grid=(2 * ng,),
        in_specs=[pl.BlockSpec((s, cw), lambda j: (0, j)),
                  pl.BlockSpec((s, cw), lambda j: (0, att(j))),
                  pl.BlockSpec((s, cw), lambda j: (0, 3 * ng + grp(j))),
                  pl.BlockSpec((s, cw), lambda j: (0, 4 * ng + j)),
                  pl.BlockSpec((1, cw, cw), lambda j: (grp(j), 0, 0)),
                  pl.BlockSpec((1, cw), lambda j: (0, grp(j)))],
        out_specs=[pl.BlockSpec((s, cw), lambda j: (0, att(j))),
                   pl.BlockSpec((s, cw), lambda j: (0, grp(j))),
                   pl.BlockSpec((s, cw), lambda j: (0, j)),
                   pl.BlockSpec((1, cw, cw), lambda j: (grp(j), 0, 0)),
                   pl.BlockSpec((1, cw), lambda j: (0, grp(j)))],
        out_shape=[jax.ShapeDtypeStruct((s, ng * cw), BF16), jax.ShapeDtypeStruct((s, ng * cw), BF16),
                   jax.ShapeDtypeStruct((s, 2 * ng * cw), BF16),
                   jax.ShapeDtypeStruct((ng, cw, cw), F32), jax.ShapeDtypeStruct((1, ng * cw), F32)],
        scratch_shapes=[pltpu.VMEM((HALO + s, cw), F32), pltpu.VMEM((s + HALO, cw), F32),
                        pltpu.VMEM((s, cw), F32), pltpu.VMEM((cw, cw), F32), pltpu.VMEM((8, cw), F32)],
        compiler_params=_params("arbitrary"),
    )(dy, a, p, p, pool_w, pool_scale)


def _halo_before(tm):
    return lambda i: jnp.maximum(i * (tm // HALO) - 1, 0)


def _halo_after(tm, s):
    return lambda i: jnp.minimum((i + 1) * (tm // HALO), s // HALO - 1)


def odd_mix_fwd(p, sconv_w, dconv_w, dconv_b, cnorm_g, cnorm_b, name, tm=128, dep=None):
    s = p.shape[0]
    cw = sconv_w.shape[1]
    n = s // tm
    lanes = 128
    hb = _halo_before(tm)

    dep_args, dep_specs = _after(dep)

    def body(hc_ref, hch_ref, bc_ref, cc_ref, cch_ref, ga_ref, gah_ref, gb_ref, gbh_ref, g1_ref, g2_ref,
             sw_ref, dw_ref, db_ref, gam_ref, bet_ref, *rest):
        y_ref, dc_ref = rest[-2:]
        first = pl.program_id(0) == 0
        for l in range(cw // lanes):
            cols = slice(l * lanes, (l + 1) * lanes)
            mh = jnp.where(first, 0.0, cch_ref[:, cols].astype(F32) * hch_ref[:, cols].astype(F32))
            mm = cc_ref[:, cols].astype(F32) * hc_ref[:, cols].astype(F32)
            xx = jnp.concatenate([mh, mm], axis=0)
            tap = _Taps(xx, tm, True)
            cv = jnp.zeros((tm, lanes), F32)
            for k in range(SCONV_K):
                cv = cv + sw_ref[k:k + 1, cols] * tap(SCONV_K - 1 - k)
            c_out = bc_ref[:, cols].astype(F32) * cv
            y_ref[:, cols] = (c_out * _silu(g1_ref[:, cols].astype(F32))).astype(BF16)
            dh = jnp.where(first, 0.0, gah_ref[:, cols].astype(F32) * _sigmoid(gbh_ref[:, cols].astype(F32)))
            dm = ga_ref[:, cols].astype(F32) * _sigmoid(gb_ref[:, cols].astype(F32))
            xx = jnp.concatenate([dh, dm], axis=0)
            tap = _Taps(xx, tm, True)
            acc = jnp.zeros((tm, lanes), F32) + db_ref[:, cols]
            for k in range(CONF_K):
                acc = acc + dw_ref[k:k + 1, cols] * tap(CONF_K - 1 - k)
            dc_ref[:, cols] = acc
        rs = 64
        for r in range(tm // rs):
            rows = slice(r * rs, (r + 1) * rs)
            xv = dc_ref[rows, :]
            mu = jnp.mean(xv, axis=-1, keepdims=True)
            xc = xv - mu
            rstd = lax.rsqrt(jnp.mean(xc * xc, axis=-1, keepdims=True) + EPS)
            ln = xc * rstd * gam_ref[...] + bet_ref[...]
            y_ref[rows, cw:2 * cw] = (_silu(ln) * _silu(g2_ref[rows, :].astype(F32))).astype(BF16)

    main = lambda c: pl.BlockSpec((tm, cw), lambda i: (i, c))
    halo = lambda c: pl.BlockSpec((HALO, cw), lambda i: (hb(i), c))
    vec = lambda r: pl.BlockSpec((r, cw), lambda i: (0, 0))
    return pl.pallas_call(
        body, name=name, grid=(n,),
        in_specs=[main(0), halo(0), main(1), main(2), halo(2), main(3), halo(3), main(4), halo(4),
                  main(5), main(6), vec(SCONV_K), vec(CONF_K), vec(1), vec(1), vec(1)] + dep_specs,
        out_specs=[pl.BlockSpec((tm, 2 * cw), lambda i: (i, 0)), pl.BlockSpec((tm, cw), lambda i: (i, 0))],
        out_shape=[jax.ShapeDtypeStruct((s, 2 * cw), BF16), jax.ShapeDtypeStruct((s, cw), F32)],
        compiler_params=_params("parallel"),
    )(p, p, p, p, p, p, p, p, p, p, p, sconv_w, dconv_w, dconv_b, cnorm_g, cnorm_b, *dep_args)


def odd_bwd_ln(dy, p, dc, cnorm_g, cnorm_b, name, tm=256):
    s = p.shape[0]
    cw = dc.shape[1]
    n = s // tm
    rs = 128

    def body(dy_ref, g2_ref, dc_ref, gam_ref, bet_ref, ddc_ref, dg_ref, dgam_ref, dbet_ref, gacc, bacc):
        i = pl.program_id(0)

        @pl.when(i == 0)
        def _():
            gacc[...] = jnp.zeros_like(gacc)
            bacc[...] = jnp.zeros_like(bacc)

        def chunk(ci, carry):
            rows = pl.ds(pl.multiple_of(ci * rs, rs), rs)
            xv = dc_ref[rows, :]
            mu = jnp.mean(xv, axis=-1, keepdims=True)
            xc = xv - mu
            rstd = lax.rsqrt(jnp.mean(xc * xc, axis=-1, keepdims=True) + EPS)
            xh = xc * rstd
            gam = gam_ref[...]
            sl, dsl = _silu_and_grad(xh * gam + bet_ref[...])
            sg, dsg = _silu_and_grad(g2_ref[rows, :].astype(F32))
            dyv = dy_ref[rows, :].astype(F32)
            dg_ref[rows, :] = (dyv * sl * dsg).astype(BF16)
            dln = dyv * sg * dsl
            gacc[...] += _rowsum8(dln * xh)
            bacc[...] += _rowsum8(dln)
            dxh = dln * gam
            ddc_ref[rows, :] = rstd * (dxh - jnp.mean(dxh, axis=-1, keepdims=True)
                                       - xh * jnp.mean(dxh * xh, axis=-1, keepdims=True))
            return carry

        lax.fori_loop(0, tm // rs, chunk, 0)

        @pl.when(i == n - 1)
        def _():
            dgam_ref[...] = jnp.sum(gacc[...], axis=0, keepdims=True)
            dbet_ref[...] = jnp.sum(bacc[...], axis=0, keepdims=True)

    vec = pl.BlockSpec((1, cw), lambda i: (0, 0))
    return pl.pallas_call(
        body, name=name, grid=(n,),
        in_specs=[pl.BlockSpec((tm, cw), lambda i: (i, 1)), pl.BlockSpec((tm, cw), lambda i: (i, 6)),
                  pl.BlockSpec((tm, cw), lambda i: (i, 0)), vec, vec],
        out_specs=[pl.BlockSpec((tm, cw), lambda i: (i, 0)), pl.BlockSpec((tm, cw), lambda i: (i, 0)), vec, vec],
        out_shape=[jax.ShapeDtypeStruct((s, cw), F32), jax.ShapeDtypeStruct((s, cw), BF16),
                   jax.ShapeDtypeStruct((1, cw), F32), jax.ShapeDtypeStruct((1, cw), F32)],
        scratch_shapes=[pltpu.VMEM((8, cw), F32), pltpu.VMEM((8, cw), F32)],
        compiler_params=_params("arbitrary"),
    )(dy, p, dc, cnorm_g, cnorm_b)


def odd_bwd_conv(dy, p, ddc, dg2, sconv_w, dconv_w, name, tm=128):
    s = p.shape[0]
    cw = ddc.shape[1]
    n = s // tm
    lanes = 128
    hb = _halo_before(tm)
    ha = _halo_after(tm, s)

    def body(dy_ref, dya_ref, g1_ref, g1a_ref, bc_ref, bca_ref, hc_ref, hch_ref, cc_ref, cch_ref,
             ddc_ref, ddca_ref, ga_ref, gah_ref, gb_ref, gbh_ref, dg2_ref, sw_ref, dw_ref,
             dp_ref, dsw_ref, ddw_ref, ddb_ref, sw_acc, dw_acc, db_acc):
        i = pl.program_id(0)
        first = i == 0
        last = i == n - 1

        @pl.when(first)
        def _():
            sw_acc[...] = jnp.zeros_like(sw_acc)
            dw_acc[...] = jnp.zeros_like(dw_acc)
            db_acc[...] = jnp.zeros_like(db_acc)

        for l in range(cw // lanes):
            cols = slice(l * lanes, (l + 1) * lanes)
            mh = jnp.where(first, 0.0, cch_ref[:, cols].astype(F32) * hch_ref[:, cols].astype(F32))
            hcv = hc_ref[:, cols].astype(F32)
            ccv = cc_ref[:, cols].astype(F32)
            xx = jnp.concatenate([mh, ccv * hcv], axis=0)
            tap = _Taps(xx, tm, True)
            taps = [tap(SCONV_K - 1 - k) for k in range(SCONV_K)]
            cv = jnp.zeros((tm, lanes), F32)
            for k in range(SCONV_K):
                cv = cv + sw_ref[k:k + 1, cols] * taps[k]
            bcv = bc_ref[:, cols].astype(F32)
            dyv = dy_ref[:, cols].astype(F32)
            sg, dsg = _silu_and_grad(g1_ref[:, cols].astype(F32))
            dco = dyv * sg
            dp_ref[:, 5 * cw + l * lanes:5 * cw + (l + 1) * lanes] = (dyv * bcv * cv * dsg).astype(BF16)
            dp_ref[:, cw + l * lanes:cw + (l + 1) * lanes] = (dco * cv).astype(BF16)
            dcv = dco * bcv
            for k in range(SCONV_K):
                sw_acc[k * 8:(k + 1) * 8, cols] += _rowsum8(dcv * taps[k])
            dcv_a = jnp.where(last, 0.0, dya_ref[:, cols].astype(F32) * _silu(g1a_ref[:, cols].astype(F32))
                              * bca_ref[:, cols].astype(F32))
            xx = jnp.concatenate([dcv, dcv_a], axis=0)
            tap = _Taps(xx, tm, False)
            dm = jnp.zeros((tm, lanes), F32)
            for k in range(SCONV_K):
                dm = dm + sw_ref[k:k + 1, cols] * tap(SCONV_K - 1 - k)
            dp_ref[:, l * lanes:(l + 1) * lanes] = (dm * ccv).astype(BF16)
            dp_ref[:, 2 * cw + l * lanes:2 * cw + (l + 1) * lanes] = (dm * hcv).astype(BF16)
            gav = ga_ref[:, cols].astype(F32)
            sb = _sigmoid(gb_ref[:, cols].astype(F32))
            dh = jnp.where(first, 0.0, gah_ref[:, cols].astype(F32) * _sigmoid(gbh_ref[:, cols].astype(F32)))
            xx = jnp.concatenate([dh, gav * sb], axis=0)
            ddcv = ddc_ref[:, cols]
            db_acc[:, cols] += _rowsum8(ddcv)
            tap = _Taps(xx, tm, True)
            for k in range(CONF_K):
                dw_acc[k * 8:(k + 1) * 8, cols] += _rowsum8(ddcv * tap(CONF_K - 1 - k))
            ddc_a = jnp.where(last, 0.0, ddca_ref[:, cols])
            xx = jnp.concatenate([ddcv, ddc_a], axis=0)
            tap = _Taps(xx, tm, False)
            dgl = jnp.zeros((tm, lanes), F32)
            for k in range(CONF_K):
                dgl = dgl + dw_ref[k:k + 1, cols] * tap(CONF_K - 1 - k)
            dp_ref[:, 3 * cw + l * lanes:3 * cw + (l + 1) * lanes] = (dgl * sb).astype(BF16)
            dp_ref[:, 4 * cw + l * lanes:4 * cw + (l + 1) * lanes] = (dgl * gav * sb * (1.0 - sb)).astype(BF16)
        dp_ref[:, 6 * cw:7 * cw] = dg2_ref[...]

        @pl.when(last)
        def _():
            for k in range(SCONV_K):
                dsw_ref[k:k + 1, :] = jnp.sum(sw_acc[k * 8:(k + 1) * 8, :], axis=0, keepdims=True)
            for k in range(CONF_K):
                ddw_ref[k:k + 1, :] = jnp.sum(dw_acc[k * 8:(k + 1) * 8, :], axis=0, keepdims=True)
            ddb_ref[...] = jnp.sum(db_acc[...], axis=0, keepdims=True)

    def main(c):
        return pl.BlockSpec((tm, cw), lambda i: (i, c))

    def before(c):
        return pl.BlockSpec((HALO, cw), lambda i: (hb(i), c))

    def after(c):
        return pl.BlockSpec((HALO, cw), lambda i: (ha(i), c))

    def vec(r):
        return pl.BlockSpec((r, cw), lambda i: (0, 0))

    return pl.pallas_call(
        body, name=name, grid=(n,),
        in_specs=[main(0), after(0), main(5), after(5), main(1), after(1), main(0), before(0), main(2), before(2),
                  main(0), after(0), main(3), before(3), main(4), before(4), main(0), vec(SCONV_K), vec(CONF_K)],
        out_specs=[pl.BlockSpec((tm, 7 * cw), lambda i: (i, 0)), vec(SCONV_K), vec(CONF_K), vec(1)],
        out_shape=[jax.ShapeDtypeStruct((s, 7 * cw), BF16), jax.ShapeDtypeStruct((SCONV_K, cw), F32),
                   jax.ShapeDtypeStruct((CONF_K, cw), F32), jax.ShapeDtypeStruct((1, cw), F32)],
        scratch_shapes=[pltpu.VMEM((8 * SCONV_K, cw), F32), pltpu.VMEM((8 * CONF_K, cw), F32),
                        pltpu.VMEM((8, cw), F32)],
        compiler_params=_params("arbitrary"),
    )(dy, dy, p, p, p, p, p, p, p, p, ddc, ddc, p, p, p, p, dg2, sconv_w, dconv_w)


_ANY = pl.BlockSpec(memory_space=pl.ANY)


def _place():
    return lax.axis_index("x"), lax.axis_index("y"), lax.axis_index("c")


def all_gather(arrs, name, deps=()):
    n = len(arrs)

    def body(*refs):
        ins, outs = refs[:n], refs[n + len(deps):2 * n + len(deps)]
        send_sems, recv_sems, local_sems = refs[-3:]
        x, y, c = _place()
        me, sibling = (x, y, c), (x, y, 1 - c)
        chips = [(1 - x, y), (x, 1 - y), (1 - x, 1 - y)]

        def copy(a, k, block, to, src=None):
            px, py, pc = block
            dst = outs[a].at[4 * px + 2 * py + pc]
            return pltpu.make_async_remote_copy(
                src_ref=dst if src is None else src, dst_ref=dst,
                send_sem=send_sems.at[7 * a + k], recv_sem=recv_sems.at[7 * a + k],
                device_id=to, device_id_type=MESH)

        mine = [pltpu.make_async_copy(ins[a], outs[a].at[4 * x + 2 * y + c], local_sems.at[a]) for a in range(n)]
        first = []
        for a in range(n):
            first.append(copy(a, 0, me, sibling, src=ins[a]))
            first += [copy(a, 1 + j, me, (*chip, c), src=ins[a]) for j, chip in enumerate(chips)]
        for cp in first + mine:
            cp.start()
        passed = []
        for a in range(n):
            for j, chip in enumerate(chips):
                copy(a, 1 + j, (*chip, c), me).wait_recv()
                cp = copy(a, 4 + j, (*chip, c), sibling)
                cp.start()
                passed.append(cp)
        for a in range(n):
            copy(a, 0, sibling, me).wait_recv()
            for j, chip in enumerate(chips):
                copy(a, 4 + j, (*chip, 1 - c), me).wait_recv()
        for cp in first + passed:
            cp.wait_send()
        for cp in mine:
            cp.wait()

    return pl.pallas_call(
        body, name=name,
        out_shape=[jax.ShapeDtypeStruct((N_DEV,) + a.shape, a.dtype) for a in arrs],
        in_specs=[_ANY] * (n + len(deps)), out_specs=[_ANY] * n,
        scratch_shapes=[pltpu.SemaphoreType.DMA((7 * n,)), pltpu.SemaphoreType.DMA((7 * n,)),
                        pltpu.SemaphoreType.DMA((n,))],
    )(*arrs, *deps)


def in_proj_gathered(xs, g, w_own, extras, name, tm=1024):
    s, d = xs.shape
    n = w_own.shape[1]
    tm = min(tm, s)
    arrs = [w_own] + list(extras)
    na = len(arrs)
    tr = 256

    def body(*refs):
        x_ref, g_ref, ins = refs[0], refs[1], refs[2:2 + na]
        h_out, p_ref, outs = refs[2 + na], refs[3 + na], refs[4 + na:4 + 2 * na]
        (h_ref, xbuf, wbuf, obuf, send_sems, recv_sems, load_sem, store_sems, own_sems, h_sem,
         x_sems) = refs[4 + 2 * na:]
        x, y, c = _place()
        me, sibling = (x, y, c), (x, y, 1 - c)
        x_first = c == 0
        near = (jnp.where(x_first, 1 - x, x), jnp.where(x_first, y, 1 - y))
        far = (jnp.where(x_first, x, 1 - x), jnp.where(x_first, 1 - y, y))
        diag = (1 - x, 1 - y)
        k_near, k_far = jnp.where(x_first, 1, 2), jnp.where(x_first, 2, 1)
        f_near, f_far = k_near + 3, k_far + 3

        def slot(block):
            return 4 * block[0] + 2 * block[1] + block[2]

        def copy(a, k, block, to, src=None):
            dst = outs[a].at[slot(block)]
            return pltpu.make_async_remote_copy(
                src_ref=dst if src is None else src, dst_ref=dst,
                send_sem=send_sems.at[7 * a + k], recv_sem=recv_sems.at[7 * a + k],
                device_id=to, device_id_type=MESH)

        first = []
        for a in range(na):
            first += [copy(a, 0, me, sibling, src=ins[a]), copy(a, 1, me, (1 - x, y, c), src=ins[a]),
                      copy(a, 2, me, (x, 1 - y, c), src=ins[a])]
        for cp in first:
            cp.start()
        own = pltpu.make_async_copy(wbuf.at[0], outs[0].at[slot(me)], own_sems.at[0])
        mine = [pltpu.make_async_copy(ins[a], outs[a].at[slot(me)], own_sems.at[a]) for a in range(1, na)]
        stores = [None, None]

        def x_load(i):
            return pltpu.make_async_copy(x_ref.at[pl.ds(i * tr, tr), :], xbuf.at[i % 2], x_sems.at[i % 2])

        x_load(0).start()
        for i in range(s // tr):
            if i + 1 < s // tr:
                x_load(i + 1).start()
            x_load(i).wait()
            xv = xbuf[i % 2]
            r = lax.rsqrt(jnp.mean(xv * xv, axis=-1, keepdims=True) + EPS)
            h_ref[i * tr:(i + 1) * tr, :] = (xv * r * g_ref[...]).astype(BF16)
        h_store = pltpu.make_async_copy(h_ref, h_out, h_sem)
        h_store.start()

        def multiply(k, block, w_from):
            b = k % 2
            if k == 2:
                own.wait()
            load = pltpu.make_async_copy(w_from, wbuf.at[b], load_sem)
            load.start()
            if stores[b] is not None:
                stores[b].wait()
            load.wait()
            if k == 0:
                own.start()

            def chunk(i, carry):
                rows = pl.ds(pl.multiple_of(i * tm, tm), tm)
                obuf[b, rows, :] = jnp.dot(h_ref[rows, :], wbuf[b], preferred_element_type=F32).astype(BF16)
                return carry

            lax.fori_loop(0, s // tm, chunk, 0)
            stores[b] = pltpu.make_async_copy(
                obuf.at[b], p_ref.at[:, pl.ds(pl.multiple_of(slot(block) * n, 128), n)], store_sems.at[b])
            stores[b].start()

        passed = []

        def arrive(a, k, block):
            copy(a, k, block, me).wait_recv()

        def pass_on(a, k, block, to):
            cp = copy(a, k, block, to)
            cp.start()
            passed.append(cp)

        def gather(arrays, use):
            def arrive_all(k, block):
                for a in arrays:
                    arrive(a, k, block)

            def pass_all(k, block, to):
                for a in arrays:
                    pass_on(a, k, block, to)

            use(0, me)
            arrive_all(0, sibling)
            use(1, sibling)
            arrive_all(k_near, (*near, c))
            pass_all(3, (*near, c), (*far, c))
            pass_all(f_near, (*near, c), sibling)
            use(2, (*near, c))
            arrive_all(f_far, (*far, 1 - c))
            use(3, (*far, 1 - c))
            arrive_all(k_far, (*far, c))
            pass_all(f_far, (*far, c), sibling)
            use(4, (*far, c))
            arrive_all(f_near, (*near, 1 - c))
            use(5, (*near, 1 - c))
            arrive_all(3, (*diag, c))
            pass_all(6, (*diag, c), sibling)
            use(6, (*diag, c))
            arrive_all(6, (*diag, 1 - c))
            use(7, (*diag, 1 - c))

        gather(range(na), lambda k, block: multiply(k, block, ins[0] if k == 0 else outs[0].at[slot(block)]))
        for cp in mine:
            cp.start()
        for cp in first + passed:
            cp.wait_send()
        for cp in mine + stores + [h_store]:
            cp.wait()

    vmem = pl.BlockSpec(memory_space=pltpu.VMEM)
    outs = pl.pallas_call(
        body, name=name,
        out_shape=[jax.ShapeDtypeStruct((s, d), BF16), jax.ShapeDtypeStruct((s, N_DEV * n), BF16)]
        + [jax.ShapeDtypeStruct((N_DEV,) + a.shape, a.dtype) for a in arrs],
        in_specs=[_ANY, vmem] + [_ANY] * na, out_specs=[_ANY] * (2 + na),
        scratch_shapes=[pltpu.VMEM((s, d), BF16), pltpu.VMEM((2, tr, d), F32), pltpu.VMEM((2, d, n), BF16),
                        pltpu.VMEM((2, s, n), BF16),
                        pltpu.SemaphoreType.DMA((7 * na,)), pltpu.SemaphoreType.DMA((7 * na,)),
                        pltpu.SemaphoreType.DMA, pltpu.SemaphoreType.DMA((2,)), pltpu.SemaphoreType.DMA((na,)),
                        pltpu.SemaphoreType.DMA, pltpu.SemaphoreType.DMA((2,))],
        compiler_params=pltpu.CompilerParams(vmem_limit_bytes=VMEM_LIMIT),
    )(xs, g, *arrs)
    return outs[0], outs[1], outs[2], outs[3:]


_HBM = pl.BlockSpec(memory_space=pltpu.HBM)
_SEM = pl.BlockSpec(memory_space=pltpu.SEMAPHORE)
_DATAFLOW = pltpu.SideEffectType.DATAFLOW_SIDE_EFFECTING


def _peers_per_array(kind):
    return 1 if kind in ("sibling", "halves") else 3


def _split_copies(kind, srcs, lands, send_sems, recv_sems):
    x, y, c = _place()
    per = _peers_per_array(kind)
    out = []
    for a in range(len(lands)):
        if kind == "sibling":
            part = srcs[a] if srcs[a].shape[1] == 1 else srcs[a].at[:, pl.ds(1 - c, 1)]
            peers = [((x, y, 1 - c), part, lands[a], lands[a])]
        elif kind == "halves":
            mine, its = lands[a].at[:, pl.ds(c, 1)], lands[a].at[:, pl.ds(1 - c, 1)]
            peers = [((x, y, 1 - c), mine, mine, its)]
        else:
            peers = []
            for px, py in [(1 - x, y), (x, 1 - y), (1 - x, 1 - y)]:
                if kind == "gather":
                    views = (srcs[a], lands[a].at[4 * x + 2 * y + c], lands[a].at[4 * px + 2 * py + c])
                else:
                    views = (srcs[a].at[2 * px + py], lands[a].at[2 * x + y], lands[a].at[2 * px + py])
                peers.append(((px, py, c),) + views)
        for j, (peer, src, dst, arrives) in enumerate(peers):
            sems = dict(send_sem=send_sems.at[per * a + j], recv_sem=recv_sems.at[per * a + j],
                        device_id=peer, device_id_type=MESH)
            out.append((pltpu.make_async_remote_copy(src_ref=src, dst_ref=dst, **sems),
                        pltpu.make_async_remote_copy(src_ref=src, dst_ref=arrives, **sems)))
    return out


def split_start(kind, srcs, lands, deps, name):
    ns, nl = len(srcs), len(lands)
    n_sems = _peers_per_array(kind) * nl
    held = list(srcs) + list(lands)

    def body(*refs):
        send_sems, recv_sems = refs[len(held) + len(deps)], refs[len(held) + len(deps) + 1]
        for copy, _ in _split_copies(kind, refs[:ns], refs[ns:ns + nl], send_sems, recv_sems):
            copy.start()
        token = refs[-1]
        token[...] = jnp.zeros_like(token)

    outs = pl.pallas_call(
        body, name=name,
        out_shape=(pltpu.SemaphoreType.DMA((n_sems,)), pltpu.SemaphoreType.DMA((n_sems,)),
                   *[pltpu.HBM(a.shape, a.dtype) for a in held], jax.ShapeDtypeStruct((8, 128), F32)),
        in_specs=[_HBM] * len(held) + [_ANY] * len(deps),
        out_specs=(_SEM, _SEM, *([_HBM] * len(held)), pl.BlockSpec(memory_space=pltpu.VMEM)),
        input_output_aliases={i: 2 + i for i in range(len(held))},
        compiler_params=pltpu.CompilerParams(has_side_effects=_DATAFLOW),
    )(*[pltpu.with_memory_space_constraint(a, pltpu.HBM) for a in held], *deps)
    return outs[0], outs[1], list(outs[2:2 + ns]), list(outs[2 + ns:2 + ns + nl]), outs[-1]


def split_wait(kind, send_sems, recv_sems, srcs, lands, afters, name):
    ns, nl = len(srcs), len(lands)
    held = list(srcs) + list(lands)

    def body(*refs):
        for _, arrival in _split_copies(kind, refs[:ns], refs[ns:ns + nl], refs[ns + nl], refs[ns + nl + 1]):
            arrival.wait_send()
            arrival.wait_recv()

    outs = pl.pallas_call(
        body, name=name,
        out_shape=[pltpu.HBM(a.shape, a.dtype) for a in held],
        in_specs=[_HBM] * len(held) + [_SEM, _SEM] + [_ANY] * len(afters),
        out_specs=[_HBM] * len(held),
        input_output_aliases={i: i for i in range(len(held))},
        compiler_params=pltpu.CompilerParams(has_side_effects=_DATAFLOW),
    )(*held, send_sems, recv_sems, *afters)
    return list(outs[:ns]), list(outs[ns:])


def place_block(land, block, dev, name):
    r, c = block.shape
    tr = min(r, 512)

    def body(dev_ref, land_ref, b_ref, o_ref):
        del dev_ref, land_ref
        o_ref[...] = b_ref[...]

    return pl.pallas_call(
        body, name=name,
        grid_spec=pltpu.PrefetchScalarGridSpec(
            num_scalar_prefetch=1, grid=(r // tr,),
            in_specs=[_ANY, pl.BlockSpec((tr, c), lambda i, dev_ref: (i, 0))],
            out_specs=pl.BlockSpec((None, tr, c), lambda i, dev_ref: (dev_ref[0], i, 0))),
        out_shape=jax.ShapeDtypeStruct(land.shape, land.dtype),
        input_output_aliases={1: 0},
        compiler_params=_params("parallel"),
    )(dev, land, block)


def pair_add(own, recv, core, name):
    _, _, r, c = own.shape
    tr = min(r, 2048)

    def body(core_ref, own_ref, recv_ref, o_ref):
        del core_ref
        o_ref[...] = (own_ref[...].astype(F32) + recv_ref[...].astype(F32)).astype(BF16)

    return pl.pallas_call(
        body, name=name,
        grid_spec=pltpu.PrefetchScalarGridSpec(
            num_scalar_prefetch=1, grid=(4, r // tr),
            in_specs=[pl.BlockSpec((None, None, tr, c), lambda k, i, core_ref: (k, core_ref[0], i, 0)),
                      pl.BlockSpec((None, None, tr, c), lambda k, i, core_ref: (k, 0, i, 0))],
            out_specs=pl.BlockSpec((None, tr, c), lambda k, i, core_ref: (k, i, 0))),
        out_shape=jax.ShapeDtypeStruct((4, r, c), BF16),
        compiler_params=_params("parallel", "parallel"),
    )(core, own, recv)


def _adamw_math(w, g, m, v):
    m2 = ADAM_B1 * m + (1.0 - ADAM_B1) * g
    v2 = ADAM_B2 * v + (1.0 - ADAM_B2) * (g * g)
    m_hat = m2 / (1.0 - ADAM_B1 ** ADAM_STEP)
    v_hat = v2 / (1.0 - ADAM_B2 ** ADAM_STEP)
    delta = -ADAM_LR * (m_hat / (jnp.sqrt(v_hat) + ADAM_EPS) + ADAM_WD * w)
    return delta, m2, v2


def adamw_big(w, m, v, own, got, chip, name):
    r, c = w.shape
    tr = min(r, 512)

    def body(chip_ref, w_ref, m_ref, v_ref, p0, p1, p2, p3, g_ref, d_ref, m2_ref, v2_ref):
        del chip_ref
        g = ((p0[...].astype(F32) + p1[...].astype(F32)) + p2[...].astype(F32)) + p3[...].astype(F32)
        delta, m2, v2 = _adamw_math(w_ref[...], g, m_ref[...], v_ref[...])
        g_ref[...] = g
        d_ref[...] = delta
        m2_ref[...] = m2
        v2_ref[...] = v2

    row = pl.BlockSpec((tr, c), lambda i, chip_ref: (i, 0))

    def slab(flip):
        return pl.BlockSpec((None, tr, c), lambda i, chip_ref: (chip_ref[0] ^ flip, i, 0))

    return pl.pallas_call(
        body, name=name,
        grid_spec=pltpu.PrefetchScalarGridSpec(
            num_scalar_prefetch=1, grid=(r // tr,),
            in_specs=[row, row, row, slab(0), slab(1), slab(2), slab(3)],
            out_specs=[row] * 4),
        out_shape=[jax.ShapeDtypeStruct((r, c), F32)] * 4,
        compiler_params=_params("parallel"),
    )(chip, w, m, v, own, got, got, got)


def sum_devices(g8, name):
    def body(g_ref, o_ref):
        tot = g_ref[0]
        for k in range(1, N_DEV):
            tot = tot + g_ref[k]
        o_ref[...] = tot

    return pl.pallas_call(body, name=name, out_shape=jax.ShapeDtypeStruct(g8.shape[1:], F32))(g8)


def adamw_small(ws, gs, ms, vs, name):
    n = len(ws)

    def body(*refs):
        w_r, g_r, m_r, v_r = refs[:n], refs[n:2 * n], refs[2 * n:3 * n], refs[3 * n:4 * n]
        d_o, m_o, v_o = refs[4 * n:5 * n], refs[5 * n:6 * n], refs[6 * n:7 * n]
        for k in range(n):
            delta, m2, v2 = _adamw_math(w_r[k][...], g_r[k][...], m_r[k][...], v_r[k][...])
            d_o[k][...] = delta
            m_o[k][...] = m2
            v_o[k][...] = v2

    shapes = [jax.ShapeDtypeStruct(w.shape, F32) for w in ws]
    outs = pl.pallas_call(body, name=name, out_shape=shapes * 3)(*ws, *gs, *ms, *vs)
    return outs[:n], outs[n:2 * n], outs[2 * n:]


def _rows128(a):
    return a.reshape(-1, 128)


def _pad_rows(a, rows):
    return jnp.pad(a, ((0, rows - a.shape[0]), (0, 0)))


def kernel(x, ln_pre_even, w_in_even, pool_w, pool_scale, w_out_even, ln_post_even, ln_pre_odd, w_in_odd, sconv_w, dconv_w, dconv_b, cnorm_g, cnorm_b, w_out_odd, ln_post_odd, loss_target, m_ln_pre_even, m_w_in_even, m_pool_w, m_pool_scale, m_w_out_even, m_ln_post_even, m_ln_pre_odd, m_w_in_odd, m_sconv_w, m_dconv_w, m_dconv_b, m_cnorm_g, m_cnorm_b, m_w_out_odd, m_ln_post_odd, v_ln_pre_even, v_w_in_even, v_pool_w, v_pool_scale, v_w_out_even, v_ln_post_even, v_ln_pre_odd, v_w_in_odd, v_sconv_w, v_dconv_w, v_dconv_b, v_cnorm_g, v_cnorm_b, v_w_out_odd, v_ln_post_odd):
    xs = x[0]
    tgt = loss_target[0]
    s, d = xs.shape
    half = d // 2
    n_heads = half // HEAD_DIM
    ng = len(POOL_WINDOWS)
    cwp = half // ng
    dev = 4 * lax.axis_index("x") + 2 * lax.axis_index("y") + lax.axis_index("c")
    core = lax.axis_index("c").astype(jnp.int32).reshape(1)

    pr = pool_w.shape[2]
    cl = sconv_w.shape[2]
    small_parts = [(_rows128(ln_pre_odd), 8), (sconv_w[0], 8), (dconv_w[0], 32), (dconv_b, 8),
                   (cnorm_g, 8), (cnorm_b, 8), (_rows128(ln_post_odd), 8)]
    small_local = jnp.concatenate([_pad_rows(a, r) for a, r in small_parts], axis=0)
    h0, p0, g_wie, (g_pw, g_small) = in_proj_gathered(
        xs, ln_pre_even, w_in_even[0].astype(BF16), [pool_w[0].reshape(ng * pr, cwp).astype(BF16), small_local],
        "ag_in_proj_even")
    comm = _Exchanges(dev, core, d)
    token = comm.start_weights("out_even", [w_out_even[0].astype(BF16)], [p0])
    token = comm.start_weights("in_odd", [w_in_odd[0].astype(BF16)], [token])
    sb_dep = comm.start_weights("out_odd", [w_out_odd[0].astype(BF16)], [token])
    pool_full = g_pw.reshape(N_DEV, ng, pr, cwp).transpose(1, 0, 2, 3).reshape(ng, cwp, cwp)
    nl = ln_pre_odd.shape[1] // 128

    def chan(lo, rows):
        return g_small[:, lo:lo + rows].transpose(1, 0, 2).reshape(rows, N_DEV * cl)

    ln_pre_odd_f = g_small[:, 0:nl].reshape(1, d)
    sconv_f = chan(8, SCONV_K)
    dconv_f = chan(16, CONF_K)
    dconv_b_f = chan(48, 1)
    cnorm_g_f = chan(56, 1)
    cnorm_b_f = chan(64, 1)
    ln_post_odd_f = g_small[:, 72:72 + nl].reshape(1, d)

    loss_blk, grad_x, small_g = _fwd_bwd(
        xs, tgt, ln_pre_even, h0, p0, g_wie, pool_full, pool_scale, ln_post_even, ln_pre_odd_f,
        sconv_f, dconv_f, dconv_b_f, cnorm_g_f, cnorm_b_f, ln_post_odd_f, comm, sb_dep)
    small_w = [ln_pre_even, pool_scale, ln_post_even, ln_pre_odd, sconv_w[0], dconv_w[0], dconv_b, cnorm_g, cnorm_b, ln_post_odd]
    small_m = [m_ln_pre_even, m_pool_scale, m_ln_post_even, m_ln_pre_odd, m_sconv_w[0], m_dconv_w[0], m_dconv_b, m_cnorm_g, m_cnorm_b, m_ln_post_odd]
    small_v = [v_ln_pre_even, v_pool_scale, v_ln_post_even, v_ln_pre_odd, v_sconv_w[0], v_dconv_w[0], v_dconv_b, v_cnorm_g, v_cnorm_b, v_ln_post_odd]
    big = {"w_in_even": (w_in_even, m_w_in_even, v_w_in_even), "pool_w": (pool_w, m_pool_w, v_pool_w),
           "w_out_even": (w_out_even, m_w_out_even, v_w_out_even), "w_in_odd": (w_in_odd, m_w_in_odd, v_w_in_odd),
           "w_out_odd": (w_out_odd, m_w_out_odd, v_w_out_odd)}
    upd = comm.finish_updates(big, [grad_x])
    upd.update(comm.finish_updates(big, [grad_x]))
    sg, sd, sm, sv, loss = _update_small(small_g, loss_blk, small_w, small_m, small_v, dev, d, cl,
                                         deps=[upd["w_in_odd"][1], upd["w_out_even"][1]])
    upd.update(comm.finish_updates(big, sd))
    (g_wie_o, d_wie, m_wie, v_wie), (g_pw_o, d_pw, m_pw, v_pw) = upd["w_in_even"], upd["pool_w"]
    (g_woe_o, d_woe, m_woe, v_woe), (g_wio_o, d_wio, m_wio, v_wio) = upd["w_out_even"], upd["w_in_odd"]
    g_woo_o, d_woo, m_woo, v_woo = upd["w_out_odd"]

    def order(small, wie, pw, woe, wio, woo):
        return [small[0], wie, pw, small[1], woe, small[2], small[3], wio, small[4], small[5], small[6],
                small[7], small[8], woo, small[9]]

    grads = order(sg, g_wie_o, g_pw_o, g_woe_o, g_wio_o, g_woo_o)
    deltas = order(sd, d_wie, d_pw, d_woe, d_wio, d_woo)
    new_m = order(sm, m_wie, m_pw, m_woe, m_wio, m_woo)
    new_v = order(sv, v_wie, v_pw, v_woe, v_wio, v_woo)
    return (loss, grad_x[None], *grads, *deltas, *new_m, *new_v)


def _fwd_bwd(xs, tgt, ln_pre_even, h0, p0, g_wie, pool_full, pool_scale, ln_post_even, ln_pre_odd_f,
             sconv_f, dconv_f, dconv_b_f, cnorm_g_f, cnorm_b_f, ln_post_odd_f, comm, sb_dep):
    d = xs.shape[1]
    n_heads = d // 2 // HEAD_DIM
    ng, cwp = pool_full.shape[0], pool_full.shape[1]
    a0, sb_wts = sb_fwd(p0, n_heads, "sb_fwd", dep=sb_dep)
    dep = comm.weights_arrived("out_even", after=a0)
    y0 = even_mix_fwd(a0, p0, pool_full, pool_scale, "even_mix_fwd", dep=dep)
    (w_out_e,) = comm.weights("out_even", after=y0)
    w_out_e = w_out_e.reshape(1, d, d)
    o0 = mm_nn(y0, w_out_e, BF16, "out_proj_even", tn=512)
    dep = comm.weights_arrived("in_odd", after=o0)
    x1, h1 = postnorm_fwd(xs, o0, ln_post_even, ln_pre_odd_f, "post_even", dep=dep)
    (g_wio,) = comm.weights("in_odd", after=x1)
    p1 = mm_nn(h1, g_wio, BF16, "in_proj_odd", group=2)
    dep = comm.weights_arrived("out_odd", after=p1)
    y1, dc = odd_mix_fwd(p1, sconv_f, dconv_f, dconv_b_f, cnorm_g_f, cnorm_b_f, "odd_mix_fwd", dep=dep)
    (w_out_o,) = comm.weights("out_odd", after=y1)
    w_out_o = w_out_o.reshape(1, d, d)
    o1 = mm_nn(y1, w_out_o, BF16, "out_proj_odd", tn=512)
    loss_blk, gx2, do1, dg_post_odd = final_fwd_bwd(x1, o1, ln_post_odd_f, tgt, "post_odd_loss")

    dw_out_o = mm_tn(y1, do1, 1, BF16, "dw_out_odd")
    dy1 = mm_nt(do1, w_out_o, BF16, "dy_odd")
    ddc, dg2, dgam, dbet = odd_bwd_ln(dy1, p1, dc, cnorm_g_f, cnorm_b_f, "odd_bwd_ln")
    dp1, dsconv, ddconv, ddconv_b = odd_bwd_conv(dy1, p1, ddc, dg2, sconv_f, dconv_f, "odd_bwd_conv")
    dw_in_o = mm_tn(h1, dp1, N_DEV, BF16, "dw_in_odd", group=2)
    dep = comm.reduce_begin({"w_out_odd": dw_out_o.reshape(N_DEV, d // N_DEV, d), "w_in_odd": dw_in_o}, "odd")
    dh1 = mm_nt(dp1, g_wio, BF16, "dh_odd", dep=dep, group=2)
    dep = comm.reduce_send(after=dh1)
    gx1, dg_pre_odd, do0, dg_post_even = norm_bwd(dh1, x1, ln_pre_odd_f, gx2, "pre_odd_post_even_bwd",
                                                  inp2=o0, g2=ln_post_even, dep=dep)

    dw_out_e = mm_tn(y0, do0, 1, BF16, "dw_out_even")
    dy0 = mm_nt(do0, w_out_e, BF16, "dy_even")
    da0, du0, dg0, dpool, dpool_scale = even_mix_bwd(dy0, a0, p0, pool_full, pool_scale, "even_mix_bwd")
    pr = cwp // N_DEV
    dpool_slabs = dpool.astype(BF16).reshape(ng, N_DEV, pr, cwp).transpose(1, 0, 2, 3).reshape(N_DEV, ng * pr, cwp)
    dep = comm.reduce_begin({"w_out_even": dw_out_e.reshape(N_DEV, d // N_DEV, d), "pool_w": dpool_slabs}, "even_out")
    dq0, dk0, dv0 = sb_bwd(p0, a0, sb_wts, da0, n_heads, "sb_bwd", dep=dep)
    dep = comm.reduce_send(after=dq0)
    dp0 = jnp.concatenate([dq0, dk0, dv0, du0, dg0], axis=1)
    dw_sibling = mm_tn(h0, dp0, N_DEV // 2, BF16, "dw_in_even_sibling", dep=dep, pick=(2, 1 - comm.core))
    dep = comm.reduce_begin({"w_in_even": dw_sibling}, "even_in", sibling_part=True)
    dw_own = mm_tn(h0, dp0, N_DEV // 2, BF16, "dw_in_even_own", dep=dep, pick=(2, comm.core))
    dep = comm.reduce_send(after=dw_own, own_part={"w_in_even": dw_own})
    dh0 = mm_nt(dp0, g_wie, BF16, "dh_even", dep=dep, group=2)
    dep = None
    grad_x, dg_pre_even = norm_bwd(dh0, xs, ln_pre_even, gx1, "pre_even_bwd", tm=512, dep=dep)
    small_g = [dg_pre_even, dpool_scale, dg_post_even, dg_pre_odd, dsconv, ddconv, ddconv_b, dgam, dbet, dg_post_odd]
    return loss_blk, grad_x, small_g


class _Exchanges:
    def __init__(self, dev, core, d):
        self.dev = dev.astype(jnp.int32).reshape(1)
        self.core = core
        self.chip = (dev // 2).astype(jnp.int32).reshape(1)
        self.d = d
        self.in_flight = {}
        self.to_sibling = None
        self.pending = []

    def start_weights(self, tag, blocks, afters):
        lands = [lax.empty((N_DEV,) + b.shape, b.dtype) for b in blocks]
        send, recv, srcs, lands, token = split_start("gather", blocks, lands, afters, "ag_start_" + tag)
        self.in_flight[tag] = (send, recv, srcs, lands)
        return token

    def weights_arrived(self, tag, after):
        send, recv, srcs, lands = self.in_flight.pop(tag)
        srcs, lands = split_wait("gather", send, recv, srcs, lands, [after], "ag_wait_" + tag)
        lands = [place_block(l, b, self.dev, "ag_own_%s_%d" % (tag, k)) for k, (l, b) in enumerate(zip(lands, srcs))]
        lands = [l.reshape((4, 2) + l.shape[1:]) for l in lands]
        send, recv, _, lands, token = split_start("halves", [], lands, [], "ag_sibling_start_" + tag)
        self.in_flight[tag] = (send, recv, lands)
        return token

    def weights(self, tag, after):
        send, recv, lands = self.in_flight.pop(tag)
        _, lands = split_wait("halves", send, recv, [], lands, [after], "ag_sibling_wait_" + tag)
        return [l.reshape((N_DEV,) + l.shape[2:]) for l in lands]

    def reduce_begin(self, partials, tag, sibling_part=False):
        names = list(partials)
        arrs = [partials[k].reshape((4, 1 if sibling_part else 2) + partials[k].shape[1:]) for k in names]
        lands = [lax.empty((4, 1) + a.shape[2:], a.dtype) for a in arrs]
        send, recv, srcs, lands, token = split_start("sibling", arrs, lands, [], "rs_sibling_start_" + tag)
        self.to_sibling = (tag, names, send, recv, srcs, lands)
        return token

    def reduce_send(self, after, own_part=None):
        tag, names, send, recv, srcs, lands = self.to_sibling
        srcs, lands = split_wait("sibling", send, recv, srcs, lands, [after], "rs_sibling_wait_" + tag)
        which = self.core
        if own_part is not None:
            srcs = [own_part[k].reshape((4, 1) + own_part[k].shape[1:]) for k in names]
            which = jnp.zeros((1,), jnp.int32)
        sums = [pair_add(o, r, which, "rs_pair_add_" + k) for k, o, r in zip(names, srcs, lands)]
        zones = [lax.empty(a.shape, a.dtype) for a in sums]
        send, recv, srcs, zones, token = split_start("scatter", sums, zones, [], "rs_start_" + tag)
        self.pending.append((tag, names, send, recv, srcs, zones))
        return token

    def finish_updates(self, big, afters):
        tag, names, send, recv, srcs, lands = self.pending.pop(0)
        srcs, lands = split_wait("scatter", send, recv, srcs, lands, afters, "rs_wait_" + tag)
        out = {}
        for name, own, got in zip(names, srcs, lands):
            w, m, v = big[name]
            shp = own.shape[1:]
            outs = adamw_big(w.reshape(shp), m.reshape(shp), v.reshape(shp), own, got, self.chip, "adamw_" + name)
            out[name] = [o.reshape(w.shape) for o in outs]
        return out


def _update_small(small_g, loss_blk, small_w, small_m, small_v, dev, d, cl, deps):
    packed = jnp.concatenate([_rows128(g) for g in small_g] + [loss_blk], axis=0)
    (g8,) = all_gather([packed], "ag_small_grads", deps)
    tot = sum_devices(g8, "sum_small_grads")
    loss = tot[packed.shape[0] - 8, 0]
    full_g = []
    lo = 0
    for g in small_g:
        rows = g.size // 128
        full_g.append(tot[lo:lo + rows].reshape(g.shape))
        lo += rows

    def mine(g, width):
        return lax.dynamic_slice_in_dim(g, dev * width, width, axis=g.ndim - 1)

    fg = full_g
    small_gl = [fg[0], fg[1], fg[2], mine(fg[3], d // N_DEV), mine(fg[4], cl), mine(fg[5], cl), mine(fg[6], cl),
                mine(fg[7], cl), mine(fg[8], cl), mine(fg[9], d // N_DEV)]
    sd, sm, sv = adamw_small(small_w, small_gl, small_m, small_v, "adamw_small")

    def like(k, a):
        return a[None] if k in (4, 5) else a

    sg = [like(k, a) for k, a in enumerate(small_gl)]
    sd = [like(k, a) for k, a in enumerate(sd)]
    sm = [like(k, a) for k, a in enumerate(sm)]
    sv = [like(k, a) for k, a in enumerate(sv)]
    return sg, sd, sm, sv, loss
```

```python
import functools
import math

import jax
import jax.numpy as jnp
from jax import lax
from jax.experimental import pallas as pl
from jax.experimental.pallas import tpu as pltpu

F32 = jnp.float32
BF16 = jnp.bfloat16
EPS = 1e-6
HEAD_DIM = 128
POOL_WINDOWS = (2, 4, 8, 16)
SCONV_K = 3
CONF_K = 31
HALO = 32
N_DEV = 8
VMEM_LIMIT = 56 * 1024 * 1024
MESH = pl.DeviceIdType.MESH

ADAM_LR = 0.001
ADAM_B1 = 0.9
ADAM_B2 = 0.999
ADAM_EPS = 1e-08
ADAM_WD = 0.01
ADAM_STEP = 10


def _params(*sem):
    return pltpu.CompilerParams(dimension_semantics=sem, vmem_limit_bytes=VMEM_LIMIT)


def _sigmoid(v):
    return 1.0 / (1.0 + jnp.exp(-v))


def _silu(v):
    return v * _sigmoid(v)


def _silu_and_grad(v):
    s = _sigmoid(v)
    return v * s, s * (1.0 + v * (1.0 - s))


def _rowsum8(v):
    r, c = v.shape
    return jnp.sum(v.reshape(r // 8, 8, c), axis=0)


SUBLANES = 8


class _Taps:
    def __init__(self, xx, rows, before):
        self.xx, self.rows, self.before, self.rotated = xx, rows, before, {}

    def __call__(self, i):
        r, q = i % SUBLANES, i // SUBLANES
        if r not in self.rotated:
            n = self.xx.shape[0]
            self.rotated[r] = self.xx if r == 0 else pltpu.roll(self.xx, r if self.before else n - r, 0)
        lo = HALO - SUBLANES * q if self.before else SUBLANES * q
        return self.rotated[r][lo:lo + self.rows]


def _window_sum(xx, win, before):
    n = xx.shape[0]
    acc = xx
    k = 1
    while k < win:
        acc = acc + pltpu.roll(acc, k if before else n - k, 0)
        k *= 2
    return acc


def postnorm_fwd(x, o, g, g_next, name, tm=512, dep=None):
    s, d = x.shape
    dep_args, dep_specs = _after(dep)

    def body(x_ref, o_ref, g_ref, gn_ref, *rest):
        y_ref, h_ref = rest[-2:]
        ov = o_ref[...].astype(F32)
        r = lax.rsqrt(jnp.mean(ov * ov, axis=-1, keepdims=True) + EPS)
        y = x_ref[...] + ov * r * g_ref[...]
        y_ref[...] = y
        r2 = lax.rsqrt(jnp.mean(y * y, axis=-1, keepdims=True) + EPS)
        h_ref[...] = (y * r2 * gn_ref[...]).astype(BF16)

    row = pl.BlockSpec((tm, d), lambda i: (i, 0))
    vec = pl.BlockSpec((1, d), lambda i: (0, 0))
    return pl.pallas_call(
        body, name=name, grid=(s // tm,),
        in_specs=[row, row, vec, vec] + dep_specs, out_specs=[row, row],
        out_shape=[jax.ShapeDtypeStruct((s, d), F32), jax.ShapeDtypeStruct((s, d), BF16)],
        compiler_params=_params("parallel"),
    )(x, o, g, g_next, *dep_args)


def final_fwd_bwd(x1, o, g, target, name, tm=512):
    s, d = x1.shape
    n = s // tm

    def body(x_ref, o_ref, g_ref, t_ref, loss_ref, gx_ref, do_ref, dg_ref, lacc, gacc):
        i = pl.program_id(0)

        @pl.when(i == 0)
        def _():
            lacc[...] = jnp.zeros_like(lacc)
            gacc[...] = jnp.zeros_like(gacc)

        ov = o_ref[...].astype(F32)
        gv = g_ref[...]
        r = lax.rsqrt(jnp.mean(ov * ov, axis=-1, keepdims=True) + EPS)
        oh = ov * r
        diff = x_ref[...] + oh * gv - t_ref[...]
        lacc[...] += _rowsum8(diff * diff)
        gx = diff * (1.0 / d)
        gx_ref[...] = gx
        gacc[...] += _rowsum8(gx * oh)
        dn = gx * gv
        do_ref[...] = (r * (dn - oh * jnp.mean(dn * oh, axis=-1, keepdims=True))).astype(BF16)

        @pl.when(i == n - 1)
        def _():
            tot = jnp.sum(jnp.sum(lacc[...], axis=0, keepdims=True), axis=1, keepdims=True)
            loss_ref[...] = jnp.broadcast_to(tot * (0.5 / d), loss_ref.shape)
            dg_ref[...] = jnp.sum(gacc[...], axis=0, keepdims=True)

    row = pl.BlockSpec((tm, d), lambda i: (i, 0))
    vec = pl.BlockSpec((1, d), lambda i: (0, 0))
    return pl.pallas_call(
        body, name=name, grid=(n,),
        in_specs=[row, row, vec, row],
        out_specs=[pl.BlockSpec((8, 128), lambda i: (0, 0)), row, row, vec],
        out_shape=[jax.ShapeDtypeStruct((8, 128), F32), jax.ShapeDtypeStruct((s, d), F32),
                   jax.ShapeDtypeStruct((s, d), BF16), jax.ShapeDtypeStruct((1, d), F32)],
        scratch_shapes=[pltpu.VMEM((8, d), F32), pltpu.VMEM((8, d), F32)],
        compiler_params=_params("arbitrary"),
    )(x1, o, g, target)


def _rms_bwd_rows(dyv, xv, gv):
    r = lax.rsqrt(jnp.mean(xv * xv, axis=-1, keepdims=True) + EPS)
    xh = xv * r
    dn = dyv * gv
    return r * (dn - xh * jnp.mean(dn * xh, axis=-1, keepdims=True)), _rowsum8(dyv * xh)


def norm_bwd(dy, inp, g, resid, name, inp2=None, g2=None, tm=256, dep=None):
    s, d = inp.shape
    n = s // tm
    chain = inp2 is not None

    def body(*refs):
        dy_ref, x_ref, g_ref, r_ref = refs[:4]
        outs = refs[-6:] if chain else refs[-3:]
        i = pl.program_id(0)

        @pl.when(i == 0)
        def _():
            for acc in outs[-2:] if chain else outs[-1:]:
                acc[...] = jnp.zeros_like(acc)

        if chain:
            x2_ref, g2_ref = refs[4:6]
            dx_ref, dg_ref, dx2_ref, dg2_ref, gacc, gacc2 = outs
        else:
            dx_ref, dg_ref, gacc = outs
        dx, dg_rows = _rms_bwd_rows(dy_ref[...].astype(F32), x_ref[...], g_ref[...])
        dx = dx + r_ref[...]
        dx_ref[...] = dx
        gacc[...] += dg_rows
        if chain:
            dx2, dg2_rows = _rms_bwd_rows(dx, x2_ref[...].astype(F32), g2_ref[...])
            dx2_ref[...] = dx2.astype(BF16)
            gacc2[...] += dg2_rows

        @pl.when(i == n - 1)
        def _():
            dg_ref[...] = jnp.sum(gacc[...], axis=0, keepdims=True)
            if chain:
                dg2_ref[...] = jnp.sum(gacc2[...], axis=0, keepdims=True)

    row = pl.BlockSpec((tm, d), lambda i: (i, 0))
    vec = pl.BlockSpec((1, d), lambda i: (0, 0))
    dep_args, dep_specs = _after(dep)
    extra = [inp2, g2] if chain else []
    return pl.pallas_call(
        body, name=name, grid=(n,),
        in_specs=[row, row, vec, row] + ([row, vec] if chain else []) + dep_specs,
        out_specs=[row, vec] * (2 if chain else 1),
        out_shape=[jax.ShapeDtypeStruct((s, d), F32), jax.ShapeDtypeStruct((1, d), F32)]
        + ([jax.ShapeDtypeStruct((s, d), BF16), jax.ShapeDtypeStruct((1, d), F32)] if chain else []),
        scratch_shapes=[pltpu.VMEM((8, d), F32)] * (2 if chain else 1),
        compiler_params=_params("arbitrary"),
    )(dy, inp, g, resid, *extra, *dep_args)


def _after(dep):
    if dep is None:
        return [], []
    return [dep], [pl.BlockSpec((8, 128), lambda *_: (0, 0))]


def _lane_concat(ref, count):
    return ref[0] if count == 1 else jnp.concatenate([ref[i] for i in range(count)], axis=1)


def mm_nn(a, w, out_dtype, name, tm=2048, tn=None, dep=None, group=1):
    m, k = a.shape
    tm = min(tm, m)
    ns, _, n = w.shape
    tn = n if tn is None else tn
    nj = n // tn
    assert group == 1 or nj == 1
    dep_args, dep_specs = _after(dep)

    def body(a_ref, w_ref, *rest):
        o_ref = rest[-1]
        o_ref[...] = jnp.dot(a_ref[...], _lane_concat(w_ref, group), preferred_element_type=F32).astype(out_dtype)

    return pl.pallas_call(
        body, name=name, grid=(ns // group, nj, m // tm),
        in_specs=[pl.BlockSpec((tm, k), lambda s, j, i: (i, 0)),
                  pl.BlockSpec((group, k, tn), lambda s, j, i: (s, 0, j))] + dep_specs,
        out_specs=pl.BlockSpec((tm, group * tn), lambda s, j, i: (i, s * nj + j)),
        out_shape=jax.ShapeDtypeStruct((m, ns * n), out_dtype),
        compiler_params=_params("parallel", "parallel", "parallel"),
    )(a, w, *dep_args)


def mm_nt(a, w, out_dtype, name, tm=1024, tn=None, dep=None, group=1):
    m = a.shape[0]
    tm = min(tm, m)
    ns, k, n = w.shape
    tn = n if tn is None else tn
    nj = n // tn
    assert group == 1 or nj == 1
    steps = ns * nj // group
    dep_args, dep_specs = _after(dep)

    def body(a_ref, w_ref, *rest):
        o_ref, acc = rest[-2:]
        r = pl.program_id(1)

        @pl.when(r == 0)
        def _():
            acc[...] = jnp.zeros_like(acc)

        acc[...] += lax.dot_general(a_ref[...], _lane_concat(w_ref, group), (((1,), (1,)), ((), ())),
                                    preferred_element_type=F32)

        @pl.when(r == steps - 1)
        def _():
            o_ref[...] = acc[...].astype(out_dtype)

    return pl.pallas_call(
        body, name=name, grid=(m // tm, steps),
        in_specs=[pl.BlockSpec((tm, group * tn), lambda i, r: (i, r)),
                  pl.BlockSpec((group, k, tn), lambda i, r: (r // nj, 0, r % nj))] + dep_specs,
        out_specs=pl.BlockSpec((tm, k), lambda i, r: (i, 0)),
        out_shape=jax.ShapeDtypeStruct((m, k), out_dtype),
        scratch_shapes=[pltpu.VMEM((tm, k), F32)],
        compiler_params=_params("parallel", "arbitrary"),
    )(a, w, *dep_args)


def mm_tn(a, b, ns, out_dtype, name, tk=1024, tm=2048, dep=None, pick=None, group=1):
    m, k = a.shape
    tm = min(tm, m)
    step, offset = (1, None) if pick is None else pick
    assert group == 1 or pick is None
    n = b.shape[1] // (ns * step)
    steps = m // tm
    dep_args, dep_specs = _after(dep)
    n_pre = 0 if pick is None else 1

    def b_block(s, j, r, *pre):
        return (r, s if pick is None else step * s + pre[0][0])

    def body(*refs):
        a_ref, b_ref = refs[n_pre:n_pre + 2]
        o_ref, acc = refs[-2:]
        r = pl.program_id(2)

        @pl.when(r == 0)
        def _():
            acc[...] = jnp.zeros_like(acc)

        acc[...] += lax.dot_general(a_ref[...], b_ref[...], (((0,), (0,)), ((), ())),
                                    preferred_element_type=F32)

        @pl.when(r == steps - 1)
        def _():
            for i in range(group):
                o_ref[i] = acc[:, i * n:(i + 1) * n].astype(out_dtype)

    return pl.pallas_call(
        body, name=name,
        grid_spec=pltpu.PrefetchScalarGridSpec(
            num_scalar_prefetch=n_pre, grid=(ns // group, k // tk, steps),
            in_specs=[pl.BlockSpec((tm, tk), lambda s, j, r, *pre: (r, j)),
                      pl.BlockSpec((tm, group * n), b_block)] + dep_specs,
            out_specs=pl.BlockSpec((group, tk, n), lambda s, j, r, *pre: (s, j, 0)),
            scratch_shapes=[pltpu.VMEM((tk, group * n), F32)]),
        out_shape=jax.ShapeDtypeStruct((ns, k, n), out_dtype),
        compiler_params=_params("parallel", "parallel", "arbitrary"),
    )(*([] if pick is None else [offset]), a, b, *dep_args)


SB_BLK = 128


LOG2E = 1.0 / math.log(2.0)


def _split_dot(v, tri2):
    hi = pltpu.bitcast(pltpu.bitcast(v, jnp.uint32) & jnp.uint32(0xFFFF0000), F32)
    lo = (v - hi).astype(BF16)
    return jnp.dot(jnp.concatenate([hi.astype(BF16), lo], axis=1), tri2, preferred_element_type=F32)


def _sb_scores(z2, lim, dcol, tri_ex, masked):
    sp = jnp.log2(1.0 + jnp.exp2(-jnp.abs(z2)))
    lb = jnp.minimum(z2, 0.0) - sp
    l1m = lb - z2
    mask = None
    if masked:
        mask = dcol < lim
        l1m = jnp.where(mask, l1m, 0.0)
    return mask, lb, l1m, _split_dot(l1m, tri_ex)


def _sb_consts():
    row = lax.broadcasted_iota(jnp.int32, (SB_BLK, SB_BLK), 0)
    col = lax.broadcasted_iota(jnp.int32, (SB_BLK, SB_BLK), 1)
    tri_ex = jnp.where(row > col, 1.0, 0.0).astype(BF16)
    tri_in = jnp.where(row >= col, 1.0, 0.0).astype(BF16)
    return col - row, jnp.concatenate([tri_ex, tri_ex], axis=0), jnp.concatenate([tri_in, tri_in], axis=0)


def sb_fwd(p, n_heads, name, tq=1024, nsub=8, dep=None):
    s = p.shape[0]
    h_n = n_heads
    b = SB_BLK
    nqs = tq // b
    tk = nsub * b
    scale = 1.0 / math.sqrt(HEAD_DIM)

    dep_args, dep_specs = _after(dep)

    def body(q_ref, k_ref, v_ref, *rest):
        o_ref, w_ref = rest[-2:]
        qi = pl.program_id(1)
        dcol, tri_ex, _ = _sb_consts()
        qv = [q_ref[qs * b:(qs + 1) * b, :] for qs in range(nqs)]
        n_groups = ((qi + 1) * nqs - 1) // nsub + 1

        def step(it, carry, masked):
            c1s, accs = carry
            g = n_groups - 1 - it
            off = pl.multiple_of(g * tk, tk)
            kg = k_ref[pl.ds(off, tk), :]
            vg = v_ref[pl.ds(off, tk), :]
            new_c1, new_acc = [], []
            for qs in range(nqs):
                qb = qi * nqs + qs
                square = masked and nqs == nsub
                nk = qs + 1 if square else nsub
                kq, vq = kg[:nk * b], vg[:nk * b]
                z2 = lax.dot_general(qv[qs], kq, (((1,), (1,)), ((), ())),
                                     preferred_element_type=F32) * (scale * LOG2E)
                blocks = [_sb_scores(z2[:, j * b:(j + 1) * b], (qb - (g * nsub + j)) * b, dcol, tri_ex,
                                     masked and (j == qs or not square)) for j in range(nk)]
                run = c1s[qs]
                ws = [None] * nk
                for j in reversed(range(nk)):
                    mask, lb, l1m, ls_loc = blocks[j]
                    wj = jnp.exp2(lb + ls_loc + run)
                    ws[j] = (wj if mask is None else jnp.where(mask, wj, 0.0)).astype(BF16)
                    run = run + jnp.sum(l1m, axis=1, keepdims=True)
                w = jnp.concatenate(ws, axis=1)
                w_ref[0, g, qs * b:(qs + 1) * b, 0:nk * b] = w
                new_acc.append(accs[qs] + jnp.dot(w, vq, preferred_element_type=F32))
                new_c1.append(run)
            return tuple(new_c1), tuple(new_acc)

        init = (tuple(jnp.zeros((b, 1), F32) for _ in range(nqs)),
                tuple(jnp.zeros((b, HEAD_DIM), F32) for _ in range(nqs)))
        assert all(((i + 1) * nqs - 1) // nsub * nsub <= i * nqs for i in range(s // tq))
        first = step(0, init, True)
        _, accs = lax.fori_loop(1, n_groups, functools.partial(step, masked=False), first)
        for qs in range(nqs):
            o_ref[qs * b:(qs + 1) * b, :] = accs[qs]

    return pl.pallas_call(
        body, name=name, grid=(h_n, s // tq),
        in_specs=[pl.BlockSpec((tq, HEAD_DIM), lambda h, i: (i, h)),
                  pl.BlockSpec((s, HEAD_DIM), lambda h, i: (0, h_n + h)),
                  pl.BlockSpec((s, HEAD_DIM), lambda h, i: (0, 2 * h_n + h))] + dep_specs,
        out_specs=[pl.BlockSpec((tq, HEAD_DIM), lambda h, i: (i, h)),
                   pl.BlockSpec((1, s // tk, tq, tk), lambda h, i: (h, 0, i, 0))],
        out_shape=[jax.ShapeDtypeStruct((s, h_n * HEAD_DIM), F32),
                   jax.ShapeDtypeStruct((h_n, s // tk, s, tk), BF16)],
        compiler_params=_params("parallel", "arbitrary"),
    )(p, p, p, *dep_args)


def sb_bwd(p, a, wts, da, n_heads, name, tq=1024, dep=None):
    s = p.shape[0]
    h_n = n_heads
    nq = s // tq
    b = SB_BLK
    nqs = tq // b
    tk = wts.shape[3]
    nsub = tk // b
    scale = 1.0 / math.sqrt(HEAD_DIM)
    dep_args, dep_specs = _after(dep)

    def body(q_ref, k_ref, v_ref, a_ref, da_ref, w_ref, *rest):
        dq_ref, dk_ref, dv_ref, dk_acc, dv_acc = rest[-5:]
        qi = pl.program_id(1)

        @pl.when(qi == 0)
        def _():
            dk_acc[...] = jnp.zeros_like(dk_acc)
            dv_acc[...] = jnp.zeros_like(dv_acc)

        dcol, _, tri_in = _sb_consts()
        q_all = q_ref[...]
        do_all = da_ref[...]
        qv = [q_ref[qs * b:(qs + 1) * b, :] for qs in range(nqs)]
        dov = [da_ref[qs * b:(qs + 1) * b, :] for qs in range(nqs)]
        tots = [jnp.sum(dov[qs].astype(F32) * a_ref[qs * b:(qs + 1) * b, :], axis=1, keepdims=True)
                for qs in range(nqs)]
        n_groups = ((qi + 1) * nqs - 1) // nsub + 1

        def step(it, carry, masked):
            c2s, dqs = carry
            g = n_groups - 1 - it
            off = pl.multiple_of(g * tk, tk)
            kg = k_ref[pl.ds(off, tk), :]
            vg = v_ref[pl.ds(off, tk), :]
            square = masked and nqs == nsub
            new_c2, new_dq, dz_rows, w_rows = [], [], [], []
            for qs in range(nqs):
                qb = qi * nqs + qs
                nk = qs + 1 if square else nsub
                kq, vq = kg[:nk * b], vg[:nk * b]
                z2 = lax.dot_general(qv[qs], kq, (((1,), (1,)), ((), ())),
                                     preferred_element_type=F32) * (-scale * LOG2E)
                dw = lax.dot_general(dov[qs], vq, (((1,), (1,)), ((), ())), preferred_element_type=F32)
                beta = 1.0 / (1.0 + jnp.exp2(z2))
                wq = w_ref[0, g, qs * b:(qs + 1) * b, 0:nk * b]
                e = dw * wq.astype(F32)
                run2 = c2s[qs]
                dzs = [None] * nk
                for j in reversed(range(nk)):
                    cols = slice(j * b, (j + 1) * b)
                    later = _split_dot(e[:, cols], tri_in) + run2
                    bj = beta[:, cols]
                    dz = (e[:, cols] * (1.0 - bj) - bj * (tots[qs] - later)) * scale
                    if masked and (j == qs or not square):
                        dz = jnp.where(dcol < (qb - (g * nsub + j)) * b, dz, 0.0)
                    dzs[j] = dz.astype(BF16)
                    run2 = run2 + jnp.sum(e[:, cols], axis=1, keepdims=True)
                dzq = jnp.concatenate(dzs, axis=1)
                new_dq.append(dqs[qs] + jnp.dot(dzq, kq, preferred_element_type=F32))
                new_c2.append(run2)
                pad = [jnp.zeros((b, (nsub - nk) * b), BF16)] if nk < nsub else []
                dz_rows.append(jnp.concatenate([dzq] + pad, axis=1))
                w_rows.append(jnp.concatenate([wq] + pad, axis=1))
            dz_all = jnp.concatenate(dz_rows, axis=0)
            w_all = jnp.concatenate(w_rows, axis=0)
            dk_acc[pl.ds(off, tk), :] += lax.dot_general(dz_all, q_all, (((0,), (0,)), ((), ())),
                                                         preferred_element_type=F32)
            dv_acc[pl.ds(off, tk), :] += lax.dot_general(w_all, do_all, (((0,), (0,)), ((), ())),
                                                         preferred_element_type=F32)
            return tuple(new_c2), tuple(new_dq)

        zeros = tuple(jnp.zeros((b, 1), F32) for _ in range(nqs))
        assert all(((i + 1) * nqs - 1) // nsub * nsub <= i * nqs for i in range(s // tq))
        first = step(0, (zeros, tuple(jnp.zeros((b, HEAD_DIM), F32) for _ in range(nqs))), True)
        _, dqs = lax.fori_loop(1, n_groups, functools.partial(step, masked=False), first)
        for qs in range(nqs):
            dq_ref[qs * b:(qs + 1) * b, :] = dqs[qs].astype(BF16)

        @pl.when(qi == nq - 1)
        def _():
            dk_ref[...] = dk_acc[...].astype(BF16)
            dv_ref[...] = dv_acc[...].astype(BF16)

    blk = pl.BlockSpec((tq, HEAD_DIM), lambda h, i: (i, h))
    full = pl.BlockSpec((s, HEAD_DIM), lambda h, i: (0, h))
    return pl.pallas_call(
        body, name=name, grid=(h_n, nq),
        in_specs=[blk, pl.BlockSpec((s, HEAD_DIM), lambda h, i: (0, h_n + h)),
                  pl.BlockSpec((s, HEAD_DIM), lambda h, i: (0, 2 * h_n + h)), blk, blk,
                  pl.BlockSpec((1, s // tk, tq, tk), lambda h, i: (h, 0, i, 0))] + dep_specs,
        out_specs=[blk, full, full],
        out_shape=[jax.ShapeDtypeStruct((s, h_n * HEAD_DIM), BF16)] * 3,
        scratch_shapes=[pltpu.VMEM((s, HEAD_DIM), F32), pltpu.VMEM((s, HEAD_DIM), F32)],
        compiler_params=_params("parallel", "arbitrary"),
    )(p, p, p, a, da, wts, *dep_args)


def _pool_window(xx, win, r0, rc):
    cur = xx[HALO:HALO + rc]
    ws = _window_sum(xx, win, True)[HALO:HALO + rc]
    t_idx = r0 + lax.broadcasted_iota(jnp.int32, (rc, 1), 0)
    inv = 1.0 / jnp.minimum(win, t_idx + 1).astype(F32)
    return ws * inv - cur, inv


def even_mix_fwd(a, p, pool_w, pool_scale, name, rc=512, dep=None):
    s = p.shape[0]
    ng = len(POOL_WINDOWS)
    cw = pool_w.shape[1]
    n_chunks = s // rc
    dep_args, dep_specs = _after(dep)

    def body(a_ref, u_ref, g_ref, w_ref, sc_ref, *rest):
        y_ref, upad = rest[-2:]
        j = pl.program_id(0)

        @pl.when(j < ng)
        def _():
            def chunk(ci, carry):
                rows = pl.ds(pl.multiple_of(ci * rc, rc), rc)
                y_ref[rows, :] = (a_ref[rows, :] * _silu(g_ref[rows, :].astype(F32))).astype(BF16)
                return carry

            lax.fori_loop(0, n_chunks, chunk, 0)

        for gi, win in enumerate(POOL_WINDOWS):
            @pl.when(j == ng + gi)
            def _(win=win):
                upad[0:HALO, :] = jnp.zeros((HALO, cw), F32)

                def fill(ci, carry):
                    r0 = pl.multiple_of(ci * rc, rc)
                    upad[pl.ds(pl.multiple_of(r0 + HALO, HALO), rc), :] = u_ref[pl.ds(r0, rc), :].astype(F32)
                    return carry

                lax.fori_loop(0, n_chunks, fill, 0)

                def chunk(ci, carry):
                    r0 = pl.multiple_of(ci * rc, rc)
                    rows = pl.ds(r0, rc)
                    pooled, _ = _pool_window(upad[pl.ds(r0, HALO + rc), :], win, r0, rc)
                    t = jnp.dot(pooled.astype(BF16), w_ref[0], preferred_element_type=F32)
                    y_ref[rows, :] = (t * sc_ref[...] * _silu(g_ref[rows, :].astype(F32))).astype(BF16)
                    return carry

                lax.fori_loop(0, n_chunks, chunk, 0)

    grp = lambda j: jnp.maximum(j - ng, 0)
    return pl.pallas_call(
        body, name=name, grid=(2 * ng,),
        in_specs=[pl.BlockSpec((s, cw), lambda j: (0, jnp.minimum(j, ng - 1))),
                  pl.BlockSpec((s, cw), lambda j: (0, 3 * ng + grp(j))),
                  pl.BlockSpec((s, cw), lambda j: (0, 4 * ng + j)),
                  pl.BlockSpec((1, cw, cw), lambda j: (grp(j), 0, 0)),
                  pl.BlockSpec((1, cw), lambda j: (0, grp(j)))] + dep_specs,
        out_specs=pl.BlockSpec((s, cw), lambda j: (0, j)),
        out_shape=jax.ShapeDtypeStruct((s, 2 * ng * cw), BF16),
        scratch_shapes=[pltpu.VMEM((HALO + s, cw), F32)],
        compiler_params=_params("arbitrary"),
    )(a, p, p, pool_w, pool_scale, *dep_args)


def even_mix_bwd(dy, a, p, pool_w, pool_scale, name, rc=512):
    s = p.shape[0]
    ng = len(POOL_WINDOWS)
    cw = pool_w.shape[1]
    n_chunks = s // rc

    def body(dy_ref, a_ref, u_ref, g_ref, w_ref, sc_ref, da_ref, du_ref, dg_ref, dw_ref, dsc_ref,
             upad, rpad, dpl, dw_acc, dsc_acc):
        j = pl.program_id(0)

        @pl.when(j < ng)
        def _():
            def chunk(ci, carry):
                rows = pl.ds(pl.multiple_of(ci * rc, rc), rc)
                dyv = dy_ref[rows, :].astype(F32)
                sg, dsg = _silu_and_grad(g_ref[rows, :].astype(F32))
                da_ref[rows, :] = (dyv * sg).astype(BF16)
                dg_ref[rows, :] = (dyv * a_ref[rows, :] * dsg).astype(BF16)
                return carry

            lax.fori_loop(0, n_chunks, chunk, 0)

        for gi, win in enumerate(POOL_WINDOWS):
            @pl.when(j == ng + gi)
            def _(win=win):
                upad[0:HALO, :] = jnp.zeros((HALO, cw), F32)
                rpad[s:s + HALO, :] = jnp.zeros((HALO, cw), F32)
                dw_acc[...] = jnp.zeros_like(dw_acc)
                dsc_acc[...] = jnp.zeros_like(dsc_acc)

                def fill(ci, carry):
                    r0 = pl.multiple_of(ci * rc, rc)
                    upad[pl.ds(pl.multiple_of(r0 + HALO, HALO), rc), :] = u_ref[pl.ds(r0, rc), :].astype(F32)
                    return carry

                lax.fori_loop(0, n_chunks, fill, 0)

                def chunk(ci, carry):
                    r0 = pl.multiple_of(ci * rc, rc)
                    rows = pl.ds(r0, rc)
                    pooled, inv = _pool_window(upad[pl.ds(r0, HALO + rc), :], win, r0, rc)
                    pb = pooled.astype(BF16)
                    wv = w_ref[0]
                    t = jnp.dot(pb, wv, preferred_element_type=F32)
                    scv = sc_ref[...]
                    dyv = dy_ref[rows, :].astype(F32)
                    sg, dsg = _silu_and_grad(g_ref[rows, :].astype(F32))
                    dpo = dyv * sg
                    dg_ref[rows, :] = (dyv * t * scv * dsg).astype(BF16)
                    dsc_acc[...] += _rowsum8(dpo * t)
                    dtb = (dpo * scv).astype(BF16)
                    dw_acc[...] += lax.dot_general(pb, dtb, (((0,), (0,)), ((), ())),
                                                   preferred_element_type=F32)
                    dpooled = lax.dot_general(dtb, wv, (((1,), (1,)), ((), ())),
                                              preferred_element_type=F32)
                    dpl[rows, :] = dpooled
                    rpad[rows, :] = dpooled * inv
                    return carry

                lax.fori_loop(0, n_chunks, chunk, 0)

                def chunk2(ci, carry):
                    r0 = pl.multiple_of(ci * rc, rc)
                    rows = pl.ds(r0, rc)
                    xx = rpad[pl.ds(r0, rc + HALO), :]
                    fs = _window_sum(xx, win, False)[0:rc]
                    du_ref[rows, :] = (fs - dpl[rows, :]).astype(BF16)
                    return carry

                lax.fori_loop(0, n_chunks, chunk2, 0)
                dw_ref[0] = dw_acc[...]
                dsc_ref[...] = jnp.sum(dsc_acc[...], axis=0, keepdims=True)

    grp = lambda j: jnp.maximum(j - ng, 0)
    att = lambda j: jnp.minimum(j, ng - 1)
    return pl.pallas_call(
        body, name=name, grid=(2 * ng,),
        in_specs=[pl.BlockSpec((s, cw), lambda j: (0, j)),
                  pl.BlockSpec((s, cw), lambda j: (0, att(j))),
                  pl.BlockSpec((s, cw), lambda j: (0, 3 * ng + grp(j))),
                  pl.BlockSpec((s, cw), lambda j: (0, 4 * ng + j)),
                  pl.BlockSpec((1, cw, cw), lambda j: (grp(j), 0, 0)),
                  pl.BlockSpec((1, cw), lambda j: (0, grp(j)))],
        out_specs=[pl.BlockSpec((s, cw), lambda j: (0, att(j))),
                   pl.BlockSpec((s, cw), lambda j: (0, grp(j))),
                   pl.BlockSpec((s, cw), lambda j: (0, j)),
                   pl.BlockSpec((1, cw, cw), lambda j: (grp(j), 0, 0)),
                   pl.BlockSpec((1, cw), lambda j: (0, grp(j)))],
        out_shape=[jax.ShapeDtypeStruct((s, ng * cw), BF16), jax.ShapeDtypeStruct((s, ng * cw), BF16),
                   jax.ShapeDtypeStruct((s, 2 * ng * cw), BF16),
                   jax.ShapeDtypeStruct((ng, cw, cw), F32), jax.ShapeDtypeStruct((1, ng * cw), F32)],
        scratch_shapes=[pltpu.VMEM((HALO + s, cw), F32), pltpu.VMEM((s + HALO, cw), F32),
                        pltpu.VMEM((s, cw), F32), pltpu.VMEM((cw, cw), F32), pltpu.VMEM((8, cw), F32)],
        compiler_params=_params("arbitrary"),
    )(dy, a, p, p, pool_w, pool_scale)


def _halo_before(tm):
    return lambda i: jnp.maximum(i * (tm // HALO) - 1, 0)


def _halo_after(tm, s):
    return lambda i: jnp.minimum((i + 1) * (tm // HALO), s // HALO - 1)


def odd_mix_fwd(p, sconv_w, dconv_w, dconv_b, cnorm_g, cnorm_b, name, tm=128, dep=None):
    s = p.shape[0]
    cw = sconv_w.shape[1]
    n = s // tm
    lanes = 128
    hb = _halo_before(tm)

    dep_args, dep_specs = _after(dep)

    def body(hc_ref, hch_ref, bc_ref, cc_ref, cch_ref, ga_ref, gah_ref, gb_ref, gbh_ref, g1_ref, g2_ref,
             sw_ref, dw_ref, db_ref, gam_ref, bet_ref, *rest):
        y_ref, dc_ref = rest[-2:]
        first = pl.program_id(0) == 0
        for l in range(cw // lanes):
            cols = slice(l * lanes, (l + 1) * lanes)
            mh = jnp.where(first, 0.0, cch_ref[:, cols].astype(F32) * hch_ref[:, cols].astype(F32))
            mm = cc_ref[:, cols].astype(F32) * hc_ref[:, cols].astype(F32)
            xx = jnp.concatenate([mh, mm], axis=0)
            tap = _Taps(xx, tm, True)
            cv = jnp.zeros((tm, lanes), F32)
            for k in range(SCONV_K):
                cv = cv + sw_ref[k:k + 1, cols] * tap(SCONV_K - 1 - k)
            c_out = bc_ref[:, cols].astype(F32) * cv
            y_ref[:, cols] = (c_out * _silu(g1_ref[:, cols].astype(F32))).astype(BF16)
            dh = jnp.where(first, 0.0, gah_ref[:, cols].astype(F32) * _sigmoid(gbh_ref[:, cols].astype(F32)))
            dm = ga_ref[:, cols].astype(F32) * _sigmoid(gb_ref[:, cols].astype(F32))
            xx = jnp.concatenate([dh, dm], axis=0)
            tap = _Taps(xx, tm, True)
            acc = jnp.zeros((tm, lanes), F32) + db_ref[:, cols]
            for k in range(CONF_K):
                acc = acc + dw_ref[k:k + 1, cols] * tap(CONF_K - 1 - k)
            dc_ref[:, cols] = acc
        rs = 64
        for r in range(tm // rs):
            rows = slice(r * rs, (r + 1) * rs)
            xv = dc_ref[rows, :]
            mu = jnp.mean(xv, axis=-1, keepdims=True)
            xc = xv - mu
            rstd = lax.rsqrt(jnp.mean(xc * xc, axis=-1, keepdims=True) + EPS)
            ln = xc * rstd * gam_ref[...] + bet_ref[...]
            y_ref[rows, cw:2 * cw] = (_silu(ln) * _silu(g2_ref[rows, :].astype(F32))).astype(BF16)

    main = lambda c: pl.BlockSpec((tm, cw), lambda i: (i, c))
    halo = lambda c: pl.BlockSpec((HALO, cw), lambda i: (hb(i), c))
    vec = lambda r: pl.BlockSpec((r, cw), lambda i: (0, 0))
    return pl.pallas_call(
        body, name=name, grid=(n,),
        in_specs=[main(0), halo(0), main(1), main(2), halo(2), main(3), halo(3), main(4), halo(4),
                  main(5), main(6), vec(SCONV_K), vec(CONF_K), vec(1), vec(1), vec(1)] + dep_specs,
        out_specs=[pl.BlockSpec((tm, 2 * cw), lambda i: (i, 0)), pl.BlockSpec((tm, cw), lambda i: (i, 0))],
        out_shape=[jax.ShapeDtypeStruct((s, 2 * cw), BF16), jax.ShapeDtypeStruct((s, cw), F32)],
        compiler_params=_params("parallel"),
    )(p, p, p, p, p, p, p, p, p, p, p, sconv_w, dconv_w, dconv_b, cnorm_g, cnorm_b, *dep_args)


def odd_bwd_ln(dy, p, dc, cnorm_g, cnorm_b, name, tm=256):
    s = p.shape[0]
    cw = dc.shape[1]
    n = s // tm
    rs = 128

    def body(dy_ref, g2_ref, dc_ref, gam_ref, bet_ref, ddc_ref, dg_ref, dgam_ref, dbet_ref, gacc, bacc):
        i = pl.program_id(0)

        @pl.when(i == 0)
        def _():
            gacc[...] = jnp.zeros_like(gacc)
            bacc[...] = jnp.zeros_like(bacc)

        def chunk(ci, carry):
            rows = pl.ds(pl.multiple_of(ci * rs, rs), rs)
            xv = dc_ref[rows, :]
            mu = jnp.mean(xv, axis=-1, keepdims=True)
            xc = xv - mu
            rstd = lax.rsqrt(jnp.mean(xc * xc, axis=-1, keepdims=True) + EPS)
            xh = xc * rstd
            gam = gam_ref[...]
            sl, dsl = _silu_and_grad(xh * gam + bet_ref[...])
            sg, dsg = _silu_and_grad(g2_ref[rows, :].astype(F32))
            dyv = dy_ref[rows, :].astype(F32)
            dg_ref[rows, :] = (dyv * sl * dsg).astype(BF16)
            dln = dyv * sg * dsl
            gacc[...] += _rowsum8(dln * xh)
            bacc[...] += _rowsum8(dln)
            dxh = dln * gam
            ddc_ref[rows, :] = rstd * (dxh - jnp.mean(dxh, axis=-1, keepdims=True)
                                       - xh * jnp.mean(dxh * xh, axis=-1, keepdims=True))
            return carry

        lax.fori_loop(0, tm // rs, chunk, 0)

        @pl.when(i == n - 1)
        def _():
            dgam_ref[...] = jnp.sum(gacc[...], axis=0, keepdims=True)
            dbet_ref[...] = jnp.sum(bacc[...], axis=0, keepdims=True)

    vec = pl.BlockSpec((1, cw), lambda i: (0, 0))
    return pl.pallas_call(
        body, name=name, grid=(n,),
        in_specs=[pl.BlockSpec((tm, cw), lambda i: (i, 1)), pl.BlockSpec((tm, cw), lambda i: (i, 6)),
                  pl.BlockSpec((tm, cw), lambda i: (i, 0)), vec, vec],
        out_specs=[pl.BlockSpec((tm, cw), lambda i: (i, 0)), pl.BlockSpec((tm, cw), lambda i: (i, 0)), vec, vec],
        out_shape=[jax.ShapeDtypeStruct((s, cw), F32), jax.ShapeDtypeStruct((s, cw), BF16),
                   jax.ShapeDtypeStruct((1, cw), F32), jax.ShapeDtypeStruct((1, cw), F32)],
        scratch_shapes=[pltpu.VMEM((8, cw), F32), pltpu.VMEM((8, cw), F32)],
        compiler_params=_params("arbitrary"),
    )(dy, p, dc, cnorm_g, cnorm_b)


def odd_bwd_conv(dy, p, ddc, dg2, sconv_w, dconv_w, name, tm=128):
    s = p.shape[0]
    cw = ddc.shape[1]
    n = s // tm
    lanes = 128
    hb = _halo_before(tm)
    ha = _halo_after(tm, s)

    def body(dy_ref, dya_ref, g1_ref, g1a_ref, bc_ref, bca_ref, hc_ref, hch_ref, cc_ref, cch_ref,
             ddc_ref, ddca_ref, ga_ref, gah_ref, gb_ref, gbh_ref, dg2_ref, sw_ref, dw_ref,
             dp_ref, dsw_ref, ddw_ref, ddb_ref, sw_acc, dw_acc, db_acc):
        i = pl.program_id(0)
        first = i == 0
        last = i == n - 1

        @pl.when(first)
        def _():
            sw_acc[...] = jnp.zeros_like(sw_acc)
            dw_acc[...] = jnp.zeros_like(dw_acc)
            db_acc[...] = jnp.zeros_like(db_acc)

        for l in range(cw // lanes):
            cols = slice(l * lanes, (l + 1) * lanes)
            mh = jnp.where(first, 0.0, cch_ref[:, cols].astype(F32) * hch_ref[:, cols].astype(F32))
            hcv = hc_ref[:, cols].astype(F32)
            ccv = cc_ref[:, cols].astype(F32)
            xx = jnp.concatenate([mh, ccv * hcv], axis=0)
            tap = _Taps(xx, tm, True)
            taps = [tap(SCONV_K - 1 - k) for k in range(SCONV_K)]
            cv = jnp.zeros((tm, lanes), F32)
            for k in range(SCONV_K):
                cv = cv + sw_ref[k:k + 1, cols] * taps[k]
            bcv = bc_ref[:, cols].astype(F32)
            dyv = dy_ref[:, cols].astype(F32)
            sg, dsg = _silu_and_grad(g1_ref[:, cols].astype(F32))
            dco = dyv * sg
            dp_ref[:, 5 * cw + l * lanes:5 * cw + (l + 1) * lanes] = (dyv * bcv * cv * dsg).astype(BF16)
            dp_ref[:, cw + l * lanes:cw + (l + 1) * lanes] = (dco * cv).astype(BF16)
            dcv = dco * bcv
            for k in range(SCONV_K):
                sw_acc[k * 8:(k + 1) * 8, cols] += _rowsum8(dcv * taps[k])
            dcv_a = jnp.where(last, 0.0, dya_ref[:, cols].astype(F32) * _silu(g1a_ref[:, cols].astype(F32))
                              * bca_ref[:, cols].astype(F32))
            xx = jnp.concatenate([dcv, dcv_a], axis=0)
            tap = _Taps(xx, tm, False)
            dm = jnp.zeros((tm, lanes), F32)
            for k in range(SCONV_K):
                dm = dm + sw_ref[k:k + 1, cols] * tap(SCONV_K - 1 - k)
            dp_ref[:, l * lanes:(l + 1) * lanes] = (dm * ccv).astype(BF16)
            dp_ref[:, 2 * cw + l * lanes:2 * cw + (l + 1) * lanes] = (dm * hcv).astype(BF16)
            gav = ga_ref[:, cols].astype(F32)
            sb = _sigmoid(gb_ref[:, cols].astype(F32))
            dh = jnp.where(first, 0.0, gah_ref[:, cols].astype(F32) * _sigmoid(gbh_ref[:, cols].astype(F32)))
            xx = jnp.concatenate([dh, gav * sb], axis=0)
            ddcv = ddc_ref[:, cols]
            db_acc[:, cols] += _rowsum8(ddcv)
            tap = _Taps(xx, tm, True)
            for k in range(CONF_K):
                dw_acc[k * 8:(k + 1) * 8, cols] += _rowsum8(ddcv * tap(CONF_K - 1 - k))
            ddc_a = jnp.where(last, 0.0, ddca_ref[:, cols])
            xx = jnp.concatenate([ddcv, ddc_a], axis=0)
            tap = _Taps(xx, tm, False)
            dgl = jnp.zeros((tm, lanes), F32)
            for k in range(CONF_K):
                dgl = dgl + dw_ref[k:k + 1, cols] * tap(CONF_K - 1 - k)
            dp_ref[:, 3 * cw + l * lanes:3 * cw + (l + 1) * lanes] = (dgl * sb).astype(BF16)
            dp_ref[:, 4 * cw + l * lanes:4 * cw + (l + 1) * lanes] = (dgl * gav * sb * (1.0 - sb)).astype(BF16)
        dp_ref[:, 6 * cw:7 * cw] = dg2_ref[...]

        @pl.when(last)
        def _():
            for k in range(SCONV_K):
                dsw_ref[k:k + 1, :] = jnp.sum(sw_acc[k * 8:(k + 1) * 8, :], axis=0, keepdims=True)
            for k in range(CONF_K):
                ddw_ref[k:k + 1, :] = jnp.sum(dw_acc[k * 8:(k + 1) * 8, :], axis=0, keepdims=True)
            ddb_ref[...] = jnp.sum(db_acc[...], axis=0, keepdims=True)

    def main(c):
        return pl.BlockSpec((tm, cw), lambda i: (i, c))

    def before(c):
        return pl.BlockSpec((HALO, cw), lambda i: (hb(i), c))

    def after(c):
        return pl.BlockSpec((HALO, cw), lambda i: (ha(i), c))

    def vec(r):
        return pl.BlockSpec((r, cw), lambda i: (0, 0))

    return pl.pallas_call(
        body, name=name, grid=(n,),
        in_specs=[main(0), after(0), main(5), after(5), main(1), after(1), main(0), before(0), main(2), before(2),
                  main(0), after(0), main(3), before(3), main(4), before(4), main(0), vec(SCONV_K), vec(CONF_K)],
        out_specs=[pl.BlockSpec((tm, 7 * cw), lambda i: (i, 0)), vec(SCONV_K), vec(CONF_K), vec(1)],
        out_shape=[jax.ShapeDtypeStruct((s, 7 * cw), BF16), jax.ShapeDtypeStruct((SCONV_K, cw), F32),
                   jax.ShapeDtypeStruct((CONF_K, cw), F32), jax.ShapeDtypeStruct((1, cw), F32)],
        scratch_shapes=[pltpu.VMEM((8 * SCONV_K, cw), F32), pltpu.VMEM((8 * CONF_K, cw), F32),
                        pltpu.VMEM((8, cw), F32)],
        compiler_params=_params("arbitrary"),
    )(dy, dy, p, p, p, p, p, p, p, p, ddc, ddc, p, p, p, p, dg2, sconv_w, dconv_w)


_ANY = pl.BlockSpec(memory_space=pl.ANY)


def _place():
    return lax.axis_index("x"), lax.axis_index("y"), lax.axis_index("c")


def all_gather(arrs, name, deps=()):
    n = len(arrs)

    def body(*refs):
        ins, outs = refs[:n], refs[n + len(deps):2 * n + len(deps)]
        send_sems, recv_sems, local_sems = refs[-3:]
        x, y, c = _place()
        me, sibling = (x, y, c), (x, y, 1 - c)
        chips = [(1 - x, y), (x, 1 - y), (1 - x, 1 - y)]

        def copy(a, k, block, to, src=None):
            px, py, pc = block
            dst = outs[a].at[4 * px + 2 * py + pc]
            return pltpu.make_async_remote_copy(
                src_ref=dst if src is None else src, dst_ref=dst,
                send_sem=send_sems.at[7 * a + k], recv_sem=recv_sems.at[7 * a + k],
                device_id=to, device_id_type=MESH)

        mine = [pltpu.make_async_copy(ins[a], outs[a].at[4 * x + 2 * y + c], local_sems.at[a]) for a in range(n)]
        first = []
        for a in range(n):
            first.append(copy(a, 0, me, sibling, src=ins[a]))
            first += [copy(a, 1 + j, me, (*chip, c), src=ins[a]) for j, chip in enumerate(chips)]
        for cp in first + mine:
            cp.start()
        passed = []
        for a in range(n):
            for j, chip in enumerate(chips):
                copy(a, 1 + j, (*chip, c), me).wait_recv()
                cp = copy(a, 4 + j, (*chip, c), sibling)
                cp.start()
                passed.append(cp)
        for a in range(n):
            copy(a, 0, sibling, me).wait_recv()
            for j, chip in enumerate(chips):
                copy(a, 4 + j, (*chip, 1 - c), me).wait_recv()
        for cp in first + passed:
            cp.wait_send()
        for cp in mine:
            cp.wait()

    return pl.pallas_call(
        body, name=name,
        out_shape=[jax.ShapeDtypeStruct((N_DEV,) + a.shape, a.dtype) for a in arrs],
        in_specs=[_ANY] * (n + len(deps)), out_specs=[_ANY] * n,
        scratch_shapes=[pltpu.SemaphoreType.DMA((7 * n,)), pltpu.SemaphoreType.DMA((7 * n,)),
                        pltpu.SemaphoreType.DMA((n,))],
    )(*arrs, *deps)


def in_proj_gathered(xs, g, w_own, extras, name, tm=1024):
    s, d = xs.shape
    n = w_own.shape[1]
    tm = min(tm, s)
    arrs = [w_own] + list(extras)
    na = len(arrs)
    tr = 256

    def body(*refs):
        x_ref, g_ref, ins = refs[0], refs[1], refs[2:2 + na]
        h_out, p_ref, outs = refs[2 + na], refs[3 + na], refs[4 + na:4 + 2 * na]
        (h_ref, xbuf, wbuf, obuf, send_sems, recv_sems, load_sem, store_sems, own_sems, h_sem,
         x_sems) = refs[4 + 2 * na:]
        x, y, c = _place()
        me, sibling = (x, y, c), (x, y, 1 - c)
        x_first = c == 0
        near = (jnp.where(x_first, 1 - x, x), jnp.where(x_first, y, 1 - y))
        far = (jnp.where(x_first, x, 1 - x), jnp.where(x_first, 1 - y, y))
        diag = (1 - x, 1 - y)
        k_near, k_far = jnp.where(x_first, 1, 2), jnp.where(x_first, 2, 1)
        f_near, f_far = k_near + 3, k_far + 3

        def slot(block):
            return 4 * block[0] + 2 * block[1] + block[2]

        def copy(a, k, block, to, src=None):
            dst = outs[a].at[slot(block)]
            return pltpu.make_async_remote_copy(
                src_ref=dst if src is None else src, dst_ref=dst,
                send_sem=send_sems.at[7 * a + k], recv_sem=recv_sems.at[7 * a + k],
                device_id=to, device_id_type=MESH)

        first = []
        for a in range(na):
            first += [copy(a, 0, me, sibling, src=ins[a]), copy(a, 1, me, (1 - x, y, c), src=ins[a]),
                      copy(a, 2, me, (x, 1 - y, c), src=ins[a])]
        for cp in first:
            cp.start()
        own = pltpu.make_async_copy(wbuf.at[0], outs[0].at[slot(me)], own_sems.at[0])
        mine = [pltpu.make_async_copy(ins[a], outs[a].at[slot(me)], own_sems.at[a]) for a in range(1, na)]
        stores = [None, None]

        def x_load(i):
            return pltpu.make_async_copy(x_ref.at[pl.ds(i * tr, tr), :], xbuf.at[i % 2], x_sems.at[i % 2])

        x_load(0).start()
        for i in range(s // tr):
            if i + 1 < s // tr:
                x_load(i + 1).start()
            x_load(i).wait()
            xv = xbuf[i % 2]
            r = lax.rsqrt(jnp.mean(xv * xv, axis=-1, keepdims=True) + EPS)
            h_ref[i * tr:(i + 1) * tr, :] = (xv * r * g_ref[...]).astype(BF16)
        h_store = pltpu.make_async_copy(h_ref, h_out, h_sem)
        h_store.start()

        def multiply(k, block, w_from):
            b = k % 2
            if k == 2:
                own.wait()
            load = pltpu.make_async_copy(w_from, wbuf.at[b], load_sem)
            load.start()
            if stores[b] is not None:
                stores[b].wait()
            load.wait()
            if k == 0:
                own.start()

            def chunk(i, carry):
                rows = pl.ds(pl.multiple_of(i * tm, tm), tm)
                obuf[b, rows, :] = jnp.dot(h_ref[rows, :], wbuf[b], preferred_element_type=F32).astype(BF16)
                return carry

            lax.fori_loop(0, s // tm, chunk, 0)
            stores[b] = pltpu.make_async_copy(
                obuf.at[b], p_ref.at[:, pl.ds(pl.multiple_of(slot(block) * n, 128), n)], store_sems.at[b])
            stores[b].start()

        passed = []

        def arrive(a, k, block):
            copy(a, k, block, me).wait_recv()

        def pass_on(a, k, block, to):
            cp = copy(a, k, block, to)
            cp.start()
            passed.append(cp)

        def gather(arrays, use):
            def arrive_all(k, block):
                for a in arrays:
                    arrive(a, k, block)

            def pass_all(k, block, to):
                for a in arrays:
                    pass_on(a, k, block, to)

            use(0, me)
            arrive_all(0, sibling)
            use(1, sibling)
            arrive_all(k_near, (*near, c))
            pass_all(3, (*near, c), (*far, c))
            pass_all(f_near, (*near, c), sibling)
            use(2, (*near, c))
            arrive_all(f_far, (*far, 1 - c))
            use(3, (*far, 1 - c))
            arrive_all(k_far, (*far, c))
            pass_all(f_far, (*far, c), sibling)
            use(4, (*far, c))
            arrive_all(f_near, (*near, 1 - c))
            use(5, (*near, 1 - c))
            arrive_all(3, (*diag, c))
            pass_all(6, (*diag, c), sibling)
            use(6, (*diag, c))
            arrive_all(6, (*diag, 1 - c))
            use(7, (*diag, 1 - c))

        gather(range(na), lambda k, block: multiply(k, block, ins[0] if k == 0 else outs[0].at[slot(block)]))
        for cp in mine:
            cp.start()
        for cp in first + passed:
            cp.wait_send()
        for cp in mine + stores + [h_store]:
            cp.wait()

    vmem = pl.BlockSpec(memory_space=pltpu.VMEM)
    outs = pl.pallas_call(
        body, name=name,
        out_shape=[jax.ShapeDtypeStruct((s, d), BF16), jax.ShapeDtypeStruct((s, N_DEV * n), BF16)]
        + [jax.ShapeDtypeStruct((N_DEV,) + a.shape, a.dtype) for a in arrs],
        in_specs=[_ANY, vmem] + [_ANY] * na, out_specs=[_ANY] * (2 + na),
        scratch_shapes=[pltpu.VMEM((s, d), BF16), pltpu.VMEM((2, tr, d), F32), pltpu.VMEM((2, d, n), BF16),
                        pltpu.VMEM((2, s, n), BF16),
                        pltpu.SemaphoreType.DMA((7 * na,)), pltpu.SemaphoreType.DMA((7 * na,)),
                        pltpu.SemaphoreType.DMA, pltpu.SemaphoreType.DMA((2,)), pltpu.SemaphoreType.DMA((na,)),
                        pltpu.SemaphoreType.DMA, pltpu.SemaphoreType.DMA((2,))],
        compiler_params=pltpu.CompilerParams(vmem_limit_bytes=VMEM_LIMIT),
    )(xs, g, *arrs)
    return outs[0], outs[1], outs[2], outs[3:]


_HBM = pl.BlockSpec(memory_space=pltpu.HBM)
_SEM = pl.BlockSpec(memory_space=pltpu.SEMAPHORE)
_DATAFLOW = pltpu.SideEffectType.DATAFLOW_SIDE_EFFECTING


def _peers_per_array(kind):
    return {"sibling": 1, "halves": 1, "pass": 1, "neighbours": 2}.get(kind, 3)


def _near_far():
    x, y, c = _place()
    x_first = c == 0
    near = (jnp.where(x_first, 1 - x, x), jnp.where(x_first, y, 1 - y))
    far = (jnp.where(x_first, x, 1 - x), jnp.where(x_first, 1 - y, y))
    return near, far, jnp.where(x_first, 0, 1), jnp.where(x_first, 1, 0)


def _split_copies(kind, srcs, lands, send_sems, recv_sems):
    x, y, c = _place()
    per = _peers_per_array(kind)
    out = []
    for a in range(len(lands)):
        if kind == "sibling":
            part = srcs[a] if srcs[a].shape[1] == 1 else srcs[a].at[:, pl.ds(1 - c, 1)]
            peers = [((x, y, 1 - c), part, lands[a], lands[a])]
        elif kind == "halves":
            mine, its = lands[a].at[:, pl.ds(c, 1)], lands[a].at[:, pl.ds(1 - c, 1)]
            peers = [((x, y, 1 - c), mine, mine, its)]
        elif kind == "neighbours":
            here = lands[a].at[4 * x + 2 * y + c]
            peers = [((px, py, c), srcs[a], here, lands[a].at[4 * px + 2 * py + c])
                     for px, py in [(1 - x, y), (x, 1 - y)]]
        elif kind == "pass":
            near, far, _, _ = _near_far()
            block = lands[a].at[4 * near[0] + 2 * near[1] + c]
            peers = [((*far, c), block, block, lands[a].at[4 * (1 - x) + 2 * (1 - y) + c])]
        else:
            peers = []
            for px, py in [(1 - x, y), (x, 1 - y), (1 - x, 1 - y)]:
                if kind == "gather":
                    views = (srcs[a], lands[a].at[4 * x + 2 * y + c], lands[a].at[4 * px + 2 * py + c])
                else:
                    views = (srcs[a].at[2 * px + py], lands[a].at[2 * x + y], lands[a].at[2 * px + py])
                peers.append(((px, py, c),) + views)
        for j, (peer, src, dst, arrives) in enumerate(peers):
            sems = dict(send_sem=send_sems.at[per * a + j], recv_sem=recv_sems.at[per * a + j],
                        device_id=peer, device_id_type=MESH)
            out.append((pltpu.make_async_remote_copy(src_ref=src, dst_ref=dst, **sems),
                        pltpu.make_async_remote_copy(src_ref=src, dst_ref=arrives, **sems)))
    return out


def split_start(kind, srcs, lands, deps, name):
    ns, nl = len(srcs), len(lands)
    n_sems = _peers_per_array(kind) * nl
    held = list(srcs) + list(lands)

    def body(*refs):
        send_sems, recv_sems = refs[len(held) + len(deps)], refs[len(held) + len(deps) + 1]
        for copy, _ in _split_copies(kind, refs[:ns], refs[ns:ns + nl], send_sems, recv_sems):
            copy.start()
        token = refs[-1]
        token[...] = jnp.zeros_like(token)

    outs = pl.pallas_call(
        body, name=name,
        out_shape=(pltpu.SemaphoreType.DMA((n_sems,)), pltpu.SemaphoreType.DMA((n_sems,)),
                   *[pltpu.HBM(a.shape, a.dtype) for a in held], jax.ShapeDtypeStruct((8, 128), F32)),
        in_specs=[_HBM] * len(held) + [_ANY] * len(deps),
        out_specs=(_SEM, _SEM, *([_HBM] * len(held)), pl.BlockSpec(memory_space=pltpu.VMEM)),
        input_output_aliases={i: 2 + i for i in range(len(held))},
        compiler_params=pltpu.CompilerParams(has_side_effects=_DATAFLOW),
    )(*[pltpu.with_memory_space_constraint(a, pltpu.HBM) for a in held], *deps)
    return outs[0], outs[1], list(outs[2:2 + ns]), list(outs[2 + ns:2 + ns + nl]), outs[-1]


def split_wait(kind, send_sems, recv_sems, srcs, lands, afters, name):
    ns, nl = len(srcs), len(lands)
    held = list(srcs) + list(lands)

    def body(*refs):
        for _, arrival in _split_copies(kind, refs[:ns], refs[ns:ns + nl], refs[ns + nl], refs[ns + nl + 1]):
            arrival.wait_send()
            arrival.wait_recv()

    outs = pl.pallas_call(
        body, name=name,
        out_shape=[pltpu.HBM(a.shape, a.dtype) for a in held],
        in_specs=[_HBM] * len(held) + [_SEM, _SEM] + [_ANY] * len(afters),
        out_specs=[_HBM] * len(held),
        input_output_aliases={i: i for i in range(len(held))},
        compiler_params=pltpu.CompilerParams(has_side_effects=_DATAFLOW),
    )(*held, send_sems, recv_sems, *afters)
    return list(outs[:ns]), list(outs[ns:])


def split_pass_on(lands, first_recv, afters, name):
    n = len(lands)

    def body(*refs):
        lands_r, first = refs[:n], refs[n]
        send_sems, recv_sems = refs[n + 1 + len(afters)], refs[n + 2 + len(afters)]
        c = lax.axis_index("c")
        near, _, k_near, _ = _near_far()
        for a in range(n):
            block = lands_r[a].at[4 * near[0] + 2 * near[1] + c]
            pltpu.make_async_remote_copy(
                src_ref=block, dst_ref=block, send_sem=first.at[2 * a + k_near], recv_sem=first.at[2 * a + k_near],
                device_id=(near[0], near[1], c), device_id_type=MESH).wait_recv()
        for copy, _ in _split_copies("pass", [], lands_r, send_sems, recv_sems):
            copy.start()
        token = refs[-1]
        token[...] = jnp.zeros_like(token)

    outs = pl.pallas_call(
        body, name=name,
        out_shape=(pltpu.SemaphoreType.DMA((n,)), pltpu.SemaphoreType.DMA((n,)),
                   *[pltpu.HBM(a.shape, a.dtype) for a in lands], jax.ShapeDtypeStruct((8, 128), F32)),
        in_specs=[_HBM] * n + [_SEM] + [_ANY] * len(afters),
        out_specs=(_SEM, _SEM, *([_HBM] * n), pl.BlockSpec(memory_space=pltpu.VMEM)),
        input_output_aliases={i: 2 + i for i in range(n)},
        compiler_params=pltpu.CompilerParams(has_side_effects=_DATAFLOW),
    )(*lands, first_recv, *afters)
    return outs[0], outs[1], list(outs[2:2 + n]), outs[-1]


def split_wait_neighbours(first_send, first_recv, pass_send, pass_recv, srcs, lands, afters, name):
    n = len(lands)
    held = list(srcs) + list(lands)

    def body(*refs):
        srcs_r, lands_r = refs[:n], refs[n:2 * n]
        send1, recv1, send2, recv2 = refs[2 * n:2 * n + 4]
        c = lax.axis_index("c")
        _, far, _, k_far = _near_far()
        for _, arrival in _split_copies("neighbours", srcs_r, lands_r, send1, recv1):
            arrival.wait_send()
        for a in range(n):
            block = lands_r[a].at[4 * far[0] + 2 * far[1] + c]
            pltpu.make_async_remote_copy(
                src_ref=block, dst_ref=block, send_sem=recv1.at[2 * a + k_far], recv_sem=recv1.at[2 * a + k_far],
                device_id=(far[0], far[1], c), device_id_type=MESH).wait_recv()
        for _, arrival in _split_copies("pass", [], lands_r, send2, recv2):
            arrival.wait_send()
            arrival.wait_recv()

    outs = pl.pallas_call(
        body, name=name,
        out_shape=[pltpu.HBM(a.shape, a.dtype) for a in held],
        in_specs=[_HBM] * len(held) + [_SEM] * 4 + [_ANY] * len(afters),
        out_specs=[_HBM] * len(held),
        input_output_aliases={i: i for i in range(len(held))},
        compiler_params=pltpu.CompilerParams(has_side_effects=_DATAFLOW),
    )(*held, first_send, first_recv, pass_send, pass_recv, *afters)
    return list(outs[:n]), list(outs[n:])


def place_block(land, block, dev, name):
    r, c = block.shape
    tr = min(r, 512)

    def body(dev_ref, land_ref, b_ref, o_ref):
        del dev_ref, land_ref
        o_ref[...] = b_ref[...]

    return pl.pallas_call(
        body, name=name,
        grid_spec=pltpu.PrefetchScalarGridSpec(
            num_scalar_prefetch=1, grid=(r // tr,),
            in_specs=[_ANY, pl.BlockSpec((tr, c), lambda i, dev_ref: (i, 0))],
            out_specs=pl.BlockSpec((None, tr, c), lambda i, dev_ref: (dev_ref[0], i, 0))),
        out_shape=jax.ShapeDtypeStruct(land.shape, land.dtype),
        input_output_aliases={1: 0},
        compiler_params=_params("parallel"),
    )(dev, land, block)


def pair_add(own, recv, core, name):
    _, _, r, c = own.shape
    tr = min(r, 2048)

    def body(core_ref, own_ref, recv_ref, o_ref):
        del core_ref
        o_ref[...] = (own_ref[...].astype(F32) + recv_ref[...].astype(F32)).astype(BF16)

    return pl.pallas_call(
        body, name=name,
        grid_spec=pltpu.PrefetchScalarGridSpec(
            num_scalar_prefetch=1, grid=(4, r // tr),
            in_specs=[pl.BlockSpec((None, None, tr, c), lambda k, i, core_ref: (k, core_ref[0], i, 0)),
                      pl.BlockSpec((None, None, tr, c), lambda k, i, core_ref: (k, 0, i, 0))],
            out_specs=pl.BlockSpec((None, tr, c), lambda k, i, core_ref: (k, i, 0))),
        out_shape=jax.ShapeDtypeStruct((4, r, c), BF16),
        compiler_params=_params("parallel", "parallel"),
    )(core, own, recv)


def _adamw_math(w, g, m, v):
    m2 = ADAM_B1 * m + (1.0 - ADAM_B1) * g
    v2 = ADAM_B2 * v + (1.0 - ADAM_B2) * (g * g)
    m_hat = m2 / (1.0 - ADAM_B1 ** ADAM_STEP)
    v_hat = v2 / (1.0 - ADAM_B2 ** ADAM_STEP)
    delta = -ADAM_LR * (m_hat / (jnp.sqrt(v_hat) + ADAM_EPS) + ADAM_WD * w)
    return delta, m2, v2


def adamw_big(w, m, v, own, got, chip, name):
    r, c = w.shape
    tr = min(r, 512)

    def body(chip_ref, w_ref, m_ref, v_ref, p0, p1, p2, p3, g_ref, d_ref, m2_ref, v2_ref):
        del chip_ref
        g = ((p0[...].astype(F32) + p1[...].astype(F32)) + p2[...].astype(F32)) + p3[...].astype(F32)
        delta, m2, v2 = _adamw_math(w_ref[...], g, m_ref[...], v_ref[...])
        g_ref[...] = g
        d_ref[...] = delta
        m2_ref[...] = m2
        v2_ref[...] = v2

    row = pl.BlockSpec((tr, c), lambda i, chip_ref: (i, 0))

    def slab(flip):
        return pl.BlockSpec((None, tr, c), lambda i, chip_ref: (chip_ref[0] ^ flip, i, 0))

    return pl.pallas_call(
        body, name=name,
        grid_spec=pltpu.PrefetchScalarGridSpec(
            num_scalar_prefetch=1, grid=(r // tr,),
            in_specs=[row, row, row, slab(0), slab(1), slab(2), slab(3)],
            out_specs=[row] * 4),
        out_shape=[jax.ShapeDtypeStruct((r, c), F32)] * 4,
        compiler_params=_params("parallel"),
    )(chip, w, m, v, own, got, got, got)


def sum_devices(g8, name):
    def body(g_ref, o_ref):
        tot = g_ref[0]
        for k in range(1, N_DEV):
            tot = tot + g_ref[k]
        o_ref[...] = tot

    return pl.pallas_call(body, name=name, out_shape=jax.ShapeDtypeStruct(g8.shape[1:], F32))(g8)


def adamw_small(ws, gs, ms, vs, name):
    n = len(ws)

    def body(*refs):
        w_r, g_r, m_r, v_r = refs[:n], refs[n:2 * n], refs[2 * n:3 * n], refs[3 * n:4 * n]
        d_o, m_o, v_o = refs[4 * n:5 * n], refs[5 * n:6 * n], refs[6 * n:7 * n]
        for k in range(n):
            delta, m2, v2 = _adamw_math(w_r[k][...], g_r[k][...], m_r[k][...], v_r[k][...])
            d_o[k][...] = delta
            m_o[k][...] = m2
            v_o[k][...] = v2

    shapes = [jax.ShapeDtypeStruct(w.shape, F32) for w in ws]
    outs = pl.pallas_call(body, name=name, out_shape=shapes * 3)(*ws, *gs, *ms, *vs)
    return outs[:n], outs[n:2 * n], outs[2 * n:]


def _rows128(a):
    return a.reshape(-1, 128)


def _pad_rows(a, rows):
    return jnp.pad(a, ((0, rows - a.shape[0]), (0, 0)))


def kernel(x, ln_pre_even, w_in_even, pool_w, pool_scale, w_out_even, ln_post_even, ln_pre_odd, w_in_odd, sconv_w, dconv_w, dconv_b, cnorm_g, cnorm_b, w_out_odd, ln_post_odd, loss_target, m_ln_pre_even, m_w_in_even, m_pool_w, m_pool_scale, m_w_out_even, m_ln_post_even, m_ln_pre_odd, m_w_in_odd, m_sconv_w, m_dconv_w, m_dconv_b, m_cnorm_g, m_cnorm_b, m_w_out_odd, m_ln_post_odd, v_ln_pre_even, v_w_in_even, v_pool_w, v_pool_scale, v_w_out_even, v_ln_post_even, v_ln_pre_odd, v_w_in_odd, v_sconv_w, v_dconv_w, v_dconv_b, v_cnorm_g, v_cnorm_b, v_w_out_odd, v_ln_post_odd):
    xs = x[0]
    tgt = loss_target[0]
    s, d = xs.shape
    half = d // 2
    n_heads = half // HEAD_DIM
    ng = len(POOL_WINDOWS)
    cwp = half // ng
    dev = 4 * lax.axis_index("x") + 2 * lax.axis_index("y") + lax.axis_index("c")
    core = lax.axis_index("c").astype(jnp.int32).reshape(1)

    pr = pool_w.shape[2]
    cl = sconv_w.shape[2]
    small_parts = [(_rows128(ln_pre_odd), 8), (sconv_w[0], 8), (dconv_w[0], 32), (dconv_b, 8),
                   (cnorm_g, 8), (cnorm_b, 8), (_rows128(ln_post_odd), 8)]
    small_local = jnp.concatenate([_pad_rows(a, r) for a, r in small_parts], axis=0)
    h0, p0, g_wie, (g_pw, g_small) = in_proj_gathered(
        xs, ln_pre_even, w_in_even[0].astype(BF16), [pool_w[0].reshape(ng * pr, cwp).astype(BF16), small_local],
        "ag_in_proj_even")
    comm = _Exchanges(dev, core, d)
    token = comm.start_weights("out_even", [w_out_even[0].astype(BF16)], [p0])
    token = comm.start_weights("in_odd", [w_in_odd[0].astype(BF16)], [token], neighbours=True)
    sb_dep = comm.start_weights("out_odd", [w_out_odd[0].astype(BF16)], [token])
    pool_full = g_pw.reshape(N_DEV, ng, pr, cwp).transpose(1, 0, 2, 3).reshape(ng, cwp, cwp)
    nl = ln_pre_odd.shape[1] // 128

    def chan(lo, rows):
        return g_small[:, lo:lo + rows].transpose(1, 0, 2).reshape(rows, N_DEV * cl)

    ln_pre_odd_f = g_small[:, 0:nl].reshape(1, d)
    sconv_f = chan(8, SCONV_K)
    dconv_f = chan(16, CONF_K)
    dconv_b_f = chan(48, 1)
    cnorm_g_f = chan(56, 1)
    cnorm_b_f = chan(64, 1)
    ln_post_odd_f = g_small[:, 72:72 + nl].reshape(1, d)

    loss_blk, grad_x, small_g = _fwd_bwd(
        xs, tgt, ln_pre_even, h0, p0, g_wie, pool_full, pool_scale, ln_post_even, ln_pre_odd_f,
        sconv_f, dconv_f, dconv_b_f, cnorm_g_f, cnorm_b_f, ln_post_odd_f, comm, sb_dep)
    small_w = [ln_pre_even, pool_scale, ln_post_even, ln_pre_odd, sconv_w[0], dconv_w[0], dconv_b, cnorm_g, cnorm_b, ln_post_odd]
    small_m = [m_ln_pre_even, m_pool_scale, m_ln_post_even, m_ln_pre_odd, m_sconv_w[0], m_dconv_w[0], m_dconv_b, m_cnorm_g, m_cnorm_b, m_ln_post_odd]
    small_v = [v_ln_pre_even, v_pool_scale, v_ln_post_even, v_ln_pre_odd, v_sconv_w[0], v_dconv_w[0], v_dconv_b, v_cnorm_g, v_cnorm_b, v_ln_post_odd]
    big = {"w_in_even": (w_in_even, m_w_in_even, v_w_in_even), "pool_w": (pool_w, m_pool_w, v_pool_w),
           "w_out_even": (w_out_even, m_w_out_even, v_w_out_even), "w_in_odd": (w_in_odd, m_w_in_odd, v_w_in_odd),
           "w_out_odd": (w_out_odd, m_w_out_odd, v_w_out_odd)}
    upd = comm.finish_updates(big, [grad_x])
    upd.update(comm.finish_updates(big, [grad_x]))
    sg, sd, sm, sv, loss = _update_small(small_g, loss_blk, small_w, small_m, small_v, dev, d, cl,
                                         deps=[upd["w_in_odd"][1], upd["w_out_even"][1]])
    upd.update(comm.finish_updates(big, sd))
    (g_wie_o, d_wie, m_wie, v_wie), (g_pw_o, d_pw, m_pw, v_pw) = upd["w_in_even"], upd["pool_w"]
    (g_woe_o, d_woe, m_woe, v_woe), (g_wio_o, d_wio, m_wio, v_wio) = upd["w_out_even"], upd["w_in_odd"]
    g_woo_o, d_woo, m_woo, v_woo = upd["w_out_odd"]

    def order(small, wie, pw, woe, wio, woo):
        return [small[0], wie, pw, small[1], woe, small[2], small[3], wio, small[4], small[5], small[6],
                small[7], small[8], woo, small[9]]

    grads = order(sg, g_wie_o, g_pw_o, g_woe_o, g_wio_o, g_woo_o)
    deltas = order(sd, d_wie, d_pw, d_woe, d_wio, d_woo)
    new_m = order(sm, m_wie, m_pw, m_woe, m_wio, m_woo)
    new_v = order(sv, v_wie, v_pw, v_woe, v_wio, v_woo)
    return (loss, grad_x[None], *grads, *deltas, *new_m, *new_v)


def _fwd_bwd(xs, tgt, ln_pre_even, h0, p0, g_wie, pool_full, pool_scale, ln_post_even, ln_pre_odd_f,
             sconv_f, dconv_f, dconv_b_f, cnorm_g_f, cnorm_b_f, ln_post_odd_f, comm, sb_dep):
    d = xs.shape[1]
    n_heads = d // 2 // HEAD_DIM
    ng, cwp = pool_full.shape[0], pool_full.shape[1]
    a0, sb_wts = sb_fwd(p0, n_heads, "sb_fwd", dep=sb_dep)
    dep = comm.weights_arrived("out_even", after=a0)
    y0 = even_mix_fwd(a0, p0, pool_full, pool_scale, "even_mix_fwd", dep=dep)
    (w_out_e,) = comm.weights("out_even", after=y0)
    w_out_e = w_out_e.reshape(1, d, d)
    dep = comm.weights_pass_on("in_odd", after=w_out_e)
    o0 = mm_nn(y0, w_out_e, BF16, "out_proj_even", tn=512, dep=dep)
    dep = comm.weights_arrived("in_odd", after=o0)
    x1, h1 = postnorm_fwd(xs, o0, ln_post_even, ln_pre_odd_f, "post_even", dep=dep)
    (g_wio,) = comm.weights("in_odd", after=x1)
    p1 = mm_nn(h1, g_wio, BF16, "in_proj_odd", group=2)
    dep = comm.weights_arrived("out_odd", after=p1)
    y1, dc = odd_mix_fwd(p1, sconv_f, dconv_f, dconv_b_f, cnorm_g_f, cnorm_b_f, "odd_mix_fwd", dep=dep)
    (w_out_o,) = comm.weights("out_odd", after=y1)
    w_out_o = w_out_o.reshape(1, d, d)
    o1 = mm_nn(y1, w_out_o, BF16, "out_proj_odd", tn=512)
    loss_blk, gx2, do1, dg_post_odd = final_fwd_bwd(x1, o1, ln_post_odd_f, tgt, "post_odd_loss")

    dw_out_o = mm_tn(y1, do1, 1, BF16, "dw_out_odd")
    dy1 = mm_nt(do1, w_out_o, BF16, "dy_odd")
    ddc, dg2, dgam, dbet = odd_bwd_ln(dy1, p1, dc, cnorm_g_f, cnorm_b_f, "odd_bwd_ln")
    dp1, dsconv, ddconv, ddconv_b = odd_bwd_conv(dy1, p1, ddc, dg2, sconv_f, dconv_f, "odd_bwd_conv")
    dw_in_o = mm_tn(h1, dp1, N_DEV, BF16, "dw_in_odd", group=2)
    dep = comm.reduce_begin({"w_out_odd": dw_out_o.reshape(N_DEV, d // N_DEV, d), "w_in_odd": dw_in_o}, "odd")
    dh1 = mm_nt(dp1, g_wio, BF16, "dh_odd", dep=dep, group=2)
    dep = comm.reduce_send(after=dh1)
    gx1, dg_pre_odd, do0, dg_post_even = norm_bwd(dh1, x1, ln_pre_odd_f, gx2, "pre_odd_post_even_bwd",
                                                  inp2=o0, g2=ln_post_even, dep=dep)

    dw_out_e = mm_tn(y0, do0, 1, BF16, "dw_out_even")
    dy0 = mm_nt(do0, w_out_e, BF16, "dy_even")
    da0, du0, dg0, dpool, dpool_scale = even_mix_bwd(dy0, a0, p0, pool_full, pool_scale, "even_mix_bwd")
    pr = cwp // N_DEV
    dpool_slabs = dpool.astype(BF16).reshape(ng, N_DEV, pr, cwp).transpose(1, 0, 2, 3).reshape(N_DEV, ng * pr, cwp)
    dep = comm.reduce_begin({"w_out_even": dw_out_e.reshape(N_DEV, d // N_DEV, d), "pool_w": dpool_slabs}, "even_out")
    dq0, dk0, dv0 = sb_bwd(p0, a0, sb_wts, da0, n_heads, "sb_bwd", dep=dep)
    dep = comm.reduce_send(after=dq0)
    dp0 = jnp.concatenate([dq0, dk0, dv0, du0, dg0], axis=1)
    dw_sibling = mm_tn(h0, dp0, N_DEV // 2, BF16, "dw_in_even_sibling", dep=dep, pick=(2, 1 - comm.core))
    dep = comm.reduce_begin({"w_in_even": dw_sibling}, "even_in", sibling_part=True)
    dw_own = mm_tn(h0, dp0, N_DEV // 2, BF16, "dw_in_even_own", dep=dep, pick=(2, comm.core))
    dep = comm.reduce_send(after=dw_own, own_part={"w_in_even": dw_own})
    dh0 = mm_nt(dp0, g_wie, BF16, "dh_even", dep=dep, group=2)
    dep = None
    grad_x, dg_pre_even = norm_bwd(dh0, xs, ln_pre_even, gx1, "pre_even_bwd", tm=512, dep=dep)
    small_g = [dg_pre_even, dpool_scale, dg_post_even, dg_pre_odd, dsconv, ddconv, ddconv_b, dgam, dbet, dg_post_odd]
    return loss_blk, grad_x, small_g


class _Exchanges:
    def __init__(self, dev, core, d):
        self.dev = dev.astype(jnp.int32).reshape(1)
        self.core = core
        self.chip = (dev // 2).astype(jnp.int32).reshape(1)
        self.d = d
        self.in_flight = {}
        self.to_sibling = None
        self.pending = []

    def start_weights(self, tag, blocks, afters, neighbours=False):
        lands = [lax.empty((N_DEV,) + b.shape, b.dtype) for b in blocks]
        kind = "neighbours" if neighbours else "gather"
        send, recv, srcs, lands, token = split_start(kind, blocks, lands, afters, "ag_start_" + tag)
        self.in_flight[tag] = (send, recv, srcs, lands)
        return token

    def weights_pass_on(self, tag, after):
        send, recv, srcs, lands = self.in_flight.pop(tag)
        send2, recv2, lands, token = split_pass_on(lands, recv, [after], "ag_pass_on_" + tag)
        self.in_flight[tag] = (send, recv, srcs, lands, send2, recv2)
        return token

    def weights_arrived(self, tag, after):
        entry = self.in_flight.pop(tag)
        send, recv, srcs, lands = entry[:4]
        if len(entry) == 6:
            srcs, lands = split_wait_neighbours(send, recv, entry[4], entry[5], srcs, lands, [after], "ag_wait_" + tag)
        else:
            srcs, lands = split_wait("gather", send, recv, srcs, lands, [after], "ag_wait_" + tag)
        lands = [place_block(l, b, self.dev, "ag_own_%s_%d" % (tag, k)) for k, (l, b) in enumerate(zip(lands, srcs))]
        lands = [l.reshape((4, 2) + l.shape[1:]) for l in lands]
        send, recv, _, lands, token = split_start("halves", [], lands, [], "ag_sibling_start_" + tag)
        self.in_flight[tag] = (send, recv, lands)
        return token

    def weights(self, tag, after):
        send, recv, lands = self.in_flight.pop(tag)
        _, lands = split_wait("halves", send, recv, [], lands, [after], "ag_sibling_wait_" + tag)
        return [l.reshape((N_DEV,) + l.shape[2:]) for l in lands]

    def reduce_begin(self, partials, tag, sibling_part=False):
        names = list(partials)
        arrs = [partials[k].reshape((4, 1 if sibling_part else 2) + partials[k].shape[1:]) for k in names]
        lands = [lax.empty((4, 1) + a.shape[2:], a.dtype) for a in arrs]
        send, recv, srcs, lands, token = split_start("sibling", arrs, lands, [], "rs_sibling_start_" + tag)
        self.to_sibling = (tag, names, send, recv, srcs, lands)
        return token

    def reduce_send(self, after, own_part=None):
        tag, names, send, recv, srcs, lands = self.to_sibling
        srcs, lands = split_wait("sibling", send, recv, srcs, lands, [after], "rs_sibling_wait_" + tag)
        which = self.core
        if own_part is not None:
            srcs = [own_part[k].reshape((4, 1) + own_part[k].shape[1:]) for k in names]
            which = jnp.zeros((1,), jnp.int32)
        sums = [pair_add(o, r, which, "rs_pair_add_" + k) for k, o, r in zip(names, srcs, lands)]
        zones = [lax.empty(a.shape, a.dtype) for a in sums]
        send, recv, srcs, zones, token = split_start("scatter", sums, zones, [], "rs_start_" + tag)
        self.pending.append((tag, names, send, recv, srcs, zones))
        return token

    def finish_updates(self, big, afters):
        tag, names, send, recv, srcs, lands = self.pending.pop(0)
        srcs, lands = split_wait("scatter", send, recv, srcs, lands, afters, "rs_wait_" + tag)
        out = {}
        for name, own, got in zip(names, srcs, lands):
            w, m, v = big[name]
            shp = own.shape[1:]
            outs = adamw_big(w.reshape(shp), m.reshape(shp), v.reshape(shp), own, got, self.chip, "adamw_" + name)
            out[name] = [o.reshape(w.shape) for o in outs]
        return out


def _update_small(small_g, loss_blk, small_w, small_m, small_v, dev, d, cl, deps):
    packed = jnp.concatenate([_rows128(g) for g in small_g] + [loss_blk], axis=0)
    (g8,) = all_gather([packed], "ag_small_grads", deps)
    tot = sum_devices(g8, "sum_small_grads")
    loss = tot[packed.shape[0] - 8, 0]
    full_g = []
    lo = 0
    for g in small_g:
        rows = g.size // 128
        full_g.append(tot[lo:lo + rows].reshape(g.shape))
        lo += rows

    def mine(g, width):
        return lax.dynamic_slice_in_dim(g, dev * width, width, axis=g.ndim - 1)

    fg = full_g
    small_gl = [fg[0], fg[1], fg[2], mine(fg[3], d // N_DEV), mine(fg[4], cl), mine(fg[5], cl), mine(fg[6], cl),
                mine(fg[7], cl), mine(fg[8], cl), mine(fg[9], d // N_DEV)]
    sd, sm, sv = adamw_small(small_w, small_gl, small_m, small_v, "adamw_small")

    def like(k, a):
        return a[None] if k in (4, 5) else a

    sg = [like(k, a) for k, a in enumerate(small_gl)]
    sd = [like(k, a) for k, a in enumerate(sd)]
    sm = [like(k, a) for k, a in enumerate(sm)]
    sv = [like(k, a) for k, a in enumerate(sv)]
    return sg, sd, sm, sv, loss
```

```python
import functools
import math

import jax
import jax.numpy as jnp
from jax import lax
from jax.experimental import pallas as pl
from jax.experimental.pallas import tpu as pltpu

F32 = jnp.float32
BF16 = jnp.bfloat16
EPS = 1e-6
HEAD_DIM = 128
POOL_WINDOWS = (2, 4, 8, 16)
SCONV_K = 3
CONF_K = 31
HALO = 32
N_DEV = 8
VMEM_LIMIT = 56 * 1024 * 1024
MESH = pl.DeviceIdType.MESH

ADAM_LR = 0.001
ADAM_B1 = 0.9
ADAM_B2 = 0.999
ADAM_EPS = 1e-08
ADAM_WD = 0.01
ADAM_STEP = 10


def _params(*sem):
    return pltpu.CompilerParams(dimension_semantics=sem, vmem_limit_bytes=VMEM_LIMIT)


def _sigmoid(v):
    return 1.0 / (1.0 + jnp.exp(-v))


def _silu(v):
    return v * _sigmoid(v)


def _silu_and_grad(v):
    s = _sigmoid(v)
    return v * s, s * (1.0 + v * (1.0 - s))


def _rowsum8(v):
    r, c = v.shape
    return jnp.sum(v.reshape(r // 8, 8, c), axis=0)


SUBLANES = 8


class _Taps:
    def __init__(self, xx, rows, before):
        self.xx, self.rows, self.before, self.rotated = xx, rows, before, {}

    def __call__(self, i):
        r, q = i % SUBLANES, i // SUBLANES
        if r not in self.rotated:
            n = self.xx.shape[0]
            self.rotated[r] = self.xx if r == 0 else pltpu.roll(self.xx, r if self.before else n - r, 0)
        lo = HALO - SUBLANES * q if self.before else SUBLANES * q
        return self.rotated[r][lo:lo + self.rows]


def _window_sum(xx, win, before):
    n = xx.shape[0]
    acc = xx
    k = 1
    while k < win:
        acc = acc + pltpu.roll(acc, k if before else n - k, 0)
        k *= 2
    return acc


def postnorm_fwd(x, o, g, g_next, name, tm=512, dep=None):
    s, d = x.shape
    dep_args, dep_specs = _after(dep)

    def body(x_ref, o_ref, g_ref, gn_ref, *rest):
        y_ref, h_ref = rest[-2:]
        ov = o_ref[...].astype(F32)
        r = lax.rsqrt(jnp.mean(ov * ov, axis=-1, keepdims=True) + EPS)
        y = x_ref[...] + ov * r * g_ref[...]
        y_ref[...] = y
        r2 = lax.rsqrt(jnp.mean(y * y, axis=-1, keepdims=True) + EPS)
        h_ref[...] = (y * r2 * gn_ref[...]).astype(BF16)

    row = pl.BlockSpec((tm, d), lambda i: (i, 0))
    vec = pl.BlockSpec((1, d), lambda i: (0, 0))
    return pl.pallas_call(
        body, name=name, grid=(s // tm,),
        in_specs=[row, row, vec, vec] + dep_specs, out_specs=[row, row],
        out_shape=[jax.ShapeDtypeStruct((s, d), F32), jax.ShapeDtypeStruct((s, d), BF16)],
        compiler_params=_params("parallel"),
    )(x, o, g, g_next, *dep_args)


def final_fwd_bwd(x1, o, g, target, name, tm=512):
    s, d = x1.shape
    n = s // tm

    def body(x_ref, o_ref, g_ref, t_ref, loss_ref, gx_ref, do_ref, dg_ref, lacc, gacc):
        i = pl.program_id(0)

        @pl.when(i == 0)
        def _():
            lacc[...] = jnp.zeros_like(lacc)
            gacc[...] = jnp.zeros_like(gacc)

        ov = o_ref[...].astype(F32)
        gv = g_ref[...]
        r = lax.rsqrt(jnp.mean(ov * ov, axis=-1, keepdims=True) + EPS)
        oh = ov * r
        diff = x_ref[...] + oh * gv - t_ref[...]
        lacc[...] += _rowsum8(diff * diff)
        gx = diff * (1.0 / d)
        gx_ref[...] = gx
        gacc[...] += _rowsum8(gx * oh)
        dn = gx * gv
        do_ref[...] = (r * (dn - oh * jnp.mean(dn * oh, axis=-1, keepdims=True))).astype(BF16)

        @pl.when(i == n - 1)
        def _():
            tot = jnp.sum(jnp.sum(lacc[...], axis=0, keepdims=True), axis=1, keepdims=True)
            loss_ref[...] = jnp.broadcast_to(tot * (0.5 / d), loss_ref.shape)
            dg_ref[...] = jnp.sum(gacc[...], axis=0, keepdims=True)

    row = pl.BlockSpec((tm, d), lambda i: (i, 0))
    vec = pl.BlockSpec((1, d), lambda i: (0, 0))
    return pl.pallas_call(
        body, name=name, grid=(n,),
        in_specs=[row, row, vec, row],
        out_specs=[pl.BlockSpec((8, 128), lambda i: (0, 0)), row, row, vec],
        out_shape=[jax.ShapeDtypeStruct((8, 128), F32), jax.ShapeDtypeStruct((s, d), F32),
                   jax.ShapeDtypeStruct((s, d), BF16), jax.ShapeDtypeStruct((1, d), F32)],
        scratch_shapes=[pltpu.VMEM((8, d), F32), pltpu.VMEM((8, d), F32)],
        compiler_params=_params("arbitrary"),
    )(x1, o, g, target)


def _rms_bwd_rows(dyv, xv, gv):
    r = lax.rsqrt(jnp.mean(xv * xv, axis=-1, keepdims=True) + EPS)
    xh = xv * r
    dn = dyv * gv
    return r * (dn - xh * jnp.mean(dn * xh, axis=-1, keepdims=True)), _rowsum8(dyv * xh)


def norm_bwd(dy, inp, g, resid, name, inp2=None, g2=None, tm=256, dep=None):
    s, d = inp.shape
    n = s // tm
    chain = inp2 is not None

    def body(*refs):
        dy_ref, x_ref, g_ref, r_ref = refs[:4]
        outs = refs[-6:] if chain else refs[-3:]
        i = pl.program_id(0)

        @pl.when(i == 0)
        def _():
            for acc in outs[-2:] if chain else outs[-1:]:
                acc[...] = jnp.zeros_like(acc)

        if chain:
            x2_ref, g2_ref = refs[4:6]
            dx_ref, dg_ref, dx2_ref, dg2_ref, gacc, gacc2 = outs
        else:
            dx_ref, dg_ref, gacc = outs
        dx, dg_rows = _rms_bwd_rows(dy_ref[...].astype(F32), x_ref[...], g_ref[...])
        dx = dx + r_ref[...]
        dx_ref[...] = dx
        gacc[...] += dg_rows
        if chain:
            dx2, dg2_rows = _rms_bwd_rows(dx, x2_ref[...].astype(F32), g2_ref[...])
            dx2_ref[...] = dx2.astype(BF16)
            gacc2[...] += dg2_rows

        @pl.when(i == n - 1)
        def _():
            dg_ref[...] = jnp.sum(gacc[...], axis=0, keepdims=True)
            if chain:
                dg2_ref[...] = jnp.sum(gacc2[...], axis=0, keepdims=True)

    row = pl.BlockSpec((tm, d), lambda i: (i, 0))
    vec = pl.BlockSpec((1, d), lambda i: (0, 0))
    dep_args, dep_specs = _after(dep)
    extra = [inp2, g2] if chain else []
    return pl.pallas_call(
        body, name=name, grid=(n,),
        in_specs=[row, row, vec, row] + ([row, vec] if chain else []) + dep_specs,
        out_specs=[row, vec] * (2 if chain else 1),
        out_shape=[jax.ShapeDtypeStruct((s, d), F32), jax.ShapeDtypeStruct((1, d), F32)]
        + ([jax.ShapeDtypeStruct((s, d), BF16), jax.ShapeDtypeStruct((1, d), F32)] if chain else []),
        scratch_shapes=[pltpu.VMEM((8, d), F32)] * (2 if chain else 1),
        compiler_params=_params("arbitrary"),
    )(dy, inp, g, resid, *extra, *dep_args)


def _after(dep):
    if dep is None:
        return [], []
    return [dep], [pl.BlockSpec((8, 128), lambda *_: (0, 0))]


def _lane_concat(ref, count):
    return ref[0] if count == 1 else jnp.concatenate([ref[i] for i in range(count)], axis=1)


def mm_nn(a, w, out_dtype, name, tm=2048, tn=None, dep=None, group=1):
    m, k = a.shape
    tm = min(tm, m)
    ns, _, n = w.shape
    tn = n if tn is None else tn
    nj = n // tn
    assert group == 1 or nj == 1
    dep_args, dep_specs = _after(dep)

    def body(a_ref, w_ref, *rest):
        o_ref = rest[-1]
        o_ref[...] = jnp.dot(a_ref[...], _lane_concat(w_ref, group), preferred_element_type=F32).astype(out_dtype)

    return pl.pallas_call(
        body, name=name, grid=(ns // group, nj, m // tm),
        in_specs=[pl.BlockSpec((tm, k), lambda s, j, i: (i, 0)),
                  pl.BlockSpec((group, k, tn), lambda s, j, i: (s, 0, j))] + dep_specs,
        out_specs=pl.BlockSpec((tm, group * tn), lambda s, j, i: (i, s * nj + j)),
        out_shape=jax.ShapeDtypeStruct((m, ns * n), out_dtype),
        compiler_params=_params("parallel", "parallel", "parallel"),
    )(a, w, *dep_args)


def mm_nt(a, w, out_dtype, name, tm=1024, tn=None, dep=None, group=1):
    m = a.shape[0]
    tm = min(tm, m)
    ns, k, n = w.shape
    tn = n if tn is None else tn
    nj = n // tn
    assert group == 1 or nj == 1
    steps = ns * nj // group
    dep_args, dep_specs = _after(dep)

    def body(a_ref, w_ref, *rest):
        o_ref, acc = rest[-2:]
        r = pl.program_id(1)

        @pl.when(r == 0)
        def _():
            acc[...] = jnp.zeros_like(acc)

        acc[...] += lax.dot_general(a_ref[...], _lane_concat(w_ref, group), (((1,), (1,)), ((), ())),
                                    preferred_element_type=F32)

        @pl.when(r == steps - 1)
        def _():
            o_ref[...] = acc[...].astype(out_dtype)

    return pl.pallas_call(
        body, name=name, grid=(m // tm, steps),
        in_specs=[pl.BlockSpec((tm, group * tn), lambda i, r: (i, r)),
                  pl.BlockSpec((group, k, tn), lambda i, r: (r // nj, 0, r % nj))] + dep_specs,
        out_specs=pl.BlockSpec((tm, k), lambda i, r: (i, 0)),
        out_shape=jax.ShapeDtypeStruct((m, k), out_dtype),
        scratch_shapes=[pltpu.VMEM((tm, k), F32)],
        compiler_params=_params("parallel", "arbitrary"),
    )(a, w, *dep_args)


def mm_tn(a, b, ns, out_dtype, name, tk=1024, tm=2048, dep=None, pick=None, group=1):
    m, k = a.shape
    tm = min(tm, m)
    step, offset = (1, None) if pick is None else pick
    assert group == 1 or pick is None
    n = b.shape[1] // (ns * step)
    steps = m // tm
    dep_args, dep_specs = _after(dep)
    n_pre = 0 if pick is None else 1

    def b_block(s, j, r, *pre):
        return (r, s if pick is None else step * s + pre[0][0])

    def body(*refs):
        a_ref, b_ref = refs[n_pre:n_pre + 2]
        o_ref, acc = refs[-2:]
        r = pl.program_id(2)

        @pl.when(r == 0)
        def _():
            acc[...] = jnp.zeros_like(acc)

        acc[...] += lax.dot_general(a_ref[...], b_ref[...], (((0,), (0,)), ((), ())),
                                    preferred_element_type=F32)

        @pl.when(r == steps - 1)
        def _():
            for i in range(group):
                o_ref[i] = acc[:, i * n:(i + 1) * n].astype(out_dtype)

    return pl.pallas_call(
        body, name=name,
        grid_spec=pltpu.PrefetchScalarGridSpec(
            num_scalar_prefetch=n_pre, grid=(ns // group, k // tk, steps),
            in_specs=[pl.BlockSpec((tm, tk), lambda s, j, r, *pre: (r, j)),
                      pl.BlockSpec((tm, group * n), b_block)] + dep_specs,
            out_specs=pl.BlockSpec((group, tk, n), lambda s, j, r, *pre: (s, j, 0)),
            scratch_shapes=[pltpu.VMEM((tk, group * n), F32)]),
        out_shape=jax.ShapeDtypeStruct((ns, k, n), out_dtype),
        compiler_params=_params("parallel", "parallel", "arbitrary"),
    )(*([] if pick is None else [offset]), a, b, *dep_args)


SB_BLK = 128


LOG2E = 1.0 / math.log(2.0)


def _split_dot(v, tri2):
    hi = pltpu.bitcast(pltpu.bitcast(v, jnp.uint32) & jnp.uint32(0xFFFF0000), F32)
    lo = (v - hi).astype(BF16)
    return jnp.dot(jnp.concatenate([hi.astype(BF16), lo], axis=1), tri2, preferred_element_type=F32)


def _sb_scores(z2, lim, dcol, tri_ex, masked):
    sp = jnp.log2(1.0 + jnp.exp2(-jnp.abs(z2)))
    lb = jnp.minimum(z2, 0.0) - sp
    l1m = lb - z2
    mask = None
    if masked:
        mask = dcol < lim
        l1m = jnp.where(mask, l1m, 0.0)
    return mask, lb, l1m, _split_dot(l1m, tri_ex)


def _sb_consts():
    row = lax.broadcasted_iota(jnp.int32, (SB_BLK, SB_BLK), 0)
    col = lax.broadcasted_iota(jnp.int32, (SB_BLK, SB_BLK), 1)
    tri_ex = jnp.where(row > col, 1.0, 0.0).astype(BF16)
    tri_in = jnp.where(row >= col, 1.0, 0.0).astype(BF16)
    return col - row, jnp.concatenate([tri_ex, tri_ex], axis=0), jnp.concatenate([tri_in, tri_in], axis=0)


def sb_fwd(p, n_heads, name, tq=1024, nsub=8, dep=None):
    s = p.shape[0]
    h_n = n_heads
    b = SB_BLK
    nqs = tq // b
    tk = nsub * b
    scale = 1.0 / math.sqrt(HEAD_DIM)

    dep_args, dep_specs = _after(dep)

    def body(q_ref, k_ref, v_ref, *rest):
        o_ref, w_ref = rest[-2:]
        qi = pl.program_id(1)
        dcol, tri_ex, _ = _sb_consts()
        qv = [q_ref[qs * b:(qs + 1) * b, :] for qs in range(nqs)]
        n_groups = ((qi + 1) * nqs - 1) // nsub + 1

        def step(it, carry, masked):
            c1s, accs = carry
            g = n_groups - 1 - it
            off = pl.multiple_of(g * tk, tk)
            kg = k_ref[pl.ds(off, tk), :]
            vg = v_ref[pl.ds(off, tk), :]
            new_c1, new_acc = [], []
            for qs in range(nqs):
                qb = qi * nqs + qs
                square = masked and nqs == nsub
                nk = qs + 1 if square else nsub
                kq, vq = kg[:nk * b], vg[:nk * b]
                z2 = lax.dot_general(qv[qs], kq, (((1,), (1,)), ((), ())),
                                     preferred_element_type=F32) * (scale * LOG2E)
                blocks = [_sb_scores(z2[:, j * b:(j + 1) * b], (qb - (g * nsub + j)) * b, dcol, tri_ex,
                                     masked and (j == qs or not square)) for j in range(nk)]
                run = c1s[qs]
                ws = [None] * nk
                for j in reversed(range(nk)):
                    mask, lb, l1m, ls_loc = blocks[j]
                    wj = jnp.exp2(lb + ls_loc + run)
                    ws[j] = (wj if mask is None else jnp.where(mask, wj, 0.0)).astype(BF16)
                    run = run + jnp.sum(l1m, axis=1, keepdims=True)
                w = jnp.concatenate(ws, axis=1)
                w_ref[0, g, qs * b:(qs + 1) * b, 0:nk * b] = w
                new_acc.append(accs[qs] + jnp.dot(w, vq, preferred_element_type=F32))
                new_c1.append(run)
            return tuple(new_c1), tuple(new_acc)

        init = (tuple(jnp.zeros((b, 1), F32) for _ in range(nqs)),
                tuple(jnp.zeros((b, HEAD_DIM), F32) for _ in range(nqs)))
        assert all(((i + 1) * nqs - 1) // nsub * nsub <= i * nqs for i in range(s // tq))
        first = step(0, init, True)
        _, accs = lax.fori_loop(1, n_groups, functools.partial(step, masked=False), first)
        for qs in range(nqs):
            o_ref[qs * b:(qs + 1) * b, :] = accs[qs]

    return pl.pallas_call(
        body, name=name, grid=(h_n, s // tq),
        in_specs=[pl.BlockSpec((tq, HEAD_DIM), lambda h, i: (i, h)),
                  pl.BlockSpec((s, HEAD_DIM), lambda h, i: (0, h_n + h)),
                  pl.BlockSpec((s, HEAD_DIM), lambda h, i: (0, 2 * h_n + h))] + dep_specs,
        out_specs=[pl.BlockSpec((tq, HEAD_DIM), lambda h, i: (i, h)),
                   pl.BlockSpec((1, s // tk, tq, tk), lambda h, i: (h, 0, i, 0))],
        out_shape=[jax.ShapeDtypeStruct((s, h_n * HEAD_DIM), F32),
                   jax.ShapeDtypeStruct((h_n, s // tk, s, tk), BF16)],
        compiler_params=_params("parallel", "arbitrary"),
    )(p, p, p, *dep_args)


def sb_bwd(p, a, wts, da, n_heads, name, tq=1024, dep=None):
    s = p.shape[0]
    h_n = n_heads
    nq = s // tq
    b = SB_BLK
    nqs = tq // b
    tk = wts.shape[3]
    nsub = tk // b
    scale = 1.0 / math.sqrt(HEAD_DIM)
    dep_args, dep_specs = _after(dep)

    def body(q_ref, k_ref, v_ref, a_ref, da_ref, w_ref, *rest):
        dq_ref, dk_ref, dv_ref, dk_acc, dv_acc = rest[-5:]
        qi = pl.program_id(1)

        @pl.when(qi == 0)
        def _():
            dk_acc[...] = jnp.zeros_like(dk_acc)
            dv_acc[...] = jnp.zeros_like(dv_acc)

        dcol, _, tri_in = _sb_consts()
        q_all = q_ref[...]
        do_all = da_ref[...]
        qv = [q_ref[qs * b:(qs + 1) * b, :] for qs in range(nqs)]
        dov = [da_ref[qs * b:(qs + 1) * b, :] for qs in range(nqs)]
        tots = [jnp.sum(dov[qs].astype(F32) * a_ref[qs * b:(qs + 1) * b, :], axis=1, keepdims=True)
                for qs in range(nqs)]
        n_groups = ((qi + 1) * nqs - 1) // nsub + 1

        def step(it, carry, masked):
            c2s, dqs = carry
            g = n_groups - 1 - it
            off = pl.multiple_of(g * tk, tk)
            kg = k_ref[pl.ds(off, tk), :]
            vg = v_ref[pl.ds(off, tk), :]
            square = masked and nqs == nsub
            new_c2, new_dq, dz_rows, w_rows = [], [], [], []
            for qs in range(nqs):
                qb = qi * nqs + qs
                nk = qs + 1 if square else nsub
                kq, vq = kg[:nk * b], vg[:nk * b]
                z2 = lax.dot_general(qv[qs], kq, (((1,), (1,)), ((), ())),
                                     preferred_element_type=F32) * (-scale * LOG2E)
                dw = lax.dot_general(dov[qs], vq, (((1,), (1,)), ((), ())), preferred_element_type=F32)
                beta = 1.0 / (1.0 + jnp.exp2(z2))
                wq = w_ref[0, g, qs * b:(qs + 1) * b, 0:nk * b]
                e = dw * wq.astype(F32)
                run2 = c2s[qs]
                dzs = [None] * nk
                for j in reversed(range(nk)):
                    cols = slice(j * b, (j + 1) * b)
                    later = _split_dot(e[:, cols], tri_in) + run2
                    bj = beta[:, cols]
                    dz = (e[:, cols] * (1.0 - bj) - bj * (tots[qs] - later)) * scale
                    if masked and (j == qs or not square):
                        dz = jnp.where(dcol < (qb - (g * nsub + j)) * b, dz, 0.0)
                    dzs[j] = dz.astype(BF16)
                    run2 = run2 + jnp.sum(e[:, cols], axis=1, keepdims=True)
                dzq = jnp.concatenate(dzs, axis=1)
                new_dq.append(dqs[qs] + jnp.dot(dzq, kq, preferred_element_type=F32))
                new_c2.append(run2)
                pad = [jnp.zeros((b, (nsub - nk) * b), BF16)] if nk < nsub else []
                dz_rows.append(jnp.concatenate([dzq] + pad, axis=1))
                w_rows.append(jnp.concatenate([wq] + pad, axis=1))
            dz_all = jnp.concatenate(dz_rows, axis=0)
            w_all = jnp.concatenate(w_rows, axis=0)
            dk_acc[pl.ds(off, tk), :] += lax.dot_general(dz_all, q_all, (((0,), (0,)), ((), ())),
                                                         preferred_element_type=F32)
            dv_acc[pl.ds(off, tk), :] += lax.dot_general(w_all, do_all, (((0,), (0,)), ((), ())),
                                                         preferred_element_type=F32)
            return tuple(new_c2), tuple(new_dq)

        zeros = tuple(jnp.zeros((b, 1), F32) for _ in range(nqs))
        assert all(((i + 1) * nqs - 1) // nsub * nsub <= i * nqs for i in range(s // tq))
        first = step(0, (zeros, tuple(jnp.zeros((b, HEAD_DIM), F32) for _ in range(nqs))), True)
        _, dqs = lax.fori_loop(1, n_groups, functools.partial(step, masked=False), first)
        for qs in range(nqs):
            dq_ref[qs * b:(qs + 1) * b, :] = dqs[qs].astype(BF16)

        @pl.when(qi == nq - 1)
        def _():
            dk_ref[...] = dk_acc[...].astype(BF16)
            dv_ref[...] = dv_acc[...].astype(BF16)

    blk = pl.BlockSpec((tq, HEAD_DIM), lambda h, i: (i, h))
    full = pl.BlockSpec((s, HEAD_DIM), lambda h, i: (0, h))
    return pl.pallas_call(
        body, name=name, grid=(h_n, nq),
        in_specs=[blk, pl.BlockSpec((s, HEAD_DIM), lambda h, i: (0, h_n + h)),
                  pl.BlockSpec((s, HEAD_DIM), lambda h, i: (0, 2 * h_n + h)), blk, blk,
                  pl.BlockSpec((1, s // tk, tq, tk), lambda h, i: (h, 0, i, 0))] + dep_specs,
        out_specs=[blk, full, full],
        out_shape=[jax.ShapeDtypeStruct((s, h_n * HEAD_DIM), BF16)] * 3,
        scratch_shapes=[pltpu.VMEM((s, HEAD_DIM), F32), pltpu.VMEM((s, HEAD_DIM), F32)],
        compiler_params=_params("parallel", "arbitrary"),
    )(p, p, p, a, da, wts, *dep_args)


def _pool_window(xx, win, r0, rc):
    cur = xx[HALO:HALO + rc]
    ws = _window_sum(xx, win, True)[HALO:HALO + rc]
    t_idx = r0 + lax.broadcasted_iota(jnp.int32, (rc, 1), 0)
    inv = 1.0 / jnp.minimum(win, t_idx + 1).astype(F32)
    return ws * inv - cur, inv


def even_mix_fwd(a, p, pool_w, pool_scale, name, rc=512, dep=None):
    s = p.shape[0]
    ng = len(POOL_WINDOWS)
    cw = pool_w.shape[1]
    n_chunks = s // rc
    dep_args, dep_specs = _after(dep)

    def body(a_ref, u_ref, g_ref, w_ref, sc_ref, *rest):
        y_ref, upad = rest[-2:]
        j = pl.program_id(0)

        @pl.when(j < ng)
        def _():
            def chunk(ci, carry):
                rows = pl.ds(pl.multiple_of(ci * rc, rc), rc)
                y_ref[rows, :] = (a_ref[rows, :] * _silu(g_ref[rows, :].astype(F32))).astype(BF16)
                return carry

            lax.fori_loop(0, n_chunks, chunk, 0)

        for gi, win in enumerate(POOL_WINDOWS):
            @pl.when(j == ng + gi)
            def _(win=win):
                upad[0:HALO, :] = jnp.zeros((HALO, cw), F32)

                def fill(ci, carry):
                    r0 = pl.multiple_of(ci * rc, rc)
                    upad[pl.ds(pl.multiple_of(r0 + HALO, HALO), rc), :] = u_ref[pl.ds(r0, rc), :].astype(F32)
                    return carry

                lax.fori_loop(0, n_chunks, fill, 0)

                def chunk(ci, carry):
                    r0 = pl.multiple_of(ci * rc, rc)
                    rows = pl.ds(r0, rc)
                    pooled, _ = _pool_window(upad[pl.ds(r0, HALO + rc), :], win, r0, rc)
                    t = jnp.dot(pooled.astype(BF16), w_ref[0], preferred_element_type=F32)
                    y_ref[rows, :] = (t * sc_ref[...] * _silu(g_ref[rows, :].astype(F32))).astype(BF16)
                    return carry

                lax.fori_loop(0, n_chunks, chunk, 0)

    grp = lambda j: jnp.maximum(j - ng, 0)
    return pl.pallas_call(
        body, name=name, grid=(2 * ng,),
        in_specs=[pl.BlockSpec((s, cw), lambda j: (0, jnp.minimum(j, ng - 1))),
                  pl.BlockSpec((s, cw), lambda j: (0, 3 * ng + grp(j))),
                  pl.BlockSpec((s, cw), lambda j: (0, 4 * ng + j)),
                  pl.BlockSpec((1, cw, cw), lambda j: (grp(j), 0, 0)),
                  pl.BlockSpec((1, cw), lambda j: (0, grp(j)))] + dep_specs,
        out_specs=pl.BlockSpec((s, cw), lambda j: (0, j)),
        out_shape=jax.ShapeDtypeStruct((s, 2 * ng * cw), BF16),
        scratch_shapes=[pltpu.VMEM((HALO + s, cw), F32)],
        compiler_params=_params("arbitrary"),
    )(a, p, p, pool_w, pool_scale, *dep_args)


def even_mix_bwd(dy, a, p, pool_w, pool_scale, name, rc=512):
    s = p.shape[0]
    ng = len(POOL_WINDOWS)
    cw = pool_w.shape[1]
    n_chunks = s // rc

    def body(dy_ref, a_ref, u_ref, g_ref, w_ref, sc_ref, da_ref, du_ref, dg_ref, dw_ref, dsc_ref,
             upad, rpad, dpl, dw_acc, dsc_acc):
        j = pl.program_id(0)

        @pl.when(j < ng)
        def _():
            def chunk(ci, carry):
                rows = pl.ds(pl.multiple_of(ci * rc, rc), rc)
                dyv = dy_ref[rows, :].astype(F32)
                sg, dsg = _silu_and_grad(g_ref[rows, :].astype(F32))
                da_ref[rows, :] = (dyv * sg).astype(BF16)
                dg_ref[rows, :] = (dyv * a_ref[rows, :] * dsg).astype(BF16)
                return carry

            lax.fori_loop(0, n_chunks, chunk, 0)

        for gi, win in enumerate(POOL_WINDOWS):
            @pl.when(j == ng + gi)
            def _(win=win):
                upad[0:HALO, :] = jnp.zeros((HALO, cw), F32)
                rpad[s:s + HALO, :] = jnp.zeros((HALO, cw), F32)
                dw_acc[...] = jnp.zeros_like(dw_acc)
                dsc_acc[...] = jnp.zeros_like(dsc_acc)

                def fill(ci, carry):
                    r0 = pl.multiple_of(ci * rc, rc)
                    upad[pl.ds(pl.multiple_of(r0 + HALO, HALO), rc), :] = u_ref[pl.ds(r0, rc), :].astype(F32)
                    return carry

                lax.fori_loop(0, n_chunks, fill, 0)

                def chunk(ci, carry):
                    r0 = pl.multiple_of(ci * rc, rc)
                    rows = pl.ds(r0, rc)
                    pooled, inv = _pool_window(upad[pl.ds(r0, HALO + rc), :], win, r0, rc)
                    pb = pooled.astype(BF16)
                    wv = w_ref[0]
                    t = jnp.dot(pb, wv, preferred_element_type=F32)
                    scv = sc_ref[...]
                    dyv = dy_ref[rows, :].astype(F32)
                    sg, dsg = _silu_and_grad(g_ref[rows, :].astype(F32))
                    dpo = dyv * sg
                    dg_ref[rows, :] = (dyv * t * scv * dsg).astype(BF16)
                    dsc_acc[...] += _rowsum8(dpo * t)
                    dtb = (dpo * scv).astype(BF16)
                    dw_acc[...] += lax.dot_general(pb, dtb, (((0,), (0,)), ((), ())),
                                                   preferred_element_type=F32)
                    dpooled = lax.dot_general(dtb, wv, (((1,), (1,)), ((), ())),
                                              preferred_element_type=F32)
                    dpl[rows, :] = dpooled
                    rpad[rows, :] = dpooled * inv
                    return carry

                lax.fori_loop(0, n_chunks, chunk, 0)

                def chunk2(ci, carry):
                    r0 = pl.multiple_of(ci * rc, rc)
                    rows = pl.ds(r0, rc)
                    xx = rpad[pl.ds(r0, rc + HALO), :]
                    fs = _window_sum(xx, win, False)[0:rc]
                    du_ref[rows, :] = (fs - dpl[rows, :]).astype(BF16)
                    return carry

                lax.fori_loop(0, n_chunks, chunk2, 0)
                dw_ref[0] = dw_acc[...]
                dsc_ref[...] = jnp.sum(dsc_acc[...], axis=0, keepdims=True)

    grp = lambda j: jnp.maximum(j - ng, 0)
    att = lambda j: jnp.minimum(j, ng - 1)
    return pl.pallas_call(
        body, name=name, grid=(2 * ng,),
        in_specs=[pl.BlockSpec((s, cw), lambda j: (0, j)),
                  pl.BlockSpec((s, cw), lambda j: (0, att(j))),
                  pl.BlockSpec((s, cw), lambda j: (0, 3 * ng + grp(j))),
                  pl.BlockSpec((s, cw), lambda j: (0, 4 * ng + j)),
                  pl.BlockSpec((1, cw, cw), lambda j: (grp(j), 0, 0)),
                  pl.BlockSpec((1, cw), lambda j: (0, grp(j)))],
        out_specs=[pl.BlockSpec((s, cw), lambda j: (0, att(j))),
                   pl.BlockSpec((s, cw), lambda j: (0, grp(j))),
                   pl.BlockSpec((s, cw), lambda j: (0, j)),
                   pl.BlockSpec((1, cw, cw), lambda j: (grp(j), 0, 0)),
                   pl.BlockSpec((1, cw), lambda j: (0, grp(j)))],
        out_shape=[jax.ShapeDtypeStruct((s, ng * cw), BF16), jax.ShapeDtypeStruct((s, ng * cw), BF16),
                   jax.ShapeDtypeStruct((s, 2 * ng * cw), BF16),
                   jax.ShapeDtypeStruct((ng, cw, cw), F32), jax.ShapeDtypeStruct((1, ng * cw), F32)],
        scratch_shapes=[pltpu.VMEM((HALO + s, cw), F32), pltpu.VMEM((s + HALO, cw), F32),
                        pltpu.VMEM((s, cw), F32), pltpu.VMEM((cw, cw), F32), pltpu.VMEM((8, cw), F32)],
        compiler_params=_params("arbitrary"),
    )(dy, a, p, p, pool_w, pool_scale)


def _halo_before(tm):
    return lambda i: jnp.maximum(i * (tm // HALO) - 1, 0)


def _halo_after(tm, s):
    return lambda i: jnp.minimum((i + 1) * (tm // HALO), s // HALO - 1)


def odd_mix_fwd(p, sconv_w, dconv_w, dconv_b, cnorm_g, cnorm_b, name, tm=128, dep=None):
    s = p.shape[0]
    cw = sconv_w.shape[1]
    n = s // tm
    lanes = 128
    hb = _halo_before(tm)

    dep_args, dep_specs = _after(dep)

    def body(hc_ref, hch_ref, bc_ref, cc_ref, cch_ref, ga_ref, gah_ref, gb_ref, gbh_ref, g1_ref, g2_ref,
             sw_ref, dw_ref, db_ref, gam_ref, bet_ref, *rest):
        y_ref, dc_ref = rest[-2:]
        first = pl.program_id(0) == 0
        for l in range(cw // lanes):
            cols = slice(l * lanes, (l + 1) * lanes)
            mh = jnp.where(first, 0.0, cch_ref[:, cols].astype(F32) * hch_ref[:, cols].astype(F32))
            mm = cc_ref[:, cols].astype(F32) * hc_ref[:, cols].astype(F32)
            xx = jnp.concatenate([mh, mm], axis=0)
            tap = _Taps(xx, tm, True)
            cv = jnp.zeros((tm, lanes), F32)
            for k in range(SCONV_K):
                cv = cv + sw_ref[k:k + 1, cols] * tap(SCONV_K - 1 - k)
            c_out = bc_ref[:, cols].astype(F32) * cv
            y_ref[:, cols] = (c_out * _silu(g1_ref[:, cols].astype(F32))).astype(BF16)
            dh = jnp.where(first, 0.0, gah_ref[:, cols].astype(F32) * _sigmoid(gbh_ref[:, cols].astype(F32)))
            dm = ga_ref[:, cols].astype(F32) * _sigmoid(gb_ref[:, cols].astype(F32))
            xx = jnp.concatenate([dh, dm], axis=0)
            tap = _Taps(xx, tm, True)
            acc = jnp.zeros((tm, lanes), F32) + db_ref[:, cols]
            for k in range(CONF_K):
                acc = acc + dw_ref[k:k + 1, cols] * tap(CONF_K - 1 - k)
            dc_ref[:, cols] = acc
        rs = 64
        for r in range(tm // rs):
            rows = slice(r * rs, (r + 1) * rs)
            xv = dc_ref[rows, :]
            mu = jnp.mean(xv, axis=-1, keepdims=True)
            xc = xv - mu
            rstd = lax.rsqrt(jnp.mean(xc * xc, axis=-1, keepdims=True) + EPS)
            ln = xc * rstd * gam_ref[...] + bet_ref[...]
            y_ref[rows, cw:2 * cw] = (_silu(ln) * _silu(g2_ref[rows, :].astype(F32))).astype(BF16)

    main = lambda c: pl.BlockSpec((tm, cw), lambda i: (i, c))
    halo = lambda c: pl.BlockSpec((HALO, cw), lambda i: (hb(i), c))
    vec = lambda r: pl.BlockSpec((r, cw), lambda i: (0, 0))
    return pl.pallas_call(
        body, name=name, grid=(n,),
        in_specs=[main(0), halo(0), main(1), main(2), halo(2), main(3), halo(3), main(4), halo(4),
                  main(5), main(6), vec(SCONV_K), vec(CONF_K), vec(1), vec(1), vec(1)] + dep_specs,
        out_specs=[pl.BlockSpec((tm, 2 * cw), lambda i: (i, 0)), pl.BlockSpec((tm, cw), lambda i: (i, 0))],
        out_shape=[jax.ShapeDtypeStruct((s, 2 * cw), BF16), jax.ShapeDtypeStruct((s, cw), F32)],
        compiler_params=_params("parallel"),
    )(p, p, p, p, p, p, p, p, p, p, p, sconv_w, dconv_w, dconv_b, cnorm_g, cnorm_b, *dep_args)


def odd_bwd_ln(dy, p, dc, cnorm_g, cnorm_b, name, tm=256):
    s = p.shape[0]
    cw = dc.shape[1]
    n = s // tm
    rs = 128

    def body(dy_ref, g2_ref, dc_ref, gam_ref, bet_ref, ddc_ref, dg_ref, dgam_ref, dbet_ref, gacc, bacc):
        i = pl.program_id(0)

        @pl.when(i == 0)
        def _():
            gacc[...] = jnp.zeros_like(gacc)
            bacc[...] = jnp.zeros_like(bacc)

        def chunk(ci, carry):
            rows = pl.ds(pl.multiple_of(ci * rs, rs), rs)
            xv = dc_ref[rows, :]
            mu = jnp.mean(xv, axis=-1, keepdims=True)
            xc = xv - mu
            rstd = lax.rsqrt(jnp.mean(xc * xc, axis=-1, keepdims=True) + EPS)
            xh = xc * rstd
            gam = gam_ref[...]
            sl, dsl = _silu_and_grad(xh * gam + bet_ref[...])
            sg, dsg = _silu_and_grad(g2_ref[rows, :].astype(F32))
            dyv = dy_ref[rows, :].astype(F32)
            dg_ref[rows, :] = (dyv * sl * dsg).astype(BF16)
            dln = dyv * sg * dsl
            gacc[...] += _rowsum8(dln * xh)
            bacc[...] += _rowsum8(dln)
            dxh = dln * gam
            ddc_ref[rows, :] = rstd * (dxh - jnp.mean(dxh, axis=-1, keepdims=True)
                                       - xh * jnp.mean(dxh * xh, axis=-1, keepdims=True))
            return carry

        lax.fori_loop(0, tm // rs, chunk, 0)

        @pl.when(i == n - 1)
        def _():
            dgam_ref[...] = jnp.sum(gacc[...], axis=0, keepdims=True)
            dbet_ref[...] = jnp.sum(bacc[...], axis=0, keepdims=True)

    vec = pl.BlockSpec((1, cw), lambda i: (0, 0))
    return pl.pallas_call(
        body, name=name, grid=(n,),
        in_specs=[pl.BlockSpec((tm, cw), lambda i: (i, 1)), pl.BlockSpec((tm, cw), lambda i: (i, 6)),
                  pl.BlockSpec((tm, cw), lambda i: (i, 0)), vec, vec],
        out_specs=[pl.BlockSpec((tm, cw), lambda i: (i, 0)), pl.BlockSpec((tm, cw), lambda i: (i, 0)), vec, vec],
        out_shape=[jax.ShapeDtypeStruct((s, cw), F32), jax.ShapeDtypeStruct((s, cw), BF16),
                   jax.ShapeDtypeStruct((1, cw), F32), jax.ShapeDtypeStruct((1, cw), F32)],
        scratch_shapes=[pltpu.VMEM((8, cw), F32), pltpu.VMEM((8, cw), F32)],
        compiler_params=_params("arbitrary"),
    )(dy, p, dc, cnorm_g, cnorm_b)


def odd_bwd_conv(dy, p, ddc, dg2, sconv_w, dconv_w, name, tm=128):
    s = p.shape[0]
    cw = ddc.shape[1]
    n = s // tm
    lanes = 128
    hb = _halo_before(tm)
    ha = _halo_after(tm, s)

    def body(dy_ref, dya_ref, g1_ref, g1a_ref, bc_ref, bca_ref, hc_ref, hch_ref, cc_ref, cch_ref,
             ddc_ref, ddca_ref, ga_ref, gah_ref, gb_ref, gbh_ref, dg2_ref, sw_ref, dw_ref,
             dp_ref, dsw_ref, ddw_ref, ddb_ref, sw_acc, dw_acc, db_acc):
        i = pl.program_id(0)
        first = i == 0
        last = i == n - 1

        @pl.when(first)
        def _():
            sw_acc[...] = jnp.zeros_like(sw_acc)
            dw_acc[...] = jnp.zeros_like(dw_acc)
            db_acc[...] = jnp.zeros_like(db_acc)

        for l in range(cw // lanes):
            cols = slice(l * lanes, (l + 1) * lanes)
            mh = jnp.where(first, 0.0, cch_ref[:, cols].astype(F32) * hch_ref[:, cols].astype(F32))
            hcv = hc_ref[:, cols].astype(F32)
            ccv = cc_ref[:, cols].astype(F32)
            xx = jnp.concatenate([mh, ccv * hcv], axis=0)
            tap = _Taps(xx, tm, True)
            taps = [tap(SCONV_K - 1 - k) for k in range(SCONV_K)]
            cv = jnp.zeros((tm, lanes), F32)
            for k in range(SCONV_K):
                cv = cv + sw_ref[k:k + 1, cols] * taps[k]
            bcv = bc_ref[:, cols].astype(F32)
            dyv = dy_ref[:, cols].astype(F32)
            sg, dsg = _silu_and_grad(g1_ref[:, cols].astype(F32))
            dco = dyv * sg
            dp_ref[:, 5 * cw + l * lanes:5 * cw + (l + 1) * lanes] = (dyv * bcv * cv * dsg).astype(BF16)
            dp_ref[:, cw + l * lanes:cw + (l + 1) * lanes] = (dco * cv).astype(BF16)
            dcv = dco * bcv
            for k in range(SCONV_K):
                sw_acc[k * 8:(k + 1) * 8, cols] += _rowsum8(dcv * taps[k])
            dcv_a = jnp.where(last, 0.0, dya_ref[:, cols].astype(F32) * _silu(g1a_ref[:, cols].astype(F32))
                              * bca_ref[:, cols].astype(F32))
            xx = jnp.concatenate([dcv, dcv_a], axis=0)
            tap = _Taps(xx, tm, False)
            dm = jnp.zeros((tm, lanes), F32)
            for k in range(SCONV_K):
                dm = dm + sw_ref[k:k + 1, cols] * tap(SCONV_K - 1 - k)
            dp_ref[:, l * lanes:(l + 1) * lanes] = (dm * ccv).astype(BF16)
            dp_ref[:, 2 * cw + l * lanes:2 * cw + (l + 1) * lanes] = (dm * hcv).astype(BF16)
            gav = ga_ref[:, cols].astype(F32)
            sb = _sigmoid(gb_ref[:, cols].astype(F32))
            dh = jnp.where(first, 0.0, gah_ref[:, cols].astype(F32) * _sigmoid(gbh_ref[:, cols].astype(F32)))
            xx = jnp.concatenate([dh, gav * sb], axis=0)
            ddcv = ddc_ref[:, cols]
            db_acc[:, cols] += _rowsum8(ddcv)
            tap = _Taps(xx, tm, True)
            for k in range(CONF_K):
                dw_acc[k * 8:(k + 1) * 8, cols] += _rowsum8(ddcv * tap(CONF_K - 1 - k))
            ddc_a = jnp.where(last, 0.0, ddca_ref[:, cols])
            xx = jnp.concatenate([ddcv, ddc_a], axis=0)
            tap = _Taps(xx, tm, False)
            dgl = jnp.zeros((tm, lanes), F32)
            for k in range(CONF_K):
                dgl = dgl + dw_ref[k:k + 1, cols] * tap(CONF_K - 1 - k)
            dp_ref[:, 3 * cw + l * lanes:3 * cw + (l + 1) * lanes] = (dgl * sb).astype(BF16)
            dp_ref[:, 4 * cw + l * lanes:4 * cw + (l + 1) * lanes] = (dgl * gav * sb * (1.0 - sb)).astype(BF16)
        dp_ref[:, 6 * cw:7 * cw] = dg2_ref[...]

        @pl.when(last)
        def _():
            for k in range(SCONV_K):
                dsw_ref[k:k + 1, :] = jnp.sum(sw_acc[k * 8:(k + 1) * 8, :], axis=0, keepdims=True)
            for k in range(CONF_K):
                ddw_ref[k:k + 1, :] = jnp.sum(dw_acc[k * 8:(k + 1) * 8, :], axis=0, keepdims=True)
            ddb_ref[...] = jnp.sum(db_acc[...], axis=0, keepdims=True)

    def main(c):
        return pl.BlockSpec((tm, cw), lambda i: (i, c))

    def before(c):
        return pl.BlockSpec((HALO, cw), lambda i: (hb(i), c))

    def after(c):
        return pl.BlockSpec((HALO, cw), lambda i: (ha(i), c))

    def vec(r):
        return pl.BlockSpec((r, cw), lambda i: (0, 0))

    return pl.pallas_call(
        body, name=name, grid=(n,),
        in_specs=[main(0), after(0), main(5), after(5), main(1), after(1), main(0), before(0), main(2), before(2),
                  main(0), after(0), main(3), before(3), main(4), before(4), main(0), vec(SCONV_K), vec(CONF_K)],
        out_specs=[pl.BlockSpec((tm, 7 * cw), lambda i: (i, 0)), vec(SCONV_K), vec(CONF_K), vec(1)],
        out_shape=[jax.ShapeDtypeStruct((s, 7 * cw), BF16), jax.ShapeDtypeStruct((SCONV_K, cw), F32),
                   jax.ShapeDtypeStruct((CONF_K, cw), F32), jax.ShapeDtypeStruct((1, cw), F32)],
        scratch_shapes=[pltpu.VMEM((8 * SCONV_K, cw), F32), pltpu.VMEM((8 * CONF_K, cw), F32),
                        pltpu.VMEM((8, cw), F32)],
        compiler_params=_params("arbitrary"),
    )(dy, dy, p, p, p, p, p, p, p, p, ddc, ddc, p, p, p, p, dg2, sconv_w, dconv_w)


_ANY = pl.BlockSpec(memory_space=pl.ANY)


def _place():
    return lax.axis_index("x"), lax.axis_index("y"), lax.axis_index("c")


def all_gather(arrs, name, deps=()):
    n = len(arrs)

    def body(*refs):
        ins, outs = refs[:n], refs[n + len(deps):2 * n + len(deps)]
        send_sems, recv_sems, local_sems = refs[-3:]
        x, y, c = _place()
        me, sibling = (x, y, c), (x, y, 1 - c)
        chips = [(1 - x, y), (x, 1 - y), (1 - x, 1 - y)]

        def copy(a, k, block, to, src=None):
            px, py, pc = block
            dst = outs[a].at[4 * px + 2 * py + pc]
            return pltpu.make_async_remote_copy(
                src_ref=dst if src is None else src, dst_ref=dst,
                send_sem=send_sems.at[7 * a + k], recv_sem=recv_sems.at[7 * a + k],
                device_id=to, device_id_type=MESH)

        mine = [pltpu.make_async_copy(ins[a], outs[a].at[4 * x + 2 * y + c], local_sems.at[a]) for a in range(n)]
        first = []
        for a in range(n):
            first.append(copy(a, 0, me, sibling, src=ins[a]))
            first += [copy(a, 1 + j, me, (*chip, c), src=ins[a]) for j, chip in enumerate(chips)]
        for cp in first + mine:
            cp.start()
        passed = []
        for a in range(n):
            for j, chip in enumerate(chips):
                copy(a, 1 + j, (*chip, c), me).wait_recv()
                cp = copy(a, 4 + j, (*chip, c), sibling)
                cp.start()
                passed.append(cp)
        for a in range(n):
            copy(a, 0, sibling, me).wait_recv()
            for j, chip in enumerate(chips):
                copy(a, 4 + j, (*chip, 1 - c), me).wait_recv()
        for cp in first + passed:
            cp.wait_send()
        for cp in mine:
            cp.wait()

    return pl.pallas_call(
        body, name=name,
        out_shape=[jax.ShapeDtypeStruct((N_DEV,) + a.shape, a.dtype) for a in arrs],
        in_specs=[_ANY] * (n + len(deps)), out_specs=[_ANY] * n,
        scratch_shapes=[pltpu.SemaphoreType.DMA((7 * n,)), pltpu.SemaphoreType.DMA((7 * n,)),
                        pltpu.SemaphoreType.DMA((n,))],
    )(*arrs, *deps)


def in_proj_gathered(xs, g, w_own, extras, name, tm=1024):
    s, d = xs.shape
    n = w_own.shape[1]
    tm = min(tm, s)
    arrs = [w_own] + list(extras)
    na = len(arrs)
    tr = 256

    def body(*refs):
        x_ref, g_ref, ins = refs[0], refs[1], refs[2:2 + na]
        h_out, p_ref, outs = refs[2 + na], refs[3 + na], refs[4 + na:4 + 2 * na]
        (h_ref, xbuf, wbuf, obuf, send_sems, recv_sems, load_sem, store_sems, own_sems, h_sem,
         x_sems) = refs[4 + 2 * na:]
        x, y, c = _place()
        me, sibling = (x, y, c), (x, y, 1 - c)
        x_first = c == 0
        near = (jnp.where(x_first, 1 - x, x), jnp.where(x_first, y, 1 - y))
        far = (jnp.where(x_first, x, 1 - x), jnp.where(x_first, 1 - y, y))
        diag = (1 - x, 1 - y)
        k_near, k_far = jnp.where(x_first, 1, 2), jnp.where(x_first, 2, 1)
        f_near, f_far = k_near + 3, k_far + 3

        def slot(block):
            return 4 * block[0] + 2 * block[1] + block[2]

        def copy(a, k, block, to, src=None):
            dst = outs[a].at[slot(block)]
            return pltpu.make_async_remote_copy(
                src_ref=dst if src is None else src, dst_ref=dst,
                send_sem=send_sems.at[7 * a + k], recv_sem=recv_sems.at[7 * a + k],
                device_id=to, device_id_type=MESH)

        first = []
        for a in range(na):
            first += [copy(a, 0, me, sibling, src=ins[a]), copy(a, 1, me, (1 - x, y, c), src=ins[a]),
                      copy(a, 2, me, (x, 1 - y, c), src=ins[a])]
        for cp in first:
            cp.start()
        own = pltpu.make_async_copy(wbuf.at[0], outs[0].at[slot(me)], own_sems.at[0])
        mine = [pltpu.make_async_copy(ins[a], outs[a].at[slot(me)], own_sems.at[a]) for a in range(1, na)]
        stores = [None, None]

        def x_load(i):
            return pltpu.make_async_copy(x_ref.at[pl.ds(i * tr, tr), :], xbuf.at[i % 2], x_sems.at[i % 2])

        x_load(0).start()
        for i in range(s // tr):
            if i + 1 < s // tr:
                x_load(i + 1).start()
            x_load(i).wait()
            xv = xbuf[i % 2]
            r = lax.rsqrt(jnp.mean(xv * xv, axis=-1, keepdims=True) + EPS)
            h_ref[i * tr:(i + 1) * tr, :] = (xv * r * g_ref[...]).astype(BF16)
        h_store = pltpu.make_async_copy(h_ref, h_out, h_sem)
        h_store.start()

        def multiply(k, block, w_from):
            b = k % 2
            if k == 2:
                own.wait()
            load = pltpu.make_async_copy(w_from, wbuf.at[b], load_sem)
            load.start()
            if stores[b] is not None:
                stores[b].wait()
            load.wait()
            if k == 0:
                own.start()

            def chunk(i, carry):
                rows = pl.ds(pl.multiple_of(i * tm, tm), tm)
                obuf[b, rows, :] = jnp.dot(h_ref[rows, :], wbuf[b], preferred_element_type=F32).astype(BF16)
                return carry

            lax.fori_loop(0, s // tm, chunk, 0)
            stores[b] = pltpu.make_async_copy(
                obuf.at[b], p_ref.at[:, pl.ds(pl.multiple_of(slot(block) * n, 128), n)], store_sems.at[b])
            stores[b].start()

        passed = []

        def arrive(a, k, block):
            copy(a, k, block, me).wait_recv()

        def pass_on(a, k, block, to):
            cp = copy(a, k, block, to)
            cp.start()
            passed.append(cp)

        def gather(arrays, use):
            def arrive_all(k, block):
                for a in arrays:
                    arrive(a, k, block)

            def pass_all(k, block, to):
                for a in arrays:
                    pass_on(a, k, block, to)

            use(0, me)
            arrive_all(0, sibling)
            use(1, sibling)
            arrive_all(k_near, (*near, c))
            pass_all(3, (*near, c), (*far, c))
            pass_all(f_near, (*near, c), sibling)
            use(2, (*near, c))
            arrive_all(f_far, (*far, 1 - c))
            use(3, (*far, 1 - c))
            arrive_all(k_far, (*far, c))
            pass_all(f_far, (*far, c), sibling)
            use(4, (*far, c))
            arrive_all(f_near, (*near, 1 - c))
            use(5, (*near, 1 - c))
            arrive_all(3, (*diag, c))
            pass_all(6, (*diag, c), sibling)
            use(6, (*diag, c))
            arrive_all(6, (*diag, 1 - c))
            use(7, (*diag, 1 - c))

        gather(range(na), lambda k, block: multiply(k, block, ins[0] if k == 0 else outs[0].at[slot(block)]))
        for cp in mine:
            cp.start()
        for cp in first + passed:
            cp.wait_send()
        for cp in mine + stores + [h_store]:
            cp.wait()

    vmem = pl.BlockSpec(memory_space=pltpu.VMEM)
    outs = pl.pallas_call(
        body, name=name,
        out_shape=[jax.ShapeDtypeStruct((s, d), BF16), jax.ShapeDtypeStruct((s, N_DEV * n), BF16)]
        + [jax.ShapeDtypeStruct((N_DEV,) + a.shape, a.dtype) for a in arrs],
        in_specs=[_ANY, vmem] + [_ANY] * na, out_specs=[_ANY] * (2 + na),
        scratch_shapes=[pltpu.VMEM((s, d), BF16), pltpu.VMEM((2, tr, d), F32), pltpu.VMEM((2, d, n), BF16),
                        pltpu.VMEM((2, s, n), BF16),
                        pltpu.SemaphoreType.DMA((7 * na,)), pltpu.SemaphoreType.DMA((7 * na,)),
                        pltpu.SemaphoreType.DMA, pltpu.SemaphoreType.DMA((2,)), pltpu.SemaphoreType.DMA((na,)),
                        pltpu.SemaphoreType.DMA, pltpu.SemaphoreType.DMA((2,))],
        compiler_params=pltpu.CompilerParams(vmem_limit_bytes=VMEM_LIMIT),
    )(xs, g, *arrs)
    return outs[0], outs[1], outs[2], outs[3:]


_HBM = pl.BlockSpec(memory_space=pltpu.HBM)
_SEM = pl.BlockSpec(memory_space=pltpu.SEMAPHORE)
_DATAFLOW = pltpu.SideEffectType.DATAFLOW_SIDE_EFFECTING


def _peers_per_array(kind):
    return {"sibling": 1, "halves": 1, "pass": 1, "neighbours": 2}.get(kind, 3)


def _near_far():
    x, y, c = _place()
    x_first = c == 0
    near = (jnp.where(x_first, 1 - x, x), jnp.where(x_first, y, 1 - y))
    far = (jnp.where(x_first, x, 1 - x), jnp.where(x_first, 1 - y, y))
    return near, far, jnp.where(x_first, 0, 1), jnp.where(x_first, 1, 0)


def _split_copies(kind, srcs, lands, send_sems, recv_sems):
    x, y, c = _place()
    per = _peers_per_array(kind)
    out = []
    for a in range(len(lands)):
        if kind == "sibling":
            part = srcs[a] if srcs[a].shape[1] == 1 else srcs[a].at[:, pl.ds(1 - c, 1)]
            peers = [((x, y, 1 - c), part, lands[a], lands[a])]
        elif kind == "halves":
            mine, its = lands[a].at[:, pl.ds(c, 1)], lands[a].at[:, pl.ds(1 - c, 1)]
            peers = [((x, y, 1 - c), mine, mine, its)]
        elif kind == "neighbours":
            here = lands[a].at[4 * x + 2 * y + c]
            peers = [((px, py, c), srcs[a], here, lands[a].at[4 * px + 2 * py + c])
                     for px, py in [(1 - x, y), (x, 1 - y)]]
        elif kind == "pass":
            near, far, _, _ = _near_far()
            block = lands[a].at[4 * near[0] + 2 * near[1] + c]
            peers = [((*far, c), block, block, lands[a].at[4 * (1 - x) + 2 * (1 - y) + c])]
        else:
            peers = []
            for px, py in [(1 - x, y), (x, 1 - y), (1 - x, 1 - y)]:
                if kind == "gather":
                    views = (srcs[a], lands[a].at[4 * x + 2 * y + c], lands[a].at[4 * px + 2 * py + c])
                else:
                    views = (srcs[a].at[2 * px + py], lands[a].at[2 * x + y], lands[a].at[2 * px + py])
                peers.append(((px, py, c),) + views)
        for j, (peer, src, dst, arrives) in enumerate(peers):
            sems = dict(send_sem=send_sems.at[per * a + j], recv_sem=recv_sems.at[per * a + j],
                        device_id=peer, device_id_type=MESH)
            out.append((pltpu.make_async_remote_copy(src_ref=src, dst_ref=dst, **sems),
                        pltpu.make_async_remote_copy(src_ref=src, dst_ref=arrives, **sems)))
    return out


def split_start(kind, srcs, lands, deps, name):
    ns, nl = len(srcs), len(lands)
    n_sems = _peers_per_array(kind) * nl
    held = list(srcs) + list(lands)

    def body(*refs):
        send_sems, recv_sems = refs[len(held) + len(deps)], refs[len(held) + len(deps) + 1]
        for copy, _ in _split_copies(kind, refs[:ns], refs[ns:ns + nl], send_sems, recv_sems):
            copy.start()
        token = refs[-1]
        token[...] = jnp.zeros_like(token)

    outs = pl.pallas_call(
        body, name=name,
        out_shape=(pltpu.SemaphoreType.DMA((n_sems,)), pltpu.SemaphoreType.DMA((n_sems,)),
                   *[pltpu.HBM(a.shape, a.dtype) for a in held], jax.ShapeDtypeStruct((8, 128), F32)),
        in_specs=[_HBM] * len(held) + [_ANY] * len(deps),
        out_specs=(_SEM, _SEM, *([_HBM] * len(held)), pl.BlockSpec(memory_space=pltpu.VMEM)),
        input_output_aliases={i: 2 + i for i in range(len(held))},
        compiler_params=pltpu.CompilerParams(has_side_effects=_DATAFLOW),
    )(*[pltpu.with_memory_space_constraint(a, pltpu.HBM) for a in held], *deps)
    return outs[0], outs[1], list(outs[2:2 + ns]), list(outs[2 + ns:2 + ns + nl]), outs[-1]


def split_wait(kind, send_sems, recv_sems, srcs, lands, afters, name):
    ns, nl = len(srcs), len(lands)
    held = list(srcs) + list(lands)

    def body(*refs):
        for _, arrival in _split_copies(kind, refs[:ns], refs[ns:ns + nl], refs[ns + nl], refs[ns + nl + 1]):
            arrival.wait_send()
            arrival.wait_recv()

    outs = pl.pallas_call(
        body, name=name,
        out_shape=[pltpu.HBM(a.shape, a.dtype) for a in held],
        in_specs=[_HBM] * len(held) + [_SEM, _SEM] + [_ANY] * len(afters),
        out_specs=[_HBM] * len(held),
        input_output_aliases={i: i for i in range(len(held))},
        compiler_params=pltpu.CompilerParams(has_side_effects=_DATAFLOW),
    )(*held, send_sems, recv_sems, *afters)
    return list(outs[:ns]), list(outs[ns:])


def split_pass_on(lands, first_recv, afters, name):
    n = len(lands)

    def body(*refs):
        lands_r, first = refs[:n], refs[n]
        send_sems, recv_sems = refs[n + 1 + len(afters)], refs[n + 2 + len(afters)]
        c = lax.axis_index("c")
        near, _, k_near, _ = _near_far()
        for a in range(n):
            block = lands_r[a].at[4 * near[0] + 2 * near[1] + c]
            pltpu.make_async_remote_copy(
                src_ref=block, dst_ref=block, send_sem=first.at[2 * a + k_near], recv_sem=first.at[2 * a + k_near],
                device_id=(near[0], near[1], c), device_id_type=MESH).wait_recv()
        for copy, _ in _split_copies("pass", [], lands_r, send_sems, recv_sems):
            copy.start()
        token = refs[-1]
        token[...] = jnp.zeros_like(token)

    outs = pl.pallas_call(
        body, name=name,
        out_shape=(pltpu.SemaphoreType.DMA((n,)), pltpu.SemaphoreType.DMA((n,)),
                   *[pltpu.HBM(a.shape, a.dtype) for a in lands], jax.ShapeDtypeStruct((8, 128), F32)),
        in_specs=[_HBM] * n + [_SEM] + [_ANY] * len(afters),
        out_specs=(_SEM, _SEM, *([_HBM] * n), pl.BlockSpec(memory_space=pltpu.VMEM)),
        input_output_aliases={i: 2 + i for i in range(n)},
        compiler_params=pltpu.CompilerParams(has_side_effects=_DATAFLOW),
    )(*lands, first_recv, *afters)
    return outs[0], outs[1], list(outs[2:2 + n]), outs[-1]


def split_wait_neighbours(first_send, first_recv, pass_send, pass_recv, srcs, lands, afters, name):
    n = len(lands)
    held = list(srcs) + list(lands)

    def body(*refs):
        srcs_r, lands_r = refs[:n], refs[n:2 * n]
        send1, recv1, send2, recv2 = refs[2 * n:2 * n + 4]
        c = lax.axis_index("c")
        _, far, _, k_far = _near_far()
        for _, arrival in _split_copies("neighbours", srcs_r, lands_r, send1, recv1):
            arrival.wait_send()
        for a in range(n):
            block = lands_r[a].at[4 * far[0] + 2 * far[1] + c]
            pltpu.make_async_remote_copy(
                src_ref=block, dst_ref=block, send_sem=recv1.at[2 * a + k_far], recv_sem=recv1.at[2 * a + k_far],
                device_id=(far[0], far[1], c), device_id_type=MESH).wait_recv()
        for _, arrival in _split_copies("pass", [], lands_r, send2, recv2):
            arrival.wait_send()
            arrival.wait_recv()

    outs = pl.pallas_call(
        body, name=name,
        out_shape=[pltpu.HBM(a.shape, a.dtype) for a in held],
        in_specs=[_HBM] * len(held) + [_SEM] * 4 + [_ANY] * len(afters),
        out_specs=[_HBM] * len(held),
        input_output_aliases={i: i for i in range(len(held))},
        compiler_params=pltpu.CompilerParams(has_side_effects=_DATAFLOW),
    )(*held, first_send, first_recv, pass_send, pass_recv, *afters)
    return list(outs[:n]), list(outs[n:])


def place_block(land, block, dev, name):
    r, c = block.shape
    tr = min(r, 512)

    def body(dev_ref, land_ref, b_ref, o_ref):
        del dev_ref, land_ref
        o_ref[...] = b_ref[...]

    return pl.pallas_call(
        body, name=name,
        grid_spec=pltpu.PrefetchScalarGridSpec(
            num_scalar_prefetch=1, grid=(r // tr,),
            in_specs=[_ANY, pl.BlockSpec((tr, c), lambda i, dev_ref: (i, 0))],
            out_specs=pl.BlockSpec((None, tr, c), lambda i, dev_ref: (dev_ref[0], i, 0))),
        out_shape=jax.ShapeDtypeStruct(land.shape, land.dtype),
        input_output_aliases={1: 0},
        compiler_params=_params("parallel"),
    )(dev, land, block)


def pair_add(own, recv, core, name):
    _, _, r, c = own.shape
    tr = min(r, 2048)

    def body(core_ref, own_ref, recv_ref, o_ref):
        del core_ref
        o_ref[...] = (own_ref[...].astype(F32) + recv_ref[...].astype(F32)).astype(BF16)

    return pl.pallas_call(
        body, name=name,
        grid_spec=pltpu.PrefetchScalarGridSpec(
            num_scalar_prefetch=1, grid=(4, r // tr),
            in_specs=[pl.BlockSpec((None, None, tr, c), lambda k, i, core_ref: (k, core_ref[0], i, 0)),
                      pl.BlockSpec((None, None, tr, c), lambda k, i, core_ref: (k, 0, i, 0))],
            out_specs=pl.BlockSpec((None, tr, c), lambda k, i, core_ref: (k, i, 0))),
        out_shape=jax.ShapeDtypeStruct((4, r, c), BF16),
        compiler_params=_params("parallel", "parallel"),
    )(core, own, recv)


def _adamw_math(w, g, m, v):
    m2 = ADAM_B1 * m + (1.0 - ADAM_B1) * g
    v2 = ADAM_B2 * v + (1.0 - ADAM_B2) * (g * g)
    m_hat = m2 / (1.0 - ADAM_B1 ** ADAM_STEP)
    v_hat = v2 / (1.0 - ADAM_B2 ** ADAM_STEP)
    delta = -ADAM_LR * (m_hat / (jnp.sqrt(v_hat) + ADAM_EPS) + ADAM_WD * w)
    return delta, m2, v2


def adamw_big(w, m, v, own, got, chip, name):
    r, c = w.shape
    tr = min(r, 512)

    def body(chip_ref, w_ref, m_ref, v_ref, p0, p1, p2, p3, g_ref, d_ref, m2_ref, v2_ref):
        del chip_ref
        g = ((p0[...].astype(F32) + p1[...].astype(F32)) + p2[...].astype(F32)) + p3[...].astype(F32)
        delta, m2, v2 = _adamw_math(w_ref[...], g, m_ref[...], v_ref[...])
        g_ref[...] = g
        d_ref[...] = delta
        m2_ref[...] = m2
        v2_ref[...] = v2

    row = pl.BlockSpec((tr, c), lambda i, chip_ref: (i, 0))

    def slab(flip):
        return pl.BlockSpec((None, tr, c), lambda i, chip_ref: (chip_ref[0] ^ flip, i, 0))

    return pl.pallas_call(
        body, name=name,
        grid_spec=pltpu.PrefetchScalarGridSpec(
            num_scalar_prefetch=1, grid=(r // tr,),
            in_specs=[row, row, row, slab(0), slab(1), slab(2), slab(3)],
            out_specs=[row] * 4),
        out_shape=[jax.ShapeDtypeStruct((r, c), F32)] * 4,
        compiler_params=_params("parallel"),
    )(chip, w, m, v, own, got, got, got)


def sum_devices(g8, name):
    def body(g_ref, o_ref):
        tot = g_ref[0]
        for k in range(1, N_DEV):
            tot = tot + g_ref[k]
        o_ref[...] = tot

    return pl.pallas_call(body, name=name, out_shape=jax.ShapeDtypeStruct(g8.shape[1:], F32))(g8)


def adamw_small(ws, gs, ms, vs, name):
    n = len(ws)

    def body(*refs):
        w_r, g_r, m_r, v_r = refs[:n], refs[n:2 * n], refs[2 * n:3 * n], refs[3 * n:4 * n]
        d_o, m_o, v_o = refs[4 * n:5 * n], refs[5 * n:6 * n], refs[6 * n:7 * n]
        for k in range(n):
            delta, m2, v2 = _adamw_math(w_r[k][...], g_r[k][...], m_r[k][...], v_r[k][...])
            d_o[k][...] = delta
            m_o[k][...] = m2
            v_o[k][...] = v2

    shapes = [jax.ShapeDtypeStruct(w.shape, F32) for w in ws]
    outs = pl.pallas_call(body, name=name, out_shape=shapes * 3)(*ws, *gs, *ms, *vs)
    return outs[:n], outs[n:2 * n], outs[2 * n:]


def _rows128(a):
    return a.reshape(-1, 128)


def _pad_rows(a, rows):
    return jnp.pad(a, ((0, rows - a.shape[0]), (0, 0)))


def kernel(x, ln_pre_even, w_in_even, pool_w, pool_scale, w_out_even, ln_post_even, ln_pre_odd, w_in_odd, sconv_w, dconv_w, dconv_b, cnorm_g, cnorm_b, w_out_odd, ln_post_odd, loss_target, m_ln_pre_even, m_w_in_even, m_pool_w, m_pool_scale, m_w_out_even, m_ln_post_even, m_ln_pre_odd, m_w_in_odd, m_sconv_w, m_dconv_w, m_dconv_b, m_cnorm_g, m_cnorm_b, m_w_out_odd, m_ln_post_odd, v_ln_pre_even, v_w_in_even, v_pool_w, v_pool_scale, v_w_out_even, v_ln_post_even, v_ln_pre_odd, v_w_in_odd, v_sconv_w, v_dconv_w, v_dconv_b, v_cnorm_g, v_cnorm_b, v_w_out_odd, v_ln_post_odd):
    xs = x[0]
    tgt = loss_target[0]
    s, d = xs.shape
    half = d // 2
    n_heads = half // HEAD_DIM
    ng = len(POOL_WINDOWS)
    cwp = half // ng
    dev = 4 * lax.axis_index("x") + 2 * lax.axis_index("y") + lax.axis_index("c")
    core = lax.axis_index("c").astype(jnp.int32).reshape(1)

    pr = pool_w.shape[2]
    cl = sconv_w.shape[2]
    small_parts = [(_rows128(ln_pre_odd), 8), (sconv_w[0], 8), (dconv_w[0], 32), (dconv_b, 8),
                   (cnorm_g, 8), (cnorm_b, 8), (_rows128(ln_post_odd), 8)]
    small_local = jnp.concatenate([_pad_rows(a, r) for a, r in small_parts], axis=0)
    h0, p0, g_wie, (g_pw, g_small) = in_proj_gathered(
        xs, ln_pre_even, w_in_even[0].astype(BF16), [pool_w[0].reshape(ng * pr, cwp).astype(BF16), small_local],
        "ag_in_proj_even")
    comm = _Exchanges(dev, core, d)
    token = comm.start_weights("out_even", [w_out_even[0].astype(BF16)], [p0])
    token = comm.start_weights("in_odd", [w_in_odd[0].astype(BF16)], [token], neighbours=True)
    sb_dep = token
    comm.later["out_odd"] = [w_out_odd[0].astype(BF16)]
    pool_full = g_pw.reshape(N_DEV, ng, pr, cwp).transpose(1, 0, 2, 3).reshape(ng, cwp, cwp)
    nl = ln_pre_odd.shape[1] // 128

    def chan(lo, rows):
        return g_small[:, lo:lo + rows].transpose(1, 0, 2).reshape(rows, N_DEV * cl)

    ln_pre_odd_f = g_small[:, 0:nl].reshape(1, d)
    sconv_f = chan(8, SCONV_K)
    dconv_f = chan(16, CONF_K)
    dconv_b_f = chan(48, 1)
    cnorm_g_f = chan(56, 1)
    cnorm_b_f = chan(64, 1)
    ln_post_odd_f = g_small[:, 72:72 + nl].reshape(1, d)

    loss_blk, grad_x, small_g = _fwd_bwd(
        xs, tgt, ln_pre_even, h0, p0, g_wie, pool_full, pool_scale, ln_post_even, ln_pre_odd_f,
        sconv_f, dconv_f, dconv_b_f, cnorm_g_f, cnorm_b_f, ln_post_odd_f, comm, sb_dep)
    small_w = [ln_pre_even, pool_scale, ln_post_even, ln_pre_odd, sconv_w[0], dconv_w[0], dconv_b, cnorm_g, cnorm_b, ln_post_odd]
    small_m = [m_ln_pre_even, m_pool_scale, m_ln_post_even, m_ln_pre_odd, m_sconv_w[0], m_dconv_w[0], m_dconv_b, m_cnorm_g, m_cnorm_b, m_ln_post_odd]
    small_v = [v_ln_pre_even, v_pool_scale, v_ln_post_even, v_ln_pre_odd, v_sconv_w[0], v_dconv_w[0], v_dconv_b, v_cnorm_g, v_cnorm_b, v_ln_post_odd]
    big = {"w_in_even": (w_in_even, m_w_in_even, v_w_in_even), "pool_w": (pool_w, m_pool_w, v_pool_w),
           "w_out_even": (w_out_even, m_w_out_even, v_w_out_even), "w_in_odd": (w_in_odd, m_w_in_odd, v_w_in_odd),
           "w_out_odd": (w_out_odd, m_w_out_odd, v_w_out_odd)}
    upd = comm.finish_updates(big, [grad_x])
    upd.update(comm.finish_updates(big, [grad_x]))
    sg, sd, sm, sv, loss = _update_small(small_g, loss_blk, small_w, small_m, small_v, dev, d, cl,
                                         deps=[upd["w_in_odd"][1], upd["w_out_even"][1]])
    upd.update(comm.finish_updates(big, sd))
    (g_wie_o, d_wie, m_wie, v_wie), (g_pw_o, d_pw, m_pw, v_pw) = upd["w_in_even"], upd["pool_w"]
    (g_woe_o, d_woe, m_woe, v_woe), (g_wio_o, d_wio, m_wio, v_wio) = upd["w_out_even"], upd["w_in_odd"]
    g_woo_o, d_woo, m_woo, v_woo = upd["w_out_odd"]

    def order(small, wie, pw, woe, wio, woo):
        return [small[0], wie, pw, small[1], woe, small[2], small[3], wio, small[4], small[5], small[6],
                small[7], small[8], woo, small[9]]

    grads = order(sg, g_wie_o, g_pw_o, g_woe_o, g_wio_o, g_woo_o)
    deltas = order(sd, d_wie, d_pw, d_woe, d_wio, d_woo)
    new_m = order(sm, m_wie, m_pw, m_woe, m_wio, m_woo)
    new_v = order(sv, v_wie, v_pw, v_woe, v_wio, v_woo)
    return (loss, grad_x[None], *grads, *deltas, *new_m, *new_v)


def _fwd_bwd(xs, tgt, ln_pre_even, h0, p0, g_wie, pool_full, pool_scale, ln_post_even, ln_pre_odd_f,
             sconv_f, dconv_f, dconv_b_f, cnorm_g_f, cnorm_b_f, ln_post_odd_f, comm, sb_dep):
    d = xs.shape[1]
    n_heads = d // 2 // HEAD_DIM
    ng, cwp = pool_full.shape[0], pool_full.shape[1]
    a0, sb_wts = sb_fwd(p0, n_heads, "sb_fwd", dep=sb_dep)
    dep = comm.weights_arrived("out_even", after=a0)
    y0 = even_mix_fwd(a0, p0, pool_full, pool_scale, "even_mix_fwd", dep=dep)
    (w_out_e,) = comm.weights("out_even", after=y0)
    w_out_e = w_out_e.reshape(1, d, d)
    dep = comm.weights_pass_on("in_odd", after=w_out_e)
    dep = comm.start_weights("out_odd", comm.later.pop("out_odd"), [dep])
    o0 = mm_nn(y0, w_out_e, BF16, "out_proj_even", tn=512, dep=dep)
    dep = comm.weights_arrived("in_odd", after=o0)
    x1, h1 = postnorm_fwd(xs, o0, ln_post_even, ln_pre_odd_f, "post_even", dep=dep)
    (g_wio,) = comm.weights("in_odd", after=x1)
    p1 = mm_nn(h1, g_wio, BF16, "in_proj_odd", group=2)
    dep = comm.weights_arrived("out_odd", after=p1)
    y1, dc = odd_mix_fwd(p1, sconv_f, dconv_f, dconv_b_f, cnorm_g_f, cnorm_b_f, "odd_mix_fwd", dep=dep)
    (w_out_o,) = comm.weights("out_odd", after=y1)
    w_out_o = w_out_o.reshape(1, d, d)
    o1 = mm_nn(y1, w_out_o, BF16, "out_proj_odd", tn=512)
    loss_blk, gx2, do1, dg_post_odd = final_fwd_bwd(x1, o1, ln_post_odd_f, tgt, "post_odd_loss")

    dw_out_o = mm_tn(y1, do1, 1, BF16, "dw_out_odd")
    dy1 = mm_nt(do1, w_out_o, BF16, "dy_odd")
    ddc, dg2, dgam, dbet = odd_bwd_ln(dy1, p1, dc, cnorm_g_f, cnorm_b_f, "odd_bwd_ln")
    dp1, dsconv, ddconv, ddconv_b = odd_bwd_conv(dy1, p1, ddc, dg2, sconv_f, dconv_f, "odd_bwd_conv")
    dw_in_o = mm_tn(h1, dp1, N_DEV, BF16, "dw_in_odd", group=2)
    dep = comm.reduce_begin({"w_out_odd": dw_out_o.reshape(N_DEV, d // N_DEV, d), "w_in_odd": dw_in_o}, "odd")
    dh1 = mm_nt(dp1, g_wio, BF16, "dh_odd", dep=dep, group=2)
    dep = comm.reduce_send(after=dh1)
    gx1, dg_pre_odd, do0, dg_post_even = norm_bwd(dh1, x1, ln_pre_odd_f, gx2, "pre_odd_post_even_bwd",
                                                  inp2=o0, g2=ln_post_even, dep=dep)

    dw_out_e = mm_tn(y0, do0, 1, BF16, "dw_out_even")
    dy0 = mm_nt(do0, w_out_e, BF16, "dy_even")
    da0, du0, dg0, dpool, dpool_scale = even_mix_bwd(dy0, a0, p0, pool_full, pool_scale, "even_mix_bwd")
    pr = cwp // N_DEV
    dpool_slabs = dpool.astype(BF16).reshape(ng, N_DEV, pr, cwp).transpose(1, 0, 2, 3).reshape(N_DEV, ng * pr, cwp)
    dep = comm.reduce_begin({"w_out_even": dw_out_e.reshape(N_DEV, d // N_DEV, d), "pool_w": dpool_slabs}, "even_out")
    dq0, dk0, dv0 = sb_bwd(p0, a0, sb_wts, da0, n_heads, "sb_bwd", dep=dep)
    dep = comm.reduce_send(after=dq0)
    dp0 = jnp.concatenate([dq0, dk0, dv0, du0, dg0], axis=1)
    dw_sibling = mm_tn(h0, dp0, N_DEV // 2, BF16, "dw_in_even_sibling", dep=dep, pick=(2, 1 - comm.core))
    dep = comm.reduce_begin({"w_in_even": dw_sibling}, "even_in", sibling_part=True)
    dw_own = mm_tn(h0, dp0, N_DEV // 2, BF16, "dw_in_even_own", dep=dep, pick=(2, comm.core))
    dep = comm.reduce_send(after=dw_own, own_part={"w_in_even": dw_own})
    dh0 = mm_nt(dp0, g_wie, BF16, "dh_even", dep=dep, group=2)
    dep = None
    grad_x, dg_pre_even = norm_bwd(dh0, xs, ln_pre_even, gx1, "pre_even_bwd", tm=512, dep=dep)
    small_g = [dg_pre_even, dpool_scale, dg_post_even, dg_pre_odd, dsconv, ddconv, ddconv_b, dgam, dbet, dg_post_odd]
    return loss_blk, grad_x, small_g


class _Exchanges:
    def __init__(self, dev, core, d):
        self.dev = dev.astype(jnp.int32).reshape(1)
        self.core = core
        self.chip = (dev // 2).astype(jnp.int32).reshape(1)
        self.d = d
        self.in_flight = {}
        self.later = {}
        self.to_sibling = None
        self.pending = []

    def start_weights(self, tag, blocks, afters, neighbours=False):
        lands = [lax.empty((N_DEV,) + b.shape, b.dtype) for b in blocks]
        kind = "neighbours" if neighbours else "gather"
        send, recv, srcs, lands, token = split_start(kind, blocks, lands, afters, "ag_start_" + tag)
        self.in_flight[tag] = (send, recv, srcs, lands)
        return token

    def weights_pass_on(self, tag, after):
        send, recv, srcs, lands = self.in_flight.pop(tag)
        send2, recv2, lands, token = split_pass_on(lands, recv, [after], "ag_pass_on_" + tag)
        self.in_flight[tag] = (send, recv, srcs, lands, send2, recv2)
        return token

    def weights_arrived(self, tag, after):
        entry = self.in_flight.pop(tag)
        send, recv, srcs, lands = entry[:4]
        if len(entry) == 6:
            srcs, lands = split_wait_neighbours(send, recv, entry[4], entry[5], srcs, lands, [after], "ag_wait_" + tag)
        else:
            srcs, lands = split_wait("gather", send, recv, srcs, lands, [after], "ag_wait_" + tag)
        lands = [place_block(l, b, self.dev, "ag_own_%s_%d" % (tag, k)) for k, (l, b) in enumerate(zip(lands, srcs))]
        lands = [l.reshape((4, 2) + l.shape[1:]) for l in lands]
        send, recv, _, lands, token = split_start("halves", [], lands, [], "ag_sibling_start_" + tag)
        self.in_flight[tag] = (send, recv, lands)
        return token

    def weights(self, tag, after):
        send, recv, lands = self.in_flight.pop(tag)
        _, lands = split_wait("halves", send, recv, [], lands, [after], "ag_sibling_wait_" + tag)
        return [l.reshape((N_DEV,) + l.shape[2:]) for l in lands]

    def reduce_begin(self, partials, tag, sibling_part=False):
        names = list(partials)
        arrs = [partials[k].reshape((4, 1 if sibling_part else 2) + partials[k].shape[1:]) for k in names]
        lands = [lax.empty((4, 1) + a.shape[2:], a.dtype) for a in arrs]
        send, recv, srcs, lands, token = split_start("sibling", arrs, lands, [], "rs_sibling_start_" + tag)
        self.to_sibling = (tag, names, send, recv, srcs, lands)
        return token

    def reduce_send(self, after, own_part=None):
        tag, names, send, recv, srcs, lands = self.to_sibling
        srcs, lands = split_wait("sibling", send, recv, srcs, lands, [after], "rs_sibling_wait_" + tag)
        which = self.core
        if own_part is not None:
            srcs = [own_part[k].reshape((4, 1) + own_part[k].shape[1:]) for k in names]
            which = jnp.zeros((1,), jnp.int32)
        sums = [pair_add(o, r, which, "rs_pair_add_" + k) for k, o, r in zip(names, srcs, lands)]
        zones = [lax.empty(a.shape, a.dtype) for a in sums]
        send, recv, srcs, zones, token = split_start("scatter", sums, zones, [], "rs_start_" + tag)
        self.pending.append((tag, names, send, recv, srcs, zones))
        return token

    def finish_updates(self, big, afters):
        tag, names, send, recv, srcs, lands = self.pending.pop(0)
        srcs, lands = split_wait("scatter", send, recv, srcs, lands, afters, "rs_wait_" + tag)
        out = {}
        for name, own, got in zip(names, srcs, lands):
            w, m, v = big[name]
            shp = own.shape[1:]
            outs = adamw_big(w.reshape(shp), m.reshape(shp), v.reshape(shp), own, got, self.chip, "adamw_" + name)
            out[name] = [o.reshape(w.shape) for o in outs]
        return out


def _update_small(small_g, loss_blk, small_w, small_m, small_v, dev, d, cl, deps):
    packed = jnp.concatenate([_rows128(g) for g in small_g] + [loss_blk], axis=0)
    (g8,) = all_gather([packed], "ag_small_grads", deps)
    tot = sum_devices(g8, "sum_small_grads")
    loss = tot[packed.shape[0] - 8, 0]
    full_g = []
    lo = 0
    for g in small_g:
        rows = g.size // 128
        full_g.append(tot[lo:lo + rows].reshape(g.shape))
        lo += rows

    def mine(g, width):
        return lax.dynamic_slice_in_dim(g, dev * width, width, axis=g.ndim - 1)

    fg = full_g
    small_gl = [fg[0], fg[1], fg[2], mine(fg[3], d // N_DEV), mine(fg[4], cl), mine(fg[5], cl), mine(fg[6], cl),
                mine(fg[7], cl), mine(fg[8], cl), mine(fg[9], d // N_DEV)]
    sd, sm, sv = adamw_small(small_w, small_gl, small_m, small_v, "adamw_small")

    def like(k, a):
        return a[None] if k in (4, 5) else a

    sg = [like(k, a) for k, a in enumerate(small_gl)]
    sd = [like(k, a) for k, a in enumerate(sd)]
    sm = [like(k, a) for k, a in enumerate(sm)]
    sv = [like(k, a) for k, a in enumerate(sv)]
    return sg, sd, sm, sv, loss
```

```python
import functools
import math

import jax
import jax.numpy as jnp
from jax import lax
from jax.experimental import pallas as pl
from jax.experimental.pallas import tpu as pltpu

F32 = jnp.float32
BF16 = jnp.bfloat16
EPS = 1e-6
HEAD_DIM = 128
POOL_WINDOWS = (2, 4, 8, 16)
SCONV_K = 3
CONF_K = 31
HALO = 32
N_DEV = 8
VMEM_LIMIT = 56 * 1024 * 1024
MESH = pl.DeviceIdType.MESH

ADAM_LR = 0.001
ADAM_B1 = 0.9
ADAM_B2 = 0.999
ADAM_EPS = 1e-08
ADAM_WD = 0.01
ADAM_STEP = 10


def _params(*sem):
    return pltpu.CompilerParams(dimension_semantics=sem, vmem_limit_bytes=VMEM_LIMIT)


def _sigmoid(v):
    return 1.0 / (1.0 + jnp.exp(-v))


def _silu(v):
    return v * _sigmoid(v)


def _silu_and_grad(v):
    s = _sigmoid(v)
    return v * s, s * (1.0 + v * (1.0 - s))


def _rowsum8(v):
    r, c = v.shape
    return jnp.sum(v.reshape(r // 8, 8, c), axis=0)


SUBLANES = 8


class _Taps:
    def __init__(self, xx, rows, before):
        self.xx, self.rows, self.before, self.rotated = xx, rows, before, {}

    def __call__(self, i):
        r, q = i % SUBLANES, i // SUBLANES
        if r not in self.rotated:
            n = self.xx.shape[0]
            self.rotated[r] = self.xx if r == 0 else pltpu.roll(self.xx, r if self.before else n - r, 0)
        lo = HALO - SUBLANES * q if self.before else SUBLANES * q
        return self.rotated[r][lo:lo + self.rows]


def _window_sum(xx, win, before):
    n = xx.shape[0]
    acc = xx
    k = 1
    while k < win:
        acc = acc + pltpu.roll(acc, k if before else n - k, 0)
        k *= 2
    return acc


def postnorm_fwd(x, o, g, g_next, name, tm=512, dep=None):
    s, d = x.shape
    dep_args, dep_specs = _after(dep)

    def body(x_ref, o_ref, g_ref, gn_ref, *rest):
        y_ref, h_ref = rest[-2:]
        ov = o_ref[...].astype(F32)
        r = lax.rsqrt(jnp.mean(ov * ov, axis=-1, keepdims=True) + EPS)
        y = x_ref[...] + ov * r * g_ref[...]
        y_ref[...] = y
        r2 = lax.rsqrt(jnp.mean(y * y, axis=-1, keepdims=True) + EPS)
        h_ref[...] = (y * r2 * gn_ref[...]).astype(BF16)

    row = pl.BlockSpec((tm, d), lambda i: (i, 0))
    vec = pl.BlockSpec((1, d), lambda i: (0, 0))
    return pl.pallas_call(
        body, name=name, grid=(s // tm,),
        in_specs=[row, row, vec, vec] + dep_specs, out_specs=[row, row],
        out_shape=[jax.ShapeDtypeStruct((s, d), F32), jax.ShapeDtypeStruct((s, d), BF16)],
        compiler_params=_params("parallel"),
    )(x, o, g, g_next, *dep_args)


def final_fwd_bwd(x1, o, g, target, name, tm=512):
    s, d = x1.shape
    n = s // tm

    def body(x_ref, o_ref, g_ref, t_ref, loss_ref, gx_ref, do_ref, dg_ref, lacc, gacc):
        i = pl.program_id(0)

        @pl.when(i == 0)
        def _():
            lacc[...] = jnp.zeros_like(lacc)
            gacc[...] = jnp.zeros_like(gacc)

        ov = o_ref[...].astype(F32)
        gv = g_ref[...]
        r = lax.rsqrt(jnp.mean(ov * ov, axis=-1, keepdims=True) + EPS)
        oh = ov * r
        diff = x_ref[...] + oh * gv - t_ref[...]
        lacc[...] += _rowsum8(diff * diff)
        gx = diff * (1.0 / d)
        gx_ref[...] = gx
        gacc[...] += _rowsum8(gx * oh)
        dn = gx * gv
        do_ref[...] = (r * (dn - oh * jnp.mean(dn * oh, axis=-1, keepdims=True))).astype(BF16)

        @pl.when(i == n - 1)
        def _():
            tot = jnp.sum(jnp.sum(lacc[...], axis=0, keepdims=True), axis=1, keepdims=True)
            loss_ref[...] = jnp.broadcast_to(tot * (0.5 / d), loss_ref.shape)
            dg_ref[...] = jnp.sum(gacc[...], axis=0, keepdims=True)

    row = pl.BlockSpec((tm, d), lambda i: (i, 0))
    vec = pl.BlockSpec((1, d), lambda i: (0, 0))
    return pl.pallas_call(
        body, name=name, grid=(n,),
        in_specs=[row, row, vec, row],
        out_specs=[pl.BlockSpec((8, 128), lambda i: (0, 0)), row, row, vec],
        out_shape=[jax.ShapeDtypeStruct((8, 128), F32), jax.ShapeDtypeStruct((s, d), F32),
                   jax.ShapeDtypeStruct((s, d), BF16), jax.ShapeDtypeStruct((1, d), F32)],
        scratch_shapes=[pltpu.VMEM((8, d), F32), pltpu.VMEM((8, d), F32)],
        compiler_params=_params("arbitrary"),
    )(x1, o, g, target)


def _rms_bwd_rows(dyv, xv, gv):
    r = lax.rsqrt(jnp.mean(xv * xv, axis=-1, keepdims=True) + EPS)
    xh = xv * r
    dn = dyv * gv
    return r * (dn - xh * jnp.mean(dn * xh, axis=-1, keepdims=True)), _rowsum8(dyv * xh)


def norm_bwd(dy, inp, g, resid, name, inp2=None, g2=None, tm=256, dep=None):
    s, d = inp.shape
    n = s // tm
    chain = inp2 is not None

    def body(*refs):
        dy_ref, x_ref, g_ref, r_ref = refs[:4]
        outs = refs[-6:] if chain else refs[-3:]
        i = pl.program_id(0)

        @pl.when(i == 0)
        def _():
            for acc in outs[-2:] if chain else outs[-1:]:
                acc[...] = jnp.zeros_like(acc)

        if chain:
            x2_ref, g2_ref = refs[4:6]
            dx_ref, dg_ref, dx2_ref, dg2_ref, gacc, gacc2 = outs
        else:
            dx_ref, dg_ref, gacc = outs
        dx, dg_rows = _rms_bwd_rows(dy_ref[...].astype(F32), x_ref[...], g_ref[...])
        dx = dx + r_ref[...]
        dx_ref[...] = dx
        gacc[...] += dg_rows
        if chain:
            dx2, dg2_rows = _rms_bwd_rows(dx, x2_ref[...].astype(F32), g2_ref[...])
            dx2_ref[...] = dx2.astype(BF16)
            gacc2[...] += dg2_rows

        @pl.when(i == n - 1)
        def _():
            dg_ref[...] = jnp.sum(gacc[...], axis=0, keepdims=True)
            if chain:
                dg2_ref[...] = jnp.sum(gacc2[...], axis=0, keepdims=True)

    row = pl.BlockSpec((tm, d), lambda i: (i, 0))
    vec = pl.BlockSpec((1, d), lambda i: (0, 0))
    dep_args, dep_specs = _after(dep)
    extra = [inp2, g2] if chain else []
    return pl.pallas_call(
        body, name=name, grid=(n,),
        in_specs=[row, row, vec, row] + ([row, vec] if chain else []) + dep_specs,
        out_specs=[row, vec] * (2 if chain else 1),
        out_shape=[jax.ShapeDtypeStruct((s, d), F32), jax.ShapeDtypeStruct((1, d), F32)]
        + ([jax.ShapeDtypeStruct((s, d), BF16), jax.ShapeDtypeStruct((1, d), F32)] if chain else []),
        scratch_shapes=[pltpu.VMEM((8, d), F32)] * (2 if chain else 1),
        compiler_params=_params("arbitrary"),
    )(dy, inp, g, resid, *extra, *dep_args)


def _after(dep):
    if dep is None:
        return [], []
    return [dep], [pl.BlockSpec((8, 128), lambda *_: (0, 0))]


def _lane_concat(ref, count):
    return ref[0] if count == 1 else jnp.concatenate([ref[i] for i in range(count)], axis=1)


def mm_nn(a, w, out_dtype, name, tm=2048, tn=None, dep=None, group=1):
    m, k = a.shape
    tm = min(tm, m)
    ns, _, n = w.shape
    tn = n if tn is None else tn
    nj = n // tn
    assert group == 1 or nj == 1
    dep_args, dep_specs = _after(dep)

    def body(a_ref, w_ref, *rest):
        o_ref = rest[-1]
        o_ref[...] = jnp.dot(a_ref[...], _lane_concat(w_ref, group), preferred_element_type=F32).astype(out_dtype)

    return pl.pallas_call(
        body, name=name, grid=(ns // group, nj, m // tm),
        in_specs=[pl.BlockSpec((tm, k), lambda s, j, i: (i, 0)),
                  pl.BlockSpec((group, k, tn), lambda s, j, i: (s, 0, j))] + dep_specs,
        out_specs=pl.BlockSpec((tm, group * tn), lambda s, j, i: (i, s * nj + j)),
        out_shape=jax.ShapeDtypeStruct((m, ns * n), out_dtype),
        compiler_params=_params("parallel", "parallel", "parallel"),
    )(a, w, *dep_args)


def mm_nt(a, w, out_dtype, name, tm=1024, tn=None, dep=None, group=1):
    m = a.shape[0]
    tm = min(tm, m)
    ns, k, n = w.shape
    tn = n if tn is None else tn
    nj = n // tn
    assert group == 1 or nj == 1
    steps = ns * nj // group
    dep_args, dep_specs = _after(dep)

    def body(a_ref, w_ref, *rest):
        o_ref, acc = rest[-2:]
        r = pl.program_id(1)

        @pl.when(r == 0)
        def _():
            acc[...] = jnp.zeros_like(acc)

        acc[...] += lax.dot_general(a_ref[...], _lane_concat(w_ref, group), (((1,), (1,)), ((), ())),
                                    preferred_element_type=F32)

        @pl.when(r == steps - 1)
        def _():
            o_ref[...] = acc[...].astype(out_dtype)

    return pl.pallas_call(
        body, name=name, grid=(m // tm, steps),
        in_specs=[pl.BlockSpec((tm, group * tn), lambda i, r: (i, r)),
                  pl.BlockSpec((group, k, tn), lambda i, r: (r // nj, 0, r % nj))] + dep_specs,
        out_specs=pl.BlockSpec((tm, k), lambda i, r: (i, 0)),
        out_shape=jax.ShapeDtypeStruct((m, k), out_dtype),
        scratch_shapes=[pltpu.VMEM((tm, k), F32)],
        compiler_params=_params("parallel", "arbitrary"),
    )(a, w, *dep_args)


def mm_tn(a, b, ns, out_dtype, name, tk=1024, tm=2048, dep=None, pick=None, group=1):
    m, k = a.shape
    tm = min(tm, m)
    step, offset = (1, None) if pick is None else pick
    assert group == 1 or pick is None
    n = b.shape[1] // (ns * step)
    steps = m // tm
    dep_args, dep_specs = _after(dep)
    n_pre = 0 if pick is None else 1

    def b_block(s, j, r, *pre):
        return (r, s if pick is None else step * s + pre[0][0])

    def body(*refs):
        a_ref, b_ref = refs[n_pre:n_pre + 2]
        o_ref, acc = refs[-2:]
        r = pl.program_id(2)

        @pl.when(r == 0)
        def _():
            acc[...] = jnp.zeros_like(acc)

        acc[...] += lax.dot_general(a_ref[...], b_ref[...], (((0,), (0,)), ((), ())),
                                    preferred_element_type=F32)

        @pl.when(r == steps - 1)
        def _():
            for i in range(group):
                o_ref[i] = acc[:, i * n:(i + 1) * n].astype(out_dtype)

    return pl.pallas_call(
        body, name=name,
        grid_spec=pltpu.PrefetchScalarGridSpec(
            num_scalar_prefetch=n_pre, grid=(ns // group, k // tk, steps),
            in_specs=[pl.BlockSpec((tm, tk), lambda s, j, r, *pre: (r, j)),
                      pl.BlockSpec((tm, group * n), b_block)] + dep_specs,
            out_specs=pl.BlockSpec((group, tk, n), lambda s, j, r, *pre: (s, j, 0)),
            scratch_shapes=[pltpu.VMEM((tk, group * n), F32)]),
        out_shape=jax.ShapeDtypeStruct((ns, k, n), out_dtype),
        compiler_params=_params("parallel", "parallel", "arbitrary"),
    )(*([] if pick is None else [offset]), a, b, *dep_args)


SB_BLK = 128


LOG2E = 1.0 / math.log(2.0)


def _split_dot(v, tri2):
    hi = pltpu.bitcast(pltpu.bitcast(v, jnp.uint32) & jnp.uint32(0xFFFF0000), F32)
    lo = (v - hi).astype(BF16)
    return jnp.dot(jnp.concatenate([hi.astype(BF16), lo], axis=1), tri2, preferred_element_type=F32)


def _sb_scores(z2, lim, dcol, tri_ex, masked):
    sp = jnp.log2(1.0 + jnp.exp2(-jnp.abs(z2)))
    lb = jnp.minimum(z2, 0.0) - sp
    l1m = lb - z2
    mask = None
    if masked:
        mask = dcol < lim
        l1m = jnp.where(mask, l1m, 0.0)
    return mask, lb, l1m, _split_dot(l1m, tri_ex)


def _sb_consts():
    row = lax.broadcasted_iota(jnp.int32, (SB_BLK, SB_BLK), 0)
    col = lax.broadcasted_iota(jnp.int32, (SB_BLK, SB_BLK), 1)
    tri_ex = jnp.where(row > col, 1.0, 0.0).astype(BF16)
    tri_in = jnp.where(row >= col, 1.0, 0.0).astype(BF16)
    return col - row, jnp.concatenate([tri_ex, tri_ex], axis=0), jnp.concatenate([tri_in, tri_in], axis=0)


def sb_fwd(p, n_heads, name, tq=1024, nsub=8, dep=None):
    s = p.shape[0]
    h_n = n_heads
    b = SB_BLK
    nqs = tq // b
    tk = nsub * b
    scale = 1.0 / math.sqrt(HEAD_DIM)

    dep_args, dep_specs = _after(dep)

    def body(q_ref, k_ref, v_ref, *rest):
        o_ref, w_ref = rest[-2:]
        qi = pl.program_id(1)
        dcol, tri_ex, _ = _sb_consts()
        qv = [q_ref[qs * b:(qs + 1) * b, :] for qs in range(nqs)]
        n_groups = ((qi + 1) * nqs - 1) // nsub + 1

        def step(it, carry, masked):
            c1s, accs = carry
            g = n_groups - 1 - it
            off = pl.multiple_of(g * tk, tk)
            kg = k_ref[pl.ds(off, tk), :]
            vg = v_ref[pl.ds(off, tk), :]
            new_c1, new_acc = [], []
            for qs in range(nqs):
                qb = qi * nqs + qs
                square = masked and nqs == nsub
                nk = qs + 1 if square else nsub
                kq, vq = kg[:nk * b], vg[:nk * b]
                z2 = lax.dot_general(qv[qs], kq, (((1,), (1,)), ((), ())),
                                     preferred_element_type=F32) * (scale * LOG2E)
                blocks = [_sb_scores(z2[:, j * b:(j + 1) * b], (qb - (g * nsub + j)) * b, dcol, tri_ex,
                                     masked and (j == qs or not square)) for j in range(nk)]
                run = c1s[qs]
                ws = [None] * nk
                for j in reversed(range(nk)):
                    mask, lb, l1m, ls_loc = blocks[j]
                    wj = jnp.exp2(lb + ls_loc + run)
                    ws[j] = (wj if mask is None else jnp.where(mask, wj, 0.0)).astype(BF16)
                    run = run + jnp.sum(l1m, axis=1, keepdims=True)
                w = jnp.concatenate(ws, axis=1)
                w_ref[0, g, qs * b:(qs + 1) * b, 0:nk * b] = w
                new_acc.append(accs[qs] + jnp.dot(w, vq, preferred_element_type=F32))
                new_c1.append(run)
            return tuple(new_c1), tuple(new_acc)

        init = (tuple(jnp.zeros((b, 1), F32) for _ in range(nqs)),
                tuple(jnp.zeros((b, HEAD_DIM), F32) for _ in range(nqs)))
        assert all(((i + 1) * nqs - 1) // nsub * nsub <= i * nqs for i in range(s // tq))
        first = step(0, init, True)
        _, accs = lax.fori_loop(1, n_groups, functools.partial(step, masked=False), first)
        for qs in range(nqs):
            o_ref[qs * b:(qs + 1) * b, :] = accs[qs]

    return pl.pallas_call(
        body, name=name, grid=(h_n, s // tq),
        in_specs=[pl.BlockSpec((tq, HEAD_DIM), lambda h, i: (i, h)),
                  pl.BlockSpec((s, HEAD_DIM), lambda h, i: (0, h_n + h)),
                  pl.BlockSpec((s, HEAD_DIM), lambda h, i: (0, 2 * h_n + h))] + dep_specs,
        out_specs=[pl.BlockSpec((tq, HEAD_DIM), lambda h, i: (i, h)),
                   pl.BlockSpec((1, s // tk, tq, tk), lambda h, i: (h, 0, i, 0))],
        out_shape=[jax.ShapeDtypeStruct((s, h_n * HEAD_DIM), F32),
                   jax.ShapeDtypeStruct((h_n, s // tk, s, tk), BF16)],
        compiler_params=_params("parallel", "arbitrary"),
    )(p, p, p, *dep_args)


def sb_bwd(p, a, wts, da, n_heads, name, tq=1024, dep=None):
    s = p.shape[0]
    h_n = n_heads
    nq = s // tq
    b = SB_BLK
    nqs = tq // b
    tk = wts.shape[3]
    nsub = tk // b
    scale = 1.0 / math.sqrt(HEAD_DIM)
    dep_args, dep_specs = _after(dep)

    def body(q_ref, k_ref, v_ref, a_ref, da_ref, w_ref, *rest):
        dq_ref, dk_ref, dv_ref, dk_acc, dv_acc = rest[-5:]
        qi = pl.program_id(1)

        @pl.when(qi == 0)
        def _():
            dk_acc[...] = jnp.zeros_like(dk_acc)
            dv_acc[...] = jnp.zeros_like(dv_acc)

        dcol, _, tri_in = _sb_consts()
        q_all = q_ref[...]
        do_all = da_ref[...]
        qv = [q_ref[qs * b:(qs + 1) * b, :] for qs in range(nqs)]
        dov = [da_ref[qs * b:(qs + 1) * b, :] for qs in range(nqs)]
        tots = [jnp.sum(dov[qs].astype(F32) * a_ref[qs * b:(qs + 1) * b, :], axis=1, keepdims=True)
                for qs in range(nqs)]
        n_groups = ((qi + 1) * nqs - 1) // nsub + 1

        def step(it, carry, masked):
            c2s, dqs = carry
            g = n_groups - 1 - it
            off = pl.multiple_of(g * tk, tk)
            kg = k_ref[pl.ds(off, tk), :]
            vg = v_ref[pl.ds(off, tk), :]
            square = masked and nqs == nsub
            new_c2, new_dq, dz_rows, w_rows = [], [], [], []
            for qs in range(nqs):
                qb = qi * nqs + qs
                nk = qs + 1 if square else nsub
                kq, vq = kg[:nk * b], vg[:nk * b]
                z2 = lax.dot_general(qv[qs], kq, (((1,), (1,)), ((), ())),
                                     preferred_element_type=F32) * (-scale * LOG2E)
                dw = lax.dot_general(dov[qs], vq, (((1,), (1,)), ((), ())), preferred_element_type=F32)
                beta = 1.0 / (1.0 + jnp.exp2(z2))
                wq = w_ref[0, g, qs * b:(qs + 1) * b, 0:nk * b]
                e = dw * wq.astype(F32)
                run2 = c2s[qs]
                dzs = [None] * nk
                for j in reversed(range(nk)):
                    cols = slice(j * b, (j + 1) * b)
                    later = _split_dot(e[:, cols], tri_in) + run2
                    bj = beta[:, cols]
                    dz = (e[:, cols] * (1.0 - bj) - bj * (tots[qs] - later)) * scale
                    if masked and (j == qs or not square):
                        dz = jnp.where(dcol < (qb - (g * nsub + j)) * b, dz, 0.0)
                    dzs[j] = dz.astype(BF16)
                    run2 = run2 + jnp.sum(e[:, cols], axis=1, keepdims=True)
                dzq = jnp.concatenate(dzs, axis=1)
                new_dq.append(dqs[qs] + jnp.dot(dzq, kq, preferred_element_type=F32))
                new_c2.append(run2)
                pad = [jnp.zeros((b, (nsub - nk) * b), BF16)] if nk < nsub else []
                dz_rows.append(jnp.concatenate([dzq] + pad, axis=1))
                w_rows.append(jnp.concatenate([wq] + pad, axis=1))
            dz_all = jnp.concatenate(dz_rows, axis=0)
            w_all = jnp.concatenate(w_rows, axis=0)
            dk_acc[pl.ds(off, tk), :] += lax.dot_general(dz_all, q_all, (((0,), (0,)), ((), ())),
                                                         preferred_element_type=F32)
            dv_acc[pl.ds(off, tk), :] += lax.dot_general(w_all, do_all, (((0,), (0,)), ((), ())),
                                                         preferred_element_type=F32)
            return tuple(new_c2), tuple(new_dq)

        zeros = tuple(jnp.zeros((b, 1), F32) for _ in range(nqs))
        assert all(((i + 1) * nqs - 1) // nsub * nsub <= i * nqs for i in range(s // tq))
        first = step(0, (zeros, tuple(jnp.zeros((b, HEAD_DIM), F32) for _ in range(nqs))), True)
        _, dqs = lax.fori_loop(1, n_groups, functools.partial(step, masked=False), first)
        for qs in range(nqs):
            dq_ref[qs * b:(qs + 1) * b, :] = dqs[qs].astype(BF16)

        @pl.when(qi == nq - 1)
        def _():
            dk_ref[...] = dk_acc[...].astype(BF16)
            dv_ref[...] = dv_acc[...].astype(BF16)

    blk = pl.BlockSpec((tq, HEAD_DIM), lambda h, i: (i, h))
    full = pl.BlockSpec((s, HEAD_DIM), lambda h, i: (0, h))
    return pl.pallas_call(
        body, name=name, grid=(h_n, nq),
        in_specs=[blk, pl.BlockSpec((s, HEAD_DIM), lambda h, i: (0, h_n + h)),
                  pl.BlockSpec((s, HEAD_DIM), lambda h, i: (0, 2 * h_n + h)), blk, blk,
                  pl.BlockSpec((1, s // tk, tq, tk), lambda h, i: (h, 0, i, 0))] + dep_specs,
        out_specs=[blk, full, full],
        out_shape=[jax.ShapeDtypeStruct((s, h_n * HEAD_DIM), BF16)] * 3,
        scratch_shapes=[pltpu.VMEM((s, HEAD_DIM), F32), pltpu.VMEM((s, HEAD_DIM), F32)],
        compiler_params=_params("parallel", "arbitrary"),
    )(p, p, p, a, da, wts, *dep_args)


def _pool_window(xx, win, r0, rc):
    cur = xx[HALO:HALO + rc]
    ws = _window_sum(xx, win, True)[HALO:HALO + rc]
    t_idx = r0 + lax.broadcasted_iota(jnp.int32, (rc, 1), 0)
    inv = 1.0 / jnp.minimum(win, t_idx + 1).astype(F32)
    return ws * inv - cur, inv


def even_mix_fwd(a, p, pool_w, pool_scale, name, rc=512, dep=None):
    s = p.shape[0]
    ng = len(POOL_WINDOWS)
    cw = pool_w.shape[1]
    n_chunks = s // rc
    dep_args, dep_specs = _after(dep)

    def body(a_ref, u_ref, g_ref, w_ref, sc_ref, *rest):
        y_ref, upad = rest[-2:]
        j = pl.program_id(0)

        @pl.when(j < ng)
        def _():
            def chunk(ci, carry):
                rows = pl.ds(pl.multiple_of(ci * rc, rc), rc)
                y_ref[rows, :] = (a_ref[rows, :] * _silu(g_ref[rows, :].astype(F32))).astype(BF16)
                return carry

            lax.fori_loop(0, n_chunks, chunk, 0)

        for gi, win in enumerate(POOL_WINDOWS):
            @pl.when(j == ng + gi)
            def _(win=win):
                upad[0:HALO, :] = jnp.zeros((HALO, cw), F32)

                def fill(ci, carry):
                    r0 = pl.multiple_of(ci * rc, rc)
                    upad[pl.ds(pl.multiple_of(r0 + HALO, HALO), rc), :] = u_ref[pl.ds(r0, rc), :].astype(F32)
                    return carry

                lax.fori_loop(0, n_chunks, fill, 0)

                def chunk(ci, carry):
                    r0 = pl.multiple_of(ci * rc, rc)
                    rows = pl.ds(r0, rc)
                    pooled, _ = _pool_window(upad[pl.ds(r0, HALO + rc), :], win, r0, rc)
                    t = jnp.dot(pooled.astype(BF16), w_ref[0], preferred_element_type=F32)
                    y_ref[rows, :] = (t * sc_ref[...] * _silu(g_ref[rows, :].astype(F32))).astype(BF16)
                    return carry

                lax.fori_loop(0, n_chunks, chunk, 0)

    grp = lambda j: jnp.maximum(j - ng, 0)
    return pl.pallas_call(
        body, name=name, grid=(2 * ng,),
        in_specs=[pl.BlockSpec((s, cw), lambda j: (0, jnp.minimum(j, ng - 1))),
                  pl.BlockSpec((s, cw), lambda j: (0, 3 * ng + grp(j))),
                  pl.BlockSpec((s, cw), lambda j: (0, 4 * ng + j)),
                  pl.BlockSpec((1, cw, cw), lambda j: (grp(j), 0, 0)),
                  pl.BlockSpec((1, cw), lambda j: (0, grp(j)))] + dep_specs,
        out_specs=pl.BlockSpec((s, cw), lambda j: (0, j)),
        out_shape=jax.ShapeDtypeStruct((s, 2 * ng * cw), BF16),
        scratch_shapes=[pltpu.VMEM((HALO + s, cw), F32)],
        compiler_params=_params("arbitrary"),
    )(a, p, p, pool_w, pool_scale, *dep_args)


def even_mix_bwd(dy, a, p, pool_w, pool_scale, name, rc=512):
    s = p.shape[0]
    ng = len(POOL_WINDOWS)
    cw = pool_w.shape[1]
    n_chunks = s // rc

    def body(dy_ref, a_ref, u_ref, g_ref, w_ref, sc_ref, da_ref, du_ref, dg_ref, dw_ref, dsc_ref,
             upad, rpad, dpl, dw_acc, dsc_acc):
        j = pl.program_id(0)

        @pl.when(j < ng)
        def _():
            def chunk(ci, carry):
                rows = pl.ds(pl.multiple_of(ci * rc, rc), rc)
                dyv = dy_ref[rows, :].astype(F32)
                sg, dsg = _silu_and_grad(g_ref[rows, :].astype(F32))
                da_ref[rows, :] = (dyv * sg).astype(BF16)
                dg_ref[rows, :] = (dyv * a_ref[rows, :] * dsg).astype(BF16)
                return carry

            lax.fori_loop(0, n_chunks, chunk, 0)

        for gi, win in enumerate(POOL_WINDOWS):
            @pl.when(j == ng + gi)
            def _(win=win):
                upad[0:HALO, :] = jnp.zeros((HALO, cw), F32)
                rpad[s:s + HALO, :] = jnp.zeros((HALO, cw), F32)
                dw_acc[...] = jnp.zeros_like(dw_acc)
                dsc_acc[...] = jnp.zeros_like(dsc_acc)

                def fill(ci, carry):
                    r0 = pl.multiple_of(ci * rc, rc)
                    upad[pl.ds(pl.multiple_of(r0 + HALO, HALO), rc), :] = u_ref[pl.ds(r0, rc), :].astype(F32)
                    return carry

                lax.fori_loop(0, n_chunks, fill, 0)

                def chunk(ci, carry):
                    r0 = pl.multiple_of(ci * rc, rc)
                    rows = pl.ds(r0, rc)
                    pooled, inv = _pool_window(upad[pl.ds(r0, HALO + rc), :], win, r0, rc)
                    pb = pooled.astype(BF16)
                    wv = w_ref[0]
                    t = jnp.dot(pb, wv, preferred_element_type=F32)
                    scv = sc_ref[...]
                    dyv = dy_ref[rows, :].astype(F32)
                    sg, dsg = _silu_and_grad(g_ref[rows, :].astype(F32))
                    dpo = dyv * sg
                    dg_ref[rows, :] = (dyv * t * scv * dsg).astype(BF16)
                    dsc_acc[...] += _rowsum8(dpo * t)
                    dtb = (dpo * scv).astype(BF16)
                    dw_acc[...] += lax.dot_general(pb, dtb, (((0,), (0,)), ((), ())),
                                                   preferred_element_type=F32)
                    dpooled = lax.dot_general(dtb, wv, (((1,), (1,)), ((), ())),
                                              preferred_element_type=F32)
                    dpl[rows, :] = dpooled
                    rpad[rows, :] = dpooled * inv
                    return carry

                lax.fori_loop(0, n_chunks, chunk, 0)

                def chunk2(ci, carry):
                    r0 = pl.multiple_of(ci * rc, rc)
                    rows = pl.ds(r0, rc)
                    xx = rpad[pl.ds(r0, rc + HALO), :]
                    fs = _window_sum(xx, win, False)[0:rc]
                    du_ref[rows, :] = (fs - dpl[rows, :]).astype(BF16)
                    return carry

                lax.fori_loop(0, n_chunks, chunk2, 0)
                dw_ref[0] = dw_acc[...]
                dsc_ref[...] = jnp.sum(dsc_acc[...], axis=0, keepdims=True)

    grp = lambda j: jnp.maximum(j - ng, 0)
    att = lambda j: jnp.minimum(j, ng - 1)
    return pl.pallas_call(
        body, name=name, grid=(2 * ng,),
        in_specs=[pl.BlockSpec((s, cw), lambda j: (0, j)),
                  pl.BlockSpec((s, cw), lambda j: (0, att(j))),
                  pl.BlockSpec((s, cw), lambda j: (0, 3 * ng + grp(j))),
                  pl.BlockSpec((s, cw), lambda j: (0, 4 * ng + j)),
                  pl.BlockSpec((1, cw, cw), lambda j: (grp(j), 0, 0)),
                  pl.BlockSpec((1, cw), lambda j: (0, grp(j)))],
        out_specs=[pl.BlockSpec((s, cw), lambda j: (0, att(j))),
                   pl.BlockSpec((s, cw), lambda j: (0, grp(j))),
                   pl.BlockSpec((s, cw), lambda j: (0, j)),
                   pl.BlockSpec((1, cw, cw), lambda j: (grp(j), 0, 0)),
                   pl.BlockSpec((1, cw), lambda j: (0, grp(j)))],
        out_shape=[jax.ShapeDtypeStruct((s, ng * cw), BF16), jax.ShapeDtypeStruct((s, ng * cw), BF16),
                   jax.ShapeDtypeStruct((s, 2 * ng * cw), BF16),
                   jax.ShapeDtypeStruct((ng, cw, cw), F32), jax.ShapeDtypeStruct((1, ng * cw), F32)],
        scratch_shapes=[pltpu.VMEM((HALO + s, cw), F32), pltpu.VMEM((s + HALO, cw), F32),
                        pltpu.VMEM((s, cw), F32), pltpu.VMEM((cw, cw), F32), pltpu.VMEM((8, cw), F32)],
        compiler_params=_params("arbitrary"),
    )(dy, a, p, p, pool_w, pool_scale)


def _halo_before(tm):
    return lambda i: jnp.maximum(i * (tm // HALO) - 1, 0)


def _halo_after(tm, s):
    return lambda i: jnp.minimum((i + 1) * (tm // HALO), s // HALO - 1)


def odd_mix_fwd(p, sconv_w, dconv_w, dconv_b, cnorm_g, cnorm_b, name, tm=128, dep=None):
    s = p.shape[0]
    cw = sconv_w.shape[1]
    n = s // tm
    lanes = 128
    hb = _halo_before(tm)

    dep_args, dep_specs = _after(dep)

    def body(hc_ref, hch_ref, bc_ref, cc_ref, cch_ref, ga_ref, gah_ref, gb_ref, gbh_ref, g1_ref, g2_ref,
             sw_ref, dw_ref, db_ref, gam_ref, bet_ref, *rest):
        y_ref, dc_ref = rest[-2:]
        first = pl.program_id(0) == 0
        for l in range(cw // lanes):
            cols = slice(l * lanes, (l + 1) * lanes)
            mh = jnp.where(first, 0.0, cch_ref[:, cols].astype(F32) * hch_ref[:, cols].astype(F32))
            mm = cc_ref[:, cols].astype(F32) * hc_ref[:, cols].astype(F32)
            xx = jnp.concatenate([mh, mm], axis=0)
            tap = _Taps(xx, tm, True)
            cv = jnp.zeros((tm, lanes), F32)
            for k in range(SCONV_K):
                cv = cv + sw_ref[k:k + 1, cols] * tap(SCONV_K - 1 - k)
            c_out = bc_ref[:, cols].astype(F32) * cv
            y_ref[:, cols] = (c_out * _silu(g1_ref[:, cols].astype(F32))).astype(BF16)
            dh = jnp.where(first, 0.0, gah_ref[:, cols].astype(F32) * _sigmoid(gbh_ref[:, cols].astype(F32)))
            dm = ga_ref[:, cols].astype(F32) * _sigmoid(gb_ref[:, cols].astype(F32))
            xx = jnp.concatenate([dh, dm], axis=0)
            tap = _Taps(xx, tm, True)
            acc = jnp.zeros((tm, lanes), F32) + db_ref[:, cols]
            for k in range(CONF_K):
                acc = acc + dw_ref[k:k + 1, cols] * tap(CONF_K - 1 - k)
            dc_ref[:, cols] = acc
        rs = 64
        for r in range(tm // rs):
            rows = slice(r * rs, (r + 1) * rs)
            xv = dc_ref[rows, :]
            mu = jnp.mean(xv, axis=-1, keepdims=True)
            xc = xv - mu
            rstd = lax.rsqrt(jnp.mean(xc * xc, axis=-1, keepdims=True) + EPS)
            ln = xc * rstd * gam_ref[...] + bet_ref[...]
            y_ref[rows, cw:2 * cw] = (_silu(ln) * _silu(g2_ref[rows, :].astype(F32))).astype(BF16)

    main = lambda c: pl.BlockSpec((tm, cw), lambda i: (i, c))
    halo = lambda c: pl.BlockSpec((HALO, cw), lambda i: (hb(i), c))
    vec = lambda r: pl.BlockSpec((r, cw), lambda i: (0, 0))
    return pl.pallas_call(
        body, name=name, grid=(n,),
        in_specs=[main(0), halo(0), main(1), main(2), halo(2), main(3), halo(3), main(4), halo(4),
                  main(5), main(6), vec(SCONV_K), vec(CONF_K), vec(1), vec(1), vec(1)] + dep_specs,
        out_specs=[pl.BlockSpec((tm, 2 * cw), lambda i: (i, 0)), pl.BlockSpec((tm, cw), lambda i: (i, 0))],
        out_shape=[jax.ShapeDtypeStruct((s, 2 * cw), BF16), jax.ShapeDtypeStruct((s, cw), F32)],
        compiler_params=_params("parallel"),
    )(p, p, p, p, p, p, p, p, p, p, p, sconv_w, dconv_w, dconv_b, cnorm_g, cnorm_b, *dep_args)


def odd_bwd_ln(dy, p, dc, cnorm_g, cnorm_b, name, tm=256):
    s = p.shape[0]
    cw = dc.shape[1]
    n = s // tm
    rs = 128

    def body(dy_ref, g2_ref, dc_ref, gam_ref, bet_ref, ddc_ref, dg_ref, dgam_ref, dbet_ref, gacc, bacc):
        i = pl.program_id(0)

        @pl.when(i == 0)
        def _():
            gacc[...] = jnp.zeros_like(gacc)
            bacc[...] = jnp.zeros_like(bacc)

        def chunk(ci, carry):
            rows = pl.ds(pl.multiple_of(ci * rs, rs), rs)
            xv = dc_ref[rows, :]
            mu = jnp.mean(xv, axis=-1, keepdims=True)
            xc = xv - mu
            rstd = lax.rsqrt(jnp.mean(xc * xc, axis=-1, keepdims=True) + EPS)
            xh = xc * rstd
            gam = gam_ref[...]
            sl, dsl = _silu_and_grad(xh * gam + bet_ref[...])
            sg, dsg = _silu_and_grad(g2_ref[rows, :].astype(F32))
            dyv = dy_ref[rows, :].astype(F32)
            dg_ref[rows, :] = (dyv * sl * dsg).astype(BF16)
            dln = dyv * sg * dsl
            gacc[...] += _rowsum8(dln * xh)
            bacc[...] += _rowsum8(dln)
            dxh = dln * gam
            ddc_ref[rows, :] = rstd * (dxh - jnp.mean(dxh, axis=-1, keepdims=True)
                                       - xh * jnp.mean(dxh * xh, axis=-1, keepdims=True))
            return carry

        lax.fori_loop(0, tm // rs, chunk, 0)

        @pl.when(i == n - 1)
        def _():
            dgam_ref[...] = jnp.sum(gacc[...], axis=0, keepdims=True)
            dbet_ref[...] = jnp.sum(bacc[...], axis=0, keepdims=True)

    vec = pl.BlockSpec((1, cw), lambda i: (0, 0))
    return pl.pallas_call(
        body, name=name, grid=(n,),
        in_specs=[pl.BlockSpec((tm, cw), lambda i: (i, 1)), pl.BlockSpec((tm, cw), lambda i: (i, 6)),
                  pl.BlockSpec((tm, cw), lambda i: (i, 0)), vec, vec],
        out_specs=[pl.BlockSpec((tm, cw), lambda i: (i, 0)), pl.BlockSpec((tm, cw), lambda i: (i, 0)), vec, vec],
        out_shape=[jax.ShapeDtypeStruct((s, cw), F32), jax.ShapeDtypeStruct((s, cw), BF16),
                   jax.ShapeDtypeStruct((1, cw), F32), jax.ShapeDtypeStruct((1, cw), F32)],
        scratch_shapes=[pltpu.VMEM((8, cw), F32), pltpu.VMEM((8, cw), F32)],
        compiler_params=_params("arbitrary"),
    )(dy, p, dc, cnorm_g, cnorm_b)


def odd_bwd_conv(dy, p, ddc, dg2, sconv_w, dconv_w, name, tm=128):
    s = p.shape[0]
    cw = ddc.shape[1]
    n = s // tm
    lanes = 128
    hb = _halo_before(tm)
    ha = _halo_after(tm, s)

    def body(dy_ref, dya_ref, g1_ref, g1a_ref, bc_ref, bca_ref, hc_ref, hch_ref, cc_ref, cch_ref,
             ddc_ref, ddca_ref, ga_ref, gah_ref, gb_ref, gbh_ref, dg2_ref, sw_ref, dw_ref,
             dp_ref, dsw_ref, ddw_ref, ddb_ref, sw_acc, dw_acc, db_acc):
        i = pl.program_id(0)
        first = i == 0
        last = i == n - 1

        @pl.when(first)
        def _():
            sw_acc[...] = jnp.zeros_like(sw_acc)
            dw_acc[...] = jnp.zeros_like(dw_acc)
            db_acc[...] = jnp.zeros_like(db_acc)

        for l in range(cw // lanes):
            cols = slice(l * lanes, (l + 1) * lanes)
            mh = jnp.where(first, 0.0, cch_ref[:, cols].astype(F32) * hch_ref[:, cols].astype(F32))
            hcv = hc_ref[:, cols].astype(F32)
            ccv = cc_ref[:, cols].astype(F32)
            xx = jnp.concatenate([mh, ccv * hcv], axis=0)
            tap = _Taps(xx, tm, True)
            taps = [tap(SCONV_K - 1 - k) for k in range(SCONV_K)]
            cv = jnp.zeros((tm, lanes), F32)
            for k in range(SCONV_K):
                cv = cv + sw_ref[k:k + 1, cols] * taps[k]
            bcv = bc_ref[:, cols].astype(F32)
            dyv = dy_ref[:, cols].astype(F32)
            sg, dsg = _silu_and_grad(g1_ref[:, cols].astype(F32))
            dco = dyv * sg
            dp_ref[:, 5 * cw + l * lanes:5 * cw + (l + 1) * lanes] = (dyv * bcv * cv * dsg).astype(BF16)
            dp_ref[:, cw + l * lanes:cw + (l + 1) * lanes] = (dco * cv).astype(BF16)
            dcv = dco * bcv
            for k in range(SCONV_K):
                sw_acc[k * 8:(k + 1) * 8, cols] += _rowsum8(dcv * taps[k])
            dcv_a = jnp.where(last, 0.0, dya_ref[:, cols].astype(F32) * _silu(g1a_ref[:, cols].astype(F32))
                              * bca_ref[:, cols].astype(F32))
            xx = jnp.concatenate([dcv, dcv_a], axis=0)
            tap = _Taps(xx, tm, False)
            dm = jnp.zeros((tm, lanes), F32)
            for k in range(SCONV_K):
                dm = dm + sw_ref[k:k + 1, cols] * tap(SCONV_K - 1 - k)
            dp_ref[:, l * lanes:(l + 1) * lanes] = (dm * ccv).astype(BF16)
            dp_ref[:, 2 * cw + l * lanes:2 * cw + (l + 1) * lanes] = (dm * hcv).astype(BF16)
            gav = ga_ref[:, cols].astype(F32)
            sb = _sigmoid(gb_ref[:, cols].astype(F32))
            dh = jnp.where(first, 0.0, gah_ref[:, cols].astype(F32) * _sigmoid(gbh_ref[:, cols].astype(F32)))
            xx = jnp.concatenate([dh, gav * sb], axis=0)
            ddcv = ddc_ref[:, cols]
            db_acc[:, cols] += _rowsum8(ddcv)
            tap = _Taps(xx, tm, True)
            for k in range(CONF_K):
                dw_acc[k * 8:(k + 1) * 8, cols] += _rowsum8(ddcv * tap(CONF_K - 1 - k))
            ddc_a = jnp.where(last, 0.0, ddca_ref[:, cols])
            xx = jnp.concatenate([ddcv, ddc_a], axis=0)
            tap = _Taps(xx, tm, False)
            dgl = jnp.zeros((tm, lanes), F32)
            for k in range(CONF_K):
                dgl = dgl + dw_ref[k:k + 1, cols] * tap(CONF_K - 1 - k)
            dp_ref[:, 3 * cw + l * lanes:3 * cw + (l + 1) * lanes] = (dgl * sb).astype(BF16)
            dp_ref[:, 4 * cw + l * lanes:4 * cw + (l + 1) * lanes] = (dgl * gav * sb * (1.0 - sb)).astype(BF16)
        dp_ref[:, 6 * cw:7 * cw] = dg2_ref[...]

        @pl.when(last)
        def _():
            for k in range(SCONV_K):
                dsw_ref[k:k + 1, :] = jnp.sum(sw_acc[k * 8:(k + 1) * 8, :], axis=0, keepdims=True)
            for k in range(CONF_K):
                ddw_ref[k:k + 1, :] = jnp.sum(dw_acc[k * 8:(k + 1) * 8, :], axis=0, keepdims=True)
            ddb_ref[...] = jnp.sum(db_acc[...], axis=0, keepdims=True)

    def main(c):
        return pl.BlockSpec((tm, cw), lambda i: (i, c))

    def before(c):
        return pl.BlockSpec((HALO, cw), lambda i: (hb(i), c))

    def after(c):
        return pl.BlockSpec((HALO, cw), lambda i: (ha(i), c))

    def vec(r):
        return pl.BlockSpec((r, cw), lambda i: (0, 0))

    return pl.pallas_call(
        body, name=name, grid=(n,),
        in_specs=[main(0), after(0), main(5), after(5), main(1), after(1), main(0), before(0), main(2), before(2),
                  main(0), after(0), main(3), before(3), main(4), before(4), main(0), vec(SCONV_K), vec(CONF_K)],
        out_specs=[pl.BlockSpec((tm, 7 * cw), lambda i: (i, 0)), vec(SCONV_K), vec(CONF_K), vec(1)],
        out_shape=[jax.ShapeDtypeStruct((s, 7 * cw), BF16), jax.ShapeDtypeStruct((SCONV_K, cw), F32),
                   jax.ShapeDtypeStruct((CONF_K, cw), F32), jax.ShapeDtypeStruct((1, cw), F32)],
        scratch_shapes=[pltpu.VMEM((8 * SCONV_K, cw), F32), pltpu.VMEM((8 * CONF_K, cw), F32),
                        pltpu.VMEM((8, cw), F32)],
        compiler_params=_params("arbitrary"),
    )(dy, dy, p, p, p, p, p, p, p, p, ddc, ddc, p, p, p, p, dg2, sconv_w, dconv_w)


_ANY = pl.BlockSpec(memory_space=pl.ANY)


def _place():
    return lax.axis_index("x"), lax.axis_index("y"), lax.axis_index("c")


def all_gather(arrs, name, deps=()):
    n = len(arrs)

    def body(*refs):
        ins, outs = refs[:n], refs[n + len(deps):2 * n + len(deps)]
        send_sems, recv_sems, local_sems = refs[-3:]
        x, y, c = _place()
        me, sibling = (x, y, c), (x, y, 1 - c)
        chips = [(1 - x, y), (x, 1 - y), (1 - x, 1 - y)]

        def copy(a, k, block, to, src=None):
            px, py, pc = block
            dst = outs[a].at[4 * px + 2 * py + pc]
            return pltpu.make_async_remote_copy(
                src_ref=dst if src is None else src, dst_ref=dst,
                send_sem=send_sems.at[7 * a + k], recv_sem=recv_sems.at[7 * a + k],
                device_id=to, device_id_type=MESH)

        mine = [pltpu.make_async_copy(ins[a], outs[a].at[4 * x + 2 * y + c], local_sems.at[a]) for a in range(n)]
        first = []
        for a in range(n):
            first.append(copy(a, 0, me, sibling, src=ins[a]))
            first += [copy(a, 1 + j, me, (*chip, c), src=ins[a]) for j, chip in enumerate(chips)]
        for cp in first + mine:
            cp.start()
        passed = []
        for a in range(n):
            for j, chip in enumerate(chips):
                copy(a, 1 + j, (*chip, c), me).wait_recv()
                cp = copy(a, 4 + j, (*chip, c), sibling)
                cp.start()
                passed.append(cp)
        for a in range(n):
            copy(a, 0, sibling, me).wait_recv()
            for j, chip in enumerate(chips):
                copy(a, 4 + j, (*chip, 1 - c), me).wait_recv()
        for cp in first + passed:
            cp.wait_send()
        for cp in mine:
            cp.wait()

    return pl.pallas_call(
        body, name=name,
        out_shape=[jax.ShapeDtypeStruct((N_DEV,) + a.shape, a.dtype) for a in arrs],
        in_specs=[_ANY] * (n + len(deps)), out_specs=[_ANY] * n,
        scratch_shapes=[pltpu.SemaphoreType.DMA((7 * n,)), pltpu.SemaphoreType.DMA((7 * n,)),
                        pltpu.SemaphoreType.DMA((n,))],
    )(*arrs, *deps)


def in_proj_gathered(xs, g, w_own, extras, name, tm=1024, late=0):
    s, d = xs.shape
    n = w_own.shape[1]
    tm = min(tm, s)
    arrs = [w_own] + list(extras)
    na = len(arrs)
    tr = 256

    def body(*refs):
        x_ref, g_ref, ins = refs[0], refs[1], refs[2:2 + na]
        h_out, p_ref, outs = refs[2 + na], refs[3 + na], refs[4 + na:4 + 2 * na]
        (h_ref, xbuf, wbuf, obuf, send_sems, recv_sems, load_sem, store_sems, own_sems, h_sem,
         x_sems) = refs[4 + 2 * na:]
        x, y, c = _place()
        me, sibling = (x, y, c), (x, y, 1 - c)
        x_first = c == 0
        near = (jnp.where(x_first, 1 - x, x), jnp.where(x_first, y, 1 - y))
        far = (jnp.where(x_first, x, 1 - x), jnp.where(x_first, 1 - y, y))
        diag = (1 - x, 1 - y)
        k_near, k_far = jnp.where(x_first, 1, 2), jnp.where(x_first, 2, 1)
        f_near, f_far = k_near + 3, k_far + 3

        def slot(block):
            return 4 * block[0] + 2 * block[1] + block[2]

        def copy(a, k, block, to, src=None):
            dst = outs[a].at[slot(block)]
            return pltpu.make_async_remote_copy(
                src_ref=dst if src is None else src, dst_ref=dst,
                send_sem=send_sems.at[7 * a + k], recv_sem=recv_sems.at[7 * a + k],
                device_id=to, device_id_type=MESH)

        first = []
        for a in range(na):
            first += [copy(a, 0, me, sibling, src=ins[a]), copy(a, 1, me, (1 - x, y, c), src=ins[a]),
                      copy(a, 2, me, (x, 1 - y, c), src=ins[a])]
        for cp in first:
            cp.start()
        own = pltpu.make_async_copy(wbuf.at[0], outs[0].at[slot(me)], own_sems.at[0])
        mine = [pltpu.make_async_copy(ins[a], outs[a].at[slot(me)], own_sems.at[a]) for a in range(1, na)]
        for cp in mine[na - 1 - late:]:
            cp.start()
        stores = [None, None]

        def x_load(i):
            return pltpu.make_async_copy(x_ref.at[pl.ds(i * tr, tr), :], xbuf.at[i % 2], x_sems.at[i % 2])

        x_load(0).start()
        for i in range(s // tr):
            if i + 1 < s // tr:
                x_load(i + 1).start()
            x_load(i).wait()
            xv = xbuf[i % 2]
            r = lax.rsqrt(jnp.mean(xv * xv, axis=-1, keepdims=True) + EPS)
            h_ref[i * tr:(i + 1) * tr, :] = (xv * r * g_ref[...]).astype(BF16)
        h_store = pltpu.make_async_copy(h_ref, h_out, h_sem)
        h_store.start()

        def multiply(k, block, w_from):
            b = k % 2
            if k == 2:
                own.wait()
            load = pltpu.make_async_copy(w_from, wbuf.at[b], load_sem)
            load.start()
            if stores[b] is not None:
                stores[b].wait()
            load.wait()
            if k == 0:
                own.start()

            def chunk(i, carry):
                rows = pl.ds(pl.multiple_of(i * tm, tm), tm)
                obuf[b, rows, :] = jnp.dot(h_ref[rows, :], wbuf[b], preferred_element_type=F32).astype(BF16)
                return carry

            lax.fori_loop(0, s // tm, chunk, 0)
            stores[b] = pltpu.make_async_copy(
                obuf.at[b], p_ref.at[:, pl.ds(pl.multiple_of(slot(block) * n, 128), n)], store_sems.at[b])
            stores[b].start()

        passed = []

        def arrive(a, k, block):
            copy(a, k, block, me).wait_recv()

        def pass_on(a, k, block, to):
            cp = copy(a, k, block, to)
            cp.start()
            passed.append(cp)

        def gather(arrays, use):
            def arrive_all(k, block):
                for a in arrays:
                    arrive(a, k, block)

            def pass_all(k, block, to):
                for a in arrays:
                    pass_on(a, k, block, to)

            use(0, me)
            arrive_all(0, sibling)
            use(1, sibling)
            arrive_all(k_near, (*near, c))
            pass_all(3, (*near, c), (*far, c))
            pass_all(f_near, (*near, c), sibling)
            use(2, (*near, c))
            arrive_all(f_far, (*far, 1 - c))
            use(3, (*far, 1 - c))
            arrive_all(k_far, (*far, c))
            pass_all(f_far, (*far, c), sibling)
            use(4, (*far, c))
            arrive_all(f_near, (*near, 1 - c))
            use(5, (*near, 1 - c))
            arrive_all(3, (*diag, c))
            pass_all(6, (*diag, c), sibling)
            use(6, (*diag, c))
            arrive_all(6, (*diag, 1 - c))
            use(7, (*diag, 1 - c))

        gather(range(na - late), lambda k, block: multiply(k, block, ins[0] if k == 0 else outs[0].at[slot(block)]))
        for cp in mine[:na - 1 - late]:
            cp.start()
        if late:
            gather(range(na - late, na), lambda k, block: None)
        for cp in first + passed:
            cp.wait_send()
        for cp in mine + stores + [h_store]:
            cp.wait()

    vmem = pl.BlockSpec(memory_space=pltpu.VMEM)
    outs = pl.pallas_call(
        body, name=name,
        out_shape=[jax.ShapeDtypeStruct((s, d), BF16), jax.ShapeDtypeStruct((s, N_DEV * n), BF16)]
        + [jax.ShapeDtypeStruct((N_DEV,) + a.shape, a.dtype) for a in arrs],
        in_specs=[_ANY, vmem] + [_ANY] * na, out_specs=[_ANY] * (2 + na),
        scratch_shapes=[pltpu.VMEM((s, d), BF16), pltpu.VMEM((2, tr, d), F32), pltpu.VMEM((2, d, n), BF16),
                        pltpu.VMEM((2, s, n), BF16),
                        pltpu.SemaphoreType.DMA((7 * na,)), pltpu.SemaphoreType.DMA((7 * na,)),
                        pltpu.SemaphoreType.DMA, pltpu.SemaphoreType.DMA((2,)), pltpu.SemaphoreType.DMA((na,)),
                        pltpu.SemaphoreType.DMA, pltpu.SemaphoreType.DMA((2,))],
        compiler_params=pltpu.CompilerParams(vmem_limit_bytes=VMEM_LIMIT),
    )(xs, g, *arrs)
    return outs[0], outs[1], outs[2], outs[3:]


_HBM = pl.BlockSpec(memory_space=pltpu.HBM)
_SEM = pl.BlockSpec(memory_space=pltpu.SEMAPHORE)
_DATAFLOW = pltpu.SideEffectType.DATAFLOW_SIDE_EFFECTING


def _peers_per_array(kind):
    return {"sibling": 1, "halves": 1, "pass": 1, "neighbours": 2}.get(kind, 3)


def _near_far():
    x, y, c = _place()
    x_first = c == 0
    near = (jnp.where(x_first, 1 - x, x), jnp.where(x_first, y, 1 - y))
    far = (jnp.where(x_first, x, 1 - x), jnp.where(x_first, 1 - y, y))
    return near, far, jnp.where(x_first, 0, 1), jnp.where(x_first, 1, 0)


def _split_copies(kind, srcs, lands, send_sems, recv_sems):
    x, y, c = _place()
    per = _peers_per_array(kind)
    out = []
    for a in range(len(lands)):
        if kind == "sibling":
            part = srcs[a] if srcs[a].shape[1] == 1 else srcs[a].at[:, pl.ds(1 - c, 1)]
            peers = [((x, y, 1 - c), part, lands[a], lands[a])]
        elif kind == "halves":
            mine, its = lands[a].at[:, pl.ds(c, 1)], lands[a].at[:, pl.ds(1 - c, 1)]
            peers = [((x, y, 1 - c), mine, mine, its)]
        elif kind == "neighbours":
            here = lands[a].at[4 * x + 2 * y + c]
            peers = [((px, py, c), srcs[a], here, lands[a].at[4 * px + 2 * py + c])
                     for px, py in [(1 - x, y), (x, 1 - y)]]
        elif kind == "pass":
            near, far, _, _ = _near_far()
            block = lands[a].at[4 * near[0] + 2 * near[1] + c]
            peers = [((*far, c), block, block, lands[a].at[4 * (1 - x) + 2 * (1 - y) + c])]
        else:
            peers = []
            for px, py in [(1 - x, y), (x, 1 - y), (1 - x, 1 - y)]:
                if kind == "gather":
                    views = (srcs[a], lands[a].at[4 * x + 2 * y + c], lands[a].at[4 * px + 2 * py + c])
                else:
                    views = (srcs[a].at[2 * px + py], lands[a].at[2 * x + y], lands[a].at[2 * px + py])
                peers.append(((px, py, c),) + views)
        for j, (peer, src, dst, arrives) in enumerate(peers):
            sems = dict(send_sem=send_sems.at[per * a + j], recv_sem=recv_sems.at[per * a + j],
                        device_id=peer, device_id_type=MESH)
            out.append((pltpu.make_async_remote_copy(src_ref=src, dst_ref=dst, **sems),
                        pltpu.make_async_remote_copy(src_ref=src, dst_ref=arrives, **sems)))
    return out


def split_start(kind, srcs, lands, deps, name):
    ns, nl = len(srcs), len(lands)
    n_sems = _peers_per_array(kind) * nl
    held = list(srcs) + list(lands)

    def body(*refs):
        send_sems, recv_sems = refs[len(held) + len(deps)], refs[len(held) + len(deps) + 1]
        for copy, _ in _split_copies(kind, refs[:ns], refs[ns:ns + nl], send_sems, recv_sems):
            copy.start()
        token = refs[-1]
        token[...] = jnp.zeros_like(token)

    outs = pl.pallas_call(
        body, name=name,
        out_shape=(pltpu.SemaphoreType.DMA((n_sems,)), pltpu.SemaphoreType.DMA((n_sems,)),
                   *[pltpu.HBM(a.shape, a.dtype) for a in held], jax.ShapeDtypeStruct((8, 128), F32)),
        in_specs=[_HBM] * len(held) + [_ANY] * len(deps),
        out_specs=(_SEM, _SEM, *([_HBM] * len(held)), pl.BlockSpec(memory_space=pltpu.VMEM)),
        input_output_aliases={i: 2 + i for i in range(len(held))},
        compiler_params=pltpu.CompilerParams(has_side_effects=_DATAFLOW),
    )(*[pltpu.with_memory_space_constraint(a, pltpu.HBM) for a in held], *deps)
    return outs[0], outs[1], list(outs[2:2 + ns]), list(outs[2 + ns:2 + ns + nl]), outs[-1]


def split_wait(kind, send_sems, recv_sems, srcs, lands, afters, name):
    ns, nl = len(srcs), len(lands)
    held = list(srcs) + list(lands)

    def body(*refs):
        for _, arrival in _split_copies(kind, refs[:ns], refs[ns:ns + nl], refs[ns + nl], refs[ns + nl + 1]):
            arrival.wait_send()
            arrival.wait_recv()

    outs = pl.pallas_call(
        body, name=name,
        out_shape=[pltpu.HBM(a.shape, a.dtype) for a in held],
        in_specs=[_HBM] * len(held) + [_SEM, _SEM] + [_ANY] * len(afters),
        out_specs=[_HBM] * len(held),
        input_output_aliases={i: i for i in range(len(held))},
        compiler_params=pltpu.CompilerParams(has_side_effects=_DATAFLOW),
    )(*held, send_sems, recv_sems, *afters)
    return list(outs[:ns]), list(outs[ns:])


def split_pass_on(lands, first_recv, afters, name):
    n = len(lands)

    def body(*refs):
        lands_r, first = refs[:n], refs[n]
        send_sems, recv_sems = refs[n + 1 + len(afters)], refs[n + 2 + len(afters)]
        c = lax.axis_index("c")
        near, _, k_near, _ = _near_far()
        for a in range(n):
            block = lands_r[a].at[4 * near[0] + 2 * near[1] + c]
            pltpu.make_async_remote_copy(
                src_ref=block, dst_ref=block, send_sem=first.at[2 * a + k_near], recv_sem=first.at[2 * a + k_near],
                device_id=(near[0], near[1], c), device_id_type=MESH).wait_recv()
        for copy, _ in _split_copies("pass", [], lands_r, send_sems, recv_sems):
            copy.start()
        token = refs[-1]
        token[...] = jnp.zeros_like(token)

    outs = pl.pallas_call(
        body, name=name,
        out_shape=(pltpu.SemaphoreType.DMA((n,)), pltpu.SemaphoreType.DMA((n,)),
                   *[pltpu.HBM(a.shape, a.dtype) for a in lands], jax.ShapeDtypeStruct((8, 128), F32)),
        in_specs=[_HBM] * n + [_SEM] + [_ANY] * len(afters),
        out_specs=(_SEM, _SEM, *([_HBM] * n), pl.BlockSpec(memory_space=pltpu.VMEM)),
        input_output_aliases={i: 2 + i for i in range(n)},
        compiler_params=pltpu.CompilerParams(has_side_effects=_DATAFLOW),
    )(*lands, first_recv, *afters)
    return outs[0], outs[1], list(outs[2:2 + n]), outs[-1]


def split_wait_neighbours(first_send, first_recv, pass_send, pass_recv, srcs, lands, afters, name):
    n = len(lands)
    held = list(srcs) + list(lands)

    def body(*refs):
        srcs_r, lands_r = refs[:n], refs[n:2 * n]
        send1, recv1, send2, recv2 = refs[2 * n:2 * n + 4]
        c = lax.axis_index("c")
        _, far, _, k_far = _near_far()
        for _, arrival in _split_copies("neighbours", srcs_r, lands_r, send1, recv1):
            arrival.wait_send()
        for a in range(n):
            block = lands_r[a].at[4 * far[0] + 2 * far[1] + c]
            pltpu.make_async_remote_copy(
                src_ref=block, dst_ref=block, send_sem=recv1.at[2 * a + k_far], recv_sem=recv1.at[2 * a + k_far],
                device_id=(far[0], far[1], c), device_id_type=MESH).wait_recv()
        for _, arrival in _split_copies("pass", [], lands_r, send2, recv2):
            arrival.wait_send()
            arrival.wait_recv()

    outs = pl.pallas_call(
        body, name=name,
        out_shape=[pltpu.HBM(a.shape, a.dtype) for a in held],
        in_specs=[_HBM] * len(held) + [_SEM] * 4 + [_ANY] * len(afters),
        out_specs=[_HBM] * len(held),
        input_output_aliases={i: i for i in range(len(held))},
        compiler_params=pltpu.CompilerParams(has_side_effects=_DATAFLOW),
    )(*held, first_send, first_recv, pass_send, pass_recv, *afters)
    return list(outs[:n]), list(outs[n:])


def place_block(land, block, dev, name):
    r, c = block.shape
    tr = min(r, 512)

    def body(dev_ref, land_ref, b_ref, o_ref):
        del dev_ref, land_ref
        o_ref[...] = b_ref[...]

    return pl.pallas_call(
        body, name=name,
        grid_spec=pltpu.PrefetchScalarGridSpec(
            num_scalar_prefetch=1, grid=(r // tr,),
            in_specs=[_ANY, pl.BlockSpec((tr, c), lambda i, dev_ref: (i, 0))],
            out_specs=pl.BlockSpec((None, tr, c), lambda i, dev_ref: (dev_ref[0], i, 0))),
        out_shape=jax.ShapeDtypeStruct(land.shape, land.dtype),
        input_output_aliases={1: 0},
        compiler_params=_params("parallel"),
    )(dev, land, block)


def pair_add(own, recv, core, name):
    _, _, r, c = own.shape
    tr = min(r, 2048)

    def body(core_ref, own_ref, recv_ref, o_ref):
        del core_ref
        o_ref[...] = (own_ref[...].astype(F32) + recv_ref[...].astype(F32)).astype(BF16)

    return pl.pallas_call(
        body, name=name,
        grid_spec=pltpu.PrefetchScalarGridSpec(
            num_scalar_prefetch=1, grid=(4, r // tr),
            in_specs=[pl.BlockSpec((None, None, tr, c), lambda k, i, core_ref: (k, core_ref[0], i, 0)),
                      pl.BlockSpec((None, None, tr, c), lambda k, i, core_ref: (k, 0, i, 0))],
            out_specs=pl.BlockSpec((None, tr, c), lambda k, i, core_ref: (k, i, 0))),
        out_shape=jax.ShapeDtypeStruct((4, r, c), BF16),
        compiler_params=_params("parallel", "parallel"),
    )(core, own, recv)


def _adamw_math(w, g, m, v):
    m2 = ADAM_B1 * m + (1.0 - ADAM_B1) * g
    v2 = ADAM_B2 * v + (1.0 - ADAM_B2) * (g * g)
    m_hat = m2 / (1.0 - ADAM_B1 ** ADAM_STEP)
    v_hat = v2 / (1.0 - ADAM_B2 ** ADAM_STEP)
    delta = -ADAM_LR * (m_hat / (jnp.sqrt(v_hat) + ADAM_EPS) + ADAM_WD * w)
    return delta, m2, v2


def adamw_big(w, m, v, own, got, chip, name):
    r, c = w.shape
    tr = min(r, 512)

    def body(chip_ref, w_ref, m_ref, v_ref, p0, p1, p2, p3, g_ref, d_ref, m2_ref, v2_ref):
        del chip_ref
        g = ((p0[...].astype(F32) + p1[...].astype(F32)) + p2[...].astype(F32)) + p3[...].astype(F32)
        delta, m2, v2 = _adamw_math(w_ref[...], g, m_ref[...], v_ref[...])
        g_ref[...] = g
        d_ref[...] = delta
        m2_ref[...] = m2
        v2_ref[...] = v2

    row = pl.BlockSpec((tr, c), lambda i, chip_ref: (i, 0))

    def slab(flip):
        return pl.BlockSpec((None, tr, c), lambda i, chip_ref: (chip_ref[0] ^ flip, i, 0))

    return pl.pallas_call(
        body, name=name,
        grid_spec=pltpu.PrefetchScalarGridSpec(
            num_scalar_prefetch=1, grid=(r // tr,),
            in_specs=[row, row, row, slab(0), slab(1), slab(2), slab(3)],
            out_specs=[row] * 4),
        out_shape=[jax.ShapeDtypeStruct((r, c), F32)] * 4,
        compiler_params=_params("parallel"),
    )(chip, w, m, v, own, got, got, got)


def sum_devices(g8, name):
    def body(g_ref, o_ref):
        tot = g_ref[0]
        for k in range(1, N_DEV):
            tot = tot + g_ref[k]
        o_ref[...] = tot

    return pl.pallas_call(body, name=name, out_shape=jax.ShapeDtypeStruct(g8.shape[1:], F32))(g8)


def adamw_small(ws, gs, ms, vs, name):
    n = len(ws)

    def body(*refs):
        w_r, g_r, m_r, v_r = refs[:n], refs[n:2 * n], refs[2 * n:3 * n], refs[3 * n:4 * n]
        d_o, m_o, v_o = refs[4 * n:5 * n], refs[5 * n:6 * n], refs[6 * n:7 * n]
        for k in range(n):
            delta, m2, v2 = _adamw_math(w_r[k][...], g_r[k][...], m_r[k][...], v_r[k][...])
            d_o[k][...] = delta
            m_o[k][...] = m2
            v_o[k][...] = v2

    shapes = [jax.ShapeDtypeStruct(w.shape, F32) for w in ws]
    outs = pl.pallas_call(body, name=name, out_shape=shapes * 3)(*ws, *gs, *ms, *vs)
    return outs[:n], outs[n:2 * n], outs[2 * n:]


def _rows128(a):
    return a.reshape(-1, 128)


def _pad_rows(a, rows):
    return jnp.pad(a, ((0, rows - a.shape[0]), (0, 0)))


def kernel(x, ln_pre_even, w_in_even, pool_w, pool_scale, w_out_even, ln_post_even, ln_pre_odd, w_in_odd, sconv_w, dconv_w, dconv_b, cnorm_g, cnorm_b, w_out_odd, ln_post_odd, loss_target, m_ln_pre_even, m_w_in_even, m_pool_w, m_pool_scale, m_w_out_even, m_ln_post_even, m_ln_pre_odd, m_w_in_odd, m_sconv_w, m_dconv_w, m_dconv_b, m_cnorm_g, m_cnorm_b, m_w_out_odd, m_ln_post_odd, v_ln_pre_even, v_w_in_even, v_pool_w, v_pool_scale, v_w_out_even, v_ln_post_even, v_ln_pre_odd, v_w_in_odd, v_sconv_w, v_dconv_w, v_dconv_b, v_cnorm_g, v_cnorm_b, v_w_out_odd, v_ln_post_odd):
    xs = x[0]
    tgt = loss_target[0]
    s, d = xs.shape
    half = d // 2
    n_heads = half // HEAD_DIM
    ng = len(POOL_WINDOWS)
    cwp = half // ng
    dev = 4 * lax.axis_index("x") + 2 * lax.axis_index("y") + lax.axis_index("c")
    core = lax.axis_index("c").astype(jnp.int32).reshape(1)

    pr = pool_w.shape[2]
    cl = sconv_w.shape[2]
    small_parts = [(_rows128(ln_pre_odd), 8), (sconv_w[0], 8), (dconv_w[0], 32), (dconv_b, 8),
                   (cnorm_g, 8), (cnorm_b, 8), (_rows128(ln_post_odd), 8)]
    small_local = jnp.concatenate([_pad_rows(a, r) for a, r in small_parts], axis=0)
    h0, p0, g_wie, (g_pw, g_small, g_woe) = in_proj_gathered(
        xs, ln_pre_even, w_in_even[0].astype(BF16),
        [pool_w[0].reshape(ng * pr, cwp).astype(BF16), small_local, w_out_even[0].astype(BF16)],
        "ag_in_proj_even", late=1)
    w_out_e = g_woe.reshape(1, d, d)
    comm = _Exchanges(dev, core, d)
    token = comm.start_weights("in_odd", [w_in_odd[0].astype(BF16)], [p0], neighbours=True)
    sb_dep = token
    comm.later["out_odd"] = [w_out_odd[0].astype(BF16)]
    pool_full = g_pw.reshape(N_DEV, ng, pr, cwp).transpose(1, 0, 2, 3).reshape(ng, cwp, cwp)
    nl = ln_pre_odd.shape[1] // 128

    def chan(lo, rows):
        return g_small[:, lo:lo + rows].transpose(1, 0, 2).reshape(rows, N_DEV * cl)

    ln_pre_odd_f = g_small[:, 0:nl].reshape(1, d)
    sconv_f = chan(8, SCONV_K)
    dconv_f = chan(16, CONF_K)
    dconv_b_f = chan(48, 1)
    cnorm_g_f = chan(56, 1)
    cnorm_b_f = chan(64, 1)
    ln_post_odd_f = g_small[:, 72:72 + nl].reshape(1, d)

    loss_blk, grad_x, small_g = _fwd_bwd(
        xs, tgt, ln_pre_even, h0, p0, g_wie, pool_full, pool_scale, w_out_e, ln_post_even, ln_pre_odd_f,
        sconv_f, dconv_f, dconv_b_f, cnorm_g_f, cnorm_b_f, ln_post_odd_f, comm, sb_dep)
    small_w = [ln_pre_even, pool_scale, ln_post_even, ln_pre_odd, sconv_w[0], dconv_w[0], dconv_b, cnorm_g, cnorm_b, ln_post_odd]
    small_m = [m_ln_pre_even, m_pool_scale, m_ln_post_even, m_ln_pre_odd, m_sconv_w[0], m_dconv_w[0], m_dconv_b, m_cnorm_g, m_cnorm_b, m_ln_post_odd]
    small_v = [v_ln_pre_even, v_pool_scale, v_ln_post_even, v_ln_pre_odd, v_sconv_w[0], v_dconv_w[0], v_dconv_b, v_cnorm_g, v_cnorm_b, v_ln_post_odd]
    big = {"w_in_even": (w_in_even, m_w_in_even, v_w_in_even), "pool_w": (pool_w, m_pool_w, v_pool_w),
           "w_out_even": (w_out_even, m_w_out_even, v_w_out_even), "w_in_odd": (w_in_odd, m_w_in_odd, v_w_in_odd),
           "w_out_odd": (w_out_odd, m_w_out_odd, v_w_out_odd)}
    upd = comm.finish_updates(big, [grad_x])
    upd.update(comm.finish_updates(big, [grad_x]))
    sg, sd, sm, sv, loss = _update_small(small_g, loss_blk, small_w, small_m, small_v, dev, d, cl,
                                         deps=[upd["w_in_odd"][1], upd["w_out_even"][1]])
    upd.update(comm.finish_updates(big, sd))
    (g_wie_o, d_wie, m_wie, v_wie), (g_pw_o, d_pw, m_pw, v_pw) = upd["w_in_even"], upd["pool_w"]
    (g_woe_o, d_woe, m_woe, v_woe), (g_wio_o, d_wio, m_wio, v_wio) = upd["w_out_even"], upd["w_in_odd"]
    g_woo_o, d_woo, m_woo, v_woo = upd["w_out_odd"]

    def order(small, wie, pw, woe, wio, woo):
        return [small[0], wie, pw, small[1], woe, small[2], small[3], wio, small[4], small[5], small[6],
                small[7], small[8], woo, small[9]]

    grads = order(sg, g_wie_o, g_pw_o, g_woe_o, g_wio_o, g_woo_o)
    deltas = order(sd, d_wie, d_pw, d_woe, d_wio, d_woo)
    new_m = order(sm, m_wie, m_pw, m_woe, m_wio, m_woo)
    new_v = order(sv, v_wie, v_pw, v_woe, v_wio, v_woo)
    return (loss, grad_x[None], *grads, *deltas, *new_m, *new_v)


def _fwd_bwd(xs, tgt, ln_pre_even, h0, p0, g_wie, pool_full, pool_scale, w_out_e, ln_post_even, ln_pre_odd_f,
             sconv_f, dconv_f, dconv_b_f, cnorm_g_f, cnorm_b_f, ln_post_odd_f, comm, sb_dep):
    d = xs.shape[1]
    n_heads = d // 2 // HEAD_DIM
    ng, cwp = pool_full.shape[0], pool_full.shape[1]
    a0, sb_wts = sb_fwd(p0, n_heads, "sb_fwd", dep=sb_dep)
    y0 = even_mix_fwd(a0, p0, pool_full, pool_scale, "even_mix_fwd")
    dep = comm.weights_pass_on("in_odd", after=y0)
    dep = comm.start_weights("out_odd", comm.later.pop("out_odd"), [dep])
    o0 = mm_nn(y0, w_out_e, BF16, "out_proj_even", tn=512, dep=dep)
    dep = comm.weights_arrived("in_odd", after=o0)
    x1, h1 = postnorm_fwd(xs, o0, ln_post_even, ln_pre_odd_f, "post_even", dep=dep)
    (g_wio,) = comm.weights("in_odd", after=x1)
    p1 = mm_nn(h1, g_wio, BF16, "in_proj_odd", group=2)
    dep = comm.weights_arrived("out_odd", after=p1)
    y1, dc = odd_mix_fwd(p1, sconv_f, dconv_f, dconv_b_f, cnorm_g_f, cnorm_b_f, "odd_mix_fwd", dep=dep)
    (w_out_o,) = comm.weights("out_odd", after=y1)
    w_out_o = w_out_o.reshape(1, d, d)
    o1 = mm_nn(y1, w_out_o, BF16, "out_proj_odd", tn=512)
    loss_blk, gx2, do1, dg_post_odd = final_fwd_bwd(x1, o1, ln_post_odd_f, tgt, "post_odd_loss")

    dw_out_o = mm_tn(y1, do1, 1, BF16, "dw_out_odd")
    dy1 = mm_nt(do1, w_out_o, BF16, "dy_odd")
    ddc, dg2, dgam, dbet = odd_bwd_ln(dy1, p1, dc, cnorm_g_f, cnorm_b_f, "odd_bwd_ln")
    dp1, dsconv, ddconv, ddconv_b = odd_bwd_conv(dy1, p1, ddc, dg2, sconv_f, dconv_f, "odd_bwd_conv")
    dw_in_o = mm_tn(h1, dp1, N_DEV, BF16, "dw_in_odd", group=2)
    dep = comm.reduce_begin({"w_out_odd": dw_out_o.reshape(N_DEV, d // N_DEV, d), "w_in_odd": dw_in_o}, "odd")
    dh1 = mm_nt(dp1, g_wio, BF16, "dh_odd", dep=dep, group=2)
    dep = comm.reduce_send(after=dh1)
    gx1, dg_pre_odd, do0, dg_post_even = norm_bwd(dh1, x1, ln_pre_odd_f, gx2, "pre_odd_post_even_bwd",
                                                  inp2=o0, g2=ln_post_even, dep=dep)

    dw_out_e = mm_tn(y0, do0, 1, BF16, "dw_out_even")
    dy0 = mm_nt(do0, w_out_e, BF16, "dy_even")
    da0, du0, dg0, dpool, dpool_scale = even_mix_bwd(dy0, a0, p0, pool_full, pool_scale, "even_mix_bwd")
    pr = cwp // N_DEV
    dpool_slabs = dpool.astype(BF16).reshape(ng, N_DEV, pr, cwp).transpose(1, 0, 2, 3).reshape(N_DEV, ng * pr, cwp)
    dep = comm.reduce_begin({"w_out_even": dw_out_e.reshape(N_DEV, d // N_DEV, d), "pool_w": dpool_slabs}, "even_out")
    dq0, dk0, dv0 = sb_bwd(p0, a0, sb_wts, da0, n_heads, "sb_bwd", dep=dep)
    dep = comm.reduce_send(after=dq0)
    dp0 = jnp.concatenate([dq0, dk0, dv0, du0, dg0], axis=1)
    dw_sibling = mm_tn(h0, dp0, N_DEV // 2, BF16, "dw_in_even_sibling", dep=dep, pick=(2, 1 - comm.core))
    dep = comm.reduce_begin({"w_in_even": dw_sibling}, "even_in", sibling_part=True)
    dw_own = mm_tn(h0, dp0, N_DEV // 2, BF16, "dw_in_even_own", dep=dep, pick=(2, comm.core))
    dep = comm.reduce_send(after=dw_own, own_part={"w_in_even": dw_own})
    dh0 = mm_nt(dp0, g_wie, BF16, "dh_even", dep=dep, group=2)
    dep = None
    grad_x, dg_pre_even = norm_bwd(dh0, xs, ln_pre_even, gx1, "pre_even_bwd", tm=512, dep=dep)
    small_g = [dg_pre_even, dpool_scale, dg_post_even, dg_pre_odd, dsconv, ddconv, ddconv_b, dgam, dbet, dg_post_odd]
    return loss_blk, grad_x, small_g


class _Exchanges:
    def __init__(self, dev, core, d):
        self.dev = dev.astype(jnp.int32).reshape(1)
        self.core = core
        self.chip = (dev // 2).astype(jnp.int32).reshape(1)
        self.d = d
        self.in_flight = {}
        self.later = {}
        self.to_sibling = None
        self.pending = []

    def start_weights(self, tag, blocks, afters, neighbours=False):
        lands = [lax.empty((N_DEV,) + b.shape, b.dtype) for b in blocks]
        kind = "neighbours" if neighbours else "gather"
        send, recv, srcs, lands, token = split_start(kind, blocks, lands, afters, "ag_start_" + tag)
        self.in_flight[tag] = (send, recv, srcs, lands)
        return token

    def weights_pass_on(self, tag, after):
        send, recv, srcs, lands = self.in_flight.pop(tag)
        send2, recv2, lands, token = split_pass_on(lands, recv, [after], "ag_pass_on_" + tag)
        self.in_flight[tag] = (send, recv, srcs, lands, send2, recv2)
        return token

    def weights_arrived(self, tag, after):
        entry = self.in_flight.pop(tag)
        send, recv, srcs, lands = entry[:4]
        if len(entry) == 6:
            srcs, lands = split_wait_neighbours(send, recv, entry[4], entry[5], srcs, lands, [after], "ag_wait_" + tag)
        else:
            srcs, lands = split_wait("gather", send, recv, srcs, lands, [after], "ag_wait_" + tag)
        lands = [place_block(l, b, self.dev, "ag_own_%s_%d" % (tag, k)) for k, (l, b) in enumerate(zip(lands, srcs))]
        lands = [l.reshape((4, 2) + l.shape[1:]) for l in lands]
        send, recv, _, lands, token = split_start("halves", [], lands, [], "ag_sibling_start_" + tag)
        self.in_flight[tag] = (send, recv, lands)
        return token

    def weights(self, tag, after):
        send, recv, lands = self.in_flight.pop(tag)
        _, lands = split_wait("halves", send, recv, [], lands, [after], "ag_sibling_wait_" + tag)
        return [l.reshape((N_DEV,) + l.shape[2:]) for l in lands]

    def reduce_begin(self, partials, tag, sibling_part=False):
        names = list(partials)
        arrs = [partials[k].reshape((4, 1 if sibling_part else 2) + partials[k].shape[1:]) for k in names]
        lands = [lax.empty((4, 1) + a.shape[2:], a.dtype) for a in arrs]
        send, recv, srcs, lands, token = split_start("sibling", arrs, lands, [], "rs_sibling_start_" + tag)
        self.to_sibling = (tag, names, send, recv, srcs, lands)
        return token

    def reduce_send(self, after, own_part=None):
        tag, names, send, recv, srcs, lands = self.to_sibling
        srcs, lands = split_wait("sibling", send, recv, srcs, lands, [after], "rs_sibling_wait_" + tag)
        which = self.core
        if own_part is not None:
            srcs = [own_part[k].reshape((4, 1) + own_part[k].shape[1:]) for k in names]
            which = jnp.zeros((1,), jnp.int32)
        sums = [pair_add(o, r, which, "rs_pair_add_" + k) for k, o, r in zip(names, srcs, lands)]
        zones = [lax.empty(a.shape, a.dtype) for a in sums]
        send, recv, srcs, zones, token = split_start("scatter", sums, zones, [], "rs_start_" + tag)
        self.pending.append((tag, names, send, recv, srcs, zones))
        return token

    def finish_updates(self, big, afters):
        tag, names, send, recv, srcs, lands = self.pending.pop(0)
        srcs, lands = split_wait("scatter", send, recv, srcs, lands, afters, "rs_wait_" + tag)
        out = {}
        for name, own, got in zip(names, srcs, lands):
            w, m, v = big[name]
            shp = own.shape[1:]
            outs = adamw_big(w.reshape(shp), m.reshape(shp), v.reshape(shp), own, got, self.chip, "adamw_" + name)
            out[name] = [o.reshape(w.shape) for o in outs]
        return out


def _update_small(small_g, loss_blk, small_w, small_m, small_v, dev, d, cl, deps):
    packed = jnp.concatenate([_rows128(g) for g in small_g] + [loss_blk], axis=0)
    (g8,) = all_gather([packed], "ag_small_grads", deps)
    tot = sum_devices(g8, "sum_small_grads")
    loss = tot[packed.shape[0] - 8, 0]
    full_g = []
    lo = 0
    for g in small_g:
        rows = g.size // 128
        full_g.append(tot[lo:lo + rows].reshape(g.shape))
        lo += rows

    def mine(g, width):
        return lax.dynamic_slice_in_dim(g, dev * width, width, axis=g.ndim - 1)

    fg = full_g
    small_gl = [fg[0], fg[1], fg[2], mine(fg[3], d // N_DEV), mine(fg[4], cl), mine(fg[5], cl), mine(fg[6], cl),
                mine(fg[7], cl), mine(fg[8], cl), mine(fg[9], d // N_DEV)]
    sd, sm, sv = adamw_small(small_w, small_gl, small_m, small_v, "adamw_small")

    def like(k, a):
        return a[None] if k in (4, 5) else a

    sg = [like(k, a) for k, a in enumerate(small_gl)]
    sd = [like(k, a) for k, a in enumerate(sd)]
    sm = [like(k, a) for k, a in enumerate(sm)]
    sv = [like(k, a) for k, a in enumerate(sv)]
    return sg, sd, sm, sv, loss
```

```python
import functools
import math

import jax
import jax.numpy as jnp
from jax import lax
from jax.experimental import pallas as pl
from jax.experimental.pallas import tpu as pltpu

F32 = jnp.float32
BF16 = jnp.bfloat16
EPS = 1e-6
HEAD_DIM = 128
POOL_WINDOWS = (2, 4, 8, 16)
SCONV_K = 3
CONF_K = 31
HALO = 32
N_DEV = 8
VMEM_LIMIT = 56 * 1024 * 1024
MESH = pl.DeviceIdType.MESH

ADAM_LR = 0.001
ADAM_B1 = 0.9
ADAM_B2 = 0.999
ADAM_EPS = 1e-08
ADAM_WD = 0.01
ADAM_STEP = 10


def _params(*sem):
    return pltpu.CompilerParams(dimension_semantics=sem, vmem_limit_bytes=VMEM_LIMIT)


def _sigmoid(v):
    return 1.0 / (1.0 + jnp.exp(-v))


def _silu(v):
    return v * _sigmoid(v)


def _silu_and_grad(v):
    s = _sigmoid(v)
    return v * s, s * (1.0 + v * (1.0 - s))


def _rowsum8(v):
    r, c = v.shape
    return jnp.sum(v.reshape(r // 8, 8, c), axis=0)


SUBLANES = 8


class _Taps:
    def __init__(self, xx, rows, before):
        self.xx, self.rows, self.before, self.rotated = xx, rows, before, {}

    def __call__(self, i):
        r, q = i % SUBLANES, i // SUBLANES
        if r not in self.rotated:
            n = self.xx.shape[0]
            self.rotated[r] = self.xx if r == 0 else pltpu.roll(self.xx, r if self.before else n - r, 0)
        lo = HALO - SUBLANES * q if self.before else SUBLANES * q
        return self.rotated[r][lo:lo + self.rows]


def _window_sum(xx, win, before):
    n = xx.shape[0]
    acc = xx
    k = 1
    while k < win:
        acc = acc + pltpu.roll(acc, k if before else n - k, 0)
        k *= 2
    return acc


def postnorm_fwd(x, o, g, g_next, name, tm=512, dep=None):
    s, d = x.shape
    dep_args, dep_specs = _after(dep)

    def body(x_ref, o_ref, g_ref, gn_ref, *rest):
        y_ref, h_ref = rest[-2:]
        ov = o_ref[...].astype(F32)
        r = lax.rsqrt(jnp.mean(ov * ov, axis=-1, keepdims=True) + EPS)
        y = x_ref[...] + ov * r * g_ref[...]
        y_ref[...] = y
        r2 = lax.rsqrt(jnp.mean(y * y, axis=-1, keepdims=True) + EPS)
        h_ref[...] = (y * r2 * gn_ref[...]).astype(BF16)

    row = pl.BlockSpec((tm, d), lambda i: (i, 0))
    vec = pl.BlockSpec((1, d), lambda i: (0, 0))
    return pl.pallas_call(
        body, name=name, grid=(s // tm,),
        in_specs=[row, row, vec, vec] + dep_specs, out_specs=[row, row],
        out_shape=[jax.ShapeDtypeStruct((s, d), F32), jax.ShapeDtypeStruct((s, d), BF16)],
        compiler_params=_params("parallel"),
    )(x, o, g, g_next, *dep_args)


def final_fwd_bwd(x1, o, g, target, name, tm=512):
    s, d = x1.shape
    n = s // tm

    def body(x_ref, o_ref, g_ref, t_ref, loss_ref, gx_ref, do_ref, dg_ref, lacc, gacc):
        i = pl.program_id(0)

        @pl.when(i == 0)
        def _():
            lacc[...] = jnp.zeros_like(lacc)
            gacc[...] = jnp.zeros_like(gacc)

        ov = o_ref[...].astype(F32)
        gv = g_ref[...]
        r = lax.rsqrt(jnp.mean(ov * ov, axis=-1, keepdims=True) + EPS)
        oh = ov * r
        diff = x_ref[...] + oh * gv - t_ref[...]
        lacc[...] += _rowsum8(diff * diff)
        gx = diff * (1.0 / d)
        gx_ref[...] = gx
        gacc[...] += _rowsum8(gx * oh)
        dn = gx * gv
        do_ref[...] = (r * (dn - oh * jnp.mean(dn * oh, axis=-1, keepdims=True))).astype(BF16)

        @pl.when(i == n - 1)
        def _():
            tot = jnp.sum(jnp.sum(lacc[...], axis=0, keepdims=True), axis=1, keepdims=True)
            loss_ref[...] = jnp.broadcast_to(tot * (0.5 / d), loss_ref.shape)
            dg_ref[...] = jnp.sum(gacc[...], axis=0, keepdims=True)

    row = pl.BlockSpec((tm, d), lambda i: (i, 0))
    vec = pl.BlockSpec((1, d), lambda i: (0, 0))
    return pl.pallas_call(
        body, name=name, grid=(n,),
        in_specs=[row, row, vec, row],
        out_specs=[pl.BlockSpec((8, 128), lambda i: (0, 0)), row, row, vec],
        out_shape=[jax.ShapeDtypeStruct((8, 128), F32), jax.ShapeDtypeStruct((s, d), F32),
                   jax.ShapeDtypeStruct((s, d), BF16), jax.ShapeDtypeStruct((1, d), F32)],
        scratch_shapes=[pltpu.VMEM((8, d), F32), pltpu.VMEM((8, d), F32)],
        compiler_params=_params("arbitrary"),
    )(x1, o, g, target)


def _rms_bwd_rows(dyv, xv, gv):
    r = lax.rsqrt(jnp.mean(xv * xv, axis=-1, keepdims=True) + EPS)
    xh = xv * r
    dn = dyv * gv
    return r * (dn - xh * jnp.mean(dn * xh, axis=-1, keepdims=True)), _rowsum8(dyv * xh)


def norm_bwd(dy, inp, g, resid, name, inp2=None, g2=None, tm=256, dep=None):
    s, d = inp.shape
    n = s // tm
    chain = inp2 is not None

    def body(*refs):
        dy_ref, x_ref, g_ref, r_ref = refs[:4]
        outs = refs[-6:] if chain else refs[-3:]
        i = pl.program_id(0)

        @pl.when(i == 0)
        def _():
            for acc in outs[-2:] if chain else outs[-1:]:
                acc[...] = jnp.zeros_like(acc)

        if chain:
            x2_ref, g2_ref = refs[4:6]
            dx_ref, dg_ref, dx2_ref, dg2_ref, gacc, gacc2 = outs
        else:
            dx_ref, dg_ref, gacc = outs
        dx, dg_rows = _rms_bwd_rows(dy_ref[...].astype(F32), x_ref[...], g_ref[...])
        dx = dx + r_ref[...]
        dx_ref[...] = dx
        gacc[...] += dg_rows
        if chain:
            dx2, dg2_rows = _rms_bwd_rows(dx, x2_ref[...].astype(F32), g2_ref[...])
            dx2_ref[...] = dx2.astype(BF16)
            gacc2[...] += dg2_rows

        @pl.when(i == n - 1)
        def _():
            dg_ref[...] = jnp.sum(gacc[...], axis=0, keepdims=True)
            if chain:
                dg2_ref[...] = jnp.sum(gacc2[...], axis=0, keepdims=True)

    row = pl.BlockSpec((tm, d), lambda i: (i, 0))
    vec = pl.BlockSpec((1, d), lambda i: (0, 0))
    dep_args, dep_specs = _after(dep)
    extra = [inp2, g2] if chain else []
    return pl.pallas_call(
        body, name=name, grid=(n,),
        in_specs=[row, row, vec, row] + ([row, vec] if chain else []) + dep_specs,
        out_specs=[row, vec] * (2 if chain else 1),
        out_shape=[jax.ShapeDtypeStruct((s, d), F32), jax.ShapeDtypeStruct((1, d), F32)]
        + ([jax.ShapeDtypeStruct((s, d), BF16), jax.ShapeDtypeStruct((1, d), F32)] if chain else []),
        scratch_shapes=[pltpu.VMEM((8, d), F32)] * (2 if chain else 1),
        compiler_params=_params("arbitrary"),
    )(dy, inp, g, resid, *extra, *dep_args)


def _after(dep):
    if dep is None:
        return [], []
    return [dep], [pl.BlockSpec((8, 128), lambda *_: (0, 0))]


def _lane_concat(ref, count):
    return ref[0] if count == 1 else jnp.concatenate([ref[i] for i in range(count)], axis=1)


def mm_nn(a, w, out_dtype, name, tm=2048, tn=None, dep=None, group=1):
    m, k = a.shape
    tm = min(tm, m)
    ns, _, n = w.shape
    tn = n if tn is None else tn
    nj = n // tn
    assert group == 1 or nj == 1
    dep_args, dep_specs = _after(dep)

    def body(a_ref, w_ref, *rest):
        o_ref = rest[-1]
        o_ref[...] = jnp.dot(a_ref[...], _lane_concat(w_ref, group), preferred_element_type=F32).astype(out_dtype)

    return pl.pallas_call(
        body, name=name, grid=(ns // group, nj, m // tm),
        in_specs=[pl.BlockSpec((tm, k), lambda s, j, i: (i, 0)),
                  pl.BlockSpec((group, k, tn), lambda s, j, i: (s, 0, j))] + dep_specs,
        out_specs=pl.BlockSpec((tm, group * tn), lambda s, j, i: (i, s * nj + j)),
        out_shape=jax.ShapeDtypeStruct((m, ns * n), out_dtype),
        compiler_params=_params("parallel", "parallel", "parallel"),
    )(a, w, *dep_args)


def mm_nt(a, w, out_dtype, name, tm=1024, tn=None, dep=None, group=1):
    m = a.shape[0]
    tm = min(tm, m)
    ns, k, n = w.shape
    tn = n if tn is None else tn
    nj = n // tn
    assert group == 1 or nj == 1
    steps = ns * nj // group
    dep_args, dep_specs = _after(dep)

    def body(a_ref, w_ref, *rest):
        o_ref, acc = rest[-2:]
        r = pl.program_id(1)

        @pl.when(r == 0)
        def _():
            acc[...] = jnp.zeros_like(acc)

        acc[...] += lax.dot_general(a_ref[...], _lane_concat(w_ref, group), (((1,), (1,)), ((), ())),
                                    preferred_element_type=F32)

        @pl.when(r == steps - 1)
        def _():
            o_ref[...] = acc[...].astype(out_dtype)

    return pl.pallas_call(
        body, name=name, grid=(m // tm, steps),
        in_specs=[pl.BlockSpec((tm, group * tn), lambda i, r: (i, r)),
                  pl.BlockSpec((group, k, tn), lambda i, r: (r // nj, 0, r % nj))] + dep_specs,
        out_specs=pl.BlockSpec((tm, k), lambda i, r: (i, 0)),
        out_shape=jax.ShapeDtypeStruct((m, k), out_dtype),
        scratch_shapes=[pltpu.VMEM((tm, k), F32)],
        compiler_params=_params("parallel", "arbitrary"),
    )(a, w, *dep_args)


def mm_tn(a, b, ns, out_dtype, name, tk=1024, tm=2048, dep=None, pick=None, group=1):
    m, k = a.shape
    tm = min(tm, m)
    step, offset = (1, None) if pick is None else pick
    assert group == 1 or pick is None
    n = b.shape[1] // (ns * step)
    steps = m // tm
    dep_args, dep_specs = _after(dep)
    n_pre = 0 if pick is None else 1

    def b_block(s, j, r, *pre):
        return (r, s if pick is None else step * s + pre[0][0])

    def body(*refs):
        a_ref, b_ref = refs[n_pre:n_pre + 2]
        o_ref, acc = refs[-2:]
        r = pl.program_id(2)

        @pl.when(r == 0)
        def _():
            acc[...] = jnp.zeros_like(acc)

        acc[...] += lax.dot_general(a_ref[...], b_ref[...], (((0,), (0,)), ((), ())),
                                    preferred_element_type=F32)

        @pl.when(r == steps - 1)
        def _():
            for i in range(group):
                o_ref[i] = acc[:, i * n:(i + 1) * n].astype(out_dtype)

    return pl.pallas_call(
        body, name=name,
        grid_spec=pltpu.PrefetchScalarGridSpec(
            num_scalar_prefetch=n_pre, grid=(ns // group, k // tk, steps),
            in_specs=[pl.BlockSpec((tm, tk), lambda s, j, r, *pre: (r, j)),
                      pl.BlockSpec((tm, group * n), b_block)] + dep_specs,
            out_specs=pl.BlockSpec((group, tk, n), lambda s, j, r, *pre: (s, j, 0)),
            scratch_shapes=[pltpu.VMEM((tk, group * n), F32)]),
        out_shape=jax.ShapeDtypeStruct((ns, k, n), out_dtype),
        compiler_params=_params("parallel", "parallel", "arbitrary"),
    )(*([] if pick is None else [offset]), a, b, *dep_args)


SB_BLK = 128


LOG2E = 1.0 / math.log(2.0)


def _split_dot(v, tri2):
    hi = pltpu.bitcast(pltpu.bitcast(v, jnp.uint32) & jnp.uint32(0xFFFF0000), F32)
    lo = (v - hi).astype(BF16)
    return jnp.dot(jnp.concatenate([hi.astype(BF16), lo], axis=1), tri2, preferred_element_type=F32)


def _sb_scores(z2, lim, dcol, tri_ex, masked):
    sp = jnp.log2(1.0 + jnp.exp2(-jnp.abs(z2)))
    lb = jnp.minimum(z2, 0.0) - sp
    l1m = lb - z2
    mask = None
    if masked:
        mask = dcol < lim
        l1m = jnp.where(mask, l1m, 0.0)
    return mask, lb, l1m, _split_dot(l1m, tri_ex)


def _sb_consts():
    row = lax.broadcasted_iota(jnp.int32, (SB_BLK, SB_BLK), 0)
    col = lax.broadcasted_iota(jnp.int32, (SB_BLK, SB_BLK), 1)
    tri_ex = jnp.where(row > col, 1.0, 0.0).astype(BF16)
    tri_in = jnp.where(row >= col, 1.0, 0.0).astype(BF16)
    return col - row, jnp.concatenate([tri_ex, tri_ex], axis=0), jnp.concatenate([tri_in, tri_in], axis=0)


def sb_fwd(p, n_heads, name, tq=1024, nsub=8, dep=None):
    s = p.shape[0]
    h_n = n_heads
    b = SB_BLK
    nqs = tq // b
    tk = nsub * b
    scale = 1.0 / math.sqrt(HEAD_DIM)

    dep_args, dep_specs = _after(dep)

    def body(q_ref, k_ref, v_ref, *rest):
        o_ref, w_ref = rest[-2:]
        qi = pl.program_id(1)
        dcol, tri_ex, _ = _sb_consts()
        qv = [q_ref[qs * b:(qs + 1) * b, :] for qs in range(nqs)]
        n_groups = ((qi + 1) * nqs - 1) // nsub + 1

        def step(it, carry, masked):
            c1s, accs = carry
            g = n_groups - 1 - it
            off = pl.multiple_of(g * tk, tk)
            kg = k_ref[pl.ds(off, tk), :]
            vg = v_ref[pl.ds(off, tk), :]
            new_c1, new_acc = [], []
            for qs in range(nqs):
                qb = qi * nqs + qs
                square = masked and nqs == nsub
                nk = qs + 1 if square else nsub
                kq, vq = kg[:nk * b], vg[:nk * b]
                z2 = lax.dot_general(qv[qs], kq, (((1,), (1,)), ((), ())),
                                     preferred_element_type=F32) * (scale * LOG2E)
                blocks = [_sb_scores(z2[:, j * b:(j + 1) * b], (qb - (g * nsub + j)) * b, dcol, tri_ex,
                                     masked and (j == qs or not square)) for j in range(nk)]
                run = c1s[qs]
                ws = [None] * nk
                for j in reversed(range(nk)):
                    mask, lb, l1m, ls_loc = blocks[j]
                    wj = jnp.exp2(lb + ls_loc + run)
                    ws[j] = (wj if mask is None else jnp.where(mask, wj, 0.0)).astype(BF16)
                    run = run + jnp.sum(l1m, axis=1, keepdims=True)
                w = jnp.concatenate(ws, axis=1)
                w_ref[0, g, qs * b:(qs + 1) * b, 0:nk * b] = w
                new_acc.append(accs[qs] + jnp.dot(w, vq, preferred_element_type=F32))
                new_c1.append(run)
            return tuple(new_c1), tuple(new_acc)

        init = (tuple(jnp.zeros((b, 1), F32) for _ in range(nqs)),
                tuple(jnp.zeros((b, HEAD_DIM), F32) for _ in range(nqs)))
        assert all(((i + 1) * nqs - 1) // nsub * nsub <= i * nqs for i in range(s // tq))
        first = step(0, init, True)
        _, accs = lax.fori_loop(1, n_groups, functools.partial(step, masked=False), first)
        for qs in range(nqs):
            o_ref[qs * b:(qs + 1) * b, :] = accs[qs]

    return pl.pallas_call(
        body, name=name, grid=(h_n, s // tq),
        in_specs=[pl.BlockSpec((tq, HEAD_DIM), lambda h, i: (i, h)),
                  pl.BlockSpec((s, HEAD_DIM), lambda h, i: (0, h_n + h)),
                  pl.BlockSpec((s, HEAD_DIM), lambda h, i: (0, 2 * h_n + h))] + dep_specs,
        out_specs=[pl.BlockSpec((tq, HEAD_DIM), lambda h, i: (i, h)),
                   pl.BlockSpec((1, s // tk, tq, tk), lambda h, i: (h, 0, i, 0))],
        out_shape=[jax.ShapeDtypeStruct((s, h_n * HEAD_DIM), F32),
                   jax.ShapeDtypeStruct((h_n, s // tk, s, tk), BF16)],
        compiler_params=_params("parallel", "arbitrary"),
    )(p, p, p, *dep_args)


def sb_bwd(p, a, wts, da, n_heads, name, tq=1024, dep=None):
    s = p.shape[0]
    h_n = n_heads
    nq = s // tq
    b = SB_BLK
    nqs = tq // b
    tk = wts.shape[3]
    nsub = tk // b
    scale = 1.0 / math.sqrt(HEAD_DIM)
    dep_args, dep_specs = _after(dep)

    def body(q_ref, k_ref, v_ref, a_ref, da_ref, w_ref, *rest):
        dq_ref, dk_ref, dv_ref, dk_acc, dv_acc = rest[-5:]
        qi = pl.program_id(1)

        @pl.when(qi == 0)
        def _():
            dk_acc[...] = jnp.zeros_like(dk_acc)
            dv_acc[...] = jnp.zeros_like(dv_acc)

        dcol, _, tri_in = _sb_consts()
        q_all = q_ref[...]
        do_all = da_ref[...]
        qv = [q_ref[qs * b:(qs + 1) * b, :] for qs in range(nqs)]
        dov = [da_ref[qs * b:(qs + 1) * b, :] for qs in range(nqs)]
        tots = [jnp.sum(dov[qs].astype(F32) * a_ref[qs * b:(qs + 1) * b, :], axis=1, keepdims=True)
                for qs in range(nqs)]
        n_groups = ((qi + 1) * nqs - 1) // nsub + 1

        def step(it, carry, masked):
            c2s, dqs = carry
            g = n_groups - 1 - it
            off = pl.multiple_of(g * tk, tk)
            kg = k_ref[pl.ds(off, tk), :]
            vg = v_ref[pl.ds(off, tk), :]
            square = masked and nqs == nsub
            new_c2, new_dq, dz_rows, w_rows = [], [], [], []
            for qs in range(nqs):
                qb = qi * nqs + qs
                nk = qs + 1 if square else nsub
                kq, vq = kg[:nk * b], vg[:nk * b]
                z2 = lax.dot_general(qv[qs], kq, (((1,), (1,)), ((), ())),
                                     preferred_element_type=F32) * (-scale * LOG2E)
                dw = lax.dot_general(dov[qs], vq, (((1,), (1,)), ((), ())), preferred_element_type=F32)
                beta = 1.0 / (1.0 + jnp.exp2(z2))
                wq = w_ref[0, g, qs * b:(qs + 1) * b, 0:nk * b]
                e = dw * wq.astype(F32)
                run2 = c2s[qs]
                dzs = [None] * nk
                for j in reversed(range(nk)):
                    cols = slice(j * b, (j + 1) * b)
                    later = _split_dot(e[:, cols], tri_in) + run2
                    bj = beta[:, cols]
                    dz = (e[:, cols] * (1.0 - bj) - bj * (tots[qs] - later)) * scale
                    if masked and (j == qs or not square):
                        dz = jnp.where(dcol < (qb - (g * nsub + j)) * b, dz, 0.0)
                    dzs[j] = dz.astype(BF16)
                    run2 = run2 + jnp.sum(e[:, cols], axis=1, keepdims=True)
                dzq = jnp.concatenate(dzs, axis=1)
                new_dq.append(dqs[qs] + jnp.dot(dzq, kq, preferred_element_type=F32))
                new_c2.append(run2)
                pad = [jnp.zeros((b, (nsub - nk) * b), BF16)] if nk < nsub else []
                dz_rows.append(jnp.concatenate([dzq] + pad, axis=1))
                w_rows.append(jnp.concatenate([wq] + pad, axis=1))
            dz_all = jnp.concatenate(dz_rows, axis=0)
            w_all = jnp.concatenate(w_rows, axis=0)
            dk_acc[pl.ds(off, tk), :] += lax.dot_general(dz_all, q_all, (((0,), (0,)), ((), ())),
                                                         preferred_element_type=F32)
            dv_acc[pl.ds(off, tk), :] += lax.dot_general(w_all, do_all, (((0,), (0,)), ((), ())),
                                                         preferred_element_type=F32)
            return tuple(new_c2), tuple(new_dq)

        zeros = tuple(jnp.zeros((b, 1), F32) for _ in range(nqs))
        assert all(((i + 1) * nqs - 1) // nsub * nsub <= i * nqs for i in range(s // tq))
        first = step(0, (zeros, tuple(jnp.zeros((b, HEAD_DIM), F32) for _ in range(nqs))), True)
        _, dqs = lax.fori_loop(1, n_groups, functools.partial(step, masked=False), first)
        for qs in range(nqs):
            dq_ref[qs * b:(qs + 1) * b, :] = dqs[qs].astype(BF16)

        @pl.when(qi == nq - 1)
        def _():
            dk_ref[...] = dk_acc[...].astype(BF16)
            dv_ref[...] = dv_acc[...].astype(BF16)

    blk = pl.BlockSpec((tq, HEAD_DIM), lambda h, i: (i, h))
    full = pl.BlockSpec((s, HEAD_DIM), lambda h, i: (0, h))
    return pl.pallas_call(
        body, name=name, grid=(h_n, nq),
        in_specs=[blk, pl.BlockSpec((s, HEAD_DIM), lambda h, i: (0, h_n + h)),
                  pl.BlockSpec((s, HEAD_DIM), lambda h, i: (0, 2 * h_n + h)), blk, blk,
                  pl.BlockSpec((1, s // tk, tq, tk), lambda h, i: (h, 0, i, 0))] + dep_specs,
        out_specs=[blk, full, full],
        out_shape=[jax.ShapeDtypeStruct((s, h_n * HEAD_DIM), BF16)] * 3,
        scratch_shapes=[pltpu.VMEM((s, HEAD_DIM), F32), pltpu.VMEM((s, HEAD_DIM), F32)],
        compiler_params=_params("parallel", "arbitrary"),
    )(p, p, p, a, da, wts, *dep_args)


def _pool_window(xx, win, r0, rc):
    cur = xx[HALO:HALO + rc]
    ws = _window_sum(xx, win, True)[HALO:HALO + rc]
    t_idx = r0 + lax.broadcasted_iota(jnp.int32, (rc, 1), 0)
    inv = 1.0 / jnp.minimum(win, t_idx + 1).astype(F32)
    return ws * inv - cur, inv


def even_mix_fwd(a, p, pool_w, pool_scale, name, rc=512, dep=None):
    s = p.shape[0]
    ng = len(POOL_WINDOWS)
    cw = pool_w.shape[1]
    n_chunks = s // rc
    dep_args, dep_specs = _after(dep)

    def body(a_ref, u_ref, g_ref, w_ref, sc_ref, *rest):
        y_ref, upad = rest[-2:]
        j = pl.program_id(0)

        @pl.when(j < ng)
        def _():
            def chunk(ci, carry):
                rows = pl.ds(pl.multiple_of(ci * rc, rc), rc)
                y_ref[rows, :] = (a_ref[rows, :] * _silu(g_ref[rows, :].astype(F32))).astype(BF16)
                return carry

            lax.fori_loop(0, n_chunks, chunk, 0)

        for gi, win in enumerate(POOL_WINDOWS):
            @pl.when(j == ng + gi)
            def _(win=win):
                upad[0:HALO, :] = jnp.zeros((HALO, cw), F32)

                def fill(ci, carry):
                    r0 = pl.multiple_of(ci * rc, rc)
                    upad[pl.ds(pl.multiple_of(r0 + HALO, HALO), rc), :] = u_ref[pl.ds(r0, rc), :].astype(F32)
                    return carry

                lax.fori_loop(0, n_chunks, fill, 0)

                def chunk(ci, carry):
                    r0 = pl.multiple_of(ci * rc, rc)
                    rows = pl.ds(r0, rc)
                    pooled, _ = _pool_window(upad[pl.ds(r0, HALO + rc), :], win, r0, rc)
                    t = jnp.dot(pooled.astype(BF16), w_ref[0], preferred_element_type=F32)
                    y_ref[rows, :] = (t * sc_ref[...] * _silu(g_ref[rows, :].astype(F32))).astype(BF16)
                    return carry

                lax.fori_loop(0, n_chunks, chunk, 0)

    grp = lambda j: jnp.maximum(j - ng, 0)
    return pl.pallas_call(
        body, name=name, grid=(2 * ng,),
        in_specs=[pl.BlockSpec((s, cw), lambda j: (0, jnp.minimum(j, ng - 1))),
                  pl.BlockSpec((s, cw), lambda j: (0, 3 * ng + grp(j))),
                  pl.BlockSpec((s, cw), lambda j: (0, 4 * ng + j)),
                  pl.BlockSpec((1, cw, cw), lambda j: (grp(j), 0, 0)),
                  pl.BlockSpec((1, cw), lambda j: (0, grp(j)))] + dep_specs,
        out_specs=pl.BlockSpec((s, cw), lambda j: (0, j)),
        out_shape=jax.ShapeDtypeStruct((s, 2 * ng * cw), BF16),
        scratch_shapes=[pltpu.VMEM((HALO + s, cw), F32)],
        compiler_params=_params("arbitrary"),
    )(a, p, p, pool_w, pool_scale, *dep_args)


def even_mix_bwd(dy, a, p, pool_w, pool_scale, name, rc=512):
    s = p.shape[0]
    ng = len(POOL_WINDOWS)
    cw = pool_w.shape[1]
    n_chunks = s // rc

    def body(dy_ref, a_ref, u_ref, g_ref, w_ref, sc_ref, da_ref, du_ref, dg_ref, dw_ref, dsc_ref,
             upad, rpad, dpl, dw_acc, dsc_acc):
        j = pl.program_id(0)

        @pl.when(j < ng)
        def _():
            def chunk(ci, carry):
                rows = pl.ds(pl.multiple_of(ci * rc, rc), rc)
                dyv = dy_ref[rows, :].astype(F32)
                sg, dsg = _silu_and_grad(g_ref[rows, :].astype(F32))
                da_ref[rows, :] = (dyv * sg).astype(BF16)
                dg_ref[rows, :] = (dyv * a_ref[rows, :] * dsg).astype(BF16)
                return carry

            lax.fori_loop(0, n_chunks, chunk, 0)

        for gi, win in enumerate(POOL_WINDOWS):
            @pl.when(j == ng + gi)
            def _(win=win):
                upad[0:HALO, :] = jnp.zeros((HALO, cw), F32)
                rpad[s:s + HALO, :] = jnp.zeros((HALO, cw), F32)
                dw_acc[...] = jnp.zeros_like(dw_acc)
                dsc_acc[...] = jnp.zeros_like(dsc_acc)

                def fill(ci, carry):
                    r0 = pl.multiple_of(ci * rc, rc)
                    upad[pl.ds(pl.multiple_of(r0 + HALO, HALO), rc), :] = u_ref[pl.ds(r0, rc), :].astype(F32)
                    return carry

                lax.fori_loop(0, n_chunks, fill, 0)

                def chunk(ci, carry):
                    r0 = pl.multiple_of(ci * rc, rc)
                    rows = pl.ds(r0, rc)
                    pooled, inv = _pool_window(upad[pl.ds(r0, HALO + rc), :], win, r0, rc)
                    pb = pooled.astype(BF16)
                    wv = w_ref[0]
                    t = jnp.dot(pb, wv, preferred_element_type=F32)
                    scv = sc_ref[...]
                    dyv = dy_ref[rows, :].astype(F32)
                    sg, dsg = _silu_and_grad(g_ref[rows, :].astype(F32))
                    dpo = dyv * sg
                    dg_ref[rows, :] = (dyv * t * scv * dsg).astype(BF16)
                    dsc_acc[...] += _rowsum8(dpo * t)
                    dtb = (dpo * scv).astype(BF16)
                    dw_acc[...] += lax.dot_general(pb, dtb, (((0,), (0,)), ((), ())),
                                                   preferred_element_type=F32)
                    dpooled = lax.dot_general(dtb, wv, (((1,), (1,)), ((), ())),
                                              preferred_element_type=F32)
                    dpl[rows, :] = dpooled
                    rpad[rows, :] = dpooled * inv
                    return carry

                lax.fori_loop(0, n_chunks, chunk, 0)

                def chunk2(ci, carry):
                    r0 = pl.multiple_of(ci * rc, rc)
                    rows = pl.ds(r0, rc)
                    xx = rpad[pl.ds(r0, rc + HALO), :]
                    fs = _window_sum(xx, win, False)[0:rc]
                    du_ref[rows, :] = (fs - dpl[rows, :]).astype(BF16)
                    return carry

                lax.fori_loop(0, n_chunks, chunk2, 0)
                dw_ref[0] = dw_acc[...]
                dsc_ref[...] = jnp.sum(dsc_acc[...], axis=0, keepdims=True)

    grp = lambda j: jnp.maximum(j - ng, 0)
    att = lambda j: jnp.minimum(j, ng - 1)
    return pl.pallas_call(
        body, name=name, grid=(2 * ng,),
        in_specs=[pl.BlockSpec((s, cw), lambda j: (0, j)),
                  pl.BlockSpec((s, cw), lambda j: (0, att(j))),
                  pl.BlockSpec((s, cw), lambda j: (0, 3 * ng + grp(j))),
                  pl.BlockSpec((s, cw), lambda j: (0, 4 * ng + j)),
                  pl.BlockSpec((1, cw, cw), lambda j: (grp(j), 0, 0)),
                  pl.BlockSpec((1, cw), lambda j: (0, grp(j)))],
        out_specs=[pl.BlockSpec((s, cw), lambda j: (0, att(j))),
                   pl.BlockSpec((s, cw), lambda j: (0, grp(j))),
                   pl.BlockSpec((s, cw), lambda j: (0, j)),
                   pl.BlockSpec((1, cw, cw), lambda j: (grp(j), 0, 0)),
                   pl.BlockSpec((1, cw), lambda j: (0, grp(j)))],
        out_shape=[jax.ShapeDtypeStruct((s, ng * cw), BF16), jax.ShapeDtypeStruct((s, ng * cw), BF16),
                   jax.ShapeDtypeStruct((s, 2 * ng * cw), BF16),
                   jax.ShapeDtypeStruct((ng, cw, cw), F32), jax.ShapeDtypeStruct((1, ng * cw), F32)],
        scratch_shapes=[pltpu.VMEM((HALO + s, cw), F32), pltpu.VMEM((s + HALO, cw), F32),
                        pltpu.VMEM((s, cw), F32), pltpu.VMEM((cw, cw), F32), pltpu.VMEM((8, cw), F32)],
        compiler_params=_params("arbitrary"),
    )(dy, a, p, p, pool_w, pool_scale)


def _halo_before(tm):
    return lambda i: jnp.maximum(i * (tm // HALO) - 1, 0)


def _halo_after(tm, s):
    return lambda i: jnp.minimum((i + 1) * (tm // HALO), s // HALO - 1)


def odd_mix_fwd(p, sconv_w, dconv_w, dconv_b, cnorm_g, cnorm_b, name, tm=128, dep=None):
    s = p.shape[0]
    cw = sconv_w.shape[1]
    n = s // tm
    lanes = 128
    hb = _halo_before(tm)

    dep_args, dep_specs = _after(dep)

    def body(hc_ref, hch_ref, bc_ref, cc_ref, cch_ref, ga_ref, gah_ref, gb_ref, gbh_ref, g1_ref, g2_ref,
             sw_ref, dw_ref, db_ref, gam_ref, bet_ref, *rest):
        y_ref, dc_ref = rest[-2:]
        first = pl.program_id(0) == 0
        for l in range(cw // lanes):
            cols = slice(l * lanes, (l + 1) * lanes)
            mh = jnp.where(first, 0.0, cch_ref[:, cols].astype(F32) * hch_ref[:, cols].astype(F32))
            mm = cc_ref[:, cols].astype(F32) * hc_ref[:, cols].astype(F32)
            xx = jnp.concatenate([mh, mm], axis=0)
            tap = _Taps(xx, tm, True)
            cv = jnp.zeros((tm, lanes), F32)
            for k in range(SCONV_K):
                cv = cv + sw_ref[k:k + 1, cols] * tap(SCONV_K - 1 - k)
            c_out = bc_ref[:, cols].astype(F32) * cv
            y_ref[:, cols] = (c_out * _silu(g1_ref[:, cols].astype(F32))).astype(BF16)
            dh = jnp.where(first, 0.0, gah_ref[:, cols].astype(F32) * _sigmoid(gbh_ref[:, cols].astype(F32)))
            dm = ga_ref[:, cols].astype(F32) * _sigmoid(gb_ref[:, cols].astype(F32))
            xx = jnp.concatenate([dh, dm], axis=0)
            tap = _Taps(xx, tm, True)
            acc = jnp.zeros((tm, lanes), F32) + db_ref[:, cols]
            for k in range(CONF_K):
                acc = acc + dw_ref[k:k + 1, cols] * tap(CONF_K - 1 - k)
            dc_ref[:, cols] = acc
        rs = 64
        for r in range(tm // rs):
            rows = slice(r * rs, (r + 1) * rs)
            xv = dc_ref[rows, :]
            mu = jnp.mean(xv, axis=-1, keepdims=True)
            xc = xv - mu
            rstd = lax.rsqrt(jnp.mean(xc * xc, axis=-1, keepdims=True) + EPS)
            ln = xc * rstd * gam_ref[...] + bet_ref[...]
            y_ref[rows, cw:2 * cw] = (_silu(ln) * _silu(g2_ref[rows, :].astype(F32))).astype(BF16)

    main = lambda c: pl.BlockSpec((tm, cw), lambda i: (i, c))
    halo = lambda c: pl.BlockSpec((HALO, cw), lambda i: (hb(i), c))
    vec = lambda r: pl.BlockSpec((r, cw), lambda i: (0, 0))
    return pl.pallas_call(
        body, name=name, grid=(n,),
        in_specs=[main(0), halo(0), main(1), main(2), halo(2), main(3), halo(3), main(4), halo(4),
                  main(5), main(6), vec(SCONV_K), vec(CONF_K), vec(1), vec(1), vec(1)] + dep_specs,
        out_specs=[pl.BlockSpec((tm, 2 * cw), lambda i: (i, 0)), pl.BlockSpec((tm, cw), lambda i: (i, 0))],
        out_shape=[jax.ShapeDtypeStruct((s, 2 * cw), BF16), jax.ShapeDtypeStruct((s, cw), F32)],
        compiler_params=_params("parallel"),
    )(p, p, p, p, p, p, p, p, p, p, p, sconv_w, dconv_w, dconv_b, cnorm_g, cnorm_b, *dep_args)


def odd_bwd_ln(dy, p, dc, cnorm_g, cnorm_b, name, tm=256):
    s = p.shape[0]
    cw = dc.shape[1]
    n = s // tm
    rs = 128

    def body(dy_ref, g2_ref, dc_ref, gam_ref, bet_ref, ddc_ref, dg_ref, dgam_ref, dbet_ref, gacc, bacc):
        i = pl.program_id(0)

        @pl.when(i == 0)
        def _():
            gacc[...] = jnp.zeros_like(gacc)
            bacc[...] = jnp.zeros_like(bacc)

        def chunk(ci, carry):
            rows = pl.ds(pl.multiple_of(ci * rs, rs), rs)
            xv = dc_ref[rows, :]
            mu = jnp.mean(xv, axis=-1, keepdims=True)
            xc = xv - mu
            rstd = lax.rsqrt(jnp.mean(xc * xc, axis=-1, keepdims=True) + EPS)
            xh = xc * rstd
            gam = gam_ref[...]
            sl, dsl = _silu_and_grad(xh * gam + bet_ref[...])
            sg, dsg = _silu_and_grad(g2_ref[rows, :].astype(F32))
            dyv = dy_ref[rows, :].astype(F32)
            dg_ref[rows, :] = (dyv * sl * dsg).astype(BF16)
            dln = dyv * sg * dsl
            gacc[...] += _rowsum8(dln * xh)
            bacc[...] += _rowsum8(dln)
            dxh = dln * gam
            ddc_ref[rows, :] = rstd * (dxh - jnp.mean(dxh, axis=-1, keepdims=True)
                                       - xh * jnp.mean(dxh * xh, axis=-1, keepdims=True))
            return carry

        lax.fori_loop(0, tm // rs, chunk, 0)

        @pl.when(i == n - 1)
        def _():
            dgam_ref[...] = jnp.sum(gacc[...], axis=0, keepdims=True)
            dbet_ref[...] = jnp.sum(bacc[...], axis=0, keepdims=True)

    vec = pl.BlockSpec((1, cw), lambda i: (0, 0))
    return pl.pallas_call(
        body, name=name, grid=(n,),
        in_specs=[pl.BlockSpec((tm, cw), lambda i: (i, 1)), pl.BlockSpec((tm, cw), lambda i: (i, 6)),
                  pl.BlockSpec((tm, cw), lambda i: (i, 0)), vec, vec],
        out_specs=[pl.BlockSpec((tm, cw), lambda i: (i, 0)), pl.BlockSpec((tm, cw), lambda i: (i, 0)), vec, vec],
        out_shape=[jax.ShapeDtypeStruct((s, cw), F32), jax.ShapeDtypeStruct((s, cw), BF16),
                   jax.ShapeDtypeStruct((1, cw), F32), jax.ShapeDtypeStruct((1, cw), F32)],
        scratch_shapes=[pltpu.VMEM((8, cw), F32), pltpu.VMEM((8, cw), F32)],
        compiler_params=_params("arbitrary"),
    )(dy, p, dc, cnorm_g, cnorm_b)


def odd_bwd_conv(dy, p, ddc, dg2, sconv_w, dconv_w, name, tm=128):
    s = p.shape[0]
    cw = ddc.shape[1]
    n = s // tm
    lanes = 128
    hb = _halo_before(tm)
    ha = _halo_after(tm, s)

    def body(dy_ref, dya_ref, g1_ref, g1a_ref, bc_ref, bca_ref, hc_ref, hch_ref, cc_ref, cch_ref,
             ddc_ref, ddca_ref, ga_ref, gah_ref, gb_ref, gbh_ref, dg2_ref, sw_ref, dw_ref,
             dp_ref, dsw_ref, ddw_ref, ddb_ref, sw_acc, dw_acc, db_acc):
        i = pl.program_id(0)
        first = i == 0
        last = i == n - 1

        @pl.when(first)
        def _():
            sw_acc[...] = jnp.zeros_like(sw_acc)
            dw_acc[...] = jnp.zeros_like(dw_acc)
            db_acc[...] = jnp.zeros_like(db_acc)

        for l in range(cw // lanes):
            cols = slice(l * lanes, (l + 1) * lanes)
            mh = jnp.where(first, 0.0, cch_ref[:, cols].astype(F32) * hch_ref[:, cols].astype(F32))
            hcv = hc_ref[:, cols].astype(F32)
            ccv = cc_ref[:, cols].astype(F32)
            xx = jnp.concatenate([mh, ccv * hcv], axis=0)
            tap = _Taps(xx, tm, True)
            taps = [tap(SCONV_K - 1 - k) for k in range(SCONV_K)]
            cv = jnp.zeros((tm, lanes), F32)
            for k in range(SCONV_K):
                cv = cv + sw_ref[k:k + 1, cols] * taps[k]
            bcv = bc_ref[:, cols].astype(F32)
            dyv = dy_ref[:, cols].astype(F32)
            sg, dsg = _silu_and_grad(g1_ref[:, cols].astype(F32))
            dco = dyv * sg
            dp_ref[:, 5 * cw + l * lanes:5 * cw + (l + 1) * lanes] = (dyv * bcv * cv * dsg).astype(BF16)
            dp_ref[:, cw + l * lanes:cw + (l + 1) * lanes] = (dco * cv).astype(BF16)
            dcv = dco * bcv
            for k in range(SCONV_K):
                sw_acc[k * 8:(k + 1) * 8, cols] += _rowsum8(dcv * taps[k])
            dcv_a = jnp.where(last, 0.0, dya_ref[:, cols].astype(F32) * _silu(g1a_ref[:, cols].astype(F32))
                              * bca_ref[:, cols].astype(F32))
            xx = jnp.concatenate([dcv, dcv_a], axis=0)
            tap = _Taps(xx, tm, False)
            dm = jnp.zeros((tm, lanes), F32)
            for k in range(SCONV_K):
                dm = dm + sw_ref[k:k + 1, cols] * tap(SCONV_K - 1 - k)
            dp_ref[:, l * lanes:(l + 1) * lanes] = (dm * ccv).astype(BF16)
            dp_ref[:, 2 * cw + l * lanes:2 * cw + (l + 1) * lanes] = (dm * hcv).astype(BF16)
            gav = ga_ref[:, cols].astype(F32)
            sb = _sigmoid(gb_ref[:, cols].astype(F32))
            dh = jnp.where(first, 0.0, gah_ref[:, cols].astype(F32) * _sigmoid(gbh_ref[:, cols].astype(F32)))
            xx = jnp.concatenate([dh, gav * sb], axis=0)
            ddcv = ddc_ref[:, cols]
            db_acc[:, cols] += _rowsum8(ddcv)
            tap = _Taps(xx, tm, True)
            for k in range(CONF_K):
                dw_acc[k * 8:(k + 1) * 8, cols] += _rowsum8(ddcv * tap(CONF_K - 1 - k))
            ddc_a = jnp.where(last, 0.0, ddca_ref[:, cols])
            xx = jnp.concatenate([ddcv, ddc_a], axis=0)
            tap = _Taps(xx, tm, False)
            dgl = jnp.zeros((tm, lanes), F32)
            for k in range(CONF_K):
                dgl = dgl + dw_ref[k:k + 1, cols] * tap(CONF_K - 1 - k)
            dp_ref[:, 3 * cw + l * lanes:3 * cw + (l + 1) * lanes] = (dgl * sb).astype(BF16)
            dp_ref[:, 4 * cw + l * lanes:4 * cw + (l + 1) * lanes] = (dgl * gav * sb * (1.0 - sb)).astype(BF16)
        dp_ref[:, 6 * cw:7 * cw] = dg2_ref[...]

        @pl.when(last)
        def _():
            for k in range(SCONV_K):
                dsw_ref[k:k + 1, :] = jnp.sum(sw_acc[k * 8:(k + 1) * 8, :], axis=0, keepdims=True)
            for k in range(CONF_K):
                ddw_ref[k:k + 1, :] = jnp.sum(dw_acc[k * 8:(k + 1) * 8, :], axis=0, keepdims=True)
            ddb_ref[...] = jnp.sum(db_acc[...], axis=0, keepdims=True)

    def main(c):
        return pl.BlockSpec((tm, cw), lambda i: (i, c))

    def before(c):
        return pl.BlockSpec((HALO, cw), lambda i: (hb(i), c))

    def after(c):
        return pl.BlockSpec((HALO, cw), lambda i: (ha(i), c))

    def vec(r):
        return pl.BlockSpec((r, cw), lambda i: (0, 0))

    return pl.pallas_call(
        body, name=name, grid=(n,),
        in_specs=[main(0), after(0), main(5), after(5), main(1), after(1), main(0), before(0), main(2), before(2),
                  main(0), after(0), main(3), before(3), main(4), before(4), main(0), vec(SCONV_K), vec(CONF_K)],
        out_specs=[pl.BlockSpec((tm, 7 * cw), lambda i: (i, 0)), vec(SCONV_K), vec(CONF_K), vec(1)],
        out_shape=[jax.ShapeDtypeStruct((s, 7 * cw), BF16), jax.ShapeDtypeStruct((SCONV_K, cw), F32),
                   jax.ShapeDtypeStruct((CONF_K, cw), F32), jax.ShapeDtypeStruct((1, cw), F32)],
        scratch_shapes=[pltpu.VMEM((8 * SCONV_K, cw), F32), pltpu.VMEM((8 * CONF_K, cw), F32),
                        pltpu.VMEM((8, cw), F32)],
        compiler_params=_params("arbitrary"),
    )(dy, dy, p, p, p, p, p, p, p, p, ddc, ddc, p, p, p, p, dg2, sconv_w, dconv_w)


_ANY = pl.BlockSpec(memory_space=pl.ANY)


def _place():
    return lax.axis_index("x"), lax.axis_index("y"), lax.axis_index("c")


def all_gather(arrs, name, deps=()):
    n = len(arrs)

    def body(*refs):
        ins, outs = refs[:n], refs[n + len(deps):2 * n + len(deps)]
        send_sems, recv_sems, local_sems = refs[-3:]
        x, y, c = _place()
        me, sibling = (x, y, c), (x, y, 1 - c)
        chips = [(1 - x, y), (x, 1 - y), (1 - x, 1 - y)]

        def copy(a, k, block, to, src=None):
            px, py, pc = block
            dst = outs[a].at[4 * px + 2 * py + pc]
            return pltpu.make_async_remote_copy(
                src_ref=dst if src is None else src, dst_ref=dst,
                send_sem=send_sems.at[7 * a + k], recv_sem=recv_sems.at[7 * a + k],
                device_id=to, device_id_type=MESH)

        mine = [pltpu.make_async_copy(ins[a], outs[a].at[4 * x + 2 * y + c], local_sems.at[a]) for a in range(n)]
        first = []
        for a in range(n):
            first.append(copy(a, 0, me, sibling, src=ins[a]))
            first += [copy(a, 1 + j, me, (*chip, c), src=ins[a]) for j, chip in enumerate(chips)]
        for cp in first + mine:
            cp.start()
        passed = []
        for a in range(n):
            for j, chip in enumerate(chips):
                copy(a, 1 + j, (*chip, c), me).wait_recv()
                cp = copy(a, 4 + j, (*chip, c), sibling)
                cp.start()
                passed.append(cp)
        for a in range(n):
            copy(a, 0, sibling, me).wait_recv()
            for j, chip in enumerate(chips):
                copy(a, 4 + j, (*chip, 1 - c), me).wait_recv()
        for cp in first + passed:
            cp.wait_send()
        for cp in mine:
            cp.wait()

    return pl.pallas_call(
        body, name=name,
        out_shape=[jax.ShapeDtypeStruct((N_DEV,) + a.shape, a.dtype) for a in arrs],
        in_specs=[_ANY] * (n + len(deps)), out_specs=[_ANY] * n,
        scratch_shapes=[pltpu.SemaphoreType.DMA((7 * n,)), pltpu.SemaphoreType.DMA((7 * n,)),
                        pltpu.SemaphoreType.DMA((n,))],
    )(*arrs, *deps)


def in_proj_gathered(xs, g, w_own, extras, name, tm=1024):
    s, d = xs.shape
    n = w_own.shape[1]
    tm = min(tm, s)
    arrs = [w_own] + list(extras)
    na = len(arrs)
    tr = 256

    def body(*refs):
        x_ref, g_ref, ins = refs[0], refs[1], refs[2:2 + na]
        h_out, p_ref, outs = refs[2 + na], refs[3 + na], refs[4 + na:4 + 2 * na]
        (h_ref, xbuf, wbuf, obuf, send_sems, recv_sems, load_sem, store_sems, own_sems, h_sem,
         x_sems) = refs[4 + 2 * na:]
        x, y, c = _place()
        me, sibling = (x, y, c), (x, y, 1 - c)
        x_first = c == 0
        near = (jnp.where(x_first, 1 - x, x), jnp.where(x_first, y, 1 - y))
        far = (jnp.where(x_first, x, 1 - x), jnp.where(x_first, 1 - y, y))
        diag = (1 - x, 1 - y)
        k_near, k_far = jnp.where(x_first, 1, 2), jnp.where(x_first, 2, 1)
        f_near, f_far = k_near + 3, k_far + 3

        def slot(block):
            return 4 * block[0] + 2 * block[1] + block[2]

        def copy(a, k, block, to, src=None):
            dst = outs[a].at[slot(block)]
            return pltpu.make_async_remote_copy(
                src_ref=dst if src is None else src, dst_ref=dst,
                send_sem=send_sems.at[7 * a + k], recv_sem=recv_sems.at[7 * a + k],
                device_id=to, device_id_type=MESH)

        first = []
        for a in range(na):
            first += [copy(a, 0, me, sibling, src=ins[a]), copy(a, 1, me, (1 - x, y, c), src=ins[a]),
                      copy(a, 2, me, (x, 1 - y, c), src=ins[a])]
        for cp in first:
            cp.start()
        own = pltpu.make_async_copy(wbuf.at[0], outs[0].at[slot(me)], own_sems.at[0])
        mine = [pltpu.make_async_copy(ins[a], outs[a].at[slot(me)], own_sems.at[a]) for a in range(1, na)]
        stores = [None, None]

        def x_load(i):
            return pltpu.make_async_copy(x_ref.at[pl.ds(i * tr, tr), :], xbuf.at[i % 2], x_sems.at[i % 2])

        x_load(0).start()
        for i in range(s // tr):
            if i + 1 < s // tr:
                x_load(i + 1).start()
            x_load(i).wait()
            xv = xbuf[i % 2]
            r = lax.rsqrt(jnp.mean(xv * xv, axis=-1, keepdims=True) + EPS)
            h_ref[i * tr:(i + 1) * tr, :] = (xv * r * g_ref[...]).astype(BF16)
        h_store = pltpu.make_async_copy(h_ref, h_out, h_sem)
        h_store.start()

        def multiply(k, block, w_from):
            b = k % 2
            if k == 2:
                own.wait()
            load = pltpu.make_async_copy(w_from, wbuf.at[b], load_sem)
            load.start()
            if stores[b] is not None:
                stores[b].wait()
            load.wait()
            if k == 0:
                own.start()

            def chunk(i, carry):
                rows = pl.ds(pl.multiple_of(i * tm, tm), tm)
                obuf[b, rows, :] = jnp.dot(h_ref[rows, :], wbuf[b], preferred_element_type=F32).astype(BF16)
                return carry

            lax.fori_loop(0, s // tm, chunk, 0)
            stores[b] = pltpu.make_async_copy(
                obuf.at[b], p_ref.at[:, pl.ds(pl.multiple_of(slot(block) * n, 128), n)], store_sems.at[b])
            stores[b].start()

        passed = []

        def arrive(a, k, block):
            copy(a, k, block, me).wait_recv()

        def pass_on(a, k, block, to):
            cp = copy(a, k, block, to)
            cp.start()
            passed.append(cp)

        def gather(arrays, use):
            def arrive_all(k, block):
                for a in arrays:
                    arrive(a, k, block)

            def pass_all(k, block, to):
                for a in arrays:
                    pass_on(a, k, block, to)

            use(0, me)
            arrive_all(0, sibling)
            use(1, sibling)
            arrive_all(k_near, (*near, c))
            pass_all(3, (*near, c), (*far, c))
            pass_all(f_near, (*near, c), sibling)
            use(2, (*near, c))
            arrive_all(f_far, (*far, 1 - c))
            use(3, (*far, 1 - c))
            arrive_all(k_far, (*far, c))
            pass_all(f_far, (*far, c), sibling)
            use(4, (*far, c))
            arrive_all(f_near, (*near, 1 - c))
            use(5, (*near, 1 - c))
            arrive_all(3, (*diag, c))
            pass_all(6, (*diag, c), sibling)
            use(6, (*diag, c))
            arrive_all(6, (*diag, 1 - c))
            use(7, (*diag, 1 - c))

        gather(range(na), lambda k, block: multiply(k, block, ins[0] if k == 0 else outs[0].at[slot(block)]))
        for cp in mine:
            cp.start()
        for cp in first + passed:
            cp.wait_send()
        for cp in mine + stores + [h_store]:
            cp.wait()

    vmem = pl.BlockSpec(memory_space=pltpu.VMEM)
    outs = pl.pallas_call(
        body, name=name,
        out_shape=[jax.ShapeDtypeStruct((s, d), BF16), jax.ShapeDtypeStruct((s, N_DEV * n), BF16)]
        + [jax.ShapeDtypeStruct((N_DEV,) + a.shape, a.dtype) for a in arrs],
        in_specs=[_ANY, vmem] + [_ANY] * na, out_specs=[_ANY] * (2 + na),
        scratch_shapes=[pltpu.VMEM((s, d), BF16), pltpu.VMEM((2, tr, d), F32), pltpu.VMEM((2, d, n), BF16),
                        pltpu.VMEM((2, s, n), BF16),
                        pltpu.SemaphoreType.DMA((7 * na,)), pltpu.SemaphoreType.DMA((7 * na,)),
                        pltpu.SemaphoreType.DMA, pltpu.SemaphoreType.DMA((2,)), pltpu.SemaphoreType.DMA((na,)),
                        pltpu.SemaphoreType.DMA, pltpu.SemaphoreType.DMA((2,))],
        compiler_params=pltpu.CompilerParams(vmem_limit_bytes=VMEM_LIMIT),
    )(xs, g, *arrs)
    return outs[0], outs[1], outs[2], outs[3:]


_HBM = pl.BlockSpec(memory_space=pltpu.HBM)
_SEM = pl.BlockSpec(memory_space=pltpu.SEMAPHORE)
_DATAFLOW = pltpu.SideEffectType.DATAFLOW_SIDE_EFFECTING


def _peers_per_array(kind):
    return {"sibling": 1, "halves": 1, "pass": 1, "neighbours": 2}.get(kind, 3)


def _near_far():
    x, y, c = _place()
    x_first = c == 0
    near = (jnp.where(x_first, 1 - x, x), jnp.where(x_first, y, 1 - y))
    far = (jnp.where(x_first, x, 1 - x), jnp.where(x_first, 1 - y, y))
    return near, far, jnp.where(x_first, 0, 1), jnp.where(x_first, 1, 0)


def _split_copies(kind, srcs, lands, send_sems, recv_sems):
    x, y, c = _place()
    per = _peers_per_array(kind)
    out = []
    for a in range(len(lands)):
        if kind == "sibling":
            part = srcs[a] if srcs[a].shape[1] == 1 else srcs[a].at[:, pl.ds(1 - c, 1)]
            peers = [((x, y, 1 - c), part, lands[a], lands[a])]
        elif kind == "halves":
            mine, its = lands[a].at[:, pl.ds(c, 1)], lands[a].at[:, pl.ds(1 - c, 1)]
            peers = [((x, y, 1 - c), mine, mine, its)]
        elif kind == "neighbours":
            here = lands[a].at[4 * x + 2 * y + c]
            peers = [((px, py, c), srcs[a], here, lands[a].at[4 * px + 2 * py + c])
                     for px, py in [(1 - x, y), (x, 1 - y)]]
        elif kind == "pass":
            near, far, _, _ = _near_far()
            block = lands[a].at[4 * near[0] + 2 * near[1] + c]
            peers = [((*far, c), block, block, lands[a].at[4 * (1 - x) + 2 * (1 - y) + c])]
        else:
            peers = []
            for px, py in [(1 - x, y), (x, 1 - y), (1 - x, 1 - y)]:
                if kind == "gather":
                    views = (srcs[a], lands[a].at[4 * x + 2 * y + c], lands[a].at[4 * px + 2 * py + c])
                else:
                    views = (srcs[a].at[2 * px + py], lands[a].at[2 * x + y], lands[a].at[2 * px + py])
                peers.append(((px, py, c),) + views)
        for j, (peer, src, dst, arrives) in enumerate(peers):
            sems = dict(send_sem=send_sems.at[per * a + j], recv_sem=recv_sems.at[per * a + j],
                        device_id=peer, device_id_type=MESH)
            out.append((pltpu.make_async_remote_copy(src_ref=src, dst_ref=dst, **sems),
                        pltpu.make_async_remote_copy(src_ref=src, dst_ref=arrives, **sems)))
    return out


def split_start(kind, srcs, lands, deps, name):
    ns, nl = len(srcs), len(lands)
    n_sems = _peers_per_array(kind) * nl
    held = list(srcs) + list(lands)

    def body(*refs):
        send_sems, recv_sems = refs[len(held) + len(deps)], refs[len(held) + len(deps) + 1]
        for copy, _ in _split_copies(kind, refs[:ns], refs[ns:ns + nl], send_sems, recv_sems):
            copy.start()
        token = refs[-1]
        token[...] = jnp.zeros_like(token)

    outs = pl.pallas_call(
        body, name=name,
        out_shape=(pltpu.SemaphoreType.DMA((n_sems,)), pltpu.SemaphoreType.DMA((n_sems,)),
                   *[pltpu.HBM(a.shape, a.dtype) for a in held], jax.ShapeDtypeStruct((8, 128), F32)),
        in_specs=[_HBM] * len(held) + [_ANY] * len(deps),
        out_specs=(_SEM, _SEM, *([_HBM] * len(held)), pl.BlockSpec(memory_space=pltpu.VMEM)),
        input_output_aliases={i: 2 + i for i in range(len(held))},
        compiler_params=pltpu.CompilerParams(has_side_effects=_DATAFLOW),
    )(*[pltpu.with_memory_space_constraint(a, pltpu.HBM) for a in held], *deps)
    return outs[0], outs[1], list(outs[2:2 + ns]), list(outs[2 + ns:2 + ns + nl]), outs[-1]


def split_wait(kind, send_sems, recv_sems, srcs, lands, afters, name):
    ns, nl = len(srcs), len(lands)
    held = list(srcs) + list(lands)

    def body(*refs):
        for _, arrival in _split_copies(kind, refs[:ns], refs[ns:ns + nl], refs[ns + nl], refs[ns + nl + 1]):
            arrival.wait_send()
            arrival.wait_recv()

    outs = pl.pallas_call(
        body, name=name,
        out_shape=[pltpu.HBM(a.shape, a.dtype) for a in held],
        in_specs=[_HBM] * len(held) + [_SEM, _SEM] + [_ANY] * len(afters),
        out_specs=[_HBM] * len(held),
        input_output_aliases={i: i for i in range(len(held))},
        compiler_params=pltpu.CompilerParams(has_side_effects=_DATAFLOW),
    )(*held, send_sems, recv_sems, *afters)
    return list(outs[:ns]), list(outs[ns:])


def split_pass_on(lands, first_recv, afters, name):
    n = len(lands)

    def body(*refs):
        lands_r, first = refs[:n], refs[n]
        send_sems, recv_sems = refs[n + 1 + len(afters)], refs[n + 2 + len(afters)]
        c = lax.axis_index("c")
        near, _, k_near, _ = _near_far()
        for a in range(n):
            block = lands_r[a].at[4 * near[0] + 2 * near[1] + c]
            pltpu.make_async_remote_copy(
                src_ref=block, dst_ref=block, send_sem=first.at[2 * a + k_near], recv_sem=first.at[2 * a + k_near],
                device_id=(near[0], near[1], c), device_id_type=MESH).wait_recv()
        for copy, _ in _split_copies("pass", [], lands_r, send_sems, recv_sems):
            copy.start()
        token = refs[-1]
        token[...] = jnp.zeros_like(token)

    outs = pl.pallas_call(
        body, name=name,
        out_shape=(pltpu.SemaphoreType.DMA((n,)), pltpu.SemaphoreType.DMA((n,)),
                   *[pltpu.HBM(a.shape, a.dtype) for a in lands], jax.ShapeDtypeStruct((8, 128), F32)),
        in_specs=[_HBM] * n + [_SEM] + [_ANY] * len(afters),
        out_specs=(_SEM, _SEM, *([_HBM] * n), pl.BlockSpec(memory_space=pltpu.VMEM)),
        input_output_aliases={i: 2 + i for i in range(n)},
        compiler_params=pltpu.CompilerParams(has_side_effects=_DATAFLOW),
    )(*lands, first_recv, *afters)
    return outs[0], outs[1], list(outs[2:2 + n]), outs[-1]


def split_wait_neighbours(first_send, first_recv, pass_send, pass_recv, srcs, lands, afters, name):
    n = len(lands)
    held = list(srcs) + list(lands)

    def body(*refs):
        srcs_r, lands_r = refs[:n], refs[n:2 * n]
        send1, recv1, send2, recv2 = refs[2 * n:2 * n + 4]
        c = lax.axis_index("c")
        _, far, _, k_far = _near_far()
        for _, arrival in _split_copies("neighbours", srcs_r, lands_r, send1, recv1):
            arrival.wait_send()
        for a in range(n):
            block = lands_r[a].at[4 * far[0] + 2 * far[1] + c]
            pltpu.make_async_remote_copy(
                src_ref=block, dst_ref=block, send_sem=recv1.at[2 * a + k_far], recv_sem=recv1.at[2 * a + k_far],
                device_id=(far[0], far[1], c), device_id_type=MESH).wait_recv()
        for _, arrival in _split_copies("pass", [], lands_r, send2, recv2):
            arrival.wait_send()
            arrival.wait_recv()

    outs = pl.pallas_call(
        body, name=name,
        out_shape=[pltpu.HBM(a.shape, a.dtype) for a in held],
        in_specs=[_HBM] * len(held) + [_SEM] * 4 + [_ANY] * len(afters),
        out_specs=[_HBM] * len(held),
        input_output_aliases={i: i for i in range(len(held))},
        compiler_params=pltpu.CompilerParams(has_side_effects=_DATAFLOW),
    )(*held, first_send, first_recv, pass_send, pass_recv, *afters)
    return list(outs[:n]), list(outs[n:])


def place_block(land, block, dev, name):
    r, c = block.shape
    tr = min(r, 512)

    def body(dev_ref, land_ref, b_ref, o_ref):
        del dev_ref, land_ref
        o_ref[...] = b_ref[...]

    return pl.pallas_call(
        body, name=name,
        grid_spec=pltpu.PrefetchScalarGridSpec(
            num_scalar_prefetch=1, grid=(r // tr,),
            in_specs=[_ANY, pl.BlockSpec((tr, c), lambda i, dev_ref: (i, 0))],
            out_specs=pl.BlockSpec((None, tr, c), lambda i, dev_ref: (dev_ref[0], i, 0))),
        out_shape=jax.ShapeDtypeStruct(land.shape, land.dtype),
        input_output_aliases={1: 0},
        compiler_params=_params("parallel"),
    )(dev, land, block)


def pair_add(own, recv, core, name):
    _, _, r, c = own.shape
    tr = min(r, 2048)

    def body(core_ref, own_ref, recv_ref, o_ref):
        del core_ref
        o_ref[...] = (own_ref[...].astype(F32) + recv_ref[...].astype(F32)).astype(BF16)

    return pl.pallas_call(
        body, name=name,
        grid_spec=pltpu.PrefetchScalarGridSpec(
            num_scalar_prefetch=1, grid=(4, r // tr),
            in_specs=[pl.BlockSpec((None, None, tr, c), lambda k, i, core_ref: (k, core_ref[0], i, 0)),
                      pl.BlockSpec((None, None, tr, c), lambda k, i, core_ref: (k, 0, i, 0))],
            out_specs=pl.BlockSpec((None, tr, c), lambda k, i, core_ref: (k, i, 0))),
        out_shape=jax.ShapeDtypeStruct((4, r, c), BF16),
        compiler_params=_params("parallel", "parallel"),
    )(core, own, recv)


def _adamw_math(w, g, m, v):
    m2 = ADAM_B1 * m + (1.0 - ADAM_B1) * g
    v2 = ADAM_B2 * v + (1.0 - ADAM_B2) * (g * g)
    m_hat = m2 / (1.0 - ADAM_B1 ** ADAM_STEP)
    v_hat = v2 / (1.0 - ADAM_B2 ** ADAM_STEP)
    delta = -ADAM_LR * (m_hat / (jnp.sqrt(v_hat) + ADAM_EPS) + ADAM_WD * w)
    return delta, m2, v2


def adamw_big(w, m, v, own, got, chip, name):
    r, c = w.shape
    tr = min(r, 512)

    def body(chip_ref, w_ref, m_ref, v_ref, p0, p1, p2, p3, g_ref, d_ref, m2_ref, v2_ref):
        del chip_ref
        g = ((p0[...].astype(F32) + p1[...].astype(F32)) + p2[...].astype(F32)) + p3[...].astype(F32)
        delta, m2, v2 = _adamw_math(w_ref[...], g, m_ref[...], v_ref[...])
        g_ref[...] = g
        d_ref[...] = delta
        m2_ref[...] = m2
        v2_ref[...] = v2

    row = pl.BlockSpec((tr, c), lambda i, chip_ref: (i, 0))

    def slab(flip):
        return pl.BlockSpec((None, tr, c), lambda i, chip_ref: (chip_ref[0] ^ flip, i, 0))

    return pl.pallas_call(
        body, name=name,
        grid_spec=pltpu.PrefetchScalarGridSpec(
            num_scalar_prefetch=1, grid=(r // tr,),
            in_specs=[row, row, row, slab(0), slab(1), slab(2), slab(3)],
            out_specs=[row] * 4),
        out_shape=[jax.ShapeDtypeStruct((r, c), F32)] * 4,
        compiler_params=_params("parallel"),
    )(chip, w, m, v, own, got, got, got)


def sum_devices(g8, name):
    def body(g_ref, o_ref):
        tot = g_ref[0]
        for k in range(1, N_DEV):
            tot = tot + g_ref[k]
        o_ref[...] = tot

    return pl.pallas_call(body, name=name, out_shape=jax.ShapeDtypeStruct(g8.shape[1:], F32))(g8)


def adamw_small(ws, gs, ms, vs, name):
    n = len(ws)

    def body(*refs):
        w_r, g_r, m_r, v_r = refs[:n], refs[n:2 * n], refs[2 * n:3 * n], refs[3 * n:4 * n]
        d_o, m_o, v_o = refs[4 * n:5 * n], refs[5 * n:6 * n], refs[6 * n:7 * n]
        for k in range(n):
            delta, m2, v2 = _adamw_math(w_r[k][...], g_r[k][...], m_r[k][...], v_r[k][...])
            d_o[k][...] = delta
            m_o[k][...] = m2
            v_o[k][...] = v2

    shapes = [jax.ShapeDtypeStruct(w.shape, F32) for w in ws]
    outs = pl.pallas_call(body, name=name, out_shape=shapes * 3)(*ws, *gs, *ms, *vs)
    return outs[:n], outs[n:2 * n], outs[2 * n:]


def _rows128(a):
    return a.reshape(-1, 128)


def _pad_rows(a, rows):
    return jnp.pad(a, ((0, rows - a.shape[0]), (0, 0)))


def kernel(x, ln_pre_even, w_in_even, pool_w, pool_scale, w_out_even, ln_post_even, ln_pre_odd, w_in_odd, sconv_w, dconv_w, dconv_b, cnorm_g, cnorm_b, w_out_odd, ln_post_odd, loss_target, m_ln_pre_even, m_w_in_even, m_pool_w, m_pool_scale, m_w_out_even, m_ln_post_even, m_ln_pre_odd, m_w_in_odd, m_sconv_w, m_dconv_w, m_dconv_b, m_cnorm_g, m_cnorm_b, m_w_out_odd, m_ln_post_odd, v_ln_pre_even, v_w_in_even, v_pool_w, v_pool_scale, v_w_out_even, v_ln_post_even, v_ln_pre_odd, v_w_in_odd, v_sconv_w, v_dconv_w, v_dconv_b, v_cnorm_g, v_cnorm_b, v_w_out_odd, v_ln_post_odd):
    xs = x[0]
    tgt = loss_target[0]
    s, d = xs.shape
    half = d // 2
    n_heads = half // HEAD_DIM
    ng = len(POOL_WINDOWS)
    cwp = half // ng
    dev = 4 * lax.axis_index("x") + 2 * lax.axis_index("y") + lax.axis_index("c")
    core = lax.axis_index("c").astype(jnp.int32).reshape(1)

    pr = pool_w.shape[2]
    cl = sconv_w.shape[2]
    small_parts = [(_rows128(ln_pre_odd), 8), (sconv_w[0], 8), (dconv_w[0], 32), (dconv_b, 8),
                   (cnorm_g, 8), (cnorm_b, 8), (_rows128(ln_post_odd), 8)]
    small_local = jnp.concatenate([_pad_rows(a, r) for a, r in small_parts], axis=0)
    h0, p0, g_wie, (g_pw, g_small) = in_proj_gathered(
        xs, ln_pre_even, w_in_even[0].astype(BF16), [pool_w[0].reshape(ng * pr, cwp).astype(BF16), small_local],
        "ag_in_proj_even")
    comm = _Exchanges(dev, core, d)
    token = comm.start_weights("out_even", [w_out_even[0].astype(BF16)], [p0])
    token = comm.start_weights("in_odd", [w_in_odd[0].astype(BF16)], [token], neighbours=True)
    sb_dep = token
    comm.later["out_odd"] = [w_out_odd[0].astype(BF16)]
    pool_full = g_pw.reshape(N_DEV, ng, pr, cwp).transpose(1, 0, 2, 3).reshape(ng, cwp, cwp)
    nl = ln_pre_odd.shape[1] // 128

    def chan(lo, rows):
        return g_small[:, lo:lo + rows].transpose(1, 0, 2).reshape(rows, N_DEV * cl)

    ln_pre_odd_f = g_small[:, 0:nl].reshape(1, d)
    sconv_f = chan(8, SCONV_K)
    dconv_f = chan(16, CONF_K)
    dconv_b_f = chan(48, 1)
    cnorm_g_f = chan(56, 1)
    cnorm_b_f = chan(64, 1)
    ln_post_odd_f = g_small[:, 72:72 + nl].reshape(1, d)

    loss_blk, grad_x, small_g = _fwd_bwd(
        xs, tgt, ln_pre_even, h0, p0, g_wie, pool_full, pool_scale, ln_post_even, ln_pre_odd_f,
        sconv_f, dconv_f, dconv_b_f, cnorm_g_f, cnorm_b_f, ln_post_odd_f, comm, sb_dep)
    small_w = [ln_pre_even, pool_scale, ln_post_even, ln_pre_odd, sconv_w[0], dconv_w[0], dconv_b, cnorm_g, cnorm_b, ln_post_odd]
    small_m = [m_ln_pre_even, m_pool_scale, m_ln_post_even, m_ln_pre_odd, m_sconv_w[0], m_dconv_w[0], m_dconv_b, m_cnorm_g, m_cnorm_b, m_ln_post_odd]
    small_v = [v_ln_pre_even, v_pool_scale, v_ln_post_even, v_ln_pre_odd, v_sconv_w[0], v_dconv_w[0], v_dconv_b, v_cnorm_g, v_cnorm_b, v_ln_post_odd]
    big = {"w_in_even": (w_in_even, m_w_in_even, v_w_in_even), "pool_w": (pool_w, m_pool_w, v_pool_w),
           "w_out_even": (w_out_even, m_w_out_even, v_w_out_even), "w_in_odd": (w_in_odd, m_w_in_odd, v_w_in_odd),
           "w_out_odd": (w_out_odd, m_w_out_odd, v_w_out_odd)}
    upd = comm.finish_updates(big, [grad_x])
    upd.update(comm.finish_updates(big, [grad_x]))
    sg, sd, sm, sv, loss = _update_small(small_g, loss_blk, small_w, small_m, small_v, dev, d, cl,
                                         deps=[upd["w_in_odd"][1], upd["w_out_even"][1]])
    upd.update(comm.finish_updates(big, sd))
    (g_wie_o, d_wie, m_wie, v_wie), (g_pw_o, d_pw, m_pw, v_pw) = upd["w_in_even"], upd["pool_w"]
    (g_woe_o, d_woe, m_woe, v_woe), (g_wio_o, d_wio, m_wio, v_wio) = upd["w_out_even"], upd["w_in_odd"]
    g_woo_o, d_woo, m_woo, v_woo = upd["w_out_odd"]

    def order(small, wie, pw, woe, wio, woo):
        return [small[0], wie, pw, small[1], woe, small[2], small[3], wio, small[4], small[5], small[6],
                small[7], small[8], woo, small[9]]

    grads = order(sg, g_wie_o, g_pw_o, g_woe_o, g_wio_o, g_woo_o)
    deltas = order(sd, d_wie, d_pw, d_woe, d_wio, d_woo)
    new_m = order(sm, m_wie, m_pw, m_woe, m_wio, m_woo)
    new_v = order(sv, v_wie, v_pw, v_woe, v_wio, v_woo)
    return (loss, grad_x[None], *grads, *deltas, *new_m, *new_v)


def _fwd_bwd(xs, tgt, ln_pre_even, h0, p0, g_wie, pool_full, pool_scale, ln_post_even, ln_pre_odd_f,
             sconv_f, dconv_f, dconv_b_f, cnorm_g_f, cnorm_b_f, ln_post_odd_f, comm, sb_dep):
    d = xs.shape[1]
    n_heads = d // 2 // HEAD_DIM
    ng, cwp = pool_full.shape[0], pool_full.shape[1]
    a0, sb_wts = sb_fwd(p0, n_heads, "sb_fwd", dep=sb_dep)
    dep = comm.weights_arrived("out_even", after=a0)
    y0 = even_mix_fwd(a0, p0, pool_full, pool_scale, "even_mix_fwd", dep=dep)
    (w_out_e,) = comm.weights("out_even", after=y0)
    w_out_e = w_out_e.reshape(1, d, d)
    dep = comm.weights_pass_on("in_odd", after=w_out_e)
    o0 = mm_nn(y0, w_out_e, BF16, "out_proj_even", tn=512, dep=dep)
    dep = comm.weights_arrived("in_odd", after=o0)
    dep = comm.start_weights("out_odd", comm.later.pop("out_odd"), [dep])
    x1, h1 = postnorm_fwd(xs, o0, ln_post_even, ln_pre_odd_f, "post_even", dep=dep)
    (g_wio,) = comm.weights("in_odd", after=x1)
    p1 = mm_nn(h1, g_wio, BF16, "in_proj_odd", group=2)
    dep = comm.weights_arrived("out_odd", after=p1)
    y1, dc = odd_mix_fwd(p1, sconv_f, dconv_f, dconv_b_f, cnorm_g_f, cnorm_b_f, "odd_mix_fwd", dep=dep)
    (w_out_o,) = comm.weights("out_odd", after=y1)
    w_out_o = w_out_o.reshape(1, d, d)
    o1 = mm_nn(y1, w_out_o, BF16, "out_proj_odd", tn=512)
    loss_blk, gx2, do1, dg_post_odd = final_fwd_bwd(x1, o1, ln_post_odd_f, tgt, "post_odd_loss")

    dw_out_o = mm_tn(y1, do1, 1, BF16, "dw_out_odd")
    dy1 = mm_nt(do1, w_out_o, BF16, "dy_odd")
    ddc, dg2, dgam, dbet = odd_bwd_ln(dy1, p1, dc, cnorm_g_f, cnorm_b_f, "odd_bwd_ln")
    dp1, dsconv, ddconv, ddconv_b = odd_bwd_conv(dy1, p1, ddc, dg2, sconv_f, dconv_f, "odd_bwd_conv")
    dw_in_o = mm_tn(h1, dp1, N_DEV, BF16, "dw_in_odd", group=2)
    dep = comm.reduce_begin({"w_out_odd": dw_out_o.reshape(N_DEV, d // N_DEV, d), "w_in_odd": dw_in_o}, "odd")
    dh1 = mm_nt(dp1, g_wio, BF16, "dh_odd", dep=dep, group=2)
    dep = comm.reduce_send(after=dh1)
    gx1, dg_pre_odd, do0, dg_post_even = norm_bwd(dh1, x1, ln_pre_odd_f, gx2, "pre_odd_post_even_bwd",
                                                  inp2=o0, g2=ln_post_even, dep=dep)

    dw_out_e = mm_tn(y0, do0, 1, BF16, "dw_out_even")
    dy0 = mm_nt(do0, w_out_e, BF16, "dy_even")
    da0, du0, dg0, dpool, dpool_scale = even_mix_bwd(dy0, a0, p0, pool_full, pool_scale, "even_mix_bwd")
    pr = cwp // N_DEV
    dpool_slabs = dpool.astype(BF16).reshape(ng, N_DEV, pr, cwp).transpose(1, 0, 2, 3).reshape(N_DEV, ng * pr, cwp)
    dep = comm.reduce_begin({"w_out_even": dw_out_e.reshape(N_DEV, d // N_DEV, d), "pool_w": dpool_slabs}, "even_out")
    dq0, dk0, dv0 = sb_bwd(p0, a0, sb_wts, da0, n_heads, "sb_bwd", dep=dep)
    dep = comm.reduce_send(after=dq0)
    dp0 = jnp.concatenate([dq0, dk0, dv0, du0, dg0], axis=1)
    dw_sibling = mm_tn(h0, dp0, N_DEV // 2, BF16, "dw_in_even_sibling", dep=dep, pick=(2, 1 - comm.core))
    dep = comm.reduce_begin({"w_in_even": dw_sibling}, "even_in", sibling_part=True)
    dw_own = mm_tn(h0, dp0, N_DEV // 2, BF16, "dw_in_even_own", dep=dep, pick=(2, comm.core))
    dep = comm.reduce_send(after=dw_own, own_part={"w_in_even": dw_own})
    dh0 = mm_nt(dp0, g_wie, BF16, "dh_even", dep=dep, group=2)
    dep = None
    grad_x, dg_pre_even = norm_bwd(dh0, xs, ln_pre_even, gx1, "pre_even_bwd", tm=512, dep=dep)
    small_g = [dg_pre_even, dpool_scale, dg_post_even, dg_pre_odd, dsconv, ddconv, ddconv_b, dgam, dbet, dg_post_odd]
    return loss_blk, grad_x, small_g


class _Exchanges:
    def __init__(self, dev, core, d):
        self.dev = dev.astype(jnp.int32).reshape(1)
        self.core = core
        self.chip = (dev // 2).astype(jnp.int32).reshape(1)
        self.d = d
        self.in_flight = {}
        self.later = {}
        self.to_sibling = None
        self.pending = []

    def start_weights(self, tag, blocks, afters, neighbours=False):
        lands = [lax.empty((N_DEV,) + b.shape, b.dtype) for b in blocks]
        kind = "neighbours" if neighbours else "gather"
        send, recv, srcs, lands, token = split_start(kind, blocks, lands, afters, "ag_start_" + tag)
        self.in_flight[tag] = (send, recv, srcs, lands)
        return token

    def weights_pass_on(self, tag, after):
        send, recv, srcs, lands = self.in_flight.pop(tag)
        send2, recv2, lands, token = split_pass_on(lands, recv, [after], "ag_pass_on_" + tag)
        self.in_flight[tag] = (send, recv, srcs, lands, send2, recv2)
        return token

    def weights_arrived(self, tag, after):
        entry = self.in_flight.pop(tag)
        send, recv, srcs, lands = entry[:4]
        if len(entry) == 6:
            srcs, lands = split_wait_neighbours(send, recv, entry[4], entry[5], srcs, lands, [after], "ag_wait_" + tag)
        else:
            srcs, lands = split_wait("gather", send, recv, srcs, lands, [after], "ag_wait_" + tag)
        lands = [place_block(l, b, self.dev, "ag_own_%s_%d" % (tag, k)) for k, (l, b) in enumerate(zip(lands, srcs))]
        lands = [l.reshape((4, 2) + l.shape[1:]) for l in lands]
        send, recv, _, lands, token = split_start("halves", [], lands, [], "ag_sibling_start_" + tag)
        self.in_flight[tag] = (send, recv, lands)
        return token

    def weights(self, tag, after):
        send, recv, lands = self.in_flight.pop(tag)
        _, lands = split_wait("halves", send, recv, [], lands, [after], "ag_sibling_wait_" + tag)
        return [l.reshape((N_DEV,) + l.shape[2:]) for l in lands]

    def reduce_begin(self, partials, tag, sibling_part=False):
        names = list(partials)
        arrs = [partials[k].reshape((4, 1 if sibling_part else 2) + partials[k].shape[1:]) for k in names]
        lands = [lax.empty((4, 1) + a.shape[2:], a.dtype) for a in arrs]
        send, recv, srcs, lands, token = split_start("sibling", arrs, lands, [], "rs_sibling_start_" + tag)
        self.to_sibling = (tag, names, send, recv, srcs, lands)
        return token

    def reduce_send(self, after, own_part=None):
        tag, names, send, recv, srcs, lands = self.to_sibling
        srcs, lands = split_wait("sibling", send, recv, srcs, lands, [after], "rs_sibling_wait_" + tag)
        which = self.core
        if own_part is not None:
            srcs = [own_part[k].reshape((4, 1) + own_part[k].shape[1:]) for k in names]
            which = jnp.zeros((1,), jnp.int32)
        sums = [pair_add(o, r, which, "rs_pair_add_" + k) for k, o, r in zip(names, srcs, lands)]
        zones = [lax.empty(a.shape, a.dtype) for a in sums]
        send, recv, srcs, zones, token = split_start("scatter", sums, zones, [], "rs_start_" + tag)
        self.pending.append((tag, names, send, recv, srcs, zones))
        return token

    def finish_updates(self, big, afters):
        tag, names, send, recv, srcs, lands = self.pending.pop(0)
        srcs, lands = split_wait("scatter", send, recv, srcs, lands, afters, "rs_wait_" + tag)
        out = {}
        for name, own, got in zip(names, srcs, lands):
            w, m, v = big[name]
            shp = own.shape[1:]
            outs = adamw_big(w.reshape(shp), m.reshape(shp), v.reshape(shp), own, got, self.chip, "adamw_" + name)
            out[name] = [o.reshape(w.shape) for o in outs]
        return out


def _update_small(small_g, loss_blk, small_w, small_m, small_v, dev, d, cl, deps):
    packed = jnp.concatenate([_rows128(g) for g in small_g] + [loss_blk], axis=0)
    (g8,) = all_gather([packed], "ag_small_grads", deps)
    tot = sum_devices(g8, "sum_small_grads")
    loss = tot[packed.shape[0] - 8, 0]
    full_g = []
    lo = 0
    for g in small_g:
        rows = g.size // 128
        full_g.append(tot[lo:lo + rows].reshape(g.shape))
        lo += rows

    def mine(g, width):
        return lax.dynamic_slice_in_dim(g, dev * width, width, axis=g.ndim - 1)

    fg = full_g
    small_gl = [fg[0], fg[1], fg[2], mine(fg[3], d // N_DEV), mine(fg[4], cl), mine(fg[5], cl), mine(fg[6], cl),
                mine(fg[7], cl), mine(fg[8], cl), mine(fg[9], d // N_DEV)]
    sd, sm, sv = adamw_small(small_w, small_gl, small_m, small_v, "adamw_small")

    def like(k, a):
        return a[None] if k in (4, 5) else a

    sg = [like(k, a) for k, a in enumerate(small_gl)]
    sd = [like(k, a) for k, a in enumerate(sd)]
    sm = [like(k, a) for k, a in enumerate(sm)]
    sv = [like(k, a) for k, a in enumerate(sv)]
    return sg, sd, sm, sv, loss
```

```python
import functools
import math

import jax
import jax.numpy as jnp
from jax import lax
from jax.experimental import pallas as pl
from jax.experimental.pallas import tpu as pltpu

F32 = jnp.float32
BF16 = jnp.bfloat16
EPS = 1e-6
HEAD_DIM = 128
POOL_WINDOWS = (2, 4, 8, 16)
SCONV_K = 3
CONF_K = 31
HALO = 32
N_DEV = 8
VMEM_LIMIT = 56 * 1024 * 1024
MESH = pl.DeviceIdType.MESH

ADAM_LR = 0.001
ADAM_B1 = 0.9
ADAM_B2 = 0.999
ADAM_EPS = 1e-08
ADAM_WD = 0.01
ADAM_STEP = 10


def _params(*sem):
    return pltpu.CompilerParams(dimension_semantics=sem, vmem_limit_bytes=VMEM_LIMIT)


def _sigmoid(v):
    return 1.0 / (1.0 + jnp.exp(-v))


def _silu(v):
    return v * _sigmoid(v)


def _silu_and_grad(v):
    s = _sigmoid(v)
    return v * s, s * (1.0 + v * (1.0 - s))


def _rowsum8(v):
    r, c = v.shape
    return jnp.sum(v.reshape(r // 8, 8, c), axis=0)


SUBLANES = 8


class _Taps:
    def __init__(self, xx, rows, before):
        self.xx, self.rows, self.before, self.rotated = xx, rows, before, {}

    def __call__(self, i):
        r, q = i % SUBLANES, i // SUBLANES
        if r not in self.rotated:
            n = self.xx.shape[0]
            self.rotated[r] = self.xx if r == 0 else pltpu.roll(self.xx, r if self.before else n - r, 0)
        lo = HALO - SUBLANES * q if self.before else SUBLANES * q
        return self.rotated[r][lo:lo + self.rows]


def _window_sum(xx, win, before):
    n = xx.shape[0]
    acc = xx
    k = 1
    while k < win:
        acc = acc + pltpu.roll(acc, k if before else n - k, 0)
        k *= 2
    return acc


def postnorm_fwd(x, o, g, g_next, name, tm=512, dep=None):
    s, d = x.shape
    dep_args, dep_specs = _after(dep)

    def body(x_ref, o_ref, g_ref, gn_ref, *rest):
        y_ref, h_ref = rest[-2:]
        ov = o_ref[...].astype(F32)
        r = lax.rsqrt(jnp.mean(ov * ov, axis=-1, keepdims=True) + EPS)
        y = x_ref[...] + ov * r * g_ref[...]
        y_ref[...] = y
        r2 = lax.rsqrt(jnp.mean(y * y, axis=-1, keepdims=True) + EPS)
        h_ref[...] = (y * r2 * gn_ref[...]).astype(BF16)

    row = pl.BlockSpec((tm, d), lambda i: (i, 0))
    vec = pl.BlockSpec((1, d), lambda i: (0, 0))
    return pl.pallas_call(
        body, name=name, grid=(s // tm,),
        in_specs=[row, row, vec, vec] + dep_specs, out_specs=[row, row],
        out_shape=[jax.ShapeDtypeStruct((s, d), F32), jax.ShapeDtypeStruct((s, d), BF16)],
        compiler_params=_params("parallel"),
    )(x, o, g, g_next, *dep_args)


def final_fwd_bwd(x1, o, g, target, name, tm=512):
    s, d = x1.shape
    n = s // tm

    def body(x_ref, o_ref, g_ref, t_ref, loss_ref, gx_ref, do_ref, dg_ref, lacc, gacc):
        i = pl.program_id(0)

        @pl.when(i == 0)
        def _():
            lacc[...] = jnp.zeros_like(lacc)
            gacc[...] = jnp.zeros_like(gacc)

        ov = o_ref[...].astype(F32)
        gv = g_ref[...]
        r = lax.rsqrt(jnp.mean(ov * ov, axis=-1, keepdims=True) + EPS)
        oh = ov * r
        diff = x_ref[...] + oh * gv - t_ref[...]
        lacc[...] += _rowsum8(diff * diff)
        gx = diff * (1.0 / d)
        gx_ref[...] = gx
        gacc[...] += _rowsum8(gx * oh)
        dn = gx * gv
        do_ref[...] = (r * (dn - oh * jnp.mean(dn * oh, axis=-1, keepdims=True))).astype(BF16)

        @pl.when(i == n - 1)
        def _():
            tot = jnp.sum(jnp.sum(lacc[...], axis=0, keepdims=True), axis=1, keepdims=True)
            loss_ref[...] = jnp.broadcast_to(tot * (0.5 / d), loss_ref.shape)
            dg_ref[...] = jnp.sum(gacc[...], axis=0, keepdims=True)

    row = pl.BlockSpec((tm, d), lambda i: (i, 0))
    vec = pl.BlockSpec((1, d), lambda i: (0, 0))
    return pl.pallas_call(
        body, name=name, grid=(n,),
        in_specs=[row, row, vec, row],
        out_specs=[pl.BlockSpec((8, 128), lambda i: (0, 0)), row, row, vec],
        out_shape=[jax.ShapeDtypeStruct((8, 128), F32), jax.ShapeDtypeStruct((s, d), F32),
                   jax.ShapeDtypeStruct((s, d), BF16), jax.ShapeDtypeStruct((1, d), F32)],
        scratch_shapes=[pltpu.VMEM((8, d), F32), pltpu.VMEM((8, d), F32)],
        compiler_params=_params("arbitrary"),
    )(x1, o, g, target)


def _rms_bwd_rows(dyv, xv, gv):
    r = lax.rsqrt(jnp.mean(xv * xv, axis=-1, keepdims=True) + EPS)
    xh = xv * r
    dn = dyv * gv
    return r * (dn - xh * jnp.mean(dn * xh, axis=-1, keepdims=True)), _rowsum8(dyv * xh)


def norm_bwd(dy, inp, g, resid, name, inp2=None, g2=None, tm=256, dep=None):
    s, d = inp.shape
    n = s // tm
    chain = inp2 is not None

    def body(*refs):
        dy_ref, x_ref, g_ref, r_ref = refs[:4]
        outs = refs[-6:] if chain else refs[-3:]
        i = pl.program_id(0)

        @pl.when(i == 0)
        def _():
            for acc in outs[-2:] if chain else outs[-1:]:
                acc[...] = jnp.zeros_like(acc)

        if chain:
            x2_ref, g2_ref = refs[4:6]
            dx_ref, dg_ref, dx2_ref, dg2_ref, gacc, gacc2 = outs
        else:
            dx_ref, dg_ref, gacc = outs
        dx, dg_rows = _rms_bwd_rows(dy_ref[...].astype(F32), x_ref[...], g_ref[...])
        dx = dx + r_ref[...]
        dx_ref[...] = dx
        gacc[...] += dg_rows
        if chain:
            dx2, dg2_rows = _rms_bwd_rows(dx, x2_ref[...].astype(F32), g2_ref[...])
            dx2_ref[...] = dx2.astype(BF16)
            gacc2[...] += dg2_rows

        @pl.when(i == n - 1)
        def _():
            dg_ref[...] = jnp.sum(gacc[...], axis=0, keepdims=True)
            if chain:
                dg2_ref[...] = jnp.sum(gacc2[...], axis=0, keepdims=True)

    row = pl.BlockSpec((tm, d), lambda i: (i, 0))
    vec = pl.BlockSpec((1, d), lambda i: (0, 0))
    dep_args, dep_specs = _after(dep)
    extra = [inp2, g2] if chain else []
    return pl.pallas_call(
        body, name=name, grid=(n,),
        in_specs=[row, row, vec, row] + ([row, vec] if chain else []) + dep_specs,
        out_specs=[row, vec] * (2 if chain else 1),
        out_shape=[jax.ShapeDtypeStruct((s, d), F32), jax.ShapeDtypeStruct((1, d), F32)]
        + ([jax.ShapeDtypeStruct((s, d), BF16), jax.ShapeDtypeStruct((1, d), F32)] if chain else []),
        scratch_shapes=[pltpu.VMEM((8, d), F32)] * (2 if chain else 1),
        compiler_params=_params("arbitrary"),
    )(dy, inp, g, resid, *extra, *dep_args)


def _after(dep):
    if dep is None:
        return [], []
    return [dep], [pl.BlockSpec((8, 128), lambda *_: (0, 0))]


def _lane_concat(ref, count):
    return ref[0] if count == 1 else jnp.concatenate([ref[i] for i in range(count)], axis=1)


def mm_nn(a, w, out_dtype, name, tm=2048, tn=None, dep=None, group=1):
    m, k = a.shape
    tm = min(tm, m)
    ns, _, n = w.shape
    tn = n if tn is None else tn
    nj = n // tn
    assert group == 1 or nj == 1
    dep_args, dep_specs = _after(dep)

    def body(a_ref, w_ref, *rest):
        o_ref = rest[-1]
        o_ref[...] = jnp.dot(a_ref[...], _lane_concat(w_ref, group), preferred_element_type=F32).astype(out_dtype)

    return pl.pallas_call(
        body, name=name, grid=(ns // group, nj, m // tm),
        in_specs=[pl.BlockSpec((tm, k), lambda s, j, i: (i, 0)),
                  pl.BlockSpec((group, k, tn), lambda s, j, i: (s, 0, j))] + dep_specs,
        out_specs=pl.BlockSpec((tm, group * tn), lambda s, j, i: (i, s * nj + j)),
        out_shape=jax.ShapeDtypeStruct((m, ns * n), out_dtype),
        compiler_params=_params("parallel", "parallel", "parallel"),
    )(a, w, *dep_args)


def mm_nt(a, w, out_dtype, name, tm=1024, tn=None, dep=None, group=1):
    m = a.shape[0]
    tm = min(tm, m)
    ns, k, n = w.shape
    tn = n if tn is None else tn
    nj = n // tn
    assert group == 1 or nj == 1
    steps = ns * nj // group
    dep_args, dep_specs = _after(dep)

    def body(a_ref, w_ref, *rest):
        o_ref, acc = rest[-2:]
        r = pl.program_id(1)

        @pl.when(r == 0)
        def _():
            acc[...] = jnp.zeros_like(acc)

        acc[...] += lax.dot_general(a_ref[...], _lane_concat(w_ref, group), (((1,), (1,)), ((), ())),
                                    preferred_element_type=F32)

        @pl.when(r == steps - 1)
        def _():
            o_ref[...] = acc[...].astype(out_dtype)

    return pl.pallas_call(
        body, name=name, grid=(m // tm, steps),
        in_specs=[pl.BlockSpec((tm, group * tn), lambda i, r: (i, r)),
                  pl.BlockSpec((group, k, tn), lambda i, r: (r // nj, 0, r % nj))] + dep_specs,
        out_specs=pl.BlockSpec((tm, k), lambda i, r: (i, 0)),
        out_shape=jax.ShapeDtypeStruct((m, k), out_dtype),
        scratch_shapes=[pltpu.VMEM((tm, k), F32)],
        compiler_params=_params("parallel", "arbitrary"),
    )(a, w, *dep_args)


def mm_tn(a, b, ns, out_dtype, name, tk=1024, tm=2048, dep=None, pick=None, group=1):
    m, k = a.shape
    tm = min(tm, m)
    step, offset = (1, None) if pick is None else pick
    assert group == 1 or pick is None
    n = b.shape[1] // (ns * step)
    steps = m // tm
    dep_args, dep_specs = _after(dep)
    n_pre = 0 if pick is None else 1

    def b_block(s, j, r, *pre):
        return (r, s if pick is None else step * s + pre[0][0])

    def body(*refs):
        a_ref, b_ref = refs[n_pre:n_pre + 2]
        o_ref, acc = refs[-2:]
        r = pl.program_id(2)

        @pl.when(r == 0)
        def _():
            acc[...] = jnp.zeros_like(acc)

        acc[...] += lax.dot_general(a_ref[...], b_ref[...], (((0,), (0,)), ((), ())),
                                    preferred_element_type=F32)

        @pl.when(r == steps - 1)
        def _():
            for i in range(group):
                o_ref[i] = acc[:, i * n:(i + 1) * n].astype(out_dtype)

    return pl.pallas_call(
        body, name=name,
        grid_spec=pltpu.PrefetchScalarGridSpec(
            num_scalar_prefetch=n_pre, grid=(ns // group, k // tk, steps),
            in_specs=[pl.BlockSpec((tm, tk), lambda s, j, r, *pre: (r, j)),
                      pl.BlockSpec((tm, group * n), b_block)] + dep_specs,
            out_specs=pl.BlockSpec((group, tk, n), lambda s, j, r, *pre: (s, j, 0)),
            scratch_shapes=[pltpu.VMEM((tk, group * n), F32)]),
        out_shape=jax.ShapeDtypeStruct((ns, k, n), out_dtype),
        compiler_params=_params("parallel", "parallel", "arbitrary"),
    )(*([] if pick is None else [offset]), a, b, *dep_args)


SB_BLK = 128


LOG2E = 1.0 / math.log(2.0)


def _split_dot(v, tri2):
    hi = pltpu.bitcast(pltpu.bitcast(v, jnp.uint32) & jnp.uint32(0xFFFF0000), F32)
    lo = (v - hi).astype(BF16)
    return jnp.dot(jnp.concatenate([hi.astype(BF16), lo], axis=1), tri2, preferred_element_type=F32)


def _sb_scores(z2, lim, dcol, tri_ex, masked):
    sp = jnp.log2(1.0 + jnp.exp2(-jnp.abs(z2)))
    lb = jnp.minimum(z2, 0.0) - sp
    l1m = lb - z2
    mask = None
    if masked:
        mask = dcol < lim
        l1m = jnp.where(mask, l1m, 0.0)
    return mask, lb, l1m, _split_dot(l1m, tri_ex)


def _sb_consts():
    row = lax.broadcasted_iota(jnp.int32, (SB_BLK, SB_BLK), 0)
    col = lax.broadcasted_iota(jnp.int32, (SB_BLK, SB_BLK), 1)
    tri_ex = jnp.where(row > col, 1.0, 0.0).astype(BF16)
    tri_in = jnp.where(row >= col, 1.0, 0.0).astype(BF16)
    return col - row, jnp.concatenate([tri_ex, tri_ex], axis=0), jnp.concatenate([tri_in, tri_in], axis=0)


def sb_fwd(p, n_heads, name, tq=1024, nsub=8, dep=None):
    s = p.shape[0]
    h_n = n_heads
    b = SB_BLK
    nqs = tq // b
    tk = nsub * b
    scale = 1.0 / math.sqrt(HEAD_DIM)

    dep_args, dep_specs = _after(dep)

    def body(q_ref, k_ref, v_ref, *rest):
        o_ref, w_ref = rest[-2:]
        qi = pl.program_id(1)
        dcol, tri_ex, _ = _sb_consts()
        qv = [q_ref[qs * b:(qs + 1) * b, :] for qs in range(nqs)]
        n_groups = ((qi + 1) * nqs - 1) // nsub + 1

        def step(it, carry, masked):
            c1s, accs = carry
            g = n_groups - 1 - it
            off = pl.multiple_of(g * tk, tk)
            kg = k_ref[pl.ds(off, tk), :]
            vg = v_ref[pl.ds(off, tk), :]
            new_c1, new_acc = [], []
            for qs in range(nqs):
                qb = qi * nqs + qs
                square = masked and nqs == nsub
                nk = qs + 1 if square else nsub
                kq, vq = kg[:nk * b], vg[:nk * b]
                z2 = lax.dot_general(qv[qs], kq, (((1,), (1,)), ((), ())),
                                     preferred_element_type=F32) * (scale * LOG2E)
                blocks = [_sb_scores(z2[:, j * b:(j + 1) * b], (qb - (g * nsub + j)) * b, dcol, tri_ex,
                                     masked and (j == qs or not square)) for j in range(nk)]
                run = c1s[qs]
                ws = [None] * nk
                for j in reversed(range(nk)):
                    mask, lb, l1m, ls_loc = blocks[j]
                    wj = jnp.exp2(lb + ls_loc + run)
                    ws[j] = (wj if mask is None else jnp.where(mask, wj, 0.0)).astype(BF16)
                    run = run + jnp.sum(l1m, axis=1, keepdims=True)
                w = jnp.concatenate(ws, axis=1)
                w_ref[0, g, qs * b:(qs + 1) * b, 0:nk * b] = w
                new_acc.append(accs[qs] + jnp.dot(w, vq, preferred_element_type=F32))
                new_c1.append(run)
            return tuple(new_c1), tuple(new_acc)

        init = (tuple(jnp.zeros((b, 1), F32) for _ in range(nqs)),
                tuple(jnp.zeros((b, HEAD_DIM), F32) for _ in range(nqs)))
        assert all(((i + 1) * nqs - 1) // nsub * nsub <= i * nqs for i in range(s // tq))
        first = step(0, init, True)
        _, accs = lax.fori_loop(1, n_groups, functools.partial(step, masked=False), first)
        for qs in range(nqs):
            o_ref[qs * b:(qs + 1) * b, :] = accs[qs]

    return pl.pallas_call(
        body, name=name, grid=(h_n, s // tq),
        in_specs=[pl.BlockSpec((tq, HEAD_DIM), lambda h, i: (i, h)),
                  pl.BlockSpec((s, HEAD_DIM), lambda h, i: (0, h_n + h)),
                  pl.BlockSpec((s, HEAD_DIM), lambda h, i: (0, 2 * h_n + h))] + dep_specs,
        out_specs=[pl.BlockSpec((tq, HEAD_DIM), lambda h, i: (i, h)),
                   pl.BlockSpec((1, s // tk, tq, tk), lambda h, i: (h, 0, i, 0))],
        out_shape=[jax.ShapeDtypeStruct((s, h_n * HEAD_DIM), F32),
                   jax.ShapeDtypeStruct((h_n, s // tk, s, tk), BF16)],
        compiler_params=_params("parallel", "arbitrary"),
    )(p, p, p, *dep_args)


def sb_bwd(p, a, wts, da, n_heads, name, tq=1024, dep=None):
    s = p.shape[0]
    h_n = n_heads
    nq = s // tq
    b = SB_BLK
    nqs = tq // b
    tk = wts.shape[3]
    nsub = tk // b
    scale = 1.0 / math.sqrt(HEAD_DIM)
    dep_args, dep_specs = _after(dep)

    def body(q_ref, k_ref, v_ref, a_ref, da_ref, w_ref, *rest):
        dq_ref, dk_ref, dv_ref, dk_acc, dv_acc = rest[-5:]
        qi = pl.program_id(1)

        @pl.when(qi == 0)
        def _():
            dk_acc[...] = jnp.zeros_like(dk_acc)
            dv_acc[...] = jnp.zeros_like(dv_acc)

        dcol, _, tri_in = _sb_consts()
        q_all = q_ref[...]
        do_all = da_ref[...]
        qv = [q_ref[qs * b:(qs + 1) * b, :] for qs in range(nqs)]
        dov = [da_ref[qs * b:(qs + 1) * b, :] for qs in range(nqs)]
        tots = [jnp.sum(dov[qs].astype(F32) * a_ref[qs * b:(qs + 1) * b, :], axis=1, keepdims=True)
                for qs in range(nqs)]
        n_groups = ((qi + 1) * nqs - 1) // nsub + 1

        def step(it, carry, masked):
            c2s, dqs = carry
            g = n_groups - 1 - it
            off = pl.multiple_of(g * tk, tk)
            kg = k_ref[pl.ds(off, tk), :]
            vg = v_ref[pl.ds(off, tk), :]
            square = masked and nqs == nsub
            new_c2, new_dq, dz_rows, w_rows = [], [], [], []
            for qs in range(nqs):
                qb = qi * nqs + qs
                nk = qs + 1 if square else nsub
                kq, vq = kg[:nk * b], vg[:nk * b]
                z2 = lax.dot_general(qv[qs], kq, (((1,), (1,)), ((), ())),
                                     preferred_element_type=F32) * (-scale * LOG2E)
                dw = lax.dot_general(dov[qs], vq, (((1,), (1,)), ((), ())), preferred_element_type=F32)
                beta = 1.0 / (1.0 + jnp.exp2(z2))
                wq = w_ref[0, g, qs * b:(qs + 1) * b, 0:nk * b]
                e = dw * wq.astype(F32)
                run2 = c2s[qs]
                dzs = [None] * nk
                for j in reversed(range(nk)):
                    cols = slice(j * b, (j + 1) * b)
                    later = _split_dot(e[:, cols], tri_in) + run2
                    bj = beta[:, cols]
                    dz = (e[:, cols] * (1.0 - bj) - bj * (tots[qs] - later)) * scale
                    if masked and (j == qs or not square):
                        dz = jnp.where(dcol < (qb - (g * nsub + j)) * b, dz, 0.0)
                    dzs[j] = dz.astype(BF16)
                    run2 = run2 + jnp.sum(e[:, cols], axis=1, keepdims=True)
                dzq = jnp.concatenate(dzs, axis=1)
                new_dq.append(dqs[qs] + jnp.dot(dzq, kq, preferred_element_type=F32))
                new_c2.append(run2)
                pad = [jnp.zeros((b, (nsub - nk) * b), BF16)] if nk < nsub else []
                dz_rows.append(jnp.concatenate([dzq] + pad, axis=1))
                w_rows.append(jnp.concatenate([wq] + pad, axis=1))
            dz_all = jnp.concatenate(dz_rows, axis=0)
            w_all = jnp.concatenate(w_rows, axis=0)
            dk_acc[pl.ds(off, tk), :] += lax.dot_general(dz_all, q_all, (((0,), (0,)), ((), ())),
                                                         preferred_element_type=F32)
            dv_acc[pl.ds(off, tk), :] += lax.dot_general(w_all, do_all, (((0,), (0,)), ((), ())),
                                                         preferred_element_type=F32)
            return tuple(new_c2), tuple(new_dq)

        zeros = tuple(jnp.zeros((b, 1), F32) for _ in range(nqs))
        assert all(((i + 1) * nqs - 1) // nsub * nsub <= i * nqs for i in range(s // tq))
        first = step(0, (zeros, tuple(jnp.zeros((b, HEAD_DIM), F32) for _ in range(nqs))), True)
        _, dqs = lax.fori_loop(1, n_groups, functools.partial(step, masked=False), first)
        for qs in range(nqs):
            dq_ref[qs * b:(qs + 1) * b, :] = dqs[qs].astype(BF16)

        @pl.when(qi == nq - 1)
        def _():
            dk_ref[...] = dk_acc[...].astype(BF16)
            dv_ref[...] = dv_acc[...].astype(BF16)

    blk = pl.BlockSpec((tq, HEAD_DIM), lambda h, i: (i, h))
    full = pl.BlockSpec((s, HEAD_DIM), lambda h, i: (0, h))
    return pl.pallas_call(
        body, name=name, grid=(h_n, nq),
        in_specs=[blk, pl.BlockSpec((s, HEAD_DIM), lambda h, i: (0, h_n + h)),
                  pl.BlockSpec((s, HEAD_DIM), lambda h, i: (0, 2 * h_n + h)), blk, blk,
                  pl.BlockSpec((1, s // tk, tq, tk), lambda h, i: (h, 0, i, 0))] + dep_specs,
        out_specs=[blk, full, full],
        out_shape=[jax.ShapeDtypeStruct((s, h_n * HEAD_DIM), BF16)] * 3,
        scratch_shapes=[pltpu.VMEM((s, HEAD_DIM), F32), pltpu.VMEM((s, HEAD_DIM), F32)],
        compiler_params=_params("parallel", "arbitrary"),
    )(p, p, p, a, da, wts, *dep_args)


def _pool_window(xx, win, r0, rc):
    cur = xx[HALO:HALO + rc]
    ws = _window_sum(xx, win, True)[HALO:HALO + rc]
    t_idx = r0 + lax.broadcasted_iota(jnp.int32, (rc, 1), 0)
    inv = 1.0 / jnp.minimum(win, t_idx + 1).astype(F32)
    return ws * inv - cur, inv


def even_mix_fwd(a, p, pool_w, pool_scale, name, rc=512, dep=None):
    s = p.shape[0]
    ng = len(POOL_WINDOWS)
    cw = pool_w.shape[1]
    n_chunks = s // rc
    dep_args, dep_specs = _after(dep)

    def body(a_ref, u_ref, g_ref, w_ref, sc_ref, *rest):
        y_ref, upad = rest[-2:]
        j = pl.program_id(0)

        @pl.when(j < ng)
        def _():
            def chunk(ci, carry):
                rows = pl.ds(pl.multiple_of(ci * rc, rc), rc)
                y_ref[rows, :] = (a_ref[rows, :] * _silu(g_ref[rows, :].astype(F32))).astype(BF16)
                return carry

            lax.fori_loop(0, n_chunks, chunk, 0)

        for gi, win in enumerate(POOL_WINDOWS):
            @pl.when(j == ng + gi)
            def _(win=win):
                upad[0:HALO, :] = jnp.zeros((HALO, cw), F32)

                def fill(ci, carry):
                    r0 = pl.multiple_of(ci * rc, rc)
                    upad[pl.ds(pl.multiple_of(r0 + HALO, HALO), rc), :] = u_ref[pl.ds(r0, rc), :].astype(F32)
                    return carry

                lax.fori_loop(0, n_chunks, fill, 0)

                def chunk(ci, carry):
                    r0 = pl.multiple_of(ci * rc, rc)
                    rows = pl.ds(r0, rc)
                    pooled, _ = _pool_window(upad[pl.ds(r0, HALO + rc), :], win, r0, rc)
                    t = jnp.dot(pooled.astype(BF16), w_ref[0], preferred_element_type=F32)
                    y_ref[rows, :] = (t * sc_ref[...] * _silu(g_ref[rows, :].astype(F32))).astype(BF16)
                    return carry

                lax.fori_loop(0, n_chunks, chunk, 0)

    grp = lambda j: jnp.maximum(j - ng, 0)
    return pl.pallas_call(
        body, name=name, grid=(2 * ng,),
        in_specs=[pl.BlockSpec((s, cw), lambda j: (0, jnp.minimum(j, ng - 1))),
                  pl.BlockSpec((s, cw), lambda j: (0, 3 * ng + grp(j))),
                  pl.BlockSpec((s, cw), lambda j: (0, 4 * ng + j)),
                  pl.BlockSpec((1, cw, cw), lambda j: (grp(j), 0, 0)),
                  pl.BlockSpec((1, cw), lambda j: (0, grp(j)))] + dep_specs,
        out_specs=pl.BlockSpec((s, cw), lambda j: (0, j)),
        out_shape=jax.ShapeDtypeStruct((s, 2 * ng * cw), BF16),
        scratch_shapes=[pltpu.VMEM((HALO + s, cw), F32)],
        compiler_params=_params("arbitrary"),
    )(a, p, p, pool_w, pool_scale, *dep_args)


def even_mix_bwd(dy, a, p, pool_w, pool_scale, name, rc=512):
    s = p.shape[0]
    ng = len(POOL_WINDOWS)
    cw = pool_w.shape[1]
    n_chunks = s // rc

    def body(dy_ref, a_ref, u_ref, g_ref, w_ref, sc_ref, da_ref, du_ref, dg_ref, dw_ref, dsc_ref,
             upad, rpad, dpl, dw_acc, dsc_acc):
        j = pl.program_id(0)

        @pl.when(j < ng)
        def _():
            def chunk(ci, carry):
                rows = pl.ds(pl.multiple_of(ci * rc, rc), rc)
                dyv = dy_ref[rows, :].astype(F32)
                sg, dsg = _silu_and_grad(g_ref[rows, :].astype(F32))
                da_ref[rows, :] = (dyv * sg).astype(BF16)
                dg_ref[rows, :] = (dyv * a_ref[rows, :] * dsg).astype(BF16)
                return carry

            lax.fori_loop(0, n_chunks, chunk, 0)

        for gi, win in enumerate(POOL_WINDOWS):
            @pl.when(j == ng + gi)
            def _(win=win):
                upad[0:HALO, :] = jnp.zeros((HALO, cw), F32)
                rpad[s:s + HALO, :] = jnp.zeros((HALO, cw), F32)
                dw_acc[...] = jnp.zeros_like(dw_acc)
                dsc_acc[...] = jnp.zeros_like(dsc_acc)

                def fill(ci, carry):
                    r0 = pl.multiple_of(ci * rc, rc)
                    upad[pl.ds(pl.multiple_of(r0 + HALO, HALO), rc), :] = u_ref[pl.ds(r0, rc), :].astype(F32)
                    return carry

                lax.fori_loop(0, n_chunks, fill, 0)

                def chunk(ci, carry):
                    r0 = pl.multiple_of(ci * rc, rc)
                    rows = pl.ds(r0, rc)
                    pooled, inv = _pool_window(upad[pl.ds(r0, HALO + rc), :], win, r0, rc)
                    pb = pooled.astype(BF16)
                    wv = w_ref[0]
                    t = jnp.dot(pb, wv, preferred_element_type=F32)
                    scv = sc_ref[...]
                    dyv = dy_ref[rows, :].astype(F32)
                    sg, dsg = _silu_and_grad(g_ref[rows, :].astype(F32))
                    dpo = dyv * sg
                    dg_ref[rows, :] = (dyv * t * scv * dsg).astype(BF16)
                    dsc_acc[...] += _rowsum8(dpo * t)
                    dtb = (dpo * scv).astype(BF16)
                    dw_acc[...] += lax.dot_general(pb, dtb, (((0,), (0,)), ((), ())),
                                                   preferred_element_type=F32)
                    dpooled = lax.dot_general(dtb, wv, (((1,), (1,)), ((), ())),
                                              preferred_element_type=F32)
                    dpl[rows, :] = dpooled
                    rpad[rows, :] = dpooled * inv
                    return carry

                lax.fori_loop(0, n_chunks, chunk, 0)

                def chunk2(ci, carry):
                    r0 = pl.multiple_of(ci * rc, rc)
                    rows = pl.ds(r0, rc)
                    xx = rpad[pl.ds(r0, rc + HALO), :]
                    fs = _window_sum(xx, win, False)[0:rc]
                    du_ref[rows, :] = (fs - dpl[rows, :]).astype(BF16)
                    return carry

                lax.fori_loop(0, n_chunks, chunk2, 0)
                dw_ref[0] = dw_acc[...]
                dsc_ref[...] = jnp.sum(dsc_acc[...], axis=0, keepdims=True)

    grp = lambda j: jnp.maximum(j - ng, 0)
    att = lambda j: jnp.minimum(j, ng - 1)
    return pl.pallas_call(
        body, name=name, grid=(2 * ng,),
        in_specs=[pl.BlockSpec((s, cw), lambda j: (0, j)),
                  pl.BlockSpec((s, cw), lambda j: (0, att(j))),
                  pl.BlockSpec((s, cw), lambda j: (0, 3 * ng + grp(j))),
                  pl.BlockSpec((s, cw), lambda j: (0, 4 * ng + j)),
                  pl.BlockSpec((1, cw, cw), lambda j: (grp(j), 0, 0)),
                  pl.BlockSpec((1, cw), lambda j: (0, grp(j)))],
        out_specs=[pl.BlockSpec((s, cw), lambda j: (0, att(j))),
                   pl.BlockSpec((s, cw), lambda j: (0, grp(j))),
                   pl.BlockSpec((s, cw), lambda j: (0, j)),
                   pl.BlockSpec((1, cw, cw), lambda j: (grp(j), 0, 0)),
                   pl.BlockSpec((1, cw), lambda j: (0, grp(j)))],
        out_shape=[jax.ShapeDtypeStruct((s, ng * cw), BF16), jax.ShapeDtypeStruct((s, ng * cw), BF16),
                   jax.ShapeDtypeStruct((s, 2 * ng * cw), BF16),
                   jax.ShapeDtypeStruct((ng, cw, cw), F32), jax.ShapeDtypeStruct((1, ng * cw), F32)],
        scratch_shapes=[pltpu.VMEM((HALO + s, cw), F32), pltpu.VMEM((s + HALO, cw), F32),
                        pltpu.VMEM((s, cw), F32), pltpu.VMEM((cw, cw), F32), pltpu.VMEM((8, cw), F32)],
        compiler_params=_params("arbitrary"),
    )(dy, a, p, p, pool_w, pool_scale)


def _halo_before(tm):
    return lambda i: jnp.maximum(i * (tm // HALO) - 1, 0)


def _halo_after(tm, s):
    return lambda i: jnp.minimum((i + 1) * (tm // HALO), s // HALO - 1)


def odd_mix_fwd(p, sconv_w, dconv_w, dconv_b, cnorm_g, cnorm_b, name, tm=128, dep=None):
    s = p.shape[0]
    cw = sconv_w.shape[1]
    n = s // tm
    lanes = 128
    hb = _halo_before(tm)

    dep_args, dep_specs = _after(dep)

    def body(hc_ref, hch_ref, bc_ref, cc_ref, cch_ref, ga_ref, gah_ref, gb_ref, gbh_ref, g1_ref, g2_ref,
             sw_ref, dw_ref, db_ref, gam_ref, bet_ref, *rest):
        y_ref, dc_ref = rest[-2:]
        first = pl.program_id(0) == 0
        for l in range(cw // lanes):
            cols = slice(l * lanes, (l + 1) * lanes)
            mh = jnp.where(first, 0.0, cch_ref[:, cols].astype(F32) * hch_ref[:, cols].astype(F32))
            mm = cc_ref[:, cols].astype(F32) * hc_ref[:, cols].astype(F32)
            xx = jnp.concatenate([mh, mm], axis=0)
            tap = _Taps(xx, tm, True)
            cv = jnp.zeros((tm, lanes), F32)
            for k in range(SCONV_K):
                cv = cv + sw_ref[k:k + 1, cols] * tap(SCONV_K - 1 - k)
            c_out = bc_ref[:, cols].astype(F32) * cv
            y_ref[:, cols] = (c_out * _silu(g1_ref[:, cols].astype(F32))).astype(BF16)
            dh = jnp.where(first, 0.0, gah_ref[:, cols].astype(F32) * _sigmoid(gbh_ref[:, cols].astype(F32)))
            dm = ga_ref[:, cols].astype(F32) * _sigmoid(gb_ref[:, cols].astype(F32))
            xx = jnp.concatenate([dh, dm], axis=0)
            tap = _Taps(xx, tm, True)
            acc = jnp.zeros((tm, lanes), F32) + db_ref[:, cols]
            for k in range(CONF_K):
                acc = acc + dw_ref[k:k + 1, cols] * tap(CONF_K - 1 - k)
            dc_ref[:, cols] = acc
        rs = 64
        for r in range(tm // rs):
            rows = slice(r * rs, (r + 1) * rs)
            xv = dc_ref[rows, :]
            mu = jnp.mean(xv, axis=-1, keepdims=True)
            xc = xv - mu
            rstd = lax.rsqrt(jnp.mean(xc * xc, axis=-1, keepdims=True) + EPS)
            ln = xc * rstd * gam_ref[...] + bet_ref[...]
            y_ref[rows, cw:2 * cw] = (_silu(ln) * _silu(g2_ref[rows, :].astype(F32))).astype(BF16)

    main = lambda c: pl.BlockSpec((tm, cw), lambda i: (i, c))
    halo = lambda c: pl.BlockSpec((HALO, cw), lambda i: (hb(i), c))
    vec = lambda r: pl.BlockSpec((r, cw), lambda i: (0, 0))
    return pl.pallas_call(
        body, name=name, grid=(n,),
        in_specs=[main(0), halo(0), main(1), main(2), halo(2), main(3), halo(3), main(4), halo(4),
                  main(5), main(6), vec(SCONV_K), vec(CONF_K), vec(1), vec(1), vec(1)] + dep_specs,
        out_specs=[pl.BlockSpec((tm, 2 * cw), lambda i: (i, 0)), pl.BlockSpec((tm, cw), lambda i: (i, 0))],
        out_shape=[jax.ShapeDtypeStruct((s, 2 * cw), BF16), jax.ShapeDtypeStruct((s, cw), F32)],
        compiler_params=_params("parallel"),
    )(p, p, p, p, p, p, p, p, p, p, p, sconv_w, dconv_w, dconv_b, cnorm_g, cnorm_b, *dep_args)


def odd_bwd_ln(dy, p, dc, cnorm_g, cnorm_b, name, tm=256):
    s = p.shape[0]
    cw = dc.shape[1]
    n = s // tm
    rs = 128

    def body(dy_ref, g2_ref, dc_ref, gam_ref, bet_ref, ddc_ref, dg_ref, dgam_ref, dbet_ref, gacc, bacc):
        i = pl.program_id(0)

        @pl.when(i == 0)
        def _():
            gacc[...] = jnp.zeros_like(gacc)
            bacc[...] = jnp.zeros_like(bacc)

        def chunk(ci, carry):
            rows = pl.ds(pl.multiple_of(ci * rs, rs), rs)
            xv = dc_ref[rows, :]
            mu = jnp.mean(xv, axis=-1, keepdims=True)
            xc = xv - mu
            rstd = lax.rsqrt(jnp.mean(xc * xc, axis=-1, keepdims=True) + EPS)
            xh = xc * rstd
            gam = gam_ref[...]
            sl, dsl = _silu_and_grad(xh * gam + bet_ref[...])
            sg, dsg = _silu_and_grad(g2_ref[rows, :].astype(F32))
            dyv = dy_ref[rows, :].astype(F32)
            dg_ref[rows, :] = (dyv * sl * dsg).astype(BF16)
            dln = dyv * sg * dsl
            gacc[...] += _rowsum8(dln * xh)
            bacc[...] += _rowsum8(dln)
            dxh = dln * gam
            ddc_ref[rows, :] = rstd * (dxh - jnp.mean(dxh, axis=-1, keepdims=True)
                                       - xh * jnp.mean(dxh * xh, axis=-1, keepdims=True))
            return carry

        lax.fori_loop(0, tm // rs, chunk, 0)

        @pl.when(i == n - 1)
        def _():
            dgam_ref[...] = jnp.sum(gacc[...], axis=0, keepdims=True)
            dbet_ref[...] = jnp.sum(bacc[...], axis=0, keepdims=True)

    vec = pl.BlockSpec((1, cw), lambda i: (0, 0))
    return pl.pallas_call(
        body, name=name, grid=(n,),
        in_specs=[pl.BlockSpec((tm, cw), lambda i: (i, 1)), pl.BlockSpec((tm, cw), lambda i: (i, 6)),
                  pl.BlockSpec((tm, cw), lambda i: (i, 0)), vec, vec],
        out_specs=[pl.BlockSpec((tm, cw), lambda i: (i, 0)), pl.BlockSpec((tm, cw), lambda i: (i, 0)), vec, vec],
        out_shape=[jax.ShapeDtypeStruct((s, cw), F32), jax.ShapeDtypeStruct((s, cw), BF16),
                   jax.ShapeDtypeStruct((1, cw), F32), jax.ShapeDtypeStruct((1, cw), F32)],
        scratch_shapes=[pltpu.VMEM((8, cw), F32), pltpu.VMEM((8, cw), F32)],
        compiler_params=_params("arbitrary"),
    )(dy, p, dc, cnorm_g, cnorm_b)


def odd_bwd_conv(dy, p, ddc, dg2, sconv_w, dconv_w, name, tm=128):
    s = p.shape[0]
    cw = ddc.shape[1]
    n = s // tm
    lanes = 128
    hb = _halo_before(tm)
    ha = _halo_after(tm, s)

    def body(dy_ref, dya_ref, g1_ref, g1a_ref, bc_ref, bca_ref, hc_ref, hch_ref, cc_ref, cch_ref,
             ddc_ref, ddca_ref, ga_ref, gah_ref, gb_ref, gbh_ref, dg2_ref, sw_ref, dw_ref,
             dp_ref, dsw_ref, ddw_ref, ddb_ref, sw_acc, dw_acc, db_acc):
        i = pl.program_id(0)
        first = i == 0
        last = i == n - 1

        @pl.when(first)
        def _():
            sw_acc[...] = jnp.zeros_like(sw_acc)
            dw_acc[...] = jnp.zeros_like(dw_acc)
            db_acc[...] = jnp.zeros_like(db_acc)

        for l in range(cw // lanes):
            cols = slice(l * lanes, (l + 1) * lanes)
            mh = jnp.where(first, 0.0, cch_ref[:, cols].astype(F32) * hch_ref[:, cols].astype(F32))
            hcv = hc_ref[:, cols].astype(F32)
            ccv = cc_ref[:, cols].astype(F32)
            xx = jnp.concatenate([mh, ccv * hcv], axis=0)
            tap = _Taps(xx, tm, True)
            taps = [tap(SCONV_K - 1 - k) for k in range(SCONV_K)]
            cv = jnp.zeros((tm, lanes), F32)
            for k in range(SCONV_K):
                cv = cv + sw_ref[k:k + 1, cols] * taps[k]
            bcv = bc_ref[:, cols].astype(F32)
            dyv = dy_ref[:, cols].astype(F32)
            sg, dsg = _silu_and_grad(g1_ref[:, cols].astype(F32))
            dco = dyv * sg
            dp_ref[:, 5 * cw + l * lanes:5 * cw + (l + 1) * lanes] = (dyv * bcv * cv * dsg).astype(BF16)
            dp_ref[:, cw + l * lanes:cw + (l + 1) * lanes] = (dco * cv).astype(BF16)
            dcv = dco * bcv
            for k in range(SCONV_K):
                sw_acc[k * 8:(k + 1) * 8, cols] += _rowsum8(dcv * taps[k])
            dcv_a = jnp.where(last, 0.0, dya_ref[:, cols].astype(F32) * _silu(g1a_ref[:, cols].astype(F32))
                              * bca_ref[:, cols].astype(F32))
            xx = jnp.concatenate([dcv, dcv_a], axis=0)
            tap = _Taps(xx, tm, False)
            dm = jnp.zeros((tm, lanes), F32)
            for k in range(SCONV_K):
                dm = dm + sw_ref[k:k + 1, cols] * tap(SCONV_K - 1 - k)
            dp_ref[:, l * lanes:(l + 1) * lanes] = (dm * ccv).astype(BF16)
            dp_ref[:, 2 * cw + l * lanes:2 * cw + (l + 1) * lanes] = (dm * hcv).astype(BF16)
            gav = ga_ref[:, cols].astype(F32)
            sb = _sigmoid(gb_ref[:, cols].astype(F32))
            dh = jnp.where(first, 0.0, gah_ref[:, cols].astype(F32) * _sigmoid(gbh_ref[:, cols].astype(F32)))
            xx = jnp.concatenate([dh, gav * sb], axis=0)
            ddcv = ddc_ref[:, cols]
            db_acc[:, cols] += _rowsum8(ddcv)
            tap = _Taps(xx, tm, True)
            for k in range(CONF_K):
                dw_acc[k * 8:(k + 1) * 8, cols] += _rowsum8(ddcv * tap(CONF_K - 1 - k))
            ddc_a = jnp.where(last, 0.0, ddca_ref[:, cols])
            xx = jnp.concatenate([ddcv, ddc_a], axis=0)
            tap = _Taps(xx, tm, False)
            dgl = jnp.zeros((tm, lanes), F32)
            for k in range(CONF_K):
                dgl = dgl + dw_ref[k:k + 1, cols] * tap(CONF_K - 1 - k)
            dp_ref[:, 3 * cw + l * lanes:3 * cw + (l + 1) * lanes] = (dgl * sb).astype(BF16)
            dp_ref[:, 4 * cw + l * lanes:4 * cw + (l + 1) * lanes] = (dgl * gav * sb * (1.0 - sb)).astype(BF16)
        dp_ref[:, 6 * cw:7 * cw] = dg2_ref[...]

        @pl.when(last)
        def _():
            for k in range(SCONV_K):
                dsw_ref[k:k + 1, :] = jnp.sum(sw_acc[k * 8:(k + 1) * 8, :], axis=0, keepdims=True)
            for k in range(CONF_K):
                ddw_ref[k:k + 1, :] = jnp.sum(dw_acc[k * 8:(k + 1) * 8, :], axis=0, keepdims=True)
            ddb_ref[...] = jnp.sum(db_acc[...], axis=0, keepdims=True)

    def main(c):
        return pl.BlockSpec((tm, cw), lambda i: (i, c))

    def before(c):
        return pl.BlockSpec((HALO, cw), lambda i: (hb(i), c))

    def after(c):
        return pl.BlockSpec((HALO, cw), lambda i: (ha(i), c))

    def vec(r):
        return pl.BlockSpec((r, cw), lambda i: (0, 0))

    return pl.pallas_call(
        body, name=name, grid=(n,),
        in_specs=[main(0), after(0), main(5), after(5), main(1), after(1), main(0), before(0), main(2), before(2),
                  main(0), after(0), main(3), before(3), main(4), before(4), main(0), vec(SCONV_K), vec(CONF_K)],
        out_specs=[pl.BlockSpec((tm, 7 * cw), lambda i: (i, 0)), vec(SCONV_K), vec(CONF_K), vec(1)],
        out_shape=[jax.ShapeDtypeStruct((s, 7 * cw), BF16), jax.ShapeDtypeStruct((SCONV_K, cw), F32),
                   jax.ShapeDtypeStruct((CONF_K, cw), F32), jax.ShapeDtypeStruct((1, cw), F32)],
        scratch_shapes=[pltpu.VMEM((8 * SCONV_K, cw), F32), pltpu.VMEM((8 * CONF_K, cw), F32),
                        pltpu.VMEM((8, cw), F32)],
        compiler_params=_params("arbitrary"),
    )(dy, dy, p, p, p, p, p, p, p, p, ddc, ddc, p, p, p, p, dg2, sconv_w, dconv_w)


_ANY = pl.BlockSpec(memory_space=pl.ANY)


def _place():
    return lax.axis_index("x"), lax.axis_index("y"), lax.axis_index("c")


def all_gather(arrs, name, deps=()):
    n = len(arrs)

    def body(*refs):
        ins, outs = refs[:n], refs[n + len(deps):2 * n + len(deps)]
        send_sems, recv_sems, local_sems = refs[-3:]
        x, y, c = _place()
        me, sibling = (x, y, c), (x, y, 1 - c)
        chips = [(1 - x, y), (x, 1 - y), (1 - x, 1 - y)]

        def copy(a, k, block, to, src=None):
            px, py, pc = block
            dst = outs[a].at[4 * px + 2 * py + pc]
            return pltpu.make_async_remote_copy(
                src_ref=dst if src is None else src, dst_ref=dst,
                send_sem=send_sems.at[7 * a + k], recv_sem=recv_sems.at[7 * a + k],
                device_id=to, device_id_type=MESH)

        mine = [pltpu.make_async_copy(ins[a], outs[a].at[4 * x + 2 * y + c], local_sems.at[a]) for a in range(n)]
        first = []
        for a in range(n):
            first.append(copy(a, 0, me, sibling, src=ins[a]))
            first += [copy(a, 1 + j, me, (*chip, c), src=ins[a]) for j, chip in enumerate(chips)]
        for cp in first + mine:
            cp.start()
        passed = []
        for a in range(n):
            for j, chip in enumerate(chips):
                copy(a, 1 + j, (*chip, c), me).wait_recv()
                cp = copy(a, 4 + j, (*chip, c), sibling)
                cp.start()
                passed.append(cp)
        for a in range(n):
            copy(a, 0, sibling, me).wait_recv()
            for j, chip in enumerate(chips):
                copy(a, 4 + j, (*chip, 1 - c), me).wait_recv()
        for cp in first + passed:
            cp.wait_send()
        for cp in mine:
            cp.wait()

    return pl.pallas_call(
        body, name=name,
        out_shape=[jax.ShapeDtypeStruct((N_DEV,) + a.shape, a.dtype) for a in arrs],
        in_specs=[_ANY] * (n + len(deps)), out_specs=[_ANY] * n,
        scratch_shapes=[pltpu.SemaphoreType.DMA((7 * n,)), pltpu.SemaphoreType.DMA((7 * n,)),
                        pltpu.SemaphoreType.DMA((n,))],
    )(*arrs, *deps)


def in_proj_gathered(xs, g, w_own, extras, name, tm=1024):
    s, d = xs.shape
    n = w_own.shape[1]
    tm = min(tm, s)
    arrs = [w_own] + list(extras)
    na = len(arrs)
    tr = 256

    def body(*refs):
        x_ref, g_ref, ins = refs[0], refs[1], refs[2:2 + na]
        h_out, p_ref, outs = refs[2 + na], refs[3 + na], refs[4 + na:4 + 2 * na]
        (h_ref, xbuf, wbuf, obuf, send_sems, recv_sems, load_sem, store_sems, own_sems, h_sem,
         x_sems) = refs[4 + 2 * na:]
        x, y, c = _place()
        me, sibling = (x, y, c), (x, y, 1 - c)
        x_first = c == 0
        near = (jnp.where(x_first, 1 - x, x), jnp.where(x_first, y, 1 - y))
        far = (jnp.where(x_first, x, 1 - x), jnp.where(x_first, 1 - y, y))
        diag = (1 - x, 1 - y)
        k_near, k_far = jnp.where(x_first, 1, 2), jnp.where(x_first, 2, 1)
        f_near, f_far = k_near + 3, k_far + 3

        def slot(block):
            return 4 * block[0] + 2 * block[1] + block[2]

        def copy(a, k, block, to, src=None):
            dst = outs[a].at[slot(block)]
            return pltpu.make_async_remote_copy(
                src_ref=dst if src is None else src, dst_ref=dst,
                send_sem=send_sems.at[7 * a + k], recv_sem=recv_sems.at[7 * a + k],
                device_id=to, device_id_type=MESH)

        first = []
        for a in range(na):
            first += [copy(a, 0, me, sibling, src=ins[a]), copy(a, 1, me, (1 - x, y, c), src=ins[a]),
                      copy(a, 2, me, (x, 1 - y, c), src=ins[a])]
        for cp in first:
            cp.start()
        own = pltpu.make_async_copy(wbuf.at[0], outs[0].at[slot(me)], own_sems.at[0])
        mine = [pltpu.make_async_copy(ins[a], outs[a].at[slot(me)], own_sems.at[a]) for a in range(1, na)]
        stores = [None, None]

        def x_load(i):
            return pltpu.make_async_copy(x_ref.at[pl.ds(i * tr, tr), :], xbuf.at[i % 2], x_sems.at[i % 2])

        x_load(0).start()
        for i in range(s // tr):
            if i + 1 < s // tr:
                x_load(i + 1).start()
            x_load(i).wait()
            xv = xbuf[i % 2]
            r = lax.rsqrt(jnp.mean(xv * xv, axis=-1, keepdims=True) + EPS)
            h_ref[i * tr:(i + 1) * tr, :] = (xv * r * g_ref[...]).astype(BF16)
        h_store = pltpu.make_async_copy(h_ref, h_out, h_sem)
        h_store.start()

        def multiply(k, block, w_from):
            b = k % 2
            if k == 2:
                own.wait()
            load = pltpu.make_async_copy(w_from, wbuf.at[b], load_sem)
            load.start()
            if stores[b] is not None:
                stores[b].wait()
            load.wait()
            if k == 0:
                own.start()

            def chunk(i, carry):
                rows = pl.ds(pl.multiple_of(i * tm, tm), tm)
                obuf[b, rows, :] = jnp.dot(h_ref[rows, :], wbuf[b], preferred_element_type=F32).astype(BF16)
                return carry

            lax.fori_loop(0, s // tm, chunk, 0)
            stores[b] = pltpu.make_async_copy(
                obuf.at[b], p_ref.at[:, pl.ds(pl.multiple_of(slot(block) * n, 128), n)], store_sems.at[b])
            stores[b].start()

        passed = []

        def arrive(a, k, block):
            copy(a, k, block, me).wait_recv()

        def pass_on(a, k, block, to):
            cp = copy(a, k, block, to)
            cp.start()
            passed.append(cp)

        def gather(arrays, use):
            def arrive_all(k, block):
                for a in arrays:
                    arrive(a, k, block)

            def pass_all(k, block, to):
                for a in arrays:
                    pass_on(a, k, block, to)

            use(0, me)
            arrive_all(0, sibling)
            use(1, sibling)
            arrive_all(k_near, (*near, c))
            pass_all(3, (*near, c), (*far, c))
            pass_all(f_near, (*near, c), sibling)
            use(2, (*near, c))
            arrive_all(f_far, (*far, 1 - c))
            use(3, (*far, 1 - c))
            arrive_all(k_far, (*far, c))
            pass_all(f_far, (*far, c), sibling)
            use(4, (*far, c))
            arrive_all(f_near, (*near, 1 - c))
            use(5, (*near, 1 - c))
            arrive_all(3, (*diag, c))
            pass_all(6, (*diag, c), sibling)
            use(6, (*diag, c))
            arrive_all(6, (*diag, 1 - c))
            use(7, (*diag, 1 - c))

        gather(range(na), lambda k, block: multiply(k, block, ins[0] if k == 0 else outs[0].at[slot(block)]))
        for cp in mine:
            cp.start()
        for cp in first + passed:
            cp.wait_send()
        for cp in mine + stores + [h_store]:
            cp.wait()

    vmem = pl.BlockSpec(memory_space=pltpu.VMEM)
    outs = pl.pallas_call(
        body, name=name,
        out_shape=[jax.ShapeDtypeStruct((s, d), BF16), jax.ShapeDtypeStruct((s, N_DEV * n), BF16)]
        + [jax.ShapeDtypeStruct((N_DEV,) + a.shape, a.dtype) for a in arrs],
        in_specs=[_ANY, vmem] + [_ANY] * na, out_specs=[_ANY] * (2 + na),
        scratch_shapes=[pltpu.VMEM((s, d), BF16), pltpu.VMEM((2, tr, d), F32), pltpu.VMEM((2, d, n), BF16),
                        pltpu.VMEM((2, s, n), BF16),
                        pltpu.SemaphoreType.DMA((7 * na,)), pltpu.SemaphoreType.DMA((7 * na,)),
                        pltpu.SemaphoreType.DMA, pltpu.SemaphoreType.DMA((2,)), pltpu.SemaphoreType.DMA((na,)),
                        pltpu.SemaphoreType.DMA, pltpu.SemaphoreType.DMA((2,))],
        compiler_params=pltpu.CompilerParams(vmem_limit_bytes=VMEM_LIMIT),
    )(xs, g, *arrs)
    return outs[0], outs[1], outs[2], outs[3:]


_HBM = pl.BlockSpec(memory_space=pltpu.HBM)
_SEM = pl.BlockSpec(memory_space=pltpu.SEMAPHORE)
_DATAFLOW = pltpu.SideEffectType.DATAFLOW_SIDE_EFFECTING


def _peers_per_array(kind):
    return {"sibling": 1, "halves": 1, "pass": 1, "neighbours": 2}.get(kind, 3)


def _near_far():
    x, y, c = _place()
    x_first = c == 0
    near = (jnp.where(x_first, 1 - x, x), jnp.where(x_first, y, 1 - y))
    far = (jnp.where(x_first, x, 1 - x), jnp.where(x_first, 1 - y, y))
    return near, far, jnp.where(x_first, 0, 1), jnp.where(x_first, 1, 0)


def _split_copies(kind, srcs, lands, send_sems, recv_sems):
    x, y, c = _place()
    per = _peers_per_array(kind)
    out = []
    for a in range(len(lands)):
        if kind == "sibling":
            part = srcs[a] if srcs[a].shape[1] == 1 else srcs[a].at[:, pl.ds(1 - c, 1)]
            peers = [((x, y, 1 - c), part, lands[a], lands[a])]
        elif kind == "halves":
            mine, its = lands[a].at[:, pl.ds(c, 1)], lands[a].at[:, pl.ds(1 - c, 1)]
            peers = [((x, y, 1 - c), mine, mine, its)]
        elif kind == "neighbours":
            here = lands[a].at[4 * x + 2 * y + c]
            peers = [((px, py, c), srcs[a], here, lands[a].at[4 * px + 2 * py + c])
                     for px, py in [(1 - x, y), (x, 1 - y)]]
        elif kind == "pass":
            near, far, _, _ = _near_far()
            block = lands[a].at[4 * near[0] + 2 * near[1] + c]
            peers = [((*far, c), block, block, lands[a].at[4 * (1 - x) + 2 * (1 - y) + c])]
        else:
            peers = []
            for px, py in [(1 - x, y), (x, 1 - y), (1 - x, 1 - y)]:
                if kind == "gather":
                    views = (srcs[a], lands[a].at[4 * x + 2 * y + c], lands[a].at[4 * px + 2 * py + c])
                else:
                    views = (srcs[a].at[2 * px + py], lands[a].at[2 * x + y], lands[a].at[2 * px + py])
                peers.append(((px, py, c),) + views)
        for j, (peer, src, dst, arrives) in enumerate(peers):
            sems = dict(send_sem=send_sems.at[per * a + j], recv_sem=recv_sems.at[per * a + j],
                        device_id=peer, device_id_type=MESH)
            out.append((pltpu.make_async_remote_copy(src_ref=src, dst_ref=dst, **sems),
                        pltpu.make_async_remote_copy(src_ref=src, dst_ref=arrives, **sems)))
    return out


def split_start(kind, srcs, lands, deps, name):
    ns, nl = len(srcs), len(lands)
    n_sems = _peers_per_array(kind) * nl
    held = list(srcs) + list(lands)

    def body(*refs):
        send_sems, recv_sems = refs[len(held) + len(deps)], refs[len(held) + len(deps) + 1]
        for copy, _ in _split_copies(kind, refs[:ns], refs[ns:ns + nl], send_sems, recv_sems):
            copy.start()
        token = refs[-1]
        token[...] = jnp.zeros_like(token)

    outs = pl.pallas_call(
        body, name=name,
        out_shape=(pltpu.SemaphoreType.DMA((n_sems,)), pltpu.SemaphoreType.DMA((n_sems,)),
                   *[pltpu.HBM(a.shape, a.dtype) for a in held], jax.ShapeDtypeStruct((8, 128), F32)),
        in_specs=[_HBM] * len(held) + [_ANY] * len(deps),
        out_specs=(_SEM, _SEM, *([_HBM] * len(held)), pl.BlockSpec(memory_space=pltpu.VMEM)),
        input_output_aliases={i: 2 + i for i in range(len(held))},
        compiler_params=pltpu.CompilerParams(has_side_effects=_DATAFLOW),
    )(*[pltpu.with_memory_space_constraint(a, pltpu.HBM) for a in held], *deps)
    return outs[0], outs[1], list(outs[2:2 + ns]), list(outs[2 + ns:2 + ns + nl]), outs[-1]


def split_wait(kind, send_sems, recv_sems, srcs, lands, afters, name):
    ns, nl = len(srcs), len(lands)
    held = list(srcs) + list(lands)

    def body(*refs):
        for _, arrival in _split_copies(kind, refs[:ns], refs[ns:ns + nl], refs[ns + nl], refs[ns + nl + 1]):
            arrival.wait_send()
            arrival.wait_recv()

    outs = pl.pallas_call(
        body, name=name,
        out_shape=[pltpu.HBM(a.shape, a.dtype) for a in held],
        in_specs=[_HBM] * len(held) + [_SEM, _SEM] + [_ANY] * len(afters),
        out_specs=[_HBM] * len(held),
        input_output_aliases={i: i for i in range(len(held))},
        compiler_params=pltpu.CompilerParams(has_side_effects=_DATAFLOW),
    )(*held, send_sems, recv_sems, *afters)
    return list(outs[:ns]), list(outs[ns:])


def split_pass_on(lands, first_recv, afters, name):
    n = len(lands)

    def body(*refs):
        lands_r, first = refs[:n], refs[n]
        send_sems, recv_sems = refs[n + 1 + len(afters)], refs[n + 2 + len(afters)]
        c = lax.axis_index("c")
        near, _, k_near, _ = _near_far()
        for a in range(n):
            block = lands_r[a].at[4 * near[0] + 2 * near[1] + c]
            pltpu.make_async_remote_copy(
                src_ref=block, dst_ref=block, send_sem=first.at[2 * a + k_near], recv_sem=first.at[2 * a + k_near],
                device_id=(near[0], near[1], c), device_id_type=MESH).wait_recv()
        for copy, _ in _split_copies("pass", [], lands_r, send_sems, recv_sems):
            copy.start()
        token = refs[-1]
        token[...] = jnp.zeros_like(token)

    outs = pl.pallas_call(
        body, name=name,
        out_shape=(pltpu.SemaphoreType.DMA((n,)), pltpu.SemaphoreType.DMA((n,)),
                   *[pltpu.HBM(a.shape, a.dtype) for a in lands], jax.ShapeDtypeStruct((8, 128), F32)),
        in_specs=[_HBM] * n + [_SEM] + [_ANY] * len(afters),
        out_specs=(_SEM, _SEM, *([_HBM] * n), pl.BlockSpec(memory_space=pltpu.VMEM)),
        input_output_aliases={i: 2 + i for i in range(n)},
        compiler_params=pltpu.CompilerParams(has_side_effects=_DATAFLOW),
    )(*lands, first_recv, *afters)
    return outs[0], outs[1], list(outs[2:2 + n]), outs[-1]


def split_wait_neighbours(first_send, first_recv, pass_send, pass_recv, srcs, lands, afters, name):
    n = len(lands)
    held = list(srcs) + list(lands)

    def body(*refs):
        srcs_r, lands_r = refs[:n], refs[n:2 * n]
        send1, recv1, send2, recv2 = refs[2 * n:2 * n + 4]
        c = lax.axis_index("c")
        _, far, _, k_far = _near_far()
        for _, arrival in _split_copies("neighbours", srcs_r, lands_r, send1, recv1):
            arrival.wait_send()
        for a in range(n):
            block = lands_r[a].at[4 * far[0] + 2 * far[1] + c]
            pltpu.make_async_remote_copy(
                src_ref=block, dst_ref=block, send_sem=recv1.at[2 * a + k_far], recv_sem=recv1.at[2 * a + k_far],
                device_id=(far[0], far[1], c), device_id_type=MESH).wait_recv()
        for _, arrival in _split_copies("pass", [], lands_r, send2, recv2):
            arrival.wait_send()
            arrival.wait_recv()

    outs = pl.pallas_call(
        body, name=name,
        out_shape=[pltpu.HBM(a.shape, a.dtype) for a in held],
        in_specs=[_HBM] * len(held) + [_SEM] * 4 + [_ANY] * len(afters),
        out_specs=[_HBM] * len(held),
        input_output_aliases={i: i for i in range(len(held))},
        compiler_params=pltpu.CompilerParams(has_side_effects=_DATAFLOW),
    )(*held, first_send, first_recv, pass_send, pass_recv, *afters)
    return list(outs[:n]), list(outs[n:])


def place_block(land, block, dev, name):
    r, c = block.shape
    tr = min(r, 512)

    def body(dev_ref, land_ref, b_ref, o_ref):
        del dev_ref, land_ref
        o_ref[...] = b_ref[...]

    return pl.pallas_call(
        body, name=name,
        grid_spec=pltpu.PrefetchScalarGridSpec(
            num_scalar_prefetch=1, grid=(r // tr,),
            in_specs=[_ANY, pl.BlockSpec((tr, c), lambda i, dev_ref: (i, 0))],
            out_specs=pl.BlockSpec((None, tr, c), lambda i, dev_ref: (dev_ref[0], i, 0))),
        out_shape=jax.ShapeDtypeStruct(land.shape, land.dtype),
        input_output_aliases={1: 0},
        compiler_params=_params("parallel"),
    )(dev, land, block)


def pair_add(own, recv, core, name):
    _, _, r, c = own.shape
    tr = min(r, 2048)

    def body(core_ref, own_ref, recv_ref, o_ref):
        del core_ref
        o_ref[...] = (own_ref[...].astype(F32) + recv_ref[...].astype(F32)).astype(BF16)

    return pl.pallas_call(
        body, name=name,
        grid_spec=pltpu.PrefetchScalarGridSpec(
            num_scalar_prefetch=1, grid=(4, r // tr),
            in_specs=[pl.BlockSpec((None, None, tr, c), lambda k, i, core_ref: (k, core_ref[0], i, 0)),
                      pl.BlockSpec((None, None, tr, c), lambda k, i, core_ref: (k, 0, i, 0))],
            out_specs=pl.BlockSpec((None, tr, c), lambda k, i, core_ref: (k, i, 0))),
        out_shape=jax.ShapeDtypeStruct((4, r, c), BF16),
        compiler_params=_params("parallel", "parallel"),
    )(core, own, recv)


def _adamw_math(w, g, m, v):
    m2 = ADAM_B1 * m + (1.0 - ADAM_B1) * g
    v2 = ADAM_B2 * v + (1.0 - ADAM_B2) * (g * g)
    m_hat = m2 / (1.0 - ADAM_B1 ** ADAM_STEP)
    v_hat = v2 / (1.0 - ADAM_B2 ** ADAM_STEP)
    delta = -ADAM_LR * (m_hat / (jnp.sqrt(v_hat) + ADAM_EPS) + ADAM_WD * w)
    return delta, m2, v2


def adamw_big(w, m, v, own, got, chip, name):
    r, c = w.shape
    tr = min(r, 512)

    def body(chip_ref, w_ref, m_ref, v_ref, p0, p1, p2, p3, g_ref, d_ref, m2_ref, v2_ref):
        del chip_ref
        g = ((p0[...].astype(F32) + p1[...].astype(F32)) + p2[...].astype(F32)) + p3[...].astype(F32)
        delta, m2, v2 = _adamw_math(w_ref[...], g, m_ref[...], v_ref[...])
        g_ref[...] = g
        d_ref[...] = delta
        m2_ref[...] = m2
        v2_ref[...] = v2

    row = pl.BlockSpec((tr, c), lambda i, chip_ref: (i, 0))

    def slab(flip):
        return pl.BlockSpec((None, tr, c), lambda i, chip_ref: (chip_ref[0] ^ flip, i, 0))

    return pl.pallas_call(
        body, name=name,
        grid_spec=pltpu.PrefetchScalarGridSpec(
            num_scalar_prefetch=1, grid=(r // tr,),
            in_specs=[row, row, row, slab(0), slab(1), slab(2), slab(3)],
            out_specs=[row] * 4),
        out_shape=[jax.ShapeDtypeStruct((r, c), F32)] * 4,
        compiler_params=_params("parallel"),
    )(chip, w, m, v, own, got, got, got)


def sum_devices(g8, name):
    def body(g_ref, o_ref):
        tot = g_ref[0]
        for k in range(1, N_DEV):
            tot = tot + g_ref[k]
        o_ref[...] = tot

    return pl.pallas_call(body, name=name, out_shape=jax.ShapeDtypeStruct(g8.shape[1:], F32))(g8)


def adamw_small(ws, gs, ms, vs, name):
    n = len(ws)

    def body(*refs):
        w_r, g_r, m_r, v_r = refs[:n], refs[n:2 * n], refs[2 * n:3 * n], refs[3 * n:4 * n]
        d_o, m_o, v_o = refs[4 * n:5 * n], refs[5 * n:6 * n], refs[6 * n:7 * n]
        for k in range(n):
            delta, m2, v2 = _adamw_math(w_r[k][...], g_r[k][...], m_r[k][...], v_r[k][...])
            d_o[k][...] = delta
            m_o[k][...] = m2
            v_o[k][...] = v2

    shapes = [jax.ShapeDtypeStruct(w.shape, F32) for w in ws]
    outs = pl.pallas_call(body, name=name, out_shape=shapes * 3)(*ws, *gs, *ms, *vs)
    return outs[:n], outs[n:2 * n], outs[2 * n:]


def _rows128(a):
    return a.reshape(-1, 128)


def _pad_rows(a, rows):
    return jnp.pad(a, ((0, rows - a.shape[0]), (0, 0)))


def kernel(x, ln_pre_even, w_in_even, pool_w, pool_scale, w_out_even, ln_post_even, ln_pre_odd, w_in_odd, sconv_w, dconv_w, dconv_b, cnorm_g, cnorm_b, w_out_odd, ln_post_odd, loss_target, m_ln_pre_even, m_w_in_even, m_pool_w, m_pool_scale, m_w_out_even, m_ln_post_even, m_ln_pre_odd, m_w_in_odd, m_sconv_w, m_dconv_w, m_dconv_b, m_cnorm_g, m_cnorm_b, m_w_out_odd, m_ln_post_odd, v_ln_pre_even, v_w_in_even, v_pool_w, v_pool_scale, v_w_out_even, v_ln_post_even, v_ln_pre_odd, v_w_in_odd, v_sconv_w, v_dconv_w, v_dconv_b, v_cnorm_g, v_cnorm_b, v_w_out_odd, v_ln_post_odd):
    xs = x[0]
    tgt = loss_target[0]
    s, d = xs.shape
    half = d // 2
    n_heads = half // HEAD_DIM
    ng = len(POOL_WINDOWS)
    cwp = half // ng
    dev = 4 * lax.axis_index("x") + 2 * lax.axis_index("y") + lax.axis_index("c")
    core = lax.axis_index("c").astype(jnp.int32).reshape(1)

    pr = pool_w.shape[2]
    cl = sconv_w.shape[2]
    small_parts = [(_rows128(ln_pre_odd), 8), (sconv_w[0], 8), (dconv_w[0], 32), (dconv_b, 8),
                   (cnorm_g, 8), (cnorm_b, 8), (_rows128(ln_post_odd), 8)]
    small_local = jnp.concatenate([_pad_rows(a, r) for a, r in small_parts], axis=0)
    h0, p0, g_wie, (g_pw, g_small) = in_proj_gathered(
        xs, ln_pre_even, w_in_even[0].astype(BF16), [pool_w[0].reshape(ng * pr, cwp).astype(BF16), small_local],
        "ag_in_proj_even")
    comm = _Exchanges(dev, core, d)
    token = comm.start_weights("out_even", [w_out_even[0].astype(BF16)], [p0])
    token = comm.start_weights("in_odd", [w_in_odd[0].astype(BF16)], [token], neighbours=True)
    sb_dep = token
    comm.later["out_odd"] = [w_out_odd[0].astype(BF16)]
    pool_full = g_pw.reshape(N_DEV, ng, pr, cwp).transpose(1, 0, 2, 3).reshape(ng, cwp, cwp)
    nl = ln_pre_odd.shape[1] // 128

    def chan(lo, rows):
        return g_small[:, lo:lo + rows].transpose(1, 0, 2).reshape(rows, N_DEV * cl)

    ln_pre_odd_f = g_small[:, 0:nl].reshape(1, d)
    sconv_f = chan(8, SCONV_K)
    dconv_f = chan(16, CONF_K)
    dconv_b_f = chan(48, 1)
    cnorm_g_f = chan(56, 1)
    cnorm_b_f = chan(64, 1)
    ln_post_odd_f = g_small[:, 72:72 + nl].reshape(1, d)

    loss_blk, grad_x, small_g = _fwd_bwd(
        xs, tgt, ln_pre_even, h0, p0, g_wie, pool_full, pool_scale, ln_post_even, ln_pre_odd_f,
        sconv_f, dconv_f, dconv_b_f, cnorm_g_f, cnorm_b_f, ln_post_odd_f, comm, sb_dep)
    small_w = [ln_pre_even, pool_scale, ln_post_even, ln_pre_odd, sconv_w[0], dconv_w[0], dconv_b, cnorm_g, cnorm_b, ln_post_odd]
    small_m = [m_ln_pre_even, m_pool_scale, m_ln_post_even, m_ln_pre_odd, m_sconv_w[0], m_dconv_w[0], m_dconv_b, m_cnorm_g, m_cnorm_b, m_ln_post_odd]
    small_v = [v_ln_pre_even, v_pool_scale, v_ln_post_even, v_ln_pre_odd, v_sconv_w[0], v_dconv_w[0], v_dconv_b, v_cnorm_g, v_cnorm_b, v_ln_post_odd]
    big = {"w_in_even": (w_in_even, m_w_in_even, v_w_in_even), "pool_w": (pool_w, m_pool_w, v_pool_w),
           "w_out_even": (w_out_even, m_w_out_even, v_w_out_even), "w_in_odd": (w_in_odd, m_w_in_odd, v_w_in_odd),
           "w_out_odd": (w_out_odd, m_w_out_odd, v_w_out_odd)}
    upd = comm.finish_updates(big, [grad_x])
    upd.update(comm.finish_updates(big, [grad_x]))
    sg, sd, sm, sv, loss = _update_small(small_g, loss_blk, small_w, small_m, small_v, dev, d, cl,
                                         deps=[upd["w_in_odd"][1], upd["w_out_even"][1]])
    upd.update(comm.finish_updates(big, sd))
    (g_wie_o, d_wie, m_wie, v_wie), (g_pw_o, d_pw, m_pw, v_pw) = upd["w_in_even"], upd["pool_w"]
    (g_woe_o, d_woe, m_woe, v_woe), (g_wio_o, d_wio, m_wio, v_wio) = upd["w_out_even"], upd["w_in_odd"]
    g_woo_o, d_woo, m_woo, v_woo = upd["w_out_odd"]

    def order(small, wie, pw, woe, wio, woo):
        return [small[0], wie, pw, small[1], woe, small[2], small[3], wio, small[4], small[5], small[6],
                small[7], small[8], woo, small[9]]

    grads = order(sg, g_wie_o, g_pw_o, g_woe_o, g_wio_o, g_woo_o)
    deltas = order(sd, d_wie, d_pw, d_woe, d_wio, d_woo)
    new_m = order(sm, m_wie, m_pw, m_woe, m_wio, m_woo)
    new_v = order(sv, v_wie, v_pw, v_woe, v_wio, v_woo)
    return (loss, grad_x[None], *grads, *deltas, *new_m, *new_v)


def _fwd_bwd(xs, tgt, ln_pre_even, h0, p0, g_wie, pool_full, pool_scale, ln_post_even, ln_pre_odd_f,
             sconv_f, dconv_f, dconv_b_f, cnorm_g_f, cnorm_b_f, ln_post_odd_f, comm, sb_dep):
    d = xs.shape[1]
    n_heads = d // 2 // HEAD_DIM
    ng, cwp = pool_full.shape[0], pool_full.shape[1]
    a0, sb_wts = sb_fwd(p0, n_heads, "sb_fwd", dep=sb_dep)
    dep = comm.weights_arrived("out_even", after=a0)
    dep = comm.weights_pass_on("in_odd", after=dep)
    y0 = even_mix_fwd(a0, p0, pool_full, pool_scale, "even_mix_fwd", dep=dep)
    (w_out_e,) = comm.weights("out_even", after=y0)
    w_out_e = w_out_e.reshape(1, d, d)
    o0 = mm_nn(y0, w_out_e, BF16, "out_proj_even", tn=512)
    dep = comm.weights_arrived("in_odd", after=o0)
    dep = comm.start_weights("out_odd", comm.later.pop("out_odd"), [dep])
    x1, h1 = postnorm_fwd(xs, o0, ln_post_even, ln_pre_odd_f, "post_even", dep=dep)
    (g_wio,) = comm.weights("in_odd", after=x1)
    p1 = mm_nn(h1, g_wio, BF16, "in_proj_odd", group=2)
    dep = comm.weights_arrived("out_odd", after=p1)
    y1, dc = odd_mix_fwd(p1, sconv_f, dconv_f, dconv_b_f, cnorm_g_f, cnorm_b_f, "odd_mix_fwd", dep=dep)
    (w_out_o,) = comm.weights("out_odd", after=y1)
    w_out_o = w_out_o.reshape(1, d, d)
    o1 = mm_nn(y1, w_out_o, BF16, "out_proj_odd", tn=512)
    loss_blk, gx2, do1, dg_post_odd = final_fwd_bwd(x1, o1, ln_post_odd_f, tgt, "post_odd_loss")

    dw_out_o = mm_tn(y1, do1, 1, BF16, "dw_out_odd")
    dy1 = mm_nt(do1, w_out_o, BF16, "dy_odd")
    ddc, dg2, dgam, dbet = odd_bwd_ln(dy1, p1, dc, cnorm_g_f, cnorm_b_f, "odd_bwd_ln")
    dp1, dsconv, ddconv, ddconv_b = odd_bwd_conv(dy1, p1, ddc, dg2, sconv_f, dconv_f, "odd_bwd_conv")
    dw_in_o = mm_tn(h1, dp1, N_DEV, BF16, "dw_in_odd", group=2)
    dep = comm.reduce_begin({"w_out_odd": dw_out_o.reshape(N_DEV, d // N_DEV, d), "w_in_odd": dw_in_o}, "odd")
    dh1 = mm_nt(dp1, g_wio, BF16, "dh_odd", dep=dep, group=2)
    dep = comm.reduce_send(after=dh1)
    gx1, dg_pre_odd, do0, dg_post_even = norm_bwd(dh1, x1, ln_pre_odd_f, gx2, "pre_odd_post_even_bwd",
                                                  inp2=o0, g2=ln_post_even, dep=dep)

    dw_out_e = mm_tn(y0, do0, 1, BF16, "dw_out_even")
    dy0 = mm_nt(do0, w_out_e, BF16, "dy_even")
    da0, du0, dg0, dpool, dpool_scale = even_mix_bwd(dy0, a0, p0, pool_full, pool_scale, "even_mix_bwd")
    pr = cwp // N_DEV
    dpool_slabs = dpool.astype(BF16).reshape(ng, N_DEV, pr, cwp).transpose(1, 0, 2, 3).reshape(N_DEV, ng * pr, cwp)
    dep = comm.reduce_begin({"w_out_even": dw_out_e.reshape(N_DEV, d // N_DEV, d), "pool_w": dpool_slabs}, "even_out")
    dq0, dk0, dv0 = sb_bwd(p0, a0, sb_wts, da0, n_heads, "sb_bwd", dep=dep)
    dep = comm.reduce_send(after=dq0)
    dp0 = jnp.concatenate([dq0, dk0, dv0, du0, dg0], axis=1)
    dw_sibling = mm_tn(h0, dp0, N_DEV // 2, BF16, "dw_in_even_sibling", dep=dep, pick=(2, 1 - comm.core))
    dep = comm.reduce_begin({"w_in_even": dw_sibling}, "even_in", sibling_part=True)
    dw_own = mm_tn(h0, dp0, N_DEV // 2, BF16, "dw_in_even_own", dep=dep, pick=(2, comm.core))
    dep = comm.reduce_send(after=dw_own, own_part={"w_in_even": dw_own})
    dh0 = mm_nt(dp0, g_wie, BF16, "dh_even", dep=dep, group=2)
    dep = None
    grad_x, dg_pre_even = norm_bwd(dh0, xs, ln_pre_even, gx1, "pre_even_bwd", tm=512, dep=dep)
    small_g = [dg_pre_even, dpool_scale, dg_post_even, dg_pre_odd, dsconv, ddconv, ddconv_b, dgam, dbet, dg_post_odd]
    return loss_blk, grad_x, small_g


class _Exchanges:
    def __init__(self, dev, core, d):
        self.dev = dev.astype(jnp.int32).reshape(1)
        self.core = core
        self.chip = (dev // 2).astype(jnp.int32).reshape(1)
        self.d = d
        self.in_flight = {}
        self.later = {}
        self.to_sibling = None
        self.pending = []

    def start_weights(self, tag, blocks, afters, neighbours=False):
        lands = [lax.empty((N_DEV,) + b.shape, b.dtype) for b in blocks]
        kind = "neighbours" if neighbours else "gather"
        send, recv, srcs, lands, token = split_start(kind, blocks, lands, afters, "ag_start_" + tag)
        self.in_flight[tag] = (send, recv, srcs, lands)
        return token

    def weights_pass_on(self, tag, after):
        send, recv, srcs, lands = self.in_flight.pop(tag)
        send2, recv2, lands, token = split_pass_on(lands, recv, [after], "ag_pass_on_" + tag)
        self.in_flight[tag] = (send, recv, srcs, lands, send2, recv2)
        return token

    def weights_arrived(self, tag, after):
        entry = self.in_flight.pop(tag)
        send, recv, srcs, lands = entry[:4]
        if len(entry) == 6:
            srcs, lands = split_wait_neighbours(send, recv, entry[4], entry[5], srcs, lands, [after], "ag_wait_" + tag)
        else:
            srcs, lands = split_wait("gather", send, recv, srcs, lands, [after], "ag_wait_" + tag)
        lands = [place_block(l, b, self.dev, "ag_own_%s_%d" % (tag, k)) for k, (l, b) in enumerate(zip(lands, srcs))]
        lands = [l.reshape((4, 2) + l.shape[1:]) for l in lands]
        send, recv, _, lands, token = split_start("halves", [], lands, [], "ag_sibling_start_" + tag)
        self.in_flight[tag] = (send, recv, lands)
        return token

    def weights(self, tag, after):
        send, recv, lands = self.in_flight.pop(tag)
        _, lands = split_wait("halves", send, recv, [], lands, [after], "ag_sibling_wait_" + tag)
        return [l.reshape((N_DEV,) + l.shape[2:]) for l in lands]

    def reduce_begin(self, partials, tag, sibling_part=False):
        names = list(partials)
        arrs = [partials[k].reshape((4, 1 if sibling_part else 2) + partials[k].shape[1:]) for k in names]
        lands = [lax.empty((4, 1) + a.shape[2:], a.dtype) for a in arrs]
        send, recv, srcs, lands, token = split_start("sibling", arrs, lands, [], "rs_sibling_start_" + tag)
        self.to_sibling = (tag, names, send, recv, srcs, lands)
        return token

    def reduce_send(self, after, own_part=None):
        tag, names, send, recv, srcs, lands = self.to_sibling
        srcs, lands = split_wait("sibling", send, recv, srcs, lands, [after], "rs_sibling_wait_" + tag)
        which = self.core
        if own_part is not None:
            srcs = [own_part[k].reshape((4, 1) + own_part[k].shape[1:]) for k in names]
            which = jnp.zeros((1,), jnp.int32)
        sums = [pair_add(o, r, which, "rs_pair_add_" + k) for k, o, r in zip(names, srcs, lands)]
        zones = [lax.empty(a.shape, a.dtype) for a in sums]
        send, recv, srcs, zones, token = split_start("scatter", sums, zones, [], "rs_start_" + tag)
        self.pending.append((tag, names, send, recv, srcs, zones))
        return token

    def finish_updates(self, big, afters):
        tag, names, send, recv, srcs, lands = self.pending.pop(0)
        srcs, lands = split_wait("scatter", send, recv, srcs, lands, afters, "rs_wait_" + tag)
        out = {}
        for name, own, got in zip(names, srcs, lands):
            w, m, v = big[name]
            shp = own.shape[1:]
            outs = adamw_big(w.reshape(shp), m.reshape(shp), v.reshape(shp), own, got, self.chip, "adamw_" + name)
            out[name] = [o.reshape(w.shape) for o in outs]
        return out


def _update_small(small_g, loss_blk, small_w, small_m, small_v, dev, d, cl, deps):
    packed = jnp.concatenate([_rows128(g) for g in small_g] + [loss_blk], axis=0)
    (g8,) = all_gather([packed], "ag_small_grads", deps)
    tot = sum_devices(g8, "sum_small_grads")
    loss = tot[packed.shape[0] - 8, 0]
    full_g = []
    lo = 0
    for g in small_g:
        rows = g.size // 128
        full_g.append(tot[lo:lo + rows].reshape(g.shape))
        lo += rows

    def mine(g, width):
        return lax.dynamic_slice_in_dim(g, dev * width, width, axis=g.ndim - 1)

    fg = full_g
    small_gl = [fg[0], fg[1], fg[2], mine(fg[3], d // N_DEV), mine(fg[4], cl), mine(fg[5], cl), mine(fg[6], cl),
                mine(fg[7], cl), mine(fg[8], cl), mine(fg[9], d // N_DEV)]
    sd, sm, sv = adamw_small(small_w, small_gl, small_m, small_v, "adamw_small")

    def like(k, a):
        return a[None] if k in (4, 5) else a

    sg = [like(k, a) for k, a in enumerate(small_gl)]
    sd = [like(k, a) for k, a in enumerate(sd)]
    sm = [like(k, a) for k, a in enumerate(sm)]
    sv = [like(k, a) for k, a in enumerate(sv)]
    return sg, sd, sm, sv, loss
```
